```python
import functools
import jax, jax.numpy as jnp
from jax import lax
import numpy as np

D_MODEL = 1024
BATCH = 8
SEQ = 4096
DEPTH = 2

HEAD_DIM = 64
GROUP_WIDTH = D_MODEL // 4
MIX_WIDTH = 4 * GROUP_WIDTH
N_GROUP_HEADS = GROUP_WIDTH // HEAD_DIM
SHORT_CONV_K = 3
DSWA_CONFIGS = ((128, 1), (512, 4), (2048, 16))
POOL_WINDOWS = (2, 4, 8, 16)
POOL_GROUP = GROUP_WIDTH // len(POOL_WINDOWS)
ROPE_THETA = 500000.0
ROPE_DIM = HEAD_DIM // 4
Q_BLOCK = 128
MEM_LEN = 256
XA_HEADS = 4
XA_HEAD_DIM = D_MODEL // XA_HEADS
D_FF = ((8 * D_MODEL // 3 + 127) // 128) * 128
FFN_CONV_K = 3
RMS_EPS = 1e-6
NEG_INF = -1e30
FORGET_BIAS_INIT = 3.0
N_IN = 3 * GROUP_WIDTH + 3 * GROUP_WIDTH + 3 * GROUP_WIDTH + N_GROUP_HEADS + GROUP_WIDTH

kernel_name = "hybrid_parallel_heads_conv_dilated_fox_pool"


def rms_norm(t, g):
    tf = t.astype(jnp.float32)
    y = tf * lax.rsqrt(jnp.mean(tf * tf, axis=-1, keepdims=True) + RMS_EPS)
    return (y * g.astype(jnp.float32)).astype(t.dtype)


def causal_depthwise_conv(t, w):
    K, C = w.shape
    return lax.conv_general_dilated(
        t, w.astype(t.dtype)[:, None, :], window_strides=(1,), padding=((K - 1, 0),),
        dimension_numbers=("NWC", "WIO", "NWC"), feature_group_count=C)


def to_heads(t):
    B, S, W = t.shape
    return t.reshape(B, S, W // HEAD_DIM, HEAD_DIM).transpose(0, 2, 1, 3)


def from_heads(t):
    B, H, S, dh = t.shape
    return t.transpose(0, 2, 1, 3).reshape(B, S, H * dh)


def rope_tables(positions):
    inv_freq = ROPE_THETA ** (-jnp.arange(0, ROPE_DIM, 2, dtype=jnp.float32) / ROPE_DIM)
    ang = positions.astype(jnp.float32)[:, None, :, None] * inv_freq
    return jnp.cos(ang), jnp.sin(ang)


def apply_partial_rope(t, cos, sin):
    tf = t.astype(jnp.float32)
    r1, r2 = jnp.split(tf[..., :ROPE_DIM], 2, axis=-1)
    rot = jnp.concatenate([r1 * cos - r2 * sin, r2 * cos + r1 * sin], axis=-1)
    return jnp.concatenate([rot, tf[..., ROPE_DIM:]], axis=-1).astype(t.dtype)


def short_conv_mixer(p, w_conv):
    h, b_gate, c_gate = jnp.split(p, 3, axis=-1)
    return b_gate * causal_depthwise_conv(c_gate * h, w_conv)


def dilated_attention(q, k, v):
    B, H, S, dh = q.shape
    nb = S // Q_BLOCK
    qb = q.reshape(B, H, nb, Q_BLOCK, dh).transpose(2, 0, 1, 3, 4)
    kf = k.astype(jnp.float32)
    vf = v.astype(jnp.float32)
    scale = dh ** -0.5

    def block(args):
        i, qi = args
        t = i * Q_BLOCK + jnp.arange(Q_BLOCK)
        qs = qi.astype(jnp.float32) * scale
        scores, values = [], []
        for window, dil in DSWA_CONFIGS:
            idx = t[:, None] - dil * jnp.arange(window // dil + 1)[None, :]
            idxc = jnp.maximum(idx, 0)
            kg = kf[:, :, idxc]
            s = jnp.einsum("bhqd,bhqnd->bhqn", qs, kg)
            scores.append(jnp.where(idx >= 0, s, NEG_INF))
            values.append(vf[:, :, idxc])
        m = functools.reduce(jnp.maximum, [s.max(axis=-1, keepdims=True) for s in scores])
        num, den = 0.0, 0.0
        for s, vg in zip(scores, values):
            e = jnp.exp(s - m)
            num = num + jnp.einsum("bhqn,bhqnd->bhqd", e, vg)
            den = den + e.sum(axis=-1, keepdims=True)
        return num / den

    o = lax.map(block, (jnp.arange(nb), qb))
    return o.transpose(1, 2, 0, 3, 4).reshape(B, H, S, dh).astype(q.dtype)


def dilated_mixer(p, cos, sin):
    q, k, v = [to_heads(t) for t in jnp.split(p, 3, axis=-1)]
    q = apply_partial_rope(q, cos, sin)
    k = apply_partial_rope(k, cos, sin)
    return from_heads(dilated_attention(q, k, v))


def forgetting_attention(q, k, v, c):
    B, H, S, dh = q.shape
    nb = S // Q_BLOCK
    qb = q.reshape(B, H, nb, Q_BLOCK, dh).transpose(2, 0, 1, 3, 4)
    cb = c.reshape(B, H, nb, Q_BLOCK).transpose(2, 0, 1, 3)
    kf = k.astype(jnp.float32)
    vf = v.astype(jnp.float32)
    kpos = jnp.arange(S)
    scale = dh ** -0.5

    def block(args):
        i, qi, ci = args
        t = i * Q_BLOCK + jnp.arange(Q_BLOCK)
        s = jnp.einsum("bhqd,bhkd->bhqk", qi.astype(jnp.float32) * scale, kf)
        s = s + ci[..., None] - c[:, :, None, :]
        s = jnp.where(kpos[None, :] <= t[:, None], s, NEG_INF)
        return jnp.einsum("bhqk,bhkd->bhqd", jax.nn.softmax(s, axis=-1), vf)

    o = lax.map(block, (jnp.arange(nb), qb, cb))
    return o.transpose(1, 2, 0, 3, 4).reshape(B, H, S, dh).astype(q.dtype)


def forgetting_mixer(p, b_f):
    G = GROUP_WIDTH
    q, k, v = to_heads(p[..., :G]), to_heads(p[..., G:2 * G]), to_heads(p[..., 2 * G:3 * G])
    logf = jax.nn.log_sigmoid(p[..., 3 * G:].astype(jnp.float32) + b_f.astype(jnp.float32))
    c = jnp.cumsum(logf, axis=1).transpose(0, 2, 1)
    return from_heads(forgetting_attention(q, k, v, c))


def pool_mixer(p, w_pool, scale):
    B, S, _ = p.shape
    pf = p.astype(jnp.float32).reshape(B, S, len(POOL_WINDOWS), POOL_GROUP)
    cs = jnp.cumsum(pf, axis=1)
    cs0 = jnp.concatenate([jnp.zeros_like(cs[:, :1]), cs], axis=1)
    t = jnp.arange(S)
    outs = []
    for g, w in enumerate(POOL_WINDOWS):
        upper = cs0[:, 1:, g]
        lower = cs0[:, jnp.maximum(t + 1 - w, 0), g]
        cnt = jnp.minimum(t + 1, w).astype(jnp.float32)[None, :, None]
        outs.append((upper - lower) / cnt - pf[:, :, g])
    z = jnp.stack(outs, axis=2)
    y = jnp.einsum("bsgc,gcd->bsgd", z, w_pool.astype(jnp.float32)).reshape(B, S, GROUP_WIDTH)
    return (y * scale.astype(jnp.float32)).astype(p.dtype)


def memory_cross_attention(xn, memn, w_q, w_kv, w_o):
    B, S, D = xn.shape
    M = memn.shape[1]
    q = (xn @ w_q).reshape(B, S, XA_HEADS, XA_HEAD_DIM)
    k, v = jnp.split(memn @ w_kv, 2, axis=-1)
    k = k.reshape(B, M, XA_HEADS, XA_HEAD_DIM)
    v = v.reshape(B, M, XA_HEADS, XA_HEAD_DIM)
    s = jnp.einsum("bshd,bmhd->bhsm", q.astype(jnp.float32), k.astype(jnp.float32)) * XA_HEAD_DIM ** -0.5
    o = jnp.einsum("bhsm,bmhd->bshd", jax.nn.softmax(s, axis=-1), v.astype(jnp.float32))
    return o.reshape(B, S, D).astype(xn.dtype) @ w_o


def conv_ffn(xn, w_up, w_conv, w_down):
    u = causal_depthwise_conv(xn @ w_up, w_conv)
    a, g = jnp.split(u, 2, axis=-1)
    return (a * jax.nn.silu(g)) @ w_down


def _fwd_setup_inputs(seed: int = 0) -> dict:
    key = jax.random.key(seed)
    ks = jax.random.split(key, 24)
    f32 = jnp.float32
    nrm = lambda k, shape, s: jax.random.normal(k, shape, f32) * s
    L, D, G = DEPTH, D_MODEL, GROUP_WIDTH
    x = jax.random.normal(ks[0], (BATCH, SEQ, D), f32)
    mem = jax.random.normal(ks[1], (BATCH, MEM_LEN, D), f32)
    offset = jax.random.randint(ks[2], (BATCH, 1), 0, 1024, dtype=jnp.int32)
    positions = (jnp.arange(SEQ, dtype=jnp.int32)[None, :] + offset).astype(jnp.int32)
    return {
        "x": x,
        "mem": mem,
        "positions": positions,
        "g_mix": 1.0 + nrm(ks[3], (L, D), 0.02),
        "w_in": nrm(ks[4], (L, D, N_IN), D ** -0.5),
        "b_forget": FORGET_BIAS_INIT + nrm(ks[5], (L, N_GROUP_HEADS), 0.1),
        "w_sconv": nrm(ks[6], (L, SHORT_CONV_K, G), SHORT_CONV_K ** -0.5),
        "w_pool": nrm(ks[7], (L, len(POOL_WINDOWS), POOL_GROUP, POOL_GROUP), POOL_GROUP ** -0.5),
        "pool_scale": 1.0 + nrm(ks[8], (L, G), 0.02),
        "w_out": nrm(ks[9], (L, MIX_WIDTH, D), MIX_WIDTH ** -0.5),
        "g_xa": 1.0 + nrm(ks[10], (L, D), 0.02),
        "g_mem": 1.0 + nrm(ks[11], (L, D), 0.02),
        "w_xq": nrm(ks[12], (L, D, D), D ** -0.5),
        "w_xkv": nrm(ks[13], (L, D, 2 * D), D ** -0.5),
        "w_xo": nrm(ks[14], (L, D, D), D ** -0.5),
        "g_ffn": 1.0 + nrm(ks[15], (L, D), 0.02),
        "w_up": nrm(ks[16], (L, D, 2 * D_FF), D ** -0.5),
        "w_ffconv": nrm(ks[17], (L, FFN_CONV_K, 2 * D_FF), FFN_CONV_K ** -0.5),
        "w_down": nrm(ks[18], (L, D_FF, D), D_FF ** -0.5),
        "g_final": 1.0 + nrm(ks[19], (D,), 0.02),
    }


def _fwd_reference(x, mem, positions, g_mix, w_in, b_forget, w_sconv, w_pool, pool_scale, w_out,
              g_xa, g_mem, w_xq, w_xkv, w_xo, g_ffn, w_up, w_ffconv, w_down, g_final):
    G = GROUP_WIDTH
    cos, sin = rope_tables(positions)
    h = x
    for l in range(DEPTH):
        xn = rms_norm(h, g_mix[l])
        proj = xn @ w_in[l]
        pa, pb, pc, pd = jnp.split(proj, [3 * G, 6 * G, 9 * G + N_GROUP_HEADS], axis=-1)
        ya = short_conv_mixer(pa, w_sconv[l])
        yb = dilated_mixer(pb, cos, sin)
        yc = forgetting_mixer(pc, b_forget[l])
        yd = pool_mixer(pd, w_pool[l], pool_scale[l])
        h = h + jnp.concatenate([ya, yb, yc, yd], axis=-1) @ w_out[l]
        h = h + memory_cross_attention(rms_norm(h, g_xa[l]), rms_norm(mem, g_mem[l]),
                                       w_xq[l], w_xkv[l], w_xo[l])
        h = h + conv_ffn(rms_norm(h, g_ffn[l]), w_up[l], w_ffconv[l], w_down[l])
    return rms_norm(h, g_final)


import jax as _jax
import jax.numpy as _jnp

TWIN_FORMAT = 'train_step'
FWD_PARAMS = ['x', 'mem', 'positions', 'g_mix', 'w_in', 'b_forget', 'w_sconv', 'w_pool', 'pool_scale', 'w_out', 'g_xa', 'g_mem', 'w_xq', 'w_xkv', 'w_xo', 'g_ffn', 'w_up', 'w_ffconv', 'w_down', 'g_final']
TWIN_WEIGHTS = ['g_mix', 'w_in', 'b_forget', 'w_sconv', 'w_pool', 'pool_scale', 'w_out', 'g_xa', 'g_mem', 'w_xq', 'w_xkv', 'w_xo', 'g_ffn', 'w_up', 'w_ffconv', 'w_down', 'g_final']
TWIN_DIFF_INPUT = 'x'
TWIN_INPUTS = ['x', 'mem', 'positions', 'g_mix', 'w_in', 'b_forget', 'w_sconv', 'w_pool', 'pool_scale', 'w_out', 'g_xa', 'g_mem', 'w_xq', 'w_xkv', 'w_xo', 'g_ffn', 'w_up', 'w_ffconv', 'w_down', 'g_final', 'loss_target', 'm_g_mix', 'm_w_in', 'm_b_forget', 'm_w_sconv', 'm_w_pool', 'm_pool_scale', 'm_w_out', 'm_g_xa', 'm_g_mem', 'm_w_xq', 'm_w_xkv', 'm_w_xo', 'm_g_ffn', 'm_w_up', 'm_w_ffconv', 'm_w_down', 'm_g_final', 'v_g_mix', 'v_w_in', 'v_b_forget', 'v_w_sconv', 'v_w_pool', 'v_pool_scale', 'v_w_out', 'v_g_xa', 'v_g_mem', 'v_w_xq', 'v_w_xkv', 'v_w_xo', 'v_g_ffn', 'v_w_up', 'v_w_ffconv', 'v_w_down', 'v_g_final']
TWIN_OUTPUTS = ['loss', 'grad_x', 'grad_g_mix', 'grad_w_in', 'grad_b_forget', 'grad_w_sconv', 'grad_w_pool', 'grad_pool_scale', 'grad_w_out', 'grad_g_xa', 'grad_g_mem', 'grad_w_xq', 'grad_w_xkv', 'grad_w_xo', 'grad_g_ffn', 'grad_w_up', 'grad_w_ffconv', 'grad_w_down', 'grad_g_final', 'delta_g_mix', 'delta_w_in', 'delta_b_forget', 'delta_w_sconv', 'delta_w_pool', 'delta_pool_scale', 'delta_w_out', 'delta_g_xa', 'delta_g_mem', 'delta_w_xq', 'delta_w_xkv', 'delta_w_xo', 'delta_g_ffn', 'delta_w_up', 'delta_w_ffconv', 'delta_w_down', 'delta_g_final', 'new_m_g_mix', 'new_m_w_in', 'new_m_b_forget', 'new_m_w_sconv', 'new_m_w_pool', 'new_m_pool_scale', 'new_m_w_out', 'new_m_g_xa', 'new_m_g_mem', 'new_m_w_xq', 'new_m_w_xkv', 'new_m_w_xo', 'new_m_g_ffn', 'new_m_w_up', 'new_m_w_ffconv', 'new_m_w_down', 'new_m_g_final', 'new_v_g_mix', 'new_v_w_in', 'new_v_b_forget', 'new_v_w_sconv', 'new_v_w_pool', 'new_v_pool_scale', 'new_v_w_out', 'new_v_g_xa', 'new_v_g_mem', 'new_v_w_xq', 'new_v_w_xkv', 'new_v_w_xo', 'new_v_g_ffn', 'new_v_w_up', 'new_v_w_ffconv', 'new_v_w_down', 'new_v_g_final']
TWIN_LEAF_KINDS = {'loss': 'loss', 'grad_x': 'grad_x', 'grad_g_mix': 'grad_w', 'grad_w_in': 'grad_w', 'grad_b_forget': 'grad_w', 'grad_w_sconv': 'grad_w', 'grad_w_pool': 'grad_w', 'grad_pool_scale': 'grad_w', 'grad_w_out': 'grad_w', 'grad_g_xa': 'grad_w', 'grad_g_mem': 'grad_w', 'grad_w_xq': 'grad_w', 'grad_w_xkv': 'grad_w', 'grad_w_xo': 'grad_w', 'grad_g_ffn': 'grad_w', 'grad_w_up': 'grad_w', 'grad_w_ffconv': 'grad_w', 'grad_w_down': 'grad_w', 'grad_g_final': 'grad_w', 'delta_g_mix': 'delta_w', 'delta_w_in': 'delta_w', 'delta_b_forget': 'delta_w', 'delta_w_sconv': 'delta_w', 'delta_w_pool': 'delta_w', 'delta_pool_scale': 'delta_w', 'delta_w_out': 'delta_w', 'delta_g_xa': 'delta_w', 'delta_g_mem': 'delta_w', 'delta_w_xq': 'delta_w', 'delta_w_xkv': 'delta_w', 'delta_w_xo': 'delta_w', 'delta_g_ffn': 'delta_w', 'delta_w_up': 'delta_w', 'delta_w_ffconv': 'delta_w', 'delta_w_down': 'delta_w', 'delta_g_final': 'delta_w', 'new_m_g_mix': 'new_m', 'new_m_w_in': 'new_m', 'new_m_b_forget': 'new_m', 'new_m_w_sconv': 'new_m', 'new_m_w_pool': 'new_m', 'new_m_pool_scale': 'new_m', 'new_m_w_out': 'new_m', 'new_m_g_xa': 'new_m', 'new_m_g_mem': 'new_m', 'new_m_w_xq': 'new_m', 'new_m_w_xkv': 'new_m', 'new_m_w_xo': 'new_m', 'new_m_g_ffn': 'new_m', 'new_m_w_up': 'new_m', 'new_m_w_ffconv': 'new_m', 'new_m_w_down': 'new_m', 'new_m_g_final': 'new_m', 'new_v_g_mix': 'new_v', 'new_v_w_in': 'new_v', 'new_v_b_forget': 'new_v', 'new_v_w_sconv': 'new_v', 'new_v_w_pool': 'new_v', 'new_v_pool_scale': 'new_v', 'new_v_w_out': 'new_v', 'new_v_g_xa': 'new_v', 'new_v_g_mem': 'new_v', 'new_v_w_xq': 'new_v', 'new_v_w_xkv': 'new_v', 'new_v_w_xo': 'new_v', 'new_v_g_ffn': 'new_v', 'new_v_w_up': 'new_v', 'new_v_w_ffconv': 'new_v', 'new_v_w_down': 'new_v', 'new_v_g_final': 'new_v'}


def _forward(args):
    return _fwd_reference(*[args[k] for k in FWD_PARAMS])


def _output_shape():
    out = _jax.eval_shape(lambda: _forward(_fwd_setup_inputs(0)))
    return out.shape, out.dtype

N_MICROBATCH = 1
ADAM_LR = 0.001
ADAM_B1 = 0.9
ADAM_B2 = 0.999
ADAM_EPS = 1e-08
ADAM_WD = 0.01
ADAM_STEP = 10
PER_EXAMPLE_BATCH_AXIS = {'x': 0, 'mem': 0, 'positions': 0, 'loss_target': 0}
SHARED_INPUTS = []
_WEIGHT_DTYPES = {'g_mix': _jnp.float32, 'w_in': _jnp.float32, 'b_forget': _jnp.float32, 'w_sconv': _jnp.float32, 'w_pool': _jnp.float32, 'pool_scale': _jnp.float32, 'w_out': _jnp.float32, 'g_xa': _jnp.float32, 'g_mem': _jnp.float32, 'w_xq': _jnp.float32, 'w_xkv': _jnp.float32, 'w_xo': _jnp.float32, 'g_ffn': _jnp.float32, 'w_up': _jnp.float32, 'w_ffconv': _jnp.float32, 'w_down': _jnp.float32, 'g_final': _jnp.float32}
MOMENT_SCALE = {'g_mix': 1.675359e-01, 'w_in': 1.037643e-01, 'b_forget': 2.717831e-01, 'w_sconv': 1.605316e-01, 'w_pool': 1.446627e-01, 'pool_scale': 1.383356e-01, 'w_out': 1.113332e-01, 'g_xa': 1.814160e-02, 'g_mem': 2.548794e-02, 'w_xq': 1.691780e-02, 'w_xkv': 1.728736e-02, 'w_xo': 1.726151e-02, 'g_ffn': 1.268756e-01, 'w_up': 5.023926e-02, 'w_ffconv': 5.061684e-02, 'w_down': 8.170039e-02, 'g_final': 3.199044e+01}


def _to_microbatches(a, axis):
    t = _jnp.moveaxis(a, axis, 0)
    t = t.reshape((N_MICROBATCH, t.shape[0] // N_MICROBATCH) + t.shape[1:])
    return _jnp.moveaxis(t, 1, axis + 1)


def setup_inputs(seed: int = 0) -> dict:
    inp = _fwd_setup_inputs(seed)
    key = _jax.random.fold_in(_jax.random.key(seed), 7919)
    shape, _ = _output_shape()
    out = dict(inp)
    out["loss_target"] = _jax.random.normal(_jax.random.fold_in(key, 0), shape, _jnp.float32)
    for i, name in enumerate(TWIN_WEIGHTS):
        w = inp[name].astype(_jnp.float32)
        if MOMENT_SCALE is None:
            s = _jnp.sqrt(_jnp.mean(_jnp.square(w)) + 1e-30)
        else:
            s = MOMENT_SCALE[name]
        km, kv = _jax.random.split(_jax.random.fold_in(key, i + 1))
        out[name] = w
        out["m_" + name] = s * _jax.random.normal(km, w.shape, _jnp.float32)
        out["v_" + name] = (s * s) * _jax.random.uniform(kv, w.shape, _jnp.float32, 0.5, 1.5)
    if N_MICROBATCH > 1:
        for name, axis in PER_EXAMPLE_BATCH_AXIS.items():
            out[name] = _to_microbatches(out[name], axis)
    return {'x': out['x'], 'mem': out['mem'], 'positions': out['positions'], 'g_mix': out['g_mix'], 'w_in': out['w_in'], 'b_forget': out['b_forget'], 'w_sconv': out['w_sconv'], 'w_pool': out['w_pool'], 'pool_scale': out['pool_scale'], 'w_out': out['w_out'], 'g_xa': out['g_xa'], 'g_mem': out['g_mem'], 'w_xq': out['w_xq'], 'w_xkv': out['w_xkv'], 'w_xo': out['w_xo'], 'g_ffn': out['g_ffn'], 'w_up': out['w_up'], 'w_ffconv': out['w_ffconv'], 'w_down': out['w_down'], 'g_final': out['g_final'], 'loss_target': out['loss_target'], 'm_g_mix': out['m_g_mix'], 'm_w_in': out['m_w_in'], 'm_b_forget': out['m_b_forget'], 'm_w_sconv': out['m_w_sconv'], 'm_w_pool': out['m_w_pool'], 'm_pool_scale': out['m_pool_scale'], 'm_w_out': out['m_w_out'], 'm_g_xa': out['m_g_xa'], 'm_g_mem': out['m_g_mem'], 'm_w_xq': out['m_w_xq'], 'm_w_xkv': out['m_w_xkv'], 'm_w_xo': out['m_w_xo'], 'm_g_ffn': out['m_g_ffn'], 'm_w_up': out['m_w_up'], 'm_w_ffconv': out['m_w_ffconv'], 'm_w_down': out['m_w_down'], 'm_g_final': out['m_g_final'], 'v_g_mix': out['v_g_mix'], 'v_w_in': out['v_w_in'], 'v_b_forget': out['v_b_forget'], 'v_w_sconv': out['v_w_sconv'], 'v_w_pool': out['v_w_pool'], 'v_pool_scale': out['v_pool_scale'], 'v_w_out': out['v_w_out'], 'v_g_xa': out['v_g_xa'], 'v_g_mem': out['v_g_mem'], 'v_w_xq': out['v_w_xq'], 'v_w_xkv': out['v_w_xkv'], 'v_w_xo': out['v_w_xo'], 'v_g_ffn': out['v_g_ffn'], 'v_w_up': out['v_w_up'], 'v_w_ffconv': out['v_w_ffconv'], 'v_w_down': out['v_w_down'], 'v_g_final': out['v_g_final']}


def _loss(weights, diff, rest, loss_target):
    with _jax.named_scope("forward"):
        args = {**rest, TWIN_DIFF_INPUT: diff, **{k: w.astype(_WEIGHT_DTYPES[k]) for k, w in weights.items()}}
        y = _forward(args)
    with _jax.named_scope("loss_head"):
        err = _jnp.square(y.astype(_jnp.float32) - loss_target)
        return 0.5 * _jnp.sum(_jnp.mean(err, axis=-1)) if err.ndim else 0.5 * err


def _adamw(w, g, m, v):
    m = ADAM_B1 * m + (1.0 - ADAM_B1) * g
    v = ADAM_B2 * v + (1.0 - ADAM_B2) * _jnp.square(g)
    m_hat = m / (1.0 - ADAM_B1 ** ADAM_STEP)
    v_hat = v / (1.0 - ADAM_B2 ** ADAM_STEP)
    delta = -ADAM_LR * (m_hat / (_jnp.sqrt(v_hat) + ADAM_EPS) + ADAM_WD * w)
    return delta, m, v


def reference(x, mem, positions, g_mix, w_in, b_forget, w_sconv, w_pool, pool_scale, w_out, g_xa, g_mem, w_xq, w_xkv, w_xo, g_ffn, w_up, w_ffconv, w_down, g_final, loss_target, m_g_mix, m_w_in, m_b_forget, m_w_sconv, m_w_pool, m_pool_scale, m_w_out, m_g_xa, m_g_mem, m_w_xq, m_w_xkv, m_w_xo, m_g_ffn, m_w_up, m_w_ffconv, m_w_down, m_g_final, v_g_mix, v_w_in, v_b_forget, v_w_sconv, v_w_pool, v_pool_scale, v_w_out, v_g_xa, v_g_mem, v_w_xq, v_w_xkv, v_w_xo, v_g_ffn, v_w_up, v_w_ffconv, v_w_down, v_g_final):
    given = dict(x=x, mem=mem, positions=positions, g_mix=g_mix, w_in=w_in, b_forget=b_forget, w_sconv=w_sconv, w_pool=w_pool, pool_scale=pool_scale, w_out=w_out, g_xa=g_xa, g_mem=g_mem, w_xq=w_xq, w_xkv=w_xkv, w_xo=w_xo, g_ffn=g_ffn, w_up=w_up, w_ffconv=w_ffconv, w_down=w_down, g_final=g_final, loss_target=loss_target, m_g_mix=m_g_mix, m_w_in=m_w_in, m_b_forget=m_b_forget, m_w_sconv=m_w_sconv, m_w_pool=m_w_pool, m_pool_scale=m_pool_scale, m_w_out=m_w_out, m_g_xa=m_g_xa, m_g_mem=m_g_mem, m_w_xq=m_w_xq, m_w_xkv=m_w_xkv, m_w_xo=m_w_xo, m_g_ffn=m_g_ffn, m_w_up=m_w_up, m_w_ffconv=m_w_ffconv, m_w_down=m_w_down, m_g_final=m_g_final, v_g_mix=v_g_mix, v_w_in=v_w_in, v_b_forget=v_b_forget, v_w_sconv=v_w_sconv, v_w_pool=v_w_pool, v_pool_scale=v_pool_scale, v_w_out=v_w_out, v_g_xa=v_g_xa, v_g_mem=v_g_mem, v_w_xq=v_w_xq, v_w_xkv=v_w_xkv, v_w_xo=v_w_xo, v_g_ffn=v_g_ffn, v_w_up=v_w_up, v_w_ffconv=v_w_ffconv, v_w_down=v_w_down, v_g_final=v_g_final)
    weights = {n: given[n] for n in TWIN_WEIGHTS}
    shared = {n: given[n] for n in SHARED_INPUTS}
    per_example = {n: given[n] for n in ['x', 'mem', 'positions']}
    grad_fn = _jax.value_and_grad(_loss, argnums=(0, 1))

    def one_microbatch(ex, loss_target):
        ex = dict(ex)
        diff = ex.pop(TWIN_DIFF_INPUT)
        return grad_fn(weights, diff, {**shared, **ex}, loss_target)

    if N_MICROBATCH == 1:
        loss, (grad_w, grad_x) = one_microbatch(per_example, given["loss_target"])
    else:
        def body(carry, xs):
            loss_sum, grad_sum = carry
            l_k, (gw_k, gx_k) = one_microbatch(xs[0], xs[1])
            with _jax.named_scope("update"):
                return (loss_sum + l_k, _jax.tree.map(_jnp.add, grad_sum, gw_k)), gx_k

        init = (_jnp.zeros((), _jnp.float32), _jax.tree.map(_jnp.zeros_like, weights))
        (loss, grad_w), grad_x = _jax.lax.scan(body, init, (per_example, given["loss_target"]))
    with _jax.named_scope("update"):
        delta_w, new_m, new_v = {}, {}, {}
        for n in TWIN_WEIGHTS:
            delta_w[n], new_m[n], new_v[n] = _adamw(weights[n], grad_w[n], given["m_" + n], given["v_" + n])
    return (loss, grad_x, *[grad_w[n] for n in TWIN_WEIGHTS], *[delta_w[n] for n in TWIN_WEIGHTS],
            *[new_m[n] for n in TWIN_WEIGHTS], *[new_v[n] for n in TWIN_WEIGHTS])
```

```python
import functools

import jax
import jax.numpy as jnp
from jax import lax
from jax.experimental import pallas as pl
from jax.experimental.pallas import tpu as pltpu

F32 = jnp.float32
BF16 = jnp.bfloat16

N_DEV = 8
D_MODEL = 1024
GROUP = 256
HEAD_DIM = 64
N_HEADS = 4
N_IN = 2564
N_IN_PAD = 2688
COL_GATE = 2560
XA_HEADS = 4
XA_DIM = 256
MEM_LEN = 256
D_FF = 2816
FF_SHARD = 704
FF_HALF = 4
ROPE_THETA = 500000.0
ROPE_DIM = 16
RMS_EPS = 1e-6
NEG = -1e30
POOL_WINDOWS = (2, 4, 8, 16)
ADAM_LR, ADAM_B1, ADAM_B2, ADAM_EPS, ADAM_WD, ADAM_STEP = 0.001, 0.9, 0.999, 1e-08, 0.01, 10

ROW_TILE = 512
ATT_TQ = 256
ATT_TK = 256
VMEM_LIMIT = 56 * 1024 * 1024
PACK_W = 1024
SMALL_ROWS = 616
ADAMW_BLOCK_BYTES = 4 * 1024 * 1024

MESH_AXES = ("x", "y", "c")


def _params(**kw):
    return pltpu.CompilerParams(vmem_limit_bytes=VMEM_LIMIT, **kw)


def _exchange(x, scatter, name):
    shard_shape = x.shape[1:] if scatter else x.shape

    def body(x_ref, out_ref, send_sems, recv_sems, local_sem):
        mx, my, mc = lax.axis_index("x"), lax.axis_index("y"), lax.axis_index("c")
        me = 4 * mx + 2 * my + mc
        mine = pltpu.make_async_copy(x_ref.at[me] if scatter else x_ref, out_ref.at[me], local_sem)
        mine.start()
        copies = []
        for k in range(1, N_DEV):
            kx, ky, kc = (k >> 2) & 1, (k >> 1) & 1, k & 1
            peer = (mx ^ kx, my ^ ky, mc ^ kc)
            peer_lin = me ^ k
            src = x_ref.at[peer_lin] if scatter else x_ref
            copies.append((pltpu.make_async_remote_copy(
                src_ref=src, dst_ref=out_ref.at[me], send_sem=send_sems.at[k - 1], recv_sem=recv_sems.at[k - 1],
                device_id=peer, device_id_type=pl.DeviceIdType.MESH), peer_lin, k))
        for cp, _, _ in copies:
            cp.start()
        for cp, peer_lin, k in copies:
            cp.wait_send()
            pltpu.make_async_remote_copy(
                src_ref=out_ref.at[peer_lin], dst_ref=out_ref.at[peer_lin], send_sem=send_sems.at[k - 1],
                recv_sem=recv_sems.at[k - 1], device_id=(mx, my, mc), device_id_type=pl.DeviceIdType.MESH).wait_recv()
        mine.wait()

    return pl.pallas_call(
        body, name=name,
        out_shape=jax.ShapeDtypeStruct((N_DEV,) + tuple(shard_shape), x.dtype),
        in_specs=[pl.BlockSpec(memory_space=pl.ANY)],
        out_specs=pl.BlockSpec(memory_space=pl.ANY),
        scratch_shapes=[pltpu.SemaphoreType.DMA((N_DEV - 1,)), pltpu.SemaphoreType.DMA((N_DEV - 1,)),
                        pltpu.SemaphoreType.DMA],
    )(x)


NN = ((1,), (0,))
NT = ((1,), (1,))
TN = ((0,), (0,))


def _matmul(a, b, out_shape, *, grid, a_spec, b_spec, o_spec, dims, nred, name, res=None, res_spec=None,
            out_dtype=F32):
    has_res = res is not None

    def body(*refs):
        a_ref, b_ref = refs[0], refs[1]
        r_ref = refs[2] if has_res else None
        o_ref = refs[3] if has_res else refs[2]
        part = lax.dot_general(a_ref[...].astype(BF16), b_ref[...].astype(BF16), (dims, ((), ())),
                               preferred_element_type=F32)
        if nred == 1:
            if has_res:
                part = part + r_ref[...]
            o_ref[...] = part.astype(o_ref.dtype)
        else:
            acc = refs[-1]
            r = pl.program_id(2)

            @pl.when(r == 0)
            def _():
                acc[...] = part

            @pl.when(r > 0)
            def _():
                acc[...] += part

            @pl.when(r == nred - 1)
            def _():
                tot = acc[...]
                if has_res:
                    tot = tot + r_ref[...]
                o_ref[...] = tot.astype(o_ref.dtype)

    in_specs = [a_spec, b_spec] + ([res_spec] if has_res else [])
    args = (a, b) + ((res,) if has_res else ())
    acc_shape = tuple(d for d in o_spec.block_shape if d is not None)
    return pl.pallas_call(
        body, name=name, grid=grid, out_shape=jax.ShapeDtypeStruct(out_shape, out_dtype),
        in_specs=in_specs, out_specs=o_spec,
        scratch_shapes=[pltpu.VMEM(acc_shape, F32)] if nred > 1 else [],
        compiler_params=_params(),
    )(*args)


def _mm_nn(a, w, name, res=None, tn=None):
    m, k = a.shape
    n = w.shape[1]
    tn = tn or n
    tm = min(ROW_TILE, m)
    ospec = pl.BlockSpec((tm, tn), lambda i, j, r: (i, j))
    return _matmul(a, w, (m, n), grid=(m // tm, n // tn, 1),
                   a_spec=pl.BlockSpec((tm, k), lambda i, j, r: (i, 0)),
                   b_spec=pl.BlockSpec((k, tn), lambda i, j, r: (0, j)),
                   o_spec=ospec, dims=NN, nred=1, name=name, res=res, res_spec=ospec if res is not None else None)


def _mm_nt(a, w, name, out_dtype=F32):
    m, n = a.shape
    k = w.shape[0]
    tm = min(ROW_TILE, m)
    return _matmul(a, w, (m, k), grid=(m // tm, 1, 1),
                   a_spec=pl.BlockSpec((tm, n), lambda i, j, r: (i, 0)),
                   b_spec=pl.BlockSpec((k, n), lambda i, j, r: (0, 0)),
                   o_spec=pl.BlockSpec((tm, k), lambda i, j, r: (i, 0)), dims=NT, nred=1, name=name,
                   out_dtype=out_dtype)


def _mm_tn(a, b, name, tk=512, tn=None, ts=1024):
    s, k = a.shape
    n = b.shape[1]
    tn = tn or n
    tk = min(tk, k)
    ts = min(ts, s)
    return _matmul(a, b, (k, n), grid=(k // tk, n // tn, s // ts),
                   a_spec=pl.BlockSpec((ts, tk), lambda i, j, r: (r, i)),
                   b_spec=pl.BlockSpec((ts, tn), lambda i, j, r: (r, j)),
                   o_spec=pl.BlockSpec((tk, tn), lambda i, j, r: (i, j)), dims=TN, nred=s // ts, name=name)


def _rms_fwd(h, g, name):
    s, d = h.shape
    tm = min(ROW_TILE, s)

    def body(h_ref, g_ref, o_ref):
        hv = h_ref[...]
        r = lax.rsqrt(jnp.mean(hv * hv, axis=-1, keepdims=True) + RMS_EPS)
        o_ref[...] = (hv * r * g_ref[...]).astype(o_ref.dtype)

    return pl.pallas_call(
        body, name=name, grid=(s // tm,), out_shape=jax.ShapeDtypeStruct((s, d), BF16),
        in_specs=[pl.BlockSpec((tm, d), lambda i: (i, 0)), pl.BlockSpec((1, d), lambda i: (0, 0))],
        out_specs=pl.BlockSpec((tm, d), lambda i: (i, 0)), compiler_params=_params(),
    )(h, g.reshape(1, d))


def _rms_bwd(dy, h, g, res, name):
    s, d = h.shape
    tm = min(ROW_TILE, s)
    has_res = res is not None

    def body(*refs):
        dy_ref, h_ref, g_ref = refs[:3]
        r_ref = refs[3] if has_res else None
        dh_ref, dg_ref = refs[-2], refs[-1]
        hv = h_ref[...]
        r = lax.rsqrt(jnp.mean(hv * hv, axis=-1, keepdims=True) + RMS_EPS)
        hn = hv * r
        dyv = dy_ref[...].astype(F32)
        u = dyv * g_ref[...]
        dh = r * (u - hn * jnp.mean(u * hn, axis=-1, keepdims=True))
        if has_res:
            dh = dh + r_ref[...]
        dh_ref[...] = dh
        part = jnp.sum(dyv * hn, axis=0, keepdims=True)

        @pl.when(pl.program_id(0) == 0)
        def _():
            dg_ref[...] = part

        @pl.when(pl.program_id(0) > 0)
        def _():
            dg_ref[...] += part

    row = pl.BlockSpec((tm, d), lambda i: (i, 0))
    vec = pl.BlockSpec((1, d), lambda i: (0, 0))
    dh, dg = pl.pallas_call(
        body, name=name, grid=(s // tm,),
        out_shape=(jax.ShapeDtypeStruct((s, d), F32), jax.ShapeDtypeStruct((1, d), F32)),
        in_specs=[row, row, vec] + ([row] if has_res else []),
        out_specs=(row, vec), compiler_params=_params(),
    )(*((dy, h, g.reshape(1, d)) + ((res,) if has_res else ())))
    return dh, dg.reshape(d)


def _loss_head(h, g, target):
    s, d = h.shape
    tm = min(ROW_TILE, s)

    def body(h_ref, g_ref, t_ref, loss_ref, dh_ref, dg_ref):
        hv = h_ref[...]
        r = lax.rsqrt(jnp.mean(hv * hv, axis=-1, keepdims=True) + RMS_EPS)
        hn = hv * r
        gv = g_ref[...]
        err = hn * gv - t_ref[...]
        rows = jnp.mean(err * err, axis=-1, keepdims=True)
        lpart = 0.5 * jnp.sum(rows, axis=0, keepdims=True) + jnp.zeros((1, 128), F32)
        dy = err * (1.0 / d)
        u = dy * gv
        dh_ref[...] = r * (u - hn * jnp.mean(u * hn, axis=-1, keepdims=True))
        gpart = jnp.sum(dy * hn, axis=0, keepdims=True)

        @pl.when(pl.program_id(0) == 0)
        def _():
            dg_ref[...] = gpart
            loss_ref[...] = lpart

        @pl.when(pl.program_id(0) > 0)
        def _():
            dg_ref[...] += gpart
            loss_ref[...] += lpart

    row = pl.BlockSpec((tm, d), lambda i: (i, 0))
    vec = pl.BlockSpec((1, d), lambda i: (0, 0))
    return pl.pallas_call(
        body, name="loss_head", grid=(s // tm,),
        out_shape=(jax.ShapeDtypeStruct((1, 128), F32), jax.ShapeDtypeStruct((s, d), F32),
                   jax.ShapeDtypeStruct((1, d), F32)),
        in_specs=[row, vec, row],
        out_specs=(pl.BlockSpec((1, 128), lambda i: (0, 0)), row, vec), compiler_params=_params(),
    )(h, g.reshape(1, d), target)


def _shift_down(x, k):
    return pltpu.roll(x, k, 0)


def _shift_up(x, k):
    return pltpu.roll(x, x.shape[0] - k, 0)


def _conv3(x, w):
    return w[2:3, :] * x + w[1:2, :] * _shift_down(x, 1) + w[0:1, :] * _shift_down(x, 2)


def _conv3_t(x, w):
    return w[2:3, :] * x + w[1:2, :] * _shift_up(x, 1) + w[0:1, :] * _shift_up(x, 2)


def _sigmoid(x):
    return 1.0 / (1.0 + jnp.exp(-x))


def _prev_map(tile, halo, col):
    return lambda i: (jnp.maximum(i * (tile // halo) - 1, 0), col)


def _next_map(tile, halo, col, nrows):
    return lambda i: (jnp.minimum((i + 1) * (tile // halo), nrows // halo - 1), col)


def _sconv_fwd(proj, w):
    s = proj.shape[0]
    t = min(ROW_TILE, s)

    def body(cur_ref, prev_ref, w_ref, o_ref):
        i = pl.program_id(0)
        prev = prev_ref[...] * (i > 0).astype(F32)
        ext = jnp.concatenate([prev, cur_ref[...]], axis=0)
        sv = ext[:, 2 * GROUP:3 * GROUP] * ext[:, 0:GROUP]
        y = ext[:, GROUP:2 * GROUP] * _conv3(sv, w_ref[...])
        o_ref[...] = y[8:].astype(o_ref.dtype)

    return pl.pallas_call(
        body, name="sconv_fwd", grid=(s // t,), out_shape=jax.ShapeDtypeStruct((s, GROUP), BF16),
        in_specs=[pl.BlockSpec((t, 3 * GROUP), lambda i: (i, 0)),
                  pl.BlockSpec((8, 3 * GROUP), _prev_map(t, 8, 0)),
                  pl.BlockSpec((3, GROUP), lambda i: (0, 0))],
        out_specs=pl.BlockSpec((t, GROUP), lambda i: (i, 0)), compiler_params=_params(),
    )(proj, proj, w)


def _sconv_bwd(proj, w, dy):
    s = proj.shape[0]
    t = min(ROW_TILE, s)
    nt = s // t

    def body(cur_ref, prev_ref, next_ref, w_ref, dy_ref, dyn_ref, dp_ref, dw_ref):
        i = pl.program_id(0)
        first = (i > 0).astype(F32)
        last = (i < nt - 1).astype(F32)
        ext = jnp.concatenate([prev_ref[...] * first, cur_ref[...], next_ref[...] * last], axis=0)
        dye = jnp.concatenate([jnp.zeros((8, GROUP), F32), dy_ref[...], dyn_ref[...] * last], axis=0)
        hv, bv, cv = ext[:, 0:GROUP], ext[:, GROUP:2 * GROUP], ext[:, 2 * GROUP:3 * GROUP]
        wv = w_ref[...]
        sv = cv * hv
        conv = _conv3(sv, wv)
        dconv = dye * bv
        ds = _conv3_t(dconv, wv)
        dp = jnp.concatenate([ds * cv, dye * conv, ds * hv], axis=1)
        dp_ref[...] = dp[8:8 + t].astype(dp_ref.dtype)
        dc = dconv[8:8 + t]
        dw = jnp.concatenate([
            jnp.sum(dc * _shift_down(sv, 2)[8:8 + t], axis=0, keepdims=True),
            jnp.sum(dc * _shift_down(sv, 1)[8:8 + t], axis=0, keepdims=True),
            jnp.sum(dc * sv[8:8 + t], axis=0, keepdims=True),
            jnp.zeros((5, GROUP), F32)], axis=0)

        @pl.when(i == 0)
        def _():
            dw_ref[...] = dw

        @pl.when(i > 0)
        def _():
            dw_ref[...] += dw

    dp, dw = pl.pallas_call(
        body, name="sconv_bwd", grid=(nt,),
        out_shape=(jax.ShapeDtypeStruct((s, 3 * GROUP), BF16), jax.ShapeDtypeStruct((8, GROUP), F32)),
        in_specs=[pl.BlockSpec((t, 3 * GROUP), lambda i: (i, 0)),
                  pl.BlockSpec((8, 3 * GROUP), _prev_map(t, 8, 0)),
                  pl.BlockSpec((8, 3 * GROUP), _next_map(t, 8, 0, s)),
                  pl.BlockSpec((3, GROUP), lambda i: (0, 0)),
                  pl.BlockSpec((t, GROUP), lambda i: (i, 0)),
                  pl.BlockSpec((8, GROUP), _next_map(t, 8, 0, s))],
        out_specs=(pl.BlockSpec((t, 3 * GROUP), lambda i: (i, 0)), pl.BlockSpec((8, GROUP), lambda i: (0, 0))),
        compiler_params=_params(),
    )(proj, proj, proj, w, dy, dy)
    return dp, dw[:3]


def _lane_window(shape):
    lane = lax.broadcasted_iota(jnp.int32, shape, 1)
    return lane, jnp.where(lane < 64, 2.0, jnp.where(lane < 128, 4.0, jnp.where(lane < 192, 8.0, 16.0)))


def _by_group(lane, s1, s2, s3, s4):
    return jnp.where(lane < 64, s1, jnp.where(lane < 128, s2, jnp.where(lane < 192, s3, s4)))


def _pool_z(ext, row0):
    s1 = ext + _shift_down(ext, 1)
    s2 = s1 + _shift_down(s1, 2)
    s3 = s2 + _shift_down(s2, 4)
    s4 = s3 + _shift_down(s3, 8)
    lane, win = _lane_window(ext.shape)
    tpos = (lax.broadcasted_iota(jnp.int32, ext.shape, 0) + (row0 - 16 + 1)).astype(F32)
    cnt = jnp.maximum(jnp.minimum(tpos, win), 1.0)
    return _by_group(lane, s1, s2, s3, s4) / cnt - ext


def _pool_fwd(proj, wbd, scale):
    s = proj.shape[0]
    t = min(ROW_TILE, s)
    col = (COL_GATE - GROUP) // GROUP

    def body(cur_ref, prev_ref, w_ref, sc_ref, o_ref):
        i = pl.program_id(0)
        ext = jnp.concatenate([prev_ref[...] * (i > 0).astype(F32), cur_ref[...]], axis=0)
        z = _pool_z(ext, i * t)[16:]
        y = jnp.dot(z.astype(BF16), w_ref[...].astype(BF16), preferred_element_type=F32)
        o_ref[...] = (y * sc_ref[...]).astype(o_ref.dtype)

    return pl.pallas_call(
        body, name="pool_fwd", grid=(s // t,), out_shape=jax.ShapeDtypeStruct((s, GROUP), BF16),
        in_specs=[pl.BlockSpec((t, GROUP), lambda i: (i, col)),
                  pl.BlockSpec((16, GROUP), _prev_map(t, 16, col)),
                  pl.BlockSpec((GROUP, GROUP), lambda i: (0, 0)),
                  pl.BlockSpec((1, GROUP), lambda i: (0, 0))],
        out_specs=pl.BlockSpec((t, GROUP), lambda i: (i, 0)), compiler_params=_params(),
    )(proj, proj, wbd, scale.reshape(1, GROUP))


def _pool_bwd(proj, wbd, scale, dy):
    s = proj.shape[0]
    t = min(ROW_TILE, s)
    nt = s // t
    col = (COL_GATE - GROUP) // GROUP

    def body(cur_ref, prev_ref, w_ref, sc_ref, dy_ref, dyn_ref, dp_ref, dw_ref, dsc_ref):
        i = pl.program_id(0)
        ext = jnp.concatenate([prev_ref[...] * (i > 0).astype(F32), cur_ref[...]], axis=0)
        z = _pool_z(ext, i * t)[16:]
        wv = w_ref[...].astype(BF16)
        dyc = dy_ref[...]
        dye = jnp.concatenate([dyc, dyn_ref[...] * (i < nt - 1).astype(F32)], axis=0) * sc_ref[...]
        dz = lax.dot_general(dye.astype(BF16), wv, (NT, ((), ())), preferred_element_type=F32)
        lane, win = _lane_window(dz.shape)
        tpos = (lax.broadcasted_iota(jnp.int32, dz.shape, 0) + (i * t + 1)).astype(F32)
        e = dz / jnp.minimum(tpos, win)
        f1 = e + _shift_up(e, 1)
        f2 = f1 + _shift_up(f1, 2)
        f3 = f2 + _shift_up(f2, 4)
        f4 = f3 + _shift_up(f3, 8)
        dp = _by_group(lane, f1, f2, f3, f4) - dz
        dp_ref[...] = dp[:t].astype(dp_ref.dtype)
        zb = z.astype(BF16)
        y = jnp.dot(zb, wv, preferred_element_type=F32)
        dsc = jnp.sum(dyc * y, axis=0, keepdims=True)
        dw = lax.dot_general(zb, dye[:t].astype(BF16), (TN, ((), ())), preferred_element_type=F32)

        @pl.when(i == 0)
        def _():
            dw_ref[...] = dw
            dsc_ref[...] = dsc

        @pl.when(i > 0)
        def _():
            dw_ref[...] += dw
            dsc_ref[...] += dsc

    dp, dw, dsc = pl.pallas_call(
        body, name="pool_bwd", grid=(nt,),
        out_shape=(jax.ShapeDtypeStruct((s, GROUP), BF16), jax.ShapeDtypeStruct((GROUP, GROUP), F32),
                   jax.ShapeDtypeStruct((1, GROUP), F32)),
        in_specs=[pl.BlockSpec((t, GROUP), lambda i: (i, col)),
                  pl.BlockSpec((16, GROUP), _prev_map(t, 16, col)),
                  pl.BlockSpec((GROUP, GROUP), lambda i: (0, 0)),
                  pl.BlockSpec((1, GROUP), lambda i: (0, 0)),
                  pl.BlockSpec((t, GROUP), lambda i: (i, 0)),
                  pl.BlockSpec((16, GROUP), _next_map(t, 16, 0, s))],
        out_specs=(pl.BlockSpec((t, GROUP), lambda i: (i, 0)), pl.BlockSpec((GROUP, GROUP), lambda i: (0, 0)),
                   pl.BlockSpec((1, GROUP), lambda i: (0, 0))),
        compiler_params=_params(),
    )(proj, proj, wbd, scale.reshape(1, GROUP), dy, dy)
    return dp, dw, dsc.reshape(GROUP)


def _ffn_gate_fwd(u0, w):
    s = u0.shape[1]
    t = min(ROW_TILE, s)

    def body(a_ref, ap_ref, g_ref, gp_ref, wa_ref, wg_ref, o_ref):
        first = (pl.program_id(1) > 0).astype(F32)
        a = _conv3(jnp.concatenate([ap_ref[...] * first, a_ref[...]], axis=0), wa_ref[...])[8:]
        g = _conv3(jnp.concatenate([gp_ref[...] * first, g_ref[...]], axis=0), wg_ref[...])[8:]
        o_ref[...] = (a * (g * _sigmoid(g))).astype(o_ref.dtype)

    def cur(off):
        return pl.BlockSpec((None, t, FF_SHARD), lambda j, i: (j + off, i, 0))

    def prev(off):
        return pl.BlockSpec((None, 8, FF_SHARD), lambda j, i: (j + off, jnp.maximum(i * (t // 8) - 1, 0), 0))

    def wspec(off):
        return pl.BlockSpec((None, 3, FF_SHARD), lambda j, i: (j + off, 0, 0))

    return pl.pallas_call(
        body, name="ffn_gate_fwd", grid=(FF_HALF, s // t),
        out_shape=jax.ShapeDtypeStruct((FF_HALF, s, FF_SHARD), BF16),
        in_specs=[cur(0), prev(0), cur(FF_HALF), prev(FF_HALF), wspec(0), wspec(FF_HALF)],
        out_specs=pl.BlockSpec((None, t, FF_SHARD), lambda j, i: (j, i, 0)), compiler_params=_params(),
    )(u0, u0, u0, u0, w, w)


def _ffn_gate_bwd(u0, w, dact):
    s = u0.shape[1]
    t = min(ROW_TILE, s)
    nt = s // t

    def body(sc_ref, sp_ref, sn_ref, pc_ref, pp_ref, pn_ref, ws_ref, wp_ref, d_ref, dn_ref, du_ref, dw_ref):
        j, i = pl.program_id(0), pl.program_id(1)
        first = (i > 0).astype(F32)
        last = (i < nt - 1).astype(F32)
        self_ext = jnp.concatenate([sp_ref[...] * first, sc_ref[...], sn_ref[...] * last], axis=0)
        part_ext = jnp.concatenate([pp_ref[...] * first, pc_ref[...], pn_ref[...] * last], axis=0)
        dext = jnp.concatenate([jnp.zeros((8, FF_SHARD), F32), d_ref[...], dn_ref[...] * last], axis=0)
        wsv = ws_ref[...]
        conv_self = _conv3(self_ext, wsv)
        conv_part = _conv3(part_ext, wp_ref[...])
        is_a = j < FF_HALF
        a = jnp.where(is_a, conv_self, conv_part)
        g = jnp.where(is_a, conv_part, conv_self)
        sg = _sigmoid(g)
        du = dext * jnp.where(is_a, g * sg, a * sg * (1.0 + g * (1.0 - sg)))
        du_ref[...] = _conv3_t(du, wsv)[8:8 + t].astype(du_ref.dtype)
        dc = du[8:8 + t]
        dw = jnp.concatenate([
            jnp.sum(dc * _shift_down(self_ext, 2)[8:8 + t], axis=0, keepdims=True),
            jnp.sum(dc * _shift_down(self_ext, 1)[8:8 + t], axis=0, keepdims=True),
            jnp.sum(dc * self_ext[8:8 + t], axis=0, keepdims=True),
            jnp.zeros((5, FF_SHARD), F32)], axis=0)

        @pl.when(i == 0)
        def _():
            dw_ref[...] = dw

        @pl.when(i > 0)
        def _():
            dw_ref[...] += dw

    def slab(which):
        if which == "self":
            return lambda j: j
        if which == "partner":
            return lambda j: (j + FF_HALF) % (2 * FF_HALF)
        return lambda j: j % FF_HALF

    def cur(which):
        f = slab(which)
        return pl.BlockSpec((None, t, FF_SHARD), lambda j, i: (f(j), i, 0))

    def prev(which):
        f = slab(which)
        return pl.BlockSpec((None, 8, FF_SHARD), lambda j, i: (f(j), jnp.maximum(i * (t // 8) - 1, 0), 0))

    def nxt(which):
        f = slab(which)
        return pl.BlockSpec((None, 8, FF_SHARD),
                            lambda j, i: (f(j), jnp.minimum((i + 1) * (t // 8), s // 8 - 1), 0))

    def wspec(which):
        f = slab(which)
        return pl.BlockSpec((None, 3, FF_SHARD), lambda j, i: (f(j), 0, 0))

    du, dw = pl.pallas_call(
        body, name="ffn_gate_bwd", grid=(2 * FF_HALF, nt),
        out_shape=(jax.ShapeDtypeStruct((2 * FF_HALF, s, FF_SHARD), BF16),
                   jax.ShapeDtypeStruct((2 * FF_HALF, 8, FF_SHARD), F32)),
        in_specs=[cur("self"), prev("self"), nxt("self"), cur("partner"), prev("partner"), nxt("partner"),
                  wspec("self"), wspec("partner"), cur("dact"), nxt("dact")],
        out_specs=(pl.BlockSpec((None, t, FF_SHARD), lambda j, i: (j, i, 0)),
                   pl.BlockSpec((None, 8, FF_SHARD), lambda j, i: (j, 0, 0))),
        compiler_params=_params(),
    )(u0, u0, u0, u0, u0, u0, w, w, dact, dact)
    return du, dw[:, :3]


def _rope_tables(positions):
    inv_freq = ROPE_THETA ** (-jnp.arange(0, ROPE_DIM, 2, dtype=F32) / ROPE_DIM)
    ang = positions.astype(F32)[:, None] * inv_freq
    cos, sin = jnp.cos(ang), jnp.sin(ang)
    s = positions.shape[0]
    half = ROPE_DIM // 2
    rest = HEAD_DIM - ROPE_DIM
    ca = jnp.concatenate([cos, cos, jnp.ones((s, rest), F32)], axis=1)
    cb = jnp.concatenate([-sin, jnp.zeros((s, HEAD_DIM - half), F32)], axis=1)
    cc = jnp.concatenate([jnp.zeros((s, half), F32), sin, jnp.zeros((s, rest), F32)], axis=1)
    return tuple(jnp.tile(tb, (1, N_HEADS)) for tb in (ca, cb, cc))


def _heads_split(proj, col, tables, name):
    s = proj.shape[0]
    t = min(ROW_TILE, s)
    rope = tables is not None

    def body(*refs):
        x_ref = refs[0]
        q_ref, k_ref, v_ref = refs[-3:]
        xv = x_ref[...]
        parts = [xv[:, 0:GROUP], xv[:, GROUP:2 * GROUP], xv[:, 2 * GROUP:3 * GROUP]]
        if rope:
            ca, cb, cc = refs[1][...], refs[2][...], refs[3][...]
            for n in range(2):
                p = parts[n]
                parts[n] = p * ca + pltpu.roll(p, GROUP - 8, 1) * cb + pltpu.roll(p, 8, 1) * cc
        parts[0] = parts[0] * (HEAD_DIM ** -0.5)
        for o_ref, p in zip((q_ref, k_ref, v_ref), parts):
            for h in range(N_HEADS):
                o_ref[h] = p[:, h * HEAD_DIM:(h + 1) * HEAD_DIM].astype(o_ref.dtype)

    tab = pl.BlockSpec((t, GROUP), lambda i: (i, 0))
    heads = pl.BlockSpec((N_HEADS, t, HEAD_DIM), lambda i: (0, i, 0))
    hshape = jax.ShapeDtypeStruct((N_HEADS, s, HEAD_DIM), BF16)
    return pl.pallas_call(
        body, name=name, grid=(s // t,), out_shape=(hshape, hshape, hshape),
        in_specs=[pl.BlockSpec((t, 3 * GROUP), lambda i: (i, col))] + ([tab, tab, tab] if rope else []),
        out_specs=(heads, heads, heads), compiler_params=_params(),
    )(*((proj,) + (tuple(tables) if rope else ())))


def _heads_merge(dq, dk, dv, tables, name):
    s = dq.shape[1]
    t = min(ROW_TILE, s)
    rope = tables is not None

    def body(*refs):
        o_ref = refs[-1]
        parts = [jnp.concatenate([r[h] for h in range(N_HEADS)], axis=1) for r in refs[:3]]
        parts[0] = parts[0] * (HEAD_DIM ** -0.5)
        if rope:
            ca, cb, cc = refs[3][...], refs[4][...], refs[5][...]
            for n in range(2):
                p = parts[n]
                parts[n] = p * ca + pltpu.roll(p * cb, 8, 1) + pltpu.roll(p * cc, GROUP - 8, 1)
        o_ref[...] = jnp.concatenate(parts, axis=1).astype(o_ref.dtype)

    tab = pl.BlockSpec((t, GROUP), lambda i: (i, 0))
    heads = pl.BlockSpec((N_HEADS, t, HEAD_DIM), lambda i: (0, i, 0))
    return pl.pallas_call(
        body, name=name, grid=(s // t,), out_shape=jax.ShapeDtypeStruct((s, 3 * GROUP), BF16),
        in_specs=[heads, heads, heads] + ([tab, tab, tab] if rope else []),
        out_specs=pl.BlockSpec((t, 3 * GROUP), lambda i: (i, 0)), compiler_params=_params(),
    )(*((dq, dk, dv) + (tuple(tables) if rope else ())))


def _log_sigmoid(x):
    return jnp.minimum(x, 0.0) - jnp.log(1.0 + jnp.exp(-jnp.abs(x)))


def _scan_rows(x, reverse):
    n = x.shape[0]
    row = lax.broadcasted_iota(jnp.int32, x.shape, 0)
    k = 1
    while k < n:
        if reverse:
            x = x + jnp.where(row < n - k, _shift_up(x, k), 0.0)
        else:
            x = x + jnp.where(row >= k, _shift_down(x, k), 0.0)
        k *= 2
    return x


def _gate_cumsum(proj, bias):
    s = proj.shape[0]
    col = COL_GATE // 128

    def body(z_ref, b_ref, c_ref):
        c_ref[...] = _scan_rows(_log_sigmoid(z_ref[...] + b_ref[...]), False)

    return pl.pallas_call(
        body, name="gate_cumsum", grid=(1,), out_shape=jax.ShapeDtypeStruct((s, 128), F32),
        in_specs=[pl.BlockSpec((s, 128), lambda i: (0, col)), pl.BlockSpec((1, 128), lambda i: (0, 0))],
        out_specs=pl.BlockSpec((s, 128), lambda i: (0, 0)), compiler_params=_params(),
    )(proj, bias)


def _gate_cumsum_bwd(proj, bias, dc):
    s = proj.shape[0]
    col = COL_GATE // 128

    def body(z_ref, b_ref, dc_ref, dz_ref, db_ref):
        dlogf = _scan_rows(dc_ref[...], True)
        dz = dlogf * _sigmoid(-(z_ref[...] + b_ref[...]))
        dz_ref[...] = dz.astype(dz_ref.dtype)
        db_ref[...] = jnp.sum(dz, axis=0, keepdims=True)

    return pl.pallas_call(
        body, name="gate_cumsum_bwd", grid=(1,),
        out_shape=(jax.ShapeDtypeStruct((s, 128), BF16), jax.ShapeDtypeStruct((1, 128), F32)),
        in_specs=[pl.BlockSpec((s, 128), lambda i: (0, col)), pl.BlockSpec((1, 128), lambda i: (0, 0)),
                  pl.BlockSpec((s, 128), lambda i: (0, 0))],
        out_specs=(pl.BlockSpec((s, 128), lambda i: (0, 0)), pl.BlockSpec((1, 128), lambda i: (0, 0))),
        compiler_params=_params(),
    )(proj, bias, dc)


DIL_REACH = 2048


def _pair_weight(mode, d):
    if mode == "fox":
        return jnp.where(d >= 0, 1.0, 0.0)
    w1 = jnp.where(jnp.abs(d - 64) <= 64, 1.0, 0.0)
    w2 = jnp.where((d & 3) == 0, jnp.where(jnp.abs(d - 256) <= 256, 1.0, 0.0), 0.0)
    w3 = jnp.where((d & 15) == 0, jnp.where(jnp.abs(d - 1024) <= 1024, 1.0, 0.0), 0.0)
    return w1 + w2 + w3


def _first_key_chunk(mode, q0, tk):
    if mode == "fox":
        return 0
    return jnp.maximum(q0 - DIL_REACH, 0) // tk


def _attn_fwd(mode, q, k, v, c_col, c_row):
    s = q.shape[1]
    tq, tk = min(ATT_TQ, s), min(ATT_TK, s)
    fox = mode == "fox"

    def body(*refs):
        q_ref, k_ref, v_ref = refs[:3]
        cc_ref, cr_ref = (refs[3], refs[4]) if fox else (None, None)
        y_ref, o_ref, lse_ref = refs[-3:]
        i = pl.program_id(0)
        q0 = i * tq
        lo = _first_key_chunk(mode, q0, tk)
        hi = (q0 + tq + tk - 1) // tk
        rows = lax.broadcasted_iota(jnp.int32, (tq, tk), 0) + q0
        cols = lax.broadcasted_iota(jnp.int32, (tq, tk), 1)
        for h in range(N_HEADS):
            qv = q_ref[h]
            ccol = cc_ref[h] if fox else None

            def step(c, carry, h=h, qv=qv, ccol=ccol):
                m, l, acc = carry
                k0 = pl.multiple_of(c * tk, tk)
                kv = k_ref[h, pl.ds(k0, tk), :]
                vv = v_ref[h, pl.ds(k0, tk), :]
                sc = lax.dot_general(qv, kv, (NT, ((), ())), preferred_element_type=F32)
                if fox:
                    sc = sc + ccol - cr_ref[h, :, pl.ds(k0, tk)]
                w = _pair_weight(mode, rows - (cols + k0))
                sc = jnp.where(w > 0.0, sc, NEG)
                m_new = jnp.maximum(m, jnp.max(sc, axis=-1, keepdims=True))
                alpha = jnp.exp(m - m_new)
                p = w * jnp.exp(sc - m_new)
                l = alpha * l + jnp.sum(p, axis=-1, keepdims=True)
                acc = alpha * acc + jnp.dot(p.astype(BF16), vv, preferred_element_type=F32)
                return m_new, l, acc

            m, l, acc = lax.fori_loop(
                lo, hi, step,
                (jnp.full((tq, 1), NEG, F32), jnp.zeros((tq, 1), F32), jnp.zeros((tq, HEAD_DIM), F32)))
            out = acc / l
            y_ref[:, h * HEAD_DIM:(h + 1) * HEAD_DIM] = out.astype(y_ref.dtype)
            o_ref[:, h * HEAD_DIM:(h + 1) * HEAD_DIM] = out
            lse_ref[h] = m + jnp.log(l)

    qspec = pl.BlockSpec((N_HEADS, tq, HEAD_DIM), lambda i: (0, i, 0))
    full = pl.BlockSpec((N_HEADS, s, HEAD_DIM), lambda i: (0, 0, 0))
    colspec = pl.BlockSpec((N_HEADS, tq, 1), lambda i: (0, i, 0))
    rowfull = pl.BlockSpec((N_HEADS, 1, s), lambda i: (0, 0, 0))
    out_row = pl.BlockSpec((tq, GROUP), lambda i: (i, 0))
    return pl.pallas_call(
        body, name="attn_fwd_" + mode, grid=(s // tq,),
        out_shape=(jax.ShapeDtypeStruct((s, GROUP), BF16), jax.ShapeDtypeStruct((s, GROUP), F32),
                   jax.ShapeDtypeStruct((N_HEADS, s, 1), F32)),
        in_specs=[qspec, full, full] + ([colspec, rowfull] if fox else []),
        out_specs=(out_row, out_row, colspec), compiler_params=_params(),
    )(*((q, k, v) + ((c_col, c_row) if fox else ())))


def _attn_bwd_q(mode, q, k, v, o, do, lse, c_col, c_row):
    s = q.shape[1]
    tq, tk = min(ATT_TQ, s), min(ATT_TK, s)
    fox = mode == "fox"

    def body(*refs):
        q_ref, k_ref, v_ref, o_ref, do_ref, lse_ref = refs[:6]
        cc_ref, cr_ref = (refs[6], refs[7]) if fox else (None, None)
        dq_ref, delta_ref, dcr_ref = refs[-3:]
        i = pl.program_id(0)
        q0 = i * tq
        lo = _first_key_chunk(mode, q0, tk)
        hi = (q0 + tq + tk - 1) // tk
        rows = lax.broadcasted_iota(jnp.int32, (tq, tk), 0) + q0
        cols = lax.broadcasted_iota(jnp.int32, (tq, tk), 1)
        for h in range(N_HEADS):
            qv = q_ref[h]
            dov = do_ref[:, h * HEAD_DIM:(h + 1) * HEAD_DIM]
            delta = jnp.sum(dov * o_ref[:, h * HEAD_DIM:(h + 1) * HEAD_DIM], axis=-1, keepdims=True)
            dob = dov.astype(BF16)
            lse = lse_ref[h]
            ccol = cc_ref[h] if fox else None

            def step(c, carry, h=h, qv=qv, dob=dob, delta=delta, lse=lse, ccol=ccol):
                dq, dcr = carry
                k0 = pl.multiple_of(c * tk, tk)
                kv = k_ref[h, pl.ds(k0, tk), :]
                vv = v_ref[h, pl.ds(k0, tk), :]
                sc = lax.dot_general(qv, kv, (NT, ((), ())), preferred_element_type=F32)
                if fox:
                    sc = sc + ccol - cr_ref[h, :, pl.ds(k0, tk)]
                w = _pair_weight(mode, rows - (cols + k0))
                p = w * jnp.exp(jnp.where(w > 0.0, sc, NEG) - lse)
                dp = lax.dot_general(dob, vv, (NT, ((), ())), preferred_element_type=F32)
                ds = p * (dp - delta)
                dq = dq + jnp.dot(ds.astype(BF16), kv, preferred_element_type=F32)
                if fox:
                    dcr = dcr + jnp.sum(ds, axis=-1, keepdims=True)
                return dq, dcr

            dq, dcr = lax.fori_loop(lo, hi, step, (jnp.zeros((tq, HEAD_DIM), F32), jnp.zeros((tq, 1), F32)))
            dq_ref[h] = dq
            delta_ref[h] = delta
            dcr_ref[h] = dcr

    qspec = pl.BlockSpec((N_HEADS, tq, HEAD_DIM), lambda i: (0, i, 0))
    full = pl.BlockSpec((N_HEADS, s, HEAD_DIM), lambda i: (0, 0, 0))
    colspec = pl.BlockSpec((N_HEADS, tq, 1), lambda i: (0, i, 0))
    rowfull = pl.BlockSpec((N_HEADS, 1, s), lambda i: (0, 0, 0))
    row = pl.BlockSpec((tq, GROUP), lambda i: (i, 0))
    col_shape = jax.ShapeDtypeStruct((N_HEADS, s, 1), F32)
    return pl.pallas_call(
        body, name="attn_bwd_q_" + mode, grid=(s // tq,),
        out_shape=(jax.ShapeDtypeStruct((N_HEADS, s, HEAD_DIM), F32), col_shape, col_shape),
        in_specs=[qspec, full, full, row, row, colspec] + ([colspec, rowfull] if fox else []),
        out_specs=(qspec, colspec, colspec), compiler_params=_params(),
    )(*((q, k, v, o, do, lse) + ((c_col, c_row) if fox else ())))


def _attn_bwd_kv(mode, q, k, v, do, lse_row, delta_row, c_col, c_row):
    s = q.shape[1]
    tq, tk = min(ATT_TQ, s), min(ATT_TK, s)
    nq = s // tq
    fox = mode == "fox"

    def body(*refs):
        q_ref, k_ref, v_ref, do_ref, lse_ref, delta_ref = refs[:6]
        cc_ref, cr_ref = (refs[6], refs[7]) if fox else (None, None)
        dk_ref, dv_ref, dcc_ref = refs[-3:]
        i = pl.program_id(0)
        k0 = i * tk
        lo = k0 // tq
        hi = nq if fox else jnp.minimum((k0 + tk - 1 + DIL_REACH) // tq + 1, nq)
        keys = lax.broadcasted_iota(jnp.int32, (tk, tq), 0) + k0
        qcols = lax.broadcasted_iota(jnp.int32, (tk, tq), 1)
        for h in range(N_HEADS):
            kv = k_ref[h]
            vv = v_ref[h]
            ccol = cc_ref[h] if fox else None

            def step(c, carry, h=h, kv=kv, vv=vv, ccol=ccol):
                dk, dv, dcc = carry
                q0 = pl.multiple_of(c * tq, tq)
                qv = q_ref[h, pl.ds(q0, tq), :]
                dob = do_ref[pl.ds(q0, tq), h * HEAD_DIM:(h + 1) * HEAD_DIM].astype(BF16)
                sc = lax.dot_general(kv, qv, (NT, ((), ())), preferred_element_type=F32)
                if fox:
                    sc = sc + cr_ref[h, :, pl.ds(q0, tq)] - ccol
                w = _pair_weight(mode, (qcols + q0) - keys)
                p = w * jnp.exp(jnp.where(w > 0.0, sc, NEG) - lse_ref[h, :, pl.ds(q0, tq)])
                dp = lax.dot_general(vv, dob, (NT, ((), ())), preferred_element_type=F32)
                ds = p * (dp - delta_ref[h, :, pl.ds(q0, tq)])
                dv = dv + jnp.dot(p.astype(BF16), dob, preferred_element_type=F32)
                dk = dk + jnp.dot(ds.astype(BF16), qv, preferred_element_type=F32)
                if fox:
                    dcc = dcc + jnp.sum(ds, axis=-1, keepdims=True)
                return dk, dv, dcc

            dk, dv, dcc = lax.fori_loop(
                lo, hi, step,
                (jnp.zeros((tk, HEAD_DIM), F32), jnp.zeros((tk, HEAD_DIM), F32), jnp.zeros((tk, 1), F32)))
            dk_ref[h] = dk
            dv_ref[h] = dv
            dcc_ref[h] = dcc

    kspec = pl.BlockSpec((N_HEADS, tk, HEAD_DIM), lambda i: (0, i, 0))
    full = pl.BlockSpec((N_HEADS, s, HEAD_DIM), lambda i: (0, 0, 0))
    colspec = pl.BlockSpec((N_HEADS, tk, 1), lambda i: (0, i, 0))
    rowfull = pl.BlockSpec((N_HEADS, 1, s), lambda i: (0, 0, 0))
    dofull = pl.BlockSpec((s, GROUP), lambda i: (0, 0))
    hshape = jax.ShapeDtypeStruct((N_HEADS, s, HEAD_DIM), F32)
    return pl.pallas_call(
        body, name="attn_bwd_kv_" + mode, grid=(s // tk,),
        out_shape=(hshape, hshape, jax.ShapeDtypeStruct((N_HEADS, s, 1), F32)),
        in_specs=[full, kspec, kspec, dofull, rowfull, rowfull] + ([colspec, rowfull] if fox else []),
        out_specs=(kspec, kspec, colspec), compiler_params=_params(),
    )(*((q, k, v, do, lse_row, delta_row) + ((c_col, c_row) if fox else ())))


def _xattn_fwd(qx, kvm):
    s = qx.shape[0]
    t = min(ROW_TILE, s)

    def body(q_ref, kv_ref, o_ref):
        for h in range(XA_HEADS):
            qv = q_ref[:, h * XA_DIM:(h + 1) * XA_DIM].astype(BF16)
            kv = kv_ref[h].astype(BF16)
            vv = kv_ref[XA_HEADS + h].astype(BF16)
            sc = lax.dot_general(qv, kv, (NT, ((), ())), preferred_element_type=F32) * (XA_DIM ** -0.5)
            e = jnp.exp(sc - jnp.max(sc, axis=-1, keepdims=True))
            p = e / jnp.sum(e, axis=-1, keepdims=True)
            o_ref[:, h * XA_DIM:(h + 1) * XA_DIM] = jnp.dot(p.astype(BF16), vv,
                                                             preferred_element_type=F32).astype(o_ref.dtype)

    return pl.pallas_call(
        body, name="xattn_fwd", grid=(s // t,), out_shape=jax.ShapeDtypeStruct((s, D_MODEL), BF16),
        in_specs=[pl.BlockSpec((t, D_MODEL), lambda i: (i, 0)),
                  pl.BlockSpec((2 * XA_HEADS, MEM_LEN, XA_DIM), lambda i: (0, 0, 0))],
        out_specs=pl.BlockSpec((t, D_MODEL), lambda i: (i, 0)), compiler_params=_params(),
    )(qx, kvm)


def _xattn_bwd(qx, kvm, do):
    s = qx.shape[0]
    t = min(ROW_TILE, s)

    def body(q_ref, kv_ref, do_ref, dq_ref, dkv_ref):
        i = pl.program_id(0)
        for h in range(XA_HEADS):
            qv = q_ref[:, h * XA_DIM:(h + 1) * XA_DIM].astype(BF16)
            dov = do_ref[:, h * XA_DIM:(h + 1) * XA_DIM].astype(BF16)
            kv = kv_ref[h].astype(BF16)
            vv = kv_ref[XA_HEADS + h].astype(BF16)
            sc = lax.dot_general(qv, kv, (NT, ((), ())), preferred_element_type=F32) * (XA_DIM ** -0.5)
            e = jnp.exp(sc - jnp.max(sc, axis=-1, keepdims=True))
            p = e / jnp.sum(e, axis=-1, keepdims=True)
            dp = lax.dot_general(dov, vv, (NT, ((), ())), preferred_element_type=F32)
            ds = (p * (dp - jnp.sum(p * dp, axis=-1, keepdims=True)) * (XA_DIM ** -0.5)).astype(BF16)
            dq_ref[:, h * XA_DIM:(h + 1) * XA_DIM] = jnp.dot(ds, kv, preferred_element_type=F32).astype(dq_ref.dtype)
            dk = lax.dot_general(ds, qv, (TN, ((), ())), preferred_element_type=F32)
            dv = lax.dot_general(p.astype(BF16), dov, (TN, ((), ())), preferred_element_type=F32)

            @pl.when(i == 0)
            def _(h=h, dk=dk, dv=dv):
                dkv_ref[h] = dk
                dkv_ref[XA_HEADS + h] = dv

            @pl.when(i > 0)
            def _(h=h, dk=dk, dv=dv):
                dkv_ref[h] += dk
                dkv_ref[XA_HEADS + h] += dv

    row = pl.BlockSpec((t, D_MODEL), lambda i: (i, 0))
    kvs = pl.BlockSpec((2 * XA_HEADS, MEM_LEN, XA_DIM), lambda i: (0, 0, 0))
    return pl.pallas_call(
        body, name="xattn_bwd", grid=(s // t,),
        out_shape=(jax.ShapeDtypeStruct((s, D_MODEL), BF16),
                   jax.ShapeDtypeStruct((2 * XA_HEADS, MEM_LEN, XA_DIM), F32)),
        in_specs=[row, kvs, row], out_specs=(row, kvs), compiler_params=_params(),
    )(qx, kvm, do)


def _adamw(parts, w, m, v, name):
    r, c = w.shape
    tr = r
    for cand in (256, 128, 64, 32, 16, 8):
        if r % cand == 0 and r > cand and N_DEV * cand * c * 4 <= ADAMW_BLOCK_BYTES:
            tr = cand
            break

    def body(p_ref, w_ref, m_ref, v_ref, g_ref, d_ref, nm_ref, nv_ref):
        g = p_ref[0]
        for d in range(1, N_DEV):
            g = g + p_ref[d]
        mn = ADAM_B1 * m_ref[...] + (1.0 - ADAM_B1) * g
        vn = ADAM_B2 * v_ref[...] + (1.0 - ADAM_B2) * (g * g)
        m_hat = mn / (1.0 - ADAM_B1 ** ADAM_STEP)
        v_hat = vn / (1.0 - ADAM_B2 ** ADAM_STEP)
        g_ref[...] = g
        d_ref[...] = -ADAM_LR * (m_hat / (jnp.sqrt(v_hat) + ADAM_EPS) + ADAM_WD * w_ref[...])
        nm_ref[...] = mn
        nv_ref[...] = vn

    blk = pl.BlockSpec((tr, c), lambda i: (i, 0))
    shp = jax.ShapeDtypeStruct((r, c), F32)
    return pl.pallas_call(
        body, name=name, grid=(r // tr,), out_shape=(shp, shp, shp, shp),
        in_specs=[pl.BlockSpec((N_DEV, tr, c), lambda i: (0, i, 0)), blk, blk, blk],
        out_specs=(blk, blk, blk, blk), compiler_params=_params(),
    )(parts, w, m, v)


BIG_LAYOUT = (("w_in", 336), ("w_out", 128), ("w_xq", 128), ("w_xo", 128), ("w_xkv", 256), ("w_up", 704),
              ("w_down", 352))
BIG_ROWS = sum(n for _, n in BIG_LAYOUT)


def _relay_in_cols(w):
    pad = jnp.zeros(w.shape[:-1] + (N_IN_PAD - N_IN,), w.dtype)
    return jnp.concatenate([w[..., :2304], w[..., 2308:N_IN], w[..., 2304:2308], pad], axis=-1)


def _unrelay_in_cols(w):
    return jnp.concatenate([w[..., :2304], w[..., COL_GATE:COL_GATE + 4], w[..., 2304:COL_GATE]], axis=-1)


def _pack_layer_weights(ws):
    rows = [ws[name].astype(BF16).reshape(n, PACK_W) for name, n in BIG_LAYOUT]
    return jnp.concatenate(rows, axis=0)


def _unpack_layer_weights(g):
    out, off = {}, 0
    for name, n in BIG_LAYOUT:
        out[name] = g[:, off:off + n]
        off += n
    return {
        "w_in": out["w_in"].reshape(D_MODEL, N_IN_PAD),
        "w_out": out["w_out"].reshape(D_MODEL, D_MODEL),
        "w_xq": out["w_xq"].reshape(D_MODEL, D_MODEL),
        "w_xo": out["w_xo"].reshape(D_MODEL, D_MODEL),
        "w_xkv": out["w_xkv"].reshape(N_DEV, D_MODEL, 2 * D_MODEL // N_DEV),
        "w_up": out["w_up"].reshape(N_DEV, D_MODEL, FF_SHARD),
        "w_down": out["w_down"].reshape(FF_HALF, FF_SHARD, D_MODEL),
    }


def _pack_layer_grads(gs):
    rows = [gs[name].reshape(N_DEV, n, PACK_W) for name, n in BIG_LAYOUT]
    return jnp.concatenate(rows, axis=1)


def _layer_fwd(h, memv, w, sm, tables):
    sv = {"h0": h}
    xn = _rms_fwd(h, sm["g_mix"], "rms_mix")
    proj = _mm_nn(xn, w["w_in"], "mm_in", tn=896)
    sv["xn"], sv["proj"] = xn, proj
    ya = _sconv_fwd(proj, sm["w_sconv"])
    qd, kd, vd = _heads_split(proj, 1, tables, "split_dil")
    yb, ob, lse_b = _attn_fwd("dil", qd, kd, vd, None, None)
    sv["dil"] = (qd, kd, vd, ob, lse_b)
    qf, kf, vf = _heads_split(proj, 2, None, "split_fox")
    c = _gate_cumsum(proj, sm["b_forget_pad"])
    ct = c[:, :N_HEADS].T
    c_col, c_row = ct.reshape(N_HEADS, -1, 1), ct.reshape(N_HEADS, 1, -1)
    yc, oc, lse_c = _attn_fwd("fox", qf, kf, vf, c_col, c_row)
    sv["fox"] = (qf, kf, vf, oc, lse_c, c_col, c_row)
    yd = _pool_fwd(proj, sm["w_pool_bd"], sm["pool_scale"])
    ycat = jnp.concatenate([ya, yb, yc, yd], axis=1)
    sv["ycat"] = ycat
    h1 = _mm_nn(ycat, w["w_out"], "mm_out", res=h)
    sv["h1"] = h1
    xq = _rms_fwd(h1, sm["g_xa"], "rms_xa")
    memn = _rms_fwd(memv, sm["g_mem"], "rms_mem")
    qx = _mm_nn(xq, w["w_xq"], "mm_xq")
    kvm = _matmul(memn, w["w_xkv"], (N_DEV, MEM_LEN, XA_DIM), grid=(N_DEV, 1, 1),
                  a_spec=pl.BlockSpec((MEM_LEN, D_MODEL), lambda i, j, r: (0, 0)),
                  b_spec=pl.BlockSpec((None, D_MODEL, XA_DIM), lambda i, j, r: (i, 0, 0)),
                  o_spec=pl.BlockSpec((None, MEM_LEN, XA_DIM), lambda i, j, r: (i, 0, 0)),
                  dims=NN, nred=1, name="mm_xkv")
    ox = _xattn_fwd(qx, kvm)
    sv.update(xq=xq, memn=memn, qx=qx, kvm=kvm, ox=ox)
    h2 = _mm_nn(ox, w["w_xo"], "mm_xo", res=h1)
    sv["h2"] = h2
    xf = _rms_fwd(h2, sm["g_ffn"], "rms_ffn")
    s = h.shape[0]
    tm = min(ROW_TILE, s)
    u0 = _matmul(xf, w["w_up"], (N_DEV, s, FF_SHARD), grid=(s // tm, N_DEV, 1),
                 a_spec=pl.BlockSpec((tm, D_MODEL), lambda i, j, r: (i, 0)),
                 b_spec=pl.BlockSpec((None, D_MODEL, FF_SHARD), lambda i, j, r: (j, 0, 0)),
                 o_spec=pl.BlockSpec((None, tm, FF_SHARD), lambda i, j, r: (j, i, 0)),
                 dims=NN, nred=1, name="mm_up")
    act = _ffn_gate_fwd(u0, sm["w_ffconv"])
    sv.update(xf=xf, u0=u0, act=act)
    ospec = pl.BlockSpec((tm, D_MODEL), lambda i, j, r: (i, 0))
    h3 = _matmul(act, w["w_down"], (s, D_MODEL), grid=(s // tm, 1, FF_HALF),
                 a_spec=pl.BlockSpec((None, tm, FF_SHARD), lambda i, j, r: (r, i, 0)),
                 b_spec=pl.BlockSpec((None, FF_SHARD, D_MODEL), lambda i, j, r: (r, 0, 0)),
                 o_spec=ospec, dims=NN, nred=FF_HALF, name="mm_down", res=h2, res_spec=ospec)
    return h3, sv


def _layer_bwd(dh3, memv, w, sm, tables, sv):
    s = dh3.shape[0]
    tm = min(ROW_TILE, s)
    ts = min(1024, s)
    big, small = {}, {}
    dact = _matmul(dh3, w["w_down"], (FF_HALF, s, FF_SHARD), grid=(s // tm, FF_HALF, 1),
                   a_spec=pl.BlockSpec((tm, D_MODEL), lambda i, j, r: (i, 0)),
                   b_spec=pl.BlockSpec((None, FF_SHARD, D_MODEL), lambda i, j, r: (j, 0, 0)),
                   o_spec=pl.BlockSpec((None, tm, FF_SHARD), lambda i, j, r: (j, i, 0)),
                   dims=NT, nred=1, name="mm_dact")
    big["w_down"] = _matmul(sv["act"], dh3, (FF_HALF, FF_SHARD, D_MODEL), grid=(FF_HALF, 1, s // ts),
                            a_spec=pl.BlockSpec((None, ts, FF_SHARD), lambda i, j, r: (i, r, 0)),
                            b_spec=pl.BlockSpec((ts, D_MODEL), lambda i, j, r: (r, 0)),
                            o_spec=pl.BlockSpec((None, FF_SHARD, D_MODEL), lambda i, j, r: (i, 0, 0)),
                            dims=TN, nred=s // ts, name="mm_dw_down")
    du0, small["w_ffconv"] = _ffn_gate_bwd(sv["u0"], sm["w_ffconv"], dact)
    dxf = _matmul(du0, w["w_up"], (s, D_MODEL), grid=(s // tm, 1, N_DEV),
                  a_spec=pl.BlockSpec((None, tm, FF_SHARD), lambda i, j, r: (r, i, 0)),
                  b_spec=pl.BlockSpec((None, D_MODEL, FF_SHARD), lambda i, j, r: (r, 0, 0)),
                  o_spec=pl.BlockSpec((tm, D_MODEL), lambda i, j, r: (i, 0)),
                  dims=NT, nred=N_DEV, name="mm_dxf")
    big["w_up"] = _matmul(sv["xf"], du0, (N_DEV, D_MODEL, FF_SHARD), grid=(N_DEV, 1, s // ts),
                          a_spec=pl.BlockSpec((ts, D_MODEL), lambda i, j, r: (r, 0)),
                          b_spec=pl.BlockSpec((None, ts, FF_SHARD), lambda i, j, r: (i, r, 0)),
                          o_spec=pl.BlockSpec((None, D_MODEL, FF_SHARD), lambda i, j, r: (i, 0, 0)),
                          dims=TN, nred=s // ts, name="mm_dw_up")
    dh2, small["g_ffn"] = _rms_bwd(dxf, sv["h2"], sm["g_ffn"], dh3, "rms_ffn_bwd")
    dox = _mm_nt(dh2, w["w_xo"], "mm_dox")
    big["w_xo"] = _mm_tn(sv["ox"], dh2, "mm_dw_xo")
    dqx, dkvm = _xattn_bwd(sv["qx"], sv["kvm"], dox)
    dxq = _mm_nt(dqx, w["w_xq"], "mm_dxq")
    big["w_xq"] = _mm_tn(sv["xq"], dqx, "mm_dw_xq")
    big["w_xkv"] = _matmul(sv["memn"], dkvm, (N_DEV, D_MODEL, XA_DIM), grid=(N_DEV, 1, 1),
                           a_spec=pl.BlockSpec((MEM_LEN, D_MODEL), lambda i, j, r: (0, 0)),
                           b_spec=pl.BlockSpec((None, MEM_LEN, XA_DIM), lambda i, j, r: (i, 0, 0)),
                           o_spec=pl.BlockSpec((None, D_MODEL, XA_DIM), lambda i, j, r: (i, 0, 0)),
                           dims=TN, nred=1, name="mm_dw_xkv")
    dmemn = _matmul(dkvm, w["w_xkv"], (MEM_LEN, D_MODEL), grid=(1, 1, N_DEV),
                    a_spec=pl.BlockSpec((None, MEM_LEN, XA_DIM), lambda i, j, r: (r, 0, 0)),
                    b_spec=pl.BlockSpec((None, D_MODEL, XA_DIM), lambda i, j, r: (r, 0, 0)),
                    o_spec=pl.BlockSpec((MEM_LEN, D_MODEL), lambda i, j, r: (0, 0)),
                    dims=NT, nred=N_DEV, name="mm_dmemn")
    _, small["g_mem"] = _rms_bwd(dmemn, memv, sm["g_mem"], None, "rms_mem_bwd")
    dh1, small["g_xa"] = _rms_bwd(dxq, sv["h1"], sm["g_xa"], dh2, "rms_xa_bwd")
    dycat = _mm_nt(dh1, w["w_out"], "mm_dycat")
    big["w_out"] = _mm_tn(sv["ycat"], dh1, "mm_dw_out")
    proj = sv["proj"]
    dpa, small["w_sconv"] = _sconv_bwd(proj, sm["w_sconv"], dycat[:, 0:GROUP])
    qd, kd, vd, ob, lse_b = sv["dil"]
    dyb = dycat[:, GROUP:2 * GROUP]
    dq, delta, _ = _attn_bwd_q("dil", qd, kd, vd, ob, dyb, lse_b, None, None)
    dk, dv, _ = _attn_bwd_kv("dil", qd, kd, vd, dyb, lse_b.reshape(N_HEADS, 1, s), delta.reshape(N_HEADS, 1, s),
                             None, None)
    dpb = _heads_merge(dq, dk, dv, tables, "merge_dil")
    qf, kf, vf, oc, lse_c, c_col, c_row = sv["fox"]
    dyc = dycat[:, 2 * GROUP:3 * GROUP]
    dq, delta, dc_rows = _attn_bwd_q("fox", qf, kf, vf, oc, dyc, lse_c, c_col, c_row)
    dk, dv, dc_cols = _attn_bwd_kv("fox", qf, kf, vf, dyc, lse_c.reshape(N_HEADS, 1, s),
                                   delta.reshape(N_HEADS, 1, s), c_col, c_row)
    dpc = _heads_merge(dq, dk, dv, None, "merge_fox")
    dc = (dc_rows - dc_cols).reshape(N_HEADS, s).T
    dc = jnp.concatenate([dc, jnp.zeros((s, 128 - N_HEADS), F32)], axis=1)
    dz, dbias = _gate_cumsum_bwd(proj, sm["b_forget_pad"], dc)
    small["b_forget"] = dbias[0, :N_HEADS]
    dpd, dwbd, small["pool_scale"] = _pool_bwd(proj, sm["w_pool_bd"], sm["pool_scale"], dycat[:, 3 * GROUP:])
    small["w_pool"] = jnp.stack([dwbd[64 * g:64 * (g + 1), 64 * g:64 * (g + 1)] for g in range(4)])
    dproj = jnp.concatenate([dpa, dpb, dpc, dpd, dz], axis=1)
    dxn = _mm_nt(dproj, w["w_in"], "mm_dxn")
    big["w_in"] = _mm_tn(sv["xn"], dproj, "mm_dw_in", tn=896)
    dh0, small["g_mix"] = _rms_bwd(dxn, sv["h0"], sm["g_mix"], dh1, "rms_mix_bwd")
    return dh0, big, small


SMALL_NAMES = ("g_mix", "b_forget", "w_pool", "pool_scale", "g_xa", "g_mem", "g_ffn", "w_sconv", "w_ffconv")
WEIGHT_NAMES = ("g_mix", "w_in", "b_forget", "w_sconv", "w_pool", "pool_scale", "w_out", "g_xa", "g_mem", "w_xq",
                "w_xkv", "w_xo", "g_ffn", "w_up", "w_ffconv", "w_down", "g_final")


def _block_diag(w_pool):
    z = jnp.zeros((64, 64), F32)
    return jnp.concatenate(
        [jnp.concatenate([w_pool[g] if c == g else z for c in range(4)], axis=1) for g in range(4)], axis=0)


def kernel(x, mem, positions, g_mix, w_in, b_forget, w_sconv, w_pool, pool_scale, w_out, g_xa, g_mem, w_xq, w_xkv, w_xo, g_ffn, w_up, w_ffconv, w_down, g_final, loss_target, m_g_mix, m_w_in, m_b_forget, m_w_sconv, m_w_pool, m_pool_scale, m_w_out, m_g_xa, m_g_mem, m_w_xq, m_w_xkv, m_w_xo, m_g_ffn, m_w_up, m_w_ffconv, m_w_down, m_g_final, v_g_mix, v_w_in, v_b_forget, v_w_sconv, v_w_pool, v_pool_scale, v_w_out, v_g_xa, v_g_mem, v_w_xq, v_w_xkv, v_w_xo, v_g_ffn, v_w_up, v_w_ffconv, v_w_down, v_g_final):
    weights = dict(g_mix=g_mix, w_in=w_in, b_forget=b_forget, w_sconv=w_sconv, w_pool=w_pool, pool_scale=pool_scale,
                   w_out=w_out, g_xa=g_xa, g_mem=g_mem, w_xq=w_xq, w_xkv=w_xkv, w_xo=w_xo, g_ffn=g_ffn, w_up=w_up,
                   w_ffconv=w_ffconv, w_down=w_down, g_final=g_final)
    m_in = dict(g_mix=m_g_mix, w_in=m_w_in, b_forget=m_b_forget, w_sconv=m_w_sconv, w_pool=m_w_pool,
                pool_scale=m_pool_scale, w_out=m_w_out, g_xa=m_g_xa, g_mem=m_g_mem, w_xq=m_w_xq, w_xkv=m_w_xkv,
                w_xo=m_w_xo, g_ffn=m_g_ffn, w_up=m_w_up, w_ffconv=m_w_ffconv, w_down=m_w_down, g_final=m_g_final)
    v_in = dict(g_mix=v_g_mix, w_in=v_w_in, b_forget=v_b_forget, w_sconv=v_w_sconv, w_pool=v_w_pool,
                pool_scale=v_pool_scale, w_out=v_w_out, g_xa=v_g_xa, g_mem=v_g_mem, w_xq=v_w_xq, w_xkv=v_w_xkv,
                w_xo=v_w_xo, g_ffn=v_g_ffn, w_up=v_w_up, w_ffconv=v_w_ffconv, w_down=v_w_down, g_final=v_g_final)
    depth = w_in.shape[0]
    me = 4 * lax.axis_index("x") + 2 * lax.axis_index("y") + lax.axis_index("c")
    h = x[0]
    memv = mem[0]
    s = h.shape[0]
    tables = _rope_tables(positions[0])

    w_in_r = _relay_in_cols(w_in)
    gathered = []
    for l in range(depth):
        shard = _pack_layer_weights(dict(w_in=w_in_r[l], w_out=w_out[l], w_xq=w_xq[l], w_xo=w_xo[l], w_xkv=w_xkv[l],
                                         w_up=w_up[l], w_down=w_down[l]))
        gathered.append(_unpack_layer_weights(_exchange(shard, False, "gather_weights_%d" % l)))
    conv_shard = jnp.concatenate([w_sconv.reshape(-1), w_ffconv.reshape(-1)])
    conv_pad = (-conv_shard.shape[0]) % 1024
    conv_all = _exchange(jnp.concatenate([conv_shard, jnp.zeros((conv_pad,), F32)]).reshape(-1, 128), False,
                         "gather_conv_weights").reshape(N_DEV, -1)
    n_sc = depth * 3 * (GROUP // N_DEV)
    sconv_full = conv_all[:, :n_sc].reshape(N_DEV, depth, 3, GROUP // N_DEV).transpose(1, 2, 0, 3).reshape(
        depth, 3, GROUP)
    ffconv_full = conv_all[:, n_sc:n_sc + depth * 3 * FF_SHARD].reshape(N_DEV, depth, 3, FF_SHARD).transpose(
        1, 0, 2, 3)

    smalls = []
    for l in range(depth):
        smalls.append(dict(
            g_mix=g_mix[l], g_xa=g_xa[l], g_mem=g_mem[l], g_ffn=g_ffn[l], pool_scale=pool_scale[l],
            w_pool_bd=_block_diag(w_pool[l]), w_sconv=sconv_full[l], w_ffconv=ffconv_full[l],
            b_forget_pad=jnp.concatenate([b_forget[l], jnp.zeros((128 - N_HEADS,), F32)]).reshape(1, 128)))

    saved = []
    for l in range(depth):
        h, sv = _layer_fwd(h, memv, gathered[l], smalls[l], tables)
        saved.append(sv)
    loss_part, dh, dg_final = _loss_head(h, g_final, loss_target[0])
    loss = lax.psum(loss_part[0, 0], MESH_AXES)

    big_grads, small_grads = [None] * depth, [None] * depth
    for l in reversed(range(depth)):
        dh, big_grads[l], small_grads[l] = _layer_bwd(dh, memv, gathered[l], smalls[l], tables, saved[l])
    grad_x = dh[None]

    pieces = [_exchange(_pack_layer_grads(big_grads[l]), True, "scatter_grads_%d" % l) for l in range(depth)]
    flat = [small_grads[l][n].reshape(-1) for n in SMALL_NAMES for l in range(depth)] + [dg_final.reshape(-1)]
    flat = jnp.concatenate(flat)
    flat = jnp.concatenate([flat, jnp.zeros((SMALL_ROWS * 128 - flat.shape[0],), F32)]).reshape(SMALL_ROWS, 128)
    small_all = _exchange(flat, False, "gather_small_grads").reshape(N_DEV, -1)

    parts = {}
    off = 0
    for name, n in BIG_LAYOUT:
        p = jnp.stack([pieces[l][:, off:off + n] for l in range(depth)], axis=1)
        if name == "w_in":
            p = _unrelay_in_cols(p.reshape(N_DEV, depth, D_MODEL // N_DEV, N_IN_PAD))
        parts[name] = p.reshape((N_DEV,) + weights[name].shape)
        off += n
    off = 0
    for name in SMALL_NAMES:
        full_shape = {"w_sconv": (3, GROUP), "w_ffconv": (N_DEV, 3, FF_SHARD)}.get(name, weights[name].shape[1:])
        n = 1
        for dim in full_shape:
            n *= dim
        p = small_all[:, off:off + depth * n].reshape((N_DEV, depth) + tuple(full_shape))
        off += depth * n
        if name == "w_sconv":
            p = lax.dynamic_slice_in_dim(p, me * (GROUP // N_DEV), GROUP // N_DEV, axis=3)
        elif name == "w_ffconv":
            p = lax.dynamic_index_in_dim(p, me, axis=2, keepdims=False)
        parts[name] = p
    parts["g_final"] = small_all[:, off:off + D_MODEL]

    results = {}
    for name in WEIGHT_NAMES:
        wv = weights[name]
        shape2 = (1, wv.shape[0]) if wv.ndim == 1 else (-1, wv.shape[-1])
        w2 = wv.reshape(shape2)
        outs = _adamw(parts[name].reshape((N_DEV,) + w2.shape), w2, m_in[name].reshape(shape2),
                      v_in[name].reshape(shape2), "adamw_" + name)
        results[name] = [o.reshape(wv.shape) for o in outs]

    return (loss, grad_x, *[results[n][0] for n in WEIGHT_NAMES], *[results[n][1] for n in WEIGHT_NAMES],
            *[results[n][2] for n in WEIGHT_NAMES], *[results[n][3] for n in WEIGHT_NAMES])
```

```python
import functools

import jax
import jax.numpy as jnp
from jax import lax
from jax.experimental import pallas as pl
from jax.experimental.pallas import tpu as pltpu

F32 = jnp.float32
BF16 = jnp.bfloat16

N_DEV = 8
D_MODEL = 1024
GROUP = 256
HEAD_DIM = 64
N_HEADS = 4
N_IN = 2564
N_IN_PAD = 2688
COL_GATE = 2560
XA_HEADS = 4
XA_DIM = 256
MEM_LEN = 256
D_FF = 2816
FF_SHARD = 704
FF_HALF = 4
ROPE_THETA = 500000.0
ROPE_DIM = 16
RMS_EPS = 1e-6
NEG = -1e30
POOL_WINDOWS = (2, 4, 8, 16)
ADAM_LR, ADAM_B1, ADAM_B2, ADAM_EPS, ADAM_WD, ADAM_STEP = 0.001, 0.9, 0.999, 1e-08, 0.01, 10

ROW_TILE = 512
ATT_TQ = 256
ATT_TK = 256
VMEM_LIMIT = 56 * 1024 * 1024
PACK_W = 1024
SMALL_ROWS = 616
ADAMW_BLOCK_BYTES = 4 * 1024 * 1024

MESH_AXES = ("x", "y", "c")


def _params(**kw):
    return pltpu.CompilerParams(vmem_limit_bytes=VMEM_LIMIT, **kw)


def _exchange(x, scatter, name):
    shard_shape = x.shape[1:] if scatter else x.shape

    def body(x_ref, out_ref, send_sems, recv_sems, local_sem):
        mx, my, mc = lax.axis_index("x"), lax.axis_index("y"), lax.axis_index("c")
        me = 4 * mx + 2 * my + mc
        mine = pltpu.make_async_copy(x_ref.at[me] if scatter else x_ref, out_ref.at[me], local_sem)
        mine.start()
        copies = []
        for k in range(1, N_DEV):
            kx, ky, kc = (k >> 2) & 1, (k >> 1) & 1, k & 1
            peer = (mx ^ kx, my ^ ky, mc ^ kc)
            peer_lin = me ^ k
            src = x_ref.at[peer_lin] if scatter else x_ref
            copies.append((pltpu.make_async_remote_copy(
                src_ref=src, dst_ref=out_ref.at[me], send_sem=send_sems.at[k - 1], recv_sem=recv_sems.at[k - 1],
                device_id=peer, device_id_type=pl.DeviceIdType.MESH), peer_lin, k))
        for cp, _, _ in copies:
            cp.start()
        for cp, peer_lin, k in copies:
            cp.wait_send()
            pltpu.make_async_remote_copy(
                src_ref=out_ref.at[peer_lin], dst_ref=out_ref.at[peer_lin], send_sem=send_sems.at[k - 1],
                recv_sem=recv_sems.at[k - 1], device_id=(mx, my, mc), device_id_type=pl.DeviceIdType.MESH).wait_recv()
        mine.wait()

    return pl.pallas_call(
        body, name=name,
        out_shape=jax.ShapeDtypeStruct((N_DEV,) + tuple(shard_shape), x.dtype),
        in_specs=[pl.BlockSpec(memory_space=pl.ANY)],
        out_specs=pl.BlockSpec(memory_space=pl.ANY),
        scratch_shapes=[pltpu.SemaphoreType.DMA((N_DEV - 1,)), pltpu.SemaphoreType.DMA((N_DEV - 1,)),
                        pltpu.SemaphoreType.DMA],
    )(x)


HBM_SPEC = pl.BlockSpec(memory_space=pltpu.HBM)
SEM_SPEC = pl.BlockSpec(memory_space=pltpu.SEMAPHORE)
DATAFLOW = pltpu.SideEffectType.DATAFLOW_SIDE_EFFECTING


def _peer_copies(x_ref, land_ref, send_sems, recv_sems, scatter):
    mx, my, mc = lax.axis_index("x"), lax.axis_index("y"), lax.axis_index("c")
    me = 4 * mx + 2 * my + mc
    pairs = []
    for k in range(1, N_DEV):
        kx, ky, kc = (k >> 2) & 1, (k >> 1) & 1, k & 1
        peer_lin = me ^ k
        send = pltpu.make_async_remote_copy(
            src_ref=x_ref.at[peer_lin] if scatter else x_ref, dst_ref=land_ref.at[me],
            send_sem=send_sems.at[k - 1], recv_sem=recv_sems.at[k - 1],
            device_id=(mx ^ kx, my ^ ky, mc ^ kc), device_id_type=pl.DeviceIdType.MESH)
        arrival = pltpu.make_async_remote_copy(
            src_ref=land_ref.at[peer_lin], dst_ref=land_ref.at[peer_lin],
            send_sem=send_sems.at[k - 1], recv_sem=recv_sems.at[k - 1],
            device_id=(mx, my, mc), device_id_type=pl.DeviceIdType.MESH)
        pairs.append((send, arrival))
    return pairs


def _exchange_start(x, scatter, name):
    shard_shape = x.shape[1:] if scatter else x.shape
    land_shape = (N_DEV,) + tuple(shard_shape)

    def body(x_ref, land_ref, send_sems, recv_sems, x_thru, land_thru, token):
        for send, _ in _peer_copies(x_ref, land_ref, send_sems, recv_sems, scatter):
            send.start()
        token[...] = jnp.zeros_like(token)

    sems = pltpu.SemaphoreType.DMA((N_DEV - 1,))
    send_sems, recv_sems, x_thru, land_thru, token = pl.pallas_call(
        body, name=name,
        out_shape=(sems, sems, pltpu.HBM(x.shape, x.dtype), pltpu.HBM(land_shape, x.dtype),
                   jax.ShapeDtypeStruct((8, 128), F32)),
        in_specs=(HBM_SPEC, HBM_SPEC),
        out_specs=(SEM_SPEC, SEM_SPEC, HBM_SPEC, HBM_SPEC, pl.BlockSpec(memory_space=pltpu.VMEM)),
        input_output_aliases={0: 2, 1: 3},
        compiler_params=pltpu.CompilerParams(has_side_effects=DATAFLOW),
    )(pltpu.with_memory_space_constraint(x, pltpu.HBM),
      pltpu.with_memory_space_constraint(lax.empty(land_shape, x.dtype), pltpu.HBM))
    return (send_sems, recv_sems, x_thru, land_thru, scatter), token


def _exchange_wait(state, after, name):
    send_sems, recv_sems, x_thru, land_thru, scatter = state

    def body(x_ref, land_ref, send_sems, recv_sems, after_ref, x_dead, got_ref):
        for send, arrival in _peer_copies(x_ref, land_ref, send_sems, recv_sems, scatter):
            send.wait_send()
            arrival.wait_recv()

    x_done, got = pl.pallas_call(
        body, name=name,
        out_shape=(pltpu.HBM(x_thru.shape, x_thru.dtype), pltpu.HBM(land_thru.shape, land_thru.dtype)),
        in_specs=(HBM_SPEC, HBM_SPEC, SEM_SPEC, SEM_SPEC, pl.BlockSpec(memory_space=pl.ANY)),
        out_specs=(HBM_SPEC, HBM_SPEC), input_output_aliases={0: 0, 1: 1},
        compiler_params=pltpu.CompilerParams(has_side_effects=DATAFLOW),
    )(x_thru, land_thru, send_sems, recv_sems, after)
    me = 4 * lax.axis_index("x") + 2 * lax.axis_index("y") + lax.axis_index("c")
    own = lax.dynamic_index_in_dim(x_done, me, axis=0, keepdims=True) if scatter else x_done[None]
    return lax.dynamic_update_slice_in_dim(got, own, me, axis=0)


NN = ((1,), (0,))
NT = ((1,), (1,))
TN = ((0,), (0,))


def _matmul(a, b, out_shape, *, grid, a_spec, b_spec, o_spec, dims, nred, name, res=None, res_spec=None,
            out_dtype=F32):
    has_res = res is not None

    def body(*refs):
        a_ref, b_ref = refs[0], refs[1]
        r_ref = refs[2] if has_res else None
        o_ref = refs[3] if has_res else refs[2]
        part = lax.dot_general(a_ref[...].astype(BF16), b_ref[...].astype(BF16), (dims, ((), ())),
                               preferred_element_type=F32)
        if nred == 1:
            if has_res:
                part = part + r_ref[...]
            o_ref[...] = part.astype(o_ref.dtype)
        else:
            acc = refs[-1]
            r = pl.program_id(2)

            @pl.when(r == 0)
            def _():
                acc[...] = part

            @pl.when(r > 0)
            def _():
                acc[...] += part

            @pl.when(r == nred - 1)
            def _():
                tot = acc[...]
                if has_res:
                    tot = tot + r_ref[...]
                o_ref[...] = tot.astype(o_ref.dtype)

    in_specs = [a_spec, b_spec] + ([res_spec] if has_res else [])
    args = (a, b) + ((res,) if has_res else ())
    acc_shape = tuple(d for d in o_spec.block_shape if d is not None)
    return pl.pallas_call(
        body, name=name, grid=grid, out_shape=jax.ShapeDtypeStruct(out_shape, out_dtype),
        in_specs=in_specs, out_specs=o_spec,
        scratch_shapes=[pltpu.VMEM(acc_shape, F32)] if nred > 1 else [],
        compiler_params=_params(),
    )(*args)


def _mm_nn(a, w, name, res=None, tn=None):
    m, k = a.shape
    n = w.shape[1]
    tn = tn or n
    tm = min(ROW_TILE, m)
    ospec = pl.BlockSpec((tm, tn), lambda i, j, r: (i, j))
    return _matmul(a, w, (m, n), grid=(m // tm, n // tn, 1),
                   a_spec=pl.BlockSpec((tm, k), lambda i, j, r: (i, 0)),
                   b_spec=pl.BlockSpec((k, tn), lambda i, j, r: (0, j)),
                   o_spec=ospec, dims=NN, nred=1, name=name, res=res, res_spec=ospec if res is not None else None)


def _mm_nt(a, w, name, out_dtype=F32):
    m, n = a.shape
    k = w.shape[0]
    tm = min(ROW_TILE, m)
    return _matmul(a, w, (m, k), grid=(m // tm, 1, 1),
                   a_spec=pl.BlockSpec((tm, n), lambda i, j, r: (i, 0)),
                   b_spec=pl.BlockSpec((k, n), lambda i, j, r: (0, 0)),
                   o_spec=pl.BlockSpec((tm, k), lambda i, j, r: (i, 0)), dims=NT, nred=1, name=name,
                   out_dtype=out_dtype)


def _mm_tn(a, b, name, tk=512, tn=None, ts=1024):
    s, k = a.shape
    n = b.shape[1]
    tn = tn or n
    tk = min(tk, k)
    ts = min(ts, s)
    return _matmul(a, b, (k, n), grid=(k // tk, n // tn, s // ts),
                   a_spec=pl.BlockSpec((ts, tk), lambda i, j, r: (r, i)),
                   b_spec=pl.BlockSpec((ts, tn), lambda i, j, r: (r, j)),
                   o_spec=pl.BlockSpec((tk, tn), lambda i, j, r: (i, j)), dims=TN, nred=s // ts, name=name)


def _rms_fwd(h, g, name):
    s, d = h.shape
    tm = min(ROW_TILE, s)

    def body(h_ref, g_ref, o_ref):
        hv = h_ref[...]
        r = lax.rsqrt(jnp.mean(hv * hv, axis=-1, keepdims=True) + RMS_EPS)
        o_ref[...] = (hv * r * g_ref[...]).astype(o_ref.dtype)

    return pl.pallas_call(
        body, name=name, grid=(s // tm,), out_shape=jax.ShapeDtypeStruct((s, d), BF16),
        in_specs=[pl.BlockSpec((tm, d), lambda i: (i, 0)), pl.BlockSpec((1, d), lambda i: (0, 0))],
        out_specs=pl.BlockSpec((tm, d), lambda i: (i, 0)), compiler_params=_params(),
    )(h, g.reshape(1, d))


def _rms_bwd(dy, h, g, res, name):
    s, d = h.shape
    tm = min(ROW_TILE, s)
    has_res = res is not None

    def body(*refs):
        dy_ref, h_ref, g_ref = refs[:3]
        r_ref = refs[3] if has_res else None
        dh_ref, dg_ref = refs[-2], refs[-1]
        hv = h_ref[...]
        r = lax.rsqrt(jnp.mean(hv * hv, axis=-1, keepdims=True) + RMS_EPS)
        hn = hv * r
        dyv = dy_ref[...].astype(F32)
        u = dyv * g_ref[...]
        dh = r * (u - hn * jnp.mean(u * hn, axis=-1, keepdims=True))
        if has_res:
            dh = dh + r_ref[...]
        dh_ref[...] = dh
        part = jnp.sum(dyv * hn, axis=0, keepdims=True)

        @pl.when(pl.program_id(0) == 0)
        def _():
            dg_ref[...] = part

        @pl.when(pl.program_id(0) > 0)
        def _():
            dg_ref[...] += part

    row = pl.BlockSpec((tm, d), lambda i: (i, 0))
    vec = pl.BlockSpec((1, d), lambda i: (0, 0))
    dh, dg = pl.pallas_call(
        body, name=name, grid=(s // tm,),
        out_shape=(jax.ShapeDtypeStruct((s, d), F32), jax.ShapeDtypeStruct((1, d), F32)),
        in_specs=[row, row, vec] + ([row] if has_res else []),
        out_specs=(row, vec), compiler_params=_params(),
    )(*((dy, h, g.reshape(1, d)) + ((res,) if has_res else ())))
    return dh, dg.reshape(d)


def _loss_head(h, g, target):
    s, d = h.shape
    tm = min(ROW_TILE, s)

    def body(h_ref, g_ref, t_ref, loss_ref, dh_ref, dg_ref):
        hv = h_ref[...]
        r = lax.rsqrt(jnp.mean(hv * hv, axis=-1, keepdims=True) + RMS_EPS)
        hn = hv * r
        gv = g_ref[...]
        err = hn * gv - t_ref[...]
        rows = jnp.mean(err * err, axis=-1, keepdims=True)
        lpart = 0.5 * jnp.sum(rows, axis=0, keepdims=True) + jnp.zeros((1, 128), F32)
        dy = err * (1.0 / d)
        u = dy * gv
        dh_ref[...] = r * (u - hn * jnp.mean(u * hn, axis=-1, keepdims=True))
        gpart = jnp.sum(dy * hn, axis=0, keepdims=True)

        @pl.when(pl.program_id(0) == 0)
        def _():
            dg_ref[...] = gpart
            loss_ref[...] = lpart

        @pl.when(pl.program_id(0) > 0)
        def _():
            dg_ref[...] += gpart
            loss_ref[...] += lpart

    row = pl.BlockSpec((tm, d), lambda i: (i, 0))
    vec = pl.BlockSpec((1, d), lambda i: (0, 0))
    return pl.pallas_call(
        body, name="loss_head", grid=(s // tm,),
        out_shape=(jax.ShapeDtypeStruct((1, 128), F32), jax.ShapeDtypeStruct((s, d), F32),
                   jax.ShapeDtypeStruct((1, d), F32)),
        in_specs=[row, vec, row],
        out_specs=(pl.BlockSpec((1, 128), lambda i: (0, 0)), row, vec), compiler_params=_params(),
    )(h, g.reshape(1, d), target)


def _shift_down(x, k):
    return pltpu.roll(x, k, 0)


def _shift_up(x, k):
    return pltpu.roll(x, x.shape[0] - k, 0)


def _conv3(x, w):
    return w[2:3, :] * x + w[1:2, :] * _shift_down(x, 1) + w[0:1, :] * _shift_down(x, 2)


def _conv3_t(x, w):
    return w[2:3, :] * x + w[1:2, :] * _shift_up(x, 1) + w[0:1, :] * _shift_up(x, 2)


def _sigmoid(x):
    return 1.0 / (1.0 + jnp.exp(-x))


def _prev_map(tile, halo, col):
    return lambda i: (jnp.maximum(i * (tile // halo) - 1, 0), col)


def _next_map(tile, halo, col, nrows):
    return lambda i: (jnp.minimum((i + 1) * (tile // halo), nrows // halo - 1), col)


def _sconv_fwd(proj, w):
    s = proj.shape[0]
    t = min(ROW_TILE, s)

    def body(cur_ref, prev_ref, w_ref, o_ref):
        i = pl.program_id(0)
        prev = prev_ref[...] * (i > 0).astype(F32)
        ext = jnp.concatenate([prev, cur_ref[...]], axis=0)
        sv = ext[:, 2 * GROUP:3 * GROUP] * ext[:, 0:GROUP]
        y = ext[:, GROUP:2 * GROUP] * _conv3(sv, w_ref[...])
        o_ref[...] = y[8:].astype(o_ref.dtype)

    return pl.pallas_call(
        body, name="sconv_fwd", grid=(s // t,), out_shape=jax.ShapeDtypeStruct((s, GROUP), BF16),
        in_specs=[pl.BlockSpec((t, 3 * GROUP), lambda i: (i, 0)),
                  pl.BlockSpec((8, 3 * GROUP), _prev_map(t, 8, 0)),
                  pl.BlockSpec((3, GROUP), lambda i: (0, 0))],
        out_specs=pl.BlockSpec((t, GROUP), lambda i: (i, 0)), compiler_params=_params(),
    )(proj, proj, w)


def _sconv_bwd(proj, w, dy):
    s = proj.shape[0]
    t = min(ROW_TILE, s)
    nt = s // t

    def body(cur_ref, prev_ref, next_ref, w_ref, dy_ref, dyn_ref, dp_ref, dw_ref):
        i = pl.program_id(0)
        first = (i > 0).astype(F32)
        last = (i < nt - 1).astype(F32)
        ext = jnp.concatenate([prev_ref[...] * first, cur_ref[...], next_ref[...] * last], axis=0)
        dye = jnp.concatenate([jnp.zeros((8, GROUP), F32), dy_ref[...], dyn_ref[...] * last], axis=0)
        hv, bv, cv = ext[:, 0:GROUP], ext[:, GROUP:2 * GROUP], ext[:, 2 * GROUP:3 * GROUP]
        wv = w_ref[...]
        sv = cv * hv
        conv = _conv3(sv, wv)
        dconv = dye * bv
        ds = _conv3_t(dconv, wv)
        dp = jnp.concatenate([ds * cv, dye * conv, ds * hv], axis=1)
        dp_ref[...] = dp[8:8 + t].astype(dp_ref.dtype)
        dc = dconv[8:8 + t]
        dw = jnp.concatenate([
            jnp.sum(dc * _shift_down(sv, 2)[8:8 + t], axis=0, keepdims=True),
            jnp.sum(dc * _shift_down(sv, 1)[8:8 + t], axis=0, keepdims=True),
            jnp.sum(dc * sv[8:8 + t], axis=0, keepdims=True),
            jnp.zeros((5, GROUP), F32)], axis=0)

        @pl.when(i == 0)
        def _():
            dw_ref[...] = dw

        @pl.when(i > 0)
        def _():
            dw_ref[...] += dw

    dp, dw = pl.pallas_call(
        body, name="sconv_bwd", grid=(nt,),
        out_shape=(jax.ShapeDtypeStruct((s, 3 * GROUP), BF16), jax.ShapeDtypeStruct((8, GROUP), F32)),
        in_specs=[pl.BlockSpec((t, 3 * GROUP), lambda i: (i, 0)),
                  pl.BlockSpec((8, 3 * GROUP), _prev_map(t, 8, 0)),
                  pl.BlockSpec((8, 3 * GROUP), _next_map(t, 8, 0, s)),
                  pl.BlockSpec((3, GROUP), lambda i: (0, 0)),
                  pl.BlockSpec((t, GROUP), lambda i: (i, 0)),
                  pl.BlockSpec((8, GROUP), _next_map(t, 8, 0, s))],
        out_specs=(pl.BlockSpec((t, 3 * GROUP), lambda i: (i, 0)), pl.BlockSpec((8, GROUP), lambda i: (0, 0))),
        compiler_params=_params(),
    )(proj, proj, proj, w, dy, dy)
    return dp, dw[:3]


def _lane_window(shape):
    lane = lax.broadcasted_iota(jnp.int32, shape, 1)
    return lane, jnp.where(lane < 64, 2.0, jnp.where(lane < 128, 4.0, jnp.where(lane < 192, 8.0, 16.0)))


def _by_group(lane, s1, s2, s3, s4):
    return jnp.where(lane < 64, s1, jnp.where(lane < 128, s2, jnp.where(lane < 192, s3, s4)))


def _pool_z(ext, row0):
    s1 = ext + _shift_down(ext, 1)
    s2 = s1 + _shift_down(s1, 2)
    s3 = s2 + _shift_down(s2, 4)
    s4 = s3 + _shift_down(s3, 8)
    lane, win = _lane_window(ext.shape)
    tpos = (lax.broadcasted_iota(jnp.int32, ext.shape, 0) + (row0 - 16 + 1)).astype(F32)
    cnt = jnp.maximum(jnp.minimum(tpos, win), 1.0)
    return _by_group(lane, s1, s2, s3, s4) / cnt - ext


def _pool_fwd(proj, wbd, scale):
    s = proj.shape[0]
    t = min(ROW_TILE, s)
    col = (COL_GATE - GROUP) // GROUP

    def body(cur_ref, prev_ref, w_ref, sc_ref, o_ref):
        i = pl.program_id(0)
        ext = jnp.concatenate([prev_ref[...] * (i > 0).astype(F32), cur_ref[...]], axis=0)
        z = _pool_z(ext, i * t)[16:]
        y = jnp.dot(z.astype(BF16), w_ref[...].astype(BF16), preferred_element_type=F32)
        o_ref[...] = (y * sc_ref[...]).astype(o_ref.dtype)

    return pl.pallas_call(
        body, name="pool_fwd", grid=(s // t,), out_shape=jax.ShapeDtypeStruct((s, GROUP), BF16),
        in_specs=[pl.BlockSpec((t, GROUP), lambda i: (i, col)),
                  pl.BlockSpec((16, GROUP), _prev_map(t, 16, col)),
                  pl.BlockSpec((GROUP, GROUP), lambda i: (0, 0)),
                  pl.BlockSpec((1, GROUP), lambda i: (0, 0))],
        out_specs=pl.BlockSpec((t, GROUP), lambda i: (i, 0)), compiler_params=_params(),
    )(proj, proj, wbd, scale.reshape(1, GROUP))


def _pool_bwd(proj, wbd, scale, dy):
    s = proj.shape[0]
    t = min(ROW_TILE, s)
    nt = s // t
    col = (COL_GATE - GROUP) // GROUP

    def body(cur_ref, prev_ref, w_ref, sc_ref, dy_ref, dyn_ref, dp_ref, dw_ref, dsc_ref):
        i = pl.program_id(0)
        ext = jnp.concatenate([prev_ref[...] * (i > 0).astype(F32), cur_ref[...]], axis=0)
        z = _pool_z(ext, i * t)[16:]
        wv = w_ref[...].astype(BF16)
        dyc = dy_ref[...]
        dye = jnp.concatenate([dyc, dyn_ref[...] * (i < nt - 1).astype(F32)], axis=0) * sc_ref[...]
        dz = lax.dot_general(dye.astype(BF16), wv, (NT, ((), ())), preferred_element_type=F32)
        lane, win = _lane_window(dz.shape)
        tpos = (lax.broadcasted_iota(jnp.int32, dz.shape, 0) + (i * t + 1)).astype(F32)
        e = dz / jnp.minimum(tpos, win)
        f1 = e + _shift_up(e, 1)
        f2 = f1 + _shift_up(f1, 2)
        f3 = f2 + _shift_up(f2, 4)
        f4 = f3 + _shift_up(f3, 8)
        dp = _by_group(lane, f1, f2, f3, f4) - dz
        dp_ref[...] = dp[:t].astype(dp_ref.dtype)
        zb = z.astype(BF16)
        y = jnp.dot(zb, wv, preferred_element_type=F32)
        dsc = jnp.sum(dyc * y, axis=0, keepdims=True)
        dw = lax.dot_general(zb, dye[:t].astype(BF16), (TN, ((), ())), preferred_element_type=F32)

        @pl.when(i == 0)
        def _():
            dw_ref[...] = dw
            dsc_ref[...] = dsc

        @pl.when(i > 0)
        def _():
            dw_ref[...] += dw
            dsc_ref[...] += dsc

    dp, dw, dsc = pl.pallas_call(
        body, name="pool_bwd", grid=(nt,),
        out_shape=(jax.ShapeDtypeStruct((s, GROUP), BF16), jax.ShapeDtypeStruct((GROUP, GROUP), F32),
                   jax.ShapeDtypeStruct((1, GROUP), F32)),
        in_specs=[pl.BlockSpec((t, GROUP), lambda i: (i, col)),
                  pl.BlockSpec((16, GROUP), _prev_map(t, 16, col)),
                  pl.BlockSpec((GROUP, GROUP), lambda i: (0, 0)),
                  pl.BlockSpec((1, GROUP), lambda i: (0, 0)),
                  pl.BlockSpec((t, GROUP), lambda i: (i, 0)),
                  pl.BlockSpec((16, GROUP), _next_map(t, 16, 0, s))],
        out_specs=(pl.BlockSpec((t, GROUP), lambda i: (i, 0)), pl.BlockSpec((GROUP, GROUP), lambda i: (0, 0)),
                   pl.BlockSpec((1, GROUP), lambda i: (0, 0))),
        compiler_params=_params(),
    )(proj, proj, wbd, scale.reshape(1, GROUP), dy, dy)
    return dp, dw, dsc.reshape(GROUP)


def _ffn_gate_fwd(u0, w):
    s = u0.shape[1]
    t = min(ROW_TILE, s)

    def body(a_ref, ap_ref, g_ref, gp_ref, wa_ref, wg_ref, o_ref):
        first = (pl.program_id(1) > 0).astype(F32)
        a = _conv3(jnp.concatenate([ap_ref[...] * first, a_ref[...]], axis=0), wa_ref[...])[8:]
        g = _conv3(jnp.concatenate([gp_ref[...] * first, g_ref[...]], axis=0), wg_ref[...])[8:]
        o_ref[...] = (a * (g * _sigmoid(g))).astype(o_ref.dtype)

    def cur(off):
        return pl.BlockSpec((None, t, FF_SHARD), lambda j, i: (j + off, i, 0))

    def prev(off):
        return pl.BlockSpec((None, 8, FF_SHARD), lambda j, i: (j + off, jnp.maximum(i * (t // 8) - 1, 0), 0))

    def wspec(off):
        return pl.BlockSpec((None, 3, FF_SHARD), lambda j, i: (j + off, 0, 0))

    return pl.pallas_call(
        body, name="ffn_gate_fwd", grid=(FF_HALF, s // t),
        out_shape=jax.ShapeDtypeStruct((FF_HALF, s, FF_SHARD), BF16),
        in_specs=[cur(0), prev(0), cur(FF_HALF), prev(FF_HALF), wspec(0), wspec(FF_HALF)],
        out_specs=pl.BlockSpec((None, t, FF_SHARD), lambda j, i: (j, i, 0)), compiler_params=_params(),
    )(u0, u0, u0, u0, w, w)


def _ffn_gate_bwd(u0, w, dact):
    s = u0.shape[1]
    t = min(ROW_TILE, s)
    nt = s // t

    def body(c_ref, p_ref, n_ref, w_ref, d_ref, dn_ref, du_ref, dw_ref):
        i = pl.program_id(1)
        first = (i > 0).astype(F32)
        last = (i < nt - 1).astype(F32)
        dext = jnp.concatenate([jnp.zeros((8, FF_SHARD), F32), d_ref[...], dn_ref[...] * last], axis=0)
        ext = [jnp.concatenate([p_ref[n] * first, c_ref[n], n_ref[n] * last], axis=0) for n in range(2)]
        a = _conv3(ext[0], w_ref[0])
        g = _conv3(ext[1], w_ref[1])
        sg = _sigmoid(g)
        silu = g * sg
        dus = (dext * silu, dext * a * (sg + silu * (1.0 - sg)))
        for n in range(2):
            du_ref[n] = _conv3_t(dus[n], w_ref[n])[8:8 + t].astype(du_ref.dtype)
            dc = dus[n][8:8 + t]
            dw = jnp.concatenate([
                jnp.sum(dc * _shift_down(ext[n], 2)[8:8 + t], axis=0, keepdims=True),
                jnp.sum(dc * _shift_down(ext[n], 1)[8:8 + t], axis=0, keepdims=True),
                jnp.sum(dc * ext[n][8:8 + t], axis=0, keepdims=True),
                jnp.zeros((5, FF_SHARD), F32)], axis=0)

            @pl.when(i == 0)
            def _(n=n, dw=dw):
                dw_ref[n] = dw

            @pl.when(i > 0)
            def _(n=n, dw=dw):
                dw_ref[n] += dw

    def pair(rows, row_map):
        return pl.BlockSpec((2, None, rows, FF_SHARD), lambda j, i: (0, j, row_map(i), 0))

    prev_row = lambda i: jnp.maximum(i * (t // 8) - 1, 0)
    next_row = lambda i: jnp.minimum((i + 1) * (t // 8), s // 8 - 1)
    u2 = u0.reshape(2, FF_HALF, s, FF_SHARD)
    du, dw = pl.pallas_call(
        body, name="ffn_gate_bwd", grid=(FF_HALF, nt),
        out_shape=(jax.ShapeDtypeStruct((2, FF_HALF, s, FF_SHARD), BF16),
                   jax.ShapeDtypeStruct((2, FF_HALF, 8, FF_SHARD), F32)),
        in_specs=[pair(t, lambda i: i), pair(8, prev_row), pair(8, next_row), pair(3, lambda i: 0),
                  pl.BlockSpec((None, t, FF_SHARD), lambda j, i: (j, i, 0)),
                  pl.BlockSpec((None, 8, FF_SHARD), lambda j, i: (j, next_row(i), 0))],
        out_specs=(pair(t, lambda i: i), pair(8, lambda i: 0)),
        compiler_params=_params(),
    )(u2, u2, u2, w.reshape(2, FF_HALF, 3, FF_SHARD), dact, dact)
    return du.reshape(2 * FF_HALF, s, FF_SHARD), dw.reshape(2 * FF_HALF, 8, FF_SHARD)[:, :3]


def _rope_tables(positions):
    inv_freq = ROPE_THETA ** (-jnp.arange(0, ROPE_DIM, 2, dtype=F32) / ROPE_DIM)
    ang = positions.astype(F32)[:, None] * inv_freq
    cos, sin = jnp.cos(ang), jnp.sin(ang)
    s = positions.shape[0]
    half = ROPE_DIM // 2
    rest = HEAD_DIM - ROPE_DIM
    ca = jnp.concatenate([cos, cos, jnp.ones((s, rest), F32)], axis=1)
    cb = jnp.concatenate([-sin, jnp.zeros((s, HEAD_DIM - half), F32)], axis=1)
    cc = jnp.concatenate([jnp.zeros((s, half), F32), sin, jnp.zeros((s, rest), F32)], axis=1)
    return tuple(jnp.tile(tb, (1, N_HEADS)) for tb in (ca, cb, cc))


def _heads_split(proj, col, tables, name):
    s = proj.shape[0]
    t = min(ROW_TILE, s)
    rope = tables is not None

    def body(*refs):
        x_ref = refs[0]
        q_ref, k_ref, v_ref = refs[-3:]
        xv = x_ref[...]
        parts = [xv[:, 0:GROUP], xv[:, GROUP:2 * GROUP], xv[:, 2 * GROUP:3 * GROUP]]
        if rope:
            ca, cb, cc = refs[1][...], refs[2][...], refs[3][...]
            for n in range(2):
                p = parts[n]
                parts[n] = p * ca + pltpu.roll(p, GROUP - 8, 1) * cb + pltpu.roll(p, 8, 1) * cc
        parts[0] = parts[0] * (HEAD_DIM ** -0.5)
        for o_ref, p in zip((q_ref, k_ref, v_ref), parts):
            for h in range(N_HEADS):
                o_ref[h] = p[:, h * HEAD_DIM:(h + 1) * HEAD_DIM].astype(o_ref.dtype)

    tab = pl.BlockSpec((t, GROUP), lambda i: (i, 0))
    heads = pl.BlockSpec((N_HEADS, t, HEAD_DIM), lambda i: (0, i, 0))
    hshape = jax.ShapeDtypeStruct((N_HEADS, s, HEAD_DIM), BF16)
    return pl.pallas_call(
        body, name=name, grid=(s // t,), out_shape=(hshape, hshape, hshape),
        in_specs=[pl.BlockSpec((t, 3 * GROUP), lambda i: (i, col))] + ([tab, tab, tab] if rope else []),
        out_specs=(heads, heads, heads), compiler_params=_params(),
    )(*((proj,) + (tuple(tables) if rope else ())))


def _heads_merge(dq, dk, dv, tables, name):
    s = dq.shape[1]
    t = min(ROW_TILE, s)
    rope = tables is not None

    def body(*refs):
        o_ref = refs[-1]
        parts = [jnp.concatenate([r[h] for h in range(N_HEADS)], axis=1) for r in refs[:3]]
        parts[0] = parts[0] * (HEAD_DIM ** -0.5)
        if rope:
            ca, cb, cc = refs[3][...], refs[4][...], refs[5][...]
            for n in range(2):
                p = parts[n]
                parts[n] = p * ca + pltpu.roll(p * cb, 8, 1) + pltpu.roll(p * cc, GROUP - 8, 1)
        o_ref[...] = jnp.concatenate(parts, axis=1).astype(o_ref.dtype)

    tab = pl.BlockSpec((t, GROUP), lambda i: (i, 0))
    heads = pl.BlockSpec((N_HEADS, t, HEAD_DIM), lambda i: (0, i, 0))
    return pl.pallas_call(
        body, name=name, grid=(s // t,), out_shape=jax.ShapeDtypeStruct((s, 3 * GROUP), BF16),
        in_specs=[heads, heads, heads] + ([tab, tab, tab] if rope else []),
        out_specs=pl.BlockSpec((t, 3 * GROUP), lambda i: (i, 0)), compiler_params=_params(),
    )(*((dq, dk, dv) + (tuple(tables) if rope else ())))


def _log_sigmoid(x):
    return jnp.minimum(x, 0.0) - jnp.log(1.0 + jnp.exp(-jnp.abs(x)))


def _scan_rows(x, reverse):
    n = x.shape[0]
    row = lax.broadcasted_iota(jnp.int32, x.shape, 0)
    k = 1
    while k < n:
        if reverse:
            x = x + jnp.where(row < n - k, _shift_up(x, k), 0.0)
        else:
            x = x + jnp.where(row >= k, _shift_down(x, k), 0.0)
        k *= 2
    return x


def _gate_cumsum(proj, bias):
    s = proj.shape[0]
    col = COL_GATE // 128

    def body(z_ref, b_ref, c_ref):
        c_ref[...] = _scan_rows(_log_sigmoid(z_ref[...] + b_ref[...]), False)

    return pl.pallas_call(
        body, name="gate_cumsum", grid=(1,), out_shape=jax.ShapeDtypeStruct((s, 128), F32),
        in_specs=[pl.BlockSpec((s, 128), lambda i: (0, col)), pl.BlockSpec((1, 128), lambda i: (0, 0))],
        out_specs=pl.BlockSpec((s, 128), lambda i: (0, 0)), compiler_params=_params(),
    )(proj, bias)


def _gate_cumsum_bwd(proj, bias, dc):
    s = proj.shape[0]
    col = COL_GATE // 128

    def body(z_ref, b_ref, dc_ref, dz_ref, db_ref):
        dlogf = _scan_rows(dc_ref[...], True)
        dz = dlogf * _sigmoid(-(z_ref[...] + b_ref[...]))
        dz_ref[...] = dz.astype(dz_ref.dtype)
        db_ref[...] = jnp.sum(dz, axis=0, keepdims=True)

    return pl.pallas_call(
        body, name="gate_cumsum_bwd", grid=(1,),
        out_shape=(jax.ShapeDtypeStruct((s, 128), BF16), jax.ShapeDtypeStruct((1, 128), F32)),
        in_specs=[pl.BlockSpec((s, 128), lambda i: (0, col)), pl.BlockSpec((1, 128), lambda i: (0, 0)),
                  pl.BlockSpec((s, 128), lambda i: (0, 0))],
        out_specs=(pl.BlockSpec((s, 128), lambda i: (0, 0)), pl.BlockSpec((1, 128), lambda i: (0, 0))),
        compiler_params=_params(),
    )(proj, bias, dc)


DIL_REACH = 2048


def _pair_weight(mode, d):
    if mode == "fox":
        return jnp.where(d >= 0, 1.0, 0.0)
    w1 = jnp.where(jnp.abs(d - 64) <= 64, 1.0, 0.0)
    w2 = jnp.where((d & 3) == 0, jnp.where(jnp.abs(d - 256) <= 256, 1.0, 0.0), 0.0)
    w3 = jnp.where((d & 15) == 0, jnp.where(jnp.abs(d - 1024) <= 1024, 1.0, 0.0), 0.0)
    return w1 + w2 + w3


def _first_key_chunk(mode, q0, tk):
    if mode == "fox":
        return 0
    return jnp.maximum(q0 - DIL_REACH, 0) // tk


def _attn_fwd(mode, q, k, v, c_col, c_row):
    s = q.shape[1]
    tq, tk = min(ATT_TQ, s), min(ATT_TK, s)
    fox = mode == "fox"

    def body(*refs):
        q_ref, k_ref, v_ref = refs[:3]
        cc_ref, cr_ref = (refs[3], refs[4]) if fox else (None, None)
        y_ref, o_ref, lse_ref = refs[-3:]
        i = pl.program_id(0)
        q0 = i * tq
        lo = _first_key_chunk(mode, q0, tk)
        hi = (q0 + tq + tk - 1) // tk
        rows = lax.broadcasted_iota(jnp.int32, (tq, tk), 0) + q0
        cols = lax.broadcasted_iota(jnp.int32, (tq, tk), 1)
        for h in range(N_HEADS):
            qv = q_ref[h]
            ccol = cc_ref[h] if fox else None

            def step(c, carry, h=h, qv=qv, ccol=ccol):
                m, l, acc = carry
                k0 = pl.multiple_of(c * tk, tk)
                kv = k_ref[h, pl.ds(k0, tk), :]
                vv = v_ref[h, pl.ds(k0, tk), :]
                sc = lax.dot_general(qv, kv, (NT, ((), ())), preferred_element_type=F32)
                if fox:
                    sc = sc + ccol - cr_ref[h, :, pl.ds(k0, tk)]
                w = _pair_weight(mode, rows - (cols + k0))
                sc = jnp.where(w > 0.0, sc, NEG)
                m_new = jnp.maximum(m, jnp.max(sc, axis=-1, keepdims=True))
                alpha = jnp.exp(m - m_new)
                p = w * jnp.exp(sc - m_new)
                l = alpha * l + jnp.sum(p, axis=-1, keepdims=True)
                acc = alpha * acc + jnp.dot(p.astype(BF16), vv, preferred_element_type=F32)
                return m_new, l, acc

            m, l, acc = lax.fori_loop(
                lo, hi, step,
                (jnp.full((tq, 1), NEG, F32), jnp.zeros((tq, 1), F32), jnp.zeros((tq, HEAD_DIM), F32)))
            out = acc / l
            y_ref[:, h * HEAD_DIM:(h + 1) * HEAD_DIM] = out.astype(y_ref.dtype)
            o_ref[:, h * HEAD_DIM:(h + 1) * HEAD_DIM] = out
            lse_ref[h] = m + jnp.log(l)

    qspec = pl.BlockSpec((N_HEADS, tq, HEAD_DIM), lambda i: (0, i, 0))
    full = pl.BlockSpec((N_HEADS, s, HEAD_DIM), lambda i: (0, 0, 0))
    colspec = pl.BlockSpec((N_HEADS, tq, 1), lambda i: (0, i, 0))
    rowfull = pl.BlockSpec((N_HEADS, 1, s), lambda i: (0, 0, 0))
    out_row = pl.BlockSpec((tq, GROUP), lambda i: (i, 0))
    return pl.pallas_call(
        body, name="attn_fwd_" + mode, grid=(s // tq,),
        out_shape=(jax.ShapeDtypeStruct((s, GROUP), BF16), jax.ShapeDtypeStruct((s, GROUP), F32),
                   jax.ShapeDtypeStruct((N_HEADS, s, 1), F32)),
        in_specs=[qspec, full, full] + ([colspec, rowfull] if fox else []),
        out_specs=(out_row, out_row, colspec), compiler_params=_params(),
    )(*((q, k, v) + ((c_col, c_row) if fox else ())))


def _attn_bwd_q(mode, q, k, v, o, do, lse, c_col, c_row):
    s = q.shape[1]
    tq, tk = min(ATT_TQ, s), min(ATT_TK, s)
    fox = mode == "fox"

    def body(*refs):
        q_ref, k_ref, v_ref, o_ref, do_ref, lse_ref = refs[:6]
        cc_ref, cr_ref = (refs[6], refs[7]) if fox else (None, None)
        dq_ref, delta_ref, dcr_ref = refs[-3:]
        i = pl.program_id(0)
        q0 = i * tq
        lo = _first_key_chunk(mode, q0, tk)
        hi = (q0 + tq + tk - 1) // tk
        rows = lax.broadcasted_iota(jnp.int32, (tq, tk), 0) + q0
        cols = lax.broadcasted_iota(jnp.int32, (tq, tk), 1)
        for h in range(N_HEADS):
            qv = q_ref[h]
            dov = do_ref[:, h * HEAD_DIM:(h + 1) * HEAD_DIM]
            delta = jnp.sum(dov * o_ref[:, h * HEAD_DIM:(h + 1) * HEAD_DIM], axis=-1, keepdims=True)
            dob = dov.astype(BF16)
            lse = lse_ref[h]
            ccol = cc_ref[h] if fox else None

            def step(c, carry, h=h, qv=qv, dob=dob, delta=delta, lse=lse, ccol=ccol):
                dq, dcr = carry
                k0 = pl.multiple_of(c * tk, tk)
                kv = k_ref[h, pl.ds(k0, tk), :]
                vv = v_ref[h, pl.ds(k0, tk), :]
                sc = lax.dot_general(qv, kv, (NT, ((), ())), preferred_element_type=F32)
                if fox:
                    sc = sc + ccol - cr_ref[h, :, pl.ds(k0, tk)]
                w = _pair_weight(mode, rows - (cols + k0))
                p = w * jnp.exp(jnp.where(w > 0.0, sc, NEG) - lse)
                dp = lax.dot_general(dob, vv, (NT, ((), ())), preferred_element_type=F32)
                ds = p * (dp - delta)
                dq = dq + jnp.dot(ds.astype(BF16), kv, preferred_element_type=F32)
                if fox:
                    dcr = dcr + jnp.sum(ds, axis=-1, keepdims=True)
                return dq, dcr

            dq, dcr = lax.fori_loop(lo, hi, step, (jnp.zeros((tq, HEAD_DIM), F32), jnp.zeros((tq, 1), F32)))
            dq_ref[h] = dq
            delta_ref[h] = delta
            dcr_ref[h] = dcr

    qspec = pl.BlockSpec((N_HEADS, tq, HEAD_DIM), lambda i: (0, i, 0))
    full = pl.BlockSpec((N_HEADS, s, HEAD_DIM), lambda i: (0, 0, 0))
    colspec = pl.BlockSpec((N_HEADS, tq, 1), lambda i: (0, i, 0))
    rowfull = pl.BlockSpec((N_HEADS, 1, s), lambda i: (0, 0, 0))
    row = pl.BlockSpec((tq, GROUP), lambda i: (i, 0))
    col_shape = jax.ShapeDtypeStruct((N_HEADS, s, 1), F32)
    return pl.pallas_call(
        body, name="attn_bwd_q_" + mode, grid=(s // tq,),
        out_shape=(jax.ShapeDtypeStruct((N_HEADS, s, HEAD_DIM), F32), col_shape, col_shape),
        in_specs=[qspec, full, full, row, row, colspec] + ([colspec, rowfull] if fox else []),
        out_specs=(qspec, colspec, colspec), compiler_params=_params(),
    )(*((q, k, v, o, do, lse) + ((c_col, c_row) if fox else ())))


def _attn_bwd_kv(mode, q, k, v, do, lse_row, delta_row, c_col, c_row):
    s = q.shape[1]
    tq, tk = min(ATT_TQ, s), min(ATT_TK, s)
    nq = s // tq
    fox = mode == "fox"

    def body(*refs):
        q_ref, k_ref, v_ref, do_ref, lse_ref, delta_ref = refs[:6]
        cc_ref, cr_ref = (refs[6], refs[7]) if fox else (None, None)
        dk_ref, dv_ref, dcc_ref = refs[-3:]
        i = pl.program_id(0)
        k0 = i * tk
        lo = k0 // tq
        hi = nq if fox else jnp.minimum((k0 + tk - 1 + DIL_REACH) // tq + 1, nq)
        keys = lax.broadcasted_iota(jnp.int32, (tk, tq), 0) + k0
        qcols = lax.broadcasted_iota(jnp.int32, (tk, tq), 1)
        for h in range(N_HEADS):
            kv = k_ref[h]
            vv = v_ref[h]
            ccol = cc_ref[h] if fox else None

            def step(c, carry, h=h, kv=kv, vv=vv, ccol=ccol):
                dk, dv, dcc = carry
                q0 = pl.multiple_of(c * tq, tq)
                qv = q_ref[h, pl.ds(q0, tq), :]
                dob = do_ref[pl.ds(q0, tq), h * HEAD_DIM:(h + 1) * HEAD_DIM].astype(BF16)
                sc = lax.dot_general(kv, qv, (NT, ((), ())), preferred_element_type=F32)
                if fox:
                    sc = sc + cr_ref[h, :, pl.ds(q0, tq)] - ccol
                w = _pair_weight(mode, (qcols + q0) - keys)
                p = w * jnp.exp(jnp.where(w > 0.0, sc, NEG) - lse_ref[h, :, pl.ds(q0, tq)])
                dp = lax.dot_general(vv, dob, (NT, ((), ())), preferred_element_type=F32)
                ds = p * (dp - delta_ref[h, :, pl.ds(q0, tq)])
                dv = dv + jnp.dot(p.astype(BF16), dob, preferred_element_type=F32)
                dk = dk + jnp.dot(ds.astype(BF16), qv, preferred_element_type=F32)
                if fox:
                    dcc = dcc + jnp.sum(ds, axis=-1, keepdims=True)
                return dk, dv, dcc

            dk, dv, dcc = lax.fori_loop(
                lo, hi, step,
                (jnp.zeros((tk, HEAD_DIM), F32), jnp.zeros((tk, HEAD_DIM), F32), jnp.zeros((tk, 1), F32)))
            dk_ref[h] = dk
            dv_ref[h] = dv
            dcc_ref[h] = dcc

    kspec = pl.BlockSpec((N_HEADS, tk, HEAD_DIM), lambda i: (0, i, 0))
    full = pl.BlockSpec((N_HEADS, s, HEAD_DIM), lambda i: (0, 0, 0))
    colspec = pl.BlockSpec((N_HEADS, tk, 1), lambda i: (0, i, 0))
    rowfull = pl.BlockSpec((N_HEADS, 1, s), lambda i: (0, 0, 0))
    dofull = pl.BlockSpec((s, GROUP), lambda i: (0, 0))
    hshape = jax.ShapeDtypeStruct((N_HEADS, s, HEAD_DIM), F32)
    return pl.pallas_call(
        body, name="attn_bwd_kv_" + mode, grid=(s // tk,),
        out_shape=(hshape, hshape, jax.ShapeDtypeStruct((N_HEADS, s, 1), F32)),
        in_specs=[full, kspec, kspec, dofull, rowfull, rowfull] + ([colspec, rowfull] if fox else []),
        out_specs=(kspec, kspec, colspec), compiler_params=_params(),
    )(*((q, k, v, do, lse_row, delta_row) + ((c_col, c_row) if fox else ())))


def _xattn_fwd(qx, kvm):
    s = qx.shape[0]
    t = min(ROW_TILE, s)

    def body(q_ref, kv_ref, o_ref):
        for h in range(XA_HEADS):
            qv = q_ref[:, h * XA_DIM:(h + 1) * XA_DIM].astype(BF16)
            kv = kv_ref[h].astype(BF16)
            vv = kv_ref[XA_HEADS + h].astype(BF16)
            sc = lax.dot_general(qv, kv, (NT, ((), ())), preferred_element_type=F32) * (XA_DIM ** -0.5)
            e = jnp.exp(sc - jnp.max(sc, axis=-1, keepdims=True))
            p = e / jnp.sum(e, axis=-1, keepdims=True)
            o_ref[:, h * XA_DIM:(h + 1) * XA_DIM] = jnp.dot(p.astype(BF16), vv,
                                                             preferred_element_type=F32).astype(o_ref.dtype)

    return pl.pallas_call(
        body, name="xattn_fwd", grid=(s // t,), out_shape=jax.ShapeDtypeStruct((s, D_MODEL), BF16),
        in_specs=[pl.BlockSpec((t, D_MODEL), lambda i: (i, 0)),
                  pl.BlockSpec((2 * XA_HEADS, MEM_LEN, XA_DIM), lambda i: (0, 0, 0))],
        out_specs=pl.BlockSpec((t, D_MODEL), lambda i: (i, 0)), compiler_params=_params(),
    )(qx, kvm)


def _xattn_bwd(qx, kvm, do):
    s = qx.shape[0]
    t = min(ROW_TILE, s)

    def body(q_ref, kv_ref, do_ref, dq_ref, dkv_ref):
        i = pl.program_id(0)
        for h in range(XA_HEADS):
            qv = q_ref[:, h * XA_DIM:(h + 1) * XA_DIM].astype(BF16)
            dov = do_ref[:, h * XA_DIM:(h + 1) * XA_DIM].astype(BF16)
            kv = kv_ref[h].astype(BF16)
            vv = kv_ref[XA_HEADS + h].astype(BF16)
            sc = lax.dot_general(qv, kv, (NT, ((), ())), preferred_element_type=F32) * (XA_DIM ** -0.5)
            e = jnp.exp(sc - jnp.max(sc, axis=-1, keepdims=True))
            p = e / jnp.sum(e, axis=-1, keepdims=True)
            dp = lax.dot_general(dov, vv, (NT, ((), ())), preferred_element_type=F32)
            ds = (p * (dp - jnp.sum(p * dp, axis=-1, keepdims=True)) * (XA_DIM ** -0.5)).astype(BF16)
            dq_ref[:, h * XA_DIM:(h + 1) * XA_DIM] = jnp.dot(ds, kv, preferred_element_type=F32).astype(dq_ref.dtype)
            dk = lax.dot_general(ds, qv, (TN, ((), ())), preferred_element_type=F32)
            dv = lax.dot_general(p.astype(BF16), dov, (TN, ((), ())), preferred_element_type=F32)

            @pl.when(i == 0)
            def _(h=h, dk=dk, dv=dv):
                dkv_ref[h] = dk
                dkv_ref[XA_HEADS + h] = dv

            @pl.when(i > 0)
            def _(h=h, dk=dk, dv=dv):
                dkv_ref[h] += dk
                dkv_ref[XA_HEADS + h] += dv

    row = pl.BlockSpec((t, D_MODEL), lambda i: (i, 0))
    kvs = pl.BlockSpec((2 * XA_HEADS, MEM_LEN, XA_DIM), lambda i: (0, 0, 0))
    return pl.pallas_call(
        body, name="xattn_bwd", grid=(s // t,),
        out_shape=(jax.ShapeDtypeStruct((s, D_MODEL), BF16),
                   jax.ShapeDtypeStruct((2 * XA_HEADS, MEM_LEN, XA_DIM), F32)),
        in_specs=[row, kvs, row], out_specs=(row, kvs), compiler_params=_params(),
    )(qx, kvm, do)


def _adamw(parts, w, m, v, name):
    r, c = w.shape
    tr = r
    for cand in (256, 128, 64, 32, 16, 8):
        if r % cand == 0 and r > cand and N_DEV * cand * c * 4 <= ADAMW_BLOCK_BYTES:
            tr = cand
            break

    def body(p_ref, w_ref, m_ref, v_ref, g_ref, d_ref, nm_ref, nv_ref):
        g = p_ref[0]
        for d in range(1, N_DEV):
            g = g + p_ref[d]
        mn = ADAM_B1 * m_ref[...] + (1.0 - ADAM_B1) * g
        vn = ADAM_B2 * v_ref[...] + (1.0 - ADAM_B2) * (g * g)
        m_hat = mn / (1.0 - ADAM_B1 ** ADAM_STEP)
        v_hat = vn / (1.0 - ADAM_B2 ** ADAM_STEP)
        g_ref[...] = g
        d_ref[...] = -ADAM_LR * (m_hat / (jnp.sqrt(v_hat) + ADAM_EPS) + ADAM_WD * w_ref[...])
        nm_ref[...] = mn
        nv_ref[...] = vn

    blk = pl.BlockSpec((tr, c), lambda i: (i, 0))
    shp = jax.ShapeDtypeStruct((r, c), F32)
    return pl.pallas_call(
        body, name=name, grid=(r // tr,), out_shape=(shp, shp, shp, shp),
        in_specs=[pl.BlockSpec((N_DEV, tr, c), lambda i: (0, i, 0)), blk, blk, blk],
        out_specs=(blk, blk, blk, blk), compiler_params=_params(),
    )(parts, w, m, v)


GROUPS = {"in": (("w_in", 336),),
          "rest": (("w_out", 128), ("w_xq", 128), ("w_xo", 128), ("w_xkv", 256), ("w_up", 704), ("w_down", 352))}
FULL_SHAPES = {"w_in": (D_MODEL, N_IN_PAD), "w_out": (D_MODEL, D_MODEL), "w_xq": (D_MODEL, D_MODEL),
               "w_xo": (D_MODEL, D_MODEL), "w_xkv": (N_DEV, D_MODEL, 2 * D_MODEL // N_DEV),
               "w_up": (N_DEV, D_MODEL, FF_SHARD), "w_down": (FF_HALF, FF_SHARD, D_MODEL)}


def _pack_rows(ts, group, lead, dtype):
    return jnp.concatenate([ts[name].astype(dtype).reshape(lead + (n, PACK_W)) for name, n in GROUPS[group]],
                           axis=len(lead))


def _split_rows(g, group):
    out, off = {}, 0
    for name, n in GROUPS[group]:
        out[name] = g[:, off:off + n]
        off += n
    return out


class _GatheredWeights:
    def __init__(self, states, layer):
        self.states, self.layer, self.full = dict(states), layer, {}

    def need(self, group, after):
        if group in self.states:
            g = _exchange_wait(self.states.pop(group), after, "gather_%s_wait_%d" % (group, self.layer))
            for name, rows in _split_rows(g, group).items():
                self.full[name] = rows.reshape(FULL_SHAPES[name])

    def __getitem__(self, name):
        return self.full[name]


def _relay_in_cols(w):
    pad = jnp.zeros(w.shape[:-1] + (N_IN_PAD - N_IN,), w.dtype)
    return jnp.concatenate([w[..., :2304], w[..., 2308:N_IN], w[..., 2304:2308], pad], axis=-1)


def _unrelay_in_cols(w):
    return jnp.concatenate([w[..., :2304], w[..., COL_GATE:COL_GATE + 4], w[..., 2304:COL_GATE]], axis=-1)


def _layer_fwd(h, memv, w, sm, tables):
    sv = {"h0": h}
    xn = _rms_fwd(h, sm["g_mix"], "rms_mix")
    w.need("in", xn)
    proj = _mm_nn(xn, w["w_in"], "mm_in", tn=896)
    sv["xn"], sv["proj"] = xn, proj
    ya = _sconv_fwd(proj, sm["w_sconv"])
    qd, kd, vd = _heads_split(proj, 1, tables, "split_dil")
    yb, ob, lse_b = _attn_fwd("dil", qd, kd, vd, None, None)
    sv["dil"] = (qd, kd, vd, ob, lse_b)
    qf, kf, vf = _heads_split(proj, 2, None, "split_fox")
    c = _gate_cumsum(proj, sm["b_forget_pad"])
    ct = c[:, :N_HEADS].T
    c_col, c_row = ct.reshape(N_HEADS, -1, 1), ct.reshape(N_HEADS, 1, -1)
    yc, oc, lse_c = _attn_fwd("fox", qf, kf, vf, c_col, c_row)
    sv["fox"] = (qf, kf, vf, oc, lse_c, c_col, c_row)
    yd = _pool_fwd(proj, sm["w_pool_bd"], sm["pool_scale"])
    ycat = jnp.concatenate([ya, yb, yc, yd], axis=1)
    sv["ycat"] = ycat
    w.need("rest", ycat)
    h1 = _mm_nn(ycat, w["w_out"], "mm_out", res=h)
    sv["h1"] = h1
    xq = _rms_fwd(h1, sm["g_xa"], "rms_xa")
    memn = _rms_fwd(memv, sm["g_mem"], "rms_mem")
    qx = _mm_nn(xq, w["w_xq"], "mm_xq")
    kvm = _matmul(memn, w["w_xkv"], (N_DEV, MEM_LEN, XA_DIM), grid=(N_DEV, 1, 1),
                  a_spec=pl.BlockSpec((MEM_LEN, D_MODEL), lambda i, j, r: (0, 0)),
                  b_spec=pl.BlockSpec((None, D_MODEL, XA_DIM), lambda i, j, r: (i, 0, 0)),
                  o_spec=pl.BlockSpec((None, MEM_LEN, XA_DIM), lambda i, j, r: (i, 0, 0)),
                  dims=NN, nred=1, name="mm_xkv")
    ox = _xattn_fwd(qx, kvm)
    sv.update(xq=xq, memn=memn, qx=qx, kvm=kvm, ox=ox)
    h2 = _mm_nn(ox, w["w_xo"], "mm_xo", res=h1)
    sv["h2"] = h2
    xf = _rms_fwd(h2, sm["g_ffn"], "rms_ffn")
    s = h.shape[0]
    tm = min(ROW_TILE, s)
    u0 = _matmul(xf, w["w_up"], (N_DEV, s, FF_SHARD), grid=(s // tm, N_DEV, 1),
                 a_spec=pl.BlockSpec((tm, D_MODEL), lambda i, j, r: (i, 0)),
                 b_spec=pl.BlockSpec((None, D_MODEL, FF_SHARD), lambda i, j, r: (j, 0, 0)),
                 o_spec=pl.BlockSpec((None, tm, FF_SHARD), lambda i, j, r: (j, i, 0)),
                 dims=NN, nred=1, name="mm_up")
    act = _ffn_gate_fwd(u0, sm["w_ffconv"])
    sv.update(xf=xf, u0=u0, act=act)
    ospec = pl.BlockSpec((tm, D_MODEL), lambda i, j, r: (i, 0))
    h3 = _matmul(act, w["w_down"], (s, D_MODEL), grid=(s // tm, 1, FF_HALF),
                 a_spec=pl.BlockSpec((None, tm, FF_SHARD), lambda i, j, r: (r, i, 0)),
                 b_spec=pl.BlockSpec((None, FF_SHARD, D_MODEL), lambda i, j, r: (r, 0, 0)),
                 o_spec=ospec, dims=NN, nred=FF_HALF, name="mm_down", res=h2, res_spec=ospec)
    return h3, sv


def _layer_bwd(dh3, memv, w, sm, tables, sv, rest_ready):
    s = dh3.shape[0]
    tm = min(ROW_TILE, s)
    ts = min(1024, s)
    big, small = {}, {}
    dact = _matmul(dh3, w["w_down"], (FF_HALF, s, FF_SHARD), grid=(s // tm, FF_HALF, 1),
                   a_spec=pl.BlockSpec((tm, D_MODEL), lambda i, j, r: (i, 0)),
                   b_spec=pl.BlockSpec((None, FF_SHARD, D_MODEL), lambda i, j, r: (j, 0, 0)),
                   o_spec=pl.BlockSpec((None, tm, FF_SHARD), lambda i, j, r: (j, i, 0)),
                   dims=NT, nred=1, name="mm_dact")
    big["w_down"] = _matmul(sv["act"], dh3, (FF_HALF, FF_SHARD, D_MODEL), grid=(FF_HALF, 1, s // ts),
                            a_spec=pl.BlockSpec((None, ts, FF_SHARD), lambda i, j, r: (i, r, 0)),
                            b_spec=pl.BlockSpec((ts, D_MODEL), lambda i, j, r: (r, 0)),
                            o_spec=pl.BlockSpec((None, FF_SHARD, D_MODEL), lambda i, j, r: (i, 0, 0)),
                            dims=TN, nred=s // ts, name="mm_dw_down")
    du0, small["w_ffconv"] = _ffn_gate_bwd(sv["u0"], sm["w_ffconv"], dact)
    dxf = _matmul(du0, w["w_up"], (s, D_MODEL), grid=(s // tm, 1, N_DEV),
                  a_spec=pl.BlockSpec((None, tm, FF_SHARD), lambda i, j, r: (r, i, 0)),
                  b_spec=pl.BlockSpec((None, D_MODEL, FF_SHARD), lambda i, j, r: (r, 0, 0)),
                  o_spec=pl.BlockSpec((tm, D_MODEL), lambda i, j, r: (i, 0)),
                  dims=NT, nred=N_DEV, name="mm_dxf")
    big["w_up"] = _matmul(sv["xf"], du0, (N_DEV, D_MODEL, FF_SHARD), grid=(N_DEV, 1, s // ts),
                          a_spec=pl.BlockSpec((ts, D_MODEL), lambda i, j, r: (r, 0)),
                          b_spec=pl.BlockSpec((None, ts, FF_SHARD), lambda i, j, r: (i, r, 0)),
                          o_spec=pl.BlockSpec((None, D_MODEL, FF_SHARD), lambda i, j, r: (i, 0, 0)),
                          dims=TN, nred=s // ts, name="mm_dw_up")
    dh2, small["g_ffn"] = _rms_bwd(dxf, sv["h2"], sm["g_ffn"], dh3, "rms_ffn_bwd")
    dox = _mm_nt(dh2, w["w_xo"], "mm_dox")
    big["w_xo"] = _mm_tn(sv["ox"], dh2, "mm_dw_xo")
    dqx, dkvm = _xattn_bwd(sv["qx"], sv["kvm"], dox)
    dxq = _mm_nt(dqx, w["w_xq"], "mm_dxq")
    big["w_xq"] = _mm_tn(sv["xq"], dqx, "mm_dw_xq")
    big["w_xkv"] = _matmul(sv["memn"], dkvm, (N_DEV, D_MODEL, XA_DIM), grid=(N_DEV, 1, 1),
                           a_spec=pl.BlockSpec((MEM_LEN, D_MODEL), lambda i, j, r: (0, 0)),
                           b_spec=pl.BlockSpec((None, MEM_LEN, XA_DIM), lambda i, j, r: (i, 0, 0)),
                           o_spec=pl.BlockSpec((None, D_MODEL, XA_DIM), lambda i, j, r: (i, 0, 0)),
                           dims=TN, nred=1, name="mm_dw_xkv")
    dmemn = _matmul(dkvm, w["w_xkv"], (MEM_LEN, D_MODEL), grid=(1, 1, N_DEV),
                    a_spec=pl.BlockSpec((None, MEM_LEN, XA_DIM), lambda i, j, r: (r, 0, 0)),
                    b_spec=pl.BlockSpec((None, D_MODEL, XA_DIM), lambda i, j, r: (r, 0, 0)),
                    o_spec=pl.BlockSpec((MEM_LEN, D_MODEL), lambda i, j, r: (0, 0)),
                    dims=NT, nred=N_DEV, name="mm_dmemn")
    _, small["g_mem"] = _rms_bwd(dmemn, memv, sm["g_mem"], None, "rms_mem_bwd")
    dh1, small["g_xa"] = _rms_bwd(dxq, sv["h1"], sm["g_xa"], dh2, "rms_xa_bwd")
    dycat = _mm_nt(dh1, w["w_out"], "mm_dycat")
    big["w_out"] = _mm_tn(sv["ycat"], dh1, "mm_dw_out")
    proj = sv["proj"]
    dpa, small["w_sconv"] = _sconv_bwd(proj, sm["w_sconv"] + rest_ready(big), dycat[:, 0:GROUP])
    qd, kd, vd, ob, lse_b = sv["dil"]
    dyb = dycat[:, GROUP:2 * GROUP]
    dq, delta, _ = _attn_bwd_q("dil", qd, kd, vd, ob, dyb, lse_b, None, None)
    dk, dv, _ = _attn_bwd_kv("dil", qd, kd, vd, dyb, lse_b.reshape(N_HEADS, 1, s), delta.reshape(N_HEADS, 1, s),
                             None, None)
    dpb = _heads_merge(dq, dk, dv, tables, "merge_dil")
    qf, kf, vf, oc, lse_c, c_col, c_row = sv["fox"]
    dyc = dycat[:, 2 * GROUP:3 * GROUP]
    dq, delta, dc_rows = _attn_bwd_q("fox", qf, kf, vf, oc, dyc, lse_c, c_col, c_row)
    dk, dv, dc_cols = _attn_bwd_kv("fox", qf, kf, vf, dyc, lse_c.reshape(N_HEADS, 1, s),
                                   delta.reshape(N_HEADS, 1, s), c_col, c_row)
    dpc = _heads_merge(dq, dk, dv, None, "merge_fox")
    dc = (dc_rows - dc_cols).reshape(N_HEADS, s).T
    dc = jnp.concatenate([dc, jnp.zeros((s, 128 - N_HEADS), F32)], axis=1)
    dz, dbias = _gate_cumsum_bwd(proj, sm["b_forget_pad"], dc)
    small["b_forget"] = dbias[0, :N_HEADS]
    dpd, dwbd, small["pool_scale"] = _pool_bwd(proj, sm["w_pool_bd"], sm["pool_scale"], dycat[:, 3 * GROUP:])
    small["w_pool"] = jnp.stack([dwbd[64 * g:64 * (g + 1), 64 * g:64 * (g + 1)] for g in range(4)])
    dproj = jnp.concatenate([dpa, dpb, dpc, dpd, dz], axis=1)
    dxn = _mm_nt(dproj, w["w_in"], "mm_dxn")
    big["w_in"] = _mm_tn(sv["xn"], dproj, "mm_dw_in", tn=896)
    dh0, small["g_mix"] = _rms_bwd(dxn, sv["h0"], sm["g_mix"], dh1, "rms_mix_bwd")
    return dh0, big, small


SMALL_NAMES = ("g_mix", "b_forget", "w_pool", "pool_scale", "g_xa", "g_mem", "g_ffn", "w_sconv", "w_ffconv")
WEIGHT_NAMES = ("g_mix", "w_in", "b_forget", "w_sconv", "w_pool", "pool_scale", "w_out", "g_xa", "g_mem", "w_xq",
                "w_xkv", "w_xo", "g_ffn", "w_up", "w_ffconv", "w_down", "g_final")


def _block_diag(w_pool):
    z = jnp.zeros((64, 64), F32)
    return jnp.concatenate(
        [jnp.concatenate([w_pool[g] if c == g else z for c in range(4)], axis=1) for g in range(4)], axis=0)


def kernel(x, mem, positions, g_mix, w_in, b_forget, w_sconv, w_pool, pool_scale, w_out, g_xa, g_mem, w_xq, w_xkv, w_xo, g_ffn, w_up, w_ffconv, w_down, g_final, loss_target, m_g_mix, m_w_in, m_b_forget, m_w_sconv, m_w_pool, m_pool_scale, m_w_out, m_g_xa, m_g_mem, m_w_xq, m_w_xkv, m_w_xo, m_g_ffn, m_w_up, m_w_ffconv, m_w_down, m_g_final, v_g_mix, v_w_in, v_b_forget, v_w_sconv, v_w_pool, v_pool_scale, v_w_out, v_g_xa, v_g_mem, v_w_xq, v_w_xkv, v_w_xo, v_g_ffn, v_w_up, v_w_ffconv, v_w_down, v_g_final):
    weights = dict(g_mix=g_mix, w_in=w_in, b_forget=b_forget, w_sconv=w_sconv, w_pool=w_pool, pool_scale=pool_scale,
                   w_out=w_out, g_xa=g_xa, g_mem=g_mem, w_xq=w_xq, w_xkv=w_xkv, w_xo=w_xo, g_ffn=g_ffn, w_up=w_up,
                   w_ffconv=w_ffconv, w_down=w_down, g_final=g_final)
    m_in = dict(g_mix=m_g_mix, w_in=m_w_in, b_forget=m_b_forget, w_sconv=m_w_sconv, w_pool=m_w_pool,
                pool_scale=m_pool_scale, w_out=m_w_out, g_xa=m_g_xa, g_mem=m_g_mem, w_xq=m_w_xq, w_xkv=m_w_xkv,
                w_xo=m_w_xo, g_ffn=m_g_ffn, w_up=m_w_up, w_ffconv=m_w_ffconv, w_down=m_w_down, g_final=m_g_final)
    v_in = dict(g_mix=v_g_mix, w_in=v_w_in, b_forget=v_b_forget, w_sconv=v_w_sconv, w_pool=v_w_pool,
                pool_scale=v_pool_scale, w_out=v_w_out, g_xa=v_g_xa, g_mem=v_g_mem, w_xq=v_w_xq, w_xkv=v_w_xkv,
                w_xo=v_w_xo, g_ffn=v_g_ffn, w_up=v_w_up, w_ffconv=v_w_ffconv, w_down=v_w_down, g_final=v_g_final)
    depth = w_in.shape[0]
    me = 4 * lax.axis_index("x") + 2 * lax.axis_index("y") + lax.axis_index("c")
    h = x[0]
    memv = mem[0]
    s = h.shape[0]
    tables = _rope_tables(positions[0])

    w_in_r = _relay_in_cols(w_in)
    gathered = []
    order = jnp.zeros((), F32)
    for l in range(depth):
        shards = dict(w_in=w_in_r[l], w_out=w_out[l], w_xq=w_xq[l], w_xo=w_xo[l], w_xkv=w_xkv[l], w_up=w_up[l],
                      w_down=w_down[l])
        states = {}
        for group in ("in", "rest"):
            packed = _pack_rows(shards, group, (), BF16) + order.astype(BF16)
            states[group], token = _exchange_start(packed, False, "gather_%s_start_%d" % (group, l))
            order = order + token[0, 0]
        gathered.append(_GatheredWeights(states, l))
    conv_shard = jnp.concatenate([w_sconv.reshape(-1), w_ffconv.reshape(-1)])
    conv_pad = (-conv_shard.shape[0]) % 1024
    conv_all = _exchange(jnp.concatenate([conv_shard, jnp.zeros((conv_pad,), F32)]).reshape(-1, 128), False,
                         "gather_conv_weights").reshape(N_DEV, -1)
    n_sc = depth * 3 * (GROUP // N_DEV)
    sconv_full = conv_all[:, :n_sc].reshape(N_DEV, depth, 3, GROUP // N_DEV).transpose(1, 2, 0, 3).reshape(
        depth, 3, GROUP)
    ffconv_full = conv_all[:, n_sc:n_sc + depth * 3 * FF_SHARD].reshape(N_DEV, depth, 3, FF_SHARD).transpose(
        1, 0, 2, 3)

    smalls = []
    for l in range(depth):
        smalls.append(dict(
            g_mix=g_mix[l], g_xa=g_xa[l], g_mem=g_mem[l], g_ffn=g_ffn[l], pool_scale=pool_scale[l],
            w_pool_bd=_block_diag(w_pool[l]), w_sconv=sconv_full[l], w_ffconv=ffconv_full[l],
            b_forget_pad=jnp.concatenate([b_forget[l], jnp.zeros((128 - N_HEADS,), F32)]).reshape(1, 128)))
    smalls[0]["g_mix"] = smalls[0]["g_mix"] + order

    saved = []
    for l in range(depth):
        h, sv = _layer_fwd(h, memv, gathered[l], smalls[l], tables)
        saved.append(sv)
    loss_part, dh, dg_final = _loss_head(h, g_final, loss_target[0])
    loss = lax.psum(loss_part[0, 0], MESH_AXES)

    small_grads = [None] * depth
    scatters = {}
    for l in reversed(range(depth)):
        def rest_ready(big, l=l):
            scatters[l, "rest"], token = _exchange_start(_pack_rows(big, "rest", (N_DEV,), F32), True,
                                                         "scatter_rest_start_%d" % l)
            return token[0, 0]

        dh, big, small_grads[l] = _layer_bwd(dh, memv, gathered[l], smalls[l], tables, saved[l], rest_ready)
        scatters[l, "in"], token = _exchange_start(_pack_rows(big, "in", (N_DEV,), F32), True,
                                                   "scatter_in_start_%d" % l)
        if l > 0:
            smalls[l - 1]["w_ffconv"] = smalls[l - 1]["w_ffconv"] + token[0, 0]
    grad_x = dh[None]

    flat = [small_grads[l][n].reshape(-1) for n in SMALL_NAMES for l in range(depth)] + [dg_final.reshape(-1)]
    flat = jnp.concatenate(flat)
    flat = jnp.concatenate([flat, jnp.zeros((SMALL_ROWS * 128 - flat.shape[0],), F32)]).reshape(SMALL_ROWS, 128)
    small_all = _exchange(flat, False, "gather_small_grads").reshape(N_DEV, -1)

    parts = {}

    def wait_group(group, after):
        pieces = {}
        for l in reversed(range(depth)):
            got = _exchange_wait(scatters[l, group], after, "scatter_%s_wait_%d" % (group, l))
            for name, rows in _split_rows(got, group).items():
                pieces[l, name] = rows
        for name, _ in GROUPS[group]:
            p = jnp.stack([pieces[l, name] for l in range(depth)], axis=1)
            if name == "w_in":
                p = _unrelay_in_cols(p.reshape(N_DEV, depth, D_MODEL // N_DEV, N_IN_PAD))
            parts[name] = p.reshape((N_DEV,) + weights[name].shape)

    wait_group("rest", small_all)
    off = 0
    for name in SMALL_NAMES:
        full_shape = {"w_sconv": (3, GROUP), "w_ffconv": (N_DEV, 3, FF_SHARD)}.get(name, weights[name].shape[1:])
        n = 1
        for dim in full_shape:
            n *= dim
        p = small_all[:, off:off + depth * n].reshape((N_DEV, depth) + tuple(full_shape))
        off += depth * n
        if name == "w_sconv":
            p = lax.dynamic_slice_in_dim(p, me * (GROUP // N_DEV), GROUP // N_DEV, axis=3)
        elif name == "w_ffconv":
            p = lax.dynamic_index_in_dim(p, me, axis=2, keepdims=False)
        parts[name] = p
    parts["g_final"] = small_all[:, off:off + D_MODEL]

    results = {}

    def update(name):
        wv = weights[name]
        shape2 = (1, wv.shape[0]) if wv.ndim == 1 else (-1, wv.shape[-1])
        w2 = wv.reshape(shape2)
        outs = _adamw(parts[name].reshape((N_DEV,) + w2.shape), w2, m_in[name].reshape(shape2),
                      v_in[name].reshape(shape2), "adamw_" + name)
        results[name] = [o.reshape(wv.shape) for o in outs]

    for name in WEIGHT_NAMES:
        if name != "w_in":
            update(name)
    wait_group("in", results["w_down"][1])
    update("w_in")

    return (loss, grad_x, *[results[n][0] for n in WEIGHT_NAMES], *[results[n][1] for n in WEIGHT_NAMES],
            *[results[n][2] for n in WEIGHT_NAMES], *[results[n][3] for n in WEIGHT_NAMES])
```

```python
import functools

import jax
import jax.numpy as jnp
from jax import lax
from jax.experimental import pallas as pl
from jax.experimental.pallas import tpu as pltpu

F32 = jnp.float32
BF16 = jnp.bfloat16

N_DEV = 8
D_MODEL = 1024
GROUP = 256
HEAD_DIM = 64
N_HEADS = 4
N_IN = 2564
N_IN_PAD = 2688
COL_GATE = 2560
XA_HEADS = 4
XA_DIM = 256
MEM_LEN = 256
D_FF = 2816
FF_SHARD = 704
FF_HALF = 4
ROPE_THETA = 500000.0
ROPE_DIM = 16
RMS_EPS = 1e-6
NEG = -1e30
POOL_WINDOWS = (2, 4, 8, 16)
ADAM_LR, ADAM_B1, ADAM_B2, ADAM_EPS, ADAM_WD, ADAM_STEP = 0.001, 0.9, 0.999, 1e-08, 0.01, 10

ROW_TILE = 512
ATT_TQ = 256
ATT_TK = 256
VMEM_LIMIT = 56 * 1024 * 1024
ADAMW_BLOCK_BYTES = 4 * 1024 * 1024

MESH_AXES = ("x", "y", "c")


def _params(**kw):
    return pltpu.CompilerParams(vmem_limit_bytes=VMEM_LIMIT, **kw)


HBM_SPEC = pl.BlockSpec(memory_space=pltpu.HBM)
SEM_SPEC = pl.BlockSpec(memory_space=pltpu.SEMAPHORE)
DATAFLOW = pltpu.SideEffectType.DATAFLOW_SIDE_EFFECTING


def _peer_copies(x_ref, land_ref, send_sems, recv_sems, scatter):
    mx, my, mc = lax.axis_index("x"), lax.axis_index("y"), lax.axis_index("c")
    me = 4 * mx + 2 * my + mc
    pairs = []
    for k in range(1, N_DEV):
        kx, ky, kc = (k >> 2) & 1, (k >> 1) & 1, k & 1
        peer_lin = me ^ k
        send = pltpu.make_async_remote_copy(
            src_ref=x_ref.at[peer_lin] if scatter else x_ref, dst_ref=land_ref.at[me],
            send_sem=send_sems.at[k - 1], recv_sem=recv_sems.at[k - 1],
            device_id=(mx ^ kx, my ^ ky, mc ^ kc), device_id_type=pl.DeviceIdType.MESH)
        arrival = pltpu.make_async_remote_copy(
            src_ref=land_ref.at[peer_lin], dst_ref=land_ref.at[peer_lin],
            send_sem=send_sems.at[k - 1], recv_sem=recv_sems.at[k - 1],
            device_id=(mx, my, mc), device_id_type=pl.DeviceIdType.MESH)
        pairs.append((send, arrival))
    return pairs


def _exchange_start(xs, scatter, name):
    n = len(xs)
    lands = [(N_DEV,) + tuple(x.shape[1:] if scatter else x.shape) for x in xs]

    def body(*refs):
        for t in range(n):
            for send, _ in _peer_copies(refs[t], refs[n + t], refs[2 * n + t], refs[3 * n + t], scatter):
                send.start()
        token = refs[-1]
        token[...] = jnp.zeros_like(token)

    sems = pltpu.SemaphoreType.DMA((N_DEV - 1,))
    outs = pl.pallas_call(
        body, name=name,
        out_shape=(sems,) * (2 * n) + tuple(pltpu.HBM(x.shape, x.dtype) for x in xs)
        + tuple(pltpu.HBM(shp, x.dtype) for shp, x in zip(lands, xs)) + (jax.ShapeDtypeStruct((8, 128), F32),),
        in_specs=(HBM_SPEC,) * (2 * n),
        out_specs=(SEM_SPEC,) * (2 * n) + (HBM_SPEC,) * (2 * n) + (pl.BlockSpec(memory_space=pltpu.VMEM),),
        input_output_aliases={i: 2 * n + i for i in range(2 * n)},
        compiler_params=pltpu.CompilerParams(has_side_effects=DATAFLOW),
    )(*[pltpu.with_memory_space_constraint(x, pltpu.HBM) for x in xs],
      *[pltpu.with_memory_space_constraint(lax.empty(shp, x.dtype), pltpu.HBM) for shp, x in zip(lands, xs)])
    return (outs[:-1], scatter), outs[-1]


def _exchange_wait(state, after, name):
    held, scatter = state
    n = len(held) // 4
    sems, x_thru, land_thru = held[:2 * n], held[2 * n:3 * n], held[3 * n:]

    def body(*refs):
        for t in range(n):
            for send, arrival in _peer_copies(refs[t], refs[n + t], refs[2 * n + t], refs[3 * n + t], scatter):
                send.wait_send()
                arrival.wait_recv()

    outs = pl.pallas_call(
        body, name=name,
        out_shape=tuple(pltpu.HBM(a.shape, a.dtype) for a in x_thru + land_thru),
        in_specs=(HBM_SPEC,) * (2 * n) + (SEM_SPEC,) * (2 * n) + (pl.BlockSpec(memory_space=pl.ANY),),
        out_specs=(HBM_SPEC,) * (2 * n), input_output_aliases={i: i for i in range(2 * n)},
        compiler_params=pltpu.CompilerParams(has_side_effects=DATAFLOW),
    )(*x_thru, *land_thru, *sems, after)

    def own_body(*refs):
        x_refs, got_refs, local_sems = refs[:n], refs[2 * n:3 * n], refs[-1]
        me = 4 * lax.axis_index("x") + 2 * lax.axis_index("y") + lax.axis_index("c")
        own = [pltpu.make_async_copy(x_refs[t].at[me] if scatter else x_refs[t], got_refs[t].at[me], local_sems.at[t])
               for t in range(n)]
        for cp in own:
            cp.start()
        for cp in own:
            cp.wait()

    anyspec = pl.BlockSpec(memory_space=pl.ANY)
    return list(pl.pallas_call(
        own_body, name=name + "_own",
        out_shape=tuple(jax.ShapeDtypeStruct(a.shape, a.dtype) for a in land_thru),
        in_specs=(anyspec,) * (2 * n), out_specs=(anyspec,) * n,
        input_output_aliases={n + i: i for i in range(n)},
        scratch_shapes=[pltpu.SemaphoreType.DMA((n,))],
    )(*outs))


NN = ((1,), (0,))
NT = ((1,), (1,))
TN = ((0,), (0,))


def _matmul(a, b, out_shape, *, grid, a_spec, b_spec, o_spec, dims, nred, name, res=None, res_spec=None,
            out_dtype=F32):
    has_res = res is not None

    def body(*refs):
        a_ref, b_ref = refs[0], refs[1]
        r_ref = refs[2] if has_res else None
        o_ref = refs[3] if has_res else refs[2]
        part = lax.dot_general(a_ref[...].astype(BF16), b_ref[...].astype(BF16), (dims, ((), ())),
                               preferred_element_type=F32)
        if nred == 1:
            if has_res:
                part = part + r_ref[...]
            o_ref[...] = part.astype(o_ref.dtype)
        else:
            acc = refs[-1]
            r = pl.program_id(2)

            @pl.when(r == 0)
            def _():
                acc[...] = part

            @pl.when(r > 0)
            def _():
                acc[...] += part

            @pl.when(r == nred - 1)
            def _():
                tot = acc[...]
                if has_res:
                    tot = tot + r_ref[...]
                o_ref[...] = tot.astype(o_ref.dtype)

    in_specs = [a_spec, b_spec] + ([res_spec] if has_res else [])
    args = (a, b) + ((res,) if has_res else ())
    acc_shape = tuple(d for d in o_spec.block_shape if d is not None)
    return pl.pallas_call(
        body, name=name, grid=grid, out_shape=jax.ShapeDtypeStruct(out_shape, out_dtype),
        in_specs=in_specs, out_specs=o_spec,
        scratch_shapes=[pltpu.VMEM(acc_shape, F32)] if nred > 1 else [],
        compiler_params=_params(),
    )(*args)


def _mm_nn(a, w, name, res=None, tn=None):
    m, k = a.shape
    n = w.shape[1]
    tn = tn or n
    tm = min(ROW_TILE, m)
    ospec = pl.BlockSpec((tm, tn), lambda i, j, r: (i, j))
    return _matmul(a, w, (m, n), grid=(m // tm, n // tn, 1),
                   a_spec=pl.BlockSpec((tm, k), lambda i, j, r: (i, 0)),
                   b_spec=pl.BlockSpec((k, tn), lambda i, j, r: (0, j)),
                   o_spec=ospec, dims=NN, nred=1, name=name, res=res, res_spec=ospec if res is not None else None)


def _mm_nt(a, w, name, out_dtype=F32):
    m, n = a.shape
    k = w.shape[0]
    tm = min(ROW_TILE, m)
    return _matmul(a, w, (m, k), grid=(m // tm, 1, 1),
                   a_spec=pl.BlockSpec((tm, n), lambda i, j, r: (i, 0)),
                   b_spec=pl.BlockSpec((k, n), lambda i, j, r: (0, 0)),
                   o_spec=pl.BlockSpec((tm, k), lambda i, j, r: (i, 0)), dims=NT, nred=1, name=name,
                   out_dtype=out_dtype)


def _mm_tn(a, b, name, tk=512, tn=None, ts=1024):
    s, k = a.shape
    n = b.shape[1]
    tn = tn or n
    tk = min(tk, k)
    ts = min(ts, s)
    return _matmul(a, b, (k, n), grid=(k // tk, n // tn, s // ts),
                   a_spec=pl.BlockSpec((ts, tk), lambda i, j, r: (r, i)),
                   b_spec=pl.BlockSpec((ts, tn), lambda i, j, r: (r, j)),
                   o_spec=pl.BlockSpec((tk, tn), lambda i, j, r: (i, j)), dims=TN, nred=s // ts, name=name)


def _rms_fwd(h, g, name):
    s, d = h.shape
    tm = min(ROW_TILE, s)

    def body(h_ref, g_ref, o_ref):
        hv = h_ref[...]
        r = lax.rsqrt(jnp.mean(hv * hv, axis=-1, keepdims=True) + RMS_EPS)
        o_ref[...] = (hv * r * g_ref[...]).astype(o_ref.dtype)

    return pl.pallas_call(
        body, name=name, grid=(s // tm,), out_shape=jax.ShapeDtypeStruct((s, d), BF16),
        in_specs=[pl.BlockSpec((tm, d), lambda i: (i, 0)), pl.BlockSpec((1, d), lambda i: (0, 0))],
        out_specs=pl.BlockSpec((tm, d), lambda i: (i, 0)), compiler_params=_params(),
    )(h, g.reshape(1, d))


def _rms_bwd(dy, h, g, res, name):
    s, d = h.shape
    tm = min(ROW_TILE, s)
    has_res = res is not None

    def body(*refs):
        dy_ref, h_ref, g_ref = refs[:3]
        r_ref = refs[3] if has_res else None
        dh_ref, dg_ref = refs[-2], refs[-1]
        hv = h_ref[...]
        r = lax.rsqrt(jnp.mean(hv * hv, axis=-1, keepdims=True) + RMS_EPS)
        hn = hv * r
        dyv = dy_ref[...].astype(F32)
        u = dyv * g_ref[...]
        dh = r * (u - hn * jnp.mean(u * hn, axis=-1, keepdims=True))
        if has_res:
            dh = dh + r_ref[...]
        dh_ref[...] = dh
        part = jnp.sum(dyv * hn, axis=0, keepdims=True)

        @pl.when(pl.program_id(0) == 0)
        def _():
            dg_ref[...] = part

        @pl.when(pl.program_id(0) > 0)
        def _():
            dg_ref[...] += part

    row = pl.BlockSpec((tm, d), lambda i: (i, 0))
    vec = pl.BlockSpec((1, d), lambda i: (0, 0))
    dh, dg = pl.pallas_call(
        body, name=name, grid=(s // tm,),
        out_shape=(jax.ShapeDtypeStruct((s, d), F32), jax.ShapeDtypeStruct((1, d), F32)),
        in_specs=[row, row, vec] + ([row] if has_res else []),
        out_specs=(row, vec), compiler_params=_params(),
    )(*((dy, h, g.reshape(1, d)) + ((res,) if has_res else ())))
    return dh, dg.reshape(d)


def _loss_head(h, g, target):
    s, d = h.shape
    tm = min(ROW_TILE, s)

    def body(h_ref, g_ref, t_ref, loss_ref, dh_ref, dg_ref):
        hv = h_ref[...]
        r = lax.rsqrt(jnp.mean(hv * hv, axis=-1, keepdims=True) + RMS_EPS)
        hn = hv * r
        gv = g_ref[...]
        err = hn * gv - t_ref[...]
        rows = jnp.mean(err * err, axis=-1, keepdims=True)
        lpart = 0.5 * jnp.sum(rows, axis=0, keepdims=True) + jnp.zeros((1, 128), F32)
        dy = err * (1.0 / d)
        u = dy * gv
        dh_ref[...] = r * (u - hn * jnp.mean(u * hn, axis=-1, keepdims=True))
        gpart = jnp.sum(dy * hn, axis=0, keepdims=True)

        @pl.when(pl.program_id(0) == 0)
        def _():
            dg_ref[...] = gpart
            loss_ref[...] = lpart

        @pl.when(pl.program_id(0) > 0)
        def _():
            dg_ref[...] += gpart
            loss_ref[...] += lpart

    row = pl.BlockSpec((tm, d), lambda i: (i, 0))
    vec = pl.BlockSpec((1, d), lambda i: (0, 0))
    return pl.pallas_call(
        body, name="loss_head", grid=(s // tm,),
        out_shape=(jax.ShapeDtypeStruct((1, 128), F32), jax.ShapeDtypeStruct((s, d), F32),
                   jax.ShapeDtypeStruct((1, d), F32)),
        in_specs=[row, vec, row],
        out_specs=(pl.BlockSpec((1, 128), lambda i: (0, 0)), row, vec), compiler_params=_params(),
    )(h, g.reshape(1, d), target)


def _shift_down(x, k):
    return pltpu.roll(x, k, 0)


def _shift_up(x, k):
    return pltpu.roll(x, x.shape[0] - k, 0)


def _conv3(x, w):
    return w[2:3, :] * x + w[1:2, :] * _shift_down(x, 1) + w[0:1, :] * _shift_down(x, 2)


def _conv3_t(x, w):
    return w[2:3, :] * x + w[1:2, :] * _shift_up(x, 1) + w[0:1, :] * _shift_up(x, 2)


def _sigmoid(x):
    return 1.0 / (1.0 + jnp.exp(-x))


def _prev_map(tile, halo, col):
    return lambda i: (jnp.maximum(i * (tile // halo) - 1, 0), col)


def _next_map(tile, halo, col, nrows):
    return lambda i: (jnp.minimum((i + 1) * (tile // halo), nrows // halo - 1), col)


def _sconv_fwd(proj, w):
    s = proj.shape[0]
    t = min(ROW_TILE, s)

    def body(cur_ref, prev_ref, w_ref, o_ref):
        i = pl.program_id(0)
        prev = prev_ref[...] * (i > 0).astype(F32)
        ext = jnp.concatenate([prev, cur_ref[...]], axis=0)
        sv = ext[:, 2 * GROUP:3 * GROUP] * ext[:, 0:GROUP]
        y = ext[:, GROUP:2 * GROUP] * _conv3(sv, w_ref[...])
        o_ref[...] = y[8:].astype(o_ref.dtype)

    return pl.pallas_call(
        body, name="sconv_fwd", grid=(s // t,), out_shape=jax.ShapeDtypeStruct((s, 4 * GROUP), BF16),
        in_specs=[pl.BlockSpec((t, 3 * GROUP), lambda i: (i, 0)),
                  pl.BlockSpec((8, 3 * GROUP), _prev_map(t, 8, 0)),
                  pl.BlockSpec((3, GROUP), lambda i: (0, 0))],
        out_specs=pl.BlockSpec((t, GROUP), lambda i: (i, 0)), compiler_params=_params(),
    )(proj, proj, w)


def _sconv_bwd(proj, w, dy):
    s = proj.shape[0]
    t = min(ROW_TILE, s)
    nt = s // t

    def body(cur_ref, prev_ref, next_ref, w_ref, dy_ref, dyn_ref, dp_ref, dw_ref):
        i = pl.program_id(0)
        first = (i > 0).astype(F32)
        last = (i < nt - 1).astype(F32)
        ext = jnp.concatenate([prev_ref[...] * first, cur_ref[...], next_ref[...] * last], axis=0)
        dye = jnp.concatenate([jnp.zeros((8, GROUP), F32), dy_ref[...], dyn_ref[...] * last], axis=0)
        hv, bv, cv = ext[:, 0:GROUP], ext[:, GROUP:2 * GROUP], ext[:, 2 * GROUP:3 * GROUP]
        wv = w_ref[...]
        sv = cv * hv
        conv = _conv3(sv, wv)
        dconv = dye * bv
        ds = _conv3_t(dconv, wv)
        dp = jnp.concatenate([ds * cv, dye * conv, ds * hv], axis=1)
        dp_ref[...] = dp[8:8 + t].astype(dp_ref.dtype)
        dc = dconv[8:8 + t]
        dw = jnp.concatenate([
            jnp.sum(dc * _shift_down(sv, 2)[8:8 + t], axis=0, keepdims=True),
            jnp.sum(dc * _shift_down(sv, 1)[8:8 + t], axis=0, keepdims=True),
            jnp.sum(dc * sv[8:8 + t], axis=0, keepdims=True),
            jnp.zeros((5, GROUP), F32)], axis=0)

        @pl.when(i == 0)
        def _():
            dw_ref[...] = dw

        @pl.when(i > 0)
        def _():
            dw_ref[...] += dw

    dp, dw = pl.pallas_call(
        body, name="sconv_bwd", grid=(nt,),
        out_shape=(jax.ShapeDtypeStruct((s, N_IN_PAD), BF16), jax.ShapeDtypeStruct((8, GROUP), F32)),
        in_specs=[pl.BlockSpec((t, 3 * GROUP), lambda i: (i, 0)),
                  pl.BlockSpec((8, 3 * GROUP), _prev_map(t, 8, 0)),
                  pl.BlockSpec((8, 3 * GROUP), _next_map(t, 8, 0, s)),
                  pl.BlockSpec((3, GROUP), lambda i: (0, 0)),
                  pl.BlockSpec((t, GROUP), lambda i: (i, 0)),
                  pl.BlockSpec((8, GROUP), _next_map(t, 8, 0, s))],
        out_specs=(pl.BlockSpec((t, 3 * GROUP), lambda i: (i, 0)), pl.BlockSpec((8, GROUP), lambda i: (0, 0))),
        compiler_params=_params(),
    )(proj, proj, proj, w, dy, dy)
    return dp, dw[:3]


def _lane_window(shape):
    lane = lax.broadcasted_iota(jnp.int32, shape, 1)
    return lane, jnp.where(lane < 64, 2.0, jnp.where(lane < 128, 4.0, jnp.where(lane < 192, 8.0, 16.0)))


def _by_group(lane, s1, s2, s3, s4):
    return jnp.where(lane < 64, s1, jnp.where(lane < 128, s2, jnp.where(lane < 192, s3, s4)))


def _pool_z(ext, row0):
    s1 = ext + _shift_down(ext, 1)
    s2 = s1 + _shift_down(s1, 2)
    s3 = s2 + _shift_down(s2, 4)
    s4 = s3 + _shift_down(s3, 8)
    lane, win = _lane_window(ext.shape)
    tpos = (lax.broadcasted_iota(jnp.int32, ext.shape, 0) + (row0 - 16 + 1)).astype(F32)
    cnt = jnp.maximum(jnp.minimum(tpos, win), 1.0)
    return _by_group(lane, s1, s2, s3, s4) / cnt - ext


ANY_SPEC = pl.BlockSpec(memory_space=pl.ANY)


def _pool_fwd(proj, wbd, scale, ybuf):
    s = proj.shape[0]
    t = min(ROW_TILE, s)
    col = (COL_GATE - GROUP) // GROUP

    def body(cur_ref, prev_ref, w_ref, sc_ref, buf_ref, o_ref):
        i = pl.program_id(0)
        ext = jnp.concatenate([prev_ref[...] * (i > 0).astype(F32), cur_ref[...]], axis=0)
        z = _pool_z(ext, i * t)[16:]
        y = jnp.dot(z.astype(BF16), w_ref[...].astype(BF16), preferred_element_type=F32)
        o_ref[...] = (y * sc_ref[...]).astype(o_ref.dtype)

    return pl.pallas_call(
        body, name="pool_fwd", grid=(s // t,), out_shape=jax.ShapeDtypeStruct(ybuf.shape, ybuf.dtype),
        in_specs=[pl.BlockSpec((t, GROUP), lambda i: (i, col)),
                  pl.BlockSpec((16, GROUP), _prev_map(t, 16, col)),
                  pl.BlockSpec((GROUP, GROUP), lambda i: (0, 0)),
                  pl.BlockSpec((1, GROUP), lambda i: (0, 0)), ANY_SPEC],
        out_specs=pl.BlockSpec((t, GROUP), lambda i: (i, 3)), input_output_aliases={4: 0},
        compiler_params=_params(),
    )(proj, proj, wbd, scale.reshape(1, GROUP), ybuf)


def _pool_bwd(proj, wbd, scale, dy, dbuf):
    s = proj.shape[0]
    t = min(ROW_TILE, s)
    nt = s // t
    col = (COL_GATE - GROUP) // GROUP

    def body(cur_ref, prev_ref, w_ref, sc_ref, dy_ref, dyn_ref, buf_ref, dp_ref, dw_ref, dsc_ref):
        i = pl.program_id(0)
        ext = jnp.concatenate([prev_ref[...] * (i > 0).astype(F32), cur_ref[...]], axis=0)
        z = _pool_z(ext, i * t)[16:]
        wv = w_ref[...].astype(BF16)
        dyc = dy_ref[...]
        dye = jnp.concatenate([dyc, dyn_ref[...] * (i < nt - 1).astype(F32)], axis=0) * sc_ref[...]
        dz = lax.dot_general(dye.astype(BF16), wv, (NT, ((), ())), preferred_element_type=F32)
        lane, win = _lane_window(dz.shape)
        tpos = (lax.broadcasted_iota(jnp.int32, dz.shape, 0) + (i * t + 1)).astype(F32)
        e = dz / jnp.minimum(tpos, win)
        f1 = e + _shift_up(e, 1)
        f2 = f1 + _shift_up(f1, 2)
        f3 = f2 + _shift_up(f2, 4)
        f4 = f3 + _shift_up(f3, 8)
        dp = _by_group(lane, f1, f2, f3, f4) - dz
        dp_ref[...] = dp[:t].astype(dp_ref.dtype)
        zb = z.astype(BF16)
        y = jnp.dot(zb, wv, preferred_element_type=F32)
        dsc = jnp.sum(dyc * y, axis=0, keepdims=True)
        dw = lax.dot_general(zb, dye[:t].astype(BF16), (TN, ((), ())), preferred_element_type=F32)

        @pl.when(i == 0)
        def _():
            dw_ref[...] = dw
            dsc_ref[...] = dsc

        @pl.when(i > 0)
        def _():
            dw_ref[...] += dw
            dsc_ref[...] += dsc

    dp, dw, dsc = pl.pallas_call(
        body, name="pool_bwd", grid=(nt,),
        out_shape=(jax.ShapeDtypeStruct(dbuf.shape, dbuf.dtype), jax.ShapeDtypeStruct((GROUP, GROUP), F32),
                   jax.ShapeDtypeStruct((1, GROUP), F32)),
        in_specs=[pl.BlockSpec((t, GROUP), lambda i: (i, col)),
                  pl.BlockSpec((16, GROUP), _prev_map(t, 16, col)),
                  pl.BlockSpec((GROUP, GROUP), lambda i: (0, 0)),
                  pl.BlockSpec((1, GROUP), lambda i: (0, 0)),
                  pl.BlockSpec((t, GROUP), lambda i: (i, 3)),
                  pl.BlockSpec((16, GROUP), _next_map(t, 16, 3, s)), ANY_SPEC],
        out_specs=(pl.BlockSpec((t, GROUP), lambda i: (i, col)), pl.BlockSpec((GROUP, GROUP), lambda i: (0, 0)),
                   pl.BlockSpec((1, GROUP), lambda i: (0, 0))),
        input_output_aliases={6: 0}, compiler_params=_params(),
    )(proj, proj, wbd, scale.reshape(1, GROUP), dy, dy, dbuf)
    return dp, dw, dsc.reshape(GROUP)


def _ffn_gate_fwd(u0, w):
    s = u0.shape[1]
    t = min(ROW_TILE, s)

    def body(a_ref, ap_ref, g_ref, gp_ref, wa_ref, wg_ref, o_ref):
        first = (pl.program_id(1) > 0).astype(F32)
        a = _conv3(jnp.concatenate([ap_ref[...] * first, a_ref[...]], axis=0), wa_ref[...])[8:]
        g = _conv3(jnp.concatenate([gp_ref[...] * first, g_ref[...]], axis=0), wg_ref[...])[8:]
        o_ref[...] = (a * (g * _sigmoid(g))).astype(o_ref.dtype)

    def cur(off):
        return pl.BlockSpec((None, t, FF_SHARD), lambda j, i: (j + off, i, 0))

    def prev(off):
        return pl.BlockSpec((None, 8, FF_SHARD), lambda j, i: (j + off, jnp.maximum(i * (t // 8) - 1, 0), 0))

    def wspec(off):
        return pl.BlockSpec((None, 3, FF_SHARD), lambda j, i: (j + off, 0, 0))

    return pl.pallas_call(
        body, name="ffn_gate_fwd", grid=(FF_HALF, s // t),
        out_shape=jax.ShapeDtypeStruct((FF_HALF, s, FF_SHARD), BF16),
        in_specs=[cur(0), prev(0), cur(FF_HALF), prev(FF_HALF), wspec(0), wspec(FF_HALF)],
        out_specs=pl.BlockSpec((None, t, FF_SHARD), lambda j, i: (j, i, 0)), compiler_params=_params(),
    )(u0, u0, u0, u0, w, w)


def _ffn_gate_bwd(u0, w, dact):
    s = u0.shape[1]
    t = min(ROW_TILE, s)
    nt = s // t

    def body(c_ref, p_ref, n_ref, w_ref, d_ref, dn_ref, du_ref, dw_ref):
        i = pl.program_id(1)
        first = (i > 0).astype(F32)
        last = (i < nt - 1).astype(F32)
        dext = jnp.concatenate([jnp.zeros((8, FF_SHARD), F32), d_ref[...], dn_ref[...] * last], axis=0)
        ext = [jnp.concatenate([p_ref[n] * first, c_ref[n], n_ref[n] * last], axis=0) for n in range(2)]
        a = _conv3(ext[0], w_ref[0])
        g = _conv3(ext[1], w_ref[1])
        sg = _sigmoid(g)
        silu = g * sg
        dus = (dext * silu, dext * a * (sg + silu * (1.0 - sg)))
        for n in range(2):
            du_ref[n] = _conv3_t(dus[n], w_ref[n])[8:8 + t].astype(du_ref.dtype)
            dc = dus[n][8:8 + t]
            dw = jnp.concatenate([
                jnp.sum(dc * _shift_down(ext[n], 2)[8:8 + t], axis=0, keepdims=True),
                jnp.sum(dc * _shift_down(ext[n], 1)[8:8 + t], axis=0, keepdims=True),
                jnp.sum(dc * ext[n][8:8 + t], axis=0, keepdims=True),
                jnp.zeros((5, FF_SHARD), F32)], axis=0)

            @pl.when(i == 0)
            def _(n=n, dw=dw):
                dw_ref[n] = dw

            @pl.when(i > 0)
            def _(n=n, dw=dw):
                dw_ref[n] += dw

    def pair(rows, row_map):
        return pl.BlockSpec((2, None, rows, FF_SHARD), lambda j, i: (0, j, row_map(i), 0))

    prev_row = lambda i: jnp.maximum(i * (t // 8) - 1, 0)
    next_row = lambda i: jnp.minimum((i + 1) * (t // 8), s // 8 - 1)
    u2 = u0.reshape(2, FF_HALF, s, FF_SHARD)
    du, dw = pl.pallas_call(
        body, name="ffn_gate_bwd", grid=(FF_HALF, nt),
        out_shape=(jax.ShapeDtypeStruct((2, FF_HALF, s, FF_SHARD), BF16),
                   jax.ShapeDtypeStruct((2, FF_HALF, 8, FF_SHARD), F32)),
        in_specs=[pair(t, lambda i: i), pair(8, prev_row), pair(8, next_row), pair(3, lambda i: 0),
                  pl.BlockSpec((None, t, FF_SHARD), lambda j, i: (j, i, 0)),
                  pl.BlockSpec((None, 8, FF_SHARD), lambda j, i: (j, next_row(i), 0))],
        out_specs=(pair(t, lambda i: i), pair(8, lambda i: 0)),
        compiler_params=_params(),
    )(u2, u2, u2, w.reshape(2, FF_HALF, 3, FF_SHARD), dact, dact)
    return du.reshape(2 * FF_HALF, s, FF_SHARD), dw.reshape(2 * FF_HALF, 8, FF_SHARD)[:, :3]


def _rope_tables(positions):
    inv_freq = ROPE_THETA ** (-jnp.arange(0, ROPE_DIM, 2, dtype=F32) / ROPE_DIM)
    ang = positions.astype(F32)[:, None] * inv_freq
    cos, sin = jnp.cos(ang), jnp.sin(ang)
    s = positions.shape[0]
    half = ROPE_DIM // 2
    rest = HEAD_DIM - ROPE_DIM
    ca = jnp.concatenate([cos, cos, jnp.ones((s, rest), F32)], axis=1)
    cb = jnp.concatenate([-sin, jnp.zeros((s, HEAD_DIM - half), F32)], axis=1)
    cc = jnp.concatenate([jnp.zeros((s, half), F32), sin, jnp.zeros((s, rest), F32)], axis=1)
    return tuple(jnp.tile(tb, (1, N_HEADS)) for tb in (ca, cb, cc))


def _heads_split(proj, col, tables, name):
    s = proj.shape[0]
    t = min(ROW_TILE, s)
    rope = tables is not None

    def body(*refs):
        x_ref = refs[0]
        q_ref, k_ref, v_ref = refs[-3:]
        xv = x_ref[...]
        parts = [xv[:, 0:GROUP], xv[:, GROUP:2 * GROUP], xv[:, 2 * GROUP:3 * GROUP]]
        if rope:
            ca, cb, cc = refs[1][...], refs[2][...], refs[3][...]
            for n in range(2):
                p = parts[n]
                parts[n] = p * ca + pltpu.roll(p, GROUP - 8, 1) * cb + pltpu.roll(p, 8, 1) * cc
        parts[0] = parts[0] * (HEAD_DIM ** -0.5)
        for o_ref, p in zip((q_ref, k_ref, v_ref), parts):
            for h in range(N_HEADS):
                o_ref[h] = p[:, h * HEAD_DIM:(h + 1) * HEAD_DIM].astype(o_ref.dtype)

    tab = pl.BlockSpec((t, GROUP), lambda i: (i, 0))
    heads = pl.BlockSpec((N_HEADS, t, HEAD_DIM), lambda i: (0, i, 0))
    hshape = jax.ShapeDtypeStruct((N_HEADS, s, HEAD_DIM), BF16)
    return pl.pallas_call(
        body, name=name, grid=(s // t,), out_shape=(hshape, hshape, hshape),
        in_specs=[pl.BlockSpec((t, 3 * GROUP), lambda i: (i, col))] + ([tab, tab, tab] if rope else []),
        out_specs=(heads, heads, heads), compiler_params=_params(),
    )(*((proj,) + (tuple(tables) if rope else ())))


def _heads_merge(dq, dk, dv, tables, name, dbuf, col):
    s = dq.shape[1]
    t = min(ROW_TILE, s)
    rope = tables is not None

    def body(*refs):
        o_ref = refs[-1]
        parts = [jnp.concatenate([r[h] for h in range(N_HEADS)], axis=1) for r in refs[:3]]
        parts[0] = parts[0] * (HEAD_DIM ** -0.5)
        if rope:
            ca, cb, cc = refs[3][...], refs[4][...], refs[5][...]
            for n in range(2):
                p = parts[n]
                parts[n] = p * ca + pltpu.roll(p * cb, 8, 1) + pltpu.roll(p * cc, GROUP - 8, 1)
        o_ref[...] = jnp.concatenate(parts, axis=1).astype(o_ref.dtype)

    tab = pl.BlockSpec((t, GROUP), lambda i: (i, 0))
    heads = pl.BlockSpec((N_HEADS, t, HEAD_DIM), lambda i: (0, i, 0))
    n_in = 6 if rope else 3
    return pl.pallas_call(
        body, name=name, grid=(s // t,), out_shape=jax.ShapeDtypeStruct(dbuf.shape, dbuf.dtype),
        in_specs=[heads, heads, heads] + ([tab, tab, tab] if rope else []) + [ANY_SPEC],
        out_specs=pl.BlockSpec((t, 3 * GROUP), lambda i: (i, col)), input_output_aliases={n_in: 0},
        compiler_params=_params(),
    )(*((dq, dk, dv) + (tuple(tables) if rope else ()) + (dbuf,)))


def _log_sigmoid(x):
    return jnp.minimum(x, 0.0) - jnp.log(1.0 + jnp.exp(-jnp.abs(x)))


def _scan_rows(x, reverse):
    n = x.shape[0]
    row = lax.broadcasted_iota(jnp.int32, x.shape, 0)
    k = 1
    while k < n:
        if reverse:
            x = x + jnp.where(row < n - k, _shift_up(x, k), 0.0)
        else:
            x = x + jnp.where(row >= k, _shift_down(x, k), 0.0)
        k *= 2
    return x


def _gate_cumsum(proj, bias):
    s = proj.shape[0]
    col = COL_GATE // 128

    def body(z_ref, b_ref, c_ref):
        c_ref[...] = _scan_rows(_log_sigmoid(z_ref[...] + b_ref[...]), False)

    return pl.pallas_call(
        body, name="gate_cumsum", grid=(1,), out_shape=jax.ShapeDtypeStruct((s, 128), F32),
        in_specs=[pl.BlockSpec((s, 128), lambda i: (0, col)), pl.BlockSpec((1, 128), lambda i: (0, 0))],
        out_specs=pl.BlockSpec((s, 128), lambda i: (0, 0)), compiler_params=_params(),
    )(proj, bias)


def _gate_cumsum_bwd(proj, bias, dc, dbuf):
    s = proj.shape[0]
    col = COL_GATE // 128

    def body(z_ref, b_ref, dc_ref, buf_ref, dz_ref, db_ref):
        dlogf = _scan_rows(dc_ref[...], True)
        dz = dlogf * _sigmoid(-(z_ref[...] + b_ref[...]))
        dz_ref[...] = dz.astype(dz_ref.dtype)
        db_ref[...] = jnp.sum(dz, axis=0, keepdims=True)

    return pl.pallas_call(
        body, name="gate_cumsum_bwd", grid=(1,),
        out_shape=(jax.ShapeDtypeStruct(dbuf.shape, dbuf.dtype), jax.ShapeDtypeStruct((1, 128), F32)),
        in_specs=[pl.BlockSpec((s, 128), lambda i: (0, col)), pl.BlockSpec((1, 128), lambda i: (0, 0)),
                  pl.BlockSpec((s, 128), lambda i: (0, 0)), ANY_SPEC],
        out_specs=(pl.BlockSpec((s, 128), lambda i: (0, col)), pl.BlockSpec((1, 128), lambda i: (0, 0))),
        input_output_aliases={3: 0}, compiler_params=_params(),
    )(proj, bias, dc, dbuf)


DIL_REACH = 2048


def _pair_weight(mode, d):
    if mode == "fox":
        return jnp.where(d >= 0, 1.0, 0.0)
    w1 = jnp.where(jnp.abs(d - 64) <= 64, 1.0, 0.0)
    w2 = jnp.where((d & 3) == 0, jnp.where(jnp.abs(d - 256) <= 256, 1.0, 0.0), 0.0)
    w3 = jnp.where((d & 15) == 0, jnp.where(jnp.abs(d - 1024) <= 1024, 1.0, 0.0), 0.0)
    return w1 + w2 + w3


def _bias_tables(mode, tq, tk):
    nb = 2 if mode == "fox" else DIL_REACH // tk + 1
    n = lax.broadcasted_iota(jnp.int32, (nb, tq, tk), 0)
    r = lax.broadcasted_iota(jnp.int32, (nb, tq, tk), 1)
    c = lax.broadcasted_iota(jnp.int32, (nb, tq, tk), 2)
    w = _pair_weight(mode, n * tk + r - c)
    tab = jnp.where(w > 0.0, jnp.log(jnp.maximum(w, 1.0)), NEG)
    return tab, tab.transpose(0, 2, 1)


M_INIT = -1e29


def _first_key_chunk(mode, q0, tk):
    if mode == "fox":
        return 0
    return jnp.maximum(q0 - DIL_REACH, 0) // tk


def _attn_fwd(mode, q, k, v, tab, c_col, c_row, ybuf, col):
    s = q.shape[1]
    tq, tk = min(ATT_TQ, s), min(ATT_TK, s)
    fox = mode == "fox"
    nb = tab.shape[0]

    def body(*refs):
        q_ref, k_ref, v_ref, tab_ref = refs[:4]
        cc_ref, cr_ref = (refs[4], refs[5]) if fox else (None, None)
        y_ref, o_ref, lse_ref = refs[-3:]
        i = pl.program_id(0)
        q0 = i * tq
        lo = _first_key_chunk(mode, q0, tk)
        hi = (q0 + tq + tk - 1) // tk
        for h in range(N_HEADS):
            qv = q_ref[h]
            ccol = cc_ref[h] if fox else None

            def step(c, carry, h=h, qv=qv, ccol=ccol):
                m, l, acc = carry
                k0 = pl.multiple_of(c * tk, tk)
                kv = k_ref[h, pl.ds(k0, tk), :]
                vv = v_ref[h, pl.ds(k0, tk), :]
                sc = lax.dot_general(qv, kv, (NT, ((), ())), preferred_element_type=F32)
                sc = sc + tab_ref[jnp.minimum(i - c, nb - 1)]
                if fox:
                    sc = sc + (ccol - cr_ref[h, :, pl.ds(k0, tk)])
                m_new = jnp.maximum(m, jnp.max(sc, axis=-1, keepdims=True))
                alpha = jnp.exp(m - m_new)
                p = jnp.exp(sc - m_new)
                l = alpha * l + jnp.sum(p, axis=-1, keepdims=True)
                acc = alpha * acc + jnp.dot(p.astype(BF16), vv, preferred_element_type=F32)
                return m_new, l, acc

            m, l, acc = lax.fori_loop(
                lo, hi, step,
                (jnp.full((tq, 1), M_INIT, F32), jnp.zeros((tq, 1), F32), jnp.zeros((tq, HEAD_DIM), F32)))
            out = acc / l
            y_ref[:, h * HEAD_DIM:(h + 1) * HEAD_DIM] = out.astype(y_ref.dtype)
            o_ref[:, h * HEAD_DIM:(h + 1) * HEAD_DIM] = out
            lse_ref[h] = m + jnp.log(l)

    qspec = pl.BlockSpec((N_HEADS, tq, HEAD_DIM), lambda i: (0, i, 0))
    full = pl.BlockSpec((N_HEADS, s, HEAD_DIM), lambda i: (0, 0, 0))
    tabspec = pl.BlockSpec((nb, tq, tk), lambda i: (0, 0, 0))
    colspec = pl.BlockSpec((N_HEADS, tq, 1), lambda i: (0, i, 0))
    rowfull = pl.BlockSpec((N_HEADS, 1, s), lambda i: (0, 0, 0))
    n_in = 6 if fox else 4
    return pl.pallas_call(
        body, name="attn_fwd_" + mode, grid=(s // tq,),
        out_shape=(jax.ShapeDtypeStruct(ybuf.shape, ybuf.dtype), jax.ShapeDtypeStruct((s, GROUP), F32),
                   jax.ShapeDtypeStruct((N_HEADS, s, 1), F32)),
        in_specs=[qspec, full, full, tabspec] + ([colspec, rowfull] if fox else []) + [ANY_SPEC],
        out_specs=(pl.BlockSpec((tq, GROUP), lambda i: (i, col)), pl.BlockSpec((tq, GROUP), lambda i: (i, 0)),
                   colspec),
        input_output_aliases={n_in: 0}, compiler_params=_params(),
    )(*((q, k, v, tab) + ((c_col, c_row) if fox else ()) + (ybuf,)))


def _attn_bwd_q(mode, q, k, v, tab, o, do, col, lse, c_col, c_row):
    s = q.shape[1]
    tq, tk = min(ATT_TQ, s), min(ATT_TK, s)
    fox = mode == "fox"
    nb = tab.shape[0]

    def body(*refs):
        q_ref, k_ref, v_ref, tab_ref, o_ref, do_ref, lse_ref = refs[:7]
        cc_ref, cr_ref = (refs[7], refs[8]) if fox else (None, None)
        dq_ref, delta_ref, dcr_ref = refs[-3:]
        i = pl.program_id(0)
        q0 = i * tq
        lo = _first_key_chunk(mode, q0, tk)
        hi = (q0 + tq + tk - 1) // tk
        for h in range(N_HEADS):
            qv = q_ref[h]
            dov = do_ref[:, h * HEAD_DIM:(h + 1) * HEAD_DIM]
            delta = jnp.sum(dov * o_ref[:, h * HEAD_DIM:(h + 1) * HEAD_DIM], axis=-1, keepdims=True)
            dob = dov.astype(BF16)
            lse = lse_ref[h]
            ccol = cc_ref[h] if fox else None

            def step(c, carry, h=h, qv=qv, dob=dob, delta=delta, lse=lse, ccol=ccol):
                dq, dcr = carry
                k0 = pl.multiple_of(c * tk, tk)
                kv = k_ref[h, pl.ds(k0, tk), :]
                vv = v_ref[h, pl.ds(k0, tk), :]
                sc = lax.dot_general(qv, kv, (NT, ((), ())), preferred_element_type=F32)
                sc = sc + tab_ref[jnp.minimum(i - c, nb - 1)]
                if fox:
                    sc = sc + (ccol - cr_ref[h, :, pl.ds(k0, tk)])
                p = jnp.exp(sc - lse)
                dp = lax.dot_general(dob, vv, (NT, ((), ())), preferred_element_type=F32)
                ds = p * (dp - delta)
                dq = dq + jnp.dot(ds.astype(BF16), kv, preferred_element_type=F32)
                if fox:
                    dcr = dcr + jnp.sum(ds, axis=-1, keepdims=True)
                return dq, dcr

            dq, dcr = lax.fori_loop(lo, hi, step, (jnp.zeros((tq, HEAD_DIM), F32), jnp.zeros((tq, 1), F32)))
            dq_ref[h] = dq
            delta_ref[h] = delta
            dcr_ref[h] = dcr

    qspec = pl.BlockSpec((N_HEADS, tq, HEAD_DIM), lambda i: (0, i, 0))
    full = pl.BlockSpec((N_HEADS, s, HEAD_DIM), lambda i: (0, 0, 0))
    colspec = pl.BlockSpec((N_HEADS, tq, 1), lambda i: (0, i, 0))
    rowfull = pl.BlockSpec((N_HEADS, 1, s), lambda i: (0, 0, 0))
    row = pl.BlockSpec((tq, GROUP), lambda i: (i, 0))
    dorow = pl.BlockSpec((tq, GROUP), lambda i: (i, col))
    tabspec = pl.BlockSpec((nb, tq, tk), lambda i: (0, 0, 0))
    col_shape = jax.ShapeDtypeStruct((N_HEADS, s, 1), F32)
    return pl.pallas_call(
        body, name="attn_bwd_q_" + mode, grid=(s // tq,),
        out_shape=(jax.ShapeDtypeStruct((N_HEADS, s, HEAD_DIM), F32), col_shape, col_shape),
        in_specs=[qspec, full, full, tabspec, row, dorow, colspec] + ([colspec, rowfull] if fox else []),
        out_specs=(qspec, colspec, colspec), compiler_params=_params(),
    )(*((q, k, v, tab, o, do, lse) + ((c_col, c_row) if fox else ())))


def _attn_bwd_kv(mode, q, k, v, tab_t, do, col, lse_row, delta_row, c_col, c_row):
    s = q.shape[1]
    tq, tk = min(ATT_TQ, s), min(ATT_TK, s)
    nq = s // tq
    fox = mode == "fox"
    nb = tab_t.shape[0]

    def body(*refs):
        q_ref, k_ref, v_ref, tab_ref, do_ref, lse_ref, delta_ref = refs[:7]
        cc_ref, cr_ref = (refs[7], refs[8]) if fox else (None, None)
        dk_ref, dv_ref, dcc_ref = refs[-3:]
        i = pl.program_id(0)
        k0 = i * tk
        lo = k0 // tq
        hi = nq if fox else jnp.minimum((k0 + tk - 1 + DIL_REACH) // tq + 1, nq)
        for h in range(N_HEADS):
            kv = k_ref[h]
            vv = v_ref[h]
            ccol = cc_ref[h] if fox else None

            def step(c, carry, h=h, kv=kv, vv=vv, ccol=ccol):
                dk, dv, dcc = carry
                q0 = pl.multiple_of(c * tq, tq)
                qv = q_ref[h, pl.ds(q0, tq), :]
                dob = do_ref[pl.ds(q0, tq), h * HEAD_DIM:(h + 1) * HEAD_DIM].astype(BF16)
                sc = lax.dot_general(kv, qv, (NT, ((), ())), preferred_element_type=F32)
                sc = sc + tab_ref[jnp.minimum(c - i, nb - 1)]
                if fox:
                    sc = sc + (cr_ref[h, :, pl.ds(q0, tq)] - ccol)
                p = jnp.exp(sc - lse_ref[h, :, pl.ds(q0, tq)])
                dp = lax.dot_general(vv, dob, (NT, ((), ())), preferred_element_type=F32)
                ds = p * (dp - delta_ref[h, :, pl.ds(q0, tq)])
                dv = dv + jnp.dot(p.astype(BF16), dob, preferred_element_type=F32)
                dk = dk + jnp.dot(ds.astype(BF16), qv, preferred_element_type=F32)
                if fox:
                    dcc = dcc + jnp.sum(ds, axis=-1, keepdims=True)
                return dk, dv, dcc

            dk, dv, dcc = lax.fori_loop(
                lo, hi, step,
                (jnp.zeros((tk, HEAD_DIM), F32), jnp.zeros((tk, HEAD_DIM), F32), jnp.zeros((tk, 1), F32)))
            dk_ref[h] = dk
            dv_ref[h] = dv
            dcc_ref[h] = dcc

    kspec = pl.BlockSpec((N_HEADS, tk, HEAD_DIM), lambda i: (0, i, 0))
    full = pl.BlockSpec((N_HEADS, s, HEAD_DIM), lambda i: (0, 0, 0))
    colspec = pl.BlockSpec((N_HEADS, tk, 1), lambda i: (0, i, 0))
    rowfull = pl.BlockSpec((N_HEADS, 1, s), lambda i: (0, 0, 0))
    dofull = pl.BlockSpec((s, GROUP), lambda i: (0, col))
    tabspec = pl.BlockSpec((nb, tk, tq), lambda i: (0, 0, 0))
    hshape = jax.ShapeDtypeStruct((N_HEADS, s, HEAD_DIM), F32)
    return pl.pallas_call(
        body, name="attn_bwd_kv_" + mode, grid=(s // tk,),
        out_shape=(hshape, hshape, jax.ShapeDtypeStruct((N_HEADS, s, 1), F32)),
        in_specs=[full, kspec, kspec, tabspec, dofull, rowfull, rowfull] + ([colspec, rowfull] if fox else []),
        out_specs=(kspec, kspec, colspec), compiler_params=_params(),
    )(*((q, k, v, tab_t, do, lse_row, delta_row) + ((c_col, c_row) if fox else ())))


def _xattn_fwd(qx, kvm):
    s = qx.shape[0]
    t = min(ROW_TILE, s)

    def body(q_ref, kv_ref, o_ref):
        for h in range(XA_HEADS):
            qv = q_ref[:, h * XA_DIM:(h + 1) * XA_DIM].astype(BF16)
            kv = kv_ref[h].astype(BF16)
            vv = kv_ref[XA_HEADS + h].astype(BF16)
            sc = lax.dot_general(qv, kv, (NT, ((), ())), preferred_element_type=F32) * (XA_DIM ** -0.5)
            e = jnp.exp(sc - jnp.max(sc, axis=-1, keepdims=True))
            p = e / jnp.sum(e, axis=-1, keepdims=True)
            o_ref[:, h * XA_DIM:(h + 1) * XA_DIM] = jnp.dot(p.astype(BF16), vv,
                                                             preferred_element_type=F32).astype(o_ref.dtype)

    return pl.pallas_call(
        body, name="xattn_fwd", grid=(s // t,), out_shape=jax.ShapeDtypeStruct((s, D_MODEL), BF16),
        in_specs=[pl.BlockSpec((t, D_MODEL), lambda i: (i, 0)),
                  pl.BlockSpec((2 * XA_HEADS, MEM_LEN, XA_DIM), lambda i: (0, 0, 0))],
        out_specs=pl.BlockSpec((t, D_MODEL), lambda i: (i, 0)), compiler_params=_params(),
    )(qx, kvm)


def _xattn_bwd(qx, kvm, do):
    s = qx.shape[0]
    t = min(ROW_TILE, s)

    def body(q_ref, kv_ref, do_ref, dq_ref, dkv_ref):
        i = pl.program_id(0)
        for h in range(XA_HEADS):
            qv = q_ref[:, h * XA_DIM:(h + 1) * XA_DIM].astype(BF16)
            dov = do_ref[:, h * XA_DIM:(h + 1) * XA_DIM].astype(BF16)
            kv = kv_ref[h].astype(BF16)
            vv = kv_ref[XA_HEADS + h].astype(BF16)
            sc = lax.dot_general(qv, kv, (NT, ((), ())), preferred_element_type=F32) * (XA_DIM ** -0.5)
            e = jnp.exp(sc - jnp.max(sc, axis=-1, keepdims=True))
            p = e / jnp.sum(e, axis=-1, keepdims=True)
            dp = lax.dot_general(dov, vv, (NT, ((), ())), preferred_element_type=F32)
            ds = (p * (dp - jnp.sum(p * dp, axis=-1, keepdims=True)) * (XA_DIM ** -0.5)).astype(BF16)
            dq_ref[:, h * XA_DIM:(h + 1) * XA_DIM] = jnp.dot(ds, kv, preferred_element_type=F32).astype(dq_ref.dtype)
            dk = lax.dot_general(ds, qv, (TN, ((), ())), preferred_element_type=F32)
            dv = lax.dot_general(p.astype(BF16), dov, (TN, ((), ())), preferred_element_type=F32)

            @pl.when(i == 0)
            def _(h=h, dk=dk, dv=dv):
                dkv_ref[h] = dk
                dkv_ref[XA_HEADS + h] = dv

            @pl.when(i > 0)
            def _(h=h, dk=dk, dv=dv):
                dkv_ref[h] += dk
                dkv_ref[XA_HEADS + h] += dv

    row = pl.BlockSpec((t, D_MODEL), lambda i: (i, 0))
    kvs = pl.BlockSpec((2 * XA_HEADS, MEM_LEN, XA_DIM), lambda i: (0, 0, 0))
    return pl.pallas_call(
        body, name="xattn_bwd", grid=(s // t,),
        out_shape=(jax.ShapeDtypeStruct((s, D_MODEL), BF16),
                   jax.ShapeDtypeStruct((2 * XA_HEADS, MEM_LEN, XA_DIM), F32)),
        in_specs=[row, kvs, row], out_specs=(row, kvs), compiler_params=_params(),
    )(qx, kvm, do)


def _adamw(parts, w, m, v, name):
    nl, r, c = w.shape
    tr = r
    for cand in (256, 128, 64, 32, 16, 8):
        if r % cand == 0 and r > cand and N_DEV * cand * c * 4 <= ADAMW_BLOCK_BYTES:
            tr = cand
            break
    nt = r // tr

    def body(*refs):
        p_refs = refs[:nl]
        w_ref, m_ref, v_ref, g_ref, d_ref, nm_ref, nv_ref = refs[nl:]
        layer = pl.program_id(0)
        g = None
        for l in range(nl):
            gl = p_refs[l][0]
            for d in range(1, N_DEV):
                gl = gl + p_refs[l][d]
            g = gl if g is None else jnp.where(layer == l, gl, g)
        mn = ADAM_B1 * m_ref[...] + (1.0 - ADAM_B1) * g
        vn = ADAM_B2 * v_ref[...] + (1.0 - ADAM_B2) * (g * g)
        m_hat = mn / (1.0 - ADAM_B1 ** ADAM_STEP)
        v_hat = vn / (1.0 - ADAM_B2 ** ADAM_STEP)
        g_ref[...] = g
        d_ref[...] = -ADAM_LR * (m_hat / (jnp.sqrt(v_hat) + ADAM_EPS) + ADAM_WD * w_ref[...])
        nm_ref[...] = mn
        nv_ref[...] = vn

    def part_spec(l):
        return pl.BlockSpec((N_DEV, tr, c),
                            lambda ll, i: (0, jnp.where(ll == l, i, jnp.where(ll < l, 0, nt - 1)), 0))

    blk = pl.BlockSpec((None, tr, c), lambda ll, i: (ll, i, 0))
    shp = jax.ShapeDtypeStruct((nl, r, c), F32)
    return pl.pallas_call(
        body, name=name, grid=(nl, nt), out_shape=(shp, shp, shp, shp),
        in_specs=[part_spec(l) for l in range(nl)] + [blk, blk, blk],
        out_specs=(blk, blk, blk, blk), compiler_params=_params(),
    )(*parts, w, m, v)


GROUPS = {"in": ("w_in",), "rest": ("w_out", "w_xq", "w_xo", "w_xkv", "w_up", "w_down")}
FULL_SHAPES = {"w_in": (D_MODEL, N_IN_PAD), "w_out": (D_MODEL, D_MODEL), "w_xq": (D_MODEL, D_MODEL),
               "w_xo": (D_MODEL, D_MODEL), "w_xkv": (N_DEV, D_MODEL, 2 * D_MODEL // N_DEV),
               "w_up": (N_DEV, D_MODEL, FF_SHARD), "w_down": (FF_HALF, FF_SHARD, D_MODEL)}
PIECE_SHAPES = {"w_in": (N_DEV, D_MODEL // N_DEV, N_IN_PAD), "w_out": (N_DEV, D_MODEL // N_DEV, D_MODEL),
                "w_xq": (N_DEV, D_MODEL // N_DEV, D_MODEL), "w_xo": (N_DEV, D_MODEL // N_DEV, D_MODEL),
                "w_xkv": (N_DEV, D_MODEL, 2 * D_MODEL // N_DEV), "w_up": (N_DEV, D_MODEL, FF_SHARD),
                "w_down": (N_DEV, D_FF // N_DEV, D_MODEL)}
CONV_WORDS = 8192
SMALL_WORDS = 80 * 1024


class _GatheredWeights:
    def __init__(self, states, layer):
        self.states, self.layer, self.full, self.extra = dict(states), layer, {}, None

    def need(self, group, after):
        if group in self.states:
            got = _exchange_wait(self.states.pop(group), after, "gather_%s_wait_%d" % (group, self.layer))
            for name, g in zip(GROUPS[group], got):
                self.full[name] = g.reshape(FULL_SHAPES[name])
            self.extra = got[len(GROUPS[group]):]

    def __getitem__(self, name):
        return self.full[name]


def _relay_in_cols(w):
    pad = jnp.zeros(w.shape[:-1] + (N_IN_PAD - N_IN,), w.dtype)
    return jnp.concatenate([w[..., :2304], w[..., 2308:N_IN], w[..., 2304:2308], pad], axis=-1)


def _unrelay_in_cols(w):
    return jnp.concatenate([w[..., :2304], w[..., COL_GATE:COL_GATE + 4], w[..., 2304:COL_GATE]], axis=-1)


def _layer_fwd(h, memv, w, sm, tables):
    sv = {"h0": h}
    xn = _rms_fwd(h, sm["g_mix"], "rms_mix")
    w.need("in", xn)
    proj = _mm_nn(xn, w["w_in"], "mm_in", tn=896)
    sv["xn"], sv["proj"] = xn, proj
    ycat = _sconv_fwd(proj, sm["w_sconv"])
    qd, kd, vd = _heads_split(proj, 1, tables["rope"], "split_dil")
    ycat, ob, lse_b = _attn_fwd("dil", qd, kd, vd, tables["dil"][0], None, None, ycat, 1)
    sv["dil"] = (qd, kd, vd, ob, lse_b)
    qf, kf, vf = _heads_split(proj, 2, None, "split_fox")
    c = _gate_cumsum(proj, sm["b_forget_pad"])
    ct = c[:, :N_HEADS].T
    c_col, c_row = ct.reshape(N_HEADS, -1, 1), ct.reshape(N_HEADS, 1, -1)
    ycat, oc, lse_c = _attn_fwd("fox", qf, kf, vf, tables["fox"][0], c_col, c_row, ycat, 2)
    sv["fox"] = (qf, kf, vf, oc, lse_c, c_col, c_row)
    ycat = _pool_fwd(proj, sm["w_pool_bd"], sm["pool_scale"], ycat)
    sv["ycat"] = ycat
    w.need("rest", ycat)
    h1 = _mm_nn(ycat, w["w_out"], "mm_out", res=h)
    sv["h1"] = h1
    xq = _rms_fwd(h1, sm["g_xa"], "rms_xa")
    memn = _rms_fwd(memv, sm["g_mem"], "rms_mem")
    qx = _mm_nn(xq, w["w_xq"], "mm_xq")
    kvm = _matmul(memn, w["w_xkv"], (N_DEV, MEM_LEN, XA_DIM), grid=(N_DEV, 1, 1),
                  a_spec=pl.BlockSpec((MEM_LEN, D_MODEL), lambda i, j, r: (0, 0)),
                  b_spec=pl.BlockSpec((None, D_MODEL, XA_DIM), lambda i, j, r: (i, 0, 0)),
                  o_spec=pl.BlockSpec((None, MEM_LEN, XA_DIM), lambda i, j, r: (i, 0, 0)),
                  dims=NN, nred=1, name="mm_xkv")
    ox = _xattn_fwd(qx, kvm)
    sv.update(xq=xq, memn=memn, qx=qx, kvm=kvm, ox=ox)
    h2 = _mm_nn(ox, w["w_xo"], "mm_xo", res=h1)
    sv["h2"] = h2
    xf = _rms_fwd(h2, sm["g_ffn"], "rms_ffn")
    s = h.shape[0]
    tm = min(ROW_TILE, s)
    u0 = _matmul(xf, w["w_up"], (N_DEV, s, FF_SHARD), grid=(s // tm, N_DEV, 1),
                 a_spec=pl.BlockSpec((tm, D_MODEL), lambda i, j, r: (i, 0)),
                 b_spec=pl.BlockSpec((None, D_MODEL, FF_SHARD), lambda i, j, r: (j, 0, 0)),
                 o_spec=pl.BlockSpec((None, tm, FF_SHARD), lambda i, j, r: (j, i, 0)),
                 dims=NN, nred=1, name="mm_up")
    act = _ffn_gate_fwd(u0, sm["w_ffconv"])
    sv.update(xf=xf, u0=u0, act=act)
    ospec = pl.BlockSpec((tm, D_MODEL), lambda i, j, r: (i, 0))
    h3 = _matmul(act, w["w_down"], (s, D_MODEL), grid=(s // tm, 1, FF_HALF),
                 a_spec=pl.BlockSpec((None, tm, FF_SHARD), lambda i, j, r: (r, i, 0)),
                 b_spec=pl.BlockSpec((None, FF_SHARD, D_MODEL), lambda i, j, r: (r, 0, 0)),
                 o_spec=ospec, dims=NN, nred=FF_HALF, name="mm_down", res=h2, res_spec=ospec)
    return h3, sv


def _layer_bwd(dh3, memv, w, sm, tables, sv, rest_ready):
    s = dh3.shape[0]
    tm = min(ROW_TILE, s)
    ts = min(1024, s)
    big, small = {}, {}
    dact = _matmul(dh3, w["w_down"], (FF_HALF, s, FF_SHARD), grid=(s // tm, FF_HALF, 1),
                   a_spec=pl.BlockSpec((tm, D_MODEL), lambda i, j, r: (i, 0)),
                   b_spec=pl.BlockSpec((None, FF_SHARD, D_MODEL), lambda i, j, r: (j, 0, 0)),
                   o_spec=pl.BlockSpec((None, tm, FF_SHARD), lambda i, j, r: (j, i, 0)),
                   dims=NT, nred=1, name="mm_dact")
    big["w_down"] = _matmul(sv["act"], dh3, (FF_HALF, FF_SHARD, D_MODEL), grid=(FF_HALF, 1, s // ts),
                            a_spec=pl.BlockSpec((None, ts, FF_SHARD), lambda i, j, r: (i, r, 0)),
                            b_spec=pl.BlockSpec((ts, D_MODEL), lambda i, j, r: (r, 0)),
                            o_spec=pl.BlockSpec((None, FF_SHARD, D_MODEL), lambda i, j, r: (i, 0, 0)),
                            dims=TN, nred=s // ts, name="mm_dw_down")
    du0, small["w_ffconv"] = _ffn_gate_bwd(sv["u0"], sm["w_ffconv"], dact)
    dxf = _matmul(du0, w["w_up"], (s, D_MODEL), grid=(s // tm, 1, N_DEV),
                  a_spec=pl.BlockSpec((None, tm, FF_SHARD), lambda i, j, r: (r, i, 0)),
                  b_spec=pl.BlockSpec((None, D_MODEL, FF_SHARD), lambda i, j, r: (r, 0, 0)),
                  o_spec=pl.BlockSpec((tm, D_MODEL), lambda i, j, r: (i, 0)),
                  dims=NT, nred=N_DEV, name="mm_dxf")
    big["w_up"] = _matmul(sv["xf"], du0, (N_DEV, D_MODEL, FF_SHARD), grid=(N_DEV, 1, s // ts),
                          a_spec=pl.BlockSpec((ts, D_MODEL), lambda i, j, r: (r, 0)),
                          b_spec=pl.BlockSpec((None, ts, FF_SHARD), lambda i, j, r: (i, r, 0)),
                          o_spec=pl.BlockSpec((None, D_MODEL, FF_SHARD), lambda i, j, r: (i, 0, 0)),
                          dims=TN, nred=s // ts, name="mm_dw_up")
    dh2, small["g_ffn"] = _rms_bwd(dxf, sv["h2"], sm["g_ffn"], dh3, "rms_ffn_bwd")
    dox = _mm_nt(dh2, w["w_xo"], "mm_dox")
    big["w_xo"] = _mm_tn(sv["ox"], dh2, "mm_dw_xo")
    dqx, dkvm = _xattn_bwd(sv["qx"], sv["kvm"], dox)
    dxq = _mm_nt(dqx, w["w_xq"], "mm_dxq")
    big["w_xq"] = _mm_tn(sv["xq"], dqx, "mm_dw_xq")
    big["w_xkv"] = _matmul(sv["memn"], dkvm, (N_DEV, D_MODEL, XA_DIM), grid=(N_DEV, 1, 1),
                           a_spec=pl.BlockSpec((MEM_LEN, D_MODEL), lambda i, j, r: (0, 0)),
                           b_spec=pl.BlockSpec((None, MEM_LEN, XA_DIM), lambda i, j, r: (i, 0, 0)),
                           o_spec=pl.BlockSpec((None, D_MODEL, XA_DIM), lambda i, j, r: (i, 0, 0)),
                           dims=TN, nred=1, name="mm_dw_xkv")
    dmemn = _matmul(dkvm, w["w_xkv"], (MEM_LEN, D_MODEL), grid=(1, 1, N_DEV),
                    a_spec=pl.BlockSpec((None, MEM_LEN, XA_DIM), lambda i, j, r: (r, 0, 0)),
                    b_spec=pl.BlockSpec((None, D_MODEL, XA_DIM), lambda i, j, r: (r, 0, 0)),
                    o_spec=pl.BlockSpec((MEM_LEN, D_MODEL), lambda i, j, r: (0, 0)),
                    dims=NT, nred=N_DEV, name="mm_dmemn")
    _, small["g_mem"] = _rms_bwd(dmemn, memv, sm["g_mem"], None, "rms_mem_bwd")
    dh1, small["g_xa"] = _rms_bwd(dxq, sv["h1"], sm["g_xa"], dh2, "rms_xa_bwd")
    dycat = _mm_nt(dh1, w["w_out"], "mm_dycat")
    big["w_out"] = _mm_tn(sv["ycat"], dh1, "mm_dw_out")
    proj = sv["proj"]
    dproj, small["w_sconv"] = _sconv_bwd(proj, sm["w_sconv"] + rest_ready(big), dycat)
    qd, kd, vd, ob, lse_b = sv["dil"]
    tab, tab_t = tables["dil"]
    dq, delta, _ = _attn_bwd_q("dil", qd, kd, vd, tab, ob, dycat, 1, lse_b, None, None)
    dk, dv, _ = _attn_bwd_kv("dil", qd, kd, vd, tab_t, dycat, 1, lse_b.reshape(N_HEADS, 1, s),
                             delta.reshape(N_HEADS, 1, s), None, None)
    dproj = _heads_merge(dq, dk, dv, tables["rope"], "merge_dil", dproj, 1)
    qf, kf, vf, oc, lse_c, c_col, c_row = sv["fox"]
    tab, tab_t = tables["fox"]
    dq, delta, dc_rows = _attn_bwd_q("fox", qf, kf, vf, tab, oc, dycat, 2, lse_c, c_col, c_row)
    dk, dv, dc_cols = _attn_bwd_kv("fox", qf, kf, vf, tab_t, dycat, 2, lse_c.reshape(N_HEADS, 1, s),
                                   delta.reshape(N_HEADS, 1, s), c_col, c_row)
    dproj = _heads_merge(dq, dk, dv, None, "merge_fox", dproj, 2)
    dc = (dc_rows - dc_cols).reshape(N_HEADS, s).T
    dc = jnp.concatenate([dc, jnp.zeros((s, 128 - N_HEADS), F32)], axis=1)
    dproj, dbias = _gate_cumsum_bwd(proj, sm["b_forget_pad"], dc, dproj)
    small["b_forget"] = dbias[0, :N_HEADS]
    dproj, dwbd, small["pool_scale"] = _pool_bwd(proj, sm["w_pool_bd"], sm["pool_scale"], dycat, dproj)
    small["w_pool"] = jnp.stack([dwbd[64 * g:64 * (g + 1), 64 * g:64 * (g + 1)] for g in range(4)])
    dxn = _mm_nt(dproj, w["w_in"], "mm_dxn")
    big["w_in"] = _mm_tn(sv["xn"], dproj, "mm_dw_in", tn=896)
    dh0, small["g_mix"] = _rms_bwd(dxn, sv["h0"], sm["g_mix"], dh1, "rms_mix_bwd")
    return dh0, big, small


SMALL_NAMES = ("g_mix", "b_forget", "w_pool", "pool_scale", "g_xa", "g_mem", "g_ffn", "w_sconv", "w_ffconv")
WEIGHT_NAMES = ("g_mix", "w_in", "b_forget", "w_sconv", "w_pool", "pool_scale", "w_out", "g_xa", "g_mem", "w_xq",
                "w_xkv", "w_xo", "g_ffn", "w_up", "w_ffconv", "w_down", "g_final")


def _block_diag(w_pool):
    z = jnp.zeros((64, 64), F32)
    return jnp.concatenate(
        [jnp.concatenate([w_pool[g] if c == g else z for c in range(4)], axis=1) for g in range(4)], axis=0)


def kernel(x, mem, positions, g_mix, w_in, b_forget, w_sconv, w_pool, pool_scale, w_out, g_xa, g_mem, w_xq, w_xkv, w_xo, g_ffn, w_up, w_ffconv, w_down, g_final, loss_target, m_g_mix, m_w_in, m_b_forget, m_w_sconv, m_w_pool, m_pool_scale, m_w_out, m_g_xa, m_g_mem, m_w_xq, m_w_xkv, m_w_xo, m_g_ffn, m_w_up, m_w_ffconv, m_w_down, m_g_final, v_g_mix, v_w_in, v_b_forget, v_w_sconv, v_w_pool, v_pool_scale, v_w_out, v_g_xa, v_g_mem, v_w_xq, v_w_xkv, v_w_xo, v_g_ffn, v_w_up, v_w_ffconv, v_w_down, v_g_final):
    weights = dict(g_mix=g_mix, w_in=w_in, b_forget=b_forget, w_sconv=w_sconv, w_pool=w_pool, pool_scale=pool_scale,
                   w_out=w_out, g_xa=g_xa, g_mem=g_mem, w_xq=w_xq, w_xkv=w_xkv, w_xo=w_xo, g_ffn=g_ffn, w_up=w_up,
                   w_ffconv=w_ffconv, w_down=w_down, g_final=g_final)
    m_in = dict(g_mix=m_g_mix, w_in=m_w_in, b_forget=m_b_forget, w_sconv=m_w_sconv, w_pool=m_w_pool,
                pool_scale=m_pool_scale, w_out=m_w_out, g_xa=m_g_xa, g_mem=m_g_mem, w_xq=m_w_xq, w_xkv=m_w_xkv,
                w_xo=m_w_xo, g_ffn=m_g_ffn, w_up=m_w_up, w_ffconv=m_w_ffconv, w_down=m_w_down, g_final=m_g_final)
    v_in = dict(g_mix=v_g_mix, w_in=v_w_in, b_forget=v_b_forget, w_sconv=v_w_sconv, w_pool=v_w_pool,
                pool_scale=v_pool_scale, w_out=v_w_out, g_xa=v_g_xa, g_mem=v_g_mem, w_xq=v_w_xq, w_xkv=v_w_xkv,
                w_xo=v_w_xo, g_ffn=v_g_ffn, w_up=v_w_up, w_ffconv=v_w_ffconv, w_down=v_w_down, g_final=v_g_final)
    depth = w_in.shape[0]
    me = 4 * lax.axis_index("x") + 2 * lax.axis_index("y") + lax.axis_index("c")
    h = x[0]
    memv = mem[0]
    s = h.shape[0]
    tq = min(ATT_TQ, s)
    tables = {"rope": _rope_tables(positions[0]), "dil": _bias_tables("dil", tq, tq),
              "fox": _bias_tables("fox", tq, tq)}

    w_in_r = _relay_in_cols(w_in)
    conv_shard = jnp.concatenate([w_sconv.reshape(-1), w_ffconv.reshape(-1)])
    conv_shard = jnp.concatenate([conv_shard, jnp.zeros((CONV_WORDS - conv_shard.shape[0],), F32)])
    conv_bits = lax.bitcast_convert_type(conv_shard, BF16).reshape(2 * CONV_WORDS // 1024, 1024)
    gathered = []
    order = jnp.zeros((), F32)
    for l in range(depth):
        shards = dict(w_in=w_in_r[l], w_out=w_out[l], w_xq=w_xq[l], w_xo=w_xo[l], w_xkv=w_xkv[l], w_up=w_up[l],
                      w_down=w_down[l])
        states = {}
        for group in ("in", "rest"):
            xs = [shards[name].astype(BF16) + order.astype(BF16) for name in GROUPS[group]]
            if l == 0 and group == "in":
                xs.append(conv_bits)
            states[group], token = _exchange_start(xs, False, "gather_%s_start_%d" % (group, l))
            order = order + token[0, 0]
        gathered.append(_GatheredWeights(states, l))
    gathered[0].need("in", tables["rope"][0])
    conv_all = lax.bitcast_convert_type(gathered[0].extra[0].reshape(N_DEV, CONV_WORDS, 2), F32)
    n_sc = depth * 3 * (GROUP // N_DEV)
    sconv_full = conv_all[:, :n_sc].reshape(N_DEV, depth, 3, GROUP // N_DEV).transpose(1, 2, 0, 3).reshape(
        depth, 3, GROUP)
    ffconv_full = conv_all[:, n_sc:n_sc + depth * 3 * FF_SHARD].reshape(N_DEV, depth, 3, FF_SHARD).transpose(
        1, 0, 2, 3)

    smalls = []
    for l in range(depth):
        smalls.append(dict(
            g_mix=g_mix[l], g_xa=g_xa[l], g_mem=g_mem[l], g_ffn=g_ffn[l], pool_scale=pool_scale[l],
            w_pool_bd=_block_diag(w_pool[l]), w_sconv=sconv_full[l], w_ffconv=ffconv_full[l],
            b_forget_pad=jnp.concatenate([b_forget[l], jnp.zeros((128 - N_HEADS,), F32)]).reshape(1, 128)))
    smalls[0]["g_mix"] = smalls[0]["g_mix"] + order

    saved = []
    for l in range(depth):
        h, sv = _layer_fwd(h, memv, gathered[l], smalls[l], tables)
        saved.append(sv)
    loss_part, dh, dg_final = _loss_head(h, g_final, loss_target[0])
    loss = lax.psum(loss_part[0, 0], MESH_AXES)

    small_grads = [None] * depth
    scatters = {}

    def pieces_of(big, group):
        return [big[name].reshape(PIECE_SHAPES[name]) for name in GROUPS[group]]

    for l in reversed(range(depth)):
        def rest_ready(big, l=l):
            scatters[l, "rest"], token = _exchange_start(pieces_of(big, "rest"), True, "scatter_rest_start_%d" % l)
            return token[0, 0]

        dh, big, small_grads[l] = _layer_bwd(dh, memv, gathered[l], smalls[l], tables, saved[l], rest_ready)
        xs = pieces_of(big, "in")
        if l == 0:
            flat = [small_grads[ll][n].reshape(-1) for n in SMALL_NAMES for ll in range(depth)]
            flat = jnp.concatenate(flat + [dg_final.reshape(-1)])
            flat = jnp.concatenate([flat, jnp.zeros((SMALL_WORDS - flat.shape[0],), F32)])
            xs.append(jnp.broadcast_to(flat.reshape(1, -1, 1024), (N_DEV, SMALL_WORDS // 1024, 1024)))
        scatters[l, "in"], token = _exchange_start(xs, True, "scatter_in_start_%d" % l)
        if l > 0:
            smalls[l - 1]["w_ffconv"] = smalls[l - 1]["w_ffconv"] + token[0, 0]
    grad_x = dh[None]

    parts = {}

    def wait_group(group, after):
        extra = None
        for l in reversed(range(depth)):
            got = _exchange_wait(scatters[l, group], after, "scatter_%s_wait_%d" % (group, l))
            for name, g in zip(GROUPS[group], got):
                parts.setdefault(name, [None] * depth)[l] = g
            extra = got[len(GROUPS[group]):]
        return extra

    results = {}

    def update(name, w3, m3, v3):
        outs = _adamw(parts[name], w3, m3, v3, "adamw_" + name)
        results[name] = [o.reshape(weights[name].shape) for o in outs]

    wait_group("rest", grad_x)
    for name in GROUPS["rest"]:
        update(name, weights[name], m_in[name], v_in[name])
    small_all = wait_group("in", results["w_down"][1])[0].reshape(N_DEV, -1)
    outs = _adamw(parts["w_in"], w_in_r, _relay_in_cols(m_w_in), _relay_in_cols(v_w_in), "adamw_w_in")
    results["w_in"] = [_unrelay_in_cols(o) for o in outs]
    off = 0
    for name in SMALL_NAMES + ("g_final",):
        wv = weights[name]
        full_shape = {"w_sconv": (depth, 3, GROUP), "w_ffconv": (depth, N_DEV, 3, FF_SHARD)}.get(name, wv.shape)
        n = 1
        for dim in full_shape:
            n *= dim
        p = small_all[:, off:off + n].reshape((N_DEV,) + tuple(full_shape))
        off += n
        if name == "w_sconv":
            p = lax.dynamic_slice_in_dim(p, me * (GROUP // N_DEV), GROUP // N_DEV, axis=3)
        elif name == "w_ffconv":
            p = lax.dynamic_index_in_dim(p, me, axis=2, keepdims=False)
        shape3 = (1, 1, wv.shape[0]) if wv.ndim == 1 else (1, -1, wv.shape[-1])
        w3 = wv.reshape(shape3)
        parts[name] = [p.reshape((N_DEV,) + w3.shape[1:])]
        update(name, w3, m_in[name].reshape(shape3), v_in[name].reshape(shape3))

    return (loss, grad_x, *[results[n][0] for n in WEIGHT_NAMES], *[results[n][1] for n in WEIGHT_NAMES],
            *[results[n][2] for n in WEIGHT_NAMES], *[results[n][3] for n in WEIGHT_NAMES])
```

```python
import functools

import jax
import jax.numpy as jnp
from jax import lax
from jax.experimental import pallas as pl
from jax.experimental.pallas import tpu as pltpu

F32 = jnp.float32
BF16 = jnp.bfloat16

N_DEV = 8
D_MODEL = 1024
GROUP = 256
HEAD_DIM = 64
N_HEADS = 4
N_IN = 2564
N_IN_PAD = 2688
COL_GATE = 2560
XA_HEADS = 4
XA_DIM = 256
MEM_LEN = 256
D_FF = 2816
FF_SHARD = 704
FF_HALF = 4
ROPE_THETA = 500000.0
ROPE_DIM = 16
RMS_EPS = 1e-6
NEG = -1e30
POOL_WINDOWS = (2, 4, 8, 16)
ADAM_LR, ADAM_B1, ADAM_B2, ADAM_EPS, ADAM_WD, ADAM_STEP = 0.001, 0.9, 0.999, 1e-08, 0.01, 10

ROW_TILE = 512
ATT_TQ = 256
ATT_TK = 256
VMEM_LIMIT = 56 * 1024 * 1024
ADAMW_BLOCK_BYTES = 4 * 1024 * 1024
PLACE_BLOCK_BYTES = 4 * 1024 * 1024

MESH_AXES = ("x", "y", "c")


def _params(**kw):
    return pltpu.CompilerParams(vmem_limit_bytes=VMEM_LIMIT, **kw)


HBM_SPEC = pl.BlockSpec(memory_space=pltpu.HBM)
SEM_SPEC = pl.BlockSpec(memory_space=pltpu.SEMAPHORE)
DATAFLOW = pltpu.SideEffectType.DATAFLOW_SIDE_EFFECTING


def _peer_copies(x_ref, land_ref, send_sems, recv_sems, scatter):
    mx, my, mc = lax.axis_index("x"), lax.axis_index("y"), lax.axis_index("c")
    me = 4 * mx + 2 * my + mc
    pairs = []
    for k in range(1, N_DEV):
        kx, ky, kc = (k >> 2) & 1, (k >> 1) & 1, k & 1
        peer_lin = me ^ k
        send = pltpu.make_async_remote_copy(
            src_ref=x_ref.at[peer_lin] if scatter else land_ref.at[me], dst_ref=land_ref.at[me],
            send_sem=send_sems.at[k - 1], recv_sem=recv_sems.at[k - 1],
            device_id=(mx ^ kx, my ^ ky, mc ^ kc), device_id_type=pl.DeviceIdType.MESH)
        arrival = pltpu.make_async_remote_copy(
            src_ref=land_ref.at[peer_lin], dst_ref=land_ref.at[peer_lin],
            send_sem=send_sems.at[k - 1], recv_sem=recv_sems.at[k - 1],
            device_id=(mx, my, mc), device_id_type=pl.DeviceIdType.MESH)
        pairs.append((send, arrival))
    return pairs


def _exchange_start(xs, scatter, name):
    n = len(xs)
    ns = n if scatter else 0

    def body(*refs):
        srcs = refs[:ns] if scatter else (None,) * n
        lands, sends, recvs = refs[ns:ns + n], refs[ns + n:ns + 2 * n], refs[ns + 2 * n:ns + 3 * n]
        for t in range(n):
            for send, _ in _peer_copies(srcs[t], lands[t], sends[t], recvs[t], scatter):
                send.start()
        token = refs[-1]
        token[...] = jnp.zeros_like(token)

    sems = pltpu.SemaphoreType.DMA((N_DEV - 1,))
    operands = [pltpu.with_memory_space_constraint(x, pltpu.HBM) for x in xs]
    if scatter:
        operands += [pltpu.with_memory_space_constraint(lax.empty(x.shape, x.dtype), pltpu.HBM) for x in xs]
    outs = pl.pallas_call(
        body, name=name,
        out_shape=(sems,) * (2 * n) + tuple(pltpu.HBM(a.shape, a.dtype) for a in operands)
        + (jax.ShapeDtypeStruct((8, 128), F32),),
        in_specs=(HBM_SPEC,) * (ns + n),
        out_specs=(SEM_SPEC,) * (2 * n) + (HBM_SPEC,) * (ns + n) + (pl.BlockSpec(memory_space=pltpu.VMEM),),
        input_output_aliases={i: 2 * n + i for i in range(ns + n)},
        compiler_params=pltpu.CompilerParams(has_side_effects=DATAFLOW),
    )(*operands)
    return (outs[:-1], scatter), outs[-1]


def _exchange_wait(state, after, name):
    held, scatter = state
    n = len(held) // (4 if scatter else 3)
    ns = n if scatter else 0
    sems, thru = held[:2 * n], held[2 * n:]

    def body(*refs):
        srcs = refs[:ns] if scatter else (None,) * n
        lands, sends, recvs = refs[ns:ns + n], refs[ns + n:ns + 2 * n], refs[ns + 2 * n:ns + 3 * n]
        for t in range(n):
            for send, arrival in _peer_copies(srcs[t], lands[t], sends[t], recvs[t], scatter):
                send.wait_send()
                arrival.wait_recv()

    outs = pl.pallas_call(
        body, name=name,
        out_shape=tuple(pltpu.HBM(a.shape, a.dtype) for a in thru),
        in_specs=(HBM_SPEC,) * (ns + n) + (SEM_SPEC,) * (2 * n) + (pl.BlockSpec(memory_space=pl.ANY),),
        out_specs=(HBM_SPEC,) * (ns + n), input_output_aliases={i: i for i in range(ns + n)},
        compiler_params=pltpu.CompilerParams(has_side_effects=DATAFLOW),
    )(*thru, *sems, after)
    return list(outs[ns:]), list(outs[:ns])


def _place_shard(x, me, dtype, name):
    r, c = x.shape
    tr = r
    if r * c * 4 > PLACE_BLOCK_BYTES:
        for cand in (512, 256, 128, 64, 32, 16):
            if r % cand == 0 and cand * c * 4 <= PLACE_BLOCK_BYTES:
                tr = cand
                break

    def body(me_ref, x_ref, o_ref):
        o_ref[...] = x_ref[...].astype(o_ref.dtype)

    return pl.pallas_call(
        body, name=name, out_shape=jax.ShapeDtypeStruct((N_DEV, r, c), dtype),
        grid_spec=pltpu.PrefetchScalarGridSpec(
            num_scalar_prefetch=1, grid=(r // tr,),
            in_specs=[pl.BlockSpec((tr, c), lambda i, me_ref: (i, 0))],
            out_specs=pl.BlockSpec((None, tr, c), lambda i, me_ref: (me_ref[0], i, 0))),
        compiler_params=_params(),
    )(me.reshape(1), x)


NN = ((1,), (0,))
NT = ((1,), (1,))
TN = ((0,), (0,))


def _matmul(a, b, out_shape, *, grid, a_spec, b_spec, o_spec, dims, nred, name, res=None, res_spec=None,
            out_dtype=F32):
    has_res = res is not None

    def body(*refs):
        a_ref, b_ref = refs[0], refs[1]
        r_ref = refs[2] if has_res else None
        o_ref = refs[3] if has_res else refs[2]
        part = lax.dot_general(a_ref[...].astype(BF16), b_ref[...].astype(BF16), (dims, ((), ())),
                               preferred_element_type=F32)
        if nred == 1:
            if has_res:
                part = part + r_ref[...]
            o_ref[...] = part.astype(o_ref.dtype)
        else:
            acc = refs[-1]
            r = pl.program_id(2)

            @pl.when(r == 0)
            def _():
                acc[...] = part

            @pl.when(r > 0)
            def _():
                acc[...] += part

            @pl.when(r == nred - 1)
            def _():
                tot = acc[...]
                if has_res:
                    tot = tot + r_ref[...]
                o_ref[...] = tot.astype(o_ref.dtype)

    in_specs = [a_spec, b_spec] + ([res_spec] if has_res else [])
    args = (a, b) + ((res,) if has_res else ())
    acc_shape = tuple(d for d in o_spec.block_shape if d is not None)
    return pl.pallas_call(
        body, name=name, grid=grid, out_shape=jax.ShapeDtypeStruct(out_shape, out_dtype),
        in_specs=in_specs, out_specs=o_spec,
        scratch_shapes=[pltpu.VMEM(acc_shape, F32)] if nred > 1 else [],
        compiler_params=_params(),
    )(*args)


def _mm_nn(a, w, name, res=None, tn=None):
    m, k = a.shape
    n = w.shape[1]
    tn = tn or n
    tm = min(ROW_TILE, m)
    ospec = pl.BlockSpec((tm, tn), lambda i, j, r: (i, j))
    return _matmul(a, w, (m, n), grid=(m // tm, n // tn, 1),
                   a_spec=pl.BlockSpec((tm, k), lambda i, j, r: (i, 0)),
                   b_spec=pl.BlockSpec((k, tn), lambda i, j, r: (0, j)),
                   o_spec=ospec, dims=NN, nred=1, name=name, res=res, res_spec=ospec if res is not None else None)


def _mm_nt(a, w, name, out_dtype=F32):
    m, n = a.shape
    k = w.shape[0]
    tm = min(ROW_TILE, m)
    return _matmul(a, w, (m, k), grid=(m // tm, 1, 1),
                   a_spec=pl.BlockSpec((tm, n), lambda i, j, r: (i, 0)),
                   b_spec=pl.BlockSpec((k, n), lambda i, j, r: (0, 0)),
                   o_spec=pl.BlockSpec((tm, k), lambda i, j, r: (i, 0)), dims=NT, nred=1, name=name,
                   out_dtype=out_dtype)


GRAD_DTYPE = BF16


def _mm_tn(a, b, name, tk=512, tn=None, ts=1024):
    s, k = a.shape
    n = b.shape[1]
    tn = tn or n
    tk = min(tk, k)
    ts = min(ts, s)
    return _matmul(a, b, (k, n), grid=(k // tk, n // tn, s // ts),
                   a_spec=pl.BlockSpec((ts, tk), lambda i, j, r: (r, i)),
                   b_spec=pl.BlockSpec((ts, tn), lambda i, j, r: (r, j)),
                   o_spec=pl.BlockSpec((tk, tn), lambda i, j, r: (i, j)), dims=TN, nred=s // ts, name=name,
                   out_dtype=GRAD_DTYPE)


def _rms_fwd(h, g, name):
    s, d = h.shape
    tm = min(ROW_TILE, s)

    def body(h_ref, g_ref, o_ref):
        hv = h_ref[...]
        r = lax.rsqrt(jnp.mean(hv * hv, axis=-1, keepdims=True) + RMS_EPS)
        o_ref[...] = (hv * r * g_ref[...]).astype(o_ref.dtype)

    return pl.pallas_call(
        body, name=name, grid=(s // tm,), out_shape=jax.ShapeDtypeStruct((s, d), BF16),
        in_specs=[pl.BlockSpec((tm, d), lambda i: (i, 0)), pl.BlockSpec((1, d), lambda i: (0, 0))],
        out_specs=pl.BlockSpec((tm, d), lambda i: (i, 0)), compiler_params=_params(),
    )(h, g.reshape(1, d))


def _rms_bwd(dy, h, g, res, name):
    s, d = h.shape
    tm = min(ROW_TILE, s)
    has_res = res is not None

    def body(*refs):
        dy_ref, h_ref, g_ref = refs[:3]
        r_ref = refs[3] if has_res else None
        dh_ref, dg_ref = refs[-2], refs[-1]
        hv = h_ref[...]
        r = lax.rsqrt(jnp.mean(hv * hv, axis=-1, keepdims=True) + RMS_EPS)
        hn = hv * r
        dyv = dy_ref[...].astype(F32)
        u = dyv * g_ref[...]
        dh = r * (u - hn * jnp.mean(u * hn, axis=-1, keepdims=True))
        if has_res:
            dh = dh + r_ref[...]
        dh_ref[...] = dh
        part = jnp.sum(dyv * hn, axis=0, keepdims=True)

        @pl.when(pl.program_id(0) == 0)
        def _():
            dg_ref[...] = part

        @pl.when(pl.program_id(0) > 0)
        def _():
            dg_ref[...] += part

    row = pl.BlockSpec((tm, d), lambda i: (i, 0))
    vec = pl.BlockSpec((1, d), lambda i: (0, 0))
    dh, dg = pl.pallas_call(
        body, name=name, grid=(s // tm,),
        out_shape=(jax.ShapeDtypeStruct((s, d), F32), jax.ShapeDtypeStruct((1, d), F32)),
        in_specs=[row, row, vec] + ([row] if has_res else []),
        out_specs=(row, vec), compiler_params=_params(),
    )(*((dy, h, g.reshape(1, d)) + ((res,) if has_res else ())))
    return dh, dg.reshape(d)


def _loss_head(h, g, target):
    s, d = h.shape
    tm = min(ROW_TILE, s)

    def body(h_ref, g_ref, t_ref, loss_ref, dh_ref, dg_ref):
        hv = h_ref[...]
        r = lax.rsqrt(jnp.mean(hv * hv, axis=-1, keepdims=True) + RMS_EPS)
        hn = hv * r
        gv = g_ref[...]
        err = hn * gv - t_ref[...]
        rows = jnp.mean(err * err, axis=-1, keepdims=True)
        lpart = 0.5 * jnp.sum(rows, axis=0, keepdims=True) + jnp.zeros((1, 128), F32)
        dy = err * (1.0 / d)
        u = dy * gv
        dh_ref[...] = r * (u - hn * jnp.mean(u * hn, axis=-1, keepdims=True))
        gpart = jnp.sum(dy * hn, axis=0, keepdims=True)

        @pl.when(pl.program_id(0) == 0)
        def _():
            dg_ref[...] = gpart
            loss_ref[...] = lpart

        @pl.when(pl.program_id(0) > 0)
        def _():
            dg_ref[...] += gpart
            loss_ref[...] += lpart

    row = pl.BlockSpec((tm, d), lambda i: (i, 0))
    vec = pl.BlockSpec((1, d), lambda i: (0, 0))
    return pl.pallas_call(
        body, name="loss_head", grid=(s // tm,),
        out_shape=(jax.ShapeDtypeStruct((1, 128), F32), jax.ShapeDtypeStruct((s, d), F32),
                   jax.ShapeDtypeStruct((1, d), F32)),
        in_specs=[row, vec, row],
        out_specs=(pl.BlockSpec((1, 128), lambda i: (0, 0)), row, vec), compiler_params=_params(),
    )(h, g.reshape(1, d), target)


def _shift_down(x, k):
    return pltpu.roll(x, k, 0)


def _shift_up(x, k):
    return pltpu.roll(x, x.shape[0] - k, 0)


def _conv3(x, w):
    return w[2:3, :] * x + w[1:2, :] * _shift_down(x, 1) + w[0:1, :] * _shift_down(x, 2)


def _conv3_t(x, w):
    return w[2:3, :] * x + w[1:2, :] * _shift_up(x, 1) + w[0:1, :] * _shift_up(x, 2)


def _sigmoid(x):
    return 1.0 / (1.0 + jnp.exp(-x))


def _prev_map(tile, halo, col):
    return lambda i: (jnp.maximum(i * (tile // halo) - 1, 0), col)


def _next_map(tile, halo, col, nrows):
    return lambda i: (jnp.minimum((i + 1) * (tile // halo), nrows // halo - 1), col)


def _sconv_fwd(proj, w):
    s = proj.shape[0]
    t = min(ROW_TILE, s)

    def body(cur_ref, prev_ref, w_ref, o_ref):
        i = pl.program_id(0)
        prev = prev_ref[...] * (i > 0).astype(F32)
        ext = jnp.concatenate([prev, cur_ref[...]], axis=0)
        sv = ext[:, 2 * GROUP:3 * GROUP] * ext[:, 0:GROUP]
        y = ext[:, GROUP:2 * GROUP] * _conv3(sv, w_ref[...])
        o_ref[...] = y[8:].astype(o_ref.dtype)

    return pl.pallas_call(
        body, name="sconv_fwd", grid=(s // t,), out_shape=jax.ShapeDtypeStruct((s, 4 * GROUP), BF16),
        in_specs=[pl.BlockSpec((t, 3 * GROUP), lambda i: (i, 0)),
                  pl.BlockSpec((8, 3 * GROUP), _prev_map(t, 8, 0)),
                  pl.BlockSpec((3, GROUP), lambda i: (0, 0))],
        out_specs=pl.BlockSpec((t, GROUP), lambda i: (i, 0)), compiler_params=_params(),
    )(proj, proj, w)


def _sconv_bwd(proj, w, dy):
    s = proj.shape[0]
    t = min(ROW_TILE, s)
    nt = s // t

    def body(cur_ref, prev_ref, next_ref, w_ref, dy_ref, dyn_ref, dp_ref, dw_ref):
        i = pl.program_id(0)
        first = (i > 0).astype(F32)
        last = (i < nt - 1).astype(F32)
        ext = jnp.concatenate([prev_ref[...] * first, cur_ref[...], next_ref[...] * last], axis=0)
        dye = jnp.concatenate([jnp.zeros((8, GROUP), F32), dy_ref[...], dyn_ref[...] * last], axis=0)
        hv, bv, cv = ext[:, 0:GROUP], ext[:, GROUP:2 * GROUP], ext[:, 2 * GROUP:3 * GROUP]
        wv = w_ref[...]
        sv = cv * hv
        conv = _conv3(sv, wv)
        dconv = dye * bv
        ds = _conv3_t(dconv, wv)
        dp = jnp.concatenate([ds * cv, dye * conv, ds * hv], axis=1)
        dp_ref[...] = dp[8:8 + t].astype(dp_ref.dtype)
        dc = dconv[8:8 + t]
        dw = jnp.concatenate([
            jnp.sum(dc * _shift_down(sv, 2)[8:8 + t], axis=0, keepdims=True),
            jnp.sum(dc * _shift_down(sv, 1)[8:8 + t], axis=0, keepdims=True),
            jnp.sum(dc * sv[8:8 + t], axis=0, keepdims=True),
            jnp.zeros((5, GROUP), F32)], axis=0)

        @pl.when(i == 0)
        def _():
            dw_ref[...] = dw

        @pl.when(i > 0)
        def _():
            dw_ref[...] += dw

    dp, dw = pl.pallas_call(
        body, name="sconv_bwd", grid=(nt,),
        out_shape=(jax.ShapeDtypeStruct((s, N_IN_PAD), BF16), jax.ShapeDtypeStruct((8, GROUP), F32)),
        in_specs=[pl.BlockSpec((t, 3 * GROUP), lambda i: (i, 0)),
                  pl.BlockSpec((8, 3 * GROUP), _prev_map(t, 8, 0)),
                  pl.BlockSpec((8, 3 * GROUP), _next_map(t, 8, 0, s)),
                  pl.BlockSpec((3, GROUP), lambda i: (0, 0)),
                  pl.BlockSpec((t, GROUP), lambda i: (i, 0)),
                  pl.BlockSpec((8, GROUP), _next_map(t, 8, 0, s))],
        out_specs=(pl.BlockSpec((t, 3 * GROUP), lambda i: (i, 0)), pl.BlockSpec((8, GROUP), lambda i: (0, 0))),
        compiler_params=_params(),
    )(proj, proj, proj, w, dy, dy)
    return dp, dw[:3]


def _lane_window(shape):
    lane = lax.broadcasted_iota(jnp.int32, shape, 1)
    return lane, jnp.where(lane < 64, 2.0, jnp.where(lane < 128, 4.0, jnp.where(lane < 192, 8.0, 16.0)))


def _by_group(lane, s1, s2, s3, s4):
    return jnp.where(lane < 64, s1, jnp.where(lane < 128, s2, jnp.where(lane < 192, s3, s4)))


def _pool_z(ext, row0):
    s1 = ext + _shift_down(ext, 1)
    s2 = s1 + _shift_down(s1, 2)
    s3 = s2 + _shift_down(s2, 4)
    s4 = s3 + _shift_down(s3, 8)
    lane, win = _lane_window(ext.shape)
    tpos = (lax.broadcasted_iota(jnp.int32, ext.shape, 0) + (row0 - 16 + 1)).astype(F32)
    cnt = jnp.maximum(jnp.minimum(tpos, win), 1.0)
    return _by_group(lane, s1, s2, s3, s4) / cnt - ext


ANY_SPEC = pl.BlockSpec(memory_space=pl.ANY)


def _pool_fwd(proj, wbd, scale, ybuf):
    s = proj.shape[0]
    t = min(ROW_TILE, s)
    col = (COL_GATE - GROUP) // GROUP

    def body(cur_ref, prev_ref, w_ref, sc_ref, buf_ref, o_ref):
        i = pl.program_id(0)
        ext = jnp.concatenate([prev_ref[...] * (i > 0).astype(F32), cur_ref[...]], axis=0)
        z = _pool_z(ext, i * t)[16:]
        y = jnp.dot(z.astype(BF16), w_ref[...].astype(BF16), preferred_element_type=F32)
        o_ref[...] = (y * sc_ref[...]).astype(o_ref.dtype)

    return pl.pallas_call(
        body, name="pool_fwd", grid=(s // t,), out_shape=jax.ShapeDtypeStruct(ybuf.shape, ybuf.dtype),
        in_specs=[pl.BlockSpec((t, GROUP), lambda i: (i, col)),
                  pl.BlockSpec((16, GROUP), _prev_map(t, 16, col)),
                  pl.BlockSpec((GROUP, GROUP), lambda i: (0, 0)),
                  pl.BlockSpec((1, GROUP), lambda i: (0, 0)), ANY_SPEC],
        out_specs=pl.BlockSpec((t, GROUP), lambda i: (i, 3)), input_output_aliases={4: 0},
        compiler_params=_params(),
    )(proj, proj, wbd, scale.reshape(1, GROUP), ybuf)


def _pool_bwd(proj, wbd, scale, dy, dbuf):
    s = proj.shape[0]
    t = min(ROW_TILE, s)
    nt = s // t
    col = (COL_GATE - GROUP) // GROUP

    def body(cur_ref, prev_ref, w_ref, sc_ref, dy_ref, dyn_ref, buf_ref, dp_ref, dw_ref, dsc_ref):
        i = pl.program_id(0)
        ext = jnp.concatenate([prev_ref[...] * (i > 0).astype(F32), cur_ref[...]], axis=0)
        z = _pool_z(ext, i * t)[16:]
        wv = w_ref[...].astype(BF16)
        dyc = dy_ref[...]
        dye = jnp.concatenate([dyc, dyn_ref[...] * (i < nt - 1).astype(F32)], axis=0) * sc_ref[...]
        dz = lax.dot_general(dye.astype(BF16), wv, (NT, ((), ())), preferred_element_type=F32)
        lane, win = _lane_window(dz.shape)
        tpos = (lax.broadcasted_iota(jnp.int32, dz.shape, 0) + (i * t + 1)).astype(F32)
        e = dz / jnp.minimum(tpos, win)
        f1 = e + _shift_up(e, 1)
        f2 = f1 + _shift_up(f1, 2)
        f3 = f2 + _shift_up(f2, 4)
        f4 = f3 + _shift_up(f3, 8)
        dp = _by_group(lane, f1, f2, f3, f4) - dz
        dp_ref[...] = dp[:t].astype(dp_ref.dtype)
        zb = z.astype(BF16)
        y = jnp.dot(zb, wv, preferred_element_type=F32)
        dsc = jnp.sum(dyc * y, axis=0, keepdims=True)
        dw = lax.dot_general(zb, dye[:t].astype(BF16), (TN, ((), ())), preferred_element_type=F32)

        @pl.when(i == 0)
        def _():
            dw_ref[...] = dw
            dsc_ref[...] = dsc

        @pl.when(i > 0)
        def _():
            dw_ref[...] += dw
            dsc_ref[...] += dsc

    dp, dw, dsc = pl.pallas_call(
        body, name="pool_bwd", grid=(nt,),
        out_shape=(jax.ShapeDtypeStruct(dbuf.shape, dbuf.dtype), jax.ShapeDtypeStruct((GROUP, GROUP), F32),
                   jax.ShapeDtypeStruct((1, GROUP), F32)),
        in_specs=[pl.BlockSpec((t, GROUP), lambda i: (i, col)),
                  pl.BlockSpec((16, GROUP), _prev_map(t, 16, col)),
                  pl.BlockSpec((GROUP, GROUP), lambda i: (0, 0)),
                  pl.BlockSpec((1, GROUP), lambda i: (0, 0)),
                  pl.BlockSpec((t, GROUP), lambda i: (i, 3)),
                  pl.BlockSpec((16, GROUP), _next_map(t, 16, 3, s)), ANY_SPEC],
        out_specs=(pl.BlockSpec((t, GROUP), lambda i: (i, col)), pl.BlockSpec((GROUP, GROUP), lambda i: (0, 0)),
                   pl.BlockSpec((1, GROUP), lambda i: (0, 0))),
        input_output_aliases={6: 0}, compiler_params=_params(),
    )(proj, proj, wbd, scale.reshape(1, GROUP), dy, dy, dbuf)
    return dp, dw, dsc.reshape(GROUP)


def _ffn_gate_fwd(u0, w):
    s = u0.shape[1]
    t = min(ROW_TILE, s)

    def body(a_ref, ap_ref, g_ref, gp_ref, wa_ref, wg_ref, o_ref):
        first = (pl.program_id(1) > 0).astype(F32)
        a = _conv3(jnp.concatenate([ap_ref[...] * first, a_ref[...]], axis=0), wa_ref[...])[8:]
        g = _conv3(jnp.concatenate([gp_ref[...] * first, g_ref[...]], axis=0), wg_ref[...])[8:]
        o_ref[...] = (a * (g * _sigmoid(g))).astype(o_ref.dtype)

    def cur(off):
        return pl.BlockSpec((None, t, FF_SHARD), lambda j, i: (j + off, i, 0))

    def prev(off):
        return pl.BlockSpec((None, 8, FF_SHARD), lambda j, i: (j + off, jnp.maximum(i * (t // 8) - 1, 0), 0))

    def wspec(off):
        return pl.BlockSpec((None, 3, FF_SHARD), lambda j, i: (j + off, 0, 0))

    return pl.pallas_call(
        body, name="ffn_gate_fwd", grid=(FF_HALF, s // t),
        out_shape=jax.ShapeDtypeStruct((FF_HALF, s, FF_SHARD), BF16),
        in_specs=[cur(0), prev(0), cur(FF_HALF), prev(FF_HALF), wspec(0), wspec(FF_HALF)],
        out_specs=pl.BlockSpec((None, t, FF_SHARD), lambda j, i: (j, i, 0)), compiler_params=_params(),
    )(u0, u0, u0, u0, w, w)


def _ffn_gate_bwd(u0, w, dact):
    s = u0.shape[1]
    t = min(ROW_TILE, s)
    nt = s // t

    def body(c_ref, p_ref, n_ref, w_ref, d_ref, dn_ref, du_ref, dw_ref):
        i = pl.program_id(1)
        first = (i > 0).astype(F32)
        last = (i < nt - 1).astype(F32)
        dext = jnp.concatenate([jnp.zeros((8, FF_SHARD), F32), d_ref[...], dn_ref[...] * last], axis=0)
        ext = [jnp.concatenate([p_ref[n] * first, c_ref[n], n_ref[n] * last], axis=0) for n in range(2)]
        a = _conv3(ext[0], w_ref[0])
        g = _conv3(ext[1], w_ref[1])
        sg = _sigmoid(g)
        silu = g * sg
        dus = (dext * silu, dext * a * (sg + silu * (1.0 - sg)))
        for n in range(2):
            du_ref[n] = _conv3_t(dus[n], w_ref[n])[8:8 + t].astype(du_ref.dtype)
            dc = dus[n][8:8 + t]
            dw = jnp.concatenate([
                jnp.sum(dc * _shift_down(ext[n], 2)[8:8 + t], axis=0, keepdims=True),
                jnp.sum(dc * _shift_down(ext[n], 1)[8:8 + t], axis=0, keepdims=True),
                jnp.sum(dc * ext[n][8:8 + t], axis=0, keepdims=True),
                jnp.zeros((5, FF_SHARD), F32)], axis=0)

            @pl.when(i == 0)
            def _(n=n, dw=dw):
                dw_ref[n] = dw

            @pl.when(i > 0)
            def _(n=n, dw=dw):
                dw_ref[n] += dw

    def pair(rows, row_map):
        return pl.BlockSpec((2, None, rows, FF_SHARD), lambda j, i: (0, j, row_map(i), 0))

    prev_row = lambda i: jnp.maximum(i * (t // 8) - 1, 0)
    next_row = lambda i: jnp.minimum((i + 1) * (t // 8), s // 8 - 1)
    u2 = u0.reshape(2, FF_HALF, s, FF_SHARD)
    du, dw = pl.pallas_call(
        body, name="ffn_gate_bwd", grid=(FF_HALF, nt),
        out_shape=(jax.ShapeDtypeStruct((2, FF_HALF, s, FF_SHARD), BF16),
                   jax.ShapeDtypeStruct((2, FF_HALF, 8, FF_SHARD), F32)),
        in_specs=[pair(t, lambda i: i), pair(8, prev_row), pair(8, next_row), pair(3, lambda i: 0),
                  pl.BlockSpec((None, t, FF_SHARD), lambda j, i: (j, i, 0)),
                  pl.BlockSpec((None, 8, FF_SHARD), lambda j, i: (j, next_row(i), 0))],
        out_specs=(pair(t, lambda i: i), pair(8, lambda i: 0)),
        compiler_params=_params(),
    )(u2, u2, u2, w.reshape(2, FF_HALF, 3, FF_SHARD), dact, dact)
    return du.reshape(2 * FF_HALF, s, FF_SHARD), dw.reshape(2 * FF_HALF, 8, FF_SHARD)[:, :3]


def _rope_tables(positions):
    inv_freq = ROPE_THETA ** (-jnp.arange(0, ROPE_DIM, 2, dtype=F32) / ROPE_DIM)
    ang = positions.astype(F32)[:, None] * inv_freq
    cos, sin = jnp.cos(ang), jnp.sin(ang)
    s = positions.shape[0]
    half = ROPE_DIM // 2
    rest = HEAD_DIM - ROPE_DIM
    ca = jnp.concatenate([cos, cos, jnp.ones((s, rest), F32)], axis=1)
    cb = jnp.concatenate([-sin, jnp.zeros((s, HEAD_DIM - half), F32)], axis=1)
    cc = jnp.concatenate([jnp.zeros((s, half), F32), sin, jnp.zeros((s, rest), F32)], axis=1)
    return tuple(jnp.tile(tb, (1, N_HEADS)) for tb in (ca, cb, cc))


def _heads_split(proj, col, tables, name):
    s = proj.shape[0]
    t = min(ROW_TILE, s)
    rope = tables is not None

    def body(*refs):
        x_ref = refs[0]
        q_ref, k_ref, v_ref = refs[-3:]
        xv = x_ref[...]
        parts = [xv[:, 0:GROUP], xv[:, GROUP:2 * GROUP], xv[:, 2 * GROUP:3 * GROUP]]
        if rope:
            ca, cb, cc = refs[1][...], refs[2][...], refs[3][...]
            for n in range(2):
                p = parts[n]
                parts[n] = p * ca + pltpu.roll(p, GROUP - 8, 1) * cb + pltpu.roll(p, 8, 1) * cc
        parts[0] = parts[0] * (HEAD_DIM ** -0.5)
        for o_ref, p in zip((q_ref, k_ref, v_ref), parts):
            for h in range(N_HEADS):
                o_ref[h] = p[:, h * HEAD_DIM:(h + 1) * HEAD_DIM].astype(o_ref.dtype)

    tab = pl.BlockSpec((t, GROUP), lambda i: (i, 0))
    heads = pl.BlockSpec((N_HEADS, t, HEAD_DIM), lambda i: (0, i, 0))
    hshape = jax.ShapeDtypeStruct((N_HEADS, s, HEAD_DIM), BF16)
    return pl.pallas_call(
        body, name=name, grid=(s // t,), out_shape=(hshape, hshape, hshape),
        in_specs=[pl.BlockSpec((t, 3 * GROUP), lambda i: (i, col))] + ([tab, tab, tab] if rope else []),
        out_specs=(heads, heads, heads), compiler_params=_params(),
    )(*((proj,) + (tuple(tables) if rope else ())))


def _heads_merge(dq, dk, dv, tables, name, dbuf, col):
    s = dq.shape[1]
    t = min(ROW_TILE, s)
    rope = tables is not None

    def body(*refs):
        o_ref = refs[-1]
        parts = [jnp.concatenate([r[h] for h in range(N_HEADS)], axis=1) for r in refs[:3]]
        parts[0] = parts[0] * (HEAD_DIM ** -0.5)
        if rope:
            ca, cb, cc = refs[3][...], refs[4][...], refs[5][...]
            for n in range(2):
                p = parts[n]
                parts[n] = p * ca + pltpu.roll(p * cb, 8, 1) + pltpu.roll(p * cc, GROUP - 8, 1)
        o_ref[...] = jnp.concatenate(parts, axis=1).astype(o_ref.dtype)

    tab = pl.BlockSpec((t, GROUP), lambda i: (i, 0))
    heads = pl.BlockSpec((N_HEADS, t, HEAD_DIM), lambda i: (0, i, 0))
    n_in = 6 if rope else 3
    return pl.pallas_call(
        body, name=name, grid=(s // t,), out_shape=jax.ShapeDtypeStruct(dbuf.shape, dbuf.dtype),
        in_specs=[heads, heads, heads] + ([tab, tab, tab] if rope else []) + [ANY_SPEC],
        out_specs=pl.BlockSpec((t, 3 * GROUP), lambda i: (i, col)), input_output_aliases={n_in: 0},
        compiler_params=_params(),
    )(*((dq, dk, dv) + (tuple(tables) if rope else ()) + (dbuf,)))


def _log_sigmoid(x):
    return jnp.minimum(x, 0.0) - jnp.log(1.0 + jnp.exp(-jnp.abs(x)))


def _scan_rows(x, reverse):
    n = x.shape[0]
    row = lax.broadcasted_iota(jnp.int32, x.shape, 0)
    k = 1
    while k < n:
        if reverse:
            x = x + jnp.where(row < n - k, _shift_up(x, k), 0.0)
        else:
            x = x + jnp.where(row >= k, _shift_down(x, k), 0.0)
        k *= 2
    return x


def _gate_cumsum(proj, bias):
    s = proj.shape[0]
    col = COL_GATE // 128

    def body(z_ref, b_ref, c_ref):
        c_ref[...] = _scan_rows(_log_sigmoid(z_ref[...] + b_ref[...]), False)

    return pl.pallas_call(
        body, name="gate_cumsum", grid=(1,), out_shape=jax.ShapeDtypeStruct((s, 128), F32),
        in_specs=[pl.BlockSpec((s, 128), lambda i: (0, col)), pl.BlockSpec((1, 128), lambda i: (0, 0))],
        out_specs=pl.BlockSpec((s, 128), lambda i: (0, 0)), compiler_params=_params(),
    )(proj, bias)


def _gate_cumsum_bwd(proj, bias, dc, dbuf):
    s = proj.shape[0]
    col = COL_GATE // 128

    def body(z_ref, b_ref, dc_ref, buf_ref, dz_ref, db_ref):
        dlogf = _scan_rows(dc_ref[...], True)
        dz = dlogf * _sigmoid(-(z_ref[...] + b_ref[...]))
        dz_ref[...] = dz.astype(dz_ref.dtype)
        db_ref[...] = jnp.sum(dz, axis=0, keepdims=True)

    return pl.pallas_call(
        body, name="gate_cumsum_bwd", grid=(1,),
        out_shape=(jax.ShapeDtypeStruct(dbuf.shape, dbuf.dtype), jax.ShapeDtypeStruct((1, 128), F32)),
        in_specs=[pl.BlockSpec((s, 128), lambda i: (0, col)), pl.BlockSpec((1, 128), lambda i: (0, 0)),
                  pl.BlockSpec((s, 128), lambda i: (0, 0)), ANY_SPEC],
        out_specs=(pl.BlockSpec((s, 128), lambda i: (0, col)), pl.BlockSpec((1, 128), lambda i: (0, 0))),
        input_output_aliases={3: 0}, compiler_params=_params(),
    )(proj, bias, dc, dbuf)


DIL_REACH = 2048


def _pair_weight(mode, d):
    if mode == "fox":
        return jnp.where(d >= 0, 1.0, 0.0)
    w1 = jnp.where(jnp.abs(d - 64) <= 64, 1.0, 0.0)
    w2 = jnp.where((d & 3) == 0, jnp.where(jnp.abs(d - 256) <= 256, 1.0, 0.0), 0.0)
    w3 = jnp.where((d & 15) == 0, jnp.where(jnp.abs(d - 1024) <= 1024, 1.0, 0.0), 0.0)
    return w1 + w2 + w3


def _bias_tables(mode, tq, tk):
    nb = 2 if mode == "fox" else DIL_REACH // tk + 1
    n = lax.broadcasted_iota(jnp.int32, (nb, tq, tk), 0)
    r = lax.broadcasted_iota(jnp.int32, (nb, tq, tk), 1)
    c = lax.broadcasted_iota(jnp.int32, (nb, tq, tk), 2)
    w = _pair_weight(mode, n * tk + r - c)
    tab = jnp.where(w > 0.0, jnp.log(jnp.maximum(w, 1.0)), NEG)
    return tab, tab.transpose(0, 2, 1)


M_INIT = -1e29


def _first_key_chunk(mode, q0, tk):
    if mode == "fox":
        return 0
    return jnp.maximum(q0 - DIL_REACH, 0) // tk


def _attn_fwd(mode, q, k, v, tab, c_col, c_row, ybuf, col):
    s = q.shape[1]
    tq, tk = min(ATT_TQ, s), min(ATT_TK, s)
    fox = mode == "fox"
    nb = tab.shape[0]

    def body(*refs):
        q_ref, k_ref, v_ref, tab_ref = refs[:4]
        cc_ref, cr_ref = (refs[4], refs[5]) if fox else (None, None)
        y_ref, o_ref, lse_ref = refs[-3:]
        i = pl.program_id(0)
        q0 = i * tq
        lo = _first_key_chunk(mode, q0, tk)
        hi = (q0 + tq + tk - 1) // tk
        for h in range(N_HEADS):
            qv = q_ref[h]
            ccol = cc_ref[h] if fox else None

            def step(c, carry, h=h, qv=qv, ccol=ccol):
                m, l, acc = carry
                k0 = pl.multiple_of(c * tk, tk)
                kv = k_ref[h, pl.ds(k0, tk), :]
                vv = v_ref[h, pl.ds(k0, tk), :]
                sc = lax.dot_general(qv, kv, (NT, ((), ())), preferred_element_type=F32)
                sc = sc + tab_ref[jnp.minimum(i - c, nb - 1)]
                if fox:
                    sc = sc + (ccol - cr_ref[h, :, pl.ds(k0, tk)])
                m_new = jnp.maximum(m, jnp.max(sc, axis=-1, keepdims=True))
                alpha = jnp.exp(m - m_new)
                p = jnp.exp(sc - m_new)
                l = alpha * l + jnp.sum(p, axis=-1, keepdims=True)
                acc = alpha * acc + jnp.dot(p.astype(BF16), vv, preferred_element_type=F32)
                return m_new, l, acc

            m, l, acc = lax.fori_loop(
                lo, hi, step,
                (jnp.full((tq, 1), M_INIT, F32), jnp.zeros((tq, 1), F32), jnp.zeros((tq, HEAD_DIM), F32)))
            out = acc / l
            y_ref[:, h * HEAD_DIM:(h + 1) * HEAD_DIM] = out.astype(y_ref.dtype)
            o_ref[:, h * HEAD_DIM:(h + 1) * HEAD_DIM] = out
            lse_ref[h] = m + jnp.log(l)

    qspec = pl.BlockSpec((N_HEADS, tq, HEAD_DIM), lambda i: (0, i, 0))
    full = pl.BlockSpec((N_HEADS, s, HEAD_DIM), lambda i: (0, 0, 0))
    tabspec = pl.BlockSpec((nb, tq, tk), lambda i: (0, 0, 0))
    colspec = pl.BlockSpec((N_HEADS, tq, 1), lambda i: (0, i, 0))
    rowfull = pl.BlockSpec((N_HEADS, 1, s), lambda i: (0, 0, 0))
    n_in = 6 if fox else 4
    return pl.pallas_call(
        body, name="attn_fwd_" + mode, grid=(s // tq,),
        out_shape=(jax.ShapeDtypeStruct(ybuf.shape, ybuf.dtype), jax.ShapeDtypeStruct((s, GROUP), F32),
                   jax.ShapeDtypeStruct((N_HEADS, s, 1), F32)),
        in_specs=[qspec, full, full, tabspec] + ([colspec, rowfull] if fox else []) + [ANY_SPEC],
        out_specs=(pl.BlockSpec((tq, GROUP), lambda i: (i, col)), pl.BlockSpec((tq, GROUP), lambda i: (i, 0)),
                   colspec),
        input_output_aliases={n_in: 0}, compiler_params=_params(),
    )(*((q, k, v, tab) + ((c_col, c_row) if fox else ()) + (ybuf,)))


def _attn_bwd_q(mode, q, k, v, tab, o, do, col, lse, c_col, c_row):
    s = q.shape[1]
    tq, tk = min(ATT_TQ, s), min(ATT_TK, s)
    fox = mode == "fox"
    nb = tab.shape[0]

    def body(*refs):
        q_ref, k_ref, v_ref, tab_ref, o_ref, do_ref, lse_ref = refs[:7]
        cc_ref, cr_ref = (refs[7], refs[8]) if fox else (None, None)
        dq_ref, delta_ref, dcr_ref = refs[-3:]
        i = pl.program_id(0)
        q0 = i * tq
        lo = _first_key_chunk(mode, q0, tk)
        hi = (q0 + tq + tk - 1) // tk
        for h in range(N_HEADS):
            qv = q_ref[h]
            dov = do_ref[:, h * HEAD_DIM:(h + 1) * HEAD_DIM]
            delta = jnp.sum(dov * o_ref[:, h * HEAD_DIM:(h + 1) * HEAD_DIM], axis=-1, keepdims=True)
            dob = dov.astype(BF16)
            lse = lse_ref[h]
            ccol = cc_ref[h] if fox else None

            def step(c, carry, h=h, qv=qv, dob=dob, delta=delta, lse=lse, ccol=ccol):
                dq, dcr = carry
                k0 = pl.multiple_of(c * tk, tk)
                kv = k_ref[h, pl.ds(k0, tk), :]
                vv = v_ref[h, pl.ds(k0, tk), :]
                sc = lax.dot_general(qv, kv, (NT, ((), ())), preferred_element_type=F32)
                sc = sc + tab_ref[jnp.minimum(i - c, nb - 1)]
                if fox:
                    sc = sc + (ccol - cr_ref[h, :, pl.ds(k0, tk)])
                p = jnp.exp(sc - lse)
                dp = lax.dot_general(dob, vv, (NT, ((), ())), preferred_element_type=F32)
                ds = p * (dp - delta)
                dq = dq + jnp.dot(ds.astype(BF16), kv, preferred_element_type=F32)
                if fox:
                    dcr = dcr + jnp.sum(ds, axis=-1, keepdims=True)
                return dq, dcr

            dq, dcr = lax.fori_loop(lo, hi, step, (jnp.zeros((tq, HEAD_DIM), F32), jnp.zeros((tq, 1), F32)))
            dq_ref[h] = dq
            delta_ref[h] = delta
            dcr_ref[h] = dcr

    qspec = pl.BlockSpec((N_HEADS, tq, HEAD_DIM), lambda i: (0, i, 0))
    full = pl.BlockSpec((N_HEADS, s, HEAD_DIM), lambda i: (0, 0, 0))
    colspec = pl.BlockSpec((N_HEADS, tq, 1), lambda i: (0, i, 0))
    rowfull = pl.BlockSpec((N_HEADS, 1, s), lambda i: (0, 0, 0))
    row = pl.BlockSpec((tq, GROUP), lambda i: (i, 0))
    dorow = pl.BlockSpec((tq, GROUP), lambda i: (i, col))
    tabspec = pl.BlockSpec((nb, tq, tk), lambda i: (0, 0, 0))
    col_shape = jax.ShapeDtypeStruct((N_HEADS, s, 1), F32)
    return pl.pallas_call(
        body, name="attn_bwd_q_" + mode, grid=(s // tq,),
        out_shape=(jax.ShapeDtypeStruct((N_HEADS, s, HEAD_DIM), F32), col_shape, col_shape),
        in_specs=[qspec, full, full, tabspec, row, dorow, colspec] + ([colspec, rowfull] if fox else []),
        out_specs=(qspec, colspec, colspec), compiler_params=_params(),
    )(*((q, k, v, tab, o, do, lse) + ((c_col, c_row) if fox else ())))


def _attn_bwd_kv(mode, q, k, v, tab_t, do, col, lse_row, delta_row, c_col, c_row):
    s = q.shape[1]
    tq, tk = min(ATT_TQ, s), min(ATT_TK, s)
    nq = s // tq
    fox = mode == "fox"
    nb = tab_t.shape[0]

    def body(*refs):
        q_ref, k_ref, v_ref, tab_ref, do_ref, lse_ref, delta_ref = refs[:7]
        cc_ref, cr_ref = (refs[7], refs[8]) if fox else (None, None)
        dk_ref, dv_ref, dcc_ref = refs[-3:]
        i = pl.program_id(0)
        k0 = i * tk
        lo = k0 // tq
        hi = nq if fox else jnp.minimum((k0 + tk - 1 + DIL_REACH) // tq + 1, nq)
        for h in range(N_HEADS):
            kv = k_ref[h]
            vv = v_ref[h]
            ccol = cc_ref[h] if fox else None

            def step(c, carry, h=h, kv=kv, vv=vv, ccol=ccol):
                dk, dv, dcc = carry
                q0 = pl.multiple_of(c * tq, tq)
                qv = q_ref[h, pl.ds(q0, tq), :]
                dob = do_ref[pl.ds(q0, tq), h * HEAD_DIM:(h + 1) * HEAD_DIM].astype(BF16)
                sc = lax.dot_general(kv, qv, (NT, ((), ())), preferred_element_type=F32)
                sc = sc + tab_ref[jnp.minimum(c - i, nb - 1)]
                if fox:
                    sc = sc + (cr_ref[h, :, pl.ds(q0, tq)] - ccol)
                p = jnp.exp(sc - lse_ref[h, :, pl.ds(q0, tq)])
                dp = lax.dot_general(vv, dob, (NT, ((), ())), preferred_element_type=F32)
                ds = p * (dp - delta_ref[h, :, pl.ds(q0, tq)])
                dv = dv + jnp.dot(p.astype(BF16), dob, preferred_element_type=F32)
                dk = dk + jnp.dot(ds.astype(BF16), qv, preferred_element_type=F32)
                if fox:
                    dcc = dcc + jnp.sum(ds, axis=-1, keepdims=True)
                return dk, dv, dcc

            dk, dv, dcc = lax.fori_loop(
                lo, hi, step,
                (jnp.zeros((tk, HEAD_DIM), F32), jnp.zeros((tk, HEAD_DIM), F32), jnp.zeros((tk, 1), F32)))
            dk_ref[h] = dk
            dv_ref[h] = dv
            dcc_ref[h] = dcc

    kspec = pl.BlockSpec((N_HEADS, tk, HEAD_DIM), lambda i: (0, i, 0))
    full = pl.BlockSpec((N_HEADS, s, HEAD_DIM), lambda i: (0, 0, 0))
    colspec = pl.BlockSpec((N_HEADS, tk, 1), lambda i: (0, i, 0))
    rowfull = pl.BlockSpec((N_HEADS, 1, s), lambda i: (0, 0, 0))
    dofull = pl.BlockSpec((s, GROUP), lambda i: (0, col))
    tabspec = pl.BlockSpec((nb, tk, tq), lambda i: (0, 0, 0))
    hshape = jax.ShapeDtypeStruct((N_HEADS, s, HEAD_DIM), F32)
    return pl.pallas_call(
        body, name="attn_bwd_kv_" + mode, grid=(s // tk,),
        out_shape=(hshape, hshape, jax.ShapeDtypeStruct((N_HEADS, s, 1), F32)),
        in_specs=[full, kspec, kspec, tabspec, dofull, rowfull, rowfull] + ([colspec, rowfull] if fox else []),
        out_specs=(kspec, kspec, colspec), compiler_params=_params(),
    )(*((q, k, v, tab_t, do, lse_row, delta_row) + ((c_col, c_row) if fox else ())))


def _xattn_fwd(qx, kvm):
    s = qx.shape[0]
    t = min(ROW_TILE, s)

    def body(q_ref, kv_ref, o_ref):
        for h in range(XA_HEADS):
            qv = q_ref[:, h * XA_DIM:(h + 1) * XA_DIM].astype(BF16)
            kv = kv_ref[h].astype(BF16)
            vv = kv_ref[XA_HEADS + h].astype(BF16)
            sc = lax.dot_general(qv, kv, (NT, ((), ())), preferred_element_type=F32) * (XA_DIM ** -0.5)
            e = jnp.exp(sc - jnp.max(sc, axis=-1, keepdims=True))
            p = e / jnp.sum(e, axis=-1, keepdims=True)
            o_ref[:, h * XA_DIM:(h + 1) * XA_DIM] = jnp.dot(p.astype(BF16), vv,
                                                             preferred_element_type=F32).astype(o_ref.dtype)

    return pl.pallas_call(
        body, name="xattn_fwd", grid=(s // t,), out_shape=jax.ShapeDtypeStruct((s, D_MODEL), BF16),
        in_specs=[pl.BlockSpec((t, D_MODEL), lambda i: (i, 0)),
                  pl.BlockSpec((2 * XA_HEADS, MEM_LEN, XA_DIM), lambda i: (0, 0, 0))],
        out_specs=pl.BlockSpec((t, D_MODEL), lambda i: (i, 0)), compiler_params=_params(),
    )(qx, kvm)


def _xattn_bwd(qx, kvm, do):
    s = qx.shape[0]
    t = min(ROW_TILE, s)

    def body(q_ref, kv_ref, do_ref, dq_ref, dkv_ref):
        i = pl.program_id(0)
        for h in range(XA_HEADS):
            qv = q_ref[:, h * XA_DIM:(h + 1) * XA_DIM].astype(BF16)
            dov = do_ref[:, h * XA_DIM:(h + 1) * XA_DIM].astype(BF16)
            kv = kv_ref[h].astype(BF16)
            vv = kv_ref[XA_HEADS + h].astype(BF16)
            sc = lax.dot_general(qv, kv, (NT, ((), ())), preferred_element_type=F32) * (XA_DIM ** -0.5)
            e = jnp.exp(sc - jnp.max(sc, axis=-1, keepdims=True))
            p = e / jnp.sum(e, axis=-1, keepdims=True)
            dp = lax.dot_general(dov, vv, (NT, ((), ())), preferred_element_type=F32)
            ds = (p * (dp - jnp.sum(p * dp, axis=-1, keepdims=True)) * (XA_DIM ** -0.5)).astype(BF16)
            dq_ref[:, h * XA_DIM:(h + 1) * XA_DIM] = jnp.dot(ds, kv, preferred_element_type=F32).astype(dq_ref.dtype)
            dk = lax.dot_general(ds, qv, (TN, ((), ())), preferred_element_type=F32)
            dv = lax.dot_general(p.astype(BF16), dov, (TN, ((), ())), preferred_element_type=F32)

            @pl.when(i == 0)
            def _(h=h, dk=dk, dv=dv):
                dkv_ref[h] = dk
                dkv_ref[XA_HEADS + h] = dv

            @pl.when(i > 0)
            def _(h=h, dk=dk, dv=dv):
                dkv_ref[h] += dk
                dkv_ref[XA_HEADS + h] += dv

    row = pl.BlockSpec((t, D_MODEL), lambda i: (i, 0))
    kvs = pl.BlockSpec((2 * XA_HEADS, MEM_LEN, XA_DIM), lambda i: (0, 0, 0))
    return pl.pallas_call(
        body, name="xattn_bwd", grid=(s // t,),
        out_shape=(jax.ShapeDtypeStruct((s, D_MODEL), BF16),
                   jax.ShapeDtypeStruct((2 * XA_HEADS, MEM_LEN, XA_DIM), F32)),
        in_specs=[row, kvs, row], out_specs=(row, kvs), compiler_params=_params(),
    )(qx, kvm, do)


def _adamw(parts, owns, me, w, m, v, name):
    nl, r, c = w.shape
    tr = r
    for cand in (256, 128, 64, 32, 16, 8):
        if r % cand == 0 and r > cand and N_DEV * cand * c * 4 <= ADAMW_BLOCK_BYTES:
            tr = cand
            break
    nt = r // tr
    per_layer = N_DEV + (1 if owns is not None else 0)

    def body(me_ref, *refs):
        w_ref, m_ref, v_ref, g_ref, d_ref, nm_ref, nv_ref = refs[nl * per_layer:]
        layer = pl.program_id(0)
        g = None
        for l in range(nl):
            p_refs = refs[l * per_layer:(l + 1) * per_layer]
            gl = None
            for d in range(N_DEV):
                term = p_refs[d][...].astype(F32)
                if owns is not None:
                    term = jnp.where(me_ref[0] == d, p_refs[N_DEV][...].astype(F32), term)
                gl = term if gl is None else gl + term
            g = gl if g is None else jnp.where(layer == l, gl, g)
        mn = ADAM_B1 * m_ref[...] + (1.0 - ADAM_B1) * g
        vn = ADAM_B2 * v_ref[...] + (1.0 - ADAM_B2) * (g * g)
        m_hat = mn / (1.0 - ADAM_B1 ** ADAM_STEP)
        v_hat = vn / (1.0 - ADAM_B2 ** ADAM_STEP)
        g_ref[...] = g
        d_ref[...] = -ADAM_LR * (m_hat / (jnp.sqrt(v_hat) + ADAM_EPS) + ADAM_WD * w_ref[...])
        nm_ref[...] = mn
        nv_ref[...] = vn

    def rows(l, ll, i):
        return jnp.where(ll == l, i, jnp.where(ll < l, 0, nt - 1))

    def part_spec(l, d):
        if owns is None:
            return pl.BlockSpec((None, tr, c), lambda ll, i, me_ref: (d, rows(l, ll, i), 0))
        return pl.BlockSpec((None, tr, c),
                            lambda ll, i, me_ref: (jnp.where(me_ref[0] == d, (d + 1) % N_DEV, d), rows(l, ll, i), 0))

    def own_spec(l):
        return pl.BlockSpec((None, tr, c), lambda ll, i, me_ref: (me_ref[0], rows(l, ll, i), 0))

    in_specs, operands = [], []
    for l in range(nl):
        in_specs += [part_spec(l, d) for d in range(N_DEV)]
        operands += [parts[l]] * N_DEV
        if owns is not None:
            in_specs.append(own_spec(l))
            operands.append(owns[l])
    blk = pl.BlockSpec((None, tr, c), lambda ll, i, me_ref: (ll, i, 0))
    shp = jax.ShapeDtypeStruct((nl, r, c), F32)
    return pl.pallas_call(
        body, name=name, out_shape=(shp, shp, shp, shp),
        grid_spec=pltpu.PrefetchScalarGridSpec(
            num_scalar_prefetch=1, grid=(nl, nt), in_specs=in_specs + [blk, blk, blk],
            out_specs=(blk, blk, blk, blk)),
        compiler_params=_params(),
    )(me.reshape(1), *operands, w, m, v)


GROUPS = {"in": ("w_in",), "rest": ("w_out", "w_xq", "w_xo", "w_xkv", "w_up", "w_down")}
FULL_SHAPES = {"w_in": (D_MODEL, N_IN_PAD), "w_out": (D_MODEL, D_MODEL), "w_xq": (D_MODEL, D_MODEL),
               "w_xo": (D_MODEL, D_MODEL), "w_xkv": (N_DEV, D_MODEL, 2 * D_MODEL // N_DEV),
               "w_up": (N_DEV, D_MODEL, FF_SHARD), "w_down": (FF_HALF, FF_SHARD, D_MODEL)}
PIECE_SHAPES = {"w_in": (N_DEV, D_MODEL // N_DEV, N_IN_PAD), "w_out": (N_DEV, D_MODEL // N_DEV, D_MODEL),
                "w_xq": (N_DEV, D_MODEL // N_DEV, D_MODEL), "w_xo": (N_DEV, D_MODEL // N_DEV, D_MODEL),
                "w_xkv": (N_DEV, D_MODEL, 2 * D_MODEL // N_DEV), "w_up": (N_DEV, D_MODEL, FF_SHARD),
                "w_down": (N_DEV, D_FF // N_DEV, D_MODEL)}
CONV_WORDS = 8192
SMALL_WORDS = 80 * 1024


class _GatheredWeights:
    def __init__(self, states, layer):
        self.states, self.layer, self.full, self.extra = dict(states), layer, {}, None

    def need(self, group, after):
        if group in self.states:
            got, _ = _exchange_wait(self.states.pop(group), after, "gather_%s_wait_%d" % (group, self.layer))
            for name, g in zip(GROUPS[group], got):
                self.full[name] = g.reshape(FULL_SHAPES[name])
            self.extra = got[len(GROUPS[group]):]

    def __getitem__(self, name):
        return self.full[name]


def _relay_in_cols(w):
    pad = jnp.zeros(w.shape[:-1] + (N_IN_PAD - N_IN,), w.dtype)
    return jnp.concatenate([w[..., :2304], w[..., 2308:N_IN], w[..., 2304:2308], pad], axis=-1)


def _unrelay_in_cols(w):
    return jnp.concatenate([w[..., :2304], w[..., COL_GATE:COL_GATE + 4], w[..., 2304:COL_GATE]], axis=-1)


def _layer_fwd(h, memv, w, sm, tables):
    sv = {"h0": h}
    xn = _rms_fwd(h, sm["g_mix"], "rms_mix")
    w.need("in", xn)
    proj = _mm_nn(xn, w["w_in"], "mm_in", tn=896)
    sv["xn"], sv["proj"] = xn, proj
    ycat = _sconv_fwd(proj, sm["w_sconv"])
    qd, kd, vd = _heads_split(proj, 1, tables["rope"], "split_dil")
    ycat, ob, lse_b = _attn_fwd("dil", qd, kd, vd, tables["dil"][0], None, None, ycat, 1)
    sv["dil"] = (qd, kd, vd, ob, lse_b)
    qf, kf, vf = _heads_split(proj, 2, None, "split_fox")
    c = _gate_cumsum(proj, sm["b_forget_pad"])
    ct = c[:, :N_HEADS].T
    c_col, c_row = ct.reshape(N_HEADS, -1, 1), ct.reshape(N_HEADS, 1, -1)
    ycat, oc, lse_c = _attn_fwd("fox", qf, kf, vf, tables["fox"][0], c_col, c_row, ycat, 2)
    sv["fox"] = (qf, kf, vf, oc, lse_c, c_col, c_row)
    ycat = _pool_fwd(proj, sm["w_pool_bd"], sm["pool_scale"], ycat)
    sv["ycat"] = ycat
    w.need("rest", ycat)
    h1 = _mm_nn(ycat, w["w_out"], "mm_out", res=h)
    sv["h1"] = h1
    xq = _rms_fwd(h1, sm["g_xa"], "rms_xa")
    memn = _rms_fwd(memv, sm["g_mem"], "rms_mem")
    qx = _mm_nn(xq, w["w_xq"], "mm_xq")
    kvm = _matmul(memn, w["w_xkv"], (N_DEV, MEM_LEN, XA_DIM), grid=(N_DEV, 1, 1),
                  a_spec=pl.BlockSpec((MEM_LEN, D_MODEL), lambda i, j, r: (0, 0)),
                  b_spec=pl.BlockSpec((None, D_MODEL, XA_DIM), lambda i, j, r: (i, 0, 0)),
                  o_spec=pl.BlockSpec((None, MEM_LEN, XA_DIM), lambda i, j, r: (i, 0, 0)),
                  dims=NN, nred=1, name="mm_xkv")
    ox = _xattn_fwd(qx, kvm)
    sv.update(xq=xq, memn=memn, qx=qx, kvm=kvm, ox=ox)
    h2 = _mm_nn(ox, w["w_xo"], "mm_xo", res=h1)
    sv["h2"] = h2
    xf = _rms_fwd(h2, sm["g_ffn"], "rms_ffn")
    s = h.shape[0]
    tm = min(ROW_TILE, s)
    u0 = _matmul(xf, w["w_up"], (N_DEV, s, FF_SHARD), grid=(s // tm, N_DEV, 1),
                 a_spec=pl.BlockSpec((tm, D_MODEL), lambda i, j, r: (i, 0)),
                 b_spec=pl.BlockSpec((None, D_MODEL, FF_SHARD), lambda i, j, r: (j, 0, 0)),
                 o_spec=pl.BlockSpec((None, tm, FF_SHARD), lambda i, j, r: (j, i, 0)),
                 dims=NN, nred=1, name="mm_up")
    act = _ffn_gate_fwd(u0, sm["w_ffconv"])
    sv.update(xf=xf, u0=u0, act=act)
    ospec = pl.BlockSpec((tm, D_MODEL), lambda i, j, r: (i, 0))
    h3 = _matmul(act, w["w_down"], (s, D_MODEL), grid=(s // tm, 1, FF_HALF),
                 a_spec=pl.BlockSpec((None, tm, FF_SHARD), lambda i, j, r: (r, i, 0)),
                 b_spec=pl.BlockSpec((None, FF_SHARD, D_MODEL), lambda i, j, r: (r, 0, 0)),
                 o_spec=ospec, dims=NN, nred=FF_HALF, name="mm_down", res=h2, res_spec=ospec)
    return h3, sv


def _layer_bwd(dh3, memv, w, sm, tables, sv, rest_ready):
    s = dh3.shape[0]
    tm = min(ROW_TILE, s)
    ts = min(1024, s)
    big, small = {}, {}
    dact = _matmul(dh3, w["w_down"], (FF_HALF, s, FF_SHARD), grid=(s // tm, FF_HALF, 1),
                   a_spec=pl.BlockSpec((tm, D_MODEL), lambda i, j, r: (i, 0)),
                   b_spec=pl.BlockSpec((None, FF_SHARD, D_MODEL), lambda i, j, r: (j, 0, 0)),
                   o_spec=pl.BlockSpec((None, tm, FF_SHARD), lambda i, j, r: (j, i, 0)),
                   dims=NT, nred=1, name="mm_dact")
    big["w_down"] = _matmul(sv["act"], dh3, (FF_HALF, FF_SHARD, D_MODEL), grid=(FF_HALF, 1, s // ts),
                            a_spec=pl.BlockSpec((None, ts, FF_SHARD), lambda i, j, r: (i, r, 0)),
                            b_spec=pl.BlockSpec((ts, D_MODEL), lambda i, j, r: (r, 0)),
                            o_spec=pl.BlockSpec((None, FF_SHARD, D_MODEL), lambda i, j, r: (i, 0, 0)),
                            dims=TN, nred=s // ts, name="mm_dw_down", out_dtype=GRAD_DTYPE)
    du0, small["w_ffconv"] = _ffn_gate_bwd(sv["u0"], sm["w_ffconv"], dact)
    dxf = _matmul(du0, w["w_up"], (s, D_MODEL), grid=(s // tm, 1, N_DEV),
                  a_spec=pl.BlockSpec((None, tm, FF_SHARD), lambda i, j, r: (r, i, 0)),
                  b_spec=pl.BlockSpec((None, D_MODEL, FF_SHARD), lambda i, j, r: (r, 0, 0)),
                  o_spec=pl.BlockSpec((tm, D_MODEL), lambda i, j, r: (i, 0)),
                  dims=NT, nred=N_DEV, name="mm_dxf")
    big["w_up"] = _matmul(sv["xf"], du0, (N_DEV, D_MODEL, FF_SHARD), grid=(N_DEV, 1, s // ts),
                          a_spec=pl.BlockSpec((ts, D_MODEL), lambda i, j, r: (r, 0)),
                          b_spec=pl.BlockSpec((None, ts, FF_SHARD), lambda i, j, r: (i, r, 0)),
                          o_spec=pl.BlockSpec((None, D_MODEL, FF_SHARD), lambda i, j, r: (i, 0, 0)),
                          dims=TN, nred=s // ts, name="mm_dw_up", out_dtype=GRAD_DTYPE)
    dh2, small["g_ffn"] = _rms_bwd(dxf, sv["h2"], sm["g_ffn"], dh3, "rms_ffn_bwd")
    dox = _mm_nt(dh2, w["w_xo"], "mm_dox")
    big["w_xo"] = _mm_tn(sv["ox"], dh2, "mm_dw_xo")
    dqx, dkvm = _xattn_bwd(sv["qx"], sv["kvm"], dox)
    dxq = _mm_nt(dqx, w["w_xq"], "mm_dxq")
    big["w_xq"] = _mm_tn(sv["xq"], dqx, "mm_dw_xq")
    big["w_xkv"] = _matmul(sv["memn"], dkvm, (N_DEV, D_MODEL, XA_DIM), grid=(N_DEV, 1, 1),
                           a_spec=pl.BlockSpec((MEM_LEN, D_MODEL), lambda i, j, r: (0, 0)),
                           b_spec=pl.BlockSpec((None, MEM_LEN, XA_DIM), lambda i, j, r: (i, 0, 0)),
                           o_spec=pl.BlockSpec((None, D_MODEL, XA_DIM), lambda i, j, r: (i, 0, 0)),
                           dims=TN, nred=1, name="mm_dw_xkv", out_dtype=GRAD_DTYPE)
    dmemn = _matmul(dkvm, w["w_xkv"], (MEM_LEN, D_MODEL), grid=(1, 1, N_DEV),
                    a_spec=pl.BlockSpec((None, MEM_LEN, XA_DIM), lambda i, j, r: (r, 0, 0)),
                    b_spec=pl.BlockSpec((None, D_MODEL, XA_DIM), lambda i, j, r: (r, 0, 0)),
                    o_spec=pl.BlockSpec((MEM_LEN, D_MODEL), lambda i, j, r: (0, 0)),
                    dims=NT, nred=N_DEV, name="mm_dmemn")
    _, small["g_mem"] = _rms_bwd(dmemn, memv, sm["g_mem"], None, "rms_mem_bwd")
    dh1, small["g_xa"] = _rms_bwd(dxq, sv["h1"], sm["g_xa"], dh2, "rms_xa_bwd")
    dycat = _mm_nt(dh1, w["w_out"], "mm_dycat")
    big["w_out"] = _mm_tn(sv["ycat"], dh1, "mm_dw_out")
    proj = sv["proj"]
    dproj, small["w_sconv"] = _sconv_bwd(proj, sm["w_sconv"] + rest_ready(big), dycat)
    qd, kd, vd, ob, lse_b = sv["dil"]
    tab, tab_t = tables["dil"]
    dq, delta, _ = _attn_bwd_q("dil", qd, kd, vd, tab, ob, dycat, 1, lse_b, None, None)
    dk, dv, _ = _attn_bwd_kv("dil", qd, kd, vd, tab_t, dycat, 1, lse_b.reshape(N_HEADS, 1, s),
                             delta.reshape(N_HEADS, 1, s), None, None)
    dproj = _heads_merge(dq, dk, dv, tables["rope"], "merge_dil", dproj, 1)
    qf, kf, vf, oc, lse_c, c_col, c_row = sv["fox"]
    tab, tab_t = tables["fox"]
    dq, delta, dc_rows = _attn_bwd_q("fox", qf, kf, vf, tab, oc, dycat, 2, lse_c, c_col, c_row)
    dk, dv, dc_cols = _attn_bwd_kv("fox", qf, kf, vf, tab_t, dycat, 2, lse_c.reshape(N_HEADS, 1, s),
                                   delta.reshape(N_HEADS, 1, s), c_col, c_row)
    dproj = _heads_merge(dq, dk, dv, None, "merge_fox", dproj, 2)
    dc = (dc_rows - dc_cols).reshape(N_HEADS, s).T
    dc = jnp.concatenate([dc, jnp.zeros((s, 128 - N_HEADS), F32)], axis=1)
    dproj, dbias = _gate_cumsum_bwd(proj, sm["b_forget_pad"], dc, dproj)
    small["b_forget"] = dbias[0, :N_HEADS]
    dproj, dwbd, small["pool_scale"] = _pool_bwd(proj, sm["w_pool_bd"], sm["pool_scale"], dycat, dproj)
    small["w_pool"] = jnp.stack([dwbd[64 * g:64 * (g + 1), 64 * g:64 * (g + 1)] for g in range(4)])
    dxn = _mm_nt(dproj, w["w_in"], "mm_dxn")
    big["w_in"] = _mm_tn(sv["xn"], dproj, "mm_dw_in", tn=896)
    dh0, small["g_mix"] = _rms_bwd(dxn, sv["h0"], sm["g_mix"], dh1, "rms_mix_bwd")
    return dh0, big, small


SMALL_NAMES = ("g_mix", "b_forget", "w_pool", "pool_scale", "g_xa", "g_mem", "g_ffn", "w_sconv", "w_ffconv")
WEIGHT_NAMES = ("g_mix", "w_in", "b_forget", "w_sconv", "w_pool", "pool_scale", "w_out", "g_xa", "g_mem", "w_xq",
                "w_xkv", "w_xo", "g_ffn", "w_up", "w_ffconv", "w_down", "g_final")


def _block_diag(w_pool):
    z = jnp.zeros((64, 64), F32)
    return jnp.concatenate(
        [jnp.concatenate([w_pool[g] if c == g else z for c in range(4)], axis=1) for g in range(4)], axis=0)


def kernel(x, mem, positions, g_mix, w_in, b_forget, w_sconv, w_pool, pool_scale, w_out, g_xa, g_mem, w_xq, w_xkv, w_xo, g_ffn, w_up, w_ffconv, w_down, g_final, loss_target, m_g_mix, m_w_in, m_b_forget, m_w_sconv, m_w_pool, m_pool_scale, m_w_out, m_g_xa, m_g_mem, m_w_xq, m_w_xkv, m_w_xo, m_g_ffn, m_w_up, m_w_ffconv, m_w_down, m_g_final, v_g_mix, v_w_in, v_b_forget, v_w_sconv, v_w_pool, v_pool_scale, v_w_out, v_g_xa, v_g_mem, v_w_xq, v_w_xkv, v_w_xo, v_g_ffn, v_w_up, v_w_ffconv, v_w_down, v_g_final):
    weights = dict(g_mix=g_mix, w_in=w_in, b_forget=b_forget, w_sconv=w_sconv, w_pool=w_pool, pool_scale=pool_scale,
                   w_out=w_out, g_xa=g_xa, g_mem=g_mem, w_xq=w_xq, w_xkv=w_xkv, w_xo=w_xo, g_ffn=g_ffn, w_up=w_up,
                   w_ffconv=w_ffconv, w_down=w_down, g_final=g_final)
    m_in = dict(g_mix=m_g_mix, w_in=m_w_in, b_forget=m_b_forget, w_sconv=m_w_sconv, w_pool=m_w_pool,
                pool_scale=m_pool_scale, w_out=m_w_out, g_xa=m_g_xa, g_mem=m_g_mem, w_xq=m_w_xq, w_xkv=m_w_xkv,
                w_xo=m_w_xo, g_ffn=m_g_ffn, w_up=m_w_up, w_ffconv=m_w_ffconv, w_down=m_w_down, g_final=m_g_final)
    v_in = dict(g_mix=v_g_mix, w_in=v_w_in, b_forget=v_b_forget, w_sconv=v_w_sconv, w_pool=v_w_pool,
                pool_scale=v_pool_scale, w_out=v_w_out, g_xa=v_g_xa, g_mem=v_g_mem, w_xq=v_w_xq, w_xkv=v_w_xkv,
                w_xo=v_w_xo, g_ffn=v_g_ffn, w_up=v_w_up, w_ffconv=v_w_ffconv, w_down=v_w_down, g_final=v_g_final)
    depth = w_in.shape[0]
    me = 4 * lax.axis_index("x") + 2 * lax.axis_index("y") + lax.axis_index("c")
    h = x[0]
    memv = mem[0]
    s = h.shape[0]
    tq = min(ATT_TQ, s)
    tables = {"rope": _rope_tables(positions[0]), "dil": _bias_tables("dil", tq, tq),
              "fox": _bias_tables("fox", tq, tq)}

    w_in_r = _relay_in_cols(w_in)
    conv_shard = jnp.concatenate([w_sconv.reshape(-1), w_ffconv.reshape(-1)])
    conv_shard = jnp.concatenate([conv_shard, jnp.zeros((CONV_WORDS - conv_shard.shape[0],), F32)])
    conv_bits = lax.bitcast_convert_type(conv_shard, BF16).reshape(2 * CONV_WORDS // 1024, 1024)
    gathered = []
    order = jnp.zeros((), F32)
    for l in range(depth):
        shards = dict(w_in=w_in_r[l], w_out=w_out[l], w_xq=w_xq[l], w_xo=w_xo[l], w_xkv=w_xkv[l], w_up=w_up[l],
                      w_down=w_down[l])
        states = {}
        for group in ("in", "rest"):
            shards[GROUPS[group][0]] = shards[GROUPS[group][0]] + order
            xs = [_place_shard(shards[name], me, BF16, "place_%s_%d" % (name, l)) for name in GROUPS[group]]
            if l == 0 and group == "in":
                xs.append(_place_shard(conv_bits, me, BF16, "place_conv"))
            states[group], token = _exchange_start(xs, False, "gather_%s_start_%d" % (group, l))
            order = order + token[0, 0]
        gathered.append(_GatheredWeights(states, l))
    gathered[0].need("in", tables["rope"][0])
    conv_all = lax.bitcast_convert_type(gathered[0].extra[0].reshape(N_DEV, CONV_WORDS, 2), F32)
    n_sc = depth * 3 * (GROUP // N_DEV)
    sconv_full = conv_all[:, :n_sc].reshape(N_DEV, depth, 3, GROUP // N_DEV).transpose(1, 2, 0, 3).reshape(
        depth, 3, GROUP)
    ffconv_full = conv_all[:, n_sc:n_sc + depth * 3 * FF_SHARD].reshape(N_DEV, depth, 3, FF_SHARD).transpose(
        1, 0, 2, 3)

    smalls = []
    for l in range(depth):
        smalls.append(dict(
            g_mix=g_mix[l], g_xa=g_xa[l], g_mem=g_mem[l], g_ffn=g_ffn[l], pool_scale=pool_scale[l],
            w_pool_bd=_block_diag(w_pool[l]), w_sconv=sconv_full[l], w_ffconv=ffconv_full[l],
            b_forget_pad=jnp.concatenate([b_forget[l], jnp.zeros((128 - N_HEADS,), F32)]).reshape(1, 128)))
    smalls[0]["g_mix"] = smalls[0]["g_mix"] + order

    saved = []
    for l in range(depth):
        h, sv = _layer_fwd(h, memv, gathered[l], smalls[l], tables)
        saved.append(sv)
    loss_part, dh, dg_final = _loss_head(h, g_final, loss_target[0])
    loss = lax.psum(loss_part[0, 0], MESH_AXES)

    small_grads = [None] * depth
    scatters = {}

    def pieces_of(big, group):
        return [big[name].reshape(PIECE_SHAPES[name]) for name in GROUPS[group]]

    for l in reversed(range(depth)):
        def rest_ready(big, l=l):
            scatters[l, "rest"], token = _exchange_start(pieces_of(big, "rest"), True, "scatter_rest_start_%d" % l)
            return token[0, 0]

        dh, big, small_grads[l] = _layer_bwd(dh, memv, gathered[l], smalls[l], tables, saved[l], rest_ready)
        xs = pieces_of(big, "in")
        if l == 0:
            flat = [small_grads[ll][n].reshape(-1) for n in SMALL_NAMES for ll in range(depth)]
            flat = jnp.concatenate(flat + [dg_final.reshape(-1)])
            flat = jnp.concatenate([flat, jnp.zeros((SMALL_WORDS - flat.shape[0],), F32)])
            xs.append(jnp.broadcast_to(flat.reshape(1, -1, 1024), (N_DEV, SMALL_WORDS // 1024, 1024)))
        scatters[l, "in"], token = _exchange_start(xs, True, "scatter_in_start_%d" % l)
        if l > 0:
            smalls[l - 1]["w_ffconv"] = smalls[l - 1]["w_ffconv"] + token[0, 0]
    grad_x = dh[None]

    parts, owns = {}, {}

    def wait_group(group, after):
        extra = None
        for l in reversed(range(depth)):
            got, given = _exchange_wait(scatters[l, group], after, "scatter_%s_wait_%d" % (group, l))
            for name, g, x in zip(GROUPS[group], got, given):
                parts.setdefault(name, [None] * depth)[l] = g
                owns.setdefault(name, [None] * depth)[l] = x
            extra = (got[len(GROUPS[group]):], given[len(GROUPS[group]):])
        return extra

    results = {}

    def update(name, w3, m3, v3):
        outs = _adamw(parts[name], owns.get(name), me, w3, m3, v3, "adamw_" + name)
        results[name] = [o.reshape(weights[name].shape) for o in outs]

    wait_group("rest", grad_x)
    for name in GROUPS["rest"]:
        update(name, weights[name], m_in[name], v_in[name])
    small_got, small_given = wait_group("in", results["w_down"][1])
    small_all = lax.dynamic_update_slice_in_dim(small_got[0], small_given[0][:1], me, axis=0).reshape(N_DEV, -1)
    outs = _adamw(parts["w_in"], owns["w_in"], me, w_in_r, _relay_in_cols(m_w_in), _relay_in_cols(v_w_in),
                  "adamw_w_in")
    results["w_in"] = [_unrelay_in_cols(o) for o in outs]
    off = 0
    for name in SMALL_NAMES + ("g_final",):
        wv = weights[name]
        full_shape = {"w_sconv": (depth, 3, GROUP), "w_ffconv": (depth, N_DEV, 3, FF_SHARD)}.get(name, wv.shape)
        n = 1
        for dim in full_shape:
            n *= dim
        p = small_all[:, off:off + n].reshape((N_DEV,) + tuple(full_shape))
        off += n
        if name == "w_sconv":
            p = lax.dynamic_slice_in_dim(p, me * (GROUP // N_DEV), GROUP // N_DEV, axis=3)
        elif name == "w_ffconv":
            p = lax.dynamic_index_in_dim(p, me, axis=2, keepdims=False)
        shape3 = (1, 1, wv.shape[0]) if wv.ndim == 1 else (1, -1, wv.shape[-1])
        w3 = wv.reshape(shape3)
        parts[name] = [p.reshape((N_DEV,) + w3.shape[1:])]
        update(name, w3, m_in[name].reshape(shape3), v_in[name].reshape(shape3))

    return (loss, grad_x, *[results[n][0] for n in WEIGHT_NAMES], *[results[n][1] for n in WEIGHT_NAMES],
            *[results[n][2] for n in WEIGHT_NAMES], *[results[n][3] for n in WEIGHT_NAMES])
```

```python
import functools

import jax
import jax.numpy as jnp
from jax import lax
from jax.experimental import pallas as pl
from jax.experimental.pallas import tpu as pltpu

F32 = jnp.float32
BF16 = jnp.bfloat16

N_DEV = 8
D_MODEL = 1024
GROUP = 256
HEAD_DIM = 64
N_HEADS = 4
N_IN = 2564
N_IN_PAD = 2688
COL_GATE = 2560
XA_HEADS = 4
XA_DIM = 256
MEM_LEN = 256
D_FF = 2816
FF_SHARD = 704
FF_HALF = 4
ROPE_THETA = 500000.0
ROPE_DIM = 16
RMS_EPS = 1e-6
NEG = -1e30
POOL_WINDOWS = (2, 4, 8, 16)
ADAM_LR, ADAM_B1, ADAM_B2, ADAM_EPS, ADAM_WD, ADAM_STEP = 0.001, 0.9, 0.999, 1e-08, 0.01, 10

ROW_TILE = 512
ATT_TQ = 256
VMEM_LIMIT = 56 * 1024 * 1024
ADAMW_BLOCK_BYTES = 4 * 1024 * 1024
PLACE_BLOCK_BYTES = 4 * 1024 * 1024

MESH_AXES = ("x", "y", "c")


def _params(**kw):
    return pltpu.CompilerParams(vmem_limit_bytes=VMEM_LIMIT, **kw)


HBM_SPEC = pl.BlockSpec(memory_space=pltpu.HBM)
SEM_SPEC = pl.BlockSpec(memory_space=pltpu.SEMAPHORE)
DATAFLOW = pltpu.SideEffectType.DATAFLOW_SIDE_EFFECTING


def _peer_copies(x_ref, land_ref, send_sems, recv_sems, scatter):
    mx, my, mc = lax.axis_index("x"), lax.axis_index("y"), lax.axis_index("c")
    me = 4 * mx + 2 * my + mc
    pairs = []
    for k in range(1, N_DEV):
        kx, ky, kc = (k >> 2) & 1, (k >> 1) & 1, k & 1
        peer_lin = me ^ k
        send = pltpu.make_async_remote_copy(
            src_ref=x_ref.at[peer_lin] if scatter else land_ref.at[me], dst_ref=land_ref.at[me],
            send_sem=send_sems.at[k - 1], recv_sem=recv_sems.at[k - 1],
            device_id=(mx ^ kx, my ^ ky, mc ^ kc), device_id_type=pl.DeviceIdType.MESH)
        arrival = pltpu.make_async_remote_copy(
            src_ref=land_ref.at[peer_lin], dst_ref=land_ref.at[peer_lin],
            send_sem=send_sems.at[k - 1], recv_sem=recv_sems.at[k - 1],
            device_id=(mx, my, mc), device_id_type=pl.DeviceIdType.MESH)
        pairs.append((send, arrival))
    return pairs


def _exchange_start(xs, scatter, name):
    n = len(xs)
    ns = n if scatter else 0

    def body(*refs):
        srcs = refs[:ns] if scatter else (None,) * n
        lands, sends, recvs = refs[ns:ns + n], refs[ns + n:ns + 2 * n], refs[ns + 2 * n:ns + 3 * n]
        for t in range(n):
            for send, _ in _peer_copies(srcs[t], lands[t], sends[t], recvs[t], scatter):
                send.start()
        token = refs[-1]
        token[...] = jnp.zeros_like(token)

    sems = pltpu.SemaphoreType.DMA((N_DEV - 1,))
    operands = [pltpu.with_memory_space_constraint(x, pltpu.HBM) for x in xs]
    if scatter:
        operands += [pltpu.with_memory_space_constraint(lax.empty(x.shape, x.dtype), pltpu.HBM) for x in xs]
    outs = pl.pallas_call(
        body, name=name,
        out_shape=(sems,) * (2 * n) + tuple(pltpu.HBM(a.shape, a.dtype) for a in operands)
        + (jax.ShapeDtypeStruct((8, 128), F32),),
        in_specs=(HBM_SPEC,) * (ns + n),
        out_specs=(SEM_SPEC,) * (2 * n) + (HBM_SPEC,) * (ns + n) + (pl.BlockSpec(memory_space=pltpu.VMEM),),
        input_output_aliases={i: 2 * n + i for i in range(ns + n)},
        compiler_params=pltpu.CompilerParams(has_side_effects=DATAFLOW),
    )(*operands)
    return (outs[:-1], scatter), outs[-1]


def _exchange_wait(state, after, name):
    held, scatter = state
    n = len(held) // (4 if scatter else 3)
    ns = n if scatter else 0
    sems, thru = held[:2 * n], held[2 * n:]

    def body(*refs):
        srcs = refs[:ns] if scatter else (None,) * n
        lands, sends, recvs = refs[ns:ns + n], refs[ns + n:ns + 2 * n], refs[ns + 2 * n:ns + 3 * n]
        for t in range(n):
            for send, arrival in _peer_copies(srcs[t], lands[t], sends[t], recvs[t], scatter):
                send.wait_send()
                arrival.wait_recv()

    outs = pl.pallas_call(
        body, name=name,
        out_shape=tuple(pltpu.HBM(a.shape, a.dtype) for a in thru),
        in_specs=(HBM_SPEC,) * (ns + n) + (SEM_SPEC,) * (2 * n) + (pl.BlockSpec(memory_space=pl.ANY),),
        out_specs=(HBM_SPEC,) * (ns + n), input_output_aliases={i: i for i in range(ns + n)},
        compiler_params=pltpu.CompilerParams(has_side_effects=DATAFLOW),
    )(*thru, *sems, after)
    return list(outs[ns:]), list(outs[:ns])


def _place_shard(x, me, dtype, name):
    r, c = x.shape
    tr = r
    if r * c * 4 > PLACE_BLOCK_BYTES:
        for cand in (512, 256, 128, 64, 32, 16):
            if r % cand == 0 and cand * c * 4 <= PLACE_BLOCK_BYTES:
                tr = cand
                break

    def body(me_ref, x_ref, o_ref):
        o_ref[...] = x_ref[...].astype(o_ref.dtype)

    return pl.pallas_call(
        body, name=name, out_shape=jax.ShapeDtypeStruct((N_DEV, r, c), dtype),
        grid_spec=pltpu.PrefetchScalarGridSpec(
            num_scalar_prefetch=1, grid=(r // tr,),
            in_specs=[pl.BlockSpec((tr, c), lambda i, me_ref: (i, 0))],
            out_specs=pl.BlockSpec((None, tr, c), lambda i, me_ref: (me_ref[0], i, 0))),
        compiler_params=_params(),
    )(me.reshape(1), x)


NN = ((1,), (0,))
NT = ((1,), (1,))
TN = ((0,), (0,))


def _matmul(a, b, out_shape, *, grid, a_spec, b_spec, o_spec, dims, nred, name, res=None, res_spec=None,
            out_dtype=F32):
    has_res = res is not None

    def body(*refs):
        a_ref, b_ref = refs[0], refs[1]
        r_ref = refs[2] if has_res else None
        o_ref = refs[3] if has_res else refs[2]
        part = lax.dot_general(a_ref[...].astype(BF16), b_ref[...].astype(BF16), (dims, ((), ())),
                               preferred_element_type=F32)
        if nred == 1:
            if has_res:
                part = part + r_ref[...]
            o_ref[...] = part.astype(o_ref.dtype)
        else:
            acc = refs[-1]
            r = pl.program_id(2)

            @pl.when(r == 0)
            def _():
                acc[...] = part

            @pl.when(r > 0)
            def _():
                acc[...] += part

            @pl.when(r == nred - 1)
            def _():
                tot = acc[...]
                if has_res:
                    tot = tot + r_ref[...]
                o_ref[...] = tot.astype(o_ref.dtype)

    in_specs = [a_spec, b_spec] + ([res_spec] if has_res else [])
    args = (a, b) + ((res,) if has_res else ())
    acc_shape = tuple(d for d in o_spec.block_shape if d is not None)
    return pl.pallas_call(
        body, name=name, grid=grid, out_shape=jax.ShapeDtypeStruct(out_shape, out_dtype),
        in_specs=in_specs, out_specs=o_spec,
        scratch_shapes=[pltpu.VMEM(acc_shape, F32)] if nred > 1 else [],
        compiler_params=_params(),
    )(*args)


def _mm_nn(a, w, name, res=None, tn=None):
    m, k = a.shape
    n = w.shape[1]
    tn = tn or n
    tm = min(ROW_TILE, m)
    ospec = pl.BlockSpec((tm, tn), lambda i, j, r: (i, j))
    return _matmul(a, w, (m, n), grid=(m // tm, n // tn, 1),
                   a_spec=pl.BlockSpec((tm, k), lambda i, j, r: (i, 0)),
                   b_spec=pl.BlockSpec((k, tn), lambda i, j, r: (0, j)),
                   o_spec=ospec, dims=NN, nred=1, name=name, res=res, res_spec=ospec if res is not None else None)


def _mm_nt(a, w, name, out_dtype=F32):
    m, n = a.shape
    k = w.shape[0]
    tm = min(ROW_TILE, m)
    return _matmul(a, w, (m, k), grid=(m // tm, 1, 1),
                   a_spec=pl.BlockSpec((tm, n), lambda i, j, r: (i, 0)),
                   b_spec=pl.BlockSpec((k, n), lambda i, j, r: (0, 0)),
                   o_spec=pl.BlockSpec((tm, k), lambda i, j, r: (i, 0)), dims=NT, nred=1, name=name,
                   out_dtype=out_dtype)


GRAD_DTYPE = BF16


def _mm_tn(a, b, name, tk=512, tn=None, ts=1024):
    s, k = a.shape
    n = b.shape[1]
    tn = tn or n
    tk = min(tk, k)
    ts = min(ts, s)
    return _matmul(a, b, (k, n), grid=(k // tk, n // tn, s // ts),
                   a_spec=pl.BlockSpec((ts, tk), lambda i, j, r: (r, i)),
                   b_spec=pl.BlockSpec((ts, tn), lambda i, j, r: (r, j)),
                   o_spec=pl.BlockSpec((tk, tn), lambda i, j, r: (i, j)), dims=TN, nred=s // ts, name=name,
                   out_dtype=GRAD_DTYPE)


def _rms_fwd(h, g, name):
    s, d = h.shape
    tm = min(ROW_TILE, s)

    def body(h_ref, g_ref, o_ref):
        hv = h_ref[...]
        r = lax.rsqrt(jnp.mean(hv * hv, axis=-1, keepdims=True) + RMS_EPS)
        o_ref[...] = (hv * r * g_ref[...]).astype(o_ref.dtype)

    return pl.pallas_call(
        body, name=name, grid=(s // tm,), out_shape=jax.ShapeDtypeStruct((s, d), BF16),
        in_specs=[pl.BlockSpec((tm, d), lambda i: (i, 0)), pl.BlockSpec((1, d), lambda i: (0, 0))],
        out_specs=pl.BlockSpec((tm, d), lambda i: (i, 0)), compiler_params=_params(),
    )(h, g.reshape(1, d))


def _rms_bwd(dy, h, g, res, name):
    s, d = h.shape
    tm = min(ROW_TILE, s)
    has_res = res is not None

    def body(*refs):
        dy_ref, h_ref, g_ref = refs[:3]
        r_ref = refs[3] if has_res else None
        dh_ref, dg_ref = refs[-2], refs[-1]
        hv = h_ref[...]
        r = lax.rsqrt(jnp.mean(hv * hv, axis=-1, keepdims=True) + RMS_EPS)
        hn = hv * r
        dyv = dy_ref[...].astype(F32)
        u = dyv * g_ref[...]
        dh = r * (u - hn * jnp.mean(u * hn, axis=-1, keepdims=True))
        if has_res:
            dh = dh + r_ref[...]
        dh_ref[...] = dh
        part = jnp.sum(dyv * hn, axis=0, keepdims=True)

        @pl.when(pl.program_id(0) == 0)
        def _():
            dg_ref[...] = part

        @pl.when(pl.program_id(0) > 0)
        def _():
            dg_ref[...] += part

    row = pl.BlockSpec((tm, d), lambda i: (i, 0))
    vec = pl.BlockSpec((1, d), lambda i: (0, 0))
    dh, dg = pl.pallas_call(
        body, name=name, grid=(s // tm,),
        out_shape=(jax.ShapeDtypeStruct((s, d), F32), jax.ShapeDtypeStruct((1, d), F32)),
        in_specs=[row, row, vec] + ([row] if has_res else []),
        out_specs=(row, vec), compiler_params=_params(),
    )(*((dy, h, g.reshape(1, d)) + ((res,) if has_res else ())))
    return dh, dg.reshape(d)


def _loss_head(h, g, target):
    s, d = h.shape
    tm = min(ROW_TILE, s)

    def body(h_ref, g_ref, t_ref, loss_ref, dh_ref, dg_ref):
        hv = h_ref[...]
        r = lax.rsqrt(jnp.mean(hv * hv, axis=-1, keepdims=True) + RMS_EPS)
        hn = hv * r
        gv = g_ref[...]
        err = hn * gv - t_ref[...]
        rows = jnp.mean(err * err, axis=-1, keepdims=True)
        lpart = 0.5 * jnp.sum(rows, axis=0, keepdims=True) + jnp.zeros((1, 128), F32)
        dy = err * (1.0 / d)
        u = dy * gv
        dh_ref[...] = r * (u - hn * jnp.mean(u * hn, axis=-1, keepdims=True))
        gpart = jnp.sum(dy * hn, axis=0, keepdims=True)

        @pl.when(pl.program_id(0) == 0)
        def _():
            dg_ref[...] = gpart
            loss_ref[...] = lpart

        @pl.when(pl.program_id(0) > 0)
        def _():
            dg_ref[...] += gpart
            loss_ref[...] += lpart

    row = pl.BlockSpec((tm, d), lambda i: (i, 0))
    vec = pl.BlockSpec((1, d), lambda i: (0, 0))
    return pl.pallas_call(
        body, name="loss_head", grid=(s // tm,),
        out_shape=(jax.ShapeDtypeStruct((1, 128), F32), jax.ShapeDtypeStruct((s, d), F32),
                   jax.ShapeDtypeStruct((1, d), F32)),
        in_specs=[row, vec, row],
        out_specs=(pl.BlockSpec((1, 128), lambda i: (0, 0)), row, vec), compiler_params=_params(),
    )(h, g.reshape(1, d), target)


def _shift_down(x, k):
    return pltpu.roll(x, k, 0)


def _shift_up(x, k):
    return pltpu.roll(x, x.shape[0] - k, 0)


def _conv3(x, w):
    return w[2:3, :] * x + w[1:2, :] * _shift_down(x, 1) + w[0:1, :] * _shift_down(x, 2)


def _conv3_t(x, w):
    return w[2:3, :] * x + w[1:2, :] * _shift_up(x, 1) + w[0:1, :] * _shift_up(x, 2)


def _sigmoid(x):
    return 1.0 / (1.0 + jnp.exp(-x))


def _prev_map(tile, halo, col):
    return lambda i: (jnp.maximum(i * (tile // halo) - 1, 0), col)


def _next_map(tile, halo, col, nrows):
    return lambda i: (jnp.minimum((i + 1) * (tile // halo), nrows // halo - 1), col)


def _sconv_fwd(proj, w):
    s = proj.shape[0]
    t = min(ROW_TILE, s)

    def body(cur_ref, prev_ref, w_ref, o_ref):
        i = pl.program_id(0)
        prev = prev_ref[...] * (i > 0).astype(F32)
        ext = jnp.concatenate([prev, cur_ref[...]], axis=0)
        sv = ext[:, 2 * GROUP:3 * GROUP] * ext[:, 0:GROUP]
        y = ext[:, GROUP:2 * GROUP] * _conv3(sv, w_ref[...])
        o_ref[...] = y[8:].astype(o_ref.dtype)

    return pl.pallas_call(
        body, name="sconv_fwd", grid=(s // t,), out_shape=jax.ShapeDtypeStruct((s, 4 * GROUP), BF16),
        in_specs=[pl.BlockSpec((t, 3 * GROUP), lambda i: (i, 0)),
                  pl.BlockSpec((8, 3 * GROUP), _prev_map(t, 8, 0)),
                  pl.BlockSpec((3, GROUP), lambda i: (0, 0))],
        out_specs=pl.BlockSpec((t, GROUP), lambda i: (i, 0)), compiler_params=_params(),
    )(proj, proj, w)


def _sconv_bwd(proj, w, dy):
    s = proj.shape[0]
    t = min(ROW_TILE, s)
    nt = s // t

    def body(cur_ref, prev_ref, next_ref, w_ref, dy_ref, dyn_ref, dp_ref, dw_ref):
        i = pl.program_id(0)
        first = (i > 0).astype(F32)
        last = (i < nt - 1).astype(F32)
        ext = jnp.concatenate([prev_ref[...] * first, cur_ref[...], next_ref[...] * last], axis=0)
        dye = jnp.concatenate([jnp.zeros((8, GROUP), F32), dy_ref[...], dyn_ref[...] * last], axis=0)
        hv, bv, cv = ext[:, 0:GROUP], ext[:, GROUP:2 * GROUP], ext[:, 2 * GROUP:3 * GROUP]
        wv = w_ref[...]
        sv = cv * hv
        conv = _conv3(sv, wv)
        dconv = dye * bv
        ds = _conv3_t(dconv, wv)
        dp = jnp.concatenate([ds * cv, dye * conv, ds * hv], axis=1)
        dp_ref[...] = dp[8:8 + t].astype(dp_ref.dtype)
        dc = dconv[8:8 + t]
        dw = jnp.concatenate([
            jnp.sum(dc * _shift_down(sv, 2)[8:8 + t], axis=0, keepdims=True),
            jnp.sum(dc * _shift_down(sv, 1)[8:8 + t], axis=0, keepdims=True),
            jnp.sum(dc * sv[8:8 + t], axis=0, keepdims=True),
            jnp.zeros((5, GROUP), F32)], axis=0)

        @pl.when(i == 0)
        def _():
            dw_ref[...] = dw

        @pl.when(i > 0)
        def _():
            dw_ref[...] += dw

    dp, dw = pl.pallas_call(
        body, name="sconv_bwd", grid=(nt,),
        out_shape=(jax.ShapeDtypeStruct((s, N_IN_PAD), BF16), jax.ShapeDtypeStruct((8, GROUP), F32)),
        in_specs=[pl.BlockSpec((t, 3 * GROUP), lambda i: (i, 0)),
                  pl.BlockSpec((8, 3 * GROUP), _prev_map(t, 8, 0)),
                  pl.BlockSpec((8, 3 * GROUP), _next_map(t, 8, 0, s)),
                  pl.BlockSpec((3, GROUP), lambda i: (0, 0)),
                  pl.BlockSpec((t, GROUP), lambda i: (i, 0)),
                  pl.BlockSpec((8, GROUP), _next_map(t, 8, 0, s))],
        out_specs=(pl.BlockSpec((t, 3 * GROUP), lambda i: (i, 0)), pl.BlockSpec((8, GROUP), lambda i: (0, 0))),
        compiler_params=_params(),
    )(proj, proj, proj, w, dy, dy)
    return dp, dw[:3]


def _lane_window(shape):
    lane = lax.broadcasted_iota(jnp.int32, shape, 1)
    return lane, jnp.where(lane < 64, 2.0, jnp.where(lane < 128, 4.0, jnp.where(lane < 192, 8.0, 16.0)))


def _by_group(lane, s1, s2, s3, s4):
    return jnp.where(lane < 64, s1, jnp.where(lane < 128, s2, jnp.where(lane < 192, s3, s4)))


def _pool_z(ext, row0):
    s1 = ext + _shift_down(ext, 1)
    s2 = s1 + _shift_down(s1, 2)
    s3 = s2 + _shift_down(s2, 4)
    s4 = s3 + _shift_down(s3, 8)
    lane, win = _lane_window(ext.shape)
    tpos = (lax.broadcasted_iota(jnp.int32, ext.shape, 0) + (row0 - 16 + 1)).astype(F32)
    cnt = jnp.maximum(jnp.minimum(tpos, win), 1.0)
    return _by_group(lane, s1, s2, s3, s4) / cnt - ext


ANY_SPEC = pl.BlockSpec(memory_space=pl.ANY)


def _pool_fwd(proj, wbd, scale, ybuf):
    s = proj.shape[0]
    t = min(ROW_TILE, s)
    col = (COL_GATE - GROUP) // GROUP

    def body(cur_ref, prev_ref, w_ref, sc_ref, buf_ref, o_ref):
        i = pl.program_id(0)
        ext = jnp.concatenate([prev_ref[...] * (i > 0).astype(F32), cur_ref[...]], axis=0)
        z = _pool_z(ext, i * t)[16:]
        y = jnp.dot(z.astype(BF16), w_ref[...].astype(BF16), preferred_element_type=F32)
        o_ref[...] = (y * sc_ref[...]).astype(o_ref.dtype)

    return pl.pallas_call(
        body, name="pool_fwd", grid=(s // t,), out_shape=jax.ShapeDtypeStruct(ybuf.shape, ybuf.dtype),
        in_specs=[pl.BlockSpec((t, GROUP), lambda i: (i, col)),
                  pl.BlockSpec((16, GROUP), _prev_map(t, 16, col)),
                  pl.BlockSpec((GROUP, GROUP), lambda i: (0, 0)),
                  pl.BlockSpec((1, GROUP), lambda i: (0, 0)), ANY_SPEC],
        out_specs=pl.BlockSpec((t, GROUP), lambda i: (i, 3)), input_output_aliases={4: 0},
        compiler_params=_params(),
    )(proj, proj, wbd, scale.reshape(1, GROUP), ybuf)


def _pool_bwd(proj, wbd, scale, dy, dbuf):
    s = proj.shape[0]
    t = min(ROW_TILE, s)
    nt = s // t
    col = (COL_GATE - GROUP) // GROUP

    def body(cur_ref, prev_ref, w_ref, sc_ref, dy_ref, dyn_ref, buf_ref, dp_ref, dw_ref, dsc_ref):
        i = pl.program_id(0)
        ext = jnp.concatenate([prev_ref[...] * (i > 0).astype(F32), cur_ref[...]], axis=0)
        z = _pool_z(ext, i * t)[16:]
        wv = w_ref[...].astype(BF16)
        dyc = dy_ref[...]
        dye = jnp.concatenate([dyc, dyn_ref[...] * (i < nt - 1).astype(F32)], axis=0) * sc_ref[...]
        dz = lax.dot_general(dye.astype(BF16), wv, (NT, ((), ())), preferred_element_type=F32)
        lane, win = _lane_window(dz.shape)
        tpos = (lax.broadcasted_iota(jnp.int32, dz.shape, 0) + (i * t + 1)).astype(F32)
        e = dz / jnp.minimum(tpos, win)
        f1 = e + _shift_up(e, 1)
        f2 = f1 + _shift_up(f1, 2)
        f3 = f2 + _shift_up(f2, 4)
        f4 = f3 + _shift_up(f3, 8)
        dp = _by_group(lane, f1, f2, f3, f4) - dz
        dp_ref[...] = dp[:t].astype(dp_ref.dtype)
        zb = z.astype(BF16)
        y = jnp.dot(zb, wv, preferred_element_type=F32)
        dsc = jnp.sum(dyc * y, axis=0, keepdims=True)
        dw = lax.dot_general(zb, dye[:t].astype(BF16), (TN, ((), ())), preferred_element_type=F32)

        @pl.when(i == 0)
        def _():
            dw_ref[...] = dw
            dsc_ref[...] = dsc

        @pl.when(i > 0)
        def _():
            dw_ref[...] += dw
            dsc_ref[...] += dsc

    dp, dw, dsc = pl.pallas_call(
        body, name="pool_bwd", grid=(nt,),
        out_shape=(jax.ShapeDtypeStruct(dbuf.shape, dbuf.dtype), jax.ShapeDtypeStruct((GROUP, GROUP), F32),
                   jax.ShapeDtypeStruct((1, GROUP), F32)),
        in_specs=[pl.BlockSpec((t, GROUP), lambda i: (i, col)),
                  pl.BlockSpec((16, GROUP), _prev_map(t, 16, col)),
                  pl.BlockSpec((GROUP, GROUP), lambda i: (0, 0)),
                  pl.BlockSpec((1, GROUP), lambda i: (0, 0)),
                  pl.BlockSpec((t, GROUP), lambda i: (i, 3)),
                  pl.BlockSpec((16, GROUP), _next_map(t, 16, 3, s)), ANY_SPEC],
        out_specs=(pl.BlockSpec((t, GROUP), lambda i: (i, col)), pl.BlockSpec((GROUP, GROUP), lambda i: (0, 0)),
                   pl.BlockSpec((1, GROUP), lambda i: (0, 0))),
        input_output_aliases={6: 0}, compiler_params=_params(),
    )(proj, proj, wbd, scale.reshape(1, GROUP), dy, dy, dbuf)
    return dp, dw, dsc.reshape(GROUP)


def _ffn_gate_fwd(u0, w):
    s = u0.shape[1]
    t = min(ROW_TILE, s)

    def body(a_ref, ap_ref, g_ref, gp_ref, wa_ref, wg_ref, o_ref):
        first = (pl.program_id(1) > 0).astype(F32)
        a = _conv3(jnp.concatenate([ap_ref[...] * first, a_ref[...]], axis=0), wa_ref[...])[8:]
        g = _conv3(jnp.concatenate([gp_ref[...] * first, g_ref[...]], axis=0), wg_ref[...])[8:]
        o_ref[...] = (a * (g * _sigmoid(g))).astype(o_ref.dtype)

    def cur(off):
        return pl.BlockSpec((None, t, FF_SHARD), lambda j, i: (j + off, i, 0))

    def prev(off):
        return pl.BlockSpec((None, 8, FF_SHARD), lambda j, i: (j + off, jnp.maximum(i * (t // 8) - 1, 0), 0))

    def wspec(off):
        return pl.BlockSpec((None, 3, FF_SHARD), lambda j, i: (j + off, 0, 0))

    return pl.pallas_call(
        body, name="ffn_gate_fwd", grid=(FF_HALF, s // t),
        out_shape=jax.ShapeDtypeStruct((FF_HALF, s, FF_SHARD), BF16),
        in_specs=[cur(0), prev(0), cur(FF_HALF), prev(FF_HALF), wspec(0), wspec(FF_HALF)],
        out_specs=pl.BlockSpec((None, t, FF_SHARD), lambda j, i: (j, i, 0)), compiler_params=_params(),
    )(u0, u0, u0, u0, w, w)


def _ffn_gate_bwd(u0, w, dact):
    s = u0.shape[1]
    t = min(ROW_TILE, s)
    nt = s // t

    def body(c_ref, p_ref, n_ref, w_ref, d_ref, dn_ref, du_ref, dw_ref):
        i = pl.program_id(1)
        first = (i > 0).astype(F32)
        last = (i < nt - 1).astype(F32)
        dext = jnp.concatenate([jnp.zeros((8, FF_SHARD), F32), d_ref[...], dn_ref[...] * last], axis=0)
        ext = [jnp.concatenate([p_ref[n] * first, c_ref[n], n_ref[n] * last], axis=0) for n in range(2)]
        a = _conv3(ext[0], w_ref[0])
        g = _conv3(ext[1], w_ref[1])
        sg = _sigmoid(g)
        silu = g * sg
        dus = (dext * silu, dext * a * (sg + silu * (1.0 - sg)))
        for n in range(2):
            du_ref[n] = _conv3_t(dus[n], w_ref[n])[8:8 + t].astype(du_ref.dtype)
            dc = dus[n][8:8 + t]
            dw = jnp.concatenate([
                jnp.sum(dc * _shift_down(ext[n], 2)[8:8 + t], axis=0, keepdims=True),
                jnp.sum(dc * _shift_down(ext[n], 1)[8:8 + t], axis=0, keepdims=True),
                jnp.sum(dc * ext[n][8:8 + t], axis=0, keepdims=True),
                jnp.zeros((5, FF_SHARD), F32)], axis=0)

            @pl.when(i == 0)
            def _(n=n, dw=dw):
                dw_ref[n] = dw

            @pl.when(i > 0)
            def _(n=n, dw=dw):
                dw_ref[n] += dw

    def pair(rows, row_map):
        return pl.BlockSpec((2, None, rows, FF_SHARD), lambda j, i: (0, j, row_map(i), 0))

    prev_row = lambda i: jnp.maximum(i * (t // 8) - 1, 0)
    next_row = lambda i: jnp.minimum((i + 1) * (t // 8), s // 8 - 1)
    u2 = u0.reshape(2, FF_HALF, s, FF_SHARD)
    du, dw = pl.pallas_call(
        body, name="ffn_gate_bwd", grid=(FF_HALF, nt),
        out_shape=(jax.ShapeDtypeStruct((2, FF_HALF, s, FF_SHARD), BF16),
                   jax.ShapeDtypeStruct((2, FF_HALF, 8, FF_SHARD), F32)),
        in_specs=[pair(t, lambda i: i), pair(8, prev_row), pair(8, next_row), pair(3, lambda i: 0),
                  pl.BlockSpec((None, t, FF_SHARD), lambda j, i: (j, i, 0)),
                  pl.BlockSpec((None, 8, FF_SHARD), lambda j, i: (j, next_row(i), 0))],
        out_specs=(pair(t, lambda i: i), pair(8, lambda i: 0)),
        compiler_params=_params(),
    )(u2, u2, u2, w.reshape(2, FF_HALF, 3, FF_SHARD), dact, dact)
    return du.reshape(2 * FF_HALF, s, FF_SHARD), dw.reshape(2 * FF_HALF, 8, FF_SHARD)[:, :3]


def _rope_tables(positions):
    inv_freq = ROPE_THETA ** (-jnp.arange(0, ROPE_DIM, 2, dtype=F32) / ROPE_DIM)
    ang = positions.astype(F32)[:, None] * inv_freq
    cos, sin = jnp.cos(ang), jnp.sin(ang)
    s = positions.shape[0]
    half = ROPE_DIM // 2
    rest = HEAD_DIM - ROPE_DIM
    ca = jnp.concatenate([cos, cos, jnp.ones((s, rest), F32)], axis=1)
    cb = jnp.concatenate([-sin, jnp.zeros((s, HEAD_DIM - half), F32)], axis=1)
    cc = jnp.concatenate([jnp.zeros((s, half), F32), sin, jnp.zeros((s, rest), F32)], axis=1)
    return tuple(jnp.tile(tb, (1, N_HEADS)) for tb in (ca, cb, cc))


QK_WIDE = 128
LANE_CQ, LANE_CK = 64, 67
KT_ROWS = 80


def _three_bf16(x):
    hi = x.astype(BF16).astype(F32)
    mid = (x - hi).astype(BF16).astype(F32)
    lo = (x - hi - mid).astype(BF16).astype(F32)
    return hi, mid, lo


def _heads_split(proj, col, tables, c, name):
    s = proj.shape[0]
    t = min(ROW_TILE, s)
    rope = tables is not None
    wide = c is not None
    width = QK_WIDE if wide else HEAD_DIM

    def body(*refs):
        x_ref = refs[0]
        q_ref, k_ref, v_ref, kt_ref, vt_ref = refs[-5:]
        xv = x_ref[...]
        parts = [xv[:, 0:GROUP], xv[:, GROUP:2 * GROUP], xv[:, 2 * GROUP:3 * GROUP]]
        if rope:
            ca, cb, cc = refs[1][...], refs[2][...], refs[3][...]
            for n in range(2):
                p = parts[n]
                parts[n] = p * ca + pltpu.roll(p, GROUP - 8, 1) * cb + pltpu.roll(p, 8, 1) * cc
        parts[0] = parts[0] * (HEAD_DIM ** -0.5)
        k_t, v_t = parts[1].T, parts[2].T
        ones_row = jnp.where(lax.broadcasted_iota(jnp.int32, (KT_ROWS - HEAD_DIM, t), 0) == 0, 1.0, 0.0)
        lane = lax.broadcasted_iota(jnp.int32, (t, QK_WIDE), 1)
        zeros = jnp.zeros((t, QK_WIDE - HEAD_DIM), F32)
        for h in range(N_HEADS):
            hs = slice(h * HEAD_DIM, (h + 1) * HEAD_DIM)
            qh, kh = parts[0][:, hs], parts[1][:, hs]
            if wide:
                terms = _three_bf16(refs[-6][:, h:h + 1])
                qh = jnp.concatenate([qh, zeros], axis=1)
                kh = jnp.concatenate([kh, zeros], axis=1)
                for n in range(3):
                    qh = jnp.where(lane == LANE_CQ + n, terms[n], jnp.where(lane == LANE_CK + n, 1.0, qh))
                    kh = jnp.where(lane == LANE_CK + n, -terms[n], jnp.where(lane == LANE_CQ + n, 1.0, kh))
            q_ref[h] = qh.astype(q_ref.dtype)
            k_ref[h] = kh.astype(k_ref.dtype)
            v_ref[h] = parts[2][:, hs].astype(v_ref.dtype)
            kt_ref[h] = jnp.concatenate([k_t[hs, :], ones_row], axis=0).astype(kt_ref.dtype)
            vt_ref[h] = v_t[hs, :].astype(vt_ref.dtype)

    tab = pl.BlockSpec((t, GROUP), lambda i: (i, 0))
    qk = pl.BlockSpec((N_HEADS, t, width), lambda i: (0, i, 0))
    heads = pl.BlockSpec((N_HEADS, t, HEAD_DIM), lambda i: (0, i, 0))
    heads_t = pl.BlockSpec((N_HEADS, HEAD_DIM, t), lambda i: (0, 0, i))
    qk_shape = jax.ShapeDtypeStruct((N_HEADS, s, width), BF16)
    return pl.pallas_call(
        body, name=name, grid=(s // t,),
        out_shape=(qk_shape, qk_shape, jax.ShapeDtypeStruct((N_HEADS, s, HEAD_DIM), BF16),
                   jax.ShapeDtypeStruct((N_HEADS, KT_ROWS, s), BF16),
                   jax.ShapeDtypeStruct((N_HEADS, HEAD_DIM, s), BF16)),
        in_specs=[pl.BlockSpec((t, 3 * GROUP), lambda i: (i, col))] + ([tab, tab, tab] if rope else [])
        + ([pl.BlockSpec((t, 128), lambda i: (i, 0))] if wide else []),
        out_specs=(qk, qk, heads, pl.BlockSpec((N_HEADS, KT_ROWS, t), lambda i: (0, 0, i)), heads_t),
        compiler_params=_params(),
    )(*((proj,) + (tuple(tables) if rope else ()) + ((c,) if wide else ())))


def _heads_merge(dqt, dk, dv, tables, name, dbuf, col):
    s = dv.shape[1]
    t = min(ROW_TILE, s)
    rope = tables is not None

    def body(*refs):
        o_ref = refs[-1]
        dq = jnp.concatenate([refs[0][h, :HEAD_DIM, :] for h in range(N_HEADS)], axis=0).T
        parts = [dq] + [jnp.concatenate([r[h][:, :HEAD_DIM] for h in range(N_HEADS)], axis=1) for r in refs[1:3]]
        parts[0] = parts[0] * (HEAD_DIM ** -0.5)
        if rope:
            ca, cb, cc = refs[3][...], refs[4][...], refs[5][...]
            for n in range(2):
                p = parts[n]
                parts[n] = p * ca + pltpu.roll(p * cb, 8, 1) + pltpu.roll(p * cc, GROUP - 8, 1)
        o_ref[...] = jnp.concatenate(parts, axis=1).astype(o_ref.dtype)

    tab = pl.BlockSpec((t, GROUP), lambda i: (i, 0))
    heads = pl.BlockSpec((N_HEADS, t, HEAD_DIM), lambda i: (0, i, 0))
    n_in = 6 if rope else 3
    return pl.pallas_call(
        body, name=name, grid=(s // t,), out_shape=jax.ShapeDtypeStruct(dbuf.shape, dbuf.dtype),
        in_specs=[pl.BlockSpec((N_HEADS, KT_ROWS, t), lambda i: (0, 0, i)),
                  pl.BlockSpec((N_HEADS, t, dk.shape[2]), lambda i: (0, i, 0)), heads]
        + ([tab, tab, tab] if rope else []) + [ANY_SPEC],
        out_specs=pl.BlockSpec((t, 3 * GROUP), lambda i: (i, col)), input_output_aliases={n_in: 0},
        compiler_params=_params(),
    )(*((dqt, dk, dv) + (tuple(tables) if rope else ()) + (dbuf,)))


def _log_sigmoid(x):
    return jnp.minimum(x, 0.0) - jnp.log(1.0 + jnp.exp(-jnp.abs(x)))


def _scan_rows(x, reverse):
    n = x.shape[0]
    row = lax.broadcasted_iota(jnp.int32, x.shape, 0)
    k = 1
    while k < n:
        if reverse:
            x = x + jnp.where(row < n - k, _shift_up(x, k), 0.0)
        else:
            x = x + jnp.where(row >= k, _shift_down(x, k), 0.0)
        k *= 2
    return x


def _gate_cumsum(proj, bias):
    s = proj.shape[0]
    col = COL_GATE // 128

    def body(z_ref, b_ref, c_ref):
        c_ref[...] = _scan_rows(_log_sigmoid(z_ref[...] + b_ref[...]), False)

    return pl.pallas_call(
        body, name="gate_cumsum", grid=(1,), out_shape=jax.ShapeDtypeStruct((s, 128), F32),
        in_specs=[pl.BlockSpec((s, 128), lambda i: (0, col)), pl.BlockSpec((1, 128), lambda i: (0, 0))],
        out_specs=pl.BlockSpec((s, 128), lambda i: (0, 0)), compiler_params=_params(),
    )(proj, bias)


def _gate_cumsum_bwd(proj, bias, dc, dbuf):
    s = proj.shape[0]
    col = COL_GATE // 128

    def body(z_ref, b_ref, dc_ref, buf_ref, dz_ref, db_ref):
        dlogf = _scan_rows(dc_ref[...], True)
        dz = dlogf * _sigmoid(-(z_ref[...] + b_ref[...]))
        dz_ref[...] = dz.astype(dz_ref.dtype)
        db_ref[...] = jnp.sum(dz, axis=0, keepdims=True)

    return pl.pallas_call(
        body, name="gate_cumsum_bwd", grid=(1,),
        out_shape=(jax.ShapeDtypeStruct(dbuf.shape, dbuf.dtype), jax.ShapeDtypeStruct((1, 128), F32)),
        in_specs=[pl.BlockSpec((s, 128), lambda i: (0, col)), pl.BlockSpec((1, 128), lambda i: (0, 0)),
                  pl.BlockSpec((s, 128), lambda i: (0, 0)), ANY_SPEC],
        out_specs=(pl.BlockSpec((s, 128), lambda i: (0, col)), pl.BlockSpec((1, 128), lambda i: (0, 0))),
        input_output_aliases={3: 0}, compiler_params=_params(),
    )(proj, bias, dc, dbuf)


DIL_REACH = 2048


def _pair_weight(mode, d):
    if mode == "fox":
        return jnp.where(d >= 0, 1.0, 0.0)
    w1 = jnp.where(jnp.abs(d - 64) <= 64, 1.0, 0.0)
    w2 = jnp.where((d & 3) == 0, jnp.where(jnp.abs(d - 256) <= 256, 1.0, 0.0), 0.0)
    w3 = jnp.where((d & 15) == 0, jnp.where(jnp.abs(d - 1024) <= 1024, 1.0, 0.0), 0.0)
    return w1 + w2 + w3


def _bias_tables(mode, tq, tk):
    nb = 2 if mode == "fox" else DIL_REACH // tk + 1
    n = lax.broadcasted_iota(jnp.int32, (nb, tk, tq), 0)
    key = lax.broadcasted_iota(jnp.int32, (nb, tk, tq), 1)
    query = lax.broadcasted_iota(jnp.int32, (nb, tk, tq), 2)
    w = _pair_weight(mode, n * tk + query - key)
    return jnp.where(w > 0.0, jnp.log(jnp.maximum(w, 1.0)), NEG)


M_INIT = -1e29


def _first_key_chunk(mode, q0, tk):
    if mode == "fox":
        return 0
    return jnp.maximum(q0 - DIL_REACH, 0) // tk


def _attention_fwd(mode, q, k, vt, tab_t, ybuf, col):
    s, width = q.shape[1], q.shape[2]
    tq = min(ATT_TQ, s)
    tk = tq
    nb = tab_t.shape[0]

    def body(q_ref, k_ref, vt_ref, tab_ref, buf_ref, y_ref, o_ref, lse_ref):
        i = pl.program_id(0)
        lo = _first_key_chunk(mode, i * tq, tk)
        outs = []
        for h in range(N_HEADS):
            qv = q_ref[h]

            def step(c, carry, h=h, qv=qv):
                m, l, acc = carry
                k0 = pl.multiple_of(c * tk, tk)
                sc = lax.dot_general(k_ref[h, pl.ds(k0, tk), :], qv, (NT, ((), ())), preferred_element_type=F32)
                sc = sc + tab_ref[jnp.minimum(i - c, nb - 1)]
                m_new = jnp.maximum(m, jnp.max(sc, axis=0, keepdims=True))
                alpha = jnp.exp(m - m_new)
                p = jnp.exp(sc - m_new)
                l = alpha * l + jnp.sum(p, axis=0, keepdims=True)
                acc = alpha * acc + jnp.dot(vt_ref[h, :, pl.ds(k0, tk)], p.astype(BF16), preferred_element_type=F32)
                return m_new, l, acc

            m, l, acc = lax.fori_loop(
                lo, i + 1, step,
                (jnp.full((1, tq), M_INIT, F32), jnp.zeros((1, tq), F32), jnp.zeros((HEAD_DIM, tq), F32)))
            outs.append(acc / l)
            lse_ref[h] = m + jnp.log(l)
        out = jnp.concatenate(outs, axis=0).T
        y_ref[...] = out.astype(y_ref.dtype)
        o_ref[...] = out

    rowspec = pl.BlockSpec((N_HEADS, 1, tq), lambda i: (0, 0, i))
    return pl.pallas_call(
        body, name="attention_fwd_" + mode, grid=(s // tq,),
        out_shape=(jax.ShapeDtypeStruct(ybuf.shape, ybuf.dtype), jax.ShapeDtypeStruct((s, GROUP), F32),
                   jax.ShapeDtypeStruct((N_HEADS, 1, s), F32)),
        in_specs=[pl.BlockSpec((N_HEADS, tq, width), lambda i: (0, i, 0)),
                  pl.BlockSpec((N_HEADS, s, width), lambda i: (0, 0, 0)),
                  pl.BlockSpec((N_HEADS, HEAD_DIM, s), lambda i: (0, 0, 0)),
                  pl.BlockSpec((nb, tk, tq), lambda i: (0, 0, 0)), ANY_SPEC],
        out_specs=(pl.BlockSpec((tq, GROUP), lambda i: (i, col)), pl.BlockSpec((tq, GROUP), lambda i: (i, 0)),
                   rowspec),
        input_output_aliases={4: 0}, compiler_params=_params(),
    )(q, k, vt, tab_t, ybuf)


def _attention_delta(o, do, col):
    s = o.shape[0]
    t = min(ROW_TILE, s)

    def body(o_ref, do_ref, delta_ref, dob_ref):
        dov = do_ref[...]
        prod_t = (o_ref[...] * dov).T
        for h in range(N_HEADS):
            hs = slice(h * HEAD_DIM, (h + 1) * HEAD_DIM)
            delta_ref[h] = jnp.sum(prod_t[hs, :], axis=0, keepdims=True)
            dob_ref[h] = dov[:, hs].astype(dob_ref.dtype)

    return pl.pallas_call(
        body, name="attention_delta", grid=(s // t,),
        out_shape=(jax.ShapeDtypeStruct((N_HEADS, 1, s), F32), jax.ShapeDtypeStruct((N_HEADS, s, HEAD_DIM), BF16)),
        in_specs=[pl.BlockSpec((t, GROUP), lambda i: (i, 0)), pl.BlockSpec((t, GROUP), lambda i: (i, col))],
        out_specs=(pl.BlockSpec((N_HEADS, 1, t), lambda i: (0, 0, i)),
                   pl.BlockSpec((N_HEADS, t, HEAD_DIM), lambda i: (0, i, 0))),
        compiler_params=_params(),
    )(o, do)


def _attention_bwd(mode, q, k, v, kt, tab_t, dob, lse, delta):
    s, width = q.shape[1], q.shape[2]
    tq = min(ATT_TQ, s)
    tk = tq
    nq = s // tq
    nb = tab_t.shape[0]

    def body(q_ref, k_ref, v_ref, kt_ref, tab_ref, dob_ref, lse_ref, delta_ref, dqt_ref, dk_ref, dv_ref):
        i = pl.program_id(0)

        @pl.when(i == 0)
        def _():
            dqt_ref[...] = jnp.zeros_like(dqt_ref)

        hi = nq if mode == "fox" else jnp.minimum((i * tk + tk - 1 + DIL_REACH) // tq + 1, nq)
        for h in range(N_HEADS):
            kv, vv, ktv = k_ref[h], v_ref[h], kt_ref[h]

            def step(c, carry, h=h, kv=kv, vv=vv, ktv=ktv):
                dk, dv = carry
                q0 = pl.multiple_of(c * tq, tq)
                qs = pl.ds(q0, tq)
                qv = q_ref[h, qs, :]
                dov = dob_ref[h, qs, :]
                sc = lax.dot_general(kv, qv, (NT, ((), ())), preferred_element_type=F32)
                p = jnp.exp(sc + tab_ref[jnp.minimum(c - i, nb - 1)] - lse_ref[h, :, qs])
                dp = lax.dot_general(vv, dov, (NT, ((), ())), preferred_element_type=F32)
                ds = p * (dp - delta_ref[h, :, qs])
                dsb = ds.astype(BF16)
                dv = dv + jnp.dot(p.astype(BF16), dov, preferred_element_type=F32)
                dk = dk + jnp.dot(dsb, qv, preferred_element_type=F32)
                dqt_ref[h, :, qs] += jnp.dot(ktv, dsb, preferred_element_type=F32)
                return dk, dv

            dk, dv = lax.fori_loop(i, hi, step, (jnp.zeros((tk, width), F32), jnp.zeros((tk, HEAD_DIM), F32)))
            dk_ref[h] = dk
            dv_ref[h] = dv

    def full(shape):
        return pl.BlockSpec(shape, lambda i: (0, 0, 0))

    kblk = pl.BlockSpec((N_HEADS, tk, width), lambda i: (0, i, 0))
    vblk = pl.BlockSpec((N_HEADS, tk, HEAD_DIM), lambda i: (0, i, 0))
    return pl.pallas_call(
        body, name="attention_bwd_" + mode, grid=(s // tk,),
        out_shape=(jax.ShapeDtypeStruct((N_HEADS, KT_ROWS, s), F32), jax.ShapeDtypeStruct((N_HEADS, s, width), F32),
                   jax.ShapeDtypeStruct((N_HEADS, s, HEAD_DIM), F32)),
        in_specs=[full((N_HEADS, s, width)), kblk, vblk, pl.BlockSpec((N_HEADS, KT_ROWS, tk), lambda i: (0, 0, i)),
                  full((nb, tk, tq)), full((N_HEADS, s, HEAD_DIM)), full((N_HEADS, 1, s)), full((N_HEADS, 1, s))],
        out_specs=(full((N_HEADS, KT_ROWS, s)), kblk, vblk),
        compiler_params=_params(),
    )(q, k, v, kt, tab_t, dob, lse, delta)


def _xattn_fwd(qx, kvm):
    s = qx.shape[0]
    t = min(ROW_TILE, s)

    def body(q_ref, kv_ref, o_ref):
        for h in range(XA_HEADS):
            qv = q_ref[:, h * XA_DIM:(h + 1) * XA_DIM].astype(BF16)
            kv = kv_ref[h].astype(BF16)
            vv = kv_ref[XA_HEADS + h].astype(BF16)
            sc = lax.dot_general(qv, kv, (NT, ((), ())), preferred_element_type=F32) * (XA_DIM ** -0.5)
            e = jnp.exp(sc - jnp.max(sc, axis=-1, keepdims=True))
            p = e / jnp.sum(e, axis=-1, keepdims=True)
            o_ref[:, h * XA_DIM:(h + 1) * XA_DIM] = jnp.dot(p.astype(BF16), vv,
                                                             preferred_element_type=F32).astype(o_ref.dtype)

    return pl.pallas_call(
        body, name="xattn_fwd", grid=(s // t,), out_shape=jax.ShapeDtypeStruct((s, D_MODEL), BF16),
        in_specs=[pl.BlockSpec((t, D_MODEL), lambda i: (i, 0)),
                  pl.BlockSpec((2 * XA_HEADS, MEM_LEN, XA_DIM), lambda i: (0, 0, 0))],
        out_specs=pl.BlockSpec((t, D_MODEL), lambda i: (i, 0)), compiler_params=_params(),
    )(qx, kvm)


def _xattn_bwd(qx, kvm, do):
    s = qx.shape[0]
    t = min(ROW_TILE, s)

    def body(q_ref, kv_ref, do_ref, dq_ref, dkv_ref):
        i = pl.program_id(0)
        for h in range(XA_HEADS):
            qv = q_ref[:, h * XA_DIM:(h + 1) * XA_DIM].astype(BF16)
            dov = do_ref[:, h * XA_DIM:(h + 1) * XA_DIM].astype(BF16)
            kv = kv_ref[h].astype(BF16)
            vv = kv_ref[XA_HEADS + h].astype(BF16)
            sc = lax.dot_general(qv, kv, (NT, ((), ())), preferred_element_type=F32) * (XA_DIM ** -0.5)
            e = jnp.exp(sc - jnp.max(sc, axis=-1, keepdims=True))
            p = e / jnp.sum(e, axis=-1, keepdims=True)
            dp = lax.dot_general(dov, vv, (NT, ((), ())), preferred_element_type=F32)
            ds = (p * (dp - jnp.sum(p * dp, axis=-1, keepdims=True)) * (XA_DIM ** -0.5)).astype(BF16)
            dq_ref[:, h * XA_DIM:(h + 1) * XA_DIM] = jnp.dot(ds, kv, preferred_element_type=F32).astype(dq_ref.dtype)
            dk = lax.dot_general(ds, qv, (TN, ((), ())), preferred_element_type=F32)
            dv = lax.dot_general(p.astype(BF16), dov, (TN, ((), ())), preferred_element_type=F32)

            @pl.when(i == 0)
            def _(h=h, dk=dk, dv=dv):
                dkv_ref[h] = dk
                dkv_ref[XA_HEADS + h] = dv

            @pl.when(i > 0)
            def _(h=h, dk=dk, dv=dv):
                dkv_ref[h] += dk
                dkv_ref[XA_HEADS + h] += dv

    row = pl.BlockSpec((t, D_MODEL), lambda i: (i, 0))
    kvs = pl.BlockSpec((2 * XA_HEADS, MEM_LEN, XA_DIM), lambda i: (0, 0, 0))
    return pl.pallas_call(
        body, name="xattn_bwd", grid=(s // t,),
        out_shape=(jax.ShapeDtypeStruct((s, D_MODEL), BF16),
                   jax.ShapeDtypeStruct((2 * XA_HEADS, MEM_LEN, XA_DIM), F32)),
        in_specs=[row, kvs, row], out_specs=(row, kvs), compiler_params=_params(),
    )(qx, kvm, do)


def _adamw(parts, owns, me, w, m, v, name):
    nl, r, c = w.shape
    tr = r
    for cand in (256, 128, 64, 32, 16, 8):
        if r % cand == 0 and r > cand and N_DEV * cand * c * 4 <= ADAMW_BLOCK_BYTES:
            tr = cand
            break
    nt = r // tr
    per_layer = N_DEV + (1 if owns is not None else 0)

    def body(me_ref, *refs):
        w_ref, m_ref, v_ref, g_ref, d_ref, nm_ref, nv_ref = refs[nl * per_layer:]
        layer = pl.program_id(0)
        g = None
        for l in range(nl):
            p_refs = refs[l * per_layer:(l + 1) * per_layer]
            gl = None
            for d in range(N_DEV):
                term = p_refs[d][...].astype(F32)
                if owns is not None:
                    term = jnp.where(me_ref[0] == d, p_refs[N_DEV][...].astype(F32), term)
                gl = term if gl is None else gl + term
            g = gl if g is None else jnp.where(layer == l, gl, g)
        mn = ADAM_B1 * m_ref[...] + (1.0 - ADAM_B1) * g
        vn = ADAM_B2 * v_ref[...] + (1.0 - ADAM_B2) * (g * g)
        m_hat = mn / (1.0 - ADAM_B1 ** ADAM_STEP)
        v_hat = vn / (1.0 - ADAM_B2 ** ADAM_STEP)
        g_ref[...] = g
        d_ref[...] = -ADAM_LR * (m_hat / (jnp.sqrt(v_hat) + ADAM_EPS) + ADAM_WD * w_ref[...])
        nm_ref[...] = mn
        nv_ref[...] = vn

    def rows(l, ll, i):
        return jnp.where(ll == l, i, jnp.where(ll < l, 0, nt - 1))

    def part_spec(l, d):
        if owns is None:
            return pl.BlockSpec((None, tr, c), lambda ll, i, me_ref: (d, rows(l, ll, i), 0))
        return pl.BlockSpec((None, tr, c),
                            lambda ll, i, me_ref: (jnp.where(me_ref[0] == d, (d + 1) % N_DEV, d), rows(l, ll, i), 0))

    def own_spec(l):
        return pl.BlockSpec((None, tr, c), lambda ll, i, me_ref: (me_ref[0], rows(l, ll, i), 0))

    in_specs, operands = [], []
    for l in range(nl):
        in_specs += [part_spec(l, d) for d in range(N_DEV)]
        operands += [parts[l]] * N_DEV
        if owns is not None:
            in_specs.append(own_spec(l))
            operands.append(owns[l])
    blk = pl.BlockSpec((None, tr, c), lambda ll, i, me_ref: (ll, i, 0))
    shp = jax.ShapeDtypeStruct((nl, r, c), F32)
    return pl.pallas_call(
        body, name=name, out_shape=(shp, shp, shp, shp),
        grid_spec=pltpu.PrefetchScalarGridSpec(
            num_scalar_prefetch=1, grid=(nl, nt), in_specs=in_specs + [blk, blk, blk],
            out_specs=(blk, blk, blk, blk)),
        compiler_params=_params(),
    )(me.reshape(1), *operands, w, m, v)


GROUPS = {"in": ("w_in",), "rest": ("w_out", "w_xq", "w_xo", "w_xkv", "w_up", "w_down")}
FULL_SHAPES = {"w_in": (D_MODEL, N_IN_PAD), "w_out": (D_MODEL, D_MODEL), "w_xq": (D_MODEL, D_MODEL),
               "w_xo": (D_MODEL, D_MODEL), "w_xkv": (N_DEV, D_MODEL, 2 * D_MODEL // N_DEV),
               "w_up": (N_DEV, D_MODEL, FF_SHARD), "w_down": (FF_HALF, FF_SHARD, D_MODEL)}
PIECE_SHAPES = {"w_in": (N_DEV, D_MODEL // N_DEV, N_IN_PAD), "w_out": (N_DEV, D_MODEL // N_DEV, D_MODEL),
                "w_xq": (N_DEV, D_MODEL // N_DEV, D_MODEL), "w_xo": (N_DEV, D_MODEL // N_DEV, D_MODEL),
                "w_xkv": (N_DEV, D_MODEL, 2 * D_MODEL // N_DEV), "w_up": (N_DEV, D_MODEL, FF_SHARD),
                "w_down": (N_DEV, D_FF // N_DEV, D_MODEL)}
CONV_WORDS = 8192
SMALL_WORDS = 80 * 1024


class _GatheredWeights:
    def __init__(self, states, layer):
        self.states, self.layer, self.full, self.extra = dict(states), layer, {}, None

    def need(self, group, after):
        if group in self.states:
            got, _ = _exchange_wait(self.states.pop(group), after, "gather_%s_wait_%d" % (group, self.layer))
            for name, g in zip(GROUPS[group], got):
                self.full[name] = g.reshape(FULL_SHAPES[name])
            self.extra = got[len(GROUPS[group]):]

    def __getitem__(self, name):
        return self.full[name]


def _relay_in_cols(w):
    pad = jnp.zeros(w.shape[:-1] + (N_IN_PAD - N_IN,), w.dtype)
    return jnp.concatenate([w[..., :2304], w[..., 2308:N_IN], w[..., 2304:2308], pad], axis=-1)


def _unrelay_in_cols(w):
    return jnp.concatenate([w[..., :2304], w[..., COL_GATE:COL_GATE + 4], w[..., 2304:COL_GATE]], axis=-1)


def _layer_fwd(h, memv, w, sm, tables):
    sv = {"h0": h}
    xn = _rms_fwd(h, sm["g_mix"], "rms_mix")
    w.need("in", xn)
    proj = _mm_nn(xn, w["w_in"], "mm_in", tn=896)
    sv["xn"], sv["proj"] = xn, proj
    ycat = _sconv_fwd(proj, sm["w_sconv"])
    qd, kd, vd, ktd, vtd = _heads_split(proj, 1, tables["rope"], None, "split_dil")
    ycat, ob, lse_b = _attention_fwd("dil", qd, kd, vtd, tables["dil"], ycat, 1)
    sv["dil"] = (qd, kd, vd, ktd, ob, lse_b)
    c = _gate_cumsum(proj, sm["b_forget_pad"])
    qf, kf, vf, ktf, vtf = _heads_split(proj, 2, None, c, "split_fox")
    ycat, oc, lse_c = _attention_fwd("fox", qf, kf, vtf, tables["fox"], ycat, 2)
    sv["fox"] = (qf, kf, vf, ktf, oc, lse_c)
    ycat = _pool_fwd(proj, sm["w_pool_bd"], sm["pool_scale"], ycat)
    sv["ycat"] = ycat
    w.need("rest", ycat)
    h1 = _mm_nn(ycat, w["w_out"], "mm_out", res=h)
    sv["h1"] = h1
    xq = _rms_fwd(h1, sm["g_xa"], "rms_xa")
    memn = _rms_fwd(memv, sm["g_mem"], "rms_mem")
    qx = _mm_nn(xq, w["w_xq"], "mm_xq")
    kvm = _matmul(memn, w["w_xkv"], (N_DEV, MEM_LEN, XA_DIM), grid=(N_DEV, 1, 1),
                  a_spec=pl.BlockSpec((MEM_LEN, D_MODEL), lambda i, j, r: (0, 0)),
                  b_spec=pl.BlockSpec((None, D_MODEL, XA_DIM), lambda i, j, r: (i, 0, 0)),
                  o_spec=pl.BlockSpec((None, MEM_LEN, XA_DIM), lambda i, j, r: (i, 0, 0)),
                  dims=NN, nred=1, name="mm_xkv")
    ox = _xattn_fwd(qx, kvm)
    sv.update(xq=xq, memn=memn, qx=qx, kvm=kvm, ox=ox)
    h2 = _mm_nn(ox, w["w_xo"], "mm_xo", res=h1)
    sv["h2"] = h2
    xf = _rms_fwd(h2, sm["g_ffn"], "rms_ffn")
    s = h.shape[0]
    tm = min(ROW_TILE, s)
    u0 = _matmul(xf, w["w_up"], (N_DEV, s, FF_SHARD), grid=(s // tm, N_DEV, 1),
                 a_spec=pl.BlockSpec((tm, D_MODEL), lambda i, j, r: (i, 0)),
                 b_spec=pl.BlockSpec((None, D_MODEL, FF_SHARD), lambda i, j, r: (j, 0, 0)),
                 o_spec=pl.BlockSpec((None, tm, FF_SHARD), lambda i, j, r: (j, i, 0)),
                 dims=NN, nred=1, name="mm_up")
    act = _ffn_gate_fwd(u0, sm["w_ffconv"])
    sv.update(xf=xf, u0=u0, act=act)
    ospec = pl.BlockSpec((tm, D_MODEL), lambda i, j, r: (i, 0))
    h3 = _matmul(act, w["w_down"], (s, D_MODEL), grid=(s // tm, 1, FF_HALF),
                 a_spec=pl.BlockSpec((None, tm, FF_SHARD), lambda i, j, r: (r, i, 0)),
                 b_spec=pl.BlockSpec((None, FF_SHARD, D_MODEL), lambda i, j, r: (r, 0, 0)),
                 o_spec=ospec, dims=NN, nred=FF_HALF, name="mm_down", res=h2, res_spec=ospec)
    return h3, sv


def _layer_bwd(dh3, memv, w, sm, tables, sv, rest_ready):
    s = dh3.shape[0]
    tm = min(ROW_TILE, s)
    ts = min(1024, s)
    big, small = {}, {}
    dact = _matmul(dh3, w["w_down"], (FF_HALF, s, FF_SHARD), grid=(s // tm, FF_HALF, 1),
                   a_spec=pl.BlockSpec((tm, D_MODEL), lambda i, j, r: (i, 0)),
                   b_spec=pl.BlockSpec((None, FF_SHARD, D_MODEL), lambda i, j, r: (j, 0, 0)),
                   o_spec=pl.BlockSpec((None, tm, FF_SHARD), lambda i, j, r: (j, i, 0)),
                   dims=NT, nred=1, name="mm_dact")
    big["w_down"] = _matmul(sv["act"], dh3, (FF_HALF, FF_SHARD, D_MODEL), grid=(FF_HALF, 1, s // ts),
                            a_spec=pl.BlockSpec((None, ts, FF_SHARD), lambda i, j, r: (i, r, 0)),
                            b_spec=pl.BlockSpec((ts, D_MODEL), lambda i, j, r: (r, 0)),
                            o_spec=pl.BlockSpec((None, FF_SHARD, D_MODEL), lambda i, j, r: (i, 0, 0)),
                            dims=TN, nred=s // ts, name="mm_dw_down", out_dtype=GRAD_DTYPE)
    du0, small["w_ffconv"] = _ffn_gate_bwd(sv["u0"], sm["w_ffconv"], dact)
    dxf = _matmul(du0, w["w_up"], (s, D_MODEL), grid=(s // tm, 1, N_DEV),
                  a_spec=pl.BlockSpec((None, tm, FF_SHARD), lambda i, j, r: (r, i, 0)),
                  b_spec=pl.BlockSpec((None, D_MODEL, FF_SHARD), lambda i, j, r: (r, 0, 0)),
                  o_spec=pl.BlockSpec((tm, D_MODEL), lambda i, j, r: (i, 0)),
                  dims=NT, nred=N_DEV, name="mm_dxf")
    big["w_up"] = _matmul(sv["xf"], du0, (N_DEV, D_MODEL, FF_SHARD), grid=(N_DEV, 1, s // ts),
                          a_spec=pl.BlockSpec((ts, D_MODEL), lambda i, j, r: (r, 0)),
                          b_spec=pl.BlockSpec((None, ts, FF_SHARD), lambda i, j, r: (i, r, 0)),
                          o_spec=pl.BlockSpec((None, D_MODEL, FF_SHARD), lambda i, j, r: (i, 0, 0)),
                          dims=TN, nred=s // ts, name="mm_dw_up", out_dtype=GRAD_DTYPE)
    dh2, small["g_ffn"] = _rms_bwd(dxf, sv["h2"], sm["g_ffn"], dh3, "rms_ffn_bwd")
    dox = _mm_nt(dh2, w["w_xo"], "mm_dox")
    big["w_xo"] = _mm_tn(sv["ox"], dh2, "mm_dw_xo")
    dqx, dkvm = _xattn_bwd(sv["qx"], sv["kvm"], dox)
    dxq = _mm_nt(dqx, w["w_xq"], "mm_dxq")
    big["w_xq"] = _mm_tn(sv["xq"], dqx, "mm_dw_xq")
    big["w_xkv"] = _matmul(sv["memn"], dkvm, (N_DEV, D_MODEL, XA_DIM), grid=(N_DEV, 1, 1),
                           a_spec=pl.BlockSpec((MEM_LEN, D_MODEL), lambda i, j, r: (0, 0)),
                           b_spec=pl.BlockSpec((None, MEM_LEN, XA_DIM), lambda i, j, r: (i, 0, 0)),
                           o_spec=pl.BlockSpec((None, D_MODEL, XA_DIM), lambda i, j, r: (i, 0, 0)),
                           dims=TN, nred=1, name="mm_dw_xkv", out_dtype=GRAD_DTYPE)
    dmemn = _matmul(dkvm, w["w_xkv"], (MEM_LEN, D_MODEL), grid=(1, 1, N_DEV),
                    a_spec=pl.BlockSpec((None, MEM_LEN, XA_DIM), lambda i, j, r: (r, 0, 0)),
                    b_spec=pl.BlockSpec((None, D_MODEL, XA_DIM), lambda i, j, r: (r, 0, 0)),
                    o_spec=pl.BlockSpec((MEM_LEN, D_MODEL), lambda i, j, r: (0, 0)),
                    dims=NT, nred=N_DEV, name="mm_dmemn")
    _, small["g_mem"] = _rms_bwd(dmemn, memv, sm["g_mem"], None, "rms_mem_bwd")
    dh1, small["g_xa"] = _rms_bwd(dxq, sv["h1"], sm["g_xa"], dh2, "rms_xa_bwd")
    dycat = _mm_nt(dh1, w["w_out"], "mm_dycat")
    big["w_out"] = _mm_tn(sv["ycat"], dh1, "mm_dw_out")
    proj = sv["proj"]
    dproj, small["w_sconv"] = _sconv_bwd(proj, sm["w_sconv"] + rest_ready(big), dycat)
    qd, kd, vd, ktd, ob, lse_b = sv["dil"]
    delta, dob = _attention_delta(ob, dycat, 1)
    dqt, dk, dv = _attention_bwd("dil", qd, kd, vd, ktd, tables["dil"], dob, lse_b, delta)
    dproj = _heads_merge(dqt, dk, dv, tables["rope"], "merge_dil", dproj, 1)
    qf, kf, vf, ktf, oc, lse_c = sv["fox"]
    delta, dob = _attention_delta(oc, dycat, 2)
    dqt, dk, dv = _attention_bwd("fox", qf, kf, vf, ktf, tables["fox"], dob, lse_c, delta)
    dproj = _heads_merge(dqt, dk, dv, None, "merge_fox", dproj, 2)
    dc = (dqt[:, HEAD_DIM, :] - dk[:, :, LANE_CK]).T
    dc = jnp.concatenate([dc, jnp.zeros((s, 128 - N_HEADS), F32)], axis=1)
    dproj, dbias = _gate_cumsum_bwd(proj, sm["b_forget_pad"], dc, dproj)
    small["b_forget"] = dbias[0, :N_HEADS]
    dproj, dwbd, small["pool_scale"] = _pool_bwd(proj, sm["w_pool_bd"], sm["pool_scale"], dycat, dproj)
    small["w_pool"] = jnp.stack([dwbd[64 * g:64 * (g + 1), 64 * g:64 * (g + 1)] for g in range(4)])
    dxn = _mm_nt(dproj, w["w_in"], "mm_dxn")
    big["w_in"] = _mm_tn(sv["xn"], dproj, "mm_dw_in", tn=896)
    dh0, small["g_mix"] = _rms_bwd(dxn, sv["h0"], sm["g_mix"], dh1, "rms_mix_bwd")
    return dh0, big, small


SMALL_NAMES = ("g_mix", "b_forget", "w_pool", "pool_scale", "g_xa", "g_mem", "g_ffn", "w_sconv", "w_ffconv")
WEIGHT_NAMES = ("g_mix", "w_in", "b_forget", "w_sconv", "w_pool", "pool_scale", "w_out", "g_xa", "g_mem", "w_xq",
                "w_xkv", "w_xo", "g_ffn", "w_up", "w_ffconv", "w_down", "g_final")


def _block_diag(w_pool):
    z = jnp.zeros((64, 64), F32)
    return jnp.concatenate(
        [jnp.concatenate([w_pool[g] if c == g else z for c in range(4)], axis=1) for g in range(4)], axis=0)


def kernel(x, mem, positions, g_mix, w_in, b_forget, w_sconv, w_pool, pool_scale, w_out, g_xa, g_mem, w_xq, w_xkv, w_xo, g_ffn, w_up, w_ffconv, w_down, g_final, loss_target, m_g_mix, m_w_in, m_b_forget, m_w_sconv, m_w_pool, m_pool_scale, m_w_out, m_g_xa, m_g_mem, m_w_xq, m_w_xkv, m_w_xo, m_g_ffn, m_w_up, m_w_ffconv, m_w_down, m_g_final, v_g_mix, v_w_in, v_b_forget, v_w_sconv, v_w_pool, v_pool_scale, v_w_out, v_g_xa, v_g_mem, v_w_xq, v_w_xkv, v_w_xo, v_g_ffn, v_w_up, v_w_ffconv, v_w_down, v_g_final):
    weights = dict(g_mix=g_mix, w_in=w_in, b_forget=b_forget, w_sconv=w_sconv, w_pool=w_pool, pool_scale=pool_scale,
                   w_out=w_out, g_xa=g_xa, g_mem=g_mem, w_xq=w_xq, w_xkv=w_xkv, w_xo=w_xo, g_ffn=g_ffn, w_up=w_up,
                   w_ffconv=w_ffconv, w_down=w_down, g_final=g_final)
    m_in = dict(g_mix=m_g_mix, w_in=m_w_in, b_forget=m_b_forget, w_sconv=m_w_sconv, w_pool=m_w_pool,
                pool_scale=m_pool_scale, w_out=m_w_out, g_xa=m_g_xa, g_mem=m_g_mem, w_xq=m_w_xq, w_xkv=m_w_xkv,
                w_xo=m_w_xo, g_ffn=m_g_ffn, w_up=m_w_up, w_ffconv=m_w_ffconv, w_down=m_w_down, g_final=m_g_final)
    v_in = dict(g_mix=v_g_mix, w_in=v_w_in, b_forget=v_b_forget, w_sconv=v_w_sconv, w_pool=v_w_pool,
                pool_scale=v_pool_scale, w_out=v_w_out, g_xa=v_g_xa, g_mem=v_g_mem, w_xq=v_w_xq, w_xkv=v_w_xkv,
                w_xo=v_w_xo, g_ffn=v_g_ffn, w_up=v_w_up, w_ffconv=v_w_ffconv, w_down=v_w_down, g_final=v_g_final)
    depth = w_in.shape[0]
    me = 4 * lax.axis_index("x") + 2 * lax.axis_index("y") + lax.axis_index("c")
    h = x[0]
    memv = mem[0]
    s = h.shape[0]
    tq = min(ATT_TQ, s)
    tables = {"rope": _rope_tables(positions[0]), "dil": _bias_tables("dil", tq, tq),
              "fox": _bias_tables("fox", tq, tq)}

    w_in_r = _relay_in_cols(w_in)
    conv_shard = jnp.concatenate([w_sconv.reshape(-1), w_ffconv.reshape(-1)])
    conv_shard = jnp.concatenate([conv_shard, jnp.zeros((CONV_WORDS - conv_shard.shape[0],), F32)])
    conv_bits = lax.bitcast_convert_type(conv_shard, BF16).reshape(2 * CONV_WORDS // 1024, 1024)
    gathered = []
    order = jnp.zeros((), F32)
    for l in range(depth):
        shards = dict(w_in=w_in_r[l], w_out=w_out[l], w_xq=w_xq[l], w_xo=w_xo[l], w_xkv=w_xkv[l], w_up=w_up[l],
                      w_down=w_down[l])
        states = {}
        for group in ("in", "rest"):
            shards[GROUPS[group][0]] = shards[GROUPS[group][0]] + order
            xs = [_place_shard(shards[name], me, BF16, "place_%s_%d" % (name, l)) for name in GROUPS[group]]
            if l == 0 and group == "in":
                xs.append(_place_shard(conv_bits, me, BF16, "place_conv"))
            states[group], token = _exchange_start(xs, False, "gather_%s_start_%d" % (group, l))
            order = order + token[0, 0]
        gathered.append(_GatheredWeights(states, l))
    gathered[0].need("in", tables["rope"][0])
    conv_all = lax.bitcast_convert_type(gathered[0].extra[0].reshape(N_DEV, CONV_WORDS, 2), F32)
    n_sc = depth * 3 * (GROUP // N_DEV)
    sconv_full = conv_all[:, :n_sc].reshape(N_DEV, depth, 3, GROUP // N_DEV).transpose(1, 2, 0, 3).reshape(
        depth, 3, GROUP)
    ffconv_full = conv_all[:, n_sc:n_sc + depth * 3 * FF_SHARD].reshape(N_DEV, depth, 3, FF_SHARD).transpose(
        1, 0, 2, 3)

    smalls = []
    for l in range(depth):
        smalls.append(dict(
            g_mix=g_mix[l], g_xa=g_xa[l], g_mem=g_mem[l], g_ffn=g_ffn[l], pool_scale=pool_scale[l],
            w_pool_bd=_block_diag(w_pool[l]), w_sconv=sconv_full[l], w_ffconv=ffconv_full[l],
            b_forget_pad=jnp.concatenate([b_forget[l], jnp.zeros((128 - N_HEADS,), F32)]).reshape(1, 128)))
    smalls[0]["g_mix"] = smalls[0]["g_mix"] + order

    saved = []
    for l in range(depth):
        h, sv = _layer_fwd(h, memv, gathered[l], smalls[l], tables)
        saved.append(sv)
    loss_part, dh, dg_final = _loss_head(h, g_final, loss_target[0])
    loss = lax.psum(loss_part[0, 0], MESH_AXES)

    small_grads = [None] * depth
    scatters = {}

    def pieces_of(big, group):
        return [big[name].reshape(PIECE_SHAPES[name]) for name in GROUPS[group]]

    for l in reversed(range(depth)):
        def rest_ready(big, l=l):
            scatters[l, "rest"], token = _exchange_start(pieces_of(big, "rest"), True, "scatter_rest_start_%d" % l)
            return token[0, 0]

        dh, big, small_grads[l] = _layer_bwd(dh, memv, gathered[l], smalls[l], tables, saved[l], rest_ready)
        xs = pieces_of(big, "in")
        if l == 0:
            flat = [small_grads[ll][n].reshape(-1) for n in SMALL_NAMES for ll in range(depth)]
            flat = jnp.concatenate(flat + [dg_final.reshape(-1)])
            flat = jnp.concatenate([flat, jnp.zeros((SMALL_WORDS - flat.shape[0],), F32)])
            xs.append(jnp.broadcast_to(flat.reshape(1, -1, 1024), (N_DEV, SMALL_WORDS // 1024, 1024)))
        scatters[l, "in"], token = _exchange_start(xs, True, "scatter_in_start_%d" % l)
        if l > 0:
            smalls[l - 1]["w_ffconv"] = smalls[l - 1]["w_ffconv"] + token[0, 0]
    grad_x = dh[None]

    parts, owns = {}, {}

    def wait_group(group, after):
        extra = None
        for l in reversed(range(depth)):
            got, given = _exchange_wait(scatters[l, group], after, "scatter_%s_wait_%d" % (group, l))
            for name, g, x in zip(GROUPS[group], got, given):
                parts.setdefault(name, [None] * depth)[l] = g
                owns.setdefault(name, [None] * depth)[l] = x
            extra = (got[len(GROUPS[group]):], given[len(GROUPS[group]):])
        return extra

    results = {}

    def update(name, w3, m3, v3):
        outs = _adamw(parts[name], owns.get(name), me, w3, m3, v3, "adamw_" + name)
        results[name] = [o.reshape(weights[name].shape) for o in outs]

    wait_group("rest", grad_x)
    for name in GROUPS["rest"]:
        update(name, weights[name], m_in[name], v_in[name])
    small_got, small_given = wait_group("in", results["w_down"][1])
    small_all = lax.dynamic_update_slice_in_dim(small_got[0], small_given[0][:1], me, axis=0).reshape(N_DEV, -1)
    outs = _adamw(parts["w_in"], owns["w_in"], me, w_in_r, _relay_in_cols(m_w_in), _relay_in_cols(v_w_in),
                  "adamw_w_in")
    results["w_in"] = [_unrelay_in_cols(o) for o in outs]
    off = 0
    for name in SMALL_NAMES + ("g_final",):
        wv = weights[name]
        full_shape = {"w_sconv": (depth, 3, GROUP), "w_ffconv": (depth, N_DEV, 3, FF_SHARD)}.get(name, wv.shape)
        n = 1
        for dim in full_shape:
            n *= dim
        p = small_all[:, off:off + n].reshape((N_DEV,) + tuple(full_shape))
        off += n
        if name == "w_sconv":
            p = lax.dynamic_slice_in_dim(p, me * (GROUP // N_DEV), GROUP // N_DEV, axis=3)
        elif name == "w_ffconv":
            p = lax.dynamic_index_in_dim(p, me, axis=2, keepdims=False)
        shape3 = (1, 1, wv.shape[0]) if wv.ndim == 1 else (1, -1, wv.shape[-1])
        w3 = wv.reshape(shape3)
        parts[name] = [p.reshape((N_DEV,) + w3.shape[1:])]
        update(name, w3, m_in[name].reshape(shape3), v_in[name].reshape(shape3))

    return (loss, grad_x, *[results[n][0] for n in WEIGHT_NAMES], *[results[n][1] for n in WEIGHT_NAMES],
            *[results[n][2] for n in WEIGHT_NAMES], *[results[n][3] for n in WEIGHT_NAMES])
```

```python
import functools

import jax
import jax.numpy as jnp
from jax import lax
from jax.experimental import pallas as pl
from jax.experimental.pallas import tpu as pltpu

F32 = jnp.float32
BF16 = jnp.bfloat16

N_DEV = 8
D_MODEL = 1024
GROUP = 256
HEAD_DIM = 64
N_HEADS = 4
N_IN = 2564
N_IN_PAD = 2688
COL_GATE = 2560
XA_HEADS = 4
XA_DIM = 256
MEM_LEN = 256
D_FF = 2816
FF_SHARD = 704
FF_HALF = 4
ROPE_THETA = 500000.0
ROPE_DIM = 16
RMS_EPS = 1e-6
NEG = -1e30
POOL_WINDOWS = (2, 4, 8, 16)
ADAM_LR, ADAM_B1, ADAM_B2, ADAM_EPS, ADAM_WD, ADAM_STEP = 0.001, 0.9, 0.999, 1e-08, 0.01, 10

ROW_TILE = 512
ATT_TQ = 256
BWD_HEADS = 2
VMEM_LIMIT = 56 * 1024 * 1024
ADAMW_BLOCK_BYTES = 4 * 1024 * 1024
PLACE_BLOCK_BYTES = 4 * 1024 * 1024

MESH_AXES = ("x", "y", "c")


def _params(**kw):
    return pltpu.CompilerParams(vmem_limit_bytes=VMEM_LIMIT, **kw)


HBM_SPEC = pl.BlockSpec(memory_space=pltpu.HBM)
SEM_SPEC = pl.BlockSpec(memory_space=pltpu.SEMAPHORE)
DATAFLOW = pltpu.SideEffectType.DATAFLOW_SIDE_EFFECTING


def _peer_copies(x_ref, land_ref, send_sems, recv_sems, scatter):
    mx, my, mc = lax.axis_index("x"), lax.axis_index("y"), lax.axis_index("c")
    me = 4 * mx + 2 * my + mc
    pairs = []
    for k in range(1, N_DEV):
        kx, ky, kc = (k >> 2) & 1, (k >> 1) & 1, k & 1
        peer_lin = me ^ k
        send = pltpu.make_async_remote_copy(
            src_ref=x_ref.at[peer_lin] if scatter else land_ref.at[me], dst_ref=land_ref.at[me],
            send_sem=send_sems.at[k - 1], recv_sem=recv_sems.at[k - 1],
            device_id=(mx ^ kx, my ^ ky, mc ^ kc), device_id_type=pl.DeviceIdType.MESH)
        arrival = pltpu.make_async_remote_copy(
            src_ref=land_ref.at[peer_lin], dst_ref=land_ref.at[peer_lin],
            send_sem=send_sems.at[k - 1], recv_sem=recv_sems.at[k - 1],
            device_id=(mx, my, mc), device_id_type=pl.DeviceIdType.MESH)
        pairs.append((send, arrival))
    return pairs


def _exchange_start(xs, scatter, name):
    n = len(xs)
    ns = n if scatter else 0

    def body(*refs):
        srcs = refs[:ns] if scatter else (None,) * n
        lands, sends, recvs = refs[ns:ns + n], refs[ns + n:ns + 2 * n], refs[ns + 2 * n:ns + 3 * n]
        for t in range(n):
            for send, _ in _peer_copies(srcs[t], lands[t], sends[t], recvs[t], scatter):
                send.start()
        token = refs[-1]
        token[...] = jnp.zeros_like(token)

    sems = pltpu.SemaphoreType.DMA((N_DEV - 1,))
    operands = [pltpu.with_memory_space_constraint(x, pltpu.HBM) for x in xs]
    if scatter:
        operands += [pltpu.with_memory_space_constraint(lax.empty(x.shape, x.dtype), pltpu.HBM) for x in xs]
    outs = pl.pallas_call(
        body, name=name,
        out_shape=(sems,) * (2 * n) + tuple(pltpu.HBM(a.shape, a.dtype) for a in operands)
        + (jax.ShapeDtypeStruct((8, 128), F32),),
        in_specs=(HBM_SPEC,) * (ns + n),
        out_specs=(SEM_SPEC,) * (2 * n) + (HBM_SPEC,) * (ns + n) + (pl.BlockSpec(memory_space=pltpu.VMEM),),
        input_output_aliases={i: 2 * n + i for i in range(ns + n)},
        compiler_params=pltpu.CompilerParams(has_side_effects=DATAFLOW),
    )(*operands)
    return (outs[:-1], scatter), outs[-1]


def _exchange_wait(state, after, name):
    held, scatter = state
    n = len(held) // (4 if scatter else 3)
    ns = n if scatter else 0
    sems, thru = held[:2 * n], held[2 * n:]

    def body(*refs):
        srcs = refs[:ns] if scatter else (None,) * n
        lands, sends, recvs = refs[ns:ns + n], refs[ns + n:ns + 2 * n], refs[ns + 2 * n:ns + 3 * n]
        for t in range(n):
            for send, arrival in _peer_copies(srcs[t], lands[t], sends[t], recvs[t], scatter):
                send.wait_send()
                arrival.wait_recv()

    outs = pl.pallas_call(
        body, name=name,
        out_shape=tuple(pltpu.HBM(a.shape, a.dtype) for a in thru),
        in_specs=(HBM_SPEC,) * (ns + n) + (SEM_SPEC,) * (2 * n) + (pl.BlockSpec(memory_space=pl.ANY),),
        out_specs=(HBM_SPEC,) * (ns + n), input_output_aliases={i: i for i in range(ns + n)},
        compiler_params=pltpu.CompilerParams(has_side_effects=DATAFLOW),
    )(*thru, *sems, after)
    return list(outs[ns:]), list(outs[:ns])


def _place_shard(x, me, dtype, name):
    r, c = x.shape
    tr = r
    if r * c * 4 > PLACE_BLOCK_BYTES:
        for cand in (512, 256, 128, 64, 32, 16):
            if r % cand == 0 and cand * c * 4 <= PLACE_BLOCK_BYTES:
                tr = cand
                break

    def body(me_ref, x_ref, o_ref):
        o_ref[...] = x_ref[...].astype(o_ref.dtype)

    return pl.pallas_call(
        body, name=name, out_shape=jax.ShapeDtypeStruct((N_DEV, r, c), dtype),
        grid_spec=pltpu.PrefetchScalarGridSpec(
            num_scalar_prefetch=1, grid=(r // tr,),
            in_specs=[pl.BlockSpec((tr, c), lambda i, me_ref: (i, 0))],
            out_specs=pl.BlockSpec((None, tr, c), lambda i, me_ref: (me_ref[0], i, 0))),
        compiler_params=_params(),
    )(me.reshape(1), x)


NN = ((1,), (0,))
NT = ((1,), (1,))
TN = ((0,), (0,))


def _matmul(a, b, out_shape, *, grid, a_spec, b_spec, o_spec, dims, nred, name, res=None, res_spec=None,
            out_dtype=F32, slabs=0):
    has_res = res is not None

    def body(*refs):
        a_ref, b_ref = refs[0], refs[1]
        r_ref = refs[2] if has_res else None
        o_ref = refs[3] if has_res else refs[2]
        if slabs:
            part = None
            for n in range(slabs):
                term = lax.dot_general(a_ref[n].astype(BF16), b_ref[n].astype(BF16), (dims, ((), ())),
                                       preferred_element_type=F32)
                part = term if part is None else part + term
        else:
            part = lax.dot_general(a_ref[...].astype(BF16), b_ref[...].astype(BF16), (dims, ((), ())),
                                   preferred_element_type=F32)
        if nred == 1:
            if has_res:
                part = part + r_ref[...]
            o_ref[...] = part.astype(o_ref.dtype)
        else:
            acc = refs[-1]
            r = pl.program_id(2)

            @pl.when(r == 0)
            def _():
                acc[...] = part

            @pl.when(r > 0)
            def _():
                acc[...] += part

            @pl.when(r == nred - 1)
            def _():
                tot = acc[...]
                if has_res:
                    tot = tot + r_ref[...]
                o_ref[...] = tot.astype(o_ref.dtype)

    in_specs = [a_spec, b_spec] + ([res_spec] if has_res else [])
    args = (a, b) + ((res,) if has_res else ())
    acc_shape = tuple(d for d in o_spec.block_shape if d is not None)
    return pl.pallas_call(
        body, name=name, grid=grid, out_shape=jax.ShapeDtypeStruct(out_shape, out_dtype),
        in_specs=in_specs, out_specs=o_spec,
        scratch_shapes=[pltpu.VMEM(acc_shape, F32)] if nred > 1 else [],
        compiler_params=_params(),
    )(*args)


def _mm_nn(a, w, name, res=None, tn=None, out_dtype=F32):
    m, k = a.shape
    n = w.shape[1]
    tn = tn or n
    tm = min(ROW_TILE, m)
    ospec = pl.BlockSpec((tm, tn), lambda i, j, r: (i, j))
    return _matmul(a, w, (m, n), grid=(m // tm, n // tn, 1),
                   a_spec=pl.BlockSpec((tm, k), lambda i, j, r: (i, 0)),
                   b_spec=pl.BlockSpec((k, tn), lambda i, j, r: (0, j)),
                   o_spec=ospec, dims=NN, nred=1, name=name, res=res, res_spec=ospec if res is not None else None,
                   out_dtype=out_dtype)


def _mm_nt(a, w, name, out_dtype=F32):
    m, n = a.shape
    k = w.shape[0]
    tm = min(ROW_TILE, m)
    return _matmul(a, w, (m, k), grid=(m // tm, 1, 1),
                   a_spec=pl.BlockSpec((tm, n), lambda i, j, r: (i, 0)),
                   b_spec=pl.BlockSpec((k, n), lambda i, j, r: (0, 0)),
                   o_spec=pl.BlockSpec((tm, k), lambda i, j, r: (i, 0)), dims=NT, nred=1, name=name,
                   out_dtype=out_dtype)


GRAD_DTYPE = BF16


def _mm_tn(a, b, name, tk=512, tn=None):
    s, k = a.shape
    n = b.shape[1]
    tn = tn or n
    tk = min(tk, k)
    ts = s if b.dtype == BF16 else max(s // 2, 1)
    return _matmul(a, b, (k, n), grid=(k // tk, n // tn, s // ts),
                   a_spec=pl.BlockSpec((ts, tk), lambda i, j, r: (r, i)),
                   b_spec=pl.BlockSpec((ts, tn), lambda i, j, r: (r, j)),
                   o_spec=pl.BlockSpec((tk, tn), lambda i, j, r: (i, j)), dims=TN, nred=s // ts, name=name,
                   out_dtype=GRAD_DTYPE)


def _rms_fwd(h, g, name):
    s, d = h.shape
    tm = min(ROW_TILE, s)

    def body(h_ref, g_ref, o_ref):
        hv = h_ref[...]
        r = lax.rsqrt(jnp.mean(hv * hv, axis=-1, keepdims=True) + RMS_EPS)
        o_ref[...] = (hv * r * g_ref[...]).astype(o_ref.dtype)

    return pl.pallas_call(
        body, name=name, grid=(s // tm,), out_shape=jax.ShapeDtypeStruct((s, d), BF16),
        in_specs=[pl.BlockSpec((tm, d), lambda i: (i, 0)), pl.BlockSpec((1, d), lambda i: (0, 0))],
        out_specs=pl.BlockSpec((tm, d), lambda i: (i, 0)), compiler_params=_params(),
    )(h, g.reshape(1, d))


def _rms_bwd(dy, h, g, res, name):
    s, d = h.shape
    tm = min(ROW_TILE, s)
    has_res = res is not None

    def body(*refs):
        dy_ref, h_ref, g_ref = refs[:3]
        r_ref = refs[3] if has_res else None
        dh_ref, dg_ref = refs[-2], refs[-1]
        hv = h_ref[...]
        r = lax.rsqrt(jnp.mean(hv * hv, axis=-1, keepdims=True) + RMS_EPS)
        hn = hv * r
        dyv = dy_ref[...].astype(F32)
        u = dyv * g_ref[...]
        dh = r * (u - hn * jnp.mean(u * hn, axis=-1, keepdims=True))
        if has_res:
            dh = dh + r_ref[...]
        dh_ref[...] = dh
        part = jnp.sum(dyv * hn, axis=0, keepdims=True)

        @pl.when(pl.program_id(0) == 0)
        def _():
            dg_ref[...] = part

        @pl.when(pl.program_id(0) > 0)
        def _():
            dg_ref[...] += part

    row = pl.BlockSpec((tm, d), lambda i: (i, 0))
    vec = pl.BlockSpec((1, d), lambda i: (0, 0))
    dh, dg = pl.pallas_call(
        body, name=name, grid=(s // tm,),
        out_shape=(jax.ShapeDtypeStruct((s, d), F32), jax.ShapeDtypeStruct((1, d), F32)),
        in_specs=[row, row, vec] + ([row] if has_res else []),
        out_specs=(row, vec), compiler_params=_params(),
    )(*((dy, h, g.reshape(1, d)) + ((res,) if has_res else ())))
    return dh, dg.reshape(d)


def _loss_head(h, g, target):
    s, d = h.shape
    tm = min(ROW_TILE, s)

    def body(h_ref, g_ref, t_ref, loss_ref, dh_ref, dg_ref):
        hv = h_ref[...]
        r = lax.rsqrt(jnp.mean(hv * hv, axis=-1, keepdims=True) + RMS_EPS)
        hn = hv * r
        gv = g_ref[...]
        err = hn * gv - t_ref[...]
        rows = jnp.mean(err * err, axis=-1, keepdims=True)
        lpart = 0.5 * jnp.sum(rows, axis=0, keepdims=True) + jnp.zeros((1, 128), F32)
        dy = err * (1.0 / d)
        u = dy * gv
        dh_ref[...] = r * (u - hn * jnp.mean(u * hn, axis=-1, keepdims=True))
        gpart = jnp.sum(dy * hn, axis=0, keepdims=True)

        @pl.when(pl.program_id(0) == 0)
        def _():
            dg_ref[...] = gpart
            loss_ref[...] = lpart

        @pl.when(pl.program_id(0) > 0)
        def _():
            dg_ref[...] += gpart
            loss_ref[...] += lpart

    row = pl.BlockSpec((tm, d), lambda i: (i, 0))
    vec = pl.BlockSpec((1, d), lambda i: (0, 0))
    return pl.pallas_call(
        body, name="loss_head", grid=(s // tm,),
        out_shape=(jax.ShapeDtypeStruct((1, 128), F32), jax.ShapeDtypeStruct((s, d), F32),
                   jax.ShapeDtypeStruct((1, d), F32)),
        in_specs=[row, vec, row],
        out_specs=(pl.BlockSpec((1, 128), lambda i: (0, 0)), row, vec), compiler_params=_params(),
    )(h, g.reshape(1, d), target)


def _shift_down(x, k):
    return pltpu.roll(x, k, 0)


def _shift_up(x, k):
    return pltpu.roll(x, x.shape[0] - k, 0)


def _conv3(x, w):
    return w[2:3, :] * x + w[1:2, :] * _shift_down(x, 1) + w[0:1, :] * _shift_down(x, 2)


def _conv3_t(x, w):
    return w[2:3, :] * x + w[1:2, :] * _shift_up(x, 1) + w[0:1, :] * _shift_up(x, 2)


def _sigmoid(x):
    return 1.0 / (1.0 + jnp.exp(-x))


def _prev_map(tile, halo, col):
    return lambda i: (jnp.maximum(i * (tile // halo) - 1, 0), col)


def _next_map(tile, halo, col, nrows):
    return lambda i: (jnp.minimum((i + 1) * (tile // halo), nrows // halo - 1), col)


def _sconv_fwd(proj, w):
    s = proj.shape[0]
    t = min(ROW_TILE, s)

    def body(cur_ref, prev_ref, w_ref, o_ref):
        i = pl.program_id(0)
        prev = prev_ref[...] * (i > 0).astype(F32)
        ext = jnp.concatenate([prev, cur_ref[...]], axis=0)
        sv = ext[:, 2 * GROUP:3 * GROUP] * ext[:, 0:GROUP]
        y = ext[:, GROUP:2 * GROUP] * _conv3(sv, w_ref[...])
        o_ref[...] = y[8:].astype(o_ref.dtype)

    return pl.pallas_call(
        body, name="sconv_fwd", grid=(s // t,), out_shape=jax.ShapeDtypeStruct((s, 4 * GROUP), BF16),
        in_specs=[pl.BlockSpec((t, 3 * GROUP), lambda i: (i, 0)),
                  pl.BlockSpec((8, 3 * GROUP), _prev_map(t, 8, 0)),
                  pl.BlockSpec((3, GROUP), lambda i: (0, 0))],
        out_specs=pl.BlockSpec((t, GROUP), lambda i: (i, 0)), compiler_params=_params(),
    )(proj, proj, w)


def _sconv_bwd(proj, w, dy):
    s = proj.shape[0]
    t = min(ROW_TILE, s)
    nt = s // t

    def body(cur_ref, prev_ref, next_ref, w_ref, dy_ref, dyn_ref, dp_ref, dw_ref):
        i = pl.program_id(0)
        first = (i > 0).astype(F32)
        last = (i < nt - 1).astype(F32)
        ext = jnp.concatenate([prev_ref[...] * first, cur_ref[...], next_ref[...] * last], axis=0)
        dye = jnp.concatenate([jnp.zeros((8, GROUP), F32), dy_ref[...], dyn_ref[...] * last], axis=0)
        hv, bv, cv = ext[:, 0:GROUP], ext[:, GROUP:2 * GROUP], ext[:, 2 * GROUP:3 * GROUP]
        wv = w_ref[...]
        sv = cv * hv
        conv = _conv3(sv, wv)
        dconv = dye * bv
        ds = _conv3_t(dconv, wv)
        dp = jnp.concatenate([ds * cv, dye * conv, ds * hv], axis=1)
        dp_ref[...] = dp[8:8 + t].astype(dp_ref.dtype)
        dc = dconv[8:8 + t]
        dw = jnp.concatenate([
            jnp.sum(dc * _shift_down(sv, 2)[8:8 + t], axis=0, keepdims=True),
            jnp.sum(dc * _shift_down(sv, 1)[8:8 + t], axis=0, keepdims=True),
            jnp.sum(dc * sv[8:8 + t], axis=0, keepdims=True),
            jnp.zeros((5, GROUP), F32)], axis=0)

        @pl.when(i == 0)
        def _():
            dw_ref[...] = dw

        @pl.when(i > 0)
        def _():
            dw_ref[...] += dw

    dp, dw = pl.pallas_call(
        body, name="sconv_bwd", grid=(nt,),
        out_shape=(jax.ShapeDtypeStruct((s, N_IN_PAD), BF16), jax.ShapeDtypeStruct((8, GROUP), F32)),
        in_specs=[pl.BlockSpec((t, 3 * GROUP), lambda i: (i, 0)),
                  pl.BlockSpec((8, 3 * GROUP), _prev_map(t, 8, 0)),
                  pl.BlockSpec((8, 3 * GROUP), _next_map(t, 8, 0, s)),
                  pl.BlockSpec((3, GROUP), lambda i: (0, 0)),
                  pl.BlockSpec((t, GROUP), lambda i: (i, 0)),
                  pl.BlockSpec((8, GROUP), _next_map(t, 8, 0, s))],
        out_specs=(pl.BlockSpec((t, 3 * GROUP), lambda i: (i, 0)), pl.BlockSpec((8, GROUP), lambda i: (0, 0))),
        compiler_params=_params(),
    )(proj, proj, proj, w, dy, dy)
    return dp, dw[:3]


def _lane_window(shape):
    lane = lax.broadcasted_iota(jnp.int32, shape, 1)
    return lane, jnp.where(lane < 64, 2.0, jnp.where(lane < 128, 4.0, jnp.where(lane < 192, 8.0, 16.0)))


def _by_group(lane, s1, s2, s3, s4):
    return jnp.where(lane < 64, s1, jnp.where(lane < 128, s2, jnp.where(lane < 192, s3, s4)))


def _pool_z(ext, row0):
    s1 = ext + _shift_down(ext, 1)
    s2 = s1 + _shift_down(s1, 2)
    s3 = s2 + _shift_down(s2, 4)
    s4 = s3 + _shift_down(s3, 8)
    lane, win = _lane_window(ext.shape)
    tpos = (lax.broadcasted_iota(jnp.int32, ext.shape, 0) + (row0 - 16 + 1)).astype(F32)
    cnt = jnp.maximum(jnp.minimum(tpos, win), 1.0)
    return _by_group(lane, s1, s2, s3, s4) / cnt - ext


ANY_SPEC = pl.BlockSpec(memory_space=pl.ANY)


def _pool_fwd(proj, wbd, scale, ybuf):
    s = proj.shape[0]
    t = min(ROW_TILE, s)
    col = (COL_GATE - GROUP) // GROUP

    def body(cur_ref, prev_ref, w_ref, sc_ref, buf_ref, o_ref):
        i = pl.program_id(0)
        ext = jnp.concatenate([prev_ref[...] * (i > 0).astype(F32), cur_ref[...]], axis=0)
        z = _pool_z(ext, i * t)[16:]
        y = jnp.dot(z.astype(BF16), w_ref[...].astype(BF16), preferred_element_type=F32)
        o_ref[...] = (y * sc_ref[...]).astype(o_ref.dtype)

    return pl.pallas_call(
        body, name="pool_fwd", grid=(s // t,), out_shape=jax.ShapeDtypeStruct(ybuf.shape, ybuf.dtype),
        in_specs=[pl.BlockSpec((t, GROUP), lambda i: (i, col)),
                  pl.BlockSpec((16, GROUP), _prev_map(t, 16, col)),
                  pl.BlockSpec((GROUP, GROUP), lambda i: (0, 0)),
                  pl.BlockSpec((1, GROUP), lambda i: (0, 0)), ANY_SPEC],
        out_specs=pl.BlockSpec((t, GROUP), lambda i: (i, 3)), input_output_aliases={4: 0},
        compiler_params=_params(),
    )(proj, proj, wbd, scale.reshape(1, GROUP), ybuf)


def _pool_bwd(proj, wbd, scale, dy, dbuf):
    s = proj.shape[0]
    t = min(ROW_TILE, s)
    nt = s // t
    col = (COL_GATE - GROUP) // GROUP

    def body(cur_ref, prev_ref, w_ref, sc_ref, dy_ref, dyn_ref, buf_ref, dp_ref, dw_ref, dsc_ref):
        i = pl.program_id(0)
        ext = jnp.concatenate([prev_ref[...] * (i > 0).astype(F32), cur_ref[...]], axis=0)
        z = _pool_z(ext, i * t)[16:]
        wv = w_ref[...].astype(BF16)
        dyc = dy_ref[...]
        dye = jnp.concatenate([dyc, dyn_ref[...] * (i < nt - 1).astype(F32)], axis=0) * sc_ref[...]
        dz = lax.dot_general(dye.astype(BF16), wv, (NT, ((), ())), preferred_element_type=F32)
        lane, win = _lane_window(dz.shape)
        tpos = (lax.broadcasted_iota(jnp.int32, dz.shape, 0) + (i * t + 1)).astype(F32)
        e = dz / jnp.minimum(tpos, win)
        f1 = e + _shift_up(e, 1)
        f2 = f1 + _shift_up(f1, 2)
        f3 = f2 + _shift_up(f2, 4)
        f4 = f3 + _shift_up(f3, 8)
        dp = _by_group(lane, f1, f2, f3, f4) - dz
        dp_ref[...] = dp[:t].astype(dp_ref.dtype)
        zb = z.astype(BF16)
        y = jnp.dot(zb, wv, preferred_element_type=F32)
        dsc = jnp.sum(dyc * y, axis=0, keepdims=True)
        dw = lax.dot_general(zb, dye[:t].astype(BF16), (TN, ((), ())), preferred_element_type=F32)

        @pl.when(i == 0)
        def _():
            dw_ref[...] = dw
            dsc_ref[...] = dsc

        @pl.when(i > 0)
        def _():
            dw_ref[...] += dw
            dsc_ref[...] += dsc

    dp, dw, dsc = pl.pallas_call(
        body, name="pool_bwd", grid=(nt,),
        out_shape=(jax.ShapeDtypeStruct(dbuf.shape, dbuf.dtype), jax.ShapeDtypeStruct((GROUP, GROUP), F32),
                   jax.ShapeDtypeStruct((1, GROUP), F32)),
        in_specs=[pl.BlockSpec((t, GROUP), lambda i: (i, col)),
                  pl.BlockSpec((16, GROUP), _prev_map(t, 16, col)),
                  pl.BlockSpec((GROUP, GROUP), lambda i: (0, 0)),
                  pl.BlockSpec((1, GROUP), lambda i: (0, 0)),
                  pl.BlockSpec((t, GROUP), lambda i: (i, 3)),
                  pl.BlockSpec((16, GROUP), _next_map(t, 16, 3, s)), ANY_SPEC],
        out_specs=(pl.BlockSpec((t, GROUP), lambda i: (i, col)), pl.BlockSpec((GROUP, GROUP), lambda i: (0, 0)),
                   pl.BlockSpec((1, GROUP), lambda i: (0, 0))),
        input_output_aliases={6: 0}, compiler_params=_params(),
    )(proj, proj, wbd, scale.reshape(1, GROUP), dy, dy, dbuf)
    return dp, dw, dsc.reshape(GROUP)


def _ffn_gate_fwd(u0, w):
    s = u0.shape[1]
    t = min(ROW_TILE, s)

    def body(a_ref, ap_ref, g_ref, gp_ref, wa_ref, wg_ref, o_ref):
        first = (pl.program_id(1) > 0).astype(F32)
        a = _conv3(jnp.concatenate([ap_ref[...] * first, a_ref[...]], axis=0), wa_ref[...])[8:]
        g = _conv3(jnp.concatenate([gp_ref[...] * first, g_ref[...]], axis=0), wg_ref[...])[8:]
        o_ref[...] = (a * (g * _sigmoid(g))).astype(o_ref.dtype)

    def cur(off):
        return pl.BlockSpec((None, t, FF_SHARD), lambda j, i: (j + off, i, 0))

    def prev(off):
        return pl.BlockSpec((None, 8, FF_SHARD), lambda j, i: (j + off, jnp.maximum(i * (t // 8) - 1, 0), 0))

    def wspec(off):
        return pl.BlockSpec((None, 3, FF_SHARD), lambda j, i: (j + off, 0, 0))

    return pl.pallas_call(
        body, name="ffn_gate_fwd", grid=(FF_HALF, s // t),
        out_shape=jax.ShapeDtypeStruct((FF_HALF, s, FF_SHARD), BF16),
        in_specs=[cur(0), prev(0), cur(FF_HALF), prev(FF_HALF), wspec(0), wspec(FF_HALF)],
        out_specs=pl.BlockSpec((None, t, FF_SHARD), lambda j, i: (j, i, 0)), compiler_params=_params(),
    )(u0, u0, u0, u0, w, w)


def _ffn_gate_bwd(u0, w, dact):
    s = u0.shape[1]
    t = min(ROW_TILE, s)
    nt = s // t

    def body(c_ref, p_ref, n_ref, w_ref, d_ref, dn_ref, du_ref, dw_ref):
        i = pl.program_id(1)
        first = (i > 0).astype(F32)
        last = (i < nt - 1).astype(F32)
        dext = jnp.concatenate([jnp.zeros((8, FF_SHARD), F32), d_ref[...], dn_ref[...] * last], axis=0)
        ext = [jnp.concatenate([p_ref[n] * first, c_ref[n], n_ref[n] * last], axis=0) for n in range(2)]
        a = _conv3(ext[0], w_ref[0])
        g = _conv3(ext[1], w_ref[1])
        sg = _sigmoid(g)
        silu = g * sg
        dus = (dext * silu, dext * a * (sg + silu * (1.0 - sg)))
        for n in range(2):
            du_ref[n] = _conv3_t(dus[n], w_ref[n])[8:8 + t].astype(du_ref.dtype)
            dc = dus[n][8:8 + t]
            dw = jnp.concatenate([
                jnp.sum(dc * _shift_down(ext[n], 2)[8:8 + t], axis=0, keepdims=True),
                jnp.sum(dc * _shift_down(ext[n], 1)[8:8 + t], axis=0, keepdims=True),
                jnp.sum(dc * ext[n][8:8 + t], axis=0, keepdims=True),
                jnp.zeros((5, FF_SHARD), F32)], axis=0)

            @pl.when(i == 0)
            def _(n=n, dw=dw):
                dw_ref[n] = dw

            @pl.when(i > 0)
            def _(n=n, dw=dw):
                dw_ref[n] += dw

    def pair(rows, row_map):
        return pl.BlockSpec((2, None, rows, FF_SHARD), lambda j, i: (0, j, row_map(i), 0))

    prev_row = lambda i: jnp.maximum(i * (t // 8) - 1, 0)
    next_row = lambda i: jnp.minimum((i + 1) * (t // 8), s // 8 - 1)
    u2 = u0.reshape(2, FF_HALF, s, FF_SHARD)
    du, dw = pl.pallas_call(
        body, name="ffn_gate_bwd", grid=(FF_HALF, nt),
        out_shape=(jax.ShapeDtypeStruct((2, FF_HALF, s, FF_SHARD), BF16),
                   jax.ShapeDtypeStruct((2, FF_HALF, 8, FF_SHARD), F32)),
        in_specs=[pair(t, lambda i: i), pair(8, prev_row), pair(8, next_row), pair(3, lambda i: 0),
                  pl.BlockSpec((None, t, FF_SHARD), lambda j, i: (j, i, 0)),
                  pl.BlockSpec((None, 8, FF_SHARD), lambda j, i: (j, next_row(i), 0))],
        out_specs=(pair(t, lambda i: i), pair(8, lambda i: 0)),
        compiler_params=_params(),
    )(u2, u2, u2, w.reshape(2, FF_HALF, 3, FF_SHARD), dact, dact)
    return du.reshape(2 * FF_HALF, s, FF_SHARD), dw.reshape(2 * FF_HALF, 8, FF_SHARD)[:, :3]


def _rope_tables(positions):
    inv_freq = ROPE_THETA ** (-jnp.arange(0, ROPE_DIM, 2, dtype=F32) / ROPE_DIM)
    ang = positions.astype(F32)[:, None] * inv_freq
    cos, sin = jnp.cos(ang), jnp.sin(ang)
    s = positions.shape[0]
    half = ROPE_DIM // 2
    rest = HEAD_DIM - ROPE_DIM
    ca = jnp.concatenate([cos, cos, jnp.ones((s, rest), F32)], axis=1)
    cb = jnp.concatenate([-sin, jnp.zeros((s, HEAD_DIM - half), F32)], axis=1)
    cc = jnp.concatenate([jnp.zeros((s, half), F32), sin, jnp.zeros((s, rest), F32)], axis=1)
    return tuple(jnp.tile(tb, (1, N_HEADS)) for tb in (ca, cb, cc))


QK_WIDE = 128
LANE_CQ, LANE_CK = 64, 67
KT_ROWS = 80


def _three_bf16(x):
    hi = x.astype(BF16).astype(F32)
    mid = (x - hi).astype(BF16).astype(F32)
    lo = (x - hi - mid).astype(BF16).astype(F32)
    return hi, mid, lo


def _heads_split(proj, col, tables, c, name):
    s = proj.shape[0]
    t = min(ROW_TILE, s)
    rope = tables is not None
    wide = c is not None
    width = QK_WIDE if wide else HEAD_DIM

    def body(*refs):
        x_ref = refs[0]
        q_ref, k_ref, v_ref, kt_ref, vt_ref = refs[-5:]
        xv = x_ref[...]
        parts = [xv[:, 0:GROUP], xv[:, GROUP:2 * GROUP], xv[:, 2 * GROUP:3 * GROUP]]
        if rope:
            ca, cb, cc = refs[1][...], refs[2][...], refs[3][...]
            for n in range(2):
                p = parts[n]
                parts[n] = p * ca + pltpu.roll(p, GROUP - 8, 1) * cb + pltpu.roll(p, 8, 1) * cc
        parts[0] = parts[0] * (HEAD_DIM ** -0.5)
        k_t, v_t = parts[1].T, parts[2].T
        ones_row = jnp.where(lax.broadcasted_iota(jnp.int32, (KT_ROWS - HEAD_DIM, t), 0) == 0, 1.0, 0.0)
        lane = lax.broadcasted_iota(jnp.int32, (t, QK_WIDE), 1)
        zeros = jnp.zeros((t, QK_WIDE - HEAD_DIM), F32)
        for h in range(N_HEADS):
            hs = slice(h * HEAD_DIM, (h + 1) * HEAD_DIM)
            qh, kh = parts[0][:, hs], parts[1][:, hs]
            if wide:
                terms = _three_bf16(refs[-6][:, h:h + 1])
                qh = jnp.concatenate([qh, zeros], axis=1)
                kh = jnp.concatenate([kh, zeros], axis=1)
                for n in range(3):
                    qh = jnp.where(lane == LANE_CQ + n, terms[n], jnp.where(lane == LANE_CK + n, 1.0, qh))
                    kh = jnp.where(lane == LANE_CK + n, -terms[n], jnp.where(lane == LANE_CQ + n, 1.0, kh))
            q_ref[h] = qh.astype(q_ref.dtype)
            k_ref[h] = kh.astype(k_ref.dtype)
            v_ref[h] = parts[2][:, hs].astype(v_ref.dtype)
            kt_ref[h] = jnp.concatenate([k_t[hs, :], ones_row], axis=0).astype(kt_ref.dtype)
            vt_ref[h] = v_t[hs, :].astype(vt_ref.dtype)

    tab = pl.BlockSpec((t, GROUP), lambda i: (i, 0))
    qk = pl.BlockSpec((N_HEADS, t, width), lambda i: (0, i, 0))
    heads = pl.BlockSpec((N_HEADS, t, HEAD_DIM), lambda i: (0, i, 0))
    heads_t = pl.BlockSpec((N_HEADS, HEAD_DIM, t), lambda i: (0, 0, i))
    qk_shape = jax.ShapeDtypeStruct((N_HEADS, s, width), BF16)
    return pl.pallas_call(
        body, name=name, grid=(s // t,),
        out_shape=(qk_shape, qk_shape, jax.ShapeDtypeStruct((N_HEADS, s, HEAD_DIM), BF16),
                   jax.ShapeDtypeStruct((N_HEADS, KT_ROWS, s), BF16),
                   jax.ShapeDtypeStruct((N_HEADS, HEAD_DIM, s), BF16)),
        in_specs=[pl.BlockSpec((t, 3 * GROUP), lambda i: (i, col))] + ([tab, tab, tab] if rope else [])
        + ([pl.BlockSpec((t, 128), lambda i: (i, 0))] if wide else []),
        out_specs=(qk, qk, heads, pl.BlockSpec((N_HEADS, KT_ROWS, t), lambda i: (0, 0, i)), heads_t),
        compiler_params=_params(),
    )(*((proj,) + (tuple(tables) if rope else ()) + ((c,) if wide else ())))


def _heads_merge(dqt, dk, dv, tables, name, dbuf, col):
    s = dv.shape[1]
    t = min(ROW_TILE, s)
    rope = tables is not None

    wide = dk.shape[2] == QK_WIDE

    def body(*refs):
        o_ref = refs[n_in + 1]
        dq = jnp.concatenate([refs[0][h, :HEAD_DIM, :] for h in range(N_HEADS)], axis=0).T
        parts = [dq] + [jnp.concatenate([r[h][:, :HEAD_DIM] for h in range(N_HEADS)], axis=1) for r in refs[1:3]]
        parts[0] = parts[0] * (HEAD_DIM ** -0.5)
        if rope:
            ca, cb, cc = refs[3][...], refs[4][...], refs[5][...]
            for n in range(2):
                p = parts[n]
                parts[n] = p * ca + pltpu.roll(p * cb, 8, 1) + pltpu.roll(p * cc, GROUP - 8, 1)
        o_ref[...] = jnp.concatenate(parts, axis=1).astype(o_ref.dtype)
        if wide:
            over_keys = jnp.concatenate([refs[0][h, HEAD_DIM:HEAD_DIM + 8, :] for h in range(N_HEADS)]
                                        + [jnp.zeros((128 - 8 * N_HEADS, t), F32)], axis=0).T
            lane = lax.broadcasted_iota(jnp.int32, (t, 128), 1)
            dc = jnp.zeros((t, 128), F32)
            for h in range(N_HEADS):
                dc = jnp.where(lane == h, over_keys[:, 8 * h:8 * h + 1] - refs[1][h][:, LANE_CK:LANE_CK + 1], dc)
            refs[n_in + 2][...] = dc

    tab = pl.BlockSpec((t, GROUP), lambda i: (i, 0))
    heads = pl.BlockSpec((N_HEADS, t, HEAD_DIM), lambda i: (0, i, 0))
    n_in = 6 if rope else 3
    dspec = pl.BlockSpec((t, 3 * GROUP), lambda i: (i, col))
    dshape = jax.ShapeDtypeStruct(dbuf.shape, dbuf.dtype)
    return pl.pallas_call(
        body, name=name, grid=(s // t,),
        out_shape=(dshape, jax.ShapeDtypeStruct((s, 128), F32)) if wide else dshape,
        in_specs=[pl.BlockSpec((N_HEADS, KT_ROWS, t), lambda i: (0, 0, i)),
                  pl.BlockSpec((N_HEADS, t, dk.shape[2]), lambda i: (0, i, 0)), heads]
        + ([tab, tab, tab] if rope else []) + [ANY_SPEC],
        out_specs=(dspec, pl.BlockSpec((t, 128), lambda i: (i, 0))) if wide else dspec,
        input_output_aliases={n_in: 0}, compiler_params=_params(),
    )(*((dqt, dk, dv) + (tuple(tables) if rope else ()) + (dbuf,)))


def _log_sigmoid(x):
    return jnp.minimum(x, 0.0) - jnp.log(1.0 + jnp.exp(-jnp.abs(x)))


def _scan_rows(x, reverse):
    n = x.shape[0]
    row = lax.broadcasted_iota(jnp.int32, x.shape, 0)
    k = 1
    while k < n:
        if reverse:
            x = x + jnp.where(row < n - k, _shift_up(x, k), 0.0)
        else:
            x = x + jnp.where(row >= k, _shift_down(x, k), 0.0)
        k *= 2
    return x


def _gate_cumsum(proj, bias):
    s = proj.shape[0]
    col = COL_GATE // 128

    def body(z_ref, b_ref, c_ref):
        c_ref[...] = _scan_rows(_log_sigmoid(z_ref[...] + b_ref[...]), False)

    return pl.pallas_call(
        body, name="gate_cumsum", grid=(1,), out_shape=jax.ShapeDtypeStruct((s, 128), F32),
        in_specs=[pl.BlockSpec((s, 128), lambda i: (0, col)), pl.BlockSpec((1, 128), lambda i: (0, 0))],
        out_specs=pl.BlockSpec((s, 128), lambda i: (0, 0)), compiler_params=_params(),
    )(proj, bias)


def _gate_cumsum_bwd(proj, bias, dc, dbuf):
    s = proj.shape[0]
    col = COL_GATE // 128

    def body(z_ref, b_ref, dc_ref, buf_ref, dz_ref, db_ref):
        dlogf = _scan_rows(dc_ref[...], True)
        dz = dlogf * _sigmoid(-(z_ref[...] + b_ref[...]))
        dz_ref[...] = dz.astype(dz_ref.dtype)
        db_ref[...] = jnp.sum(dz, axis=0, keepdims=True)

    return pl.pallas_call(
        body, name="gate_cumsum_bwd", grid=(1,),
        out_shape=(jax.ShapeDtypeStruct(dbuf.shape, dbuf.dtype), jax.ShapeDtypeStruct((1, 128), F32)),
        in_specs=[pl.BlockSpec((s, 128), lambda i: (0, col)), pl.BlockSpec((1, 128), lambda i: (0, 0)),
                  pl.BlockSpec((s, 128), lambda i: (0, 0)), ANY_SPEC],
        out_specs=(pl.BlockSpec((s, 128), lambda i: (0, col)), pl.BlockSpec((1, 128), lambda i: (0, 0))),
        input_output_aliases={3: 0}, compiler_params=_params(),
    )(proj, bias, dc, dbuf)


DIL_REACH = 2048


def _pair_weight(mode, d):
    if mode == "fox":
        return jnp.where(d >= 0, 1.0, 0.0)
    w1 = jnp.where(jnp.abs(d - 64) <= 64, 1.0, 0.0)
    w2 = jnp.where((d & 3) == 0, jnp.where(jnp.abs(d - 256) <= 256, 1.0, 0.0), 0.0)
    w3 = jnp.where((d & 15) == 0, jnp.where(jnp.abs(d - 1024) <= 1024, 1.0, 0.0), 0.0)
    return w1 + w2 + w3


def _bias_tables(mode, tq, tk):
    nb = 2 if mode == "fox" else DIL_REACH // tk + 1
    n = lax.broadcasted_iota(jnp.int32, (nb, tk, tq), 0)
    key = lax.broadcasted_iota(jnp.int32, (nb, tk, tq), 1)
    query = lax.broadcasted_iota(jnp.int32, (nb, tk, tq), 2)
    w = _pair_weight(mode, n * tk + query - key)
    return jnp.where(w > 0.0, jnp.log(jnp.maximum(w, 1.0)), NEG)


M_INIT = -1e29


def _first_key_chunk(mode, q0, tk):
    if mode == "fox":
        return 0
    return jnp.maximum(q0 - DIL_REACH, 0) // tk


def _attention_fwd(mode, q, k, vt, tab_t, ybuf, col):
    s, width = q.shape[1], q.shape[2]
    tq = min(ATT_TQ, s)
    tk = tq
    nb = tab_t.shape[0]

    def body(q_ref, k_ref, vt_ref, tab_ref, buf_ref, y_ref, o_ref, lse_ref):
        i = pl.program_id(0)
        lo = _first_key_chunk(mode, i * tq, tk)

        def step(c, carry):
            k0 = pl.multiple_of(c * tk, tk)
            tab = tab_ref[jnp.minimum(i - c, nb - 1)]
            new = []
            for h in range(N_HEADS):
                m, l, acc = carry[3 * h:3 * h + 3]
                sc = lax.dot_general(k_ref[h, pl.ds(k0, tk), :], q_ref[h], (NT, ((), ())),
                                     preferred_element_type=F32) + tab
                m_new = jnp.maximum(m, jnp.max(sc, axis=0, keepdims=True))
                alpha = jnp.exp(m - m_new)
                p = jnp.exp(sc - m_new)
                l = alpha * l + jnp.sum(p, axis=0, keepdims=True)
                acc = alpha * acc + jnp.dot(vt_ref[h, :, pl.ds(k0, tk)], p.astype(BF16), preferred_element_type=F32)
                new += [m_new, l, acc]
            return tuple(new)

        start = (jnp.full((1, tq), M_INIT, F32), jnp.zeros((1, tq), F32), jnp.zeros((HEAD_DIM, tq), F32))
        done = lax.fori_loop(lo, i + 1, step, start * N_HEADS)
        outs = []
        for h in range(N_HEADS):
            m, l, acc = done[3 * h:3 * h + 3]
            outs.append(acc / l)
            lse_ref[h] = m + jnp.log(l)
        out = jnp.concatenate(outs, axis=0).T
        y_ref[...] = out.astype(y_ref.dtype)
        o_ref[...] = out

    rowspec = pl.BlockSpec((N_HEADS, 1, tq), lambda i: (0, 0, i))
    return pl.pallas_call(
        body, name="attention_fwd_" + mode, grid=(s // tq,),
        out_shape=(jax.ShapeDtypeStruct(ybuf.shape, ybuf.dtype), jax.ShapeDtypeStruct((s, GROUP), F32),
                   jax.ShapeDtypeStruct((N_HEADS, 1, s), F32)),
        in_specs=[pl.BlockSpec((N_HEADS, tq, width), lambda i: (0, i, 0)),
                  pl.BlockSpec((N_HEADS, s, width), lambda i: (0, 0, 0)),
                  pl.BlockSpec((N_HEADS, HEAD_DIM, s), lambda i: (0, 0, 0)),
                  pl.BlockSpec((nb, tk, tq), lambda i: (0, 0, 0)), ANY_SPEC],
        out_specs=(pl.BlockSpec((tq, GROUP), lambda i: (i, col)), pl.BlockSpec((tq, GROUP), lambda i: (i, 0)),
                   rowspec),
        input_output_aliases={4: 0}, compiler_params=_params(),
    )(q, k, vt, tab_t, ybuf)


def _attention_delta(o, do, col):
    s = o.shape[0]
    t = min(ROW_TILE, s)

    def body(o_ref, do_ref, delta_ref, dob_ref):
        dov = do_ref[...]
        prod_t = (o_ref[...] * dov).T
        for h in range(N_HEADS):
            hs = slice(h * HEAD_DIM, (h + 1) * HEAD_DIM)
            delta_ref[h] = jnp.sum(prod_t[hs, :], axis=0, keepdims=True)
            dob_ref[h] = dov[:, hs].astype(dob_ref.dtype)

    return pl.pallas_call(
        body, name="attention_delta", grid=(s // t,),
        out_shape=(jax.ShapeDtypeStruct((N_HEADS, 1, s), F32), jax.ShapeDtypeStruct((N_HEADS, s, HEAD_DIM), BF16)),
        in_specs=[pl.BlockSpec((t, GROUP), lambda i: (i, 0)), pl.BlockSpec((t, GROUP), lambda i: (i, col))],
        out_specs=(pl.BlockSpec((N_HEADS, 1, t), lambda i: (0, 0, i)),
                   pl.BlockSpec((N_HEADS, t, HEAD_DIM), lambda i: (0, i, 0))),
        compiler_params=_params(),
    )(o, do)


def _attention_bwd(mode, q, k, v, kt, tab_t, dob, lse, delta):
    s, width = q.shape[1], q.shape[2]
    tq = min(ATT_TQ, s)
    tk = tq
    nq = s // tq
    nb = tab_t.shape[0]

    def body(q_ref, k_ref, v_ref, kt_ref, tab_ref, dob_ref, lse_ref, delta_ref, dqt_ref, dk_ref, dv_ref):
        i = pl.program_id(0)

        @pl.when(i == 0)
        def _():
            dqt_ref[...] = jnp.zeros_like(dqt_ref)

        hi = nq if mode == "fox" else jnp.minimum((i * tk + tk - 1 + DIL_REACH) // tq + 1, nq)
        for h0 in range(0, N_HEADS, BWD_HEADS):
            heads = range(h0, h0 + BWD_HEADS)

            def step(c, carry, heads=heads):
                q0 = pl.multiple_of(c * tq, tq)
                qs = pl.ds(q0, tq)
                tab = tab_ref[jnp.minimum(c - i, nb - 1)]
                new = []
                for n, h in enumerate(heads):
                    dk, dv = carry[2 * n:2 * n + 2]
                    qv = q_ref[h, qs, :]
                    dov = dob_ref[h, qs, :]
                    sc = lax.dot_general(k_ref[h], qv, (NT, ((), ())), preferred_element_type=F32)
                    p = jnp.exp(sc + tab - lse_ref[h, :, qs])
                    dp = lax.dot_general(v_ref[h], dov, (NT, ((), ())), preferred_element_type=F32)
                    dsb = (p * (dp - delta_ref[h, :, qs])).astype(BF16)
                    dv = dv + jnp.dot(p.astype(BF16), dov, preferred_element_type=F32)
                    dk = dk + jnp.dot(dsb, qv, preferred_element_type=F32)
                    dqt_ref[h, :, qs] += jnp.dot(kt_ref[h], dsb, preferred_element_type=F32)
                    new += [dk, dv]
                return tuple(new)

            start = (jnp.zeros((tk, width), F32), jnp.zeros((tk, HEAD_DIM), F32))
            done = lax.fori_loop(i, hi, step, start * BWD_HEADS)
            for n, h in enumerate(heads):
                dk_ref[h] = done[2 * n]
                dv_ref[h] = done[2 * n + 1]

    def full(shape):
        return pl.BlockSpec(shape, lambda i: (0, 0, 0))

    kblk = pl.BlockSpec((N_HEADS, tk, width), lambda i: (0, i, 0))
    vblk = pl.BlockSpec((N_HEADS, tk, HEAD_DIM), lambda i: (0, i, 0))
    return pl.pallas_call(
        body, name="attention_bwd_" + mode, grid=(s // tk,),
        out_shape=(jax.ShapeDtypeStruct((N_HEADS, KT_ROWS, s), F32), jax.ShapeDtypeStruct((N_HEADS, s, width), F32),
                   jax.ShapeDtypeStruct((N_HEADS, s, HEAD_DIM), F32)),
        in_specs=[full((N_HEADS, s, width)), kblk, vblk, pl.BlockSpec((N_HEADS, KT_ROWS, tk), lambda i: (0, 0, i)),
                  full((nb, tk, tq)), full((N_HEADS, s, HEAD_DIM)), full((N_HEADS, 1, s)), full((N_HEADS, 1, s))],
        out_specs=(full((N_HEADS, KT_ROWS, s)), kblk, vblk),
        compiler_params=_params(),
    )(q, k, v, kt, tab_t, dob, lse, delta)


def _xattn_fwd(qx, kvm):
    s = qx.shape[0]
    t = min(ROW_TILE, s)

    def body(q_ref, kv_ref, o_ref):
        for h in range(XA_HEADS):
            qv = q_ref[:, h * XA_DIM:(h + 1) * XA_DIM].astype(BF16)
            kv = kv_ref[h].astype(BF16)
            vv = kv_ref[XA_HEADS + h].astype(BF16)
            sc = lax.dot_general(qv, kv, (NT, ((), ())), preferred_element_type=F32) * (XA_DIM ** -0.5)
            e = jnp.exp(sc - jnp.max(sc, axis=-1, keepdims=True))
            p = e / jnp.sum(e, axis=-1, keepdims=True)
            o_ref[:, h * XA_DIM:(h + 1) * XA_DIM] = jnp.dot(p.astype(BF16), vv,
                                                             preferred_element_type=F32).astype(o_ref.dtype)

    return pl.pallas_call(
        body, name="xattn_fwd", grid=(s // t,), out_shape=jax.ShapeDtypeStruct((s, D_MODEL), BF16),
        in_specs=[pl.BlockSpec((t, D_MODEL), lambda i: (i, 0)),
                  pl.BlockSpec((2 * XA_HEADS, MEM_LEN, XA_DIM), lambda i: (0, 0, 0))],
        out_specs=pl.BlockSpec((t, D_MODEL), lambda i: (i, 0)), compiler_params=_params(),
    )(qx, kvm)


def _xattn_bwd(qx, kvm, do):
    s = qx.shape[0]
    t = min(ROW_TILE, s)

    def body(q_ref, kv_ref, do_ref, dq_ref, dkv_ref):
        i = pl.program_id(0)
        for h in range(XA_HEADS):
            qv = q_ref[:, h * XA_DIM:(h + 1) * XA_DIM].astype(BF16)
            dov = do_ref[:, h * XA_DIM:(h + 1) * XA_DIM].astype(BF16)
            kv = kv_ref[h].astype(BF16)
            vv = kv_ref[XA_HEADS + h].astype(BF16)
            sc = lax.dot_general(qv, kv, (NT, ((), ())), preferred_element_type=F32) * (XA_DIM ** -0.5)
            e = jnp.exp(sc - jnp.max(sc, axis=-1, keepdims=True))
            p = e / jnp.sum(e, axis=-1, keepdims=True)
            dp = lax.dot_general(dov, vv, (NT, ((), ())), preferred_element_type=F32)
            ds = (p * (dp - jnp.sum(p * dp, axis=-1, keepdims=True)) * (XA_DIM ** -0.5)).astype(BF16)
            dq_ref[:, h * XA_DIM:(h + 1) * XA_DIM] = jnp.dot(ds, kv, preferred_element_type=F32).astype(dq_ref.dtype)
            dk = lax.dot_general(ds, qv, (TN, ((), ())), preferred_element_type=F32)
            dv = lax.dot_general(p.astype(BF16), dov, (TN, ((), ())), preferred_element_type=F32)

            @pl.when(i == 0)
            def _(h=h, dk=dk, dv=dv):
                dkv_ref[h] = dk
                dkv_ref[XA_HEADS + h] = dv

            @pl.when(i > 0)
            def _(h=h, dk=dk, dv=dv):
                dkv_ref[h] += dk
                dkv_ref[XA_HEADS + h] += dv

    row = pl.BlockSpec((t, D_MODEL), lambda i: (i, 0))
    kvs = pl.BlockSpec((2 * XA_HEADS, MEM_LEN, XA_DIM), lambda i: (0, 0, 0))
    return pl.pallas_call(
        body, name="xattn_bwd", grid=(s // t,),
        out_shape=(jax.ShapeDtypeStruct((s, D_MODEL), BF16),
                   jax.ShapeDtypeStruct((2 * XA_HEADS, MEM_LEN, XA_DIM), F32)),
        in_specs=[row, kvs, row], out_specs=(row, kvs), compiler_params=_params(),
    )(qx, kvm, do)


def _adamw(parts, owns, me, w, m, v, name):
    nl, r, c = w.shape
    tr = r
    for cand in (256, 128, 64, 32, 16, 8):
        if r % cand == 0 and r > cand and N_DEV * cand * c * 4 <= ADAMW_BLOCK_BYTES:
            tr = cand
            break
    nt = r // tr
    per_layer = N_DEV + (1 if owns is not None else 0)

    def body(me_ref, *refs):
        w_ref, m_ref, v_ref, g_ref, d_ref, nm_ref, nv_ref = refs[nl * per_layer:]
        layer = pl.program_id(0)
        g = None
        for l in range(nl):
            p_refs = refs[l * per_layer:(l + 1) * per_layer]
            gl = None
            for d in range(N_DEV):
                term = p_refs[d][...].astype(F32)
                if owns is not None:
                    term = jnp.where(me_ref[0] == d, p_refs[N_DEV][...].astype(F32), term)
                gl = term if gl is None else gl + term
            g = gl if g is None else jnp.where(layer == l, gl, g)
        mn = ADAM_B1 * m_ref[...] + (1.0 - ADAM_B1) * g
        vn = ADAM_B2 * v_ref[...] + (1.0 - ADAM_B2) * (g * g)
        m_hat = mn / (1.0 - ADAM_B1 ** ADAM_STEP)
        v_hat = vn / (1.0 - ADAM_B2 ** ADAM_STEP)
        g_ref[...] = g
        d_ref[...] = -ADAM_LR * (m_hat / (jnp.sqrt(v_hat) + ADAM_EPS) + ADAM_WD * w_ref[...])
        nm_ref[...] = mn
        nv_ref[...] = vn

    def rows(l, ll, i):
        return jnp.where(ll == l, i, jnp.where(ll < l, 0, nt - 1))

    def part_spec(l, d):
        if owns is None:
            return pl.BlockSpec((None, tr, c), lambda ll, i, me_ref: (d, rows(l, ll, i), 0))
        return pl.BlockSpec((None, tr, c),
                            lambda ll, i, me_ref: (jnp.where(me_ref[0] == d, (d + 1) % N_DEV, d), rows(l, ll, i), 0))

    def own_spec(l):
        return pl.BlockSpec((None, tr, c), lambda ll, i, me_ref: (me_ref[0], rows(l, ll, i), 0))

    in_specs, operands = [], []
    for l in range(nl):
        in_specs += [part_spec(l, d) for d in range(N_DEV)]
        operands += [parts[l]] * N_DEV
        if owns is not None:
            in_specs.append(own_spec(l))
            operands.append(owns[l])
    blk = pl.BlockSpec((None, tr, c), lambda ll, i, me_ref: (ll, i, 0))
    shp = jax.ShapeDtypeStruct((nl, r, c), F32)
    return pl.pallas_call(
        body, name=name, out_shape=(shp, shp, shp, shp),
        grid_spec=pltpu.PrefetchScalarGridSpec(
            num_scalar_prefetch=1, grid=(nl, nt), in_specs=in_specs + [blk, blk, blk],
            out_specs=(blk, blk, blk, blk)),
        compiler_params=_params(),
    )(me.reshape(1), *operands, w, m, v)


GROUPS = {"in": ("w_in",), "rest": ("w_out", "w_xq", "w_xo", "w_xkv", "w_up", "w_down")}
FULL_SHAPES = {"w_in": (D_MODEL, N_IN_PAD), "w_out": (D_MODEL, D_MODEL), "w_xq": (D_MODEL, D_MODEL),
               "w_xo": (D_MODEL, D_MODEL), "w_xkv": (N_DEV, D_MODEL, 2 * D_MODEL // N_DEV),
               "w_up": (N_DEV, D_MODEL, FF_SHARD), "w_down": (FF_HALF, FF_SHARD, D_MODEL)}
PIECE_SHAPES = {"w_in": (N_DEV, D_MODEL // N_DEV, N_IN_PAD), "w_out": (N_DEV, D_MODEL // N_DEV, D_MODEL),
                "w_xq": (N_DEV, D_MODEL // N_DEV, D_MODEL), "w_xo": (N_DEV, D_MODEL // N_DEV, D_MODEL),
                "w_xkv": (N_DEV, D_MODEL, 2 * D_MODEL // N_DEV), "w_up": (N_DEV, D_MODEL, FF_SHARD),
                "w_down": (N_DEV, D_FF // N_DEV, D_MODEL)}
CONV_WORDS = 8192
SMALL_WORDS = 80 * 1024


class _GatheredWeights:
    def __init__(self, states, layer):
        self.states, self.layer, self.full, self.extra = dict(states), layer, {}, None

    def need(self, group, after):
        if group in self.states:
            got, _ = _exchange_wait(self.states.pop(group), after, "gather_%s_wait_%d" % (group, self.layer))
            for name, g in zip(GROUPS[group], got):
                self.full[name] = g.reshape(FULL_SHAPES[name])
            self.extra = got[len(GROUPS[group]):]

    def __getitem__(self, name):
        return self.full[name]


def _relay_in_cols(w):
    pad = jnp.zeros(w.shape[:-1] + (N_IN_PAD - N_IN,), w.dtype)
    return jnp.concatenate([w[..., :2304], w[..., 2308:N_IN], w[..., 2304:2308], pad], axis=-1)


def _unrelay_in_cols(w):
    return jnp.concatenate([w[..., :2304], w[..., COL_GATE:COL_GATE + 4], w[..., 2304:COL_GATE]], axis=-1)


def _layer_fwd(h, memv, w, sm, tables):
    sv = {"h0": h}
    xn = _rms_fwd(h, sm["g_mix"], "rms_mix")
    w.need("in", xn)
    proj = _mm_nn(xn, w["w_in"], "mm_in", tn=896)
    sv["xn"], sv["proj"] = xn, proj
    ycat = _sconv_fwd(proj, sm["w_sconv"])
    qd, kd, vd, ktd, vtd = _heads_split(proj, 1, tables["rope"], None, "split_dil")
    ycat, ob, lse_b = _attention_fwd("dil", qd, kd, vtd, tables["dil"], ycat, 1)
    sv["dil"] = (qd, kd, vd, ktd, ob, lse_b)
    c = _gate_cumsum(proj, sm["b_forget_pad"])
    qf, kf, vf, ktf, vtf = _heads_split(proj, 2, None, c, "split_fox")
    ycat, oc, lse_c = _attention_fwd("fox", qf, kf, vtf, tables["fox"], ycat, 2)
    sv["fox"] = (qf, kf, vf, ktf, oc, lse_c)
    ycat = _pool_fwd(proj, sm["w_pool_bd"], sm["pool_scale"], ycat)
    sv["ycat"] = ycat
    w.need("rest", ycat)
    h1 = _mm_nn(ycat, w["w_out"], "mm_out", res=h)
    sv["h1"] = h1
    xq = _rms_fwd(h1, sm["g_xa"], "rms_xa")
    memn = _rms_fwd(memv, sm["g_mem"], "rms_mem")
    qx = _mm_nn(xq, w["w_xq"], "mm_xq", out_dtype=BF16)
    kvm = _matmul(memn, w["w_xkv"], (N_DEV, MEM_LEN, XA_DIM), grid=(N_DEV, 1, 1),
                  a_spec=pl.BlockSpec((MEM_LEN, D_MODEL), lambda i, j, r: (0, 0)),
                  b_spec=pl.BlockSpec((None, D_MODEL, XA_DIM), lambda i, j, r: (i, 0, 0)),
                  o_spec=pl.BlockSpec((None, MEM_LEN, XA_DIM), lambda i, j, r: (i, 0, 0)),
                  dims=NN, nred=1, name="mm_xkv")
    ox = _xattn_fwd(qx, kvm)
    sv.update(xq=xq, memn=memn, qx=qx, kvm=kvm, ox=ox)
    h2 = _mm_nn(ox, w["w_xo"], "mm_xo", res=h1)
    sv["h2"] = h2
    xf = _rms_fwd(h2, sm["g_ffn"], "rms_ffn")
    s = h.shape[0]
    tm = min(ROW_TILE, s)
    u0 = _matmul(xf, w["w_up"], (N_DEV, s, FF_SHARD), grid=(s // tm, N_DEV, 1),
                 a_spec=pl.BlockSpec((tm, D_MODEL), lambda i, j, r: (i, 0)),
                 b_spec=pl.BlockSpec((None, D_MODEL, FF_SHARD), lambda i, j, r: (j, 0, 0)),
                 o_spec=pl.BlockSpec((None, tm, FF_SHARD), lambda i, j, r: (j, i, 0)),
                 dims=NN, nred=1, name="mm_up")
    act = _ffn_gate_fwd(u0, sm["w_ffconv"])
    sv.update(xf=xf, u0=u0, act=act)
    ospec = pl.BlockSpec((tm, D_MODEL), lambda i, j, r: (i, 0))
    h3 = _matmul(act, w["w_down"], (s, D_MODEL), grid=(s // tm, 1, 1),
                 a_spec=pl.BlockSpec((FF_HALF, tm, FF_SHARD), lambda i, j, r: (0, i, 0)),
                 b_spec=pl.BlockSpec((FF_HALF, FF_SHARD, D_MODEL), lambda i, j, r: (0, 0, 0)),
                 o_spec=ospec, dims=NN, nred=1, slabs=FF_HALF, name="mm_down", res=h2, res_spec=ospec)
    return h3, sv


def _layer_bwd(dh3, memv, w, sm, tables, sv, rest_ready):
    s = dh3.shape[0]
    tm = min(ROW_TILE, s)
    big, small = {}, {}
    ts = max(s // 2, 1)
    dact = _matmul(dh3, w["w_down"], (FF_HALF, s, FF_SHARD), grid=(s // tm, FF_HALF, 1),
                   a_spec=pl.BlockSpec((tm, D_MODEL), lambda i, j, r: (i, 0)),
                   b_spec=pl.BlockSpec((None, FF_SHARD, D_MODEL), lambda i, j, r: (j, 0, 0)),
                   o_spec=pl.BlockSpec((None, tm, FF_SHARD), lambda i, j, r: (j, i, 0)),
                   dims=NT, nred=1, name="mm_dact")
    big["w_down"] = _matmul(sv["act"], dh3, (FF_HALF, FF_SHARD, D_MODEL), grid=(FF_HALF, 1, s // ts),
                            a_spec=pl.BlockSpec((None, ts, FF_SHARD), lambda i, j, r: (i, r, 0)),
                            b_spec=pl.BlockSpec((ts, D_MODEL), lambda i, j, r: (r, 0)),
                            o_spec=pl.BlockSpec((None, FF_SHARD, D_MODEL), lambda i, j, r: (i, 0, 0)),
                            dims=TN, nred=s // ts, name="mm_dw_down", out_dtype=GRAD_DTYPE)
    du0, small["w_ffconv"] = _ffn_gate_bwd(sv["u0"], sm["w_ffconv"], dact)
    dxf = _matmul(du0, w["w_up"], (s, D_MODEL), grid=(s // tm, 1, 1),
                  a_spec=pl.BlockSpec((N_DEV, tm, FF_SHARD), lambda i, j, r: (0, i, 0)),
                  b_spec=pl.BlockSpec((N_DEV, D_MODEL, FF_SHARD), lambda i, j, r: (0, 0, 0)),
                  o_spec=pl.BlockSpec((tm, D_MODEL), lambda i, j, r: (i, 0)),
                  dims=NT, nred=1, slabs=N_DEV, name="mm_dxf")
    big["w_up"] = _matmul(sv["xf"], du0, (N_DEV, D_MODEL, FF_SHARD), grid=(N_DEV, 1, 1),
                          a_spec=pl.BlockSpec((s, D_MODEL), lambda i, j, r: (0, 0)),
                          b_spec=pl.BlockSpec((None, s, FF_SHARD), lambda i, j, r: (i, 0, 0)),
                          o_spec=pl.BlockSpec((None, D_MODEL, FF_SHARD), lambda i, j, r: (i, 0, 0)),
                          dims=TN, nred=1, name="mm_dw_up", out_dtype=GRAD_DTYPE)
    dh2, small["g_ffn"] = _rms_bwd(dxf, sv["h2"], sm["g_ffn"], dh3, "rms_ffn_bwd")
    dox = _mm_nt(dh2, w["w_xo"], "mm_dox", out_dtype=BF16)
    big["w_xo"] = _mm_tn(sv["ox"], dh2, "mm_dw_xo")
    dqx, dkvm = _xattn_bwd(sv["qx"], sv["kvm"], dox)
    dxq = _mm_nt(dqx, w["w_xq"], "mm_dxq")
    big["w_xq"] = _mm_tn(sv["xq"], dqx, "mm_dw_xq")
    big["w_xkv"] = _matmul(sv["memn"], dkvm, (N_DEV, D_MODEL, XA_DIM), grid=(N_DEV, 1, 1),
                           a_spec=pl.BlockSpec((MEM_LEN, D_MODEL), lambda i, j, r: (0, 0)),
                           b_spec=pl.BlockSpec((None, MEM_LEN, XA_DIM), lambda i, j, r: (i, 0, 0)),
                           o_spec=pl.BlockSpec((None, D_MODEL, XA_DIM), lambda i, j, r: (i, 0, 0)),
                           dims=TN, nred=1, name="mm_dw_xkv", out_dtype=GRAD_DTYPE)
    dmemn = _matmul(dkvm, w["w_xkv"], (MEM_LEN, D_MODEL), grid=(1, 1, 1),
                    a_spec=pl.BlockSpec((N_DEV, MEM_LEN, XA_DIM), lambda i, j, r: (0, 0, 0)),
                    b_spec=pl.BlockSpec((N_DEV, D_MODEL, XA_DIM), lambda i, j, r: (0, 0, 0)),
                    o_spec=pl.BlockSpec((MEM_LEN, D_MODEL), lambda i, j, r: (0, 0)),
                    dims=NT, nred=1, slabs=N_DEV, name="mm_dmemn")
    _, small["g_mem"] = _rms_bwd(dmemn, memv, sm["g_mem"], None, "rms_mem_bwd")
    dh1, small["g_xa"] = _rms_bwd(dxq, sv["h1"], sm["g_xa"], dh2, "rms_xa_bwd")
    dycat = _mm_nt(dh1, w["w_out"], "mm_dycat")
    big["w_out"] = _mm_tn(sv["ycat"], dh1, "mm_dw_out")
    proj = sv["proj"]
    dproj, small["w_sconv"] = _sconv_bwd(proj, sm["w_sconv"] + rest_ready(big), dycat)
    qd, kd, vd, ktd, ob, lse_b = sv["dil"]
    delta, dob = _attention_delta(ob, dycat, 1)
    dqt, dk, dv = _attention_bwd("dil", qd, kd, vd, ktd, tables["dil"], dob, lse_b, delta)
    dproj = _heads_merge(dqt, dk, dv, tables["rope"], "merge_dil", dproj, 1)
    qf, kf, vf, ktf, oc, lse_c = sv["fox"]
    delta, dob = _attention_delta(oc, dycat, 2)
    dqt, dk, dv = _attention_bwd("fox", qf, kf, vf, ktf, tables["fox"], dob, lse_c, delta)
    dproj, dc = _heads_merge(dqt, dk, dv, None, "merge_fox", dproj, 2)
    dproj, dbias = _gate_cumsum_bwd(proj, sm["b_forget_pad"], dc, dproj)
    small["b_forget"] = dbias[0, :N_HEADS]
    dproj, dwbd, small["pool_scale"] = _pool_bwd(proj, sm["w_pool_bd"], sm["pool_scale"], dycat, dproj)
    small["w_pool"] = jnp.stack([dwbd[64 * g:64 * (g + 1), 64 * g:64 * (g + 1)] for g in range(4)])
    dxn = _mm_nt(dproj, w["w_in"], "mm_dxn")
    big["w_in"] = _mm_tn(sv["xn"], dproj, "mm_dw_in", tn=896)
    dh0, small["g_mix"] = _rms_bwd(dxn, sv["h0"], sm["g_mix"], dh1, "rms_mix_bwd")
    return dh0, big, small


SMALL_NAMES = ("g_mix", "b_forget", "w_pool", "pool_scale", "g_xa", "g_mem", "g_ffn", "w_sconv", "w_ffconv")
WEIGHT_NAMES = ("g_mix", "w_in", "b_forget", "w_sconv", "w_pool", "pool_scale", "w_out", "g_xa", "g_mem", "w_xq",
                "w_xkv", "w_xo", "g_ffn", "w_up", "w_ffconv", "w_down", "g_final")


def _block_diag(w_pool):
    z = jnp.zeros((64, 64), F32)
    return jnp.concatenate(
        [jnp.concatenate([w_pool[g] if c == g else z for c in range(4)], axis=1) for g in range(4)], axis=0)


def kernel(x, mem, positions, g_mix, w_in, b_forget, w_sconv, w_pool, pool_scale, w_out, g_xa, g_mem, w_xq, w_xkv, w_xo, g_ffn, w_up, w_ffconv, w_down, g_final, loss_target, m_g_mix, m_w_in, m_b_forget, m_w_sconv, m_w_pool, m_pool_scale, m_w_out, m_g_xa, m_g_mem, m_w_xq, m_w_xkv, m_w_xo, m_g_ffn, m_w_up, m_w_ffconv, m_w_down, m_g_final, v_g_mix, v_w_in, v_b_forget, v_w_sconv, v_w_pool, v_pool_scale, v_w_out, v_g_xa, v_g_mem, v_w_xq, v_w_xkv, v_w_xo, v_g_ffn, v_w_up, v_w_ffconv, v_w_down, v_g_final):
    weights = dict(g_mix=g_mix, w_in=w_in, b_forget=b_forget, w_sconv=w_sconv, w_pool=w_pool, pool_scale=pool_scale,
                   w_out=w_out, g_xa=g_xa, g_mem=g_mem, w_xq=w_xq, w_xkv=w_xkv, w_xo=w_xo, g_ffn=g_ffn, w_up=w_up,
                   w_ffconv=w_ffconv, w_down=w_down, g_final=g_final)
    m_in = dict(g_mix=m_g_mix, w_in=m_w_in, b_forget=m_b_forget, w_sconv=m_w_sconv, w_pool=m_w_pool,
                pool_scale=m_pool_scale, w_out=m_w_out, g_xa=m_g_xa, g_mem=m_g_mem, w_xq=m_w_xq, w_xkv=m_w_xkv,
                w_xo=m_w_xo, g_ffn=m_g_ffn, w_up=m_w_up, w_ffconv=m_w_ffconv, w_down=m_w_down, g_final=m_g_final)
    v_in = dict(g_mix=v_g_mix, w_in=v_w_in, b_forget=v_b_forget, w_sconv=v_w_sconv, w_pool=v_w_pool,
                pool_scale=v_pool_scale, w_out=v_w_out, g_xa=v_g_xa, g_mem=v_g_mem, w_xq=v_w_xq, w_xkv=v_w_xkv,
                w_xo=v_w_xo, g_ffn=v_g_ffn, w_up=v_w_up, w_ffconv=v_w_ffconv, w_down=v_w_down, g_final=v_g_final)
    depth = w_in.shape[0]
    me = 4 * lax.axis_index("x") + 2 * lax.axis_index("y") + lax.axis_index("c")
    h = x[0]
    memv = mem[0]
    s = h.shape[0]
    tq = min(ATT_TQ, s)
    tables = {"rope": _rope_tables(positions[0]), "dil": _bias_tables("dil", tq, tq),
              "fox": _bias_tables("fox", tq, tq)}

    w_in_r = _relay_in_cols(w_in)
    conv_shard = jnp.concatenate([w_sconv.reshape(-1), w_ffconv.reshape(-1)])
    conv_shard = jnp.concatenate([conv_shard, jnp.zeros((CONV_WORDS - conv_shard.shape[0],), F32)])
    conv_bits = lax.bitcast_convert_type(conv_shard, BF16).reshape(2 * CONV_WORDS // 1024, 1024)
    gathered = []
    order = jnp.zeros((), F32)
    for l in range(depth):
        shards = dict(w_in=w_in_r[l], w_out=w_out[l], w_xq=w_xq[l], w_xo=w_xo[l], w_xkv=w_xkv[l], w_up=w_up[l],
                      w_down=w_down[l])
        states = {}
        for group in ("in", "rest"):
            shards[GROUPS[group][0]] = shards[GROUPS[group][0]] + order
            xs = [_place_shard(shards[name], me, BF16, "place_%s_%d" % (name, l)) for name in GROUPS[group]]
            if l == 0 and group == "in":
                xs.append(_place_shard(conv_bits, me, BF16, "place_conv"))
            states[group], token = _exchange_start(xs, False, "gather_%s_start_%d" % (group, l))
            order = order + token[0, 0]
        gathered.append(_GatheredWeights(states, l))
    gathered[0].need("in", tables["rope"][0])
    conv_all = lax.bitcast_convert_type(gathered[0].extra[0].reshape(N_DEV, CONV_WORDS, 2), F32)
    n_sc = depth * 3 * (GROUP // N_DEV)
    sconv_full = conv_all[:, :n_sc].reshape(N_DEV, depth, 3, GROUP // N_DEV).transpose(1, 2, 0, 3).reshape(
        depth, 3, GROUP)
    ffconv_full = conv_all[:, n_sc:n_sc + depth * 3 * FF_SHARD].reshape(N_DEV, depth, 3, FF_SHARD).transpose(
        1, 0, 2, 3)

    smalls = []
    for l in range(depth):
        smalls.append(dict(
            g_mix=g_mix[l], g_xa=g_xa[l], g_mem=g_mem[l], g_ffn=g_ffn[l], pool_scale=pool_scale[l],
            w_pool_bd=_block_diag(w_pool[l]), w_sconv=sconv_full[l], w_ffconv=ffconv_full[l],
            b_forget_pad=jnp.concatenate([b_forget[l], jnp.zeros((128 - N_HEADS,), F32)]).reshape(1, 128)))
    smalls[0]["g_mix"] = smalls[0]["g_mix"] + order

    saved = []
    for l in range(depth):
        h, sv = _layer_fwd(h, memv, gathered[l], smalls[l], tables)
        saved.append(sv)
    loss_part, dh, dg_final = _loss_head(h, g_final, loss_target[0])
    loss = lax.psum(loss_part[0, 0], MESH_AXES)

    small_grads = [None] * depth
    scatters = {}

    def pieces_of(big, group):
        return [big[name].reshape(PIECE_SHAPES[name]) for name in GROUPS[group]]

    for l in reversed(range(depth)):
        def rest_ready(big, l=l):
            scatters[l, "rest"], token = _exchange_start(pieces_of(big, "rest"), True, "scatter_rest_start_%d" % l)
            return token[0, 0]

        dh, big, small_grads[l] = _layer_bwd(dh, memv, gathered[l], smalls[l], tables, saved[l], rest_ready)
        xs = pieces_of(big, "in")
        if l == 0:
            flat = [small_grads[ll][n].reshape(-1) for n in SMALL_NAMES for ll in range(depth)]
            flat = jnp.concatenate(flat + [dg_final.reshape(-1)])
            flat = jnp.concatenate([flat, jnp.zeros((SMALL_WORDS - flat.shape[0],), F32)])
            xs.append(jnp.broadcast_to(flat.reshape(1, -1, 1024), (N_DEV, SMALL_WORDS // 1024, 1024)))
        scatters[l, "in"], token = _exchange_start(xs, True, "scatter_in_start_%d" % l)
        if l > 0:
            smalls[l - 1]["w_ffconv"] = smalls[l - 1]["w_ffconv"] + token[0, 0]
    grad_x = dh[None]

    parts, owns = {}, {}

    def wait_group(group, after):
        extra = None
        for l in reversed(range(depth)):
            got, given = _exchange_wait(scatters[l, group], after, "scatter_%s_wait_%d" % (group, l))
            for name, g, x in zip(GROUPS[group], got, given):
                parts.setdefault(name, [None] * depth)[l] = g
                owns.setdefault(name, [None] * depth)[l] = x
            extra = (got[len(GROUPS[group]):], given[len(GROUPS[group]):])
        return extra

    results = {}

    def update(name, w3, m3, v3):
        outs = _adamw(parts[name], owns.get(name), me, w3, m3, v3, "adamw_" + name)
        results[name] = [o.reshape(weights[name].shape) for o in outs]

    wait_group("rest", grad_x)
    for name in GROUPS["rest"]:
        update(name, weights[name], m_in[name], v_in[name])
    small_got, small_given = wait_group("in", results["w_down"][1])
    small_all = lax.dynamic_update_slice_in_dim(small_got[0], small_given[0][:1], me, axis=0).reshape(N_DEV, -1)
    outs = _adamw(parts["w_in"], owns["w_in"], me, w_in_r, _relay_in_cols(m_w_in), _relay_in_cols(v_w_in),
                  "adamw_w_in")
    results["w_in"] = [_unrelay_in_cols(o) for o in outs]
    off = 0
    for name in SMALL_NAMES + ("g_final",):
        wv = weights[name]
        full_shape = {"w_sconv": (depth, 3, GROUP), "w_ffconv": (depth, N_DEV, 3, FF_SHARD)}.get(name, wv.shape)
        n = 1
        for dim in full_shape:
            n *= dim
        p = small_all[:, off:off + n].reshape((N_DEV,) + tuple(full_shape))
        off += n
        if name == "w_sconv":
            p = lax.dynamic_slice_in_dim(p, me * (GROUP // N_DEV), GROUP // N_DEV, axis=3)
        elif name == "w_ffconv":
            p = lax.dynamic_index_in_dim(p, me, axis=2, keepdims=False)
        shape3 = (1, 1, wv.shape[0]) if wv.ndim == 1 else (1, -1, wv.shape[-1])
        w3 = wv.reshape(shape3)
        parts[name] = [p.reshape((N_DEV,) + w3.shape[1:])]
        update(name, w3, m_in[name].reshape(shape3), v_in[name].reshape(shape3))

    return (loss, grad_x, *[results[n][0] for n in WEIGHT_NAMES], *[results[n][1] for n in WEIGHT_NAMES],
            *[results[n][2] for n in WEIGHT_NAMES], *[results[n][3] for n in WEIGHT_NAMES])
```

```python
import functools

import jax
import jax.numpy as jnp
from jax import lax
from jax.experimental import pallas as pl
from jax.experimental.pallas import tpu as pltpu

F32 = jnp.float32
BF16 = jnp.bfloat16

N_DEV = 8
D_MODEL = 1024
GROUP = 256
HEAD_DIM = 64
N_HEADS = 4
N_IN = 2564
N_IN_PAD = 2688
COL_GATE = 2560
XA_HEADS = 4
XA_DIM = 256
MEM_LEN = 256
D_FF = 2816
FF_SHARD = 704
FF_HALF = 4
ROPE_THETA = 500000.0
ROPE_DIM = 16
RMS_EPS = 1e-6
NEG = -1e30
POOL_WINDOWS = (2, 4, 8, 16)
ADAM_LR, ADAM_B1, ADAM_B2, ADAM_EPS, ADAM_WD, ADAM_STEP = 0.001, 0.9, 0.999, 1e-08, 0.01, 10

ROW_TILE = 512
ATT_TQ = 256
BWD_HEADS = 4
VMEM_LIMIT = 56 * 1024 * 1024
ADAMW_BLOCK_BYTES = 4 * 1024 * 1024
PLACE_BLOCK_BYTES = 4 * 1024 * 1024

MESH_AXES = ("x", "y", "c")


def _params(**kw):
    return pltpu.CompilerParams(vmem_limit_bytes=VMEM_LIMIT, **kw)


HBM_SPEC = pl.BlockSpec(memory_space=pltpu.HBM)
SEM_SPEC = pl.BlockSpec(memory_space=pltpu.SEMAPHORE)
DATAFLOW = pltpu.SideEffectType.DATAFLOW_SIDE_EFFECTING


def _peer_copies(x_ref, land_ref, send_sems, recv_sems, scatter):
    mx, my, mc = lax.axis_index("x"), lax.axis_index("y"), lax.axis_index("c")
    me = 4 * mx + 2 * my + mc
    pairs = []
    for k in range(1, N_DEV):
        kx, ky, kc = (k >> 2) & 1, (k >> 1) & 1, k & 1
        peer_lin = me ^ k
        send = pltpu.make_async_remote_copy(
            src_ref=x_ref.at[peer_lin] if scatter else land_ref.at[me], dst_ref=land_ref.at[me],
            send_sem=send_sems.at[k - 1], recv_sem=recv_sems.at[k - 1],
            device_id=(mx ^ kx, my ^ ky, mc ^ kc), device_id_type=pl.DeviceIdType.MESH)
        arrival = pltpu.make_async_remote_copy(
            src_ref=land_ref.at[peer_lin], dst_ref=land_ref.at[peer_lin],
            send_sem=send_sems.at[k - 1], recv_sem=recv_sems.at[k - 1],
            device_id=(mx, my, mc), device_id_type=pl.DeviceIdType.MESH)
        pairs.append((send, arrival))
    return pairs


def _exchange_start(xs, scatter, name):
    n = len(xs)
    ns = n if scatter else 0

    def body(*refs):
        srcs = refs[:ns] if scatter else (None,) * n
        lands, sends, recvs = refs[ns:ns + n], refs[ns + n:ns + 2 * n], refs[ns + 2 * n:ns + 3 * n]
        for t in range(n):
            for send, _ in _peer_copies(srcs[t], lands[t], sends[t], recvs[t], scatter):
                send.start()
        token = refs[-1]
        token[...] = jnp.zeros_like(token)

    sems = pltpu.SemaphoreType.DMA((N_DEV - 1,))
    operands = [pltpu.with_memory_space_constraint(x, pltpu.HBM) for x in xs]
    if scatter:
        operands += [pltpu.with_memory_space_constraint(lax.empty(x.shape, x.dtype), pltpu.HBM) for x in xs]
    outs = pl.pallas_call(
        body, name=name,
        out_shape=(sems,) * (2 * n) + tuple(pltpu.HBM(a.shape, a.dtype) for a in operands)
        + (jax.ShapeDtypeStruct((8, 128), F32),),
        in_specs=(HBM_SPEC,) * (ns + n),
        out_specs=(SEM_SPEC,) * (2 * n) + (HBM_SPEC,) * (ns + n) + (pl.BlockSpec(memory_space=pltpu.VMEM),),
        input_output_aliases={i: 2 * n + i for i in range(ns + n)},
        compiler_params=pltpu.CompilerParams(has_side_effects=DATAFLOW),
    )(*operands)
    return (outs[:-1], scatter), outs[-1]


def _exchange_wait(state, after, name):
    held, scatter = state
    n = len(held) // (4 if scatter else 3)
    ns = n if scatter else 0
    sems, thru = held[:2 * n], held[2 * n:]

    def body(*refs):
        srcs = refs[:ns] if scatter else (None,) * n
        lands, sends, recvs = refs[ns:ns + n], refs[ns + n:ns + 2 * n], refs[ns + 2 * n:ns + 3 * n]
        for t in range(n):
            for send, arrival in _peer_copies(srcs[t], lands[t], sends[t], recvs[t], scatter):
                send.wait_send()
                arrival.wait_recv()

    outs = pl.pallas_call(
        body, name=name,
        out_shape=tuple(pltpu.HBM(a.shape, a.dtype) for a in thru),
        in_specs=(HBM_SPEC,) * (ns + n) + (SEM_SPEC,) * (2 * n) + (pl.BlockSpec(memory_space=pl.ANY),),
        out_specs=(HBM_SPEC,) * (ns + n), input_output_aliases={i: i for i in range(ns + n)},
        compiler_params=pltpu.CompilerParams(has_side_effects=DATAFLOW),
    )(*thru, *sems, after)
    return list(outs[ns:]), list(outs[:ns])


def _place_shard(x, me, dtype, name):
    r, c = x.shape
    tr = r
    if r * c * 4 > PLACE_BLOCK_BYTES:
        for cand in (512, 256, 128, 64, 32, 16):
            if r % cand == 0 and cand * c * 4 <= PLACE_BLOCK_BYTES:
                tr = cand
                break

    def body(me_ref, x_ref, o_ref):
        o_ref[...] = x_ref[...].astype(o_ref.dtype)

    return pl.pallas_call(
        body, name=name, out_shape=jax.ShapeDtypeStruct((N_DEV, r, c), dtype),
        grid_spec=pltpu.PrefetchScalarGridSpec(
            num_scalar_prefetch=1, grid=(r // tr,),
            in_specs=[pl.BlockSpec((tr, c), lambda i, me_ref: (i, 0))],
            out_specs=pl.BlockSpec((None, tr, c), lambda i, me_ref: (me_ref[0], i, 0))),
        compiler_params=_params(),
    )(me.reshape(1), x)


NN = ((1,), (0,))
NT = ((1,), (1,))
TN = ((0,), (0,))


def _matmul(a, b, out_shape, *, grid, a_spec, b_spec, o_spec, dims, nred, name, res=None, res_spec=None,
            out_dtype=F32, slabs=0):
    has_res = res is not None

    def body(*refs):
        a_ref, b_ref = refs[0], refs[1]
        r_ref = refs[2] if has_res else None
        o_ref = refs[3] if has_res else refs[2]
        if slabs:
            part = None
            for n in range(slabs):
                term = lax.dot_general(a_ref[n].astype(BF16), b_ref[n].astype(BF16), (dims, ((), ())),
                                       preferred_element_type=F32)
                part = term if part is None else part + term
        else:
            part = lax.dot_general(a_ref[...].astype(BF16), b_ref[...].astype(BF16), (dims, ((), ())),
                                   preferred_element_type=F32)
        if nred == 1:
            if has_res:
                part = part + r_ref[...]
            o_ref[...] = part.astype(o_ref.dtype)
        else:
            acc = refs[-1]
            r = pl.program_id(2)

            @pl.when(r == 0)
            def _():
                acc[...] = part

            @pl.when(r > 0)
            def _():
                acc[...] += part

            @pl.when(r == nred - 1)
            def _():
                tot = acc[...]
                if has_res:
                    tot = tot + r_ref[...]
                o_ref[...] = tot.astype(o_ref.dtype)

    in_specs = [a_spec, b_spec] + ([res_spec] if has_res else [])
    args = (a, b) + ((res,) if has_res else ())
    acc_shape = tuple(d for d in o_spec.block_shape if d is not None)
    return pl.pallas_call(
        body, name=name, grid=grid, out_shape=jax.ShapeDtypeStruct(out_shape, out_dtype),
        in_specs=in_specs, out_specs=o_spec,
        scratch_shapes=[pltpu.VMEM(acc_shape, F32)] if nred > 1 else [],
        compiler_params=_params(),
    )(*args)


def _mm_nn(a, w, name, res=None, tn=None, out_dtype=F32):
    m, k = a.shape
    n = w.shape[1]
    tn = tn or n
    tm = min(ROW_TILE, m)
    ospec = pl.BlockSpec((tm, tn), lambda i, j, r: (i, j))
    return _matmul(a, w, (m, n), grid=(m // tm, n // tn, 1),
                   a_spec=pl.BlockSpec((tm, k), lambda i, j, r: (i, 0)),
                   b_spec=pl.BlockSpec((k, tn), lambda i, j, r: (0, j)),
                   o_spec=ospec, dims=NN, nred=1, name=name, res=res, res_spec=ospec if res is not None else None,
                   out_dtype=out_dtype)


def _mm_nt(a, w, name, out_dtype=F32):
    m, n = a.shape
    k = w.shape[0]
    tm = min(ROW_TILE, m)
    return _matmul(a, w, (m, k), grid=(m // tm, 1, 1),
                   a_spec=pl.BlockSpec((tm, n), lambda i, j, r: (i, 0)),
                   b_spec=pl.BlockSpec((k, n), lambda i, j, r: (0, 0)),
                   o_spec=pl.BlockSpec((tm, k), lambda i, j, r: (i, 0)), dims=NT, nred=1, name=name,
                   out_dtype=out_dtype)


GRAD_DTYPE = BF16


def _mm_tn(a, b, name, tk=512, tn=None):
    s, k = a.shape
    n = b.shape[1]
    tn = tn or n
    tk = min(tk, k)
    ts = s if b.dtype == BF16 else max(s // 2, 1)
    return _matmul(a, b, (k, n), grid=(k // tk, n // tn, s // ts),
                   a_spec=pl.BlockSpec((ts, tk), lambda i, j, r: (r, i)),
                   b_spec=pl.BlockSpec((ts, tn), lambda i, j, r: (r, j)),
                   o_spec=pl.BlockSpec((tk, tn), lambda i, j, r: (i, j)), dims=TN, nred=s // ts, name=name,
                   out_dtype=GRAD_DTYPE)


def _rms_fwd(h, g, name):
    s, d = h.shape
    tm = min(ROW_TILE, s)

    def body(h_ref, g_ref, o_ref):
        hv = h_ref[...]
        r = lax.rsqrt(jnp.mean(hv * hv, axis=-1, keepdims=True) + RMS_EPS)
        o_ref[...] = (hv * r * g_ref[...]).astype(o_ref.dtype)

    return pl.pallas_call(
        body, name=name, grid=(s // tm,), out_shape=jax.ShapeDtypeStruct((s, d), BF16),
        in_specs=[pl.BlockSpec((tm, d), lambda i: (i, 0)), pl.BlockSpec((1, d), lambda i: (0, 0))],
        out_specs=pl.BlockSpec((tm, d), lambda i: (i, 0)), compiler_params=_params(),
    )(h, g.reshape(1, d))


def _rms_bwd(dy, h, g, res, name):
    s, d = h.shape
    tm = min(ROW_TILE, s)
    has_res = res is not None

    def body(*refs):
        dy_ref, h_ref, g_ref = refs[:3]
        r_ref = refs[3] if has_res else None
        dh_ref, dg_ref = refs[-2], refs[-1]
        hv = h_ref[...]
        r = lax.rsqrt(jnp.mean(hv * hv, axis=-1, keepdims=True) + RMS_EPS)
        hn = hv * r
        dyv = dy_ref[...].astype(F32)
        u = dyv * g_ref[...]
        dh = r * (u - hn * jnp.mean(u * hn, axis=-1, keepdims=True))
        if has_res:
            dh = dh + r_ref[...]
        dh_ref[...] = dh
        part = jnp.sum(dyv * hn, axis=0, keepdims=True)

        @pl.when(pl.program_id(0) == 0)
        def _():
            dg_ref[...] = part

        @pl.when(pl.program_id(0) > 0)
        def _():
            dg_ref[...] += part

    row = pl.BlockSpec((tm, d), lambda i: (i, 0))
    vec = pl.BlockSpec((1, d), lambda i: (0, 0))
    dh, dg = pl.pallas_call(
        body, name=name, grid=(s // tm,),
        out_shape=(jax.ShapeDtypeStruct((s, d), F32), jax.ShapeDtypeStruct((1, d), F32)),
        in_specs=[row, row, vec] + ([row] if has_res else []),
        out_specs=(row, vec), compiler_params=_params(),
    )(*((dy, h, g.reshape(1, d)) + ((res,) if has_res else ())))
    return dh, dg.reshape(d)


def _loss_head(h, g, target):
    s, d = h.shape
    tm = min(ROW_TILE, s)

    def body(h_ref, g_ref, t_ref, loss_ref, dh_ref, dg_ref):
        hv = h_ref[...]
        r = lax.rsqrt(jnp.mean(hv * hv, axis=-1, keepdims=True) + RMS_EPS)
        hn = hv * r
        gv = g_ref[...]
        err = hn * gv - t_ref[...]
        rows = jnp.mean(err * err, axis=-1, keepdims=True)
        lpart = 0.5 * jnp.sum(rows, axis=0, keepdims=True) + jnp.zeros((1, 128), F32)
        dy = err * (1.0 / d)
        u = dy * gv
        dh_ref[...] = r * (u - hn * jnp.mean(u * hn, axis=-1, keepdims=True))
        gpart = jnp.sum(dy * hn, axis=0, keepdims=True)

        @pl.when(pl.program_id(0) == 0)
        def _():
            dg_ref[...] = gpart
            loss_ref[...] = lpart

        @pl.when(pl.program_id(0) > 0)
        def _():
            dg_ref[...] += gpart
            loss_ref[...] += lpart

    row = pl.BlockSpec((tm, d), lambda i: (i, 0))
    vec = pl.BlockSpec((1, d), lambda i: (0, 0))
    return pl.pallas_call(
        body, name="loss_head", grid=(s // tm,),
        out_shape=(jax.ShapeDtypeStruct((1, 128), F32), jax.ShapeDtypeStruct((s, d), F32),
                   jax.ShapeDtypeStruct((1, d), F32)),
        in_specs=[row, vec, row],
        out_specs=(pl.BlockSpec((1, 128), lambda i: (0, 0)), row, vec), compiler_params=_params(),
    )(h, g.reshape(1, d), target)


def _shift_down(x, k):
    return pltpu.roll(x, k, 0)


def _shift_up(x, k):
    return pltpu.roll(x, x.shape[0] - k, 0)


def _conv3(x, w):
    return w[2:3, :] * x + w[1:2, :] * _shift_down(x, 1) + w[0:1, :] * _shift_down(x, 2)


def _conv3_t(x, w):
    return w[2:3, :] * x + w[1:2, :] * _shift_up(x, 1) + w[0:1, :] * _shift_up(x, 2)


def _sigmoid(x):
    return 1.0 / (1.0 + jnp.exp(-x))


def _prev_map(tile, halo, col):
    return lambda i: (jnp.maximum(i * (tile // halo) - 1, 0), col)


def _next_map(tile, halo, col, nrows):
    return lambda i: (jnp.minimum((i + 1) * (tile // halo), nrows // halo - 1), col)


def _sconv_fwd(proj, w):
    s = proj.shape[0]
    t = min(ROW_TILE, s)

    def body(cur_ref, prev_ref, w_ref, o_ref):
        i = pl.program_id(0)
        prev = prev_ref[...] * (i > 0).astype(F32)
        ext = jnp.concatenate([prev, cur_ref[...]], axis=0)
        sv = ext[:, 2 * GROUP:3 * GROUP] * ext[:, 0:GROUP]
        y = ext[:, GROUP:2 * GROUP] * _conv3(sv, w_ref[...])
        o_ref[...] = y[8:].astype(o_ref.dtype)

    return pl.pallas_call(
        body, name="sconv_fwd", grid=(s // t,), out_shape=jax.ShapeDtypeStruct((s, 4 * GROUP), BF16),
        in_specs=[pl.BlockSpec((t, 3 * GROUP), lambda i: (i, 0)),
                  pl.BlockSpec((8, 3 * GROUP), _prev_map(t, 8, 0)),
                  pl.BlockSpec((3, GROUP), lambda i: (0, 0))],
        out_specs=pl.BlockSpec((t, GROUP), lambda i: (i, 0)), compiler_params=_params(),
    )(proj, proj, w)


def _sconv_bwd(proj, w, dy):
    s = proj.shape[0]
    t = min(ROW_TILE, s)
    nt = s // t

    def body(cur_ref, prev_ref, next_ref, w_ref, dy_ref, dyn_ref, dp_ref, dw_ref):
        i = pl.program_id(0)
        first = (i > 0).astype(F32)
        last = (i < nt - 1).astype(F32)
        ext = jnp.concatenate([prev_ref[...] * first, cur_ref[...], next_ref[...] * last], axis=0)
        dye = jnp.concatenate([jnp.zeros((8, GROUP), F32), dy_ref[...], dyn_ref[...] * last], axis=0)
        hv, bv, cv = ext[:, 0:GROUP], ext[:, GROUP:2 * GROUP], ext[:, 2 * GROUP:3 * GROUP]
        wv = w_ref[...]
        sv = cv * hv
        conv = _conv3(sv, wv)
        dconv = dye * bv
        ds = _conv3_t(dconv, wv)
        dp = jnp.concatenate([ds * cv, dye * conv, ds * hv], axis=1)
        dp_ref[...] = dp[8:8 + t].astype(dp_ref.dtype)
        dc = dconv[8:8 + t]
        dw = jnp.concatenate([
            jnp.sum(dc * _shift_down(sv, 2)[8:8 + t], axis=0, keepdims=True),
            jnp.sum(dc * _shift_down(sv, 1)[8:8 + t], axis=0, keepdims=True),
            jnp.sum(dc * sv[8:8 + t], axis=0, keepdims=True),
            jnp.zeros((5, GROUP), F32)], axis=0)

        @pl.when(i == 0)
        def _():
            dw_ref[...] = dw

        @pl.when(i > 0)
        def _():
            dw_ref[...] += dw

    dp, dw = pl.pallas_call(
        body, name="sconv_bwd", grid=(nt,),
        out_shape=(jax.ShapeDtypeStruct((s, N_IN_PAD), BF16), jax.ShapeDtypeStruct((8, GROUP), F32)),
        in_specs=[pl.BlockSpec((t, 3 * GROUP), lambda i: (i, 0)),
                  pl.BlockSpec((8, 3 * GROUP), _prev_map(t, 8, 0)),
                  pl.BlockSpec((8, 3 * GROUP), _next_map(t, 8, 0, s)),
                  pl.BlockSpec((3, GROUP), lambda i: (0, 0)),
                  pl.BlockSpec((t, GROUP), lambda i: (i, 0)),
                  pl.BlockSpec((8, GROUP), _next_map(t, 8, 0, s))],
        out_specs=(pl.BlockSpec((t, 3 * GROUP), lambda i: (i, 0)), pl.BlockSpec((8, GROUP), lambda i: (0, 0))),
        compiler_params=_params(),
    )(proj, proj, proj, w, dy, dy)
    return dp, dw[:3]


def _lane_window(shape):
    lane = lax.broadcasted_iota(jnp.int32, shape, 1)
    return lane, jnp.where(lane < 64, 2.0, jnp.where(lane < 128, 4.0, jnp.where(lane < 192, 8.0, 16.0)))


def _by_group(lane, s1, s2, s3, s4):
    return jnp.where(lane < 64, s1, jnp.where(lane < 128, s2, jnp.where(lane < 192, s3, s4)))


def _pool_z(ext, row0):
    s1 = ext + _shift_down(ext, 1)
    s2 = s1 + _shift_down(s1, 2)
    s3 = s2 + _shift_down(s2, 4)
    s4 = s3 + _shift_down(s3, 8)
    lane, win = _lane_window(ext.shape)
    tpos = (lax.broadcasted_iota(jnp.int32, ext.shape, 0) + (row0 - 16 + 1)).astype(F32)
    cnt = jnp.maximum(jnp.minimum(tpos, win), 1.0)
    return _by_group(lane, s1, s2, s3, s4) / cnt - ext


ANY_SPEC = pl.BlockSpec(memory_space=pl.ANY)


def _pool_fwd(proj, wbd, scale, ybuf):
    s = proj.shape[0]
    t = min(ROW_TILE, s)
    col = (COL_GATE - GROUP) // GROUP

    def body(cur_ref, prev_ref, w_ref, sc_ref, buf_ref, o_ref):
        i = pl.program_id(0)
        ext = jnp.concatenate([prev_ref[...] * (i > 0).astype(F32), cur_ref[...]], axis=0)
        z = _pool_z(ext, i * t)[16:]
        y = jnp.dot(z.astype(BF16), w_ref[...].astype(BF16), preferred_element_type=F32)
        o_ref[...] = (y * sc_ref[...]).astype(o_ref.dtype)

    return pl.pallas_call(
        body, name="pool_fwd", grid=(s // t,), out_shape=jax.ShapeDtypeStruct(ybuf.shape, ybuf.dtype),
        in_specs=[pl.BlockSpec((t, GROUP), lambda i: (i, col)),
                  pl.BlockSpec((16, GROUP), _prev_map(t, 16, col)),
                  pl.BlockSpec((GROUP, GROUP), lambda i: (0, 0)),
                  pl.BlockSpec((1, GROUP), lambda i: (0, 0)), ANY_SPEC],
        out_specs=pl.BlockSpec((t, GROUP), lambda i: (i, 3)), input_output_aliases={4: 0},
        compiler_params=_params(),
    )(proj, proj, wbd, scale.reshape(1, GROUP), ybuf)


def _pool_bwd(proj, wbd, scale, dy, dbuf):
    s = proj.shape[0]
    t = min(ROW_TILE, s)
    nt = s // t
    col = (COL_GATE - GROUP) // GROUP

    def body(cur_ref, prev_ref, w_ref, sc_ref, dy_ref, dyn_ref, buf_ref, dp_ref, dw_ref, dsc_ref):
        i = pl.program_id(0)
        ext = jnp.concatenate([prev_ref[...] * (i > 0).astype(F32), cur_ref[...]], axis=0)
        z = _pool_z(ext, i * t)[16:]
        wv = w_ref[...].astype(BF16)
        dyc = dy_ref[...]
        dye = jnp.concatenate([dyc, dyn_ref[...] * (i < nt - 1).astype(F32)], axis=0) * sc_ref[...]
        dz = lax.dot_general(dye.astype(BF16), wv, (NT, ((), ())), preferred_element_type=F32)
        lane, win = _lane_window(dz.shape)
        tpos = (lax.broadcasted_iota(jnp.int32, dz.shape, 0) + (i * t + 1)).astype(F32)
        e = dz / jnp.minimum(tpos, win)
        f1 = e + _shift_up(e, 1)
        f2 = f1 + _shift_up(f1, 2)
        f3 = f2 + _shift_up(f2, 4)
        f4 = f3 + _shift_up(f3, 8)
        dp = _by_group(lane, f1, f2, f3, f4) - dz
        dp_ref[...] = dp[:t].astype(dp_ref.dtype)
        zb = z.astype(BF16)
        y = jnp.dot(zb, wv, preferred_element_type=F32)
        dsc = jnp.sum(dyc * y, axis=0, keepdims=True)
        dw = lax.dot_general(zb, dye[:t].astype(BF16), (TN, ((), ())), preferred_element_type=F32)

        @pl.when(i == 0)
        def _():
            dw_ref[...] = dw
            dsc_ref[...] = dsc

        @pl.when(i > 0)
        def _():
            dw_ref[...] += dw
            dsc_ref[...] += dsc

    dp, dw, dsc = pl.pallas_call(
        body, name="pool_bwd", grid=(nt,),
        out_shape=(jax.ShapeDtypeStruct(dbuf.shape, dbuf.dtype), jax.ShapeDtypeStruct((GROUP, GROUP), F32),
                   jax.ShapeDtypeStruct((1, GROUP), F32)),
        in_specs=[pl.BlockSpec((t, GROUP), lambda i: (i, col)),
                  pl.BlockSpec((16, GROUP), _prev_map(t, 16, col)),
                  pl.BlockSpec((GROUP, GROUP), lambda i: (0, 0)),
                  pl.BlockSpec((1, GROUP), lambda i: (0, 0)),
                  pl.BlockSpec((t, GROUP), lambda i: (i, 3)),
                  pl.BlockSpec((16, GROUP), _next_map(t, 16, 3, s)), ANY_SPEC],
        out_specs=(pl.BlockSpec((t, GROUP), lambda i: (i, col)), pl.BlockSpec((GROUP, GROUP), lambda i: (0, 0)),
                   pl.BlockSpec((1, GROUP), lambda i: (0, 0))),
        input_output_aliases={6: 0}, compiler_params=_params(),
    )(proj, proj, wbd, scale.reshape(1, GROUP), dy, dy, dbuf)
    return dp, dw, dsc.reshape(GROUP)


def _ffn_gate_fwd(u0, w):
    s = u0.shape[1]
    t = min(ROW_TILE, s)

    def body(a_ref, ap_ref, g_ref, gp_ref, wa_ref, wg_ref, o_ref):
        first = (pl.program_id(1) > 0).astype(F32)
        a = _conv3(jnp.concatenate([ap_ref[...] * first, a_ref[...]], axis=0), wa_ref[...])[8:]
        g = _conv3(jnp.concatenate([gp_ref[...] * first, g_ref[...]], axis=0), wg_ref[...])[8:]
        o_ref[...] = (a * (g * _sigmoid(g))).astype(o_ref.dtype)

    def cur(off):
        return pl.BlockSpec((None, t, FF_SHARD), lambda j, i: (j + off, i, 0))

    def prev(off):
        return pl.BlockSpec((None, 8, FF_SHARD), lambda j, i: (j + off, jnp.maximum(i * (t // 8) - 1, 0), 0))

    def wspec(off):
        return pl.BlockSpec((None, 3, FF_SHARD), lambda j, i: (j + off, 0, 0))

    return pl.pallas_call(
        body, name="ffn_gate_fwd", grid=(FF_HALF, s // t),
        out_shape=jax.ShapeDtypeStruct((FF_HALF, s, FF_SHARD), BF16),
        in_specs=[cur(0), prev(0), cur(FF_HALF), prev(FF_HALF), wspec(0), wspec(FF_HALF)],
        out_specs=pl.BlockSpec((None, t, FF_SHARD), lambda j, i: (j, i, 0)), compiler_params=_params(),
    )(u0, u0, u0, u0, w, w)


def _ffn_gate_bwd(u0, w, dact):
    s = u0.shape[1]
    t = min(ROW_TILE, s)
    nt = s // t

    def body(c_ref, p_ref, n_ref, w_ref, d_ref, dn_ref, du_ref, dw_ref):
        i = pl.program_id(1)
        first = (i > 0).astype(F32)
        last = (i < nt - 1).astype(F32)
        dext = jnp.concatenate([jnp.zeros((8, FF_SHARD), F32), d_ref[...], dn_ref[...] * last], axis=0)
        ext = [jnp.concatenate([p_ref[n] * first, c_ref[n], n_ref[n] * last], axis=0) for n in range(2)]
        a = _conv3(ext[0], w_ref[0])
        g = _conv3(ext[1], w_ref[1])
        sg = _sigmoid(g)
        silu = g * sg
        dus = (dext * silu, dext * a * (sg + silu * (1.0 - sg)))
        for n in range(2):
            du_ref[n] = _conv3_t(dus[n], w_ref[n])[8:8 + t].astype(du_ref.dtype)
            dc = dus[n][8:8 + t]
            dw = jnp.concatenate([
                jnp.sum(dc * _shift_down(ext[n], 2)[8:8 + t], axis=0, keepdims=True),
                jnp.sum(dc * _shift_down(ext[n], 1)[8:8 + t], axis=0, keepdims=True),
                jnp.sum(dc * ext[n][8:8 + t], axis=0, keepdims=True),
                jnp.zeros((5, FF_SHARD), F32)], axis=0)

            @pl.when(i == 0)
            def _(n=n, dw=dw):
                dw_ref[n] = dw

            @pl.when(i > 0)
            def _(n=n, dw=dw):
                dw_ref[n] += dw

    def pair(rows, row_map):
        return pl.BlockSpec((2, None, rows, FF_SHARD), lambda j, i: (0, j, row_map(i), 0))

    prev_row = lambda i: jnp.maximum(i * (t // 8) - 1, 0)
    next_row = lambda i: jnp.minimum((i + 1) * (t // 8), s // 8 - 1)
    u2 = u0.reshape(2, FF_HALF, s, FF_SHARD)
    du, dw = pl.pallas_call(
        body, name="ffn_gate_bwd", grid=(FF_HALF, nt),
        out_shape=(jax.ShapeDtypeStruct((2, FF_HALF, s, FF_SHARD), BF16),
                   jax.ShapeDtypeStruct((2, FF_HALF, 8, FF_SHARD), F32)),
        in_specs=[pair(t, lambda i: i), pair(8, prev_row), pair(8, next_row), pair(3, lambda i: 0),
                  pl.BlockSpec((None, t, FF_SHARD), lambda j, i: (j, i, 0)),
                  pl.BlockSpec((None, 8, FF_SHARD), lambda j, i: (j, next_row(i), 0))],
        out_specs=(pair(t, lambda i: i), pair(8, lambda i: 0)),
        compiler_params=_params(),
    )(u2, u2, u2, w.reshape(2, FF_HALF, 3, FF_SHARD), dact, dact)
    return du.reshape(2 * FF_HALF, s, FF_SHARD), dw.reshape(2 * FF_HALF, 8, FF_SHARD)[:, :3]


def _rope_tables(positions):
    inv_freq = ROPE_THETA ** (-jnp.arange(0, ROPE_DIM, 2, dtype=F32) / ROPE_DIM)
    ang = positions.astype(F32)[:, None] * inv_freq
    cos, sin = jnp.cos(ang), jnp.sin(ang)
    s = positions.shape[0]
    half = ROPE_DIM // 2
    rest = HEAD_DIM - ROPE_DIM
    ca = jnp.concatenate([cos, cos, jnp.ones((s, rest), F32)], axis=1)
    cb = jnp.concatenate([-sin, jnp.zeros((s, HEAD_DIM - half), F32)], axis=1)
    cc = jnp.concatenate([jnp.zeros((s, half), F32), sin, jnp.zeros((s, rest), F32)], axis=1)
    return tuple(jnp.tile(tb, (1, N_HEADS)) for tb in (ca, cb, cc))


QK_WIDE = 128
LANE_CQ, LANE_CK = 64, 67
KT_ROWS = 80


def _three_bf16(x):
    hi = x.astype(BF16).astype(F32)
    mid = (x - hi).astype(BF16).astype(F32)
    lo = (x - hi - mid).astype(BF16).astype(F32)
    return hi, mid, lo


def _heads_split(proj, col, tables, c, name):
    s = proj.shape[0]
    t = min(ROW_TILE, s)
    rope = tables is not None
    wide = c is not None
    width = QK_WIDE if wide else HEAD_DIM

    def body(*refs):
        x_ref = refs[0]
        q_ref, k_ref, v_ref, kt_ref, vt_ref = refs[-5:]
        xv = x_ref[...]
        parts = [xv[:, 0:GROUP], xv[:, GROUP:2 * GROUP], xv[:, 2 * GROUP:3 * GROUP]]
        if rope:
            ca, cb, cc = refs[1][...], refs[2][...], refs[3][...]
            for n in range(2):
                p = parts[n]
                parts[n] = p * ca + pltpu.roll(p, GROUP - 8, 1) * cb + pltpu.roll(p, 8, 1) * cc
        parts[0] = parts[0] * (HEAD_DIM ** -0.5)
        k_t, v_t = parts[1].T, parts[2].T
        ones_row = jnp.where(lax.broadcasted_iota(jnp.int32, (KT_ROWS - HEAD_DIM, t), 0) == 0, 1.0, 0.0)
        lane = lax.broadcasted_iota(jnp.int32, (t, QK_WIDE), 1)
        zeros = jnp.zeros((t, QK_WIDE - HEAD_DIM), F32)
        for h in range(N_HEADS):
            hs = slice(h * HEAD_DIM, (h + 1) * HEAD_DIM)
            qh, kh = parts[0][:, hs], parts[1][:, hs]
            if wide:
                terms = _three_bf16(refs[-6][:, h:h + 1])
                qh = jnp.concatenate([qh, zeros], axis=1)
                kh = jnp.concatenate([kh, zeros], axis=1)
                for n in range(3):
                    qh = jnp.where(lane == LANE_CQ + n, terms[n], jnp.where(lane == LANE_CK + n, 1.0, qh))
                    kh = jnp.where(lane == LANE_CK + n, -terms[n], jnp.where(lane == LANE_CQ + n, 1.0, kh))
            q_ref[h] = qh.astype(q_ref.dtype)
            k_ref[h] = kh.astype(k_ref.dtype)
            v_ref[h] = parts[2][:, hs].astype(v_ref.dtype)
            kt_ref[h] = jnp.concatenate([k_t[hs, :], ones_row], axis=0).astype(kt_ref.dtype)
            vt_ref[h] = v_t[hs, :].astype(vt_ref.dtype)

    tab = pl.BlockSpec((t, GROUP), lambda i: (i, 0))
    qk = pl.BlockSpec((N_HEADS, t, width), lambda i: (0, i, 0))
    heads = pl.BlockSpec((N_HEADS, t, HEAD_DIM), lambda i: (0, i, 0))
    heads_t = pl.BlockSpec((N_HEADS, HEAD_DIM, t), lambda i: (0, 0, i))
    qk_shape = jax.ShapeDtypeStruct((N_HEADS, s, width), BF16)
    return pl.pallas_call(
        body, name=name, grid=(s // t,),
        out_shape=(qk_shape, qk_shape, jax.ShapeDtypeStruct((N_HEADS, s, HEAD_DIM), BF16),
                   jax.ShapeDtypeStruct((N_HEADS, KT_ROWS, s), BF16),
                   jax.ShapeDtypeStruct((N_HEADS, HEAD_DIM, s), BF16)),
        in_specs=[pl.BlockSpec((t, 3 * GROUP), lambda i: (i, col))] + ([tab, tab, tab] if rope else [])
        + ([pl.BlockSpec((t, 128), lambda i: (i, 0))] if wide else []),
        out_specs=(qk, qk, heads, pl.BlockSpec((N_HEADS, KT_ROWS, t), lambda i: (0, 0, i)), heads_t),
        compiler_params=_params(),
    )(*((proj,) + (tuple(tables) if rope else ()) + ((c,) if wide else ())))


def _heads_merge(dqt, dk, dv, tables, name, dbuf, col):
    s = dv.shape[1]
    t = min(ROW_TILE, s)
    rope = tables is not None

    wide = dk.shape[2] == QK_WIDE

    def body(*refs):
        o_ref = refs[n_in + 1]
        dq = jnp.concatenate([refs[0][h, :HEAD_DIM, :] for h in range(N_HEADS)], axis=0).T
        parts = [dq] + [jnp.concatenate([r[h][:, :HEAD_DIM] for h in range(N_HEADS)], axis=1) for r in refs[1:3]]
        parts[0] = parts[0] * (HEAD_DIM ** -0.5)
        if rope:
            ca, cb, cc = refs[3][...], refs[4][...], refs[5][...]
            for n in range(2):
                p = parts[n]
                parts[n] = p * ca + pltpu.roll(p * cb, 8, 1) + pltpu.roll(p * cc, GROUP - 8, 1)
        o_ref[...] = jnp.concatenate(parts, axis=1).astype(o_ref.dtype)
        if wide:
            over_keys = jnp.concatenate([refs[0][h, HEAD_DIM:HEAD_DIM + 8, :] for h in range(N_HEADS)]
                                        + [jnp.zeros((128 - 8 * N_HEADS, t), F32)], axis=0).T
            lane = lax.broadcasted_iota(jnp.int32, (t, 128), 1)
            dc = jnp.zeros((t, 128), F32)
            for h in range(N_HEADS):
                dc = jnp.where(lane == h, over_keys[:, 8 * h:8 * h + 1] - refs[1][h][:, LANE_CK:LANE_CK + 1], dc)
            refs[n_in + 2][...] = dc

    tab = pl.BlockSpec((t, GROUP), lambda i: (i, 0))
    heads = pl.BlockSpec((N_HEADS, t, HEAD_DIM), lambda i: (0, i, 0))
    n_in = 6 if rope else 3
    dspec = pl.BlockSpec((t, 3 * GROUP), lambda i: (i, col))
    dshape = jax.ShapeDtypeStruct(dbuf.shape, dbuf.dtype)
    return pl.pallas_call(
        body, name=name, grid=(s // t,),
        out_shape=(dshape, jax.ShapeDtypeStruct((s, 128), F32)) if wide else dshape,
        in_specs=[pl.BlockSpec((N_HEADS, KT_ROWS, t), lambda i: (0, 0, i)),
                  pl.BlockSpec((N_HEADS, t, dk.shape[2]), lambda i: (0, i, 0)), heads]
        + ([tab, tab, tab] if rope else []) + [ANY_SPEC],
        out_specs=(dspec, pl.BlockSpec((t, 128), lambda i: (i, 0))) if wide else dspec,
        input_output_aliases={n_in: 0}, compiler_params=_params(),
    )(*((dqt, dk, dv) + (tuple(tables) if rope else ()) + (dbuf,)))


def _log_sigmoid(x):
    return jnp.minimum(x, 0.0) - jnp.log(1.0 + jnp.exp(-jnp.abs(x)))


def _scan_rows(x, reverse):
    n = x.shape[0]
    row = lax.broadcasted_iota(jnp.int32, x.shape, 0)
    k = 1
    while k < n:
        if reverse:
            x = x + jnp.where(row < n - k, _shift_up(x, k), 0.0)
        else:
            x = x + jnp.where(row >= k, _shift_down(x, k), 0.0)
        k *= 2
    return x


def _gate_cumsum(proj, bias):
    s = proj.shape[0]
    col = COL_GATE // 128

    def body(z_ref, b_ref, c_ref):
        c_ref[...] = _scan_rows(_log_sigmoid(z_ref[...] + b_ref[...]), False)

    return pl.pallas_call(
        body, name="gate_cumsum", grid=(1,), out_shape=jax.ShapeDtypeStruct((s, 128), F32),
        in_specs=[pl.BlockSpec((s, 128), lambda i: (0, col)), pl.BlockSpec((1, 128), lambda i: (0, 0))],
        out_specs=pl.BlockSpec((s, 128), lambda i: (0, 0)), compiler_params=_params(),
    )(proj, bias)


def _gate_cumsum_bwd(proj, bias, dc, dbuf):
    s = proj.shape[0]
    col = COL_GATE // 128

    def body(z_ref, b_ref, dc_ref, buf_ref, dz_ref, db_ref):
        dlogf = _scan_rows(dc_ref[...], True)
        dz = dlogf * _sigmoid(-(z_ref[...] + b_ref[...]))
        dz_ref[...] = dz.astype(dz_ref.dtype)
        db_ref[...] = jnp.sum(dz, axis=0, keepdims=True)

    return pl.pallas_call(
        body, name="gate_cumsum_bwd", grid=(1,),
        out_shape=(jax.ShapeDtypeStruct(dbuf.shape, dbuf.dtype), jax.ShapeDtypeStruct((1, 128), F32)),
        in_specs=[pl.BlockSpec((s, 128), lambda i: (0, col)), pl.BlockSpec((1, 128), lambda i: (0, 0)),
                  pl.BlockSpec((s, 128), lambda i: (0, 0)), ANY_SPEC],
        out_specs=(pl.BlockSpec((s, 128), lambda i: (0, col)), pl.BlockSpec((1, 128), lambda i: (0, 0))),
        input_output_aliases={3: 0}, compiler_params=_params(),
    )(proj, bias, dc, dbuf)


DIL_REACH = 2048


def _pair_weight(mode, d):
    if mode == "fox":
        return jnp.where(d >= 0, 1.0, 0.0)
    w1 = jnp.where(jnp.abs(d - 64) <= 64, 1.0, 0.0)
    w2 = jnp.where((d & 3) == 0, jnp.where(jnp.abs(d - 256) <= 256, 1.0, 0.0), 0.0)
    w3 = jnp.where((d & 15) == 0, jnp.where(jnp.abs(d - 1024) <= 1024, 1.0, 0.0), 0.0)
    return w1 + w2 + w3


def _bias_tables(mode, tq, tk):
    nb = 2 if mode == "fox" else DIL_REACH // tk + 1
    n = lax.broadcasted_iota(jnp.int32, (nb, tk, tq), 0)
    key = lax.broadcasted_iota(jnp.int32, (nb, tk, tq), 1)
    query = lax.broadcasted_iota(jnp.int32, (nb, tk, tq), 2)
    w = _pair_weight(mode, n * tk + query - key)
    return jnp.where(w > 0.0, jnp.log(jnp.maximum(w, 1.0)), NEG)


M_INIT = -1e29


def _first_key_chunk(mode, q0, tk):
    if mode == "fox":
        return 0
    return jnp.maximum(q0 - DIL_REACH, 0) // tk


def _attention_fwd(mode, q, k, vt, tab_t, ybuf, col):
    s, width = q.shape[1], q.shape[2]
    tq = min(ATT_TQ, s)
    tk = tq
    nb = tab_t.shape[0]

    def body(q_ref, k_ref, vt_ref, tab_ref, buf_ref, y_ref, o_ref, lse_ref):
        i = pl.program_id(0)
        lo = _first_key_chunk(mode, i * tq, tk)

        def step(c, carry):
            k0 = pl.multiple_of(c * tk, tk)
            tab = tab_ref[jnp.minimum(i - c, nb - 1)]
            scores = [lax.dot_general(k_ref[h, pl.ds(k0, tk), :], q_ref[h], (NT, ((), ())),
                                      preferred_element_type=F32) for h in range(N_HEADS)]
            stats, probs = [], []
            for h in range(N_HEADS):
                m, l = carry[3 * h:3 * h + 2]
                sc = scores[h] + tab
                m_new = jnp.maximum(m, jnp.max(sc, axis=0, keepdims=True))
                alpha = jnp.exp(m - m_new)
                p = jnp.exp(sc - m_new)
                stats.append((m_new, alpha * l + jnp.sum(p, axis=0, keepdims=True), alpha))
                probs.append(p.astype(BF16))
            pv = [jnp.dot(vt_ref[h, :, pl.ds(k0, tk)], probs[h], preferred_element_type=F32) for h in range(N_HEADS)]
            new = []
            for h in range(N_HEADS):
                m_new, l, alpha = stats[h]
                new += [m_new, l, alpha * carry[3 * h + 2] + pv[h]]
            return tuple(new)

        start = (jnp.full((1, tq), M_INIT, F32), jnp.zeros((1, tq), F32), jnp.zeros((HEAD_DIM, tq), F32))
        done = lax.fori_loop(lo, i + 1, step, start * N_HEADS)
        outs = []
        for h in range(N_HEADS):
            m, l, acc = done[3 * h:3 * h + 3]
            outs.append(acc / l)
            lse_ref[h] = m + jnp.log(l)
        out = jnp.concatenate(outs, axis=0).T
        y_ref[...] = out.astype(y_ref.dtype)
        o_ref[...] = out

    rowspec = pl.BlockSpec((N_HEADS, 1, tq), lambda i: (0, 0, i))
    return pl.pallas_call(
        body, name="attention_fwd_" + mode, grid=(s // tq,),
        out_shape=(jax.ShapeDtypeStruct(ybuf.shape, ybuf.dtype), jax.ShapeDtypeStruct((s, GROUP), F32),
                   jax.ShapeDtypeStruct((N_HEADS, 1, s), F32)),
        in_specs=[pl.BlockSpec((N_HEADS, tq, width), lambda i: (0, i, 0)),
                  pl.BlockSpec((N_HEADS, s, width), lambda i: (0, 0, 0)),
                  pl.BlockSpec((N_HEADS, HEAD_DIM, s), lambda i: (0, 0, 0)),
                  pl.BlockSpec((nb, tk, tq), lambda i: (0, 0, 0)), ANY_SPEC],
        out_specs=(pl.BlockSpec((tq, GROUP), lambda i: (i, col)), pl.BlockSpec((tq, GROUP), lambda i: (i, 0)),
                   rowspec),
        input_output_aliases={4: 0}, compiler_params=_params(),
    )(q, k, vt, tab_t, ybuf)


def _attention_delta(o, do, col):
    s = o.shape[0]
    t = min(ROW_TILE, s)

    def body(o_ref, do_ref, delta_ref, dob_ref):
        dov = do_ref[...]
        prod_t = (o_ref[...] * dov).T
        for h in range(N_HEADS):
            hs = slice(h * HEAD_DIM, (h + 1) * HEAD_DIM)
            delta_ref[h] = jnp.sum(prod_t[hs, :], axis=0, keepdims=True)
            dob_ref[h] = dov[:, hs].astype(dob_ref.dtype)

    return pl.pallas_call(
        body, name="attention_delta", grid=(s // t,),
        out_shape=(jax.ShapeDtypeStruct((N_HEADS, 1, s), F32), jax.ShapeDtypeStruct((N_HEADS, s, HEAD_DIM), BF16)),
        in_specs=[pl.BlockSpec((t, GROUP), lambda i: (i, 0)), pl.BlockSpec((t, GROUP), lambda i: (i, col))],
        out_specs=(pl.BlockSpec((N_HEADS, 1, t), lambda i: (0, 0, i)),
                   pl.BlockSpec((N_HEADS, t, HEAD_DIM), lambda i: (0, i, 0))),
        compiler_params=_params(),
    )(o, do)


def _attention_bwd(mode, q, k, v, kt, tab_t, dob, lse, delta):
    s, width = q.shape[1], q.shape[2]
    tq = min(ATT_TQ, s)
    tk = tq
    nq = s // tq
    nb = tab_t.shape[0]

    def body(q_ref, k_ref, v_ref, kt_ref, tab_ref, dob_ref, lse_ref, delta_ref, dqt_ref, dk_ref, dv_ref):
        i = pl.program_id(0)

        @pl.when(i == 0)
        def _():
            dqt_ref[...] = jnp.zeros_like(dqt_ref)

        hi = nq if mode == "fox" else jnp.minimum((i * tk + tk - 1 + DIL_REACH) // tq + 1, nq)
        for h0 in range(0, N_HEADS, BWD_HEADS):
            heads = range(h0, h0 + BWD_HEADS)

            def step(c, carry, heads=heads):
                q0 = pl.multiple_of(c * tq, tq)
                qs = pl.ds(q0, tq)
                tab = tab_ref[jnp.minimum(c - i, nb - 1)]
                qv = [q_ref[h, qs, :] for h in heads]
                dov = [dob_ref[h, qs, :] for h in heads]
                sc = [lax.dot_general(k_ref[h], qv[n], (NT, ((), ())), preferred_element_type=F32)
                      for n, h in enumerate(heads)]
                dp = [lax.dot_general(v_ref[h], dov[n], (NT, ((), ())), preferred_element_type=F32)
                      for n, h in enumerate(heads)]
                pb, dsb = [], []
                for n, h in enumerate(heads):
                    p = jnp.exp(sc[n] + tab - lse_ref[h, :, qs])
                    pb.append(p.astype(BF16))
                    dsb.append((p * (dp[n] - delta_ref[h, :, qs])).astype(BF16))
                new = []
                for n, h in enumerate(heads):
                    new += [carry[2 * n] + jnp.dot(dsb[n], qv[n], preferred_element_type=F32),
                            carry[2 * n + 1] + jnp.dot(pb[n], dov[n], preferred_element_type=F32)]
                for n, h in enumerate(heads):
                    dqt_ref[h, :, qs] += jnp.dot(kt_ref[h], dsb[n], preferred_element_type=F32)
                return tuple(new)

            start = (jnp.zeros((tk, width), F32), jnp.zeros((tk, HEAD_DIM), F32))
            done = lax.fori_loop(i, hi, step, start * BWD_HEADS)
            for n, h in enumerate(heads):
                dk_ref[h] = done[2 * n]
                dv_ref[h] = done[2 * n + 1]

    def full(shape):
        return pl.BlockSpec(shape, lambda i: (0, 0, 0))

    kblk = pl.BlockSpec((N_HEADS, tk, width), lambda i: (0, i, 0))
    vblk = pl.BlockSpec((N_HEADS, tk, HEAD_DIM), lambda i: (0, i, 0))
    return pl.pallas_call(
        body, name="attention_bwd_" + mode, grid=(s // tk,),
        out_shape=(jax.ShapeDtypeStruct((N_HEADS, KT_ROWS, s), F32), jax.ShapeDtypeStruct((N_HEADS, s, width), F32),
                   jax.ShapeDtypeStruct((N_HEADS, s, HEAD_DIM), F32)),
        in_specs=[full((N_HEADS, s, width)), kblk, vblk, pl.BlockSpec((N_HEADS, KT_ROWS, tk), lambda i: (0, 0, i)),
                  full((nb, tk, tq)), full((N_HEADS, s, HEAD_DIM)), full((N_HEADS, 1, s)), full((N_HEADS, 1, s))],
        out_specs=(full((N_HEADS, KT_ROWS, s)), kblk, vblk),
        compiler_params=_params(),
    )(q, k, v, kt, tab_t, dob, lse, delta)


def _xattn_fwd(qx, kvm):
    s = qx.shape[0]
    t = min(ROW_TILE, s)

    def body(q_ref, kv_ref, o_ref):
        for h in range(XA_HEADS):
            qv = q_ref[:, h * XA_DIM:(h + 1) * XA_DIM].astype(BF16)
            kv = kv_ref[h].astype(BF16)
            vv = kv_ref[XA_HEADS + h].astype(BF16)
            sc = lax.dot_general(qv, kv, (NT, ((), ())), preferred_element_type=F32) * (XA_DIM ** -0.5)
            e = jnp.exp(sc - jnp.max(sc, axis=-1, keepdims=True))
            p = e / jnp.sum(e, axis=-1, keepdims=True)
            o_ref[:, h * XA_DIM:(h + 1) * XA_DIM] = jnp.dot(p.astype(BF16), vv,
                                                             preferred_element_type=F32).astype(o_ref.dtype)

    return pl.pallas_call(
        body, name="xattn_fwd", grid=(s // t,), out_shape=jax.ShapeDtypeStruct((s, D_MODEL), BF16),
        in_specs=[pl.BlockSpec((t, D_MODEL), lambda i: (i, 0)),
                  pl.BlockSpec((2 * XA_HEADS, MEM_LEN, XA_DIM), lambda i: (0, 0, 0))],
        out_specs=pl.BlockSpec((t, D_MODEL), lambda i: (i, 0)), compiler_params=_params(),
    )(qx, kvm)


def _xattn_bwd(qx, kvm, do):
    s = qx.shape[0]
    t = min(ROW_TILE, s)

    def body(q_ref, kv_ref, do_ref, dq_ref, dkv_ref):
        i = pl.program_id(0)
        for h in range(XA_HEADS):
            qv = q_ref[:, h * XA_DIM:(h + 1) * XA_DIM].astype(BF16)
            dov = do_ref[:, h * XA_DIM:(h + 1) * XA_DIM].astype(BF16)
            kv = kv_ref[h].astype(BF16)
            vv = kv_ref[XA_HEADS + h].astype(BF16)
            sc = lax.dot_general(qv, kv, (NT, ((), ())), preferred_element_type=F32) * (XA_DIM ** -0.5)
            e = jnp.exp(sc - jnp.max(sc, axis=-1, keepdims=True))
            p = e / jnp.sum(e, axis=-1, keepdims=True)
            dp = lax.dot_general(dov, vv, (NT, ((), ())), preferred_element_type=F32)
            ds = (p * (dp - jnp.sum(p * dp, axis=-1, keepdims=True)) * (XA_DIM ** -0.5)).astype(BF16)
            dq_ref[:, h * XA_DIM:(h + 1) * XA_DIM] = jnp.dot(ds, kv, preferred_element_type=F32).astype(dq_ref.dtype)
            dk = lax.dot_general(ds, qv, (TN, ((), ())), preferred_element_type=F32)
            dv = lax.dot_general(p.astype(BF16), dov, (TN, ((), ())), preferred_element_type=F32)

            @pl.when(i == 0)
            def _(h=h, dk=dk, dv=dv):
                dkv_ref[h] = dk
                dkv_ref[XA_HEADS + h] = dv

            @pl.when(i > 0)
            def _(h=h, dk=dk, dv=dv):
                dkv_ref[h] += dk
                dkv_ref[XA_HEADS + h] += dv

    row = pl.BlockSpec((t, D_MODEL), lambda i: (i, 0))
    kvs = pl.BlockSpec((2 * XA_HEADS, MEM_LEN, XA_DIM), lambda i: (0, 0, 0))
    return pl.pallas_call(
        body, name="xattn_bwd", grid=(s // t,),
        out_shape=(jax.ShapeDtypeStruct((s, D_MODEL), BF16),
                   jax.ShapeDtypeStruct((2 * XA_HEADS, MEM_LEN, XA_DIM), F32)),
        in_specs=[row, kvs, row], out_specs=(row, kvs), compiler_params=_params(),
    )(qx, kvm, do)


def _adamw(parts, owns, me, w, m, v, name):
    nl, r, c = w.shape
    tr = r
    for cand in (256, 128, 64, 32, 16, 8):
        if r % cand == 0 and r > cand and N_DEV * cand * c * 4 <= ADAMW_BLOCK_BYTES:
            tr = cand
            break
    nt = r // tr
    per_layer = N_DEV + (1 if owns is not None else 0)

    def body(me_ref, *refs):
        w_ref, m_ref, v_ref, g_ref, d_ref, nm_ref, nv_ref = refs[nl * per_layer:]
        layer = pl.program_id(0)
        g = None
        for l in range(nl):
            p_refs = refs[l * per_layer:(l + 1) * per_layer]
            gl = None
            for d in range(N_DEV):
                term = p_refs[d][...].astype(F32)
                if owns is not None:
                    term = jnp.where(me_ref[0] == d, p_refs[N_DEV][...].astype(F32), term)
                gl = term if gl is None else gl + term
            g = gl if g is None else jnp.where(layer == l, gl, g)
        mn = ADAM_B1 * m_ref[...] + (1.0 - ADAM_B1) * g
        vn = ADAM_B2 * v_ref[...] + (1.0 - ADAM_B2) * (g * g)
        m_hat = mn / (1.0 - ADAM_B1 ** ADAM_STEP)
        v_hat = vn / (1.0 - ADAM_B2 ** ADAM_STEP)
        g_ref[...] = g
        d_ref[...] = -ADAM_LR * (m_hat / (jnp.sqrt(v_hat) + ADAM_EPS) + ADAM_WD * w_ref[...])
        nm_ref[...] = mn
        nv_ref[...] = vn

    def rows(l, ll, i):
        return jnp.where(ll == l, i, jnp.where(ll < l, 0, nt - 1))

    def part_spec(l, d):
        if owns is None:
            return pl.BlockSpec((None, tr, c), lambda ll, i, me_ref: (d, rows(l, ll, i), 0))
        return pl.BlockSpec((None, tr, c),
                            lambda ll, i, me_ref: (jnp.where(me_ref[0] == d, (d + 1) % N_DEV, d), rows(l, ll, i), 0))

    def own_spec(l):
        return pl.BlockSpec((None, tr, c), lambda ll, i, me_ref: (me_ref[0], rows(l, ll, i), 0))

    in_specs, operands = [], []
    for l in range(nl):
        in_specs += [part_spec(l, d) for d in range(N_DEV)]
        operands += [parts[l]] * N_DEV
        if owns is not None:
            in_specs.append(own_spec(l))
            operands.append(owns[l])
    blk = pl.BlockSpec((None, tr, c), lambda ll, i, me_ref: (ll, i, 0))
    shp = jax.ShapeDtypeStruct((nl, r, c), F32)
    return pl.pallas_call(
        body, name=name, out_shape=(shp, shp, shp, shp),
        grid_spec=pltpu.PrefetchScalarGridSpec(
            num_scalar_prefetch=1, grid=(nl, nt), in_specs=in_specs + [blk, blk, blk],
            out_specs=(blk, blk, blk, blk)),
        compiler_params=_params(),
    )(me.reshape(1), *operands, w, m, v)


GROUPS = {"in": ("w_in",), "rest": ("w_out", "w_xq", "w_xo", "w_xkv", "w_up", "w_down")}
FULL_SHAPES = {"w_in": (D_MODEL, N_IN_PAD), "w_out": (D_MODEL, D_MODEL), "w_xq": (D_MODEL, D_MODEL),
               "w_xo": (D_MODEL, D_MODEL), "w_xkv": (N_DEV, D_MODEL, 2 * D_MODEL // N_DEV),
               "w_up": (N_DEV, D_MODEL, FF_SHARD), "w_down": (FF_HALF, FF_SHARD, D_MODEL)}
PIECE_SHAPES = {"w_in": (N_DEV, D_MODEL // N_DEV, N_IN_PAD), "w_out": (N_DEV, D_MODEL // N_DEV, D_MODEL),
                "w_xq": (N_DEV, D_MODEL // N_DEV, D_MODEL), "w_xo": (N_DEV, D_MODEL // N_DEV, D_MODEL),
                "w_xkv": (N_DEV, D_MODEL, 2 * D_MODEL // N_DEV), "w_up": (N_DEV, D_MODEL, FF_SHARD),
                "w_down": (N_DEV, D_FF // N_DEV, D_MODEL)}
CONV_WORDS = 8192
SMALL_WORDS = 80 * 1024


class _GatheredWeights:
    def __init__(self, states, layer):
        self.states, self.layer, self.full, self.extra = dict(states), layer, {}, None

    def need(self, group, after):
        if group in self.states:
            got, _ = _exchange_wait(self.states.pop(group), after, "gather_%s_wait_%d" % (group, self.layer))
            for name, g in zip(GROUPS[group], got):
                self.full[name] = g.reshape(FULL_SHAPES[name])
            self.extra = got[len(GROUPS[group]):]

    def __getitem__(self, name):
        return self.full[name]


def _relay_in_cols(w):
    pad = jnp.zeros(w.shape[:-1] + (N_IN_PAD - N_IN,), w.dtype)
    return jnp.concatenate([w[..., :2304], w[..., 2308:N_IN], w[..., 2304:2308], pad], axis=-1)


def _unrelay_in_cols(w):
    return jnp.concatenate([w[..., :2304], w[..., COL_GATE:COL_GATE + 4], w[..., 2304:COL_GATE]], axis=-1)


def _layer_fwd(h, memv, w, sm, tables):
    sv = {"h0": h}
    xn = _rms_fwd(h, sm["g_mix"], "rms_mix")
    w.need("in", xn)
    proj = _mm_nn(xn, w["w_in"], "mm_in", tn=896)
    sv["xn"], sv["proj"] = xn, proj
    ycat = _sconv_fwd(proj, sm["w_sconv"])
    qd, kd, vd, ktd, vtd = _heads_split(proj, 1, tables["rope"], None, "split_dil")
    ycat, ob, lse_b = _attention_fwd("dil", qd, kd, vtd, tables["dil"], ycat, 1)
    sv["dil"] = (qd, kd, vd, ktd, ob, lse_b)
    c = _gate_cumsum(proj, sm["b_forget_pad"])
    qf, kf, vf, ktf, vtf = _heads_split(proj, 2, None, c, "split_fox")
    ycat, oc, lse_c = _attention_fwd("fox", qf, kf, vtf, tables["fox"], ycat, 2)
    sv["fox"] = (qf, kf, vf, ktf, oc, lse_c)
    ycat = _pool_fwd(proj, sm["w_pool_bd"], sm["pool_scale"], ycat)
    sv["ycat"] = ycat
    w.need("rest", ycat)
    h1 = _mm_nn(ycat, w["w_out"], "mm_out", res=h)
    sv["h1"] = h1
    xq = _rms_fwd(h1, sm["g_xa"], "rms_xa")
    memn = _rms_fwd(memv, sm["g_mem"], "rms_mem")
    qx = _mm_nn(xq, w["w_xq"], "mm_xq", out_dtype=BF16)
    kvm = _matmul(memn, w["w_xkv"], (N_DEV, MEM_LEN, XA_DIM), grid=(N_DEV, 1, 1),
                  a_spec=pl.BlockSpec((MEM_LEN, D_MODEL), lambda i, j, r: (0, 0)),
                  b_spec=pl.BlockSpec((None, D_MODEL, XA_DIM), lambda i, j, r: (i, 0, 0)),
                  o_spec=pl.BlockSpec((None, MEM_LEN, XA_DIM), lambda i, j, r: (i, 0, 0)),
                  dims=NN, nred=1, name="mm_xkv")
    ox = _xattn_fwd(qx, kvm)
    sv.update(xq=xq, memn=memn, qx=qx, kvm=kvm, ox=ox)
    h2 = _mm_nn(ox, w["w_xo"], "mm_xo", res=h1)
    sv["h2"] = h2
    xf = _rms_fwd(h2, sm["g_ffn"], "rms_ffn")
    s = h.shape[0]
    tm = min(ROW_TILE, s)
    u0 = _matmul(xf, w["w_up"], (N_DEV, s, FF_SHARD), grid=(s // tm, N_DEV, 1),
                 a_spec=pl.BlockSpec((tm, D_MODEL), lambda i, j, r: (i, 0)),
                 b_spec=pl.BlockSpec((None, D_MODEL, FF_SHARD), lambda i, j, r: (j, 0, 0)),
                 o_spec=pl.BlockSpec((None, tm, FF_SHARD), lambda i, j, r: (j, i, 0)),
                 dims=NN, nred=1, name="mm_up")
    act = _ffn_gate_fwd(u0, sm["w_ffconv"])
    sv.update(xf=xf, u0=u0, act=act)
    ospec = pl.BlockSpec((tm, D_MODEL), lambda i, j, r: (i, 0))
    h3 = _matmul(act, w["w_down"], (s, D_MODEL), grid=(s // tm, 1, 1),
                 a_spec=pl.BlockSpec((FF_HALF, tm, FF_SHARD), lambda i, j, r: (0, i, 0)),
                 b_spec=pl.BlockSpec((FF_HALF, FF_SHARD, D_MODEL), lambda i, j, r: (0, 0, 0)),
                 o_spec=ospec, dims=NN, nred=1, slabs=FF_HALF, name="mm_down", res=h2, res_spec=ospec)
    return h3, sv


def _layer_bwd(dh3, memv, w, sm, tables, sv, rest_ready):
    s = dh3.shape[0]
    tm = min(ROW_TILE, s)
    big, small = {}, {}
    ts = max(s // 2, 1)
    dact = _matmul(dh3, w["w_down"], (FF_HALF, s, FF_SHARD), grid=(s // tm, FF_HALF, 1),
                   a_spec=pl.BlockSpec((tm, D_MODEL), lambda i, j, r: (i, 0)),
                   b_spec=pl.BlockSpec((None, FF_SHARD, D_MODEL), lambda i, j, r: (j, 0, 0)),
                   o_spec=pl.BlockSpec((None, tm, FF_SHARD), lambda i, j, r: (j, i, 0)),
                   dims=NT, nred=1, name="mm_dact")
    big["w_down"] = _matmul(sv["act"], dh3, (FF_HALF, FF_SHARD, D_MODEL), grid=(FF_HALF, 1, s // ts),
                            a_spec=pl.BlockSpec((None, ts, FF_SHARD), lambda i, j, r: (i, r, 0)),
                            b_spec=pl.BlockSpec((ts, D_MODEL), lambda i, j, r: (r, 0)),
                            o_spec=pl.BlockSpec((None, FF_SHARD, D_MODEL), lambda i, j, r: (i, 0, 0)),
                            dims=TN, nred=s // ts, name="mm_dw_down", out_dtype=GRAD_DTYPE)
    du0, small["w_ffconv"] = _ffn_gate_bwd(sv["u0"], sm["w_ffconv"], dact)
    dxf = _matmul(du0, w["w_up"], (s, D_MODEL), grid=(s // tm, 1, 1),
                  a_spec=pl.BlockSpec((N_DEV, tm, FF_SHARD), lambda i, j, r: (0, i, 0)),
                  b_spec=pl.BlockSpec((N_DEV, D_MODEL, FF_SHARD), lambda i, j, r: (0, 0, 0)),
                  o_spec=pl.BlockSpec((tm, D_MODEL), lambda i, j, r: (i, 0)),
                  dims=NT, nred=1, slabs=N_DEV, name="mm_dxf")
    big["w_up"] = _matmul(sv["xf"], du0, (N_DEV, D_MODEL, FF_SHARD), grid=(N_DEV, 1, 1),
                          a_spec=pl.BlockSpec((s, D_MODEL), lambda i, j, r: (0, 0)),
                          b_spec=pl.BlockSpec((None, s, FF_SHARD), lambda i, j, r: (i, 0, 0)),
                          o_spec=pl.BlockSpec((None, D_MODEL, FF_SHARD), lambda i, j, r: (i, 0, 0)),
                          dims=TN, nred=1, name="mm_dw_up", out_dtype=GRAD_DTYPE)
    dh2, small["g_ffn"] = _rms_bwd(dxf, sv["h2"], sm["g_ffn"], dh3, "rms_ffn_bwd")
    dox = _mm_nt(dh2, w["w_xo"], "mm_dox", out_dtype=BF16)
    big["w_xo"] = _mm_tn(sv["ox"], dh2, "mm_dw_xo")
    dqx, dkvm = _xattn_bwd(sv["qx"], sv["kvm"], dox)
    dxq = _mm_nt(dqx, w["w_xq"], "mm_dxq")
    big["w_xq"] = _mm_tn(sv["xq"], dqx, "mm_dw_xq")
    big["w_xkv"] = _matmul(sv["memn"], dkvm, (N_DEV, D_MODEL, XA_DIM), grid=(N_DEV, 1, 1),
                           a_spec=pl.BlockSpec((MEM_LEN, D_MODEL), lambda i, j, r: (0, 0)),
                           b_spec=pl.BlockSpec((None, MEM_LEN, XA_DIM), lambda i, j, r: (i, 0, 0)),
                           o_spec=pl.BlockSpec((None, D_MODEL, XA_DIM), lambda i, j, r: (i, 0, 0)),
                           dims=TN, nred=1, name="mm_dw_xkv", out_dtype=GRAD_DTYPE)
    dmemn = _matmul(dkvm, w["w_xkv"], (MEM_LEN, D_MODEL), grid=(1, 1, 1),
                    a_spec=pl.BlockSpec((N_DEV, MEM_LEN, XA_DIM), lambda i, j, r: (0, 0, 0)),
                    b_spec=pl.BlockSpec((N_DEV, D_MODEL, XA_DIM), lambda i, j, r: (0, 0, 0)),
                    o_spec=pl.BlockSpec((MEM_LEN, D_MODEL), lambda i, j, r: (0, 0)),
                    dims=NT, nred=1, slabs=N_DEV, name="mm_dmemn")
    _, small["g_mem"] = _rms_bwd(dmemn, memv, sm["g_mem"], None, "rms_mem_bwd")
    dh1, small["g_xa"] = _rms_bwd(dxq, sv["h1"], sm["g_xa"], dh2, "rms_xa_bwd")
    dycat = _mm_nt(dh1, w["w_out"], "mm_dycat")
    big["w_out"] = _mm_tn(sv["ycat"], dh1, "mm_dw_out")
    proj = sv["proj"]
    dproj, small["w_sconv"] = _sconv_bwd(proj, sm["w_sconv"] + rest_ready(big), dycat)
    qd, kd, vd, ktd, ob, lse_b = sv["dil"]
    delta, dob = _attention_delta(ob, dycat, 1)
    dqt, dk, dv = _attention_bwd("dil", qd, kd, vd, ktd, tables["dil"], dob, lse_b, delta)
    dproj = _heads_merge(dqt, dk, dv, tables["rope"], "merge_dil", dproj, 1)
    qf, kf, vf, ktf, oc, lse_c = sv["fox"]
    delta, dob = _attention_delta(oc, dycat, 2)
    dqt, dk, dv = _attention_bwd("fox", qf, kf, vf, ktf, tables["fox"], dob, lse_c, delta)
    dproj, dc = _heads_merge(dqt, dk, dv, None, "merge_fox", dproj, 2)
    dproj, dbias = _gate_cumsum_bwd(proj, sm["b_forget_pad"], dc, dproj)
    small["b_forget"] = dbias[0, :N_HEADS]
    dproj, dwbd, small["pool_scale"] = _pool_bwd(proj, sm["w_pool_bd"], sm["pool_scale"], dycat, dproj)
    small["w_pool"] = jnp.stack([dwbd[64 * g:64 * (g + 1), 64 * g:64 * (g + 1)] for g in range(4)])
    dxn = _mm_nt(dproj, w["w_in"], "mm_dxn")
    big["w_in"] = _mm_tn(sv["xn"], dproj, "mm_dw_in", tn=896)
    dh0, small["g_mix"] = _rms_bwd(dxn, sv["h0"], sm["g_mix"], dh1, "rms_mix_bwd")
    return dh0, big, small


SMALL_NAMES = ("g_mix", "b_forget", "w_pool", "pool_scale", "g_xa", "g_mem", "g_ffn", "w_sconv", "w_ffconv")
WEIGHT_NAMES = ("g_mix", "w_in", "b_forget", "w_sconv", "w_pool", "pool_scale", "w_out", "g_xa", "g_mem", "w_xq",
                "w_xkv", "w_xo", "g_ffn", "w_up", "w_ffconv", "w_down", "g_final")


def _block_diag(w_pool):
    z = jnp.zeros((64, 64), F32)
    return jnp.concatenate(
        [jnp.concatenate([w_pool[g] if c == g else z for c in range(4)], axis=1) for g in range(4)], axis=0)


def kernel(x, mem, positions, g_mix, w_in, b_forget, w_sconv, w_pool, pool_scale, w_out, g_xa, g_mem, w_xq, w_xkv, w_xo, g_ffn, w_up, w_ffconv, w_down, g_final, loss_target, m_g_mix, m_w_in, m_b_forget, m_w_sconv, m_w_pool, m_pool_scale, m_w_out, m_g_xa, m_g_mem, m_w_xq, m_w_xkv, m_w_xo, m_g_ffn, m_w_up, m_w_ffconv, m_w_down, m_g_final, v_g_mix, v_w_in, v_b_forget, v_w_sconv, v_w_pool, v_pool_scale, v_w_out, v_g_xa, v_g_mem, v_w_xq, v_w_xkv, v_w_xo, v_g_ffn, v_w_up, v_w_ffconv, v_w_down, v_g_final):
    weights = dict(g_mix=g_mix, w_in=w_in, b_forget=b_forget, w_sconv=w_sconv, w_pool=w_pool, pool_scale=pool_scale,
                   w_out=w_out, g_xa=g_xa, g_mem=g_mem, w_xq=w_xq, w_xkv=w_xkv, w_xo=w_xo, g_ffn=g_ffn, w_up=w_up,
                   w_ffconv=w_ffconv, w_down=w_down, g_final=g_final)
    m_in = dict(g_mix=m_g_mix, w_in=m_w_in, b_forget=m_b_forget, w_sconv=m_w_sconv, w_pool=m_w_pool,
                pool_scale=m_pool_scale, w_out=m_w_out, g_xa=m_g_xa, g_mem=m_g_mem, w_xq=m_w_xq, w_xkv=m_w_xkv,
                w_xo=m_w_xo, g_ffn=m_g_ffn, w_up=m_w_up, w_ffconv=m_w_ffconv, w_down=m_w_down, g_final=m_g_final)
    v_in = dict(g_mix=v_g_mix, w_in=v_w_in, b_forget=v_b_forget, w_sconv=v_w_sconv, w_pool=v_w_pool,
                pool_scale=v_pool_scale, w_out=v_w_out, g_xa=v_g_xa, g_mem=v_g_mem, w_xq=v_w_xq, w_xkv=v_w_xkv,
                w_xo=v_w_xo, g_ffn=v_g_ffn, w_up=v_w_up, w_ffconv=v_w_ffconv, w_down=v_w_down, g_final=v_g_final)
    depth = w_in.shape[0]
    me = 4 * lax.axis_index("x") + 2 * lax.axis_index("y") + lax.axis_index("c")
    h = x[0]
    memv = mem[0]
    s = h.shape[0]
    tq = min(ATT_TQ, s)
    tables = {"rope": _rope_tables(positions[0]), "dil": _bias_tables("dil", tq, tq),
              "fox": _bias_tables("fox", tq, tq)}

    w_in_r = _relay_in_cols(w_in)
    conv_shard = jnp.concatenate([w_sconv.reshape(-1), w_ffconv.reshape(-1)])
    conv_shard = jnp.concatenate([conv_shard, jnp.zeros((CONV_WORDS - conv_shard.shape[0],), F32)])
    conv_bits = lax.bitcast_convert_type(conv_shard, BF16).reshape(2 * CONV_WORDS // 1024, 1024)
    gathered = []
    order = jnp.zeros((), F32)
    for l in range(depth):
        shards = dict(w_in=w_in_r[l], w_out=w_out[l], w_xq=w_xq[l], w_xo=w_xo[l], w_xkv=w_xkv[l], w_up=w_up[l],
                      w_down=w_down[l])
        states = {}
        for group in ("in", "rest"):
            shards[GROUPS[group][0]] = shards[GROUPS[group][0]] + order
            xs = [_place_shard(shards[name], me, BF16, "place_%s_%d" % (name, l)) for name in GROUPS[group]]
            if l == 0 and group == "in":
                xs.append(_place_shard(conv_bits, me, BF16, "place_conv"))
            states[group], token = _exchange_start(xs, False, "gather_%s_start_%d" % (group, l))
            order = order + token[0, 0]
        gathered.append(_GatheredWeights(states, l))
    gathered[0].need("in", tables["rope"][0])
    conv_all = lax.bitcast_convert_type(gathered[0].extra[0].reshape(N_DEV, CONV_WORDS, 2), F32)
    n_sc = depth * 3 * (GROUP // N_DEV)
    sconv_full = conv_all[:, :n_sc].reshape(N_DEV, depth, 3, GROUP // N_DEV).transpose(1, 2, 0, 3).reshape(
        depth, 3, GROUP)
    ffconv_full = conv_all[:, n_sc:n_sc + depth * 3 * FF_SHARD].reshape(N_DEV, depth, 3, FF_SHARD).transpose(
        1, 0, 2, 3)

    smalls = []
    for l in range(depth):
        smalls.append(dict(
            g_mix=g_mix[l], g_xa=g_xa[l], g_mem=g_mem[l], g_ffn=g_ffn[l], pool_scale=pool_scale[l],
            w_pool_bd=_block_diag(w_pool[l]), w_sconv=sconv_full[l], w_ffconv=ffconv_full[l],
            b_forget_pad=jnp.concatenate([b_forget[l], jnp.zeros((128 - N_HEADS,), F32)]).reshape(1, 128)))
    smalls[0]["g_mix"] = smalls[0]["g_mix"] + order

    saved = []
    for l in range(depth):
        h, sv = _layer_fwd(h, memv, gathered[l], smalls[l], tables)
        saved.append(sv)
    loss_part, dh, dg_final = _loss_head(h, g_final, loss_target[0])
    loss = lax.psum(loss_part[0, 0], MESH_AXES)

    small_grads = [None] * depth
    scatters = {}

    def pieces_of(big, group):
        return [big[name].reshape(PIECE_SHAPES[name]) for name in GROUPS[group]]

    for l in reversed(range(depth)):
        def rest_ready(big, l=l):
            scatters[l, "rest"], token = _exchange_start(pieces_of(big, "rest"), True, "scatter_rest_start_%d" % l)
            return token[0, 0]

        dh, big, small_grads[l] = _layer_bwd(dh, memv, gathered[l], smalls[l], tables, saved[l], rest_ready)
        xs = pieces_of(big, "in")
        if l == 0:
            flat = [small_grads[ll][n].reshape(-1) for n in SMALL_NAMES for ll in range(depth)]
            flat = jnp.concatenate(flat + [dg_final.reshape(-1)])
            flat = jnp.concatenate([flat, jnp.zeros((SMALL_WORDS - flat.shape[0],), F32)])
            xs.append(jnp.broadcast_to(flat.reshape(1, -1, 1024), (N_DEV, SMALL_WORDS // 1024, 1024)))
        scatters[l, "in"], token = _exchange_start(xs, True, "scatter_in_start_%d" % l)
        if l > 0:
            smalls[l - 1]["w_ffconv"] = smalls[l - 1]["w_ffconv"] + token[0, 0]
    grad_x = dh[None]

    parts, owns = {}, {}

    def wait_group(group, after):
        extra = None
        for l in reversed(range(depth)):
            got, given = _exchange_wait(scatters[l, group], after, "scatter_%s_wait_%d" % (group, l))
            for name, g, x in zip(GROUPS[group], got, given):
                parts.setdefault(name, [None] * depth)[l] = g
                owns.setdefault(name, [None] * depth)[l] = x
            extra = (got[len(GROUPS[group]):], given[len(GROUPS[group]):])
        return extra

    results = {}

    def update(name, w3, m3, v3):
        outs = _adamw(parts[name], owns.get(name), me, w3, m3, v3, "adamw_" + name)
        results[name] = [o.reshape(weights[name].shape) for o in outs]

    wait_group("rest", grad_x)
    for name in GROUPS["rest"]:
        update(name, weights[name], m_in[name], v_in[name])
    small_got, small_given = wait_group("in", results["w_down"][1])
    small_all = lax.dynamic_update_slice_in_dim(small_got[0], small_given[0][:1], me, axis=0).reshape(N_DEV, -1)
    outs = _adamw(parts["w_in"], owns["w_in"], me, w_in_r, _relay_in_cols(m_w_in), _relay_in_cols(v_w_in),
                  "adamw_w_in")
    results["w_in"] = [_unrelay_in_cols(o) for o in outs]
    off = 0
    for name in SMALL_NAMES + ("g_final",):
        wv = weights[name]
        full_shape = {"w_sconv": (depth, 3, GROUP), "w_ffconv": (depth, N_DEV, 3, FF_SHARD)}.get(name, wv.shape)
        n = 1
        for dim in full_shape:
            n *= dim
        p = small_all[:, off:off + n].reshape((N_DEV,) + tuple(full_shape))
        off += n
        if name == "w_sconv":
            p = lax.dynamic_slice_in_dim(p, me * (GROUP // N_DEV), GROUP // N_DEV, axis=3)
        elif name == "w_ffconv":
            p = lax.dynamic_index_in_dim(p, me, axis=2, keepdims=False)
        shape3 = (1, 1, wv.shape[0]) if wv.ndim == 1 else (1, -1, wv.shape[-1])
        w3 = wv.reshape(shape3)
        parts[name] = [p.reshape((N_DEV,) + w3.shape[1:])]
        update(name, w3, m_in[name].reshape(shape3), v_in[name].reshape(shape3))

    return (loss, grad_x, *[results[n][0] for n in WEIGHT_NAMES], *[results[n][1] for n in WEIGHT_NAMES],
            *[results[n][2] for n in WEIGHT_NAMES], *[results[n][3] for n in WEIGHT_NAMES])
```

```python
import functools

import jax
import jax.numpy as jnp
from jax import lax
from jax.experimental import pallas as pl
from jax.experimental.pallas import tpu as pltpu

F32 = jnp.float32
BF16 = jnp.bfloat16

N_DEV = 8
D_MODEL = 1024
GROUP = 256
HEAD_DIM = 64
N_HEADS = 4
N_IN = 2564
N_IN_PAD = 2688
COL_GATE = 2560
XA_HEADS = 4
XA_DIM = 256
MEM_LEN = 256
D_FF = 2816
FF_SHARD = 704
FF_HALF = 4
ROPE_THETA = 500000.0
ROPE_DIM = 16
RMS_EPS = 1e-6
NEG = -1e30
POOL_WINDOWS = (2, 4, 8, 16)
ADAM_LR, ADAM_B1, ADAM_B2, ADAM_EPS, ADAM_WD, ADAM_STEP = 0.001, 0.9, 0.999, 1e-08, 0.01, 10

ROW_TILE = 512
MM_TILE = 1024
ATT_TQ = 256
BWD_HEADS = 4
VMEM_LIMIT = 56 * 1024 * 1024
ADAMW_BLOCK_BYTES = 4 * 1024 * 1024
PLACE_BLOCK_BYTES = 4 * 1024 * 1024

MESH_AXES = ("x", "y", "c")


def _params(**kw):
    return pltpu.CompilerParams(vmem_limit_bytes=VMEM_LIMIT, **kw)


HBM_SPEC = pl.BlockSpec(memory_space=pltpu.HBM)
SEM_SPEC = pl.BlockSpec(memory_space=pltpu.SEMAPHORE)
DATAFLOW = pltpu.SideEffectType.DATAFLOW_SIDE_EFFECTING


def _peer_copies(x_ref, land_ref, send_sems, recv_sems, scatter):
    mx, my, mc = lax.axis_index("x"), lax.axis_index("y"), lax.axis_index("c")
    me = 4 * mx + 2 * my + mc
    pairs = []
    for k in range(1, N_DEV):
        kx, ky, kc = (k >> 2) & 1, (k >> 1) & 1, k & 1
        peer_lin = me ^ k
        send = pltpu.make_async_remote_copy(
            src_ref=x_ref.at[peer_lin] if scatter else land_ref.at[me], dst_ref=land_ref.at[me],
            send_sem=send_sems.at[k - 1], recv_sem=recv_sems.at[k - 1],
            device_id=(mx ^ kx, my ^ ky, mc ^ kc), device_id_type=pl.DeviceIdType.MESH)
        arrival = pltpu.make_async_remote_copy(
            src_ref=land_ref.at[peer_lin], dst_ref=land_ref.at[peer_lin],
            send_sem=send_sems.at[k - 1], recv_sem=recv_sems.at[k - 1],
            device_id=(mx, my, mc), device_id_type=pl.DeviceIdType.MESH)
        pairs.append((send, arrival))
    return pairs


def _exchange_start(xs, scatter, name):
    n = len(xs)
    ns = n if scatter else 0

    def body(*refs):
        srcs = refs[:ns] if scatter else (None,) * n
        lands, sends, recvs = refs[ns:ns + n], refs[ns + n:ns + 2 * n], refs[ns + 2 * n:ns + 3 * n]
        for t in range(n):
            for send, _ in _peer_copies(srcs[t], lands[t], sends[t], recvs[t], scatter):
                send.start()
        token = refs[-1]
        token[...] = jnp.zeros_like(token)

    sems = pltpu.SemaphoreType.DMA((N_DEV - 1,))
    operands = [pltpu.with_memory_space_constraint(x, pltpu.HBM) for x in xs]
    if scatter:
        operands += [pltpu.with_memory_space_constraint(lax.empty(x.shape, x.dtype), pltpu.HBM) for x in xs]
    outs = pl.pallas_call(
        body, name=name,
        out_shape=(sems,) * (2 * n) + tuple(pltpu.HBM(a.shape, a.dtype) for a in operands)
        + (jax.ShapeDtypeStruct((8, 128), F32),),
        in_specs=(HBM_SPEC,) * (ns + n),
        out_specs=(SEM_SPEC,) * (2 * n) + (HBM_SPEC,) * (ns + n) + (pl.BlockSpec(memory_space=pltpu.VMEM),),
        input_output_aliases={i: 2 * n + i for i in range(ns + n)},
        compiler_params=pltpu.CompilerParams(has_side_effects=DATAFLOW),
    )(*operands)
    return (outs[:-1], scatter), outs[-1]


def _exchange_wait(state, after, name):
    held, scatter = state
    n = len(held) // (4 if scatter else 3)
    ns = n if scatter else 0
    sems, thru = held[:2 * n], held[2 * n:]

    def body(*refs):
        srcs = refs[:ns] if scatter else (None,) * n
        lands, sends, recvs = refs[ns:ns + n], refs[ns + n:ns + 2 * n], refs[ns + 2 * n:ns + 3 * n]
        for t in range(n):
            for send, arrival in _peer_copies(srcs[t], lands[t], sends[t], recvs[t], scatter):
                send.wait_send()
                arrival.wait_recv()

    outs = pl.pallas_call(
        body, name=name,
        out_shape=tuple(pltpu.HBM(a.shape, a.dtype) for a in thru),
        in_specs=(HBM_SPEC,) * (ns + n) + (SEM_SPEC,) * (2 * n) + (pl.BlockSpec(memory_space=pl.ANY),),
        out_specs=(HBM_SPEC,) * (ns + n), input_output_aliases={i: i for i in range(ns + n)},
        compiler_params=pltpu.CompilerParams(has_side_effects=DATAFLOW),
    )(*thru, *sems, after)
    return list(outs[ns:]), list(outs[:ns])


def _place_shard(x, me, dtype, name):
    r, c = x.shape
    tr = r
    if r * c * 4 > PLACE_BLOCK_BYTES:
        for cand in (512, 256, 128, 64, 32, 16):
            if r % cand == 0 and cand * c * 4 <= PLACE_BLOCK_BYTES:
                tr = cand
                break

    def body(me_ref, x_ref, o_ref):
        o_ref[...] = x_ref[...].astype(o_ref.dtype)

    return pl.pallas_call(
        body, name=name, out_shape=jax.ShapeDtypeStruct((N_DEV, r, c), dtype),
        grid_spec=pltpu.PrefetchScalarGridSpec(
            num_scalar_prefetch=1, grid=(r // tr,),
            in_specs=[pl.BlockSpec((tr, c), lambda i, me_ref: (i, 0))],
            out_specs=pl.BlockSpec((None, tr, c), lambda i, me_ref: (me_ref[0], i, 0))),
        compiler_params=_params(),
    )(me.reshape(1), x)


NN = ((1,), (0,))
NT = ((1,), (1,))
TN = ((0,), (0,))


def _matmul(a, b, out_shape, *, grid, a_spec, b_spec, o_spec, dims, nred, name, res=None, res_spec=None,
            out_dtype=F32, slabs=0):
    has_res = res is not None

    def body(*refs):
        a_ref, b_ref = refs[0], refs[1]
        r_ref = refs[2] if has_res else None
        o_ref = refs[3] if has_res else refs[2]
        if slabs:
            part = None
            for n in range(slabs):
                term = lax.dot_general(a_ref[n].astype(BF16), b_ref[n].astype(BF16), (dims, ((), ())),
                                       preferred_element_type=F32)
                part = term if part is None else part + term
        else:
            part = lax.dot_general(a_ref[...].astype(BF16), b_ref[...].astype(BF16), (dims, ((), ())),
                                   preferred_element_type=F32)
        if nred == 1:
            if has_res:
                part = part + r_ref[...]
            o_ref[...] = part.astype(o_ref.dtype)
        else:
            acc = refs[-1]
            r = pl.program_id(2)

            @pl.when(r == 0)
            def _():
                acc[...] = part

            @pl.when(r > 0)
            def _():
                acc[...] += part

            @pl.when(r == nred - 1)
            def _():
                tot = acc[...]
                if has_res:
                    tot = tot + r_ref[...]
                o_ref[...] = tot.astype(o_ref.dtype)

    in_specs = [a_spec, b_spec] + ([res_spec] if has_res else [])
    args = (a, b) + ((res,) if has_res else ())
    acc_shape = tuple(d for d in o_spec.block_shape if d is not None)
    return pl.pallas_call(
        body, name=name, grid=grid, out_shape=jax.ShapeDtypeStruct(out_shape, out_dtype),
        in_specs=in_specs, out_specs=o_spec,
        scratch_shapes=[pltpu.VMEM(acc_shape, F32)] if nred > 1 else [],
        compiler_params=_params(),
    )(*args)


def _mm_nn(a, w, name, res=None, tn=None, out_dtype=F32):
    m, k = a.shape
    n = w.shape[1]
    tn = tn or n
    tm = min(MM_TILE, m)
    ospec = pl.BlockSpec((tm, tn), lambda i, j, r: (i, j))
    return _matmul(a, w, (m, n), grid=(m // tm, n // tn, 1),
                   a_spec=pl.BlockSpec((tm, k), lambda i, j, r: (i, 0)),
                   b_spec=pl.BlockSpec((k, tn), lambda i, j, r: (0, j)),
                   o_spec=ospec, dims=NN, nred=1, name=name, res=res, res_spec=ospec if res is not None else None,
                   out_dtype=out_dtype)


def _mm_nt(a, w, name, out_dtype=F32):
    m, n = a.shape
    k = w.shape[0]
    tm = min(MM_TILE, m)
    return _matmul(a, w, (m, k), grid=(m // tm, 1, 1),
                   a_spec=pl.BlockSpec((tm, n), lambda i, j, r: (i, 0)),
                   b_spec=pl.BlockSpec((k, n), lambda i, j, r: (0, 0)),
                   o_spec=pl.BlockSpec((tm, k), lambda i, j, r: (i, 0)), dims=NT, nred=1, name=name,
                   out_dtype=out_dtype)


GRAD_DTYPE = BF16


def _mm_tn(a, b, name, tk=512, tn=None):
    s, k = a.shape
    n = b.shape[1]
    tn = tn or n
    tk = min(tk, k)
    ts = s if b.dtype == BF16 else max(s // 2, 1)
    return _matmul(a, b, (k, n), grid=(k // tk, n // tn, s // ts),
                   a_spec=pl.BlockSpec((ts, tk), lambda i, j, r: (r, i)),
                   b_spec=pl.BlockSpec((ts, tn), lambda i, j, r: (r, j)),
                   o_spec=pl.BlockSpec((tk, tn), lambda i, j, r: (i, j)), dims=TN, nred=s // ts, name=name,
                   out_dtype=GRAD_DTYPE)


def _rms_fwd(h, g, name):
    s, d = h.shape
    tm = min(ROW_TILE, s)

    def body(h_ref, g_ref, o_ref):
        hv = h_ref[...]
        r = lax.rsqrt(jnp.mean(hv * hv, axis=-1, keepdims=True) + RMS_EPS)
        o_ref[...] = (hv * r * g_ref[...]).astype(o_ref.dtype)

    return pl.pallas_call(
        body, name=name, grid=(s // tm,), out_shape=jax.ShapeDtypeStruct((s, d), BF16),
        in_specs=[pl.BlockSpec((tm, d), lambda i: (i, 0)), pl.BlockSpec((1, d), lambda i: (0, 0))],
        out_specs=pl.BlockSpec((tm, d), lambda i: (i, 0)), compiler_params=_params(),
    )(h, g.reshape(1, d))


def _rms_bwd(dy, h, g, res, name):
    s, d = h.shape
    tm = min(ROW_TILE, s)
    has_res = res is not None

    def body(*refs):
        dy_ref, h_ref, g_ref = refs[:3]
        r_ref = refs[3] if has_res else None
        dh_ref, dg_ref = refs[-2], refs[-1]
        hv = h_ref[...]
        r = lax.rsqrt(jnp.mean(hv * hv, axis=-1, keepdims=True) + RMS_EPS)
        hn = hv * r
        dyv = dy_ref[...].astype(F32)
        u = dyv * g_ref[...]
        dh = r * (u - hn * jnp.mean(u * hn, axis=-1, keepdims=True))
        if has_res:
            dh = dh + r_ref[...]
        dh_ref[...] = dh
        part = jnp.sum(dyv * hn, axis=0, keepdims=True)

        @pl.when(pl.program_id(0) == 0)
        def _():
            dg_ref[...] = part

        @pl.when(pl.program_id(0) > 0)
        def _():
            dg_ref[...] += part

    row = pl.BlockSpec((tm, d), lambda i: (i, 0))
    vec = pl.BlockSpec((1, d), lambda i: (0, 0))
    dh, dg = pl.pallas_call(
        body, name=name, grid=(s // tm,),
        out_shape=(jax.ShapeDtypeStruct((s, d), F32), jax.ShapeDtypeStruct((1, d), F32)),
        in_specs=[row, row, vec] + ([row] if has_res else []),
        out_specs=(row, vec), compiler_params=_params(),
    )(*((dy, h, g.reshape(1, d)) + ((res,) if has_res else ())))
    return dh, dg.reshape(d)


def _loss_head(h, g, target):
    s, d = h.shape
    tm = min(ROW_TILE, s)

    def body(h_ref, g_ref, t_ref, loss_ref, dh_ref, dg_ref):
        hv = h_ref[...]
        r = lax.rsqrt(jnp.mean(hv * hv, axis=-1, keepdims=True) + RMS_EPS)
        hn = hv * r
        gv = g_ref[...]
        err = hn * gv - t_ref[...]
        rows = jnp.mean(err * err, axis=-1, keepdims=True)
        lpart = 0.5 * jnp.sum(rows, axis=0, keepdims=True) + jnp.zeros((1, 128), F32)
        dy = err * (1.0 / d)
        u = dy * gv
        dh_ref[...] = r * (u - hn * jnp.mean(u * hn, axis=-1, keepdims=True))
        gpart = jnp.sum(dy * hn, axis=0, keepdims=True)

        @pl.when(pl.program_id(0) == 0)
        def _():
            dg_ref[...] = gpart
            loss_ref[...] = lpart

        @pl.when(pl.program_id(0) > 0)
        def _():
            dg_ref[...] += gpart
            loss_ref[...] += lpart

    row = pl.BlockSpec((tm, d), lambda i: (i, 0))
    vec = pl.BlockSpec((1, d), lambda i: (0, 0))
    return pl.pallas_call(
        body, name="loss_head", grid=(s // tm,),
        out_shape=(jax.ShapeDtypeStruct((1, 128), F32), jax.ShapeDtypeStruct((s, d), F32),
                   jax.ShapeDtypeStruct((1, d), F32)),
        in_specs=[row, vec, row],
        out_specs=(pl.BlockSpec((1, 128), lambda i: (0, 0)), row, vec), compiler_params=_params(),
    )(h, g.reshape(1, d), target)


def _shift_down(x, k):
    return pltpu.roll(x, k, 0)


def _shift_up(x, k):
    return pltpu.roll(x, x.shape[0] - k, 0)


def _conv3(x, w):
    return w[2:3, :] * x + w[1:2, :] * _shift_down(x, 1) + w[0:1, :] * _shift_down(x, 2)


def _conv3_t(x, w):
    return w[2:3, :] * x + w[1:2, :] * _shift_up(x, 1) + w[0:1, :] * _shift_up(x, 2)


def _sigmoid(x):
    return 1.0 / (1.0 + jnp.exp(-x))


def _prev_map(tile, halo, col):
    return lambda i: (jnp.maximum(i * (tile // halo) - 1, 0), col)


def _next_map(tile, halo, col, nrows):
    return lambda i: (jnp.minimum((i + 1) * (tile // halo), nrows // halo - 1), col)


def _sconv_fwd(proj, w):
    s = proj.shape[0]
    t = min(ROW_TILE, s)

    def body(cur_ref, prev_ref, w_ref, o_ref):
        i = pl.program_id(0)
        prev = prev_ref[...] * (i > 0).astype(F32)
        ext = jnp.concatenate([prev, cur_ref[...]], axis=0)
        sv = ext[:, 2 * GROUP:3 * GROUP] * ext[:, 0:GROUP]
        y = ext[:, GROUP:2 * GROUP] * _conv3(sv, w_ref[...])
        o_ref[...] = y[8:].astype(o_ref.dtype)

    return pl.pallas_call(
        body, name="sconv_fwd", grid=(s // t,), out_shape=jax.ShapeDtypeStruct((s, 4 * GROUP), BF16),
        in_specs=[pl.BlockSpec((t, 3 * GROUP), lambda i: (i, 0)),
                  pl.BlockSpec((8, 3 * GROUP), _prev_map(t, 8, 0)),
                  pl.BlockSpec((3, GROUP), lambda i: (0, 0))],
        out_specs=pl.BlockSpec((t, GROUP), lambda i: (i, 0)), compiler_params=_params(),
    )(proj, proj, w)


def _sconv_bwd(proj, w, dy):
    s = proj.shape[0]
    t = min(ROW_TILE, s)
    nt = s // t

    def body(cur_ref, prev_ref, next_ref, w_ref, dy_ref, dyn_ref, dp_ref, dw_ref):
        i = pl.program_id(0)
        first = (i > 0).astype(F32)
        last = (i < nt - 1).astype(F32)
        ext = jnp.concatenate([prev_ref[...] * first, cur_ref[...], next_ref[...] * last], axis=0)
        dye = jnp.concatenate([jnp.zeros((8, GROUP), F32), dy_ref[...], dyn_ref[...] * last], axis=0)
        hv, bv, cv = ext[:, 0:GROUP], ext[:, GROUP:2 * GROUP], ext[:, 2 * GROUP:3 * GROUP]
        wv = w_ref[...]
        sv = cv * hv
        conv = _conv3(sv, wv)
        dconv = dye * bv
        ds = _conv3_t(dconv, wv)
        dp = jnp.concatenate([ds * cv, dye * conv, ds * hv], axis=1)
        dp_ref[...] = dp[8:8 + t].astype(dp_ref.dtype)
        dc = dconv[8:8 + t]
        dw = jnp.concatenate([
            jnp.sum(dc * _shift_down(sv, 2)[8:8 + t], axis=0, keepdims=True),
            jnp.sum(dc * _shift_down(sv, 1)[8:8 + t], axis=0, keepdims=True),
            jnp.sum(dc * sv[8:8 + t], axis=0, keepdims=True),
            jnp.zeros((5, GROUP), F32)], axis=0)

        @pl.when(i == 0)
        def _():
            dw_ref[...] = dw

        @pl.when(i > 0)
        def _():
            dw_ref[...] += dw

    dp, dw = pl.pallas_call(
        body, name="sconv_bwd", grid=(nt,),
        out_shape=(jax.ShapeDtypeStruct((s, N_IN_PAD), BF16), jax.ShapeDtypeStruct((8, GROUP), F32)),
        in_specs=[pl.BlockSpec((t, 3 * GROUP), lambda i: (i, 0)),
                  pl.BlockSpec((8, 3 * GROUP), _prev_map(t, 8, 0)),
                  pl.BlockSpec((8, 3 * GROUP), _next_map(t, 8, 0, s)),
                  pl.BlockSpec((3, GROUP), lambda i: (0, 0)),
                  pl.BlockSpec((t, GROUP), lambda i: (i, 0)),
                  pl.BlockSpec((8, GROUP), _next_map(t, 8, 0, s))],
        out_specs=(pl.BlockSpec((t, 3 * GROUP), lambda i: (i, 0)), pl.BlockSpec((8, GROUP), lambda i: (0, 0))),
        compiler_params=_params(),
    )(proj, proj, proj, w, dy, dy)
    return dp, dw[:3]


def _lane_window(shape):
    lane = lax.broadcasted_iota(jnp.int32, shape, 1)
    return lane, jnp.where(lane < 64, 2.0, jnp.where(lane < 128, 4.0, jnp.where(lane < 192, 8.0, 16.0)))


def _by_group(lane, s1, s2, s3, s4):
    return jnp.where(lane < 64, s1, jnp.where(lane < 128, s2, jnp.where(lane < 192, s3, s4)))


def _pool_z(ext, row0):
    s1 = ext + _shift_down(ext, 1)
    s2 = s1 + _shift_down(s1, 2)
    s3 = s2 + _shift_down(s2, 4)
    s4 = s3 + _shift_down(s3, 8)
    lane, win = _lane_window(ext.shape)
    tpos = (lax.broadcasted_iota(jnp.int32, ext.shape, 0) + (row0 - 16 + 1)).astype(F32)
    cnt = jnp.maximum(jnp.minimum(tpos, win), 1.0)
    return _by_group(lane, s1, s2, s3, s4) / cnt - ext


ANY_SPEC = pl.BlockSpec(memory_space=pl.ANY)


def _pool_fwd(proj, wbd, scale, ybuf):
    s = proj.shape[0]
    t = min(ROW_TILE, s)
    col = (COL_GATE - GROUP) // GROUP

    def body(cur_ref, prev_ref, w_ref, sc_ref, buf_ref, o_ref):
        i = pl.program_id(0)
        ext = jnp.concatenate([prev_ref[...] * (i > 0).astype(F32), cur_ref[...]], axis=0)
        z = _pool_z(ext, i * t)[16:]
        y = jnp.dot(z.astype(BF16), w_ref[...].astype(BF16), preferred_element_type=F32)
        o_ref[...] = (y * sc_ref[...]).astype(o_ref.dtype)

    return pl.pallas_call(
        body, name="pool_fwd", grid=(s // t,), out_shape=jax.ShapeDtypeStruct(ybuf.shape, ybuf.dtype),
        in_specs=[pl.BlockSpec((t, GROUP), lambda i: (i, col)),
                  pl.BlockSpec((16, GROUP), _prev_map(t, 16, col)),
                  pl.BlockSpec((GROUP, GROUP), lambda i: (0, 0)),
                  pl.BlockSpec((1, GROUP), lambda i: (0, 0)), ANY_SPEC],
        out_specs=pl.BlockSpec((t, GROUP), lambda i: (i, 3)), input_output_aliases={4: 0},
        compiler_params=_params(),
    )(proj, proj, wbd, scale.reshape(1, GROUP), ybuf)


def _pool_bwd(proj, wbd, scale, dy, dbuf):
    s = proj.shape[0]
    t = min(ROW_TILE, s)
    nt = s // t
    col = (COL_GATE - GROUP) // GROUP

    def body(cur_ref, prev_ref, w_ref, sc_ref, dy_ref, dyn_ref, buf_ref, dp_ref, dw_ref, dsc_ref):
        i = pl.program_id(0)
        ext = jnp.concatenate([prev_ref[...] * (i > 0).astype(F32), cur_ref[...]], axis=0)
        z = _pool_z(ext, i * t)[16:]
        wv = w_ref[...].astype(BF16)
        dyc = dy_ref[...]
        dye = jnp.concatenate([dyc, dyn_ref[...] * (i < nt - 1).astype(F32)], axis=0) * sc_ref[...]
        dz = lax.dot_general(dye.astype(BF16), wv, (NT, ((), ())), preferred_element_type=F32)
        lane, win = _lane_window(dz.shape)
        tpos = (lax.broadcasted_iota(jnp.int32, dz.shape, 0) + (i * t + 1)).astype(F32)
        e = dz / jnp.minimum(tpos, win)
        f1 = e + _shift_up(e, 1)
        f2 = f1 + _shift_up(f1, 2)
        f3 = f2 + _shift_up(f2, 4)
        f4 = f3 + _shift_up(f3, 8)
        dp = _by_group(lane, f1, f2, f3, f4) - dz
        dp_ref[...] = dp[:t].astype(dp_ref.dtype)
        zb = z.astype(BF16)
        y = jnp.dot(zb, wv, preferred_element_type=F32)
        dsc = jnp.sum(dyc * y, axis=0, keepdims=True)
        dw = lax.dot_general(zb, dye[:t].astype(BF16), (TN, ((), ())), preferred_element_type=F32)

        @pl.when(i == 0)
        def _():
            dw_ref[...] = dw
            dsc_ref[...] = dsc

        @pl.when(i > 0)
        def _():
            dw_ref[...] += dw
            dsc_ref[...] += dsc

    dp, dw, dsc = pl.pallas_call(
        body, name="pool_bwd", grid=(nt,),
        out_shape=(jax.ShapeDtypeStruct(dbuf.shape, dbuf.dtype), jax.ShapeDtypeStruct((GROUP, GROUP), F32),
                   jax.ShapeDtypeStruct((1, GROUP), F32)),
        in_specs=[pl.BlockSpec((t, GROUP), lambda i: (i, col)),
                  pl.BlockSpec((16, GROUP), _prev_map(t, 16, col)),
                  pl.BlockSpec((GROUP, GROUP), lambda i: (0, 0)),
                  pl.BlockSpec((1, GROUP), lambda i: (0, 0)),
                  pl.BlockSpec((t, GROUP), lambda i: (i, 3)),
                  pl.BlockSpec((16, GROUP), _next_map(t, 16, 3, s)), ANY_SPEC],
        out_specs=(pl.BlockSpec((t, GROUP), lambda i: (i, col)), pl.BlockSpec((GROUP, GROUP), lambda i: (0, 0)),
                   pl.BlockSpec((1, GROUP), lambda i: (0, 0))),
        input_output_aliases={6: 0}, compiler_params=_params(),
    )(proj, proj, wbd, scale.reshape(1, GROUP), dy, dy, dbuf)
    return dp, dw, dsc.reshape(GROUP)


FF_HALO = 16


def _ffn_gate_fwd(u0, w):
    s = u0.shape[1]
    t = min(ROW_TILE, s)

    def body(a_ref, ap_ref, g_ref, gp_ref, wa_ref, wg_ref, o_ref):
        first = (pl.program_id(1) > 0).astype(F32)
        a = _conv3(jnp.concatenate([ap_ref[...] * first, a_ref[...].astype(F32)], axis=0), wa_ref[...])[FF_HALO:]
        g = _conv3(jnp.concatenate([gp_ref[...] * first, g_ref[...].astype(F32)], axis=0), wg_ref[...])[FF_HALO:]
        o_ref[...] = (a * (g * _sigmoid(g))).astype(o_ref.dtype)

    def cur(off):
        return pl.BlockSpec((None, t, FF_SHARD), lambda j, i: (j + off, i, 0))

    def prev(off):
        return pl.BlockSpec((None, FF_HALO, FF_SHARD),
                            lambda j, i: (j + off, jnp.maximum(i * (t // FF_HALO) - 1, 0), 0))

    def wspec(off):
        return pl.BlockSpec((None, 3, FF_SHARD), lambda j, i: (j + off, 0, 0))

    return pl.pallas_call(
        body, name="ffn_gate_fwd", grid=(FF_HALF, s // t),
        out_shape=jax.ShapeDtypeStruct((FF_HALF, s, FF_SHARD), BF16),
        in_specs=[cur(0), prev(0), cur(FF_HALF), prev(FF_HALF), wspec(0), wspec(FF_HALF)],
        out_specs=pl.BlockSpec((None, t, FF_SHARD), lambda j, i: (j, i, 0)), compiler_params=_params(),
    )(u0, u0, u0, u0, w, w)


def _ffn_gate_bwd(u0, w, dact):
    s = u0.shape[1]
    t = min(ROW_TILE, s)
    nt = s // t

    def body(c_ref, p_ref, n_ref, w_ref, d_ref, dn_ref, du_ref, dw_ref):
        i = pl.program_id(1)
        first = (i > 0).astype(F32)
        last = (i < nt - 1).astype(F32)
        dext = jnp.concatenate([jnp.zeros((FF_HALO, FF_SHARD), F32), d_ref[...].astype(F32), dn_ref[...] * last],
                               axis=0)
        ext = [jnp.concatenate([p_ref[n] * first, c_ref[n].astype(F32), n_ref[n] * last], axis=0) for n in range(2)]
        a = _conv3(ext[0], w_ref[0])
        g = _conv3(ext[1], w_ref[1])
        sg = _sigmoid(g)
        silu = g * sg
        dus = (dext * silu, dext * a * (sg + silu * (1.0 - sg)))
        mine = slice(FF_HALO, FF_HALO + t)
        for n in range(2):
            du_ref[n] = _conv3_t(dus[n], w_ref[n])[mine].astype(du_ref.dtype)
            dc = dus[n][mine]
            dw = jnp.concatenate([
                jnp.sum(dc * _shift_down(ext[n], 2)[mine], axis=0, keepdims=True),
                jnp.sum(dc * _shift_down(ext[n], 1)[mine], axis=0, keepdims=True),
                jnp.sum(dc * ext[n][mine], axis=0, keepdims=True),
                jnp.zeros((5, FF_SHARD), F32)], axis=0)

            @pl.when(i == 0)
            def _(n=n, dw=dw):
                dw_ref[n] = dw

            @pl.when(i > 0)
            def _(n=n, dw=dw):
                dw_ref[n] += dw

    def pair(rows, row_map):
        return pl.BlockSpec((2, None, rows, FF_SHARD), lambda j, i: (0, j, row_map(i), 0))

    prev_row = lambda i: jnp.maximum(i * (t // FF_HALO) - 1, 0)
    next_row = lambda i: jnp.minimum((i + 1) * (t // FF_HALO), s // FF_HALO - 1)
    u2 = u0.reshape(2, FF_HALF, s, FF_SHARD)
    du, dw = pl.pallas_call(
        body, name="ffn_gate_bwd", grid=(FF_HALF, nt),
        out_shape=(jax.ShapeDtypeStruct((2, FF_HALF, s, FF_SHARD), BF16),
                   jax.ShapeDtypeStruct((2, FF_HALF, 8, FF_SHARD), F32)),
        in_specs=[pair(t, lambda i: i), pair(FF_HALO, prev_row), pair(FF_HALO, next_row), pair(3, lambda i: 0),
                  pl.BlockSpec((None, t, FF_SHARD), lambda j, i: (j, i, 0)),
                  pl.BlockSpec((None, FF_HALO, FF_SHARD), lambda j, i: (j, next_row(i), 0))],
        out_specs=(pair(t, lambda i: i), pair(8, lambda i: 0)),
        compiler_params=_params(),
    )(u2, u2, u2, w.reshape(2, FF_HALF, 3, FF_SHARD), dact, dact)
    return du.reshape(2 * FF_HALF, s, FF_SHARD), dw.reshape(2 * FF_HALF, 8, FF_SHARD)[:, :3]


def _rope_tables(positions):
    inv_freq = ROPE_THETA ** (-jnp.arange(0, ROPE_DIM, 2, dtype=F32) / ROPE_DIM)
    ang = positions.astype(F32)[:, None] * inv_freq
    cos, sin = jnp.cos(ang), jnp.sin(ang)
    s = positions.shape[0]
    half = ROPE_DIM // 2
    rest = HEAD_DIM - ROPE_DIM
    ca = jnp.concatenate([cos, cos, jnp.ones((s, rest), F32)], axis=1)
    cb = jnp.concatenate([-sin, jnp.zeros((s, HEAD_DIM - half), F32)], axis=1)
    cc = jnp.concatenate([jnp.zeros((s, half), F32), sin, jnp.zeros((s, rest), F32)], axis=1)
    return tuple(jnp.tile(tb, (1, N_HEADS)) for tb in (ca, cb, cc))


QK_WIDE = 128
LANE_CQ, LANE_CK = 64, 67
KT_ROWS = 80


def _three_bf16(x):
    hi = x.astype(BF16).astype(F32)
    mid = (x - hi).astype(BF16).astype(F32)
    lo = (x - hi - mid).astype(BF16).astype(F32)
    return hi, mid, lo


def _heads_split(proj, col, tables, c, name):
    s = proj.shape[0]
    t = min(ROW_TILE, s)
    rope = tables is not None
    wide = c is not None
    width = QK_WIDE if wide else HEAD_DIM

    def body(*refs):
        x_ref = refs[0]
        q_ref, k_ref, v_ref, kt_ref, vt_ref = refs[-5:]
        xv = x_ref[...]
        parts = [xv[:, 0:GROUP], xv[:, GROUP:2 * GROUP], xv[:, 2 * GROUP:3 * GROUP]]
        if rope:
            ca, cb, cc = refs[1][...], refs[2][...], refs[3][...]
            for n in range(2):
                p = parts[n]
                parts[n] = p * ca + pltpu.roll(p, GROUP - 8, 1) * cb + pltpu.roll(p, 8, 1) * cc
        parts[0] = parts[0] * (HEAD_DIM ** -0.5)
        k_t, v_t = parts[1].T, parts[2].T
        ones_row = jnp.where(lax.broadcasted_iota(jnp.int32, (KT_ROWS - HEAD_DIM, t), 0) == 0, 1.0, 0.0)
        lane = lax.broadcasted_iota(jnp.int32, (t, QK_WIDE), 1)
        zeros = jnp.zeros((t, QK_WIDE - HEAD_DIM), F32)
        for h in range(N_HEADS):
            hs = slice(h * HEAD_DIM, (h + 1) * HEAD_DIM)
            qh, kh = parts[0][:, hs], parts[1][:, hs]
            if wide:
                terms = _three_bf16(refs[-6][:, h:h + 1])
                qh = jnp.concatenate([qh, zeros], axis=1)
                kh = jnp.concatenate([kh, zeros], axis=1)
                for n in range(3):
                    qh = jnp.where(lane == LANE_CQ + n, terms[n], jnp.where(lane == LANE_CK + n, 1.0, qh))
                    kh = jnp.where(lane == LANE_CK + n, -terms[n], jnp.where(lane == LANE_CQ + n, 1.0, kh))
            q_ref[h] = qh.astype(q_ref.dtype)
            k_ref[h] = kh.astype(k_ref.dtype)
            v_ref[h] = parts[2][:, hs].astype(v_ref.dtype)
            kt_ref[h] = jnp.concatenate([k_t[hs, :], ones_row], axis=0).astype(kt_ref.dtype)
            vt_ref[h] = v_t[hs, :].astype(vt_ref.dtype)

    tab = pl.BlockSpec((t, GROUP), lambda i: (i, 0))
    qk = pl.BlockSpec((N_HEADS, t, width), lambda i: (0, i, 0))
    heads = pl.BlockSpec((N_HEADS, t, HEAD_DIM), lambda i: (0, i, 0))
    heads_t = pl.BlockSpec((N_HEADS, HEAD_DIM, t), lambda i: (0, 0, i))
    qk_shape = jax.ShapeDtypeStruct((N_HEADS, s, width), BF16)
    return pl.pallas_call(
        body, name=name, grid=(s // t,),
        out_shape=(qk_shape, qk_shape, jax.ShapeDtypeStruct((N_HEADS, s, HEAD_DIM), BF16),
                   jax.ShapeDtypeStruct((N_HEADS, KT_ROWS, s), BF16),
                   jax.ShapeDtypeStruct((N_HEADS, HEAD_DIM, s), BF16)),
        in_specs=[pl.BlockSpec((t, 3 * GROUP), lambda i: (i, col))] + ([tab, tab, tab] if rope else [])
        + ([pl.BlockSpec((t, 128), lambda i: (i, 0))] if wide else []),
        out_specs=(qk, qk, heads, pl.BlockSpec((N_HEADS, KT_ROWS, t), lambda i: (0, 0, i)), heads_t),
        compiler_params=_params(),
    )(*((proj,) + (tuple(tables) if rope else ()) + ((c,) if wide else ())))


def _heads_merge(dqt, dk, dv, tables, name, dbuf, col):
    s = dv.shape[1]
    t = min(ROW_TILE, s)
    rope = tables is not None

    wide = dk.shape[2] == QK_WIDE

    def body(*refs):
        o_ref = refs[n_in + 1]
        dq = jnp.concatenate([refs[0][h, :HEAD_DIM, :] for h in range(N_HEADS)], axis=0).T
        parts = [dq] + [jnp.concatenate([r[h][:, :HEAD_DIM] for h in range(N_HEADS)], axis=1) for r in refs[1:3]]
        parts[0] = parts[0] * (HEAD_DIM ** -0.5)
        if rope:
            ca, cb, cc = refs[3][...], refs[4][...], refs[5][...]
            for n in range(2):
                p = parts[n]
                parts[n] = p * ca + pltpu.roll(p * cb, 8, 1) + pltpu.roll(p * cc, GROUP - 8, 1)
        o_ref[...] = jnp.concatenate(parts, axis=1).astype(o_ref.dtype)
        if wide:
            over_keys = jnp.concatenate([refs[0][h, HEAD_DIM:HEAD_DIM + 8, :] for h in range(N_HEADS)]
                                        + [jnp.zeros((128 - 8 * N_HEADS, t), F32)], axis=0).T
            lane = lax.broadcasted_iota(jnp.int32, (t, 128), 1)
            dc = jnp.zeros((t, 128), F32)
            for h in range(N_HEADS):
                dc = jnp.where(lane == h, over_keys[:, 8 * h:8 * h + 1] - refs[1][h][:, LANE_CK:LANE_CK + 1], dc)
            refs[n_in + 2][...] = dc

    tab = pl.BlockSpec((t, GROUP), lambda i: (i, 0))
    heads = pl.BlockSpec((N_HEADS, t, HEAD_DIM), lambda i: (0, i, 0))
    n_in = 6 if rope else 3
    dspec = pl.BlockSpec((t, 3 * GROUP), lambda i: (i, col))
    dshape = jax.ShapeDtypeStruct(dbuf.shape, dbuf.dtype)
    return pl.pallas_call(
        body, name=name, grid=(s // t,),
        out_shape=(dshape, jax.ShapeDtypeStruct((s, 128), F32)) if wide else dshape,
        in_specs=[pl.BlockSpec((N_HEADS, KT_ROWS, t), lambda i: (0, 0, i)),
                  pl.BlockSpec((N_HEADS, t, dk.shape[2]), lambda i: (0, i, 0)), heads]
        + ([tab, tab, tab] if rope else []) + [ANY_SPEC],
        out_specs=(dspec, pl.BlockSpec((t, 128), lambda i: (i, 0))) if wide else dspec,
        input_output_aliases={n_in: 0}, compiler_params=_params(),
    )(*((dqt, dk, dv) + (tuple(tables) if rope else ()) + (dbuf,)))


def _log_sigmoid(x):
    return jnp.minimum(x, 0.0) - jnp.log(1.0 + jnp.exp(-jnp.abs(x)))


def _scan_rows(x, reverse):
    n = x.shape[0]
    row = lax.broadcasted_iota(jnp.int32, x.shape, 0)
    k = 1
    while k < n:
        if reverse:
            x = x + jnp.where(row < n - k, _shift_up(x, k), 0.0)
        else:
            x = x + jnp.where(row >= k, _shift_down(x, k), 0.0)
        k *= 2
    return x


def _gate_cumsum(proj, bias):
    s = proj.shape[0]
    col = COL_GATE // 128

    def body(z_ref, b_ref, c_ref):
        c_ref[...] = _scan_rows(_log_sigmoid(z_ref[...] + b_ref[...]), False)

    return pl.pallas_call(
        body, name="gate_cumsum", grid=(1,), out_shape=jax.ShapeDtypeStruct((s, 128), F32),
        in_specs=[pl.BlockSpec((s, 128), lambda i: (0, col)), pl.BlockSpec((1, 128), lambda i: (0, 0))],
        out_specs=pl.BlockSpec((s, 128), lambda i: (0, 0)), compiler_params=_params(),
    )(proj, bias)


def _gate_cumsum_bwd(proj, bias, dc, dbuf):
    s = proj.shape[0]
    col = COL_GATE // 128

    def body(z_ref, b_ref, dc_ref, buf_ref, dz_ref, db_ref):
        dlogf = _scan_rows(dc_ref[...], True)
        dz = dlogf * _sigmoid(-(z_ref[...] + b_ref[...]))
        dz_ref[...] = dz.astype(dz_ref.dtype)
        db_ref[...] = jnp.sum(dz, axis=0, keepdims=True)

    return pl.pallas_call(
        body, name="gate_cumsum_bwd", grid=(1,),
        out_shape=(jax.ShapeDtypeStruct(dbuf.shape, dbuf.dtype), jax.ShapeDtypeStruct((1, 128), F32)),
        in_specs=[pl.BlockSpec((s, 128), lambda i: (0, col)), pl.BlockSpec((1, 128), lambda i: (0, 0)),
                  pl.BlockSpec((s, 128), lambda i: (0, 0)), ANY_SPEC],
        out_specs=(pl.BlockSpec((s, 128), lambda i: (0, col)), pl.BlockSpec((1, 128), lambda i: (0, 0))),
        input_output_aliases={3: 0}, compiler_params=_params(),
    )(proj, bias, dc, dbuf)


DIL_REACH = 2048


def _pair_weight(mode, d):
    if mode == "fox":
        return jnp.where(d >= 0, 1.0, 0.0)
    w1 = jnp.where(jnp.abs(d - 64) <= 64, 1.0, 0.0)
    w2 = jnp.where((d & 3) == 0, jnp.where(jnp.abs(d - 256) <= 256, 1.0, 0.0), 0.0)
    w3 = jnp.where((d & 15) == 0, jnp.where(jnp.abs(d - 1024) <= 1024, 1.0, 0.0), 0.0)
    return w1 + w2 + w3


def _bias_tables(mode, tq, tk):
    nb = 2 if mode == "fox" else DIL_REACH // tk + 1
    n = lax.broadcasted_iota(jnp.int32, (nb, tk, tq), 0)
    key = lax.broadcasted_iota(jnp.int32, (nb, tk, tq), 1)
    query = lax.broadcasted_iota(jnp.int32, (nb, tk, tq), 2)
    w = _pair_weight(mode, n * tk + query - key)
    return jnp.where(w > 0.0, jnp.log(jnp.maximum(w, 1.0)), NEG)


M_INIT = -1e29


def _first_key_chunk(mode, q0, tk):
    if mode == "fox":
        return 0
    return jnp.maximum(q0 - DIL_REACH, 0) // tk


def _attention_fwd(mode, q, k, vt, tab_t, ybuf, col):
    s, width = q.shape[1], q.shape[2]
    tq = min(ATT_TQ, s)
    tk = tq
    nb = tab_t.shape[0]

    def body(q_ref, k_ref, vt_ref, tab_ref, buf_ref, y_ref, o_ref, lse_ref):
        i = pl.program_id(0)
        lo = _first_key_chunk(mode, i * tq, tk)

        def step(c, carry):
            k0 = pl.multiple_of(c * tk, tk)
            tab = tab_ref[jnp.minimum(i - c, nb - 1)]
            scores = [lax.dot_general(k_ref[h, pl.ds(k0, tk), :], q_ref[h], (NT, ((), ())),
                                      preferred_element_type=F32) for h in range(N_HEADS)]
            stats, probs = [], []
            for h in range(N_HEADS):
                m, l = carry[3 * h:3 * h + 2]
                sc = scores[h] + tab
                m_new = jnp.maximum(m, jnp.max(sc, axis=0, keepdims=True))
                alpha = jnp.exp(m - m_new)
                p = jnp.exp(sc - m_new)
                stats.append((m_new, alpha * l + jnp.sum(p, axis=0, keepdims=True), alpha))
                probs.append(p.astype(BF16))
            pv = [jnp.dot(vt_ref[h, :, pl.ds(k0, tk)], probs[h], preferred_element_type=F32) for h in range(N_HEADS)]
            new = []
            for h in range(N_HEADS):
                m_new, l, alpha = stats[h]
                new += [m_new, l, alpha * carry[3 * h + 2] + pv[h]]
            return tuple(new)

        start = (jnp.full((1, tq), M_INIT, F32), jnp.zeros((1, tq), F32), jnp.zeros((HEAD_DIM, tq), F32))
        done = lax.fori_loop(lo, i + 1, step, start * N_HEADS)
        outs = []
        for h in range(N_HEADS):
            m, l, acc = done[3 * h:3 * h + 3]
            outs.append(acc / l)
            lse_ref[h] = m + jnp.log(l)
        out = jnp.concatenate(outs, axis=0).T
        y_ref[...] = out.astype(y_ref.dtype)
        o_ref[...] = out

    rowspec = pl.BlockSpec((N_HEADS, 1, tq), lambda i: (0, 0, i))
    return pl.pallas_call(
        body, name="attention_fwd_" + mode, grid=(s // tq,),
        out_shape=(jax.ShapeDtypeStruct(ybuf.shape, ybuf.dtype), jax.ShapeDtypeStruct((s, GROUP), F32),
                   jax.ShapeDtypeStruct((N_HEADS, 1, s), F32)),
        in_specs=[pl.BlockSpec((N_HEADS, tq, width), lambda i: (0, i, 0)),
                  pl.BlockSpec((N_HEADS, s, width), lambda i: (0, 0, 0)),
                  pl.BlockSpec((N_HEADS, HEAD_DIM, s), lambda i: (0, 0, 0)),
                  pl.BlockSpec((nb, tk, tq), lambda i: (0, 0, 0)), ANY_SPEC],
        out_specs=(pl.BlockSpec((tq, GROUP), lambda i: (i, col)), pl.BlockSpec((tq, GROUP), lambda i: (i, 0)),
                   rowspec),
        input_output_aliases={4: 0}, compiler_params=_params(),
    )(q, k, vt, tab_t, ybuf)


def _attention_delta(o, do, col):
    s = o.shape[0]
    t = min(ROW_TILE, s)

    def body(o_ref, do_ref, delta_ref, dob_ref):
        dov = do_ref[...]
        prod_t = (o_ref[...] * dov).T
        for h in range(N_HEADS):
            hs = slice(h * HEAD_DIM, (h + 1) * HEAD_DIM)
            delta_ref[h] = jnp.sum(prod_t[hs, :], axis=0, keepdims=True)
            dob_ref[h] = dov[:, hs].astype(dob_ref.dtype)

    return pl.pallas_call(
        body, name="attention_delta", grid=(s // t,),
        out_shape=(jax.ShapeDtypeStruct((N_HEADS, 1, s), F32), jax.ShapeDtypeStruct((N_HEADS, s, HEAD_DIM), BF16)),
        in_specs=[pl.BlockSpec((t, GROUP), lambda i: (i, 0)), pl.BlockSpec((t, GROUP), lambda i: (i, col))],
        out_specs=(pl.BlockSpec((N_HEADS, 1, t), lambda i: (0, 0, i)),
                   pl.BlockSpec((N_HEADS, t, HEAD_DIM), lambda i: (0, i, 0))),
        compiler_params=_params(),
    )(o, do)


def _attention_bwd(mode, q, k, v, kt, tab_t, dob, lse, delta):
    s, width = q.shape[1], q.shape[2]
    tq = min(ATT_TQ, s)
    tk = tq
    nq = s // tq
    nb = tab_t.shape[0]

    def body(q_ref, k_ref, v_ref, kt_ref, tab_ref, dob_ref, lse_ref, delta_ref, dqt_ref, dk_ref, dv_ref):
        i = pl.program_id(0)

        @pl.when(i == 0)
        def _():
            dqt_ref[...] = jnp.zeros_like(dqt_ref)

        hi = nq if mode == "fox" else jnp.minimum((i * tk + tk - 1 + DIL_REACH) // tq + 1, nq)
        for h0 in range(0, N_HEADS, BWD_HEADS):
            heads = range(h0, h0 + BWD_HEADS)

            def step(c, carry, heads=heads):
                q0 = pl.multiple_of(c * tq, tq)
                qs = pl.ds(q0, tq)
                tab = tab_ref[jnp.minimum(c - i, nb - 1)]
                qv = [q_ref[h, qs, :] for h in heads]
                dov = [dob_ref[h, qs, :] for h in heads]
                sc = [lax.dot_general(k_ref[h], qv[n], (NT, ((), ())), preferred_element_type=F32)
                      for n, h in enumerate(heads)]
                dp = [lax.dot_general(v_ref[h], dov[n], (NT, ((), ())), preferred_element_type=F32)
                      for n, h in enumerate(heads)]
                pb, dsb = [], []
                for n, h in enumerate(heads):
                    p = jnp.exp(sc[n] + tab - lse_ref[h, :, qs])
                    pb.append(p.astype(BF16))
                    dsb.append((p * (dp[n] - delta_ref[h, :, qs])).astype(BF16))
                new = []
                for n, h in enumerate(heads):
                    new += [carry[2 * n] + jnp.dot(dsb[n], qv[n], preferred_element_type=F32),
                            carry[2 * n + 1] + jnp.dot(pb[n], dov[n], preferred_element_type=F32)]
                for n, h in enumerate(heads):
                    dqt_ref[h, :, qs] += jnp.dot(kt_ref[h], dsb[n], preferred_element_type=F32)
                return tuple(new)

            start = (jnp.zeros((tk, width), F32), jnp.zeros((tk, HEAD_DIM), F32))
            done = lax.fori_loop(i, hi, step, start * BWD_HEADS)
            for n, h in enumerate(heads):
                dk_ref[h] = done[2 * n]
                dv_ref[h] = done[2 * n + 1]

    def full(shape):
        return pl.BlockSpec(shape, lambda i: (0, 0, 0))

    kblk = pl.BlockSpec((N_HEADS, tk, width), lambda i: (0, i, 0))
    vblk = pl.BlockSpec((N_HEADS, tk, HEAD_DIM), lambda i: (0, i, 0))
    return pl.pallas_call(
        body, name="attention_bwd_" + mode, grid=(s // tk,),
        out_shape=(jax.ShapeDtypeStruct((N_HEADS, KT_ROWS, s), F32), jax.ShapeDtypeStruct((N_HEADS, s, width), F32),
                   jax.ShapeDtypeStruct((N_HEADS, s, HEAD_DIM), F32)),
        in_specs=[full((N_HEADS, s, width)), kblk, vblk, pl.BlockSpec((N_HEADS, KT_ROWS, tk), lambda i: (0, 0, i)),
                  full((nb, tk, tq)), full((N_HEADS, s, HEAD_DIM)), full((N_HEADS, 1, s)), full((N_HEADS, 1, s))],
        out_specs=(full((N_HEADS, KT_ROWS, s)), kblk, vblk),
        compiler_params=_params(),
    )(q, k, v, kt, tab_t, dob, lse, delta)


def _xattn_fwd(qx, kvm):
    s = qx.shape[0]
    t = min(ROW_TILE, s)

    def body(q_ref, kv_ref, o_ref):
        for h in range(XA_HEADS):
            qv = q_ref[:, h * XA_DIM:(h + 1) * XA_DIM].astype(BF16)
            kv = kv_ref[h].astype(BF16)
            vv = kv_ref[XA_HEADS + h].astype(BF16)
            sc = lax.dot_general(qv, kv, (NT, ((), ())), preferred_element_type=F32) * (XA_DIM ** -0.5)
            e = jnp.exp(sc - jnp.max(sc, axis=-1, keepdims=True))
            p = e / jnp.sum(e, axis=-1, keepdims=True)
            o_ref[:, h * XA_DIM:(h + 1) * XA_DIM] = jnp.dot(p.astype(BF16), vv,
                                                             preferred_element_type=F32).astype(o_ref.dtype)

    return pl.pallas_call(
        body, name="xattn_fwd", grid=(s // t,), out_shape=jax.ShapeDtypeStruct((s, D_MODEL), BF16),
        in_specs=[pl.BlockSpec((t, D_MODEL), lambda i: (i, 0)),
                  pl.BlockSpec((2 * XA_HEADS, MEM_LEN, XA_DIM), lambda i: (0, 0, 0))],
        out_specs=pl.BlockSpec((t, D_MODEL), lambda i: (i, 0)), compiler_params=_params(),
    )(qx, kvm)


def _xattn_bwd(qx, kvm, do):
    s = qx.shape[0]
    t = min(ROW_TILE, s)

    def body(q_ref, kv_ref, do_ref, dq_ref, dkv_ref):
        i = pl.program_id(0)
        for h in range(XA_HEADS):
            qv = q_ref[:, h * XA_DIM:(h + 1) * XA_DIM].astype(BF16)
            dov = do_ref[:, h * XA_DIM:(h + 1) * XA_DIM].astype(BF16)
            kv = kv_ref[h].astype(BF16)
            vv = kv_ref[XA_HEADS + h].astype(BF16)
            sc = lax.dot_general(qv, kv, (NT, ((), ())), preferred_element_type=F32) * (XA_DIM ** -0.5)
            e = jnp.exp(sc - jnp.max(sc, axis=-1, keepdims=True))
            p = e / jnp.sum(e, axis=-1, keepdims=True)
            dp = lax.dot_general(dov, vv, (NT, ((), ())), preferred_element_type=F32)
            ds = (p * (dp - jnp.sum(p * dp, axis=-1, keepdims=True)) * (XA_DIM ** -0.5)).astype(BF16)
            dq_ref[:, h * XA_DIM:(h + 1) * XA_DIM] = jnp.dot(ds, kv, preferred_element_type=F32).astype(dq_ref.dtype)
            dk = lax.dot_general(ds, qv, (TN, ((), ())), preferred_element_type=F32)
            dv = lax.dot_general(p.astype(BF16), dov, (TN, ((), ())), preferred_element_type=F32)

            @pl.when(i == 0)
            def _(h=h, dk=dk, dv=dv):
                dkv_ref[h] = dk
                dkv_ref[XA_HEADS + h] = dv

            @pl.when(i > 0)
            def _(h=h, dk=dk, dv=dv):
                dkv_ref[h] += dk
                dkv_ref[XA_HEADS + h] += dv

    row = pl.BlockSpec((t, D_MODEL), lambda i: (i, 0))
    kvs = pl.BlockSpec((2 * XA_HEADS, MEM_LEN, XA_DIM), lambda i: (0, 0, 0))
    return pl.pallas_call(
        body, name="xattn_bwd", grid=(s // t,),
        out_shape=(jax.ShapeDtypeStruct((s, D_MODEL), BF16),
                   jax.ShapeDtypeStruct((2 * XA_HEADS, MEM_LEN, XA_DIM), F32)),
        in_specs=[row, kvs, row], out_specs=(row, kvs), compiler_params=_params(),
    )(qx, kvm, do)


def _adamw(parts, owns, me, w, m, v, name):
    nl, r, c = w.shape
    tr = r
    for cand in (256, 128, 64, 32, 16, 8):
        if r % cand == 0 and r > cand and N_DEV * cand * c * 4 <= ADAMW_BLOCK_BYTES:
            tr = cand
            break
    nt = r // tr
    per_layer = N_DEV + (1 if owns is not None else 0)

    def body(me_ref, *refs):
        w_ref, m_ref, v_ref, g_ref, d_ref, nm_ref, nv_ref = refs[nl * per_layer:]
        layer = pl.program_id(0)
        g = None
        for l in range(nl):
            p_refs = refs[l * per_layer:(l + 1) * per_layer]
            gl = None
            for d in range(N_DEV):
                term = p_refs[d][...].astype(F32)
                if owns is not None:
                    term = jnp.where(me_ref[0] == d, p_refs[N_DEV][...].astype(F32), term)
                gl = term if gl is None else gl + term
            g = gl if g is None else jnp.where(layer == l, gl, g)
        mn = ADAM_B1 * m_ref[...] + (1.0 - ADAM_B1) * g
        vn = ADAM_B2 * v_ref[...] + (1.0 - ADAM_B2) * (g * g)
        m_hat = mn / (1.0 - ADAM_B1 ** ADAM_STEP)
        v_hat = vn / (1.0 - ADAM_B2 ** ADAM_STEP)
        g_ref[...] = g
        d_ref[...] = -ADAM_LR * (m_hat / (jnp.sqrt(v_hat) + ADAM_EPS) + ADAM_WD * w_ref[...])
        nm_ref[...] = mn
        nv_ref[...] = vn

    def rows(l, ll, i):
        return jnp.where(ll == l, i, jnp.where(ll < l, 0, nt - 1))

    def part_spec(l, d):
        if owns is None:
            return pl.BlockSpec((None, tr, c), lambda ll, i, me_ref: (d, rows(l, ll, i), 0))
        return pl.BlockSpec((None, tr, c),
                            lambda ll, i, me_ref: (jnp.where(me_ref[0] == d, (d + 1) % N_DEV, d), rows(l, ll, i), 0))

    def own_spec(l):
        return pl.BlockSpec((None, tr, c), lambda ll, i, me_ref: (me_ref[0], rows(l, ll, i), 0))

    in_specs, operands = [], []
    for l in range(nl):
        in_specs += [part_spec(l, d) for d in range(N_DEV)]
        operands += [parts[l]] * N_DEV
        if owns is not None:
            in_specs.append(own_spec(l))
            operands.append(owns[l])
    blk = pl.BlockSpec((None, tr, c), lambda ll, i, me_ref: (ll, i, 0))
    shp = jax.ShapeDtypeStruct((nl, r, c), F32)
    return pl.pallas_call(
        body, name=name, out_shape=(shp, shp, shp, shp),
        grid_spec=pltpu.PrefetchScalarGridSpec(
            num_scalar_prefetch=1, grid=(nl, nt), in_specs=in_specs + [blk, blk, blk],
            out_specs=(blk, blk, blk, blk)),
        compiler_params=_params(),
    )(me.reshape(1), *operands, w, m, v)


GROUPS = {"in": ("w_in",), "rest": ("w_out", "w_xq", "w_xo", "w_xkv", "w_up", "w_down")}
FULL_SHAPES = {"w_in": (D_MODEL, N_IN_PAD), "w_out": (D_MODEL, D_MODEL), "w_xq": (D_MODEL, D_MODEL),
               "w_xo": (D_MODEL, D_MODEL), "w_xkv": (N_DEV, D_MODEL, 2 * D_MODEL // N_DEV),
               "w_up": (N_DEV, D_MODEL, FF_SHARD), "w_down": (FF_HALF, FF_SHARD, D_MODEL)}
PIECE_SHAPES = {"w_in": (N_DEV, D_MODEL // N_DEV, N_IN_PAD), "w_out": (N_DEV, D_MODEL // N_DEV, D_MODEL),
                "w_xq": (N_DEV, D_MODEL // N_DEV, D_MODEL), "w_xo": (N_DEV, D_MODEL // N_DEV, D_MODEL),
                "w_xkv": (N_DEV, D_MODEL, 2 * D_MODEL // N_DEV), "w_up": (N_DEV, D_MODEL, FF_SHARD),
                "w_down": (N_DEV, D_FF // N_DEV, D_MODEL)}
CONV_WORDS = 8192
SMALL_WORDS = 80 * 1024


class _GatheredWeights:
    def __init__(self, states, layer):
        self.states, self.layer, self.full, self.extra = dict(states), layer, {}, None

    def need(self, group, after):
        if group in self.states:
            got, _ = _exchange_wait(self.states.pop(group), after, "gather_%s_wait_%d" % (group, self.layer))
            for name, g in zip(GROUPS[group], got):
                self.full[name] = g.reshape(FULL_SHAPES[name])
            self.extra = got[len(GROUPS[group]):]

    def __getitem__(self, name):
        return self.full[name]


def _relay_in_cols(w):
    pad = jnp.zeros(w.shape[:-1] + (N_IN_PAD - N_IN,), w.dtype)
    return jnp.concatenate([w[..., :2304], w[..., 2308:N_IN], w[..., 2304:2308], pad], axis=-1)


def _unrelay_in_cols(w):
    return jnp.concatenate([w[..., :2304], w[..., COL_GATE:COL_GATE + 4], w[..., 2304:COL_GATE]], axis=-1)


def _layer_fwd(h, memv, w, sm, tables):
    sv = {"h0": h}
    xn = _rms_fwd(h, sm["g_mix"], "rms_mix")
    w.need("in", xn)
    proj = _mm_nn(xn, w["w_in"], "mm_in", tn=896)
    sv["xn"], sv["proj"] = xn, proj
    ycat = _sconv_fwd(proj, sm["w_sconv"])
    qd, kd, vd, ktd, vtd = _heads_split(proj, 1, tables["rope"], None, "split_dil")
    ycat, ob, lse_b = _attention_fwd("dil", qd, kd, vtd, tables["dil"], ycat, 1)
    sv["dil"] = (qd, kd, vd, ktd, ob, lse_b)
    c = _gate_cumsum(proj, sm["b_forget_pad"])
    qf, kf, vf, ktf, vtf = _heads_split(proj, 2, None, c, "split_fox")
    ycat, oc, lse_c = _attention_fwd("fox", qf, kf, vtf, tables["fox"], ycat, 2)
    sv["fox"] = (qf, kf, vf, ktf, oc, lse_c)
    ycat = _pool_fwd(proj, sm["w_pool_bd"], sm["pool_scale"], ycat)
    sv["ycat"] = ycat
    w.need("rest", ycat)
    h1 = _mm_nn(ycat, w["w_out"], "mm_out", res=h)
    sv["h1"] = h1
    xq = _rms_fwd(h1, sm["g_xa"], "rms_xa")
    memn = _rms_fwd(memv, sm["g_mem"], "rms_mem")
    qx = _mm_nn(xq, w["w_xq"], "mm_xq", out_dtype=BF16)
    kvm = _matmul(memn, w["w_xkv"], (N_DEV, MEM_LEN, XA_DIM), grid=(N_DEV, 1, 1),
                  a_spec=pl.BlockSpec((MEM_LEN, D_MODEL), lambda i, j, r: (0, 0)),
                  b_spec=pl.BlockSpec((None, D_MODEL, XA_DIM), lambda i, j, r: (i, 0, 0)),
                  o_spec=pl.BlockSpec((None, MEM_LEN, XA_DIM), lambda i, j, r: (i, 0, 0)),
                  dims=NN, nred=1, name="mm_xkv")
    ox = _xattn_fwd(qx, kvm)
    sv.update(xq=xq, memn=memn, qx=qx, kvm=kvm, ox=ox)
    h2 = _mm_nn(ox, w["w_xo"], "mm_xo", res=h1)
    sv["h2"] = h2
    xf = _rms_fwd(h2, sm["g_ffn"], "rms_ffn")
    s = h.shape[0]
    tm, tb = min(ROW_TILE, s), min(MM_TILE, s)
    u0 = _matmul(xf, w["w_up"], (N_DEV, s, FF_SHARD), grid=(s // tb, N_DEV, 1),
                 a_spec=pl.BlockSpec((tb, D_MODEL), lambda i, j, r: (i, 0)),
                 b_spec=pl.BlockSpec((None, D_MODEL, FF_SHARD), lambda i, j, r: (j, 0, 0)),
                 o_spec=pl.BlockSpec((None, tb, FF_SHARD), lambda i, j, r: (j, i, 0)),
                 dims=NN, nred=1, name="mm_up", out_dtype=BF16)
    act = _ffn_gate_fwd(u0, sm["w_ffconv"])
    sv.update(xf=xf, u0=u0, act=act)
    ospec = pl.BlockSpec((tm, D_MODEL), lambda i, j, r: (i, 0))
    h3 = _matmul(act, w["w_down"], (s, D_MODEL), grid=(s // tm, 1, 1),
                 a_spec=pl.BlockSpec((FF_HALF, tm, FF_SHARD), lambda i, j, r: (0, i, 0)),
                 b_spec=pl.BlockSpec((FF_HALF, FF_SHARD, D_MODEL), lambda i, j, r: (0, 0, 0)),
                 o_spec=ospec, dims=NN, nred=1, slabs=FF_HALF, name="mm_down", res=h2, res_spec=ospec)
    return h3, sv


def _layer_bwd(dh3, memv, w, sm, tables, sv, rest_ready):
    s = dh3.shape[0]
    tm, tb = min(ROW_TILE, s), min(MM_TILE, s)
    big, small = {}, {}
    ts = max(s // 2, 1)
    dact = _matmul(dh3, w["w_down"], (FF_HALF, s, FF_SHARD), grid=(s // tb, FF_HALF, 1),
                   a_spec=pl.BlockSpec((tb, D_MODEL), lambda i, j, r: (i, 0)),
                   b_spec=pl.BlockSpec((None, FF_SHARD, D_MODEL), lambda i, j, r: (j, 0, 0)),
                   o_spec=pl.BlockSpec((None, tb, FF_SHARD), lambda i, j, r: (j, i, 0)),
                   dims=NT, nred=1, name="mm_dact", out_dtype=BF16)
    big["w_down"] = _matmul(sv["act"], dh3, (FF_HALF, FF_SHARD, D_MODEL), grid=(FF_HALF, 1, s // ts),
                            a_spec=pl.BlockSpec((None, ts, FF_SHARD), lambda i, j, r: (i, r, 0)),
                            b_spec=pl.BlockSpec((ts, D_MODEL), lambda i, j, r: (r, 0)),
                            o_spec=pl.BlockSpec((None, FF_SHARD, D_MODEL), lambda i, j, r: (i, 0, 0)),
                            dims=TN, nred=s // ts, name="mm_dw_down", out_dtype=GRAD_DTYPE)
    du0, small["w_ffconv"] = _ffn_gate_bwd(sv["u0"], sm["w_ffconv"], dact)
    dxf = _matmul(du0, w["w_up"], (s, D_MODEL), grid=(s // tm, 1, 1),
                  a_spec=pl.BlockSpec((N_DEV, tm, FF_SHARD), lambda i, j, r: (0, i, 0)),
                  b_spec=pl.BlockSpec((N_DEV, D_MODEL, FF_SHARD), lambda i, j, r: (0, 0, 0)),
                  o_spec=pl.BlockSpec((tm, D_MODEL), lambda i, j, r: (i, 0)),
                  dims=NT, nred=1, slabs=N_DEV, name="mm_dxf")
    big["w_up"] = _matmul(sv["xf"], du0, (N_DEV, D_MODEL, FF_SHARD), grid=(N_DEV, 1, 1),
                          a_spec=pl.BlockSpec((s, D_MODEL), lambda i, j, r: (0, 0)),
                          b_spec=pl.BlockSpec((None, s, FF_SHARD), lambda i, j, r: (i, 0, 0)),
                          o_spec=pl.BlockSpec((None, D_MODEL, FF_SHARD), lambda i, j, r: (i, 0, 0)),
                          dims=TN, nred=1, name="mm_dw_up", out_dtype=GRAD_DTYPE)
    dh2, small["g_ffn"] = _rms_bwd(dxf, sv["h2"], sm["g_ffn"], dh3, "rms_ffn_bwd")
    dox = _mm_nt(dh2, w["w_xo"], "mm_dox", out_dtype=BF16)
    big["w_xo"] = _mm_tn(sv["ox"], dh2, "mm_dw_xo")
    dqx, dkvm = _xattn_bwd(sv["qx"], sv["kvm"], dox)
    dxq = _mm_nt(dqx, w["w_xq"], "mm_dxq")
    big["w_xq"] = _mm_tn(sv["xq"], dqx, "mm_dw_xq")
    big["w_xkv"] = _matmul(sv["memn"], dkvm, (N_DEV, D_MODEL, XA_DIM), grid=(N_DEV, 1, 1),
                           a_spec=pl.BlockSpec((MEM_LEN, D_MODEL), lambda i, j, r: (0, 0)),
                           b_spec=pl.BlockSpec((None, MEM_LEN, XA_DIM), lambda i, j, r: (i, 0, 0)),
                           o_spec=pl.BlockSpec((None, D_MODEL, XA_DIM), lambda i, j, r: (i, 0, 0)),
                           dims=TN, nred=1, name="mm_dw_xkv", out_dtype=GRAD_DTYPE)
    dmemn = _matmul(dkvm, w["w_xkv"], (MEM_LEN, D_MODEL), grid=(1, 1, 1),
                    a_spec=pl.BlockSpec((N_DEV, MEM_LEN, XA_DIM), lambda i, j, r: (0, 0, 0)),
                    b_spec=pl.BlockSpec((N_DEV, D_MODEL, XA_DIM), lambda i, j, r: (0, 0, 0)),
                    o_spec=pl.BlockSpec((MEM_LEN, D_MODEL), lambda i, j, r: (0, 0)),
                    dims=NT, nred=1, slabs=N_DEV, name="mm_dmemn")
    _, small["g_mem"] = _rms_bwd(dmemn, memv, sm["g_mem"], None, "rms_mem_bwd")
    dh1, small["g_xa"] = _rms_bwd(dxq, sv["h1"], sm["g_xa"], dh2, "rms_xa_bwd")
    dycat = _mm_nt(dh1, w["w_out"], "mm_dycat")
    big["w_out"] = _mm_tn(sv["ycat"], dh1, "mm_dw_out")
    proj = sv["proj"]
    dproj, small["w_sconv"] = _sconv_bwd(proj, sm["w_sconv"] + rest_ready(big), dycat)
    qd, kd, vd, ktd, ob, lse_b = sv["dil"]
    delta, dob = _attention_delta(ob, dycat, 1)
    dqt, dk, dv = _attention_bwd("dil", qd, kd, vd, ktd, tables["dil"], dob, lse_b, delta)
    dproj = _heads_merge(dqt, dk, dv, tables["rope"], "merge_dil", dproj, 1)
    qf, kf, vf, ktf, oc, lse_c = sv["fox"]
    delta, dob = _attention_delta(oc, dycat, 2)
    dqt, dk, dv = _attention_bwd("fox", qf, kf, vf, ktf, tables["fox"], dob, lse_c, delta)
    dproj, dc = _heads_merge(dqt, dk, dv, None, "merge_fox", dproj, 2)
    dproj, dbias = _gate_cumsum_bwd(proj, sm["b_forget_pad"], dc, dproj)
    small["b_forget"] = dbias[0, :N_HEADS]
    dproj, dwbd, small["pool_scale"] = _pool_bwd(proj, sm["w_pool_bd"], sm["pool_scale"], dycat, dproj)
    small["w_pool"] = jnp.stack([dwbd[64 * g:64 * (g + 1), 64 * g:64 * (g + 1)] for g in range(4)])
    dxn = _mm_nt(dproj, w["w_in"], "mm_dxn")
    big["w_in"] = _mm_tn(sv["xn"], dproj, "mm_dw_in", tn=896)
    dh0, small["g_mix"] = _rms_bwd(dxn, sv["h0"], sm["g_mix"], dh1, "rms_mix_bwd")
    return dh0, big, small


SMALL_NAMES = ("g_mix", "b_forget", "w_pool", "pool_scale", "g_xa", "g_mem", "g_ffn", "w_sconv", "w_ffconv")
WEIGHT_NAMES = ("g_mix", "w_in", "b_forget", "w_sconv", "w_pool", "pool_scale", "w_out", "g_xa", "g_mem", "w_xq",
                "w_xkv", "w_xo", "g_ffn", "w_up", "w_ffconv", "w_down", "g_final")


def _block_diag(w_pool):
    z = jnp.zeros((64, 64), F32)
    return jnp.concatenate(
        [jnp.concatenate([w_pool[g] if c == g else z for c in range(4)], axis=1) for g in range(4)], axis=0)


def kernel(x, mem, positions, g_mix, w_in, b_forget, w_sconv, w_pool, pool_scale, w_out, g_xa, g_mem, w_xq, w_xkv, w_xo, g_ffn, w_up, w_ffconv, w_down, g_final, loss_target, m_g_mix, m_w_in, m_b_forget, m_w_sconv, m_w_pool, m_pool_scale, m_w_out, m_g_xa, m_g_mem, m_w_xq, m_w_xkv, m_w_xo, m_g_ffn, m_w_up, m_w_ffconv, m_w_down, m_g_final, v_g_mix, v_w_in, v_b_forget, v_w_sconv, v_w_pool, v_pool_scale, v_w_out, v_g_xa, v_g_mem, v_w_xq, v_w_xkv, v_w_xo, v_g_ffn, v_w_up, v_w_ffconv, v_w_down, v_g_final):
    weights = dict(g_mix=g_mix, w_in=w_in, b_forget=b_forget, w_sconv=w_sconv, w_pool=w_pool, pool_scale=pool_scale,
                   w_out=w_out, g_xa=g_xa, g_mem=g_mem, w_xq=w_xq, w_xkv=w_xkv, w_xo=w_xo, g_ffn=g_ffn, w_up=w_up,
                   w_ffconv=w_ffconv, w_down=w_down, g_final=g_final)
    m_in = dict(g_mix=m_g_mix, w_in=m_w_in, b_forget=m_b_forget, w_sconv=m_w_sconv, w_pool=m_w_pool,
                pool_scale=m_pool_scale, w_out=m_w_out, g_xa=m_g_xa, g_mem=m_g_mem, w_xq=m_w_xq, w_xkv=m_w_xkv,
                w_xo=m_w_xo, g_ffn=m_g_ffn, w_up=m_w_up, w_ffconv=m_w_ffconv, w_down=m_w_down, g_final=m_g_final)
    v_in = dict(g_mix=v_g_mix, w_in=v_w_in, b_forget=v_b_forget, w_sconv=v_w_sconv, w_pool=v_w_pool,
                pool_scale=v_pool_scale, w_out=v_w_out, g_xa=v_g_xa, g_mem=v_g_mem, w_xq=v_w_xq, w_xkv=v_w_xkv,
                w_xo=v_w_xo, g_ffn=v_g_ffn, w_up=v_w_up, w_ffconv=v_w_ffconv, w_down=v_w_down, g_final=v_g_final)
    depth = w_in.shape[0]
    me = 4 * lax.axis_index("x") + 2 * lax.axis_index("y") + lax.axis_index("c")
    h = x[0]
    memv = mem[0]
    s = h.shape[0]
    tq = min(ATT_TQ, s)
    tables = {"rope": _rope_tables(positions[0]), "dil": _bias_tables("dil", tq, tq),
              "fox": _bias_tables("fox", tq, tq)}

    w_in_r = _relay_in_cols(w_in)
    conv_shard = jnp.concatenate([w_sconv.reshape(-1), w_ffconv.reshape(-1)])
    conv_shard = jnp.concatenate([conv_shard, jnp.zeros((CONV_WORDS - conv_shard.shape[0],), F32)])
    conv_bits = lax.bitcast_convert_type(conv_shard, BF16).reshape(2 * CONV_WORDS // 1024, 1024)
    gathered = []
    order = jnp.zeros((), F32)
    for l in range(depth):
        shards = dict(w_in=w_in_r[l], w_out=w_out[l], w_xq=w_xq[l], w_xo=w_xo[l], w_xkv=w_xkv[l], w_up=w_up[l],
                      w_down=w_down[l])
        states = {}
        for group in ("in", "rest"):
            shards[GROUPS[group][0]] = shards[GROUPS[group][0]] + order
            xs = [_place_shard(shards[name], me, BF16, "place_%s_%d" % (name, l)) for name in GROUPS[group]]
            if l == 0 and group == "in":
                xs.append(_place_shard(conv_bits, me, BF16, "place_conv"))
            states[group], token = _exchange_start(xs, False, "gather_%s_start_%d" % (group, l))
            order = order + token[0, 0]
        gathered.append(_GatheredWeights(states, l))
    gathered[0].need("in", tables["rope"][0])
    conv_all = lax.bitcast_convert_type(gathered[0].extra[0].reshape(N_DEV, CONV_WORDS, 2), F32)
    n_sc = depth * 3 * (GROUP // N_DEV)
    sconv_full = conv_all[:, :n_sc].reshape(N_DEV, depth, 3, GROUP // N_DEV).transpose(1, 2, 0, 3).reshape(
        depth, 3, GROUP)
    ffconv_full = conv_all[:, n_sc:n_sc + depth * 3 * FF_SHARD].reshape(N_DEV, depth, 3, FF_SHARD).transpose(
        1, 0, 2, 3)

    smalls = []
    for l in range(depth):
        smalls.append(dict(
            g_mix=g_mix[l], g_xa=g_xa[l], g_mem=g_mem[l], g_ffn=g_ffn[l], pool_scale=pool_scale[l],
            w_pool_bd=_block_diag(w_pool[l]), w_sconv=sconv_full[l], w_ffconv=ffconv_full[l],
            b_forget_pad=jnp.concatenate([b_forget[l], jnp.zeros((128 - N_HEADS,), F32)]).reshape(1, 128)))
    smalls[0]["g_mix"] = smalls[0]["g_mix"] + order

    saved = []
    for l in range(depth):
        h, sv = _layer_fwd(h, memv, gathered[l], smalls[l], tables)
        saved.append(sv)
    loss_part, dh, dg_final = _loss_head(h, g_final, loss_target[0])
    loss = lax.psum(loss_part[0, 0], MESH_AXES)

    small_grads = [None] * depth
    scatters = {}

    def pieces_of(big, group):
        return [big[name].reshape(PIECE_SHAPES[name]) for name in GROUPS[group]]

    for l in reversed(range(depth)):
        def rest_ready(big, l=l):
            scatters[l, "rest"], token = _exchange_start(pieces_of(big, "rest"), True, "scatter_rest_start_%d" % l)
            return token[0, 0]

        dh, big, small_grads[l] = _layer_bwd(dh, memv, gathered[l], smalls[l], tables, saved[l], rest_ready)
        xs = pieces_of(big, "in")
        if l == 0:
            flat = [small_grads[ll][n].reshape(-1) for n in SMALL_NAMES for ll in range(depth)]
            flat = jnp.concatenate(flat + [dg_final.reshape(-1)])
            flat = jnp.concatenate([flat, jnp.zeros((SMALL_WORDS - flat.shape[0],), F32)])
            xs.append(jnp.broadcast_to(flat.reshape(1, -1, 1024), (N_DEV, SMALL_WORDS // 1024, 1024)))
        scatters[l, "in"], token = _exchange_start(xs, True, "scatter_in_start_%d" % l)
        if l > 0:
            smalls[l - 1]["w_ffconv"] = smalls[l - 1]["w_ffconv"] + token[0, 0]
    grad_x = dh[None]

    parts, owns = {}, {}

    def wait_group(group, after):
        extra = None
        for l in reversed(range(depth)):
            got, given = _exchange_wait(scatters[l, group], after, "scatter_%s_wait_%d" % (group, l))
            for name, g, x in zip(GROUPS[group], got, given):
                parts.setdefault(name, [None] * depth)[l] = g
                owns.setdefault(name, [None] * depth)[l] = x
            extra = (got[len(GROUPS[group]):], given[len(GROUPS[group]):])
        return extra

    results = {}

    def update(name, w3, m3, v3):
        outs = _adamw(parts[name], owns.get(name), me, w3, m3, v3, "adamw_" + name)
        results[name] = [o.reshape(weights[name].shape) for o in outs]

    wait_group("rest", grad_x)
    for name in GROUPS["rest"]:
        update(name, weights[name], m_in[name], v_in[name])
    small_got, small_given = wait_group("in", results["w_down"][1])
    small_all = lax.dynamic_update_slice_in_dim(small_got[0], small_given[0][:1], me, axis=0).reshape(N_DEV, -1)
    outs = _adamw(parts["w_in"], owns["w_in"], me, w_in_r, _relay_in_cols(m_w_in), _relay_in_cols(v_w_in),
                  "adamw_w_in")
    results["w_in"] = [_unrelay_in_cols(o) for o in outs]
    off = 0
    for name in SMALL_NAMES + ("g_final",):
        wv = weights[name]
        full_shape = {"w_sconv": (depth, 3, GROUP), "w_ffconv": (depth, N_DEV, 3, FF_SHARD)}.get(name, wv.shape)
        n = 1
        for dim in full_shape:
            n *= dim
        p = small_all[:, off:off + n].reshape((N_DEV,) + tuple(full_shape))
        off += n
        if name == "w_sconv":
            p = lax.dynamic_slice_in_dim(p, me * (GROUP // N_DEV), GROUP // N_DEV, axis=3)
        elif name == "w_ffconv":
            p = lax.dynamic_index_in_dim(p, me, axis=2, keepdims=False)
        shape3 = (1, 1, wv.shape[0]) if wv.ndim == 1 else (1, -1, wv.shape[-1])
        w3 = wv.reshape(shape3)
        parts[name] = [p.reshape((N_DEV,) + w3.shape[1:])]
        update(name, w3, m_in[name].reshape(shape3), v_in[name].reshape(shape3))

    return (loss, grad_x, *[results[n][0] for n in WEIGHT_NAMES], *[results[n][1] for n in WEIGHT_NAMES],
            *[results[n][2] for n in WEIGHT_NAMES], *[results[n][3] for n in WEIGHT_NAMES])
```

```python
import functools

import jax
import jax.numpy as jnp
from jax import lax
from jax.experimental import pallas as pl
from jax.experimental.pallas import tpu as pltpu

F32 = jnp.float32
BF16 = jnp.bfloat16

N_DEV = 8
D_MODEL = 1024
GROUP = 256
HEAD_DIM = 64
N_HEADS = 4
N_IN = 2564
N_IN_PAD = 2688
COL_GATE = 2560
XA_HEADS = 4
XA_DIM = 256
MEM_LEN = 256
D_FF = 2816
FF_SHARD = 704
FF_HALF = 4
ROPE_THETA = 500000.0
ROPE_DIM = 16
RMS_EPS = 1e-6
NEG = -1e30
POOL_WINDOWS = (2, 4, 8, 16)
ADAM_LR, ADAM_B1, ADAM_B2, ADAM_EPS, ADAM_WD, ADAM_STEP = 0.001, 0.9, 0.999, 1e-08, 0.01, 10

ROW_TILE = 512
MM_TILE = 1024
ATT_TQ = 256
BWD_HEADS = 4
VMEM_LIMIT = 56 * 1024 * 1024
ADAMW_BLOCK_BYTES = 4 * 1024 * 1024
PLACE_BLOCK_BYTES = 4 * 1024 * 1024

MESH_AXES = ("x", "y", "c")


def _params(**kw):
    return pltpu.CompilerParams(vmem_limit_bytes=VMEM_LIMIT, **kw)


HBM_SPEC = pl.BlockSpec(memory_space=pltpu.HBM)
SEM_SPEC = pl.BlockSpec(memory_space=pltpu.SEMAPHORE)
DATAFLOW = pltpu.SideEffectType.DATAFLOW_SIDE_EFFECTING


def _peer_copies(x_ref, land_ref, send_sems, recv_sems, scatter):
    mx, my, mc = lax.axis_index("x"), lax.axis_index("y"), lax.axis_index("c")
    me = 4 * mx + 2 * my + mc
    pairs = []
    for k in range(1, N_DEV):
        kx, ky, kc = (k >> 2) & 1, (k >> 1) & 1, k & 1
        peer_lin = me ^ k
        send = pltpu.make_async_remote_copy(
            src_ref=x_ref.at[peer_lin] if scatter else land_ref.at[me], dst_ref=land_ref.at[me],
            send_sem=send_sems.at[k - 1], recv_sem=recv_sems.at[k - 1],
            device_id=(mx ^ kx, my ^ ky, mc ^ kc), device_id_type=pl.DeviceIdType.MESH)
        arrival = pltpu.make_async_remote_copy(
            src_ref=land_ref.at[peer_lin], dst_ref=land_ref.at[peer_lin],
            send_sem=send_sems.at[k - 1], recv_sem=recv_sems.at[k - 1],
            device_id=(mx, my, mc), device_id_type=pl.DeviceIdType.MESH)
        pairs.append((send, arrival))
    return pairs


def _exchange_start(xs, scatter, name):
    n = len(xs)
    ns = n if scatter else 0

    def body(*refs):
        srcs = refs[:ns] if scatter else (None,) * n
        lands, sends, recvs = refs[ns:ns + n], refs[ns + n:ns + 2 * n], refs[ns + 2 * n:ns + 3 * n]
        for t in range(n):
            for send, _ in _peer_copies(srcs[t], lands[t], sends[t], recvs[t], scatter):
                send.start()
        token = refs[-1]
        token[...] = jnp.zeros_like(token)

    sems = pltpu.SemaphoreType.DMA((N_DEV - 1,))
    operands = [pltpu.with_memory_space_constraint(x, pltpu.HBM) for x in xs]
    if scatter:
        operands += [pltpu.with_memory_space_constraint(lax.empty(x.shape, x.dtype), pltpu.HBM) for x in xs]
    outs = pl.pallas_call(
        body, name=name,
        out_shape=(sems,) * (2 * n) + tuple(pltpu.HBM(a.shape, a.dtype) for a in operands)
        + (jax.ShapeDtypeStruct((8, 128), F32),),
        in_specs=(HBM_SPEC,) * (ns + n),
        out_specs=(SEM_SPEC,) * (2 * n) + (HBM_SPEC,) * (ns + n) + (pl.BlockSpec(memory_space=pltpu.VMEM),),
        input_output_aliases={i: 2 * n + i for i in range(ns + n)},
        compiler_params=pltpu.CompilerParams(has_side_effects=DATAFLOW),
    )(*operands)
    return (outs[:-1], scatter), outs[-1]


def _exchange_wait(state, after, name):
    held, scatter = state
    n = len(held) // (4 if scatter else 3)
    ns = n if scatter else 0
    sems, thru = held[:2 * n], held[2 * n:]

    def body(*refs):
        srcs = refs[:ns] if scatter else (None,) * n
        lands, sends, recvs = refs[ns:ns + n], refs[ns + n:ns + 2 * n], refs[ns + 2 * n:ns + 3 * n]
        for t in range(n):
            for send, arrival in _peer_copies(srcs[t], lands[t], sends[t], recvs[t], scatter):
                send.wait_send()
                arrival.wait_recv()

    outs = pl.pallas_call(
        body, name=name,
        out_shape=tuple(pltpu.HBM(a.shape, a.dtype) for a in thru),
        in_specs=(HBM_SPEC,) * (ns + n) + (SEM_SPEC,) * (2 * n) + (pl.BlockSpec(memory_space=pl.ANY),),
        out_specs=(HBM_SPEC,) * (ns + n), input_output_aliases={i: i for i in range(ns + n)},
        compiler_params=pltpu.CompilerParams(has_side_effects=DATAFLOW),
    )(*thru, *sems, after)
    return list(outs[ns:]), list(outs[:ns])


def _place_shard(x, me, dtype, name):
    r, c = x.shape
    tr = r
    if r * c * 4 > PLACE_BLOCK_BYTES:
        for cand in (512, 256, 128, 64, 32, 16):
            if r % cand == 0 and cand * c * 4 <= PLACE_BLOCK_BYTES:
                tr = cand
                break

    def body(me_ref, x_ref, o_ref):
        o_ref[...] = x_ref[...].astype(o_ref.dtype)

    return pl.pallas_call(
        body, name=name, out_shape=jax.ShapeDtypeStruct((N_DEV, r, c), dtype),
        grid_spec=pltpu.PrefetchScalarGridSpec(
            num_scalar_prefetch=1, grid=(r // tr,),
            in_specs=[pl.BlockSpec((tr, c), lambda i, me_ref: (i, 0))],
            out_specs=pl.BlockSpec((None, tr, c), lambda i, me_ref: (me_ref[0], i, 0))),
        compiler_params=_params(),
    )(me.reshape(1), x)


NN = ((1,), (0,))
NT = ((1,), (1,))
TN = ((0,), (0,))


def _matmul(a, b, out_shape, *, grid, a_spec, b_spec, o_spec, dims, nred, name, res=None, res_spec=None,
            out_dtype=F32, slabs=0):
    has_res = res is not None

    def body(*refs):
        a_ref, b_ref = refs[0], refs[1]
        r_ref = refs[2] if has_res else None
        o_ref = refs[3] if has_res else refs[2]
        if slabs:
            part = None
            for n in range(slabs):
                term = lax.dot_general(a_ref[n].astype(BF16), b_ref[n].astype(BF16), (dims, ((), ())),
                                       preferred_element_type=F32)
                part = term if part is None else part + term
        else:
            part = lax.dot_general(a_ref[...].astype(BF16), b_ref[...].astype(BF16), (dims, ((), ())),
                                   preferred_element_type=F32)
        if nred == 1:
            if has_res:
                part = part + r_ref[...]
            o_ref[...] = part.astype(o_ref.dtype)
        else:
            acc = refs[-1]
            r = pl.program_id(2)

            @pl.when(r == 0)
            def _():
                acc[...] = part

            @pl.when(r > 0)
            def _():
                acc[...] += part

            @pl.when(r == nred - 1)
            def _():
                tot = acc[...]
                if has_res:
                    tot = tot + r_ref[...]
                o_ref[...] = tot.astype(o_ref.dtype)

    in_specs = [a_spec, b_spec] + ([res_spec] if has_res else [])
    args = (a, b) + ((res,) if has_res else ())
    acc_shape = tuple(d for d in o_spec.block_shape if d is not None)
    return pl.pallas_call(
        body, name=name, grid=grid, out_shape=jax.ShapeDtypeStruct(out_shape, out_dtype),
        in_specs=in_specs, out_specs=o_spec,
        scratch_shapes=[pltpu.VMEM(acc_shape, F32)] if nred > 1 else [],
        compiler_params=_params(),
    )(*args)


def _mm_nn(a, w, name, res=None, tn=None, out_dtype=F32):
    m, k = a.shape
    n = w.shape[1]
    tn = tn or n
    tm = min(MM_TILE, m)
    ospec = pl.BlockSpec((tm, tn), lambda i, j, r: (i, j))
    return _matmul(a, w, (m, n), grid=(m // tm, n // tn, 1),
                   a_spec=pl.BlockSpec((tm, k), lambda i, j, r: (i, 0)),
                   b_spec=pl.BlockSpec((k, tn), lambda i, j, r: (0, j)),
                   o_spec=ospec, dims=NN, nred=1, name=name, res=res, res_spec=ospec if res is not None else None,
                   out_dtype=out_dtype)


def _mm_nt(a, w, name, out_dtype=F32):
    m, n = a.shape
    k = w.shape[0]
    tm = min(MM_TILE, m)
    return _matmul(a, w, (m, k), grid=(m // tm, 1, 1),
                   a_spec=pl.BlockSpec((tm, n), lambda i, j, r: (i, 0)),
                   b_spec=pl.BlockSpec((k, n), lambda i, j, r: (0, 0)),
                   o_spec=pl.BlockSpec((tm, k), lambda i, j, r: (i, 0)), dims=NT, nred=1, name=name,
                   out_dtype=out_dtype)


GRAD_DTYPE = BF16


def _mm_tn(a, b, name, tk=512, tn=None):
    s, k = a.shape
    n = b.shape[1]
    tn = tn or n
    tk = min(tk, k)
    ts = s if b.dtype == BF16 else max(s // 2, 1)
    return _matmul(a, b, (k, n), grid=(k // tk, n // tn, s // ts),
                   a_spec=pl.BlockSpec((ts, tk), lambda i, j, r: (r, i)),
                   b_spec=pl.BlockSpec((ts, tn), lambda i, j, r: (r, j)),
                   o_spec=pl.BlockSpec((tk, tn), lambda i, j, r: (i, j)), dims=TN, nred=s // ts, name=name,
                   out_dtype=GRAD_DTYPE)


def _rms_fwd(h, g, name):
    s, d = h.shape
    tm = min(ROW_TILE, s)

    def body(h_ref, g_ref, o_ref):
        hv = h_ref[...]
        r = lax.rsqrt(jnp.mean(hv * hv, axis=-1, keepdims=True) + RMS_EPS)
        o_ref[...] = (hv * r * g_ref[...]).astype(o_ref.dtype)

    return pl.pallas_call(
        body, name=name, grid=(s // tm,), out_shape=jax.ShapeDtypeStruct((s, d), BF16),
        in_specs=[pl.BlockSpec((tm, d), lambda i: (i, 0)), pl.BlockSpec((1, d), lambda i: (0, 0))],
        out_specs=pl.BlockSpec((tm, d), lambda i: (i, 0)), compiler_params=_params(),
    )(h, g.reshape(1, d))


def _rms_bwd(dy, h, g, res, name):
    s, d = h.shape
    tm = min(ROW_TILE, s)
    has_res = res is not None

    def body(*refs):
        dy_ref, h_ref, g_ref = refs[:3]
        r_ref = refs[3] if has_res else None
        dh_ref, dg_ref = refs[-2], refs[-1]
        hv = h_ref[...]
        r = lax.rsqrt(jnp.mean(hv * hv, axis=-1, keepdims=True) + RMS_EPS)
        hn = hv * r
        dyv = dy_ref[...].astype(F32)
        u = dyv * g_ref[...]
        dh = r * (u - hn * jnp.mean(u * hn, axis=-1, keepdims=True))
        if has_res:
            dh = dh + r_ref[...]
        dh_ref[...] = dh
        part = jnp.sum(dyv * hn, axis=0, keepdims=True)

        @pl.when(pl.program_id(0) == 0)
        def _():
            dg_ref[...] = part

        @pl.when(pl.program_id(0) > 0)
        def _():
            dg_ref[...] += part

    row = pl.BlockSpec((tm, d), lambda i: (i, 0))
    vec = pl.BlockSpec((1, d), lambda i: (0, 0))
    dh, dg = pl.pallas_call(
        body, name=name, grid=(s // tm,),
        out_shape=(jax.ShapeDtypeStruct((s, d), F32), jax.ShapeDtypeStruct((1, d), F32)),
        in_specs=[row, row, vec] + ([row] if has_res else []),
        out_specs=(row, vec), compiler_params=_params(),
    )(*((dy, h, g.reshape(1, d)) + ((res,) if has_res else ())))
    return dh, dg.reshape(d)


def _loss_head(h, g, target):
    s, d = h.shape
    tm = min(ROW_TILE, s)

    def body(h_ref, g_ref, t_ref, loss_ref, dh_ref, dg_ref):
        hv = h_ref[...]
        r = lax.rsqrt(jnp.mean(hv * hv, axis=-1, keepdims=True) + RMS_EPS)
        hn = hv * r
        gv = g_ref[...]
        err = hn * gv - t_ref[...]
        rows = jnp.mean(err * err, axis=-1, keepdims=True)
        lpart = 0.5 * jnp.sum(rows, axis=0, keepdims=True) + jnp.zeros((1, 128), F32)
        dy = err * (1.0 / d)
        u = dy * gv
        dh_ref[...] = r * (u - hn * jnp.mean(u * hn, axis=-1, keepdims=True))
        gpart = jnp.sum(dy * hn, axis=0, keepdims=True)

        @pl.when(pl.program_id(0) == 0)
        def _():
            dg_ref[...] = gpart
            loss_ref[...] = lpart

        @pl.when(pl.program_id(0) > 0)
        def _():
            dg_ref[...] += gpart
            loss_ref[...] += lpart

    row = pl.BlockSpec((tm, d), lambda i: (i, 0))
    vec = pl.BlockSpec((1, d), lambda i: (0, 0))
    return pl.pallas_call(
        body, name="loss_head", grid=(s // tm,),
        out_shape=(jax.ShapeDtypeStruct((1, 128), F32), jax.ShapeDtypeStruct((s, d), F32),
                   jax.ShapeDtypeStruct((1, d), F32)),
        in_specs=[row, vec, row],
        out_specs=(pl.BlockSpec((1, 128), lambda i: (0, 0)), row, vec), compiler_params=_params(),
    )(h, g.reshape(1, d), target)


def _shift_down(x, k):
    return pltpu.roll(x, k, 0)


def _shift_up(x, k):
    return pltpu.roll(x, x.shape[0] - k, 0)


def _conv3(x, w):
    return w[2:3, :] * x + w[1:2, :] * _shift_down(x, 1) + w[0:1, :] * _shift_down(x, 2)


def _conv3_t(x, w):
    return w[2:3, :] * x + w[1:2, :] * _shift_up(x, 1) + w[0:1, :] * _shift_up(x, 2)


def _sigmoid(x):
    return 1.0 / (1.0 + jnp.exp(-x))


def _prev_map(tile, halo, col):
    return lambda i: (jnp.maximum(i * (tile // halo) - 1, 0), col)


def _next_map(tile, halo, col, nrows):
    return lambda i: (jnp.minimum((i + 1) * (tile // halo), nrows // halo - 1), col)


def _sconv_fwd(proj, w):
    s = proj.shape[0]
    t = min(ROW_TILE, s)

    def body(cur_ref, prev_ref, w_ref, o_ref):
        i = pl.program_id(0)
        prev = prev_ref[...] * (i > 0).astype(F32)
        ext = jnp.concatenate([prev, cur_ref[...]], axis=0)
        sv = ext[:, 2 * GROUP:3 * GROUP] * ext[:, 0:GROUP]
        y = ext[:, GROUP:2 * GROUP] * _conv3(sv, w_ref[...])
        o_ref[...] = y[8:].astype(o_ref.dtype)

    return pl.pallas_call(
        body, name="sconv_fwd", grid=(s // t,), out_shape=jax.ShapeDtypeStruct((s, 4 * GROUP), BF16),
        in_specs=[pl.BlockSpec((t, 3 * GROUP), lambda i: (i, 0)),
                  pl.BlockSpec((8, 3 * GROUP), _prev_map(t, 8, 0)),
                  pl.BlockSpec((3, GROUP), lambda i: (0, 0))],
        out_specs=pl.BlockSpec((t, GROUP), lambda i: (i, 0)), compiler_params=_params(),
    )(proj, proj, w)


def _sconv_bwd(proj, w, dy):
    s = proj.shape[0]
    t = min(ROW_TILE, s)
    nt = s // t

    def body(cur_ref, prev_ref, next_ref, w_ref, dy_ref, dyn_ref, dp_ref, dw_ref):
        i = pl.program_id(0)
        first = (i > 0).astype(F32)
        last = (i < nt - 1).astype(F32)
        ext = jnp.concatenate([prev_ref[...] * first, cur_ref[...], next_ref[...] * last], axis=0)
        dye = jnp.concatenate([jnp.zeros((8, GROUP), F32), dy_ref[...], dyn_ref[...] * last], axis=0)
        hv, bv, cv = ext[:, 0:GROUP], ext[:, GROUP:2 * GROUP], ext[:, 2 * GROUP:3 * GROUP]
        wv = w_ref[...]
        sv = cv * hv
        conv = _conv3(sv, wv)
        dconv = dye * bv
        ds = _conv3_t(dconv, wv)
        dp = jnp.concatenate([ds * cv, dye * conv, ds * hv], axis=1)
        dp_ref[...] = dp[8:8 + t].astype(dp_ref.dtype)
        dc = dconv[8:8 + t]
        dw = jnp.concatenate([
            jnp.sum(dc * _shift_down(sv, 2)[8:8 + t], axis=0, keepdims=True),
            jnp.sum(dc * _shift_down(sv, 1)[8:8 + t], axis=0, keepdims=True),
            jnp.sum(dc * sv[8:8 + t], axis=0, keepdims=True),
            jnp.zeros((5, GROUP), F32)], axis=0)

        @pl.when(i == 0)
        def _():
            dw_ref[...] = dw

        @pl.when(i > 0)
        def _():
            dw_ref[...] += dw

    dp, dw = pl.pallas_call(
        body, name="sconv_bwd", grid=(nt,),
        out_shape=(jax.ShapeDtypeStruct((s, N_IN_PAD), BF16), jax.ShapeDtypeStruct((8, GROUP), F32)),
        in_specs=[pl.BlockSpec((t, 3 * GROUP), lambda i: (i, 0)),
                  pl.BlockSpec((8, 3 * GROUP), _prev_map(t, 8, 0)),
                  pl.BlockSpec((8, 3 * GROUP), _next_map(t, 8, 0, s)),
                  pl.BlockSpec((3, GROUP), lambda i: (0, 0)),
                  pl.BlockSpec((t, GROUP), lambda i: (i, 0)),
                  pl.BlockSpec((8, GROUP), _next_map(t, 8, 0, s))],
        out_specs=(pl.BlockSpec((t, 3 * GROUP), lambda i: (i, 0)), pl.BlockSpec((8, GROUP), lambda i: (0, 0))),
        compiler_params=_params(),
    )(proj, proj, proj, w, dy, dy)
    return dp, dw[:3]


def _lane_window(shape):
    lane = lax.broadcasted_iota(jnp.int32, shape, 1)
    return lane, jnp.where(lane < 64, 2.0, jnp.where(lane < 128, 4.0, jnp.where(lane < 192, 8.0, 16.0)))


def _by_group(lane, s1, s2, s3, s4):
    return jnp.where(lane < 64, s1, jnp.where(lane < 128, s2, jnp.where(lane < 192, s3, s4)))


def _pool_z(ext, row0):
    s1 = ext + _shift_down(ext, 1)
    s2 = s1 + _shift_down(s1, 2)
    s3 = s2 + _shift_down(s2, 4)
    s4 = s3 + _shift_down(s3, 8)
    lane, win = _lane_window(ext.shape)
    tpos = (lax.broadcasted_iota(jnp.int32, ext.shape, 0) + (row0 - 16 + 1)).astype(F32)
    cnt = jnp.maximum(jnp.minimum(tpos, win), 1.0)
    return _by_group(lane, s1, s2, s3, s4) / cnt - ext


ANY_SPEC = pl.BlockSpec(memory_space=pl.ANY)


def _pool_fwd(proj, wbd, scale, ybuf):
    s = proj.shape[0]
    t = min(ROW_TILE, s)
    col = (COL_GATE - GROUP) // GROUP

    def body(cur_ref, prev_ref, w_ref, sc_ref, buf_ref, o_ref):
        i = pl.program_id(0)
        ext = jnp.concatenate([prev_ref[...] * (i > 0).astype(F32), cur_ref[...]], axis=0)
        z = _pool_z(ext, i * t)[16:]
        y = jnp.dot(z.astype(BF16), w_ref[...].astype(BF16), preferred_element_type=F32)
        o_ref[...] = (y * sc_ref[...]).astype(o_ref.dtype)

    return pl.pallas_call(
        body, name="pool_fwd", grid=(s // t,), out_shape=jax.ShapeDtypeStruct(ybuf.shape, ybuf.dtype),
        in_specs=[pl.BlockSpec((t, GROUP), lambda i: (i, col)),
                  pl.BlockSpec((16, GROUP), _prev_map(t, 16, col)),
                  pl.BlockSpec((GROUP, GROUP), lambda i: (0, 0)),
                  pl.BlockSpec((1, GROUP), lambda i: (0, 0)), ANY_SPEC],
        out_specs=pl.BlockSpec((t, GROUP), lambda i: (i, 3)), input_output_aliases={4: 0},
        compiler_params=_params(),
    )(proj, proj, wbd, scale.reshape(1, GROUP), ybuf)


def _pool_bwd(proj, wbd, scale, dy, dbuf):
    s = proj.shape[0]
    t = min(ROW_TILE, s)
    nt = s // t
    col = (COL_GATE - GROUP) // GROUP

    def body(cur_ref, prev_ref, w_ref, sc_ref, dy_ref, dyn_ref, buf_ref, dp_ref, dw_ref, dsc_ref):
        i = pl.program_id(0)
        ext = jnp.concatenate([prev_ref[...] * (i > 0).astype(F32), cur_ref[...]], axis=0)
        z = _pool_z(ext, i * t)[16:]
        wv = w_ref[...].astype(BF16)
        dyc = dy_ref[...]
        dye = jnp.concatenate([dyc, dyn_ref[...] * (i < nt - 1).astype(F32)], axis=0) * sc_ref[...]
        dz = lax.dot_general(dye.astype(BF16), wv, (NT, ((), ())), preferred_element_type=F32)
        lane, win = _lane_window(dz.shape)
        tpos = (lax.broadcasted_iota(jnp.int32, dz.shape, 0) + (i * t + 1)).astype(F32)
        e = dz / jnp.minimum(tpos, win)
        f1 = e + _shift_up(e, 1)
        f2 = f1 + _shift_up(f1, 2)
        f3 = f2 + _shift_up(f2, 4)
        f4 = f3 + _shift_up(f3, 8)
        dp = _by_group(lane, f1, f2, f3, f4) - dz
        dp_ref[...] = dp[:t].astype(dp_ref.dtype)
        zb = z.astype(BF16)
        y = jnp.dot(zb, wv, preferred_element_type=F32)
        dsc = jnp.sum(dyc * y, axis=0, keepdims=True)
        dw = lax.dot_general(zb, dye[:t].astype(BF16), (TN, ((), ())), preferred_element_type=F32)

        @pl.when(i == 0)
        def _():
            dw_ref[...] = dw
            dsc_ref[...] = dsc

        @pl.when(i > 0)
        def _():
            dw_ref[...] += dw
            dsc_ref[...] += dsc

    dp, dw, dsc = pl.pallas_call(
        body, name="pool_bwd", grid=(nt,),
        out_shape=(jax.ShapeDtypeStruct(dbuf.shape, dbuf.dtype), jax.ShapeDtypeStruct((GROUP, GROUP), F32),
                   jax.ShapeDtypeStruct((1, GROUP), F32)),
        in_specs=[pl.BlockSpec((t, GROUP), lambda i: (i, col)),
                  pl.BlockSpec((16, GROUP), _prev_map(t, 16, col)),
                  pl.BlockSpec((GROUP, GROUP), lambda i: (0, 0)),
                  pl.BlockSpec((1, GROUP), lambda i: (0, 0)),
                  pl.BlockSpec((t, GROUP), lambda i: (i, 3)),
                  pl.BlockSpec((16, GROUP), _next_map(t, 16, 3, s)), ANY_SPEC],
        out_specs=(pl.BlockSpec((t, GROUP), lambda i: (i, col)), pl.BlockSpec((GROUP, GROUP), lambda i: (0, 0)),
                   pl.BlockSpec((1, GROUP), lambda i: (0, 0))),
        input_output_aliases={6: 0}, compiler_params=_params(),
    )(proj, proj, wbd, scale.reshape(1, GROUP), dy, dy, dbuf)
    return dp, dw, dsc.reshape(GROUP)


FF_HALO = 16


def _ffn_gate_fwd(u0, w):
    s = u0.shape[1]
    t = min(ROW_TILE, s)

    def body(a_ref, ap_ref, g_ref, gp_ref, wa_ref, wg_ref, o_ref):
        first = (pl.program_id(1) > 0).astype(F32)
        a = _conv3(jnp.concatenate([ap_ref[...] * first, a_ref[...].astype(F32)], axis=0), wa_ref[...])[FF_HALO:]
        g = _conv3(jnp.concatenate([gp_ref[...] * first, g_ref[...].astype(F32)], axis=0), wg_ref[...])[FF_HALO:]
        o_ref[...] = (a * (g * _sigmoid(g))).astype(o_ref.dtype)

    def cur(off):
        return pl.BlockSpec((None, t, FF_SHARD), lambda j, i: (j + off, i, 0))

    def prev(off):
        return pl.BlockSpec((None, FF_HALO, FF_SHARD),
                            lambda j, i: (j + off, jnp.maximum(i * (t // FF_HALO) - 1, 0), 0))

    def wspec(off):
        return pl.BlockSpec((None, 3, FF_SHARD), lambda j, i: (j + off, 0, 0))

    return pl.pallas_call(
        body, name="ffn_gate_fwd", grid=(FF_HALF, s // t),
        out_shape=jax.ShapeDtypeStruct((FF_HALF, s, FF_SHARD), BF16),
        in_specs=[cur(0), prev(0), cur(FF_HALF), prev(FF_HALF), wspec(0), wspec(FF_HALF)],
        out_specs=pl.BlockSpec((None, t, FF_SHARD), lambda j, i: (j, i, 0)), compiler_params=_params(),
    )(u0, u0, u0, u0, w, w)


def _ffn_gate_bwd(u0, w, dact):
    s = u0.shape[1]
    t = min(ROW_TILE, s)
    nt = s // t

    def body(c_ref, p_ref, n_ref, w_ref, d_ref, dn_ref, du_ref, dw_ref):
        i = pl.program_id(1)
        first = (i > 0).astype(F32)
        last = (i < nt - 1).astype(F32)
        dext = jnp.concatenate([jnp.zeros((FF_HALO, FF_SHARD), F32), d_ref[...].astype(F32), dn_ref[...] * last],
                               axis=0)
        ext = [jnp.concatenate([p_ref[n] * first, c_ref[n].astype(F32), n_ref[n] * last], axis=0) for n in range(2)]
        a = _conv3(ext[0], w_ref[0])
        g = _conv3(ext[1], w_ref[1])
        sg = _sigmoid(g)
        silu = g * sg
        dus = (dext * silu, dext * a * (sg + silu * (1.0 - sg)))
        mine = slice(FF_HALO, FF_HALO + t)
        for n in range(2):
            shifted = (_shift_up(dus[n], 2), _shift_up(dus[n], 1), dus[n])
            wv = w_ref[n]
            du_ref[n] = (wv[0:1, :] * shifted[0] + wv[1:2, :] * shifted[1] + wv[2:3, :] * shifted[2]
                         )[mine].astype(du_ref.dtype)
            uc = ext[n][mine]
            dw = jnp.concatenate([jnp.sum(uc * shifted[k][mine], axis=0, keepdims=True) for k in range(3)]
                                 + [jnp.zeros((5, FF_SHARD), F32)], axis=0)

            @pl.when(i == 0)
            def _(n=n, dw=dw):
                dw_ref[n] = dw

            @pl.when(i > 0)
            def _(n=n, dw=dw):
                dw_ref[n] += dw

    def pair(rows, row_map):
        return pl.BlockSpec((2, None, rows, FF_SHARD), lambda j, i: (0, j, row_map(i), 0))

    prev_row = lambda i: jnp.maximum(i * (t // FF_HALO) - 1, 0)
    next_row = lambda i: jnp.minimum((i + 1) * (t // FF_HALO), s // FF_HALO - 1)
    u2 = u0.reshape(2, FF_HALF, s, FF_SHARD)
    du, dw = pl.pallas_call(
        body, name="ffn_gate_bwd", grid=(FF_HALF, nt),
        out_shape=(jax.ShapeDtypeStruct((2, FF_HALF, s, FF_SHARD), BF16),
                   jax.ShapeDtypeStruct((2, FF_HALF, 8, FF_SHARD), F32)),
        in_specs=[pair(t, lambda i: i), pair(FF_HALO, prev_row), pair(FF_HALO, next_row), pair(3, lambda i: 0),
                  pl.BlockSpec((None, t, FF_SHARD), lambda j, i: (j, i, 0)),
                  pl.BlockSpec((None, FF_HALO, FF_SHARD), lambda j, i: (j, next_row(i), 0))],
        out_specs=(pair(t, lambda i: i), pair(8, lambda i: 0)),
        compiler_params=_params(),
    )(u2, u2, u2, w.reshape(2, FF_HALF, 3, FF_SHARD), dact, dact)
    return du.reshape(2 * FF_HALF, s, FF_SHARD), dw.reshape(2 * FF_HALF, 8, FF_SHARD)[:, :3]


def _rope_tables(positions):
    inv_freq = ROPE_THETA ** (-jnp.arange(0, ROPE_DIM, 2, dtype=F32) / ROPE_DIM)
    ang = positions.astype(F32)[:, None] * inv_freq
    cos, sin = jnp.cos(ang), jnp.sin(ang)
    s = positions.shape[0]
    half = ROPE_DIM // 2
    rest = HEAD_DIM - ROPE_DIM
    ca = jnp.concatenate([cos, cos, jnp.ones((s, rest), F32)], axis=1)
    cb = jnp.concatenate([-sin, jnp.zeros((s, HEAD_DIM - half), F32)], axis=1)
    cc = jnp.concatenate([jnp.zeros((s, half), F32), sin, jnp.zeros((s, rest), F32)], axis=1)
    return tuple(jnp.tile(tb, (1, N_HEADS)) for tb in (ca, cb, cc))


QK_WIDE = 128
LANE_CQ, LANE_CK = 64, 67
KT_ROWS = 80


def _three_bf16(x):
    hi = x.astype(BF16).astype(F32)
    mid = (x - hi).astype(BF16).astype(F32)
    lo = (x - hi - mid).astype(BF16).astype(F32)
    return hi, mid, lo


def _heads_split(proj, col, tables, c, name):
    s = proj.shape[0]
    t = min(ROW_TILE, s)
    rope = tables is not None
    wide = c is not None
    width = QK_WIDE if wide else HEAD_DIM

    def body(*refs):
        x_ref = refs[0]
        q_ref, k_ref, v_ref, kt_ref, vt_ref = refs[-5:]
        xv = x_ref[...]
        parts = [xv[:, 0:GROUP], xv[:, GROUP:2 * GROUP], xv[:, 2 * GROUP:3 * GROUP]]
        if rope:
            ca, cb, cc = refs[1][...], refs[2][...], refs[3][...]
            for n in range(2):
                p = parts[n]
                parts[n] = p * ca + pltpu.roll(p, GROUP - 8, 1) * cb + pltpu.roll(p, 8, 1) * cc
        parts[0] = parts[0] * (HEAD_DIM ** -0.5)
        k_t, v_t = parts[1].T, parts[2].T
        ones_row = jnp.where(lax.broadcasted_iota(jnp.int32, (KT_ROWS - HEAD_DIM, t), 0) == 0, 1.0, 0.0)
        lane = lax.broadcasted_iota(jnp.int32, (t, QK_WIDE), 1)
        zeros = jnp.zeros((t, QK_WIDE - HEAD_DIM), F32)
        for h in range(N_HEADS):
            hs = slice(h * HEAD_DIM, (h + 1) * HEAD_DIM)
            qh, kh = parts[0][:, hs], parts[1][:, hs]
            if wide:
                terms = _three_bf16(refs[-6][:, h:h + 1])
                qh = jnp.concatenate([qh, zeros], axis=1)
                kh = jnp.concatenate([kh, zeros], axis=1)
                for n in range(3):
                    qh = jnp.where(lane == LANE_CQ + n, terms[n], jnp.where(lane == LANE_CK + n, 1.0, qh))
                    kh = jnp.where(lane == LANE_CK + n, -terms[n], jnp.where(lane == LANE_CQ + n, 1.0, kh))
            q_ref[h] = qh.astype(q_ref.dtype)
            k_ref[h] = kh.astype(k_ref.dtype)
            v_ref[h] = parts[2][:, hs].astype(v_ref.dtype)
            kt_ref[h] = jnp.concatenate([k_t[hs, :], ones_row], axis=0).astype(kt_ref.dtype)
            vt_ref[h] = v_t[hs, :].astype(vt_ref.dtype)

    tab = pl.BlockSpec((t, GROUP), lambda i: (i, 0))
    qk = pl.BlockSpec((N_HEADS, t, width), lambda i: (0, i, 0))
    heads = pl.BlockSpec((N_HEADS, t, HEAD_DIM), lambda i: (0, i, 0))
    heads_t = pl.BlockSpec((N_HEADS, HEAD_DIM, t), lambda i: (0, 0, i))
    qk_shape = jax.ShapeDtypeStruct((N_HEADS, s, width), BF16)
    return pl.pallas_call(
        body, name=name, grid=(s // t,),
        out_shape=(qk_shape, qk_shape, jax.ShapeDtypeStruct((N_HEADS, s, HEAD_DIM), BF16),
                   jax.ShapeDtypeStruct((N_HEADS, KT_ROWS, s), BF16),
                   jax.ShapeDtypeStruct((N_HEADS, HEAD_DIM, s), BF16)),
        in_specs=[pl.BlockSpec((t, 3 * GROUP), lambda i: (i, col))] + ([tab, tab, tab] if rope else [])
        + ([pl.BlockSpec((t, 128), lambda i: (i, 0))] if wide else []),
        out_specs=(qk, qk, heads, pl.BlockSpec((N_HEADS, KT_ROWS, t), lambda i: (0, 0, i)), heads_t),
        compiler_params=_params(),
    )(*((proj,) + (tuple(tables) if rope else ()) + ((c,) if wide else ())))


def _heads_merge(dqt, dk, dv, tables, name, dbuf, col):
    s = dv.shape[1]
    t = min(ROW_TILE, s)
    rope = tables is not None

    wide = dk.shape[2] == QK_WIDE

    def body(*refs):
        o_ref = refs[n_in + 1]
        dq = jnp.concatenate([refs[0][h, :HEAD_DIM, :] for h in range(N_HEADS)], axis=0).T
        parts = [dq] + [jnp.concatenate([r[h][:, :HEAD_DIM] for h in range(N_HEADS)], axis=1) for r in refs[1:3]]
        parts[0] = parts[0] * (HEAD_DIM ** -0.5)
        if rope:
            ca, cb, cc = refs[3][...], refs[4][...], refs[5][...]
            for n in range(2):
                p = parts[n]
                parts[n] = p * ca + pltpu.roll(p * cb, 8, 1) + pltpu.roll(p * cc, GROUP - 8, 1)
        o_ref[...] = jnp.concatenate(parts, axis=1).astype(o_ref.dtype)
        if wide:
            over_keys = jnp.concatenate([refs[0][h, HEAD_DIM:HEAD_DIM + 8, :] for h in range(N_HEADS)]
                                        + [jnp.zeros((128 - 8 * N_HEADS, t), F32)], axis=0).T
            lane = lax.broadcasted_iota(jnp.int32, (t, 128), 1)
            dc = jnp.zeros((t, 128), F32)
            for h in range(N_HEADS):
                dc = jnp.where(lane == h, over_keys[:, 8 * h:8 * h + 1] - refs[1][h][:, LANE_CK:LANE_CK + 1], dc)
            refs[n_in + 2][...] = dc

    tab = pl.BlockSpec((t, GROUP), lambda i: (i, 0))
    heads = pl.BlockSpec((N_HEADS, t, HEAD_DIM), lambda i: (0, i, 0))
    n_in = 6 if rope else 3
    dspec = pl.BlockSpec((t, 3 * GROUP), lambda i: (i, col))
    dshape = jax.ShapeDtypeStruct(dbuf.shape, dbuf.dtype)
    return pl.pallas_call(
        body, name=name, grid=(s // t,),
        out_shape=(dshape, jax.ShapeDtypeStruct((s, 128), F32)) if wide else dshape,
        in_specs=[pl.BlockSpec((N_HEADS, KT_ROWS, t), lambda i: (0, 0, i)),
                  pl.BlockSpec((N_HEADS, t, dk.shape[2]), lambda i: (0, i, 0)), heads]
        + ([tab, tab, tab] if rope else []) + [ANY_SPEC],
        out_specs=(dspec, pl.BlockSpec((t, 128), lambda i: (i, 0))) if wide else dspec,
        input_output_aliases={n_in: 0}, compiler_params=_params(),
    )(*((dqt, dk, dv) + (tuple(tables) if rope else ()) + (dbuf,)))


def _log_sigmoid(x):
    return jnp.minimum(x, 0.0) - jnp.log(1.0 + jnp.exp(-jnp.abs(x)))


def _scan_rows(x, reverse):
    n = x.shape[0]
    row = lax.broadcasted_iota(jnp.int32, x.shape, 0)
    k = 1
    while k < n:
        if reverse:
            x = x + jnp.where(row < n - k, _shift_up(x, k), 0.0)
        else:
            x = x + jnp.where(row >= k, _shift_down(x, k), 0.0)
        k *= 2
    return x


def _gate_cumsum(proj, bias):
    s = proj.shape[0]
    col = COL_GATE // 128

    def body(z_ref, b_ref, c_ref):
        c_ref[...] = _scan_rows(_log_sigmoid(z_ref[...] + b_ref[...]), False)

    return pl.pallas_call(
        body, name="gate_cumsum", grid=(1,), out_shape=jax.ShapeDtypeStruct((s, 128), F32),
        in_specs=[pl.BlockSpec((s, 128), lambda i: (0, col)), pl.BlockSpec((1, 128), lambda i: (0, 0))],
        out_specs=pl.BlockSpec((s, 128), lambda i: (0, 0)), compiler_params=_params(),
    )(proj, bias)


def _gate_cumsum_bwd(proj, bias, dc, dbuf):
    s = proj.shape[0]
    col = COL_GATE // 128

    def body(z_ref, b_ref, dc_ref, buf_ref, dz_ref, db_ref):
        dlogf = _scan_rows(dc_ref[...], True)
        dz = dlogf * _sigmoid(-(z_ref[...] + b_ref[...]))
        dz_ref[...] = dz.astype(dz_ref.dtype)
        db_ref[...] = jnp.sum(dz, axis=0, keepdims=True)

    return pl.pallas_call(
        body, name="gate_cumsum_bwd", grid=(1,),
        out_shape=(jax.ShapeDtypeStruct(dbuf.shape, dbuf.dtype), jax.ShapeDtypeStruct((1, 128), F32)),
        in_specs=[pl.BlockSpec((s, 128), lambda i: (0, col)), pl.BlockSpec((1, 128), lambda i: (0, 0)),
                  pl.BlockSpec((s, 128), lambda i: (0, 0)), ANY_SPEC],
        out_specs=(pl.BlockSpec((s, 128), lambda i: (0, col)), pl.BlockSpec((1, 128), lambda i: (0, 0))),
        input_output_aliases={3: 0}, compiler_params=_params(),
    )(proj, bias, dc, dbuf)


DIL_REACH = 2048


def _pair_weight(mode, d):
    if mode == "fox":
        return jnp.where(d >= 0, 1.0, 0.0)
    w1 = jnp.where(jnp.abs(d - 64) <= 64, 1.0, 0.0)
    w2 = jnp.where((d & 3) == 0, jnp.where(jnp.abs(d - 256) <= 256, 1.0, 0.0), 0.0)
    w3 = jnp.where((d & 15) == 0, jnp.where(jnp.abs(d - 1024) <= 1024, 1.0, 0.0), 0.0)
    return w1 + w2 + w3


def _bias_tables(mode, tq, tk):
    nb = 2 if mode == "fox" else DIL_REACH // tk + 1
    n = lax.broadcasted_iota(jnp.int32, (nb, tk, tq), 0)
    key = lax.broadcasted_iota(jnp.int32, (nb, tk, tq), 1)
    query = lax.broadcasted_iota(jnp.int32, (nb, tk, tq), 2)
    w = _pair_weight(mode, n * tk + query - key)
    return jnp.where(w > 0.0, jnp.log(jnp.maximum(w, 1.0)), NEG)


M_INIT = -1e29


def _first_key_chunk(mode, q0, tk):
    if mode == "fox":
        return 0
    return jnp.maximum(q0 - DIL_REACH, 0) // tk


def _attention_fwd(mode, q, k, vt, tab_t, ybuf, col):
    s, width = q.shape[1], q.shape[2]
    tq = min(ATT_TQ, s)
    tk = tq
    nb = tab_t.shape[0]

    def body(q_ref, k_ref, vt_ref, tab_ref, buf_ref, y_ref, o_ref, lse_ref):
        i = pl.program_id(0)
        lo = _first_key_chunk(mode, i * tq, tk)

        def step(c, carry):
            k0 = pl.multiple_of(c * tk, tk)
            tab = tab_ref[jnp.minimum(i - c, nb - 1)]
            scores = [lax.dot_general(k_ref[h, pl.ds(k0, tk), :], q_ref[h], (NT, ((), ())),
                                      preferred_element_type=F32) for h in range(N_HEADS)]
            stats, probs = [], []
            for h in range(N_HEADS):
                m, l = carry[3 * h:3 * h + 2]
                sc = scores[h] + tab
                m_new = jnp.maximum(m, jnp.max(sc, axis=0, keepdims=True))
                alpha = jnp.exp(m - m_new)
                p = jnp.exp(sc - m_new)
                stats.append((m_new, alpha * l + jnp.sum(p, axis=0, keepdims=True), alpha))
                probs.append(p.astype(BF16))
            pv = [jnp.dot(vt_ref[h, :, pl.ds(k0, tk)], probs[h], preferred_element_type=F32) for h in range(N_HEADS)]
            new = []
            for h in range(N_HEADS):
                m_new, l, alpha = stats[h]
                new += [m_new, l, alpha * carry[3 * h + 2] + pv[h]]
            return tuple(new)

        start = (jnp.full((1, tq), M_INIT, F32), jnp.zeros((1, tq), F32), jnp.zeros((HEAD_DIM, tq), F32))
        done = lax.fori_loop(lo, i + 1, step, start * N_HEADS)
        outs = []
        for h in range(N_HEADS):
            m, l, acc = done[3 * h:3 * h + 3]
            outs.append(acc / l)
            lse_ref[h] = m + jnp.log(l)
        out = jnp.concatenate(outs, axis=0).T
        y_ref[...] = out.astype(y_ref.dtype)
        o_ref[...] = out

    rowspec = pl.BlockSpec((N_HEADS, 1, tq), lambda i: (0, 0, i))
    return pl.pallas_call(
        body, name="attention_fwd_" + mode, grid=(s // tq,),
        out_shape=(jax.ShapeDtypeStruct(ybuf.shape, ybuf.dtype), jax.ShapeDtypeStruct((s, GROUP), F32),
                   jax.ShapeDtypeStruct((N_HEADS, 1, s), F32)),
        in_specs=[pl.BlockSpec((N_HEADS, tq, width), lambda i: (0, i, 0)),
                  pl.BlockSpec((N_HEADS, s, width), lambda i: (0, 0, 0)),
                  pl.BlockSpec((N_HEADS, HEAD_DIM, s), lambda i: (0, 0, 0)),
                  pl.BlockSpec((nb, tk, tq), lambda i: (0, 0, 0)), ANY_SPEC],
        out_specs=(pl.BlockSpec((tq, GROUP), lambda i: (i, col)), pl.BlockSpec((tq, GROUP), lambda i: (i, 0)),
                   rowspec),
        input_output_aliases={4: 0}, compiler_params=_params(),
    )(q, k, vt, tab_t, ybuf)


def _attention_delta(o, do, col):
    s = o.shape[0]
    t = min(ROW_TILE, s)

    def body(o_ref, do_ref, delta_ref, dob_ref):
        dov = do_ref[...]
        prod_t = (o_ref[...] * dov).T
        for h in range(N_HEADS):
            hs = slice(h * HEAD_DIM, (h + 1) * HEAD_DIM)
            delta_ref[h] = jnp.sum(prod_t[hs, :], axis=0, keepdims=True)
            dob_ref[h] = dov[:, hs].astype(dob_ref.dtype)

    return pl.pallas_call(
        body, name="attention_delta", grid=(s // t,),
        out_shape=(jax.ShapeDtypeStruct((N_HEADS, 1, s), F32), jax.ShapeDtypeStruct((N_HEADS, s, HEAD_DIM), BF16)),
        in_specs=[pl.BlockSpec((t, GROUP), lambda i: (i, 0)), pl.BlockSpec((t, GROUP), lambda i: (i, col))],
        out_specs=(pl.BlockSpec((N_HEADS, 1, t), lambda i: (0, 0, i)),
                   pl.BlockSpec((N_HEADS, t, HEAD_DIM), lambda i: (0, i, 0))),
        compiler_params=_params(),
    )(o, do)


def _attention_bwd(mode, q, k, v, kt, tab_t, dob, lse, delta):
    s, width = q.shape[1], q.shape[2]
    tq = min(ATT_TQ, s)
    tk = tq
    nq = s // tq
    nb = tab_t.shape[0]

    def body(q_ref, k_ref, v_ref, kt_ref, tab_ref, dob_ref, lse_ref, delta_ref, dqt_ref, dk_ref, dv_ref):
        i = pl.program_id(0)

        @pl.when(i == 0)
        def _():
            dqt_ref[...] = jnp.zeros_like(dqt_ref)

        hi = nq if mode == "fox" else jnp.minimum((i * tk + tk - 1 + DIL_REACH) // tq + 1, nq)
        for h0 in range(0, N_HEADS, BWD_HEADS):
            heads = range(h0, h0 + BWD_HEADS)

            def step(c, carry, heads=heads):
                q0 = pl.multiple_of(c * tq, tq)
                qs = pl.ds(q0, tq)
                tab = tab_ref[jnp.minimum(c - i, nb - 1)]
                qv = [q_ref[h, qs, :] for h in heads]
                dov = [dob_ref[h, qs, :] for h in heads]
                sc = [lax.dot_general(k_ref[h], qv[n], (NT, ((), ())), preferred_element_type=F32)
                      for n, h in enumerate(heads)]
                dp = [lax.dot_general(v_ref[h], dov[n], (NT, ((), ())), preferred_element_type=F32)
                      for n, h in enumerate(heads)]
                pb, dsb = [], []
                for n, h in enumerate(heads):
                    p = jnp.exp(sc[n] + tab - lse_ref[h, :, qs])
                    pb.append(p.astype(BF16))
                    dsb.append((p * (dp[n] - delta_ref[h, :, qs])).astype(BF16))
                new = []
                for n, h in enumerate(heads):
                    new += [carry[2 * n] + jnp.dot(dsb[n], qv[n], preferred_element_type=F32),
                            carry[2 * n + 1] + jnp.dot(pb[n], dov[n], preferred_element_type=F32)]
                for n, h in enumerate(heads):
                    dqt_ref[h, :, qs] += jnp.dot(kt_ref[h], dsb[n], preferred_element_type=F32)
                return tuple(new)

            start = (jnp.zeros((tk, width), F32), jnp.zeros((tk, HEAD_DIM), F32))
            done = lax.fori_loop(i, hi, step, start * BWD_HEADS)
            for n, h in enumerate(heads):
                dk_ref[h] = done[2 * n]
                dv_ref[h] = done[2 * n + 1]

    def full(shape):
        return pl.BlockSpec(shape, lambda i: (0, 0, 0))

    kblk = pl.BlockSpec((N_HEADS, tk, width), lambda i: (0, i, 0))
    vblk = pl.BlockSpec((N_HEADS, tk, HEAD_DIM), lambda i: (0, i, 0))
    return pl.pallas_call(
        body, name="attention_bwd_" + mode, grid=(s // tk,),
        out_shape=(jax.ShapeDtypeStruct((N_HEADS, KT_ROWS, s), F32), jax.ShapeDtypeStruct((N_HEADS, s, width), F32),
                   jax.ShapeDtypeStruct((N_HEADS, s, HEAD_DIM), F32)),
        in_specs=[full((N_HEADS, s, width)), kblk, vblk, pl.BlockSpec((N_HEADS, KT_ROWS, tk), lambda i: (0, 0, i)),
                  full((nb, tk, tq)), full((N_HEADS, s, HEAD_DIM)), full((N_HEADS, 1, s)), full((N_HEADS, 1, s))],
        out_specs=(full((N_HEADS, KT_ROWS, s)), kblk, vblk),
        compiler_params=_params(),
    )(q, k, v, kt, tab_t, dob, lse, delta)


def _xattn_fwd(qx, kvm):
    s = qx.shape[0]
    t = min(ROW_TILE, s)

    def body(q_ref, kv_ref, o_ref):
        for h in range(XA_HEADS):
            qv = q_ref[:, h * XA_DIM:(h + 1) * XA_DIM].astype(BF16)
            kv = kv_ref[h].astype(BF16)
            vv = kv_ref[XA_HEADS + h].astype(BF16)
            sc = lax.dot_general(qv, kv, (NT, ((), ())), preferred_element_type=F32) * (XA_DIM ** -0.5)
            e = jnp.exp(sc - jnp.max(sc, axis=-1, keepdims=True))
            p = e / jnp.sum(e, axis=-1, keepdims=True)
            o_ref[:, h * XA_DIM:(h + 1) * XA_DIM] = jnp.dot(p.astype(BF16), vv,
                                                             preferred_element_type=F32).astype(o_ref.dtype)

    return pl.pallas_call(
        body, name="xattn_fwd", grid=(s // t,), out_shape=jax.ShapeDtypeStruct((s, D_MODEL), BF16),
        in_specs=[pl.BlockSpec((t, D_MODEL), lambda i: (i, 0)),
                  pl.BlockSpec((2 * XA_HEADS, MEM_LEN, XA_DIM), lambda i: (0, 0, 0))],
        out_specs=pl.BlockSpec((t, D_MODEL), lambda i: (i, 0)), compiler_params=_params(),
    )(qx, kvm)


def _xattn_bwd(qx, kvm, do):
    s = qx.shape[0]
    t = min(ROW_TILE, s)

    def body(q_ref, kv_ref, do_ref, dq_ref, dkv_ref):
        i = pl.program_id(0)
        for h in range(XA_HEADS):
            qv = q_ref[:, h * XA_DIM:(h + 1) * XA_DIM].astype(BF16)
            dov = do_ref[:, h * XA_DIM:(h + 1) * XA_DIM].astype(BF16)
            kv = kv_ref[h].astype(BF16)
            vv = kv_ref[XA_HEADS + h].astype(BF16)
            sc = lax.dot_general(qv, kv, (NT, ((), ())), preferred_element_type=F32) * (XA_DIM ** -0.5)
            e = jnp.exp(sc - jnp.max(sc, axis=-1, keepdims=True))
            p = e / jnp.sum(e, axis=-1, keepdims=True)
            dp = lax.dot_general(dov, vv, (NT, ((), ())), preferred_element_type=F32)
            ds = (p * (dp - jnp.sum(p * dp, axis=-1, keepdims=True)) * (XA_DIM ** -0.5)).astype(BF16)
            dq_ref[:, h * XA_DIM:(h + 1) * XA_DIM] = jnp.dot(ds, kv, preferred_element_type=F32).astype(dq_ref.dtype)
            dk = lax.dot_general(ds, qv, (TN, ((), ())), preferred_element_type=F32)
            dv = lax.dot_general(p.astype(BF16), dov, (TN, ((), ())), preferred_element_type=F32)

            @pl.when(i == 0)
            def _(h=h, dk=dk, dv=dv):
                dkv_ref[h] = dk
                dkv_ref[XA_HEADS + h] = dv

            @pl.when(i > 0)
            def _(h=h, dk=dk, dv=dv):
                dkv_ref[h] += dk
                dkv_ref[XA_HEADS + h] += dv

    row = pl.BlockSpec((t, D_MODEL), lambda i: (i, 0))
    kvs = pl.BlockSpec((2 * XA_HEADS, MEM_LEN, XA_DIM), lambda i: (0, 0, 0))
    return pl.pallas_call(
        body, name="xattn_bwd", grid=(s // t,),
        out_shape=(jax.ShapeDtypeStruct((s, D_MODEL), BF16),
                   jax.ShapeDtypeStruct((2 * XA_HEADS, MEM_LEN, XA_DIM), F32)),
        in_specs=[row, kvs, row], out_specs=(row, kvs), compiler_params=_params(),
    )(qx, kvm, do)


def _adamw(parts, owns, me, w, m, v, name):
    nl, r, c = w.shape
    tr = r
    for cand in (256, 128, 64, 32, 16, 8):
        if r % cand == 0 and r > cand and N_DEV * cand * c * 4 <= ADAMW_BLOCK_BYTES:
            tr = cand
            break
    nt = r // tr
    per_layer = N_DEV + (1 if owns is not None else 0)

    def body(me_ref, *refs):
        w_ref, m_ref, v_ref, g_ref, d_ref, nm_ref, nv_ref = refs[nl * per_layer:]
        layer = pl.program_id(0)
        g = None
        for l in range(nl):
            p_refs = refs[l * per_layer:(l + 1) * per_layer]
            gl = None
            for d in range(N_DEV):
                term = p_refs[d][...].astype(F32)
                if owns is not None:
                    term = jnp.where(me_ref[0] == d, p_refs[N_DEV][...].astype(F32), term)
                gl = term if gl is None else gl + term
            g = gl if g is None else jnp.where(layer == l, gl, g)
        mn = ADAM_B1 * m_ref[...] + (1.0 - ADAM_B1) * g
        vn = ADAM_B2 * v_ref[...] + (1.0 - ADAM_B2) * (g * g)
        m_hat = mn / (1.0 - ADAM_B1 ** ADAM_STEP)
        v_hat = vn / (1.0 - ADAM_B2 ** ADAM_STEP)
        g_ref[...] = g
        d_ref[...] = -ADAM_LR * (m_hat / (jnp.sqrt(v_hat) + ADAM_EPS) + ADAM_WD * w_ref[...])
        nm_ref[...] = mn
        nv_ref[...] = vn

    def rows(l, ll, i):
        return jnp.where(ll == l, i, jnp.where(ll < l, 0, nt - 1))

    def part_spec(l, d):
        if owns is None:
            return pl.BlockSpec((None, tr, c), lambda ll, i, me_ref: (d, rows(l, ll, i), 0))
        return pl.BlockSpec((None, tr, c),
                            lambda ll, i, me_ref: (jnp.where(me_ref[0] == d, (d + 1) % N_DEV, d), rows(l, ll, i), 0))

    def own_spec(l):
        return pl.BlockSpec((None, tr, c), lambda ll, i, me_ref: (me_ref[0], rows(l, ll, i), 0))

    in_specs, operands = [], []
    for l in range(nl):
        in_specs += [part_spec(l, d) for d in range(N_DEV)]
        operands += [parts[l]] * N_DEV
        if owns is not None:
            in_specs.append(own_spec(l))
            operands.append(owns[l])
    blk = pl.BlockSpec((None, tr, c), lambda ll, i, me_ref: (ll, i, 0))
    shp = jax.ShapeDtypeStruct((nl, r, c), F32)
    return pl.pallas_call(
        body, name=name, out_shape=(shp, shp, shp, shp),
        grid_spec=pltpu.PrefetchScalarGridSpec(
            num_scalar_prefetch=1, grid=(nl, nt), in_specs=in_specs + [blk, blk, blk],
            out_specs=(blk, blk, blk, blk)),
        compiler_params=_params(),
    )(me.reshape(1), *operands, w, m, v)


GROUPS = {"in": ("w_in",), "rest": ("w_out", "w_xq", "w_xo", "w_xkv", "w_up", "w_down")}
FULL_SHAPES = {"w_in": (D_MODEL, N_IN_PAD), "w_out": (D_MODEL, D_MODEL), "w_xq": (D_MODEL, D_MODEL),
               "w_xo": (D_MODEL, D_MODEL), "w_xkv": (N_DEV, D_MODEL, 2 * D_MODEL // N_DEV),
               "w_up": (N_DEV, D_MODEL, FF_SHARD), "w_down": (FF_HALF, FF_SHARD, D_MODEL)}
PIECE_SHAPES = {"w_in": (N_DEV, D_MODEL // N_DEV, N_IN_PAD), "w_out": (N_DEV, D_MODEL // N_DEV, D_MODEL),
                "w_xq": (N_DEV, D_MODEL // N_DEV, D_MODEL), "w_xo": (N_DEV, D_MODEL // N_DEV, D_MODEL),
                "w_xkv": (N_DEV, D_MODEL, 2 * D_MODEL // N_DEV), "w_up": (N_DEV, D_MODEL, FF_SHARD),
                "w_down": (N_DEV, D_FF // N_DEV, D_MODEL)}
CONV_WORDS = 8192
SMALL_WORDS = 80 * 1024


class _GatheredWeights:
    def __init__(self, states, layer):
        self.states, self.layer, self.full, self.extra = dict(states), layer, {}, None

    def need(self, group, after):
        if group in self.states:
            got, _ = _exchange_wait(self.states.pop(group), after, "gather_%s_wait_%d" % (group, self.layer))
            for name, g in zip(GROUPS[group], got):
                self.full[name] = g.reshape(FULL_SHAPES[name])
            self.extra = got[len(GROUPS[group]):]

    def __getitem__(self, name):
        return self.full[name]


def _relay_in_cols(w):
    pad = jnp.zeros(w.shape[:-1] + (N_IN_PAD - N_IN,), w.dtype)
    return jnp.concatenate([w[..., :2304], w[..., 2308:N_IN], w[..., 2304:2308], pad], axis=-1)


def _unrelay_in_cols(w):
    return jnp.concatenate([w[..., :2304], w[..., COL_GATE:COL_GATE + 4], w[..., 2304:COL_GATE]], axis=-1)


def _layer_fwd(h, memv, w, sm, tables):
    sv = {"h0": h}
    xn = _rms_fwd(h, sm["g_mix"], "rms_mix")
    w.need("in", xn)
    proj = _mm_nn(xn, w["w_in"], "mm_in", tn=896)
    sv["xn"], sv["proj"] = xn, proj
    ycat = _sconv_fwd(proj, sm["w_sconv"])
    qd, kd, vd, ktd, vtd = _heads_split(proj, 1, tables["rope"], None, "split_dil")
    ycat, ob, lse_b = _attention_fwd("dil", qd, kd, vtd, tables["dil"], ycat, 1)
    sv["dil"] = (qd, kd, vd, ktd, ob, lse_b)
    c = _gate_cumsum(proj, sm["b_forget_pad"])
    qf, kf, vf, ktf, vtf = _heads_split(proj, 2, None, c, "split_fox")
    ycat, oc, lse_c = _attention_fwd("fox", qf, kf, vtf, tables["fox"], ycat, 2)
    sv["fox"] = (qf, kf, vf, ktf, oc, lse_c)
    ycat = _pool_fwd(proj, sm["w_pool_bd"], sm["pool_scale"], ycat)
    sv["ycat"] = ycat
    w.need("rest", ycat)
    h1 = _mm_nn(ycat, w["w_out"], "mm_out", res=h)
    sv["h1"] = h1
    xq = _rms_fwd(h1, sm["g_xa"], "rms_xa")
    memn = _rms_fwd(memv, sm["g_mem"], "rms_mem")
    qx = _mm_nn(xq, w["w_xq"], "mm_xq", out_dtype=BF16)
    kvm = _matmul(memn, w["w_xkv"], (N_DEV, MEM_LEN, XA_DIM), grid=(N_DEV, 1, 1),
                  a_spec=pl.BlockSpec((MEM_LEN, D_MODEL), lambda i, j, r: (0, 0)),
                  b_spec=pl.BlockSpec((None, D_MODEL, XA_DIM), lambda i, j, r: (i, 0, 0)),
                  o_spec=pl.BlockSpec((None, MEM_LEN, XA_DIM), lambda i, j, r: (i, 0, 0)),
                  dims=NN, nred=1, name="mm_xkv")
    ox = _xattn_fwd(qx, kvm)
    sv.update(xq=xq, memn=memn, qx=qx, kvm=kvm, ox=ox)
    h2 = _mm_nn(ox, w["w_xo"], "mm_xo", res=h1)
    sv["h2"] = h2
    xf = _rms_fwd(h2, sm["g_ffn"], "rms_ffn")
    s = h.shape[0]
    tm, tb = min(ROW_TILE, s), min(MM_TILE, s)
    u0 = _matmul(xf, w["w_up"], (N_DEV, s, FF_SHARD), grid=(s // tb, N_DEV, 1),
                 a_spec=pl.BlockSpec((tb, D_MODEL), lambda i, j, r: (i, 0)),
                 b_spec=pl.BlockSpec((None, D_MODEL, FF_SHARD), lambda i, j, r: (j, 0, 0)),
                 o_spec=pl.BlockSpec((None, tb, FF_SHARD), lambda i, j, r: (j, i, 0)),
                 dims=NN, nred=1, name="mm_up", out_dtype=BF16)
    act = _ffn_gate_fwd(u0, sm["w_ffconv"])
    sv.update(xf=xf, u0=u0, act=act)
    ospec = pl.BlockSpec((tm, D_MODEL), lambda i, j, r: (i, 0))
    h3 = _matmul(act, w["w_down"], (s, D_MODEL), grid=(s // tm, 1, 1),
                 a_spec=pl.BlockSpec((FF_HALF, tm, FF_SHARD), lambda i, j, r: (0, i, 0)),
                 b_spec=pl.BlockSpec((FF_HALF, FF_SHARD, D_MODEL), lambda i, j, r: (0, 0, 0)),
                 o_spec=ospec, dims=NN, nred=1, slabs=FF_HALF, name="mm_down", res=h2, res_spec=ospec)
    return h3, sv


def _layer_bwd(dh3, memv, w, sm, tables, sv, rest_ready):
    s = dh3.shape[0]
    tm, tb = min(ROW_TILE, s), min(MM_TILE, s)
    big, small = {}, {}
    ts = max(s // 2, 1)
    dact = _matmul(dh3, w["w_down"], (FF_HALF, s, FF_SHARD), grid=(s // tb, FF_HALF, 1),
                   a_spec=pl.BlockSpec((tb, D_MODEL), lambda i, j, r: (i, 0)),
                   b_spec=pl.BlockSpec((None, FF_SHARD, D_MODEL), lambda i, j, r: (j, 0, 0)),
                   o_spec=pl.BlockSpec((None, tb, FF_SHARD), lambda i, j, r: (j, i, 0)),
                   dims=NT, nred=1, name="mm_dact", out_dtype=BF16)
    big["w_down"] = _matmul(sv["act"], dh3, (FF_HALF, FF_SHARD, D_MODEL), grid=(FF_HALF, 1, s // ts),
                            a_spec=pl.BlockSpec((None, ts, FF_SHARD), lambda i, j, r: (i, r, 0)),
                            b_spec=pl.BlockSpec((ts, D_MODEL), lambda i, j, r: (r, 0)),
                            o_spec=pl.BlockSpec((None, FF_SHARD, D_MODEL), lambda i, j, r: (i, 0, 0)),
                            dims=TN, nred=s // ts, name="mm_dw_down", out_dtype=GRAD_DTYPE)
    du0, small["w_ffconv"] = _ffn_gate_bwd(sv["u0"], sm["w_ffconv"], dact)
    dxf = _matmul(du0, w["w_up"], (s, D_MODEL), grid=(s // tm, 1, 1),
                  a_spec=pl.BlockSpec((N_DEV, tm, FF_SHARD), lambda i, j, r: (0, i, 0)),
                  b_spec=pl.BlockSpec((N_DEV, D_MODEL, FF_SHARD), lambda i, j, r: (0, 0, 0)),
                  o_spec=pl.BlockSpec((tm, D_MODEL), lambda i, j, r: (i, 0)),
                  dims=NT, nred=1, slabs=N_DEV, name="mm_dxf")
    big["w_up"] = _matmul(sv["xf"], du0, (N_DEV, D_MODEL, FF_SHARD), grid=(N_DEV, 1, 1),
                          a_spec=pl.BlockSpec((s, D_MODEL), lambda i, j, r: (0, 0)),
                          b_spec=pl.BlockSpec((None, s, FF_SHARD), lambda i, j, r: (i, 0, 0)),
                          o_spec=pl.BlockSpec((None, D_MODEL, FF_SHARD), lambda i, j, r: (i, 0, 0)),
                          dims=TN, nred=1, name="mm_dw_up", out_dtype=GRAD_DTYPE)
    dh2, small["g_ffn"] = _rms_bwd(dxf, sv["h2"], sm["g_ffn"], dh3, "rms_ffn_bwd")
    dox = _mm_nt(dh2, w["w_xo"], "mm_dox", out_dtype=BF16)
    big["w_xo"] = _mm_tn(sv["ox"], dh2, "mm_dw_xo")
    dqx, dkvm = _xattn_bwd(sv["qx"], sv["kvm"], dox)
    dxq = _mm_nt(dqx, w["w_xq"], "mm_dxq")
    big["w_xq"] = _mm_tn(sv["xq"], dqx, "mm_dw_xq")
    big["w_xkv"] = _matmul(sv["memn"], dkvm, (N_DEV, D_MODEL, XA_DIM), grid=(N_DEV, 1, 1),
                           a_spec=pl.BlockSpec((MEM_LEN, D_MODEL), lambda i, j, r: (0, 0)),
                           b_spec=pl.BlockSpec((None, MEM_LEN, XA_DIM), lambda i, j, r: (i, 0, 0)),
                           o_spec=pl.BlockSpec((None, D_MODEL, XA_DIM), lambda i, j, r: (i, 0, 0)),
                           dims=TN, nred=1, name="mm_dw_xkv", out_dtype=GRAD_DTYPE)
    dmemn = _matmul(dkvm, w["w_xkv"], (MEM_LEN, D_MODEL), grid=(1, 1, 1),
                    a_spec=pl.BlockSpec((N_DEV, MEM_LEN, XA_DIM), lambda i, j, r: (0, 0, 0)),
                    b_spec=pl.BlockSpec((N_DEV, D_MODEL, XA_DIM), lambda i, j, r: (0, 0, 0)),
                    o_spec=pl.BlockSpec((MEM_LEN, D_MODEL), lambda i, j, r: (0, 0)),
                    dims=NT, nred=1, slabs=N_DEV, name="mm_dmemn")
    _, small["g_mem"] = _rms_bwd(dmemn, memv, sm["g_mem"], None, "rms_mem_bwd")
    dh1, small["g_xa"] = _rms_bwd(dxq, sv["h1"], sm["g_xa"], dh2, "rms_xa_bwd")
    dycat = _mm_nt(dh1, w["w_out"], "mm_dycat")
    big["w_out"] = _mm_tn(sv["ycat"], dh1, "mm_dw_out")
    proj = sv["proj"]
    dproj, small["w_sconv"] = _sconv_bwd(proj, sm["w_sconv"] + rest_ready(big), dycat)
    qd, kd, vd, ktd, ob, lse_b = sv["dil"]
    delta, dob = _attention_delta(ob, dycat, 1)
    dqt, dk, dv = _attention_bwd("dil", qd, kd, vd, ktd, tables["dil"], dob, lse_b, delta)
    dproj = _heads_merge(dqt, dk, dv, tables["rope"], "merge_dil", dproj, 1)
    qf, kf, vf, ktf, oc, lse_c = sv["fox"]
    delta, dob = _attention_delta(oc, dycat, 2)
    dqt, dk, dv = _attention_bwd("fox", qf, kf, vf, ktf, tables["fox"], dob, lse_c, delta)
    dproj, dc = _heads_merge(dqt, dk, dv, None, "merge_fox", dproj, 2)
    dproj, dbias = _gate_cumsum_bwd(proj, sm["b_forget_pad"], dc, dproj)
    small["b_forget"] = dbias[0, :N_HEADS]
    dproj, dwbd, small["pool_scale"] = _pool_bwd(proj, sm["w_pool_bd"], sm["pool_scale"], dycat, dproj)
    small["w_pool"] = jnp.stack([dwbd[64 * g:64 * (g + 1), 64 * g:64 * (g + 1)] for g in range(4)])
    dxn = _mm_nt(dproj, w["w_in"], "mm_dxn")
    big["w_in"] = _mm_tn(sv["xn"], dproj, "mm_dw_in", tn=896)
    dh0, small["g_mix"] = _rms_bwd(dxn, sv["h0"], sm["g_mix"], dh1, "rms_mix_bwd")
    return dh0, big, small


SMALL_NAMES = ("g_mix", "b_forget", "w_pool", "pool_scale", "g_xa", "g_mem", "g_ffn", "w_sconv", "w_ffconv")
WEIGHT_NAMES = ("g_mix", "w_in", "b_forget", "w_sconv", "w_pool", "pool_scale", "w_out", "g_xa", "g_mem", "w_xq",
                "w_xkv", "w_xo", "g_ffn", "w_up", "w_ffconv", "w_down", "g_final")


def _block_diag(w_pool):
    z = jnp.zeros((64, 64), F32)
    return jnp.concatenate(
        [jnp.concatenate([w_pool[g] if c == g else z for c in range(4)], axis=1) for g in range(4)], axis=0)


def kernel(x, mem, positions, g_mix, w_in, b_forget, w_sconv, w_pool, pool_scale, w_out, g_xa, g_mem, w_xq, w_xkv, w_xo, g_ffn, w_up, w_ffconv, w_down, g_final, loss_target, m_g_mix, m_w_in, m_b_forget, m_w_sconv, m_w_pool, m_pool_scale, m_w_out, m_g_xa, m_g_mem, m_w_xq, m_w_xkv, m_w_xo, m_g_ffn, m_w_up, m_w_ffconv, m_w_down, m_g_final, v_g_mix, v_w_in, v_b_forget, v_w_sconv, v_w_pool, v_pool_scale, v_w_out, v_g_xa, v_g_mem, v_w_xq, v_w_xkv, v_w_xo, v_g_ffn, v_w_up, v_w_ffconv, v_w_down, v_g_final):
    weights = dict(g_mix=g_mix, w_in=w_in, b_forget=b_forget, w_sconv=w_sconv, w_pool=w_pool, pool_scale=pool_scale,
                   w_out=w_out, g_xa=g_xa, g_mem=g_mem, w_xq=w_xq, w_xkv=w_xkv, w_xo=w_xo, g_ffn=g_ffn, w_up=w_up,
                   w_ffconv=w_ffconv, w_down=w_down, g_final=g_final)
    m_in = dict(g_mix=m_g_mix, w_in=m_w_in, b_forget=m_b_forget, w_sconv=m_w_sconv, w_pool=m_w_pool,
                pool_scale=m_pool_scale, w_out=m_w_out, g_xa=m_g_xa, g_mem=m_g_mem, w_xq=m_w_xq, w_xkv=m_w_xkv,
                w_xo=m_w_xo, g_ffn=m_g_ffn, w_up=m_w_up, w_ffconv=m_w_ffconv, w_down=m_w_down, g_final=m_g_final)
    v_in = dict(g_mix=v_g_mix, w_in=v_w_in, b_forget=v_b_forget, w_sconv=v_w_sconv, w_pool=v_w_pool,
                pool_scale=v_pool_scale, w_out=v_w_out, g_xa=v_g_xa, g_mem=v_g_mem, w_xq=v_w_xq, w_xkv=v_w_xkv,
                w_xo=v_w_xo, g_ffn=v_g_ffn, w_up=v_w_up, w_ffconv=v_w_ffconv, w_down=v_w_down, g_final=v_g_final)
    depth = w_in.shape[0]
    me = 4 * lax.axis_index("x") + 2 * lax.axis_index("y") + lax.axis_index("c")
    h = x[0]
    memv = mem[0]
    s = h.shape[0]
    tq = min(ATT_TQ, s)
    tables = {"rope": _rope_tables(positions[0]), "dil": _bias_tables("dil", tq, tq),
              "fox": _bias_tables("fox", tq, tq)}

    w_in_r = _relay_in_cols(w_in)
    conv_shard = jnp.concatenate([w_sconv.reshape(-1), w_ffconv.reshape(-1)])
    conv_shard = jnp.concatenate([conv_shard, jnp.zeros((CONV_WORDS - conv_shard.shape[0],), F32)])
    conv_bits = lax.bitcast_convert_type(conv_shard, BF16).reshape(2 * CONV_WORDS // 1024, 1024)
    gathered = []
    order = jnp.zeros((), F32)
    for l in range(depth):
        shards = dict(w_in=w_in_r[l], w_out=w_out[l], w_xq=w_xq[l], w_xo=w_xo[l], w_xkv=w_xkv[l], w_up=w_up[l],
                      w_down=w_down[l])
        states = {}
        for group in ("in", "rest"):
            shards[GROUPS[group][0]] = shards[GROUPS[group][0]] + order
            xs = [_place_shard(shards[name], me, BF16, "place_%s_%d" % (name, l)) for name in GROUPS[group]]
            if l == 0 and group == "in":
                xs.append(_place_shard(conv_bits, me, BF16, "place_conv"))
            states[group], token = _exchange_start(xs, False, "gather_%s_start_%d" % (group, l))
            order = order + token[0, 0]
        gathered.append(_GatheredWeights(states, l))
    gathered[0].need("in", tables["rope"][0])
    conv_all = lax.bitcast_convert_type(gathered[0].extra[0].reshape(N_DEV, CONV_WORDS, 2), F32)
    n_sc = depth * 3 * (GROUP // N_DEV)
    sconv_full = conv_all[:, :n_sc].reshape(N_DEV, depth, 3, GROUP // N_DEV).transpose(1, 2, 0, 3).reshape(
        depth, 3, GROUP)
    ffconv_full = conv_all[:, n_sc:n_sc + depth * 3 * FF_SHARD].reshape(N_DEV, depth, 3, FF_SHARD).transpose(
        1, 0, 2, 3)

    smalls = []
    for l in range(depth):
        smalls.append(dict(
            g_mix=g_mix[l], g_xa=g_xa[l], g_mem=g_mem[l], g_ffn=g_ffn[l], pool_scale=pool_scale[l],
            w_pool_bd=_block_diag(w_pool[l]), w_sconv=sconv_full[l], w_ffconv=ffconv_full[l],
            b_forget_pad=jnp.concatenate([b_forget[l], jnp.zeros((128 - N_HEADS,), F32)]).reshape(1, 128)))
    smalls[0]["g_mix"] = smalls[0]["g_mix"] + order

    saved = []
    for l in range(depth):
        h, sv = _layer_fwd(h, memv, gathered[l], smalls[l], tables)
        saved.append(sv)
    loss_part, dh, dg_final = _loss_head(h, g_final, loss_target[0])
    loss = lax.psum(loss_part[0, 0], MESH_AXES)

    small_grads = [None] * depth
    scatters = {}

    def pieces_of(big, group):
        return [big[name].reshape(PIECE_SHAPES[name]) for name in GROUPS[group]]

    for l in reversed(range(depth)):
        def rest_ready(big, l=l):
            scatters[l, "rest"], token = _exchange_start(pieces_of(big, "rest"), True, "scatter_rest_start_%d" % l)
            return token[0, 0]

        dh, big, small_grads[l] = _layer_bwd(dh, memv, gathered[l], smalls[l], tables, saved[l], rest_ready)
        xs = pieces_of(big, "in")
        if l == 0:
            flat = [small_grads[ll][n].reshape(-1) for n in SMALL_NAMES for ll in range(depth)]
            flat = jnp.concatenate(flat + [dg_final.reshape(-1)])
            flat = jnp.concatenate([flat, jnp.zeros((SMALL_WORDS - flat.shape[0],), F32)])
            xs.append(jnp.broadcast_to(flat.reshape(1, -1, 1024), (N_DEV, SMALL_WORDS // 1024, 1024)))
        scatters[l, "in"], token = _exchange_start(xs, True, "scatter_in_start_%d" % l)
        if l > 0:
            smalls[l - 1]["w_ffconv"] = smalls[l - 1]["w_ffconv"] + token[0, 0]
    grad_x = dh[None]

    parts, owns = {}, {}

    def wait_group(group, after):
        extra = None
        for l in reversed(range(depth)):
            got, given = _exchange_wait(scatters[l, group], after, "scatter_%s_wait_%d" % (group, l))
            for name, g, x in zip(GROUPS[group], got, given):
                parts.setdefault(name, [None] * depth)[l] = g
                owns.setdefault(name, [None] * depth)[l] = x
            extra = (got[len(GROUPS[group]):], given[len(GROUPS[group]):])
        return extra

    results = {}

    def update(name, w3, m3, v3):
        outs = _adamw(parts[name], owns.get(name), me, w3, m3, v3, "adamw_" + name)
        results[name] = [o.reshape(weights[name].shape) for o in outs]

    wait_group("rest", grad_x)
    for name in GROUPS["rest"]:
        update(name, weights[name], m_in[name], v_in[name])
    small_got, small_given = wait_group("in", results["w_down"][1])
    small_all = lax.dynamic_update_slice_in_dim(small_got[0], small_given[0][:1], me, axis=0).reshape(N_DEV, -1)
    outs = _adamw(parts["w_in"], owns["w_in"], me, w_in_r, _relay_in_cols(m_w_in), _relay_in_cols(v_w_in),
                  "adamw_w_in")
    results["w_in"] = [_unrelay_in_cols(o) for o in outs]
    off = 0
    for name in SMALL_NAMES + ("g_final",):
        wv = weights[name]
        full_shape = {"w_sconv": (depth, 3, GROUP), "w_ffconv": (depth, N_DEV, 3, FF_SHARD)}.get(name, wv.shape)
        n = 1
        for dim in full_shape:
            n *= dim
        p = small_all[:, off:off + n].reshape((N_DEV,) + tuple(full_shape))
        off += n
        if name == "w_sconv":
            p = lax.dynamic_slice_in_dim(p, me * (GROUP // N_DEV), GROUP // N_DEV, axis=3)
        elif name == "w_ffconv":
            p = lax.dynamic_index_in_dim(p, me, axis=2, keepdims=False)
        shape3 = (1, 1, wv.shape[0]) if wv.ndim == 1 else (1, -1, wv.shape[-1])
        w3 = wv.reshape(shape3)
        parts[name] = [p.reshape((N_DEV,) + w3.shape[1:])]
        update(name, w3, m_in[name].reshape(shape3), v_in[name].reshape(shape3))

    return (loss, grad_x, *[results[n][0] for n in WEIGHT_NAMES], *[results[n][1] for n in WEIGHT_NAMES],
            *[results[n][2] for n in WEIGHT_NAMES], *[results[n][3] for n in WEIGHT_NAMES])
```

```python
import functools

import jax
import jax.numpy as jnp
from jax import lax
from jax.experimental import pallas as pl
from jax.experimental.pallas import tpu as pltpu

F32 = jnp.float32
BF16 = jnp.bfloat16

N_DEV = 8
D_MODEL = 1024
GROUP = 256
HEAD_DIM = 64
N_HEADS = 4
N_IN = 2564
N_IN_PAD = 2688
COL_GATE = 2560
XA_HEADS = 4
XA_DIM = 256
MEM_LEN = 256
D_FF = 2816
FF_SHARD = 704
FF_HALF = 4
ROPE_THETA = 500000.0
ROPE_DIM = 16
RMS_EPS = 1e-6
NEG = -1e30
POOL_WINDOWS = (2, 4, 8, 16)
ADAM_LR, ADAM_B1, ADAM_B2, ADAM_EPS, ADAM_WD, ADAM_STEP = 0.001, 0.9, 0.999, 1e-08, 0.01, 10

ROW_TILE = 512
MM_TILE = 1024
ATT_TQ = 256
BWD_HEADS = 4
VMEM_LIMIT = 56 * 1024 * 1024
ADAMW_BLOCK_BYTES = 4 * 1024 * 1024
PLACE_BLOCK_BYTES = 4 * 1024 * 1024

MESH_AXES = ("x", "y", "c")


def _params(**kw):
    return pltpu.CompilerParams(vmem_limit_bytes=VMEM_LIMIT, **kw)


HBM_SPEC = pl.BlockSpec(memory_space=pltpu.HBM)
SEM_SPEC = pl.BlockSpec(memory_space=pltpu.SEMAPHORE)
DATAFLOW = pltpu.SideEffectType.DATAFLOW_SIDE_EFFECTING


def _peer_copies(x_ref, land_ref, send_sems, recv_sems, scatter):
    mx, my, mc = lax.axis_index("x"), lax.axis_index("y"), lax.axis_index("c")
    me = 4 * mx + 2 * my + mc
    pairs = []
    for k in range(1, N_DEV):
        kx, ky, kc = (k >> 2) & 1, (k >> 1) & 1, k & 1
        peer_lin = me ^ k
        send = pltpu.make_async_remote_copy(
            src_ref=x_ref.at[peer_lin] if scatter else land_ref.at[me], dst_ref=land_ref.at[me],
            send_sem=send_sems.at[k - 1], recv_sem=recv_sems.at[k - 1],
            device_id=(mx ^ kx, my ^ ky, mc ^ kc), device_id_type=pl.DeviceIdType.MESH)
        arrival = pltpu.make_async_remote_copy(
            src_ref=land_ref.at[peer_lin], dst_ref=land_ref.at[peer_lin],
            send_sem=send_sems.at[k - 1], recv_sem=recv_sems.at[k - 1],
            device_id=(mx, my, mc), device_id_type=pl.DeviceIdType.MESH)
        pairs.append((send, arrival))
    return pairs


def _exchange_start(xs, scatter, name):
    n = len(xs)
    ns = n if scatter else 0

    def body(*refs):
        srcs = refs[:ns] if scatter else (None,) * n
        lands, sends, recvs = refs[ns:ns + n], refs[ns + n:ns + 2 * n], refs[ns + 2 * n:ns + 3 * n]
        for t in range(n):
            for send, _ in _peer_copies(srcs[t], lands[t], sends[t], recvs[t], scatter):
                send.start()
        token = refs[-1]
        token[...] = jnp.zeros_like(token)

    sems = pltpu.SemaphoreType.DMA((N_DEV - 1,))
    operands = [pltpu.with_memory_space_constraint(x, pltpu.HBM) for x in xs]
    if scatter:
        operands += [pltpu.with_memory_space_constraint(lax.empty(x.shape, x.dtype), pltpu.HBM) for x in xs]
    outs = pl.pallas_call(
        body, name=name,
        out_shape=(sems,) * (2 * n) + tuple(pltpu.HBM(a.shape, a.dtype) for a in operands)
        + (jax.ShapeDtypeStruct((8, 128), F32),),
        in_specs=(HBM_SPEC,) * (ns + n),
        out_specs=(SEM_SPEC,) * (2 * n) + (HBM_SPEC,) * (ns + n) + (pl.BlockSpec(memory_space=pltpu.VMEM),),
        input_output_aliases={i: 2 * n + i for i in range(ns + n)},
        compiler_params=pltpu.CompilerParams(has_side_effects=DATAFLOW),
    )(*operands)
    return (outs[:-1], scatter), outs[-1]


def _exchange_wait(state, after, name):
    held, scatter = state
    n = len(held) // (4 if scatter else 3)
    ns = n if scatter else 0
    sems, thru = held[:2 * n], held[2 * n:]

    def body(*refs):
        srcs = refs[:ns] if scatter else (None,) * n
        lands, sends, recvs = refs[ns:ns + n], refs[ns + n:ns + 2 * n], refs[ns + 2 * n:ns + 3 * n]
        for t in range(n):
            for send, arrival in _peer_copies(srcs[t], lands[t], sends[t], recvs[t], scatter):
                send.wait_send()
                arrival.wait_recv()

    outs = pl.pallas_call(
        body, name=name,
        out_shape=tuple(pltpu.HBM(a.shape, a.dtype) for a in thru),
        in_specs=(HBM_SPEC,) * (ns + n) + (SEM_SPEC,) * (2 * n) + (pl.BlockSpec(memory_space=pl.ANY),),
        out_specs=(HBM_SPEC,) * (ns + n), input_output_aliases={i: i for i in range(ns + n)},
        compiler_params=pltpu.CompilerParams(has_side_effects=DATAFLOW),
    )(*thru, *sems, after)
    return list(outs[ns:]), list(outs[:ns])


def _place_shard(x, me, dtype, name):
    r, c = x.shape
    tr = r
    if r * c * 4 > PLACE_BLOCK_BYTES:
        for cand in (512, 256, 128, 64, 32, 16):
            if r % cand == 0 and cand * c * 4 <= PLACE_BLOCK_BYTES:
                tr = cand
                break

    def body(me_ref, x_ref, o_ref):
        o_ref[...] = x_ref[...].astype(o_ref.dtype)

    return pl.pallas_call(
        body, name=name, out_shape=jax.ShapeDtypeStruct((N_DEV, r, c), dtype),
        grid_spec=pltpu.PrefetchScalarGridSpec(
            num_scalar_prefetch=1, grid=(r // tr,),
            in_specs=[pl.BlockSpec((tr, c), lambda i, me_ref: (i, 0))],
            out_specs=pl.BlockSpec((None, tr, c), lambda i, me_ref: (me_ref[0], i, 0))),
        compiler_params=_params(),
    )(me.reshape(1), x)


NN = ((1,), (0,))
NT = ((1,), (1,))
TN = ((0,), (0,))


def _matmul(a, b, out_shape, *, grid, a_spec, b_spec, o_spec, dims, nred, name, res=None, res_spec=None,
            out_dtype=F32, slabs=0):
    has_res = res is not None

    def body(*refs):
        a_ref, b_ref = refs[0], refs[1]
        r_ref = refs[2] if has_res else None
        o_ref = refs[3] if has_res else refs[2]
        if slabs:
            part = None
            for n in range(slabs):
                term = lax.dot_general(a_ref[n].astype(BF16), b_ref[n].astype(BF16), (dims, ((), ())),
                                       preferred_element_type=F32)
                part = term if part is None else part + term
        else:
            part = lax.dot_general(a_ref[...].astype(BF16), b_ref[...].astype(BF16), (dims, ((), ())),
                                   preferred_element_type=F32)
        if nred == 1:
            if has_res:
                part = part + r_ref[...]
            o_ref[...] = part.astype(o_ref.dtype)
        else:
            acc = refs[-1]
            r = pl.program_id(2)

            @pl.when(r == 0)
            def _():
                acc[...] = part

            @pl.when(r > 0)
            def _():
                acc[...] += part

            @pl.when(r == nred - 1)
            def _():
                tot = acc[...]
                if has_res:
                    tot = tot + r_ref[...]
                o_ref[...] = tot.astype(o_ref.dtype)

    in_specs = [a_spec, b_spec] + ([res_spec] if has_res else [])
    args = (a, b) + ((res,) if has_res else ())
    acc_shape = tuple(d for d in o_spec.block_shape if d is not None)
    return pl.pallas_call(
        body, name=name, grid=grid, out_shape=jax.ShapeDtypeStruct(out_shape, out_dtype),
        in_specs=in_specs, out_specs=o_spec,
        scratch_shapes=[pltpu.VMEM(acc_shape, F32)] if nred > 1 else [],
        compiler_params=_params(),
    )(*args)


def _mm_nn(a, w, name, res=None, tn=None, out_dtype=F32):
    m, k = a.shape
    n = w.shape[1]
    tn = tn or n
    tm = min(MM_TILE, m)
    ospec = pl.BlockSpec((tm, tn), lambda i, j, r: (i, j))
    return _matmul(a, w, (m, n), grid=(m // tm, n // tn, 1),
                   a_spec=pl.BlockSpec((tm, k), lambda i, j, r: (i, 0)),
                   b_spec=pl.BlockSpec((k, tn), lambda i, j, r: (0, j)),
                   o_spec=ospec, dims=NN, nred=1, name=name, res=res, res_spec=ospec if res is not None else None,
                   out_dtype=out_dtype)


def _mm_nt(a, w, name, out_dtype=F32):
    m, n = a.shape
    k = w.shape[0]
    tm = min(MM_TILE, m)
    return _matmul(a, w, (m, k), grid=(m // tm, 1, 1),
                   a_spec=pl.BlockSpec((tm, n), lambda i, j, r: (i, 0)),
                   b_spec=pl.BlockSpec((k, n), lambda i, j, r: (0, 0)),
                   o_spec=pl.BlockSpec((tm, k), lambda i, j, r: (i, 0)), dims=NT, nred=1, name=name,
                   out_dtype=out_dtype)


def _norm_matmul(h, g, b, out_shape, *, grid, b_spec, o_spec, name, out_dtype=F32):
    s, d = h.shape
    tm = s // grid[0]

    def body(h_ref, g_ref, b_ref, o_ref, xn_ref):
        @pl.when(pl.program_id(1) == 0)
        def _():
            hv = h_ref[...]
            r = lax.rsqrt(jnp.mean(hv * hv, axis=-1, keepdims=True) + RMS_EPS)
            xn_ref[...] = (hv * r * g_ref[...]).astype(xn_ref.dtype)

        o_ref[...] = jnp.dot(xn_ref[...], b_ref[...].astype(BF16), preferred_element_type=F32).astype(o_ref.dtype)

    row = pl.BlockSpec((tm, d), lambda i, j: (i, 0))
    return pl.pallas_call(
        body, name=name, grid=grid,
        out_shape=(jax.ShapeDtypeStruct(out_shape, out_dtype), jax.ShapeDtypeStruct((s, d), BF16)),
        in_specs=[row, pl.BlockSpec((1, d), lambda i, j: (0, 0)), b_spec],
        out_specs=(o_spec, row), compiler_params=_params(),
    )(h, g.reshape(1, d), b)


def _matmul_rms_bwd(a, w, h, g, res, name):
    s, n = a.shape
    d = w.shape[0]
    tm = min(ROW_TILE, s)

    def body(a_ref, w_ref, h_ref, g_ref, r_ref, dh_ref, dg_ref):
        dy = lax.dot_general(a_ref[...].astype(BF16), w_ref[...].astype(BF16), (NT, ((), ())),
                             preferred_element_type=F32)
        hv = h_ref[...]
        r = lax.rsqrt(jnp.mean(hv * hv, axis=-1, keepdims=True) + RMS_EPS)
        hn = hv * r
        u = dy * g_ref[...]
        dh_ref[...] = r * (u - hn * jnp.mean(u * hn, axis=-1, keepdims=True)) + r_ref[...]
        part = jnp.sum(dy * hn, axis=0, keepdims=True)

        @pl.when(pl.program_id(0) == 0)
        def _():
            dg_ref[...] = part

        @pl.when(pl.program_id(0) > 0)
        def _():
            dg_ref[...] += part

    row = pl.BlockSpec((tm, d), lambda i: (i, 0))
    vec = pl.BlockSpec((1, d), lambda i: (0, 0))
    dh, dg = pl.pallas_call(
        body, name=name, grid=(s // tm,),
        out_shape=(jax.ShapeDtypeStruct((s, d), F32), jax.ShapeDtypeStruct((1, d), F32)),
        in_specs=[pl.BlockSpec((tm, n), lambda i: (i, 0)), pl.BlockSpec((d, n), lambda i: (0, 0)), row, vec, row],
        out_specs=(row, vec), compiler_params=_params(),
    )(a, w, h, g.reshape(1, d), res)
    return dh, dg.reshape(d)


GRAD_DTYPE = BF16


def _mm_tn(a, b, name, tk=512, tn=None):
    s, k = a.shape
    n = b.shape[1]
    tn = tn or n
    tk = min(tk, k)
    ts = s if b.dtype == BF16 else max(s // 2, 1)
    return _matmul(a, b, (k, n), grid=(k // tk, n // tn, s // ts),
                   a_spec=pl.BlockSpec((ts, tk), lambda i, j, r: (r, i)),
                   b_spec=pl.BlockSpec((ts, tn), lambda i, j, r: (r, j)),
                   o_spec=pl.BlockSpec((tk, tn), lambda i, j, r: (i, j)), dims=TN, nred=s // ts, name=name,
                   out_dtype=GRAD_DTYPE)


def _rms_fwd(h, g, name):
    s, d = h.shape
    tm = min(ROW_TILE, s)

    def body(h_ref, g_ref, o_ref):
        hv = h_ref[...]
        r = lax.rsqrt(jnp.mean(hv * hv, axis=-1, keepdims=True) + RMS_EPS)
        o_ref[...] = (hv * r * g_ref[...]).astype(o_ref.dtype)

    return pl.pallas_call(
        body, name=name, grid=(s // tm,), out_shape=jax.ShapeDtypeStruct((s, d), BF16),
        in_specs=[pl.BlockSpec((tm, d), lambda i: (i, 0)), pl.BlockSpec((1, d), lambda i: (0, 0))],
        out_specs=pl.BlockSpec((tm, d), lambda i: (i, 0)), compiler_params=_params(),
    )(h, g.reshape(1, d))


def _rms_bwd(dy, h, g, res, name):
    s, d = h.shape
    tm = min(ROW_TILE, s)
    has_res = res is not None

    def body(*refs):
        dy_ref, h_ref, g_ref = refs[:3]
        r_ref = refs[3] if has_res else None
        dh_ref, dg_ref = refs[-2], refs[-1]
        hv = h_ref[...]
        r = lax.rsqrt(jnp.mean(hv * hv, axis=-1, keepdims=True) + RMS_EPS)
        hn = hv * r
        dyv = dy_ref[...].astype(F32)
        u = dyv * g_ref[...]
        dh = r * (u - hn * jnp.mean(u * hn, axis=-1, keepdims=True))
        if has_res:
            dh = dh + r_ref[...]
        dh_ref[...] = dh
        part = jnp.sum(dyv * hn, axis=0, keepdims=True)

        @pl.when(pl.program_id(0) == 0)
        def _():
            dg_ref[...] = part

        @pl.when(pl.program_id(0) > 0)
        def _():
            dg_ref[...] += part

    row = pl.BlockSpec((tm, d), lambda i: (i, 0))
    vec = pl.BlockSpec((1, d), lambda i: (0, 0))
    dh, dg = pl.pallas_call(
        body, name=name, grid=(s // tm,),
        out_shape=(jax.ShapeDtypeStruct((s, d), F32), jax.ShapeDtypeStruct((1, d), F32)),
        in_specs=[row, row, vec] + ([row] if has_res else []),
        out_specs=(row, vec), compiler_params=_params(),
    )(*((dy, h, g.reshape(1, d)) + ((res,) if has_res else ())))
    return dh, dg.reshape(d)


def _loss_head(h, g, target):
    s, d = h.shape
    tm = min(ROW_TILE, s)

    def body(h_ref, g_ref, t_ref, loss_ref, dh_ref, dg_ref):
        hv = h_ref[...]
        r = lax.rsqrt(jnp.mean(hv * hv, axis=-1, keepdims=True) + RMS_EPS)
        hn = hv * r
        gv = g_ref[...]
        err = hn * gv - t_ref[...]
        rows = jnp.mean(err * err, axis=-1, keepdims=True)
        lpart = 0.5 * jnp.sum(rows, axis=0, keepdims=True) + jnp.zeros((1, 128), F32)
        dy = err * (1.0 / d)
        u = dy * gv
        dh_ref[...] = r * (u - hn * jnp.mean(u * hn, axis=-1, keepdims=True))
        gpart = jnp.sum(dy * hn, axis=0, keepdims=True)

        @pl.when(pl.program_id(0) == 0)
        def _():
            dg_ref[...] = gpart
            loss_ref[...] = lpart

        @pl.when(pl.program_id(0) > 0)
        def _():
            dg_ref[...] += gpart
            loss_ref[...] += lpart

    row = pl.BlockSpec((tm, d), lambda i: (i, 0))
    vec = pl.BlockSpec((1, d), lambda i: (0, 0))
    return pl.pallas_call(
        body, name="loss_head", grid=(s // tm,),
        out_shape=(jax.ShapeDtypeStruct((1, 128), F32), jax.ShapeDtypeStruct((s, d), F32),
                   jax.ShapeDtypeStruct((1, d), F32)),
        in_specs=[row, vec, row],
        out_specs=(pl.BlockSpec((1, 128), lambda i: (0, 0)), row, vec), compiler_params=_params(),
    )(h, g.reshape(1, d), target)


def _shift_down(x, k):
    return pltpu.roll(x, k, 0)


def _shift_up(x, k):
    return pltpu.roll(x, x.shape[0] - k, 0)


def _conv3(x, w):
    return w[2:3, :] * x + w[1:2, :] * _shift_down(x, 1) + w[0:1, :] * _shift_down(x, 2)


def _conv3_t(x, w):
    return w[2:3, :] * x + w[1:2, :] * _shift_up(x, 1) + w[0:1, :] * _shift_up(x, 2)


def _sigmoid(x):
    return 1.0 / (1.0 + jnp.exp(-x))


def _prev_map(tile, halo, col):
    return lambda i: (jnp.maximum(i * (tile // halo) - 1, 0), col)


def _next_map(tile, halo, col, nrows):
    return lambda i: (jnp.minimum((i + 1) * (tile // halo), nrows // halo - 1), col)


def _sconv_fwd(proj, w):
    s = proj.shape[0]
    t = min(ROW_TILE, s)

    def body(cur_ref, prev_ref, w_ref, o_ref):
        i = pl.program_id(0)
        prev = prev_ref[...] * (i > 0).astype(F32)
        ext = jnp.concatenate([prev, cur_ref[...]], axis=0)
        sv = ext[:, 2 * GROUP:3 * GROUP] * ext[:, 0:GROUP]
        y = ext[:, GROUP:2 * GROUP] * _conv3(sv, w_ref[...])
        o_ref[...] = y[8:].astype(o_ref.dtype)

    return pl.pallas_call(
        body, name="sconv_fwd", grid=(s // t,), out_shape=jax.ShapeDtypeStruct((s, 4 * GROUP), BF16),
        in_specs=[pl.BlockSpec((t, 3 * GROUP), lambda i: (i, 0)),
                  pl.BlockSpec((8, 3 * GROUP), _prev_map(t, 8, 0)),
                  pl.BlockSpec((3, GROUP), lambda i: (0, 0))],
        out_specs=pl.BlockSpec((t, GROUP), lambda i: (i, 0)), compiler_params=_params(),
    )(proj, proj, w)


def _sconv_bwd(proj, w, dy):
    s = proj.shape[0]
    t = min(ROW_TILE, s)
    nt = s // t

    def body(cur_ref, prev_ref, next_ref, w_ref, dy_ref, dyn_ref, dp_ref, dw_ref):
        i = pl.program_id(0)
        first = (i > 0).astype(F32)
        last = (i < nt - 1).astype(F32)
        ext = jnp.concatenate([prev_ref[...] * first, cur_ref[...], next_ref[...] * last], axis=0)
        dye = jnp.concatenate([jnp.zeros((8, GROUP), F32), dy_ref[...], dyn_ref[...] * last], axis=0)
        hv, bv, cv = ext[:, 0:GROUP], ext[:, GROUP:2 * GROUP], ext[:, 2 * GROUP:3 * GROUP]
        wv = w_ref[...]
        sv = cv * hv
        conv = _conv3(sv, wv)
        dconv = dye * bv
        ds = _conv3_t(dconv, wv)
        dp = jnp.concatenate([ds * cv, dye * conv, ds * hv], axis=1)
        dp_ref[...] = dp[8:8 + t].astype(dp_ref.dtype)
        dc = dconv[8:8 + t]
        dw = jnp.concatenate([
            jnp.sum(dc * _shift_down(sv, 2)[8:8 + t], axis=0, keepdims=True),
            jnp.sum(dc * _shift_down(sv, 1)[8:8 + t], axis=0, keepdims=True),
            jnp.sum(dc * sv[8:8 + t], axis=0, keepdims=True),
            jnp.zeros((5, GROUP), F32)], axis=0)

        @pl.when(i == 0)
        def _():
            dw_ref[...] = dw

        @pl.when(i > 0)
        def _():
            dw_ref[...] += dw

    dp, dw = pl.pallas_call(
        body, name="sconv_bwd", grid=(nt,),
        out_shape=(jax.ShapeDtypeStruct((s, N_IN_PAD), BF16), jax.ShapeDtypeStruct((8, GROUP), F32)),
        in_specs=[pl.BlockSpec((t, 3 * GROUP), lambda i: (i, 0)),
                  pl.BlockSpec((8, 3 * GROUP), _prev_map(t, 8, 0)),
                  pl.BlockSpec((8, 3 * GROUP), _next_map(t, 8, 0, s)),
                  pl.BlockSpec((3, GROUP), lambda i: (0, 0)),
                  pl.BlockSpec((t, GROUP), lambda i: (i, 0)),
                  pl.BlockSpec((8, GROUP), _next_map(t, 8, 0, s))],
        out_specs=(pl.BlockSpec((t, 3 * GROUP), lambda i: (i, 0)), pl.BlockSpec((8, GROUP), lambda i: (0, 0))),
        compiler_params=_params(),
    )(proj, proj, proj, w, dy, dy)
    return dp, dw[:3]


def _lane_window(shape):
    lane = lax.broadcasted_iota(jnp.int32, shape, 1)
    return lane, jnp.where(lane < 64, 2.0, jnp.where(lane < 128, 4.0, jnp.where(lane < 192, 8.0, 16.0)))


def _by_group(lane, s1, s2, s3, s4):
    return jnp.where(lane < 64, s1, jnp.where(lane < 128, s2, jnp.where(lane < 192, s3, s4)))


def _pool_z(ext, row0):
    s1 = ext + _shift_down(ext, 1)
    s2 = s1 + _shift_down(s1, 2)
    s3 = s2 + _shift_down(s2, 4)
    s4 = s3 + _shift_down(s3, 8)
    lane, win = _lane_window(ext.shape)
    tpos = (lax.broadcasted_iota(jnp.int32, ext.shape, 0) + (row0 - 16 + 1)).astype(F32)
    cnt = jnp.maximum(jnp.minimum(tpos, win), 1.0)
    return _by_group(lane, s1, s2, s3, s4) / cnt - ext


ANY_SPEC = pl.BlockSpec(memory_space=pl.ANY)


def _pool_fwd(proj, wbd, scale, ybuf):
    s = proj.shape[0]
    t = min(ROW_TILE, s)
    col = (COL_GATE - GROUP) // GROUP

    def body(cur_ref, prev_ref, w_ref, sc_ref, buf_ref, o_ref):
        i = pl.program_id(0)
        ext = jnp.concatenate([prev_ref[...] * (i > 0).astype(F32), cur_ref[...]], axis=0)
        z = _pool_z(ext, i * t)[16:]
        y = jnp.dot(z.astype(BF16), w_ref[...].astype(BF16), preferred_element_type=F32)
        o_ref[...] = (y * sc_ref[...]).astype(o_ref.dtype)

    return pl.pallas_call(
        body, name="pool_fwd", grid=(s // t,), out_shape=jax.ShapeDtypeStruct(ybuf.shape, ybuf.dtype),
        in_specs=[pl.BlockSpec((t, GROUP), lambda i: (i, col)),
                  pl.BlockSpec((16, GROUP), _prev_map(t, 16, col)),
                  pl.BlockSpec((GROUP, GROUP), lambda i: (0, 0)),
                  pl.BlockSpec((1, GROUP), lambda i: (0, 0)), ANY_SPEC],
        out_specs=pl.BlockSpec((t, GROUP), lambda i: (i, 3)), input_output_aliases={4: 0},
        compiler_params=_params(),
    )(proj, proj, wbd, scale.reshape(1, GROUP), ybuf)


def _pool_bwd(proj, wbd, scale, dy, dbuf):
    s = proj.shape[0]
    t = min(ROW_TILE, s)
    nt = s // t
    col = (COL_GATE - GROUP) // GROUP

    def body(cur_ref, prev_ref, w_ref, sc_ref, dy_ref, dyn_ref, buf_ref, dp_ref, dw_ref, dsc_ref):
        i = pl.program_id(0)
        ext = jnp.concatenate([prev_ref[...] * (i > 0).astype(F32), cur_ref[...]], axis=0)
        z = _pool_z(ext, i * t)[16:]
        wv = w_ref[...].astype(BF16)
        dyc = dy_ref[...]
        dye = jnp.concatenate([dyc, dyn_ref[...] * (i < nt - 1).astype(F32)], axis=0) * sc_ref[...]
        dz = lax.dot_general(dye.astype(BF16), wv, (NT, ((), ())), preferred_element_type=F32)
        lane, win = _lane_window(dz.shape)
        tpos = (lax.broadcasted_iota(jnp.int32, dz.shape, 0) + (i * t + 1)).astype(F32)
        e = dz / jnp.minimum(tpos, win)
        f1 = e + _shift_up(e, 1)
        f2 = f1 + _shift_up(f1, 2)
        f3 = f2 + _shift_up(f2, 4)
        f4 = f3 + _shift_up(f3, 8)
        dp = _by_group(lane, f1, f2, f3, f4) - dz
        dp_ref[...] = dp[:t].astype(dp_ref.dtype)
        zb = z.astype(BF16)
        y = jnp.dot(zb, wv, preferred_element_type=F32)
        dsc = jnp.sum(dyc * y, axis=0, keepdims=True)
        dw = lax.dot_general(zb, dye[:t].astype(BF16), (TN, ((), ())), preferred_element_type=F32)

        @pl.when(i == 0)
        def _():
            dw_ref[...] = dw
            dsc_ref[...] = dsc

        @pl.when(i > 0)
        def _():
            dw_ref[...] += dw
            dsc_ref[...] += dsc

    dp, dw, dsc = pl.pallas_call(
        body, name="pool_bwd", grid=(nt,),
        out_shape=(jax.ShapeDtypeStruct(dbuf.shape, dbuf.dtype), jax.ShapeDtypeStruct((GROUP, GROUP), F32),
                   jax.ShapeDtypeStruct((1, GROUP), F32)),
        in_specs=[pl.BlockSpec((t, GROUP), lambda i: (i, col)),
                  pl.BlockSpec((16, GROUP), _prev_map(t, 16, col)),
                  pl.BlockSpec((GROUP, GROUP), lambda i: (0, 0)),
                  pl.BlockSpec((1, GROUP), lambda i: (0, 0)),
                  pl.BlockSpec((t, GROUP), lambda i: (i, 3)),
                  pl.BlockSpec((16, GROUP), _next_map(t, 16, 3, s)), ANY_SPEC],
        out_specs=(pl.BlockSpec((t, GROUP), lambda i: (i, col)), pl.BlockSpec((GROUP, GROUP), lambda i: (0, 0)),
                   pl.BlockSpec((1, GROUP), lambda i: (0, 0))),
        input_output_aliases={6: 0}, compiler_params=_params(),
    )(proj, proj, wbd, scale.reshape(1, GROUP), dy, dy, dbuf)
    return dp, dw, dsc.reshape(GROUP)


FF_HALO = 16


def _ffn_gate_fwd(u0, w):
    s = u0.shape[1]
    t = min(ROW_TILE, s)

    def body(a_ref, ap_ref, g_ref, gp_ref, wa_ref, wg_ref, o_ref):
        first = (pl.program_id(1) > 0).astype(F32)
        a = _conv3(jnp.concatenate([ap_ref[...] * first, a_ref[...].astype(F32)], axis=0), wa_ref[...])[FF_HALO:]
        g = _conv3(jnp.concatenate([gp_ref[...] * first, g_ref[...].astype(F32)], axis=0), wg_ref[...])[FF_HALO:]
        o_ref[...] = (a * (g * _sigmoid(g))).astype(o_ref.dtype)

    def cur(off):
        return pl.BlockSpec((None, t, FF_SHARD), lambda j, i: (j + off, i, 0))

    def prev(off):
        return pl.BlockSpec((None, FF_HALO, FF_SHARD),
                            lambda j, i: (j + off, jnp.maximum(i * (t // FF_HALO) - 1, 0), 0))

    def wspec(off):
        return pl.BlockSpec((None, 3, FF_SHARD), lambda j, i: (j + off, 0, 0))

    return pl.pallas_call(
        body, name="ffn_gate_fwd", grid=(FF_HALF, s // t),
        out_shape=jax.ShapeDtypeStruct((FF_HALF, s, FF_SHARD), BF16),
        in_specs=[cur(0), prev(0), cur(FF_HALF), prev(FF_HALF), wspec(0), wspec(FF_HALF)],
        out_specs=pl.BlockSpec((None, t, FF_SHARD), lambda j, i: (j, i, 0)), compiler_params=_params(),
    )(u0, u0, u0, u0, w, w)


def _ffn_gate_bwd(u0, w, dact):
    s = u0.shape[1]
    t = min(ROW_TILE, s)
    nt = s // t

    def body(c_ref, p_ref, n_ref, w_ref, d_ref, dn_ref, du_ref, dw_ref):
        i = pl.program_id(1)
        first = (i > 0).astype(F32)
        last = (i < nt - 1).astype(F32)
        dext = jnp.concatenate([jnp.zeros((FF_HALO, FF_SHARD), F32), d_ref[...].astype(F32), dn_ref[...] * last],
                               axis=0)
        ext = [jnp.concatenate([p_ref[n] * first, c_ref[n].astype(F32), n_ref[n] * last], axis=0) for n in range(2)]
        a = _conv3(ext[0], w_ref[0])
        g = _conv3(ext[1], w_ref[1])
        sg = _sigmoid(g)
        silu = g * sg
        dus = (dext * silu, dext * a * (sg + silu * (1.0 - sg)))
        mine = slice(FF_HALO, FF_HALO + t)
        for n in range(2):
            du_ref[n] = _conv3_t(dus[n], w_ref[n])[mine].astype(du_ref.dtype)
            dc = dus[n][mine]
            dw = jnp.concatenate([
                jnp.sum(dc * _shift_down(ext[n], 2)[mine], axis=0, keepdims=True),
                jnp.sum(dc * _shift_down(ext[n], 1)[mine], axis=0, keepdims=True),
                jnp.sum(dc * ext[n][mine], axis=0, keepdims=True),
                jnp.zeros((5, FF_SHARD), F32)], axis=0)

            @pl.when(i == 0)
            def _(n=n, dw=dw):
                dw_ref[n] = dw

            @pl.when(i > 0)
            def _(n=n, dw=dw):
                dw_ref[n] += dw

    def pair(rows, row_map):
        return pl.BlockSpec((2, None, rows, FF_SHARD), lambda j, i: (0, j, row_map(i), 0))

    prev_row = lambda i: jnp.maximum(i * (t // FF_HALO) - 1, 0)
    next_row = lambda i: jnp.minimum((i + 1) * (t // FF_HALO), s // FF_HALO - 1)
    u2 = u0.reshape(2, FF_HALF, s, FF_SHARD)
    du, dw = pl.pallas_call(
        body, name="ffn_gate_bwd", grid=(FF_HALF, nt),
        out_shape=(jax.ShapeDtypeStruct((2, FF_HALF, s, FF_SHARD), BF16),
                   jax.ShapeDtypeStruct((2, FF_HALF, 8, FF_SHARD), F32)),
        in_specs=[pair(t, lambda i: i), pair(FF_HALO, prev_row), pair(FF_HALO, next_row), pair(3, lambda i: 0),
                  pl.BlockSpec((None, t, FF_SHARD), lambda j, i: (j, i, 0)),
                  pl.BlockSpec((None, FF_HALO, FF_SHARD), lambda j, i: (j, next_row(i), 0))],
        out_specs=(pair(t, lambda i: i), pair(8, lambda i: 0)),
        compiler_params=_params(),
    )(u2, u2, u2, w.reshape(2, FF_HALF, 3, FF_SHARD), dact, dact)
    return du.reshape(2 * FF_HALF, s, FF_SHARD), dw.reshape(2 * FF_HALF, 8, FF_SHARD)[:, :3]


def _rope_tables(positions):
    inv_freq = ROPE_THETA ** (-jnp.arange(0, ROPE_DIM, 2, dtype=F32) / ROPE_DIM)
    ang = positions.astype(F32)[:, None] * inv_freq
    cos, sin = jnp.cos(ang), jnp.sin(ang)
    s = positions.shape[0]
    half = ROPE_DIM // 2
    rest = HEAD_DIM - ROPE_DIM
    ca = jnp.concatenate([cos, cos, jnp.ones((s, rest), F32)], axis=1)
    cb = jnp.concatenate([-sin, jnp.zeros((s, HEAD_DIM - half), F32)], axis=1)
    cc = jnp.concatenate([jnp.zeros((s, half), F32), sin, jnp.zeros((s, rest), F32)], axis=1)
    return tuple(jnp.tile(tb, (1, N_HEADS)) for tb in (ca, cb, cc))


QK_WIDE = 128
LANE_CQ, LANE_CK = 64, 67
KT_ROWS = 80


def _three_bf16(x):
    hi = x.astype(BF16).astype(F32)
    mid = (x - hi).astype(BF16).astype(F32)
    lo = (x - hi - mid).astype(BF16).astype(F32)
    return hi, mid, lo


def _heads_split(proj, col, tables, c, name):
    s = proj.shape[0]
    t = min(ROW_TILE, s)
    rope = tables is not None
    wide = c is not None
    width = QK_WIDE if wide else HEAD_DIM

    def body(*refs):
        x_ref = refs[0]
        q_ref, k_ref, v_ref, kt_ref, vt_ref = refs[-5:]
        xv = x_ref[...]
        parts = [xv[:, 0:GROUP], xv[:, GROUP:2 * GROUP], xv[:, 2 * GROUP:3 * GROUP]]
        if rope:
            ca, cb, cc = refs[1][...], refs[2][...], refs[3][...]
            for n in range(2):
                p = parts[n]
                parts[n] = p * ca + pltpu.roll(p, GROUP - 8, 1) * cb + pltpu.roll(p, 8, 1) * cc
        parts[0] = parts[0] * (HEAD_DIM ** -0.5)
        k_t, v_t = parts[1].T, parts[2].T
        ones_row = jnp.where(lax.broadcasted_iota(jnp.int32, (KT_ROWS - HEAD_DIM, t), 0) == 0, 1.0, 0.0)
        lane = lax.broadcasted_iota(jnp.int32, (t, QK_WIDE), 1)
        zeros = jnp.zeros((t, QK_WIDE - HEAD_DIM), F32)
        for h in range(N_HEADS):
            hs = slice(h * HEAD_DIM, (h + 1) * HEAD_DIM)
            qh, kh = parts[0][:, hs], parts[1][:, hs]
            if wide:
                terms = _three_bf16(refs[-6][:, h:h + 1])
                qh = jnp.concatenate([qh, zeros], axis=1)
                kh = jnp.concatenate([kh, zeros], axis=1)
                for n in range(3):
                    qh = jnp.where(lane == LANE_CQ + n, terms[n], jnp.where(lane == LANE_CK + n, 1.0, qh))
                    kh = jnp.where(lane == LANE_CK + n, -terms[n], jnp.where(lane == LANE_CQ + n, 1.0, kh))
            q_ref[h] = qh.astype(q_ref.dtype)
            k_ref[h] = kh.astype(k_ref.dtype)
            v_ref[h] = parts[2][:, hs].astype(v_ref.dtype)
            kt_ref[h] = jnp.concatenate([k_t[hs, :], ones_row], axis=0).astype(kt_ref.dtype)
            vt_ref[h] = v_t[hs, :].astype(vt_ref.dtype)

    tab = pl.BlockSpec((t, GROUP), lambda i: (i, 0))
    qk = pl.BlockSpec((N_HEADS, t, width), lambda i: (0, i, 0))
    heads = pl.BlockSpec((N_HEADS, t, HEAD_DIM), lambda i: (0, i, 0))
    heads_t = pl.BlockSpec((N_HEADS, HEAD_DIM, t), lambda i: (0, 0, i))
    qk_shape = jax.ShapeDtypeStruct((N_HEADS, s, width), BF16)
    return pl.pallas_call(
        body, name=name, grid=(s // t,),
        out_shape=(qk_shape, qk_shape, jax.ShapeDtypeStruct((N_HEADS, s, HEAD_DIM), BF16),
                   jax.ShapeDtypeStruct((N_HEADS, KT_ROWS, s), BF16),
                   jax.ShapeDtypeStruct((N_HEADS, HEAD_DIM, s), BF16)),
        in_specs=[pl.BlockSpec((t, 3 * GROUP), lambda i: (i, col))] + ([tab, tab, tab] if rope else [])
        + ([pl.BlockSpec((t, 128), lambda i: (i, 0))] if wide else []),
        out_specs=(qk, qk, heads, pl.BlockSpec((N_HEADS, KT_ROWS, t), lambda i: (0, 0, i)), heads_t),
        compiler_params=_params(),
    )(*((proj,) + (tuple(tables) if rope else ()) + ((c,) if wide else ())))


def _heads_merge(dqt, dk, dv, tables, name, dbuf, col):
    s = dv.shape[1]
    t = min(ROW_TILE, s)
    rope = tables is not None

    wide = dk.shape[2] == QK_WIDE

    def body(*refs):
        o_ref = refs[n_in + 1]
        dq = jnp.concatenate([refs[0][h, :HEAD_DIM, :] for h in range(N_HEADS)], axis=0).T
        parts = [dq] + [jnp.concatenate([r[h][:, :HEAD_DIM] for h in range(N_HEADS)], axis=1) for r in refs[1:3]]
        parts[0] = parts[0] * (HEAD_DIM ** -0.5)
        if rope:
            ca, cb, cc = refs[3][...], refs[4][...], refs[5][...]
            for n in range(2):
                p = parts[n]
                parts[n] = p * ca + pltpu.roll(p * cb, 8, 1) + pltpu.roll(p * cc, GROUP - 8, 1)
        o_ref[...] = jnp.concatenate(parts, axis=1).astype(o_ref.dtype)
        if wide:
            over_keys = jnp.concatenate([refs[0][h, HEAD_DIM:HEAD_DIM + 8, :] for h in range(N_HEADS)]
                                        + [jnp.zeros((128 - 8 * N_HEADS, t), F32)], axis=0).T
            lane = lax.broadcasted_iota(jnp.int32, (t, 128), 1)
            dc = jnp.zeros((t, 128), F32)
            for h in range(N_HEADS):
                dc = jnp.where(lane == h, over_keys[:, 8 * h:8 * h + 1] - refs[1][h][:, LANE_CK:LANE_CK + 1], dc)
            refs[n_in + 2][...] = dc

    tab = pl.BlockSpec((t, GROUP), lambda i: (i, 0))
    heads = pl.BlockSpec((N_HEADS, t, HEAD_DIM), lambda i: (0, i, 0))
    n_in = 6 if rope else 3
    dspec = pl.BlockSpec((t, 3 * GROUP), lambda i: (i, col))
    dshape = jax.ShapeDtypeStruct(dbuf.shape, dbuf.dtype)
    return pl.pallas_call(
        body, name=name, grid=(s // t,),
        out_shape=(dshape, jax.ShapeDtypeStruct((s, 128), F32)) if wide else dshape,
        in_specs=[pl.BlockSpec((N_HEADS, KT_ROWS, t), lambda i: (0, 0, i)),
                  pl.BlockSpec((N_HEADS, t, dk.shape[2]), lambda i: (0, i, 0)), heads]
        + ([tab, tab, tab] if rope else []) + [ANY_SPEC],
        out_specs=(dspec, pl.BlockSpec((t, 128), lambda i: (i, 0))) if wide else dspec,
        input_output_aliases={n_in: 0}, compiler_params=_params(),
    )(*((dqt, dk, dv) + (tuple(tables) if rope else ()) + (dbuf,)))


def _log_sigmoid(x):
    return jnp.minimum(x, 0.0) - jnp.log(1.0 + jnp.exp(-jnp.abs(x)))


def _scan_rows(x, reverse):
    n = x.shape[0]
    row = lax.broadcasted_iota(jnp.int32, x.shape, 0)
    k = 1
    while k < n:
        if reverse:
            x = x + jnp.where(row < n - k, _shift_up(x, k), 0.0)
        else:
            x = x + jnp.where(row >= k, _shift_down(x, k), 0.0)
        k *= 2
    return x


def _gate_cumsum(proj, bias):
    s = proj.shape[0]
    col = COL_GATE // 128

    def body(z_ref, b_ref, c_ref):
        c_ref[...] = _scan_rows(_log_sigmoid(z_ref[...] + b_ref[...]), False)

    return pl.pallas_call(
        body, name="gate_cumsum", grid=(1,), out_shape=jax.ShapeDtypeStruct((s, 128), F32),
        in_specs=[pl.BlockSpec((s, 128), lambda i: (0, col)), pl.BlockSpec((1, 128), lambda i: (0, 0))],
        out_specs=pl.BlockSpec((s, 128), lambda i: (0, 0)), compiler_params=_params(),
    )(proj, bias)


def _gate_cumsum_bwd(proj, bias, dc, dbuf):
    s = proj.shape[0]
    col = COL_GATE // 128

    def body(z_ref, b_ref, dc_ref, buf_ref, dz_ref, db_ref):
        dlogf = _scan_rows(dc_ref[...], True)
        dz = dlogf * _sigmoid(-(z_ref[...] + b_ref[...]))
        dz_ref[...] = dz.astype(dz_ref.dtype)
        db_ref[...] = jnp.sum(dz, axis=0, keepdims=True)

    return pl.pallas_call(
        body, name="gate_cumsum_bwd", grid=(1,),
        out_shape=(jax.ShapeDtypeStruct(dbuf.shape, dbuf.dtype), jax.ShapeDtypeStruct((1, 128), F32)),
        in_specs=[pl.BlockSpec((s, 128), lambda i: (0, col)), pl.BlockSpec((1, 128), lambda i: (0, 0)),
                  pl.BlockSpec((s, 128), lambda i: (0, 0)), ANY_SPEC],
        out_specs=(pl.BlockSpec((s, 128), lambda i: (0, col)), pl.BlockSpec((1, 128), lambda i: (0, 0))),
        input_output_aliases={3: 0}, compiler_params=_params(),
    )(proj, bias, dc, dbuf)


DIL_REACH = 2048


def _pair_weight(mode, d):
    if mode == "fox":
        return jnp.where(d >= 0, 1.0, 0.0)
    w1 = jnp.where(jnp.abs(d - 64) <= 64, 1.0, 0.0)
    w2 = jnp.where((d & 3) == 0, jnp.where(jnp.abs(d - 256) <= 256, 1.0, 0.0), 0.0)
    w3 = jnp.where((d & 15) == 0, jnp.where(jnp.abs(d - 1024) <= 1024, 1.0, 0.0), 0.0)
    return w1 + w2 + w3


def _bias_tables(mode, tq, tk):
    nb = 2 if mode == "fox" else DIL_REACH // tk + 1
    n = lax.broadcasted_iota(jnp.int32, (nb, tk, tq), 0)
    key = lax.broadcasted_iota(jnp.int32, (nb, tk, tq), 1)
    query = lax.broadcasted_iota(jnp.int32, (nb, tk, tq), 2)
    w = _pair_weight(mode, n * tk + query - key)
    return jnp.where(w > 0.0, jnp.log(jnp.maximum(w, 1.0)), NEG)


M_INIT = -1e29


def _first_key_chunk(mode, q0, tk):
    if mode == "fox":
        return 0
    return jnp.maximum(q0 - DIL_REACH, 0) // tk


def _attention_fwd(mode, q, k, vt, tab_t, ybuf, col):
    s, width = q.shape[1], q.shape[2]
    tq = min(ATT_TQ, s)
    tk = tq
    nb = tab_t.shape[0]

    def body(q_ref, k_ref, vt_ref, tab_ref, buf_ref, y_ref, o_ref, lse_ref):
        i = pl.program_id(0)
        lo = _first_key_chunk(mode, i * tq, tk)

        def step(c, carry):
            k0 = pl.multiple_of(c * tk, tk)
            tab = tab_ref[jnp.minimum(i - c, nb - 1)]
            scores = [lax.dot_general(k_ref[h, pl.ds(k0, tk), :], q_ref[h], (NT, ((), ())),
                                      preferred_element_type=F32) for h in range(N_HEADS)]
            stats, probs = [], []
            for h in range(N_HEADS):
                m, l = carry[3 * h:3 * h + 2]
                sc = scores[h] + tab
                m_new = jnp.maximum(m, jnp.max(sc, axis=0, keepdims=True))
                alpha = jnp.exp(m - m_new)
                p = jnp.exp(sc - m_new)
                stats.append((m_new, alpha * l + jnp.sum(p, axis=0, keepdims=True), alpha))
                probs.append(p.astype(BF16))
            pv = [jnp.dot(vt_ref[h, :, pl.ds(k0, tk)], probs[h], preferred_element_type=F32) for h in range(N_HEADS)]
            new = []
            for h in range(N_HEADS):
                m_new, l, alpha = stats[h]
                new += [m_new, l, alpha * carry[3 * h + 2] + pv[h]]
            return tuple(new)

        start = (jnp.full((1, tq), M_INIT, F32), jnp.zeros((1, tq), F32), jnp.zeros((HEAD_DIM, tq), F32))
        done = lax.fori_loop(lo, i + 1, step, start * N_HEADS)
        outs = []
        for h in range(N_HEADS):
            m, l, acc = done[3 * h:3 * h + 3]
            outs.append(acc / l)
            lse_ref[h] = m + jnp.log(l)
        out = jnp.concatenate(outs, axis=0).T
        y_ref[...] = out.astype(y_ref.dtype)
        o_ref[...] = out

    rowspec = pl.BlockSpec((N_HEADS, 1, tq), lambda i: (0, 0, i))
    return pl.pallas_call(
        body, name="attention_fwd_" + mode, grid=(s // tq,),
        out_shape=(jax.ShapeDtypeStruct(ybuf.shape, ybuf.dtype), jax.ShapeDtypeStruct((s, GROUP), F32),
                   jax.ShapeDtypeStruct((N_HEADS, 1, s), F32)),
        in_specs=[pl.BlockSpec((N_HEADS, tq, width), lambda i: (0, i, 0)),
                  pl.BlockSpec((N_HEADS, s, width), lambda i: (0, 0, 0)),
                  pl.BlockSpec((N_HEADS, HEAD_DIM, s), lambda i: (0, 0, 0)),
                  pl.BlockSpec((nb, tk, tq), lambda i: (0, 0, 0)), ANY_SPEC],
        out_specs=(pl.BlockSpec((tq, GROUP), lambda i: (i, col)), pl.BlockSpec((tq, GROUP), lambda i: (i, 0)),
                   rowspec),
        input_output_aliases={4: 0}, compiler_params=_params(),
    )(q, k, vt, tab_t, ybuf)


def _attention_delta(o, do, col):
    s = o.shape[0]
    t = min(ROW_TILE, s)

    def body(o_ref, do_ref, delta_ref, dob_ref):
        dov = do_ref[...]
        prod_t = (o_ref[...] * dov).T
        for h in range(N_HEADS):
            hs = slice(h * HEAD_DIM, (h + 1) * HEAD_DIM)
            delta_ref[h] = jnp.sum(prod_t[hs, :], axis=0, keepdims=True)
            dob_ref[h] = dov[:, hs].astype(dob_ref.dtype)

    return pl.pallas_call(
        body, name="attention_delta", grid=(s // t,),
        out_shape=(jax.ShapeDtypeStruct((N_HEADS, 1, s), F32), jax.ShapeDtypeStruct((N_HEADS, s, HEAD_DIM), BF16)),
        in_specs=[pl.BlockSpec((t, GROUP), lambda i: (i, 0)), pl.BlockSpec((t, GROUP), lambda i: (i, col))],
        out_specs=(pl.BlockSpec((N_HEADS, 1, t), lambda i: (0, 0, i)),
                   pl.BlockSpec((N_HEADS, t, HEAD_DIM), lambda i: (0, i, 0))),
        compiler_params=_params(),
    )(o, do)


def _attention_bwd(mode, q, k, v, kt, tab_t, dob, lse, delta):
    s, width = q.shape[1], q.shape[2]
    tq = min(ATT_TQ, s)
    tk = tq
    nq = s // tq
    nb = tab_t.shape[0]

    def body(q_ref, k_ref, v_ref, kt_ref, tab_ref, dob_ref, lse_ref, delta_ref, dqt_ref, dk_ref, dv_ref):
        i = pl.program_id(0)

        @pl.when(i == 0)
        def _():
            dqt_ref[...] = jnp.zeros_like(dqt_ref)

        hi = nq if mode == "fox" else jnp.minimum((i * tk + tk - 1 + DIL_REACH) // tq + 1, nq)
        for h0 in range(0, N_HEADS, BWD_HEADS):
            heads = range(h0, h0 + BWD_HEADS)

            def step(c, carry, heads=heads):
                q0 = pl.multiple_of(c * tq, tq)
                qs = pl.ds(q0, tq)
                tab = tab_ref[jnp.minimum(c - i, nb - 1)]
                qv = [q_ref[h, qs, :] for h in heads]
                dov = [dob_ref[h, qs, :] for h in heads]
                sc = [lax.dot_general(k_ref[h], qv[n], (NT, ((), ())), preferred_element_type=F32)
                      for n, h in enumerate(heads)]
                dp = [lax.dot_general(v_ref[h], dov[n], (NT, ((), ())), preferred_element_type=F32)
                      for n, h in enumerate(heads)]
                pb, dsb = [], []
                for n, h in enumerate(heads):
                    p = jnp.exp(sc[n] + tab - lse_ref[h, :, qs])
                    pb.append(p.astype(BF16))
                    dsb.append((p * (dp[n] - delta_ref[h, :, qs])).astype(BF16))
                new = []
                for n, h in enumerate(heads):
                    new += [carry[2 * n] + jnp.dot(dsb[n], qv[n], preferred_element_type=F32),
                            carry[2 * n + 1] + jnp.dot(pb[n], dov[n], preferred_element_type=F32)]
                for n, h in enumerate(heads):
                    dqt_ref[h, :, qs] += jnp.dot(kt_ref[h], dsb[n], preferred_element_type=F32)
                return tuple(new)

            start = (jnp.zeros((tk, width), F32), jnp.zeros((tk, HEAD_DIM), F32))
            done = lax.fori_loop(i, hi, step, start * BWD_HEADS)
            for n, h in enumerate(heads):
                dk_ref[h] = done[2 * n]
                dv_ref[h] = done[2 * n + 1]

    def full(shape):
        return pl.BlockSpec(shape, lambda i: (0, 0, 0))

    kblk = pl.BlockSpec((N_HEADS, tk, width), lambda i: (0, i, 0))
    vblk = pl.BlockSpec((N_HEADS, tk, HEAD_DIM), lambda i: (0, i, 0))
    return pl.pallas_call(
        body, name="attention_bwd_" + mode, grid=(s // tk,),
        out_shape=(jax.ShapeDtypeStruct((N_HEADS, KT_ROWS, s), F32), jax.ShapeDtypeStruct((N_HEADS, s, width), F32),
                   jax.ShapeDtypeStruct((N_HEADS, s, HEAD_DIM), F32)),
        in_specs=[full((N_HEADS, s, width)), kblk, vblk, pl.BlockSpec((N_HEADS, KT_ROWS, tk), lambda i: (0, 0, i)),
                  full((nb, tk, tq)), full((N_HEADS, s, HEAD_DIM)), full((N_HEADS, 1, s)), full((N_HEADS, 1, s))],
        out_specs=(full((N_HEADS, KT_ROWS, s)), kblk, vblk),
        compiler_params=_params(),
    )(q, k, v, kt, tab_t, dob, lse, delta)


def _xattn_fwd(qx, kvm):
    s = qx.shape[0]
    t = min(ROW_TILE, s)

    def body(q_ref, kv_ref, o_ref):
        heads = range(XA_HEADS)
        sc = [lax.dot_general(q_ref[:, h * XA_DIM:(h + 1) * XA_DIM].astype(BF16), kv_ref[h].astype(BF16),
                              (NT, ((), ())), preferred_element_type=F32) * (XA_DIM ** -0.5) for h in heads]
        probs = []
        for h in heads:
            e = jnp.exp(sc[h] - jnp.max(sc[h], axis=-1, keepdims=True))
            probs.append((e / jnp.sum(e, axis=-1, keepdims=True)).astype(BF16))
        outs = [jnp.dot(probs[h], kv_ref[XA_HEADS + h].astype(BF16), preferred_element_type=F32) for h in heads]
        for h in heads:
            o_ref[:, h * XA_DIM:(h + 1) * XA_DIM] = outs[h].astype(o_ref.dtype)

    return pl.pallas_call(
        body, name="xattn_fwd", grid=(s // t,), out_shape=jax.ShapeDtypeStruct((s, D_MODEL), BF16),
        in_specs=[pl.BlockSpec((t, D_MODEL), lambda i: (i, 0)),
                  pl.BlockSpec((2 * XA_HEADS, MEM_LEN, XA_DIM), lambda i: (0, 0, 0))],
        out_specs=pl.BlockSpec((t, D_MODEL), lambda i: (i, 0)), compiler_params=_params(),
    )(qx, kvm)


def _xattn_bwd(qx, kvm, do):
    s = qx.shape[0]
    t = min(ROW_TILE, s)

    def body(q_ref, kv_ref, do_ref, dq_ref, dkv_ref):
        i = pl.program_id(0)
        heads = range(XA_HEADS)
        qv = [q_ref[:, h * XA_DIM:(h + 1) * XA_DIM].astype(BF16) for h in heads]
        dov = [do_ref[:, h * XA_DIM:(h + 1) * XA_DIM].astype(BF16) for h in heads]
        kv = [kv_ref[h].astype(BF16) for h in heads]
        sc = [lax.dot_general(qv[h], kv[h], (NT, ((), ())), preferred_element_type=F32) * (XA_DIM ** -0.5)
              for h in heads]
        dp = [lax.dot_general(dov[h], kv_ref[XA_HEADS + h].astype(BF16), (NT, ((), ())), preferred_element_type=F32)
              for h in heads]
        pb, ds = [], []
        for h in heads:
            e = jnp.exp(sc[h] - jnp.max(sc[h], axis=-1, keepdims=True))
            p = e / jnp.sum(e, axis=-1, keepdims=True)
            pb.append(p.astype(BF16))
            ds.append((p * (dp[h] - jnp.sum(p * dp[h], axis=-1, keepdims=True)) * (XA_DIM ** -0.5)).astype(BF16))
        dq = [jnp.dot(ds[h], kv[h], preferred_element_type=F32) for h in heads]
        dk = [lax.dot_general(ds[h], qv[h], (TN, ((), ())), preferred_element_type=F32) for h in heads]
        dv = [lax.dot_general(pb[h], dov[h], (TN, ((), ())), preferred_element_type=F32) for h in heads]
        for h in heads:
            dq_ref[:, h * XA_DIM:(h + 1) * XA_DIM] = dq[h].astype(dq_ref.dtype)

        @pl.when(i == 0)
        def _():
            for h in heads:
                dkv_ref[h] = dk[h]
                dkv_ref[XA_HEADS + h] = dv[h]

        @pl.when(i > 0)
        def _():
            for h in heads:
                dkv_ref[h] += dk[h]
                dkv_ref[XA_HEADS + h] += dv[h]

    row = pl.BlockSpec((t, D_MODEL), lambda i: (i, 0))
    kvs = pl.BlockSpec((2 * XA_HEADS, MEM_LEN, XA_DIM), lambda i: (0, 0, 0))
    return pl.pallas_call(
        body, name="xattn_bwd", grid=(s // t,),
        out_shape=(jax.ShapeDtypeStruct((s, D_MODEL), BF16),
                   jax.ShapeDtypeStruct((2 * XA_HEADS, MEM_LEN, XA_DIM), F32)),
        in_specs=[row, kvs, row], out_specs=(row, kvs), compiler_params=_params(),
    )(qx, kvm, do)


def _adamw(parts, owns, me, w, m, v, name):
    nl, r, c = w.shape
    tr = r
    for cand in (256, 128, 64, 32, 16, 8):
        if r % cand == 0 and r > cand and N_DEV * cand * c * 4 <= ADAMW_BLOCK_BYTES:
            tr = cand
            break
    nt = r // tr
    per_layer = N_DEV + (1 if owns is not None else 0)

    def body(me_ref, *refs):
        w_ref, m_ref, v_ref, g_ref, d_ref, nm_ref, nv_ref = refs[nl * per_layer:]
        layer = pl.program_id(0)
        g = None
        for l in range(nl):
            p_refs = refs[l * per_layer:(l + 1) * per_layer]
            gl = None
            for d in range(N_DEV):
                term = p_refs[d][...].astype(F32)
                if owns is not None:
                    term = jnp.where(me_ref[0] == d, p_refs[N_DEV][...].astype(F32), term)
                gl = term if gl is None else gl + term
            g = gl if g is None else jnp.where(layer == l, gl, g)
        mn = ADAM_B1 * m_ref[...] + (1.0 - ADAM_B1) * g
        vn = ADAM_B2 * v_ref[...] + (1.0 - ADAM_B2) * (g * g)
        m_hat = mn / (1.0 - ADAM_B1 ** ADAM_STEP)
        v_hat = vn / (1.0 - ADAM_B2 ** ADAM_STEP)
        g_ref[...] = g
        d_ref[...] = -ADAM_LR * (m_hat / (jnp.sqrt(v_hat) + ADAM_EPS) + ADAM_WD * w_ref[...])
        nm_ref[...] = mn
        nv_ref[...] = vn

    def rows(l, ll, i):
        return jnp.where(ll == l, i, jnp.where(ll < l, 0, nt - 1))

    def part_spec(l, d):
        if owns is None:
            return pl.BlockSpec((None, tr, c), lambda ll, i, me_ref: (d, rows(l, ll, i), 0))
        return pl.BlockSpec((None, tr, c),
                            lambda ll, i, me_ref: (jnp.where(me_ref[0] == d, (d + 1) % N_DEV, d), rows(l, ll, i), 0))

    def own_spec(l):
        return pl.BlockSpec((None, tr, c), lambda ll, i, me_ref: (me_ref[0], rows(l, ll, i), 0))

    in_specs, operands = [], []
    for l in range(nl):
        in_specs += [part_spec(l, d) for d in range(N_DEV)]
        operands += [parts[l]] * N_DEV
        if owns is not None:
            in_specs.append(own_spec(l))
            operands.append(owns[l])
    blk = pl.BlockSpec((None, tr, c), lambda ll, i, me_ref: (ll, i, 0))
    shp = jax.ShapeDtypeStruct((nl, r, c), F32)
    return pl.pallas_call(
        body, name=name, out_shape=(shp, shp, shp, shp),
        grid_spec=pltpu.PrefetchScalarGridSpec(
            num_scalar_prefetch=1, grid=(nl, nt), in_specs=in_specs + [blk, blk, blk],
            out_specs=(blk, blk, blk, blk)),
        compiler_params=_params(),
    )(me.reshape(1), *operands, w, m, v)


GROUPS = {"in": ("w_in",), "rest": ("w_out", "w_xq", "w_xo", "w_xkv", "w_up", "w_down")}
FULL_SHAPES = {"w_in": (D_MODEL, N_IN_PAD), "w_out": (D_MODEL, D_MODEL), "w_xq": (D_MODEL, D_MODEL),
               "w_xo": (D_MODEL, D_MODEL), "w_xkv": (N_DEV, D_MODEL, 2 * D_MODEL // N_DEV),
               "w_up": (N_DEV, D_MODEL, FF_SHARD), "w_down": (FF_HALF, FF_SHARD, D_MODEL)}
PIECE_SHAPES = {"w_in": (N_DEV, D_MODEL // N_DEV, N_IN_PAD), "w_out": (N_DEV, D_MODEL // N_DEV, D_MODEL),
                "w_xq": (N_DEV, D_MODEL // N_DEV, D_MODEL), "w_xo": (N_DEV, D_MODEL // N_DEV, D_MODEL),
                "w_xkv": (N_DEV, D_MODEL, 2 * D_MODEL // N_DEV), "w_up": (N_DEV, D_MODEL, FF_SHARD),
                "w_down": (N_DEV, D_FF // N_DEV, D_MODEL)}
CONV_WORDS = 8192
SMALL_WORDS = 80 * 1024


class _GatheredWeights:
    def __init__(self, states, layer):
        self.states, self.layer, self.full, self.extra = dict(states), layer, {}, None

    def need(self, group, after):
        if group in self.states:
            got, _ = _exchange_wait(self.states.pop(group), after, "gather_%s_wait_%d" % (group, self.layer))
            for name, g in zip(GROUPS[group], got):
                self.full[name] = g.reshape(FULL_SHAPES[name])
            self.extra = got[len(GROUPS[group]):]

    def __getitem__(self, name):
        return self.full[name]


def _relay_in_cols(w):
    pad = jnp.zeros(w.shape[:-1] + (N_IN_PAD - N_IN,), w.dtype)
    return jnp.concatenate([w[..., :2304], w[..., 2308:N_IN], w[..., 2304:2308], pad], axis=-1)


def _unrelay_in_cols(w):
    return jnp.concatenate([w[..., :2304], w[..., COL_GATE:COL_GATE + 4], w[..., 2304:COL_GATE]], axis=-1)


def _layer_fwd(h, memv, w, sm, tables):
    sv = {"h0": h}
    s = h.shape[0]
    tm, tb = min(ROW_TILE, s), min(MM_TILE, s)
    w.need("in", h)
    tn = N_IN_PAD // 3
    proj, xn = _norm_matmul(h, sm["g_mix"], w["w_in"], (s, N_IN_PAD), grid=(s // tb, 3),
                            b_spec=pl.BlockSpec((D_MODEL, tn), lambda i, j: (0, j)),
                            o_spec=pl.BlockSpec((tb, tn), lambda i, j: (i, j)), name="norm_mm_in")
    sv["xn"], sv["proj"] = xn, proj
    ycat = _sconv_fwd(proj, sm["w_sconv"])
    qd, kd, vd, ktd, vtd = _heads_split(proj, 1, tables["rope"], None, "split_dil")
    ycat, ob, lse_b = _attention_fwd("dil", qd, kd, vtd, tables["dil"], ycat, 1)
    sv["dil"] = (qd, kd, vd, ktd, ob, lse_b)
    c = _gate_cumsum(proj, sm["b_forget_pad"])
    qf, kf, vf, ktf, vtf = _heads_split(proj, 2, None, c, "split_fox")
    ycat, oc, lse_c = _attention_fwd("fox", qf, kf, vtf, tables["fox"], ycat, 2)
    sv["fox"] = (qf, kf, vf, ktf, oc, lse_c)
    ycat = _pool_fwd(proj, sm["w_pool_bd"], sm["pool_scale"], ycat)
    sv["ycat"] = ycat
    w.need("rest", ycat)
    h1 = _mm_nn(ycat, w["w_out"], "mm_out", res=h)
    sv["h1"] = h1
    memn = _rms_fwd(memv, sm["g_mem"], "rms_mem")
    qx, xq = _norm_matmul(h1, sm["g_xa"], w["w_xq"], (s, D_MODEL), grid=(s // tb, 1),
                          b_spec=pl.BlockSpec((D_MODEL, D_MODEL), lambda i, j: (0, 0)),
                          o_spec=pl.BlockSpec((tb, D_MODEL), lambda i, j: (i, 0)), name="norm_mm_xq",
                          out_dtype=BF16)
    kvm = _matmul(memn, w["w_xkv"], (N_DEV, MEM_LEN, XA_DIM), grid=(N_DEV, 1, 1),
                  a_spec=pl.BlockSpec((MEM_LEN, D_MODEL), lambda i, j, r: (0, 0)),
                  b_spec=pl.BlockSpec((None, D_MODEL, XA_DIM), lambda i, j, r: (i, 0, 0)),
                  o_spec=pl.BlockSpec((None, MEM_LEN, XA_DIM), lambda i, j, r: (i, 0, 0)),
                  dims=NN, nred=1, name="mm_xkv")
    ox = _xattn_fwd(qx, kvm)
    sv.update(xq=xq, memn=memn, qx=qx, kvm=kvm, ox=ox)
    h2 = _mm_nn(ox, w["w_xo"], "mm_xo", res=h1)
    sv["h2"] = h2
    u0, xf = _norm_matmul(h2, sm["g_ffn"], w["w_up"], (N_DEV, s, FF_SHARD), grid=(s // tb, N_DEV),
                          b_spec=pl.BlockSpec((None, D_MODEL, FF_SHARD), lambda i, j: (j, 0, 0)),
                          o_spec=pl.BlockSpec((None, tb, FF_SHARD), lambda i, j: (j, i, 0)), name="norm_mm_up",
                          out_dtype=BF16)
    act = _ffn_gate_fwd(u0, sm["w_ffconv"])
    sv.update(xf=xf, u0=u0, act=act)
    ospec = pl.BlockSpec((tm, D_MODEL), lambda i, j, r: (i, 0))
    h3 = _matmul(act, w["w_down"], (s, D_MODEL), grid=(s // tm, 1, 1),
                 a_spec=pl.BlockSpec((FF_HALF, tm, FF_SHARD), lambda i, j, r: (0, i, 0)),
                 b_spec=pl.BlockSpec((FF_HALF, FF_SHARD, D_MODEL), lambda i, j, r: (0, 0, 0)),
                 o_spec=ospec, dims=NN, nred=1, slabs=FF_HALF, name="mm_down", res=h2, res_spec=ospec)
    return h3, sv


def _layer_bwd(dh3, memv, w, sm, tables, sv, rest_ready):
    s = dh3.shape[0]
    tm, tb = min(ROW_TILE, s), min(MM_TILE, s)
    big, small = {}, {}
    ts = max(s // 2, 1)
    dact = _matmul(dh3, w["w_down"], (FF_HALF, s, FF_SHARD), grid=(s // tb, FF_HALF, 1),
                   a_spec=pl.BlockSpec((tb, D_MODEL), lambda i, j, r: (i, 0)),
                   b_spec=pl.BlockSpec((None, FF_SHARD, D_MODEL), lambda i, j, r: (j, 0, 0)),
                   o_spec=pl.BlockSpec((None, tb, FF_SHARD), lambda i, j, r: (j, i, 0)),
                   dims=NT, nred=1, name="mm_dact", out_dtype=BF16)
    big["w_down"] = _matmul(sv["act"], dh3, (FF_HALF, FF_SHARD, D_MODEL), grid=(FF_HALF, 1, s // ts),
                            a_spec=pl.BlockSpec((None, ts, FF_SHARD), lambda i, j, r: (i, r, 0)),
                            b_spec=pl.BlockSpec((ts, D_MODEL), lambda i, j, r: (r, 0)),
                            o_spec=pl.BlockSpec((None, FF_SHARD, D_MODEL), lambda i, j, r: (i, 0, 0)),
                            dims=TN, nred=s // ts, name="mm_dw_down", out_dtype=GRAD_DTYPE)
    du0, small["w_ffconv"] = _ffn_gate_bwd(sv["u0"], sm["w_ffconv"], dact)
    dxf = _matmul(du0, w["w_up"], (s, D_MODEL), grid=(s // tm, 1, 1),
                  a_spec=pl.BlockSpec((N_DEV, tm, FF_SHARD), lambda i, j, r: (0, i, 0)),
                  b_spec=pl.BlockSpec((N_DEV, D_MODEL, FF_SHARD), lambda i, j, r: (0, 0, 0)),
                  o_spec=pl.BlockSpec((tm, D_MODEL), lambda i, j, r: (i, 0)),
                  dims=NT, nred=1, slabs=N_DEV, name="mm_dxf")
    big["w_up"] = _matmul(sv["xf"], du0, (N_DEV, D_MODEL, FF_SHARD), grid=(N_DEV, 1, 1),
                          a_spec=pl.BlockSpec((s, D_MODEL), lambda i, j, r: (0, 0)),
                          b_spec=pl.BlockSpec((None, s, FF_SHARD), lambda i, j, r: (i, 0, 0)),
                          o_spec=pl.BlockSpec((None, D_MODEL, FF_SHARD), lambda i, j, r: (i, 0, 0)),
                          dims=TN, nred=1, name="mm_dw_up", out_dtype=GRAD_DTYPE)
    dh2, small["g_ffn"] = _rms_bwd(dxf, sv["h2"], sm["g_ffn"], dh3, "rms_ffn_bwd")
    dox = _mm_nt(dh2, w["w_xo"], "mm_dox", out_dtype=BF16)
    big["w_xo"] = _mm_tn(sv["ox"], dh2, "mm_dw_xo")
    dqx, dkvm = _xattn_bwd(sv["qx"], sv["kvm"], dox)
    big["w_xq"] = _mm_tn(sv["xq"], dqx, "mm_dw_xq")
    big["w_xkv"] = _matmul(sv["memn"], dkvm, (N_DEV, D_MODEL, XA_DIM), grid=(N_DEV, 1, 1),
                           a_spec=pl.BlockSpec((MEM_LEN, D_MODEL), lambda i, j, r: (0, 0)),
                           b_spec=pl.BlockSpec((None, MEM_LEN, XA_DIM), lambda i, j, r: (i, 0, 0)),
                           o_spec=pl.BlockSpec((None, D_MODEL, XA_DIM), lambda i, j, r: (i, 0, 0)),
                           dims=TN, nred=1, name="mm_dw_xkv", out_dtype=GRAD_DTYPE)
    dmemn = _matmul(dkvm, w["w_xkv"], (MEM_LEN, D_MODEL), grid=(1, 1, 1),
                    a_spec=pl.BlockSpec((N_DEV, MEM_LEN, XA_DIM), lambda i, j, r: (0, 0, 0)),
                    b_spec=pl.BlockSpec((N_DEV, D_MODEL, XA_DIM), lambda i, j, r: (0, 0, 0)),
                    o_spec=pl.BlockSpec((MEM_LEN, D_MODEL), lambda i, j, r: (0, 0)),
                    dims=NT, nred=1, slabs=N_DEV, name="mm_dmemn")
    _, small["g_mem"] = _rms_bwd(dmemn, memv, sm["g_mem"], None, "rms_mem_bwd")
    dh1, small["g_xa"] = _matmul_rms_bwd(dqx, w["w_xq"], sv["h1"], sm["g_xa"], dh2, "mm_dxq_rms_bwd")
    dycat = _mm_nt(dh1, w["w_out"], "mm_dycat")
    big["w_out"] = _mm_tn(sv["ycat"], dh1, "mm_dw_out")
    proj = sv["proj"]
    dproj, small["w_sconv"] = _sconv_bwd(proj, sm["w_sconv"] + rest_ready(big), dycat)
    qd, kd, vd, ktd, ob, lse_b = sv["dil"]
    delta, dob = _attention_delta(ob, dycat, 1)
    dqt, dk, dv = _attention_bwd("dil", qd, kd, vd, ktd, tables["dil"], dob, lse_b, delta)
    dproj = _heads_merge(dqt, dk, dv, tables["rope"], "merge_dil", dproj, 1)
    qf, kf, vf, ktf, oc, lse_c = sv["fox"]
    delta, dob = _attention_delta(oc, dycat, 2)
    dqt, dk, dv = _attention_bwd("fox", qf, kf, vf, ktf, tables["fox"], dob, lse_c, delta)
    dproj, dc = _heads_merge(dqt, dk, dv, None, "merge_fox", dproj, 2)
    dproj, dbias = _gate_cumsum_bwd(proj, sm["b_forget_pad"], dc, dproj)
    small["b_forget"] = dbias[0, :N_HEADS]
    dproj, dwbd, small["pool_scale"] = _pool_bwd(proj, sm["w_pool_bd"], sm["pool_scale"], dycat, dproj)
    small["w_pool"] = jnp.stack([dwbd[64 * g:64 * (g + 1), 64 * g:64 * (g + 1)] for g in range(4)])
    big["w_in"] = _mm_tn(sv["xn"], dproj, "mm_dw_in", tn=896)
    dh0, small["g_mix"] = _matmul_rms_bwd(dproj, w["w_in"], sv["h0"], sm["g_mix"], dh1, "mm_dxn_rms_bwd")
    return dh0, big, small


SMALL_NAMES = ("g_mix", "b_forget", "w_pool", "pool_scale", "g_xa", "g_mem", "g_ffn", "w_sconv", "w_ffconv")
WEIGHT_NAMES = ("g_mix", "w_in", "b_forget", "w_sconv", "w_pool", "pool_scale", "w_out", "g_xa", "g_mem", "w_xq",
                "w_xkv", "w_xo", "g_ffn", "w_up", "w_ffconv", "w_down", "g_final")


def _block_diag(w_pool):
    z = jnp.zeros((64, 64), F32)
    return jnp.concatenate(
        [jnp.concatenate([w_pool[g] if c == g else z for c in range(4)], axis=1) for g in range(4)], axis=0)


def kernel(x, mem, positions, g_mix, w_in, b_forget, w_sconv, w_pool, pool_scale, w_out, g_xa, g_mem, w_xq, w_xkv, w_xo, g_ffn, w_up, w_ffconv, w_down, g_final, loss_target, m_g_mix, m_w_in, m_b_forget, m_w_sconv, m_w_pool, m_pool_scale, m_w_out, m_g_xa, m_g_mem, m_w_xq, m_w_xkv, m_w_xo, m_g_ffn, m_w_up, m_w_ffconv, m_w_down, m_g_final, v_g_mix, v_w_in, v_b_forget, v_w_sconv, v_w_pool, v_pool_scale, v_w_out, v_g_xa, v_g_mem, v_w_xq, v_w_xkv, v_w_xo, v_g_ffn, v_w_up, v_w_ffconv, v_w_down, v_g_final):
    weights = dict(g_mix=g_mix, w_in=w_in, b_forget=b_forget, w_sconv=w_sconv, w_pool=w_pool, pool_scale=pool_scale,
                   w_out=w_out, g_xa=g_xa, g_mem=g_mem, w_xq=w_xq, w_xkv=w_xkv, w_xo=w_xo, g_ffn=g_ffn, w_up=w_up,
                   w_ffconv=w_ffconv, w_down=w_down, g_final=g_final)
    m_in = dict(g_mix=m_g_mix, w_in=m_w_in, b_forget=m_b_forget, w_sconv=m_w_sconv, w_pool=m_w_pool,
                pool_scale=m_pool_scale, w_out=m_w_out, g_xa=m_g_xa, g_mem=m_g_mem, w_xq=m_w_xq, w_xkv=m_w_xkv,
                w_xo=m_w_xo, g_ffn=m_g_ffn, w_up=m_w_up, w_ffconv=m_w_ffconv, w_down=m_w_down, g_final=m_g_final)
    v_in = dict(g_mix=v_g_mix, w_in=v_w_in, b_forget=v_b_forget, w_sconv=v_w_sconv, w_pool=v_w_pool,
                pool_scale=v_pool_scale, w_out=v_w_out, g_xa=v_g_xa, g_mem=v_g_mem, w_xq=v_w_xq, w_xkv=v_w_xkv,
                w_xo=v_w_xo, g_ffn=v_g_ffn, w_up=v_w_up, w_ffconv=v_w_ffconv, w_down=v_w_down, g_final=v_g_final)
    depth = w_in.shape[0]
    me = 4 * lax.axis_index("x") + 2 * lax.axis_index("y") + lax.axis_index("c")
    h = x[0]
    memv = mem[0]
    s = h.shape[0]
    tq = min(ATT_TQ, s)
    tables = {"rope": _rope_tables(positions[0]), "dil": _bias_tables("dil", tq, tq),
              "fox": _bias_tables("fox", tq, tq)}

    w_in_r = _relay_in_cols(w_in)
    conv_shard = jnp.concatenate([w_sconv.reshape(-1), w_ffconv.reshape(-1)])
    conv_shard = jnp.concatenate([conv_shard, jnp.zeros((CONV_WORDS - conv_shard.shape[0],), F32)])
    conv_bits = lax.bitcast_convert_type(conv_shard, BF16).reshape(2 * CONV_WORDS // 1024, 1024)
    gathered = []
    order = jnp.zeros((), F32)
    for l in range(depth):
        shards = dict(w_in=w_in_r[l], w_out=w_out[l], w_xq=w_xq[l], w_xo=w_xo[l], w_xkv=w_xkv[l], w_up=w_up[l],
                      w_down=w_down[l])
        states = {}
        for group in ("in", "rest"):
            shards[GROUPS[group][0]] = shards[GROUPS[group][0]] + order
            xs = [_place_shard(shards[name], me, BF16, "place_%s_%d" % (name, l)) for name in GROUPS[group]]
            if l == 0 and group == "in":
                xs.append(_place_shard(conv_bits, me, BF16, "place_conv"))
            states[group], token = _exchange_start(xs, False, "gather_%s_start_%d" % (group, l))
            order = order + token[0, 0]
        gathered.append(_GatheredWeights(states, l))
    gathered[0].need("in", tables["rope"][0])
    conv_all = lax.bitcast_convert_type(gathered[0].extra[0].reshape(N_DEV, CONV_WORDS, 2), F32)
    n_sc = depth * 3 * (GROUP // N_DEV)
    sconv_full = conv_all[:, :n_sc].reshape(N_DEV, depth, 3, GROUP // N_DEV).transpose(1, 2, 0, 3).reshape(
        depth, 3, GROUP)
    ffconv_full = conv_all[:, n_sc:n_sc + depth * 3 * FF_SHARD].reshape(N_DEV, depth, 3, FF_SHARD).transpose(
        1, 0, 2, 3)

    smalls = []
    for l in range(depth):
        smalls.append(dict(
            g_mix=g_mix[l], g_xa=g_xa[l], g_mem=g_mem[l], g_ffn=g_ffn[l], pool_scale=pool_scale[l],
            w_pool_bd=_block_diag(w_pool[l]), w_sconv=sconv_full[l], w_ffconv=ffconv_full[l],
            b_forget_pad=jnp.concatenate([b_forget[l], jnp.zeros((128 - N_HEADS,), F32)]).reshape(1, 128)))
    smalls[0]["g_mix"] = smalls[0]["g_mix"] + order

    saved = []
    for l in range(depth):
        h, sv = _layer_fwd(h, memv, gathered[l], smalls[l], tables)
        saved.append(sv)
    loss_part, dh, dg_final = _loss_head(h, g_final, loss_target[0])
    loss = lax.psum(loss_part[0, 0], MESH_AXES)

    small_grads = [None] * depth
    scatters = {}

    def pieces_of(big, group):
        return [big[name].reshape(PIECE_SHAPES[name]) for name in GROUPS[group]]

    for l in reversed(range(depth)):
        def rest_ready(big, l=l):
            scatters[l, "rest"], token = _exchange_start(pieces_of(big, "rest"), True, "scatter_rest_start_%d" % l)
            return token[0, 0]

        dh, big, small_grads[l] = _layer_bwd(dh, memv, gathered[l], smalls[l], tables, saved[l], rest_ready)
        xs = pieces_of(big, "in")
        if l == 0:
            flat = [small_grads[ll][n].reshape(-1) for n in SMALL_NAMES for ll in range(depth)]
            flat = jnp.concatenate(flat + [dg_final.reshape(-1)])
            flat = jnp.concatenate([flat, jnp.zeros((SMALL_WORDS - flat.shape[0],), F32)])
            xs.append(jnp.broadcast_to(flat.reshape(1, -1, 1024), (N_DEV, SMALL_WORDS // 1024, 1024)))
        scatters[l, "in"], token = _exchange_start(xs, True, "scatter_in_start_%d" % l)
        if l > 0:
            smalls[l - 1]["w_ffconv"] = smalls[l - 1]["w_ffconv"] + token[0, 0]
    grad_x = dh[None]

    parts, owns = {}, {}

    def wait_group(group, after):
        extra = None
        for l in reversed(range(depth)):
            got, given = _exchange_wait(scatters[l, group], after, "scatter_%s_wait_%d" % (group, l))
            for name, g, x in zip(GROUPS[group], got, given):
                parts.setdefault(name, [None] * depth)[l] = g
                owns.setdefault(name, [None] * depth)[l] = x
            extra = (got[len(GROUPS[group]):], given[len(GROUPS[group]):])
        return extra

    results = {}

    def update(name, w3, m3, v3):
        outs = _adamw(parts[name], owns.get(name), me, w3, m3, v3, "adamw_" + name)
        results[name] = [o.reshape(weights[name].shape) for o in outs]

    wait_group("rest", grad_x)
    for name in GROUPS["rest"]:
        update(name, weights[name], m_in[name], v_in[name])
    small_got, small_given = wait_group("in", results["w_down"][1])
    small_all = lax.dynamic_update_slice_in_dim(small_got[0], small_given[0][:1], me, axis=0).reshape(N_DEV, -1)
    outs = _adamw(parts["w_in"], owns["w_in"], me, w_in_r, _relay_in_cols(m_w_in), _relay_in_cols(v_w_in),
                  "adamw_w_in")
    results["w_in"] = [_unrelay_in_cols(o) for o in outs]
    off = 0
    for name in SMALL_NAMES + ("g_final",):
        wv = weights[name]
        full_shape = {"w_sconv": (depth, 3, GROUP), "w_ffconv": (depth, N_DEV, 3, FF_SHARD)}.get(name, wv.shape)
        n = 1
        for dim in full_shape:
            n *= dim
        p = small_all[:, off:off + n].reshape((N_DEV,) + tuple(full_shape))
        off += n
        if name == "w_sconv":
            p = lax.dynamic_slice_in_dim(p, me * (GROUP // N_DEV), GROUP // N_DEV, axis=3)
        elif name == "w_ffconv":
            p = lax.dynamic_index_in_dim(p, me, axis=2, keepdims=False)
        shape3 = (1, 1, wv.shape[0]) if wv.ndim == 1 else (1, -1, wv.shape[-1])
        w3 = wv.reshape(shape3)
        parts[name] = [p.reshape((N_DEV,) + w3.shape[1:])]
        update(name, w3, m_in[name].reshape(shape3), v_in[name].reshape(shape3))

    return (loss, grad_x, *[results[n][0] for n in WEIGHT_NAMES], *[results[n][1] for n in WEIGHT_NAMES],
            *[results[n][2] for n in WEIGHT_NAMES], *[results[n][3] for n in WEIGHT_NAMES])
```

```python
import functools

import jax
import jax.numpy as jnp
from jax import lax
from jax.experimental import pallas as pl
from jax.experimental.pallas import tpu as pltpu

F32 = jnp.float32
BF16 = jnp.bfloat16

N_DEV = 8
D_MODEL = 1024
GROUP = 256
HEAD_DIM = 64
N_HEADS = 4
N_IN = 2564
N_IN_PAD = 2688
COL_GATE = 2560
XA_HEADS = 4
XA_DIM = 256
MEM_LEN = 256
D_FF = 2816
FF_SHARD = 704
FF_HALF = 4
ROPE_THETA = 500000.0
ROPE_DIM = 16
RMS_EPS = 1e-6
NEG = -1e30
POOL_WINDOWS = (2, 4, 8, 16)
ADAM_LR, ADAM_B1, ADAM_B2, ADAM_EPS, ADAM_WD, ADAM_STEP = 0.001, 0.9, 0.999, 1e-08, 0.01, 10

ROW_TILE = 512
MM_TILE = 1024
ATT_TQ = 256
BWD_HEADS = 4
VMEM_LIMIT = 56 * 1024 * 1024
ADAMW_BLOCK_BYTES = 4 * 1024 * 1024
PLACE_BLOCK_BYTES = 4 * 1024 * 1024

MESH_AXES = ("x", "y", "c")


def _params(**kw):
    return pltpu.CompilerParams(vmem_limit_bytes=VMEM_LIMIT, **kw)


HBM_SPEC = pl.BlockSpec(memory_space=pltpu.HBM)
SEM_SPEC = pl.BlockSpec(memory_space=pltpu.SEMAPHORE)
DATAFLOW = pltpu.SideEffectType.DATAFLOW_SIDE_EFFECTING


def _peer_copies(x_ref, land_ref, send_sems, recv_sems, scatter):
    mx, my, mc = lax.axis_index("x"), lax.axis_index("y"), lax.axis_index("c")
    me = 4 * mx + 2 * my + mc
    pairs = []
    for k in range(1, N_DEV):
        kx, ky, kc = (k >> 2) & 1, (k >> 1) & 1, k & 1
        peer_lin = me ^ k
        send = pltpu.make_async_remote_copy(
            src_ref=x_ref.at[peer_lin] if scatter else land_ref.at[me], dst_ref=land_ref.at[me],
            send_sem=send_sems.at[k - 1], recv_sem=recv_sems.at[k - 1],
            device_id=(mx ^ kx, my ^ ky, mc ^ kc), device_id_type=pl.DeviceIdType.MESH)
        arrival = pltpu.make_async_remote_copy(
            src_ref=land_ref.at[peer_lin], dst_ref=land_ref.at[peer_lin],
            send_sem=send_sems.at[k - 1], recv_sem=recv_sems.at[k - 1],
            device_id=(mx, my, mc), device_id_type=pl.DeviceIdType.MESH)
        pairs.append((send, arrival))
    return pairs


def _exchange_start(xs, scatter, name):
    n = len(xs)
    ns = n if scatter else 0

    def body(*refs):
        srcs = refs[:ns] if scatter else (None,) * n
        lands, sends, recvs = refs[ns:ns + n], refs[ns + n:ns + 2 * n], refs[ns + 2 * n:ns + 3 * n]
        for t in range(n):
            for send, _ in _peer_copies(srcs[t], lands[t], sends[t], recvs[t], scatter):
                send.start()
        token = refs[-1]
        token[...] = jnp.zeros_like(token)

    sems = pltpu.SemaphoreType.DMA((N_DEV - 1,))
    operands = [pltpu.with_memory_space_constraint(x, pltpu.HBM) for x in xs]
    if scatter:
        operands += [pltpu.with_memory_space_constraint(lax.empty(x.shape, x.dtype), pltpu.HBM) for x in xs]
    outs = pl.pallas_call(
        body, name=name,
        out_shape=(sems,) * (2 * n) + tuple(pltpu.HBM(a.shape, a.dtype) for a in operands)
        + (jax.ShapeDtypeStruct((8, 128), F32),),
        in_specs=(HBM_SPEC,) * (ns + n),
        out_specs=(SEM_SPEC,) * (2 * n) + (HBM_SPEC,) * (ns + n) + (pl.BlockSpec(memory_space=pltpu.VMEM),),
        input_output_aliases={i: 2 * n + i for i in range(ns + n)},
        compiler_params=pltpu.CompilerParams(has_side_effects=DATAFLOW),
    )(*operands)
    return (outs[:-1], scatter), outs[-1]


def _exchange_wait(state, after, name):
    held, scatter = state
    n = len(held) // (4 if scatter else 3)
    ns = n if scatter else 0
    sems, thru = held[:2 * n], held[2 * n:]

    def body(*refs):
        srcs = refs[:ns] if scatter else (None,) * n
        lands, sends, recvs = refs[ns:ns + n], refs[ns + n:ns + 2 * n], refs[ns + 2 * n:ns + 3 * n]
        for t in range(n):
            for send, arrival in _peer_copies(srcs[t], lands[t], sends[t], recvs[t], scatter):
                send.wait_send()
                arrival.wait_recv()

    outs = pl.pallas_call(
        body, name=name,
        out_shape=tuple(pltpu.HBM(a.shape, a.dtype) for a in thru),
        in_specs=(HBM_SPEC,) * (ns + n) + (SEM_SPEC,) * (2 * n) + (pl.BlockSpec(memory_space=pl.ANY),),
        out_specs=(HBM_SPEC,) * (ns + n), input_output_aliases={i: i for i in range(ns + n)},
        compiler_params=pltpu.CompilerParams(has_side_effects=DATAFLOW),
    )(*thru, *sems, after)
    return list(outs[ns:]), list(outs[:ns])


def _place_shard(x, me, dtype, name):
    r, c = x.shape
    tr = r
    if r * c * 4 > PLACE_BLOCK_BYTES:
        for cand in (512, 256, 128, 64, 32, 16):
            if r % cand == 0 and cand * c * 4 <= PLACE_BLOCK_BYTES:
                tr = cand
                break

    def body(me_ref, x_ref, o_ref):
        o_ref[...] = x_ref[...].astype(o_ref.dtype)

    return pl.pallas_call(
        body, name=name, out_shape=jax.ShapeDtypeStruct((N_DEV, r, c), dtype),
        grid_spec=pltpu.PrefetchScalarGridSpec(
            num_scalar_prefetch=1, grid=(r // tr,),
            in_specs=[pl.BlockSpec((tr, c), lambda i, me_ref: (i, 0))],
            out_specs=pl.BlockSpec((None, tr, c), lambda i, me_ref: (me_ref[0], i, 0))),
        compiler_params=_params(),
    )(me.reshape(1), x)


NN = ((1,), (0,))
NT = ((1,), (1,))
TN = ((0,), (0,))


def _matmul(a, b, out_shape, *, grid, a_spec, b_spec, o_spec, dims, nred, name, res=None, res_spec=None,
            out_dtype=F32, slabs=0):
    has_res = res is not None

    def body(*refs):
        a_ref, b_ref = refs[0], refs[1]
        r_ref = refs[2] if has_res else None
        o_ref = refs[3] if has_res else refs[2]
        if slabs:
            part = None
            for n in range(slabs):
                term = lax.dot_general(a_ref[n].astype(BF16), b_ref[n].astype(BF16), (dims, ((), ())),
                                       preferred_element_type=F32)
                part = term if part is None else part + term
        else:
            part = lax.dot_general(a_ref[...].astype(BF16), b_ref[...].astype(BF16), (dims, ((), ())),
                                   preferred_element_type=F32)
        if nred == 1:
            if has_res:
                part = part + r_ref[...]
            o_ref[...] = part.astype(o_ref.dtype)
        else:
            acc = refs[-1]
            r = pl.program_id(2)

            @pl.when(r == 0)
            def _():
                acc[...] = part

            @pl.when(r > 0)
            def _():
                acc[...] += part

            @pl.when(r == nred - 1)
            def _():
                tot = acc[...]
                if has_res:
                    tot = tot + r_ref[...]
                o_ref[...] = tot.astype(o_ref.dtype)

    in_specs = [a_spec, b_spec] + ([res_spec] if has_res else [])
    args = (a, b) + ((res,) if has_res else ())
    acc_shape = tuple(d for d in o_spec.block_shape if d is not None)
    return pl.pallas_call(
        body, name=name, grid=grid, out_shape=jax.ShapeDtypeStruct(out_shape, out_dtype),
        in_specs=in_specs, out_specs=o_spec,
        scratch_shapes=[pltpu.VMEM(acc_shape, F32)] if nred > 1 else [],
        compiler_params=_params(),
    )(*args)


def _mm_nn(a, w, name, res=None, tn=None, out_dtype=F32):
    m, k = a.shape
    n = w.shape[1]
    tn = tn or n
    tm = min(MM_TILE, m)
    ospec = pl.BlockSpec((tm, tn), lambda i, j, r: (i, j))
    return _matmul(a, w, (m, n), grid=(m // tm, n // tn, 1),
                   a_spec=pl.BlockSpec((tm, k), lambda i, j, r: (i, 0)),
                   b_spec=pl.BlockSpec((k, tn), lambda i, j, r: (0, j)),
                   o_spec=ospec, dims=NN, nred=1, name=name, res=res, res_spec=ospec if res is not None else None,
                   out_dtype=out_dtype)


def _mm_nt(a, w, name, out_dtype=F32):
    m, n = a.shape
    k = w.shape[0]
    tm = min(MM_TILE, m)
    return _matmul(a, w, (m, k), grid=(m // tm, 1, 1),
                   a_spec=pl.BlockSpec((tm, n), lambda i, j, r: (i, 0)),
                   b_spec=pl.BlockSpec((k, n), lambda i, j, r: (0, 0)),
                   o_spec=pl.BlockSpec((tm, k), lambda i, j, r: (i, 0)), dims=NT, nred=1, name=name,
                   out_dtype=out_dtype)


def _norm_matmul(h, g, b, out_shape, *, grid, b_spec, o_spec, name, out_dtype=F32):
    s, d = h.shape
    tm = s // grid[0]

    def body(h_ref, g_ref, b_ref, o_ref, xn_ref):
        @pl.when(pl.program_id(1) == 0)
        def _():
            hv = h_ref[...]
            r = lax.rsqrt(jnp.mean(hv * hv, axis=-1, keepdims=True) + RMS_EPS)
            xn_ref[...] = (hv * r * g_ref[...]).astype(xn_ref.dtype)

        o_ref[...] = jnp.dot(xn_ref[...], b_ref[...].astype(BF16), preferred_element_type=F32).astype(o_ref.dtype)

    row = pl.BlockSpec((tm, d), lambda i, j: (i, 0))
    return pl.pallas_call(
        body, name=name, grid=grid,
        out_shape=(jax.ShapeDtypeStruct(out_shape, out_dtype), jax.ShapeDtypeStruct((s, d), BF16)),
        in_specs=[row, pl.BlockSpec((1, d), lambda i, j: (0, 0)), b_spec],
        out_specs=(o_spec, row), compiler_params=_params(),
    )(h, g.reshape(1, d), b)


def _matmul_rms_bwd(a, w, h, g, res, name, tm=ROW_TILE):
    slabs = a.shape[0] if a.ndim == 3 else 0
    s, n = a.shape[-2:]
    d = w.shape[-2]
    tm = min(tm, s)

    def body(a_ref, w_ref, h_ref, g_ref, r_ref, dh_ref, dg_ref):
        if slabs:
            dy = None
            for j in range(slabs):
                term = lax.dot_general(a_ref[j].astype(BF16), w_ref[j].astype(BF16), (NT, ((), ())),
                                       preferred_element_type=F32)
                dy = term if dy is None else dy + term
        else:
            dy = lax.dot_general(a_ref[...].astype(BF16), w_ref[...].astype(BF16), (NT, ((), ())),
                                 preferred_element_type=F32)
        hv = h_ref[...]
        r = lax.rsqrt(jnp.mean(hv * hv, axis=-1, keepdims=True) + RMS_EPS)
        hn = hv * r
        u = dy * g_ref[...]
        dh_ref[...] = r * (u - hn * jnp.mean(u * hn, axis=-1, keepdims=True)) + r_ref[...]
        part = jnp.sum(dy * hn, axis=0, keepdims=True)

        @pl.when(pl.program_id(0) == 0)
        def _():
            dg_ref[...] = part

        @pl.when(pl.program_id(0) > 0)
        def _():
            dg_ref[...] += part

    row = pl.BlockSpec((tm, d), lambda i: (i, 0))
    vec = pl.BlockSpec((1, d), lambda i: (0, 0))
    if slabs:
        a_spec = pl.BlockSpec((slabs, tm, n), lambda i: (0, i, 0))
        w_spec = pl.BlockSpec((slabs, d, n), lambda i: (0, 0, 0))
    else:
        a_spec = pl.BlockSpec((tm, n), lambda i: (i, 0))
        w_spec = pl.BlockSpec((d, n), lambda i: (0, 0))
    dh, dg = pl.pallas_call(
        body, name=name, grid=(s // tm,),
        out_shape=(jax.ShapeDtypeStruct((s, d), F32), jax.ShapeDtypeStruct((1, d), F32)),
        in_specs=[a_spec, w_spec, row, vec, row], out_specs=(row, vec), compiler_params=_params(),
    )(a, w, h, g.reshape(1, d), res)
    return dh, dg.reshape(d)


GRAD_DTYPE = BF16


def _mm_tn(a, b, name, tk=512, tn=None):
    s, k = a.shape
    n = b.shape[1]
    tn = tn or n
    tk = min(tk, k)
    ts = s if b.dtype == BF16 else max(s // 2, 1)
    return _matmul(a, b, (k, n), grid=(k // tk, n // tn, s // ts),
                   a_spec=pl.BlockSpec((ts, tk), lambda i, j, r: (r, i)),
                   b_spec=pl.BlockSpec((ts, tn), lambda i, j, r: (r, j)),
                   o_spec=pl.BlockSpec((tk, tn), lambda i, j, r: (i, j)), dims=TN, nred=s // ts, name=name,
                   out_dtype=GRAD_DTYPE)


def _rms_fwd(h, g, name):
    s, d = h.shape
    tm = min(ROW_TILE, s)

    def body(h_ref, g_ref, o_ref):
        hv = h_ref[...]
        r = lax.rsqrt(jnp.mean(hv * hv, axis=-1, keepdims=True) + RMS_EPS)
        o_ref[...] = (hv * r * g_ref[...]).astype(o_ref.dtype)

    return pl.pallas_call(
        body, name=name, grid=(s // tm,), out_shape=jax.ShapeDtypeStruct((s, d), BF16),
        in_specs=[pl.BlockSpec((tm, d), lambda i: (i, 0)), pl.BlockSpec((1, d), lambda i: (0, 0))],
        out_specs=pl.BlockSpec((tm, d), lambda i: (i, 0)), compiler_params=_params(),
    )(h, g.reshape(1, d))


def _rms_bwd(dy, h, g, res, name):
    s, d = h.shape
    tm = min(ROW_TILE, s)
    has_res = res is not None

    def body(*refs):
        dy_ref, h_ref, g_ref = refs[:3]
        r_ref = refs[3] if has_res else None
        dh_ref, dg_ref = refs[-2], refs[-1]
        hv = h_ref[...]
        r = lax.rsqrt(jnp.mean(hv * hv, axis=-1, keepdims=True) + RMS_EPS)
        hn = hv * r
        dyv = dy_ref[...].astype(F32)
        u = dyv * g_ref[...]
        dh = r * (u - hn * jnp.mean(u * hn, axis=-1, keepdims=True))
        if has_res:
            dh = dh + r_ref[...]
        dh_ref[...] = dh
        part = jnp.sum(dyv * hn, axis=0, keepdims=True)

        @pl.when(pl.program_id(0) == 0)
        def _():
            dg_ref[...] = part

        @pl.when(pl.program_id(0) > 0)
        def _():
            dg_ref[...] += part

    row = pl.BlockSpec((tm, d), lambda i: (i, 0))
    vec = pl.BlockSpec((1, d), lambda i: (0, 0))
    dh, dg = pl.pallas_call(
        body, name=name, grid=(s // tm,),
        out_shape=(jax.ShapeDtypeStruct((s, d), F32), jax.ShapeDtypeStruct((1, d), F32)),
        in_specs=[row, row, vec] + ([row] if has_res else []),
        out_specs=(row, vec), compiler_params=_params(),
    )(*((dy, h, g.reshape(1, d)) + ((res,) if has_res else ())))
    return dh, dg.reshape(d)


def _loss_head(h, g, target):
    s, d = h.shape
    tm = min(ROW_TILE, s)

    def body(h_ref, g_ref, t_ref, loss_ref, dh_ref, dg_ref):
        hv = h_ref[...]
        r = lax.rsqrt(jnp.mean(hv * hv, axis=-1, keepdims=True) + RMS_EPS)
        hn = hv * r
        gv = g_ref[...]
        err = hn * gv - t_ref[...]
        rows = jnp.mean(err * err, axis=-1, keepdims=True)
        lpart = 0.5 * jnp.sum(rows, axis=0, keepdims=True) + jnp.zeros((1, 128), F32)
        dy = err * (1.0 / d)
        u = dy * gv
        dh_ref[...] = r * (u - hn * jnp.mean(u * hn, axis=-1, keepdims=True))
        gpart = jnp.sum(dy * hn, axis=0, keepdims=True)

        @pl.when(pl.program_id(0) == 0)
        def _():
            dg_ref[...] = gpart
            loss_ref[...] = lpart

        @pl.when(pl.program_id(0) > 0)
        def _():
            dg_ref[...] += gpart
            loss_ref[...] += lpart

    row = pl.BlockSpec((tm, d), lambda i: (i, 0))
    vec = pl.BlockSpec((1, d), lambda i: (0, 0))
    return pl.pallas_call(
        body, name="loss_head", grid=(s // tm,),
        out_shape=(jax.ShapeDtypeStruct((1, 128), F32), jax.ShapeDtypeStruct((s, d), F32),
                   jax.ShapeDtypeStruct((1, d), F32)),
        in_specs=[row, vec, row],
        out_specs=(pl.BlockSpec((1, 128), lambda i: (0, 0)), row, vec), compiler_params=_params(),
    )(h, g.reshape(1, d), target)


def _shift_down(x, k):
    return pltpu.roll(x, k, 0)


def _shift_up(x, k):
    return pltpu.roll(x, x.shape[0] - k, 0)


def _conv3(x, w):
    return w[2:3, :] * x + w[1:2, :] * _shift_down(x, 1) + w[0:1, :] * _shift_down(x, 2)


def _conv3_t(x, w):
    return w[2:3, :] * x + w[1:2, :] * _shift_up(x, 1) + w[0:1, :] * _shift_up(x, 2)


def _sigmoid(x):
    return 1.0 / (1.0 + jnp.exp(-x))


def _prev_map(tile, halo, col):
    return lambda i: (jnp.maximum(i * (tile // halo) - 1, 0), col)


def _next_map(tile, halo, col, nrows):
    return lambda i: (jnp.minimum((i + 1) * (tile // halo), nrows // halo - 1), col)


def _sconv_fwd(proj, w):
    s = proj.shape[0]
    t = min(ROW_TILE, s)

    def body(cur_ref, prev_ref, w_ref, o_ref):
        i = pl.program_id(0)
        prev = prev_ref[...] * (i > 0).astype(F32)
        ext = jnp.concatenate([prev, cur_ref[...]], axis=0)
        sv = ext[:, 2 * GROUP:3 * GROUP] * ext[:, 0:GROUP]
        y = ext[:, GROUP:2 * GROUP] * _conv3(sv, w_ref[...])
        o_ref[...] = y[8:].astype(o_ref.dtype)

    return pl.pallas_call(
        body, name="sconv_fwd", grid=(s // t,), out_shape=jax.ShapeDtypeStruct((s, 4 * GROUP), BF16),
        in_specs=[pl.BlockSpec((t, 3 * GROUP), lambda i: (i, 0)),
                  pl.BlockSpec((8, 3 * GROUP), _prev_map(t, 8, 0)),
                  pl.BlockSpec((3, GROUP), lambda i: (0, 0))],
        out_specs=pl.BlockSpec((t, GROUP), lambda i: (i, 0)), compiler_params=_params(),
    )(proj, proj, w)


def _sconv_bwd(proj, w, dy):
    s = proj.shape[0]
    t = min(ROW_TILE, s)
    nt = s // t

    def body(cur_ref, prev_ref, next_ref, w_ref, dy_ref, dyn_ref, dp_ref, dw_ref):
        i = pl.program_id(0)
        first = (i > 0).astype(F32)
        last = (i < nt - 1).astype(F32)
        ext = jnp.concatenate([prev_ref[...] * first, cur_ref[...], next_ref[...] * last], axis=0)
        dye = jnp.concatenate([jnp.zeros((8, GROUP), F32), dy_ref[...], dyn_ref[...] * last], axis=0)
        hv, bv, cv = ext[:, 0:GROUP], ext[:, GROUP:2 * GROUP], ext[:, 2 * GROUP:3 * GROUP]
        wv = w_ref[...]
        sv = cv * hv
        conv = _conv3(sv, wv)
        dconv = dye * bv
        ds = _conv3_t(dconv, wv)
        dp = jnp.concatenate([ds * cv, dye * conv, ds * hv], axis=1)
        dp_ref[...] = dp[8:8 + t].astype(dp_ref.dtype)
        dc = dconv[8:8 + t]
        dw = jnp.concatenate([
            jnp.sum(dc * _shift_down(sv, 2)[8:8 + t], axis=0, keepdims=True),
            jnp.sum(dc * _shift_down(sv, 1)[8:8 + t], axis=0, keepdims=True),
            jnp.sum(dc * sv[8:8 + t], axis=0, keepdims=True),
            jnp.zeros((5, GROUP), F32)], axis=0)

        @pl.when(i == 0)
        def _():
            dw_ref[...] = dw

        @pl.when(i > 0)
        def _():
            dw_ref[...] += dw

    dp, dw = pl.pallas_call(
        body, name="sconv_bwd", grid=(nt,),
        out_shape=(jax.ShapeDtypeStruct((s, N_IN_PAD), BF16), jax.ShapeDtypeStruct((8, GROUP), F32)),
        in_specs=[pl.BlockSpec((t, 3 * GROUP), lambda i: (i, 0)),
                  pl.BlockSpec((8, 3 * GROUP), _prev_map(t, 8, 0)),
                  pl.BlockSpec((8, 3 * GROUP), _next_map(t, 8, 0, s)),
                  pl.BlockSpec((3, GROUP), lambda i: (0, 0)),
                  pl.BlockSpec((t, GROUP), lambda i: (i, 0)),
                  pl.BlockSpec((8, GROUP), _next_map(t, 8, 0, s))],
        out_specs=(pl.BlockSpec((t, 3 * GROUP), lambda i: (i, 0)), pl.BlockSpec((8, GROUP), lambda i: (0, 0))),
        compiler_params=_params(),
    )(proj, proj, proj, w, dy, dy)
    return dp, dw[:3]


def _lane_window(shape):
    lane = lax.broadcasted_iota(jnp.int32, shape, 1)
    return lane, jnp.where(lane < 64, 2.0, jnp.where(lane < 128, 4.0, jnp.where(lane < 192, 8.0, 16.0)))


def _by_group(lane, s1, s2, s3, s4):
    return jnp.where(lane < 64, s1, jnp.where(lane < 128, s2, jnp.where(lane < 192, s3, s4)))


def _pool_z(ext, row0):
    s1 = ext + _shift_down(ext, 1)
    s2 = s1 + _shift_down(s1, 2)
    s3 = s2 + _shift_down(s2, 4)
    s4 = s3 + _shift_down(s3, 8)
    lane, win = _lane_window(ext.shape)
    tpos = (lax.broadcasted_iota(jnp.int32, ext.shape, 0) + (row0 - 16 + 1)).astype(F32)
    cnt = jnp.maximum(jnp.minimum(tpos, win), 1.0)
    return _by_group(lane, s1, s2, s3, s4) / cnt - ext


ANY_SPEC = pl.BlockSpec(memory_space=pl.ANY)


def _pool_fwd(proj, wbd, scale, ybuf):
    s = proj.shape[0]
    t = min(ROW_TILE, s)
    col = (COL_GATE - GROUP) // GROUP

    def body(cur_ref, prev_ref, w_ref, sc_ref, buf_ref, o_ref):
        i = pl.program_id(0)
        ext = jnp.concatenate([prev_ref[...] * (i > 0).astype(F32), cur_ref[...]], axis=0)
        z = _pool_z(ext, i * t)[16:]
        y = jnp.dot(z.astype(BF16), w_ref[...].astype(BF16), preferred_element_type=F32)
        o_ref[...] = (y * sc_ref[...]).astype(o_ref.dtype)

    return pl.pallas_call(
        body, name="pool_fwd", grid=(s // t,), out_shape=jax.ShapeDtypeStruct(ybuf.shape, ybuf.dtype),
        in_specs=[pl.BlockSpec((t, GROUP), lambda i: (i, col)),
                  pl.BlockSpec((16, GROUP), _prev_map(t, 16, col)),
                  pl.BlockSpec((GROUP, GROUP), lambda i: (0, 0)),
                  pl.BlockSpec((1, GROUP), lambda i: (0, 0)), ANY_SPEC],
        out_specs=pl.BlockSpec((t, GROUP), lambda i: (i, 3)), input_output_aliases={4: 0},
        compiler_params=_params(),
    )(proj, proj, wbd, scale.reshape(1, GROUP), ybuf)


def _pool_bwd(proj, wbd, scale, dy, dbuf):
    s = proj.shape[0]
    t = min(ROW_TILE, s)
    nt = s // t
    col = (COL_GATE - GROUP) // GROUP

    def body(cur_ref, prev_ref, w_ref, sc_ref, dy_ref, dyn_ref, buf_ref, dp_ref, dw_ref, dsc_ref):
        i = pl.program_id(0)
        ext = jnp.concatenate([prev_ref[...] * (i > 0).astype(F32), cur_ref[...]], axis=0)
        z = _pool_z(ext, i * t)[16:]
        wv = w_ref[...].astype(BF16)
        dyc = dy_ref[...]
        dye = jnp.concatenate([dyc, dyn_ref[...] * (i < nt - 1).astype(F32)], axis=0) * sc_ref[...]
        dz = lax.dot_general(dye.astype(BF16), wv, (NT, ((), ())), preferred_element_type=F32)
        lane, win = _lane_window(dz.shape)
        tpos = (lax.broadcasted_iota(jnp.int32, dz.shape, 0) + (i * t + 1)).astype(F32)
        e = dz / jnp.minimum(tpos, win)
        f1 = e + _shift_up(e, 1)
        f2 = f1 + _shift_up(f1, 2)
        f3 = f2 + _shift_up(f2, 4)
        f4 = f3 + _shift_up(f3, 8)
        dp = _by_group(lane, f1, f2, f3, f4) - dz
        dp_ref[...] = dp[:t].astype(dp_ref.dtype)
        zb = z.astype(BF16)
        y = jnp.dot(zb, wv, preferred_element_type=F32)
        dsc = jnp.sum(dyc * y, axis=0, keepdims=True)
        dw = lax.dot_general(zb, dye[:t].astype(BF16), (TN, ((), ())), preferred_element_type=F32)

        @pl.when(i == 0)
        def _():
            dw_ref[...] = dw
            dsc_ref[...] = dsc

        @pl.when(i > 0)
        def _():
            dw_ref[...] += dw
            dsc_ref[...] += dsc

    dp, dw, dsc = pl.pallas_call(
        body, name="pool_bwd", grid=(nt,),
        out_shape=(jax.ShapeDtypeStruct(dbuf.shape, dbuf.dtype), jax.ShapeDtypeStruct((GROUP, GROUP), F32),
                   jax.ShapeDtypeStruct((1, GROUP), F32)),
        in_specs=[pl.BlockSpec((t, GROUP), lambda i: (i, col)),
                  pl.BlockSpec((16, GROUP), _prev_map(t, 16, col)),
                  pl.BlockSpec((GROUP, GROUP), lambda i: (0, 0)),
                  pl.BlockSpec((1, GROUP), lambda i: (0, 0)),
                  pl.BlockSpec((t, GROUP), lambda i: (i, 3)),
                  pl.BlockSpec((16, GROUP), _next_map(t, 16, 3, s)), ANY_SPEC],
        out_specs=(pl.BlockSpec((t, GROUP), lambda i: (i, col)), pl.BlockSpec((GROUP, GROUP), lambda i: (0, 0)),
                   pl.BlockSpec((1, GROUP), lambda i: (0, 0))),
        input_output_aliases={6: 0}, compiler_params=_params(),
    )(proj, proj, wbd, scale.reshape(1, GROUP), dy, dy, dbuf)
    return dp, dw, dsc.reshape(GROUP)


FF_HALO = 16


def _ffn_gate_fwd(u0, w):
    s = u0.shape[1]
    t = min(ROW_TILE, s)

    def body(a_ref, ap_ref, g_ref, gp_ref, wa_ref, wg_ref, o_ref):
        first = (pl.program_id(1) > 0).astype(F32)
        a = _conv3(jnp.concatenate([ap_ref[...] * first, a_ref[...].astype(F32)], axis=0), wa_ref[...])[FF_HALO:]
        g = _conv3(jnp.concatenate([gp_ref[...] * first, g_ref[...].astype(F32)], axis=0), wg_ref[...])[FF_HALO:]
        o_ref[...] = (a * (g * _sigmoid(g))).astype(o_ref.dtype)

    def cur(off):
        return pl.BlockSpec((None, t, FF_SHARD), lambda j, i: (j + off, i, 0))

    def prev(off):
        return pl.BlockSpec((None, FF_HALO, FF_SHARD),
                            lambda j, i: (j + off, jnp.maximum(i * (t // FF_HALO) - 1, 0), 0))

    def wspec(off):
        return pl.BlockSpec((None, 3, FF_SHARD), lambda j, i: (j + off, 0, 0))

    return pl.pallas_call(
        body, name="ffn_gate_fwd", grid=(FF_HALF, s // t),
        out_shape=jax.ShapeDtypeStruct((FF_HALF, s, FF_SHARD), BF16),
        in_specs=[cur(0), prev(0), cur(FF_HALF), prev(FF_HALF), wspec(0), wspec(FF_HALF)],
        out_specs=pl.BlockSpec((None, t, FF_SHARD), lambda j, i: (j, i, 0)), compiler_params=_params(),
    )(u0, u0, u0, u0, w, w)


def _ffn_gate_bwd(u0, w, dact):
    s = u0.shape[1]
    t = min(ROW_TILE, s)
    nt = s // t

    def body(c_ref, p_ref, n_ref, w_ref, d_ref, dn_ref, du_ref, dw_ref):
        i = pl.program_id(1)
        first = (i > 0).astype(F32)
        last = (i < nt - 1).astype(F32)
        dext = jnp.concatenate([jnp.zeros((FF_HALO, FF_SHARD), F32), d_ref[...].astype(F32), dn_ref[...] * last],
                               axis=0)
        ext = [jnp.concatenate([p_ref[n] * first, c_ref[n].astype(F32), n_ref[n] * last], axis=0) for n in range(2)]
        a = _conv3(ext[0], w_ref[0])
        g = _conv3(ext[1], w_ref[1])
        sg = _sigmoid(g)
        silu = g * sg
        dus = (dext * silu, dext * a * (sg + silu * (1.0 - sg)))
        mine = slice(FF_HALO, FF_HALO + t)
        for n in range(2):
            du_ref[n] = _conv3_t(dus[n], w_ref[n])[mine].astype(du_ref.dtype)
            dc = dus[n][mine]
            dw = jnp.concatenate([
                jnp.sum(dc * _shift_down(ext[n], 2)[mine], axis=0, keepdims=True),
                jnp.sum(dc * _shift_down(ext[n], 1)[mine], axis=0, keepdims=True),
                jnp.sum(dc * ext[n][mine], axis=0, keepdims=True),
                jnp.zeros((5, FF_SHARD), F32)], axis=0)

            @pl.when(i == 0)
            def _(n=n, dw=dw):
                dw_ref[n] = dw

            @pl.when(i > 0)
            def _(n=n, dw=dw):
                dw_ref[n] += dw

    def pair(rows, row_map):
        return pl.BlockSpec((2, None, rows, FF_SHARD), lambda j, i: (0, j, row_map(i), 0))

    prev_row = lambda i: jnp.maximum(i * (t // FF_HALO) - 1, 0)
    next_row = lambda i: jnp.minimum((i + 1) * (t // FF_HALO), s // FF_HALO - 1)
    u2 = u0.reshape(2, FF_HALF, s, FF_SHARD)
    du, dw = pl.pallas_call(
        body, name="ffn_gate_bwd", grid=(FF_HALF, nt),
        out_shape=(jax.ShapeDtypeStruct((2, FF_HALF, s, FF_SHARD), BF16),
                   jax.ShapeDtypeStruct((2, FF_HALF, 8, FF_SHARD), F32)),
        in_specs=[pair(t, lambda i: i), pair(FF_HALO, prev_row), pair(FF_HALO, next_row), pair(3, lambda i: 0),
                  pl.BlockSpec((None, t, FF_SHARD), lambda j, i: (j, i, 0)),
                  pl.BlockSpec((None, FF_HALO, FF_SHARD), lambda j, i: (j, next_row(i), 0))],
        out_specs=(pair(t, lambda i: i), pair(8, lambda i: 0)),
        compiler_params=_params(),
    )(u2, u2, u2, w.reshape(2, FF_HALF, 3, FF_SHARD), dact, dact)
    return du.reshape(2 * FF_HALF, s, FF_SHARD), dw.reshape(2 * FF_HALF, 8, FF_SHARD)[:, :3]


def _rope_tables(positions):
    inv_freq = ROPE_THETA ** (-jnp.arange(0, ROPE_DIM, 2, dtype=F32) / ROPE_DIM)
    ang = positions.astype(F32)[:, None] * inv_freq
    cos, sin = jnp.cos(ang), jnp.sin(ang)
    s = positions.shape[0]
    half = ROPE_DIM // 2
    rest = HEAD_DIM - ROPE_DIM
    ca = jnp.concatenate([cos, cos, jnp.ones((s, rest), F32)], axis=1)
    cb = jnp.concatenate([-sin, jnp.zeros((s, HEAD_DIM - half), F32)], axis=1)
    cc = jnp.concatenate([jnp.zeros((s, half), F32), sin, jnp.zeros((s, rest), F32)], axis=1)
    return tuple(jnp.tile(tb, (1, N_HEADS)) for tb in (ca, cb, cc))


QK_WIDE = 128
LANE_CQ, LANE_CK = 64, 67
KT_ROWS = 80


def _three_bf16(x):
    hi = x.astype(BF16).astype(F32)
    mid = (x - hi).astype(BF16).astype(F32)
    lo = (x - hi - mid).astype(BF16).astype(F32)
    return hi, mid, lo


def _heads_split(proj, col, tables, c, name):
    s = proj.shape[0]
    t = min(ROW_TILE, s)
    rope = tables is not None
    wide = c is not None
    width = QK_WIDE if wide else HEAD_DIM

    def body(*refs):
        x_ref = refs[0]
        q_ref, k_ref, v_ref, kt_ref, vt_ref = refs[-5:]
        xv = x_ref[...]
        parts = [xv[:, 0:GROUP], xv[:, GROUP:2 * GROUP], xv[:, 2 * GROUP:3 * GROUP]]
        if rope:
            ca, cb, cc = refs[1][...], refs[2][...], refs[3][...]
            for n in range(2):
                p = parts[n]
                parts[n] = p * ca + pltpu.roll(p, GROUP - 8, 1) * cb + pltpu.roll(p, 8, 1) * cc
        parts[0] = parts[0] * (HEAD_DIM ** -0.5)
        k_t, v_t = parts[1].T, parts[2].T
        ones_row = jnp.where(lax.broadcasted_iota(jnp.int32, (KT_ROWS - HEAD_DIM, t), 0) == 0, 1.0, 0.0)
        lane = lax.broadcasted_iota(jnp.int32, (t, QK_WIDE), 1)
        zeros = jnp.zeros((t, QK_WIDE - HEAD_DIM), F32)
        for h in range(N_HEADS):
            hs = slice(h * HEAD_DIM, (h + 1) * HEAD_DIM)
            qh, kh = parts[0][:, hs], parts[1][:, hs]
            if wide:
                terms = _three_bf16(refs[-6][:, h:h + 1])
                qh = jnp.concatenate([qh, zeros], axis=1)
                kh = jnp.concatenate([kh, zeros], axis=1)
                for n in range(3):
                    qh = jnp.where(lane == LANE_CQ + n, terms[n], jnp.where(lane == LANE_CK + n, 1.0, qh))
                    kh = jnp.where(lane == LANE_CK + n, -terms[n], jnp.where(lane == LANE_CQ + n, 1.0, kh))
            q_ref[h] = qh.astype(q_ref.dtype)
            k_ref[h] = kh.astype(k_ref.dtype)
            v_ref[h] = parts[2][:, hs].astype(v_ref.dtype)
            kt_ref[h] = jnp.concatenate([k_t[hs, :], ones_row], axis=0).astype(kt_ref.dtype)
            vt_ref[h] = v_t[hs, :].astype(vt_ref.dtype)

    tab = pl.BlockSpec((t, GROUP), lambda i: (i, 0))
    qk = pl.BlockSpec((N_HEADS, t, width), lambda i: (0, i, 0))
    heads = pl.BlockSpec((N_HEADS, t, HEAD_DIM), lambda i: (0, i, 0))
    heads_t = pl.BlockSpec((N_HEADS, HEAD_DIM, t), lambda i: (0, 0, i))
    qk_shape = jax.ShapeDtypeStruct((N_HEADS, s, width), BF16)
    return pl.pallas_call(
        body, name=name, grid=(s // t,),
        out_shape=(qk_shape, qk_shape, jax.ShapeDtypeStruct((N_HEADS, s, HEAD_DIM), BF16),
                   jax.ShapeDtypeStruct((N_HEADS, KT_ROWS, s), BF16),
                   jax.ShapeDtypeStruct((N_HEADS, HEAD_DIM, s), BF16)),
        in_specs=[pl.BlockSpec((t, 3 * GROUP), lambda i: (i, col))] + ([tab, tab, tab] if rope else [])
        + ([pl.BlockSpec((t, 128), lambda i: (i, 0))] if wide else []),
        out_specs=(qk, qk, heads, pl.BlockSpec((N_HEADS, KT_ROWS, t), lambda i: (0, 0, i)), heads_t),
        compiler_params=_params(),
    )(*((proj,) + (tuple(tables) if rope else ()) + ((c,) if wide else ())))


def _heads_merge(dqt, dk, dv, tables, name, dbuf, col):
    s = dv.shape[1]
    t = min(ROW_TILE, s)
    rope = tables is not None

    wide = dk.shape[2] == QK_WIDE

    def body(*refs):
        o_ref = refs[n_in + 1]
        dq = jnp.concatenate([refs[0][h, :HEAD_DIM, :] for h in range(N_HEADS)], axis=0).T
        parts = [dq] + [jnp.concatenate([r[h][:, :HEAD_DIM] for h in range(N_HEADS)], axis=1) for r in refs[1:3]]
        parts[0] = parts[0] * (HEAD_DIM ** -0.5)
        if rope:
            ca, cb, cc = refs[3][...], refs[4][...], refs[5][...]
            for n in range(2):
                p = parts[n]
                parts[n] = p * ca + pltpu.roll(p * cb, 8, 1) + pltpu.roll(p * cc, GROUP - 8, 1)
        o_ref[...] = jnp.concatenate(parts, axis=1).astype(o_ref.dtype)
        if wide:
            over_keys = jnp.concatenate([refs[0][h, HEAD_DIM:HEAD_DIM + 8, :] for h in range(N_HEADS)]
                                        + [jnp.zeros((128 - 8 * N_HEADS, t), F32)], axis=0).T
            lane = lax.broadcasted_iota(jnp.int32, (t, 128), 1)
            dc = jnp.zeros((t, 128), F32)
            for h in range(N_HEADS):
                dc = jnp.where(lane == h, over_keys[:, 8 * h:8 * h + 1] - refs[1][h][:, LANE_CK:LANE_CK + 1], dc)
            refs[n_in + 2][...] = dc

    tab = pl.BlockSpec((t, GROUP), lambda i: (i, 0))
    heads = pl.BlockSpec((N_HEADS, t, HEAD_DIM), lambda i: (0, i, 0))
    n_in = 6 if rope else 3
    dspec = pl.BlockSpec((t, 3 * GROUP), lambda i: (i, col))
    dshape = jax.ShapeDtypeStruct(dbuf.shape, dbuf.dtype)
    return pl.pallas_call(
        body, name=name, grid=(s // t,),
        out_shape=(dshape, jax.ShapeDtypeStruct((s, 128), F32)) if wide else dshape,
        in_specs=[pl.BlockSpec((N_HEADS, KT_ROWS, t), lambda i: (0, 0, i)),
                  pl.BlockSpec((N_HEADS, t, dk.shape[2]), lambda i: (0, i, 0)), heads]
        + ([tab, tab, tab] if rope else []) + [ANY_SPEC],
        out_specs=(dspec, pl.BlockSpec((t, 128), lambda i: (i, 0))) if wide else dspec,
        input_output_aliases={n_in: 0}, compiler_params=_params(),
    )(*((dqt, dk, dv) + (tuple(tables) if rope else ()) + (dbuf,)))


def _log_sigmoid(x):
    return jnp.minimum(x, 0.0) - jnp.log(1.0 + jnp.exp(-jnp.abs(x)))


def _scan_rows(x, reverse):
    n = x.shape[0]
    row = lax.broadcasted_iota(jnp.int32, x.shape, 0)
    k = 1
    while k < n:
        if reverse:
            x = x + jnp.where(row < n - k, _shift_up(x, k), 0.0)
        else:
            x = x + jnp.where(row >= k, _shift_down(x, k), 0.0)
        k *= 2
    return x


def _gate_cumsum(proj, bias):
    s = proj.shape[0]
    col = COL_GATE // 128

    def body(z_ref, b_ref, c_ref):
        c_ref[...] = _scan_rows(_log_sigmoid(z_ref[...] + b_ref[...]), False)

    return pl.pallas_call(
        body, name="gate_cumsum", grid=(1,), out_shape=jax.ShapeDtypeStruct((s, 128), F32),
        in_specs=[pl.BlockSpec((s, 128), lambda i: (0, col)), pl.BlockSpec((1, 128), lambda i: (0, 0))],
        out_specs=pl.BlockSpec((s, 128), lambda i: (0, 0)), compiler_params=_params(),
    )(proj, bias)


def _gate_cumsum_bwd(proj, bias, dc, dbuf):
    s = proj.shape[0]
    col = COL_GATE // 128

    def body(z_ref, b_ref, dc_ref, buf_ref, dz_ref, db_ref):
        dlogf = _scan_rows(dc_ref[...], True)
        dz = dlogf * _sigmoid(-(z_ref[...] + b_ref[...]))
        dz_ref[...] = dz.astype(dz_ref.dtype)
        db_ref[...] = jnp.sum(dz, axis=0, keepdims=True)

    return pl.pallas_call(
        body, name="gate_cumsum_bwd", grid=(1,),
        out_shape=(jax.ShapeDtypeStruct(dbuf.shape, dbuf.dtype), jax.ShapeDtypeStruct((1, 128), F32)),
        in_specs=[pl.BlockSpec((s, 128), lambda i: (0, col)), pl.BlockSpec((1, 128), lambda i: (0, 0)),
                  pl.BlockSpec((s, 128), lambda i: (0, 0)), ANY_SPEC],
        out_specs=(pl.BlockSpec((s, 128), lambda i: (0, col)), pl.BlockSpec((1, 128), lambda i: (0, 0))),
        input_output_aliases={3: 0}, compiler_params=_params(),
    )(proj, bias, dc, dbuf)


DIL_REACH = 2048


def _pair_weight(mode, d):
    if mode == "fox":
        return jnp.where(d >= 0, 1.0, 0.0)
    w1 = jnp.where(jnp.abs(d - 64) <= 64, 1.0, 0.0)
    w2 = jnp.where((d & 3) == 0, jnp.where(jnp.abs(d - 256) <= 256, 1.0, 0.0), 0.0)
    w3 = jnp.where((d & 15) == 0, jnp.where(jnp.abs(d - 1024) <= 1024, 1.0, 0.0), 0.0)
    return w1 + w2 + w3


def _bias_tables(mode, tq, tk):
    nb = 2 if mode == "fox" else DIL_REACH // tk + 1
    n = lax.broadcasted_iota(jnp.int32, (nb, tk, tq), 0)
    key = lax.broadcasted_iota(jnp.int32, (nb, tk, tq), 1)
    query = lax.broadcasted_iota(jnp.int32, (nb, tk, tq), 2)
    w = _pair_weight(mode, n * tk + query - key)
    return jnp.where(w > 0.0, jnp.log(jnp.maximum(w, 1.0)), NEG)


M_INIT = -1e29


def _first_key_chunk(mode, q0, tk):
    if mode == "fox":
        return 0
    return jnp.maximum(q0 - DIL_REACH, 0) // tk


def _attention_fwd(mode, q, k, vt, tab_t, ybuf, col):
    s, width = q.shape[1], q.shape[2]
    tq = min(ATT_TQ, s)
    tk = tq
    nb = tab_t.shape[0]

    def body(q_ref, k_ref, vt_ref, tab_ref, buf_ref, y_ref, o_ref, lse_ref):
        i = pl.program_id(0)
        lo = _first_key_chunk(mode, i * tq, tk)

        def step(c, carry):
            k0 = pl.multiple_of(c * tk, tk)
            tab = tab_ref[jnp.minimum(i - c, nb - 1)]
            scores = [lax.dot_general(k_ref[h, pl.ds(k0, tk), :], q_ref[h], (NT, ((), ())),
                                      preferred_element_type=F32) for h in range(N_HEADS)]
            stats, probs = [], []
            for h in range(N_HEADS):
                m, l = carry[3 * h:3 * h + 2]
                sc = scores[h] + tab
                m_new = jnp.maximum(m, jnp.max(sc, axis=0, keepdims=True))
                alpha = jnp.exp(m - m_new)
                p = jnp.exp(sc - m_new)
                stats.append((m_new, alpha * l + jnp.sum(p, axis=0, keepdims=True), alpha))
                probs.append(p.astype(BF16))
            pv = [jnp.dot(vt_ref[h, :, pl.ds(k0, tk)], probs[h], preferred_element_type=F32) for h in range(N_HEADS)]
            new = []
            for h in range(N_HEADS):
                m_new, l, alpha = stats[h]
                new += [m_new, l, alpha * carry[3 * h + 2] + pv[h]]
            return tuple(new)

        start = (jnp.full((1, tq), M_INIT, F32), jnp.zeros((1, tq), F32), jnp.zeros((HEAD_DIM, tq), F32))
        done = lax.fori_loop(lo, i + 1, step, start * N_HEADS)
        outs = []
        for h in range(N_HEADS):
            m, l, acc = done[3 * h:3 * h + 3]
            outs.append(acc / l)
            lse_ref[h] = m + jnp.log(l)
        out = jnp.concatenate(outs, axis=0).T
        y_ref[...] = out.astype(y_ref.dtype)
        o_ref[...] = out

    rowspec = pl.BlockSpec((N_HEADS, 1, tq), lambda i: (0, 0, i))
    return pl.pallas_call(
        body, name="attention_fwd_" + mode, grid=(s // tq,),
        out_shape=(jax.ShapeDtypeStruct(ybuf.shape, ybuf.dtype), jax.ShapeDtypeStruct((s, GROUP), F32),
                   jax.ShapeDtypeStruct((N_HEADS, 1, s), F32)),
        in_specs=[pl.BlockSpec((N_HEADS, tq, width), lambda i: (0, i, 0)),
                  pl.BlockSpec((N_HEADS, s, width), lambda i: (0, 0, 0)),
                  pl.BlockSpec((N_HEADS, HEAD_DIM, s), lambda i: (0, 0, 0)),
                  pl.BlockSpec((nb, tk, tq), lambda i: (0, 0, 0)), ANY_SPEC],
        out_specs=(pl.BlockSpec((tq, GROUP), lambda i: (i, col)), pl.BlockSpec((tq, GROUP), lambda i: (i, 0)),
                   rowspec),
        input_output_aliases={4: 0}, compiler_params=_params(),
    )(q, k, vt, tab_t, ybuf)


def _attention_delta(o, do, col):
    s = o.shape[0]
    t = min(ROW_TILE, s)

    def body(o_ref, do_ref, delta_ref, dob_ref):
        dov = do_ref[...]
        prod_t = (o_ref[...] * dov).T
        for h in range(N_HEADS):
            hs = slice(h * HEAD_DIM, (h + 1) * HEAD_DIM)
            delta_ref[h] = jnp.sum(prod_t[hs, :], axis=0, keepdims=True)
            dob_ref[h] = dov[:, hs].astype(dob_ref.dtype)

    return pl.pallas_call(
        body, name="attention_delta", grid=(s // t,),
        out_shape=(jax.ShapeDtypeStruct((N_HEADS, 1, s), F32), jax.ShapeDtypeStruct((N_HEADS, s, HEAD_DIM), BF16)),
        in_specs=[pl.BlockSpec((t, GROUP), lambda i: (i, 0)), pl.BlockSpec((t, GROUP), lambda i: (i, col))],
        out_specs=(pl.BlockSpec((N_HEADS, 1, t), lambda i: (0, 0, i)),
                   pl.BlockSpec((N_HEADS, t, HEAD_DIM), lambda i: (0, i, 0))),
        compiler_params=_params(),
    )(o, do)


def _attention_bwd(mode, q, k, v, kt, tab_t, dob, lse, delta):
    s, width = q.shape[1], q.shape[2]
    tq = min(ATT_TQ, s)
    tk = tq
    nq = s // tq
    nb = tab_t.shape[0]

    def body(q_ref, k_ref, v_ref, kt_ref, tab_ref, dob_ref, lse_ref, delta_ref, dqt_ref, dk_ref, dv_ref):
        i = pl.program_id(0)

        @pl.when(i == 0)
        def _():
            dqt_ref[...] = jnp.zeros_like(dqt_ref)

        hi = nq if mode == "fox" else jnp.minimum((i * tk + tk - 1 + DIL_REACH) // tq + 1, nq)
        for h0 in range(0, N_HEADS, BWD_HEADS):
            heads = range(h0, h0 + BWD_HEADS)

            def step(c, carry, heads=heads):
                q0 = pl.multiple_of(c * tq, tq)
                qs = pl.ds(q0, tq)
                tab = tab_ref[jnp.minimum(c - i, nb - 1)]
                qv = [q_ref[h, qs, :] for h in heads]
                dov = [dob_ref[h, qs, :] for h in heads]
                sc = [lax.dot_general(k_ref[h], qv[n], (NT, ((), ())), preferred_element_type=F32)
                      for n, h in enumerate(heads)]
                dp = [lax.dot_general(v_ref[h], dov[n], (NT, ((), ())), preferred_element_type=F32)
                      for n, h in enumerate(heads)]
                pb, dsb = [], []
                for n, h in enumerate(heads):
                    p = jnp.exp(sc[n] + tab - lse_ref[h, :, qs])
                    pb.append(p.astype(BF16))
                    dsb.append((p * (dp[n] - delta_ref[h, :, qs])).astype(BF16))
                new = []
                for n, h in enumerate(heads):
                    new += [carry[2 * n] + jnp.dot(dsb[n], qv[n], preferred_element_type=F32),
                            carry[2 * n + 1] + jnp.dot(pb[n], dov[n], preferred_element_type=F32)]
                for n, h in enumerate(heads):
                    dqt_ref[h, :, qs] += jnp.dot(kt_ref[h], dsb[n], preferred_element_type=F32)
                return tuple(new)

            start = (jnp.zeros((tk, width), F32), jnp.zeros((tk, HEAD_DIM), F32))
            done = lax.fori_loop(i, hi, step, start * BWD_HEADS)
            for n, h in enumerate(heads):
                dk_ref[h] = done[2 * n]
                dv_ref[h] = done[2 * n + 1]

    def full(shape):
        return pl.BlockSpec(shape, lambda i: (0, 0, 0))

    kblk = pl.BlockSpec((N_HEADS, tk, width), lambda i: (0, i, 0))
    vblk = pl.BlockSpec((N_HEADS, tk, HEAD_DIM), lambda i: (0, i, 0))
    return pl.pallas_call(
        body, name="attention_bwd_" + mode, grid=(s // tk,),
        out_shape=(jax.ShapeDtypeStruct((N_HEADS, KT_ROWS, s), F32), jax.ShapeDtypeStruct((N_HEADS, s, width), F32),
                   jax.ShapeDtypeStruct((N_HEADS, s, HEAD_DIM), F32)),
        in_specs=[full((N_HEADS, s, width)), kblk, vblk, pl.BlockSpec((N_HEADS, KT_ROWS, tk), lambda i: (0, 0, i)),
                  full((nb, tk, tq)), full((N_HEADS, s, HEAD_DIM)), full((N_HEADS, 1, s)), full((N_HEADS, 1, s))],
        out_specs=(full((N_HEADS, KT_ROWS, s)), kblk, vblk),
        compiler_params=_params(),
    )(q, k, v, kt, tab_t, dob, lse, delta)


def _xattn_fwd(qx, kvm):
    s = qx.shape[0]
    t = min(ROW_TILE, s)

    def body(q_ref, kv_ref, o_ref):
        heads = range(XA_HEADS)
        sc = [lax.dot_general(q_ref[:, h * XA_DIM:(h + 1) * XA_DIM].astype(BF16), kv_ref[h].astype(BF16),
                              (NT, ((), ())), preferred_element_type=F32) * (XA_DIM ** -0.5) for h in heads]
        probs = []
        for h in heads:
            e = jnp.exp(sc[h] - jnp.max(sc[h], axis=-1, keepdims=True))
            probs.append((e / jnp.sum(e, axis=-1, keepdims=True)).astype(BF16))
        outs = [jnp.dot(probs[h], kv_ref[XA_HEADS + h].astype(BF16), preferred_element_type=F32) for h in heads]
        for h in heads:
            o_ref[:, h * XA_DIM:(h + 1) * XA_DIM] = outs[h].astype(o_ref.dtype)

    return pl.pallas_call(
        body, name="xattn_fwd", grid=(s // t,), out_shape=jax.ShapeDtypeStruct((s, D_MODEL), BF16),
        in_specs=[pl.BlockSpec((t, D_MODEL), lambda i: (i, 0)),
                  pl.BlockSpec((2 * XA_HEADS, MEM_LEN, XA_DIM), lambda i: (0, 0, 0))],
        out_specs=pl.BlockSpec((t, D_MODEL), lambda i: (i, 0)), compiler_params=_params(),
    )(qx, kvm)


def _xattn_bwd(qx, kvm, do):
    s = qx.shape[0]
    t = min(ROW_TILE, s)

    def body(q_ref, kv_ref, do_ref, dq_ref, dkv_ref):
        i = pl.program_id(0)
        heads = range(XA_HEADS)
        qv = [q_ref[:, h * XA_DIM:(h + 1) * XA_DIM].astype(BF16) for h in heads]
        dov = [do_ref[:, h * XA_DIM:(h + 1) * XA_DIM].astype(BF16) for h in heads]
        kv = [kv_ref[h].astype(BF16) for h in heads]
        sc = [lax.dot_general(qv[h], kv[h], (NT, ((), ())), preferred_element_type=F32) * (XA_DIM ** -0.5)
              for h in heads]
        dp = [lax.dot_general(dov[h], kv_ref[XA_HEADS + h].astype(BF16), (NT, ((), ())), preferred_element_type=F32)
              for h in heads]
        pb, ds = [], []
        for h in heads:
            e = jnp.exp(sc[h] - jnp.max(sc[h], axis=-1, keepdims=True))
            p = e / jnp.sum(e, axis=-1, keepdims=True)
            pb.append(p.astype(BF16))
            ds.append((p * (dp[h] - jnp.sum(p * dp[h], axis=-1, keepdims=True)) * (XA_DIM ** -0.5)).astype(BF16))
        dq = [jnp.dot(ds[h], kv[h], preferred_element_type=F32) for h in heads]
        dk = [lax.dot_general(ds[h], qv[h], (TN, ((), ())), preferred_element_type=F32) for h in heads]
        dv = [lax.dot_general(pb[h], dov[h], (TN, ((), ())), preferred_element_type=F32) for h in heads]
        for h in heads:
            dq_ref[:, h * XA_DIM:(h + 1) * XA_DIM] = dq[h].astype(dq_ref.dtype)

        @pl.when(i == 0)
        def _():
            for h in heads:
                dkv_ref[h] = dk[h]
                dkv_ref[XA_HEADS + h] = dv[h]

        @pl.when(i > 0)
        def _():
            for h in heads:
                dkv_ref[h] += dk[h]
                dkv_ref[XA_HEADS + h] += dv[h]

    row = pl.BlockSpec((t, D_MODEL), lambda i: (i, 0))
    kvs = pl.BlockSpec((2 * XA_HEADS, MEM_LEN, XA_DIM), lambda i: (0, 0, 0))
    return pl.pallas_call(
        body, name="xattn_bwd", grid=(s // t,),
        out_shape=(jax.ShapeDtypeStruct((s, D_MODEL), BF16),
                   jax.ShapeDtypeStruct((2 * XA_HEADS, MEM_LEN, XA_DIM), F32)),
        in_specs=[row, kvs, row], out_specs=(row, kvs), compiler_params=_params(),
    )(qx, kvm, do)


def _adamw(parts, owns, me, w, m, v, name):
    nl, r, c = w.shape
    tr = r
    for cand in (256, 128, 64, 32, 16, 8):
        if r % cand == 0 and r > cand and N_DEV * cand * c * 4 <= ADAMW_BLOCK_BYTES:
            tr = cand
            break
    nt = r // tr
    per_layer = N_DEV + (1 if owns is not None else 0)

    def body(me_ref, *refs):
        w_ref, m_ref, v_ref, g_ref, d_ref, nm_ref, nv_ref = refs[nl * per_layer:]
        layer = pl.program_id(0)
        g = None
        for l in range(nl):
            p_refs = refs[l * per_layer:(l + 1) * per_layer]
            gl = None
            for d in range(N_DEV):
                term = p_refs[d][...].astype(F32)
                if owns is not None:
                    term = jnp.where(me_ref[0] == d, p_refs[N_DEV][...].astype(F32), term)
                gl = term if gl is None else gl + term
            g = gl if g is None else jnp.where(layer == l, gl, g)
        mn = ADAM_B1 * m_ref[...] + (1.0 - ADAM_B1) * g
        vn = ADAM_B2 * v_ref[...] + (1.0 - ADAM_B2) * (g * g)
        m_hat = mn / (1.0 - ADAM_B1 ** ADAM_STEP)
        v_hat = vn / (1.0 - ADAM_B2 ** ADAM_STEP)
        g_ref[...] = g
        d_ref[...] = -ADAM_LR * (m_hat / (jnp.sqrt(v_hat) + ADAM_EPS) + ADAM_WD * w_ref[...])
        nm_ref[...] = mn
        nv_ref[...] = vn

    def rows(l, ll, i):
        return jnp.where(ll == l, i, jnp.where(ll < l, 0, nt - 1))

    def part_spec(l, d):
        if owns is None:
            return pl.BlockSpec((None, tr, c), lambda ll, i, me_ref: (d, rows(l, ll, i), 0))
        return pl.BlockSpec((None, tr, c),
                            lambda ll, i, me_ref: (jnp.where(me_ref[0] == d, (d + 1) % N_DEV, d), rows(l, ll, i), 0))

    def own_spec(l):
        return pl.BlockSpec((None, tr, c), lambda ll, i, me_ref: (me_ref[0], rows(l, ll, i), 0))

    in_specs, operands = [], []
    for l in range(nl):
        in_specs += [part_spec(l, d) for d in range(N_DEV)]
        operands += [parts[l]] * N_DEV
        if owns is not None:
            in_specs.append(own_spec(l))
            operands.append(owns[l])
    blk = pl.BlockSpec((None, tr, c), lambda ll, i, me_ref: (ll, i, 0))
    shp = jax.ShapeDtypeStruct((nl, r, c), F32)
    return pl.pallas_call(
        body, name=name, out_shape=(shp, shp, shp, shp),
        grid_spec=pltpu.PrefetchScalarGridSpec(
            num_scalar_prefetch=1, grid=(nl, nt), in_specs=in_specs + [blk, blk, blk],
            out_specs=(blk, blk, blk, blk)),
        compiler_params=_params(),
    )(me.reshape(1), *operands, w, m, v)


GROUPS = {"in": ("w_in",), "rest": ("w_out", "w_xq", "w_xo", "w_xkv", "w_up", "w_down")}
FULL_SHAPES = {"w_in": (D_MODEL, N_IN_PAD), "w_out": (D_MODEL, D_MODEL), "w_xq": (D_MODEL, D_MODEL),
               "w_xo": (D_MODEL, D_MODEL), "w_xkv": (N_DEV, D_MODEL, 2 * D_MODEL // N_DEV),
               "w_up": (N_DEV, D_MODEL, FF_SHARD), "w_down": (FF_HALF, FF_SHARD, D_MODEL)}
PIECE_SHAPES = {"w_in": (N_DEV, D_MODEL // N_DEV, N_IN_PAD), "w_out": (N_DEV, D_MODEL // N_DEV, D_MODEL),
                "w_xq": (N_DEV, D_MODEL // N_DEV, D_MODEL), "w_xo": (N_DEV, D_MODEL // N_DEV, D_MODEL),
                "w_xkv": (N_DEV, D_MODEL, 2 * D_MODEL // N_DEV), "w_up": (N_DEV, D_MODEL, FF_SHARD),
                "w_down": (N_DEV, D_FF // N_DEV, D_MODEL)}
CONV_WORDS = 8192


class _GatheredWeights:
    def __init__(self, states, layer):
        self.states, self.layer, self.full, self.extra = dict(states), layer, {}, None

    def need(self, group, after):
        if group in self.states:
            got, _ = _exchange_wait(self.states.pop(group), after, "gather_%s_wait_%d" % (group, self.layer))
            for name, g in zip(GROUPS[group], got):
                self.full[name] = g.reshape(FULL_SHAPES[name])
            self.extra = got[len(GROUPS[group]):]

    def __getitem__(self, name):
        return self.full[name]


def _relay_in_cols(w):
    pad = jnp.zeros(w.shape[:-1] + (N_IN_PAD - N_IN,), w.dtype)
    return jnp.concatenate([w[..., :2304], w[..., 2308:N_IN], w[..., 2304:2308], pad], axis=-1)


def _unrelay_in_cols(w):
    return jnp.concatenate([w[..., :2304], w[..., COL_GATE:COL_GATE + 4], w[..., 2304:COL_GATE]], axis=-1)


def _layer_fwd(h, memv, w, sm, tables):
    sv = {"h0": h}
    s = h.shape[0]
    tm, tb = min(ROW_TILE, s), min(MM_TILE, s)
    w.need("in", h)
    tn = N_IN_PAD // 3
    proj, xn = _norm_matmul(h, sm["g_mix"], w["w_in"], (s, N_IN_PAD), grid=(s // tb, 3),
                            b_spec=pl.BlockSpec((D_MODEL, tn), lambda i, j: (0, j)),
                            o_spec=pl.BlockSpec((tb, tn), lambda i, j: (i, j)), name="norm_mm_in")
    sv["xn"], sv["proj"] = xn, proj
    ycat = _sconv_fwd(proj, sm["w_sconv"])
    qd, kd, vd, ktd, vtd = _heads_split(proj, 1, tables["rope"], None, "split_dil")
    ycat, ob, lse_b = _attention_fwd("dil", qd, kd, vtd, tables["dil"], ycat, 1)
    sv["dil"] = (qd, kd, vd, ktd, ob, lse_b)
    c = _gate_cumsum(proj, sm["b_forget_pad"])
    qf, kf, vf, ktf, vtf = _heads_split(proj, 2, None, c, "split_fox")
    ycat, oc, lse_c = _attention_fwd("fox", qf, kf, vtf, tables["fox"], ycat, 2)
    sv["fox"] = (qf, kf, vf, ktf, oc, lse_c)
    ycat = _pool_fwd(proj, sm["w_pool_bd"], sm["pool_scale"], ycat)
    sv["ycat"] = ycat
    w.need("rest", ycat)
    h1 = _mm_nn(ycat, w["w_out"], "mm_out", res=h)
    sv["h1"] = h1
    memn = _rms_fwd(memv, sm["g_mem"], "rms_mem")
    qx, xq = _norm_matmul(h1, sm["g_xa"], w["w_xq"], (s, D_MODEL), grid=(s // tb, 1),
                          b_spec=pl.BlockSpec((D_MODEL, D_MODEL), lambda i, j: (0, 0)),
                          o_spec=pl.BlockSpec((tb, D_MODEL), lambda i, j: (i, 0)), name="norm_mm_xq",
                          out_dtype=BF16)
    kvm = _matmul(memn, w["w_xkv"], (N_DEV, MEM_LEN, XA_DIM), grid=(N_DEV, 1, 1),
                  a_spec=pl.BlockSpec((MEM_LEN, D_MODEL), lambda i, j, r: (0, 0)),
                  b_spec=pl.BlockSpec((None, D_MODEL, XA_DIM), lambda i, j, r: (i, 0, 0)),
                  o_spec=pl.BlockSpec((None, MEM_LEN, XA_DIM), lambda i, j, r: (i, 0, 0)),
                  dims=NN, nred=1, name="mm_xkv")
    ox = _xattn_fwd(qx, kvm)
    sv.update(xq=xq, memn=memn, qx=qx, kvm=kvm, ox=ox)
    h2 = _mm_nn(ox, w["w_xo"], "mm_xo", res=h1)
    sv["h2"] = h2
    u0, xf = _norm_matmul(h2, sm["g_ffn"], w["w_up"], (N_DEV, s, FF_SHARD), grid=(s // tb, N_DEV),
                          b_spec=pl.BlockSpec((None, D_MODEL, FF_SHARD), lambda i, j: (j, 0, 0)),
                          o_spec=pl.BlockSpec((None, tb, FF_SHARD), lambda i, j: (j, i, 0)), name="norm_mm_up",
                          out_dtype=BF16)
    act = _ffn_gate_fwd(u0, sm["w_ffconv"])
    sv.update(xf=xf, u0=u0, act=act)
    ospec = pl.BlockSpec((tm, D_MODEL), lambda i, j, r: (i, 0))
    h3 = _matmul(act, w["w_down"], (s, D_MODEL), grid=(s // tm, 1, 1),
                 a_spec=pl.BlockSpec((FF_HALF, tm, FF_SHARD), lambda i, j, r: (0, i, 0)),
                 b_spec=pl.BlockSpec((FF_HALF, FF_SHARD, D_MODEL), lambda i, j, r: (0, 0, 0)),
                 o_spec=ospec, dims=NN, nred=1, slabs=FF_HALF, name="mm_down", res=h2, res_spec=ospec)
    return h3, sv


def _layer_bwd(dh3, memv, w, sm, tables, sv, rest_ready):
    s = dh3.shape[0]
    tm, tb = min(ROW_TILE, s), min(MM_TILE, s)
    big, small = {}, {}
    ts = max(s // 2, 1)
    dact = _matmul(dh3, w["w_down"], (FF_HALF, s, FF_SHARD), grid=(s // tb, FF_HALF, 1),
                   a_spec=pl.BlockSpec((tb, D_MODEL), lambda i, j, r: (i, 0)),
                   b_spec=pl.BlockSpec((None, FF_SHARD, D_MODEL), lambda i, j, r: (j, 0, 0)),
                   o_spec=pl.BlockSpec((None, tb, FF_SHARD), lambda i, j, r: (j, i, 0)),
                   dims=NT, nred=1, name="mm_dact", out_dtype=BF16)
    big["w_down"] = _matmul(sv["act"], dh3, (FF_HALF, FF_SHARD, D_MODEL), grid=(FF_HALF, 1, s // ts),
                            a_spec=pl.BlockSpec((None, ts, FF_SHARD), lambda i, j, r: (i, r, 0)),
                            b_spec=pl.BlockSpec((ts, D_MODEL), lambda i, j, r: (r, 0)),
                            o_spec=pl.BlockSpec((None, FF_SHARD, D_MODEL), lambda i, j, r: (i, 0, 0)),
                            dims=TN, nred=s // ts, name="mm_dw_down", out_dtype=GRAD_DTYPE)
    du0, small["w_ffconv"] = _ffn_gate_bwd(sv["u0"], sm["w_ffconv"], dact)
    dh2, small["g_ffn"] = _matmul_rms_bwd(du0, w["w_up"], sv["h2"], sm["g_ffn"], dh3, "mm_dxf_rms_bwd",
                                          tm=ROW_TILE // 2)
    big["w_up"] = _matmul(sv["xf"], du0, (N_DEV, D_MODEL, FF_SHARD), grid=(N_DEV, 1, 1),
                          a_spec=pl.BlockSpec((s, D_MODEL), lambda i, j, r: (0, 0)),
                          b_spec=pl.BlockSpec((None, s, FF_SHARD), lambda i, j, r: (i, 0, 0)),
                          o_spec=pl.BlockSpec((None, D_MODEL, FF_SHARD), lambda i, j, r: (i, 0, 0)),
                          dims=TN, nred=1, name="mm_dw_up", out_dtype=GRAD_DTYPE)
    dox = _mm_nt(dh2, w["w_xo"], "mm_dox", out_dtype=BF16)
    big["w_xo"] = _mm_tn(sv["ox"], dh2, "mm_dw_xo")
    dqx, dkvm = _xattn_bwd(sv["qx"], sv["kvm"], dox)
    big["w_xq"] = _mm_tn(sv["xq"], dqx, "mm_dw_xq")
    big["w_xkv"] = _matmul(sv["memn"], dkvm, (N_DEV, D_MODEL, XA_DIM), grid=(N_DEV, 1, 1),
                           a_spec=pl.BlockSpec((MEM_LEN, D_MODEL), lambda i, j, r: (0, 0)),
                           b_spec=pl.BlockSpec((None, MEM_LEN, XA_DIM), lambda i, j, r: (i, 0, 0)),
                           o_spec=pl.BlockSpec((None, D_MODEL, XA_DIM), lambda i, j, r: (i, 0, 0)),
                           dims=TN, nred=1, name="mm_dw_xkv", out_dtype=GRAD_DTYPE)
    dmemn = _matmul(dkvm, w["w_xkv"], (MEM_LEN, D_MODEL), grid=(1, 1, 1),
                    a_spec=pl.BlockSpec((N_DEV, MEM_LEN, XA_DIM), lambda i, j, r: (0, 0, 0)),
                    b_spec=pl.BlockSpec((N_DEV, D_MODEL, XA_DIM), lambda i, j, r: (0, 0, 0)),
                    o_spec=pl.BlockSpec((MEM_LEN, D_MODEL), lambda i, j, r: (0, 0)),
                    dims=NT, nred=1, slabs=N_DEV, name="mm_dmemn")
    _, small["g_mem"] = _rms_bwd(dmemn, memv, sm["g_mem"], None, "rms_mem_bwd")
    dh1, small["g_xa"] = _matmul_rms_bwd(dqx, w["w_xq"], sv["h1"], sm["g_xa"], dh2, "mm_dxq_rms_bwd")
    dycat = _mm_nt(dh1, w["w_out"], "mm_dycat")
    big["w_out"] = _mm_tn(sv["ycat"], dh1, "mm_dw_out")
    proj = sv["proj"]
    dproj, small["w_sconv"] = _sconv_bwd(proj, sm["w_sconv"] + rest_ready(big, small), dycat)
    qd, kd, vd, ktd, ob, lse_b = sv["dil"]
    delta, dob = _attention_delta(ob, dycat, 1)
    dqt, dk, dv = _attention_bwd("dil", qd, kd, vd, ktd, tables["dil"], dob, lse_b, delta)
    dproj = _heads_merge(dqt, dk, dv, tables["rope"], "merge_dil", dproj, 1)
    qf, kf, vf, ktf, oc, lse_c = sv["fox"]
    delta, dob = _attention_delta(oc, dycat, 2)
    dqt, dk, dv = _attention_bwd("fox", qf, kf, vf, ktf, tables["fox"], dob, lse_c, delta)
    dproj, dc = _heads_merge(dqt, dk, dv, None, "merge_fox", dproj, 2)
    dproj, dbias = _gate_cumsum_bwd(proj, sm["b_forget_pad"], dc, dproj)
    small["b_forget"] = dbias[0, :N_HEADS]
    dproj, dwbd, small["pool_scale"] = _pool_bwd(proj, sm["w_pool_bd"], sm["pool_scale"], dycat, dproj)
    small["w_pool"] = jnp.stack([dwbd[64 * g:64 * (g + 1), 64 * g:64 * (g + 1)] for g in range(4)])
    big["w_in"] = _mm_tn(sv["xn"], dproj, "mm_dw_in", tn=896)
    dh0, small["g_mix"] = _matmul_rms_bwd(dproj, w["w_in"], sv["h0"], sm["g_mix"], dh1, "mm_dxn_rms_bwd")
    return dh0, big, small


SMALL_NAMES = ("g_mix", "b_forget", "w_pool", "pool_scale", "g_xa", "g_mem", "g_ffn", "w_sconv", "w_ffconv")
SMALL_WITH = {"rest": ("w_ffconv", "g_ffn", "g_mem", "g_xa"),
              "in": ("w_sconv", "b_forget", "pool_scale", "w_pool", "g_mix")}
SMALL_SHAPES = {"w_sconv": (3, GROUP), "w_ffconv": (N_DEV, 3, FF_SHARD)}
WEIGHT_NAMES = ("g_mix", "w_in", "b_forget", "w_sconv", "w_pool", "pool_scale", "w_out", "g_xa", "g_mem", "w_xq",
                "w_xkv", "w_xo", "g_ffn", "w_up", "w_ffconv", "w_down", "g_final")


def _block_diag(w_pool):
    z = jnp.zeros((64, 64), F32)
    return jnp.concatenate(
        [jnp.concatenate([w_pool[g] if c == g else z for c in range(4)], axis=1) for g in range(4)], axis=0)


def kernel(x, mem, positions, g_mix, w_in, b_forget, w_sconv, w_pool, pool_scale, w_out, g_xa, g_mem, w_xq, w_xkv, w_xo, g_ffn, w_up, w_ffconv, w_down, g_final, loss_target, m_g_mix, m_w_in, m_b_forget, m_w_sconv, m_w_pool, m_pool_scale, m_w_out, m_g_xa, m_g_mem, m_w_xq, m_w_xkv, m_w_xo, m_g_ffn, m_w_up, m_w_ffconv, m_w_down, m_g_final, v_g_mix, v_w_in, v_b_forget, v_w_sconv, v_w_pool, v_pool_scale, v_w_out, v_g_xa, v_g_mem, v_w_xq, v_w_xkv, v_w_xo, v_g_ffn, v_w_up, v_w_ffconv, v_w_down, v_g_final):
    weights = dict(g_mix=g_mix, w_in=w_in, b_forget=b_forget, w_sconv=w_sconv, w_pool=w_pool, pool_scale=pool_scale,
                   w_out=w_out, g_xa=g_xa, g_mem=g_mem, w_xq=w_xq, w_xkv=w_xkv, w_xo=w_xo, g_ffn=g_ffn, w_up=w_up,
                   w_ffconv=w_ffconv, w_down=w_down, g_final=g_final)
    m_in = dict(g_mix=m_g_mix, w_in=m_w_in, b_forget=m_b_forget, w_sconv=m_w_sconv, w_pool=m_w_pool,
                pool_scale=m_pool_scale, w_out=m_w_out, g_xa=m_g_xa, g_mem=m_g_mem, w_xq=m_w_xq, w_xkv=m_w_xkv,
                w_xo=m_w_xo, g_ffn=m_g_ffn, w_up=m_w_up, w_ffconv=m_w_ffconv, w_down=m_w_down, g_final=m_g_final)
    v_in = dict(g_mix=v_g_mix, w_in=v_w_in, b_forget=v_b_forget, w_sconv=v_w_sconv, w_pool=v_w_pool,
                pool_scale=v_pool_scale, w_out=v_w_out, g_xa=v_g_xa, g_mem=v_g_mem, w_xq=v_w_xq, w_xkv=v_w_xkv,
                w_xo=v_w_xo, g_ffn=v_g_ffn, w_up=v_w_up, w_ffconv=v_w_ffconv, w_down=v_w_down, g_final=v_g_final)
    depth = w_in.shape[0]
    me = 4 * lax.axis_index("x") + 2 * lax.axis_index("y") + lax.axis_index("c")
    h = x[0]
    memv = mem[0]
    s = h.shape[0]
    tq = min(ATT_TQ, s)
    tables = {"rope": _rope_tables(positions[0]), "dil": _bias_tables("dil", tq, tq),
              "fox": _bias_tables("fox", tq, tq)}

    w_in_r = _relay_in_cols(w_in)
    conv_shard = jnp.concatenate([w_sconv.reshape(-1), w_ffconv.reshape(-1)])
    conv_shard = jnp.concatenate([conv_shard, jnp.zeros((CONV_WORDS - conv_shard.shape[0],), F32)])
    conv_bits = lax.bitcast_convert_type(conv_shard, BF16).reshape(2 * CONV_WORDS // 1024, 1024)
    gathered = []
    order = jnp.zeros((), F32)
    for l in range(depth):
        shards = dict(w_in=w_in_r[l], w_out=w_out[l], w_xq=w_xq[l], w_xo=w_xo[l], w_xkv=w_xkv[l], w_up=w_up[l],
                      w_down=w_down[l])
        states = {}
        for group in ("in", "rest"):
            shards[GROUPS[group][0]] = shards[GROUPS[group][0]] + order
            xs = [_place_shard(shards[name], me, BF16, "place_%s_%d" % (name, l)) for name in GROUPS[group]]
            if l == 0 and group == "in":
                xs.append(_place_shard(conv_bits, me, BF16, "place_conv"))
            states[group], token = _exchange_start(xs, False, "gather_%s_start_%d" % (group, l))
            order = order + token[0, 0]
        gathered.append(_GatheredWeights(states, l))
    gathered[0].need("in", tables["rope"][0])
    conv_all = lax.bitcast_convert_type(gathered[0].extra[0].reshape(N_DEV, CONV_WORDS, 2), F32)
    n_sc = depth * 3 * (GROUP // N_DEV)
    sconv_full = conv_all[:, :n_sc].reshape(N_DEV, depth, 3, GROUP // N_DEV).transpose(1, 2, 0, 3).reshape(
        depth, 3, GROUP)
    ffconv_full = conv_all[:, n_sc:n_sc + depth * 3 * FF_SHARD].reshape(N_DEV, depth, 3, FF_SHARD).transpose(
        1, 0, 2, 3)

    smalls = []
    for l in range(depth):
        smalls.append(dict(
            g_mix=g_mix[l], g_xa=g_xa[l], g_mem=g_mem[l], g_ffn=g_ffn[l], pool_scale=pool_scale[l],
            w_pool_bd=_block_diag(w_pool[l]), w_sconv=sconv_full[l], w_ffconv=ffconv_full[l],
            b_forget_pad=jnp.concatenate([b_forget[l], jnp.zeros((128 - N_HEADS,), F32)]).reshape(1, 128)))
    smalls[0]["g_mix"] = smalls[0]["g_mix"] + order

    saved = []
    for l in range(depth):
        h, sv = _layer_fwd(h, memv, gathered[l], smalls[l], tables)
        saved.append(sv)
    loss_part, dh, dg_final = _loss_head(h, g_final, loss_target[0])
    loss = lax.psum(loss_part[0, 0], MESH_AXES)

    small_grads = [None] * depth
    scatters = {}

    def pieces_of(big, group):
        return [big[name].reshape(PIECE_SHAPES[name]) for name in GROUPS[group]]

    def rider(grads):
        flat = jnp.concatenate([g.reshape(-1) for g in grads])
        rows = -(-flat.shape[0] // 1024)
        flat = jnp.concatenate([flat, jnp.zeros((rows * 1024 - flat.shape[0],), F32)])
        return jnp.broadcast_to(flat.reshape(1, rows, 1024), (N_DEV, rows, 1024))

    for l in reversed(range(depth)):
        def rest_ready(big, small, l=l):
            ready = [small[n] for n in SMALL_WITH["rest"]] + ([dg_final] if l == depth - 1 else [])
            scatters[l, "rest"], token = _exchange_start(pieces_of(big, "rest") + [rider(ready)], True,
                                                         "scatter_rest_start_%d" % l)
            return token[0, 0]

        dh, big, small_grads[l] = _layer_bwd(dh, memv, gathered[l], smalls[l], tables, saved[l], rest_ready)
        xs = pieces_of(big, "in") + [rider([small_grads[l][n] for n in SMALL_WITH["in"]])]
        scatters[l, "in"], token = _exchange_start(xs, True, "scatter_in_start_%d" % l)
        if l > 0:
            smalls[l - 1]["w_ffconv"] = smalls[l - 1]["w_ffconv"] + token[0, 0]
    grad_x = dh[None]

    parts, owns, small_parts = {}, {}, {}

    def wait_group(group, after):
        for l in reversed(range(depth)):
            got, given = _exchange_wait(scatters[l, group], after, "scatter_%s_wait_%d" % (group, l))
            for name, g, x in zip(GROUPS[group], got, given):
                parts.setdefault(name, [None] * depth)[l] = g
                owns.setdefault(name, [None] * depth)[l] = x
            flat = lax.dynamic_update_slice_in_dim(got[-1], given[-1][:1], me, axis=0).reshape(N_DEV, -1)
            off = 0
            for name in SMALL_WITH[group] + (("g_final",) if group == "rest" and l == depth - 1 else ()):
                shape = SMALL_SHAPES.get(name, weights[name].shape[-1:] if name == "g_final"
                                         else weights[name].shape[1:])
                n = 1
                for dim in shape:
                    n *= dim
                small_parts.setdefault(name, [None] * depth)[l] = flat[:, off:off + n].reshape((N_DEV,) + shape)
                off += n

    results = {}

    def update(name, w3, m3, v3):
        outs = _adamw(parts[name], owns.get(name), me, w3, m3, v3, "adamw_" + name)
        results[name] = [o.reshape(weights[name].shape) for o in outs]

    wait_group("rest", grad_x)
    for name in GROUPS["rest"]:
        update(name, weights[name], m_in[name], v_in[name])
    wait_group("in", results["w_down"][1])
    outs = _adamw(parts["w_in"], owns["w_in"], me, w_in_r, _relay_in_cols(m_w_in), _relay_in_cols(v_w_in),
                  "adamw_w_in")
    results["w_in"] = [_unrelay_in_cols(o) for o in outs]
    for name in SMALL_NAMES + ("g_final",):
        wv = weights[name]
        p = small_parts[name][depth - 1] if name == "g_final" else jnp.stack(small_parts[name], axis=1)
        if name == "w_sconv":
            p = lax.dynamic_slice_in_dim(p, me * (GROUP // N_DEV), GROUP // N_DEV, axis=3)
        elif name == "w_ffconv":
            p = lax.dynamic_index_in_dim(p, me, axis=2, keepdims=False)
        shape3 = (1, 1, wv.shape[0]) if wv.ndim == 1 else (1, -1, wv.shape[-1])
        w3 = wv.reshape(shape3)
        parts[name] = [p.reshape((N_DEV,) + w3.shape[1:])]
        update(name, w3, m_in[name].reshape(shape3), v_in[name].reshape(shape3))

    return (loss, grad_x, *[results[n][0] for n in WEIGHT_NAMES], *[results[n][1] for n in WEIGHT_NAMES],
            *[results[n][2] for n in WEIGHT_NAMES], *[results[n][3] for n in WEIGHT_NAMES])
```

```python
import functools

import jax
import jax.numpy as jnp
from jax import lax
from jax.experimental import pallas as pl
from jax.experimental.pallas import tpu as pltpu

F32 = jnp.float32
BF16 = jnp.bfloat16

N_DEV = 8
D_MODEL = 1024
GROUP = 256
HEAD_DIM = 64
N_HEADS = 4
N_IN = 2564
N_IN_PAD = 2688
COL_GATE = 2560
XA_HEADS = 4
XA_DIM = 256
MEM_LEN = 256
D_FF = 2816
FF_SHARD = 704
FF_HALF = 4
ROPE_THETA = 500000.0
ROPE_DIM = 16
RMS_EPS = 1e-6
NEG = -1e30
POOL_WINDOWS = (2, 4, 8, 16)
ADAM_LR, ADAM_B1, ADAM_B2, ADAM_EPS, ADAM_WD, ADAM_STEP = 0.001, 0.9, 0.999, 1e-08, 0.01, 10

ROW_TILE = 512
MM_TILE = 1024
ATT_TQ = 256
BWD_HEADS = 4
VMEM_LIMIT = 56 * 1024 * 1024
ADAMW_BLOCK_BYTES = 4 * 1024 * 1024
PLACE_BLOCK_BYTES = 4 * 1024 * 1024

MESH_AXES = ("x", "y", "c")


def _params(**kw):
    return pltpu.CompilerParams(vmem_limit_bytes=VMEM_LIMIT, **kw)


HBM_SPEC = pl.BlockSpec(memory_space=pltpu.HBM)
SEM_SPEC = pl.BlockSpec(memory_space=pltpu.SEMAPHORE)
DATAFLOW = pltpu.SideEffectType.DATAFLOW_SIDE_EFFECTING


def _peer_copies(x_ref, land_ref, send_sems, recv_sems, scatter):
    mx, my, mc = lax.axis_index("x"), lax.axis_index("y"), lax.axis_index("c")
    me = 4 * mx + 2 * my + mc
    pairs = []
    for k in range(1, N_DEV):
        kx, ky, kc = (k >> 2) & 1, (k >> 1) & 1, k & 1
        peer_lin = me ^ k
        send = pltpu.make_async_remote_copy(
            src_ref=x_ref.at[peer_lin] if scatter else land_ref.at[me], dst_ref=land_ref.at[me],
            send_sem=send_sems.at[k - 1], recv_sem=recv_sems.at[k - 1],
            device_id=(mx ^ kx, my ^ ky, mc ^ kc), device_id_type=pl.DeviceIdType.MESH)
        arrival = pltpu.make_async_remote_copy(
            src_ref=land_ref.at[peer_lin], dst_ref=land_ref.at[peer_lin],
            send_sem=send_sems.at[k - 1], recv_sem=recv_sems.at[k - 1],
            device_id=(mx, my, mc), device_id_type=pl.DeviceIdType.MESH)
        pairs.append((send, arrival))
    return pairs


def _exchange_start(xs, scatter, name):
    n = len(xs)
    ns = n if scatter else 0

    def body(*refs):
        srcs = refs[:ns] if scatter else (None,) * n
        lands, sends, recvs = refs[ns:ns + n], refs[ns + n:ns + 2 * n], refs[ns + 2 * n:ns + 3 * n]
        for t in range(n):
            for send, _ in _peer_copies(srcs[t], lands[t], sends[t], recvs[t], scatter):
                send.start()
        token = refs[-1]
        token[...] = jnp.zeros_like(token)

    sems = pltpu.SemaphoreType.DMA((N_DEV - 1,))
    operands = [pltpu.with_memory_space_constraint(x, pltpu.HBM) for x in xs]
    if scatter:
        operands += [pltpu.with_memory_space_constraint(lax.empty(x.shape, x.dtype), pltpu.HBM) for x in xs]
    outs = pl.pallas_call(
        body, name=name,
        out_shape=(sems,) * (2 * n) + tuple(pltpu.HBM(a.shape, a.dtype) for a in operands)
        + (jax.ShapeDtypeStruct((8, 128), F32),),
        in_specs=(HBM_SPEC,) * (ns + n),
        out_specs=(SEM_SPEC,) * (2 * n) + (HBM_SPEC,) * (ns + n) + (pl.BlockSpec(memory_space=pltpu.VMEM),),
        input_output_aliases={i: 2 * n + i for i in range(ns + n)},
        compiler_params=pltpu.CompilerParams(has_side_effects=DATAFLOW),
    )(*operands)
    return (outs[:-1], scatter), outs[-1]


def _exchange_wait(state, after, name):
    held, scatter = state
    n = len(held) // (4 if scatter else 3)
    ns = n if scatter else 0
    sems, thru = held[:2 * n], held[2 * n:]

    def body(*refs):
        srcs = refs[:ns] if scatter else (None,) * n
        lands, sends, recvs = refs[ns:ns + n], refs[ns + n:ns + 2 * n], refs[ns + 2 * n:ns + 3 * n]
        for t in range(n):
            for send, arrival in _peer_copies(srcs[t], lands[t], sends[t], recvs[t], scatter):
                send.wait_send()
                arrival.wait_recv()

    outs = pl.pallas_call(
        body, name=name,
        out_shape=tuple(pltpu.HBM(a.shape, a.dtype) for a in thru),
        in_specs=(HBM_SPEC,) * (ns + n) + (SEM_SPEC,) * (2 * n) + (pl.BlockSpec(memory_space=pl.ANY),),
        out_specs=(HBM_SPEC,) * (ns + n), input_output_aliases={i: i for i in range(ns + n)},
        compiler_params=pltpu.CompilerParams(has_side_effects=DATAFLOW),
    )(*thru, *sems, after)
    return list(outs[ns:]), list(outs[:ns])


def _place_shard(x, me, dtype, name):
    r, c = x.shape
    tr = r
    if r * c * 4 > PLACE_BLOCK_BYTES:
        for cand in (512, 256, 128, 64, 32, 16):
            if r % cand == 0 and cand * c * 4 <= PLACE_BLOCK_BYTES:
                tr = cand
                break

    def body(me_ref, x_ref, o_ref):
        o_ref[...] = x_ref[...].astype(o_ref.dtype)

    return pl.pallas_call(
        body, name=name, out_shape=jax.ShapeDtypeStruct((N_DEV, r, c), dtype),
        grid_spec=pltpu.PrefetchScalarGridSpec(
            num_scalar_prefetch=1, grid=(r // tr,),
            in_specs=[pl.BlockSpec((tr, c), lambda i, me_ref: (i, 0))],
            out_specs=pl.BlockSpec((None, tr, c), lambda i, me_ref: (me_ref[0], i, 0))),
        compiler_params=_params(),
    )(me.reshape(1), x)


NN = ((1,), (0,))
NT = ((1,), (1,))
TN = ((0,), (0,))


def _matmul(a, b, out_shape, *, grid, a_spec, b_spec, o_spec, dims, nred, name, res=None, res_spec=None,
            out_dtype=F32, slabs=0):
    has_res = res is not None

    def body(*refs):
        a_ref, b_ref = refs[0], refs[1]
        r_ref = refs[2] if has_res else None
        o_ref = refs[3] if has_res else refs[2]
        if slabs:
            part = None
            for n in range(slabs):
                term = lax.dot_general(a_ref[n].astype(BF16), b_ref[n].astype(BF16), (dims, ((), ())),
                                       preferred_element_type=F32)
                part = term if part is None else part + term
        else:
            part = lax.dot_general(a_ref[...].astype(BF16), b_ref[...].astype(BF16), (dims, ((), ())),
                                   preferred_element_type=F32)
        if nred == 1:
            if has_res:
                part = part + r_ref[...]
            o_ref[...] = part.astype(o_ref.dtype)
        else:
            acc = refs[-1]
            r = pl.program_id(2)

            @pl.when(r == 0)
            def _():
                acc[...] = part

            @pl.when(r > 0)
            def _():
                acc[...] += part

            @pl.when(r == nred - 1)
            def _():
                tot = acc[...]
                if has_res:
                    tot = tot + r_ref[...]
                o_ref[...] = tot.astype(o_ref.dtype)

    in_specs = [a_spec, b_spec] + ([res_spec] if has_res else [])
    args = (a, b) + ((res,) if has_res else ())
    acc_shape = tuple(d for d in o_spec.block_shape if d is not None)
    return pl.pallas_call(
        body, name=name, grid=grid, out_shape=jax.ShapeDtypeStruct(out_shape, out_dtype),
        in_specs=in_specs, out_specs=o_spec,
        scratch_shapes=[pltpu.VMEM(acc_shape, F32)] if nred > 1 else [],
        compiler_params=_params(),
    )(*args)


def _mm_nn(a, w, name, res=None, tn=None, out_dtype=F32):
    m, k = a.shape
    n = w.shape[1]
    tn = tn or n
    tm = min(MM_TILE, m)
    ospec = pl.BlockSpec((tm, tn), lambda i, j, r: (i, j))
    return _matmul(a, w, (m, n), grid=(m // tm, n // tn, 1),
                   a_spec=pl.BlockSpec((tm, k), lambda i, j, r: (i, 0)),
                   b_spec=pl.BlockSpec((k, tn), lambda i, j, r: (0, j)),
                   o_spec=ospec, dims=NN, nred=1, name=name, res=res, res_spec=ospec if res is not None else None,
                   out_dtype=out_dtype)


def _mm_nt(a, w, name, out_dtype=F32):
    m, n = a.shape
    k = w.shape[0]
    tm = min(MM_TILE, m)
    return _matmul(a, w, (m, k), grid=(m // tm, 1, 1),
                   a_spec=pl.BlockSpec((tm, n), lambda i, j, r: (i, 0)),
                   b_spec=pl.BlockSpec((k, n), lambda i, j, r: (0, 0)),
                   o_spec=pl.BlockSpec((tm, k), lambda i, j, r: (i, 0)), dims=NT, nred=1, name=name,
                   out_dtype=out_dtype)


def _norm_matmul(h, g, b, out_shape, *, grid, b_spec, o_spec, name, out_dtype=F32):
    s, d = h.shape
    tm = s // grid[0]

    def body(h_ref, g_ref, b_ref, o_ref, xn_ref):
        @pl.when(pl.program_id(1) == 0)
        def _():
            hv = h_ref[...]
            r = lax.rsqrt(jnp.mean(hv * hv, axis=-1, keepdims=True) + RMS_EPS)
            xn_ref[...] = (hv * r * g_ref[...]).astype(xn_ref.dtype)

        o_ref[...] = jnp.dot(xn_ref[...], b_ref[...].astype(BF16), preferred_element_type=F32).astype(o_ref.dtype)

    row = pl.BlockSpec((tm, d), lambda i, j: (i, 0))
    return pl.pallas_call(
        body, name=name, grid=grid,
        out_shape=(jax.ShapeDtypeStruct(out_shape, out_dtype), jax.ShapeDtypeStruct((s, d), BF16)),
        in_specs=[row, pl.BlockSpec((1, d), lambda i, j: (0, 0)), b_spec],
        out_specs=(o_spec, row), compiler_params=_params(),
    )(h, g.reshape(1, d), b)


def _matmul_rms_bwd(a, w, h, g, res, name, tm=ROW_TILE):
    slabs = a.shape[0] if a.ndim == 3 else 0
    s, n = a.shape[-2:]
    d = w.shape[-2]
    tm = min(tm, s)

    def body(a_ref, w_ref, h_ref, g_ref, r_ref, dh_ref, dg_ref):
        if slabs:
            dy = None
            for j in range(slabs):
                term = lax.dot_general(a_ref[j].astype(BF16), w_ref[j].astype(BF16), (NT, ((), ())),
                                       preferred_element_type=F32)
                dy = term if dy is None else dy + term
        else:
            dy = lax.dot_general(a_ref[...].astype(BF16), w_ref[...].astype(BF16), (NT, ((), ())),
                                 preferred_element_type=F32)
        hv = h_ref[...]
        r = lax.rsqrt(jnp.mean(hv * hv, axis=-1, keepdims=True) + RMS_EPS)
        hn = hv * r
        u = dy * g_ref[...]
        dh_ref[...] = r * (u - hn * jnp.mean(u * hn, axis=-1, keepdims=True)) + r_ref[...]
        part = jnp.sum(dy * hn, axis=0, keepdims=True)

        @pl.when(pl.program_id(0) == 0)
        def _():
            dg_ref[...] = part

        @pl.when(pl.program_id(0) > 0)
        def _():
            dg_ref[...] += part

    row = pl.BlockSpec((tm, d), lambda i: (i, 0))
    vec = pl.BlockSpec((1, d), lambda i: (0, 0))
    if slabs:
        a_spec = pl.BlockSpec((slabs, tm, n), lambda i: (0, i, 0))
        w_spec = pl.BlockSpec((slabs, d, n), lambda i: (0, 0, 0))
    else:
        a_spec = pl.BlockSpec((tm, n), lambda i: (i, 0))
        w_spec = pl.BlockSpec((d, n), lambda i: (0, 0))
    dh, dg = pl.pallas_call(
        body, name=name, grid=(s // tm,),
        out_shape=(jax.ShapeDtypeStruct((s, d), F32), jax.ShapeDtypeStruct((1, d), F32)),
        in_specs=[a_spec, w_spec, row, vec, row], out_specs=(row, vec), compiler_params=_params(),
    )(a, w, h, g.reshape(1, d), res)
    return dh, dg.reshape(d)


GRAD_DTYPE = BF16


def _mm_tn(a, b, name, tk=512, tn=None):
    s, k = a.shape
    n = b.shape[1]
    tn = tn or n
    tk = min(tk, k)
    ts = s if b.dtype == BF16 else max(s // 2, 1)
    return _matmul(a, b, (k, n), grid=(k // tk, n // tn, s // ts),
                   a_spec=pl.BlockSpec((ts, tk), lambda i, j, r: (r, i)),
                   b_spec=pl.BlockSpec((ts, tn), lambda i, j, r: (r, j)),
                   o_spec=pl.BlockSpec((tk, tn), lambda i, j, r: (i, j)), dims=TN, nred=s // ts, name=name,
                   out_dtype=GRAD_DTYPE)


def _rms_fwd(h, g, name):
    s, d = h.shape
    tm = min(ROW_TILE, s)

    def body(h_ref, g_ref, o_ref):
        hv = h_ref[...]
        r = lax.rsqrt(jnp.mean(hv * hv, axis=-1, keepdims=True) + RMS_EPS)
        o_ref[...] = (hv * r * g_ref[...]).astype(o_ref.dtype)

    return pl.pallas_call(
        body, name=name, grid=(s // tm,), out_shape=jax.ShapeDtypeStruct((s, d), BF16),
        in_specs=[pl.BlockSpec((tm, d), lambda i: (i, 0)), pl.BlockSpec((1, d), lambda i: (0, 0))],
        out_specs=pl.BlockSpec((tm, d), lambda i: (i, 0)), compiler_params=_params(),
    )(h, g.reshape(1, d))


def _rms_bwd(dy, h, g, res, name):
    s, d = h.shape
    tm = min(ROW_TILE, s)
    has_res = res is not None

    def body(*refs):
        dy_ref, h_ref, g_ref = refs[:3]
        r_ref = refs[3] if has_res else None
        dh_ref, dg_ref = refs[-2], refs[-1]
        hv = h_ref[...]
        r = lax.rsqrt(jnp.mean(hv * hv, axis=-1, keepdims=True) + RMS_EPS)
        hn = hv * r
        dyv = dy_ref[...].astype(F32)
        u = dyv * g_ref[...]
        dh = r * (u - hn * jnp.mean(u * hn, axis=-1, keepdims=True))
        if has_res:
            dh = dh + r_ref[...]
        dh_ref[...] = dh
        part = jnp.sum(dyv * hn, axis=0, keepdims=True)

        @pl.when(pl.program_id(0) == 0)
        def _():
            dg_ref[...] = part

        @pl.when(pl.program_id(0) > 0)
        def _():
            dg_ref[...] += part

    row = pl.BlockSpec((tm, d), lambda i: (i, 0))
    vec = pl.BlockSpec((1, d), lambda i: (0, 0))
    dh, dg = pl.pallas_call(
        body, name=name, grid=(s // tm,),
        out_shape=(jax.ShapeDtypeStruct((s, d), F32), jax.ShapeDtypeStruct((1, d), F32)),
        in_specs=[row, row, vec] + ([row] if has_res else []),
        out_specs=(row, vec), compiler_params=_params(),
    )(*((dy, h, g.reshape(1, d)) + ((res,) if has_res else ())))
    return dh, dg.reshape(d)


def _loss_head(h, g, target):
    s, d = h.shape
    tm = min(ROW_TILE, s)

    def body(h_ref, g_ref, t_ref, loss_ref, dh_ref, dg_ref):
        hv = h_ref[...]
        r = lax.rsqrt(jnp.mean(hv * hv, axis=-1, keepdims=True) + RMS_EPS)
        hn = hv * r
        gv = g_ref[...]
        err = hn * gv - t_ref[...]
        rows = jnp.mean(err * err, axis=-1, keepdims=True)
        lpart = 0.5 * jnp.sum(rows, axis=0, keepdims=True) + jnp.zeros((1, 128), F32)
        dy = err * (1.0 / d)
        u = dy * gv
        dh_ref[...] = r * (u - hn * jnp.mean(u * hn, axis=-1, keepdims=True))
        gpart = jnp.sum(dy * hn, axis=0, keepdims=True)

        @pl.when(pl.program_id(0) == 0)
        def _():
            dg_ref[...] = gpart
            loss_ref[...] = lpart

        @pl.when(pl.program_id(0) > 0)
        def _():
            dg_ref[...] += gpart
            loss_ref[...] += lpart

    row = pl.BlockSpec((tm, d), lambda i: (i, 0))
    vec = pl.BlockSpec((1, d), lambda i: (0, 0))
    return pl.pallas_call(
        body, name="loss_head", grid=(s // tm,),
        out_shape=(jax.ShapeDtypeStruct((1, 128), F32), jax.ShapeDtypeStruct((s, d), F32),
                   jax.ShapeDtypeStruct((1, d), F32)),
        in_specs=[row, vec, row],
        out_specs=(pl.BlockSpec((1, 128), lambda i: (0, 0)), row, vec), compiler_params=_params(),
    )(h, g.reshape(1, d), target)


def _shift_down(x, k):
    return pltpu.roll(x, k, 0)


def _shift_up(x, k):
    return pltpu.roll(x, x.shape[0] - k, 0)


def _conv3(x, w):
    return w[2:3, :] * x + w[1:2, :] * _shift_down(x, 1) + w[0:1, :] * _shift_down(x, 2)


def _conv3_t(x, w):
    return w[2:3, :] * x + w[1:2, :] * _shift_up(x, 1) + w[0:1, :] * _shift_up(x, 2)


def _sigmoid(x):
    return 1.0 / (1.0 + jnp.exp(-x))


def _prev_map(tile, halo, col):
    return lambda i: (jnp.maximum(i * (tile // halo) - 1, 0), col)


def _next_map(tile, halo, col, nrows):
    return lambda i: (jnp.minimum((i + 1) * (tile // halo), nrows // halo - 1), col)


def _sconv_fwd(proj, w):
    s = proj.shape[0]
    t = min(ROW_TILE, s)

    def body(cur_ref, prev_ref, w_ref, o_ref):
        i = pl.program_id(0)
        prev = prev_ref[...] * (i > 0).astype(F32)
        ext = jnp.concatenate([prev, cur_ref[...]], axis=0)
        sv = ext[:, 2 * GROUP:3 * GROUP] * ext[:, 0:GROUP]
        y = ext[:, GROUP:2 * GROUP] * _conv3(sv, w_ref[...])
        o_ref[...] = y[8:].astype(o_ref.dtype)

    return pl.pallas_call(
        body, name="sconv_fwd", grid=(s // t,), out_shape=jax.ShapeDtypeStruct((s, 4 * GROUP), BF16),
        in_specs=[pl.BlockSpec((t, 3 * GROUP), lambda i: (i, 0)),
                  pl.BlockSpec((8, 3 * GROUP), _prev_map(t, 8, 0)),
                  pl.BlockSpec((3, GROUP), lambda i: (0, 0))],
        out_specs=pl.BlockSpec((t, GROUP), lambda i: (i, 0)), compiler_params=_params(),
    )(proj, proj, w)


def _sconv_bwd(proj, w, dy):
    s = proj.shape[0]
    t = min(ROW_TILE, s)
    nt = s // t

    def body(cur_ref, prev_ref, next_ref, w_ref, dy_ref, dyn_ref, dp_ref, dw_ref):
        i = pl.program_id(0)
        first = (i > 0).astype(F32)
        last = (i < nt - 1).astype(F32)
        ext = jnp.concatenate([prev_ref[...] * first, cur_ref[...], next_ref[...] * last], axis=0)
        dye = jnp.concatenate([jnp.zeros((8, GROUP), F32), dy_ref[...], dyn_ref[...] * last], axis=0)
        hv, bv, cv = ext[:, 0:GROUP], ext[:, GROUP:2 * GROUP], ext[:, 2 * GROUP:3 * GROUP]
        wv = w_ref[...]
        sv = cv * hv
        conv = _conv3(sv, wv)
        dconv = dye * bv
        ds = _conv3_t(dconv, wv)
        dp = jnp.concatenate([ds * cv, dye * conv, ds * hv], axis=1)
        dp_ref[...] = dp[8:8 + t].astype(dp_ref.dtype)
        dc = dconv[8:8 + t]
        dw = jnp.concatenate([
            jnp.sum(dc * _shift_down(sv, 2)[8:8 + t], axis=0, keepdims=True),
            jnp.sum(dc * _shift_down(sv, 1)[8:8 + t], axis=0, keepdims=True),
            jnp.sum(dc * sv[8:8 + t], axis=0, keepdims=True),
            jnp.zeros((5, GROUP), F32)], axis=0)

        @pl.when(i == 0)
        def _():
            dw_ref[...] = dw

        @pl.when(i > 0)
        def _():
            dw_ref[...] += dw

    dp, dw = pl.pallas_call(
        body, name="sconv_bwd", grid=(nt,),
        out_shape=(jax.ShapeDtypeStruct((s, N_IN_PAD), BF16), jax.ShapeDtypeStruct((8, GROUP), F32)),
        in_specs=[pl.BlockSpec((t, 3 * GROUP), lambda i: (i, 0)),
                  pl.BlockSpec((8, 3 * GROUP), _prev_map(t, 8, 0)),
                  pl.BlockSpec((8, 3 * GROUP), _next_map(t, 8, 0, s)),
                  pl.BlockSpec((3, GROUP), lambda i: (0, 0)),
                  pl.BlockSpec((t, GROUP), lambda i: (i, 0)),
                  pl.BlockSpec((8, GROUP), _next_map(t, 8, 0, s))],
        out_specs=(pl.BlockSpec((t, 3 * GROUP), lambda i: (i, 0)), pl.BlockSpec((8, GROUP), lambda i: (0, 0))),
        compiler_params=_params(),
    )(proj, proj, proj, w, dy, dy)
    return dp, dw[:3]


def _lane_window(shape):
    lane = lax.broadcasted_iota(jnp.int32, shape, 1)
    return lane, jnp.where(lane < 64, 2.0, jnp.where(lane < 128, 4.0, jnp.where(lane < 192, 8.0, 16.0)))


def _by_group(lane, s1, s2, s3, s4):
    return jnp.where(lane < 64, s1, jnp.where(lane < 128, s2, jnp.where(lane < 192, s3, s4)))


def _pool_z(ext, row0):
    s1 = ext + _shift_down(ext, 1)
    s2 = s1 + _shift_down(s1, 2)
    s3 = s2 + _shift_down(s2, 4)
    s4 = s3 + _shift_down(s3, 8)
    lane, win = _lane_window(ext.shape)
    tpos = (lax.broadcasted_iota(jnp.int32, ext.shape, 0) + (row0 - 16 + 1)).astype(F32)
    cnt = jnp.maximum(jnp.minimum(tpos, win), 1.0)
    return _by_group(lane, s1, s2, s3, s4) / cnt - ext


ANY_SPEC = pl.BlockSpec(memory_space=pl.ANY)


def _pool_fwd(proj, wbd, scale, ybuf):
    s = proj.shape[0]
    t = min(ROW_TILE, s)
    col = (COL_GATE - GROUP) // GROUP

    def body(cur_ref, prev_ref, w_ref, sc_ref, buf_ref, o_ref):
        i = pl.program_id(0)
        ext = jnp.concatenate([prev_ref[...] * (i > 0).astype(F32), cur_ref[...]], axis=0)
        z = _pool_z(ext, i * t)[16:]
        y = jnp.dot(z.astype(BF16), w_ref[...].astype(BF16), preferred_element_type=F32)
        o_ref[...] = (y * sc_ref[...]).astype(o_ref.dtype)

    return pl.pallas_call(
        body, name="pool_fwd", grid=(s // t,), out_shape=jax.ShapeDtypeStruct(ybuf.shape, ybuf.dtype),
        in_specs=[pl.BlockSpec((t, GROUP), lambda i: (i, col)),
                  pl.BlockSpec((16, GROUP), _prev_map(t, 16, col)),
                  pl.BlockSpec((GROUP, GROUP), lambda i: (0, 0)),
                  pl.BlockSpec((1, GROUP), lambda i: (0, 0)), ANY_SPEC],
        out_specs=pl.BlockSpec((t, GROUP), lambda i: (i, 3)), input_output_aliases={4: 0},
        compiler_params=_params(),
    )(proj, proj, wbd, scale.reshape(1, GROUP), ybuf)


def _pool_bwd(proj, wbd, scale, dy, dbuf):
    s = proj.shape[0]
    t = min(ROW_TILE, s)
    nt = s // t
    col = (COL_GATE - GROUP) // GROUP

    def body(cur_ref, prev_ref, w_ref, sc_ref, dy_ref, dyn_ref, buf_ref, dp_ref, dw_ref, dsc_ref):
        i = pl.program_id(0)
        ext = jnp.concatenate([prev_ref[...] * (i > 0).astype(F32), cur_ref[...]], axis=0)
        z = _pool_z(ext, i * t)[16:]
        wv = w_ref[...].astype(BF16)
        dyc = dy_ref[...]
        dye = jnp.concatenate([dyc, dyn_ref[...] * (i < nt - 1).astype(F32)], axis=0) * sc_ref[...]
        dz = lax.dot_general(dye.astype(BF16), wv, (NT, ((), ())), preferred_element_type=F32)
        lane, win = _lane_window(dz.shape)
        tpos = (lax.broadcasted_iota(jnp.int32, dz.shape, 0) + (i * t + 1)).astype(F32)
        e = dz / jnp.minimum(tpos, win)
        f1 = e + _shift_up(e, 1)
        f2 = f1 + _shift_up(f1, 2)
        f3 = f2 + _shift_up(f2, 4)
        f4 = f3 + _shift_up(f3, 8)
        dp = _by_group(lane, f1, f2, f3, f4) - dz
        dp_ref[...] = dp[:t].astype(dp_ref.dtype)
        zb = z.astype(BF16)
        y = jnp.dot(zb, wv, preferred_element_type=F32)
        dsc = jnp.sum(dyc * y, axis=0, keepdims=True)
        dw = lax.dot_general(zb, dye[:t].astype(BF16), (TN, ((), ())), preferred_element_type=F32)

        @pl.when(i == 0)
        def _():
            dw_ref[...] = dw
            dsc_ref[...] = dsc

        @pl.when(i > 0)
        def _():
            dw_ref[...] += dw
            dsc_ref[...] += dsc

    dp, dw, dsc = pl.pallas_call(
        body, name="pool_bwd", grid=(nt,),
        out_shape=(jax.ShapeDtypeStruct(dbuf.shape, dbuf.dtype), jax.ShapeDtypeStruct((GROUP, GROUP), F32),
                   jax.ShapeDtypeStruct((1, GROUP), F32)),
        in_specs=[pl.BlockSpec((t, GROUP), lambda i: (i, col)),
                  pl.BlockSpec((16, GROUP), _prev_map(t, 16, col)),
                  pl.BlockSpec((GROUP, GROUP), lambda i: (0, 0)),
                  pl.BlockSpec((1, GROUP), lambda i: (0, 0)),
                  pl.BlockSpec((t, GROUP), lambda i: (i, 3)),
                  pl.BlockSpec((16, GROUP), _next_map(t, 16, 3, s)), ANY_SPEC],
        out_specs=(pl.BlockSpec((t, GROUP), lambda i: (i, col)), pl.BlockSpec((GROUP, GROUP), lambda i: (0, 0)),
                   pl.BlockSpec((1, GROUP), lambda i: (0, 0))),
        input_output_aliases={6: 0}, compiler_params=_params(),
    )(proj, proj, wbd, scale.reshape(1, GROUP), dy, dy, dbuf)
    return dp, dw, dsc.reshape(GROUP)


FF_HALO = 16


def _ffn_gate_fwd(u0, w):
    s = u0.shape[1]
    t = min(ROW_TILE, s)

    def body(a_ref, ap_ref, g_ref, gp_ref, wa_ref, wg_ref, o_ref):
        first = (pl.program_id(1) > 0).astype(F32)
        a = _conv3(jnp.concatenate([ap_ref[...] * first, a_ref[...].astype(F32)], axis=0), wa_ref[...])[FF_HALO:]
        g = _conv3(jnp.concatenate([gp_ref[...] * first, g_ref[...].astype(F32)], axis=0), wg_ref[...])[FF_HALO:]
        o_ref[...] = (a * (g * _sigmoid(g))).astype(o_ref.dtype)

    def cur(off):
        return pl.BlockSpec((None, t, FF_SHARD), lambda j, i: (j + off, i, 0))

    def prev(off):
        return pl.BlockSpec((None, FF_HALO, FF_SHARD),
                            lambda j, i: (j + off, jnp.maximum(i * (t // FF_HALO) - 1, 0), 0))

    def wspec(off):
        return pl.BlockSpec((None, 3, FF_SHARD), lambda j, i: (j + off, 0, 0))

    return pl.pallas_call(
        body, name="ffn_gate_fwd", grid=(FF_HALF, s // t),
        out_shape=jax.ShapeDtypeStruct((FF_HALF, s, FF_SHARD), BF16),
        in_specs=[cur(0), prev(0), cur(FF_HALF), prev(FF_HALF), wspec(0), wspec(FF_HALF)],
        out_specs=pl.BlockSpec((None, t, FF_SHARD), lambda j, i: (j, i, 0)), compiler_params=_params(),
    )(u0, u0, u0, u0, w, w)


def _ffn_gate_bwd(u0, w, dact):
    s = u0.shape[1]
    t = min(ROW_TILE, s)
    nt = s // t

    def body(c_ref, p_ref, n_ref, w_ref, d_ref, dn_ref, du_ref, dw_ref):
        i = pl.program_id(1)
        first = (i > 0).astype(F32)
        last = (i < nt - 1).astype(F32)
        dext = jnp.concatenate([jnp.zeros((FF_HALO, FF_SHARD), F32), d_ref[...].astype(F32), dn_ref[...] * last],
                               axis=0)
        ext = [jnp.concatenate([p_ref[n] * first, c_ref[n].astype(F32), n_ref[n] * last], axis=0) for n in range(2)]
        a = _conv3(ext[0], w_ref[0])
        g = _conv3(ext[1], w_ref[1])
        sg = _sigmoid(g)
        silu = g * sg
        dus = (dext * silu, dext * a * (sg + silu * (1.0 - sg)))
        mine = slice(FF_HALO, FF_HALO + t)
        for n in range(2):
            du_ref[n] = _conv3_t(dus[n], w_ref[n])[mine].astype(du_ref.dtype)
            dc = dus[n][mine]
            dw = jnp.concatenate([
                jnp.sum(dc * _shift_down(ext[n], 2)[mine], axis=0, keepdims=True),
                jnp.sum(dc * _shift_down(ext[n], 1)[mine], axis=0, keepdims=True),
                jnp.sum(dc * ext[n][mine], axis=0, keepdims=True),
                jnp.zeros((5, FF_SHARD), F32)], axis=0)

            @pl.when(i == 0)
            def _(n=n, dw=dw):
                dw_ref[n] = dw

            @pl.when(i > 0)
            def _(n=n, dw=dw):
                dw_ref[n] += dw

    def pair(rows, row_map):
        return pl.BlockSpec((2, None, rows, FF_SHARD), lambda j, i: (0, j, row_map(i), 0))

    prev_row = lambda i: jnp.maximum(i * (t // FF_HALO) - 1, 0)
    next_row = lambda i: jnp.minimum((i + 1) * (t // FF_HALO), s // FF_HALO - 1)
    u2 = u0.reshape(2, FF_HALF, s, FF_SHARD)
    du, dw = pl.pallas_call(
        body, name="ffn_gate_bwd", grid=(FF_HALF, nt),
        out_shape=(jax.ShapeDtypeStruct((2, FF_HALF, s, FF_SHARD), BF16),
                   jax.ShapeDtypeStruct((2, FF_HALF, 8, FF_SHARD), F32)),
        in_specs=[pair(t, lambda i: i), pair(FF_HALO, prev_row), pair(FF_HALO, next_row), pair(3, lambda i: 0),
                  pl.BlockSpec((None, t, FF_SHARD), lambda j, i: (j, i, 0)),
                  pl.BlockSpec((None, FF_HALO, FF_SHARD), lambda j, i: (j, next_row(i), 0))],
        out_specs=(pair(t, lambda i: i), pair(8, lambda i: 0)),
        compiler_params=_params(),
    )(u2, u2, u2, w.reshape(2, FF_HALF, 3, FF_SHARD), dact, dact)
    return du.reshape(2 * FF_HALF, s, FF_SHARD), dw.reshape(2 * FF_HALF, 8, FF_SHARD)[:, :3]


def _rope_tables(positions):
    inv_freq = ROPE_THETA ** (-jnp.arange(0, ROPE_DIM, 2, dtype=F32) / ROPE_DIM)
    ang = positions.astype(F32)[:, None] * inv_freq
    cos, sin = jnp.cos(ang), jnp.sin(ang)
    s = positions.shape[0]
    half = ROPE_DIM // 2
    rest = HEAD_DIM - ROPE_DIM
    ca = jnp.concatenate([cos, cos, jnp.ones((s, rest), F32)], axis=1)
    cb = jnp.concatenate([-sin, jnp.zeros((s, HEAD_DIM - half), F32)], axis=1)
    cc = jnp.concatenate([jnp.zeros((s, half), F32), sin, jnp.zeros((s, rest), F32)], axis=1)
    return tuple(jnp.tile(tb, (1, N_HEADS)) for tb in (ca, cb, cc))


QK_WIDE = 128
LANE_CQ, LANE_CK = 64, 67
KT_ROWS = 80


def _three_bf16(x):
    hi = x.astype(BF16).astype(F32)
    mid = (x - hi).astype(BF16).astype(F32)
    lo = (x - hi - mid).astype(BF16).astype(F32)
    return hi, mid, lo


def _heads_split(proj, col, tables, c, name):
    s = proj.shape[0]
    t = min(ROW_TILE, s)
    rope = tables is not None
    wide = c is not None
    width = QK_WIDE if wide else HEAD_DIM

    def body(*refs):
        x_ref = refs[0]
        q_ref, k_ref, v_ref, kt_ref, vt_ref = refs[-5:]
        xv = x_ref[...]
        parts = [xv[:, 0:GROUP], xv[:, GROUP:2 * GROUP], xv[:, 2 * GROUP:3 * GROUP]]
        if rope:
            ca, cb, cc = refs[1][...], refs[2][...], refs[3][...]
            for n in range(2):
                p = parts[n]
                parts[n] = p * ca + pltpu.roll(p, GROUP - 8, 1) * cb + pltpu.roll(p, 8, 1) * cc
        parts[0] = parts[0] * (HEAD_DIM ** -0.5)
        k_t, v_t = parts[1].T, parts[2].T
        ones_row = jnp.where(lax.broadcasted_iota(jnp.int32, (KT_ROWS - HEAD_DIM, t), 0) == 0, 1.0, 0.0)
        lane = lax.broadcasted_iota(jnp.int32, (t, QK_WIDE), 1)
        zeros = jnp.zeros((t, QK_WIDE - HEAD_DIM), F32)
        for h in range(N_HEADS):
            hs = slice(h * HEAD_DIM, (h + 1) * HEAD_DIM)
            qh, kh = parts[0][:, hs], parts[1][:, hs]
            if wide:
                terms = _three_bf16(refs[-6][:, h:h + 1])
                qh = jnp.concatenate([qh, zeros], axis=1)
                kh = jnp.concatenate([kh, zeros], axis=1)
                for n in range(3):
                    qh = jnp.where(lane == LANE_CQ + n, terms[n], jnp.where(lane == LANE_CK + n, 1.0, qh))
                    kh = jnp.where(lane == LANE_CK + n, -terms[n], jnp.where(lane == LANE_CQ + n, 1.0, kh))
            q_ref[h] = qh.astype(q_ref.dtype)
            k_ref[h] = kh.astype(k_ref.dtype)
            v_ref[h] = parts[2][:, hs].astype(v_ref.dtype)
            kt_ref[h] = jnp.concatenate([k_t[hs, :], ones_row], axis=0).astype(kt_ref.dtype)
            vt_ref[h] = v_t[hs, :].astype(vt_ref.dtype)

    tab = pl.BlockSpec((t, GROUP), lambda i: (i, 0))
    qk = pl.BlockSpec((N_HEADS, t, width), lambda i: (0, i, 0))
    heads = pl.BlockSpec((N_HEADS, t, HEAD_DIM), lambda i: (0, i, 0))
    heads_t = pl.BlockSpec((N_HEADS, HEAD_DIM, t), lambda i: (0, 0, i))
    qk_shape = jax.ShapeDtypeStruct((N_HEADS, s, width), BF16)
    return pl.pallas_call(
        body, name=name, grid=(s // t,),
        out_shape=(qk_shape, qk_shape, jax.ShapeDtypeStruct((N_HEADS, s, HEAD_DIM), BF16),
                   jax.ShapeDtypeStruct((N_HEADS, KT_ROWS, s), BF16),
                   jax.ShapeDtypeStruct((N_HEADS, HEAD_DIM, s), BF16)),
        in_specs=[pl.BlockSpec((t, 3 * GROUP), lambda i: (i, col))] + ([tab, tab, tab] if rope else [])
        + ([pl.BlockSpec((t, 128), lambda i: (i, 0))] if wide else []),
        out_specs=(qk, qk, heads, pl.BlockSpec((N_HEADS, KT_ROWS, t), lambda i: (0, 0, i)), heads_t),
        compiler_params=_params(),
    )(*((proj,) + (tuple(tables) if rope else ()) + ((c,) if wide else ())))


def _heads_merge(dqt, dk, dv, tables, name, dbuf, col):
    s = dv.shape[1]
    t = min(ROW_TILE, s)
    rope = tables is not None

    wide = dk.shape[2] == QK_WIDE

    def body(*refs):
        o_ref = refs[n_in + 1]
        dq = jnp.concatenate([refs[0][h, :HEAD_DIM, :] for h in range(N_HEADS)], axis=0).T
        parts = [dq] + [jnp.concatenate([r[h][:, :HEAD_DIM] for h in range(N_HEADS)], axis=1) for r in refs[1:3]]
        parts[0] = parts[0] * (HEAD_DIM ** -0.5)
        if rope:
            ca, cb, cc = refs[3][...], refs[4][...], refs[5][...]
            for n in range(2):
                p = parts[n]
                parts[n] = p * ca + pltpu.roll(p * cb, 8, 1) + pltpu.roll(p * cc, GROUP - 8, 1)
        o_ref[...] = jnp.concatenate(parts, axis=1).astype(o_ref.dtype)
        if wide:
            over_keys = jnp.concatenate([refs[0][h, HEAD_DIM:HEAD_DIM + 8, :] for h in range(N_HEADS)]
                                        + [jnp.zeros((128 - 8 * N_HEADS, t), F32)], axis=0).T
            lane = lax.broadcasted_iota(jnp.int32, (t, 128), 1)
            dc = jnp.zeros((t, 128), F32)
            for h in range(N_HEADS):
                dc = jnp.where(lane == h, over_keys[:, 8 * h:8 * h + 1] - refs[1][h][:, LANE_CK:LANE_CK + 1], dc)
            refs[n_in + 2][...] = dc

    tab = pl.BlockSpec((t, GROUP), lambda i: (i, 0))
    heads = pl.BlockSpec((N_HEADS, t, HEAD_DIM), lambda i: (0, i, 0))
    n_in = 6 if rope else 3
    dspec = pl.BlockSpec((t, 3 * GROUP), lambda i: (i, col))
    dshape = jax.ShapeDtypeStruct(dbuf.shape, dbuf.dtype)
    return pl.pallas_call(
        body, name=name, grid=(s // t,),
        out_shape=(dshape, jax.ShapeDtypeStruct((s, 128), F32)) if wide else dshape,
        in_specs=[pl.BlockSpec((N_HEADS, KT_ROWS, t), lambda i: (0, 0, i)),
                  pl.BlockSpec((N_HEADS, t, dk.shape[2]), lambda i: (0, i, 0)), heads]
        + ([tab, tab, tab] if rope else []) + [ANY_SPEC],
        out_specs=(dspec, pl.BlockSpec((t, 128), lambda i: (i, 0))) if wide else dspec,
        input_output_aliases={n_in: 0}, compiler_params=_params(),
    )(*((dqt, dk, dv) + (tuple(tables) if rope else ()) + (dbuf,)))


def _log_sigmoid(x):
    return jnp.minimum(x, 0.0) - jnp.log(1.0 + jnp.exp(-jnp.abs(x)))


def _scan_rows(x, reverse):
    n = x.shape[0]
    row = lax.broadcasted_iota(jnp.int32, x.shape, 0)
    k = 1
    while k < n:
        if reverse:
            x = x + jnp.where(row < n - k, _shift_up(x, k), 0.0)
        else:
            x = x + jnp.where(row >= k, _shift_down(x, k), 0.0)
        k *= 2
    return x


def _gate_cumsum(proj, bias):
    s = proj.shape[0]
    col = COL_GATE // 128

    def body(z_ref, b_ref, c_ref):
        c_ref[...] = _scan_rows(_log_sigmoid(z_ref[...] + b_ref[...]), False)

    return pl.pallas_call(
        body, name="gate_cumsum", grid=(1,), out_shape=jax.ShapeDtypeStruct((s, 128), F32),
        in_specs=[pl.BlockSpec((s, 128), lambda i: (0, col)), pl.BlockSpec((1, 128), lambda i: (0, 0))],
        out_specs=pl.BlockSpec((s, 128), lambda i: (0, 0)), compiler_params=_params(),
    )(proj, bias)


def _gate_cumsum_bwd(proj, bias, dc, dbuf):
    s = proj.shape[0]
    col = COL_GATE // 128

    def body(z_ref, b_ref, dc_ref, buf_ref, dz_ref, db_ref):
        dlogf = _scan_rows(dc_ref[...], True)
        dz = dlogf * _sigmoid(-(z_ref[...] + b_ref[...]))
        dz_ref[...] = dz.astype(dz_ref.dtype)
        db_ref[...] = jnp.sum(dz, axis=0, keepdims=True)

    return pl.pallas_call(
        body, name="gate_cumsum_bwd", grid=(1,),
        out_shape=(jax.ShapeDtypeStruct(dbuf.shape, dbuf.dtype), jax.ShapeDtypeStruct((1, 128), F32)),
        in_specs=[pl.BlockSpec((s, 128), lambda i: (0, col)), pl.BlockSpec((1, 128), lambda i: (0, 0)),
                  pl.BlockSpec((s, 128), lambda i: (0, 0)), ANY_SPEC],
        out_specs=(pl.BlockSpec((s, 128), lambda i: (0, col)), pl.BlockSpec((1, 128), lambda i: (0, 0))),
        input_output_aliases={3: 0}, compiler_params=_params(),
    )(proj, bias, dc, dbuf)


DIL_REACH = 2048


def _pair_weight(mode, d):
    if mode == "fox":
        return jnp.where(d >= 0, 1.0, 0.0)
    w1 = jnp.where(jnp.abs(d - 64) <= 64, 1.0, 0.0)
    w2 = jnp.where((d & 3) == 0, jnp.where(jnp.abs(d - 256) <= 256, 1.0, 0.0), 0.0)
    w3 = jnp.where((d & 15) == 0, jnp.where(jnp.abs(d - 1024) <= 1024, 1.0, 0.0), 0.0)
    return w1 + w2 + w3


def _bias_tables(mode, tq, tk):
    nb = 2 if mode == "fox" else DIL_REACH // tk + 1
    n = lax.broadcasted_iota(jnp.int32, (nb, tk, tq), 0)
    key = lax.broadcasted_iota(jnp.int32, (nb, tk, tq), 1)
    query = lax.broadcasted_iota(jnp.int32, (nb, tk, tq), 2)
    w = _pair_weight(mode, n * tk + query - key)
    return jnp.where(w > 0.0, jnp.log(jnp.maximum(w, 1.0)), NEG)


M_INIT = -1e29


def _first_key_chunk(mode, q0, tk):
    if mode == "fox":
        return 0
    return jnp.maximum(q0 - DIL_REACH, 0) // tk


def _attention_fwd(mode, q, k, vt, tab_t, ybuf, col):
    s, width = q.shape[1], q.shape[2]
    tq = min(ATT_TQ, s)
    tk = tq
    nb = tab_t.shape[0]

    def body(q_ref, k_ref, vt_ref, tab_ref, buf_ref, y_ref, o_ref, lse_ref):
        i = pl.program_id(0)
        lo = _first_key_chunk(mode, i * tq, tk)

        def step(c, carry):
            k0 = pl.multiple_of(c * tk, tk)
            tab = tab_ref[jnp.minimum(i - c, nb - 1)]
            scores = [lax.dot_general(k_ref[h, pl.ds(k0, tk), :], q_ref[h], (NT, ((), ())),
                                      preferred_element_type=F32) for h in range(N_HEADS)]
            stats, probs = [], []
            for h in range(N_HEADS):
                m, l = carry[3 * h:3 * h + 2]
                sc = scores[h] + tab
                m_new = jnp.maximum(m, jnp.max(sc, axis=0, keepdims=True))
                alpha = jnp.exp(m - m_new)
                p = jnp.exp(sc - m_new)
                stats.append((m_new, alpha * l + jnp.sum(p, axis=0, keepdims=True), alpha))
                probs.append(p.astype(BF16))
            pv = [jnp.dot(vt_ref[h, :, pl.ds(k0, tk)], probs[h], preferred_element_type=F32) for h in range(N_HEADS)]
            new = []
            for h in range(N_HEADS):
                m_new, l, alpha = stats[h]
                new += [m_new, l, alpha * carry[3 * h + 2] + pv[h]]
            return tuple(new)

        start = (jnp.full((1, tq), M_INIT, F32), jnp.zeros((1, tq), F32), jnp.zeros((HEAD_DIM, tq), F32))
        done = lax.fori_loop(lo, i + 1, step, start * N_HEADS)
        outs = []
        for h in range(N_HEADS):
            m, l, acc = done[3 * h:3 * h + 3]
            outs.append(acc / l)
            lse_ref[h] = m + jnp.log(l)
        out = jnp.concatenate(outs, axis=0).T
        y_ref[...] = out.astype(y_ref.dtype)
        o_ref[...] = out

    rowspec = pl.BlockSpec((N_HEADS, 1, tq), lambda i: (0, 0, i))
    return pl.pallas_call(
        body, name="attention_fwd_" + mode, grid=(s // tq,),
        out_shape=(jax.ShapeDtypeStruct(ybuf.shape, ybuf.dtype), jax.ShapeDtypeStruct((s, GROUP), F32),
                   jax.ShapeDtypeStruct((N_HEADS, 1, s), F32)),
        in_specs=[pl.BlockSpec((N_HEADS, tq, width), lambda i: (0, i, 0)),
                  pl.BlockSpec((N_HEADS, s, width), lambda i: (0, 0, 0)),
                  pl.BlockSpec((N_HEADS, HEAD_DIM, s), lambda i: (0, 0, 0)),
                  pl.BlockSpec((nb, tk, tq), lambda i: (0, 0, 0)), ANY_SPEC],
        out_specs=(pl.BlockSpec((tq, GROUP), lambda i: (i, col)), pl.BlockSpec((tq, GROUP), lambda i: (i, 0)),
                   rowspec),
        input_output_aliases={4: 0}, compiler_params=_params(),
    )(q, k, vt, tab_t, ybuf)


def _attention_delta(o, do, col):
    s = o.shape[0]
    t = min(ROW_TILE, s)

    def body(o_ref, do_ref, delta_ref, dob_ref):
        dov = do_ref[...]
        prod_t = (o_ref[...] * dov).T
        for h in range(N_HEADS):
            hs = slice(h * HEAD_DIM, (h + 1) * HEAD_DIM)
            delta_ref[h] = jnp.sum(prod_t[hs, :], axis=0, keepdims=True)
            dob_ref[h] = dov[:, hs].astype(dob_ref.dtype)

    return pl.pallas_call(
        body, name="attention_delta", grid=(s // t,),
        out_shape=(jax.ShapeDtypeStruct((N_HEADS, 1, s), F32), jax.ShapeDtypeStruct((N_HEADS, s, HEAD_DIM), BF16)),
        in_specs=[pl.BlockSpec((t, GROUP), lambda i: (i, 0)), pl.BlockSpec((t, GROUP), lambda i: (i, col))],
        out_specs=(pl.BlockSpec((N_HEADS, 1, t), lambda i: (0, 0, i)),
                   pl.BlockSpec((N_HEADS, t, HEAD_DIM), lambda i: (0, i, 0))),
        compiler_params=_params(),
    )(o, do)


def _attention_bwd(mode, q, k, v, kt, tab_t, dob, lse, delta):
    s, width = q.shape[1], q.shape[2]
    tq = min(ATT_TQ, s)
    tk = tq
    nq = s // tq
    nb = tab_t.shape[0]

    def body(q_ref, k_ref, v_ref, kt_ref, tab_ref, dob_ref, lse_ref, delta_ref, dqt_ref, dk_ref, dv_ref):
        i = pl.program_id(0)

        @pl.when(i == 0)
        def _():
            dqt_ref[...] = jnp.zeros_like(dqt_ref)

        hi = nq if mode == "fox" else jnp.minimum((i * tk + tk - 1 + DIL_REACH) // tq + 1, nq)
        for h0 in range(0, N_HEADS, BWD_HEADS):
            heads = range(h0, h0 + BWD_HEADS)

            def step(c, carry, heads=heads):
                q0 = pl.multiple_of(c * tq, tq)
                qs = pl.ds(q0, tq)
                tab = tab_ref[jnp.minimum(c - i, nb - 1)]
                qv = [q_ref[h, qs, :] for h in heads]
                dov = [dob_ref[h, qs, :] for h in heads]
                sc = [lax.dot_general(k_ref[h], qv[n], (NT, ((), ())), preferred_element_type=F32)
                      for n, h in enumerate(heads)]
                dp = [lax.dot_general(v_ref[h], dov[n], (NT, ((), ())), preferred_element_type=F32)
                      for n, h in enumerate(heads)]
                pb, dsb = [], []
                for n, h in enumerate(heads):
                    p = jnp.exp(sc[n] + tab - lse_ref[h, :, qs])
                    pb.append(p.astype(BF16))
                    dsb.append((p * (dp[n] - delta_ref[h, :, qs])).astype(BF16))
                new = []
                for n, h in enumerate(heads):
                    new += [carry[2 * n] + jnp.dot(dsb[n], qv[n], preferred_element_type=F32),
                            carry[2 * n + 1] + jnp.dot(pb[n], dov[n], preferred_element_type=F32)]
                for n, h in enumerate(heads):
                    dqt_ref[h, :, qs] += jnp.dot(kt_ref[h], dsb[n], preferred_element_type=F32)
                return tuple(new)

            start = (jnp.zeros((tk, width), F32), jnp.zeros((tk, HEAD_DIM), F32))
            done = lax.fori_loop(i, hi, step, start * BWD_HEADS)
            for n, h in enumerate(heads):
                dk_ref[h] = done[2 * n]
                dv_ref[h] = done[2 * n + 1]

    def full(shape):
        return pl.BlockSpec(shape, lambda i: (0, 0, 0))

    kblk = pl.BlockSpec((N_HEADS, tk, width), lambda i: (0, i, 0))
    vblk = pl.BlockSpec((N_HEADS, tk, HEAD_DIM), lambda i: (0, i, 0))
    return pl.pallas_call(
        body, name="attention_bwd_" + mode, grid=(s // tk,),
        out_shape=(jax.ShapeDtypeStruct((N_HEADS, KT_ROWS, s), F32), jax.ShapeDtypeStruct((N_HEADS, s, width), F32),
                   jax.ShapeDtypeStruct((N_HEADS, s, HEAD_DIM), F32)),
        in_specs=[full((N_HEADS, s, width)), kblk, vblk, pl.BlockSpec((N_HEADS, KT_ROWS, tk), lambda i: (0, 0, i)),
                  full((nb, tk, tq)), full((N_HEADS, s, HEAD_DIM)), full((N_HEADS, 1, s)), full((N_HEADS, 1, s))],
        out_specs=(full((N_HEADS, KT_ROWS, s)), kblk, vblk),
        compiler_params=_params(),
    )(q, k, v, kt, tab_t, dob, lse, delta)


def _xattn_fwd(qx, kvm):
    s = qx.shape[0]
    t = min(ROW_TILE, s)

    def body(q_ref, kv_ref, o_ref):
        heads = range(XA_HEADS)
        sc = [lax.dot_general(q_ref[:, h * XA_DIM:(h + 1) * XA_DIM].astype(BF16), kv_ref[h].astype(BF16),
                              (NT, ((), ())), preferred_element_type=F32) * (XA_DIM ** -0.5) for h in heads]
        probs = []
        for h in heads:
            e = jnp.exp(sc[h] - jnp.max(sc[h], axis=-1, keepdims=True))
            probs.append((e / jnp.sum(e, axis=-1, keepdims=True)).astype(BF16))
        outs = [jnp.dot(probs[h], kv_ref[XA_HEADS + h].astype(BF16), preferred_element_type=F32) for h in heads]
        for h in heads:
            o_ref[:, h * XA_DIM:(h + 1) * XA_DIM] = outs[h].astype(o_ref.dtype)

    return pl.pallas_call(
        body, name="xattn_fwd", grid=(s // t,), out_shape=jax.ShapeDtypeStruct((s, D_MODEL), BF16),
        in_specs=[pl.BlockSpec((t, D_MODEL), lambda i: (i, 0)),
                  pl.BlockSpec((2 * XA_HEADS, MEM_LEN, XA_DIM), lambda i: (0, 0, 0))],
        out_specs=pl.BlockSpec((t, D_MODEL), lambda i: (i, 0)), compiler_params=_params(),
    )(qx, kvm)


def _xattn_bwd(qx, kvm, do):
    s = qx.shape[0]
    t = min(ROW_TILE, s)

    def body(q_ref, kv_ref, do_ref, dq_ref, dkv_ref):
        i = pl.program_id(0)
        heads = range(XA_HEADS)
        qv = [q_ref[:, h * XA_DIM:(h + 1) * XA_DIM].astype(BF16) for h in heads]
        dov = [do_ref[:, h * XA_DIM:(h + 1) * XA_DIM].astype(BF16) for h in heads]
        kv = [kv_ref[h].astype(BF16) for h in heads]
        sc = [lax.dot_general(qv[h], kv[h], (NT, ((), ())), preferred_element_type=F32) * (XA_DIM ** -0.5)
              for h in heads]
        dp = [lax.dot_general(dov[h], kv_ref[XA_HEADS + h].astype(BF16), (NT, ((), ())), preferred_element_type=F32)
              for h in heads]
        pb, ds = [], []
        for h in heads:
            e = jnp.exp(sc[h] - jnp.max(sc[h], axis=-1, keepdims=True))
            p = e / jnp.sum(e, axis=-1, keepdims=True)
            pb.append(p.astype(BF16))
            ds.append((p * (dp[h] - jnp.sum(p * dp[h], axis=-1, keepdims=True)) * (XA_DIM ** -0.5)).astype(BF16))
        dq = [jnp.dot(ds[h], kv[h], preferred_element_type=F32) for h in heads]
        dk = [lax.dot_general(ds[h], qv[h], (TN, ((), ())), preferred_element_type=F32) for h in heads]
        dv = [lax.dot_general(pb[h], dov[h], (TN, ((), ())), preferred_element_type=F32) for h in heads]
        for h in heads:
            dq_ref[:, h * XA_DIM:(h + 1) * XA_DIM] = dq[h].astype(dq_ref.dtype)

        @pl.when(i == 0)
        def _():
            for h in heads:
                dkv_ref[h] = dk[h]
                dkv_ref[XA_HEADS + h] = dv[h]

        @pl.when(i > 0)
        def _():
            for h in heads:
                dkv_ref[h] += dk[h]
                dkv_ref[XA_HEADS + h] += dv[h]

    row = pl.BlockSpec((t, D_MODEL), lambda i: (i, 0))
    kvs = pl.BlockSpec((2 * XA_HEADS, MEM_LEN, XA_DIM), lambda i: (0, 0, 0))
    return pl.pallas_call(
        body, name="xattn_bwd", grid=(s // t,),
        out_shape=(jax.ShapeDtypeStruct((s, D_MODEL), BF16),
                   jax.ShapeDtypeStruct((2 * XA_HEADS, MEM_LEN, XA_DIM), F32)),
        in_specs=[row, kvs, row], out_specs=(row, kvs), compiler_params=_params(),
    )(qx, kvm, do)


def _adamw(parts, owns, me, w, m, v, name):
    nl, r, c = w.shape
    tr = r
    for cand in (256, 128, 64, 32, 16, 8):
        if r % cand == 0 and r > cand and N_DEV * cand * c * 4 <= ADAMW_BLOCK_BYTES:
            tr = cand
            break
    nt = r // tr
    per_layer = N_DEV + (1 if owns is not None else 0)

    def body(me_ref, *refs):
        w_ref, m_ref, v_ref, g_ref, d_ref, nm_ref, nv_ref = refs[nl * per_layer:]
        layer = pl.program_id(0)
        g = None
        for l in range(nl):
            p_refs = refs[l * per_layer:(l + 1) * per_layer]
            gl = None
            for d in range(N_DEV):
                term = p_refs[d][...].astype(F32)
                if owns is not None:
                    term = jnp.where(me_ref[0] == d, p_refs[N_DEV][...].astype(F32), term)
                gl = term if gl is None else gl + term
            g = gl if g is None else jnp.where(layer == l, gl, g)
        mn = ADAM_B1 * m_ref[...] + (1.0 - ADAM_B1) * g
        vn = ADAM_B2 * v_ref[...] + (1.0 - ADAM_B2) * (g * g)
        m_hat = mn / (1.0 - ADAM_B1 ** ADAM_STEP)
        v_hat = vn / (1.0 - ADAM_B2 ** ADAM_STEP)
        g_ref[...] = g
        d_ref[...] = -ADAM_LR * (m_hat / (jnp.sqrt(v_hat) + ADAM_EPS) + ADAM_WD * w_ref[...])
        nm_ref[...] = mn
        nv_ref[...] = vn

    def rows(l, ll, i):
        return jnp.where(ll == l, i, jnp.where(ll < l, 0, nt - 1))

    def part_spec(l, d):
        if owns is None:
            return pl.BlockSpec((None, tr, c), lambda ll, i, me_ref: (d, rows(l, ll, i), 0))
        return pl.BlockSpec((None, tr, c),
                            lambda ll, i, me_ref: (jnp.where(me_ref[0] == d, (d + 1) % N_DEV, d), rows(l, ll, i), 0))

    def own_spec(l):
        return pl.BlockSpec((None, tr, c), lambda ll, i, me_ref: (me_ref[0], rows(l, ll, i), 0))

    in_specs, operands = [], []
    for l in range(nl):
        in_specs += [part_spec(l, d) for d in range(N_DEV)]
        operands += [parts[l]] * N_DEV
        if owns is not None:
            in_specs.append(own_spec(l))
            operands.append(owns[l])
    blk = pl.BlockSpec((None, tr, c), lambda ll, i, me_ref: (ll, i, 0))
    shp = jax.ShapeDtypeStruct((nl, r, c), F32)
    return pl.pallas_call(
        body, name=name, out_shape=(shp, shp, shp, shp),
        grid_spec=pltpu.PrefetchScalarGridSpec(
            num_scalar_prefetch=1, grid=(nl, nt), in_specs=in_specs + [blk, blk, blk],
            out_specs=(blk, blk, blk, blk)),
        compiler_params=_params(),
    )(me.reshape(1), *operands, w, m, v)


GROUPS = {"in": ("w_in",), "rest": ("w_out", "w_xq", "w_xo", "w_xkv", "w_up", "w_down")}
FULL_SHAPES = {"w_in": (D_MODEL, N_IN_PAD), "w_out": (D_MODEL, D_MODEL), "w_xq": (D_MODEL, D_MODEL),
               "w_xo": (D_MODEL, D_MODEL), "w_xkv": (N_DEV, D_MODEL, 2 * D_MODEL // N_DEV),
               "w_up": (N_DEV, D_MODEL, FF_SHARD), "w_down": (FF_HALF, FF_SHARD, D_MODEL)}
PIECE_SHAPES = {"w_in": (N_DEV, D_MODEL // N_DEV, N_IN_PAD), "w_out": (N_DEV, D_MODEL // N_DEV, D_MODEL),
                "w_xq": (N_DEV, D_MODEL // N_DEV, D_MODEL), "w_xo": (N_DEV, D_MODEL // N_DEV, D_MODEL),
                "w_xkv": (N_DEV, D_MODEL, 2 * D_MODEL // N_DEV), "w_up": (N_DEV, D_MODEL, FF_SHARD),
                "w_down": (N_DEV, D_FF // N_DEV, D_MODEL)}
CONV_WORDS = 8192


class _GatheredWeights:
    def __init__(self, states, layer):
        self.states, self.layer, self.full, self.extra = dict(states), layer, {}, None

    def need(self, group, after):
        if group in self.states:
            got, _ = _exchange_wait(self.states.pop(group), after, "gather_%s_wait_%d" % (group, self.layer))
            for name, g in zip(GROUPS[group], got):
                self.full[name] = g.reshape(FULL_SHAPES[name])
            self.extra = got[len(GROUPS[group]):]

    def __getitem__(self, name):
        return self.full[name]


def _relay_in_cols(w):
    pad = jnp.zeros(w.shape[:-1] + (N_IN_PAD - N_IN,), w.dtype)
    return jnp.concatenate([w[..., :2304], w[..., 2308:N_IN], w[..., 2304:2308], pad], axis=-1)


def _unrelay_in_cols(w):
    return jnp.concatenate([w[..., :2304], w[..., COL_GATE:COL_GATE + 4], w[..., 2304:COL_GATE]], axis=-1)


def _layer_fwd(h, memv, w, sm, tables):
    sv = {"h0": h}
    s = h.shape[0]
    tm, tb = min(ROW_TILE, s), min(MM_TILE, s)
    w.need("in", h)
    tn = N_IN_PAD // 3
    proj, xn = _norm_matmul(h, sm["g_mix"], w["w_in"], (s, N_IN_PAD), grid=(s // tb, 3),
                            b_spec=pl.BlockSpec((D_MODEL, tn), lambda i, j: (0, j)),
                            o_spec=pl.BlockSpec((tb, tn), lambda i, j: (i, j)), name="norm_mm_in")
    sv["xn"], sv["proj"] = xn, proj
    ycat = _sconv_fwd(proj, sm["w_sconv"])
    qd, kd, vd, ktd, vtd = _heads_split(proj, 1, tables["rope"], None, "split_dil")
    ycat, ob, lse_b = _attention_fwd("dil", qd, kd, vtd, tables["dil"], ycat, 1)
    sv["dil"] = (qd, kd, vd, ktd, ob, lse_b)
    c = _gate_cumsum(proj, sm["b_forget_pad"])
    qf, kf, vf, ktf, vtf = _heads_split(proj, 2, None, c, "split_fox")
    ycat, oc, lse_c = _attention_fwd("fox", qf, kf, vtf, tables["fox"], ycat, 2)
    sv["fox"] = (qf, kf, vf, ktf, oc, lse_c)
    ycat = _pool_fwd(proj, sm["w_pool_bd"], sm["pool_scale"], ycat)
    sv["ycat"] = ycat
    w.need("rest", ycat)
    h1 = _mm_nn(ycat, w["w_out"], "mm_out", res=h)
    sv["h1"] = h1
    memn = _rms_fwd(memv, sm["g_mem"], "rms_mem")
    qx, xq = _norm_matmul(h1, sm["g_xa"], w["w_xq"], (s, D_MODEL), grid=(s // tb, 1),
                          b_spec=pl.BlockSpec((D_MODEL, D_MODEL), lambda i, j: (0, 0)),
                          o_spec=pl.BlockSpec((tb, D_MODEL), lambda i, j: (i, 0)), name="norm_mm_xq",
                          out_dtype=BF16)
    kvm = _matmul(memn, w["w_xkv"], (N_DEV, MEM_LEN, XA_DIM), grid=(N_DEV, 1, 1),
                  a_spec=pl.BlockSpec((MEM_LEN, D_MODEL), lambda i, j, r: (0, 0)),
                  b_spec=pl.BlockSpec((None, D_MODEL, XA_DIM), lambda i, j, r: (i, 0, 0)),
                  o_spec=pl.BlockSpec((None, MEM_LEN, XA_DIM), lambda i, j, r: (i, 0, 0)),
                  dims=NN, nred=1, name="mm_xkv")
    ox = _xattn_fwd(qx, kvm)
    sv.update(xq=xq, memn=memn, qx=qx, kvm=kvm, ox=ox)
    h2 = _mm_nn(ox, w["w_xo"], "mm_xo", res=h1)
    sv["h2"] = h2
    u0, xf = _norm_matmul(h2, sm["g_ffn"], w["w_up"], (N_DEV, s, FF_SHARD), grid=(s // tb, N_DEV),
                          b_spec=pl.BlockSpec((None, D_MODEL, FF_SHARD), lambda i, j: (j, 0, 0)),
                          o_spec=pl.BlockSpec((None, tb, FF_SHARD), lambda i, j: (j, i, 0)), name="norm_mm_up",
                          out_dtype=BF16)
    act = _ffn_gate_fwd(u0, sm["w_ffconv"])
    sv.update(xf=xf, u0=u0, act=act)
    ospec = pl.BlockSpec((tm, D_MODEL), lambda i, j, r: (i, 0))
    h3 = _matmul(act, w["w_down"], (s, D_MODEL), grid=(s // tm, 1, 1),
                 a_spec=pl.BlockSpec((FF_HALF, tm, FF_SHARD), lambda i, j, r: (0, i, 0)),
                 b_spec=pl.BlockSpec((FF_HALF, FF_SHARD, D_MODEL), lambda i, j, r: (0, 0, 0)),
                 o_spec=ospec, dims=NN, nred=1, slabs=FF_HALF, name="mm_down", res=h2, res_spec=ospec)
    return h3, sv


def _layer_bwd(dh3, memv, w, sm, tables, sv, rest_ready):
    s = dh3.shape[0]
    tm, tb = min(ROW_TILE, s), min(MM_TILE, s)
    big, small = {}, {}
    ts = max(s // 2, 1)
    dact = _matmul(dh3, w["w_down"], (FF_HALF, s, FF_SHARD), grid=(s // tb, FF_HALF, 1),
                   a_spec=pl.BlockSpec((tb, D_MODEL), lambda i, j, r: (i, 0)),
                   b_spec=pl.BlockSpec((None, FF_SHARD, D_MODEL), lambda i, j, r: (j, 0, 0)),
                   o_spec=pl.BlockSpec((None, tb, FF_SHARD), lambda i, j, r: (j, i, 0)),
                   dims=NT, nred=1, name="mm_dact", out_dtype=BF16)
    big["w_down"] = _matmul(sv["act"], dh3, (FF_HALF, FF_SHARD, D_MODEL), grid=(FF_HALF, 1, s // ts),
                            a_spec=pl.BlockSpec((None, ts, FF_SHARD), lambda i, j, r: (i, r, 0)),
                            b_spec=pl.BlockSpec((ts, D_MODEL), lambda i, j, r: (r, 0)),
                            o_spec=pl.BlockSpec((None, FF_SHARD, D_MODEL), lambda i, j, r: (i, 0, 0)),
                            dims=TN, nred=s // ts, name="mm_dw_down", out_dtype=GRAD_DTYPE)
    du0, small["w_ffconv"] = _ffn_gate_bwd(sv["u0"], sm["w_ffconv"], dact)
    dh2, small["g_ffn"] = _matmul_rms_bwd(du0, w["w_up"], sv["h2"], sm["g_ffn"], dh3, "mm_dxf_rms_bwd",
                                          tm=ROW_TILE // 2)
    big["w_up"] = _matmul(sv["xf"], du0, (N_DEV, D_MODEL, FF_SHARD), grid=(N_DEV, 1, 1),
                          a_spec=pl.BlockSpec((s, D_MODEL), lambda i, j, r: (0, 0)),
                          b_spec=pl.BlockSpec((None, s, FF_SHARD), lambda i, j, r: (i, 0, 0)),
                          o_spec=pl.BlockSpec((None, D_MODEL, FF_SHARD), lambda i, j, r: (i, 0, 0)),
                          dims=TN, nred=1, name="mm_dw_up", out_dtype=GRAD_DTYPE)
    dox = _mm_nt(dh2, w["w_xo"], "mm_dox", out_dtype=BF16)
    big["w_xo"] = _mm_tn(sv["ox"], dh2, "mm_dw_xo")
    dqx, dkvm = _xattn_bwd(sv["qx"], sv["kvm"], dox)
    big["w_xq"] = _mm_tn(sv["xq"], dqx, "mm_dw_xq")
    big["w_xkv"] = _matmul(sv["memn"], dkvm, (N_DEV, D_MODEL, XA_DIM), grid=(N_DEV, 1, 1),
                           a_spec=pl.BlockSpec((MEM_LEN, D_MODEL), lambda i, j, r: (0, 0)),
                           b_spec=pl.BlockSpec((None, MEM_LEN, XA_DIM), lambda i, j, r: (i, 0, 0)),
                           o_spec=pl.BlockSpec((None, D_MODEL, XA_DIM), lambda i, j, r: (i, 0, 0)),
                           dims=TN, nred=1, name="mm_dw_xkv", out_dtype=GRAD_DTYPE)
    dmemn = _matmul(dkvm, w["w_xkv"], (MEM_LEN, D_MODEL), grid=(1, 1, 1),
                    a_spec=pl.BlockSpec((N_DEV, MEM_LEN, XA_DIM), lambda i, j, r: (0, 0, 0)),
                    b_spec=pl.BlockSpec((N_DEV, D_MODEL, XA_DIM), lambda i, j, r: (0, 0, 0)),
                    o_spec=pl.BlockSpec((MEM_LEN, D_MODEL), lambda i, j, r: (0, 0)),
                    dims=NT, nred=1, slabs=N_DEV, name="mm_dmemn")
    _, small["g_mem"] = _rms_bwd(dmemn, memv, sm["g_mem"], None, "rms_mem_bwd")
    dh1, small["g_xa"] = _matmul_rms_bwd(dqx, w["w_xq"], sv["h1"], sm["g_xa"], dh2, "mm_dxq_rms_bwd")
    big["w_out"] = _mm_tn(sv["ycat"], dh1, "mm_dw_out")
    dycat = _mm_nt(dh1, w["w_out"] + rest_ready(big, small).astype(BF16), "mm_dycat")
    proj = sv["proj"]
    dproj, small["w_sconv"] = _sconv_bwd(proj, sm["w_sconv"], dycat)
    qd, kd, vd, ktd, ob, lse_b = sv["dil"]
    delta, dob = _attention_delta(ob, dycat, 1)
    dqt, dk, dv = _attention_bwd("dil", qd, kd, vd, ktd, tables["dil"], dob, lse_b, delta)
    dproj = _heads_merge(dqt, dk, dv, tables["rope"], "merge_dil", dproj, 1)
    qf, kf, vf, ktf, oc, lse_c = sv["fox"]
    delta, dob = _attention_delta(oc, dycat, 2)
    dqt, dk, dv = _attention_bwd("fox", qf, kf, vf, ktf, tables["fox"], dob, lse_c, delta)
    dproj, dc = _heads_merge(dqt, dk, dv, None, "merge_fox", dproj, 2)
    dproj, dbias = _gate_cumsum_bwd(proj, sm["b_forget_pad"], dc, dproj)
    small["b_forget"] = dbias[0, :N_HEADS]
    dproj, dwbd, small["pool_scale"] = _pool_bwd(proj, sm["w_pool_bd"], sm["pool_scale"], dycat, dproj)
    small["w_pool"] = jnp.stack([dwbd[64 * g:64 * (g + 1), 64 * g:64 * (g + 1)] for g in range(4)])
    big["w_in"] = _mm_tn(sv["xn"], dproj, "mm_dw_in", tn=896)
    dh0, small["g_mix"] = _matmul_rms_bwd(dproj, w["w_in"], sv["h0"], sm["g_mix"], dh1, "mm_dxn_rms_bwd")
    return dh0, big, small


SMALL_NAMES = ("g_mix", "b_forget", "w_pool", "pool_scale", "g_xa", "g_mem", "g_ffn", "w_sconv", "w_ffconv")
SMALL_WITH = {"rest": ("w_ffconv", "g_ffn", "g_mem", "g_xa"),
              "in": ("w_sconv", "b_forget", "pool_scale", "w_pool", "g_mix")}
SMALL_SHAPES = {"w_sconv": (3, GROUP), "w_ffconv": (N_DEV, 3, FF_SHARD)}
WEIGHT_NAMES = ("g_mix", "w_in", "b_forget", "w_sconv", "w_pool", "pool_scale", "w_out", "g_xa", "g_mem", "w_xq",
                "w_xkv", "w_xo", "g_ffn", "w_up", "w_ffconv", "w_down", "g_final")


def _block_diag(w_pool):
    z = jnp.zeros((64, 64), F32)
    return jnp.concatenate(
        [jnp.concatenate([w_pool[g] if c == g else z for c in range(4)], axis=1) for g in range(4)], axis=0)


def kernel(x, mem, positions, g_mix, w_in, b_forget, w_sconv, w_pool, pool_scale, w_out, g_xa, g_mem, w_xq, w_xkv, w_xo, g_ffn, w_up, w_ffconv, w_down, g_final, loss_target, m_g_mix, m_w_in, m_b_forget, m_w_sconv, m_w_pool, m_pool_scale, m_w_out, m_g_xa, m_g_mem, m_w_xq, m_w_xkv, m_w_xo, m_g_ffn, m_w_up, m_w_ffconv, m_w_down, m_g_final, v_g_mix, v_w_in, v_b_forget, v_w_sconv, v_w_pool, v_pool_scale, v_w_out, v_g_xa, v_g_mem, v_w_xq, v_w_xkv, v_w_xo, v_g_ffn, v_w_up, v_w_ffconv, v_w_down, v_g_final):
    weights = dict(g_mix=g_mix, w_in=w_in, b_forget=b_forget, w_sconv=w_sconv, w_pool=w_pool, pool_scale=pool_scale,
                   w_out=w_out, g_xa=g_xa, g_mem=g_mem, w_xq=w_xq, w_xkv=w_xkv, w_xo=w_xo, g_ffn=g_ffn, w_up=w_up,
                   w_ffconv=w_ffconv, w_down=w_down, g_final=g_final)
    m_in = dict(g_mix=m_g_mix, w_in=m_w_in, b_forget=m_b_forget, w_sconv=m_w_sconv, w_pool=m_w_pool,
                pool_scale=m_pool_scale, w_out=m_w_out, g_xa=m_g_xa, g_mem=m_g_mem, w_xq=m_w_xq, w_xkv=m_w_xkv,
                w_xo=m_w_xo, g_ffn=m_g_ffn, w_up=m_w_up, w_ffconv=m_w_ffconv, w_down=m_w_down, g_final=m_g_final)
    v_in = dict(g_mix=v_g_mix, w_in=v_w_in, b_forget=v_b_forget, w_sconv=v_w_sconv, w_pool=v_w_pool,
                pool_scale=v_pool_scale, w_out=v_w_out, g_xa=v_g_xa, g_mem=v_g_mem, w_xq=v_w_xq, w_xkv=v_w_xkv,
                w_xo=v_w_xo, g_ffn=v_g_ffn, w_up=v_w_up, w_ffconv=v_w_ffconv, w_down=v_w_down, g_final=v_g_final)
    depth = w_in.shape[0]
    me = 4 * lax.axis_index("x") + 2 * lax.axis_index("y") + lax.axis_index("c")
    h = x[0]
    memv = mem[0]
    s = h.shape[0]
    tq = min(ATT_TQ, s)
    tables = {"rope": _rope_tables(positions[0]), "dil": _bias_tables("dil", tq, tq),
              "fox": _bias_tables("fox", tq, tq)}

    w_in_r = _relay_in_cols(w_in)
    conv_shard = jnp.concatenate([w_sconv.reshape(-1), w_ffconv.reshape(-1)])
    conv_shard = jnp.concatenate([conv_shard, jnp.zeros((CONV_WORDS - conv_shard.shape[0],), F32)])
    conv_bits = lax.bitcast_convert_type(conv_shard, BF16).reshape(2 * CONV_WORDS // 1024, 1024)
    gathered = []
    order = jnp.zeros((), F32)
    for l in range(depth):
        shards = dict(w_in=w_in_r[l], w_out=w_out[l], w_xq=w_xq[l], w_xo=w_xo[l], w_xkv=w_xkv[l], w_up=w_up[l],
                      w_down=w_down[l])
        states = {}
        for group in ("in", "rest"):
            shards[GROUPS[group][0]] = shards[GROUPS[group][0]] + order
            xs = [_place_shard(shards[name], me, BF16, "place_%s_%d" % (name, l)) for name in GROUPS[group]]
            if l == 0 and group == "in":
                xs.append(_place_shard(conv_bits, me, BF16, "place_conv"))
            states[group], token = _exchange_start(xs, False, "gather_%s_start_%d" % (group, l))
            order = order + token[0, 0]
        gathered.append(_GatheredWeights(states, l))
    gathered[0].need("in", tables["rope"][0])
    conv_all = lax.bitcast_convert_type(gathered[0].extra[0].reshape(N_DEV, CONV_WORDS, 2), F32)
    n_sc = depth * 3 * (GROUP // N_DEV)
    sconv_full = conv_all[:, :n_sc].reshape(N_DEV, depth, 3, GROUP // N_DEV).transpose(1, 2, 0, 3).reshape(
        depth, 3, GROUP)
    ffconv_full = conv_all[:, n_sc:n_sc + depth * 3 * FF_SHARD].reshape(N_DEV, depth, 3, FF_SHARD).transpose(
        1, 0, 2, 3)

    smalls = []
    for l in range(depth):
        smalls.append(dict(
            g_mix=g_mix[l], g_xa=g_xa[l], g_mem=g_mem[l], g_ffn=g_ffn[l], pool_scale=pool_scale[l],
            w_pool_bd=_block_diag(w_pool[l]), w_sconv=sconv_full[l], w_ffconv=ffconv_full[l],
            b_forget_pad=jnp.concatenate([b_forget[l], jnp.zeros((128 - N_HEADS,), F32)]).reshape(1, 128)))
    smalls[0]["g_mix"] = smalls[0]["g_mix"] + order

    saved = []
    for l in range(depth):
        h, sv = _layer_fwd(h, memv, gathered[l], smalls[l], tables)
        saved.append(sv)
    loss_part, dh, dg_final = _loss_head(h, g_final, loss_target[0])
    loss = lax.psum(loss_part[0, 0], MESH_AXES)

    small_grads = [None] * depth
    scatters = {}

    def pieces_of(big, group):
        return [big[name].reshape(PIECE_SHAPES[name]) for name in GROUPS[group]]

    def rider(grads):
        flat = jnp.concatenate([g.reshape(-1) for g in grads])
        rows = -(-flat.shape[0] // 1024)
        flat = jnp.concatenate([flat, jnp.zeros((rows * 1024 - flat.shape[0],), F32)])
        return jnp.broadcast_to(flat.reshape(1, rows, 1024), (N_DEV, rows, 1024))

    for l in reversed(range(depth)):
        def rest_ready(big, small, l=l):
            ready = [small[n] for n in SMALL_WITH["rest"]] + ([dg_final] if l == depth - 1 else [])
            scatters[l, "rest"], token = _exchange_start(pieces_of(big, "rest") + [rider(ready)], True,
                                                         "scatter_rest_start_%d" % l)
            return token[0, 0]

        dh, big, small_grads[l] = _layer_bwd(dh, memv, gathered[l], smalls[l], tables, saved[l], rest_ready)
        xs = pieces_of(big, "in") + [rider([small_grads[l][n] for n in SMALL_WITH["in"]])]
        scatters[l, "in"], token = _exchange_start(xs, True, "scatter_in_start_%d" % l)
        if l > 0:
            smalls[l - 1]["w_ffconv"] = smalls[l - 1]["w_ffconv"] + token[0, 0]
    grad_x = dh[None]

    parts, owns, small_parts = {}, {}, {}

    def wait_group(group, after):
        for l in reversed(range(depth)):
            got, given = _exchange_wait(scatters[l, group], after, "scatter_%s_wait_%d" % (group, l))
            for name, g, x in zip(GROUPS[group], got, given):
                parts.setdefault(name, [None] * depth)[l] = g
                owns.setdefault(name, [None] * depth)[l] = x
            flat = lax.dynamic_update_slice_in_dim(got[-1], given[-1][:1], me, axis=0).reshape(N_DEV, -1)
            off = 0
            for name in SMALL_WITH[group] + (("g_final",) if group == "rest" and l == depth - 1 else ()):
                shape = SMALL_SHAPES.get(name, weights[name].shape[-1:] if name == "g_final"
                                         else weights[name].shape[1:])
                n = 1
                for dim in shape:
                    n *= dim
                small_parts.setdefault(name, [None] * depth)[l] = flat[:, off:off + n].reshape((N_DEV,) + shape)
                off += n

    results = {}

    def update(name, w3, m3, v3):
        outs = _adamw(parts[name], owns.get(name), me, w3, m3, v3, "adamw_" + name)
        results[name] = [o.reshape(weights[name].shape) for o in outs]

    wait_group("rest", grad_x)
    for name in GROUPS["rest"]:
        update(name, weights[name], m_in[name], v_in[name])
    wait_group("in", results["w_down"][1])
    outs = _adamw(parts["w_in"], owns["w_in"], me, w_in_r, _relay_in_cols(m_w_in), _relay_in_cols(v_w_in),
                  "adamw_w_in")
    results["w_in"] = [_unrelay_in_cols(o) for o in outs]
    for name in SMALL_NAMES + ("g_final",):
        wv = weights[name]
        p = small_parts[name][depth - 1] if name == "g_final" else jnp.stack(small_parts[name], axis=1)
        if name == "w_sconv":
            p = lax.dynamic_slice_in_dim(p, me * (GROUP // N_DEV), GROUP // N_DEV, axis=3)
        elif name == "w_ffconv":
            p = lax.dynamic_index_in_dim(p, me, axis=2, keepdims=False)
        shape3 = (1, 1, wv.shape[0]) if wv.ndim == 1 else (1, -1, wv.shape[-1])
        w3 = wv.reshape(shape3)
        parts[name] = [p.reshape((N_DEV,) + w3.shape[1:])]
        update(name, w3, m_in[name].reshape(shape3), v_in[name].reshape(shape3))

    return (loss, grad_x, *[results[n][0] for n in WEIGHT_NAMES], *[results[n][1] for n in WEIGHT_NAMES],
            *[results[n][2] for n in WEIGHT_NAMES], *[results[n][3] for n in WEIGHT_NAMES])
```

```python
import functools

import jax
import jax.numpy as jnp
from jax import lax
from jax.experimental import pallas as pl
from jax.experimental.pallas import tpu as pltpu

F32 = jnp.float32
BF16 = jnp.bfloat16

N_DEV = 8
D_MODEL = 1024
GROUP = 256
HEAD_DIM = 64
N_HEADS = 4
N_IN = 2564
N_IN_PAD = 2688
COL_GATE = 2560
XA_HEADS = 4
XA_DIM = 256
MEM_LEN = 256
D_FF = 2816
FF_SHARD = 704
FF_HALF = 4
ROPE_THETA = 500000.0
ROPE_DIM = 16
RMS_EPS = 1e-6
NEG = -1e30
POOL_WINDOWS = (2, 4, 8, 16)
ADAM_LR, ADAM_B1, ADAM_B2, ADAM_EPS, ADAM_WD, ADAM_STEP = 0.001, 0.9, 0.999, 1e-08, 0.01, 10

ROW_TILE = 512
MM_TILE = 1024
ATT_TQ = 512
BWD_HEADS = 4
VMEM_LIMIT = 56 * 1024 * 1024
ADAMW_BLOCK_BYTES = 4 * 1024 * 1024
PLACE_BLOCK_BYTES = 4 * 1024 * 1024

MESH_AXES = ("x", "y", "c")


def _params(**kw):
    return pltpu.CompilerParams(vmem_limit_bytes=VMEM_LIMIT, **kw)


HBM_SPEC = pl.BlockSpec(memory_space=pltpu.HBM)
SEM_SPEC = pl.BlockSpec(memory_space=pltpu.SEMAPHORE)
DATAFLOW = pltpu.SideEffectType.DATAFLOW_SIDE_EFFECTING


def _peer_copies(x_ref, land_ref, send_sems, recv_sems, scatter):
    mx, my, mc = lax.axis_index("x"), lax.axis_index("y"), lax.axis_index("c")
    me = 4 * mx + 2 * my + mc
    pairs = []
    for k in range(1, N_DEV):
        kx, ky, kc = (k >> 2) & 1, (k >> 1) & 1, k & 1
        peer_lin = me ^ k
        send = pltpu.make_async_remote_copy(
            src_ref=x_ref.at[peer_lin] if scatter else land_ref.at[me], dst_ref=land_ref.at[me],
            send_sem=send_sems.at[k - 1], recv_sem=recv_sems.at[k - 1],
            device_id=(mx ^ kx, my ^ ky, mc ^ kc), device_id_type=pl.DeviceIdType.MESH)
        arrival = pltpu.make_async_remote_copy(
            src_ref=land_ref.at[peer_lin], dst_ref=land_ref.at[peer_lin],
            send_sem=send_sems.at[k - 1], recv_sem=recv_sems.at[k - 1],
            device_id=(mx, my, mc), device_id_type=pl.DeviceIdType.MESH)
        pairs.append((send, arrival))
    return pairs


def _exchange_start(xs, scatter, name):
    n = len(xs)
    ns = n if scatter else 0

    def body(*refs):
        srcs = refs[:ns] if scatter else (None,) * n
        lands, sends, recvs = refs[ns:ns + n], refs[ns + n:ns + 2 * n], refs[ns + 2 * n:ns + 3 * n]
        for t in range(n):
            for send, _ in _peer_copies(srcs[t], lands[t], sends[t], recvs[t], scatter):
                send.start()
        token = refs[-1]
        token[...] = jnp.zeros_like(token)

    sems = pltpu.SemaphoreType.DMA((N_DEV - 1,))
    operands = [pltpu.with_memory_space_constraint(x, pltpu.HBM) for x in xs]
    if scatter:
        operands += [pltpu.with_memory_space_constraint(lax.empty(x.shape, x.dtype), pltpu.HBM) for x in xs]
    outs = pl.pallas_call(
        body, name=name,
        out_shape=(sems,) * (2 * n) + tuple(pltpu.HBM(a.shape, a.dtype) for a in operands)
        + (jax.ShapeDtypeStruct((8, 128), F32),),
        in_specs=(HBM_SPEC,) * (ns + n),
        out_specs=(SEM_SPEC,) * (2 * n) + (HBM_SPEC,) * (ns + n) + (pl.BlockSpec(memory_space=pltpu.VMEM),),
        input_output_aliases={i: 2 * n + i for i in range(ns + n)},
        compiler_params=pltpu.CompilerParams(has_side_effects=DATAFLOW),
    )(*operands)
    return (outs[:-1], scatter), outs[-1]


def _exchange_wait(state, after, name):
    held, scatter = state
    n = len(held) // (4 if scatter else 3)
    ns = n if scatter else 0
    sems, thru = held[:2 * n], held[2 * n:]

    def body(*refs):
        srcs = refs[:ns] if scatter else (None,) * n
        lands, sends, recvs = refs[ns:ns + n], refs[ns + n:ns + 2 * n], refs[ns + 2 * n:ns + 3 * n]
        for t in range(n):
            for send, arrival in _peer_copies(srcs[t], lands[t], sends[t], recvs[t], scatter):
                send.wait_send()
                arrival.wait_recv()

    outs = pl.pallas_call(
        body, name=name,
        out_shape=tuple(pltpu.HBM(a.shape, a.dtype) for a in thru),
        in_specs=(HBM_SPEC,) * (ns + n) + (SEM_SPEC,) * (2 * n) + (pl.BlockSpec(memory_space=pl.ANY),),
        out_specs=(HBM_SPEC,) * (ns + n), input_output_aliases={i: i for i in range(ns + n)},
        compiler_params=pltpu.CompilerParams(has_side_effects=DATAFLOW),
    )(*thru, *sems, after)
    return list(outs[ns:]), list(outs[:ns])


def _place_shard(x, me, dtype, name):
    r, c = x.shape
    tr = r
    if r * c * 4 > PLACE_BLOCK_BYTES:
        for cand in (512, 256, 128, 64, 32, 16):
            if r % cand == 0 and cand * c * 4 <= PLACE_BLOCK_BYTES:
                tr = cand
                break

    def body(me_ref, x_ref, o_ref):
        o_ref[...] = x_ref[...].astype(o_ref.dtype)

    return pl.pallas_call(
        body, name=name, out_shape=jax.ShapeDtypeStruct((N_DEV, r, c), dtype),
        grid_spec=pltpu.PrefetchScalarGridSpec(
            num_scalar_prefetch=1, grid=(r // tr,),
            in_specs=[pl.BlockSpec((tr, c), lambda i, me_ref: (i, 0))],
            out_specs=pl.BlockSpec((None, tr, c), lambda i, me_ref: (me_ref[0], i, 0))),
        compiler_params=_params(),
    )(me.reshape(1), x)


NN = ((1,), (0,))
NT = ((1,), (1,))
TN = ((0,), (0,))


def _matmul(a, b, out_shape, *, grid, a_spec, b_spec, o_spec, dims, nred, name, res=None, res_spec=None,
            out_dtype=F32, slabs=0):
    has_res = res is not None

    def body(*refs):
        a_ref, b_ref = refs[0], refs[1]
        r_ref = refs[2] if has_res else None
        o_ref = refs[3] if has_res else refs[2]
        if slabs:
            part = None
            for n in range(slabs):
                term = lax.dot_general(a_ref[n].astype(BF16), b_ref[n].astype(BF16), (dims, ((), ())),
                                       preferred_element_type=F32)
                part = term if part is None else part + term
        else:
            part = lax.dot_general(a_ref[...].astype(BF16), b_ref[...].astype(BF16), (dims, ((), ())),
                                   preferred_element_type=F32)
        if nred == 1:
            if has_res:
                part = part + r_ref[...]
            o_ref[...] = part.astype(o_ref.dtype)
        else:
            acc = refs[-1]
            r = pl.program_id(2)

            @pl.when(r == 0)
            def _():
                acc[...] = part

            @pl.when(r > 0)
            def _():
                acc[...] += part

            @pl.when(r == nred - 1)
            def _():
                tot = acc[...]
                if has_res:
                    tot = tot + r_ref[...]
                o_ref[...] = tot.astype(o_ref.dtype)

    in_specs = [a_spec, b_spec] + ([res_spec] if has_res else [])
    args = (a, b) + ((res,) if has_res else ())
    acc_shape = tuple(d for d in o_spec.block_shape if d is not None)
    return pl.pallas_call(
        body, name=name, grid=grid, out_shape=jax.ShapeDtypeStruct(out_shape, out_dtype),
        in_specs=in_specs, out_specs=o_spec,
        scratch_shapes=[pltpu.VMEM(acc_shape, F32)] if nred > 1 else [],
        compiler_params=_params(),
    )(*args)


def _mm_nn(a, w, name, res=None, tn=None, out_dtype=F32):
    m, k = a.shape
    n = w.shape[1]
    tn = tn or n
    tm = min(MM_TILE, m)
    ospec = pl.BlockSpec((tm, tn), lambda i, j, r: (i, j))
    return _matmul(a, w, (m, n), grid=(m // tm, n // tn, 1),
                   a_spec=pl.BlockSpec((tm, k), lambda i, j, r: (i, 0)),
                   b_spec=pl.BlockSpec((k, tn), lambda i, j, r: (0, j)),
                   o_spec=ospec, dims=NN, nred=1, name=name, res=res, res_spec=ospec if res is not None else None,
                   out_dtype=out_dtype)


def _mm_nt(a, w, name, out_dtype=F32):
    m, n = a.shape
    k = w.shape[0]
    tm = min(MM_TILE, m)
    return _matmul(a, w, (m, k), grid=(m // tm, 1, 1),
                   a_spec=pl.BlockSpec((tm, n), lambda i, j, r: (i, 0)),
                   b_spec=pl.BlockSpec((k, n), lambda i, j, r: (0, 0)),
                   o_spec=pl.BlockSpec((tm, k), lambda i, j, r: (i, 0)), dims=NT, nred=1, name=name,
                   out_dtype=out_dtype)


def _norm_matmul(h, g, b, out_shape, *, grid, b_spec, o_spec, name, out_dtype=F32):
    s, d = h.shape
    tm = s // grid[0]

    def body(h_ref, g_ref, b_ref, o_ref, xn_ref):
        @pl.when(pl.program_id(1) == 0)
        def _():
            hv = h_ref[...]
            r = lax.rsqrt(jnp.mean(hv * hv, axis=-1, keepdims=True) + RMS_EPS)
            xn_ref[...] = (hv * r * g_ref[...]).astype(xn_ref.dtype)

        o_ref[...] = jnp.dot(xn_ref[...], b_ref[...].astype(BF16), preferred_element_type=F32).astype(o_ref.dtype)

    row = pl.BlockSpec((tm, d), lambda i, j: (i, 0))
    return pl.pallas_call(
        body, name=name, grid=grid,
        out_shape=(jax.ShapeDtypeStruct(out_shape, out_dtype), jax.ShapeDtypeStruct((s, d), BF16)),
        in_specs=[row, pl.BlockSpec((1, d), lambda i, j: (0, 0)), b_spec],
        out_specs=(o_spec, row), compiler_params=_params(),
    )(h, g.reshape(1, d), b)


def _matmul_rms_bwd(a, w, h, g, res, name, tm=ROW_TILE):
    slabs = a.shape[0] if a.ndim == 3 else 0
    s, n = a.shape[-2:]
    d = w.shape[-2]
    tm = min(tm, s)

    def body(a_ref, w_ref, h_ref, g_ref, r_ref, dh_ref, dg_ref):
        if slabs:
            dy = None
            for j in range(slabs):
                term = lax.dot_general(a_ref[j].astype(BF16), w_ref[j].astype(BF16), (NT, ((), ())),
                                       preferred_element_type=F32)
                dy = term if dy is None else dy + term
        else:
            dy = lax.dot_general(a_ref[...].astype(BF16), w_ref[...].astype(BF16), (NT, ((), ())),
                                 preferred_element_type=F32)
        hv = h_ref[...]
        r = lax.rsqrt(jnp.mean(hv * hv, axis=-1, keepdims=True) + RMS_EPS)
        hn = hv * r
        u = dy * g_ref[...]
        dh_ref[...] = r * (u - hn * jnp.mean(u * hn, axis=-1, keepdims=True)) + r_ref[...]
        part = jnp.sum(dy * hn, axis=0, keepdims=True)

        @pl.when(pl.program_id(0) == 0)
        def _():
            dg_ref[...] = part

        @pl.when(pl.program_id(0) > 0)
        def _():
            dg_ref[...] += part

    row = pl.BlockSpec((tm, d), lambda i: (i, 0))
    vec = pl.BlockSpec((1, d), lambda i: (0, 0))
    if slabs:
        a_spec = pl.BlockSpec((slabs, tm, n), lambda i: (0, i, 0))
        w_spec = pl.BlockSpec((slabs, d, n), lambda i: (0, 0, 0))
    else:
        a_spec = pl.BlockSpec((tm, n), lambda i: (i, 0))
        w_spec = pl.BlockSpec((d, n), lambda i: (0, 0))
    dh, dg = pl.pallas_call(
        body, name=name, grid=(s // tm,),
        out_shape=(jax.ShapeDtypeStruct((s, d), F32), jax.ShapeDtypeStruct((1, d), F32)),
        in_specs=[a_spec, w_spec, row, vec, row], out_specs=(row, vec), compiler_params=_params(),
    )(a, w, h, g.reshape(1, d), res)
    return dh, dg.reshape(d)


GRAD_DTYPE = BF16


def _mm_tn(a, b, name, tk=512, tn=None):
    s, k = a.shape
    n = b.shape[1]
    tn = tn or n
    tk = min(tk, k)
    ts = s if b.dtype == BF16 else max(s // 2, 1)
    return _matmul(a, b, (k, n), grid=(k // tk, n // tn, s // ts),
                   a_spec=pl.BlockSpec((ts, tk), lambda i, j, r: (r, i)),
                   b_spec=pl.BlockSpec((ts, tn), lambda i, j, r: (r, j)),
                   o_spec=pl.BlockSpec((tk, tn), lambda i, j, r: (i, j)), dims=TN, nred=s // ts, name=name,
                   out_dtype=GRAD_DTYPE)


def _rms_fwd(h, g, name):
    s, d = h.shape
    tm = min(ROW_TILE, s)

    def body(h_ref, g_ref, o_ref):
        hv = h_ref[...]
        r = lax.rsqrt(jnp.mean(hv * hv, axis=-1, keepdims=True) + RMS_EPS)
        o_ref[...] = (hv * r * g_ref[...]).astype(o_ref.dtype)

    return pl.pallas_call(
        body, name=name, grid=(s // tm,), out_shape=jax.ShapeDtypeStruct((s, d), BF16),
        in_specs=[pl.BlockSpec((tm, d), lambda i: (i, 0)), pl.BlockSpec((1, d), lambda i: (0, 0))],
        out_specs=pl.BlockSpec((tm, d), lambda i: (i, 0)), compiler_params=_params(),
    )(h, g.reshape(1, d))


def _rms_bwd(dy, h, g, res, name):
    s, d = h.shape
    tm = min(ROW_TILE, s)
    has_res = res is not None

    def body(*refs):
        dy_ref, h_ref, g_ref = refs[:3]
        r_ref = refs[3] if has_res else None
        dh_ref, dg_ref = refs[-2], refs[-1]
        hv = h_ref[...]
        r = lax.rsqrt(jnp.mean(hv * hv, axis=-1, keepdims=True) + RMS_EPS)
        hn = hv * r
        dyv = dy_ref[...].astype(F32)
        u = dyv * g_ref[...]
        dh = r * (u - hn * jnp.mean(u * hn, axis=-1, keepdims=True))
        if has_res:
            dh = dh + r_ref[...]
        dh_ref[...] = dh
        part = jnp.sum(dyv * hn, axis=0, keepdims=True)

        @pl.when(pl.program_id(0) == 0)
        def _():
            dg_ref[...] = part

        @pl.when(pl.program_id(0) > 0)
        def _():
            dg_ref[...] += part

    row = pl.BlockSpec((tm, d), lambda i: (i, 0))
    vec = pl.BlockSpec((1, d), lambda i: (0, 0))
    dh, dg = pl.pallas_call(
        body, name=name, grid=(s // tm,),
        out_shape=(jax.ShapeDtypeStruct((s, d), F32), jax.ShapeDtypeStruct((1, d), F32)),
        in_specs=[row, row, vec] + ([row] if has_res else []),
        out_specs=(row, vec), compiler_params=_params(),
    )(*((dy, h, g.reshape(1, d)) + ((res,) if has_res else ())))
    return dh, dg.reshape(d)


def _loss_head(h, g, target):
    s, d = h.shape
    tm = min(ROW_TILE, s)

    def body(h_ref, g_ref, t_ref, loss_ref, dh_ref, dg_ref):
        hv = h_ref[...]
        r = lax.rsqrt(jnp.mean(hv * hv, axis=-1, keepdims=True) + RMS_EPS)
        hn = hv * r
        gv = g_ref[...]
        err = hn * gv - t_ref[...]
        rows = jnp.mean(err * err, axis=-1, keepdims=True)
        lpart = 0.5 * jnp.sum(rows, axis=0, keepdims=True) + jnp.zeros((1, 128), F32)
        dy = err * (1.0 / d)
        u = dy * gv
        dh_ref[...] = r * (u - hn * jnp.mean(u * hn, axis=-1, keepdims=True))
        gpart = jnp.sum(dy * hn, axis=0, keepdims=True)

        @pl.when(pl.program_id(0) == 0)
        def _():
            dg_ref[...] = gpart
            loss_ref[...] = lpart

        @pl.when(pl.program_id(0) > 0)
        def _():
            dg_ref[...] += gpart
            loss_ref[...] += lpart

    row = pl.BlockSpec((tm, d), lambda i: (i, 0))
    vec = pl.BlockSpec((1, d), lambda i: (0, 0))
    return pl.pallas_call(
        body, name="loss_head", grid=(s // tm,),
        out_shape=(jax.ShapeDtypeStruct((1, 128), F32), jax.ShapeDtypeStruct((s, d), F32),
                   jax.ShapeDtypeStruct((1, d), F32)),
        in_specs=[row, vec, row],
        out_specs=(pl.BlockSpec((1, 128), lambda i: (0, 0)), row, vec), compiler_params=_params(),
    )(h, g.reshape(1, d), target)


def _shift_down(x, k):
    return pltpu.roll(x, k, 0)


def _shift_up(x, k):
    return pltpu.roll(x, x.shape[0] - k, 0)


def _conv3(x, w):
    return w[2:3, :] * x + w[1:2, :] * _shift_down(x, 1) + w[0:1, :] * _shift_down(x, 2)


def _conv3_t(x, w):
    return w[2:3, :] * x + w[1:2, :] * _shift_up(x, 1) + w[0:1, :] * _shift_up(x, 2)


def _sigmoid(x):
    return 1.0 / (1.0 + jnp.exp(-x))


def _prev_map(tile, halo, col):
    return lambda i: (jnp.maximum(i * (tile // halo) - 1, 0), col)


def _next_map(tile, halo, col, nrows):
    return lambda i: (jnp.minimum((i + 1) * (tile // halo), nrows // halo - 1), col)


def _sconv_fwd(proj, w):
    s = proj.shape[0]
    t = min(ROW_TILE, s)

    def body(cur_ref, prev_ref, w_ref, o_ref):
        i = pl.program_id(0)
        prev = prev_ref[...] * (i > 0).astype(F32)
        ext = jnp.concatenate([prev, cur_ref[...]], axis=0)
        sv = ext[:, 2 * GROUP:3 * GROUP] * ext[:, 0:GROUP]
        y = ext[:, GROUP:2 * GROUP] * _conv3(sv, w_ref[...])
        o_ref[...] = y[8:].astype(o_ref.dtype)

    return pl.pallas_call(
        body, name="sconv_fwd", grid=(s // t,), out_shape=jax.ShapeDtypeStruct((s, 4 * GROUP), BF16),
        in_specs=[pl.BlockSpec((t, 3 * GROUP), lambda i: (i, 0)),
                  pl.BlockSpec((8, 3 * GROUP), _prev_map(t, 8, 0)),
                  pl.BlockSpec((3, GROUP), lambda i: (0, 0))],
        out_specs=pl.BlockSpec((t, GROUP), lambda i: (i, 0)), compiler_params=_params(),
    )(proj, proj, w)


def _sconv_bwd(proj, w, dy):
    s = proj.shape[0]
    t = min(ROW_TILE, s)
    nt = s // t

    def body(cur_ref, prev_ref, next_ref, w_ref, dy_ref, dyn_ref, dp_ref, dw_ref):
        i = pl.program_id(0)
        first = (i > 0).astype(F32)
        last = (i < nt - 1).astype(F32)
        ext = jnp.concatenate([prev_ref[...] * first, cur_ref[...], next_ref[...] * last], axis=0)
        dye = jnp.concatenate([jnp.zeros((8, GROUP), F32), dy_ref[...], dyn_ref[...] * last], axis=0)
        hv, bv, cv = ext[:, 0:GROUP], ext[:, GROUP:2 * GROUP], ext[:, 2 * GROUP:3 * GROUP]
        wv = w_ref[...]
        sv = cv * hv
        conv = _conv3(sv, wv)
        dconv = dye * bv
        ds = _conv3_t(dconv, wv)
        dp = jnp.concatenate([ds * cv, dye * conv, ds * hv], axis=1)
        dp_ref[...] = dp[8:8 + t].astype(dp_ref.dtype)
        dc = dconv[8:8 + t]
        dw = jnp.concatenate([
            jnp.sum(dc * _shift_down(sv, 2)[8:8 + t], axis=0, keepdims=True),
            jnp.sum(dc * _shift_down(sv, 1)[8:8 + t], axis=0, keepdims=True),
            jnp.sum(dc * sv[8:8 + t], axis=0, keepdims=True),
            jnp.zeros((5, GROUP), F32)], axis=0)

        @pl.when(i == 0)
        def _():
            dw_ref[...] = dw

        @pl.when(i > 0)
        def _():
            dw_ref[...] += dw

    dp, dw = pl.pallas_call(
        body, name="sconv_bwd", grid=(nt,),
        out_shape=(jax.ShapeDtypeStruct((s, N_IN_PAD), BF16), jax.ShapeDtypeStruct((8, GROUP), F32)),
        in_specs=[pl.BlockSpec((t, 3 * GROUP), lambda i: (i, 0)),
                  pl.BlockSpec((8, 3 * GROUP), _prev_map(t, 8, 0)),
                  pl.BlockSpec((8, 3 * GROUP), _next_map(t, 8, 0, s)),
                  pl.BlockSpec((3, GROUP), lambda i: (0, 0)),
                  pl.BlockSpec((t, GROUP), lambda i: (i, 0)),
                  pl.BlockSpec((8, GROUP), _next_map(t, 8, 0, s))],
        out_specs=(pl.BlockSpec((t, 3 * GROUP), lambda i: (i, 0)), pl.BlockSpec((8, GROUP), lambda i: (0, 0))),
        compiler_params=_params(),
    )(proj, proj, proj, w, dy, dy)
    return dp, dw[:3]


def _lane_window(shape):
    lane = lax.broadcasted_iota(jnp.int32, shape, 1)
    return lane, jnp.where(lane < 64, 2.0, jnp.where(lane < 128, 4.0, jnp.where(lane < 192, 8.0, 16.0)))


def _by_group(lane, s1, s2, s3, s4):
    return jnp.where(lane < 64, s1, jnp.where(lane < 128, s2, jnp.where(lane < 192, s3, s4)))


def _pool_z(ext, row0):
    s1 = ext + _shift_down(ext, 1)
    s2 = s1 + _shift_down(s1, 2)
    s3 = s2 + _shift_down(s2, 4)
    s4 = s3 + _shift_down(s3, 8)
    lane, win = _lane_window(ext.shape)
    tpos = (lax.broadcasted_iota(jnp.int32, ext.shape, 0) + (row0 - 16 + 1)).astype(F32)
    cnt = jnp.maximum(jnp.minimum(tpos, win), 1.0)
    return _by_group(lane, s1, s2, s3, s4) / cnt - ext


ANY_SPEC = pl.BlockSpec(memory_space=pl.ANY)


def _pool_fwd(proj, wbd, scale, ybuf):
    s = proj.shape[0]
    t = min(ROW_TILE, s)
    col = (COL_GATE - GROUP) // GROUP

    def body(cur_ref, prev_ref, w_ref, sc_ref, buf_ref, o_ref):
        i = pl.program_id(0)
        ext = jnp.concatenate([prev_ref[...] * (i > 0).astype(F32), cur_ref[...]], axis=0)
        z = _pool_z(ext, i * t)[16:]
        y = jnp.dot(z.astype(BF16), w_ref[...].astype(BF16), preferred_element_type=F32)
        o_ref[...] = (y * sc_ref[...]).astype(o_ref.dtype)

    return pl.pallas_call(
        body, name="pool_fwd", grid=(s // t,), out_shape=jax.ShapeDtypeStruct(ybuf.shape, ybuf.dtype),
        in_specs=[pl.BlockSpec((t, GROUP), lambda i: (i, col)),
                  pl.BlockSpec((16, GROUP), _prev_map(t, 16, col)),
                  pl.BlockSpec((GROUP, GROUP), lambda i: (0, 0)),
                  pl.BlockSpec((1, GROUP), lambda i: (0, 0)), ANY_SPEC],
        out_specs=pl.BlockSpec((t, GROUP), lambda i: (i, 3)), input_output_aliases={4: 0},
        compiler_params=_params(),
    )(proj, proj, wbd, scale.reshape(1, GROUP), ybuf)


def _pool_bwd(proj, wbd, scale, dy, dbuf):
    s = proj.shape[0]
    t = min(ROW_TILE, s)
    nt = s // t
    col = (COL_GATE - GROUP) // GROUP

    def body(cur_ref, prev_ref, w_ref, sc_ref, dy_ref, dyn_ref, buf_ref, dp_ref, dw_ref, dsc_ref):
        i = pl.program_id(0)
        ext = jnp.concatenate([prev_ref[...] * (i > 0).astype(F32), cur_ref[...]], axis=0)
        z = _pool_z(ext, i * t)[16:]
        wv = w_ref[...].astype(BF16)
        dyc = dy_ref[...]
        dye = jnp.concatenate([dyc, dyn_ref[...] * (i < nt - 1).astype(F32)], axis=0) * sc_ref[...]
        dz = lax.dot_general(dye.astype(BF16), wv, (NT, ((), ())), preferred_element_type=F32)
        lane, win = _lane_window(dz.shape)
        tpos = (lax.broadcasted_iota(jnp.int32, dz.shape, 0) + (i * t + 1)).astype(F32)
        e = dz / jnp.minimum(tpos, win)
        f1 = e + _shift_up(e, 1)
        f2 = f1 + _shift_up(f1, 2)
        f3 = f2 + _shift_up(f2, 4)
        f4 = f3 + _shift_up(f3, 8)
        dp = _by_group(lane, f1, f2, f3, f4) - dz
        dp_ref[...] = dp[:t].astype(dp_ref.dtype)
        zb = z.astype(BF16)
        y = jnp.dot(zb, wv, preferred_element_type=F32)
        dsc = jnp.sum(dyc * y, axis=0, keepdims=True)
        dw = lax.dot_general(zb, dye[:t].astype(BF16), (TN, ((), ())), preferred_element_type=F32)

        @pl.when(i == 0)
        def _():
            dw_ref[...] = dw
            dsc_ref[...] = dsc

        @pl.when(i > 0)
        def _():
            dw_ref[...] += dw
            dsc_ref[...] += dsc

    dp, dw, dsc = pl.pallas_call(
        body, name="pool_bwd", grid=(nt,),
        out_shape=(jax.ShapeDtypeStruct(dbuf.shape, dbuf.dtype), jax.ShapeDtypeStruct((GROUP, GROUP), F32),
                   jax.ShapeDtypeStruct((1, GROUP), F32)),
        in_specs=[pl.BlockSpec((t, GROUP), lambda i: (i, col)),
                  pl.BlockSpec((16, GROUP), _prev_map(t, 16, col)),
                  pl.BlockSpec((GROUP, GROUP), lambda i: (0, 0)),
                  pl.BlockSpec((1, GROUP), lambda i: (0, 0)),
                  pl.BlockSpec((t, GROUP), lambda i: (i, 3)),
                  pl.BlockSpec((16, GROUP), _next_map(t, 16, 3, s)), ANY_SPEC],
        out_specs=(pl.BlockSpec((t, GROUP), lambda i: (i, col)), pl.BlockSpec((GROUP, GROUP), lambda i: (0, 0)),
                   pl.BlockSpec((1, GROUP), lambda i: (0, 0))),
        input_output_aliases={6: 0}, compiler_params=_params(),
    )(proj, proj, wbd, scale.reshape(1, GROUP), dy, dy, dbuf)
    return dp, dw, dsc.reshape(GROUP)


FF_HALO = 16


def _ffn_gate_fwd(u0, w):
    s = u0.shape[1]
    t = min(ROW_TILE, s)

    def body(a_ref, ap_ref, g_ref, gp_ref, wa_ref, wg_ref, o_ref):
        first = (pl.program_id(1) > 0).astype(F32)
        a = _conv3(jnp.concatenate([ap_ref[...] * first, a_ref[...].astype(F32)], axis=0), wa_ref[...])[FF_HALO:]
        g = _conv3(jnp.concatenate([gp_ref[...] * first, g_ref[...].astype(F32)], axis=0), wg_ref[...])[FF_HALO:]
        o_ref[...] = (a * (g * _sigmoid(g))).astype(o_ref.dtype)

    def cur(off):
        return pl.BlockSpec((None, t, FF_SHARD), lambda j, i: (j + off, i, 0))

    def prev(off):
        return pl.BlockSpec((None, FF_HALO, FF_SHARD),
                            lambda j, i: (j + off, jnp.maximum(i * (t // FF_HALO) - 1, 0), 0))

    def wspec(off):
        return pl.BlockSpec((None, 3, FF_SHARD), lambda j, i: (j + off, 0, 0))

    return pl.pallas_call(
        body, name="ffn_gate_fwd", grid=(FF_HALF, s // t),
        out_shape=jax.ShapeDtypeStruct((FF_HALF, s, FF_SHARD), BF16),
        in_specs=[cur(0), prev(0), cur(FF_HALF), prev(FF_HALF), wspec(0), wspec(FF_HALF)],
        out_specs=pl.BlockSpec((None, t, FF_SHARD), lambda j, i: (j, i, 0)), compiler_params=_params(),
    )(u0, u0, u0, u0, w, w)


def _ffn_gate_bwd(u0, w, dact):
    s = u0.shape[1]
    t = min(ROW_TILE, s)
    nt = s // t

    def body(c_ref, p_ref, n_ref, w_ref, d_ref, dn_ref, du_ref, dw_ref):
        i = pl.program_id(1)
        first = (i > 0).astype(F32)
        last = (i < nt - 1).astype(F32)
        dext = jnp.concatenate([jnp.zeros((FF_HALO, FF_SHARD), F32), d_ref[...].astype(F32), dn_ref[...] * last],
                               axis=0)
        ext = [jnp.concatenate([p_ref[n] * first, c_ref[n].astype(F32), n_ref[n] * last], axis=0) for n in range(2)]
        a = _conv3(ext[0], w_ref[0])
        g = _conv3(ext[1], w_ref[1])
        sg = _sigmoid(g)
        silu = g * sg
        dus = (dext * silu, dext * a * (sg + silu * (1.0 - sg)))
        mine = slice(FF_HALO, FF_HALO + t)
        for n in range(2):
            du_ref[n] = _conv3_t(dus[n], w_ref[n])[mine].astype(du_ref.dtype)
            dc = dus[n][mine]
            dw = jnp.concatenate([
                jnp.sum(dc * _shift_down(ext[n], 2)[mine], axis=0, keepdims=True),
                jnp.sum(dc * _shift_down(ext[n], 1)[mine], axis=0, keepdims=True),
                jnp.sum(dc * ext[n][mine], axis=0, keepdims=True),
                jnp.zeros((5, FF_SHARD), F32)], axis=0)

            @pl.when(i == 0)
            def _(n=n, dw=dw):
                dw_ref[n] = dw

            @pl.when(i > 0)
            def _(n=n, dw=dw):
                dw_ref[n] += dw

    def pair(rows, row_map):
        return pl.BlockSpec((2, None, rows, FF_SHARD), lambda j, i: (0, j, row_map(i), 0))

    prev_row = lambda i: jnp.maximum(i * (t // FF_HALO) - 1, 0)
    next_row = lambda i: jnp.minimum((i + 1) * (t // FF_HALO), s // FF_HALO - 1)
    u2 = u0.reshape(2, FF_HALF, s, FF_SHARD)
    du, dw = pl.pallas_call(
        body, name="ffn_gate_bwd", grid=(FF_HALF, nt),
        out_shape=(jax.ShapeDtypeStruct((2, FF_HALF, s, FF_SHARD), BF16),
                   jax.ShapeDtypeStruct((2, FF_HALF, 8, FF_SHARD), F32)),
        in_specs=[pair(t, lambda i: i), pair(FF_HALO, prev_row), pair(FF_HALO, next_row), pair(3, lambda i: 0),
                  pl.BlockSpec((None, t, FF_SHARD), lambda j, i: (j, i, 0)),
                  pl.BlockSpec((None, FF_HALO, FF_SHARD), lambda j, i: (j, next_row(i), 0))],
        out_specs=(pair(t, lambda i: i), pair(8, lambda i: 0)),
        compiler_params=_params(),
    )(u2, u2, u2, w.reshape(2, FF_HALF, 3, FF_SHARD), dact, dact)
    return du.reshape(2 * FF_HALF, s, FF_SHARD), dw.reshape(2 * FF_HALF, 8, FF_SHARD)[:, :3]


def _rope_tables(positions):
    inv_freq = ROPE_THETA ** (-jnp.arange(0, ROPE_DIM, 2, dtype=F32) / ROPE_DIM)
    ang = positions.astype(F32)[:, None] * inv_freq
    cos, sin = jnp.cos(ang), jnp.sin(ang)
    s = positions.shape[0]
    half = ROPE_DIM // 2
    rest = HEAD_DIM - ROPE_DIM
    ca = jnp.concatenate([cos, cos, jnp.ones((s, rest), F32)], axis=1)
    cb = jnp.concatenate([-sin, jnp.zeros((s, HEAD_DIM - half), F32)], axis=1)
    cc = jnp.concatenate([jnp.zeros((s, half), F32), sin, jnp.zeros((s, rest), F32)], axis=1)
    return tuple(jnp.tile(tb, (1, N_HEADS)) for tb in (ca, cb, cc))


QK_WIDE = 128
LANE_CQ, LANE_CK = 64, 67
KT_ROWS = 80


def _three_bf16(x):
    hi = x.astype(BF16).astype(F32)
    mid = (x - hi).astype(BF16).astype(F32)
    lo = (x - hi - mid).astype(BF16).astype(F32)
    return hi, mid, lo


def _heads_split(proj, col, tables, c, name):
    s = proj.shape[0]
    t = min(ROW_TILE, s)
    rope = tables is not None
    wide = c is not None
    width = QK_WIDE if wide else HEAD_DIM

    def body(*refs):
        x_ref = refs[0]
        q_ref, k_ref, v_ref, kt_ref, vt_ref = refs[-5:]
        xv = x_ref[...]
        parts = [xv[:, 0:GROUP], xv[:, GROUP:2 * GROUP], xv[:, 2 * GROUP:3 * GROUP]]
        if rope:
            ca, cb, cc = refs[1][...], refs[2][...], refs[3][...]
            for n in range(2):
                p = parts[n]
                parts[n] = p * ca + pltpu.roll(p, GROUP - 8, 1) * cb + pltpu.roll(p, 8, 1) * cc
        parts[0] = parts[0] * (HEAD_DIM ** -0.5)
        k_t, v_t = parts[1].T, parts[2].T
        ones_row = jnp.where(lax.broadcasted_iota(jnp.int32, (KT_ROWS - HEAD_DIM, t), 0) == 0, 1.0, 0.0)
        lane = lax.broadcasted_iota(jnp.int32, (t, QK_WIDE), 1)
        zeros = jnp.zeros((t, QK_WIDE - HEAD_DIM), F32)
        for h in range(N_HEADS):
            hs = slice(h * HEAD_DIM, (h + 1) * HEAD_DIM)
            qh, kh = parts[0][:, hs], parts[1][:, hs]
            if wide:
                terms = _three_bf16(refs[-6][:, h:h + 1])
                qh = jnp.concatenate([qh, zeros], axis=1)
                kh = jnp.concatenate([kh, zeros], axis=1)
                for n in range(3):
                    qh = jnp.where(lane == LANE_CQ + n, terms[n], jnp.where(lane == LANE_CK + n, 1.0, qh))
                    kh = jnp.where(lane == LANE_CK + n, -terms[n], jnp.where(lane == LANE_CQ + n, 1.0, kh))
            q_ref[h] = qh.astype(q_ref.dtype)
            k_ref[h] = kh.astype(k_ref.dtype)
            v_ref[h] = parts[2][:, hs].astype(v_ref.dtype)
            kt_ref[h] = jnp.concatenate([k_t[hs, :], ones_row], axis=0).astype(kt_ref.dtype)
            vt_ref[h] = v_t[hs, :].astype(vt_ref.dtype)

    tab = pl.BlockSpec((t, GROUP), lambda i: (i, 0))
    qk = pl.BlockSpec((N_HEADS, t, width), lambda i: (0, i, 0))
    heads = pl.BlockSpec((N_HEADS, t, HEAD_DIM), lambda i: (0, i, 0))
    heads_t = pl.BlockSpec((N_HEADS, HEAD_DIM, t), lambda i: (0, 0, i))
    qk_shape = jax.ShapeDtypeStruct((N_HEADS, s, width), BF16)
    return pl.pallas_call(
        body, name=name, grid=(s // t,),
        out_shape=(qk_shape, qk_shape, jax.ShapeDtypeStruct((N_HEADS, s, HEAD_DIM), BF16),
                   jax.ShapeDtypeStruct((N_HEADS, KT_ROWS, s), BF16),
                   jax.ShapeDtypeStruct((N_HEADS, HEAD_DIM, s), BF16)),
        in_specs=[pl.BlockSpec((t, 3 * GROUP), lambda i: (i, col))] + ([tab, tab, tab] if rope else [])
        + ([pl.BlockSpec((t, 128), lambda i: (i, 0))] if wide else []),
        out_specs=(qk, qk, heads, pl.BlockSpec((N_HEADS, KT_ROWS, t), lambda i: (0, 0, i)), heads_t),
        compiler_params=_params(),
    )(*((proj,) + (tuple(tables) if rope else ()) + ((c,) if wide else ())))


def _heads_merge(dqt, dk, dv, tables, name, dbuf, col):
    s = dv.shape[1]
    t = min(ROW_TILE, s)
    rope = tables is not None

    wide = dk.shape[2] == QK_WIDE

    def body(*refs):
        o_ref = refs[n_in + 1]
        dq = jnp.concatenate([refs[0][h, :HEAD_DIM, :] for h in range(N_HEADS)], axis=0).T
        parts = [dq] + [jnp.concatenate([r[h][:, :HEAD_DIM] for h in range(N_HEADS)], axis=1) for r in refs[1:3]]
        parts[0] = parts[0] * (HEAD_DIM ** -0.5)
        if rope:
            ca, cb, cc = refs[3][...], refs[4][...], refs[5][...]
            for n in range(2):
                p = parts[n]
                parts[n] = p * ca + pltpu.roll(p * cb, 8, 1) + pltpu.roll(p * cc, GROUP - 8, 1)
        o_ref[...] = jnp.concatenate(parts, axis=1).astype(o_ref.dtype)
        if wide:
            over_keys = jnp.concatenate([refs[0][h, HEAD_DIM:HEAD_DIM + 8, :] for h in range(N_HEADS)]
                                        + [jnp.zeros((128 - 8 * N_HEADS, t), F32)], axis=0).T
            lane = lax.broadcasted_iota(jnp.int32, (t, 128), 1)
            dc = jnp.zeros((t, 128), F32)
            for h in range(N_HEADS):
                dc = jnp.where(lane == h, over_keys[:, 8 * h:8 * h + 1] - refs[1][h][:, LANE_CK:LANE_CK + 1], dc)
            refs[n_in + 2][...] = dc

    tab = pl.BlockSpec((t, GROUP), lambda i: (i, 0))
    heads = pl.BlockSpec((N_HEADS, t, HEAD_DIM), lambda i: (0, i, 0))
    n_in = 6 if rope else 3
    dspec = pl.BlockSpec((t, 3 * GROUP), lambda i: (i, col))
    dshape = jax.ShapeDtypeStruct(dbuf.shape, dbuf.dtype)
    return pl.pallas_call(
        body, name=name, grid=(s // t,),
        out_shape=(dshape, jax.ShapeDtypeStruct((s, 128), F32)) if wide else dshape,
        in_specs=[pl.BlockSpec((N_HEADS, KT_ROWS, t), lambda i: (0, 0, i)),
                  pl.BlockSpec((N_HEADS, t, dk.shape[2]), lambda i: (0, i, 0)), heads]
        + ([tab, tab, tab] if rope else []) + [ANY_SPEC],
        out_specs=(dspec, pl.BlockSpec((t, 128), lambda i: (i, 0))) if wide else dspec,
        input_output_aliases={n_in: 0}, compiler_params=_params(),
    )(*((dqt, dk, dv) + (tuple(tables) if rope else ()) + (dbuf,)))


def _log_sigmoid(x):
    return jnp.minimum(x, 0.0) - jnp.log(1.0 + jnp.exp(-jnp.abs(x)))


def _scan_rows(x, reverse):
    n = x.shape[0]
    row = lax.broadcasted_iota(jnp.int32, x.shape, 0)
    k = 1
    while k < n:
        if reverse:
            x = x + jnp.where(row < n - k, _shift_up(x, k), 0.0)
        else:
            x = x + jnp.where(row >= k, _shift_down(x, k), 0.0)
        k *= 2
    return x


def _gate_cumsum(proj, bias):
    s = proj.shape[0]
    col = COL_GATE // 128

    def body(z_ref, b_ref, c_ref):
        c_ref[...] = _scan_rows(_log_sigmoid(z_ref[...] + b_ref[...]), False)

    return pl.pallas_call(
        body, name="gate_cumsum", grid=(1,), out_shape=jax.ShapeDtypeStruct((s, 128), F32),
        in_specs=[pl.BlockSpec((s, 128), lambda i: (0, col)), pl.BlockSpec((1, 128), lambda i: (0, 0))],
        out_specs=pl.BlockSpec((s, 128), lambda i: (0, 0)), compiler_params=_params(),
    )(proj, bias)


def _gate_cumsum_bwd(proj, bias, dc, dbuf):
    s = proj.shape[0]
    col = COL_GATE // 128

    def body(z_ref, b_ref, dc_ref, buf_ref, dz_ref, db_ref):
        dlogf = _scan_rows(dc_ref[...], True)
        dz = dlogf * _sigmoid(-(z_ref[...] + b_ref[...]))
        dz_ref[...] = dz.astype(dz_ref.dtype)
        db_ref[...] = jnp.sum(dz, axis=0, keepdims=True)

    return pl.pallas_call(
        body, name="gate_cumsum_bwd", grid=(1,),
        out_shape=(jax.ShapeDtypeStruct(dbuf.shape, dbuf.dtype), jax.ShapeDtypeStruct((1, 128), F32)),
        in_specs=[pl.BlockSpec((s, 128), lambda i: (0, col)), pl.BlockSpec((1, 128), lambda i: (0, 0)),
                  pl.BlockSpec((s, 128), lambda i: (0, 0)), ANY_SPEC],
        out_specs=(pl.BlockSpec((s, 128), lambda i: (0, col)), pl.BlockSpec((1, 128), lambda i: (0, 0))),
        input_output_aliases={3: 0}, compiler_params=_params(),
    )(proj, bias, dc, dbuf)


DIL_REACH = 2048


def _pair_weight(mode, d):
    if mode == "fox":
        return jnp.where(d >= 0, 1.0, 0.0)
    w1 = jnp.where(jnp.abs(d - 64) <= 64, 1.0, 0.0)
    w2 = jnp.where((d & 3) == 0, jnp.where(jnp.abs(d - 256) <= 256, 1.0, 0.0), 0.0)
    w3 = jnp.where((d & 15) == 0, jnp.where(jnp.abs(d - 1024) <= 1024, 1.0, 0.0), 0.0)
    return w1 + w2 + w3


def _bias_tables(mode, tq, tk):
    nb = 2 if mode == "fox" else DIL_REACH // tk + 1
    n = lax.broadcasted_iota(jnp.int32, (nb, tk, tq), 0)
    key = lax.broadcasted_iota(jnp.int32, (nb, tk, tq), 1)
    query = lax.broadcasted_iota(jnp.int32, (nb, tk, tq), 2)
    w = _pair_weight(mode, n * tk + query - key)
    return jnp.where(w > 0.0, jnp.log(jnp.maximum(w, 1.0)), NEG)


M_INIT = -1e29


def _first_key_chunk(mode, q0, tk):
    if mode == "fox":
        return 0
    return jnp.maximum(q0 - DIL_REACH, 0) // tk


def _attention_fwd(mode, q, k, vt, tab_t, ybuf, col):
    s, width = q.shape[1], q.shape[2]
    tq = min(ATT_TQ, s)
    tk = tq
    nb = tab_t.shape[0]

    def body(q_ref, k_ref, vt_ref, tab_ref, buf_ref, y_ref, o_ref, lse_ref):
        i = pl.program_id(0)
        lo = _first_key_chunk(mode, i * tq, tk)

        def step(c, carry):
            k0 = pl.multiple_of(c * tk, tk)
            tab = tab_ref[jnp.minimum(i - c, nb - 1)]
            scores = [lax.dot_general(k_ref[h, pl.ds(k0, tk), :], q_ref[h], (NT, ((), ())),
                                      preferred_element_type=F32) for h in range(N_HEADS)]
            stats, probs = [], []
            for h in range(N_HEADS):
                m, l = carry[3 * h:3 * h + 2]
                sc = scores[h] + tab
                m_new = jnp.maximum(m, jnp.max(sc, axis=0, keepdims=True))
                alpha = jnp.exp(m - m_new)
                p = jnp.exp(sc - m_new)
                stats.append((m_new, alpha * l + jnp.sum(p, axis=0, keepdims=True), alpha))
                probs.append(p.astype(BF16))
            pv = [jnp.dot(vt_ref[h, :, pl.ds(k0, tk)], probs[h], preferred_element_type=F32) for h in range(N_HEADS)]
            new = []
            for h in range(N_HEADS):
                m_new, l, alpha = stats[h]
                new += [m_new, l, alpha * carry[3 * h + 2] + pv[h]]
            return tuple(new)

        start = (jnp.full((1, tq), M_INIT, F32), jnp.zeros((1, tq), F32), jnp.zeros((HEAD_DIM, tq), F32))
        done = lax.fori_loop(lo, i + 1, step, start * N_HEADS)
        outs = []
        for h in range(N_HEADS):
            m, l, acc = done[3 * h:3 * h + 3]
            outs.append(acc / l)
            lse_ref[h] = m + jnp.log(l)
        out = jnp.concatenate(outs, axis=0).T
        y_ref[...] = out.astype(y_ref.dtype)
        o_ref[...] = out

    rowspec = pl.BlockSpec((N_HEADS, 1, tq), lambda i: (0, 0, i))
    return pl.pallas_call(
        body, name="attention_fwd_" + mode, grid=(s // tq,),
        out_shape=(jax.ShapeDtypeStruct(ybuf.shape, ybuf.dtype), jax.ShapeDtypeStruct((s, GROUP), F32),
                   jax.ShapeDtypeStruct((N_HEADS, 1, s), F32)),
        in_specs=[pl.BlockSpec((N_HEADS, tq, width), lambda i: (0, i, 0)),
                  pl.BlockSpec((N_HEADS, s, width), lambda i: (0, 0, 0)),
                  pl.BlockSpec((N_HEADS, HEAD_DIM, s), lambda i: (0, 0, 0)),
                  pl.BlockSpec((nb, tk, tq), lambda i: (0, 0, 0)), ANY_SPEC],
        out_specs=(pl.BlockSpec((tq, GROUP), lambda i: (i, col)), pl.BlockSpec((tq, GROUP), lambda i: (i, 0)),
                   rowspec),
        input_output_aliases={4: 0}, compiler_params=_params(),
    )(q, k, vt, tab_t, ybuf)


def _attention_delta(o, do, col):
    s = o.shape[0]
    t = min(ROW_TILE, s)

    def body(o_ref, do_ref, delta_ref, dob_ref):
        dov = do_ref[...]
        prod_t = (o_ref[...] * dov).T
        for h in range(N_HEADS):
            hs = slice(h * HEAD_DIM, (h + 1) * HEAD_DIM)
            delta_ref[h] = jnp.sum(prod_t[hs, :], axis=0, keepdims=True)
            dob_ref[h] = dov[:, hs].astype(dob_ref.dtype)

    return pl.pallas_call(
        body, name="attention_delta", grid=(s // t,),
        out_shape=(jax.ShapeDtypeStruct((N_HEADS, 1, s), F32), jax.ShapeDtypeStruct((N_HEADS, s, HEAD_DIM), BF16)),
        in_specs=[pl.BlockSpec((t, GROUP), lambda i: (i, 0)), pl.BlockSpec((t, GROUP), lambda i: (i, col))],
        out_specs=(pl.BlockSpec((N_HEADS, 1, t), lambda i: (0, 0, i)),
                   pl.BlockSpec((N_HEADS, t, HEAD_DIM), lambda i: (0, i, 0))),
        compiler_params=_params(),
    )(o, do)


def _attention_bwd(mode, q, k, v, kt, tab_t, dob, lse, delta):
    s, width = q.shape[1], q.shape[2]
    tq = min(ATT_TQ, s)
    tk = tq
    nq = s // tq
    nb = tab_t.shape[0]

    def body(q_ref, k_ref, v_ref, kt_ref, tab_ref, dob_ref, lse_ref, delta_ref, dqt_ref, dk_ref, dv_ref):
        i = pl.program_id(0)

        @pl.when(i == 0)
        def _():
            dqt_ref[...] = jnp.zeros_like(dqt_ref)

        hi = nq if mode == "fox" else jnp.minimum((i * tk + tk - 1 + DIL_REACH) // tq + 1, nq)
        for h0 in range(0, N_HEADS, BWD_HEADS):
            heads = range(h0, h0 + BWD_HEADS)

            def step(c, carry, heads=heads):
                q0 = pl.multiple_of(c * tq, tq)
                qs = pl.ds(q0, tq)
                tab = tab_ref[jnp.minimum(c - i, nb - 1)]
                qv = [q_ref[h, qs, :] for h in heads]
                dov = [dob_ref[h, qs, :] for h in heads]
                sc = [lax.dot_general(k_ref[h], qv[n], (NT, ((), ())), preferred_element_type=F32)
                      for n, h in enumerate(heads)]
                dp = [lax.dot_general(v_ref[h], dov[n], (NT, ((), ())), preferred_element_type=F32)
                      for n, h in enumerate(heads)]
                pb, dsb = [], []
                for n, h in enumerate(heads):
                    p = jnp.exp(sc[n] + tab - lse_ref[h, :, qs])
                    pb.append(p.astype(BF16))
                    dsb.append((p * (dp[n] - delta_ref[h, :, qs])).astype(BF16))
                new = []
                for n, h in enumerate(heads):
                    new += [carry[2 * n] + jnp.dot(dsb[n], qv[n], preferred_element_type=F32),
                            carry[2 * n + 1] + jnp.dot(pb[n], dov[n], preferred_element_type=F32)]
                for n, h in enumerate(heads):
                    dqt_ref[h, :, qs] += jnp.dot(kt_ref[h], dsb[n], preferred_element_type=F32)
                return tuple(new)

            start = (jnp.zeros((tk, width), F32), jnp.zeros((tk, HEAD_DIM), F32))
            done = lax.fori_loop(i, hi, step, start * BWD_HEADS)
            for n, h in enumerate(heads):
                dk_ref[h] = done[2 * n]
                dv_ref[h] = done[2 * n + 1]

    def full(shape):
        return pl.BlockSpec(shape, lambda i: (0, 0, 0))

    kblk = pl.BlockSpec((N_HEADS, tk, width), lambda i: (0, i, 0))
    vblk = pl.BlockSpec((N_HEADS, tk, HEAD_DIM), lambda i: (0, i, 0))
    return pl.pallas_call(
        body, name="attention_bwd_" + mode, grid=(s // tk,),
        out_shape=(jax.ShapeDtypeStruct((N_HEADS, KT_ROWS, s), F32), jax.ShapeDtypeStruct((N_HEADS, s, width), F32),
                   jax.ShapeDtypeStruct((N_HEADS, s, HEAD_DIM), F32)),
        in_specs=[full((N_HEADS, s, width)), kblk, vblk, pl.BlockSpec((N_HEADS, KT_ROWS, tk), lambda i: (0, 0, i)),
                  full((nb, tk, tq)), full((N_HEADS, s, HEAD_DIM)), full((N_HEADS, 1, s)), full((N_HEADS, 1, s))],
        out_specs=(full((N_HEADS, KT_ROWS, s)), kblk, vblk),
        compiler_params=_params(),
    )(q, k, v, kt, tab_t, dob, lse, delta)


def _xattn_fwd(qx, kvm):
    s = qx.shape[0]
    t = min(ROW_TILE, s)

    def body(q_ref, kv_ref, o_ref):
        heads = range(XA_HEADS)
        sc = [lax.dot_general(q_ref[:, h * XA_DIM:(h + 1) * XA_DIM].astype(BF16), kv_ref[h].astype(BF16),
                              (NT, ((), ())), preferred_element_type=F32) * (XA_DIM ** -0.5) for h in heads]
        probs = []
        for h in heads:
            e = jnp.exp(sc[h] - jnp.max(sc[h], axis=-1, keepdims=True))
            probs.append((e / jnp.sum(e, axis=-1, keepdims=True)).astype(BF16))
        outs = [jnp.dot(probs[h], kv_ref[XA_HEADS + h].astype(BF16), preferred_element_type=F32) for h in heads]
        for h in heads:
            o_ref[:, h * XA_DIM:(h + 1) * XA_DIM] = outs[h].astype(o_ref.dtype)

    return pl.pallas_call(
        body, name="xattn_fwd", grid=(s // t,), out_shape=jax.ShapeDtypeStruct((s, D_MODEL), BF16),
        in_specs=[pl.BlockSpec((t, D_MODEL), lambda i: (i, 0)),
                  pl.BlockSpec((2 * XA_HEADS, MEM_LEN, XA_DIM), lambda i: (0, 0, 0))],
        out_specs=pl.BlockSpec((t, D_MODEL), lambda i: (i, 0)), compiler_params=_params(),
    )(qx, kvm)


def _xattn_bwd(qx, kvm, do):
    s = qx.shape[0]
    t = min(ROW_TILE, s)

    def body(q_ref, kv_ref, do_ref, dq_ref, dkv_ref):
        i = pl.program_id(0)
        heads = range(XA_HEADS)
        qv = [q_ref[:, h * XA_DIM:(h + 1) * XA_DIM].astype(BF16) for h in heads]
        dov = [do_ref[:, h * XA_DIM:(h + 1) * XA_DIM].astype(BF16) for h in heads]
        kv = [kv_ref[h].astype(BF16) for h in heads]
        sc = [lax.dot_general(qv[h], kv[h], (NT, ((), ())), preferred_element_type=F32) * (XA_DIM ** -0.5)
              for h in heads]
        dp = [lax.dot_general(dov[h], kv_ref[XA_HEADS + h].astype(BF16), (NT, ((), ())), preferred_element_type=F32)
              for h in heads]
        pb, ds = [], []
        for h in heads:
            e = jnp.exp(sc[h] - jnp.max(sc[h], axis=-1, keepdims=True))
            p = e / jnp.sum(e, axis=-1, keepdims=True)
            pb.append(p.astype(BF16))
            ds.append((p * (dp[h] - jnp.sum(p * dp[h], axis=-1, keepdims=True)) * (XA_DIM ** -0.5)).astype(BF16))
        dq = [jnp.dot(ds[h], kv[h], preferred_element_type=F32) for h in heads]
        dk = [lax.dot_general(ds[h], qv[h], (TN, ((), ())), preferred_element_type=F32) for h in heads]
        dv = [lax.dot_general(pb[h], dov[h], (TN, ((), ())), preferred_element_type=F32) for h in heads]
        for h in heads:
            dq_ref[:, h * XA_DIM:(h + 1) * XA_DIM] = dq[h].astype(dq_ref.dtype)

        @pl.when(i == 0)
        def _():
            for h in heads:
                dkv_ref[h] = dk[h]
                dkv_ref[XA_HEADS + h] = dv[h]

        @pl.when(i > 0)
        def _():
            for h in heads:
                dkv_ref[h] += dk[h]
                dkv_ref[XA_HEADS + h] += dv[h]

    row = pl.BlockSpec((t, D_MODEL), lambda i: (i, 0))
    kvs = pl.BlockSpec((2 * XA_HEADS, MEM_LEN, XA_DIM), lambda i: (0, 0, 0))
    return pl.pallas_call(
        body, name="xattn_bwd", grid=(s // t,),
        out_shape=(jax.ShapeDtypeStruct((s, D_MODEL), BF16),
                   jax.ShapeDtypeStruct((2 * XA_HEADS, MEM_LEN, XA_DIM), F32)),
        in_specs=[row, kvs, row], out_specs=(row, kvs), compiler_params=_params(),
    )(qx, kvm, do)


def _adamw(parts, owns, me, w, m, v, name):
    nl, r, c = w.shape
    tr = r
    for cand in (256, 128, 64, 32, 16, 8):
        if r % cand == 0 and r > cand and N_DEV * cand * c * 4 <= ADAMW_BLOCK_BYTES:
            tr = cand
            break
    nt = r // tr
    per_layer = N_DEV + (1 if owns is not None else 0)

    def body(me_ref, *refs):
        w_ref, m_ref, v_ref, g_ref, d_ref, nm_ref, nv_ref = refs[nl * per_layer:]
        layer = pl.program_id(0)
        g = None
        for l in range(nl):
            p_refs = refs[l * per_layer:(l + 1) * per_layer]
            gl = None
            for d in range(N_DEV):
                term = p_refs[d][...].astype(F32)
                if owns is not None:
                    term = jnp.where(me_ref[0] == d, p_refs[N_DEV][...].astype(F32), term)
                gl = term if gl is None else gl + term
            g = gl if g is None else jnp.where(layer == l, gl, g)
        mn = ADAM_B1 * m_ref[...] + (1.0 - ADAM_B1) * g
        vn = ADAM_B2 * v_ref[...] + (1.0 - ADAM_B2) * (g * g)
        m_hat = mn / (1.0 - ADAM_B1 ** ADAM_STEP)
        v_hat = vn / (1.0 - ADAM_B2 ** ADAM_STEP)
        g_ref[...] = g
        d_ref[...] = -ADAM_LR * (m_hat / (jnp.sqrt(v_hat) + ADAM_EPS) + ADAM_WD * w_ref[...])
        nm_ref[...] = mn
        nv_ref[...] = vn

    def rows(l, ll, i):
        return jnp.where(ll == l, i, jnp.where(ll < l, 0, nt - 1))

    def part_spec(l, d):
        if owns is None:
            return pl.BlockSpec((None, tr, c), lambda ll, i, me_ref: (d, rows(l, ll, i), 0))
        return pl.BlockSpec((None, tr, c),
                            lambda ll, i, me_ref: (jnp.where(me_ref[0] == d, (d + 1) % N_DEV, d), rows(l, ll, i), 0))

    def own_spec(l):
        return pl.BlockSpec((None, tr, c), lambda ll, i, me_ref: (me_ref[0], rows(l, ll, i), 0))

    in_specs, operands = [], []
    for l in range(nl):
        in_specs += [part_spec(l, d) for d in range(N_DEV)]
        operands += [parts[l]] * N_DEV
        if owns is not None:
            in_specs.append(own_spec(l))
            operands.append(owns[l])
    blk = pl.BlockSpec((None, tr, c), lambda ll, i, me_ref: (ll, i, 0))
    shp = jax.ShapeDtypeStruct((nl, r, c), F32)
    return pl.pallas_call(
        body, name=name, out_shape=(shp, shp, shp, shp),
        grid_spec=pltpu.PrefetchScalarGridSpec(
            num_scalar_prefetch=1, grid=(nl, nt), in_specs=in_specs + [blk, blk, blk],
            out_specs=(blk, blk, blk, blk)),
        compiler_params=_params(),
    )(me.reshape(1), *operands, w, m, v)


GROUPS = {"in": ("w_in",), "rest": ("w_out", "w_xq", "w_xo", "w_xkv", "w_up", "w_down")}
FULL_SHAPES = {"w_in": (D_MODEL, N_IN_PAD), "w_out": (D_MODEL, D_MODEL), "w_xq": (D_MODEL, D_MODEL),
               "w_xo": (D_MODEL, D_MODEL), "w_xkv": (N_DEV, D_MODEL, 2 * D_MODEL // N_DEV),
               "w_up": (N_DEV, D_MODEL, FF_SHARD), "w_down": (FF_HALF, FF_SHARD, D_MODEL)}
PIECE_SHAPES = {"w_in": (N_DEV, D_MODEL // N_DEV, N_IN_PAD), "w_out": (N_DEV, D_MODEL // N_DEV, D_MODEL),
                "w_xq": (N_DEV, D_MODEL // N_DEV, D_MODEL), "w_xo": (N_DEV, D_MODEL // N_DEV, D_MODEL),
                "w_xkv": (N_DEV, D_MODEL, 2 * D_MODEL // N_DEV), "w_up": (N_DEV, D_MODEL, FF_SHARD),
                "w_down": (N_DEV, D_FF // N_DEV, D_MODEL)}
CONV_WORDS = 8192


class _GatheredWeights:
    def __init__(self, states, layer):
        self.states, self.layer, self.full, self.extra = dict(states), layer, {}, None

    def need(self, group, after):
        if group in self.states:
            got, _ = _exchange_wait(self.states.pop(group), after, "gather_%s_wait_%d" % (group, self.layer))
            for name, g in zip(GROUPS[group], got):
                self.full[name] = g.reshape(FULL_SHAPES[name])
            self.extra = got[len(GROUPS[group]):]

    def __getitem__(self, name):
        return self.full[name]


def _relay_in_cols(w):
    pad = jnp.zeros(w.shape[:-1] + (N_IN_PAD - N_IN,), w.dtype)
    return jnp.concatenate([w[..., :2304], w[..., 2308:N_IN], w[..., 2304:2308], pad], axis=-1)


def _unrelay_in_cols(w):
    return jnp.concatenate([w[..., :2304], w[..., COL_GATE:COL_GATE + 4], w[..., 2304:COL_GATE]], axis=-1)


def _layer_fwd(h, memv, w, sm, tables):
    sv = {"h0": h}
    s = h.shape[0]
    tm, tb = min(ROW_TILE, s), min(MM_TILE, s)
    w.need("in", h)
    tn = N_IN_PAD // 3
    proj, xn = _norm_matmul(h, sm["g_mix"], w["w_in"], (s, N_IN_PAD), grid=(s // tb, 3),
                            b_spec=pl.BlockSpec((D_MODEL, tn), lambda i, j: (0, j)),
                            o_spec=pl.BlockSpec((tb, tn), lambda i, j: (i, j)), name="norm_mm_in")
    sv["xn"], sv["proj"] = xn, proj
    ycat = _sconv_fwd(proj, sm["w_sconv"])
    qd, kd, vd, ktd, vtd = _heads_split(proj, 1, tables["rope"], None, "split_dil")
    ycat, ob, lse_b = _attention_fwd("dil", qd, kd, vtd, tables["dil"], ycat, 1)
    sv["dil"] = (qd, kd, vd, ktd, ob, lse_b)
    c = _gate_cumsum(proj, sm["b_forget_pad"])
    qf, kf, vf, ktf, vtf = _heads_split(proj, 2, None, c, "split_fox")
    ycat, oc, lse_c = _attention_fwd("fox", qf, kf, vtf, tables["fox"], ycat, 2)
    sv["fox"] = (qf, kf, vf, ktf, oc, lse_c)
    ycat = _pool_fwd(proj, sm["w_pool_bd"], sm["pool_scale"], ycat)
    sv["ycat"] = ycat
    w.need("rest", ycat)
    h1 = _mm_nn(ycat, w["w_out"], "mm_out", res=h)
    sv["h1"] = h1
    memn = _rms_fwd(memv, sm["g_mem"], "rms_mem")
    qx, xq = _norm_matmul(h1, sm["g_xa"], w["w_xq"], (s, D_MODEL), grid=(s // tb, 1),
                          b_spec=pl.BlockSpec((D_MODEL, D_MODEL), lambda i, j: (0, 0)),
                          o_spec=pl.BlockSpec((tb, D_MODEL), lambda i, j: (i, 0)), name="norm_mm_xq",
                          out_dtype=BF16)
    kvm = _matmul(memn, w["w_xkv"], (N_DEV, MEM_LEN, XA_DIM), grid=(N_DEV, 1, 1),
                  a_spec=pl.BlockSpec((MEM_LEN, D_MODEL), lambda i, j, r: (0, 0)),
                  b_spec=pl.BlockSpec((None, D_MODEL, XA_DIM), lambda i, j, r: (i, 0, 0)),
                  o_spec=pl.BlockSpec((None, MEM_LEN, XA_DIM), lambda i, j, r: (i, 0, 0)),
                  dims=NN, nred=1, name="mm_xkv")
    ox = _xattn_fwd(qx, kvm)
    sv.update(xq=xq, memn=memn, qx=qx, kvm=kvm, ox=ox)
    h2 = _mm_nn(ox, w["w_xo"], "mm_xo", res=h1)
    sv["h2"] = h2
    u0, xf = _norm_matmul(h2, sm["g_ffn"], w["w_up"], (N_DEV, s, FF_SHARD), grid=(s // tb, N_DEV),
                          b_spec=pl.BlockSpec((None, D_MODEL, FF_SHARD), lambda i, j: (j, 0, 0)),
                          o_spec=pl.BlockSpec((None, tb, FF_SHARD), lambda i, j: (j, i, 0)), name="norm_mm_up",
                          out_dtype=BF16)
    act = _ffn_gate_fwd(u0, sm["w_ffconv"])
    sv.update(xf=xf, u0=u0, act=act)
    ospec = pl.BlockSpec((tm, D_MODEL), lambda i, j, r: (i, 0))
    h3 = _matmul(act, w["w_down"], (s, D_MODEL), grid=(s // tm, 1, 1),
                 a_spec=pl.BlockSpec((FF_HALF, tm, FF_SHARD), lambda i, j, r: (0, i, 0)),
                 b_spec=pl.BlockSpec((FF_HALF, FF_SHARD, D_MODEL), lambda i, j, r: (0, 0, 0)),
                 o_spec=ospec, dims=NN, nred=1, slabs=FF_HALF, name="mm_down", res=h2, res_spec=ospec)
    return h3, sv


def _layer_bwd(dh3, memv, w, sm, tables, sv, rest_ready):
    s = dh3.shape[0]
    tm, tb = min(ROW_TILE, s), min(MM_TILE, s)
    big, small = {}, {}
    ts = max(s // 2, 1)
    dact = _matmul(dh3, w["w_down"], (FF_HALF, s, FF_SHARD), grid=(s // tb, FF_HALF, 1),
                   a_spec=pl.BlockSpec((tb, D_MODEL), lambda i, j, r: (i, 0)),
                   b_spec=pl.BlockSpec((None, FF_SHARD, D_MODEL), lambda i, j, r: (j, 0, 0)),
                   o_spec=pl.BlockSpec((None, tb, FF_SHARD), lambda i, j, r: (j, i, 0)),
                   dims=NT, nred=1, name="mm_dact", out_dtype=BF16)
    big["w_down"] = _matmul(sv["act"], dh3, (FF_HALF, FF_SHARD, D_MODEL), grid=(FF_HALF, 1, s // ts),
                            a_spec=pl.BlockSpec((None, ts, FF_SHARD), lambda i, j, r: (i, r, 0)),
                            b_spec=pl.BlockSpec((ts, D_MODEL), lambda i, j, r: (r, 0)),
                            o_spec=pl.BlockSpec((None, FF_SHARD, D_MODEL), lambda i, j, r: (i, 0, 0)),
                            dims=TN, nred=s // ts, name="mm_dw_down", out_dtype=GRAD_DTYPE)
    du0, small["w_ffconv"] = _ffn_gate_bwd(sv["u0"], sm["w_ffconv"], dact)
    dh2, small["g_ffn"] = _matmul_rms_bwd(du0, w["w_up"], sv["h2"], sm["g_ffn"], dh3, "mm_dxf_rms_bwd",
                                          tm=ROW_TILE // 2)
    big["w_up"] = _matmul(sv["xf"], du0, (N_DEV, D_MODEL, FF_SHARD), grid=(N_DEV, 1, 1),
                          a_spec=pl.BlockSpec((s, D_MODEL), lambda i, j, r: (0, 0)),
                          b_spec=pl.BlockSpec((None, s, FF_SHARD), lambda i, j, r: (i, 0, 0)),
                          o_spec=pl.BlockSpec((None, D_MODEL, FF_SHARD), lambda i, j, r: (i, 0, 0)),
                          dims=TN, nred=1, name="mm_dw_up", out_dtype=GRAD_DTYPE)
    dox = _mm_nt(dh2, w["w_xo"], "mm_dox", out_dtype=BF16)
    big["w_xo"] = _mm_tn(sv["ox"], dh2, "mm_dw_xo")
    dqx, dkvm = _xattn_bwd(sv["qx"], sv["kvm"], dox)
    big["w_xq"] = _mm_tn(sv["xq"], dqx, "mm_dw_xq")
    big["w_xkv"] = _matmul(sv["memn"], dkvm, (N_DEV, D_MODEL, XA_DIM), grid=(N_DEV, 1, 1),
                           a_spec=pl.BlockSpec((MEM_LEN, D_MODEL), lambda i, j, r: (0, 0)),
                           b_spec=pl.BlockSpec((None, MEM_LEN, XA_DIM), lambda i, j, r: (i, 0, 0)),
                           o_spec=pl.BlockSpec((None, D_MODEL, XA_DIM), lambda i, j, r: (i, 0, 0)),
                           dims=TN, nred=1, name="mm_dw_xkv", out_dtype=GRAD_DTYPE)
    dmemn = _matmul(dkvm, w["w_xkv"], (MEM_LEN, D_MODEL), grid=(1, 1, 1),
                    a_spec=pl.BlockSpec((N_DEV, MEM_LEN, XA_DIM), lambda i, j, r: (0, 0, 0)),
                    b_spec=pl.BlockSpec((N_DEV, D_MODEL, XA_DIM), lambda i, j, r: (0, 0, 0)),
                    o_spec=pl.BlockSpec((MEM_LEN, D_MODEL), lambda i, j, r: (0, 0)),
                    dims=NT, nred=1, slabs=N_DEV, name="mm_dmemn")
    _, small["g_mem"] = _rms_bwd(dmemn, memv, sm["g_mem"], None, "rms_mem_bwd")
    dh1, small["g_xa"] = _matmul_rms_bwd(dqx, w["w_xq"], sv["h1"], sm["g_xa"], dh2, "mm_dxq_rms_bwd")
    big["w_out"] = _mm_tn(sv["ycat"], dh1, "mm_dw_out")
    dycat = _mm_nt(dh1, w["w_out"] + rest_ready(big, small).astype(BF16), "mm_dycat")
    proj = sv["proj"]
    dproj, small["w_sconv"] = _sconv_bwd(proj, sm["w_sconv"], dycat)
    qd, kd, vd, ktd, ob, lse_b = sv["dil"]
    delta, dob = _attention_delta(ob, dycat, 1)
    dqt, dk, dv = _attention_bwd("dil", qd, kd, vd, ktd, tables["dil"], dob, lse_b, delta)
    dproj = _heads_merge(dqt, dk, dv, tables["rope"], "merge_dil", dproj, 1)
    qf, kf, vf, ktf, oc, lse_c = sv["fox"]
    delta, dob = _attention_delta(oc, dycat, 2)
    dqt, dk, dv = _attention_bwd("fox", qf, kf, vf, ktf, tables["fox"], dob, lse_c, delta)
    dproj, dc = _heads_merge(dqt, dk, dv, None, "merge_fox", dproj, 2)
    dproj, dbias = _gate_cumsum_bwd(proj, sm["b_forget_pad"], dc, dproj)
    small["b_forget"] = dbias[0, :N_HEADS]
    dproj, dwbd, small["pool_scale"] = _pool_bwd(proj, sm["w_pool_bd"], sm["pool_scale"], dycat, dproj)
    small["w_pool"] = jnp.stack([dwbd[64 * g:64 * (g + 1), 64 * g:64 * (g + 1)] for g in range(4)])
    big["w_in"] = _mm_tn(sv["xn"], dproj, "mm_dw_in", tn=896)
    dh0, small["g_mix"] = _matmul_rms_bwd(dproj, w["w_in"], sv["h0"], sm["g_mix"], dh1, "mm_dxn_rms_bwd")
    return dh0, big, small


SMALL_NAMES = ("g_mix", "b_forget", "w_pool", "pool_scale", "g_xa", "g_mem", "g_ffn", "w_sconv", "w_ffconv")
SMALL_WITH = {"rest": ("w_ffconv", "g_ffn", "g_mem", "g_xa"),
              "in": ("w_sconv", "b_forget", "pool_scale", "w_pool", "g_mix")}
SMALL_SHAPES = {"w_sconv": (3, GROUP), "w_ffconv": (N_DEV, 3, FF_SHARD)}
WEIGHT_NAMES = ("g_mix", "w_in", "b_forget", "w_sconv", "w_pool", "pool_scale", "w_out", "g_xa", "g_mem", "w_xq",
                "w_xkv", "w_xo", "g_ffn", "w_up", "w_ffconv", "w_down", "g_final")


def _block_diag(w_pool):
    z = jnp.zeros((64, 64), F32)
    return jnp.concatenate(
        [jnp.concatenate([w_pool[g] if c == g else z for c in range(4)], axis=1) for g in range(4)], axis=0)


def kernel(x, mem, positions, g_mix, w_in, b_forget, w_sconv, w_pool, pool_scale, w_out, g_xa, g_mem, w_xq, w_xkv, w_xo, g_ffn, w_up, w_ffconv, w_down, g_final, loss_target, m_g_mix, m_w_in, m_b_forget, m_w_sconv, m_w_pool, m_pool_scale, m_w_out, m_g_xa, m_g_mem, m_w_xq, m_w_xkv, m_w_xo, m_g_ffn, m_w_up, m_w_ffconv, m_w_down, m_g_final, v_g_mix, v_w_in, v_b_forget, v_w_sconv, v_w_pool, v_pool_scale, v_w_out, v_g_xa, v_g_mem, v_w_xq, v_w_xkv, v_w_xo, v_g_ffn, v_w_up, v_w_ffconv, v_w_down, v_g_final):
    weights = dict(g_mix=g_mix, w_in=w_in, b_forget=b_forget, w_sconv=w_sconv, w_pool=w_pool, pool_scale=pool_scale,
                   w_out=w_out, g_xa=g_xa, g_mem=g_mem, w_xq=w_xq, w_xkv=w_xkv, w_xo=w_xo, g_ffn=g_ffn, w_up=w_up,
                   w_ffconv=w_ffconv, w_down=w_down, g_final=g_final)
    m_in = dict(g_mix=m_g_mix, w_in=m_w_in, b_forget=m_b_forget, w_sconv=m_w_sconv, w_pool=m_w_pool,
                pool_scale=m_pool_scale, w_out=m_w_out, g_xa=m_g_xa, g_mem=m_g_mem, w_xq=m_w_xq, w_xkv=m_w_xkv,
                w_xo=m_w_xo, g_ffn=m_g_ffn, w_up=m_w_up, w_ffconv=m_w_ffconv, w_down=m_w_down, g_final=m_g_final)
    v_in = dict(g_mix=v_g_mix, w_in=v_w_in, b_forget=v_b_forget, w_sconv=v_w_sconv, w_pool=v_w_pool,
                pool_scale=v_pool_scale, w_out=v_w_out, g_xa=v_g_xa, g_mem=v_g_mem, w_xq=v_w_xq, w_xkv=v_w_xkv,
                w_xo=v_w_xo, g_ffn=v_g_ffn, w_up=v_w_up, w_ffconv=v_w_ffconv, w_down=v_w_down, g_final=v_g_final)
    depth = w_in.shape[0]
    me = 4 * lax.axis_index("x") + 2 * lax.axis_index("y") + lax.axis_index("c")
    h = x[0]
    memv = mem[0]
    s = h.shape[0]
    tq = min(ATT_TQ, s)
    tables = {"rope": _rope_tables(positions[0]), "dil": _bias_tables("dil", tq, tq),
              "fox": _bias_tables("fox", tq, tq)}

    w_in_r = _relay_in_cols(w_in)
    conv_shard = jnp.concatenate([w_sconv.reshape(-1), w_ffconv.reshape(-1)])
    conv_shard = jnp.concatenate([conv_shard, jnp.zeros((CONV_WORDS - conv_shard.shape[0],), F32)])
    conv_bits = lax.bitcast_convert_type(conv_shard, BF16).reshape(2 * CONV_WORDS // 1024, 1024)
    gathered = []
    order = jnp.zeros((), F32)
    for l in range(depth):
        shards = dict(w_in=w_in_r[l], w_out=w_out[l], w_xq=w_xq[l], w_xo=w_xo[l], w_xkv=w_xkv[l], w_up=w_up[l],
                      w_down=w_down[l])
        states = {}
        for group in ("in", "rest"):
            shards[GROUPS[group][0]] = shards[GROUPS[group][0]] + order
            xs = [_place_shard(shards[name], me, BF16, "place_%s_%d" % (name, l)) for name in GROUPS[group]]
            if l == 0 and group == "in":
                xs.append(_place_shard(conv_bits, me, BF16, "place_conv"))
            states[group], token = _exchange_start(xs, False, "gather_%s_start_%d" % (group, l))
            order = order + token[0, 0]
        gathered.append(_GatheredWeights(states, l))
    gathered[0].need("in", tables["rope"][0])
    conv_all = lax.bitcast_convert_type(gathered[0].extra[0].reshape(N_DEV, CONV_WORDS, 2), F32)
    n_sc = depth * 3 * (GROUP // N_DEV)
    sconv_full = conv_all[:, :n_sc].reshape(N_DEV, depth, 3, GROUP // N_DEV).transpose(1, 2, 0, 3).reshape(
        depth, 3, GROUP)
    ffconv_full = conv_all[:, n_sc:n_sc + depth * 3 * FF_SHARD].reshape(N_DEV, depth, 3, FF_SHARD).transpose(
        1, 0, 2, 3)

    smalls = []
    for l in range(depth):
        smalls.append(dict(
            g_mix=g_mix[l], g_xa=g_xa[l], g_mem=g_mem[l], g_ffn=g_ffn[l], pool_scale=pool_scale[l],
            w_pool_bd=_block_diag(w_pool[l]), w_sconv=sconv_full[l], w_ffconv=ffconv_full[l],
            b_forget_pad=jnp.concatenate([b_forget[l], jnp.zeros((128 - N_HEADS,), F32)]).reshape(1, 128)))
    smalls[0]["g_mix"] = smalls[0]["g_mix"] + order

    saved = []
    for l in range(depth):
        h, sv = _layer_fwd(h, memv, gathered[l], smalls[l], tables)
        saved.append(sv)
    loss_part, dh, dg_final = _loss_head(h, g_final, loss_target[0])
    loss = lax.psum(loss_part[0, 0], MESH_AXES)

    small_grads = [None] * depth
    scatters = {}

    def pieces_of(big, group):
        return [big[name].reshape(PIECE_SHAPES[name]) for name in GROUPS[group]]

    def rider(grads):
        flat = jnp.concatenate([g.reshape(-1) for g in grads])
        rows = -(-flat.shape[0] // 1024)
        flat = jnp.concatenate([flat, jnp.zeros((rows * 1024 - flat.shape[0],), F32)])
        return jnp.broadcast_to(flat.reshape(1, rows, 1024), (N_DEV, rows, 1024))

    for l in reversed(range(depth)):
        def rest_ready(big, small, l=l):
            ready = [small[n] for n in SMALL_WITH["rest"]] + ([dg_final] if l == depth - 1 else [])
            scatters[l, "rest"], token = _exchange_start(pieces_of(big, "rest") + [rider(ready)], True,
                                                         "scatter_rest_start_%d" % l)
            return token[0, 0]

        dh, big, small_grads[l] = _layer_bwd(dh, memv, gathered[l], smalls[l], tables, saved[l], rest_ready)
        xs = pieces_of(big, "in") + [rider([small_grads[l][n] for n in SMALL_WITH["in"]])]
        scatters[l, "in"], token = _exchange_start(xs, True, "scatter_in_start_%d" % l)
        if l > 0:
            smalls[l - 1]["w_ffconv"] = smalls[l - 1]["w_ffconv"] + token[0, 0]
    grad_x = dh[None]

    parts, owns, small_parts = {}, {}, {}

    def wait_group(group, after):
        for l in reversed(range(depth)):
            got, given = _exchange_wait(scatters[l, group], after, "scatter_%s_wait_%d" % (group, l))
            for name, g, x in zip(GROUPS[group], got, given):
                parts.setdefault(name, [None] * depth)[l] = g
                owns.setdefault(name, [None] * depth)[l] = x
            flat = lax.dynamic_update_slice_in_dim(got[-1], given[-1][:1], me, axis=0).reshape(N_DEV, -1)
            off = 0
            for name in SMALL_WITH[group] + (("g_final",) if group == "rest" and l == depth - 1 else ()):
                shape = SMALL_SHAPES.get(name, weights[name].shape[-1:] if name == "g_final"
                                         else weights[name].shape[1:])
                n = 1
                for dim in shape:
                    n *= dim
                small_parts.setdefault(name, [None] * depth)[l] = flat[:, off:off + n].reshape((N_DEV,) + shape)
                off += n

    results = {}

    def update(name, w3, m3, v3):
        outs = _adamw(parts[name], owns.get(name), me, w3, m3, v3, "adamw_" + name)
        results[name] = [o.reshape(weights[name].shape) for o in outs]

    wait_group("rest", grad_x)
    for name in GROUPS["rest"]:
        update(name, weights[name], m_in[name], v_in[name])
    wait_group("in", results["w_down"][1])
    outs = _adamw(parts["w_in"], owns["w_in"], me, w_in_r, _relay_in_cols(m_w_in), _relay_in_cols(v_w_in),
                  "adamw_w_in")
    results["w_in"] = [_unrelay_in_cols(o) for o in outs]
    for name in SMALL_NAMES + ("g_final",):
        wv = weights[name]
        p = small_parts[name][depth - 1] if name == "g_final" else jnp.stack(small_parts[name], axis=1)
        if name == "w_sconv":
            p = lax.dynamic_slice_in_dim(p, me * (GROUP // N_DEV), GROUP // N_DEV, axis=3)
        elif name == "w_ffconv":
            p = lax.dynamic_index_in_dim(p, me, axis=2, keepdims=False)
        shape3 = (1, 1, wv.shape[0]) if wv.ndim == 1 else (1, -1, wv.shape[-1])
        w3 = wv.reshape(shape3)
        parts[name] = [p.reshape((N_DEV,) + w3.shape[1:])]
        update(name, w3, m_in[name].reshape(shape3), v_in[name].reshape(shape3))

    return (loss, grad_x, *[results[n][0] for n in WEIGHT_NAMES], *[results[n][1] for n in WEIGHT_NAMES],
            *[results[n][2] for n in WEIGHT_NAMES], *[results[n][3] for n in WEIGHT_NAMES])
```

```python
import functools

import jax
import jax.numpy as jnp
from jax import lax
from jax.experimental import pallas as pl
from jax.experimental.pallas import tpu as pltpu

F32 = jnp.float32
BF16 = jnp.bfloat16

N_DEV = 8
D_MODEL = 1024
GROUP = 256
HEAD_DIM = 64
N_HEADS = 4
N_IN = 2564
N_IN_PAD = 2688
COL_GATE = 2560
XA_HEADS = 4
XA_DIM = 256
MEM_LEN = 256
D_FF = 2816
FF_SHARD = 704
FF_HALF = 4
ROPE_THETA = 500000.0
ROPE_DIM = 16
RMS_EPS = 1e-6
NEG = -1e30
POOL_WINDOWS = (2, 4, 8, 16)
ADAM_LR, ADAM_B1, ADAM_B2, ADAM_EPS, ADAM_WD, ADAM_STEP = 0.001, 0.9, 0.999, 1e-08, 0.01, 10

ROW_TILE = 512
MM_TILE = 1024
ATT_TQ = 512
BWD_HEADS = 4
VMEM_LIMIT = 56 * 1024 * 1024
ADAMW_BLOCK_BYTES = 4 * 1024 * 1024
PLACE_BLOCK_BYTES = 4 * 1024 * 1024

MESH_AXES = ("x", "y", "c")


def _params(**kw):
    return pltpu.CompilerParams(vmem_limit_bytes=VMEM_LIMIT, **kw)


HBM_SPEC = pl.BlockSpec(memory_space=pltpu.HBM)
SEM_SPEC = pl.BlockSpec(memory_space=pltpu.SEMAPHORE)
DATAFLOW = pltpu.SideEffectType.DATAFLOW_SIDE_EFFECTING


def _peer_copies(x_ref, land_ref, send_sems, recv_sems, scatter):
    mx, my, mc = lax.axis_index("x"), lax.axis_index("y"), lax.axis_index("c")
    me = 4 * mx + 2 * my + mc
    pairs = []
    for k in range(1, N_DEV):
        kx, ky, kc = (k >> 2) & 1, (k >> 1) & 1, k & 1
        peer_lin = me ^ k
        send = pltpu.make_async_remote_copy(
            src_ref=x_ref.at[peer_lin] if scatter else land_ref.at[me], dst_ref=land_ref.at[me],
            send_sem=send_sems.at[k - 1], recv_sem=recv_sems.at[k - 1],
            device_id=(mx ^ kx, my ^ ky, mc ^ kc), device_id_type=pl.DeviceIdType.MESH)
        arrival = pltpu.make_async_remote_copy(
            src_ref=land_ref.at[peer_lin], dst_ref=land_ref.at[peer_lin],
            send_sem=send_sems.at[k - 1], recv_sem=recv_sems.at[k - 1],
            device_id=(mx, my, mc), device_id_type=pl.DeviceIdType.MESH)
        pairs.append((send, arrival))
    return pairs


def _exchange_start(xs, scatter, name):
    n = len(xs)
    ns = n if scatter else 0

    def body(*refs):
        srcs = refs[:ns] if scatter else (None,) * n
        lands, sends, recvs = refs[ns:ns + n], refs[ns + n:ns + 2 * n], refs[ns + 2 * n:ns + 3 * n]
        for t in range(n):
            for send, _ in _peer_copies(srcs[t], lands[t], sends[t], recvs[t], scatter):
                send.start()
        token = refs[-1]
        token[...] = jnp.zeros_like(token)

    sems = pltpu.SemaphoreType.DMA((N_DEV - 1,))
    operands = [pltpu.with_memory_space_constraint(x, pltpu.HBM) for x in xs]
    if scatter:
        operands += [pltpu.with_memory_space_constraint(lax.empty(x.shape, x.dtype), pltpu.HBM) for x in xs]
    outs = pl.pallas_call(
        body, name=name,
        out_shape=(sems,) * (2 * n) + tuple(pltpu.HBM(a.shape, a.dtype) for a in operands)
        + (jax.ShapeDtypeStruct((8, 128), F32),),
        in_specs=(HBM_SPEC,) * (ns + n),
        out_specs=(SEM_SPEC,) * (2 * n) + (HBM_SPEC,) * (ns + n) + (pl.BlockSpec(memory_space=pltpu.VMEM),),
        input_output_aliases={i: 2 * n + i for i in range(ns + n)},
        compiler_params=pltpu.CompilerParams(has_side_effects=DATAFLOW),
    )(*operands)
    return (outs[:-1], scatter), outs[-1]


def _exchange_wait(state, after, name):
    held, scatter = state
    n = len(held) // (4 if scatter else 3)
    ns = n if scatter else 0
    sems, thru = held[:2 * n], held[2 * n:]

    def body(*refs):
        srcs = refs[:ns] if scatter else (None,) * n
        lands, sends, recvs = refs[ns:ns + n], refs[ns + n:ns + 2 * n], refs[ns + 2 * n:ns + 3 * n]
        for t in range(n):
            for send, arrival in _peer_copies(srcs[t], lands[t], sends[t], recvs[t], scatter):
                send.wait_send()
                arrival.wait_recv()

    outs = pl.pallas_call(
        body, name=name,
        out_shape=tuple(pltpu.HBM(a.shape, a.dtype) for a in thru),
        in_specs=(HBM_SPEC,) * (ns + n) + (SEM_SPEC,) * (2 * n) + (pl.BlockSpec(memory_space=pl.ANY),),
        out_specs=(HBM_SPEC,) * (ns + n), input_output_aliases={i: i for i in range(ns + n)},
        compiler_params=pltpu.CompilerParams(has_side_effects=DATAFLOW),
    )(*thru, *sems, after)
    return list(outs[ns:]), list(outs[:ns])


def _place_shard(x, me, dtype, name):
    r, c = x.shape
    tr = r
    if r * c * 4 > PLACE_BLOCK_BYTES:
        for cand in (512, 256, 128, 64, 32, 16):
            if r % cand == 0 and cand * c * 4 <= PLACE_BLOCK_BYTES:
                tr = cand
                break

    def body(me_ref, x_ref, o_ref):
        o_ref[...] = x_ref[...].astype(o_ref.dtype)

    return pl.pallas_call(
        body, name=name, out_shape=jax.ShapeDtypeStruct((N_DEV, r, c), dtype),
        grid_spec=pltpu.PrefetchScalarGridSpec(
            num_scalar_prefetch=1, grid=(r // tr,),
            in_specs=[pl.BlockSpec((tr, c), lambda i, me_ref: (i, 0))],
            out_specs=pl.BlockSpec((None, tr, c), lambda i, me_ref: (me_ref[0], i, 0))),
        compiler_params=_params(),
    )(me.reshape(1), x)


NN = ((1,), (0,))
NT = ((1,), (1,))
TN = ((0,), (0,))


def _matmul(a, b, out_shape, *, grid, a_spec, b_spec, o_spec, dims, nred, name, res=None, res_spec=None,
            out_dtype=F32, slabs=0):
    has_res = res is not None

    def body(*refs):
        a_ref, b_ref = refs[0], refs[1]
        r_ref = refs[2] if has_res else None
        o_ref = refs[3] if has_res else refs[2]
        if slabs:
            part = None
            for n in range(slabs):
                term = lax.dot_general(a_ref[n].astype(BF16), b_ref[n].astype(BF16), (dims, ((), ())),
                                       preferred_element_type=F32)
                part = term if part is None else part + term
        else:
            part = lax.dot_general(a_ref[...].astype(BF16), b_ref[...].astype(BF16), (dims, ((), ())),
                                   preferred_element_type=F32)
        if nred == 1:
            if has_res:
                part = part + r_ref[...]
            o_ref[...] = part.astype(o_ref.dtype)
        else:
            acc = refs[-1]
            r = pl.program_id(2)

            @pl.when(r == 0)
            def _():
                acc[...] = part

            @pl.when(r > 0)
            def _():
                acc[...] += part

            @pl.when(r == nred - 1)
            def _():
                tot = acc[...]
                if has_res:
                    tot = tot + r_ref[...]
                o_ref[...] = tot.astype(o_ref.dtype)

    in_specs = [a_spec, b_spec] + ([res_spec] if has_res else [])
    args = (a, b) + ((res,) if has_res else ())
    acc_shape = tuple(d for d in o_spec.block_shape if d is not None)
    return pl.pallas_call(
        body, name=name, grid=grid, out_shape=jax.ShapeDtypeStruct(out_shape, out_dtype),
        in_specs=in_specs, out_specs=o_spec,
        scratch_shapes=[pltpu.VMEM(acc_shape, F32)] if nred > 1 else [],
        compiler_params=_params(),
    )(*args)


def _mm_nn(a, w, name, res=None, tn=None, out_dtype=F32):
    m, k = a.shape
    n = w.shape[1]
    tn = tn or n
    tm = min(MM_TILE, m)
    ospec = pl.BlockSpec((tm, tn), lambda i, j, r: (i, j))
    return _matmul(a, w, (m, n), grid=(m // tm, n // tn, 1),
                   a_spec=pl.BlockSpec((tm, k), lambda i, j, r: (i, 0)),
                   b_spec=pl.BlockSpec((k, tn), lambda i, j, r: (0, j)),
                   o_spec=ospec, dims=NN, nred=1, name=name, res=res, res_spec=ospec if res is not None else None,
                   out_dtype=out_dtype)


def _mm_nt(a, w, name, out_dtype=F32):
    m, n = a.shape
    k = w.shape[0]
    tm = min(MM_TILE, m)
    return _matmul(a, w, (m, k), grid=(m // tm, 1, 1),
                   a_spec=pl.BlockSpec((tm, n), lambda i, j, r: (i, 0)),
                   b_spec=pl.BlockSpec((k, n), lambda i, j, r: (0, 0)),
                   o_spec=pl.BlockSpec((tm, k), lambda i, j, r: (i, 0)), dims=NT, nred=1, name=name,
                   out_dtype=out_dtype)


def _norm_matmul(h, g, b, out_shape, *, grid, b_spec, o_spec, name, out_dtype=F32, dims=NN):
    s, d = h.shape
    tm = s // grid[0]

    def body(h_ref, g_ref, b_ref, o_ref, xn_ref):
        @pl.when(pl.program_id(1) == 0)
        def _():
            hv = h_ref[...]
            r = lax.rsqrt(jnp.mean(hv * hv, axis=-1, keepdims=True) + RMS_EPS)
            xn_ref[...] = (hv * r * g_ref[...]).astype(xn_ref.dtype)

        o_ref[...] = lax.dot_general(xn_ref[...], b_ref[...].astype(BF16), (dims, ((), ())),
                                     preferred_element_type=F32).astype(o_ref.dtype)

    row = pl.BlockSpec((tm, d), lambda i, j: (i, 0))
    return pl.pallas_call(
        body, name=name, grid=grid,
        out_shape=(jax.ShapeDtypeStruct(out_shape, out_dtype), jax.ShapeDtypeStruct((s, d), BF16)),
        in_specs=[row, pl.BlockSpec((1, d), lambda i, j: (0, 0)), b_spec],
        out_specs=(o_spec, row), compiler_params=_params(),
    )(h, g.reshape(1, d), b)


def _matmul_rms_bwd(a, w, h, g, res, name, tm=ROW_TILE, dims=NT):
    slabs = a.shape[0] if a.ndim == 3 else 0
    s, n = a.shape[-2:]
    d = w.shape[-2] if dims == NT else w.shape[-1]
    tm = min(tm, s)

    def body(a_ref, w_ref, h_ref, g_ref, r_ref, dh_ref, dg_ref):
        if slabs:
            dy = None
            for j in range(slabs):
                term = lax.dot_general(a_ref[j].astype(BF16), w_ref[j].astype(BF16), (dims, ((), ())),
                                       preferred_element_type=F32)
                dy = term if dy is None else dy + term
        else:
            dy = lax.dot_general(a_ref[...].astype(BF16), w_ref[...].astype(BF16), (dims, ((), ())),
                                 preferred_element_type=F32)
        hv = h_ref[...]
        r = lax.rsqrt(jnp.mean(hv * hv, axis=-1, keepdims=True) + RMS_EPS)
        hn = hv * r
        u = dy * g_ref[...]
        dh_ref[...] = r * (u - hn * jnp.mean(u * hn, axis=-1, keepdims=True)) + r_ref[...]
        part = jnp.sum(dy * hn, axis=0, keepdims=True)

        @pl.when(pl.program_id(0) == 0)
        def _():
            dg_ref[...] = part

        @pl.when(pl.program_id(0) > 0)
        def _():
            dg_ref[...] += part

    row = pl.BlockSpec((tm, d), lambda i: (i, 0))
    vec = pl.BlockSpec((1, d), lambda i: (0, 0))
    if slabs:
        a_spec = pl.BlockSpec((slabs, tm, n), lambda i: (0, i, 0))
        w_spec = pl.BlockSpec(w.shape, lambda i: (0, 0, 0))
    else:
        a_spec = pl.BlockSpec((tm, n), lambda i: (i, 0))
        w_spec = pl.BlockSpec(w.shape, lambda i: (0, 0))
    dh, dg = pl.pallas_call(
        body, name=name, grid=(s // tm,),
        out_shape=(jax.ShapeDtypeStruct((s, d), F32), jax.ShapeDtypeStruct((1, d), F32)),
        in_specs=[a_spec, w_spec, row, vec, row], out_specs=(row, vec), compiler_params=_params(),
    )(a, w, h, g.reshape(1, d), res)
    return dh, dg.reshape(d)


GRAD_DTYPE = BF16


def _mm_tn(a, b, name, tk=512, tn=None):
    s, k = a.shape
    n = b.shape[1]
    tn = tn or n
    tk = min(tk, k)
    ts = s if b.dtype == BF16 else max(s // 2, 1)
    return _matmul(a, b, (k, n), grid=(k // tk, n // tn, s // ts),
                   a_spec=pl.BlockSpec((ts, tk), lambda i, j, r: (r, i)),
                   b_spec=pl.BlockSpec((ts, tn), lambda i, j, r: (r, j)),
                   o_spec=pl.BlockSpec((tk, tn), lambda i, j, r: (i, j)), dims=TN, nred=s // ts, name=name,
                   out_dtype=GRAD_DTYPE)


def _rms_fwd(h, g, name):
    s, d = h.shape
    tm = min(ROW_TILE, s)

    def body(h_ref, g_ref, o_ref):
        hv = h_ref[...]
        r = lax.rsqrt(jnp.mean(hv * hv, axis=-1, keepdims=True) + RMS_EPS)
        o_ref[...] = (hv * r * g_ref[...]).astype(o_ref.dtype)

    return pl.pallas_call(
        body, name=name, grid=(s // tm,), out_shape=jax.ShapeDtypeStruct((s, d), BF16),
        in_specs=[pl.BlockSpec((tm, d), lambda i: (i, 0)), pl.BlockSpec((1, d), lambda i: (0, 0))],
        out_specs=pl.BlockSpec((tm, d), lambda i: (i, 0)), compiler_params=_params(),
    )(h, g.reshape(1, d))


def _rms_bwd(dy, h, g, res, name):
    s, d = h.shape
    tm = min(ROW_TILE, s)
    has_res = res is not None

    def body(*refs):
        dy_ref, h_ref, g_ref = refs[:3]
        r_ref = refs[3] if has_res else None
        dh_ref, dg_ref = refs[-2], refs[-1]
        hv = h_ref[...]
        r = lax.rsqrt(jnp.mean(hv * hv, axis=-1, keepdims=True) + RMS_EPS)
        hn = hv * r
        dyv = dy_ref[...].astype(F32)
        u = dyv * g_ref[...]
        dh = r * (u - hn * jnp.mean(u * hn, axis=-1, keepdims=True))
        if has_res:
            dh = dh + r_ref[...]
        dh_ref[...] = dh
        part = jnp.sum(dyv * hn, axis=0, keepdims=True)

        @pl.when(pl.program_id(0) == 0)
        def _():
            dg_ref[...] = part

        @pl.when(pl.program_id(0) > 0)
        def _():
            dg_ref[...] += part

    row = pl.BlockSpec((tm, d), lambda i: (i, 0))
    vec = pl.BlockSpec((1, d), lambda i: (0, 0))
    dh, dg = pl.pallas_call(
        body, name=name, grid=(s // tm,),
        out_shape=(jax.ShapeDtypeStruct((s, d), F32), jax.ShapeDtypeStruct((1, d), F32)),
        in_specs=[row, row, vec] + ([row] if has_res else []),
        out_specs=(row, vec), compiler_params=_params(),
    )(*((dy, h, g.reshape(1, d)) + ((res,) if has_res else ())))
    return dh, dg.reshape(d)


def _loss_head(h, g, target):
    s, d = h.shape
    tm = min(ROW_TILE, s)

    def body(h_ref, g_ref, t_ref, loss_ref, dh_ref, dg_ref):
        hv = h_ref[...]
        r = lax.rsqrt(jnp.mean(hv * hv, axis=-1, keepdims=True) + RMS_EPS)
        hn = hv * r
        gv = g_ref[...]
        err = hn * gv - t_ref[...]
        rows = jnp.mean(err * err, axis=-1, keepdims=True)
        lpart = 0.5 * jnp.sum(rows, axis=0, keepdims=True) + jnp.zeros((1, 128), F32)
        dy = err * (1.0 / d)
        u = dy * gv
        dh_ref[...] = r * (u - hn * jnp.mean(u * hn, axis=-1, keepdims=True))
        gpart = jnp.sum(dy * hn, axis=0, keepdims=True)

        @pl.when(pl.program_id(0) == 0)
        def _():
            dg_ref[...] = gpart
            loss_ref[...] = lpart

        @pl.when(pl.program_id(0) > 0)
        def _():
            dg_ref[...] += gpart
            loss_ref[...] += lpart

    row = pl.BlockSpec((tm, d), lambda i: (i, 0))
    vec = pl.BlockSpec((1, d), lambda i: (0, 0))
    return pl.pallas_call(
        body, name="loss_head", grid=(s // tm,),
        out_shape=(jax.ShapeDtypeStruct((1, 128), F32), jax.ShapeDtypeStruct((s, d), F32),
                   jax.ShapeDtypeStruct((1, d), F32)),
        in_specs=[row, vec, row],
        out_specs=(pl.BlockSpec((1, 128), lambda i: (0, 0)), row, vec), compiler_params=_params(),
    )(h, g.reshape(1, d), target)


def _shift_down(x, k):
    return pltpu.roll(x, k, 0)


def _shift_up(x, k):
    return pltpu.roll(x, x.shape[0] - k, 0)


def _conv3(x, w):
    return w[2:3, :] * x + w[1:2, :] * _shift_down(x, 1) + w[0:1, :] * _shift_down(x, 2)


def _conv3_t(x, w):
    return w[2:3, :] * x + w[1:2, :] * _shift_up(x, 1) + w[0:1, :] * _shift_up(x, 2)


def _sigmoid(x):
    return 1.0 / (1.0 + jnp.exp(-x))


def _prev_map(tile, halo, col):
    return lambda i: (jnp.maximum(i * (tile // halo) - 1, 0), col)


def _next_map(tile, halo, col, nrows):
    return lambda i: (jnp.minimum((i + 1) * (tile // halo), nrows // halo - 1), col)


def _sconv_fwd(proj, w):
    s = proj.shape[0]
    t = min(ROW_TILE, s)

    def body(cur_ref, prev_ref, w_ref, o_ref):
        i = pl.program_id(0)
        prev = prev_ref[...] * (i > 0).astype(F32)
        ext = jnp.concatenate([prev, cur_ref[...]], axis=0)
        sv = ext[:, 2 * GROUP:3 * GROUP] * ext[:, 0:GROUP]
        y = ext[:, GROUP:2 * GROUP] * _conv3(sv, w_ref[...])
        o_ref[...] = y[8:].astype(o_ref.dtype)

    return pl.pallas_call(
        body, name="sconv_fwd", grid=(s // t,), out_shape=jax.ShapeDtypeStruct((s, 4 * GROUP), BF16),
        in_specs=[pl.BlockSpec((t, 3 * GROUP), lambda i: (i, 0)),
                  pl.BlockSpec((8, 3 * GROUP), _prev_map(t, 8, 0)),
                  pl.BlockSpec((3, GROUP), lambda i: (0, 0))],
        out_specs=pl.BlockSpec((t, GROUP), lambda i: (i, 0)), compiler_params=_params(),
    )(proj, proj, w)


def _sconv_bwd(proj, w, dy):
    s = proj.shape[0]
    t = min(ROW_TILE, s)
    nt = s // t

    def body(cur_ref, prev_ref, next_ref, w_ref, dy_ref, dyn_ref, dp_ref, dw_ref):
        i = pl.program_id(0)
        first = (i > 0).astype(F32)
        last = (i < nt - 1).astype(F32)
        ext = jnp.concatenate([prev_ref[...] * first, cur_ref[...], next_ref[...] * last], axis=0)
        dye = jnp.concatenate([jnp.zeros((8, GROUP), F32), dy_ref[...], dyn_ref[...] * last], axis=0)
        hv, bv, cv = ext[:, 0:GROUP], ext[:, GROUP:2 * GROUP], ext[:, 2 * GROUP:3 * GROUP]
        wv = w_ref[...]
        sv = cv * hv
        conv = _conv3(sv, wv)
        dconv = dye * bv
        ds = _conv3_t(dconv, wv)
        dp = jnp.concatenate([ds * cv, dye * conv, ds * hv], axis=1)
        dp_ref[...] = dp[8:8 + t].astype(dp_ref.dtype)
        dc = dconv[8:8 + t]
        dw = jnp.concatenate([
            jnp.sum(dc * _shift_down(sv, 2)[8:8 + t], axis=0, keepdims=True),
            jnp.sum(dc * _shift_down(sv, 1)[8:8 + t], axis=0, keepdims=True),
            jnp.sum(dc * sv[8:8 + t], axis=0, keepdims=True),
            jnp.zeros((5, GROUP), F32)], axis=0)

        @pl.when(i == 0)
        def _():
            dw_ref[...] = dw

        @pl.when(i > 0)
        def _():
            dw_ref[...] += dw

    dp, dw = pl.pallas_call(
        body, name="sconv_bwd", grid=(nt,),
        out_shape=(jax.ShapeDtypeStruct((s, N_IN_PAD), BF16), jax.ShapeDtypeStruct((8, GROUP), F32)),
        in_specs=[pl.BlockSpec((t, 3 * GROUP), lambda i: (i, 0)),
                  pl.BlockSpec((8, 3 * GROUP), _prev_map(t, 8, 0)),
                  pl.BlockSpec((8, 3 * GROUP), _next_map(t, 8, 0, s)),
                  pl.BlockSpec((3, GROUP), lambda i: (0, 0)),
                  pl.BlockSpec((t, GROUP), lambda i: (i, 0)),
                  pl.BlockSpec((8, GROUP), _next_map(t, 8, 0, s))],
        out_specs=(pl.BlockSpec((t, 3 * GROUP), lambda i: (i, 0)), pl.BlockSpec((8, GROUP), lambda i: (0, 0))),
        compiler_params=_params(),
    )(proj, proj, proj, w, dy, dy)
    return dp, dw[:3]


def _lane_window(shape):
    lane = lax.broadcasted_iota(jnp.int32, shape, 1)
    return lane, jnp.where(lane < 64, 2.0, jnp.where(lane < 128, 4.0, jnp.where(lane < 192, 8.0, 16.0)))


def _by_group(lane, s1, s2, s3, s4):
    return jnp.where(lane < 64, s1, jnp.where(lane < 128, s2, jnp.where(lane < 192, s3, s4)))


def _pool_z(ext, row0):
    s1 = ext + _shift_down(ext, 1)
    s2 = s1 + _shift_down(s1, 2)
    s3 = s2 + _shift_down(s2, 4)
    s4 = s3 + _shift_down(s3, 8)
    lane, win = _lane_window(ext.shape)
    tpos = (lax.broadcasted_iota(jnp.int32, ext.shape, 0) + (row0 - 16 + 1)).astype(F32)
    cnt = jnp.maximum(jnp.minimum(tpos, win), 1.0)
    return _by_group(lane, s1, s2, s3, s4) / cnt - ext


ANY_SPEC = pl.BlockSpec(memory_space=pl.ANY)


def _pool_fwd(proj, wbd, scale, ybuf):
    s = proj.shape[0]
    t = min(ROW_TILE, s)
    col = (COL_GATE - GROUP) // GROUP

    def body(cur_ref, prev_ref, w_ref, sc_ref, buf_ref, o_ref):
        i = pl.program_id(0)
        ext = jnp.concatenate([prev_ref[...] * (i > 0).astype(F32), cur_ref[...]], axis=0)
        z = _pool_z(ext, i * t)[16:]
        y = jnp.dot(z.astype(BF16), w_ref[...].astype(BF16), preferred_element_type=F32)
        o_ref[...] = (y * sc_ref[...]).astype(o_ref.dtype)

    return pl.pallas_call(
        body, name="pool_fwd", grid=(s // t,), out_shape=jax.ShapeDtypeStruct(ybuf.shape, ybuf.dtype),
        in_specs=[pl.BlockSpec((t, GROUP), lambda i: (i, col)),
                  pl.BlockSpec((16, GROUP), _prev_map(t, 16, col)),
                  pl.BlockSpec((GROUP, GROUP), lambda i: (0, 0)),
                  pl.BlockSpec((1, GROUP), lambda i: (0, 0)), ANY_SPEC],
        out_specs=pl.BlockSpec((t, GROUP), lambda i: (i, 3)), input_output_aliases={4: 0},
        compiler_params=_params(),
    )(proj, proj, wbd, scale.reshape(1, GROUP), ybuf)


def _pool_bwd(proj, wbd, scale, dy, dbuf):
    s = proj.shape[0]
    t = min(ROW_TILE, s)
    nt = s // t
    col = (COL_GATE - GROUP) // GROUP

    def body(cur_ref, prev_ref, w_ref, sc_ref, dy_ref, dyn_ref, buf_ref, dp_ref, dw_ref, dsc_ref):
        i = pl.program_id(0)
        ext = jnp.concatenate([prev_ref[...] * (i > 0).astype(F32), cur_ref[...]], axis=0)
        z = _pool_z(ext, i * t)[16:]
        wv = w_ref[...].astype(BF16)
        dyc = dy_ref[...]
        dye = jnp.concatenate([dyc, dyn_ref[...] * (i < nt - 1).astype(F32)], axis=0) * sc_ref[...]
        dz = lax.dot_general(dye.astype(BF16), wv, (NT, ((), ())), preferred_element_type=F32)
        lane, win = _lane_window(dz.shape)
        tpos = (lax.broadcasted_iota(jnp.int32, dz.shape, 0) + (i * t + 1)).astype(F32)
        e = dz / jnp.minimum(tpos, win)
        f1 = e + _shift_up(e, 1)
        f2 = f1 + _shift_up(f1, 2)
        f3 = f2 + _shift_up(f2, 4)
        f4 = f3 + _shift_up(f3, 8)
        dp = _by_group(lane, f1, f2, f3, f4) - dz
        dp_ref[...] = dp[:t].astype(dp_ref.dtype)
        zb = z.astype(BF16)
        y = jnp.dot(zb, wv, preferred_element_type=F32)
        dsc = jnp.sum(dyc * y, axis=0, keepdims=True)
        dw = lax.dot_general(zb, dye[:t].astype(BF16), (TN, ((), ())), preferred_element_type=F32)

        @pl.when(i == 0)
        def _():
            dw_ref[...] = dw
            dsc_ref[...] = dsc

        @pl.when(i > 0)
        def _():
            dw_ref[...] += dw
            dsc_ref[...] += dsc

    dp, dw, dsc = pl.pallas_call(
        body, name="pool_bwd", grid=(nt,),
        out_shape=(jax.ShapeDtypeStruct(dbuf.shape, dbuf.dtype), jax.ShapeDtypeStruct((GROUP, GROUP), F32),
                   jax.ShapeDtypeStruct((1, GROUP), F32)),
        in_specs=[pl.BlockSpec((t, GROUP), lambda i: (i, col)),
                  pl.BlockSpec((16, GROUP), _prev_map(t, 16, col)),
                  pl.BlockSpec((GROUP, GROUP), lambda i: (0, 0)),
                  pl.BlockSpec((1, GROUP), lambda i: (0, 0)),
                  pl.BlockSpec((t, GROUP), lambda i: (i, 3)),
                  pl.BlockSpec((16, GROUP), _next_map(t, 16, 3, s)), ANY_SPEC],
        out_specs=(pl.BlockSpec((t, GROUP), lambda i: (i, col)), pl.BlockSpec((GROUP, GROUP), lambda i: (0, 0)),
                   pl.BlockSpec((1, GROUP), lambda i: (0, 0))),
        input_output_aliases={6: 0}, compiler_params=_params(),
    )(proj, proj, wbd, scale.reshape(1, GROUP), dy, dy, dbuf)
    return dp, dw, dsc.reshape(GROUP)


FF_HALO = 16


def _ffn_gate_fwd(u0, w):
    s = u0.shape[1]
    t = min(ROW_TILE, s)

    def body(a_ref, ap_ref, g_ref, gp_ref, wa_ref, wg_ref, o_ref):
        first = (pl.program_id(1) > 0).astype(F32)
        a = _conv3(jnp.concatenate([ap_ref[...] * first, a_ref[...].astype(F32)], axis=0), wa_ref[...])[FF_HALO:]
        g = _conv3(jnp.concatenate([gp_ref[...] * first, g_ref[...].astype(F32)], axis=0), wg_ref[...])[FF_HALO:]
        o_ref[...] = (a * (g * _sigmoid(g))).astype(o_ref.dtype)

    def cur(off):
        return pl.BlockSpec((None, t, FF_SHARD), lambda j, i: (j + off, i, 0))

    def prev(off):
        return pl.BlockSpec((None, FF_HALO, FF_SHARD),
                            lambda j, i: (j + off, jnp.maximum(i * (t // FF_HALO) - 1, 0), 0))

    def wspec(off):
        return pl.BlockSpec((None, 3, FF_SHARD), lambda j, i: (j + off, 0, 0))

    return pl.pallas_call(
        body, name="ffn_gate_fwd", grid=(FF_HALF, s // t),
        out_shape=jax.ShapeDtypeStruct((FF_HALF, s, FF_SHARD), BF16),
        in_specs=[cur(0), prev(0), cur(FF_HALF), prev(FF_HALF), wspec(0), wspec(FF_HALF)],
        out_specs=pl.BlockSpec((None, t, FF_SHARD), lambda j, i: (j, i, 0)), compiler_params=_params(),
    )(u0, u0, u0, u0, w, w)


def _ffn_gate_bwd(u0, w, dact):
    s = u0.shape[1]
    t = min(ROW_TILE, s)
    nt = s // t

    def body(c_ref, p_ref, n_ref, w_ref, d_ref, dn_ref, du_ref, dw_ref):
        i = pl.program_id(1)
        first = (i > 0).astype(F32)
        last = (i < nt - 1).astype(F32)
        dext = jnp.concatenate([jnp.zeros((FF_HALO, FF_SHARD), F32), d_ref[...].astype(F32), dn_ref[...] * last],
                               axis=0)
        ext = [jnp.concatenate([p_ref[n] * first, c_ref[n].astype(F32), n_ref[n] * last], axis=0) for n in range(2)]
        a = _conv3(ext[0], w_ref[0])
        g = _conv3(ext[1], w_ref[1])
        sg = _sigmoid(g)
        silu = g * sg
        dus = (dext * silu, dext * a * (sg + silu * (1.0 - sg)))
        mine = slice(FF_HALO, FF_HALO + t)
        for n in range(2):
            du_ref[n] = _conv3_t(dus[n], w_ref[n])[mine].astype(du_ref.dtype)
            dc = dus[n][mine]
            dw = jnp.concatenate([
                jnp.sum(dc * _shift_down(ext[n], 2)[mine], axis=0, keepdims=True),
                jnp.sum(dc * _shift_down(ext[n], 1)[mine], axis=0, keepdims=True),
                jnp.sum(dc * ext[n][mine], axis=0, keepdims=True),
                jnp.zeros((5, FF_SHARD), F32)], axis=0)

            @pl.when(i == 0)
            def _(n=n, dw=dw):
                dw_ref[n] = dw

            @pl.when(i > 0)
            def _(n=n, dw=dw):
                dw_ref[n] += dw

    def pair(rows, row_map):
        return pl.BlockSpec((2, None, rows, FF_SHARD), lambda j, i: (0, j, row_map(i), 0))

    prev_row = lambda i: jnp.maximum(i * (t // FF_HALO) - 1, 0)
    next_row = lambda i: jnp.minimum((i + 1) * (t // FF_HALO), s // FF_HALO - 1)
    u2 = u0.reshape(2, FF_HALF, s, FF_SHARD)
    du, dw = pl.pallas_call(
        body, name="ffn_gate_bwd", grid=(FF_HALF, nt),
        out_shape=(jax.ShapeDtypeStruct((2, FF_HALF, s, FF_SHARD), BF16),
                   jax.ShapeDtypeStruct((2, FF_HALF, 8, FF_SHARD), F32)),
        in_specs=[pair(t, lambda i: i), pair(FF_HALO, prev_row), pair(FF_HALO, next_row), pair(3, lambda i: 0),
                  pl.BlockSpec((None, t, FF_SHARD), lambda j, i: (j, i, 0)),
                  pl.BlockSpec((None, FF_HALO, FF_SHARD), lambda j, i: (j, next_row(i), 0))],
        out_specs=(pair(t, lambda i: i), pair(8, lambda i: 0)),
        compiler_params=_params(),
    )(u2, u2, u2, w.reshape(2, FF_HALF, 3, FF_SHARD), dact, dact)
    return du.reshape(2 * FF_HALF, s, FF_SHARD), dw.reshape(2 * FF_HALF, 8, FF_SHARD)[:, :3]


def _rope_tables(positions):
    inv_freq = ROPE_THETA ** (-jnp.arange(0, ROPE_DIM, 2, dtype=F32) / ROPE_DIM)
    ang = positions.astype(F32)[:, None] * inv_freq
    cos, sin = jnp.cos(ang), jnp.sin(ang)
    s = positions.shape[0]
    half = ROPE_DIM // 2
    rest = HEAD_DIM - ROPE_DIM
    ca = jnp.concatenate([cos, cos, jnp.ones((s, rest), F32)], axis=1)
    cb = jnp.concatenate([-sin, jnp.zeros((s, HEAD_DIM - half), F32)], axis=1)
    cc = jnp.concatenate([jnp.zeros((s, half), F32), sin, jnp.zeros((s, rest), F32)], axis=1)
    return tuple(jnp.tile(tb, (1, N_HEADS)) for tb in (ca, cb, cc))


QK_WIDE = 128
LANE_CQ, LANE_CK = 64, 67
KT_ROWS = 80


def _three_bf16(x):
    hi = x.astype(BF16).astype(F32)
    mid = (x - hi).astype(BF16).astype(F32)
    lo = (x - hi - mid).astype(BF16).astype(F32)
    return hi, mid, lo


def _heads_split(proj, col, tables, c, name):
    s = proj.shape[0]
    t = min(ROW_TILE, s)
    rope = tables is not None
    wide = c is not None
    width = QK_WIDE if wide else HEAD_DIM

    def body(*refs):
        x_ref = refs[0]
        q_ref, k_ref, v_ref, kt_ref, vt_ref = refs[-5:]
        xv = x_ref[...]
        parts = [xv[:, 0:GROUP], xv[:, GROUP:2 * GROUP], xv[:, 2 * GROUP:3 * GROUP]]
        if rope:
            ca, cb, cc = refs[1][...], refs[2][...], refs[3][...]
            for n in range(2):
                p = parts[n]
                parts[n] = p * ca + pltpu.roll(p, GROUP - 8, 1) * cb + pltpu.roll(p, 8, 1) * cc
        parts[0] = parts[0] * (HEAD_DIM ** -0.5)
        k_t, v_t = parts[1].T, parts[2].T
        ones_row = jnp.where(lax.broadcasted_iota(jnp.int32, (KT_ROWS - HEAD_DIM, t), 0) == 0, 1.0, 0.0)
        lane = lax.broadcasted_iota(jnp.int32, (t, QK_WIDE), 1)
        zeros = jnp.zeros((t, QK_WIDE - HEAD_DIM), F32)
        for h in range(N_HEADS):
            hs = slice(h * HEAD_DIM, (h + 1) * HEAD_DIM)
            qh, kh = parts[0][:, hs], parts[1][:, hs]
            if wide:
                terms = _three_bf16(refs[-6][:, h:h + 1])
                qh = jnp.concatenate([qh, zeros], axis=1)
                kh = jnp.concatenate([kh, zeros], axis=1)
                for n in range(3):
                    qh = jnp.where(lane == LANE_CQ + n, terms[n], jnp.where(lane == LANE_CK + n, 1.0, qh))
                    kh = jnp.where(lane == LANE_CK + n, -terms[n], jnp.where(lane == LANE_CQ + n, 1.0, kh))
            q_ref[h] = qh.astype(q_ref.dtype)
            k_ref[h] = kh.astype(k_ref.dtype)
            v_ref[h] = parts[2][:, hs].astype(v_ref.dtype)
            kt_ref[h] = jnp.concatenate([k_t[hs, :], ones_row], axis=0).astype(kt_ref.dtype)
            vt_ref[h] = v_t[hs, :].astype(vt_ref.dtype)

    tab = pl.BlockSpec((t, GROUP), lambda i: (i, 0))
    qk = pl.BlockSpec((N_HEADS, t, width), lambda i: (0, i, 0))
    heads = pl.BlockSpec((N_HEADS, t, HEAD_DIM), lambda i: (0, i, 0))
    heads_t = pl.BlockSpec((N_HEADS, HEAD_DIM, t), lambda i: (0, 0, i))
    qk_shape = jax.ShapeDtypeStruct((N_HEADS, s, width), BF16)
    return pl.pallas_call(
        body, name=name, grid=(s // t,),
        out_shape=(qk_shape, qk_shape, jax.ShapeDtypeStruct((N_HEADS, s, HEAD_DIM), BF16),
                   jax.ShapeDtypeStruct((N_HEADS, KT_ROWS, s), BF16),
                   jax.ShapeDtypeStruct((N_HEADS, HEAD_DIM, s), BF16)),
        in_specs=[pl.BlockSpec((t, 3 * GROUP), lambda i: (i, col))] + ([tab, tab, tab] if rope else [])
        + ([pl.BlockSpec((t, 128), lambda i: (i, 0))] if wide else []),
        out_specs=(qk, qk, heads, pl.BlockSpec((N_HEADS, KT_ROWS, t), lambda i: (0, 0, i)), heads_t),
        compiler_params=_params(),
    )(*((proj,) + (tuple(tables) if rope else ()) + ((c,) if wide else ())))


def _heads_merge(dqt, dk, dv, tables, name, dbuf, col):
    s = dv.shape[1]
    t = min(ROW_TILE, s)
    rope = tables is not None

    wide = dk.shape[2] == QK_WIDE

    def body(*refs):
        o_ref = refs[n_in + 1]
        dq = jnp.concatenate([refs[0][h, :HEAD_DIM, :] for h in range(N_HEADS)], axis=0).T
        parts = [dq] + [jnp.concatenate([r[h][:, :HEAD_DIM] for h in range(N_HEADS)], axis=1) for r in refs[1:3]]
        parts[0] = parts[0] * (HEAD_DIM ** -0.5)
        if rope:
            ca, cb, cc = refs[3][...], refs[4][...], refs[5][...]
            for n in range(2):
                p = parts[n]
                parts[n] = p * ca + pltpu.roll(p * cb, 8, 1) + pltpu.roll(p * cc, GROUP - 8, 1)
        o_ref[...] = jnp.concatenate(parts, axis=1).astype(o_ref.dtype)
        if wide:
            over_keys = jnp.concatenate([refs[0][h, HEAD_DIM:HEAD_DIM + 8, :] for h in range(N_HEADS)]
                                        + [jnp.zeros((128 - 8 * N_HEADS, t), F32)], axis=0).T
            lane = lax.broadcasted_iota(jnp.int32, (t, 128), 1)
            dc = jnp.zeros((t, 128), F32)
            for h in range(N_HEADS):
                dc = jnp.where(lane == h, over_keys[:, 8 * h:8 * h + 1] - refs[1][h][:, LANE_CK:LANE_CK + 1], dc)
            refs[n_in + 2][...] = dc

    tab = pl.BlockSpec((t, GROUP), lambda i: (i, 0))
    heads = pl.BlockSpec((N_HEADS, t, HEAD_DIM), lambda i: (0, i, 0))
    n_in = 6 if rope else 3
    dspec = pl.BlockSpec((t, 3 * GROUP), lambda i: (i, col))
    dshape = jax.ShapeDtypeStruct(dbuf.shape, dbuf.dtype)
    return pl.pallas_call(
        body, name=name, grid=(s // t,),
        out_shape=(dshape, jax.ShapeDtypeStruct((s, 128), F32)) if wide else dshape,
        in_specs=[pl.BlockSpec((N_HEADS, KT_ROWS, t), lambda i: (0, 0, i)),
                  pl.BlockSpec((N_HEADS, t, dk.shape[2]), lambda i: (0, i, 0)), heads]
        + ([tab, tab, tab] if rope else []) + [ANY_SPEC],
        out_specs=(dspec, pl.BlockSpec((t, 128), lambda i: (i, 0))) if wide else dspec,
        input_output_aliases={n_in: 0}, compiler_params=_params(),
    )(*((dqt, dk, dv) + (tuple(tables) if rope else ()) + (dbuf,)))


def _log_sigmoid(x):
    return jnp.minimum(x, 0.0) - jnp.log(1.0 + jnp.exp(-jnp.abs(x)))


def _scan_rows(x, reverse):
    n = x.shape[0]
    row = lax.broadcasted_iota(jnp.int32, x.shape, 0)
    k = 1
    while k < n:
        if reverse:
            x = x + jnp.where(row < n - k, _shift_up(x, k), 0.0)
        else:
            x = x + jnp.where(row >= k, _shift_down(x, k), 0.0)
        k *= 2
    return x


def _gate_cumsum(proj, bias):
    s = proj.shape[0]
    col = COL_GATE // 128

    def body(z_ref, b_ref, c_ref):
        c_ref[...] = _scan_rows(_log_sigmoid(z_ref[...] + b_ref[...]), False)

    return pl.pallas_call(
        body, name="gate_cumsum", grid=(1,), out_shape=jax.ShapeDtypeStruct((s, 128), F32),
        in_specs=[pl.BlockSpec((s, 128), lambda i: (0, col)), pl.BlockSpec((1, 128), lambda i: (0, 0))],
        out_specs=pl.BlockSpec((s, 128), lambda i: (0, 0)), compiler_params=_params(),
    )(proj, bias)


def _gate_cumsum_bwd(proj, bias, dc, dbuf):
    s = proj.shape[0]
    col = COL_GATE // 128

    def body(z_ref, b_ref, dc_ref, buf_ref, dz_ref, db_ref):
        dlogf = _scan_rows(dc_ref[...], True)
        dz = dlogf * _sigmoid(-(z_ref[...] + b_ref[...]))
        dz_ref[...] = dz.astype(dz_ref.dtype)
        db_ref[...] = jnp.sum(dz, axis=0, keepdims=True)

    return pl.pallas_call(
        body, name="gate_cumsum_bwd", grid=(1,),
        out_shape=(jax.ShapeDtypeStruct(dbuf.shape, dbuf.dtype), jax.ShapeDtypeStruct((1, 128), F32)),
        in_specs=[pl.BlockSpec((s, 128), lambda i: (0, col)), pl.BlockSpec((1, 128), lambda i: (0, 0)),
                  pl.BlockSpec((s, 128), lambda i: (0, 0)), ANY_SPEC],
        out_specs=(pl.BlockSpec((s, 128), lambda i: (0, col)), pl.BlockSpec((1, 128), lambda i: (0, 0))),
        input_output_aliases={3: 0}, compiler_params=_params(),
    )(proj, bias, dc, dbuf)


DIL_REACH = 2048


def _pair_weight(mode, d):
    if mode == "fox":
        return jnp.where(d >= 0, 1.0, 0.0)
    w1 = jnp.where(jnp.abs(d - 64) <= 64, 1.0, 0.0)
    w2 = jnp.where((d & 3) == 0, jnp.where(jnp.abs(d - 256) <= 256, 1.0, 0.0), 0.0)
    w3 = jnp.where((d & 15) == 0, jnp.where(jnp.abs(d - 1024) <= 1024, 1.0, 0.0), 0.0)
    return w1 + w2 + w3


def _bias_tables(mode, tq, tk):
    nb = 2 if mode == "fox" else DIL_REACH // tk + 1
    n = lax.broadcasted_iota(jnp.int32, (nb, tk, tq), 0)
    key = lax.broadcasted_iota(jnp.int32, (nb, tk, tq), 1)
    query = lax.broadcasted_iota(jnp.int32, (nb, tk, tq), 2)
    w = _pair_weight(mode, n * tk + query - key)
    return jnp.where(w > 0.0, jnp.log(jnp.maximum(w, 1.0)), NEG)


M_INIT = -1e29


def _first_key_chunk(mode, q0, tk):
    if mode == "fox":
        return 0
    return jnp.maximum(q0 - DIL_REACH, 0) // tk


def _attention_fwd(mode, q, k, vt, tab_t, ybuf, col):
    s, width = q.shape[1], q.shape[2]
    tq = min(ATT_TQ, s)
    tk = tq
    nb = tab_t.shape[0]

    def body(q_ref, k_ref, vt_ref, tab_ref, buf_ref, y_ref, o_ref, lse_ref):
        i = pl.program_id(0)
        lo = _first_key_chunk(mode, i * tq, tk)

        def step(c, carry):
            k0 = pl.multiple_of(c * tk, tk)
            tab = tab_ref[jnp.minimum(i - c, nb - 1)]
            scores = [lax.dot_general(k_ref[h, pl.ds(k0, tk), :], q_ref[h], (NT, ((), ())),
                                      preferred_element_type=F32) for h in range(N_HEADS)]
            stats, probs = [], []
            for h in range(N_HEADS):
                m, l = carry[3 * h:3 * h + 2]
                sc = scores[h] + tab
                m_new = jnp.maximum(m, jnp.max(sc, axis=0, keepdims=True))
                alpha = jnp.exp(m - m_new)
                p = jnp.exp(sc - m_new)
                stats.append((m_new, alpha * l + jnp.sum(p, axis=0, keepdims=True), alpha))
                probs.append(p.astype(BF16))
            pv = [jnp.dot(vt_ref[h, :, pl.ds(k0, tk)], probs[h], preferred_element_type=F32) for h in range(N_HEADS)]
            new = []
            for h in range(N_HEADS):
                m_new, l, alpha = stats[h]
                new += [m_new, l, alpha * carry[3 * h + 2] + pv[h]]
            return tuple(new)

        start = (jnp.full((1, tq), M_INIT, F32), jnp.zeros((1, tq), F32), jnp.zeros((HEAD_DIM, tq), F32))
        done = lax.fori_loop(lo, i + 1, step, start * N_HEADS)
        outs = []
        for h in range(N_HEADS):
            m, l, acc = done[3 * h:3 * h + 3]
            outs.append(acc / l)
            lse_ref[h] = m + jnp.log(l)
        out = jnp.concatenate(outs, axis=0).T
        y_ref[...] = out.astype(y_ref.dtype)
        o_ref[...] = out

    rowspec = pl.BlockSpec((N_HEADS, 1, tq), lambda i: (0, 0, i))
    return pl.pallas_call(
        body, name="attention_fwd_" + mode, grid=(s // tq,),
        out_shape=(jax.ShapeDtypeStruct(ybuf.shape, ybuf.dtype), jax.ShapeDtypeStruct((s, GROUP), F32),
                   jax.ShapeDtypeStruct((N_HEADS, 1, s), F32)),
        in_specs=[pl.BlockSpec((N_HEADS, tq, width), lambda i: (0, i, 0)),
                  pl.BlockSpec((N_HEADS, s, width), lambda i: (0, 0, 0)),
                  pl.BlockSpec((N_HEADS, HEAD_DIM, s), lambda i: (0, 0, 0)),
                  pl.BlockSpec((nb, tk, tq), lambda i: (0, 0, 0)), ANY_SPEC],
        out_specs=(pl.BlockSpec((tq, GROUP), lambda i: (i, col)), pl.BlockSpec((tq, GROUP), lambda i: (i, 0)),
                   rowspec),
        input_output_aliases={4: 0}, compiler_params=_params(),
    )(q, k, vt, tab_t, ybuf)


def _attention_delta(o, do, col):
    s = o.shape[0]
    t = min(ROW_TILE, s)

    def body(o_ref, do_ref, delta_ref, dob_ref):
        dov = do_ref[...]
        prod_t = (o_ref[...] * dov).T
        for h in range(N_HEADS):
            hs = slice(h * HEAD_DIM, (h + 1) * HEAD_DIM)
            delta_ref[h] = jnp.sum(prod_t[hs, :], axis=0, keepdims=True)
            dob_ref[h] = dov[:, hs].astype(dob_ref.dtype)

    return pl.pallas_call(
        body, name="attention_delta", grid=(s // t,),
        out_shape=(jax.ShapeDtypeStruct((N_HEADS, 1, s), F32), jax.ShapeDtypeStruct((N_HEADS, s, HEAD_DIM), BF16)),
        in_specs=[pl.BlockSpec((t, GROUP), lambda i: (i, 0)), pl.BlockSpec((t, GROUP), lambda i: (i, col))],
        out_specs=(pl.BlockSpec((N_HEADS, 1, t), lambda i: (0, 0, i)),
                   pl.BlockSpec((N_HEADS, t, HEAD_DIM), lambda i: (0, i, 0))),
        compiler_params=_params(),
    )(o, do)


def _attention_bwd(mode, q, k, v, kt, tab_t, dob, lse, delta):
    s, width = q.shape[1], q.shape[2]
    tq = min(ATT_TQ, s)
    tk = tq
    nq = s // tq
    nb = tab_t.shape[0]

    def body(q_ref, k_ref, v_ref, kt_ref, tab_ref, dob_ref, lse_ref, delta_ref, dqt_ref, dk_ref, dv_ref):
        i = pl.program_id(0)

        @pl.when(i == 0)
        def _():
            dqt_ref[...] = jnp.zeros_like(dqt_ref)

        hi = nq if mode == "fox" else jnp.minimum((i * tk + tk - 1 + DIL_REACH) // tq + 1, nq)
        for h0 in range(0, N_HEADS, BWD_HEADS):
            heads = range(h0, h0 + BWD_HEADS)

            def step(c, carry, heads=heads):
                q0 = pl.multiple_of(c * tq, tq)
                qs = pl.ds(q0, tq)
                tab = tab_ref[jnp.minimum(c - i, nb - 1)]
                qv = [q_ref[h, qs, :] for h in heads]
                dov = [dob_ref[h, qs, :] for h in heads]
                sc = [lax.dot_general(k_ref[h], qv[n], (NT, ((), ())), preferred_element_type=F32)
                      for n, h in enumerate(heads)]
                dp = [lax.dot_general(v_ref[h], dov[n], (NT, ((), ())), preferred_element_type=F32)
                      for n, h in enumerate(heads)]
                pb, dsb = [], []
                for n, h in enumerate(heads):
                    p = jnp.exp(sc[n] + tab - lse_ref[h, :, qs])
                    pb.append(p.astype(BF16))
                    dsb.append((p * (dp[n] - delta_ref[h, :, qs])).astype(BF16))
                new = []
                for n, h in enumerate(heads):
                    new += [carry[2 * n] + jnp.dot(dsb[n], qv[n], preferred_element_type=F32),
                            carry[2 * n + 1] + jnp.dot(pb[n], dov[n], preferred_element_type=F32)]
                for n, h in enumerate(heads):
                    dqt_ref[h, :, qs] += jnp.dot(kt_ref[h], dsb[n], preferred_element_type=F32)
                return tuple(new)

            start = (jnp.zeros((tk, width), F32), jnp.zeros((tk, HEAD_DIM), F32))
            done = lax.fori_loop(i, hi, step, start * BWD_HEADS)
            for n, h in enumerate(heads):
                dk_ref[h] = done[2 * n]
                dv_ref[h] = done[2 * n + 1]

    def full(shape):
        return pl.BlockSpec(shape, lambda i: (0, 0, 0))

    kblk = pl.BlockSpec((N_HEADS, tk, width), lambda i: (0, i, 0))
    vblk = pl.BlockSpec((N_HEADS, tk, HEAD_DIM), lambda i: (0, i, 0))
    return pl.pallas_call(
        body, name="attention_bwd_" + mode, grid=(s // tk,),
        out_shape=(jax.ShapeDtypeStruct((N_HEADS, KT_ROWS, s), F32), jax.ShapeDtypeStruct((N_HEADS, s, width), F32),
                   jax.ShapeDtypeStruct((N_HEADS, s, HEAD_DIM), F32)),
        in_specs=[full((N_HEADS, s, width)), kblk, vblk, pl.BlockSpec((N_HEADS, KT_ROWS, tk), lambda i: (0, 0, i)),
                  full((nb, tk, tq)), full((N_HEADS, s, HEAD_DIM)), full((N_HEADS, 1, s)), full((N_HEADS, 1, s))],
        out_specs=(full((N_HEADS, KT_ROWS, s)), kblk, vblk),
        compiler_params=_params(),
    )(q, k, v, kt, tab_t, dob, lse, delta)


def _xattn_fwd(qx, kvm):
    s = qx.shape[0]
    t = min(ROW_TILE, s)

    def body(q_ref, kv_ref, o_ref):
        heads = range(XA_HEADS)
        sc = [lax.dot_general(q_ref[:, h * XA_DIM:(h + 1) * XA_DIM].astype(BF16), kv_ref[h].astype(BF16),
                              (NT, ((), ())), preferred_element_type=F32) * (XA_DIM ** -0.5) for h in heads]
        probs = []
        for h in heads:
            e = jnp.exp(sc[h] - jnp.max(sc[h], axis=-1, keepdims=True))
            probs.append((e / jnp.sum(e, axis=-1, keepdims=True)).astype(BF16))
        outs = [jnp.dot(probs[h], kv_ref[XA_HEADS + h].astype(BF16), preferred_element_type=F32) for h in heads]
        for h in heads:
            o_ref[:, h * XA_DIM:(h + 1) * XA_DIM] = outs[h].astype(o_ref.dtype)

    return pl.pallas_call(
        body, name="xattn_fwd", grid=(s // t,), out_shape=jax.ShapeDtypeStruct((s, D_MODEL), BF16),
        in_specs=[pl.BlockSpec((t, D_MODEL), lambda i: (i, 0)),
                  pl.BlockSpec((2 * XA_HEADS, MEM_LEN, XA_DIM), lambda i: (0, 0, 0))],
        out_specs=pl.BlockSpec((t, D_MODEL), lambda i: (i, 0)), compiler_params=_params(),
    )(qx, kvm)


def _xattn_bwd(qx, kvm, do):
    s = qx.shape[0]
    t = min(ROW_TILE, s)

    def body(q_ref, kv_ref, do_ref, dq_ref, dkv_ref):
        i = pl.program_id(0)
        heads = range(XA_HEADS)
        qv = [q_ref[:, h * XA_DIM:(h + 1) * XA_DIM].astype(BF16) for h in heads]
        dov = [do_ref[:, h * XA_DIM:(h + 1) * XA_DIM].astype(BF16) for h in heads]
        kv = [kv_ref[h].astype(BF16) for h in heads]
        sc = [lax.dot_general(qv[h], kv[h], (NT, ((), ())), preferred_element_type=F32) * (XA_DIM ** -0.5)
              for h in heads]
        dp = [lax.dot_general(dov[h], kv_ref[XA_HEADS + h].astype(BF16), (NT, ((), ())), preferred_element_type=F32)
              for h in heads]
        pb, ds = [], []
        for h in heads:
            e = jnp.exp(sc[h] - jnp.max(sc[h], axis=-1, keepdims=True))
            p = e / jnp.sum(e, axis=-1, keepdims=True)
            pb.append(p.astype(BF16))
            ds.append((p * (dp[h] - jnp.sum(p * dp[h], axis=-1, keepdims=True)) * (XA_DIM ** -0.5)).astype(BF16))
        dq = [jnp.dot(ds[h], kv[h], preferred_element_type=F32) for h in heads]
        dk = [lax.dot_general(ds[h], qv[h], (TN, ((), ())), preferred_element_type=F32) for h in heads]
        dv = [lax.dot_general(pb[h], dov[h], (TN, ((), ())), preferred_element_type=F32) for h in heads]
        for h in heads:
            dq_ref[:, h * XA_DIM:(h + 1) * XA_DIM] = dq[h].astype(dq_ref.dtype)

        @pl.when(i == 0)
        def _():
            for h in heads:
                dkv_ref[h] = dk[h]
                dkv_ref[XA_HEADS + h] = dv[h]

        @pl.when(i > 0)
        def _():
            for h in heads:
                dkv_ref[h] += dk[h]
                dkv_ref[XA_HEADS + h] += dv[h]

    row = pl.BlockSpec((t, D_MODEL), lambda i: (i, 0))
    kvs = pl.BlockSpec((2 * XA_HEADS, MEM_LEN, XA_DIM), lambda i: (0, 0, 0))
    return pl.pallas_call(
        body, name="xattn_bwd", grid=(s // t,),
        out_shape=(jax.ShapeDtypeStruct((s, D_MODEL), BF16),
                   jax.ShapeDtypeStruct((2 * XA_HEADS, MEM_LEN, XA_DIM), F32)),
        in_specs=[row, kvs, row], out_specs=(row, kvs), compiler_params=_params(),
    )(qx, kvm, do)


def _adamw(parts, owns, me, w, m, v, name):
    nl, r, c = w.shape
    tr = r
    for cand in (256, 128, 64, 32, 16, 8):
        if r % cand == 0 and r > cand and N_DEV * cand * c * 4 <= ADAMW_BLOCK_BYTES:
            tr = cand
            break
    nt = r // tr
    per_layer = N_DEV + (1 if owns is not None else 0)

    def body(me_ref, *refs):
        w_ref, m_ref, v_ref, g_ref, d_ref, nm_ref, nv_ref = refs[nl * per_layer:]
        layer = pl.program_id(0)
        g = None
        for l in range(nl):
            p_refs = refs[l * per_layer:(l + 1) * per_layer]
            gl = None
            for d in range(N_DEV):
                term = p_refs[d][...].astype(F32)
                if owns is not None:
                    term = jnp.where(me_ref[0] == d, p_refs[N_DEV][...].astype(F32), term)
                gl = term if gl is None else gl + term
            g = gl if g is None else jnp.where(layer == l, gl, g)
        mn = ADAM_B1 * m_ref[...] + (1.0 - ADAM_B1) * g
        vn = ADAM_B2 * v_ref[...] + (1.0 - ADAM_B2) * (g * g)
        m_hat = mn / (1.0 - ADAM_B1 ** ADAM_STEP)
        v_hat = vn / (1.0 - ADAM_B2 ** ADAM_STEP)
        g_ref[...] = g
        d_ref[...] = -ADAM_LR * (m_hat / (jnp.sqrt(v_hat) + ADAM_EPS) + ADAM_WD * w_ref[...])
        nm_ref[...] = mn
        nv_ref[...] = vn

    def rows(l, ll, i):
        return jnp.where(ll == l, i, jnp.where(ll < l, 0, nt - 1))

    def part_spec(l, d):
        if owns is None:
            return pl.BlockSpec((None, tr, c), lambda ll, i, me_ref: (d, rows(l, ll, i), 0))
        return pl.BlockSpec((None, tr, c),
                            lambda ll, i, me_ref: (jnp.where(me_ref[0] == d, (d + 1) % N_DEV, d), rows(l, ll, i), 0))

    def own_spec(l):
        return pl.BlockSpec((None, tr, c), lambda ll, i, me_ref: (me_ref[0], rows(l, ll, i), 0))

    in_specs, operands = [], []
    for l in range(nl):
        in_specs += [part_spec(l, d) for d in range(N_DEV)]
        operands += [parts[l]] * N_DEV
        if owns is not None:
            in_specs.append(own_spec(l))
            operands.append(owns[l])
    blk = pl.BlockSpec((None, tr, c), lambda ll, i, me_ref: (ll, i, 0))
    shp = jax.ShapeDtypeStruct((nl, r, c), F32)
    return pl.pallas_call(
        body, name=name, out_shape=(shp, shp, shp, shp),
        grid_spec=pltpu.PrefetchScalarGridSpec(
            num_scalar_prefetch=1, grid=(nl, nt), in_specs=in_specs + [blk, blk, blk],
            out_specs=(blk, blk, blk, blk)),
        compiler_params=_params(),
    )(me.reshape(1), *operands, w, m, v)


GROUPS = {"in": ("w_in",), "rest": ("w_out", "w_xq", "w_xo", "w_xkv", "w_up", "w_down")}
FULL_SHAPES = {"w_in": (D_MODEL, N_IN_PAD), "w_out": (D_MODEL, D_MODEL), "w_xq": (D_MODEL, D_MODEL),
               "w_xo": (D_MODEL, D_MODEL), "w_xkv": (N_DEV, D_MODEL, 2 * D_MODEL // N_DEV),
               "w_up": (N_DEV, FF_SHARD, D_MODEL), "w_down": (FF_HALF, FF_SHARD, D_MODEL)}
PIECE_SHAPES = {"w_in": (N_DEV, D_MODEL // N_DEV, N_IN_PAD), "w_out": (N_DEV, D_MODEL // N_DEV, D_MODEL),
                "w_xq": (N_DEV, D_MODEL // N_DEV, D_MODEL), "w_xo": (N_DEV, D_MODEL // N_DEV, D_MODEL),
                "w_xkv": (N_DEV, D_MODEL, 2 * D_MODEL // N_DEV), "w_up": (N_DEV, FF_SHARD, D_MODEL),
                "w_down": (N_DEV, D_FF // N_DEV, D_MODEL)}
CONV_WORDS = 8192


class _GatheredWeights:
    def __init__(self, states, layer):
        self.states, self.layer, self.full, self.extra = dict(states), layer, {}, None

    def need(self, group, after):
        if group in self.states:
            got, _ = _exchange_wait(self.states.pop(group), after, "gather_%s_wait_%d" % (group, self.layer))
            for name, g in zip(GROUPS[group], got):
                self.full[name] = g.reshape(FULL_SHAPES[name])
            self.extra = got[len(GROUPS[group]):]

    def __getitem__(self, name):
        return self.full[name]


def _relay_in_cols(w):
    pad = jnp.zeros(w.shape[:-1] + (N_IN_PAD - N_IN,), w.dtype)
    return jnp.concatenate([w[..., :2304], w[..., 2308:N_IN], w[..., 2304:2308], pad], axis=-1)


def _unrelay_in_cols(w):
    return jnp.concatenate([w[..., :2304], w[..., COL_GATE:COL_GATE + 4], w[..., 2304:COL_GATE]], axis=-1)


def _layer_fwd(h, memv, w, sm, tables):
    sv = {"h0": h}
    s = h.shape[0]
    tm, tb = min(ROW_TILE, s), min(MM_TILE, s)
    w.need("in", h)
    tn = N_IN_PAD // 3
    proj, xn = _norm_matmul(h, sm["g_mix"], w["w_in"], (s, N_IN_PAD), grid=(s // tb, 3),
                            b_spec=pl.BlockSpec((D_MODEL, tn), lambda i, j: (0, j)),
                            o_spec=pl.BlockSpec((tb, tn), lambda i, j: (i, j)), name="norm_mm_in")
    sv["xn"], sv["proj"] = xn, proj
    ycat = _sconv_fwd(proj, sm["w_sconv"])
    qd, kd, vd, ktd, vtd = _heads_split(proj, 1, tables["rope"], None, "split_dil")
    ycat, ob, lse_b = _attention_fwd("dil", qd, kd, vtd, tables["dil"], ycat, 1)
    sv["dil"] = (qd, kd, vd, ktd, ob, lse_b)
    c = _gate_cumsum(proj, sm["b_forget_pad"])
    qf, kf, vf, ktf, vtf = _heads_split(proj, 2, None, c, "split_fox")
    ycat, oc, lse_c = _attention_fwd("fox", qf, kf, vtf, tables["fox"], ycat, 2)
    sv["fox"] = (qf, kf, vf, ktf, oc, lse_c)
    ycat = _pool_fwd(proj, sm["w_pool_bd"], sm["pool_scale"], ycat)
    sv["ycat"] = ycat
    w.need("rest", ycat)
    h1 = _mm_nn(ycat, w["w_out"], "mm_out", res=h)
    sv["h1"] = h1
    memn = _rms_fwd(memv, sm["g_mem"], "rms_mem")
    qx, xq = _norm_matmul(h1, sm["g_xa"], w["w_xq"], (s, D_MODEL), grid=(s // tb, 1),
                          b_spec=pl.BlockSpec((D_MODEL, D_MODEL), lambda i, j: (0, 0)),
                          o_spec=pl.BlockSpec((tb, D_MODEL), lambda i, j: (i, 0)), name="norm_mm_xq",
                          out_dtype=BF16)
    kvm = _matmul(memn, w["w_xkv"], (N_DEV, MEM_LEN, XA_DIM), grid=(N_DEV, 1, 1),
                  a_spec=pl.BlockSpec((MEM_LEN, D_MODEL), lambda i, j, r: (0, 0)),
                  b_spec=pl.BlockSpec((None, D_MODEL, XA_DIM), lambda i, j, r: (i, 0, 0)),
                  o_spec=pl.BlockSpec((None, MEM_LEN, XA_DIM), lambda i, j, r: (i, 0, 0)),
                  dims=NN, nred=1, name="mm_xkv")
    ox = _xattn_fwd(qx, kvm)
    sv.update(xq=xq, memn=memn, qx=qx, kvm=kvm, ox=ox)
    h2 = _mm_nn(ox, w["w_xo"], "mm_xo", res=h1)
    sv["h2"] = h2
    u0, xf = _norm_matmul(h2, sm["g_ffn"], w["w_up"], (N_DEV, s, FF_SHARD), grid=(s // tb, N_DEV),
                          b_spec=pl.BlockSpec((None, FF_SHARD, D_MODEL), lambda i, j: (j, 0, 0)),
                          o_spec=pl.BlockSpec((None, tb, FF_SHARD), lambda i, j: (j, i, 0)), name="norm_mm_up",
                          out_dtype=BF16, dims=NT)
    act = _ffn_gate_fwd(u0, sm["w_ffconv"])
    sv.update(xf=xf, u0=u0, act=act)
    ospec = pl.BlockSpec((tm, D_MODEL), lambda i, j, r: (i, 0))
    h3 = _matmul(act, w["w_down"], (s, D_MODEL), grid=(s // tm, 1, 1),
                 a_spec=pl.BlockSpec((FF_HALF, tm, FF_SHARD), lambda i, j, r: (0, i, 0)),
                 b_spec=pl.BlockSpec((FF_HALF, FF_SHARD, D_MODEL), lambda i, j, r: (0, 0, 0)),
                 o_spec=ospec, dims=NN, nred=1, slabs=FF_HALF, name="mm_down", res=h2, res_spec=ospec)
    return h3, sv


def _layer_bwd(dh3, memv, w, sm, tables, sv, rest_ready):
    s = dh3.shape[0]
    tm, tb = min(ROW_TILE, s), min(MM_TILE, s)
    big, small = {}, {}
    ts = max(s // 2, 1)
    dact = _matmul(dh3, w["w_down"], (FF_HALF, s, FF_SHARD), grid=(s // tb, FF_HALF, 1),
                   a_spec=pl.BlockSpec((tb, D_MODEL), lambda i, j, r: (i, 0)),
                   b_spec=pl.BlockSpec((None, FF_SHARD, D_MODEL), lambda i, j, r: (j, 0, 0)),
                   o_spec=pl.BlockSpec((None, tb, FF_SHARD), lambda i, j, r: (j, i, 0)),
                   dims=NT, nred=1, name="mm_dact", out_dtype=BF16)
    big["w_down"] = _matmul(sv["act"], dh3, (FF_HALF, FF_SHARD, D_MODEL), grid=(FF_HALF, 1, s // ts),
                            a_spec=pl.BlockSpec((None, ts, FF_SHARD), lambda i, j, r: (i, r, 0)),
                            b_spec=pl.BlockSpec((ts, D_MODEL), lambda i, j, r: (r, 0)),
                            o_spec=pl.BlockSpec((None, FF_SHARD, D_MODEL), lambda i, j, r: (i, 0, 0)),
                            dims=TN, nred=s // ts, name="mm_dw_down", out_dtype=GRAD_DTYPE)
    du0, small["w_ffconv"] = _ffn_gate_bwd(sv["u0"], sm["w_ffconv"], dact)
    dh2, small["g_ffn"] = _matmul_rms_bwd(du0, w["w_up"], sv["h2"], sm["g_ffn"], dh3, "mm_dxf_rms_bwd",
                                          tm=ROW_TILE // 2, dims=NN)
    big["w_up"] = _matmul(du0, sv["xf"], (N_DEV, FF_SHARD, D_MODEL), grid=(N_DEV, 1, 1),
                          a_spec=pl.BlockSpec((None, s, FF_SHARD), lambda i, j, r: (i, 0, 0)),
                          b_spec=pl.BlockSpec((s, D_MODEL), lambda i, j, r: (0, 0)),
                          o_spec=pl.BlockSpec((None, FF_SHARD, D_MODEL), lambda i, j, r: (i, 0, 0)),
                          dims=TN, nred=1, name="mm_dw_up", out_dtype=GRAD_DTYPE)
    dox = _mm_nt(dh2, w["w_xo"], "mm_dox", out_dtype=BF16)
    big["w_xo"] = _mm_tn(sv["ox"], dh2, "mm_dw_xo")
    dqx, dkvm = _xattn_bwd(sv["qx"], sv["kvm"], dox)
    big["w_xq"] = _mm_tn(sv["xq"], dqx, "mm_dw_xq")
    big["w_xkv"] = _matmul(sv["memn"], dkvm, (N_DEV, D_MODEL, XA_DIM), grid=(N_DEV, 1, 1),
                           a_spec=pl.BlockSpec((MEM_LEN, D_MODEL), lambda i, j, r: (0, 0)),
                           b_spec=pl.BlockSpec((None, MEM_LEN, XA_DIM), lambda i, j, r: (i, 0, 0)),
                           o_spec=pl.BlockSpec((None, D_MODEL, XA_DIM), lambda i, j, r: (i, 0, 0)),
                           dims=TN, nred=1, name="mm_dw_xkv", out_dtype=GRAD_DTYPE)
    dmemn = _matmul(dkvm, w["w_xkv"], (MEM_LEN, D_MODEL), grid=(1, 1, 1),
                    a_spec=pl.BlockSpec((N_DEV, MEM_LEN, XA_DIM), lambda i, j, r: (0, 0, 0)),
                    b_spec=pl.BlockSpec((N_DEV, D_MODEL, XA_DIM), lambda i, j, r: (0, 0, 0)),
                    o_spec=pl.BlockSpec((MEM_LEN, D_MODEL), lambda i, j, r: (0, 0)),
                    dims=NT, nred=1, slabs=N_DEV, name="mm_dmemn")
    _, small["g_mem"] = _rms_bwd(dmemn, memv, sm["g_mem"], None, "rms_mem_bwd")
    dh1, small["g_xa"] = _matmul_rms_bwd(dqx, w["w_xq"], sv["h1"], sm["g_xa"], dh2, "mm_dxq_rms_bwd")
    big["w_out"] = _mm_tn(sv["ycat"], dh1, "mm_dw_out")
    dycat = _mm_nt(dh1, w["w_out"] + rest_ready(big, small).astype(BF16), "mm_dycat")
    proj = sv["proj"]
    dproj, small["w_sconv"] = _sconv_bwd(proj, sm["w_sconv"], dycat)
    qd, kd, vd, ktd, ob, lse_b = sv["dil"]
    delta, dob = _attention_delta(ob, dycat, 1)
    dqt, dk, dv = _attention_bwd("dil", qd, kd, vd, ktd, tables["dil"], dob, lse_b, delta)
    dproj = _heads_merge(dqt, dk, dv, tables["rope"], "merge_dil", dproj, 1)
    qf, kf, vf, ktf, oc, lse_c = sv["fox"]
    delta, dob = _attention_delta(oc, dycat, 2)
    dqt, dk, dv = _attention_bwd("fox", qf, kf, vf, ktf, tables["fox"], dob, lse_c, delta)
    dproj, dc = _heads_merge(dqt, dk, dv, None, "merge_fox", dproj, 2)
    dproj, dbias = _gate_cumsum_bwd(proj, sm["b_forget_pad"], dc, dproj)
    small["b_forget"] = dbias[0, :N_HEADS]
    dproj, dwbd, small["pool_scale"] = _pool_bwd(proj, sm["w_pool_bd"], sm["pool_scale"], dycat, dproj)
    small["w_pool"] = jnp.stack([dwbd[64 * g:64 * (g + 1), 64 * g:64 * (g + 1)] for g in range(4)])
    big["w_in"] = _mm_tn(sv["xn"], dproj, "mm_dw_in", tn=896)
    dh0, small["g_mix"] = _matmul_rms_bwd(dproj, w["w_in"], sv["h0"], sm["g_mix"], dh1, "mm_dxn_rms_bwd")
    return dh0, big, small


SMALL_NAMES = ("g_mix", "b_forget", "w_pool", "pool_scale", "g_xa", "g_mem", "g_ffn", "w_sconv", "w_ffconv")
SMALL_WITH = {"rest": ("w_ffconv", "g_ffn", "g_mem", "g_xa"),
              "in": ("w_sconv", "b_forget", "pool_scale", "w_pool", "g_mix")}
SMALL_SHAPES = {"w_sconv": (3, GROUP), "w_ffconv": (N_DEV, 3, FF_SHARD)}
WEIGHT_NAMES = ("g_mix", "w_in", "b_forget", "w_sconv", "w_pool", "pool_scale", "w_out", "g_xa", "g_mem", "w_xq",
                "w_xkv", "w_xo", "g_ffn", "w_up", "w_ffconv", "w_down", "g_final")


def _block_diag(w_pool):
    z = jnp.zeros((64, 64), F32)
    return jnp.concatenate(
        [jnp.concatenate([w_pool[g] if c == g else z for c in range(4)], axis=1) for g in range(4)], axis=0)


def kernel(x, mem, positions, g_mix, w_in, b_forget, w_sconv, w_pool, pool_scale, w_out, g_xa, g_mem, w_xq, w_xkv, w_xo, g_ffn, w_up, w_ffconv, w_down, g_final, loss_target, m_g_mix, m_w_in, m_b_forget, m_w_sconv, m_w_pool, m_pool_scale, m_w_out, m_g_xa, m_g_mem, m_w_xq, m_w_xkv, m_w_xo, m_g_ffn, m_w_up, m_w_ffconv, m_w_down, m_g_final, v_g_mix, v_w_in, v_b_forget, v_w_sconv, v_w_pool, v_pool_scale, v_w_out, v_g_xa, v_g_mem, v_w_xq, v_w_xkv, v_w_xo, v_g_ffn, v_w_up, v_w_ffconv, v_w_down, v_g_final):
    weights = dict(g_mix=g_mix, w_in=w_in, b_forget=b_forget, w_sconv=w_sconv, w_pool=w_pool, pool_scale=pool_scale,
                   w_out=w_out, g_xa=g_xa, g_mem=g_mem, w_xq=w_xq, w_xkv=w_xkv, w_xo=w_xo, g_ffn=g_ffn, w_up=w_up,
                   w_ffconv=w_ffconv, w_down=w_down, g_final=g_final)
    m_in = dict(g_mix=m_g_mix, w_in=m_w_in, b_forget=m_b_forget, w_sconv=m_w_sconv, w_pool=m_w_pool,
                pool_scale=m_pool_scale, w_out=m_w_out, g_xa=m_g_xa, g_mem=m_g_mem, w_xq=m_w_xq, w_xkv=m_w_xkv,
                w_xo=m_w_xo, g_ffn=m_g_ffn, w_up=m_w_up, w_ffconv=m_w_ffconv, w_down=m_w_down, g_final=m_g_final)
    v_in = dict(g_mix=v_g_mix, w_in=v_w_in, b_forget=v_b_forget, w_sconv=v_w_sconv, w_pool=v_w_pool,
                pool_scale=v_pool_scale, w_out=v_w_out, g_xa=v_g_xa, g_mem=v_g_mem, w_xq=v_w_xq, w_xkv=v_w_xkv,
                w_xo=v_w_xo, g_ffn=v_g_ffn, w_up=v_w_up, w_ffconv=v_w_ffconv, w_down=v_w_down, g_final=v_g_final)
    depth = w_in.shape[0]
    me = 4 * lax.axis_index("x") + 2 * lax.axis_index("y") + lax.axis_index("c")
    h = x[0]
    memv = mem[0]
    s = h.shape[0]
    tq = min(ATT_TQ, s)
    tables = {"rope": _rope_tables(positions[0]), "dil": _bias_tables("dil", tq, tq),
              "fox": _bias_tables("fox", tq, tq)}

    w_in_r = _relay_in_cols(w_in)
    conv_shard = jnp.concatenate([w_sconv.reshape(-1), w_ffconv.reshape(-1)])
    conv_shard = jnp.concatenate([conv_shard, jnp.zeros((CONV_WORDS - conv_shard.shape[0],), F32)])
    conv_bits = lax.bitcast_convert_type(conv_shard, BF16).reshape(2 * CONV_WORDS // 1024, 1024)
    gathered = []
    order = jnp.zeros((), F32)
    for l in range(depth):
        shards = dict(w_in=w_in_r[l], w_out=w_out[l], w_xq=w_xq[l], w_xo=w_xo[l], w_xkv=w_xkv[l],
                      w_up=w_up[l].T, w_down=w_down[l])
        states = {}
        for group in ("in", "rest"):
            shards[GROUPS[group][0]] = shards[GROUPS[group][0]] + order
            xs = [_place_shard(shards[name], me, BF16, "place_%s_%d" % (name, l)) for name in GROUPS[group]]
            if l == 0 and group == "in":
                xs.append(_place_shard(conv_bits, me, BF16, "place_conv"))
            states[group], token = _exchange_start(xs, False, "gather_%s_start_%d" % (group, l))
            order = order + token[0, 0]
        gathered.append(_GatheredWeights(states, l))
    gathered[0].need("in", tables["rope"][0])
    conv_all = lax.bitcast_convert_type(gathered[0].extra[0].reshape(N_DEV, CONV_WORDS, 2), F32)
    n_sc = depth * 3 * (GROUP // N_DEV)
    sconv_full = conv_all[:, :n_sc].reshape(N_DEV, depth, 3, GROUP // N_DEV).transpose(1, 2, 0, 3).reshape(
        depth, 3, GROUP)
    ffconv_full = conv_all[:, n_sc:n_sc + depth * 3 * FF_SHARD].reshape(N_DEV, depth, 3, FF_SHARD).transpose(
        1, 0, 2, 3)

    smalls = []
    for l in range(depth):
        smalls.append(dict(
            g_mix=g_mix[l], g_xa=g_xa[l], g_mem=g_mem[l], g_ffn=g_ffn[l], pool_scale=pool_scale[l],
            w_pool_bd=_block_diag(w_pool[l]), w_sconv=sconv_full[l], w_ffconv=ffconv_full[l],
            b_forget_pad=jnp.concatenate([b_forget[l], jnp.zeros((128 - N_HEADS,), F32)]).reshape(1, 128)))
    smalls[0]["g_mix"] = smalls[0]["g_mix"] + order

    saved = []
    for l in range(depth):
        h, sv = _layer_fwd(h, memv, gathered[l], smalls[l], tables)
        saved.append(sv)
    loss_part, dh, dg_final = _loss_head(h, g_final, loss_target[0])
    loss = lax.psum(loss_part[0, 0], MESH_AXES)

    small_grads = [None] * depth
    scatters = {}

    def pieces_of(big, group):
        return [big[name].reshape(PIECE_SHAPES[name]) for name in GROUPS[group]]

    def rider(grads):
        flat = jnp.concatenate([g.reshape(-1) for g in grads])
        rows = -(-flat.shape[0] // 1024)
        flat = jnp.concatenate([flat, jnp.zeros((rows * 1024 - flat.shape[0],), F32)])
        return jnp.broadcast_to(flat.reshape(1, rows, 1024), (N_DEV, rows, 1024))

    for l in reversed(range(depth)):
        def rest_ready(big, small, l=l):
            ready = [small[n] for n in SMALL_WITH["rest"]] + ([dg_final] if l == depth - 1 else [])
            scatters[l, "rest"], token = _exchange_start(pieces_of(big, "rest") + [rider(ready)], True,
                                                         "scatter_rest_start_%d" % l)
            return token[0, 0]

        dh, big, small_grads[l] = _layer_bwd(dh, memv, gathered[l], smalls[l], tables, saved[l], rest_ready)
        xs = pieces_of(big, "in") + [rider([small_grads[l][n] for n in SMALL_WITH["in"]])]
        scatters[l, "in"], token = _exchange_start(xs, True, "scatter_in_start_%d" % l)
        if l > 0:
            smalls[l - 1]["w_ffconv"] = smalls[l - 1]["w_ffconv"] + token[0, 0]
    grad_x = dh[None]

    parts, owns, small_parts = {}, {}, {}

    def wait_group(group, after):
        for l in reversed(range(depth)):
            got, given = _exchange_wait(scatters[l, group], after, "scatter_%s_wait_%d" % (group, l))
            for name, g, x in zip(GROUPS[group], got, given):
                parts.setdefault(name, [None] * depth)[l] = g
                owns.setdefault(name, [None] * depth)[l] = x
            flat = lax.dynamic_update_slice_in_dim(got[-1], given[-1][:1], me, axis=0).reshape(N_DEV, -1)
            off = 0
            for name in SMALL_WITH[group] + (("g_final",) if group == "rest" and l == depth - 1 else ()):
                shape = SMALL_SHAPES.get(name, weights[name].shape[-1:] if name == "g_final"
                                         else weights[name].shape[1:])
                n = 1
                for dim in shape:
                    n *= dim
                small_parts.setdefault(name, [None] * depth)[l] = flat[:, off:off + n].reshape((N_DEV,) + shape)
                off += n

    results = {}

    def update(name, w3, m3, v3):
        outs = _adamw(parts[name], owns.get(name), me, w3, m3, v3, "adamw_" + name)
        results[name] = [o.reshape(weights[name].shape) for o in outs]

    wait_group("rest", grad_x)
    for name in GROUPS["rest"]:
        if name == "w_up":
            outs = _adamw(parts[name], owns[name], me, w_up.transpose(0, 2, 1), m_w_up.transpose(0, 2, 1),
                          v_w_up.transpose(0, 2, 1), "adamw_w_up")
            results[name] = [o.transpose(0, 2, 1) for o in outs]
        else:
            update(name, weights[name], m_in[name], v_in[name])
    wait_group("in", results["w_down"][1])
    outs = _adamw(parts["w_in"], owns["w_in"], me, w_in_r, _relay_in_cols(m_w_in), _relay_in_cols(v_w_in),
                  "adamw_w_in")
    results["w_in"] = [_unrelay_in_cols(o) for o in outs]
    for name in SMALL_NAMES + ("g_final",):
        wv = weights[name]
        p = small_parts[name][depth - 1] if name == "g_final" else jnp.stack(small_parts[name], axis=1)
        if name == "w_sconv":
            p = lax.dynamic_slice_in_dim(p, me * (GROUP // N_DEV), GROUP // N_DEV, axis=3)
        elif name == "w_ffconv":
            p = lax.dynamic_index_in_dim(p, me, axis=2, keepdims=False)
        shape3 = (1, 1, wv.shape[0]) if wv.ndim == 1 else (1, -1, wv.shape[-1])
        w3 = wv.reshape(shape3)
        parts[name] = [p.reshape((N_DEV,) + w3.shape[1:])]
        update(name, w3, m_in[name].reshape(shape3), v_in[name].reshape(shape3))

    return (loss, grad_x, *[results[n][0] for n in WEIGHT_NAMES], *[results[n][1] for n in WEIGHT_NAMES],
            *[results[n][2] for n in WEIGHT_NAMES], *[results[n][3] for n in WEIGHT_NAMES])
```

```python
import functools

import jax
import jax.numpy as jnp
from jax import lax
from jax.experimental import pallas as pl
from jax.experimental.pallas import tpu as pltpu

F32 = jnp.float32
BF16 = jnp.bfloat16

N_DEV = 8
D_MODEL = 1024
GROUP = 256
HEAD_DIM = 64
N_HEADS = 4
N_IN = 2564
N_IN_PAD = 2688
COL_GATE = 2560
XA_HEADS = 4
XA_DIM = 256
MEM_LEN = 256
D_FF = 2816
FF_SHARD = 704
FF_HALF = 4
ROPE_THETA = 500000.0
ROPE_DIM = 16
RMS_EPS = 1e-6
NEG = -1e30
POOL_WINDOWS = (2, 4, 8, 16)
ADAM_LR, ADAM_B1, ADAM_B2, ADAM_EPS, ADAM_WD, ADAM_STEP = 0.001, 0.9, 0.999, 1e-08, 0.01, 10

ROW_TILE = 512
MM_TILE = 1024
ATT_TQ = 512
BWD_HEADS = 4
VMEM_LIMIT = 56 * 1024 * 1024
ADAMW_BLOCK_BYTES = 8 * 1024 * 1024
PLACE_BLOCK_BYTES = 4 * 1024 * 1024

MESH_AXES = ("x", "y", "c")


def _params(**kw):
    return pltpu.CompilerParams(vmem_limit_bytes=VMEM_LIMIT, **kw)


HBM_SPEC = pl.BlockSpec(memory_space=pltpu.HBM)
SEM_SPEC = pl.BlockSpec(memory_space=pltpu.SEMAPHORE)
DATAFLOW = pltpu.SideEffectType.DATAFLOW_SIDE_EFFECTING


def _peer_copies(x_ref, land_ref, send_sems, recv_sems, scatter):
    mx, my, mc = lax.axis_index("x"), lax.axis_index("y"), lax.axis_index("c")
    me = 4 * mx + 2 * my + mc
    pairs = []
    for k in range(1, N_DEV):
        kx, ky, kc = (k >> 2) & 1, (k >> 1) & 1, k & 1
        peer_lin = me ^ k
        send = pltpu.make_async_remote_copy(
            src_ref=x_ref.at[peer_lin] if scatter else land_ref.at[me], dst_ref=land_ref.at[me],
            send_sem=send_sems.at[k - 1], recv_sem=recv_sems.at[k - 1],
            device_id=(mx ^ kx, my ^ ky, mc ^ kc), device_id_type=pl.DeviceIdType.MESH)
        arrival = pltpu.make_async_remote_copy(
            src_ref=land_ref.at[peer_lin], dst_ref=land_ref.at[peer_lin],
            send_sem=send_sems.at[k - 1], recv_sem=recv_sems.at[k - 1],
            device_id=(mx, my, mc), device_id_type=pl.DeviceIdType.MESH)
        pairs.append((send, arrival))
    return pairs


def _exchange_start(xs, scatter, name):
    n = len(xs)
    ns = n if scatter else 0

    def body(*refs):
        srcs = refs[:ns] if scatter else (None,) * n
        lands, sends, recvs = refs[ns:ns + n], refs[ns + n:ns + 2 * n], refs[ns + 2 * n:ns + 3 * n]
        for t in range(n):
            for send, _ in _peer_copies(srcs[t], lands[t], sends[t], recvs[t], scatter):
                send.start()
        token = refs[-1]
        token[...] = jnp.zeros_like(token)

    sems = pltpu.SemaphoreType.DMA((N_DEV - 1,))
    operands = [pltpu.with_memory_space_constraint(x, pltpu.HBM) for x in xs]
    if scatter:
        operands += [pltpu.with_memory_space_constraint(lax.empty(x.shape, x.dtype), pltpu.HBM) for x in xs]
    outs = pl.pallas_call(
        body, name=name,
        out_shape=(sems,) * (2 * n) + tuple(pltpu.HBM(a.shape, a.dtype) for a in operands)
        + (jax.ShapeDtypeStruct((8, 128), F32),),
        in_specs=(HBM_SPEC,) * (ns + n),
        out_specs=(SEM_SPEC,) * (2 * n) + (HBM_SPEC,) * (ns + n) + (pl.BlockSpec(memory_space=pltpu.VMEM),),
        input_output_aliases={i: 2 * n + i for i in range(ns + n)},
        compiler_params=pltpu.CompilerParams(has_side_effects=DATAFLOW),
    )(*operands)
    return (outs[:-1], scatter), outs[-1]


def _exchange_wait(state, after, name):
    held, scatter = state
    n = len(held) // (4 if scatter else 3)
    ns = n if scatter else 0
    sems, thru = held[:2 * n], held[2 * n:]

    def body(*refs):
        srcs = refs[:ns] if scatter else (None,) * n
        lands, sends, recvs = refs[ns:ns + n], refs[ns + n:ns + 2 * n], refs[ns + 2 * n:ns + 3 * n]
        for t in range(n):
            for send, arrival in _peer_copies(srcs[t], lands[t], sends[t], recvs[t], scatter):
                send.wait_send()
                arrival.wait_recv()

    outs = pl.pallas_call(
        body, name=name,
        out_shape=tuple(pltpu.HBM(a.shape, a.dtype) for a in thru),
        in_specs=(HBM_SPEC,) * (ns + n) + (SEM_SPEC,) * (2 * n) + (pl.BlockSpec(memory_space=pl.ANY),),
        out_specs=(HBM_SPEC,) * (ns + n), input_output_aliases={i: i for i in range(ns + n)},
        compiler_params=pltpu.CompilerParams(has_side_effects=DATAFLOW),
    )(*thru, *sems, after)
    return list(outs[ns:]), list(outs[:ns])


def _place_shard(x, layer, me, dtype, name):
    _, r, c = x.shape
    tr = r
    if r * c * 4 > PLACE_BLOCK_BYTES:
        for cand in (512, 256, 128, 64, 32, 16):
            if r % cand == 0 and cand * c * 4 <= PLACE_BLOCK_BYTES:
                tr = cand
                break

    def body(me_ref, x_ref, o_ref):
        o_ref[...] = x_ref[...].astype(o_ref.dtype)

    return pl.pallas_call(
        body, name=name, out_shape=jax.ShapeDtypeStruct((N_DEV, r, c), dtype),
        grid_spec=pltpu.PrefetchScalarGridSpec(
            num_scalar_prefetch=1, grid=(r // tr,),
            in_specs=[pl.BlockSpec((None, tr, c), lambda i, me_ref: (layer, i, 0))],
            out_specs=pl.BlockSpec((None, tr, c), lambda i, me_ref: (me_ref[0], i, 0))),
        compiler_params=_params(),
    )(me.reshape(1), x)


NN = ((1,), (0,))
NT = ((1,), (1,))
TN = ((0,), (0,))


def _matmul(a, b, out_shape, *, grid, a_spec, b_spec, o_spec, dims, nred, name, res=None, res_spec=None,
            out_dtype=F32, slabs=0):
    has_res = res is not None

    def body(*refs):
        a_ref, b_ref = refs[0], refs[1]
        r_ref = refs[2] if has_res else None
        o_ref = refs[3] if has_res else refs[2]
        if slabs:
            part = None
            for n in range(slabs):
                term = lax.dot_general(a_ref[n].astype(BF16), b_ref[n].astype(BF16), (dims, ((), ())),
                                       preferred_element_type=F32)
                part = term if part is None else part + term
        else:
            part = lax.dot_general(a_ref[...].astype(BF16), b_ref[...].astype(BF16), (dims, ((), ())),
                                   preferred_element_type=F32)
        if nred == 1:
            if has_res:
                part = part + r_ref[...]
            o_ref[...] = part.astype(o_ref.dtype)
        else:
            acc = refs[-1]
            r = pl.program_id(2)

            @pl.when(r == 0)
            def _():
                acc[...] = part

            @pl.when(r > 0)
            def _():
                acc[...] += part

            @pl.when(r == nred - 1)
            def _():
                tot = acc[...]
                if has_res:
                    tot = tot + r_ref[...]
                o_ref[...] = tot.astype(o_ref.dtype)

    in_specs = [a_spec, b_spec] + ([res_spec] if has_res else [])
    args = (a, b) + ((res,) if has_res else ())
    acc_shape = tuple(d for d in o_spec.block_shape if d is not None)
    return pl.pallas_call(
        body, name=name, grid=grid, out_shape=jax.ShapeDtypeStruct(out_shape, out_dtype),
        in_specs=in_specs, out_specs=o_spec,
        scratch_shapes=[pltpu.VMEM(acc_shape, F32)] if nred > 1 else [],
        compiler_params=_params(),
    )(*args)


def _mm_nn(a, w, name, res=None, tn=None, out_dtype=F32):
    m, k = a.shape
    n = w.shape[1]
    tn = tn or n
    tm = min(MM_TILE, m)
    ospec = pl.BlockSpec((tm, tn), lambda i, j, r: (i, j))
    return _matmul(a, w, (m, n), grid=(m // tm, n // tn, 1),
                   a_spec=pl.BlockSpec((tm, k), lambda i, j, r: (i, 0)),
                   b_spec=pl.BlockSpec((k, tn), lambda i, j, r: (0, j)),
                   o_spec=ospec, dims=NN, nred=1, name=name, res=res, res_spec=ospec if res is not None else None,
                   out_dtype=out_dtype)


def _mm_nt(a, w, name, out_dtype=F32):
    m, n = a.shape
    k = w.shape[0]
    tm = min(MM_TILE, m)
    return _matmul(a, w, (m, k), grid=(m // tm, 1, 1),
                   a_spec=pl.BlockSpec((tm, n), lambda i, j, r: (i, 0)),
                   b_spec=pl.BlockSpec((k, n), lambda i, j, r: (0, 0)),
                   o_spec=pl.BlockSpec((tm, k), lambda i, j, r: (i, 0)), dims=NT, nred=1, name=name,
                   out_dtype=out_dtype)


def _norm_matmul(h, g, b, out_shape, *, grid, b_spec, o_spec, name, out_dtype=F32, dims=NN):
    s, d = h.shape
    tm = s // grid[0]

    def body(h_ref, g_ref, b_ref, o_ref, xn_ref):
        @pl.when(pl.program_id(1) == 0)
        def _():
            hv = h_ref[...]
            r = lax.rsqrt(jnp.mean(hv * hv, axis=-1, keepdims=True) + RMS_EPS)
            xn_ref[...] = (hv * r * g_ref[...]).astype(xn_ref.dtype)

        o_ref[...] = lax.dot_general(xn_ref[...], b_ref[...].astype(BF16), (dims, ((), ())),
                                     preferred_element_type=F32).astype(o_ref.dtype)

    row = pl.BlockSpec((tm, d), lambda i, j: (i, 0))
    return pl.pallas_call(
        body, name=name, grid=grid,
        out_shape=(jax.ShapeDtypeStruct(out_shape, out_dtype), jax.ShapeDtypeStruct((s, d), BF16)),
        in_specs=[row, pl.BlockSpec((1, d), lambda i, j: (0, 0)), b_spec],
        out_specs=(o_spec, row), compiler_params=_params(),
    )(h, g.reshape(1, d), b)


def _matmul_rms_bwd(a, w, h, g, res, name, tm=ROW_TILE, dims=NT):
    slabs = a.shape[0] if a.ndim == 3 else 0
    s, n = a.shape[-2:]
    d = w.shape[-2] if dims == NT else w.shape[-1]
    tm = min(tm, s)

    def body(a_ref, w_ref, h_ref, g_ref, r_ref, dh_ref, dg_ref):
        if slabs:
            dy = None
            for j in range(slabs):
                term = lax.dot_general(a_ref[j].astype(BF16), w_ref[j].astype(BF16), (dims, ((), ())),
                                       preferred_element_type=F32)
                dy = term if dy is None else dy + term
        else:
            dy = lax.dot_general(a_ref[...].astype(BF16), w_ref[...].astype(BF16), (dims, ((), ())),
                                 preferred_element_type=F32)
        hv = h_ref[...]
        r = lax.rsqrt(jnp.mean(hv * hv, axis=-1, keepdims=True) + RMS_EPS)
        hn = hv * r
        u = dy * g_ref[...]
        dh_ref[...] = r * (u - hn * jnp.mean(u * hn, axis=-1, keepdims=True)) + r_ref[...]
        part = jnp.sum(dy * hn, axis=0, keepdims=True)

        @pl.when(pl.program_id(0) == 0)
        def _():
            dg_ref[...] = part

        @pl.when(pl.program_id(0) > 0)
        def _():
            dg_ref[...] += part

    row = pl.BlockSpec((tm, d), lambda i: (i, 0))
    vec = pl.BlockSpec((1, d), lambda i: (0, 0))
    if slabs:
        a_spec = pl.BlockSpec((slabs, tm, n), lambda i: (0, i, 0))
        w_spec = pl.BlockSpec(w.shape, lambda i: (0, 0, 0))
    else:
        a_spec = pl.BlockSpec((tm, n), lambda i: (i, 0))
        w_spec = pl.BlockSpec(w.shape, lambda i: (0, 0))
    dh, dg = pl.pallas_call(
        body, name=name, grid=(s // tm,),
        out_shape=(jax.ShapeDtypeStruct((s, d), F32), jax.ShapeDtypeStruct((1, d), F32)),
        in_specs=[a_spec, w_spec, row, vec, row], out_specs=(row, vec), compiler_params=_params(),
    )(a, w, h, g.reshape(1, d), res)
    return dh, dg.reshape(d)


GRAD_DTYPE = BF16


def _mm_tn(a, b, name, tk=512, tn=None):
    s, k = a.shape
    n = b.shape[1]
    tn = tn or n
    tk = min(tk, k)
    ts = s if b.dtype == BF16 else max(s // 2, 1)
    return _matmul(a, b, (k, n), grid=(k // tk, n // tn, s // ts),
                   a_spec=pl.BlockSpec((ts, tk), lambda i, j, r: (r, i)),
                   b_spec=pl.BlockSpec((ts, tn), lambda i, j, r: (r, j)),
                   o_spec=pl.BlockSpec((tk, tn), lambda i, j, r: (i, j)), dims=TN, nred=s // ts, name=name,
                   out_dtype=GRAD_DTYPE)


def _rms_fwd(h, g, name):
    s, d = h.shape
    tm = min(ROW_TILE, s)

    def body(h_ref, g_ref, o_ref):
        hv = h_ref[...]
        r = lax.rsqrt(jnp.mean(hv * hv, axis=-1, keepdims=True) + RMS_EPS)
        o_ref[...] = (hv * r * g_ref[...]).astype(o_ref.dtype)

    return pl.pallas_call(
        body, name=name, grid=(s // tm,), out_shape=jax.ShapeDtypeStruct((s, d), BF16),
        in_specs=[pl.BlockSpec((tm, d), lambda i: (i, 0)), pl.BlockSpec((1, d), lambda i: (0, 0))],
        out_specs=pl.BlockSpec((tm, d), lambda i: (i, 0)), compiler_params=_params(),
    )(h, g.reshape(1, d))


def _rms_bwd(dy, h, g, res, name):
    s, d = h.shape
    tm = min(ROW_TILE, s)
    has_res = res is not None

    def body(*refs):
        dy_ref, h_ref, g_ref = refs[:3]
        r_ref = refs[3] if has_res else None
        dh_ref, dg_ref = refs[-2], refs[-1]
        hv = h_ref[...]
        r = lax.rsqrt(jnp.mean(hv * hv, axis=-1, keepdims=True) + RMS_EPS)
        hn = hv * r
        dyv = dy_ref[...].astype(F32)
        u = dyv * g_ref[...]
        dh = r * (u - hn * jnp.mean(u * hn, axis=-1, keepdims=True))
        if has_res:
            dh = dh + r_ref[...]
        dh_ref[...] = dh
        part = jnp.sum(dyv * hn, axis=0, keepdims=True)

        @pl.when(pl.program_id(0) == 0)
        def _():
            dg_ref[...] = part

        @pl.when(pl.program_id(0) > 0)
        def _():
            dg_ref[...] += part

    row = pl.BlockSpec((tm, d), lambda i: (i, 0))
    vec = pl.BlockSpec((1, d), lambda i: (0, 0))
    dh, dg = pl.pallas_call(
        body, name=name, grid=(s // tm,),
        out_shape=(jax.ShapeDtypeStruct((s, d), F32), jax.ShapeDtypeStruct((1, d), F32)),
        in_specs=[row, row, vec] + ([row] if has_res else []),
        out_specs=(row, vec), compiler_params=_params(),
    )(*((dy, h, g.reshape(1, d)) + ((res,) if has_res else ())))
    return dh, dg.reshape(d)


def _loss_head(h, g, target):
    s, d = h.shape
    tm = min(ROW_TILE, s)

    def body(h_ref, g_ref, t_ref, loss_ref, dh_ref, dg_ref):
        hv = h_ref[...]
        r = lax.rsqrt(jnp.mean(hv * hv, axis=-1, keepdims=True) + RMS_EPS)
        hn = hv * r
        gv = g_ref[...]
        err = hn * gv - t_ref[...]
        rows = jnp.mean(err * err, axis=-1, keepdims=True)
        lpart = 0.5 * jnp.sum(rows, axis=0, keepdims=True) + jnp.zeros((1, 128), F32)
        dy = err * (1.0 / d)
        u = dy * gv
        dh_ref[...] = r * (u - hn * jnp.mean(u * hn, axis=-1, keepdims=True))
        gpart = jnp.sum(dy * hn, axis=0, keepdims=True)

        @pl.when(pl.program_id(0) == 0)
        def _():
            dg_ref[...] = gpart
            loss_ref[...] = lpart

        @pl.when(pl.program_id(0) > 0)
        def _():
            dg_ref[...] += gpart
            loss_ref[...] += lpart

    row = pl.BlockSpec((tm, d), lambda i: (i, 0))
    vec = pl.BlockSpec((1, d), lambda i: (0, 0))
    return pl.pallas_call(
        body, name="loss_head", grid=(s // tm,),
        out_shape=(jax.ShapeDtypeStruct((1, 128), F32), jax.ShapeDtypeStruct((s, d), F32),
                   jax.ShapeDtypeStruct((1, d), F32)),
        in_specs=[row, vec, row],
        out_specs=(pl.BlockSpec((1, 128), lambda i: (0, 0)), row, vec), compiler_params=_params(),
    )(h, g.reshape(1, d), target)


def _shift_down(x, k):
    return pltpu.roll(x, k, 0)


def _shift_up(x, k):
    return pltpu.roll(x, x.shape[0] - k, 0)


def _conv3(x, w):
    return w[2:3, :] * x + w[1:2, :] * _shift_down(x, 1) + w[0:1, :] * _shift_down(x, 2)


def _conv3_t(x, w):
    return w[2:3, :] * x + w[1:2, :] * _shift_up(x, 1) + w[0:1, :] * _shift_up(x, 2)


def _sigmoid(x):
    return 1.0 / (1.0 + jnp.exp(-x))


def _prev_map(tile, halo, col):
    return lambda i: (jnp.maximum(i * (tile // halo) - 1, 0), col)


def _next_map(tile, halo, col, nrows):
    return lambda i: (jnp.minimum((i + 1) * (tile // halo), nrows // halo - 1), col)


def _sconv_fwd(proj, w):
    s = proj.shape[0]
    t = min(ROW_TILE, s)

    def body(cur_ref, prev_ref, w_ref, o_ref):
        i = pl.program_id(0)
        prev = prev_ref[...] * (i > 0).astype(F32)
        ext = jnp.concatenate([prev, cur_ref[...]], axis=0)
        sv = ext[:, 2 * GROUP:3 * GROUP] * ext[:, 0:GROUP]
        y = ext[:, GROUP:2 * GROUP] * _conv3(sv, w_ref[...])
        o_ref[...] = y[8:].astype(o_ref.dtype)

    return pl.pallas_call(
        body, name="sconv_fwd", grid=(s // t,), out_shape=jax.ShapeDtypeStruct((s, 4 * GROUP), BF16),
        in_specs=[pl.BlockSpec((t, 3 * GROUP), lambda i: (i, 0)),
                  pl.BlockSpec((8, 3 * GROUP), _prev_map(t, 8, 0)),
                  pl.BlockSpec((3, GROUP), lambda i: (0, 0))],
        out_specs=pl.BlockSpec((t, GROUP), lambda i: (i, 0)), compiler_params=_params(),
    )(proj, proj, w)


def _sconv_bwd(proj, w, dy):
    s = proj.shape[0]
    t = min(ROW_TILE, s)
    nt = s // t

    def body(cur_ref, prev_ref, next_ref, w_ref, dy_ref, dyn_ref, dp_ref, dw_ref):
        i = pl.program_id(0)
        first = (i > 0).astype(F32)
        last = (i < nt - 1).astype(F32)
        ext = jnp.concatenate([prev_ref[...] * first, cur_ref[...], next_ref[...] * last], axis=0)
        dye = jnp.concatenate([jnp.zeros((8, GROUP), F32), dy_ref[...], dyn_ref[...] * last], axis=0)
        hv, bv, cv = ext[:, 0:GROUP], ext[:, GROUP:2 * GROUP], ext[:, 2 * GROUP:3 * GROUP]
        wv = w_ref[...]
        sv = cv * hv
        conv = _conv3(sv, wv)
        dconv = dye * bv
        ds = _conv3_t(dconv, wv)
        dp = jnp.concatenate([ds * cv, dye * conv, ds * hv], axis=1)
        dp_ref[...] = dp[8:8 + t].astype(dp_ref.dtype)
        dc = dconv[8:8 + t]
        dw = jnp.concatenate([
            jnp.sum(dc * _shift_down(sv, 2)[8:8 + t], axis=0, keepdims=True),
            jnp.sum(dc * _shift_down(sv, 1)[8:8 + t], axis=0, keepdims=True),
            jnp.sum(dc * sv[8:8 + t], axis=0, keepdims=True),
            jnp.zeros((5, GROUP), F32)], axis=0)

        @pl.when(i == 0)
        def _():
            dw_ref[...] = dw

        @pl.when(i > 0)
        def _():
            dw_ref[...] += dw

    dp, dw = pl.pallas_call(
        body, name="sconv_bwd", grid=(nt,),
        out_shape=(jax.ShapeDtypeStruct((s, N_IN_PAD), BF16), jax.ShapeDtypeStruct((8, GROUP), F32)),
        in_specs=[pl.BlockSpec((t, 3 * GROUP), lambda i: (i, 0)),
                  pl.BlockSpec((8, 3 * GROUP), _prev_map(t, 8, 0)),
                  pl.BlockSpec((8, 3 * GROUP), _next_map(t, 8, 0, s)),
                  pl.BlockSpec((3, GROUP), lambda i: (0, 0)),
                  pl.BlockSpec((t, GROUP), lambda i: (i, 0)),
                  pl.BlockSpec((8, GROUP), _next_map(t, 8, 0, s))],
        out_specs=(pl.BlockSpec((t, 3 * GROUP), lambda i: (i, 0)), pl.BlockSpec((8, GROUP), lambda i: (0, 0))),
        compiler_params=_params(),
    )(proj, proj, proj, w, dy, dy)
    return dp, dw[:3]


def _lane_window(shape):
    lane = lax.broadcasted_iota(jnp.int32, shape, 1)
    return lane, jnp.where(lane < 64, 2.0, jnp.where(lane < 128, 4.0, jnp.where(lane < 192, 8.0, 16.0)))


def _by_group(lane, s1, s2, s3, s4):
    return jnp.where(lane < 64, s1, jnp.where(lane < 128, s2, jnp.where(lane < 192, s3, s4)))


def _pool_z(ext, row0):
    s1 = ext + _shift_down(ext, 1)
    s2 = s1 + _shift_down(s1, 2)
    s3 = s2 + _shift_down(s2, 4)
    s4 = s3 + _shift_down(s3, 8)
    lane, win = _lane_window(ext.shape)
    tpos = (lax.broadcasted_iota(jnp.int32, ext.shape, 0) + (row0 - 16 + 1)).astype(F32)
    cnt = jnp.maximum(jnp.minimum(tpos, win), 1.0)
    return _by_group(lane, s1, s2, s3, s4) / cnt - ext


ANY_SPEC = pl.BlockSpec(memory_space=pl.ANY)


def _pool_fwd(proj, wbd, scale, ybuf):
    s = proj.shape[0]
    t = min(ROW_TILE, s)
    col = (COL_GATE - GROUP) // GROUP

    def body(cur_ref, prev_ref, w_ref, sc_ref, buf_ref, o_ref):
        i = pl.program_id(0)
        ext = jnp.concatenate([prev_ref[...] * (i > 0).astype(F32), cur_ref[...]], axis=0)
        z = _pool_z(ext, i * t)[16:]
        y = jnp.dot(z.astype(BF16), w_ref[...].astype(BF16), preferred_element_type=F32)
        o_ref[...] = (y * sc_ref[...]).astype(o_ref.dtype)

    return pl.pallas_call(
        body, name="pool_fwd", grid=(s // t,), out_shape=jax.ShapeDtypeStruct(ybuf.shape, ybuf.dtype),
        in_specs=[pl.BlockSpec((t, GROUP), lambda i: (i, col)),
                  pl.BlockSpec((16, GROUP), _prev_map(t, 16, col)),
                  pl.BlockSpec((GROUP, GROUP), lambda i: (0, 0)),
                  pl.BlockSpec((1, GROUP), lambda i: (0, 0)), ANY_SPEC],
        out_specs=pl.BlockSpec((t, GROUP), lambda i: (i, 3)), input_output_aliases={4: 0},
        compiler_params=_params(),
    )(proj, proj, wbd, scale.reshape(1, GROUP), ybuf)


def _pool_bwd(proj, wbd, scale, dy, dbuf):
    s = proj.shape[0]
    t = min(ROW_TILE, s)
    nt = s // t
    col = (COL_GATE - GROUP) // GROUP

    def body(cur_ref, prev_ref, w_ref, sc_ref, dy_ref, dyn_ref, buf_ref, dp_ref, dw_ref, dsc_ref):
        i = pl.program_id(0)
        ext = jnp.concatenate([prev_ref[...] * (i > 0).astype(F32), cur_ref[...]], axis=0)
        z = _pool_z(ext, i * t)[16:]
        wv = w_ref[...].astype(BF16)
        dyc = dy_ref[...]
        dye = jnp.concatenate([dyc, dyn_ref[...] * (i < nt - 1).astype(F32)], axis=0) * sc_ref[...]
        dz = lax.dot_general(dye.astype(BF16), wv, (NT, ((), ())), preferred_element_type=F32)
        lane, win = _lane_window(dz.shape)
        tpos = (lax.broadcasted_iota(jnp.int32, dz.shape, 0) + (i * t + 1)).astype(F32)
        e = dz / jnp.minimum(tpos, win)
        f1 = e + _shift_up(e, 1)
        f2 = f1 + _shift_up(f1, 2)
        f3 = f2 + _shift_up(f2, 4)
        f4 = f3 + _shift_up(f3, 8)
        dp = _by_group(lane, f1, f2, f3, f4) - dz
        dp_ref[...] = dp[:t].astype(dp_ref.dtype)
        zb = z.astype(BF16)
        y = jnp.dot(zb, wv, preferred_element_type=F32)
        dsc = jnp.sum(dyc * y, axis=0, keepdims=True)
        dw = lax.dot_general(zb, dye[:t].astype(BF16), (TN, ((), ())), preferred_element_type=F32)

        @pl.when(i == 0)
        def _():
            dw_ref[...] = dw
            dsc_ref[...] = dsc

        @pl.when(i > 0)
        def _():
            dw_ref[...] += dw
            dsc_ref[...] += dsc

    dp, dw, dsc = pl.pallas_call(
        body, name="pool_bwd", grid=(nt,),
        out_shape=(jax.ShapeDtypeStruct(dbuf.shape, dbuf.dtype), jax.ShapeDtypeStruct((GROUP, GROUP), F32),
                   jax.ShapeDtypeStruct((1, GROUP), F32)),
        in_specs=[pl.BlockSpec((t, GROUP), lambda i: (i, col)),
                  pl.BlockSpec((16, GROUP), _prev_map(t, 16, col)),
                  pl.BlockSpec((GROUP, GROUP), lambda i: (0, 0)),
                  pl.BlockSpec((1, GROUP), lambda i: (0, 0)),
                  pl.BlockSpec((t, GROUP), lambda i: (i, 3)),
                  pl.BlockSpec((16, GROUP), _next_map(t, 16, 3, s)), ANY_SPEC],
        out_specs=(pl.BlockSpec((t, GROUP), lambda i: (i, col)), pl.BlockSpec((GROUP, GROUP), lambda i: (0, 0)),
                   pl.BlockSpec((1, GROUP), lambda i: (0, 0))),
        input_output_aliases={6: 0}, compiler_params=_params(),
    )(proj, proj, wbd, scale.reshape(1, GROUP), dy, dy, dbuf)
    return dp, dw, dsc.reshape(GROUP)


FF_HALO = 16


def _ffn_gate_fwd(u0, w):
    s = u0.shape[1]
    t = min(ROW_TILE, s)

    def body(a_ref, ap_ref, g_ref, gp_ref, wa_ref, wg_ref, o_ref):
        first = (pl.program_id(1) > 0).astype(F32)
        a = _conv3(jnp.concatenate([ap_ref[...] * first, a_ref[...].astype(F32)], axis=0), wa_ref[...])[FF_HALO:]
        g = _conv3(jnp.concatenate([gp_ref[...] * first, g_ref[...].astype(F32)], axis=0), wg_ref[...])[FF_HALO:]
        o_ref[...] = (a * (g * _sigmoid(g))).astype(o_ref.dtype)

    def cur(off):
        return pl.BlockSpec((None, t, FF_SHARD), lambda j, i: (j + off, i, 0))

    def prev(off):
        return pl.BlockSpec((None, FF_HALO, FF_SHARD),
                            lambda j, i: (j + off, jnp.maximum(i * (t // FF_HALO) - 1, 0), 0))

    def wspec(off):
        return pl.BlockSpec((None, 3, FF_SHARD), lambda j, i: (j + off, 0, 0))

    return pl.pallas_call(
        body, name="ffn_gate_fwd", grid=(FF_HALF, s // t),
        out_shape=jax.ShapeDtypeStruct((FF_HALF, s, FF_SHARD), BF16),
        in_specs=[cur(0), prev(0), cur(FF_HALF), prev(FF_HALF), wspec(0), wspec(FF_HALF)],
        out_specs=pl.BlockSpec((None, t, FF_SHARD), lambda j, i: (j, i, 0)), compiler_params=_params(),
    )(u0, u0, u0, u0, w, w)


def _ffn_gate_bwd(u0, w, dact):
    s = u0.shape[1]
    t = min(ROW_TILE, s)
    nt = s // t

    def body(c_ref, p_ref, n_ref, w_ref, d_ref, dn_ref, du_ref, dw_ref):
        i = pl.program_id(1)
        first = (i > 0).astype(F32)
        last = (i < nt - 1).astype(F32)
        dext = jnp.concatenate([jnp.zeros((FF_HALO, FF_SHARD), F32), d_ref[...].astype(F32), dn_ref[...] * last],
                               axis=0)
        ext = [jnp.concatenate([p_ref[n] * first, c_ref[n].astype(F32), n_ref[n] * last], axis=0) for n in range(2)]
        a = _conv3(ext[0], w_ref[0])
        g = _conv3(ext[1], w_ref[1])
        sg = _sigmoid(g)
        silu = g * sg
        dus = (dext * silu, dext * a * (sg + silu * (1.0 - sg)))
        mine = slice(FF_HALO, FF_HALO + t)
        for n in range(2):
            du_ref[n] = _conv3_t(dus[n], w_ref[n])[mine].astype(du_ref.dtype)
            dc = dus[n][mine]
            dw = jnp.concatenate([
                jnp.sum(dc * _shift_down(ext[n], 2)[mine], axis=0, keepdims=True),
                jnp.sum(dc * _shift_down(ext[n], 1)[mine], axis=0, keepdims=True),
                jnp.sum(dc * ext[n][mine], axis=0, keepdims=True),
                jnp.zeros((5, FF_SHARD), F32)], axis=0)

            @pl.when(i == 0)
            def _(n=n, dw=dw):
                dw_ref[n] = dw

            @pl.when(i > 0)
            def _(n=n, dw=dw):
                dw_ref[n] += dw

    def pair(rows, row_map):
        return pl.BlockSpec((2, None, rows, FF_SHARD), lambda j, i: (0, j, row_map(i), 0))

    prev_row = lambda i: jnp.maximum(i * (t // FF_HALO) - 1, 0)
    next_row = lambda i: jnp.minimum((i + 1) * (t // FF_HALO), s // FF_HALO - 1)
    u2 = u0.reshape(2, FF_HALF, s, FF_SHARD)
    du, dw = pl.pallas_call(
        body, name="ffn_gate_bwd", grid=(FF_HALF, nt),
        out_shape=(jax.ShapeDtypeStruct((2, FF_HALF, s, FF_SHARD), BF16),
                   jax.ShapeDtypeStruct((2, FF_HALF, 8, FF_SHARD), F32)),
        in_specs=[pair(t, lambda i: i), pair(FF_HALO, prev_row), pair(FF_HALO, next_row), pair(3, lambda i: 0),
                  pl.BlockSpec((None, t, FF_SHARD), lambda j, i: (j, i, 0)),
                  pl.BlockSpec((None, FF_HALO, FF_SHARD), lambda j, i: (j, next_row(i), 0))],
        out_specs=(pair(t, lambda i: i), pair(8, lambda i: 0)),
        compiler_params=_params(),
    )(u2, u2, u2, w.reshape(2, FF_HALF, 3, FF_SHARD), dact, dact)
    return du.reshape(2 * FF_HALF, s, FF_SHARD), dw.reshape(2 * FF_HALF, 8, FF_SHARD)[:, :3]


def _rope_tables(positions):
    inv_freq = ROPE_THETA ** (-jnp.arange(0, ROPE_DIM, 2, dtype=F32) / ROPE_DIM)
    ang = positions.astype(F32)[:, None] * inv_freq
    cos, sin = jnp.cos(ang), jnp.sin(ang)
    s = positions.shape[0]
    half = ROPE_DIM // 2
    rest = HEAD_DIM - ROPE_DIM
    ca = jnp.concatenate([cos, cos, jnp.ones((s, rest), F32)], axis=1)
    cb = jnp.concatenate([-sin, jnp.zeros((s, HEAD_DIM - half), F32)], axis=1)
    cc = jnp.concatenate([jnp.zeros((s, half), F32), sin, jnp.zeros((s, rest), F32)], axis=1)
    return tuple(jnp.tile(tb, (1, N_HEADS)) for tb in (ca, cb, cc))


QK_WIDE = 128
LANE_CQ, LANE_CK = 64, 67
KT_ROWS = 80


def _three_bf16(x):
    hi = x.astype(BF16).astype(F32)
    mid = (x - hi).astype(BF16).astype(F32)
    lo = (x - hi - mid).astype(BF16).astype(F32)
    return hi, mid, lo


def _heads_split(proj, col, tables, c, name):
    s = proj.shape[0]
    t = min(ROW_TILE, s)
    rope = tables is not None
    wide = c is not None
    width = QK_WIDE if wide else HEAD_DIM

    def body(*refs):
        x_ref = refs[0]
        q_ref, k_ref, v_ref, kt_ref, vt_ref = refs[-5:]
        xv = x_ref[...]
        parts = [xv[:, 0:GROUP], xv[:, GROUP:2 * GROUP], xv[:, 2 * GROUP:3 * GROUP]]
        if rope:
            ca, cb, cc = refs[1][...], refs[2][...], refs[3][...]
            for n in range(2):
                p = parts[n]
                parts[n] = p * ca + pltpu.roll(p, GROUP - 8, 1) * cb + pltpu.roll(p, 8, 1) * cc
        parts[0] = parts[0] * (HEAD_DIM ** -0.5)
        k_t, v_t = parts[1].T, parts[2].T
        ones_row = jnp.where(lax.broadcasted_iota(jnp.int32, (KT_ROWS - HEAD_DIM, t), 0) == 0, 1.0, 0.0)
        lane = lax.broadcasted_iota(jnp.int32, (t, QK_WIDE), 1)
        zeros = jnp.zeros((t, QK_WIDE - HEAD_DIM), F32)
        for h in range(N_HEADS):
            hs = slice(h * HEAD_DIM, (h + 1) * HEAD_DIM)
            qh, kh = parts[0][:, hs], parts[1][:, hs]
            if wide:
                terms = _three_bf16(refs[-6][:, h:h + 1])
                qh = jnp.concatenate([qh, zeros], axis=1)
                kh = jnp.concatenate([kh, zeros], axis=1)
                for n in range(3):
                    qh = jnp.where(lane == LANE_CQ + n, terms[n], jnp.where(lane == LANE_CK + n, 1.0, qh))
                    kh = jnp.where(lane == LANE_CK + n, -terms[n], jnp.where(lane == LANE_CQ + n, 1.0, kh))
            q_ref[h] = qh.astype(q_ref.dtype)
            k_ref[h] = kh.astype(k_ref.dtype)
            v_ref[h] = parts[2][:, hs].astype(v_ref.dtype)
            kt_ref[h] = jnp.concatenate([k_t[hs, :], ones_row], axis=0).astype(kt_ref.dtype)
            vt_ref[h] = v_t[hs, :].astype(vt_ref.dtype)

    tab = pl.BlockSpec((t, GROUP), lambda i: (i, 0))
    qk = pl.BlockSpec((N_HEADS, t, width), lambda i: (0, i, 0))
    heads = pl.BlockSpec((N_HEADS, t, HEAD_DIM), lambda i: (0, i, 0))
    heads_t = pl.BlockSpec((N_HEADS, HEAD_DIM, t), lambda i: (0, 0, i))
    qk_shape = jax.ShapeDtypeStruct((N_HEADS, s, width), BF16)
    return pl.pallas_call(
        body, name=name, grid=(s // t,),
        out_shape=(qk_shape, qk_shape, jax.ShapeDtypeStruct((N_HEADS, s, HEAD_DIM), BF16),
                   jax.ShapeDtypeStruct((N_HEADS, KT_ROWS, s), BF16),
                   jax.ShapeDtypeStruct((N_HEADS, HEAD_DIM, s), BF16)),
        in_specs=[pl.BlockSpec((t, 3 * GROUP), lambda i: (i, col))] + ([tab, tab, tab] if rope else [])
        + ([pl.BlockSpec((t, 128), lambda i: (i, 0))] if wide else []),
        out_specs=(qk, qk, heads, pl.BlockSpec((N_HEADS, KT_ROWS, t), lambda i: (0, 0, i)), heads_t),
        compiler_params=_params(),
    )(*((proj,) + (tuple(tables) if rope else ()) + ((c,) if wide else ())))


def _heads_merge(dqt, dk, dv, tables, name, dbuf, col):
    s = dv.shape[1]
    t = min(ROW_TILE, s)
    rope = tables is not None

    wide = dk.shape[2] == QK_WIDE

    def body(*refs):
        o_ref = refs[n_in + 1]
        dq = jnp.concatenate([refs[0][h, :HEAD_DIM, :] for h in range(N_HEADS)], axis=0).T
        parts = [dq] + [jnp.concatenate([r[h][:, :HEAD_DIM] for h in range(N_HEADS)], axis=1) for r in refs[1:3]]
        parts[0] = parts[0] * (HEAD_DIM ** -0.5)
        if rope:
            ca, cb, cc = refs[3][...], refs[4][...], refs[5][...]
            for n in range(2):
                p = parts[n]
                parts[n] = p * ca + pltpu.roll(p * cb, 8, 1) + pltpu.roll(p * cc, GROUP - 8, 1)
        o_ref[...] = jnp.concatenate(parts, axis=1).astype(o_ref.dtype)
        if wide:
            over_keys = jnp.concatenate([refs[0][h, HEAD_DIM:HEAD_DIM + 8, :] for h in range(N_HEADS)]
                                        + [jnp.zeros((128 - 8 * N_HEADS, t), F32)], axis=0).T
            lane = lax.broadcasted_iota(jnp.int32, (t, 128), 1)
            dc = jnp.zeros((t, 128), F32)
            for h in range(N_HEADS):
                dc = jnp.where(lane == h, over_keys[:, 8 * h:8 * h + 1] - refs[1][h][:, LANE_CK:LANE_CK + 1], dc)
            refs[n_in + 2][...] = dc

    tab = pl.BlockSpec((t, GROUP), lambda i: (i, 0))
    heads = pl.BlockSpec((N_HEADS, t, HEAD_DIM), lambda i: (0, i, 0))
    n_in = 6 if rope else 3
    dspec = pl.BlockSpec((t, 3 * GROUP), lambda i: (i, col))
    dshape = jax.ShapeDtypeStruct(dbuf.shape, dbuf.dtype)
    return pl.pallas_call(
        body, name=name, grid=(s // t,),
        out_shape=(dshape, jax.ShapeDtypeStruct((s, 128), F32)) if wide else dshape,
        in_specs=[pl.BlockSpec((N_HEADS, KT_ROWS, t), lambda i: (0, 0, i)),
                  pl.BlockSpec((N_HEADS, t, dk.shape[2]), lambda i: (0, i, 0)), heads]
        + ([tab, tab, tab] if rope else []) + [ANY_SPEC],
        out_specs=(dspec, pl.BlockSpec((t, 128), lambda i: (i, 0))) if wide else dspec,
        input_output_aliases={n_in: 0}, compiler_params=_params(),
    )(*((dqt, dk, dv) + (tuple(tables) if rope else ()) + (dbuf,)))


def _log_sigmoid(x):
    return jnp.minimum(x, 0.0) - jnp.log(1.0 + jnp.exp(-jnp.abs(x)))


def _scan_rows(x, reverse):
    n = x.shape[0]
    row = lax.broadcasted_iota(jnp.int32, x.shape, 0)
    k = 1
    while k < n:
        if reverse:
            x = x + jnp.where(row < n - k, _shift_up(x, k), 0.0)
        else:
            x = x + jnp.where(row >= k, _shift_down(x, k), 0.0)
        k *= 2
    return x


def _gate_cumsum(proj, bias):
    s = proj.shape[0]
    col = COL_GATE // 128

    def body(z_ref, b_ref, c_ref):
        c_ref[...] = _scan_rows(_log_sigmoid(z_ref[...] + b_ref[...]), False)

    return pl.pallas_call(
        body, name="gate_cumsum", grid=(1,), out_shape=jax.ShapeDtypeStruct((s, 128), F32),
        in_specs=[pl.BlockSpec((s, 128), lambda i: (0, col)), pl.BlockSpec((1, 128), lambda i: (0, 0))],
        out_specs=pl.BlockSpec((s, 128), lambda i: (0, 0)), compiler_params=_params(),
    )(proj, bias)


def _gate_cumsum_bwd(proj, bias, dc, dbuf):
    s = proj.shape[0]
    col = COL_GATE // 128

    def body(z_ref, b_ref, dc_ref, buf_ref, dz_ref, db_ref):
        dlogf = _scan_rows(dc_ref[...], True)
        dz = dlogf * _sigmoid(-(z_ref[...] + b_ref[...]))
        dz_ref[...] = dz.astype(dz_ref.dtype)
        db_ref[...] = jnp.sum(dz, axis=0, keepdims=True)

    return pl.pallas_call(
        body, name="gate_cumsum_bwd", grid=(1,),
        out_shape=(jax.ShapeDtypeStruct(dbuf.shape, dbuf.dtype), jax.ShapeDtypeStruct((1, 128), F32)),
        in_specs=[pl.BlockSpec((s, 128), lambda i: (0, col)), pl.BlockSpec((1, 128), lambda i: (0, 0)),
                  pl.BlockSpec((s, 128), lambda i: (0, 0)), ANY_SPEC],
        out_specs=(pl.BlockSpec((s, 128), lambda i: (0, col)), pl.BlockSpec((1, 128), lambda i: (0, 0))),
        input_output_aliases={3: 0}, compiler_params=_params(),
    )(proj, bias, dc, dbuf)


DIL_REACH = 2048


def _pair_weight(mode, d):
    if mode == "fox":
        return jnp.where(d >= 0, 1.0, 0.0)
    w1 = jnp.where(jnp.abs(d - 64) <= 64, 1.0, 0.0)
    w2 = jnp.where((d & 3) == 0, jnp.where(jnp.abs(d - 256) <= 256, 1.0, 0.0), 0.0)
    w3 = jnp.where((d & 15) == 0, jnp.where(jnp.abs(d - 1024) <= 1024, 1.0, 0.0), 0.0)
    return w1 + w2 + w3


def _bias_tables(mode, tq, tk):
    nb = 2 if mode == "fox" else DIL_REACH // tk + 1
    n = lax.broadcasted_iota(jnp.int32, (nb, tk, tq), 0)
    key = lax.broadcasted_iota(jnp.int32, (nb, tk, tq), 1)
    query = lax.broadcasted_iota(jnp.int32, (nb, tk, tq), 2)
    w = _pair_weight(mode, n * tk + query - key)
    return jnp.where(w > 0.0, jnp.log(jnp.maximum(w, 1.0)), NEG)


M_INIT = -1e29


def _first_key_chunk(mode, q0, tk):
    if mode == "fox":
        return 0
    return jnp.maximum(q0 - DIL_REACH, 0) // tk


def _attention_fwd(mode, q, k, vt, tab_t, ybuf, col):
    s, width = q.shape[1], q.shape[2]
    tq = min(ATT_TQ, s)
    tk = tq
    nb = tab_t.shape[0]

    def body(q_ref, k_ref, vt_ref, tab_ref, buf_ref, y_ref, o_ref, lse_ref):
        i = pl.program_id(0)
        lo = _first_key_chunk(mode, i * tq, tk)

        def step(c, carry):
            k0 = pl.multiple_of(c * tk, tk)
            tab = tab_ref[jnp.minimum(i - c, nb - 1)]
            scores = [lax.dot_general(k_ref[h, pl.ds(k0, tk), :], q_ref[h], (NT, ((), ())),
                                      preferred_element_type=F32) for h in range(N_HEADS)]
            stats, probs = [], []
            for h in range(N_HEADS):
                m, l = carry[3 * h:3 * h + 2]
                sc = scores[h] + tab
                m_new = jnp.maximum(m, jnp.max(sc, axis=0, keepdims=True))
                alpha = jnp.exp(m - m_new)
                p = jnp.exp(sc - m_new)
                stats.append((m_new, alpha * l + jnp.sum(p, axis=0, keepdims=True), alpha))
                probs.append(p.astype(BF16))
            pv = [jnp.dot(vt_ref[h, :, pl.ds(k0, tk)], probs[h], preferred_element_type=F32) for h in range(N_HEADS)]
            new = []
            for h in range(N_HEADS):
                m_new, l, alpha = stats[h]
                new += [m_new, l, alpha * carry[3 * h + 2] + pv[h]]
            return tuple(new)

        start = (jnp.full((1, tq), M_INIT, F32), jnp.zeros((1, tq), F32), jnp.zeros((HEAD_DIM, tq), F32))
        done = lax.fori_loop(lo, i + 1, step, start * N_HEADS)
        outs = []
        for h in range(N_HEADS):
            m, l, acc = done[3 * h:3 * h + 3]
            outs.append(acc / l)
            lse_ref[h] = m + jnp.log(l)
        out = jnp.concatenate(outs, axis=0).T
        y_ref[...] = out.astype(y_ref.dtype)
        o_ref[...] = out

    rowspec = pl.BlockSpec((N_HEADS, 1, tq), lambda i: (0, 0, i))
    return pl.pallas_call(
        body, name="attention_fwd_" + mode, grid=(s // tq,),
        out_shape=(jax.ShapeDtypeStruct(ybuf.shape, ybuf.dtype), jax.ShapeDtypeStruct((s, GROUP), F32),
                   jax.ShapeDtypeStruct((N_HEADS, 1, s), F32)),
        in_specs=[pl.BlockSpec((N_HEADS, tq, width), lambda i: (0, i, 0)),
                  pl.BlockSpec((N_HEADS, s, width), lambda i: (0, 0, 0)),
                  pl.BlockSpec((N_HEADS, HEAD_DIM, s), lambda i: (0, 0, 0)),
                  pl.BlockSpec((nb, tk, tq), lambda i: (0, 0, 0)), ANY_SPEC],
        out_specs=(pl.BlockSpec((tq, GROUP), lambda i: (i, col)), pl.BlockSpec((tq, GROUP), lambda i: (i, 0)),
                   rowspec),
        input_output_aliases={4: 0}, compiler_params=_params(),
    )(q, k, vt, tab_t, ybuf)


def _attention_delta(o, do, col):
    s = o.shape[0]
    t = min(ROW_TILE, s)

    def body(o_ref, do_ref, delta_ref, dob_ref):
        dov = do_ref[...]
        prod_t = (o_ref[...] * dov).T
        for h in range(N_HEADS):
            hs = slice(h * HEAD_DIM, (h + 1) * HEAD_DIM)
            delta_ref[h] = jnp.sum(prod_t[hs, :], axis=0, keepdims=True)
            dob_ref[h] = dov[:, hs].astype(dob_ref.dtype)

    return pl.pallas_call(
        body, name="attention_delta", grid=(s // t,),
        out_shape=(jax.ShapeDtypeStruct((N_HEADS, 1, s), F32), jax.ShapeDtypeStruct((N_HEADS, s, HEAD_DIM), BF16)),
        in_specs=[pl.BlockSpec((t, GROUP), lambda i: (i, 0)), pl.BlockSpec((t, GROUP), lambda i: (i, col))],
        out_specs=(pl.BlockSpec((N_HEADS, 1, t), lambda i: (0, 0, i)),
                   pl.BlockSpec((N_HEADS, t, HEAD_DIM), lambda i: (0, i, 0))),
        compiler_params=_params(),
    )(o, do)


def _attention_bwd(mode, q, k, v, kt, tab_t, dob, lse, delta):
    s, width = q.shape[1], q.shape[2]
    tq = min(ATT_TQ, s)
    tk = tq
    nq = s // tq
    nb = tab_t.shape[0]

    def body(q_ref, k_ref, v_ref, kt_ref, tab_ref, dob_ref, lse_ref, delta_ref, dqt_ref, dk_ref, dv_ref):
        i = pl.program_id(0)

        @pl.when(i == 0)
        def _():
            dqt_ref[...] = jnp.zeros_like(dqt_ref)

        hi = nq if mode == "fox" else jnp.minimum((i * tk + tk - 1 + DIL_REACH) // tq + 1, nq)
        for h0 in range(0, N_HEADS, BWD_HEADS):
            heads = range(h0, h0 + BWD_HEADS)

            def step(c, carry, heads=heads):
                q0 = pl.multiple_of(c * tq, tq)
                qs = pl.ds(q0, tq)
                tab = tab_ref[jnp.minimum(c - i, nb - 1)]
                qv = [q_ref[h, qs, :] for h in heads]
                dov = [dob_ref[h, qs, :] for h in heads]
                sc = [lax.dot_general(k_ref[h], qv[n], (NT, ((), ())), preferred_element_type=F32)
                      for n, h in enumerate(heads)]
                dp = [lax.dot_general(v_ref[h], dov[n], (NT, ((), ())), preferred_element_type=F32)
                      for n, h in enumerate(heads)]
                pb, dsb = [], []
                for n, h in enumerate(heads):
                    p = jnp.exp(sc[n] + tab - lse_ref[h, :, qs])
                    pb.append(p.astype(BF16))
                    dsb.append((p * (dp[n] - delta_ref[h, :, qs])).astype(BF16))
                new = []
                for n, h in enumerate(heads):
                    new += [carry[2 * n] + jnp.dot(dsb[n], qv[n], preferred_element_type=F32),
                            carry[2 * n + 1] + jnp.dot(pb[n], dov[n], preferred_element_type=F32)]
                for n, h in enumerate(heads):
                    dqt_ref[h, :, qs] += jnp.dot(kt_ref[h], dsb[n], preferred_element_type=F32)
                return tuple(new)

            start = (jnp.zeros((tk, width), F32), jnp.zeros((tk, HEAD_DIM), F32))
            done = lax.fori_loop(i, hi, step, start * BWD_HEADS)
            for n, h in enumerate(heads):
                dk_ref[h] = done[2 * n]
                dv_ref[h] = done[2 * n + 1]

    def full(shape):
        return pl.BlockSpec(shape, lambda i: (0, 0, 0))

    kblk = pl.BlockSpec((N_HEADS, tk, width), lambda i: (0, i, 0))
    vblk = pl.BlockSpec((N_HEADS, tk, HEAD_DIM), lambda i: (0, i, 0))
    return pl.pallas_call(
        body, name="attention_bwd_" + mode, grid=(s // tk,),
        out_shape=(jax.ShapeDtypeStruct((N_HEADS, KT_ROWS, s), F32), jax.ShapeDtypeStruct((N_HEADS, s, width), F32),
                   jax.ShapeDtypeStruct((N_HEADS, s, HEAD_DIM), F32)),
        in_specs=[full((N_HEADS, s, width)), kblk, vblk, pl.BlockSpec((N_HEADS, KT_ROWS, tk), lambda i: (0, 0, i)),
                  full((nb, tk, tq)), full((N_HEADS, s, HEAD_DIM)), full((N_HEADS, 1, s)), full((N_HEADS, 1, s))],
        out_specs=(full((N_HEADS, KT_ROWS, s)), kblk, vblk),
        compiler_params=_params(),
    )(q, k, v, kt, tab_t, dob, lse, delta)


def _xattn_fwd(qx, kvm):
    s = qx.shape[0]
    t = min(ROW_TILE, s)

    def body(q_ref, kv_ref, o_ref):
        heads = range(XA_HEADS)
        sc = [lax.dot_general(q_ref[:, h * XA_DIM:(h + 1) * XA_DIM].astype(BF16), kv_ref[h].astype(BF16),
                              (NT, ((), ())), preferred_element_type=F32) * (XA_DIM ** -0.5) for h in heads]
        probs = []
        for h in heads:
            e = jnp.exp(sc[h] - jnp.max(sc[h], axis=-1, keepdims=True))
            probs.append((e / jnp.sum(e, axis=-1, keepdims=True)).astype(BF16))
        outs = [jnp.dot(probs[h], kv_ref[XA_HEADS + h].astype(BF16), preferred_element_type=F32) for h in heads]
        for h in heads:
            o_ref[:, h * XA_DIM:(h + 1) * XA_DIM] = outs[h].astype(o_ref.dtype)

    return pl.pallas_call(
        body, name="xattn_fwd", grid=(s // t,), out_shape=jax.ShapeDtypeStruct((s, D_MODEL), BF16),
        in_specs=[pl.BlockSpec((t, D_MODEL), lambda i: (i, 0)),
                  pl.BlockSpec((2 * XA_HEADS, MEM_LEN, XA_DIM), lambda i: (0, 0, 0))],
        out_specs=pl.BlockSpec((t, D_MODEL), lambda i: (i, 0)), compiler_params=_params(),
    )(qx, kvm)


def _xattn_bwd(qx, kvm, do):
    s = qx.shape[0]
    t = min(ROW_TILE, s)

    def body(q_ref, kv_ref, do_ref, dq_ref, dkv_ref):
        i = pl.program_id(0)
        heads = range(XA_HEADS)
        qv = [q_ref[:, h * XA_DIM:(h + 1) * XA_DIM].astype(BF16) for h in heads]
        dov = [do_ref[:, h * XA_DIM:(h + 1) * XA_DIM].astype(BF16) for h in heads]
        kv = [kv_ref[h].astype(BF16) for h in heads]
        sc = [lax.dot_general(qv[h], kv[h], (NT, ((), ())), preferred_element_type=F32) * (XA_DIM ** -0.5)
              for h in heads]
        dp = [lax.dot_general(dov[h], kv_ref[XA_HEADS + h].astype(BF16), (NT, ((), ())), preferred_element_type=F32)
              for h in heads]
        pb, ds = [], []
        for h in heads:
            e = jnp.exp(sc[h] - jnp.max(sc[h], axis=-1, keepdims=True))
            p = e / jnp.sum(e, axis=-1, keepdims=True)
            pb.append(p.astype(BF16))
            ds.append((p * (dp[h] - jnp.sum(p * dp[h], axis=-1, keepdims=True)) * (XA_DIM ** -0.5)).astype(BF16))
        dq = [jnp.dot(ds[h], kv[h], preferred_element_type=F32) for h in heads]
        dk = [lax.dot_general(ds[h], qv[h], (TN, ((), ())), preferred_element_type=F32) for h in heads]
        dv = [lax.dot_general(pb[h], dov[h], (TN, ((), ())), preferred_element_type=F32) for h in heads]
        for h in heads:
            dq_ref[:, h * XA_DIM:(h + 1) * XA_DIM] = dq[h].astype(dq_ref.dtype)

        @pl.when(i == 0)
        def _():
            for h in heads:
                dkv_ref[h] = dk[h]
                dkv_ref[XA_HEADS + h] = dv[h]

        @pl.when(i > 0)
        def _():
            for h in heads:
                dkv_ref[h] += dk[h]
                dkv_ref[XA_HEADS + h] += dv[h]

    row = pl.BlockSpec((t, D_MODEL), lambda i: (i, 0))
    kvs = pl.BlockSpec((2 * XA_HEADS, MEM_LEN, XA_DIM), lambda i: (0, 0, 0))
    return pl.pallas_call(
        body, name="xattn_bwd", grid=(s // t,),
        out_shape=(jax.ShapeDtypeStruct((s, D_MODEL), BF16),
                   jax.ShapeDtypeStruct((2 * XA_HEADS, MEM_LEN, XA_DIM), F32)),
        in_specs=[row, kvs, row], out_specs=(row, kvs), compiler_params=_params(),
    )(qx, kvm, do)


def _adamw(parts, owns, me, w, m, v, name):
    nl, r, c = w.shape
    tr = r
    for cand in (512, 352, 256, 176, 128, 64, 32, 16, 8):
        if r % cand == 0 and r > cand and N_DEV * cand * c * 4 <= ADAMW_BLOCK_BYTES:
            tr = cand
            break
    nt = r // tr
    per_layer = N_DEV + (1 if owns is not None else 0)

    def body(me_ref, *refs):
        w_ref, m_ref, v_ref, g_ref, d_ref, nm_ref, nv_ref = refs[nl * per_layer:]
        layer = pl.program_id(0)
        g = None
        for l in range(nl):
            p_refs = refs[l * per_layer:(l + 1) * per_layer]
            gl = None
            for d in range(N_DEV):
                term = p_refs[d][...].astype(F32)
                if owns is not None:
                    term = jnp.where(me_ref[0] == d, p_refs[N_DEV][...].astype(F32), term)
                gl = term if gl is None else gl + term
            g = gl if g is None else jnp.where(layer == l, gl, g)
        mn = ADAM_B1 * m_ref[...] + (1.0 - ADAM_B1) * g
        vn = ADAM_B2 * v_ref[...] + (1.0 - ADAM_B2) * (g * g)
        m_hat = mn / (1.0 - ADAM_B1 ** ADAM_STEP)
        v_hat = vn / (1.0 - ADAM_B2 ** ADAM_STEP)
        g_ref[...] = g
        d_ref[...] = -ADAM_LR * (m_hat / (jnp.sqrt(v_hat) + ADAM_EPS) + ADAM_WD * w_ref[...])
        nm_ref[...] = mn
        nv_ref[...] = vn

    def rows(l, ll, i):
        return jnp.where(ll == l, i, jnp.where(ll < l, 0, nt - 1))

    def part_spec(l, d):
        if owns is None:
            return pl.BlockSpec((None, tr, c), lambda ll, i, me_ref: (d, rows(l, ll, i), 0))
        return pl.BlockSpec((None, tr, c),
                            lambda ll, i, me_ref: (jnp.where(me_ref[0] == d, (d + 1) % N_DEV, d), rows(l, ll, i), 0))

    def own_spec(l):
        return pl.BlockSpec((None, tr, c), lambda ll, i, me_ref: (me_ref[0], rows(l, ll, i), 0))

    in_specs, operands = [], []
    for l in range(nl):
        in_specs += [part_spec(l, d) for d in range(N_DEV)]
        operands += [parts[l]] * N_DEV
        if owns is not None:
            in_specs.append(own_spec(l))
            operands.append(owns[l])
    blk = pl.BlockSpec((None, tr, c), lambda ll, i, me_ref: (ll, i, 0))
    shp = jax.ShapeDtypeStruct((nl, r, c), F32)
    return pl.pallas_call(
        body, name=name, out_shape=(shp, shp, shp, shp),
        grid_spec=pltpu.PrefetchScalarGridSpec(
            num_scalar_prefetch=1, grid=(nl, nt), in_specs=in_specs + [blk, blk, blk],
            out_specs=(blk, blk, blk, blk)),
        compiler_params=_params(),
    )(me.reshape(1), *operands, w, m, v)


GROUPS = {"in": ("w_in",), "rest": ("w_out", "w_xq", "w_xo", "w_xkv", "w_up", "w_down")}
FULL_SHAPES = {"w_in": (D_MODEL, N_IN_PAD), "w_out": (D_MODEL, D_MODEL), "w_xq": (D_MODEL, D_MODEL),
               "w_xo": (D_MODEL, D_MODEL), "w_xkv": (N_DEV, D_MODEL, 2 * D_MODEL // N_DEV),
               "w_up": (N_DEV, FF_SHARD, D_MODEL), "w_down": (FF_HALF, FF_SHARD, D_MODEL)}
PIECE_SHAPES = {"w_in": (N_DEV, D_MODEL // N_DEV, N_IN_PAD), "w_out": (N_DEV, D_MODEL // N_DEV, D_MODEL),
                "w_xq": (N_DEV, D_MODEL // N_DEV, D_MODEL), "w_xo": (N_DEV, D_MODEL // N_DEV, D_MODEL),
                "w_xkv": (N_DEV, D_MODEL, 2 * D_MODEL // N_DEV), "w_up": (N_DEV, FF_SHARD, D_MODEL),
                "w_down": (N_DEV, D_FF // N_DEV, D_MODEL)}
CONV_WORDS = 8192


class _GatheredWeights:
    def __init__(self, states, layer):
        self.states, self.layer, self.full, self.extra = dict(states), layer, {}, None

    def need(self, group, after):
        if group in self.states:
            got, _ = _exchange_wait(self.states.pop(group), after, "gather_%s_wait_%d" % (group, self.layer))
            for name, g in zip(GROUPS[group], got):
                self.full[name] = g.reshape(FULL_SHAPES[name])
            self.extra = got[len(GROUPS[group]):]

    def __getitem__(self, name):
        return self.full[name]


def _relay_in_cols(w):
    pad = jnp.zeros(w.shape[:-1] + (N_IN_PAD - N_IN,), w.dtype)
    return jnp.concatenate([w[..., :2304], w[..., 2308:N_IN], w[..., 2304:2308], pad], axis=-1)


def _unrelay_in_cols(w):
    return jnp.concatenate([w[..., :2304], w[..., COL_GATE:COL_GATE + 4], w[..., 2304:COL_GATE]], axis=-1)


def _layer_fwd(h, memv, w, sm, tables):
    sv = {"h0": h}
    s = h.shape[0]
    tm, tb = min(ROW_TILE, s), min(MM_TILE, s)
    w.need("in", h)
    tn = N_IN_PAD // 3
    proj, xn = _norm_matmul(h, sm["g_mix"], w["w_in"], (s, N_IN_PAD), grid=(s // tb, 3),
                            b_spec=pl.BlockSpec((D_MODEL, tn), lambda i, j: (0, j)),
                            o_spec=pl.BlockSpec((tb, tn), lambda i, j: (i, j)), name="norm_mm_in")
    sv["xn"], sv["proj"] = xn, proj
    ycat = _sconv_fwd(proj, sm["w_sconv"])
    qd, kd, vd, ktd, vtd = _heads_split(proj, 1, tables["rope"], None, "split_dil")
    ycat, ob, lse_b = _attention_fwd("dil", qd, kd, vtd, tables["dil"], ycat, 1)
    sv["dil"] = (qd, kd, vd, ktd, ob, lse_b)
    c = _gate_cumsum(proj, sm["b_forget_pad"])
    qf, kf, vf, ktf, vtf = _heads_split(proj, 2, None, c, "split_fox")
    ycat, oc, lse_c = _attention_fwd("fox", qf, kf, vtf, tables["fox"], ycat, 2)
    sv["fox"] = (qf, kf, vf, ktf, oc, lse_c)
    ycat = _pool_fwd(proj, sm["w_pool_bd"], sm["pool_scale"], ycat)
    sv["ycat"] = ycat
    w.need("rest", ycat)
    h1 = _mm_nn(ycat, w["w_out"], "mm_out", res=h)
    sv["h1"] = h1
    memn = _rms_fwd(memv, sm["g_mem"], "rms_mem")
    qx, xq = _norm_matmul(h1, sm["g_xa"], w["w_xq"], (s, D_MODEL), grid=(s // tb, 1),
                          b_spec=pl.BlockSpec((D_MODEL, D_MODEL), lambda i, j: (0, 0)),
                          o_spec=pl.BlockSpec((tb, D_MODEL), lambda i, j: (i, 0)), name="norm_mm_xq",
                          out_dtype=BF16)
    kvm = _matmul(memn, w["w_xkv"], (N_DEV, MEM_LEN, XA_DIM), grid=(N_DEV, 1, 1),
                  a_spec=pl.BlockSpec((MEM_LEN, D_MODEL), lambda i, j, r: (0, 0)),
                  b_spec=pl.BlockSpec((None, D_MODEL, XA_DIM), lambda i, j, r: (i, 0, 0)),
                  o_spec=pl.BlockSpec((None, MEM_LEN, XA_DIM), lambda i, j, r: (i, 0, 0)),
                  dims=NN, nred=1, name="mm_xkv")
    ox = _xattn_fwd(qx, kvm)
    sv.update(xq=xq, memn=memn, qx=qx, kvm=kvm, ox=ox)
    h2 = _mm_nn(ox, w["w_xo"], "mm_xo", res=h1)
    sv["h2"] = h2
    u0, xf = _norm_matmul(h2, sm["g_ffn"], w["w_up"], (N_DEV, s, FF_SHARD), grid=(s // tb, N_DEV),
                          b_spec=pl.BlockSpec((None, FF_SHARD, D_MODEL), lambda i, j: (j, 0, 0)),
                          o_spec=pl.BlockSpec((None, tb, FF_SHARD), lambda i, j: (j, i, 0)), name="norm_mm_up",
                          out_dtype=BF16, dims=NT)
    act = _ffn_gate_fwd(u0, sm["w_ffconv"])
    sv.update(xf=xf, u0=u0, act=act)
    ospec = pl.BlockSpec((tm, D_MODEL), lambda i, j, r: (i, 0))
    h3 = _matmul(act, w["w_down"], (s, D_MODEL), grid=(s // tm, 1, 1),
                 a_spec=pl.BlockSpec((FF_HALF, tm, FF_SHARD), lambda i, j, r: (0, i, 0)),
                 b_spec=pl.BlockSpec((FF_HALF, FF_SHARD, D_MODEL), lambda i, j, r: (0, 0, 0)),
                 o_spec=ospec, dims=NN, nred=1, slabs=FF_HALF, name="mm_down", res=h2, res_spec=ospec)
    return h3, sv


def _layer_bwd(dh3, memv, w, sm, tables, sv, rest_ready):
    s = dh3.shape[0]
    tm, tb = min(ROW_TILE, s), min(MM_TILE, s)
    big, small = {}, {}
    ts = max(s // 2, 1)
    dact = _matmul(dh3, w["w_down"], (FF_HALF, s, FF_SHARD), grid=(s // tb, FF_HALF, 1),
                   a_spec=pl.BlockSpec((tb, D_MODEL), lambda i, j, r: (i, 0)),
                   b_spec=pl.BlockSpec((None, FF_SHARD, D_MODEL), lambda i, j, r: (j, 0, 0)),
                   o_spec=pl.BlockSpec((None, tb, FF_SHARD), lambda i, j, r: (j, i, 0)),
                   dims=NT, nred=1, name="mm_dact", out_dtype=BF16)
    big["w_down"] = _matmul(sv["act"], dh3, (FF_HALF, FF_SHARD, D_MODEL), grid=(FF_HALF, 1, s // ts),
                            a_spec=pl.BlockSpec((None, ts, FF_SHARD), lambda i, j, r: (i, r, 0)),
                            b_spec=pl.BlockSpec((ts, D_MODEL), lambda i, j, r: (r, 0)),
                            o_spec=pl.BlockSpec((None, FF_SHARD, D_MODEL), lambda i, j, r: (i, 0, 0)),
                            dims=TN, nred=s // ts, name="mm_dw_down", out_dtype=GRAD_DTYPE)
    du0, small["w_ffconv"] = _ffn_gate_bwd(sv["u0"], sm["w_ffconv"], dact)
    dh2, small["g_ffn"] = _matmul_rms_bwd(du0, w["w_up"], sv["h2"], sm["g_ffn"], dh3, "mm_dxf_rms_bwd",
                                          tm=ROW_TILE // 2, dims=NN)
    big["w_up"] = _matmul(du0, sv["xf"], (N_DEV, FF_SHARD, D_MODEL), grid=(N_DEV, 1, 1),
                          a_spec=pl.BlockSpec((None, s, FF_SHARD), lambda i, j, r: (i, 0, 0)),
                          b_spec=pl.BlockSpec((s, D_MODEL), lambda i, j, r: (0, 0)),
                          o_spec=pl.BlockSpec((None, FF_SHARD, D_MODEL), lambda i, j, r: (i, 0, 0)),
                          dims=TN, nred=1, name="mm_dw_up", out_dtype=GRAD_DTYPE)
    dox = _mm_nt(dh2, w["w_xo"], "mm_dox", out_dtype=BF16)
    big["w_xo"] = _mm_tn(sv["ox"], dh2, "mm_dw_xo")
    dqx, dkvm = _xattn_bwd(sv["qx"], sv["kvm"], dox)
    big["w_xq"] = _mm_tn(sv["xq"], dqx, "mm_dw_xq")
    big["w_xkv"] = _matmul(sv["memn"], dkvm, (N_DEV, D_MODEL, XA_DIM), grid=(N_DEV, 1, 1),
                           a_spec=pl.BlockSpec((MEM_LEN, D_MODEL), lambda i, j, r: (0, 0)),
                           b_spec=pl.BlockSpec((None, MEM_LEN, XA_DIM), lambda i, j, r: (i, 0, 0)),
                           o_spec=pl.BlockSpec((None, D_MODEL, XA_DIM), lambda i, j, r: (i, 0, 0)),
                           dims=TN, nred=1, name="mm_dw_xkv", out_dtype=GRAD_DTYPE)
    dmemn = _matmul(dkvm, w["w_xkv"], (MEM_LEN, D_MODEL), grid=(1, 1, 1),
                    a_spec=pl.BlockSpec((N_DEV, MEM_LEN, XA_DIM), lambda i, j, r: (0, 0, 0)),
                    b_spec=pl.BlockSpec((N_DEV, D_MODEL, XA_DIM), lambda i, j, r: (0, 0, 0)),
                    o_spec=pl.BlockSpec((MEM_LEN, D_MODEL), lambda i, j, r: (0, 0)),
                    dims=NT, nred=1, slabs=N_DEV, name="mm_dmemn")
    _, small["g_mem"] = _rms_bwd(dmemn, memv, sm["g_mem"], None, "rms_mem_bwd")
    dh1, small["g_xa"] = _matmul_rms_bwd(dqx, w["w_xq"], sv["h1"], sm["g_xa"], dh2, "mm_dxq_rms_bwd")
    big["w_out"] = _mm_tn(sv["ycat"], dh1, "mm_dw_out")
    dycat = _mm_nt(dh1, w["w_out"] + rest_ready(big, small).astype(BF16), "mm_dycat")
    proj = sv["proj"]
    dproj, small["w_sconv"] = _sconv_bwd(proj, sm["w_sconv"], dycat)
    qd, kd, vd, ktd, ob, lse_b = sv["dil"]
    delta, dob = _attention_delta(ob, dycat, 1)
    dqt, dk, dv = _attention_bwd("dil", qd, kd, vd, ktd, tables["dil"], dob, lse_b, delta)
    dproj = _heads_merge(dqt, dk, dv, tables["rope"], "merge_dil", dproj, 1)
    qf, kf, vf, ktf, oc, lse_c = sv["fox"]
    delta, dob = _attention_delta(oc, dycat, 2)
    dqt, dk, dv = _attention_bwd("fox", qf, kf, vf, ktf, tables["fox"], dob, lse_c, delta)
    dproj, dc = _heads_merge(dqt, dk, dv, None, "merge_fox", dproj, 2)
    dproj, dbias = _gate_cumsum_bwd(proj, sm["b_forget_pad"], dc, dproj)
    small["b_forget"] = dbias[0, :N_HEADS]
    dproj, dwbd, small["pool_scale"] = _pool_bwd(proj, sm["w_pool_bd"], sm["pool_scale"], dycat, dproj)
    small["w_pool"] = jnp.stack([dwbd[64 * g:64 * (g + 1), 64 * g:64 * (g + 1)] for g in range(4)])
    big["w_in"] = _mm_tn(sv["xn"], dproj, "mm_dw_in", tn=896)
    dh0, small["g_mix"] = _matmul_rms_bwd(dproj, w["w_in"], sv["h0"], sm["g_mix"], dh1, "mm_dxn_rms_bwd")
    return dh0, big, small


SMALL_NAMES = ("g_mix", "b_forget", "w_pool", "pool_scale", "g_xa", "g_mem", "g_ffn", "w_sconv", "w_ffconv")
SMALL_WITH = {"rest": ("w_ffconv", "g_ffn", "g_mem", "g_xa"),
              "in": ("w_sconv", "b_forget", "pool_scale", "w_pool", "g_mix")}
SMALL_SHAPES = {"w_sconv": (3, GROUP), "w_ffconv": (N_DEV, 3, FF_SHARD)}
WEIGHT_NAMES = ("g_mix", "w_in", "b_forget", "w_sconv", "w_pool", "pool_scale", "w_out", "g_xa", "g_mem", "w_xq",
                "w_xkv", "w_xo", "g_ffn", "w_up", "w_ffconv", "w_down", "g_final")


def _block_diag(w_pool):
    z = jnp.zeros((64, 64), F32)
    return jnp.concatenate(
        [jnp.concatenate([w_pool[g] if c == g else z for c in range(4)], axis=1) for g in range(4)], axis=0)


def kernel(x, mem, positions, g_mix, w_in, b_forget, w_sconv, w_pool, pool_scale, w_out, g_xa, g_mem, w_xq, w_xkv, w_xo, g_ffn, w_up, w_ffconv, w_down, g_final, loss_target, m_g_mix, m_w_in, m_b_forget, m_w_sconv, m_w_pool, m_pool_scale, m_w_out, m_g_xa, m_g_mem, m_w_xq, m_w_xkv, m_w_xo, m_g_ffn, m_w_up, m_w_ffconv, m_w_down, m_g_final, v_g_mix, v_w_in, v_b_forget, v_w_sconv, v_w_pool, v_pool_scale, v_w_out, v_g_xa, v_g_mem, v_w_xq, v_w_xkv, v_w_xo, v_g_ffn, v_w_up, v_w_ffconv, v_w_down, v_g_final):
    weights = dict(g_mix=g_mix, w_in=w_in, b_forget=b_forget, w_sconv=w_sconv, w_pool=w_pool, pool_scale=pool_scale,
                   w_out=w_out, g_xa=g_xa, g_mem=g_mem, w_xq=w_xq, w_xkv=w_xkv, w_xo=w_xo, g_ffn=g_ffn, w_up=w_up,
                   w_ffconv=w_ffconv, w_down=w_down, g_final=g_final)
    m_in = dict(g_mix=m_g_mix, w_in=m_w_in, b_forget=m_b_forget, w_sconv=m_w_sconv, w_pool=m_w_pool,
                pool_scale=m_pool_scale, w_out=m_w_out, g_xa=m_g_xa, g_mem=m_g_mem, w_xq=m_w_xq, w_xkv=m_w_xkv,
                w_xo=m_w_xo, g_ffn=m_g_ffn, w_up=m_w_up, w_ffconv=m_w_ffconv, w_down=m_w_down, g_final=m_g_final)
    v_in = dict(g_mix=v_g_mix, w_in=v_w_in, b_forget=v_b_forget, w_sconv=v_w_sconv, w_pool=v_w_pool,
                pool_scale=v_pool_scale, w_out=v_w_out, g_xa=v_g_xa, g_mem=v_g_mem, w_xq=v_w_xq, w_xkv=v_w_xkv,
                w_xo=v_w_xo, g_ffn=v_g_ffn, w_up=v_w_up, w_ffconv=v_w_ffconv, w_down=v_w_down, g_final=v_g_final)
    depth = w_in.shape[0]
    me = 4 * lax.axis_index("x") + 2 * lax.axis_index("y") + lax.axis_index("c")
    h = x[0]
    memv = mem[0]
    s = h.shape[0]
    tq = min(ATT_TQ, s)
    tables = {"rope": _rope_tables(positions[0]), "dil": _bias_tables("dil", tq, tq),
              "fox": _bias_tables("fox", tq, tq)}

    w_in_r = _relay_in_cols(w_in)
    conv_shard = jnp.concatenate([w_sconv.reshape(-1), w_ffconv.reshape(-1)])
    conv_shard = jnp.concatenate([conv_shard, jnp.zeros((CONV_WORDS - conv_shard.shape[0],), F32)])
    shards = dict(w_in=w_in_r, w_out=w_out, w_xq=w_xq, w_xo=w_xo, w_xkv=w_xkv, w_up=w_up.transpose(0, 2, 1),
                  w_down=w_down)
    gathered = []
    order = jnp.zeros((), F32)
    for l in range(depth):
        states = {}
        for group in ("in", "rest"):
            first = GROUPS[group][0]
            shards[first] = shards[first] + order
            xs = [_place_shard(shards[name], l, me, BF16, "place_%s_%d" % (name, l)) for name in GROUPS[group]]
            if l == 0 and group == "in":
                xs.append(_place_shard(conv_shard.reshape(1, CONV_WORDS // 1024, 1024), 0, me, F32, "place_conv"))
            states[group], token = _exchange_start(xs, False, "gather_%s_start_%d" % (group, l))
            order = order + token[0, 0]
        gathered.append(_GatheredWeights(states, l))
    gathered[0].need("in", tables["rope"][0])
    conv_all = gathered[0].extra[0].reshape(N_DEV, CONV_WORDS)
    n_sc = depth * 3 * (GROUP // N_DEV)
    sconv_full = conv_all[:, :n_sc].reshape(N_DEV, depth, 3, GROUP // N_DEV).transpose(1, 2, 0, 3).reshape(
        depth, 3, GROUP)
    ffconv_full = conv_all[:, n_sc:n_sc + depth * 3 * FF_SHARD].reshape(N_DEV, depth, 3, FF_SHARD).transpose(
        1, 0, 2, 3)

    smalls = []
    for l in range(depth):
        smalls.append(dict(
            g_mix=g_mix[l], g_xa=g_xa[l], g_mem=g_mem[l], g_ffn=g_ffn[l], pool_scale=pool_scale[l],
            w_pool_bd=_block_diag(w_pool[l]), w_sconv=sconv_full[l], w_ffconv=ffconv_full[l],
            b_forget_pad=jnp.concatenate([b_forget[l], jnp.zeros((128 - N_HEADS,), F32)]).reshape(1, 128)))
    smalls[0]["g_mix"] = smalls[0]["g_mix"] + order

    saved = []
    for l in range(depth):
        h, sv = _layer_fwd(h, memv, gathered[l], smalls[l], tables)
        saved.append(sv)
    loss_part, dh, dg_final = _loss_head(h, g_final, loss_target[0])
    loss = lax.psum(loss_part[0, 0], MESH_AXES)

    small_grads = [None] * depth
    scatters = {}

    def pieces_of(big, group):
        return [big[name].reshape(PIECE_SHAPES[name]) for name in GROUPS[group]]

    def rider(grads):
        flat = jnp.concatenate([g.reshape(-1) for g in grads])
        rows = -(-flat.shape[0] // 1024)
        flat = jnp.concatenate([flat, jnp.zeros((rows * 1024 - flat.shape[0],), F32)])
        return jnp.broadcast_to(flat.reshape(1, rows, 1024), (N_DEV, rows, 1024))

    for l in reversed(range(depth)):
        def rest_ready(big, small, l=l):
            ready = [small[n] for n in SMALL_WITH["rest"]] + ([dg_final] if l == depth - 1 else [])
            scatters[l, "rest"], token = _exchange_start(pieces_of(big, "rest") + [rider(ready)], True,
                                                         "scatter_rest_start_%d" % l)
            return token[0, 0]

        dh, big, small_grads[l] = _layer_bwd(dh, memv, gathered[l], smalls[l], tables, saved[l], rest_ready)
        xs = pieces_of(big, "in") + [rider([small_grads[l][n] for n in SMALL_WITH["in"]])]
        scatters[l, "in"], token = _exchange_start(xs, True, "scatter_in_start_%d" % l)
        if l > 0:
            smalls[l - 1]["w_ffconv"] = smalls[l - 1]["w_ffconv"] + token[0, 0]
    grad_x = dh[None]

    parts, owns, small_parts = {}, {}, {}

    def wait_group(group, after):
        for l in reversed(range(depth)):
            got, given = _exchange_wait(scatters[l, group], after, "scatter_%s_wait_%d" % (group, l))
            for name, g, x in zip(GROUPS[group], got, given):
                parts.setdefault(name, [None] * depth)[l] = g
                owns.setdefault(name, [None] * depth)[l] = x
            flat = lax.dynamic_update_slice_in_dim(got[-1], given[-1][:1], me, axis=0).reshape(N_DEV, -1)
            off = 0
            for name in SMALL_WITH[group] + (("g_final",) if group == "rest" and l == depth - 1 else ()):
                shape = SMALL_SHAPES.get(name, weights[name].shape[-1:] if name == "g_final"
                                         else weights[name].shape[1:])
                n = 1
                for dim in shape:
                    n *= dim
                small_parts.setdefault(name, [None] * depth)[l] = flat[:, off:off + n].reshape((N_DEV,) + shape)
                off += n

    results = {}

    def update(name, w3, m3, v3):
        outs = _adamw(parts[name], owns.get(name), me, w3, m3, v3, "adamw_" + name)
        results[name] = [o.reshape(weights[name].shape) for o in outs]

    wait_group("rest", grad_x)
    for name in GROUPS["rest"]:
        if name == "w_up":
            outs = _adamw(parts[name], owns[name], me, w_up.transpose(0, 2, 1), m_w_up.transpose(0, 2, 1),
                          v_w_up.transpose(0, 2, 1), "adamw_w_up")
            results[name] = [o.transpose(0, 2, 1) for o in outs]
        else:
            update(name, weights[name], m_in[name], v_in[name])
    wait_group("in", results["w_down"][1])
    outs = _adamw(parts["w_in"], owns["w_in"], me, w_in_r, _relay_in_cols(m_w_in), _relay_in_cols(v_w_in),
                  "adamw_w_in")
    results["w_in"] = [_unrelay_in_cols(o) for o in outs]
    for name in SMALL_NAMES + ("g_final",):
        wv = weights[name]
        p = small_parts[name][depth - 1] if name == "g_final" else jnp.stack(small_parts[name], axis=1)
        if name == "w_sconv":
            p = lax.dynamic_slice_in_dim(p, me * (GROUP // N_DEV), GROUP // N_DEV, axis=3)
        elif name == "w_ffconv":
            p = lax.dynamic_index_in_dim(p, me, axis=2, keepdims=False)
        shape3 = (1, 1, wv.shape[0]) if wv.ndim == 1 else (1, -1, wv.shape[-1])
        w3 = wv.reshape(shape3)
        parts[name] = [p.reshape((N_DEV,) + w3.shape[1:])]
        update(name, w3, m_in[name].reshape(shape3), v_in[name].reshape(shape3))

    return (loss, grad_x, *[results[n][0] for n in WEIGHT_NAMES], *[results[n][1] for n in WEIGHT_NAMES],
            *[results[n][2] for n in WEIGHT_NAMES], *[results[n][3] for n in WEIGHT_NAMES])
```

```python
import functools

import jax
import jax.numpy as jnp
from jax import lax
from jax.experimental import pallas as pl
from jax.experimental.pallas import tpu as pltpu

F32 = jnp.float32
BF16 = jnp.bfloat16

N_DEV = 8
D_MODEL = 1024
GROUP = 256
HEAD_DIM = 64
N_HEADS = 4
N_IN = 2564
N_IN_PAD = 2688
COL_GATE = 2560
XA_HEADS = 4
XA_DIM = 256
MEM_LEN = 256
D_FF = 2816
FF_SHARD = 704
FF_HALF = 4
ROPE_THETA = 500000.0
ROPE_DIM = 16
RMS_EPS = 1e-6
NEG = -1e30
POOL_WINDOWS = (2, 4, 8, 16)
ADAM_LR, ADAM_B1, ADAM_B2, ADAM_EPS, ADAM_WD, ADAM_STEP = 0.001, 0.9, 0.999, 1e-08, 0.01, 10

ROW_TILE = 512
MM_TILE = 1024
ATT_TQ = 512
BWD_HEADS = 4
VMEM_LIMIT = 56 * 1024 * 1024
ADAMW_BLOCK_BYTES = 8 * 1024 * 1024
PLACE_BLOCK_BYTES = 4 * 1024 * 1024

MESH_AXES = ("x", "y", "c")


def _params(**kw):
    return pltpu.CompilerParams(vmem_limit_bytes=VMEM_LIMIT, **kw)


HBM_SPEC = pl.BlockSpec(memory_space=pltpu.HBM)
SEM_SPEC = pl.BlockSpec(memory_space=pltpu.SEMAPHORE)
DATAFLOW = pltpu.SideEffectType.DATAFLOW_SIDE_EFFECTING


def _peer_copies(x_ref, land_ref, send_sems, recv_sems, scatter):
    mx, my, mc = lax.axis_index("x"), lax.axis_index("y"), lax.axis_index("c")
    me = 4 * mx + 2 * my + mc
    pairs = []
    for k in range(1, N_DEV):
        kx, ky, kc = (k >> 2) & 1, (k >> 1) & 1, k & 1
        peer_lin = me ^ k
        send = pltpu.make_async_remote_copy(
            src_ref=x_ref.at[peer_lin] if scatter else land_ref.at[me], dst_ref=land_ref.at[me],
            send_sem=send_sems.at[k - 1], recv_sem=recv_sems.at[k - 1],
            device_id=(mx ^ kx, my ^ ky, mc ^ kc), device_id_type=pl.DeviceIdType.MESH)
        arrival = pltpu.make_async_remote_copy(
            src_ref=land_ref.at[peer_lin], dst_ref=land_ref.at[peer_lin],
            send_sem=send_sems.at[k - 1], recv_sem=recv_sems.at[k - 1],
            device_id=(mx, my, mc), device_id_type=pl.DeviceIdType.MESH)
        pairs.append((send, arrival))
    return pairs


def _exchange_start(xs, scatter, name):
    n = len(xs)
    ns = n if scatter else 0

    def body(*refs):
        srcs = refs[:ns] if scatter else (None,) * n
        lands, sends, recvs = refs[ns:ns + n], refs[ns + n:ns + 2 * n], refs[ns + 2 * n:ns + 3 * n]
        for t in range(n):
            for send, _ in _peer_copies(srcs[t], lands[t], sends[t], recvs[t], scatter):
                send.start()
        token = refs[-1]
        token[...] = jnp.zeros_like(token)

    sems = pltpu.SemaphoreType.DMA((N_DEV - 1,))
    operands = [pltpu.with_memory_space_constraint(x, pltpu.HBM) for x in xs]
    if scatter:
        operands += [pltpu.with_memory_space_constraint(lax.empty(x.shape, x.dtype), pltpu.HBM) for x in xs]
    outs = pl.pallas_call(
        body, name=name,
        out_shape=(sems,) * (2 * n) + tuple(pltpu.HBM(a.shape, a.dtype) for a in operands)
        + (jax.ShapeDtypeStruct((8, 128), F32),),
        in_specs=(HBM_SPEC,) * (ns + n),
        out_specs=(SEM_SPEC,) * (2 * n) + (HBM_SPEC,) * (ns + n) + (pl.BlockSpec(memory_space=pltpu.VMEM),),
        input_output_aliases={i: 2 * n + i for i in range(ns + n)},
        compiler_params=pltpu.CompilerParams(has_side_effects=DATAFLOW),
    )(*operands)
    return (outs[:-1], scatter), outs[-1]


def _exchange_wait(state, after, name):
    held, scatter = state
    n = len(held) // (4 if scatter else 3)
    ns = n if scatter else 0
    sems, thru = held[:2 * n], held[2 * n:]

    def body(*refs):
        srcs = refs[:ns] if scatter else (None,) * n
        lands, sends, recvs = refs[ns:ns + n], refs[ns + n:ns + 2 * n], refs[ns + 2 * n:ns + 3 * n]
        for t in range(n):
            for send, arrival in _peer_copies(srcs[t], lands[t], sends[t], recvs[t], scatter):
                send.wait_send()
                arrival.wait_recv()

    outs = pl.pallas_call(
        body, name=name,
        out_shape=tuple(pltpu.HBM(a.shape, a.dtype) for a in thru),
        in_specs=(HBM_SPEC,) * (ns + n) + (SEM_SPEC,) * (2 * n) + (pl.BlockSpec(memory_space=pl.ANY),),
        out_specs=(HBM_SPEC,) * (ns + n), input_output_aliases={i: i for i in range(ns + n)},
        compiler_params=pltpu.CompilerParams(has_side_effects=DATAFLOW),
    )(*thru, *sems, after)
    return list(outs[ns:]), list(outs[:ns])


def _place_shard(x, layer, me, dtype, name):
    _, r, c = x.shape
    tr = r
    if r * c * 4 > PLACE_BLOCK_BYTES:
        for cand in (512, 256, 128, 64, 32, 16):
            if r % cand == 0 and cand * c * 4 <= PLACE_BLOCK_BYTES:
                tr = cand
                break

    def body(me_ref, x_ref, o_ref):
        o_ref[...] = x_ref[...].astype(o_ref.dtype)

    return pl.pallas_call(
        body, name=name, out_shape=jax.ShapeDtypeStruct((N_DEV, r, c), dtype),
        grid_spec=pltpu.PrefetchScalarGridSpec(
            num_scalar_prefetch=1, grid=(r // tr,),
            in_specs=[pl.BlockSpec((None, tr, c), lambda i, me_ref: (layer, i, 0))],
            out_specs=pl.BlockSpec((None, tr, c), lambda i, me_ref: (me_ref[0], i, 0))),
        compiler_params=_params(),
    )(me.reshape(1), x)


NN = ((1,), (0,))
NT = ((1,), (1,))
TN = ((0,), (0,))


def _matmul(a, b, out_shape, *, grid, a_spec, b_spec, o_spec, dims, nred, name, res=None, res_spec=None,
            out_dtype=F32, slabs=0):
    has_res = res is not None

    def body(*refs):
        a_ref, b_ref = refs[0], refs[1]
        r_ref = refs[2] if has_res else None
        o_ref = refs[3] if has_res else refs[2]
        if slabs:
            part = None
            for n in range(slabs):
                term = lax.dot_general(a_ref[n].astype(BF16), b_ref[n].astype(BF16), (dims, ((), ())),
                                       preferred_element_type=F32)
                part = term if part is None else part + term
        else:
            part = lax.dot_general(a_ref[...].astype(BF16), b_ref[...].astype(BF16), (dims, ((), ())),
                                   preferred_element_type=F32)
        if nred == 1:
            if has_res:
                part = part + r_ref[...]
            o_ref[...] = part.astype(o_ref.dtype)
        else:
            acc = refs[-1]
            r = pl.program_id(2)

            @pl.when(r == 0)
            def _():
                acc[...] = part

            @pl.when(r > 0)
            def _():
                acc[...] += part

            @pl.when(r == nred - 1)
            def _():
                tot = acc[...]
                if has_res:
                    tot = tot + r_ref[...]
                o_ref[...] = tot.astype(o_ref.dtype)

    in_specs = [a_spec, b_spec] + ([res_spec] if has_res else [])
    args = (a, b) + ((res,) if has_res else ())
    acc_shape = tuple(d for d in o_spec.block_shape if d is not None)
    return pl.pallas_call(
        body, name=name, grid=grid, out_shape=jax.ShapeDtypeStruct(out_shape, out_dtype),
        in_specs=in_specs, out_specs=o_spec,
        scratch_shapes=[pltpu.VMEM(acc_shape, F32)] if nred > 1 else [],
        compiler_params=_params(),
    )(*args)


def _mm_nn(a, w, name, res=None, tn=None, out_dtype=F32):
    m, k = a.shape
    n = w.shape[1]
    tn = tn or n
    tm = min(MM_TILE, m)
    ospec = pl.BlockSpec((tm, tn), lambda i, j, r: (i, j))
    return _matmul(a, w, (m, n), grid=(m // tm, n // tn, 1),
                   a_spec=pl.BlockSpec((tm, k), lambda i, j, r: (i, 0)),
                   b_spec=pl.BlockSpec((k, tn), lambda i, j, r: (0, j)),
                   o_spec=ospec, dims=NN, nred=1, name=name, res=res, res_spec=ospec if res is not None else None,
                   out_dtype=out_dtype)


def _mm_nt(a, w, name, out_dtype=F32):
    m, n = a.shape
    k = w.shape[0]
    tm = min(MM_TILE, m)
    return _matmul(a, w, (m, k), grid=(m // tm, 1, 1),
                   a_spec=pl.BlockSpec((tm, n), lambda i, j, r: (i, 0)),
                   b_spec=pl.BlockSpec((k, n), lambda i, j, r: (0, 0)),
                   o_spec=pl.BlockSpec((tm, k), lambda i, j, r: (i, 0)), dims=NT, nred=1, name=name,
                   out_dtype=out_dtype)


def _norm_matmul(h, g, b, out_shape, *, grid, b_spec, o_spec, name, out_dtype=F32, dims=NN):
    s, d = h.shape
    tm = s // grid[0]

    def body(h_ref, g_ref, b_ref, o_ref, xn_ref):
        @pl.when(pl.program_id(1) == 0)
        def _():
            hv = h_ref[...]
            r = lax.rsqrt(jnp.mean(hv * hv, axis=-1, keepdims=True) + RMS_EPS)
            xn_ref[...] = (hv * r * g_ref[...]).astype(xn_ref.dtype)

        o_ref[...] = lax.dot_general(xn_ref[...], b_ref[...].astype(BF16), (dims, ((), ())),
                                     preferred_element_type=F32).astype(o_ref.dtype)

    row = pl.BlockSpec((tm, d), lambda i, j: (i, 0))
    return pl.pallas_call(
        body, name=name, grid=grid,
        out_shape=(jax.ShapeDtypeStruct(out_shape, out_dtype), jax.ShapeDtypeStruct((s, d), BF16)),
        in_specs=[row, pl.BlockSpec((1, d), lambda i, j: (0, 0)), b_spec],
        out_specs=(o_spec, row), compiler_params=_params(),
    )(h, g.reshape(1, d), b)


def _matmul_rms_bwd(a, w, h, g, res, name, tm=ROW_TILE, dims=NT):
    slabs = a.shape[0] if a.ndim == 3 else 0
    s, n = a.shape[-2:]
    d = w.shape[-2] if dims == NT else w.shape[-1]
    tm = min(tm, s)

    def body(a_ref, w_ref, h_ref, g_ref, r_ref, dh_ref, dg_ref):
        if slabs:
            dy = None
            for j in range(slabs):
                term = lax.dot_general(a_ref[j].astype(BF16), w_ref[j].astype(BF16), (dims, ((), ())),
                                       preferred_element_type=F32)
                dy = term if dy is None else dy + term
        else:
            dy = lax.dot_general(a_ref[...].astype(BF16), w_ref[...].astype(BF16), (dims, ((), ())),
                                 preferred_element_type=F32)
        hv = h_ref[...]
        r = lax.rsqrt(jnp.mean(hv * hv, axis=-1, keepdims=True) + RMS_EPS)
        hn = hv * r
        u = dy * g_ref[...]
        dh_ref[...] = r * (u - hn * jnp.mean(u * hn, axis=-1, keepdims=True)) + r_ref[...]
        part = jnp.sum(dy * hn, axis=0, keepdims=True)

        @pl.when(pl.program_id(0) == 0)
        def _():
            dg_ref[...] = part

        @pl.when(pl.program_id(0) > 0)
        def _():
            dg_ref[...] += part

    row = pl.BlockSpec((tm, d), lambda i: (i, 0))
    vec = pl.BlockSpec((1, d), lambda i: (0, 0))
    if slabs:
        a_spec = pl.BlockSpec((slabs, tm, n), lambda i: (0, i, 0))
        w_spec = pl.BlockSpec(w.shape, lambda i: (0, 0, 0))
    else:
        a_spec = pl.BlockSpec((tm, n), lambda i: (i, 0))
        w_spec = pl.BlockSpec(w.shape, lambda i: (0, 0))
    dh, dg = pl.pallas_call(
        body, name=name, grid=(s // tm,),
        out_shape=(jax.ShapeDtypeStruct((s, d), F32), jax.ShapeDtypeStruct((1, d), F32)),
        in_specs=[a_spec, w_spec, row, vec, row], out_specs=(row, vec), compiler_params=_params(),
    )(a, w, h, g.reshape(1, d), res)
    return dh, dg.reshape(d)


GRAD_DTYPE = BF16


def _mm_tn(a, b, name, tk=512, tn=None):
    s, k = a.shape
    n = b.shape[1]
    tn = tn or n
    tk = min(tk, k)
    ts = s if b.dtype == BF16 else max(s // 2, 1)
    return _matmul(a, b, (k, n), grid=(k // tk, n // tn, s // ts),
                   a_spec=pl.BlockSpec((ts, tk), lambda i, j, r: (r, i)),
                   b_spec=pl.BlockSpec((ts, tn), lambda i, j, r: (r, j)),
                   o_spec=pl.BlockSpec((tk, tn), lambda i, j, r: (i, j)), dims=TN, nred=s // ts, name=name,
                   out_dtype=GRAD_DTYPE)


def _rms_fwd(h, g, name):
    s, d = h.shape
    tm = min(ROW_TILE, s)

    def body(h_ref, g_ref, o_ref):
        hv = h_ref[...]
        r = lax.rsqrt(jnp.mean(hv * hv, axis=-1, keepdims=True) + RMS_EPS)
        o_ref[...] = (hv * r * g_ref[...]).astype(o_ref.dtype)

    return pl.pallas_call(
        body, name=name, grid=(s // tm,), out_shape=jax.ShapeDtypeStruct((s, d), BF16),
        in_specs=[pl.BlockSpec((tm, d), lambda i: (i, 0)), pl.BlockSpec((1, d), lambda i: (0, 0))],
        out_specs=pl.BlockSpec((tm, d), lambda i: (i, 0)), compiler_params=_params(),
    )(h, g.reshape(1, d))


def _rms_bwd(dy, h, g, res, name):
    s, d = h.shape
    tm = min(ROW_TILE, s)
    has_res = res is not None

    def body(*refs):
        dy_ref, h_ref, g_ref = refs[:3]
        r_ref = refs[3] if has_res else None
        dh_ref, dg_ref = refs[-2], refs[-1]
        hv = h_ref[...]
        r = lax.rsqrt(jnp.mean(hv * hv, axis=-1, keepdims=True) + RMS_EPS)
        hn = hv * r
        dyv = dy_ref[...].astype(F32)
        u = dyv * g_ref[...]
        dh = r * (u - hn * jnp.mean(u * hn, axis=-1, keepdims=True))
        if has_res:
            dh = dh + r_ref[...]
        dh_ref[...] = dh
        part = jnp.sum(dyv * hn, axis=0, keepdims=True)

        @pl.when(pl.program_id(0) == 0)
        def _():
            dg_ref[...] = part

        @pl.when(pl.program_id(0) > 0)
        def _():
            dg_ref[...] += part

    row = pl.BlockSpec((tm, d), lambda i: (i, 0))
    vec = pl.BlockSpec((1, d), lambda i: (0, 0))
    dh, dg = pl.pallas_call(
        body, name=name, grid=(s // tm,),
        out_shape=(jax.ShapeDtypeStruct((s, d), F32), jax.ShapeDtypeStruct((1, d), F32)),
        in_specs=[row, row, vec] + ([row] if has_res else []),
        out_specs=(row, vec), compiler_params=_params(),
    )(*((dy, h, g.reshape(1, d)) + ((res,) if has_res else ())))
    return dh, dg.reshape(d)


def _loss_head(h, g, target):
    s, d = h.shape
    tm = min(ROW_TILE, s)

    def body(h_ref, g_ref, t_ref, loss_ref, dh_ref, dg_ref):
        hv = h_ref[...]
        r = lax.rsqrt(jnp.mean(hv * hv, axis=-1, keepdims=True) + RMS_EPS)
        hn = hv * r
        gv = g_ref[...]
        err = hn * gv - t_ref[...]
        rows = jnp.mean(err * err, axis=-1, keepdims=True)
        lpart = 0.5 * jnp.sum(rows, axis=0, keepdims=True) + jnp.zeros((1, 128), F32)
        dy = err * (1.0 / d)
        u = dy * gv
        dh_ref[...] = r * (u - hn * jnp.mean(u * hn, axis=-1, keepdims=True))
        gpart = jnp.sum(dy * hn, axis=0, keepdims=True)

        @pl.when(pl.program_id(0) == 0)
        def _():
            dg_ref[...] = gpart
            loss_ref[...] = lpart

        @pl.when(pl.program_id(0) > 0)
        def _():
            dg_ref[...] += gpart
            loss_ref[...] += lpart

    row = pl.BlockSpec((tm, d), lambda i: (i, 0))
    vec = pl.BlockSpec((1, d), lambda i: (0, 0))
    return pl.pallas_call(
        body, name="loss_head", grid=(s // tm,),
        out_shape=(jax.ShapeDtypeStruct((1, 128), F32), jax.ShapeDtypeStruct((s, d), F32),
                   jax.ShapeDtypeStruct((1, d), F32)),
        in_specs=[row, vec, row],
        out_specs=(pl.BlockSpec((1, 128), lambda i: (0, 0)), row, vec), compiler_params=_params(),
    )(h, g.reshape(1, d), target)


def _shift_down(x, k):
    return pltpu.roll(x, k, 0)


def _shift_up(x, k):
    return pltpu.roll(x, x.shape[0] - k, 0)


def _conv3(x, w):
    return w[2:3, :] * x + w[1:2, :] * _shift_down(x, 1) + w[0:1, :] * _shift_down(x, 2)


def _conv3_t(x, w):
    return w[2:3, :] * x + w[1:2, :] * _shift_up(x, 1) + w[0:1, :] * _shift_up(x, 2)


def _sigmoid(x):
    return 1.0 / (1.0 + jnp.exp(-x))


def _prev_map(tile, halo, col):
    return lambda i: (jnp.maximum(i * (tile // halo) - 1, 0), col)


def _next_map(tile, halo, col, nrows):
    return lambda i: (jnp.minimum((i + 1) * (tile // halo), nrows // halo - 1), col)


def _sconv_fwd(proj, w):
    s = proj.shape[0]
    t = min(ROW_TILE, s)

    def body(cur_ref, prev_ref, w_ref, o_ref):
        i = pl.program_id(0)
        prev = prev_ref[...] * (i > 0).astype(F32)
        ext = jnp.concatenate([prev, cur_ref[...]], axis=0)
        sv = ext[:, 2 * GROUP:3 * GROUP] * ext[:, 0:GROUP]
        y = ext[:, GROUP:2 * GROUP] * _conv3(sv, w_ref[...])
        o_ref[...] = y[8:].astype(o_ref.dtype)

    return pl.pallas_call(
        body, name="sconv_fwd", grid=(s // t,), out_shape=jax.ShapeDtypeStruct((s, 4 * GROUP), BF16),
        in_specs=[pl.BlockSpec((t, 3 * GROUP), lambda i: (i, 0)),
                  pl.BlockSpec((8, 3 * GROUP), _prev_map(t, 8, 0)),
                  pl.BlockSpec((3, GROUP), lambda i: (0, 0))],
        out_specs=pl.BlockSpec((t, GROUP), lambda i: (i, 0)), compiler_params=_params(),
    )(proj, proj, w)


def _sconv_bwd(proj, w, dy):
    s = proj.shape[0]
    t = min(ROW_TILE, s)
    nt = s // t

    def body(cur_ref, prev_ref, next_ref, w_ref, dy_ref, dyn_ref, dp_ref, dw_ref):
        i = pl.program_id(0)
        first = (i > 0).astype(F32)
        last = (i < nt - 1).astype(F32)
        ext = jnp.concatenate([prev_ref[...] * first, cur_ref[...], next_ref[...] * last], axis=0)
        dye = jnp.concatenate([jnp.zeros((8, GROUP), F32), dy_ref[...], dyn_ref[...] * last], axis=0)
        hv, bv, cv = ext[:, 0:GROUP], ext[:, GROUP:2 * GROUP], ext[:, 2 * GROUP:3 * GROUP]
        wv = w_ref[...]
        sv = cv * hv
        conv = _conv3(sv, wv)
        dconv = dye * bv
        ds = _conv3_t(dconv, wv)
        dp = jnp.concatenate([ds * cv, dye * conv, ds * hv], axis=1)
        dp_ref[...] = dp[8:8 + t].astype(dp_ref.dtype)
        dc = dconv[8:8 + t]
        dw = jnp.concatenate([
            jnp.sum(dc * _shift_down(sv, 2)[8:8 + t], axis=0, keepdims=True),
            jnp.sum(dc * _shift_down(sv, 1)[8:8 + t], axis=0, keepdims=True),
            jnp.sum(dc * sv[8:8 + t], axis=0, keepdims=True),
            jnp.zeros((5, GROUP), F32)], axis=0)

        @pl.when(i == 0)
        def _():
            dw_ref[...] = dw

        @pl.when(i > 0)
        def _():
            dw_ref[...] += dw

    dp, dw = pl.pallas_call(
        body, name="sconv_bwd", grid=(nt,),
        out_shape=(jax.ShapeDtypeStruct((s, N_IN_PAD), BF16), jax.ShapeDtypeStruct((8, GROUP), F32)),
        in_specs=[pl.BlockSpec((t, 3 * GROUP), lambda i: (i, 0)),
                  pl.BlockSpec((8, 3 * GROUP), _prev_map(t, 8, 0)),
                  pl.BlockSpec((8, 3 * GROUP), _next_map(t, 8, 0, s)),
                  pl.BlockSpec((3, GROUP), lambda i: (0, 0)),
                  pl.BlockSpec((t, GROUP), lambda i: (i, 0)),
                  pl.BlockSpec((8, GROUP), _next_map(t, 8, 0, s))],
        out_specs=(pl.BlockSpec((t, 3 * GROUP), lambda i: (i, 0)), pl.BlockSpec((8, GROUP), lambda i: (0, 0))),
        compiler_params=_params(),
    )(proj, proj, proj, w, dy, dy)
    return dp, dw[:3]


def _lane_window(shape):
    lane = lax.broadcasted_iota(jnp.int32, shape, 1)
    return lane, jnp.where(lane < 64, 2.0, jnp.where(lane < 128, 4.0, jnp.where(lane < 192, 8.0, 16.0)))


def _by_group(lane, s1, s2, s3, s4):
    return jnp.where(lane < 64, s1, jnp.where(lane < 128, s2, jnp.where(lane < 192, s3, s4)))


def _pool_z(ext, row0):
    s1 = ext + _shift_down(ext, 1)
    s2 = s1 + _shift_down(s1, 2)
    s3 = s2 + _shift_down(s2, 4)
    s4 = s3 + _shift_down(s3, 8)
    lane, win = _lane_window(ext.shape)
    tpos = (lax.broadcasted_iota(jnp.int32, ext.shape, 0) + (row0 - 16 + 1)).astype(F32)
    cnt = jnp.maximum(jnp.minimum(tpos, win), 1.0)
    return _by_group(lane, s1, s2, s3, s4) / cnt - ext


ANY_SPEC = pl.BlockSpec(memory_space=pl.ANY)


def _pool_fwd(proj, wbd, scale, ybuf):
    s = proj.shape[0]
    t = min(ROW_TILE, s)
    col = (COL_GATE - GROUP) // GROUP

    def body(cur_ref, prev_ref, w_ref, sc_ref, buf_ref, o_ref):
        i = pl.program_id(0)
        ext = jnp.concatenate([prev_ref[...] * (i > 0).astype(F32), cur_ref[...]], axis=0)
        z = _pool_z(ext, i * t)[16:]
        y = jnp.dot(z.astype(BF16), w_ref[...].astype(BF16), preferred_element_type=F32)
        o_ref[...] = (y * sc_ref[...]).astype(o_ref.dtype)

    return pl.pallas_call(
        body, name="pool_fwd", grid=(s // t,), out_shape=jax.ShapeDtypeStruct(ybuf.shape, ybuf.dtype),
        in_specs=[pl.BlockSpec((t, GROUP), lambda i: (i, col)),
                  pl.BlockSpec((16, GROUP), _prev_map(t, 16, col)),
                  pl.BlockSpec((GROUP, GROUP), lambda i: (0, 0)),
                  pl.BlockSpec((1, GROUP), lambda i: (0, 0)), ANY_SPEC],
        out_specs=pl.BlockSpec((t, GROUP), lambda i: (i, 3)), input_output_aliases={4: 0},
        compiler_params=_params(),
    )(proj, proj, wbd, scale.reshape(1, GROUP), ybuf)


def _pool_bwd(proj, wbd, scale, dy, dbuf):
    s = proj.shape[0]
    t = min(ROW_TILE, s)
    nt = s // t
    col = (COL_GATE - GROUP) // GROUP

    def body(cur_ref, prev_ref, w_ref, sc_ref, dy_ref, dyn_ref, buf_ref, dp_ref, dw_ref, dsc_ref):
        i = pl.program_id(0)
        ext = jnp.concatenate([prev_ref[...] * (i > 0).astype(F32), cur_ref[...]], axis=0)
        z = _pool_z(ext, i * t)[16:]
        wv = w_ref[...].astype(BF16)
        dyc = dy_ref[...]
        dye = jnp.concatenate([dyc, dyn_ref[...] * (i < nt - 1).astype(F32)], axis=0) * sc_ref[...]
        dz = lax.dot_general(dye.astype(BF16), wv, (NT, ((), ())), preferred_element_type=F32)
        lane, win = _lane_window(dz.shape)
        tpos = (lax.broadcasted_iota(jnp.int32, dz.shape, 0) + (i * t + 1)).astype(F32)
        e = dz / jnp.minimum(tpos, win)
        f1 = e + _shift_up(e, 1)
        f2 = f1 + _shift_up(f1, 2)
        f3 = f2 + _shift_up(f2, 4)
        f4 = f3 + _shift_up(f3, 8)
        dp = _by_group(lane, f1, f2, f3, f4) - dz
        dp_ref[...] = dp[:t].astype(dp_ref.dtype)
        zb = z.astype(BF16)
        y = jnp.dot(zb, wv, preferred_element_type=F32)
        dsc = jnp.sum(dyc * y, axis=0, keepdims=True)
        dw = lax.dot_general(zb, dye[:t].astype(BF16), (TN, ((), ())), preferred_element_type=F32)

        @pl.when(i == 0)
        def _():
            dw_ref[...] = dw
            dsc_ref[...] = dsc

        @pl.when(i > 0)
        def _():
            dw_ref[...] += dw
            dsc_ref[...] += dsc

    dp, dw, dsc = pl.pallas_call(
        body, name="pool_bwd", grid=(nt,),
        out_shape=(jax.ShapeDtypeStruct(dbuf.shape, dbuf.dtype), jax.ShapeDtypeStruct((GROUP, GROUP), F32),
                   jax.ShapeDtypeStruct((1, GROUP), F32)),
        in_specs=[pl.BlockSpec((t, GROUP), lambda i: (i, col)),
                  pl.BlockSpec((16, GROUP), _prev_map(t, 16, col)),
                  pl.BlockSpec((GROUP, GROUP), lambda i: (0, 0)),
                  pl.BlockSpec((1, GROUP), lambda i: (0, 0)),
                  pl.BlockSpec((t, GROUP), lambda i: (i, 3)),
                  pl.BlockSpec((16, GROUP), _next_map(t, 16, 3, s)), ANY_SPEC],
        out_specs=(pl.BlockSpec((t, GROUP), lambda i: (i, col)), pl.BlockSpec((GROUP, GROUP), lambda i: (0, 0)),
                   pl.BlockSpec((1, GROUP), lambda i: (0, 0))),
        input_output_aliases={6: 0}, compiler_params=_params(),
    )(proj, proj, wbd, scale.reshape(1, GROUP), dy, dy, dbuf)
    return dp, dw, dsc.reshape(GROUP)


FF_HALO = 16


def _ffn_gate_fwd(u0, w):
    s = u0.shape[1]
    t = min(ROW_TILE, s)

    def body(a_ref, ap_ref, g_ref, gp_ref, wa_ref, wg_ref, o_ref):
        first = (pl.program_id(1) > 0).astype(F32)
        a = _conv3(jnp.concatenate([ap_ref[...] * first, a_ref[...].astype(F32)], axis=0), wa_ref[...])[FF_HALO:]
        g = _conv3(jnp.concatenate([gp_ref[...] * first, g_ref[...].astype(F32)], axis=0), wg_ref[...])[FF_HALO:]
        o_ref[...] = (a * (g * _sigmoid(g))).astype(o_ref.dtype)

    def cur(off):
        return pl.BlockSpec((None, t, FF_SHARD), lambda j, i: (j + off, i, 0))

    def prev(off):
        return pl.BlockSpec((None, FF_HALO, FF_SHARD),
                            lambda j, i: (j + off, jnp.maximum(i * (t // FF_HALO) - 1, 0), 0))

    def wspec(off):
        return pl.BlockSpec((None, 3, FF_SHARD), lambda j, i: (j + off, 0, 0))

    return pl.pallas_call(
        body, name="ffn_gate_fwd", grid=(FF_HALF, s // t),
        out_shape=jax.ShapeDtypeStruct((FF_HALF, s, FF_SHARD), BF16),
        in_specs=[cur(0), prev(0), cur(FF_HALF), prev(FF_HALF), wspec(0), wspec(FF_HALF)],
        out_specs=pl.BlockSpec((None, t, FF_SHARD), lambda j, i: (j, i, 0)), compiler_params=_params(),
    )(u0, u0, u0, u0, w, w)


def _ffn_gate_bwd(u0, w, dact):
    s = u0.shape[1]
    t = min(ROW_TILE, s)
    nt = s // t

    def body(c_ref, p_ref, n_ref, w_ref, d_ref, dn_ref, du_ref, dw_ref):
        i = pl.program_id(1)
        first = (i > 0).astype(F32)
        last = (i < nt - 1).astype(F32)
        dext = jnp.concatenate([jnp.zeros((FF_HALO, FF_SHARD), F32), d_ref[...].astype(F32), dn_ref[...] * last],
                               axis=0)
        ext = [jnp.concatenate([p_ref[n] * first, c_ref[n].astype(F32), n_ref[n] * last], axis=0) for n in range(2)]
        a = _conv3(ext[0], w_ref[0])
        g = _conv3(ext[1], w_ref[1])
        sg = _sigmoid(g)
        silu = g * sg
        dus = (dext * silu, dext * a * (sg + silu * (1.0 - sg)))
        mine = slice(FF_HALO, FF_HALO + t)
        for n in range(2):
            du_ref[n] = _conv3_t(dus[n], w_ref[n])[mine].astype(du_ref.dtype)
            dc = dus[n][mine]
            dw = jnp.concatenate([
                jnp.sum(dc * _shift_down(ext[n], 2)[mine], axis=0, keepdims=True),
                jnp.sum(dc * _shift_down(ext[n], 1)[mine], axis=0, keepdims=True),
                jnp.sum(dc * ext[n][mine], axis=0, keepdims=True),
                jnp.zeros((5, FF_SHARD), F32)], axis=0)

            @pl.when(i == 0)
            def _(n=n, dw=dw):
                dw_ref[n] = dw

            @pl.when(i > 0)
            def _(n=n, dw=dw):
                dw_ref[n] += dw

    def pair(rows, row_map):
        return pl.BlockSpec((2, None, rows, FF_SHARD), lambda j, i: (0, j, row_map(i), 0))

    prev_row = lambda i: jnp.maximum(i * (t // FF_HALO) - 1, 0)
    next_row = lambda i: jnp.minimum((i + 1) * (t // FF_HALO), s // FF_HALO - 1)
    u2 = u0.reshape(2, FF_HALF, s, FF_SHARD)
    du, dw = pl.pallas_call(
        body, name="ffn_gate_bwd", grid=(FF_HALF, nt),
        out_shape=(jax.ShapeDtypeStruct((2, FF_HALF, s, FF_SHARD), BF16),
                   jax.ShapeDtypeStruct((2, FF_HALF, 8, FF_SHARD), F32)),
        in_specs=[pair(t, lambda i: i), pair(FF_HALO, prev_row), pair(FF_HALO, next_row), pair(3, lambda i: 0),
                  pl.BlockSpec((None, t, FF_SHARD), lambda j, i: (j, i, 0)),
                  pl.BlockSpec((None, FF_HALO, FF_SHARD), lambda j, i: (j, next_row(i), 0))],
        out_specs=(pair(t, lambda i: i), pair(8, lambda i: 0)),
        compiler_params=_params(),
    )(u2, u2, u2, w.reshape(2, FF_HALF, 3, FF_SHARD), dact, dact)
    return du.reshape(2 * FF_HALF, s, FF_SHARD), dw.reshape(2 * FF_HALF, 8, FF_SHARD)[:, :3]


def _rope_tables(positions):
    inv_freq = ROPE_THETA ** (-jnp.arange(0, ROPE_DIM, 2, dtype=F32) / ROPE_DIM)
    ang = positions.astype(F32)[:, None] * inv_freq
    cos, sin = jnp.cos(ang), jnp.sin(ang)
    s = positions.shape[0]
    half = ROPE_DIM // 2
    rest = HEAD_DIM - ROPE_DIM
    ca = jnp.concatenate([cos, cos, jnp.ones((s, rest), F32)], axis=1)
    cb = jnp.concatenate([-sin, jnp.zeros((s, HEAD_DIM - half), F32)], axis=1)
    cc = jnp.concatenate([jnp.zeros((s, half), F32), sin, jnp.zeros((s, rest), F32)], axis=1)
    return tuple(jnp.tile(tb, (1, N_HEADS)) for tb in (ca, cb, cc))


QK_WIDE = 128
LANE_CQ, LANE_CK = 64, 67
KT_ROWS = 80


def _three_bf16(x):
    hi = x.astype(BF16).astype(F32)
    mid = (x - hi).astype(BF16).astype(F32)
    lo = (x - hi - mid).astype(BF16).astype(F32)
    return hi, mid, lo


def _heads_split(proj, col, tables, c, name):
    s = proj.shape[0]
    t = min(ROW_TILE, s)
    rope = tables is not None
    wide = c is not None
    width = QK_WIDE if wide else HEAD_DIM

    def body(*refs):
        x_ref = refs[0]
        q_ref, k_ref, v_ref, kt_ref, vt_ref = refs[-5:]
        xv = x_ref[...]
        parts = [xv[:, 0:GROUP], xv[:, GROUP:2 * GROUP], xv[:, 2 * GROUP:3 * GROUP]]
        if rope:
            ca, cb, cc = refs[1][...], refs[2][...], refs[3][...]
            for n in range(2):
                p = parts[n]
                parts[n] = p * ca + pltpu.roll(p, GROUP - 8, 1) * cb + pltpu.roll(p, 8, 1) * cc
        parts[0] = parts[0] * (HEAD_DIM ** -0.5)
        k_t, v_t = parts[1].T, parts[2].T
        ones_row = jnp.where(lax.broadcasted_iota(jnp.int32, (KT_ROWS - HEAD_DIM, t), 0) == 0, 1.0, 0.0)
        lane = lax.broadcasted_iota(jnp.int32, (t, QK_WIDE), 1)
        zeros = jnp.zeros((t, QK_WIDE - HEAD_DIM), F32)
        for h in range(N_HEADS):
            hs = slice(h * HEAD_DIM, (h + 1) * HEAD_DIM)
            qh, kh = parts[0][:, hs], parts[1][:, hs]
            if wide:
                terms = _three_bf16(refs[-6][:, h:h + 1])
                qh = jnp.concatenate([qh, zeros], axis=1)
                kh = jnp.concatenate([kh, zeros], axis=1)
                for n in range(3):
                    qh = jnp.where(lane == LANE_CQ + n, terms[n], jnp.where(lane == LANE_CK + n, 1.0, qh))
                    kh = jnp.where(lane == LANE_CK + n, -terms[n], jnp.where(lane == LANE_CQ + n, 1.0, kh))
            q_ref[h] = qh.astype(q_ref.dtype)
            k_ref[h] = kh.astype(k_ref.dtype)
            v_ref[h] = parts[2][:, hs].astype(v_ref.dtype)
            kt_ref[h] = jnp.concatenate([k_t[hs, :], ones_row], axis=0).astype(kt_ref.dtype)
            vt_ref[h] = v_t[hs, :].astype(vt_ref.dtype)

    tab = pl.BlockSpec((t, GROUP), lambda i: (i, 0))
    qk = pl.BlockSpec((N_HEADS, t, width), lambda i: (0, i, 0))
    heads = pl.BlockSpec((N_HEADS, t, HEAD_DIM), lambda i: (0, i, 0))
    heads_t = pl.BlockSpec((N_HEADS, HEAD_DIM, t), lambda i: (0, 0, i))
    qk_shape = jax.ShapeDtypeStruct((N_HEADS, s, width), BF16)
    return pl.pallas_call(
        body, name=name, grid=(s // t,),
        out_shape=(qk_shape, qk_shape, jax.ShapeDtypeStruct((N_HEADS, s, HEAD_DIM), BF16),
                   jax.ShapeDtypeStruct((N_HEADS, KT_ROWS, s), BF16),
                   jax.ShapeDtypeStruct((N_HEADS, HEAD_DIM, s), BF16)),
        in_specs=[pl.BlockSpec((t, 3 * GROUP), lambda i: (i, col))] + ([tab, tab, tab] if rope else [])
        + ([pl.BlockSpec((t, 128), lambda i: (i, 0))] if wide else []),
        out_specs=(qk, qk, heads, pl.BlockSpec((N_HEADS, KT_ROWS, t), lambda i: (0, 0, i)), heads_t),
        compiler_params=_params(),
    )(*((proj,) + (tuple(tables) if rope else ()) + ((c,) if wide else ())))


def _heads_merge(dqt, dk, dv, tables, name, dbuf, col):
    s = dv.shape[1]
    t = min(ROW_TILE, s)
    rope = tables is not None

    wide = dk.shape[2] == QK_WIDE

    def body(*refs):
        o_ref = refs[n_in + 1]
        dq = jnp.concatenate([refs[0][h, :HEAD_DIM, :] for h in range(N_HEADS)], axis=0).T
        parts = [dq] + [jnp.concatenate([r[h][:, :HEAD_DIM] for h in range(N_HEADS)], axis=1) for r in refs[1:3]]
        parts[0] = parts[0] * (HEAD_DIM ** -0.5)
        if rope:
            ca, cb, cc = refs[3][...], refs[4][...], refs[5][...]
            for n in range(2):
                p = parts[n]
                parts[n] = p * ca + pltpu.roll(p * cb, 8, 1) + pltpu.roll(p * cc, GROUP - 8, 1)
        o_ref[...] = jnp.concatenate(parts, axis=1).astype(o_ref.dtype)
        if wide:
            over_keys = jnp.concatenate([refs[0][h, HEAD_DIM:HEAD_DIM + 8, :] for h in range(N_HEADS)]
                                        + [jnp.zeros((128 - 8 * N_HEADS, t), F32)], axis=0).T
            lane = lax.broadcasted_iota(jnp.int32, (t, 128), 1)
            dc = jnp.zeros((t, 128), F32)
            for h in range(N_HEADS):
                dc = jnp.where(lane == h, over_keys[:, 8 * h:8 * h + 1] - refs[1][h][:, LANE_CK:LANE_CK + 1], dc)
            refs[n_in + 2][...] = dc

    tab = pl.BlockSpec((t, GROUP), lambda i: (i, 0))
    heads = pl.BlockSpec((N_HEADS, t, HEAD_DIM), lambda i: (0, i, 0))
    n_in = 6 if rope else 3
    dspec = pl.BlockSpec((t, 3 * GROUP), lambda i: (i, col))
    dshape = jax.ShapeDtypeStruct(dbuf.shape, dbuf.dtype)
    return pl.pallas_call(
        body, name=name, grid=(s // t,),
        out_shape=(dshape, jax.ShapeDtypeStruct((s, 128), F32)) if wide else dshape,
        in_specs=[pl.BlockSpec((N_HEADS, KT_ROWS, t), lambda i: (0, 0, i)),
                  pl.BlockSpec((N_HEADS, t, dk.shape[2]), lambda i: (0, i, 0)), heads]
        + ([tab, tab, tab] if rope else []) + [ANY_SPEC],
        out_specs=(dspec, pl.BlockSpec((t, 128), lambda i: (i, 0))) if wide else dspec,
        input_output_aliases={n_in: 0}, compiler_params=_params(),
    )(*((dqt, dk, dv) + (tuple(tables) if rope else ()) + (dbuf,)))


def _log_sigmoid(x):
    return jnp.minimum(x, 0.0) - jnp.log(1.0 + jnp.exp(-jnp.abs(x)))


def _scan_rows(x, reverse):
    n = x.shape[0]
    row = lax.broadcasted_iota(jnp.int32, x.shape, 0)
    k = 1
    while k < n:
        if reverse:
            x = x + jnp.where(row < n - k, _shift_up(x, k), 0.0)
        else:
            x = x + jnp.where(row >= k, _shift_down(x, k), 0.0)
        k *= 2
    return x


def _gate_cumsum(proj, bias):
    s = proj.shape[0]
    col = COL_GATE // 128

    def body(z_ref, b_ref, c_ref):
        c_ref[...] = _scan_rows(_log_sigmoid(z_ref[...] + b_ref[...]), False)

    return pl.pallas_call(
        body, name="gate_cumsum", grid=(1,), out_shape=jax.ShapeDtypeStruct((s, 128), F32),
        in_specs=[pl.BlockSpec((s, 128), lambda i: (0, col)), pl.BlockSpec((1, 128), lambda i: (0, 0))],
        out_specs=pl.BlockSpec((s, 128), lambda i: (0, 0)), compiler_params=_params(),
    )(proj, bias)


def _gate_cumsum_bwd(proj, bias, dc, dbuf):
    s = proj.shape[0]
    col = COL_GATE // 128

    def body(z_ref, b_ref, dc_ref, buf_ref, dz_ref, db_ref):
        dlogf = _scan_rows(dc_ref[...], True)
        dz = dlogf * _sigmoid(-(z_ref[...] + b_ref[...]))
        dz_ref[...] = dz.astype(dz_ref.dtype)
        db_ref[...] = jnp.sum(dz, axis=0, keepdims=True)

    return pl.pallas_call(
        body, name="gate_cumsum_bwd", grid=(1,),
        out_shape=(jax.ShapeDtypeStruct(dbuf.shape, dbuf.dtype), jax.ShapeDtypeStruct((1, 128), F32)),
        in_specs=[pl.BlockSpec((s, 128), lambda i: (0, col)), pl.BlockSpec((1, 128), lambda i: (0, 0)),
                  pl.BlockSpec((s, 128), lambda i: (0, 0)), ANY_SPEC],
        out_specs=(pl.BlockSpec((s, 128), lambda i: (0, col)), pl.BlockSpec((1, 128), lambda i: (0, 0))),
        input_output_aliases={3: 0}, compiler_params=_params(),
    )(proj, bias, dc, dbuf)


DIL_REACH = 2048


def _pair_weight(mode, d):
    if mode == "fox":
        return jnp.where(d >= 0, 1.0, 0.0)
    w1 = jnp.where(jnp.abs(d - 64) <= 64, 1.0, 0.0)
    w2 = jnp.where((d & 3) == 0, jnp.where(jnp.abs(d - 256) <= 256, 1.0, 0.0), 0.0)
    w3 = jnp.where((d & 15) == 0, jnp.where(jnp.abs(d - 1024) <= 1024, 1.0, 0.0), 0.0)
    return w1 + w2 + w3


def _bias_tables(mode, tq, tk):
    nb = 2 if mode == "fox" else DIL_REACH // tk + 1
    n = lax.broadcasted_iota(jnp.int32, (nb, tk, tq), 0)
    key = lax.broadcasted_iota(jnp.int32, (nb, tk, tq), 1)
    query = lax.broadcasted_iota(jnp.int32, (nb, tk, tq), 2)
    w = _pair_weight(mode, n * tk + query - key)
    return jnp.where(w > 0.0, jnp.log(jnp.maximum(w, 1.0)), NEG)


M_INIT = -1e29


def _first_key_chunk(mode, q0, tk):
    if mode == "fox":
        return 0
    return jnp.maximum(q0 - DIL_REACH, 0) // tk


def _attention_fwd(mode, q, k, vt, tab_t, ybuf, col):
    s, width = q.shape[1], q.shape[2]
    tq = min(ATT_TQ, s)
    tk = tq
    nb = tab_t.shape[0]

    def body(q_ref, k_ref, vt_ref, tab_ref, buf_ref, y_ref, o_ref, lse_ref):
        i = pl.program_id(0)
        lo = _first_key_chunk(mode, i * tq, tk)

        def step(c, carry):
            k0 = pl.multiple_of(c * tk, tk)
            tab = tab_ref[jnp.minimum(i - c, nb - 1)]
            scores = [lax.dot_general(k_ref[h, pl.ds(k0, tk), :], q_ref[h], (NT, ((), ())),
                                      preferred_element_type=F32) for h in range(N_HEADS)]
            stats, probs = [], []
            for h in range(N_HEADS):
                m, l = carry[3 * h:3 * h + 2]
                sc = scores[h] + tab
                m_new = jnp.maximum(m, jnp.max(sc, axis=0, keepdims=True))
                alpha = jnp.exp(m - m_new)
                p = jnp.exp(sc - m_new)
                stats.append((m_new, alpha * l + jnp.sum(p, axis=0, keepdims=True), alpha))
                probs.append(p.astype(BF16))
            pv = [jnp.dot(vt_ref[h, :, pl.ds(k0, tk)], probs[h], preferred_element_type=F32) for h in range(N_HEADS)]
            new = []
            for h in range(N_HEADS):
                m_new, l, alpha = stats[h]
                new += [m_new, l, alpha * carry[3 * h + 2] + pv[h]]
            return tuple(new)

        start = (jnp.full((1, tq), M_INIT, F32), jnp.zeros((1, tq), F32), jnp.zeros((HEAD_DIM, tq), F32))
        done = lax.fori_loop(lo, i + 1, step, start * N_HEADS)
        outs = []
        for h in range(N_HEADS):
            m, l, acc = done[3 * h:3 * h + 3]
            outs.append(acc / l)
            lse_ref[h] = m + jnp.log(l)
        out = jnp.concatenate(outs, axis=0).T
        y_ref[...] = out.astype(y_ref.dtype)
        o_ref[...] = out

    rowspec = pl.BlockSpec((N_HEADS, 1, tq), lambda i: (0, 0, i))
    return pl.pallas_call(
        body, name="attention_fwd_" + mode, grid=(s // tq,),
        out_shape=(jax.ShapeDtypeStruct(ybuf.shape, ybuf.dtype), jax.ShapeDtypeStruct((s, GROUP), F32),
                   jax.ShapeDtypeStruct((N_HEADS, 1, s), F32)),
        in_specs=[pl.BlockSpec((N_HEADS, tq, width), lambda i: (0, i, 0)),
                  pl.BlockSpec((N_HEADS, s, width), lambda i: (0, 0, 0)),
                  pl.BlockSpec((N_HEADS, HEAD_DIM, s), lambda i: (0, 0, 0)),
                  pl.BlockSpec((nb, tk, tq), lambda i: (0, 0, 0)), ANY_SPEC],
        out_specs=(pl.BlockSpec((tq, GROUP), lambda i: (i, col)), pl.BlockSpec((tq, GROUP), lambda i: (i, 0)),
                   rowspec),
        input_output_aliases={4: 0}, compiler_params=_params(),
    )(q, k, vt, tab_t, ybuf)


def _attention_delta(o, do, col):
    s = o.shape[0]
    t = min(ROW_TILE, s)

    def body(o_ref, do_ref, delta_ref, dob_ref):
        dov = do_ref[...]
        prod_t = (o_ref[...] * dov).T
        for h in range(N_HEADS):
            hs = slice(h * HEAD_DIM, (h + 1) * HEAD_DIM)
            delta_ref[h] = jnp.sum(prod_t[hs, :], axis=0, keepdims=True)
            dob_ref[h] = dov[:, hs].astype(dob_ref.dtype)

    return pl.pallas_call(
        body, name="attention_delta", grid=(s // t,),
        out_shape=(jax.ShapeDtypeStruct((N_HEADS, 1, s), F32), jax.ShapeDtypeStruct((N_HEADS, s, HEAD_DIM), BF16)),
        in_specs=[pl.BlockSpec((t, GROUP), lambda i: (i, 0)), pl.BlockSpec((t, GROUP), lambda i: (i, col))],
        out_specs=(pl.BlockSpec((N_HEADS, 1, t), lambda i: (0, 0, i)),
                   pl.BlockSpec((N_HEADS, t, HEAD_DIM), lambda i: (0, i, 0))),
        compiler_params=_params(),
    )(o, do)


def _attention_bwd(mode, q, k, v, kt, tab_t, dob, lse, delta):
    s, width = q.shape[1], q.shape[2]
    tq = min(ATT_TQ, s)
    tk = tq
    nq = s // tq
    nb = tab_t.shape[0]

    def body(q_ref, k_ref, v_ref, kt_ref, tab_ref, dob_ref, lse_ref, delta_ref, dqt_ref, dk_ref, dv_ref):
        i = pl.program_id(0)

        @pl.when(i == 0)
        def _():
            dqt_ref[...] = jnp.zeros_like(dqt_ref)

        hi = nq if mode == "fox" else jnp.minimum((i * tk + tk - 1 + DIL_REACH) // tq + 1, nq)
        for h0 in range(0, N_HEADS, BWD_HEADS):
            heads = range(h0, h0 + BWD_HEADS)

            def step(c, carry, heads=heads):
                q0 = pl.multiple_of(c * tq, tq)
                qs = pl.ds(q0, tq)
                tab = tab_ref[jnp.minimum(c - i, nb - 1)]
                qv = [q_ref[h, qs, :] for h in heads]
                dov = [dob_ref[h, qs, :] for h in heads]
                sc = [lax.dot_general(k_ref[h], qv[n], (NT, ((), ())), preferred_element_type=F32)
                      for n, h in enumerate(heads)]
                dp = [lax.dot_general(v_ref[h], dov[n], (NT, ((), ())), preferred_element_type=F32)
                      for n, h in enumerate(heads)]
                pb, dsb = [], []
                for n, h in enumerate(heads):
                    p = jnp.exp(sc[n] + tab - lse_ref[h, :, qs])
                    pb.append(p.astype(BF16))
                    dsb.append((p * (dp[n] - delta_ref[h, :, qs])).astype(BF16))
                new = []
                for n, h in enumerate(heads):
                    new += [carry[2 * n] + jnp.dot(dsb[n], qv[n], preferred_element_type=F32),
                            carry[2 * n + 1] + jnp.dot(pb[n], dov[n], preferred_element_type=F32)]
                for n, h in enumerate(heads):
                    dqt_ref[h, :, qs] += jnp.dot(kt_ref[h], dsb[n], preferred_element_type=F32)
                return tuple(new)

            start = (jnp.zeros((tk, width), F32), jnp.zeros((tk, HEAD_DIM), F32))
            done = lax.fori_loop(i, hi, step, start * BWD_HEADS)
            for n, h in enumerate(heads):
                dk_ref[h] = done[2 * n]
                dv_ref[h] = done[2 * n + 1]

    def full(shape):
        return pl.BlockSpec(shape, lambda i: (0, 0, 0))

    kblk = pl.BlockSpec((N_HEADS, tk, width), lambda i: (0, i, 0))
    vblk = pl.BlockSpec((N_HEADS, tk, HEAD_DIM), lambda i: (0, i, 0))
    return pl.pallas_call(
        body, name="attention_bwd_" + mode, grid=(s // tk,),
        out_shape=(jax.ShapeDtypeStruct((N_HEADS, KT_ROWS, s), F32), jax.ShapeDtypeStruct((N_HEADS, s, width), F32),
                   jax.ShapeDtypeStruct((N_HEADS, s, HEAD_DIM), F32)),
        in_specs=[full((N_HEADS, s, width)), kblk, vblk, pl.BlockSpec((N_HEADS, KT_ROWS, tk), lambda i: (0, 0, i)),
                  full((nb, tk, tq)), full((N_HEADS, s, HEAD_DIM)), full((N_HEADS, 1, s)), full((N_HEADS, 1, s))],
        out_specs=(full((N_HEADS, KT_ROWS, s)), kblk, vblk),
        compiler_params=_params(),
    )(q, k, v, kt, tab_t, dob, lse, delta)


def _xattn_fwd(qx, kvm):
    s = qx.shape[0]
    t = min(ROW_TILE, s)

    def body(q_ref, kv_ref, o_ref):
        heads = range(XA_HEADS)
        sc = [lax.dot_general(q_ref[:, h * XA_DIM:(h + 1) * XA_DIM].astype(BF16), kv_ref[h].astype(BF16),
                              (NT, ((), ())), preferred_element_type=F32) * (XA_DIM ** -0.5) for h in heads]
        probs = []
        for h in heads:
            e = jnp.exp(sc[h] - jnp.max(sc[h], axis=-1, keepdims=True))
            probs.append((e / jnp.sum(e, axis=-1, keepdims=True)).astype(BF16))
        outs = [jnp.dot(probs[h], kv_ref[XA_HEADS + h].astype(BF16), preferred_element_type=F32) for h in heads]
        for h in heads:
            o_ref[:, h * XA_DIM:(h + 1) * XA_DIM] = outs[h].astype(o_ref.dtype)

    return pl.pallas_call(
        body, name="xattn_fwd", grid=(s // t,), out_shape=jax.ShapeDtypeStruct((s, D_MODEL), BF16),
        in_specs=[pl.BlockSpec((t, D_MODEL), lambda i: (i, 0)),
                  pl.BlockSpec((2 * XA_HEADS, MEM_LEN, XA_DIM), lambda i: (0, 0, 0))],
        out_specs=pl.BlockSpec((t, D_MODEL), lambda i: (i, 0)), compiler_params=_params(),
    )(qx, kvm)


def _xattn_bwd(qx, kvm, do):
    s = qx.shape[0]
    t = min(ROW_TILE, s)

    def body(q_ref, kv_ref, do_ref, dq_ref, dkv_ref):
        i = pl.program_id(0)
        heads = range(XA_HEADS)
        qv = [q_ref[:, h * XA_DIM:(h + 1) * XA_DIM].astype(BF16) for h in heads]
        dov = [do_ref[:, h * XA_DIM:(h + 1) * XA_DIM].astype(BF16) for h in heads]
        kv = [kv_ref[h].astype(BF16) for h in heads]
        sc = [lax.dot_general(qv[h], kv[h], (NT, ((), ())), preferred_element_type=F32) * (XA_DIM ** -0.5)
              for h in heads]
        dp = [lax.dot_general(dov[h], kv_ref[XA_HEADS + h].astype(BF16), (NT, ((), ())), preferred_element_type=F32)
              for h in heads]
        pb, ds = [], []
        for h in heads:
            e = jnp.exp(sc[h] - jnp.max(sc[h], axis=-1, keepdims=True))
            p = e / jnp.sum(e, axis=-1, keepdims=True)
            pb.append(p.astype(BF16))
            ds.append((p * (dp[h] - jnp.sum(p * dp[h], axis=-1, keepdims=True)) * (XA_DIM ** -0.5)).astype(BF16))
        dq = [jnp.dot(ds[h], kv[h], preferred_element_type=F32) for h in heads]
        dk = [lax.dot_general(ds[h], qv[h], (TN, ((), ())), preferred_element_type=F32) for h in heads]
        dv = [lax.dot_general(pb[h], dov[h], (TN, ((), ())), preferred_element_type=F32) for h in heads]
        for h in heads:
            dq_ref[:, h * XA_DIM:(h + 1) * XA_DIM] = dq[h].astype(dq_ref.dtype)

        @pl.when(i == 0)
        def _():
            for h in heads:
                dkv_ref[h] = dk[h]
                dkv_ref[XA_HEADS + h] = dv[h]

        @pl.when(i > 0)
        def _():
            for h in heads:
                dkv_ref[h] += dk[h]
                dkv_ref[XA_HEADS + h] += dv[h]

    row = pl.BlockSpec((t, D_MODEL), lambda i: (i, 0))
    kvs = pl.BlockSpec((2 * XA_HEADS, MEM_LEN, XA_DIM), lambda i: (0, 0, 0))
    return pl.pallas_call(
        body, name="xattn_bwd", grid=(s // t,),
        out_shape=(jax.ShapeDtypeStruct((s, D_MODEL), BF16),
                   jax.ShapeDtypeStruct((2 * XA_HEADS, MEM_LEN, XA_DIM), F32)),
        in_specs=[row, kvs, row], out_specs=(row, kvs), compiler_params=_params(),
    )(qx, kvm, do)


def _adamw(parts, owns, me, w, m, v, name):
    nl, r, c = w.shape
    tr = r
    for cand in (512, 352, 256, 176, 128, 64, 32, 16, 8):
        if r % cand == 0 and r > cand and N_DEV * cand * c * 4 <= ADAMW_BLOCK_BYTES:
            tr = cand
            break
    nt = r // tr
    per_layer = N_DEV + (1 if owns is not None else 0)

    def body(me_ref, *refs):
        w_ref, m_ref, v_ref, g_ref, d_ref, nm_ref, nv_ref = refs[nl * per_layer:]
        layer = pl.program_id(0)
        g = None
        for l in range(nl):
            p_refs = refs[l * per_layer:(l + 1) * per_layer]
            gl = None
            for d in range(N_DEV):
                term = p_refs[d][...].astype(F32)
                if owns is not None:
                    term = jnp.where(me_ref[0] == d, p_refs[N_DEV][...].astype(F32), term)
                gl = term if gl is None else gl + term
            g = gl if g is None else jnp.where(layer == l, gl, g)
        mn = ADAM_B1 * m_ref[...] + (1.0 - ADAM_B1) * g
        vn = ADAM_B2 * v_ref[...] + (1.0 - ADAM_B2) * (g * g)
        m_hat = mn / (1.0 - ADAM_B1 ** ADAM_STEP)
        v_hat = vn / (1.0 - ADAM_B2 ** ADAM_STEP)
        g_ref[...] = g
        d_ref[...] = -ADAM_LR * (m_hat / (jnp.sqrt(v_hat) + ADAM_EPS) + ADAM_WD * w_ref[...])
        nm_ref[...] = mn
        nv_ref[...] = vn

    def rows(l, ll, i):
        return jnp.where(ll == l, i, jnp.where(ll < l, 0, nt - 1))

    def part_spec(l, d):
        if owns is None:
            return pl.BlockSpec((None, tr, c), lambda ll, i, me_ref: (d, rows(l, ll, i), 0))
        return pl.BlockSpec((None, tr, c),
                            lambda ll, i, me_ref: (jnp.where(me_ref[0] == d, (d + 1) % N_DEV, d), rows(l, ll, i), 0))

    def own_spec(l):
        return pl.BlockSpec((None, tr, c), lambda ll, i, me_ref: (me_ref[0], rows(l, ll, i), 0))

    in_specs, operands = [], []
    for l in range(nl):
        in_specs += [part_spec(l, d) for d in range(N_DEV)]
        operands += [parts[l]] * N_DEV
        if owns is not None:
            in_specs.append(own_spec(l))
            operands.append(owns[l])
    blk = pl.BlockSpec((None, tr, c), lambda ll, i, me_ref: (ll, i, 0))
    shp = jax.ShapeDtypeStruct((nl, r, c), F32)
    return pl.pallas_call(
        body, name=name, out_shape=(shp, shp, shp, shp),
        grid_spec=pltpu.PrefetchScalarGridSpec(
            num_scalar_prefetch=1, grid=(nl, nt), in_specs=in_specs + [blk, blk, blk],
            out_specs=(blk, blk, blk, blk)),
        compiler_params=_params(),
    )(me.reshape(1), *operands, w, m, v)


GROUPS = {"in": ("w_in",), "rest": ("w_out", "w_xq", "w_xo", "w_xkv", "w_up", "w_down")}
FULL_SHAPES = {"w_in": (D_MODEL, N_IN_PAD), "w_out": (D_MODEL, D_MODEL), "w_xq": (D_MODEL, D_MODEL),
               "w_xo": (D_MODEL, D_MODEL), "w_xkv": (N_DEV, D_MODEL, 2 * D_MODEL // N_DEV),
               "w_up": (N_DEV, FF_SHARD, D_MODEL), "w_down": (FF_HALF, FF_SHARD, D_MODEL)}
PIECE_SHAPES = {"w_in": (N_DEV, D_MODEL // N_DEV, N_IN_PAD), "w_out": (N_DEV, D_MODEL // N_DEV, D_MODEL),
                "w_xq": (N_DEV, D_MODEL // N_DEV, D_MODEL), "w_xo": (N_DEV, D_MODEL // N_DEV, D_MODEL),
                "w_xkv": (N_DEV, D_MODEL, 2 * D_MODEL // N_DEV), "w_up": (N_DEV, FF_SHARD, D_MODEL),
                "w_down": (N_DEV, D_FF // N_DEV, D_MODEL)}
CONV_WORDS = 8192


class _GatheredWeights:
    def __init__(self, states, layer):
        self.states, self.layer, self.full, self.extra = dict(states), layer, {}, None

    def need(self, group, after):
        if group in self.states:
            got, _ = _exchange_wait(self.states.pop(group), after, "gather_%s_wait_%d" % (group, self.layer))
            for name, g in zip(GROUPS[group], got):
                self.full[name] = g.reshape(FULL_SHAPES[name])
            self.extra = got[len(GROUPS[group]):]

    def __getitem__(self, name):
        return self.full[name]


def _relay_in_cols(w):
    pad = jnp.zeros(w.shape[:-1] + (N_IN_PAD - N_IN,), w.dtype)
    return jnp.concatenate([w[..., :2304], w[..., 2308:N_IN], w[..., 2304:2308], pad], axis=-1)


def _unrelay_in_cols(w):
    return jnp.concatenate([w[..., :2304], w[..., COL_GATE:COL_GATE + 4], w[..., 2304:COL_GATE]], axis=-1)


def _layer_fwd(h, memv, w, sm, tables):
    sv = {"h0": h}
    s = h.shape[0]
    tm, tb = min(ROW_TILE, s), min(MM_TILE, s)
    w.need("in", h)
    tn = N_IN_PAD // 3
    proj, xn = _norm_matmul(h, sm["g_mix"], w["w_in"], (s, N_IN_PAD), grid=(s // tb, 3),
                            b_spec=pl.BlockSpec((D_MODEL, tn), lambda i, j: (0, j)),
                            o_spec=pl.BlockSpec((tb, tn), lambda i, j: (i, j)), name="norm_mm_in")
    sv["xn"], sv["proj"] = xn, proj
    ycat = _sconv_fwd(proj, sm["w_sconv"])
    qd, kd, vd, ktd, vtd = _heads_split(proj, 1, tables["rope"], None, "split_dil")
    ycat, ob, lse_b = _attention_fwd("dil", qd, kd, vtd, tables["dil"], ycat, 1)
    sv["dil"] = (qd, kd, vd, ktd, ob, lse_b)
    c = _gate_cumsum(proj, sm["b_forget_pad"])
    qf, kf, vf, ktf, vtf = _heads_split(proj, 2, None, c, "split_fox")
    ycat, oc, lse_c = _attention_fwd("fox", qf, kf, vtf, tables["fox"], ycat, 2)
    sv["fox"] = (qf, kf, vf, ktf, oc, lse_c)
    ycat = _pool_fwd(proj, sm["w_pool_bd"], sm["pool_scale"], ycat)
    sv["ycat"] = ycat
    w.need("rest", ycat)
    h1 = _mm_nn(ycat, w["w_out"], "mm_out", res=h)
    sv["h1"] = h1
    memn = _rms_fwd(memv, sm["g_mem"], "rms_mem")
    qx, xq = _norm_matmul(h1, sm["g_xa"], w["w_xq"], (s, D_MODEL), grid=(s // tb, 1),
                          b_spec=pl.BlockSpec((D_MODEL, D_MODEL), lambda i, j: (0, 0)),
                          o_spec=pl.BlockSpec((tb, D_MODEL), lambda i, j: (i, 0)), name="norm_mm_xq",
                          out_dtype=BF16)
    kvm = _matmul(memn, w["w_xkv"], (N_DEV, MEM_LEN, XA_DIM), grid=(N_DEV, 1, 1),
                  a_spec=pl.BlockSpec((MEM_LEN, D_MODEL), lambda i, j, r: (0, 0)),
                  b_spec=pl.BlockSpec((None, D_MODEL, XA_DIM), lambda i, j, r: (i, 0, 0)),
                  o_spec=pl.BlockSpec((None, MEM_LEN, XA_DIM), lambda i, j, r: (i, 0, 0)),
                  dims=NN, nred=1, name="mm_xkv")
    ox = _xattn_fwd(qx, kvm)
    sv.update(xq=xq, memn=memn, qx=qx, kvm=kvm, ox=ox)
    h2 = _mm_nn(ox, w["w_xo"], "mm_xo", res=h1)
    sv["h2"] = h2
    u0, xf = _norm_matmul(h2, sm["g_ffn"], w["w_up"], (N_DEV, s, FF_SHARD), grid=(s // tb, N_DEV),
                          b_spec=pl.BlockSpec((None, FF_SHARD, D_MODEL), lambda i, j: (j, 0, 0)),
                          o_spec=pl.BlockSpec((None, tb, FF_SHARD), lambda i, j: (j, i, 0)), name="norm_mm_up",
                          out_dtype=BF16, dims=NT)
    act = _ffn_gate_fwd(u0, sm["w_ffconv"])
    sv.update(xf=xf, u0=u0, act=act)
    ospec = pl.BlockSpec((tm, D_MODEL), lambda i, j, r: (i, 0))
    h3 = _matmul(act, w["w_down"], (s, D_MODEL), grid=(s // tm, 1, 1),
                 a_spec=pl.BlockSpec((FF_HALF, tm, FF_SHARD), lambda i, j, r: (0, i, 0)),
                 b_spec=pl.BlockSpec((FF_HALF, FF_SHARD, D_MODEL), lambda i, j, r: (0, 0, 0)),
                 o_spec=ospec, dims=NN, nred=1, slabs=FF_HALF, name="mm_down", res=h2, res_spec=ospec)
    return h3, sv


def _layer_bwd(dh3, memv, w, sm, tables, sv, rest_ready, in_ready):
    s = dh3.shape[0]
    tm, tb = min(ROW_TILE, s), min(MM_TILE, s)
    big, small = {}, {}
    ts = max(s // 2, 1)
    dact = _matmul(dh3, w["w_down"], (FF_HALF, s, FF_SHARD), grid=(s // tb, FF_HALF, 1),
                   a_spec=pl.BlockSpec((tb, D_MODEL), lambda i, j, r: (i, 0)),
                   b_spec=pl.BlockSpec((None, FF_SHARD, D_MODEL), lambda i, j, r: (j, 0, 0)),
                   o_spec=pl.BlockSpec((None, tb, FF_SHARD), lambda i, j, r: (j, i, 0)),
                   dims=NT, nred=1, name="mm_dact", out_dtype=BF16)
    big["w_down"] = _matmul(sv["act"], dh3, (FF_HALF, FF_SHARD, D_MODEL), grid=(FF_HALF, 1, s // ts),
                            a_spec=pl.BlockSpec((None, ts, FF_SHARD), lambda i, j, r: (i, r, 0)),
                            b_spec=pl.BlockSpec((ts, D_MODEL), lambda i, j, r: (r, 0)),
                            o_spec=pl.BlockSpec((None, FF_SHARD, D_MODEL), lambda i, j, r: (i, 0, 0)),
                            dims=TN, nred=s // ts, name="mm_dw_down", out_dtype=GRAD_DTYPE)
    du0, small["w_ffconv"] = _ffn_gate_bwd(sv["u0"], sm["w_ffconv"], dact)
    dh2, small["g_ffn"] = _matmul_rms_bwd(du0, w["w_up"], sv["h2"], sm["g_ffn"], dh3, "mm_dxf_rms_bwd",
                                          tm=ROW_TILE // 2, dims=NN)
    big["w_up"] = _matmul(du0, sv["xf"], (N_DEV, FF_SHARD, D_MODEL), grid=(N_DEV, 1, 1),
                          a_spec=pl.BlockSpec((None, s, FF_SHARD), lambda i, j, r: (i, 0, 0)),
                          b_spec=pl.BlockSpec((s, D_MODEL), lambda i, j, r: (0, 0)),
                          o_spec=pl.BlockSpec((None, FF_SHARD, D_MODEL), lambda i, j, r: (i, 0, 0)),
                          dims=TN, nred=1, name="mm_dw_up", out_dtype=GRAD_DTYPE)
    dox = _mm_nt(dh2, w["w_xo"], "mm_dox", out_dtype=BF16)
    big["w_xo"] = _mm_tn(sv["ox"], dh2, "mm_dw_xo")
    dqx, dkvm = _xattn_bwd(sv["qx"], sv["kvm"], dox)
    big["w_xq"] = _mm_tn(sv["xq"], dqx, "mm_dw_xq")
    big["w_xkv"] = _matmul(sv["memn"], dkvm, (N_DEV, D_MODEL, XA_DIM), grid=(N_DEV, 1, 1),
                           a_spec=pl.BlockSpec((MEM_LEN, D_MODEL), lambda i, j, r: (0, 0)),
                           b_spec=pl.BlockSpec((None, MEM_LEN, XA_DIM), lambda i, j, r: (i, 0, 0)),
                           o_spec=pl.BlockSpec((None, D_MODEL, XA_DIM), lambda i, j, r: (i, 0, 0)),
                           dims=TN, nred=1, name="mm_dw_xkv", out_dtype=GRAD_DTYPE)
    dmemn = _matmul(dkvm, w["w_xkv"], (MEM_LEN, D_MODEL), grid=(1, 1, 1),
                    a_spec=pl.BlockSpec((N_DEV, MEM_LEN, XA_DIM), lambda i, j, r: (0, 0, 0)),
                    b_spec=pl.BlockSpec((N_DEV, D_MODEL, XA_DIM), lambda i, j, r: (0, 0, 0)),
                    o_spec=pl.BlockSpec((MEM_LEN, D_MODEL), lambda i, j, r: (0, 0)),
                    dims=NT, nred=1, slabs=N_DEV, name="mm_dmemn")
    _, small["g_mem"] = _rms_bwd(dmemn, memv, sm["g_mem"], None, "rms_mem_bwd")
    dh1, small["g_xa"] = _matmul_rms_bwd(dqx, w["w_xq"], sv["h1"], sm["g_xa"], dh2, "mm_dxq_rms_bwd")
    big["w_out"] = _mm_tn(sv["ycat"], dh1, "mm_dw_out")
    dycat = _mm_nt(dh1, w["w_out"] + rest_ready(big, small).astype(BF16), "mm_dycat")
    proj = sv["proj"]
    dproj, small["w_sconv"] = _sconv_bwd(proj, sm["w_sconv"], dycat)
    qd, kd, vd, ktd, ob, lse_b = sv["dil"]
    delta, dob = _attention_delta(ob, dycat, 1)
    dqt, dk, dv = _attention_bwd("dil", qd, kd, vd, ktd, tables["dil"], dob, lse_b, delta)
    dproj = _heads_merge(dqt, dk, dv, tables["rope"], "merge_dil", dproj, 1)
    qf, kf, vf, ktf, oc, lse_c = sv["fox"]
    delta, dob = _attention_delta(oc, dycat, 2)
    dqt, dk, dv = _attention_bwd("fox", qf, kf, vf, ktf, tables["fox"], dob, lse_c, delta)
    dproj, dc = _heads_merge(dqt, dk, dv, None, "merge_fox", dproj, 2)
    dproj, dbias = _gate_cumsum_bwd(proj, sm["b_forget_pad"], dc, dproj)
    small["b_forget"] = dbias[0, :N_HEADS]
    dproj, dwbd, small["pool_scale"] = _pool_bwd(proj, sm["w_pool_bd"], sm["pool_scale"], dycat, dproj)
    small["w_pool"] = jnp.stack([dwbd[64 * g:64 * (g + 1), 64 * g:64 * (g + 1)] for g in range(4)])
    big["w_in"] = _mm_tn(sv["xn"], dproj, "mm_dw_in", tn=896)
    dh0, small["g_mix"] = _matmul_rms_bwd(dproj, w["w_in"], sv["h0"], sm["g_mix"] + in_ready(big, small), dh1,
                                          "mm_dxn_rms_bwd")
    return dh0, big, small


SMALL_NAMES = ("g_mix", "b_forget", "w_pool", "pool_scale", "g_xa", "g_mem", "g_ffn", "w_sconv", "w_ffconv")
SMALL_WITH = {"rest": ("w_ffconv", "g_ffn", "g_mem", "g_xa"),
              "in": ("w_sconv", "b_forget", "pool_scale", "w_pool")}
SMALL_SHAPES = {"w_sconv": (3, GROUP), "w_ffconv": (N_DEV, 3, FF_SHARD)}
WEIGHT_NAMES = ("g_mix", "w_in", "b_forget", "w_sconv", "w_pool", "pool_scale", "w_out", "g_xa", "g_mem", "w_xq",
                "w_xkv", "w_xo", "g_ffn", "w_up", "w_ffconv", "w_down", "g_final")


def _block_diag(w_pool):
    z = jnp.zeros((64, 64), F32)
    return jnp.concatenate(
        [jnp.concatenate([w_pool[g] if c == g else z for c in range(4)], axis=1) for g in range(4)], axis=0)


def kernel(x, mem, positions, g_mix, w_in, b_forget, w_sconv, w_pool, pool_scale, w_out, g_xa, g_mem, w_xq, w_xkv, w_xo, g_ffn, w_up, w_ffconv, w_down, g_final, loss_target, m_g_mix, m_w_in, m_b_forget, m_w_sconv, m_w_pool, m_pool_scale, m_w_out, m_g_xa, m_g_mem, m_w_xq, m_w_xkv, m_w_xo, m_g_ffn, m_w_up, m_w_ffconv, m_w_down, m_g_final, v_g_mix, v_w_in, v_b_forget, v_w_sconv, v_w_pool, v_pool_scale, v_w_out, v_g_xa, v_g_mem, v_w_xq, v_w_xkv, v_w_xo, v_g_ffn, v_w_up, v_w_ffconv, v_w_down, v_g_final):
    weights = dict(g_mix=g_mix, w_in=w_in, b_forget=b_forget, w_sconv=w_sconv, w_pool=w_pool, pool_scale=pool_scale,
                   w_out=w_out, g_xa=g_xa, g_mem=g_mem, w_xq=w_xq, w_xkv=w_xkv, w_xo=w_xo, g_ffn=g_ffn, w_up=w_up,
                   w_ffconv=w_ffconv, w_down=w_down, g_final=g_final)
    m_in = dict(g_mix=m_g_mix, w_in=m_w_in, b_forget=m_b_forget, w_sconv=m_w_sconv, w_pool=m_w_pool,
                pool_scale=m_pool_scale, w_out=m_w_out, g_xa=m_g_xa, g_mem=m_g_mem, w_xq=m_w_xq, w_xkv=m_w_xkv,
                w_xo=m_w_xo, g_ffn=m_g_ffn, w_up=m_w_up, w_ffconv=m_w_ffconv, w_down=m_w_down, g_final=m_g_final)
    v_in = dict(g_mix=v_g_mix, w_in=v_w_in, b_forget=v_b_forget, w_sconv=v_w_sconv, w_pool=v_w_pool,
                pool_scale=v_pool_scale, w_out=v_w_out, g_xa=v_g_xa, g_mem=v_g_mem, w_xq=v_w_xq, w_xkv=v_w_xkv,
                w_xo=v_w_xo, g_ffn=v_g_ffn, w_up=v_w_up, w_ffconv=v_w_ffconv, w_down=v_w_down, g_final=v_g_final)
    depth = w_in.shape[0]
    me = 4 * lax.axis_index("x") + 2 * lax.axis_index("y") + lax.axis_index("c")
    h = x[0]
    memv = mem[0]
    s = h.shape[0]
    tq = min(ATT_TQ, s)
    tables = {"rope": _rope_tables(positions[0]), "dil": _bias_tables("dil", tq, tq),
              "fox": _bias_tables("fox", tq, tq)}

    w_in_r = _relay_in_cols(w_in)
    conv_shard = jnp.concatenate([w_sconv.reshape(-1), w_ffconv.reshape(-1)])
    conv_shard = jnp.concatenate([conv_shard, jnp.zeros((CONV_WORDS - conv_shard.shape[0],), F32)])
    shards = dict(w_in=w_in_r, w_out=w_out, w_xq=w_xq, w_xo=w_xo, w_xkv=w_xkv, w_up=w_up.transpose(0, 2, 1),
                  w_down=w_down)
    gathered = []
    order = jnp.zeros((), F32)
    for l in range(depth):
        states = {}
        for group in ("in", "rest"):
            first = GROUPS[group][0]
            shards[first] = shards[first] + order
            xs = [_place_shard(shards[name], l, me, BF16, "place_%s_%d" % (name, l)) for name in GROUPS[group]]
            if l == 0 and group == "in":
                xs.append(_place_shard(conv_shard.reshape(1, CONV_WORDS // 1024, 1024), 0, me, F32, "place_conv"))
            states[group], token = _exchange_start(xs, False, "gather_%s_start_%d" % (group, l))
            order = order + token[0, 0]
        gathered.append(_GatheredWeights(states, l))
    gathered[0].need("in", tables["rope"][0])
    conv_all = gathered[0].extra[0].reshape(N_DEV, CONV_WORDS)
    n_sc = depth * 3 * (GROUP // N_DEV)
    sconv_full = conv_all[:, :n_sc].reshape(N_DEV, depth, 3, GROUP // N_DEV).transpose(1, 2, 0, 3).reshape(
        depth, 3, GROUP)
    ffconv_full = conv_all[:, n_sc:n_sc + depth * 3 * FF_SHARD].reshape(N_DEV, depth, 3, FF_SHARD).transpose(
        1, 0, 2, 3)

    smalls = []
    for l in range(depth):
        smalls.append(dict(
            g_mix=g_mix[l], g_xa=g_xa[l], g_mem=g_mem[l], g_ffn=g_ffn[l], pool_scale=pool_scale[l],
            w_pool_bd=_block_diag(w_pool[l]), w_sconv=sconv_full[l], w_ffconv=ffconv_full[l],
            b_forget_pad=jnp.concatenate([b_forget[l], jnp.zeros((128 - N_HEADS,), F32)]).reshape(1, 128)))
    smalls[0]["g_mix"] = smalls[0]["g_mix"] + order

    saved = []
    for l in range(depth):
        h, sv = _layer_fwd(h, memv, gathered[l], smalls[l], tables)
        saved.append(sv)
    loss_part, dh, dg_final = _loss_head(h, g_final, loss_target[0])
    loss = lax.psum(loss_part[0, 0], MESH_AXES)

    small_grads = [None] * depth
    scatters = {}

    def pieces_of(big, group):
        return [big[name].reshape(PIECE_SHAPES[name]) for name in GROUPS[group]]

    def rider(grads):
        flat = jnp.concatenate([g.reshape(-1) for g in grads])
        rows = -(-flat.shape[0] // 1024)
        flat = jnp.concatenate([flat, jnp.zeros((rows * 1024 - flat.shape[0],), F32)])
        return jnp.broadcast_to(flat.reshape(1, rows, 1024), (N_DEV, rows, 1024))

    riding = {}
    done_small = {}

    def start_scatter(l, group, big, extra):
        names = [(n, l) for n in SMALL_WITH[group]] + extra
        riding[l, group] = names
        grads = [dg_final if n == "g_final" else done_small[ll][n] for n, ll in names]
        scatters[l, group], token = _exchange_start(pieces_of(big, group) + [rider(grads)], True,
                                                    "scatter_%s_start_%d" % (group, l))
        return token[0, 0]

    for l in reversed(range(depth)):
        def rest_ready(big, small, l=l):
            done_small[l] = small
            extra = ([("g_final", l)] if l == depth - 1 else []) + ([("g_mix", l + 1)] if l + 1 < depth else [])
            return start_scatter(l, "rest", big, extra)

        def in_ready(big, small, l=l):
            return start_scatter(l, "in", big, [])

        dh, _, small_grads[l] = _layer_bwd(dh, memv, gathered[l], smalls[l], tables, saved[l], rest_ready, in_ready)
    grad_x = dh[None]
    riding["tail"] = [("g_mix", 0)]
    scatters["tail"], _ = _exchange_start([rider([small_grads[0]["g_mix"]])], True, "scatter_tail_start")

    parts, owns, small_parts = {}, {}, {}

    def take_rider(key, got, given):
        flat = lax.dynamic_update_slice_in_dim(got, given[:1], me, axis=0).reshape(N_DEV, -1)
        off = 0
        for name, ll in riding[key]:
            shape = SMALL_SHAPES.get(name, weights[name].shape[-1:] if name == "g_final" else weights[name].shape[1:])
            n = 1
            for dim in shape:
                n *= dim
            small_parts.setdefault(name, [None] * depth)[ll] = flat[:, off:off + n].reshape((N_DEV,) + shape)
            off += n

    def wait_group(group, after):
        for l in reversed(range(depth)):
            got, given = _exchange_wait(scatters[l, group], after, "scatter_%s_wait_%d" % (group, l))
            for name, g, x in zip(GROUPS[group], got, given):
                parts.setdefault(name, [None] * depth)[l] = g
                owns.setdefault(name, [None] * depth)[l] = x
            take_rider((l, group), got[-1], given[-1])

    results = {}

    def update(name, w3, m3, v3):
        outs = _adamw(parts[name], owns.get(name), me, w3, m3, v3, "adamw_" + name)
        results[name] = [o.reshape(weights[name].shape) for o in outs]

    wait_group("rest", grad_x)
    for name in GROUPS["rest"]:
        if name == "w_up":
            outs = _adamw(parts[name], owns[name], me, w_up.transpose(0, 2, 1), m_w_up.transpose(0, 2, 1),
                          v_w_up.transpose(0, 2, 1), "adamw_w_up")
            results[name] = [o.transpose(0, 2, 1) for o in outs]
        else:
            update(name, weights[name], m_in[name], v_in[name])
    wait_group("in", results["w_down"][1])
    outs = _adamw(parts["w_in"], owns["w_in"], me, w_in_r, _relay_in_cols(m_w_in), _relay_in_cols(v_w_in),
                  "adamw_w_in")
    results["w_in"] = [_unrelay_in_cols(o) for o in outs]
    got, given = _exchange_wait(scatters["tail"], results["w_in"][1], "scatter_tail_wait")
    take_rider("tail", got[0], given[0])
    for name in SMALL_NAMES + ("g_final",):
        wv = weights[name]
        p = small_parts[name][depth - 1] if name == "g_final" else jnp.stack(small_parts[name], axis=1)
        if name == "w_sconv":
            p = lax.dynamic_slice_in_dim(p, me * (GROUP // N_DEV), GROUP // N_DEV, axis=3)
        elif name == "w_ffconv":
            p = lax.dynamic_index_in_dim(p, me, axis=2, keepdims=False)
        shape3 = (1, 1, wv.shape[0]) if wv.ndim == 1 else (1, -1, wv.shape[-1])
        w3 = wv.reshape(shape3)
        parts[name] = [p.reshape((N_DEV,) + w3.shape[1:])]
        update(name, w3, m_in[name].reshape(shape3), v_in[name].reshape(shape3))

    return (loss, grad_x, *[results[n][0] for n in WEIGHT_NAMES], *[results[n][1] for n in WEIGHT_NAMES],
            *[results[n][2] for n in WEIGHT_NAMES], *[results[n][3] for n in WEIGHT_NAMES])
```

```python
import functools

import jax
import jax.numpy as jnp
from jax import lax
from jax.experimental import pallas as pl
from jax.experimental.pallas import tpu as pltpu

F32 = jnp.float32
BF16 = jnp.bfloat16

N_DEV = 8
D_MODEL = 1024
GROUP = 256
HEAD_DIM = 64
N_HEADS = 4
N_IN = 2564
N_IN_PAD = 2688
COL_GATE = 2560
XA_HEADS = 4
XA_DIM = 256
MEM_LEN = 256
D_FF = 2816
FF_SHARD = 704
FF_HALF = 4
ROPE_THETA = 500000.0
ROPE_DIM = 16
RMS_EPS = 1e-6
NEG = -1e30
POOL_WINDOWS = (2, 4, 8, 16)
ADAM_LR, ADAM_B1, ADAM_B2, ADAM_EPS, ADAM_WD, ADAM_STEP = 0.001, 0.9, 0.999, 1e-08, 0.01, 10

ROW_TILE = 1024
MM_TILE = 1024
SLAB_TILE = 512
ATT_TQ = 512
BWD_HEADS = 4
VMEM_LIMIT = 56 * 1024 * 1024
ADAMW_BLOCK_BYTES = 8 * 1024 * 1024
PLACE_BLOCK_BYTES = 4 * 1024 * 1024

MESH_AXES = ("x", "y", "c")


def _params(**kw):
    return pltpu.CompilerParams(vmem_limit_bytes=VMEM_LIMIT, **kw)


HBM_SPEC = pl.BlockSpec(memory_space=pltpu.HBM)
SEM_SPEC = pl.BlockSpec(memory_space=pltpu.SEMAPHORE)
DATAFLOW = pltpu.SideEffectType.DATAFLOW_SIDE_EFFECTING


def _peer_copies(x_ref, land_ref, send_sems, recv_sems, scatter):
    mx, my, mc = lax.axis_index("x"), lax.axis_index("y"), lax.axis_index("c")
    me = 4 * mx + 2 * my + mc
    pairs = []
    for k in range(1, N_DEV):
        kx, ky, kc = (k >> 2) & 1, (k >> 1) & 1, k & 1
        peer_lin = me ^ k
        send = pltpu.make_async_remote_copy(
            src_ref=x_ref.at[peer_lin] if scatter else land_ref.at[me], dst_ref=land_ref.at[me],
            send_sem=send_sems.at[k - 1], recv_sem=recv_sems.at[k - 1],
            device_id=(mx ^ kx, my ^ ky, mc ^ kc), device_id_type=pl.DeviceIdType.MESH)
        arrival = pltpu.make_async_remote_copy(
            src_ref=land_ref.at[peer_lin], dst_ref=land_ref.at[peer_lin],
            send_sem=send_sems.at[k - 1], recv_sem=recv_sems.at[k - 1],
            device_id=(mx, my, mc), device_id_type=pl.DeviceIdType.MESH)
        pairs.append((send, arrival))
    return pairs


def _exchange_start(xs, scatter, name):
    n = len(xs)
    ns = n if scatter else 0

    def body(*refs):
        srcs = refs[:ns] if scatter else (None,) * n
        lands, sends, recvs = refs[ns:ns + n], refs[ns + n:ns + 2 * n], refs[ns + 2 * n:ns + 3 * n]
        for t in range(n):
            for send, _ in _peer_copies(srcs[t], lands[t], sends[t], recvs[t], scatter):
                send.start()
        token = refs[-1]
        token[...] = jnp.zeros_like(token)

    sems = pltpu.SemaphoreType.DMA((N_DEV - 1,))
    operands = [pltpu.with_memory_space_constraint(x, pltpu.HBM) for x in xs]
    if scatter:
        operands += [pltpu.with_memory_space_constraint(lax.empty(x.shape, x.dtype), pltpu.HBM) for x in xs]
    outs = pl.pallas_call(
        body, name=name,
        out_shape=(sems,) * (2 * n) + tuple(pltpu.HBM(a.shape, a.dtype) for a in operands)
        + (jax.ShapeDtypeStruct((8, 128), F32),),
        in_specs=(HBM_SPEC,) * (ns + n),
        out_specs=(SEM_SPEC,) * (2 * n) + (HBM_SPEC,) * (ns + n) + (pl.BlockSpec(memory_space=pltpu.VMEM),),
        input_output_aliases={i: 2 * n + i for i in range(ns + n)},
        compiler_params=pltpu.CompilerParams(has_side_effects=DATAFLOW),
    )(*operands)
    return (outs[:-1], scatter), outs[-1]


def _exchange_wait(state, after, name):
    held, scatter = state
    n = len(held) // (4 if scatter else 3)
    ns = n if scatter else 0
    sems, thru = held[:2 * n], held[2 * n:]

    def body(*refs):
        srcs = refs[:ns] if scatter else (None,) * n
        lands, sends, recvs = refs[ns:ns + n], refs[ns + n:ns + 2 * n], refs[ns + 2 * n:ns + 3 * n]
        for t in range(n):
            for send, arrival in _peer_copies(srcs[t], lands[t], sends[t], recvs[t], scatter):
                send.wait_send()
                arrival.wait_recv()

    outs = pl.pallas_call(
        body, name=name,
        out_shape=tuple(pltpu.HBM(a.shape, a.dtype) for a in thru),
        in_specs=(HBM_SPEC,) * (ns + n) + (SEM_SPEC,) * (2 * n) + (pl.BlockSpec(memory_space=pl.ANY),),
        out_specs=(HBM_SPEC,) * (ns + n), input_output_aliases={i: i for i in range(ns + n)},
        compiler_params=pltpu.CompilerParams(has_side_effects=DATAFLOW),
    )(*thru, *sems, after)
    return list(outs[ns:]), list(outs[:ns])


def _place_shard(x, layer, me, dtype, name, after=None):
    _, r, c = x.shape
    tr = r
    if r * c * 4 > PLACE_BLOCK_BYTES:
        for cand in (512, 256, 128, 64, 32, 16):
            if r % cand == 0 and cand * c * 4 <= PLACE_BLOCK_BYTES:
                tr = cand
                break

    def body(me_ref, x_ref, *rest):
        o_ref = rest[-1]
        o_ref[...] = x_ref[...].astype(o_ref.dtype)

    return pl.pallas_call(
        body, name=name, out_shape=jax.ShapeDtypeStruct((N_DEV, r, c), dtype),
        grid_spec=pltpu.PrefetchScalarGridSpec(
            num_scalar_prefetch=1, grid=(r // tr,),
            in_specs=[pl.BlockSpec((None, tr, c), lambda i, me_ref: (layer, i, 0))]
            + ([ANY_SPEC] if after is not None else []),
            out_specs=pl.BlockSpec((None, tr, c), lambda i, me_ref: (me_ref[0], i, 0))),
        compiler_params=_params(),
    )(*((me.reshape(1), x) + ((after,) if after is not None else ())))


NN = ((1,), (0,))
NT = ((1,), (1,))
TN = ((0,), (0,))


def _matmul(a, b, out_shape, *, grid, a_spec, b_spec, o_spec, dims, nred, name, res=None, res_spec=None,
            out_dtype=F32, slabs=0):
    has_res = res is not None

    def body(*refs):
        a_ref, b_ref = refs[0], refs[1]
        r_ref = refs[2] if has_res else None
        o_ref = refs[3] if has_res else refs[2]
        if slabs:
            part = None
            for n in range(slabs):
                term = lax.dot_general(a_ref[n].astype(BF16), b_ref[n].astype(BF16), (dims, ((), ())),
                                       preferred_element_type=F32)
                part = term if part is None else part + term
        else:
            part = lax.dot_general(a_ref[...].astype(BF16), b_ref[...].astype(BF16), (dims, ((), ())),
                                   preferred_element_type=F32)
        if nred == 1:
            if has_res:
                part = part + r_ref[...]
            o_ref[...] = part.astype(o_ref.dtype)
        else:
            acc = refs[-1]
            r = pl.program_id(2)

            @pl.when(r == 0)
            def _():
                acc[...] = part

            @pl.when(r > 0)
            def _():
                acc[...] += part

            @pl.when(r == nred - 1)
            def _():
                tot = acc[...]
                if has_res:
                    tot = tot + r_ref[...]
                o_ref[...] = tot.astype(o_ref.dtype)

    in_specs = [a_spec, b_spec] + ([res_spec] if has_res else [])
    args = (a, b) + ((res,) if has_res else ())
    acc_shape = tuple(d for d in o_spec.block_shape if d is not None)
    return pl.pallas_call(
        body, name=name, grid=grid, out_shape=jax.ShapeDtypeStruct(out_shape, out_dtype),
        in_specs=in_specs, out_specs=o_spec,
        scratch_shapes=[pltpu.VMEM(acc_shape, F32)] if nred > 1 else [],
        compiler_params=_params(),
    )(*args)


def _mm_nn(a, w, name, res=None, tn=None, out_dtype=F32):
    m, k = a.shape
    n = w.shape[1]
    tn = tn or n
    tm = min(MM_TILE, m)
    ospec = pl.BlockSpec((tm, tn), lambda i, j, r: (i, j))
    return _matmul(a, w, (m, n), grid=(m // tm, n // tn, 1),
                   a_spec=pl.BlockSpec((tm, k), lambda i, j, r: (i, 0)),
                   b_spec=pl.BlockSpec((k, tn), lambda i, j, r: (0, j)),
                   o_spec=ospec, dims=NN, nred=1, name=name, res=res, res_spec=ospec if res is not None else None,
                   out_dtype=out_dtype)


def _mm_nt(a, w, name, out_dtype=F32):
    m, n = a.shape
    k = w.shape[0]
    tm = min(MM_TILE, m)
    return _matmul(a, w, (m, k), grid=(m // tm, 1, 1),
                   a_spec=pl.BlockSpec((tm, n), lambda i, j, r: (i, 0)),
                   b_spec=pl.BlockSpec((k, n), lambda i, j, r: (0, 0)),
                   o_spec=pl.BlockSpec((tm, k), lambda i, j, r: (i, 0)), dims=NT, nred=1, name=name,
                   out_dtype=out_dtype)


def _norm_matmul(h, g, b, out_shape, *, grid, b_spec, o_spec, name, out_dtype=F32, dims=NN):
    s, d = h.shape
    tm = s // grid[0]

    def body(h_ref, g_ref, b_ref, o_ref, xn_ref):
        @pl.when(pl.program_id(1) == 0)
        def _():
            hv = h_ref[...]
            r = lax.rsqrt(jnp.mean(hv * hv, axis=-1, keepdims=True) + RMS_EPS)
            xn_ref[...] = (hv * r * g_ref[...]).astype(xn_ref.dtype)

        o_ref[...] = lax.dot_general(xn_ref[...], b_ref[...].astype(BF16), (dims, ((), ())),
                                     preferred_element_type=F32).astype(o_ref.dtype)

    row = pl.BlockSpec((tm, d), lambda i, j: (i, 0))
    return pl.pallas_call(
        body, name=name, grid=grid,
        out_shape=(jax.ShapeDtypeStruct(out_shape, out_dtype), jax.ShapeDtypeStruct((s, d), BF16)),
        in_specs=[row, pl.BlockSpec((1, d), lambda i, j: (0, 0)), b_spec],
        out_specs=(o_spec, row), compiler_params=_params(),
    )(h, g.reshape(1, d), b)


def _matmul_rms_bwd(a, w, h, g, res, name, tm=SLAB_TILE, dims=NT):
    slabs = a.shape[0] if a.ndim == 3 else 0
    s, n = a.shape[-2:]
    d = w.shape[-2] if dims == NT else w.shape[-1]
    tm = min(tm, s)

    def body(a_ref, w_ref, h_ref, g_ref, r_ref, dh_ref, dg_ref):
        if slabs:
            dy = None
            for j in range(slabs):
                term = lax.dot_general(a_ref[j].astype(BF16), w_ref[j].astype(BF16), (dims, ((), ())),
                                       preferred_element_type=F32)
                dy = term if dy is None else dy + term
        else:
            dy = lax.dot_general(a_ref[...].astype(BF16), w_ref[...].astype(BF16), (dims, ((), ())),
                                 preferred_element_type=F32)
        hv = h_ref[...]
        r = lax.rsqrt(jnp.mean(hv * hv, axis=-1, keepdims=True) + RMS_EPS)
        hn = hv * r
        u = dy * g_ref[...]
        dh_ref[...] = r * (u - hn * jnp.mean(u * hn, axis=-1, keepdims=True)) + r_ref[...]
        part = jnp.sum(dy * hn, axis=0, keepdims=True)

        @pl.when(pl.program_id(0) == 0)
        def _():
            dg_ref[...] = part

        @pl.when(pl.program_id(0) > 0)
        def _():
            dg_ref[...] += part

    row = pl.BlockSpec((tm, d), lambda i: (i, 0))
    vec = pl.BlockSpec((1, d), lambda i: (0, 0))
    if slabs:
        a_spec = pl.BlockSpec((slabs, tm, n), lambda i: (0, i, 0))
        w_spec = pl.BlockSpec(w.shape, lambda i: (0, 0, 0))
    else:
        a_spec = pl.BlockSpec((tm, n), lambda i: (i, 0))
        w_spec = pl.BlockSpec(w.shape, lambda i: (0, 0))
    dh, dg = pl.pallas_call(
        body, name=name, grid=(s // tm,),
        out_shape=(jax.ShapeDtypeStruct((s, d), F32), jax.ShapeDtypeStruct((1, d), F32)),
        in_specs=[a_spec, w_spec, row, vec, row], out_specs=(row, vec), compiler_params=_params(),
    )(a, w, h, g.reshape(1, d), res)
    return dh, dg.reshape(d)


GRAD_DTYPE = BF16


def _mm_tn(a, b, name, tk=512, tn=None):
    s, k = a.shape
    n = b.shape[1]
    tn = tn or n
    tk = min(tk, k)
    ts = s if b.dtype == BF16 else max(s // 2, 1)
    return _matmul(a, b, (k, n), grid=(k // tk, n // tn, s // ts),
                   a_spec=pl.BlockSpec((ts, tk), lambda i, j, r: (r, i)),
                   b_spec=pl.BlockSpec((ts, tn), lambda i, j, r: (r, j)),
                   o_spec=pl.BlockSpec((tk, tn), lambda i, j, r: (i, j)), dims=TN, nred=s // ts, name=name,
                   out_dtype=GRAD_DTYPE)


def _rms_fwd(h, g, name):
    s, d = h.shape
    tm = min(ROW_TILE, s)

    def body(h_ref, g_ref, o_ref):
        hv = h_ref[...]
        r = lax.rsqrt(jnp.mean(hv * hv, axis=-1, keepdims=True) + RMS_EPS)
        o_ref[...] = (hv * r * g_ref[...]).astype(o_ref.dtype)

    return pl.pallas_call(
        body, name=name, grid=(s // tm,), out_shape=jax.ShapeDtypeStruct((s, d), BF16),
        in_specs=[pl.BlockSpec((tm, d), lambda i: (i, 0)), pl.BlockSpec((1, d), lambda i: (0, 0))],
        out_specs=pl.BlockSpec((tm, d), lambda i: (i, 0)), compiler_params=_params(),
    )(h, g.reshape(1, d))


def _rms_bwd(dy, h, g, res, name):
    s, d = h.shape
    tm = min(ROW_TILE, s)
    has_res = res is not None

    def body(*refs):
        dy_ref, h_ref, g_ref = refs[:3]
        r_ref = refs[3] if has_res else None
        dh_ref, dg_ref = refs[-2], refs[-1]
        hv = h_ref[...]
        r = lax.rsqrt(jnp.mean(hv * hv, axis=-1, keepdims=True) + RMS_EPS)
        hn = hv * r
        dyv = dy_ref[...].astype(F32)
        u = dyv * g_ref[...]
        dh = r * (u - hn * jnp.mean(u * hn, axis=-1, keepdims=True))
        if has_res:
            dh = dh + r_ref[...]
        dh_ref[...] = dh
        part = jnp.sum(dyv * hn, axis=0, keepdims=True)

        @pl.when(pl.program_id(0) == 0)
        def _():
            dg_ref[...] = part

        @pl.when(pl.program_id(0) > 0)
        def _():
            dg_ref[...] += part

    row = pl.BlockSpec((tm, d), lambda i: (i, 0))
    vec = pl.BlockSpec((1, d), lambda i: (0, 0))
    dh, dg = pl.pallas_call(
        body, name=name, grid=(s // tm,),
        out_shape=(jax.ShapeDtypeStruct((s, d), F32), jax.ShapeDtypeStruct((1, d), F32)),
        in_specs=[row, row, vec] + ([row] if has_res else []),
        out_specs=(row, vec), compiler_params=_params(),
    )(*((dy, h, g.reshape(1, d)) + ((res,) if has_res else ())))
    return dh, dg.reshape(d)


def _loss_head(h, g, target):
    s, d = h.shape
    tm = min(ROW_TILE, s)

    def body(h_ref, g_ref, t_ref, loss_ref, dh_ref, dg_ref):
        hv = h_ref[...]
        r = lax.rsqrt(jnp.mean(hv * hv, axis=-1, keepdims=True) + RMS_EPS)
        hn = hv * r
        gv = g_ref[...]
        err = hn * gv - t_ref[...]
        rows = jnp.mean(err * err, axis=-1, keepdims=True)
        lpart = 0.5 * jnp.sum(rows, axis=0, keepdims=True) + jnp.zeros((1, 128), F32)
        dy = err * (1.0 / d)
        u = dy * gv
        dh_ref[...] = r * (u - hn * jnp.mean(u * hn, axis=-1, keepdims=True))
        gpart = jnp.sum(dy * hn, axis=0, keepdims=True)

        @pl.when(pl.program_id(0) == 0)
        def _():
            dg_ref[...] = gpart
            loss_ref[...] = lpart

        @pl.when(pl.program_id(0) > 0)
        def _():
            dg_ref[...] += gpart
            loss_ref[...] += lpart

    row = pl.BlockSpec((tm, d), lambda i: (i, 0))
    vec = pl.BlockSpec((1, d), lambda i: (0, 0))
    return pl.pallas_call(
        body, name="loss_head", grid=(s // tm,),
        out_shape=(jax.ShapeDtypeStruct((1, 128), F32), jax.ShapeDtypeStruct((s, d), F32),
                   jax.ShapeDtypeStruct((1, d), F32)),
        in_specs=[row, vec, row],
        out_specs=(pl.BlockSpec((1, 128), lambda i: (0, 0)), row, vec), compiler_params=_params(),
    )(h, g.reshape(1, d), target)


def _shift_down(x, k):
    return pltpu.roll(x, k, 0)


def _shift_up(x, k):
    return pltpu.roll(x, x.shape[0] - k, 0)


def _conv3(x, w):
    return w[2:3, :] * x + w[1:2, :] * _shift_down(x, 1) + w[0:1, :] * _shift_down(x, 2)


def _conv3_t(x, w):
    return w[2:3, :] * x + w[1:2, :] * _shift_up(x, 1) + w[0:1, :] * _shift_up(x, 2)


def _sigmoid(x):
    return 1.0 / (1.0 + jnp.exp(-x))


def _prev_map(tile, halo, col):
    return lambda i: (jnp.maximum(i * (tile // halo) - 1, 0), col)


def _next_map(tile, halo, col, nrows):
    return lambda i: (jnp.minimum((i + 1) * (tile // halo), nrows // halo - 1), col)


def _sconv_fwd(proj, w):
    s = proj.shape[0]
    t = min(ROW_TILE, s)

    def body(cur_ref, prev_ref, w_ref, o_ref):
        i = pl.program_id(0)
        prev = prev_ref[...] * (i > 0).astype(F32)
        ext = jnp.concatenate([prev, cur_ref[...]], axis=0)
        sv = ext[:, 2 * GROUP:3 * GROUP] * ext[:, 0:GROUP]
        y = ext[:, GROUP:2 * GROUP] * _conv3(sv, w_ref[...])
        o_ref[...] = y[8:].astype(o_ref.dtype)

    return pl.pallas_call(
        body, name="sconv_fwd", grid=(s // t,), out_shape=jax.ShapeDtypeStruct((s, 4 * GROUP), BF16),
        in_specs=[pl.BlockSpec((t, 3 * GROUP), lambda i: (i, 0)),
                  pl.BlockSpec((8, 3 * GROUP), _prev_map(t, 8, 0)),
                  pl.BlockSpec((3, GROUP), lambda i: (0, 0))],
        out_specs=pl.BlockSpec((t, GROUP), lambda i: (i, 0)), compiler_params=_params(),
    )(proj, proj, w)


def _sconv_bwd(proj, w, dy):
    s = proj.shape[0]
    t = min(ROW_TILE, s)
    nt = s // t

    def body(cur_ref, prev_ref, next_ref, w_ref, dy_ref, dyn_ref, dp_ref, dw_ref):
        i = pl.program_id(0)
        first = (i > 0).astype(F32)
        last = (i < nt - 1).astype(F32)
        ext = jnp.concatenate([prev_ref[...] * first, cur_ref[...], next_ref[...] * last], axis=0)
        dye = jnp.concatenate([jnp.zeros((8, GROUP), F32), dy_ref[...], dyn_ref[...] * last], axis=0)
        hv, bv, cv = ext[:, 0:GROUP], ext[:, GROUP:2 * GROUP], ext[:, 2 * GROUP:3 * GROUP]
        wv = w_ref[...]
        sv = cv * hv
        conv = _conv3(sv, wv)
        dconv = dye * bv
        ds = _conv3_t(dconv, wv)
        dp = jnp.concatenate([ds * cv, dye * conv, ds * hv], axis=1)
        dp_ref[...] = dp[8:8 + t].astype(dp_ref.dtype)
        dc = dconv[8:8 + t]
        dw = jnp.concatenate([
            jnp.sum(dc * _shift_down(sv, 2)[8:8 + t], axis=0, keepdims=True),
            jnp.sum(dc * _shift_down(sv, 1)[8:8 + t], axis=0, keepdims=True),
            jnp.sum(dc * sv[8:8 + t], axis=0, keepdims=True),
            jnp.zeros((5, GROUP), F32)], axis=0)

        @pl.when(i == 0)
        def _():
            dw_ref[...] = dw

        @pl.when(i > 0)
        def _():
            dw_ref[...] += dw

    dp, dw = pl.pallas_call(
        body, name="sconv_bwd", grid=(nt,),
        out_shape=(jax.ShapeDtypeStruct((s, N_IN_PAD), BF16), jax.ShapeDtypeStruct((8, GROUP), F32)),
        in_specs=[pl.BlockSpec((t, 3 * GROUP), lambda i: (i, 0)),
                  pl.BlockSpec((8, 3 * GROUP), _prev_map(t, 8, 0)),
                  pl.BlockSpec((8, 3 * GROUP), _next_map(t, 8, 0, s)),
                  pl.BlockSpec((3, GROUP), lambda i: (0, 0)),
                  pl.BlockSpec((t, GROUP), lambda i: (i, 0)),
                  pl.BlockSpec((8, GROUP), _next_map(t, 8, 0, s))],
        out_specs=(pl.BlockSpec((t, 3 * GROUP), lambda i: (i, 0)), pl.BlockSpec((8, GROUP), lambda i: (0, 0))),
        compiler_params=_params(),
    )(proj, proj, proj, w, dy, dy)
    return dp, dw[:3]


def _lane_window(shape):
    lane = lax.broadcasted_iota(jnp.int32, shape, 1)
    return lane, jnp.where(lane < 64, 2.0, jnp.where(lane < 128, 4.0, jnp.where(lane < 192, 8.0, 16.0)))


def _by_group(lane, s1, s2, s3, s4):
    return jnp.where(lane < 64, s1, jnp.where(lane < 128, s2, jnp.where(lane < 192, s3, s4)))


def _pool_z(ext, row0):
    s1 = ext + _shift_down(ext, 1)
    s2 = s1 + _shift_down(s1, 2)
    s3 = s2 + _shift_down(s2, 4)
    s4 = s3 + _shift_down(s3, 8)
    lane, win = _lane_window(ext.shape)
    tpos = (lax.broadcasted_iota(jnp.int32, ext.shape, 0) + (row0 - 16 + 1)).astype(F32)
    cnt = jnp.maximum(jnp.minimum(tpos, win), 1.0)
    return _by_group(lane, s1, s2, s3, s4) / cnt - ext


ANY_SPEC = pl.BlockSpec(memory_space=pl.ANY)


def _pool_fwd(proj, wbd, scale, ybuf):
    s = proj.shape[0]
    t = min(ROW_TILE, s)
    col = (COL_GATE - GROUP) // GROUP

    def body(cur_ref, prev_ref, w_ref, sc_ref, buf_ref, o_ref):
        i = pl.program_id(0)
        ext = jnp.concatenate([prev_ref[...] * (i > 0).astype(F32), cur_ref[...]], axis=0)
        z = _pool_z(ext, i * t)[16:]
        y = jnp.dot(z.astype(BF16), w_ref[...].astype(BF16), preferred_element_type=F32)
        o_ref[...] = (y * sc_ref[...]).astype(o_ref.dtype)

    return pl.pallas_call(
        body, name="pool_fwd", grid=(s // t,), out_shape=jax.ShapeDtypeStruct(ybuf.shape, ybuf.dtype),
        in_specs=[pl.BlockSpec((t, GROUP), lambda i: (i, col)),
                  pl.BlockSpec((16, GROUP), _prev_map(t, 16, col)),
                  pl.BlockSpec((GROUP, GROUP), lambda i: (0, 0)),
                  pl.BlockSpec((1, GROUP), lambda i: (0, 0)), ANY_SPEC],
        out_specs=pl.BlockSpec((t, GROUP), lambda i: (i, 3)), input_output_aliases={4: 0},
        compiler_params=_params(),
    )(proj, proj, wbd, scale.reshape(1, GROUP), ybuf)


def _pool_bwd(proj, wbd, scale, dy, dbuf):
    s = proj.shape[0]
    t = min(ROW_TILE, s)
    nt = s // t
    col = (COL_GATE - GROUP) // GROUP

    def body(cur_ref, prev_ref, w_ref, sc_ref, dy_ref, dyn_ref, buf_ref, dp_ref, dw_ref, dsc_ref):
        i = pl.program_id(0)
        ext = jnp.concatenate([prev_ref[...] * (i > 0).astype(F32), cur_ref[...]], axis=0)
        z = _pool_z(ext, i * t)[16:]
        wv = w_ref[...].astype(BF16)
        dyc = dy_ref[...]
        dye = jnp.concatenate([dyc, dyn_ref[...] * (i < nt - 1).astype(F32)], axis=0) * sc_ref[...]
        dz = lax.dot_general(dye.astype(BF16), wv, (NT, ((), ())), preferred_element_type=F32)
        lane, win = _lane_window(dz.shape)
        tpos = (lax.broadcasted_iota(jnp.int32, dz.shape, 0) + (i * t + 1)).astype(F32)
        e = dz / jnp.minimum(tpos, win)
        f1 = e + _shift_up(e, 1)
        f2 = f1 + _shift_up(f1, 2)
        f3 = f2 + _shift_up(f2, 4)
        f4 = f3 + _shift_up(f3, 8)
        dp = _by_group(lane, f1, f2, f3, f4) - dz
        dp_ref[...] = dp[:t].astype(dp_ref.dtype)
        zb = z.astype(BF16)
        y = jnp.dot(zb, wv, preferred_element_type=F32)
        dsc = jnp.sum(dyc * y, axis=0, keepdims=True)
        dw = lax.dot_general(zb, dye[:t].astype(BF16), (TN, ((), ())), preferred_element_type=F32)

        @pl.when(i == 0)
        def _():
            dw_ref[...] = dw
            dsc_ref[...] = dsc

        @pl.when(i > 0)
        def _():
            dw_ref[...] += dw
            dsc_ref[...] += dsc

    dp, dw, dsc = pl.pallas_call(
        body, name="pool_bwd", grid=(nt,),
        out_shape=(jax.ShapeDtypeStruct(dbuf.shape, dbuf.dtype), jax.ShapeDtypeStruct((GROUP, GROUP), F32),
                   jax.ShapeDtypeStruct((1, GROUP), F32)),
        in_specs=[pl.BlockSpec((t, GROUP), lambda i: (i, col)),
                  pl.BlockSpec((16, GROUP), _prev_map(t, 16, col)),
                  pl.BlockSpec((GROUP, GROUP), lambda i: (0, 0)),
                  pl.BlockSpec((1, GROUP), lambda i: (0, 0)),
                  pl.BlockSpec((t, GROUP), lambda i: (i, 3)),
                  pl.BlockSpec((16, GROUP), _next_map(t, 16, 3, s)), ANY_SPEC],
        out_specs=(pl.BlockSpec((t, GROUP), lambda i: (i, col)), pl.BlockSpec((GROUP, GROUP), lambda i: (0, 0)),
                   pl.BlockSpec((1, GROUP), lambda i: (0, 0))),
        input_output_aliases={6: 0}, compiler_params=_params(),
    )(proj, proj, wbd, scale.reshape(1, GROUP), dy, dy, dbuf)
    return dp, dw, dsc.reshape(GROUP)


FF_HALO = 16


def _ffn_gate_fwd(u0, w):
    s = u0.shape[1]
    t = min(ROW_TILE, s)

    def body(a_ref, ap_ref, g_ref, gp_ref, wa_ref, wg_ref, o_ref):
        first = (pl.program_id(1) > 0).astype(F32)
        a = _conv3(jnp.concatenate([ap_ref[...] * first, a_ref[...].astype(F32)], axis=0), wa_ref[...])[FF_HALO:]
        g = _conv3(jnp.concatenate([gp_ref[...] * first, g_ref[...].astype(F32)], axis=0), wg_ref[...])[FF_HALO:]
        o_ref[...] = (a * (g * _sigmoid(g))).astype(o_ref.dtype)

    def cur(off):
        return pl.BlockSpec((None, t, FF_SHARD), lambda j, i: (j + off, i, 0))

    def prev(off):
        return pl.BlockSpec((None, FF_HALO, FF_SHARD),
                            lambda j, i: (j + off, jnp.maximum(i * (t // FF_HALO) - 1, 0), 0))

    def wspec(off):
        return pl.BlockSpec((None, 3, FF_SHARD), lambda j, i: (j + off, 0, 0))

    return pl.pallas_call(
        body, name="ffn_gate_fwd", grid=(FF_HALF, s // t),
        out_shape=jax.ShapeDtypeStruct((FF_HALF, s, FF_SHARD), BF16),
        in_specs=[cur(0), prev(0), cur(FF_HALF), prev(FF_HALF), wspec(0), wspec(FF_HALF)],
        out_specs=pl.BlockSpec((None, t, FF_SHARD), lambda j, i: (j, i, 0)), compiler_params=_params(),
    )(u0, u0, u0, u0, w, w)


def _ffn_gate_bwd(u0, w, dact):
    s = u0.shape[1]
    t = min(ROW_TILE, s)
    nt = s // t

    def body(c_ref, p_ref, n_ref, w_ref, d_ref, dn_ref, du_ref, dw_ref):
        i = pl.program_id(1)
        first = (i > 0).astype(F32)
        last = (i < nt - 1).astype(F32)
        dext = jnp.concatenate([jnp.zeros((FF_HALO, FF_SHARD), F32), d_ref[...].astype(F32), dn_ref[...] * last],
                               axis=0)
        ext = [jnp.concatenate([p_ref[n] * first, c_ref[n].astype(F32), n_ref[n] * last], axis=0) for n in range(2)]
        a = _conv3(ext[0], w_ref[0])
        g = _conv3(ext[1], w_ref[1])
        sg = _sigmoid(g)
        silu = g * sg
        dus = (dext * silu, dext * a * (sg + silu * (1.0 - sg)))
        mine = slice(FF_HALO, FF_HALO + t)
        for n in range(2):
            du_ref[n] = _conv3_t(dus[n], w_ref[n])[mine].astype(du_ref.dtype)
            dc = dus[n][mine]
            dw = jnp.concatenate([
                jnp.sum(dc * _shift_down(ext[n], 2)[mine], axis=0, keepdims=True),
                jnp.sum(dc * _shift_down(ext[n], 1)[mine], axis=0, keepdims=True),
                jnp.sum(dc * ext[n][mine], axis=0, keepdims=True),
                jnp.zeros((5, FF_SHARD), F32)], axis=0)

            @pl.when(i == 0)
            def _(n=n, dw=dw):
                dw_ref[n] = dw

            @pl.when(i > 0)
            def _(n=n, dw=dw):
                dw_ref[n] += dw

    def pair(rows, row_map):
        return pl.BlockSpec((2, None, rows, FF_SHARD), lambda j, i: (0, j, row_map(i), 0))

    prev_row = lambda i: jnp.maximum(i * (t // FF_HALO) - 1, 0)
    next_row = lambda i: jnp.minimum((i + 1) * (t // FF_HALO), s // FF_HALO - 1)
    u2 = u0.reshape(2, FF_HALF, s, FF_SHARD)
    du, dw = pl.pallas_call(
        body, name="ffn_gate_bwd", grid=(FF_HALF, nt),
        out_shape=(jax.ShapeDtypeStruct((2, FF_HALF, s, FF_SHARD), BF16),
                   jax.ShapeDtypeStruct((2, FF_HALF, 8, FF_SHARD), F32)),
        in_specs=[pair(t, lambda i: i), pair(FF_HALO, prev_row), pair(FF_HALO, next_row), pair(3, lambda i: 0),
                  pl.BlockSpec((None, t, FF_SHARD), lambda j, i: (j, i, 0)),
                  pl.BlockSpec((None, FF_HALO, FF_SHARD), lambda j, i: (j, next_row(i), 0))],
        out_specs=(pair(t, lambda i: i), pair(8, lambda i: 0)),
        compiler_params=_params(),
    )(u2, u2, u2, w.reshape(2, FF_HALF, 3, FF_SHARD), dact, dact)
    return du.reshape(2 * FF_HALF, s, FF_SHARD), dw.reshape(2 * FF_HALF, 8, FF_SHARD)[:, :3]


def _rope_tables(positions):
    inv_freq = ROPE_THETA ** (-jnp.arange(0, ROPE_DIM, 2, dtype=F32) / ROPE_DIM)
    ang = positions.astype(F32)[:, None] * inv_freq
    cos, sin = jnp.cos(ang), jnp.sin(ang)
    s = positions.shape[0]
    half = ROPE_DIM // 2
    rest = HEAD_DIM - ROPE_DIM
    ca = jnp.concatenate([cos, cos, jnp.ones((s, rest), F32)], axis=1)
    cb = jnp.concatenate([-sin, jnp.zeros((s, HEAD_DIM - half), F32)], axis=1)
    cc = jnp.concatenate([jnp.zeros((s, half), F32), sin, jnp.zeros((s, rest), F32)], axis=1)
    return tuple(jnp.tile(tb, (1, N_HEADS)) for tb in (ca, cb, cc))


QK_WIDE = 128
LANE_CQ, LANE_CK = 64, 67
KT_ROWS = 80


def _three_bf16(x):
    hi = x.astype(BF16).astype(F32)
    mid = (x - hi).astype(BF16).astype(F32)
    lo = (x - hi - mid).astype(BF16).astype(F32)
    return hi, mid, lo


def _heads_split(proj, col, tables, c, name):
    s = proj.shape[0]
    t = min(ROW_TILE, s)
    rope = tables is not None
    wide = c is not None
    width = QK_WIDE if wide else HEAD_DIM

    def body(*refs):
        x_ref = refs[0]
        q_ref, k_ref, v_ref, kt_ref, vt_ref = refs[-5:]
        xv = x_ref[...]
        parts = [xv[:, 0:GROUP], xv[:, GROUP:2 * GROUP], xv[:, 2 * GROUP:3 * GROUP]]
        if rope:
            ca, cb, cc = refs[1][...], refs[2][...], refs[3][...]
            for n in range(2):
                p = parts[n]
                parts[n] = p * ca + pltpu.roll(p, GROUP - 8, 1) * cb + pltpu.roll(p, 8, 1) * cc
        parts[0] = parts[0] * (HEAD_DIM ** -0.5)
        k_t, v_t = parts[1].T, parts[2].T
        ones_row = jnp.where(lax.broadcasted_iota(jnp.int32, (KT_ROWS - HEAD_DIM, t), 0) == 0, 1.0, 0.0)
        lane = lax.broadcasted_iota(jnp.int32, (t, QK_WIDE), 1)
        zeros = jnp.zeros((t, QK_WIDE - HEAD_DIM), F32)
        for h in range(N_HEADS):
            hs = slice(h * HEAD_DIM, (h + 1) * HEAD_DIM)
            qh, kh = parts[0][:, hs], parts[1][:, hs]
            if wide:
                terms = _three_bf16(refs[-6][:, h:h + 1])
                qh = jnp.concatenate([qh, zeros], axis=1)
                kh = jnp.concatenate([kh, zeros], axis=1)
                for n in range(3):
                    qh = jnp.where(lane == LANE_CQ + n, terms[n], jnp.where(lane == LANE_CK + n, 1.0, qh))
                    kh = jnp.where(lane == LANE_CK + n, -terms[n], jnp.where(lane == LANE_CQ + n, 1.0, kh))
            q_ref[h] = qh.astype(q_ref.dtype)
            k_ref[h] = kh.astype(k_ref.dtype)
            v_ref[h] = parts[2][:, hs].astype(v_ref.dtype)
            kt_ref[h] = jnp.concatenate([k_t[hs, :], ones_row], axis=0).astype(kt_ref.dtype)
            vt_ref[h] = v_t[hs, :].astype(vt_ref.dtype)

    tab = pl.BlockSpec((t, GROUP), lambda i: (i, 0))
    qk = pl.BlockSpec((N_HEADS, t, width), lambda i: (0, i, 0))
    heads = pl.BlockSpec((N_HEADS, t, HEAD_DIM), lambda i: (0, i, 0))
    heads_t = pl.BlockSpec((N_HEADS, HEAD_DIM, t), lambda i: (0, 0, i))
    qk_shape = jax.ShapeDtypeStruct((N_HEADS, s, width), BF16)
    return pl.pallas_call(
        body, name=name, grid=(s // t,),
        out_shape=(qk_shape, qk_shape, jax.ShapeDtypeStruct((N_HEADS, s, HEAD_DIM), BF16),
                   jax.ShapeDtypeStruct((N_HEADS, KT_ROWS, s), BF16),
                   jax.ShapeDtypeStruct((N_HEADS, HEAD_DIM, s), BF16)),
        in_specs=[pl.BlockSpec((t, 3 * GROUP), lambda i: (i, col))] + ([tab, tab, tab] if rope else [])
        + ([pl.BlockSpec((t, 128), lambda i: (i, 0))] if wide else []),
        out_specs=(qk, qk, heads, pl.BlockSpec((N_HEADS, KT_ROWS, t), lambda i: (0, 0, i)), heads_t),
        compiler_params=_params(),
    )(*((proj,) + (tuple(tables) if rope else ()) + ((c,) if wide else ())))


def _heads_merge(dqt, dk, dv, tables, name, dbuf, col):
    s = dv.shape[1]
    t = min(ROW_TILE, s)
    rope = tables is not None

    wide = dk.shape[2] == QK_WIDE

    def body(*refs):
        o_ref = refs[n_in + 1]
        dq = jnp.concatenate([refs[0][h, :HEAD_DIM, :] for h in range(N_HEADS)], axis=0).T
        parts = [dq] + [jnp.concatenate([r[h][:, :HEAD_DIM] for h in range(N_HEADS)], axis=1) for r in refs[1:3]]
        parts[0] = parts[0] * (HEAD_DIM ** -0.5)
        if rope:
            ca, cb, cc = refs[3][...], refs[4][...], refs[5][...]
            for n in range(2):
                p = parts[n]
                parts[n] = p * ca + pltpu.roll(p * cb, 8, 1) + pltpu.roll(p * cc, GROUP - 8, 1)
        o_ref[...] = jnp.concatenate(parts, axis=1).astype(o_ref.dtype)
        if wide:
            over_keys = jnp.concatenate([refs[0][h, HEAD_DIM:HEAD_DIM + 8, :] for h in range(N_HEADS)]
                                        + [jnp.zeros((128 - 8 * N_HEADS, t), F32)], axis=0).T
            lane = lax.broadcasted_iota(jnp.int32, (t, 128), 1)
            dc = jnp.zeros((t, 128), F32)
            for h in range(N_HEADS):
                dc = jnp.where(lane == h, over_keys[:, 8 * h:8 * h + 1] - refs[1][h][:, LANE_CK:LANE_CK + 1], dc)
            refs[n_in + 2][...] = dc

    tab = pl.BlockSpec((t, GROUP), lambda i: (i, 0))
    heads = pl.BlockSpec((N_HEADS, t, HEAD_DIM), lambda i: (0, i, 0))
    n_in = 6 if rope else 3
    dspec = pl.BlockSpec((t, 3 * GROUP), lambda i: (i, col))
    dshape = jax.ShapeDtypeStruct(dbuf.shape, dbuf.dtype)
    return pl.pallas_call(
        body, name=name, grid=(s // t,),
        out_shape=(dshape, jax.ShapeDtypeStruct((s, 128), F32)) if wide else dshape,
        in_specs=[pl.BlockSpec((N_HEADS, KT_ROWS, t), lambda i: (0, 0, i)),
                  pl.BlockSpec((N_HEADS, t, dk.shape[2]), lambda i: (0, i, 0)), heads]
        + ([tab, tab, tab] if rope else []) + [ANY_SPEC],
        out_specs=(dspec, pl.BlockSpec((t, 128), lambda i: (i, 0))) if wide else dspec,
        input_output_aliases={n_in: 0}, compiler_params=_params(),
    )(*((dqt, dk, dv) + (tuple(tables) if rope else ()) + (dbuf,)))


def _log_sigmoid(x):
    return jnp.minimum(x, 0.0) - jnp.log(1.0 + jnp.exp(-jnp.abs(x)))


def _scan_rows(x, reverse):
    n = x.shape[0]
    row = lax.broadcasted_iota(jnp.int32, x.shape, 0)
    k = 1
    while k < n:
        if reverse:
            x = x + jnp.where(row < n - k, _shift_up(x, k), 0.0)
        else:
            x = x + jnp.where(row >= k, _shift_down(x, k), 0.0)
        k *= 2
    return x


def _gate_cumsum(proj, bias):
    s = proj.shape[0]
    col = COL_GATE // 128

    def body(z_ref, b_ref, c_ref):
        c_ref[...] = _scan_rows(_log_sigmoid(z_ref[...] + b_ref[...]), False)

    return pl.pallas_call(
        body, name="gate_cumsum", grid=(1,), out_shape=jax.ShapeDtypeStruct((s, 128), F32),
        in_specs=[pl.BlockSpec((s, 128), lambda i: (0, col)), pl.BlockSpec((1, 128), lambda i: (0, 0))],
        out_specs=pl.BlockSpec((s, 128), lambda i: (0, 0)), compiler_params=_params(),
    )(proj, bias)


def _gate_cumsum_bwd(proj, bias, dc, dbuf):
    s = proj.shape[0]
    col = COL_GATE // 128

    def body(z_ref, b_ref, dc_ref, buf_ref, dz_ref, db_ref):
        dlogf = _scan_rows(dc_ref[...], True)
        dz = dlogf * _sigmoid(-(z_ref[...] + b_ref[...]))
        dz_ref[...] = dz.astype(dz_ref.dtype)
        db_ref[...] = jnp.sum(dz, axis=0, keepdims=True)

    return pl.pallas_call(
        body, name="gate_cumsum_bwd", grid=(1,),
        out_shape=(jax.ShapeDtypeStruct(dbuf.shape, dbuf.dtype), jax.ShapeDtypeStruct((1, 128), F32)),
        in_specs=[pl.BlockSpec((s, 128), lambda i: (0, col)), pl.BlockSpec((1, 128), lambda i: (0, 0)),
                  pl.BlockSpec((s, 128), lambda i: (0, 0)), ANY_SPEC],
        out_specs=(pl.BlockSpec((s, 128), lambda i: (0, col)), pl.BlockSpec((1, 128), lambda i: (0, 0))),
        input_output_aliases={3: 0}, compiler_params=_params(),
    )(proj, bias, dc, dbuf)


DIL_REACH = 2048


def _pair_weight(mode, d):
    if mode == "fox":
        return jnp.where(d >= 0, 1.0, 0.0)
    w1 = jnp.where(jnp.abs(d - 64) <= 64, 1.0, 0.0)
    w2 = jnp.where((d & 3) == 0, jnp.where(jnp.abs(d - 256) <= 256, 1.0, 0.0), 0.0)
    w3 = jnp.where((d & 15) == 0, jnp.where(jnp.abs(d - 1024) <= 1024, 1.0, 0.0), 0.0)
    return w1 + w2 + w3


def _bias_tables(mode, tq, tk):
    nb = 2 if mode == "fox" else DIL_REACH // tk + 1
    n = lax.broadcasted_iota(jnp.int32, (nb, tk, tq), 0)
    key = lax.broadcasted_iota(jnp.int32, (nb, tk, tq), 1)
    query = lax.broadcasted_iota(jnp.int32, (nb, tk, tq), 2)
    w = _pair_weight(mode, n * tk + query - key)
    return jnp.where(w > 0.0, jnp.log(jnp.maximum(w, 1.0)), NEG)


M_INIT = -1e29


def _first_key_chunk(mode, q0, tk):
    if mode == "fox":
        return 0
    return jnp.maximum(q0 - DIL_REACH, 0) // tk


def _attention_fwd(mode, q, k, vt, tab_t, ybuf, col):
    s, width = q.shape[1], q.shape[2]
    tq = min(ATT_TQ, s)
    tk = tq
    nb = tab_t.shape[0]

    def body(q_ref, k_ref, vt_ref, tab_ref, buf_ref, y_ref, o_ref, lse_ref):
        i = pl.program_id(0)
        lo = _first_key_chunk(mode, i * tq, tk)

        def step(c, carry):
            k0 = pl.multiple_of(c * tk, tk)
            tab = tab_ref[jnp.minimum(i - c, nb - 1)]
            scores = [lax.dot_general(k_ref[h, pl.ds(k0, tk), :], q_ref[h], (NT, ((), ())),
                                      preferred_element_type=F32) for h in range(N_HEADS)]
            stats, probs = [], []
            for h in range(N_HEADS):
                m, l = carry[3 * h:3 * h + 2]
                sc = scores[h] + tab
                m_new = jnp.maximum(m, jnp.max(sc, axis=0, keepdims=True))
                alpha = jnp.exp(m - m_new)
                p = jnp.exp(sc - m_new)
                stats.append((m_new, alpha * l + jnp.sum(p, axis=0, keepdims=True), alpha))
                probs.append(p.astype(BF16))
            pv = [jnp.dot(vt_ref[h, :, pl.ds(k0, tk)], probs[h], preferred_element_type=F32) for h in range(N_HEADS)]
            new = []
            for h in range(N_HEADS):
                m_new, l, alpha = stats[h]
                new += [m_new, l, alpha * carry[3 * h + 2] + pv[h]]
            return tuple(new)

        start = (jnp.full((1, tq), M_INIT, F32), jnp.zeros((1, tq), F32), jnp.zeros((HEAD_DIM, tq), F32))
        done = lax.fori_loop(lo, i + 1, step, start * N_HEADS)
        outs = []
        for h in range(N_HEADS):
            m, l, acc = done[3 * h:3 * h + 3]
            outs.append(acc / l)
            lse_ref[h] = m + jnp.log(l)
        out = jnp.concatenate(outs, axis=0).T
        y_ref[...] = out.astype(y_ref.dtype)
        o_ref[...] = out

    rowspec = pl.BlockSpec((N_HEADS, 1, tq), lambda i: (0, 0, i))
    return pl.pallas_call(
        body, name="attention_fwd_" + mode, grid=(s // tq,),
        out_shape=(jax.ShapeDtypeStruct(ybuf.shape, ybuf.dtype), jax.ShapeDtypeStruct((s, GROUP), F32),
                   jax.ShapeDtypeStruct((N_HEADS, 1, s), F32)),
        in_specs=[pl.BlockSpec((N_HEADS, tq, width), lambda i: (0, i, 0)),
                  pl.BlockSpec((N_HEADS, s, width), lambda i: (0, 0, 0)),
                  pl.BlockSpec((N_HEADS, HEAD_DIM, s), lambda i: (0, 0, 0)),
                  pl.BlockSpec((nb, tk, tq), lambda i: (0, 0, 0)), ANY_SPEC],
        out_specs=(pl.BlockSpec((tq, GROUP), lambda i: (i, col)), pl.BlockSpec((tq, GROUP), lambda i: (i, 0)),
                   rowspec),
        input_output_aliases={4: 0}, compiler_params=_params(),
    )(q, k, vt, tab_t, ybuf)


def _attention_delta(o, do, col):
    s = o.shape[0]
    t = min(ROW_TILE, s)

    def body(o_ref, do_ref, delta_ref, dob_ref):
        dov = do_ref[...]
        prod_t = (o_ref[...] * dov).T
        for h in range(N_HEADS):
            hs = slice(h * HEAD_DIM, (h + 1) * HEAD_DIM)
            delta_ref[h] = jnp.sum(prod_t[hs, :], axis=0, keepdims=True)
            dob_ref[h] = dov[:, hs].astype(dob_ref.dtype)

    return pl.pallas_call(
        body, name="attention_delta", grid=(s // t,),
        out_shape=(jax.ShapeDtypeStruct((N_HEADS, 1, s), F32), jax.ShapeDtypeStruct((N_HEADS, s, HEAD_DIM), BF16)),
        in_specs=[pl.BlockSpec((t, GROUP), lambda i: (i, 0)), pl.BlockSpec((t, GROUP), lambda i: (i, col))],
        out_specs=(pl.BlockSpec((N_HEADS, 1, t), lambda i: (0, 0, i)),
                   pl.BlockSpec((N_HEADS, t, HEAD_DIM), lambda i: (0, i, 0))),
        compiler_params=_params(),
    )(o, do)


def _attention_bwd(mode, q, k, v, kt, tab_t, dob, lse, delta):
    s, width = q.shape[1], q.shape[2]
    tq = min(ATT_TQ, s)
    tk = tq
    nq = s // tq
    nb = tab_t.shape[0]

    def body(q_ref, k_ref, v_ref, kt_ref, tab_ref, dob_ref, lse_ref, delta_ref, dqt_ref, dk_ref, dv_ref):
        i = pl.program_id(0)

        @pl.when(i == 0)
        def _():
            dqt_ref[...] = jnp.zeros_like(dqt_ref)

        hi = nq if mode == "fox" else jnp.minimum((i * tk + tk - 1 + DIL_REACH) // tq + 1, nq)
        for h0 in range(0, N_HEADS, BWD_HEADS):
            heads = range(h0, h0 + BWD_HEADS)

            def step(c, carry, heads=heads):
                q0 = pl.multiple_of(c * tq, tq)
                qs = pl.ds(q0, tq)
                tab = tab_ref[jnp.minimum(c - i, nb - 1)]
                qv = [q_ref[h, qs, :] for h in heads]
                dov = [dob_ref[h, qs, :] for h in heads]
                sc = [lax.dot_general(k_ref[h], qv[n], (NT, ((), ())), preferred_element_type=F32)
                      for n, h in enumerate(heads)]
                dp = [lax.dot_general(v_ref[h], dov[n], (NT, ((), ())), preferred_element_type=F32)
                      for n, h in enumerate(heads)]
                pb, dsb = [], []
                for n, h in enumerate(heads):
                    p = jnp.exp(sc[n] + tab - lse_ref[h, :, qs])
                    pb.append(p.astype(BF16))
                    dsb.append((p * (dp[n] - delta_ref[h, :, qs])).astype(BF16))
                new = []
                for n, h in enumerate(heads):
                    new += [carry[2 * n] + jnp.dot(dsb[n], qv[n], preferred_element_type=F32),
                            carry[2 * n + 1] + jnp.dot(pb[n], dov[n], preferred_element_type=F32)]
                for n, h in enumerate(heads):
                    dqt_ref[h, :, qs] += jnp.dot(kt_ref[h], dsb[n], preferred_element_type=F32)
                return tuple(new)

            start = (jnp.zeros((tk, width), F32), jnp.zeros((tk, HEAD_DIM), F32))
            done = lax.fori_loop(i, hi, step, start * BWD_HEADS)
            for n, h in enumerate(heads):
                dk_ref[h] = done[2 * n]
                dv_ref[h] = done[2 * n + 1]

    def full(shape):
        return pl.BlockSpec(shape, lambda i: (0, 0, 0))

    kblk = pl.BlockSpec((N_HEADS, tk, width), lambda i: (0, i, 0))
    vblk = pl.BlockSpec((N_HEADS, tk, HEAD_DIM), lambda i: (0, i, 0))
    return pl.pallas_call(
        body, name="attention_bwd_" + mode, grid=(s // tk,),
        out_shape=(jax.ShapeDtypeStruct((N_HEADS, KT_ROWS, s), F32), jax.ShapeDtypeStruct((N_HEADS, s, width), F32),
                   jax.ShapeDtypeStruct((N_HEADS, s, HEAD_DIM), F32)),
        in_specs=[full((N_HEADS, s, width)), kblk, vblk, pl.BlockSpec((N_HEADS, KT_ROWS, tk), lambda i: (0, 0, i)),
                  full((nb, tk, tq)), full((N_HEADS, s, HEAD_DIM)), full((N_HEADS, 1, s)), full((N_HEADS, 1, s))],
        out_specs=(full((N_HEADS, KT_ROWS, s)), kblk, vblk),
        compiler_params=_params(),
    )(q, k, v, kt, tab_t, dob, lse, delta)


def _xattn_fwd(qx, kvm):
    s = qx.shape[0]
    t = min(ROW_TILE, s)

    def body(q_ref, kv_ref, o_ref):
        heads = range(XA_HEADS)
        sc = [lax.dot_general(q_ref[:, h * XA_DIM:(h + 1) * XA_DIM].astype(BF16), kv_ref[h].astype(BF16),
                              (NT, ((), ())), preferred_element_type=F32) * (XA_DIM ** -0.5) for h in heads]
        probs = []
        for h in heads:
            e = jnp.exp(sc[h] - jnp.max(sc[h], axis=-1, keepdims=True))
            probs.append((e / jnp.sum(e, axis=-1, keepdims=True)).astype(BF16))
        outs = [jnp.dot(probs[h], kv_ref[XA_HEADS + h].astype(BF16), preferred_element_type=F32) for h in heads]
        for h in heads:
            o_ref[:, h * XA_DIM:(h + 1) * XA_DIM] = outs[h].astype(o_ref.dtype)

    return pl.pallas_call(
        body, name="xattn_fwd", grid=(s // t,), out_shape=jax.ShapeDtypeStruct((s, D_MODEL), BF16),
        in_specs=[pl.BlockSpec((t, D_MODEL), lambda i: (i, 0)),
                  pl.BlockSpec((2 * XA_HEADS, MEM_LEN, XA_DIM), lambda i: (0, 0, 0))],
        out_specs=pl.BlockSpec((t, D_MODEL), lambda i: (i, 0)), compiler_params=_params(),
    )(qx, kvm)


def _xattn_bwd(qx, kvm, do):
    s = qx.shape[0]
    t = min(ROW_TILE, s)

    def body(q_ref, kv_ref, do_ref, dq_ref, dkv_ref):
        i = pl.program_id(0)
        heads = range(XA_HEADS)
        qv = [q_ref[:, h * XA_DIM:(h + 1) * XA_DIM].astype(BF16) for h in heads]
        dov = [do_ref[:, h * XA_DIM:(h + 1) * XA_DIM].astype(BF16) for h in heads]
        kv = [kv_ref[h].astype(BF16) for h in heads]
        sc = [lax.dot_general(qv[h], kv[h], (NT, ((), ())), preferred_element_type=F32) * (XA_DIM ** -0.5)
              for h in heads]
        dp = [lax.dot_general(dov[h], kv_ref[XA_HEADS + h].astype(BF16), (NT, ((), ())), preferred_element_type=F32)
              for h in heads]
        pb, ds = [], []
        for h in heads:
            e = jnp.exp(sc[h] - jnp.max(sc[h], axis=-1, keepdims=True))
            p = e / jnp.sum(e, axis=-1, keepdims=True)
            pb.append(p.astype(BF16))
            ds.append((p * (dp[h] - jnp.sum(p * dp[h], axis=-1, keepdims=True)) * (XA_DIM ** -0.5)).astype(BF16))
        dq = [jnp.dot(ds[h], kv[h], preferred_element_type=F32) for h in heads]
        dk = [lax.dot_general(ds[h], qv[h], (TN, ((), ())), preferred_element_type=F32) for h in heads]
        dv = [lax.dot_general(pb[h], dov[h], (TN, ((), ())), preferred_element_type=F32) for h in heads]
        for h in heads:
            dq_ref[:, h * XA_DIM:(h + 1) * XA_DIM] = dq[h].astype(dq_ref.dtype)

        @pl.when(i == 0)
        def _():
            for h in heads:
                dkv_ref[h] = dk[h]
                dkv_ref[XA_HEADS + h] = dv[h]

        @pl.when(i > 0)
        def _():
            for h in heads:
                dkv_ref[h] += dk[h]
                dkv_ref[XA_HEADS + h] += dv[h]

    row = pl.BlockSpec((t, D_MODEL), lambda i: (i, 0))
    kvs = pl.BlockSpec((2 * XA_HEADS, MEM_LEN, XA_DIM), lambda i: (0, 0, 0))
    return pl.pallas_call(
        body, name="xattn_bwd", grid=(s // t,),
        out_shape=(jax.ShapeDtypeStruct((s, D_MODEL), BF16),
                   jax.ShapeDtypeStruct((2 * XA_HEADS, MEM_LEN, XA_DIM), F32)),
        in_specs=[row, kvs, row], out_specs=(row, kvs), compiler_params=_params(),
    )(qx, kvm, do)


def _adamw(parts, owns, me, w, m, v, name):
    nl, r, c = w.shape
    tr = r
    for cand in (512, 352, 256, 176, 128, 64, 32, 16, 8):
        if r % cand == 0 and r > cand and N_DEV * cand * c * 4 <= ADAMW_BLOCK_BYTES:
            tr = cand
            break
    nt = r // tr
    per_layer = N_DEV + (1 if owns is not None else 0)

    def body(me_ref, *refs):
        w_ref, m_ref, v_ref, g_ref, d_ref, nm_ref, nv_ref = refs[nl * per_layer:]
        layer = pl.program_id(0)
        g = None
        for l in range(nl):
            p_refs = refs[l * per_layer:(l + 1) * per_layer]
            gl = None
            for d in range(N_DEV):
                term = p_refs[d][...].astype(F32)
                if owns is not None:
                    term = jnp.where(me_ref[0] == d, p_refs[N_DEV][...].astype(F32), term)
                gl = term if gl is None else gl + term
            g = gl if g is None else jnp.where(layer == l, gl, g)
        mn = ADAM_B1 * m_ref[...] + (1.0 - ADAM_B1) * g
        vn = ADAM_B2 * v_ref[...] + (1.0 - ADAM_B2) * (g * g)
        m_hat = mn / (1.0 - ADAM_B1 ** ADAM_STEP)
        v_hat = vn / (1.0 - ADAM_B2 ** ADAM_STEP)
        g_ref[...] = g
        d_ref[...] = -ADAM_LR * (m_hat / (jnp.sqrt(v_hat) + ADAM_EPS) + ADAM_WD * w_ref[...])
        nm_ref[...] = mn
        nv_ref[...] = vn

    def rows(l, ll, i):
        return jnp.where(ll == l, i, jnp.where(ll < l, 0, nt - 1))

    def part_spec(l, d):
        if owns is None:
            return pl.BlockSpec((None, tr, c), lambda ll, i, me_ref: (d, rows(l, ll, i), 0))
        return pl.BlockSpec((None, tr, c),
                            lambda ll, i, me_ref: (jnp.where(me_ref[0] == d, (d + 1) % N_DEV, d), rows(l, ll, i), 0))

    def own_spec(l):
        return pl.BlockSpec((None, tr, c), lambda ll, i, me_ref: (me_ref[0], rows(l, ll, i), 0))

    in_specs, operands = [], []
    for l in range(nl):
        in_specs += [part_spec(l, d) for d in range(N_DEV)]
        operands += [parts[l]] * N_DEV
        if owns is not None:
            in_specs.append(own_spec(l))
            operands.append(owns[l])
    blk = pl.BlockSpec((None, tr, c), lambda ll, i, me_ref: (ll, i, 0))
    shp = jax.ShapeDtypeStruct((nl, r, c), F32)
    return pl.pallas_call(
        body, name=name, out_shape=(shp, shp, shp, shp),
        grid_spec=pltpu.PrefetchScalarGridSpec(
            num_scalar_prefetch=1, grid=(nl, nt), in_specs=in_specs + [blk, blk, blk],
            out_specs=(blk, blk, blk, blk)),
        compiler_params=_params(),
    )(me.reshape(1), *operands, w, m, v)


GROUPS = {"in": ("w_in",), "rest": ("w_out", "w_xq", "w_xo", "w_xkv", "w_up", "w_down")}
FULL_SHAPES = {"w_in": (D_MODEL, N_IN_PAD), "w_out": (D_MODEL, D_MODEL), "w_xq": (D_MODEL, D_MODEL),
               "w_xo": (D_MODEL, D_MODEL), "w_xkv": (N_DEV, D_MODEL, 2 * D_MODEL // N_DEV),
               "w_up": (N_DEV, FF_SHARD, D_MODEL), "w_down": (FF_HALF, FF_SHARD, D_MODEL)}
PIECE_SHAPES = {"w_in": (N_DEV, D_MODEL // N_DEV, N_IN_PAD), "w_out": (N_DEV, D_MODEL // N_DEV, D_MODEL),
                "w_xq": (N_DEV, D_MODEL // N_DEV, D_MODEL), "w_xo": (N_DEV, D_MODEL // N_DEV, D_MODEL),
                "w_xkv": (N_DEV, D_MODEL, 2 * D_MODEL // N_DEV), "w_up": (N_DEV, FF_SHARD, D_MODEL),
                "w_down": (N_DEV, D_FF // N_DEV, D_MODEL)}
CONV_WORDS = 8192


class _GatheredWeights:
    def __init__(self, states, layer):
        self.states, self.layer, self.full, self.extra = dict(states), layer, {}, None

    def need(self, group, after):
        if group in self.states:
            got, _ = _exchange_wait(self.states.pop(group), after, "gather_%s_wait_%d" % (group, self.layer))
            for name, g in zip(GROUPS[group], got):
                self.full[name] = g.reshape(FULL_SHAPES[name])
            self.extra = got[len(GROUPS[group]):]

    def __getitem__(self, name):
        return self.full[name]


def _relay_in_cols(w):
    pad = jnp.zeros(w.shape[:-1] + (N_IN_PAD - N_IN,), w.dtype)
    return jnp.concatenate([w[..., :2304], w[..., 2308:N_IN], w[..., 2304:2308], pad], axis=-1)


def _unrelay_in_cols(w):
    return jnp.concatenate([w[..., :2304], w[..., COL_GATE:COL_GATE + 4], w[..., 2304:COL_GATE]], axis=-1)


def _layer_fwd(h, memv, w, sm, tables, rest_arrived):
    sv = {"h0": h}
    s = h.shape[0]
    tm, tb = min(SLAB_TILE, s), min(MM_TILE, s)
    w.need("in", h)
    tn = N_IN_PAD // 3
    proj, xn = _norm_matmul(h, sm["g_mix"], w["w_in"], (s, N_IN_PAD), grid=(s // tb, 3),
                            b_spec=pl.BlockSpec((D_MODEL, tn), lambda i, j: (0, j)),
                            o_spec=pl.BlockSpec((tb, tn), lambda i, j: (i, j)), name="norm_mm_in")
    sv["xn"], sv["proj"] = xn, proj
    ycat = _sconv_fwd(proj, sm["w_sconv"])
    qd, kd, vd, ktd, vtd = _heads_split(proj, 1, tables["rope"], None, "split_dil")
    ycat, ob, lse_b = _attention_fwd("dil", qd, kd, vtd, tables["dil"], ycat, 1)
    sv["dil"] = (qd, kd, vd, ktd, ob, lse_b)
    c = _gate_cumsum(proj, sm["b_forget_pad"])
    qf, kf, vf, ktf, vtf = _heads_split(proj, 2, None, c, "split_fox")
    ycat, oc, lse_c = _attention_fwd("fox", qf, kf, vtf, tables["fox"], ycat, 2)
    sv["fox"] = (qf, kf, vf, ktf, oc, lse_c)
    ycat = _pool_fwd(proj, sm["w_pool_bd"], sm["pool_scale"], ycat)
    sv["ycat"] = ycat
    w.need("rest", ycat)
    h1 = _mm_nn(ycat, w["w_out"], "mm_out", res=h)
    sv["h1"] = h1
    memn = _rms_fwd(memv, sm["g_mem"], "rms_mem")
    qx, xq = _norm_matmul(h1, sm["g_xa"] + rest_arrived(w["w_out"]), w["w_xq"], (s, D_MODEL), grid=(s // tb, 1),
                          b_spec=pl.BlockSpec((D_MODEL, D_MODEL), lambda i, j: (0, 0)),
                          o_spec=pl.BlockSpec((tb, D_MODEL), lambda i, j: (i, 0)), name="norm_mm_xq",
                          out_dtype=BF16)
    kvm = _matmul(memn, w["w_xkv"], (N_DEV, MEM_LEN, XA_DIM), grid=(N_DEV, 1, 1),
                  a_spec=pl.BlockSpec((MEM_LEN, D_MODEL), lambda i, j, r: (0, 0)),
                  b_spec=pl.BlockSpec((None, D_MODEL, XA_DIM), lambda i, j, r: (i, 0, 0)),
                  o_spec=pl.BlockSpec((None, MEM_LEN, XA_DIM), lambda i, j, r: (i, 0, 0)),
                  dims=NN, nred=1, name="mm_xkv")
    ox = _xattn_fwd(qx, kvm)
    sv.update(xq=xq, memn=memn, qx=qx, kvm=kvm, ox=ox)
    h2 = _mm_nn(ox, w["w_xo"], "mm_xo", res=h1)
    sv["h2"] = h2
    u0, xf = _norm_matmul(h2, sm["g_ffn"], w["w_up"], (N_DEV, s, FF_SHARD), grid=(s // tb, N_DEV),
                          b_spec=pl.BlockSpec((None, FF_SHARD, D_MODEL), lambda i, j: (j, 0, 0)),
                          o_spec=pl.BlockSpec((None, tb, FF_SHARD), lambda i, j: (j, i, 0)), name="norm_mm_up",
                          out_dtype=BF16, dims=NT)
    act = _ffn_gate_fwd(u0, sm["w_ffconv"])
    sv.update(xf=xf, u0=u0, act=act)
    ospec = pl.BlockSpec((tm, D_MODEL), lambda i, j, r: (i, 0))
    h3 = _matmul(act, w["w_down"], (s, D_MODEL), grid=(s // tm, 1, 1),
                 a_spec=pl.BlockSpec((FF_HALF, tm, FF_SHARD), lambda i, j, r: (0, i, 0)),
                 b_spec=pl.BlockSpec((FF_HALF, FF_SHARD, D_MODEL), lambda i, j, r: (0, 0, 0)),
                 o_spec=ospec, dims=NN, nred=1, slabs=FF_HALF, name="mm_down", res=h2, res_spec=ospec)
    return h3, sv


def _layer_bwd(dh3, memv, w, sm, tables, sv, rest_ready, in_ready):
    s = dh3.shape[0]
    tm, tb = min(ROW_TILE, s), min(MM_TILE, s)
    big, small = {}, {}
    ts = max(s // 2, 1)
    dact = _matmul(dh3, w["w_down"], (FF_HALF, s, FF_SHARD), grid=(s // tb, FF_HALF, 1),
                   a_spec=pl.BlockSpec((tb, D_MODEL), lambda i, j, r: (i, 0)),
                   b_spec=pl.BlockSpec((None, FF_SHARD, D_MODEL), lambda i, j, r: (j, 0, 0)),
                   o_spec=pl.BlockSpec((None, tb, FF_SHARD), lambda i, j, r: (j, i, 0)),
                   dims=NT, nred=1, name="mm_dact", out_dtype=BF16)
    big["w_down"] = _matmul(sv["act"], dh3, (FF_HALF, FF_SHARD, D_MODEL), grid=(FF_HALF, 1, s // ts),
                            a_spec=pl.BlockSpec((None, ts, FF_SHARD), lambda i, j, r: (i, r, 0)),
                            b_spec=pl.BlockSpec((ts, D_MODEL), lambda i, j, r: (r, 0)),
                            o_spec=pl.BlockSpec((None, FF_SHARD, D_MODEL), lambda i, j, r: (i, 0, 0)),
                            dims=TN, nred=s // ts, name="mm_dw_down", out_dtype=GRAD_DTYPE)
    du0, small["w_ffconv"] = _ffn_gate_bwd(sv["u0"], sm["w_ffconv"], dact)
    dh2, small["g_ffn"] = _matmul_rms_bwd(du0, w["w_up"], sv["h2"], sm["g_ffn"], dh3, "mm_dxf_rms_bwd",
                                          tm=SLAB_TILE // 2, dims=NN)
    big["w_up"] = _matmul(du0, sv["xf"], (N_DEV, FF_SHARD, D_MODEL), grid=(N_DEV, 1, 1),
                          a_spec=pl.BlockSpec((None, s, FF_SHARD), lambda i, j, r: (i, 0, 0)),
                          b_spec=pl.BlockSpec((s, D_MODEL), lambda i, j, r: (0, 0)),
                          o_spec=pl.BlockSpec((None, FF_SHARD, D_MODEL), lambda i, j, r: (i, 0, 0)),
                          dims=TN, nred=1, name="mm_dw_up", out_dtype=GRAD_DTYPE)
    dox = _mm_nt(dh2, w["w_xo"], "mm_dox", out_dtype=BF16)
    big["w_xo"] = _mm_tn(sv["ox"], dh2, "mm_dw_xo")
    dqx, dkvm = _xattn_bwd(sv["qx"], sv["kvm"], dox)
    big["w_xq"] = _mm_tn(sv["xq"], dqx, "mm_dw_xq")
    big["w_xkv"] = _matmul(sv["memn"], dkvm, (N_DEV, D_MODEL, XA_DIM), grid=(N_DEV, 1, 1),
                           a_spec=pl.BlockSpec((MEM_LEN, D_MODEL), lambda i, j, r: (0, 0)),
                           b_spec=pl.BlockSpec((None, MEM_LEN, XA_DIM), lambda i, j, r: (i, 0, 0)),
                           o_spec=pl.BlockSpec((None, D_MODEL, XA_DIM), lambda i, j, r: (i, 0, 0)),
                           dims=TN, nred=1, name="mm_dw_xkv", out_dtype=GRAD_DTYPE)
    dmemn = _matmul(dkvm, w["w_xkv"], (MEM_LEN, D_MODEL), grid=(1, 1, 1),
                    a_spec=pl.BlockSpec((N_DEV, MEM_LEN, XA_DIM), lambda i, j, r: (0, 0, 0)),
                    b_spec=pl.BlockSpec((N_DEV, D_MODEL, XA_DIM), lambda i, j, r: (0, 0, 0)),
                    o_spec=pl.BlockSpec((MEM_LEN, D_MODEL), lambda i, j, r: (0, 0)),
                    dims=NT, nred=1, slabs=N_DEV, name="mm_dmemn")
    _, small["g_mem"] = _rms_bwd(dmemn, memv, sm["g_mem"], None, "rms_mem_bwd")
    dh1, small["g_xa"] = _matmul_rms_bwd(dqx, w["w_xq"], sv["h1"], sm["g_xa"], dh2, "mm_dxq_rms_bwd")
    big["w_out"] = _mm_tn(sv["ycat"], dh1, "mm_dw_out")
    dycat = _mm_nt(dh1, w["w_out"] + rest_ready(big, small).astype(BF16), "mm_dycat")
    proj = sv["proj"]
    dproj, small["w_sconv"] = _sconv_bwd(proj, sm["w_sconv"], dycat)
    qd, kd, vd, ktd, ob, lse_b = sv["dil"]
    delta, dob = _attention_delta(ob, dycat, 1)
    dqt, dk, dv = _attention_bwd("dil", qd, kd, vd, ktd, tables["dil"], dob, lse_b, delta)
    dproj = _heads_merge(dqt, dk, dv, tables["rope"], "merge_dil", dproj, 1)
    qf, kf, vf, ktf, oc, lse_c = sv["fox"]
    delta, dob = _attention_delta(oc, dycat, 2)
    dqt, dk, dv = _attention_bwd("fox", qf, kf, vf, ktf, tables["fox"], dob, lse_c, delta)
    dproj, dc = _heads_merge(dqt, dk, dv, None, "merge_fox", dproj, 2)
    dproj, dbias = _gate_cumsum_bwd(proj, sm["b_forget_pad"], dc, dproj)
    small["b_forget"] = dbias[0, :N_HEADS]
    dproj, dwbd, small["pool_scale"] = _pool_bwd(proj, sm["w_pool_bd"], sm["pool_scale"], dycat, dproj)
    small["w_pool"] = jnp.stack([dwbd[64 * g:64 * (g + 1), 64 * g:64 * (g + 1)] for g in range(4)])
    big["w_in"] = _mm_tn(sv["xn"], dproj, "mm_dw_in", tn=896)
    dh0, small["g_mix"] = _matmul_rms_bwd(dproj, w["w_in"], sv["h0"], sm["g_mix"] + in_ready(big, small), dh1,
                                          "mm_dxn_rms_bwd")
    return dh0, big, small


SMALL_NAMES = ("g_mix", "b_forget", "w_pool", "pool_scale", "g_xa", "g_mem", "g_ffn", "w_sconv", "w_ffconv")
SMALL_WITH = {"rest": ("w_ffconv", "g_ffn", "g_mem", "g_xa"),
              "in": ("w_sconv", "b_forget", "pool_scale", "w_pool")}
SMALL_SHAPES = {"w_sconv": (3, GROUP), "w_ffconv": (N_DEV, 3, FF_SHARD)}
WEIGHT_NAMES = ("g_mix", "w_in", "b_forget", "w_sconv", "w_pool", "pool_scale", "w_out", "g_xa", "g_mem", "w_xq",
                "w_xkv", "w_xo", "g_ffn", "w_up", "w_ffconv", "w_down", "g_final")


def _block_diag(w_pool):
    z = jnp.zeros((64, 64), F32)
    return jnp.concatenate(
        [jnp.concatenate([w_pool[g] if c == g else z for c in range(4)], axis=1) for g in range(4)], axis=0)


def kernel(x, mem, positions, g_mix, w_in, b_forget, w_sconv, w_pool, pool_scale, w_out, g_xa, g_mem, w_xq, w_xkv, w_xo, g_ffn, w_up, w_ffconv, w_down, g_final, loss_target, m_g_mix, m_w_in, m_b_forget, m_w_sconv, m_w_pool, m_pool_scale, m_w_out, m_g_xa, m_g_mem, m_w_xq, m_w_xkv, m_w_xo, m_g_ffn, m_w_up, m_w_ffconv, m_w_down, m_g_final, v_g_mix, v_w_in, v_b_forget, v_w_sconv, v_w_pool, v_pool_scale, v_w_out, v_g_xa, v_g_mem, v_w_xq, v_w_xkv, v_w_xo, v_g_ffn, v_w_up, v_w_ffconv, v_w_down, v_g_final):
    weights = dict(g_mix=g_mix, w_in=w_in, b_forget=b_forget, w_sconv=w_sconv, w_pool=w_pool, pool_scale=pool_scale,
                   w_out=w_out, g_xa=g_xa, g_mem=g_mem, w_xq=w_xq, w_xkv=w_xkv, w_xo=w_xo, g_ffn=g_ffn, w_up=w_up,
                   w_ffconv=w_ffconv, w_down=w_down, g_final=g_final)
    m_in = dict(g_mix=m_g_mix, w_in=m_w_in, b_forget=m_b_forget, w_sconv=m_w_sconv, w_pool=m_w_pool,
                pool_scale=m_pool_scale, w_out=m_w_out, g_xa=m_g_xa, g_mem=m_g_mem, w_xq=m_w_xq, w_xkv=m_w_xkv,
                w_xo=m_w_xo, g_ffn=m_g_ffn, w_up=m_w_up, w_ffconv=m_w_ffconv, w_down=m_w_down, g_final=m_g_final)
    v_in = dict(g_mix=v_g_mix, w_in=v_w_in, b_forget=v_b_forget, w_sconv=v_w_sconv, w_pool=v_w_pool,
                pool_scale=v_pool_scale, w_out=v_w_out, g_xa=v_g_xa, g_mem=v_g_mem, w_xq=v_w_xq, w_xkv=v_w_xkv,
                w_xo=v_w_xo, g_ffn=v_g_ffn, w_up=v_w_up, w_ffconv=v_w_ffconv, w_down=v_w_down, g_final=v_g_final)
    depth = w_in.shape[0]
    me = 4 * lax.axis_index("x") + 2 * lax.axis_index("y") + lax.axis_index("c")
    h = x[0]
    memv = mem[0]
    s = h.shape[0]
    tq = min(ATT_TQ, s)
    tables = {"rope": _rope_tables(positions[0]), "dil": _bias_tables("dil", tq, tq),
              "fox": _bias_tables("fox", tq, tq)}

    w_in_r = _relay_in_cols(w_in)
    conv_shard = jnp.concatenate([w_sconv.reshape(-1), w_ffconv.reshape(-1)])
    conv_shard = jnp.concatenate([conv_shard, jnp.zeros((CONV_WORDS - conv_shard.shape[0],), F32)])
    shards = dict(w_in=w_in_r, w_out=w_out, w_xq=w_xq, w_xo=w_xo, w_xkv=w_xkv, w_up=w_up.transpose(0, 2, 1),
                  w_down=w_down)
    gathered = [None] * depth

    def start_gathers(l, after):
        states = {}
        order = jnp.zeros((), F32)
        for group in ("in", "rest"):
            first = GROUPS[group][0]
            shards[first] = shards[first] + order
            xs = [_place_shard(shards[name], l, me, BF16, "place_%s_%d" % (name, l), after)
                  for name in GROUPS[group]]
            if l == 0 and group == "in":
                xs.append(_place_shard(conv_shard.reshape(1, CONV_WORDS // 1024, 1024), 0, me, F32, "place_conv"))
            states[group], token = _exchange_start(xs, False, "gather_%s_start_%d" % (group, l))
            order = order + token[0, 0]
        gathered[l] = _GatheredWeights(states, l)
        return order

    order = start_gathers(0, None)
    gathered[0].need("in", tables["rope"][0])
    conv_all = gathered[0].extra[0].reshape(N_DEV, CONV_WORDS)
    n_sc = depth * 3 * (GROUP // N_DEV)
    sconv_full = conv_all[:, :n_sc].reshape(N_DEV, depth, 3, GROUP // N_DEV).transpose(1, 2, 0, 3).reshape(
        depth, 3, GROUP)
    ffconv_full = conv_all[:, n_sc:n_sc + depth * 3 * FF_SHARD].reshape(N_DEV, depth, 3, FF_SHARD).transpose(
        1, 0, 2, 3)

    smalls = []
    for l in range(depth):
        smalls.append(dict(
            g_mix=g_mix[l], g_xa=g_xa[l], g_mem=g_mem[l], g_ffn=g_ffn[l], pool_scale=pool_scale[l],
            w_pool_bd=_block_diag(w_pool[l]), w_sconv=sconv_full[l], w_ffconv=ffconv_full[l],
            b_forget_pad=jnp.concatenate([b_forget[l], jnp.zeros((128 - N_HEADS,), F32)]).reshape(1, 128)))
    smalls[0]["g_mix"] = smalls[0]["g_mix"] + order

    saved = []
    for l in range(depth):
        def rest_arrived(arrived, l=l):
            return start_gathers(l + 1, arrived) if l + 1 < depth else jnp.zeros((), F32)

        h, sv = _layer_fwd(h, memv, gathered[l], smalls[l], tables, rest_arrived)
        saved.append(sv)
    loss_part, dh, dg_final = _loss_head(h, g_final, loss_target[0])
    loss = lax.psum(loss_part[0, 0], MESH_AXES)

    small_grads = [None] * depth
    scatters = {}

    def pieces_of(big, group):
        return [big[name].reshape(PIECE_SHAPES[name]) for name in GROUPS[group]]

    def rider(grads):
        flat = jnp.concatenate([g.reshape(-1) for g in grads])
        rows = -(-flat.shape[0] // 1024)
        flat = jnp.concatenate([flat, jnp.zeros((rows * 1024 - flat.shape[0],), F32)])
        return jnp.broadcast_to(flat.reshape(1, rows, 1024), (N_DEV, rows, 1024))

    riding = {}
    done_small = {}

    def start_scatter(l, group, big, extra):
        names = [(n, l) for n in SMALL_WITH[group]] + extra
        riding[l, group] = names
        grads = [dg_final if n == "g_final" else done_small[ll][n] for n, ll in names]
        scatters[l, group], token = _exchange_start(pieces_of(big, group) + [rider(grads)], True,
                                                    "scatter_%s_start_%d" % (group, l))
        return token[0, 0]

    for l in reversed(range(depth)):
        def rest_ready(big, small, l=l):
            done_small[l] = small
            extra = ([("g_final", l)] if l == depth - 1 else []) + ([("g_mix", l + 1)] if l + 1 < depth else [])
            return start_scatter(l, "rest", big, extra)

        def in_ready(big, small, l=l):
            return start_scatter(l, "in", big, [])

        dh, _, small_grads[l] = _layer_bwd(dh, memv, gathered[l], smalls[l], tables, saved[l], rest_ready, in_ready)
    grad_x = dh[None]
    riding["tail"] = [("g_mix", 0)]
    scatters["tail"], _ = _exchange_start([rider([small_grads[0]["g_mix"]])], True, "scatter_tail_start")

    parts, owns, small_parts = {}, {}, {}

    def take_rider(key, got, given):
        flat = lax.dynamic_update_slice_in_dim(got, given[:1], me, axis=0).reshape(N_DEV, -1)
        off = 0
        for name, ll in riding[key]:
            shape = SMALL_SHAPES.get(name, weights[name].shape[-1:] if name == "g_final" else weights[name].shape[1:])
            n = 1
            for dim in shape:
                n *= dim
            small_parts.setdefault(name, [None] * depth)[ll] = flat[:, off:off + n].reshape((N_DEV,) + shape)
            off += n

    def wait_group(group, after):
        for l in reversed(range(depth)):
            got, given = _exchange_wait(scatters[l, group], after, "scatter_%s_wait_%d" % (group, l))
            for name, g, x in zip(GROUPS[group], got, given):
                parts.setdefault(name, [None] * depth)[l] = g
                owns.setdefault(name, [None] * depth)[l] = x
            take_rider((l, group), got[-1], given[-1])

    results = {}

    def update(name, w3, m3, v3):
        outs = _adamw(parts[name], owns.get(name), me, w3, m3, v3, "adamw_" + name)
        results[name] = [o.reshape(weights[name].shape) for o in outs]

    wait_group("rest", grad_x)
    for name in GROUPS["rest"]:
        if name == "w_up":
            outs = _adamw(parts[name], owns[name], me, w_up.transpose(0, 2, 1), m_w_up.transpose(0, 2, 1),
                          v_w_up.transpose(0, 2, 1), "adamw_w_up")
            results[name] = [o.transpose(0, 2, 1) for o in outs]
        else:
            update(name, weights[name], m_in[name], v_in[name])
    wait_group("in", results["w_down"][1])
    outs = _adamw(parts["w_in"], owns["w_in"], me, w_in_r, _relay_in_cols(m_w_in), _relay_in_cols(v_w_in),
                  "adamw_w_in")
    results["w_in"] = [_unrelay_in_cols(o) for o in outs]
    got, given = _exchange_wait(scatters["tail"], results["w_in"][1], "scatter_tail_wait")
    take_rider("tail", got[0], given[0])
    for name in SMALL_NAMES + ("g_final",):
        wv = weights[name]
        p = small_parts[name][depth - 1] if name == "g_final" else jnp.stack(small_parts[name], axis=1)
        if name == "w_sconv":
            p = lax.dynamic_slice_in_dim(p, me * (GROUP // N_DEV), GROUP // N_DEV, axis=3)
        elif name == "w_ffconv":
            p = lax.dynamic_index_in_dim(p, me, axis=2, keepdims=False)
        shape3 = (1, 1, wv.shape[0]) if wv.ndim == 1 else (1, -1, wv.shape[-1])
        w3 = wv.reshape(shape3)
        parts[name] = [p.reshape((N_DEV,) + w3.shape[1:])]
        update(name, w3, m_in[name].reshape(shape3), v_in[name].reshape(shape3))

    return (loss, grad_x, *[results[n][0] for n in WEIGHT_NAMES], *[results[n][1] for n in WEIGHT_NAMES],
            *[results[n][2] for n in WEIGHT_NAMES], *[results[n][3] for n in WEIGHT_NAMES])
```

```python
import functools

import jax
import jax.numpy as jnp
from jax import lax
from jax.experimental import pallas as pl
from jax.experimental.pallas import tpu as pltpu

F32 = jnp.float32
BF16 = jnp.bfloat16

N_DEV = 8
D_MODEL = 1024
GROUP = 256
HEAD_DIM = 64
N_HEADS = 4
N_IN = 2564
N_IN_PAD = 2688
COL_GATE = 2560
XA_HEADS = 4
XA_DIM = 256
MEM_LEN = 256
D_FF = 2816
FF_SHARD = 704
FF_HALF = 4
ROPE_THETA = 500000.0
ROPE_DIM = 16
RMS_EPS = 1e-6
NEG = -1e30
POOL_WINDOWS = (2, 4, 8, 16)
ADAM_LR, ADAM_B1, ADAM_B2, ADAM_EPS, ADAM_WD, ADAM_STEP = 0.001, 0.9, 0.999, 1e-08, 0.01, 10

ROW_TILE = 1024
MM_TILE = 1024
SLAB_TILE = 512
ATT_TQ = 512
BWD_HEADS = 4
VMEM_LIMIT = 56 * 1024 * 1024
ADAMW_BLOCK_BYTES = 8 * 1024 * 1024
PLACE_BLOCK_BYTES = 4 * 1024 * 1024

MESH_AXES = ("x", "y", "c")


def _params(**kw):
    return pltpu.CompilerParams(vmem_limit_bytes=VMEM_LIMIT, **kw)


HBM_SPEC = pl.BlockSpec(memory_space=pltpu.HBM)
SEM_SPEC = pl.BlockSpec(memory_space=pltpu.SEMAPHORE)
DATAFLOW = pltpu.SideEffectType.DATAFLOW_SIDE_EFFECTING


def _peer_copies(x_ref, land_ref, send_sems, recv_sems, scatter):
    mx, my, mc = lax.axis_index("x"), lax.axis_index("y"), lax.axis_index("c")
    me = 4 * mx + 2 * my + mc
    pairs = []
    for k in range(1, N_DEV):
        kx, ky, kc = (k >> 2) & 1, (k >> 1) & 1, k & 1
        peer_lin = me ^ k
        send = pltpu.make_async_remote_copy(
            src_ref=x_ref.at[peer_lin] if scatter else land_ref.at[me], dst_ref=land_ref.at[me],
            send_sem=send_sems.at[k - 1], recv_sem=recv_sems.at[k - 1],
            device_id=(mx ^ kx, my ^ ky, mc ^ kc), device_id_type=pl.DeviceIdType.MESH)
        arrival = pltpu.make_async_remote_copy(
            src_ref=land_ref.at[peer_lin], dst_ref=land_ref.at[peer_lin],
            send_sem=send_sems.at[k - 1], recv_sem=recv_sems.at[k - 1],
            device_id=(mx, my, mc), device_id_type=pl.DeviceIdType.MESH)
        pairs.append((send, arrival))
    return pairs


def _exchange_start(xs, scatter, name):
    n = len(xs)
    ns = n if scatter else 0

    def body(*refs):
        srcs = refs[:ns] if scatter else (None,) * n
        lands, sends, recvs = refs[ns:ns + n], refs[ns + n:ns + 2 * n], refs[ns + 2 * n:ns + 3 * n]
        for t in range(n):
            for send, _ in _peer_copies(srcs[t], lands[t], sends[t], recvs[t], scatter):
                send.start()
        token = refs[-1]
        token[...] = jnp.zeros_like(token)

    sems = pltpu.SemaphoreType.DMA((N_DEV - 1,))
    operands = [pltpu.with_memory_space_constraint(x, pltpu.HBM) for x in xs]
    if scatter:
        operands += [pltpu.with_memory_space_constraint(lax.empty(x.shape, x.dtype), pltpu.HBM) for x in xs]
    outs = pl.pallas_call(
        body, name=name,
        out_shape=(sems,) * (2 * n) + tuple(pltpu.HBM(a.shape, a.dtype) for a in operands)
        + (jax.ShapeDtypeStruct((8, 128), F32),),
        in_specs=(HBM_SPEC,) * (ns + n),
        out_specs=(SEM_SPEC,) * (2 * n) + (HBM_SPEC,) * (ns + n) + (pl.BlockSpec(memory_space=pltpu.VMEM),),
        input_output_aliases={i: 2 * n + i for i in range(ns + n)},
        compiler_params=pltpu.CompilerParams(has_side_effects=DATAFLOW),
    )(*operands)
    return (outs[:-1], scatter), outs[-1]


def _exchange_wait(state, after, name):
    held, scatter = state
    n = len(held) // (4 if scatter else 3)
    ns = n if scatter else 0
    sems, thru = held[:2 * n], held[2 * n:]

    def body(*refs):
        srcs = refs[:ns] if scatter else (None,) * n
        lands, sends, recvs = refs[ns:ns + n], refs[ns + n:ns + 2 * n], refs[ns + 2 * n:ns + 3 * n]
        for t in range(n):
            for send, arrival in _peer_copies(srcs[t], lands[t], sends[t], recvs[t], scatter):
                send.wait_send()
                arrival.wait_recv()

    outs = pl.pallas_call(
        body, name=name,
        out_shape=tuple(pltpu.HBM(a.shape, a.dtype) for a in thru),
        in_specs=(HBM_SPEC,) * (ns + n) + (SEM_SPEC,) * (2 * n) + (pl.BlockSpec(memory_space=pl.ANY),),
        out_specs=(HBM_SPEC,) * (ns + n), input_output_aliases={i: i for i in range(ns + n)},
        compiler_params=pltpu.CompilerParams(has_side_effects=DATAFLOW),
    )(*thru, *sems, after)
    return list(outs[ns:]), list(outs[:ns])


def _place_shard(x, layer, me, dtype, name, after=None):
    _, r, c = x.shape
    tr = r
    if r * c * 4 > PLACE_BLOCK_BYTES:
        for cand in (512, 256, 128, 64, 32, 16):
            if r % cand == 0 and cand * c * 4 <= PLACE_BLOCK_BYTES:
                tr = cand
                break

    def body(me_ref, x_ref, *rest):
        o_ref = rest[-1]
        o_ref[...] = x_ref[...].astype(o_ref.dtype)

    return pl.pallas_call(
        body, name=name, out_shape=jax.ShapeDtypeStruct((N_DEV, r, c), dtype),
        grid_spec=pltpu.PrefetchScalarGridSpec(
            num_scalar_prefetch=1, grid=(r // tr,),
            in_specs=[pl.BlockSpec((None, tr, c), lambda i, me_ref: (layer, i, 0))]
            + ([ANY_SPEC] if after is not None else []),
            out_specs=pl.BlockSpec((None, tr, c), lambda i, me_ref: (me_ref[0], i, 0))),
        compiler_params=_params(),
    )(*((me.reshape(1), x) + ((after,) if after is not None else ())))


NN = ((1,), (0,))
NT = ((1,), (1,))
TN = ((0,), (0,))


def _matmul(a, b, out_shape, *, grid, a_spec, b_spec, o_spec, dims, nred, name, res=None, res_spec=None,
            out_dtype=F32, slabs=0):
    has_res = res is not None

    def body(*refs):
        a_ref, b_ref = refs[0], refs[1]
        r_ref = refs[2] if has_res else None
        o_ref = refs[3] if has_res else refs[2]
        if slabs:
            part = None
            for n in range(slabs):
                term = lax.dot_general(a_ref[n].astype(BF16), b_ref[n].astype(BF16), (dims, ((), ())),
                                       preferred_element_type=F32)
                part = term if part is None else part + term
        else:
            part = lax.dot_general(a_ref[...].astype(BF16), b_ref[...].astype(BF16), (dims, ((), ())),
                                   preferred_element_type=F32)
        if nred == 1:
            if has_res:
                part = part + r_ref[...]
            o_ref[...] = part.astype(o_ref.dtype)
        else:
            acc = refs[-1]
            r = pl.program_id(2)

            @pl.when(r == 0)
            def _():
                acc[...] = part

            @pl.when(r > 0)
            def _():
                acc[...] += part

            @pl.when(r == nred - 1)
            def _():
                tot = acc[...]
                if has_res:
                    tot = tot + r_ref[...]
                o_ref[...] = tot.astype(o_ref.dtype)

    in_specs = [a_spec, b_spec] + ([res_spec] if has_res else [])
    args = (a, b) + ((res,) if has_res else ())
    acc_shape = tuple(d for d in o_spec.block_shape if d is not None)
    return pl.pallas_call(
        body, name=name, grid=grid, out_shape=jax.ShapeDtypeStruct(out_shape, out_dtype),
        in_specs=in_specs, out_specs=o_spec,
        scratch_shapes=[pltpu.VMEM(acc_shape, F32)] if nred > 1 else [],
        compiler_params=_params(),
    )(*args)


def _mm_nn(a, w, name, res=None, tn=None, out_dtype=F32):
    m, k = a.shape
    n = w.shape[1]
    tn = tn or n
    tm = min(MM_TILE, m)
    ospec = pl.BlockSpec((tm, tn), lambda i, j, r: (i, j))
    return _matmul(a, w, (m, n), grid=(m // tm, n // tn, 1),
                   a_spec=pl.BlockSpec((tm, k), lambda i, j, r: (i, 0)),
                   b_spec=pl.BlockSpec((k, tn), lambda i, j, r: (0, j)),
                   o_spec=ospec, dims=NN, nred=1, name=name, res=res, res_spec=ospec if res is not None else None,
                   out_dtype=out_dtype)


def _mm_nt(a, w, name, out_dtype=F32):
    m, n = a.shape
    k = w.shape[0]
    tm = min(MM_TILE, m)
    return _matmul(a, w, (m, k), grid=(m // tm, 1, 1),
                   a_spec=pl.BlockSpec((tm, n), lambda i, j, r: (i, 0)),
                   b_spec=pl.BlockSpec((k, n), lambda i, j, r: (0, 0)),
                   o_spec=pl.BlockSpec((tm, k), lambda i, j, r: (i, 0)), dims=NT, nred=1, name=name,
                   out_dtype=out_dtype)


def _norm_matmul(h, g, b, out_shape, *, grid, b_spec, o_spec, name, out_dtype=F32, dims=NN):
    s, d = h.shape
    tm = s // grid[0]

    def body(h_ref, g_ref, b_ref, o_ref, xn_ref):
        @pl.when(pl.program_id(1) == 0)
        def _():
            hv = h_ref[...]
            r = lax.rsqrt(jnp.mean(hv * hv, axis=-1, keepdims=True) + RMS_EPS)
            xn_ref[...] = (hv * r * g_ref[...]).astype(xn_ref.dtype)

        o_ref[...] = lax.dot_general(xn_ref[...], b_ref[...].astype(BF16), (dims, ((), ())),
                                     preferred_element_type=F32).astype(o_ref.dtype)

    row = pl.BlockSpec((tm, d), lambda i, j: (i, 0))
    return pl.pallas_call(
        body, name=name, grid=grid,
        out_shape=(jax.ShapeDtypeStruct(out_shape, out_dtype), jax.ShapeDtypeStruct((s, d), BF16)),
        in_specs=[row, pl.BlockSpec((1, d), lambda i, j: (0, 0)), b_spec],
        out_specs=(o_spec, row), compiler_params=_params(),
    )(h, g.reshape(1, d), b)


def _matmul_rms_bwd(a, w, h, g, res, name, tm=SLAB_TILE, dims=NT):
    slabs = a.shape[0] if a.ndim == 3 else 0
    s, n = a.shape[-2:]
    d = w.shape[-2] if dims == NT else w.shape[-1]
    tm = min(tm, s)

    def body(a_ref, w_ref, h_ref, g_ref, r_ref, dh_ref, dg_ref):
        if slabs:
            dy = None
            for j in range(slabs):
                term = lax.dot_general(a_ref[j].astype(BF16), w_ref[j].astype(BF16), (dims, ((), ())),
                                       preferred_element_type=F32)
                dy = term if dy is None else dy + term
        else:
            dy = lax.dot_general(a_ref[...].astype(BF16), w_ref[...].astype(BF16), (dims, ((), ())),
                                 preferred_element_type=F32)
        hv = h_ref[...]
        r = lax.rsqrt(jnp.mean(hv * hv, axis=-1, keepdims=True) + RMS_EPS)
        hn = hv * r
        u = dy * g_ref[...]
        dh_ref[...] = r * (u - hn * jnp.mean(u * hn, axis=-1, keepdims=True)) + r_ref[...]
        part = jnp.sum(dy * hn, axis=0, keepdims=True)

        @pl.when(pl.program_id(0) == 0)
        def _():
            dg_ref[...] = part

        @pl.when(pl.program_id(0) > 0)
        def _():
            dg_ref[...] += part

    row = pl.BlockSpec((tm, d), lambda i: (i, 0))
    vec = pl.BlockSpec((1, d), lambda i: (0, 0))
    if slabs:
        a_spec = pl.BlockSpec((slabs, tm, n), lambda i: (0, i, 0))
        w_spec = pl.BlockSpec(w.shape, lambda i: (0, 0, 0))
    else:
        a_spec = pl.BlockSpec((tm, n), lambda i: (i, 0))
        w_spec = pl.BlockSpec(w.shape, lambda i: (0, 0))
    dh, dg = pl.pallas_call(
        body, name=name, grid=(s // tm,),
        out_shape=(jax.ShapeDtypeStruct((s, d), F32), jax.ShapeDtypeStruct((1, d), F32)),
        in_specs=[a_spec, w_spec, row, vec, row], out_specs=(row, vec), compiler_params=_params(),
    )(a, w, h, g.reshape(1, d), res)
    return dh, dg.reshape(d)


GRAD_DTYPE = BF16


def _mm_tn(a, b, name, tk=512, tn=None):
    s, k = a.shape
    n = b.shape[1]
    tn = tn or n
    tk = min(tk, k)
    ts = s if b.dtype == BF16 else max(s // 2, 1)
    return _matmul(a, b, (k, n), grid=(k // tk, n // tn, s // ts),
                   a_spec=pl.BlockSpec((ts, tk), lambda i, j, r: (r, i)),
                   b_spec=pl.BlockSpec((ts, tn), lambda i, j, r: (r, j)),
                   o_spec=pl.BlockSpec((tk, tn), lambda i, j, r: (i, j)), dims=TN, nred=s // ts, name=name,
                   out_dtype=GRAD_DTYPE)


def _rms_fwd(h, g, name):
    s, d = h.shape
    tm = min(ROW_TILE, s)

    def body(h_ref, g_ref, o_ref):
        hv = h_ref[...]
        r = lax.rsqrt(jnp.mean(hv * hv, axis=-1, keepdims=True) + RMS_EPS)
        o_ref[...] = (hv * r * g_ref[...]).astype(o_ref.dtype)

    return pl.pallas_call(
        body, name=name, grid=(s // tm,), out_shape=jax.ShapeDtypeStruct((s, d), BF16),
        in_specs=[pl.BlockSpec((tm, d), lambda i: (i, 0)), pl.BlockSpec((1, d), lambda i: (0, 0))],
        out_specs=pl.BlockSpec((tm, d), lambda i: (i, 0)), compiler_params=_params(),
    )(h, g.reshape(1, d))


def _rms_bwd(dy, h, g, res, name):
    s, d = h.shape
    tm = min(ROW_TILE, s)
    has_res = res is not None

    def body(*refs):
        dy_ref, h_ref, g_ref = refs[:3]
        r_ref = refs[3] if has_res else None
        dh_ref, dg_ref = refs[-2], refs[-1]
        hv = h_ref[...]
        r = lax.rsqrt(jnp.mean(hv * hv, axis=-1, keepdims=True) + RMS_EPS)
        hn = hv * r
        dyv = dy_ref[...].astype(F32)
        u = dyv * g_ref[...]
        dh = r * (u - hn * jnp.mean(u * hn, axis=-1, keepdims=True))
        if has_res:
            dh = dh + r_ref[...]
        dh_ref[...] = dh
        part = jnp.sum(dyv * hn, axis=0, keepdims=True)

        @pl.when(pl.program_id(0) == 0)
        def _():
            dg_ref[...] = part

        @pl.when(pl.program_id(0) > 0)
        def _():
            dg_ref[...] += part

    row = pl.BlockSpec((tm, d), lambda i: (i, 0))
    vec = pl.BlockSpec((1, d), lambda i: (0, 0))
    dh, dg = pl.pallas_call(
        body, name=name, grid=(s // tm,),
        out_shape=(jax.ShapeDtypeStruct((s, d), F32), jax.ShapeDtypeStruct((1, d), F32)),
        in_specs=[row, row, vec] + ([row] if has_res else []),
        out_specs=(row, vec), compiler_params=_params(),
    )(*((dy, h, g.reshape(1, d)) + ((res,) if has_res else ())))
    return dh, dg.reshape(d)


def _loss_head(h, g, target):
    s, d = h.shape
    tm = min(ROW_TILE, s)

    def body(h_ref, g_ref, t_ref, loss_ref, dh_ref, dg_ref):
        hv = h_ref[...]
        r = lax.rsqrt(jnp.mean(hv * hv, axis=-1, keepdims=True) + RMS_EPS)
        hn = hv * r
        gv = g_ref[...]
        err = hn * gv - t_ref[...]
        rows = jnp.mean(err * err, axis=-1, keepdims=True)
        lpart = 0.5 * jnp.sum(rows, axis=0, keepdims=True) + jnp.zeros((1, 128), F32)
        dy = err * (1.0 / d)
        u = dy * gv
        dh_ref[...] = r * (u - hn * jnp.mean(u * hn, axis=-1, keepdims=True))
        gpart = jnp.sum(dy * hn, axis=0, keepdims=True)

        @pl.when(pl.program_id(0) == 0)
        def _():
            dg_ref[...] = gpart
            loss_ref[...] = lpart

        @pl.when(pl.program_id(0) > 0)
        def _():
            dg_ref[...] += gpart
            loss_ref[...] += lpart

    row = pl.BlockSpec((tm, d), lambda i: (i, 0))
    vec = pl.BlockSpec((1, d), lambda i: (0, 0))
    return pl.pallas_call(
        body, name="loss_head", grid=(s // tm,),
        out_shape=(jax.ShapeDtypeStruct((1, 128), F32), jax.ShapeDtypeStruct((s, d), F32),
                   jax.ShapeDtypeStruct((1, d), F32)),
        in_specs=[row, vec, row],
        out_specs=(pl.BlockSpec((1, 128), lambda i: (0, 0)), row, vec), compiler_params=_params(),
    )(h, g.reshape(1, d), target)


def _shift_down(x, k):
    return pltpu.roll(x, k, 0)


def _shift_up(x, k):
    return pltpu.roll(x, x.shape[0] - k, 0)


def _conv3(x, w):
    return w[2:3, :] * x + w[1:2, :] * _shift_down(x, 1) + w[0:1, :] * _shift_down(x, 2)


def _conv3_t(x, w):
    return w[2:3, :] * x + w[1:2, :] * _shift_up(x, 1) + w[0:1, :] * _shift_up(x, 2)


def _sigmoid(x):
    return 1.0 / (1.0 + jnp.exp(-x))


def _prev_map(tile, halo, col):
    return lambda i: (jnp.maximum(i * (tile // halo) - 1, 0), col)


def _next_map(tile, halo, col, nrows):
    return lambda i: (jnp.minimum((i + 1) * (tile // halo), nrows // halo - 1), col)


def _sconv_fwd(proj, w):
    s = proj.shape[0]
    t = min(ROW_TILE, s)

    def body(cur_ref, prev_ref, w_ref, o_ref):
        i = pl.program_id(0)
        prev = prev_ref[...] * (i > 0).astype(F32)
        ext = jnp.concatenate([prev, cur_ref[...]], axis=0)
        sv = ext[:, 2 * GROUP:3 * GROUP] * ext[:, 0:GROUP]
        y = ext[:, GROUP:2 * GROUP] * _conv3(sv, w_ref[...])
        o_ref[...] = y[8:].astype(o_ref.dtype)

    return pl.pallas_call(
        body, name="sconv_fwd", grid=(s // t,), out_shape=jax.ShapeDtypeStruct((s, 4 * GROUP), BF16),
        in_specs=[pl.BlockSpec((t, 3 * GROUP), lambda i: (i, 0)),
                  pl.BlockSpec((8, 3 * GROUP), _prev_map(t, 8, 0)),
                  pl.BlockSpec((3, GROUP), lambda i: (0, 0))],
        out_specs=pl.BlockSpec((t, GROUP), lambda i: (i, 0)), compiler_params=_params(),
    )(proj, proj, w)


def _sconv_bwd(proj, w, dy):
    s = proj.shape[0]
    t = min(ROW_TILE, s)
    nt = s // t

    def body(cur_ref, prev_ref, next_ref, w_ref, dy_ref, dyn_ref, dp_ref, dw_ref):
        i = pl.program_id(0)
        first = (i > 0).astype(F32)
        last = (i < nt - 1).astype(F32)
        ext = jnp.concatenate([prev_ref[...] * first, cur_ref[...], next_ref[...] * last], axis=0)
        dye = jnp.concatenate([jnp.zeros((8, GROUP), F32), dy_ref[...], dyn_ref[...] * last], axis=0)
        hv, bv, cv = ext[:, 0:GROUP], ext[:, GROUP:2 * GROUP], ext[:, 2 * GROUP:3 * GROUP]
        wv = w_ref[...]
        sv = cv * hv
        conv = _conv3(sv, wv)
        dconv = dye * bv
        ds = _conv3_t(dconv, wv)
        dp = jnp.concatenate([ds * cv, dye * conv, ds * hv], axis=1)
        dp_ref[...] = dp[8:8 + t].astype(dp_ref.dtype)
        dc = dconv[8:8 + t]
        dw = jnp.concatenate([
            jnp.sum(dc * _shift_down(sv, 2)[8:8 + t], axis=0, keepdims=True),
            jnp.sum(dc * _shift_down(sv, 1)[8:8 + t], axis=0, keepdims=True),
            jnp.sum(dc * sv[8:8 + t], axis=0, keepdims=True),
            jnp.zeros((5, GROUP), F32)], axis=0)

        @pl.when(i == 0)
        def _():
            dw_ref[...] = dw

        @pl.when(i > 0)
        def _():
            dw_ref[...] += dw

    dp, dw = pl.pallas_call(
        body, name="sconv_bwd", grid=(nt,),
        out_shape=(jax.ShapeDtypeStruct((s, N_IN_PAD), BF16), jax.ShapeDtypeStruct((8, GROUP), F32)),
        in_specs=[pl.BlockSpec((t, 3 * GROUP), lambda i: (i, 0)),
                  pl.BlockSpec((8, 3 * GROUP), _prev_map(t, 8, 0)),
                  pl.BlockSpec((8, 3 * GROUP), _next_map(t, 8, 0, s)),
                  pl.BlockSpec((3, GROUP), lambda i: (0, 0)),
                  pl.BlockSpec((t, GROUP), lambda i: (i, 0)),
                  pl.BlockSpec((8, GROUP), _next_map(t, 8, 0, s))],
        out_specs=(pl.BlockSpec((t, 3 * GROUP), lambda i: (i, 0)), pl.BlockSpec((8, GROUP), lambda i: (0, 0))),
        compiler_params=_params(),
    )(proj, proj, proj, w, dy, dy)
    return dp, dw[:3]


def _lane_window(shape):
    lane = lax.broadcasted_iota(jnp.int32, shape, 1)
    return lane, jnp.where(lane < 64, 2.0, jnp.where(lane < 128, 4.0, jnp.where(lane < 192, 8.0, 16.0)))


def _by_group(lane, s1, s2, s3, s4):
    return jnp.where(lane < 64, s1, jnp.where(lane < 128, s2, jnp.where(lane < 192, s3, s4)))


def _pool_z(ext, row0):
    s1 = ext + _shift_down(ext, 1)
    s2 = s1 + _shift_down(s1, 2)
    s3 = s2 + _shift_down(s2, 4)
    s4 = s3 + _shift_down(s3, 8)
    lane, win = _lane_window(ext.shape)
    tpos = (lax.broadcasted_iota(jnp.int32, ext.shape, 0) + (row0 - 16 + 1)).astype(F32)
    cnt = jnp.maximum(jnp.minimum(tpos, win), 1.0)
    return _by_group(lane, s1, s2, s3, s4) / cnt - ext


ANY_SPEC = pl.BlockSpec(memory_space=pl.ANY)


def _pool_fwd(proj, wbd, scale, ybuf):
    s = proj.shape[0]
    t = min(ROW_TILE, s)
    col = (COL_GATE - GROUP) // GROUP

    def body(cur_ref, prev_ref, w_ref, sc_ref, buf_ref, o_ref):
        i = pl.program_id(0)
        ext = jnp.concatenate([prev_ref[...] * (i > 0).astype(F32), cur_ref[...]], axis=0)
        z = _pool_z(ext, i * t)[16:]
        y = jnp.dot(z.astype(BF16), w_ref[...].astype(BF16), preferred_element_type=F32)
        o_ref[...] = (y * sc_ref[...]).astype(o_ref.dtype)

    return pl.pallas_call(
        body, name="pool_fwd", grid=(s // t,), out_shape=jax.ShapeDtypeStruct(ybuf.shape, ybuf.dtype),
        in_specs=[pl.BlockSpec((t, GROUP), lambda i: (i, col)),
                  pl.BlockSpec((16, GROUP), _prev_map(t, 16, col)),
                  pl.BlockSpec((GROUP, GROUP), lambda i: (0, 0)),
                  pl.BlockSpec((1, GROUP), lambda i: (0, 0)), ANY_SPEC],
        out_specs=pl.BlockSpec((t, GROUP), lambda i: (i, 3)), input_output_aliases={4: 0},
        compiler_params=_params(),
    )(proj, proj, wbd, scale.reshape(1, GROUP), ybuf)


def _pool_bwd(proj, wbd, scale, dy, dbuf):
    s = proj.shape[0]
    t = min(ROW_TILE, s)
    nt = s // t
    col = (COL_GATE - GROUP) // GROUP

    def body(cur_ref, prev_ref, w_ref, sc_ref, dy_ref, dyn_ref, buf_ref, dp_ref, dw_ref, dsc_ref):
        i = pl.program_id(0)
        ext = jnp.concatenate([prev_ref[...] * (i > 0).astype(F32), cur_ref[...]], axis=0)
        z = _pool_z(ext, i * t)[16:]
        wv = w_ref[...].astype(BF16)
        dyc = dy_ref[...]
        dye = jnp.concatenate([dyc, dyn_ref[...] * (i < nt - 1).astype(F32)], axis=0) * sc_ref[...]
        dz = lax.dot_general(dye.astype(BF16), wv, (NT, ((), ())), preferred_element_type=F32)
        lane, win = _lane_window(dz.shape)
        tpos = (lax.broadcasted_iota(jnp.int32, dz.shape, 0) + (i * t + 1)).astype(F32)
        e = dz / jnp.minimum(tpos, win)
        f1 = e + _shift_up(e, 1)
        f2 = f1 + _shift_up(f1, 2)
        f3 = f2 + _shift_up(f2, 4)
        f4 = f3 + _shift_up(f3, 8)
        dp = _by_group(lane, f1, f2, f3, f4) - dz
        dp_ref[...] = dp[:t].astype(dp_ref.dtype)
        zb = z.astype(BF16)
        y = jnp.dot(zb, wv, preferred_element_type=F32)
        dsc = jnp.sum(dyc * y, axis=0, keepdims=True)
        dw = lax.dot_general(zb, dye[:t].astype(BF16), (TN, ((), ())), preferred_element_type=F32)

        @pl.when(i == 0)
        def _():
            dw_ref[...] = dw
            dsc_ref[...] = dsc

        @pl.when(i > 0)
        def _():
            dw_ref[...] += dw
            dsc_ref[...] += dsc

    dp, dw, dsc = pl.pallas_call(
        body, name="pool_bwd", grid=(nt,),
        out_shape=(jax.ShapeDtypeStruct(dbuf.shape, dbuf.dtype), jax.ShapeDtypeStruct((GROUP, GROUP), F32),
                   jax.ShapeDtypeStruct((1, GROUP), F32)),
        in_specs=[pl.BlockSpec((t, GROUP), lambda i: (i, col)),
                  pl.BlockSpec((16, GROUP), _prev_map(t, 16, col)),
                  pl.BlockSpec((GROUP, GROUP), lambda i: (0, 0)),
                  pl.BlockSpec((1, GROUP), lambda i: (0, 0)),
                  pl.BlockSpec((t, GROUP), lambda i: (i, 3)),
                  pl.BlockSpec((16, GROUP), _next_map(t, 16, 3, s)), ANY_SPEC],
        out_specs=(pl.BlockSpec((t, GROUP), lambda i: (i, col)), pl.BlockSpec((GROUP, GROUP), lambda i: (0, 0)),
                   pl.BlockSpec((1, GROUP), lambda i: (0, 0))),
        input_output_aliases={6: 0}, compiler_params=_params(),
    )(proj, proj, wbd, scale.reshape(1, GROUP), dy, dy, dbuf)
    return dp, dw, dsc.reshape(GROUP)


FF_HALO = 16


def _ffn_gate_fwd(u0, w):
    s = u0.shape[1]
    t = min(ROW_TILE, s)

    def body(a_ref, ap_ref, g_ref, gp_ref, wa_ref, wg_ref, o_ref):
        first = (pl.program_id(1) > 0).astype(F32)
        a = _conv3(jnp.concatenate([ap_ref[...] * first, a_ref[...].astype(F32)], axis=0), wa_ref[...])[FF_HALO:]
        g = _conv3(jnp.concatenate([gp_ref[...] * first, g_ref[...].astype(F32)], axis=0), wg_ref[...])[FF_HALO:]
        o_ref[...] = (a * (g * _sigmoid(g))).astype(o_ref.dtype)

    def cur(off):
        return pl.BlockSpec((None, t, FF_SHARD), lambda j, i: (j + off, i, 0))

    def prev(off):
        return pl.BlockSpec((None, FF_HALO, FF_SHARD),
                            lambda j, i: (j + off, jnp.maximum(i * (t // FF_HALO) - 1, 0), 0))

    def wspec(off):
        return pl.BlockSpec((None, 3, FF_SHARD), lambda j, i: (j + off, 0, 0))

    return pl.pallas_call(
        body, name="ffn_gate_fwd", grid=(FF_HALF, s // t),
        out_shape=jax.ShapeDtypeStruct((FF_HALF, s, FF_SHARD), BF16),
        in_specs=[cur(0), prev(0), cur(FF_HALF), prev(FF_HALF), wspec(0), wspec(FF_HALF)],
        out_specs=pl.BlockSpec((None, t, FF_SHARD), lambda j, i: (j, i, 0)), compiler_params=_params(),
    )(u0, u0, u0, u0, w, w)


def _ffn_gate_bwd(u0, w, dact):
    s = u0.shape[1]
    t = min(ROW_TILE, s)
    nt = s // t

    def body(c_ref, p_ref, n_ref, w_ref, d_ref, dn_ref, du_ref, dw_ref):
        i = pl.program_id(1)
        first = (i > 0).astype(F32)
        last = (i < nt - 1).astype(F32)
        dext = jnp.concatenate([jnp.zeros((FF_HALO, FF_SHARD), F32), d_ref[...].astype(F32), dn_ref[...] * last],
                               axis=0)
        ext = [jnp.concatenate([p_ref[n] * first, c_ref[n].astype(F32), n_ref[n] * last], axis=0) for n in range(2)]
        a = _conv3(ext[0], w_ref[0])
        g = _conv3(ext[1], w_ref[1])
        sg = _sigmoid(g)
        silu = g * sg
        dus = (dext * silu, dext * a * (sg + silu * (1.0 - sg)))
        mine = slice(FF_HALO, FF_HALO + t)
        for n in range(2):
            du_ref[n] = _conv3_t(dus[n], w_ref[n])[mine].astype(du_ref.dtype)
            dc = dus[n][mine]
            dw = jnp.concatenate([
                jnp.sum(dc * _shift_down(ext[n], 2)[mine], axis=0, keepdims=True),
                jnp.sum(dc * _shift_down(ext[n], 1)[mine], axis=0, keepdims=True),
                jnp.sum(dc * ext[n][mine], axis=0, keepdims=True),
                jnp.zeros((5, FF_SHARD), F32)], axis=0)

            @pl.when(i == 0)
            def _(n=n, dw=dw):
                dw_ref[n] = dw

            @pl.when(i > 0)
            def _(n=n, dw=dw):
                dw_ref[n] += dw

    def pair(rows, row_map):
        return pl.BlockSpec((2, None, rows, FF_SHARD), lambda j, i: (0, j, row_map(i), 0))

    prev_row = lambda i: jnp.maximum(i * (t // FF_HALO) - 1, 0)
    next_row = lambda i: jnp.minimum((i + 1) * (t // FF_HALO), s // FF_HALO - 1)
    u2 = u0.reshape(2, FF_HALF, s, FF_SHARD)
    du, dw = pl.pallas_call(
        body, name="ffn_gate_bwd", grid=(FF_HALF, nt),
        out_shape=(jax.ShapeDtypeStruct((2, FF_HALF, s, FF_SHARD), BF16),
                   jax.ShapeDtypeStruct((2, FF_HALF, 8, FF_SHARD), F32)),
        in_specs=[pair(t, lambda i: i), pair(FF_HALO, prev_row), pair(FF_HALO, next_row), pair(3, lambda i: 0),
                  pl.BlockSpec((None, t, FF_SHARD), lambda j, i: (j, i, 0)),
                  pl.BlockSpec((None, FF_HALO, FF_SHARD), lambda j, i: (j, next_row(i), 0))],
        out_specs=(pair(t, lambda i: i), pair(8, lambda i: 0)),
        compiler_params=_params(),
    )(u2, u2, u2, w.reshape(2, FF_HALF, 3, FF_SHARD), dact, dact)
    return du.reshape(2 * FF_HALF, s, FF_SHARD), dw.reshape(2 * FF_HALF, 8, FF_SHARD)[:, :3]


def _rope_tables(positions):
    inv_freq = ROPE_THETA ** (-jnp.arange(0, ROPE_DIM, 2, dtype=F32) / ROPE_DIM)
    ang = positions.astype(F32)[:, None] * inv_freq
    cos, sin = jnp.cos(ang), jnp.sin(ang)
    s = positions.shape[0]
    half = ROPE_DIM // 2
    rest = HEAD_DIM - ROPE_DIM
    ca = jnp.concatenate([cos, cos, jnp.ones((s, rest), F32)], axis=1)
    cb = jnp.concatenate([-sin, jnp.zeros((s, HEAD_DIM - half), F32)], axis=1)
    cc = jnp.concatenate([jnp.zeros((s, half), F32), sin, jnp.zeros((s, rest), F32)], axis=1)
    return tuple(jnp.tile(tb, (1, N_HEADS)) for tb in (ca, cb, cc))


QK_WIDE = 128
LANE_CQ, LANE_CK = 64, 67
KT_ROWS = 80


def _three_bf16(x):
    hi = x.astype(BF16).astype(F32)
    mid = (x - hi).astype(BF16).astype(F32)
    lo = (x - hi - mid).astype(BF16).astype(F32)
    return hi, mid, lo


def _heads_split(proj, col, tables, c, name):
    s = proj.shape[0]
    t = min(ROW_TILE, s)
    rope = tables is not None
    wide = c is not None
    width = QK_WIDE if wide else HEAD_DIM

    def body(*refs):
        x_ref = refs[0]
        q_ref, k_ref, v_ref, kt_ref, vt_ref = refs[-5:]
        xv = x_ref[...]
        parts = [xv[:, 0:GROUP], xv[:, GROUP:2 * GROUP], xv[:, 2 * GROUP:3 * GROUP]]
        if rope:
            ca, cb, cc = refs[1][...], refs[2][...], refs[3][...]
            for n in range(2):
                p = parts[n]
                parts[n] = p * ca + pltpu.roll(p, GROUP - 8, 1) * cb + pltpu.roll(p, 8, 1) * cc
        parts[0] = parts[0] * (HEAD_DIM ** -0.5)
        k_t, v_t = parts[1].T, parts[2].T
        ones_row = jnp.where(lax.broadcasted_iota(jnp.int32, (KT_ROWS - HEAD_DIM, t), 0) == 0, 1.0, 0.0)
        lane = lax.broadcasted_iota(jnp.int32, (t, QK_WIDE), 1)
        zeros = jnp.zeros((t, QK_WIDE - HEAD_DIM), F32)
        for h in range(N_HEADS):
            hs = slice(h * HEAD_DIM, (h + 1) * HEAD_DIM)
            qh, kh = parts[0][:, hs], parts[1][:, hs]
            if wide:
                terms = _three_bf16(refs[-6][:, h:h + 1])
                qh = jnp.concatenate([qh, zeros], axis=1)
                kh = jnp.concatenate([kh, zeros], axis=1)
                for n in range(3):
                    qh = jnp.where(lane == LANE_CQ + n, terms[n], jnp.where(lane == LANE_CK + n, 1.0, qh))
                    kh = jnp.where(lane == LANE_CK + n, -terms[n], jnp.where(lane == LANE_CQ + n, 1.0, kh))
            q_ref[h] = qh.astype(q_ref.dtype)
            k_ref[h] = kh.astype(k_ref.dtype)
            v_ref[h] = parts[2][:, hs].astype(v_ref.dtype)
            kt_ref[h] = jnp.concatenate([k_t[hs, :], ones_row], axis=0).astype(kt_ref.dtype)
            vt_ref[h] = v_t[hs, :].astype(vt_ref.dtype)

    tab = pl.BlockSpec((t, GROUP), lambda i: (i, 0))
    qk = pl.BlockSpec((N_HEADS, t, width), lambda i: (0, i, 0))
    heads = pl.BlockSpec((N_HEADS, t, HEAD_DIM), lambda i: (0, i, 0))
    heads_t = pl.BlockSpec((N_HEADS, HEAD_DIM, t), lambda i: (0, 0, i))
    qk_shape = jax.ShapeDtypeStruct((N_HEADS, s, width), BF16)
    return pl.pallas_call(
        body, name=name, grid=(s // t,),
        out_shape=(qk_shape, qk_shape, jax.ShapeDtypeStruct((N_HEADS, s, HEAD_DIM), BF16),
                   jax.ShapeDtypeStruct((N_HEADS, KT_ROWS, s), BF16),
                   jax.ShapeDtypeStruct((N_HEADS, HEAD_DIM, s), BF16)),
        in_specs=[pl.BlockSpec((t, 3 * GROUP), lambda i: (i, col))] + ([tab, tab, tab] if rope else [])
        + ([pl.BlockSpec((t, 128), lambda i: (i, 0))] if wide else []),
        out_specs=(qk, qk, heads, pl.BlockSpec((N_HEADS, KT_ROWS, t), lambda i: (0, 0, i)), heads_t),
        compiler_params=_params(),
    )(*((proj,) + (tuple(tables) if rope else ()) + ((c,) if wide else ())))


def _heads_merge(dqt, dk, dv, tables, name, dbuf, col):
    s = dv.shape[1]
    t = min(ROW_TILE, s)
    rope = tables is not None

    wide = dk.shape[2] == QK_WIDE

    def body(*refs):
        o_ref = refs[n_in + 1]
        dq = jnp.concatenate([refs[0][h, :HEAD_DIM, :] for h in range(N_HEADS)], axis=0).T
        parts = [dq] + [jnp.concatenate([r[h][:, :HEAD_DIM] for h in range(N_HEADS)], axis=1) for r in refs[1:3]]
        parts[0] = parts[0] * (HEAD_DIM ** -0.5)
        if rope:
            ca, cb, cc = refs[3][...], refs[4][...], refs[5][...]
            for n in range(2):
                p = parts[n]
                parts[n] = p * ca + pltpu.roll(p * cb, 8, 1) + pltpu.roll(p * cc, GROUP - 8, 1)
        o_ref[...] = jnp.concatenate(parts, axis=1).astype(o_ref.dtype)
        if wide:
            over_keys = jnp.concatenate([refs[0][h, HEAD_DIM:HEAD_DIM + 8, :] for h in range(N_HEADS)]
                                        + [jnp.zeros((128 - 8 * N_HEADS, t), F32)], axis=0).T
            lane = lax.broadcasted_iota(jnp.int32, (t, 128), 1)
            dc = jnp.zeros((t, 128), F32)
            for h in range(N_HEADS):
                dc = jnp.where(lane == h, over_keys[:, 8 * h:8 * h + 1] - refs[1][h][:, LANE_CK:LANE_CK + 1], dc)
            refs[n_in + 2][...] = dc

    tab = pl.BlockSpec((t, GROUP), lambda i: (i, 0))
    heads = pl.BlockSpec((N_HEADS, t, HEAD_DIM), lambda i: (0, i, 0))
    n_in = 6 if rope else 3
    dspec = pl.BlockSpec((t, 3 * GROUP), lambda i: (i, col))
    dshape = jax.ShapeDtypeStruct(dbuf.shape, dbuf.dtype)
    return pl.pallas_call(
        body, name=name, grid=(s // t,),
        out_shape=(dshape, jax.ShapeDtypeStruct((s, 128), F32)) if wide else dshape,
        in_specs=[pl.BlockSpec((N_HEADS, KT_ROWS, t), lambda i: (0, 0, i)),
                  pl.BlockSpec((N_HEADS, t, dk.shape[2]), lambda i: (0, i, 0)), heads]
        + ([tab, tab, tab] if rope else []) + [ANY_SPEC],
        out_specs=(dspec, pl.BlockSpec((t, 128), lambda i: (i, 0))) if wide else dspec,
        input_output_aliases={n_in: 0}, compiler_params=_params(),
    )(*((dqt, dk, dv) + (tuple(tables) if rope else ()) + (dbuf,)))


def _log_sigmoid(x):
    return jnp.minimum(x, 0.0) - jnp.log(1.0 + jnp.exp(-jnp.abs(x)))


def _scan_rows(x, reverse):
    n = x.shape[0]
    row = lax.broadcasted_iota(jnp.int32, x.shape, 0)
    k = 1
    while k < n:
        if reverse:
            x = x + jnp.where(row < n - k, _shift_up(x, k), 0.0)
        else:
            x = x + jnp.where(row >= k, _shift_down(x, k), 0.0)
        k *= 2
    return x


def _gate_cumsum(proj, bias):
    s = proj.shape[0]
    col = COL_GATE // 128

    def body(z_ref, b_ref, c_ref):
        c_ref[...] = _scan_rows(_log_sigmoid(z_ref[...] + b_ref[...]), False)

    return pl.pallas_call(
        body, name="gate_cumsum", grid=(1,), out_shape=jax.ShapeDtypeStruct((s, 128), F32),
        in_specs=[pl.BlockSpec((s, 128), lambda i: (0, col)), pl.BlockSpec((1, 128), lambda i: (0, 0))],
        out_specs=pl.BlockSpec((s, 128), lambda i: (0, 0)), compiler_params=_params(),
    )(proj, bias)


def _gate_cumsum_bwd(proj, bias, dc, dbuf):
    s = proj.shape[0]
    col = COL_GATE // 128

    def body(z_ref, b_ref, dc_ref, buf_ref, dz_ref, db_ref):
        dlogf = _scan_rows(dc_ref[...], True)
        dz = dlogf * _sigmoid(-(z_ref[...] + b_ref[...]))
        dz_ref[...] = dz.astype(dz_ref.dtype)
        db_ref[...] = jnp.sum(dz, axis=0, keepdims=True)

    return pl.pallas_call(
        body, name="gate_cumsum_bwd", grid=(1,),
        out_shape=(jax.ShapeDtypeStruct(dbuf.shape, dbuf.dtype), jax.ShapeDtypeStruct((1, 128), F32)),
        in_specs=[pl.BlockSpec((s, 128), lambda i: (0, col)), pl.BlockSpec((1, 128), lambda i: (0, 0)),
                  pl.BlockSpec((s, 128), lambda i: (0, 0)), ANY_SPEC],
        out_specs=(pl.BlockSpec((s, 128), lambda i: (0, col)), pl.BlockSpec((1, 128), lambda i: (0, 0))),
        input_output_aliases={3: 0}, compiler_params=_params(),
    )(proj, bias, dc, dbuf)


DIL_REACH = 2048


def _pair_weight(mode, d):
    if mode == "fox":
        return jnp.where(d >= 0, 1.0, 0.0)
    w1 = jnp.where(jnp.abs(d - 64) <= 64, 1.0, 0.0)
    w2 = jnp.where((d & 3) == 0, jnp.where(jnp.abs(d - 256) <= 256, 1.0, 0.0), 0.0)
    w3 = jnp.where((d & 15) == 0, jnp.where(jnp.abs(d - 1024) <= 1024, 1.0, 0.0), 0.0)
    return w1 + w2 + w3


def _bias_tables(mode, tq, tk):
    nb = 2 if mode == "fox" else DIL_REACH // tk + 1
    n = lax.broadcasted_iota(jnp.int32, (nb, tk, tq), 0)
    key = lax.broadcasted_iota(jnp.int32, (nb, tk, tq), 1)
    query = lax.broadcasted_iota(jnp.int32, (nb, tk, tq), 2)
    w = _pair_weight(mode, n * tk + query - key)
    return jnp.where(w > 0.0, jnp.log(jnp.maximum(w, 1.0)), NEG)


M_INIT = -1e29


def _first_key_chunk(mode, q0, tk):
    if mode == "fox":
        return 0
    return jnp.maximum(q0 - DIL_REACH, 0) // tk


def _attention_fwd(mode, q, k, vt, tab_t, ybuf, col):
    s, width = q.shape[1], q.shape[2]
    tq = min(ATT_TQ, s)
    tk = tq
    nb = tab_t.shape[0]

    def body(q_ref, k_ref, vt_ref, tab_ref, buf_ref, y_ref, o_ref, lse_ref):
        i = pl.program_id(0)
        lo = _first_key_chunk(mode, i * tq, tk)

        def step(c, carry):
            k0 = pl.multiple_of(c * tk, tk)
            tab = tab_ref[jnp.minimum(i - c, nb - 1)]
            scores = [lax.dot_general(k_ref[h, pl.ds(k0, tk), :], q_ref[h], (NT, ((), ())),
                                      preferred_element_type=F32) for h in range(N_HEADS)]
            stats, probs = [], []
            for h in range(N_HEADS):
                m, l = carry[3 * h:3 * h + 2]
                sc = scores[h] + tab
                m_new = jnp.maximum(m, jnp.max(sc, axis=0, keepdims=True))
                alpha = jnp.exp(m - m_new)
                p = jnp.exp(sc - m_new)
                stats.append((m_new, alpha * l + jnp.sum(p, axis=0, keepdims=True), alpha))
                probs.append(p.astype(BF16))
            pv = [jnp.dot(vt_ref[h, :, pl.ds(k0, tk)], probs[h], preferred_element_type=F32) for h in range(N_HEADS)]
            new = []
            for h in range(N_HEADS):
                m_new, l, alpha = stats[h]
                new += [m_new, l, alpha * carry[3 * h + 2] + pv[h]]
            return tuple(new)

        start = (jnp.full((1, tq), M_INIT, F32), jnp.zeros((1, tq), F32), jnp.zeros((HEAD_DIM, tq), F32))
        done = lax.fori_loop(lo, i + 1, step, start * N_HEADS)
        outs = []
        for h in range(N_HEADS):
            m, l, acc = done[3 * h:3 * h + 3]
            outs.append(acc / l)
            lse_ref[h] = m + jnp.log(l)
        out = jnp.concatenate(outs, axis=0).T
        y_ref[...] = out.astype(y_ref.dtype)
        o_ref[...] = out

    rowspec = pl.BlockSpec((N_HEADS, 1, tq), lambda i: (0, 0, i))
    return pl.pallas_call(
        body, name="attention_fwd_" + mode, grid=(s // tq,),
        out_shape=(jax.ShapeDtypeStruct(ybuf.shape, ybuf.dtype), jax.ShapeDtypeStruct((s, GROUP), F32),
                   jax.ShapeDtypeStruct((N_HEADS, 1, s), F32)),
        in_specs=[pl.BlockSpec((N_HEADS, tq, width), lambda i: (0, i, 0)),
                  pl.BlockSpec((N_HEADS, s, width), lambda i: (0, 0, 0)),
                  pl.BlockSpec((N_HEADS, HEAD_DIM, s), lambda i: (0, 0, 0)),
                  pl.BlockSpec((nb, tk, tq), lambda i: (0, 0, 0)), ANY_SPEC],
        out_specs=(pl.BlockSpec((tq, GROUP), lambda i: (i, col)), pl.BlockSpec((tq, GROUP), lambda i: (i, 0)),
                   rowspec),
        input_output_aliases={4: 0}, compiler_params=_params(),
    )(q, k, vt, tab_t, ybuf)


def _attention_delta(o, do, col):
    s = o.shape[0]
    t = min(ROW_TILE, s)

    def body(o_ref, do_ref, delta_ref, dob_ref):
        dov = do_ref[...]
        prod_t = (o_ref[...] * dov).T
        for h in range(N_HEADS):
            hs = slice(h * HEAD_DIM, (h + 1) * HEAD_DIM)
            delta_ref[h] = jnp.sum(prod_t[hs, :], axis=0, keepdims=True)
            dob_ref[h] = dov[:, hs].astype(dob_ref.dtype)

    return pl.pallas_call(
        body, name="attention_delta", grid=(s // t,),
        out_shape=(jax.ShapeDtypeStruct((N_HEADS, 1, s), F32), jax.ShapeDtypeStruct((N_HEADS, s, HEAD_DIM), BF16)),
        in_specs=[pl.BlockSpec((t, GROUP), lambda i: (i, 0)), pl.BlockSpec((t, GROUP), lambda i: (i, col))],
        out_specs=(pl.BlockSpec((N_HEADS, 1, t), lambda i: (0, 0, i)),
                   pl.BlockSpec((N_HEADS, t, HEAD_DIM), lambda i: (0, i, 0))),
        compiler_params=_params(),
    )(o, do)


def _attention_bwd(mode, q, k, v, kt, tab_t, dob, lse, delta):
    s, width = q.shape[1], q.shape[2]
    tq = min(ATT_TQ, s)
    tk = tq
    nq = s // tq
    nb = tab_t.shape[0]

    def body(q_ref, k_ref, v_ref, kt_ref, tab_ref, dob_ref, lse_ref, delta_ref, dqt_ref, dk_ref, dv_ref):
        i = pl.program_id(0)

        @pl.when(i == 0)
        def _():
            dqt_ref[...] = jnp.zeros_like(dqt_ref)

        hi = nq if mode == "fox" else jnp.minimum((i * tk + tk - 1 + DIL_REACH) // tq + 1, nq)
        for h0 in range(0, N_HEADS, BWD_HEADS):
            heads = range(h0, h0 + BWD_HEADS)

            def step(c, carry, heads=heads):
                q0 = pl.multiple_of(c * tq, tq)
                qs = pl.ds(q0, tq)
                tab = tab_ref[jnp.minimum(c - i, nb - 1)]
                qv = [q_ref[h, qs, :] for h in heads]
                dov = [dob_ref[h, qs, :] for h in heads]
                sc = [lax.dot_general(k_ref[h], qv[n], (NT, ((), ())), preferred_element_type=F32)
                      for n, h in enumerate(heads)]
                dp = [lax.dot_general(v_ref[h], dov[n], (NT, ((), ())), preferred_element_type=F32)
                      for n, h in enumerate(heads)]
                pb, dsb = [], []
                for n, h in enumerate(heads):
                    p = jnp.exp(sc[n] + tab - lse_ref[h, :, qs])
                    pb.append(p.astype(BF16))
                    dsb.append((p * (dp[n] - delta_ref[h, :, qs])).astype(BF16))
                new = []
                for n, h in enumerate(heads):
                    new += [carry[2 * n] + jnp.dot(dsb[n], qv[n], preferred_element_type=F32),
                            carry[2 * n + 1] + jnp.dot(pb[n], dov[n], preferred_element_type=F32)]
                for n, h in enumerate(heads):
                    dqt_ref[h, :, qs] += jnp.dot(kt_ref[h], dsb[n], preferred_element_type=F32)
                return tuple(new)

            start = (jnp.zeros((tk, width), F32), jnp.zeros((tk, HEAD_DIM), F32))
            done = lax.fori_loop(i, hi, step, start * BWD_HEADS)
            for n, h in enumerate(heads):
                dk_ref[h] = done[2 * n]
                dv_ref[h] = done[2 * n + 1]

    def full(shape):
        return pl.BlockSpec(shape, lambda i: (0, 0, 0))

    kblk = pl.BlockSpec((N_HEADS, tk, width), lambda i: (0, i, 0))
    vblk = pl.BlockSpec((N_HEADS, tk, HEAD_DIM), lambda i: (0, i, 0))
    return pl.pallas_call(
        body, name="attention_bwd_" + mode, grid=(s // tk,),
        out_shape=(jax.ShapeDtypeStruct((N_HEADS, KT_ROWS, s), F32), jax.ShapeDtypeStruct((N_HEADS, s, width), F32),
                   jax.ShapeDtypeStruct((N_HEADS, s, HEAD_DIM), F32)),
        in_specs=[full((N_HEADS, s, width)), kblk, vblk, pl.BlockSpec((N_HEADS, KT_ROWS, tk), lambda i: (0, 0, i)),
                  full((nb, tk, tq)), full((N_HEADS, s, HEAD_DIM)), full((N_HEADS, 1, s)), full((N_HEADS, 1, s))],
        out_specs=(full((N_HEADS, KT_ROWS, s)), kblk, vblk),
        compiler_params=_params(),
    )(q, k, v, kt, tab_t, dob, lse, delta)


def _xattn_fwd(qx, kvm):
    s = qx.shape[0]
    t = min(ROW_TILE, s)

    def body(q_ref, kv_ref, o_ref):
        heads = range(XA_HEADS)
        sc = [lax.dot_general(q_ref[:, h * XA_DIM:(h + 1) * XA_DIM].astype(BF16), kv_ref[h].astype(BF16),
                              (NT, ((), ())), preferred_element_type=F32) * (XA_DIM ** -0.5) for h in heads]
        probs = []
        for h in heads:
            e = jnp.exp(sc[h] - jnp.max(sc[h], axis=-1, keepdims=True))
            probs.append((e / jnp.sum(e, axis=-1, keepdims=True)).astype(BF16))
        outs = [jnp.dot(probs[h], kv_ref[XA_HEADS + h].astype(BF16), preferred_element_type=F32) for h in heads]
        for h in heads:
            o_ref[:, h * XA_DIM:(h + 1) * XA_DIM] = outs[h].astype(o_ref.dtype)

    return pl.pallas_call(
        body, name="xattn_fwd", grid=(s // t,), out_shape=jax.ShapeDtypeStruct((s, D_MODEL), BF16),
        in_specs=[pl.BlockSpec((t, D_MODEL), lambda i: (i, 0)),
                  pl.BlockSpec((2 * XA_HEADS, MEM_LEN, XA_DIM), lambda i: (0, 0, 0))],
        out_specs=pl.BlockSpec((t, D_MODEL), lambda i: (i, 0)), compiler_params=_params(),
    )(qx, kvm)


def _xattn_bwd(qx, kvm, do):
    s = qx.shape[0]
    t = min(ROW_TILE, s)

    def body(q_ref, kv_ref, do_ref, dq_ref, dkv_ref):
        i = pl.program_id(0)
        heads = range(XA_HEADS)
        qv = [q_ref[:, h * XA_DIM:(h + 1) * XA_DIM].astype(BF16) for h in heads]
        dov = [do_ref[:, h * XA_DIM:(h + 1) * XA_DIM].astype(BF16) for h in heads]
        kv = [kv_ref[h].astype(BF16) for h in heads]
        sc = [lax.dot_general(qv[h], kv[h], (NT, ((), ())), preferred_element_type=F32) * (XA_DIM ** -0.5)
              for h in heads]
        dp = [lax.dot_general(dov[h], kv_ref[XA_HEADS + h].astype(BF16), (NT, ((), ())), preferred_element_type=F32)
              for h in heads]
        pb, ds = [], []
        for h in heads:
            e = jnp.exp(sc[h] - jnp.max(sc[h], axis=-1, keepdims=True))
            p = e / jnp.sum(e, axis=-1, keepdims=True)
            pb.append(p.astype(BF16))
            ds.append((p * (dp[h] - jnp.sum(p * dp[h], axis=-1, keepdims=True)) * (XA_DIM ** -0.5)).astype(BF16))
        dq = [jnp.dot(ds[h], kv[h], preferred_element_type=F32) for h in heads]
        dk = [lax.dot_general(ds[h], qv[h], (TN, ((), ())), preferred_element_type=F32) for h in heads]
        dv = [lax.dot_general(pb[h], dov[h], (TN, ((), ())), preferred_element_type=F32) for h in heads]
        for h in heads:
            dq_ref[:, h * XA_DIM:(h + 1) * XA_DIM] = dq[h].astype(dq_ref.dtype)

        @pl.when(i == 0)
        def _():
            for h in heads:
                dkv_ref[h] = dk[h]
                dkv_ref[XA_HEADS + h] = dv[h]

        @pl.when(i > 0)
        def _():
            for h in heads:
                dkv_ref[h] += dk[h]
                dkv_ref[XA_HEADS + h] += dv[h]

    row = pl.BlockSpec((t, D_MODEL), lambda i: (i, 0))
    kvs = pl.BlockSpec((2 * XA_HEADS, MEM_LEN, XA_DIM), lambda i: (0, 0, 0))
    return pl.pallas_call(
        body, name="xattn_bwd", grid=(s // t,),
        out_shape=(jax.ShapeDtypeStruct((s, D_MODEL), BF16),
                   jax.ShapeDtypeStruct((2 * XA_HEADS, MEM_LEN, XA_DIM), F32)),
        in_specs=[row, kvs, row], out_specs=(row, kvs), compiler_params=_params(),
    )(qx, kvm, do)


def _adamw(parts, owns, me, w, m, v, name):
    nl, r, c = w.shape
    tr = r
    for cand in (512, 352, 256, 176, 128, 64, 32, 16, 8):
        if r % cand == 0 and r > cand and N_DEV * cand * c * 4 <= ADAMW_BLOCK_BYTES:
            tr = cand
            break
    nt = r // tr
    per_layer = N_DEV + (1 if owns is not None else 0)

    def body(me_ref, *refs):
        w_ref, m_ref, v_ref, g_ref, d_ref, nm_ref, nv_ref = refs[nl * per_layer:]
        layer = pl.program_id(0)
        g = None
        for l in range(nl):
            p_refs = refs[l * per_layer:(l + 1) * per_layer]
            gl = None
            for d in range(N_DEV):
                term = p_refs[d][...].astype(F32)
                if owns is not None:
                    term = jnp.where(me_ref[0] == d, p_refs[N_DEV][...].astype(F32), term)
                gl = term if gl is None else gl + term
            g = gl if g is None else jnp.where(layer == l, gl, g)
        mn = ADAM_B1 * m_ref[...] + (1.0 - ADAM_B1) * g
        vn = ADAM_B2 * v_ref[...] + (1.0 - ADAM_B2) * (g * g)
        m_hat = mn / (1.0 - ADAM_B1 ** ADAM_STEP)
        v_hat = vn / (1.0 - ADAM_B2 ** ADAM_STEP)
        g_ref[...] = g
        d_ref[...] = -ADAM_LR * (m_hat / (jnp.sqrt(v_hat) + ADAM_EPS) + ADAM_WD * w_ref[...])
        nm_ref[...] = mn
        nv_ref[...] = vn

    def rows(l, ll, i):
        return jnp.where(ll == l, i, jnp.where(ll < l, 0, nt - 1))

    def part_spec(l, d):
        if owns is None:
            return pl.BlockSpec((None, tr, c), lambda ll, i, me_ref: (d, rows(l, ll, i), 0))
        return pl.BlockSpec((None, tr, c),
                            lambda ll, i, me_ref: (jnp.where(me_ref[0] == d, (d + 1) % N_DEV, d), rows(l, ll, i), 0))

    def own_spec(l):
        return pl.BlockSpec((None, tr, c), lambda ll, i, me_ref: (me_ref[0], rows(l, ll, i), 0))

    in_specs, operands = [], []
    for l in range(nl):
        in_specs += [part_spec(l, d) for d in range(N_DEV)]
        operands += [parts[l]] * N_DEV
        if owns is not None:
            in_specs.append(own_spec(l))
            operands.append(owns[l])
    blk = pl.BlockSpec((None, tr, c), lambda ll, i, me_ref: (ll, i, 0))
    shp = jax.ShapeDtypeStruct((nl, r, c), F32)
    return pl.pallas_call(
        body, name=name, out_shape=(shp, shp, shp, shp),
        grid_spec=pltpu.PrefetchScalarGridSpec(
            num_scalar_prefetch=1, grid=(nl, nt), in_specs=in_specs + [blk, blk, blk],
            out_specs=(blk, blk, blk, blk)),
        compiler_params=_params(),
    )(me.reshape(1), *operands, w, m, v)


GROUPS = {"in": ("w_in",), "rest": ("w_out", "w_xq", "w_xo", "w_xkv", "w_up", "w_down")}
FULL_SHAPES = {"w_in": (D_MODEL, N_IN_PAD), "w_out": (D_MODEL, D_MODEL), "w_xq": (D_MODEL, D_MODEL),
               "w_xo": (D_MODEL, D_MODEL), "w_xkv": (N_DEV, D_MODEL, 2 * D_MODEL // N_DEV),
               "w_up": (N_DEV, FF_SHARD, D_MODEL), "w_down": (FF_HALF, FF_SHARD, D_MODEL)}
PIECE_SHAPES = {"w_in": (N_DEV, D_MODEL // N_DEV, N_IN_PAD), "w_out": (N_DEV, D_MODEL // N_DEV, D_MODEL),
                "w_xq": (N_DEV, D_MODEL // N_DEV, D_MODEL), "w_xo": (N_DEV, D_MODEL // N_DEV, D_MODEL),
                "w_xkv": (N_DEV, D_MODEL, 2 * D_MODEL // N_DEV), "w_up": (N_DEV, FF_SHARD, D_MODEL),
                "w_down": (N_DEV, D_FF // N_DEV, D_MODEL)}
GATHER_GROUPS = {"in": ("w_in",), "mid": ("w_out", "w_xq", "w_xo", "w_xkv"), "ffn": ("w_up", "w_down")}
CONV_WORDS = 8192


class _GatheredWeights:
    def __init__(self, states, layer):
        self.states, self.layer, self.full, self.extra = dict(states), layer, {}, None

    def need(self, group, after):
        if group in self.states:
            got, _ = _exchange_wait(self.states.pop(group), after, "gather_%s_wait_%d" % (group, self.layer))
            for name, g in zip(GATHER_GROUPS[group], got):
                self.full[name] = g.reshape(FULL_SHAPES[name])
            self.extra = got[len(GATHER_GROUPS[group]):]

    def __getitem__(self, name):
        return self.full[name]


def _relay_in_cols(w):
    pad = jnp.zeros(w.shape[:-1] + (N_IN_PAD - N_IN,), w.dtype)
    return jnp.concatenate([w[..., :2304], w[..., 2308:N_IN], w[..., 2304:2308], pad], axis=-1)


def _unrelay_in_cols(w):
    return jnp.concatenate([w[..., :2304], w[..., COL_GATE:COL_GATE + 4], w[..., 2304:COL_GATE]], axis=-1)


def _layer_fwd(h, memv, w, sm, tables, rest_arrived):
    sv = {"h0": h}
    s = h.shape[0]
    tm, tb = min(SLAB_TILE, s), min(MM_TILE, s)
    w.need("in", h)
    proj, xn = _norm_matmul(h, sm["g_mix"], w["w_in"], (s, N_IN_PAD), grid=(s // tm, 1),
                            b_spec=pl.BlockSpec((D_MODEL, N_IN_PAD), lambda i, j: (0, 0)),
                            o_spec=pl.BlockSpec((tm, N_IN_PAD), lambda i, j: (i, 0)), name="norm_mm_in")
    sv["xn"], sv["proj"] = xn, proj
    ycat = _sconv_fwd(proj, sm["w_sconv"])
    qd, kd, vd, ktd, vtd = _heads_split(proj, 1, tables["rope"], None, "split_dil")
    ycat, ob, lse_b = _attention_fwd("dil", qd, kd, vtd, tables["dil"], ycat, 1)
    sv["dil"] = (qd, kd, vd, ktd, ob, lse_b)
    c = _gate_cumsum(proj, sm["b_forget_pad"])
    qf, kf, vf, ktf, vtf = _heads_split(proj, 2, None, c, "split_fox")
    ycat, oc, lse_c = _attention_fwd("fox", qf, kf, vtf, tables["fox"], ycat, 2)
    sv["fox"] = (qf, kf, vf, ktf, oc, lse_c)
    ycat = _pool_fwd(proj, sm["w_pool_bd"], sm["pool_scale"], ycat)
    sv["ycat"] = ycat
    w.need("mid", ycat)
    h1 = _mm_nn(ycat, w["w_out"], "mm_out", res=h)
    sv["h1"] = h1
    memn = _rms_fwd(memv, sm["g_mem"], "rms_mem")
    qx, xq = _norm_matmul(h1, sm["g_xa"], w["w_xq"], (s, D_MODEL), grid=(s // tb, 1),
                          b_spec=pl.BlockSpec((D_MODEL, D_MODEL), lambda i, j: (0, 0)),
                          o_spec=pl.BlockSpec((tb, D_MODEL), lambda i, j: (i, 0)), name="norm_mm_xq",
                          out_dtype=BF16)
    kvm = _matmul(memn, w["w_xkv"], (N_DEV, MEM_LEN, XA_DIM), grid=(N_DEV, 1, 1),
                  a_spec=pl.BlockSpec((MEM_LEN, D_MODEL), lambda i, j, r: (0, 0)),
                  b_spec=pl.BlockSpec((None, D_MODEL, XA_DIM), lambda i, j, r: (i, 0, 0)),
                  o_spec=pl.BlockSpec((None, MEM_LEN, XA_DIM), lambda i, j, r: (i, 0, 0)),
                  dims=NN, nred=1, name="mm_xkv")
    ox = _xattn_fwd(qx, kvm)
    sv.update(xq=xq, memn=memn, qx=qx, kvm=kvm, ox=ox)
    h2 = _mm_nn(ox, w["w_xo"], "mm_xo", res=h1)
    sv["h2"] = h2
    w.need("ffn", ox)
    u0, xf = _norm_matmul(h2, sm["g_ffn"] + rest_arrived(w["w_down"]), w["w_up"], (N_DEV, s, FF_SHARD),
                          grid=(s // tb, N_DEV),
                          b_spec=pl.BlockSpec((None, FF_SHARD, D_MODEL), lambda i, j: (j, 0, 0)),
                          o_spec=pl.BlockSpec((None, tb, FF_SHARD), lambda i, j: (j, i, 0)), name="norm_mm_up",
                          out_dtype=BF16, dims=NT)
    act = _ffn_gate_fwd(u0, sm["w_ffconv"])
    sv.update(xf=xf, u0=u0, act=act)
    ospec = pl.BlockSpec((tm, D_MODEL), lambda i, j, r: (i, 0))
    h3 = _matmul(act, w["w_down"], (s, D_MODEL), grid=(s // tm, 1, 1),
                 a_spec=pl.BlockSpec((FF_HALF, tm, FF_SHARD), lambda i, j, r: (0, i, 0)),
                 b_spec=pl.BlockSpec((FF_HALF, FF_SHARD, D_MODEL), lambda i, j, r: (0, 0, 0)),
                 o_spec=ospec, dims=NN, nred=1, slabs=FF_HALF, name="mm_down", res=h2, res_spec=ospec)
    return h3, sv


def _layer_bwd(dh3, memv, w, sm, tables, sv, rest_ready, in_ready):
    s = dh3.shape[0]
    tm, tb = min(ROW_TILE, s), min(MM_TILE, s)
    big, small = {}, {}
    ts = max(s // 2, 1)
    dact = _matmul(dh3, w["w_down"], (FF_HALF, s, FF_SHARD), grid=(s // tb, FF_HALF, 1),
                   a_spec=pl.BlockSpec((tb, D_MODEL), lambda i, j, r: (i, 0)),
                   b_spec=pl.BlockSpec((None, FF_SHARD, D_MODEL), lambda i, j, r: (j, 0, 0)),
                   o_spec=pl.BlockSpec((None, tb, FF_SHARD), lambda i, j, r: (j, i, 0)),
                   dims=NT, nred=1, name="mm_dact", out_dtype=BF16)
    big["w_down"] = _matmul(sv["act"], dh3, (FF_HALF, FF_SHARD, D_MODEL), grid=(FF_HALF, 1, s // ts),
                            a_spec=pl.BlockSpec((None, ts, FF_SHARD), lambda i, j, r: (i, r, 0)),
                            b_spec=pl.BlockSpec((ts, D_MODEL), lambda i, j, r: (r, 0)),
                            o_spec=pl.BlockSpec((None, FF_SHARD, D_MODEL), lambda i, j, r: (i, 0, 0)),
                            dims=TN, nred=s // ts, name="mm_dw_down", out_dtype=GRAD_DTYPE)
    du0, small["w_ffconv"] = _ffn_gate_bwd(sv["u0"], sm["w_ffconv"], dact)
    dh2, small["g_ffn"] = _matmul_rms_bwd(du0, w["w_up"], sv["h2"], sm["g_ffn"], dh3, "mm_dxf_rms_bwd",
                                          tm=SLAB_TILE // 2, dims=NN)
    big["w_up"] = _matmul(du0, sv["xf"], (N_DEV, FF_SHARD, D_MODEL), grid=(N_DEV, 1, 1),
                          a_spec=pl.BlockSpec((None, s, FF_SHARD), lambda i, j, r: (i, 0, 0)),
                          b_spec=pl.BlockSpec((s, D_MODEL), lambda i, j, r: (0, 0)),
                          o_spec=pl.BlockSpec((None, FF_SHARD, D_MODEL), lambda i, j, r: (i, 0, 0)),
                          dims=TN, nred=1, name="mm_dw_up", out_dtype=GRAD_DTYPE)
    dox = _mm_nt(dh2, w["w_xo"], "mm_dox", out_dtype=BF16)
    big["w_xo"] = _mm_tn(sv["ox"], dh2, "mm_dw_xo")
    dqx, dkvm = _xattn_bwd(sv["qx"], sv["kvm"], dox)
    big["w_xq"] = _mm_tn(sv["xq"], dqx, "mm_dw_xq")
    big["w_xkv"] = _matmul(sv["memn"], dkvm, (N_DEV, D_MODEL, XA_DIM), grid=(N_DEV, 1, 1),
                           a_spec=pl.BlockSpec((MEM_LEN, D_MODEL), lambda i, j, r: (0, 0)),
                           b_spec=pl.BlockSpec((None, MEM_LEN, XA_DIM), lambda i, j, r: (i, 0, 0)),
                           o_spec=pl.BlockSpec((None, D_MODEL, XA_DIM), lambda i, j, r: (i, 0, 0)),
                           dims=TN, nred=1, name="mm_dw_xkv", out_dtype=GRAD_DTYPE)
    dmemn = _matmul(dkvm, w["w_xkv"], (MEM_LEN, D_MODEL), grid=(1, 1, 1),
                    a_spec=pl.BlockSpec((N_DEV, MEM_LEN, XA_DIM), lambda i, j, r: (0, 0, 0)),
                    b_spec=pl.BlockSpec((N_DEV, D_MODEL, XA_DIM), lambda i, j, r: (0, 0, 0)),
                    o_spec=pl.BlockSpec((MEM_LEN, D_MODEL), lambda i, j, r: (0, 0)),
                    dims=NT, nred=1, slabs=N_DEV, name="mm_dmemn")
    _, small["g_mem"] = _rms_bwd(dmemn, memv, sm["g_mem"], None, "rms_mem_bwd")
    dh1, small["g_xa"] = _matmul_rms_bwd(dqx, w["w_xq"], sv["h1"], sm["g_xa"], dh2, "mm_dxq_rms_bwd", tm=MM_TILE)
    big["w_out"] = _mm_tn(sv["ycat"], dh1, "mm_dw_out")
    dycat = _mm_nt(dh1, w["w_out"] + rest_ready(big, small).astype(BF16), "mm_dycat")
    proj = sv["proj"]
    dproj, small["w_sconv"] = _sconv_bwd(proj, sm["w_sconv"], dycat)
    qd, kd, vd, ktd, ob, lse_b = sv["dil"]
    delta, dob = _attention_delta(ob, dycat, 1)
    dqt, dk, dv = _attention_bwd("dil", qd, kd, vd, ktd, tables["dil"], dob, lse_b, delta)
    dproj = _heads_merge(dqt, dk, dv, tables["rope"], "merge_dil", dproj, 1)
    qf, kf, vf, ktf, oc, lse_c = sv["fox"]
    delta, dob = _attention_delta(oc, dycat, 2)
    dqt, dk, dv = _attention_bwd("fox", qf, kf, vf, ktf, tables["fox"], dob, lse_c, delta)
    dproj, dc = _heads_merge(dqt, dk, dv, None, "merge_fox", dproj, 2)
    dproj, dbias = _gate_cumsum_bwd(proj, sm["b_forget_pad"], dc, dproj)
    small["b_forget"] = dbias[0, :N_HEADS]
    dproj, dwbd, small["pool_scale"] = _pool_bwd(proj, sm["w_pool_bd"], sm["pool_scale"], dycat, dproj)
    small["w_pool"] = jnp.stack([dwbd[64 * g:64 * (g + 1), 64 * g:64 * (g + 1)] for g in range(4)])
    big["w_in"] = _mm_tn(sv["xn"], dproj, "mm_dw_in", tn=896)
    dh0, small["g_mix"] = _matmul_rms_bwd(dproj, w["w_in"], sv["h0"], sm["g_mix"] + in_ready(big, small), dh1,
                                          "mm_dxn_rms_bwd")
    return dh0, big, small


SMALL_NAMES = ("g_mix", "b_forget", "w_pool", "pool_scale", "g_xa", "g_mem", "g_ffn", "w_sconv", "w_ffconv")
SMALL_WITH = {"rest": ("w_ffconv", "g_ffn", "g_mem", "g_xa"),
              "in": ("w_sconv", "b_forget", "pool_scale", "w_pool")}
SMALL_SHAPES = {"w_sconv": (3, GROUP), "w_ffconv": (N_DEV, 3, FF_SHARD)}
WEIGHT_NAMES = ("g_mix", "w_in", "b_forget", "w_sconv", "w_pool", "pool_scale", "w_out", "g_xa", "g_mem", "w_xq",
                "w_xkv", "w_xo", "g_ffn", "w_up", "w_ffconv", "w_down", "g_final")


def _block_diag(w_pool):
    z = jnp.zeros((64, 64), F32)
    return jnp.concatenate(
        [jnp.concatenate([w_pool[g] if c == g else z for c in range(4)], axis=1) for g in range(4)], axis=0)


def kernel(x, mem, positions, g_mix, w_in, b_forget, w_sconv, w_pool, pool_scale, w_out, g_xa, g_mem, w_xq, w_xkv, w_xo, g_ffn, w_up, w_ffconv, w_down, g_final, loss_target, m_g_mix, m_w_in, m_b_forget, m_w_sconv, m_w_pool, m_pool_scale, m_w_out, m_g_xa, m_g_mem, m_w_xq, m_w_xkv, m_w_xo, m_g_ffn, m_w_up, m_w_ffconv, m_w_down, m_g_final, v_g_mix, v_w_in, v_b_forget, v_w_sconv, v_w_pool, v_pool_scale, v_w_out, v_g_xa, v_g_mem, v_w_xq, v_w_xkv, v_w_xo, v_g_ffn, v_w_up, v_w_ffconv, v_w_down, v_g_final):
    weights = dict(g_mix=g_mix, w_in=w_in, b_forget=b_forget, w_sconv=w_sconv, w_pool=w_pool, pool_scale=pool_scale,
                   w_out=w_out, g_xa=g_xa, g_mem=g_mem, w_xq=w_xq, w_xkv=w_xkv, w_xo=w_xo, g_ffn=g_ffn, w_up=w_up,
                   w_ffconv=w_ffconv, w_down=w_down, g_final=g_final)
    m_in = dict(g_mix=m_g_mix, w_in=m_w_in, b_forget=m_b_forget, w_sconv=m_w_sconv, w_pool=m_w_pool,
                pool_scale=m_pool_scale, w_out=m_w_out, g_xa=m_g_xa, g_mem=m_g_mem, w_xq=m_w_xq, w_xkv=m_w_xkv,
                w_xo=m_w_xo, g_ffn=m_g_ffn, w_up=m_w_up, w_ffconv=m_w_ffconv, w_down=m_w_down, g_final=m_g_final)
    v_in = dict(g_mix=v_g_mix, w_in=v_w_in, b_forget=v_b_forget, w_sconv=v_w_sconv, w_pool=v_w_pool,
                pool_scale=v_pool_scale, w_out=v_w_out, g_xa=v_g_xa, g_mem=v_g_mem, w_xq=v_w_xq, w_xkv=v_w_xkv,
                w_xo=v_w_xo, g_ffn=v_g_ffn, w_up=v_w_up, w_ffconv=v_w_ffconv, w_down=v_w_down, g_final=v_g_final)
    depth = w_in.shape[0]
    me = 4 * lax.axis_index("x") + 2 * lax.axis_index("y") + lax.axis_index("c")
    h = x[0]
    memv = mem[0]
    s = h.shape[0]
    tq = min(ATT_TQ, s)
    tables = {"rope": _rope_tables(positions[0]), "dil": _bias_tables("dil", tq, tq),
              "fox": _bias_tables("fox", tq, tq)}

    w_in_r = _relay_in_cols(w_in)
    conv_shard = jnp.concatenate([w_sconv.reshape(-1), w_ffconv.reshape(-1)])
    conv_shard = jnp.concatenate([conv_shard, jnp.zeros((CONV_WORDS - conv_shard.shape[0],), F32)])
    shards = dict(w_in=w_in_r, w_out=w_out, w_xq=w_xq, w_xo=w_xo, w_xkv=w_xkv, w_up=w_up.transpose(0, 2, 1),
                  w_down=w_down)
    gathered = [None] * depth

    def start_gathers(l, after):
        states = {}
        order = jnp.zeros((), F32)
        for group in GATHER_GROUPS:
            first = GATHER_GROUPS[group][0]
            shards[first] = shards[first] + order
            xs = [_place_shard(shards[name], l, me, BF16, "place_%s_%d" % (name, l), after)
                  for name in GATHER_GROUPS[group]]
            if l == 0 and group == "in":
                xs.append(_place_shard(conv_shard.reshape(1, CONV_WORDS // 1024, 1024), 0, me, F32, "place_conv"))
            states[group], token = _exchange_start(xs, False, "gather_%s_start_%d" % (group, l))
            order = order + token[0, 0]
        gathered[l] = _GatheredWeights(states, l)
        return order

    order = start_gathers(0, None)
    gathered[0].need("in", tables["rope"][0])
    conv_all = gathered[0].extra[0].reshape(N_DEV, CONV_WORDS)
    n_sc = depth * 3 * (GROUP // N_DEV)
    sconv_full = conv_all[:, :n_sc].reshape(N_DEV, depth, 3, GROUP // N_DEV).transpose(1, 2, 0, 3).reshape(
        depth, 3, GROUP)
    ffconv_full = conv_all[:, n_sc:n_sc + depth * 3 * FF_SHARD].reshape(N_DEV, depth, 3, FF_SHARD).transpose(
        1, 0, 2, 3)

    smalls = []
    for l in range(depth):
        smalls.append(dict(
            g_mix=g_mix[l], g_xa=g_xa[l], g_mem=g_mem[l], g_ffn=g_ffn[l], pool_scale=pool_scale[l],
            w_pool_bd=_block_diag(w_pool[l]), w_sconv=sconv_full[l], w_ffconv=ffconv_full[l],
            b_forget_pad=jnp.concatenate([b_forget[l], jnp.zeros((128 - N_HEADS,), F32)]).reshape(1, 128)))
    smalls[0]["g_mix"] = smalls[0]["g_mix"] + order

    saved = []
    for l in range(depth):
        def rest_arrived(arrived, l=l):
            return start_gathers(l + 1, arrived) if l + 1 < depth else jnp.zeros((), F32)

        h, sv = _layer_fwd(h, memv, gathered[l], smalls[l], tables, rest_arrived)
        saved.append(sv)
    loss_part, dh, dg_final = _loss_head(h, g_final, loss_target[0])
    loss = lax.psum(loss_part[0, 0], MESH_AXES)

    small_grads = [None] * depth
    scatters = {}

    def pieces_of(big, group):
        return [big[name].reshape(PIECE_SHAPES[name]) for name in GROUPS[group]]

    def rider(grads):
        flat = jnp.concatenate([g.reshape(-1) for g in grads])
        rows = -(-flat.shape[0] // 1024)
        flat = jnp.concatenate([flat, jnp.zeros((rows * 1024 - flat.shape[0],), F32)])
        return jnp.broadcast_to(flat.reshape(1, rows, 1024), (N_DEV, rows, 1024))

    riding = {}
    done_small = {}

    def start_scatter(l, group, big, extra):
        names = [(n, l) for n in SMALL_WITH[group]] + extra
        riding[l, group] = names
        grads = [dg_final if n == "g_final" else done_small[ll][n] for n, ll in names]
        scatters[l, group], token = _exchange_start(pieces_of(big, group) + [rider(grads)], True,
                                                    "scatter_%s_start_%d" % (group, l))
        return token[0, 0]

    for l in reversed(range(depth)):
        def rest_ready(big, small, l=l):
            done_small[l] = small
            extra = ([("g_final", l)] if l == depth - 1 else []) + ([("g_mix", l + 1)] if l + 1 < depth else [])
            return start_scatter(l, "rest", big, extra)

        def in_ready(big, small, l=l):
            return start_scatter(l, "in", big, [])

        dh, _, small_grads[l] = _layer_bwd(dh, memv, gathered[l], smalls[l], tables, saved[l], rest_ready, in_ready)
    grad_x = dh[None]
    riding["tail"] = [("g_mix", 0)]
    scatters["tail"], _ = _exchange_start([rider([small_grads[0]["g_mix"]])], True, "scatter_tail_start")

    parts, owns, small_parts = {}, {}, {}

    def take_rider(key, got, given):
        flat = lax.dynamic_update_slice_in_dim(got, given[:1], me, axis=0).reshape(N_DEV, -1)
        off = 0
        for name, ll in riding[key]:
            shape = SMALL_SHAPES.get(name, weights[name].shape[-1:] if name == "g_final" else weights[name].shape[1:])
            n = 1
            for dim in shape:
                n *= dim
            small_parts.setdefault(name, [None] * depth)[ll] = flat[:, off:off + n].reshape((N_DEV,) + shape)
            off += n

    def wait_group(group, after):
        for l in reversed(range(depth)):
            got, given = _exchange_wait(scatters[l, group], after, "scatter_%s_wait_%d" % (group, l))
            for name, g, x in zip(GROUPS[group], got, given):
                parts.setdefault(name, [None] * depth)[l] = g
                owns.setdefault(name, [None] * depth)[l] = x
            take_rider((l, group), got[-1], given[-1])

    results = {}

    def update(name, w3, m3, v3):
        outs = _adamw(parts[name], owns.get(name), me, w3, m3, v3, "adamw_" + name)
        results[name] = [o.reshape(weights[name].shape) for o in outs]

    wait_group("rest", grad_x)
    for name in GROUPS["rest"]:
        if name == "w_up":
            outs = _adamw(parts[name], owns[name], me, w_up.transpose(0, 2, 1), m_w_up.transpose(0, 2, 1),
                          v_w_up.transpose(0, 2, 1), "adamw_w_up")
            results[name] = [o.transpose(0, 2, 1) for o in outs]
        else:
            update(name, weights[name], m_in[name], v_in[name])
    wait_group("in", results["w_down"][1])
    outs = _adamw(parts["w_in"], owns["w_in"], me, w_in_r, _relay_in_cols(m_w_in), _relay_in_cols(v_w_in),
                  "adamw_w_in")
    results["w_in"] = [_unrelay_in_cols(o) for o in outs]
    got, given = _exchange_wait(scatters["tail"], results["w_in"][1], "scatter_tail_wait")
    take_rider("tail", got[0], given[0])
    for name in SMALL_NAMES + ("g_final",):
        wv = weights[name]
        p = small_parts[name][depth - 1] if name == "g_final" else jnp.stack(small_parts[name], axis=1)
        if name == "w_sconv":
            p = lax.dynamic_slice_in_dim(p, me * (GROUP // N_DEV), GROUP // N_DEV, axis=3)
        elif name == "w_ffconv":
            p = lax.dynamic_index_in_dim(p, me, axis=2, keepdims=False)
        shape3 = (1, 1, wv.shape[0]) if wv.ndim == 1 else (1, -1, wv.shape[-1])
        w3 = wv.reshape(shape3)
        parts[name] = [p.reshape((N_DEV,) + w3.shape[1:])]
        update(name, w3, m_in[name].reshape(shape3), v_in[name].reshape(shape3))

    return (loss, grad_x, *[results[n][0] for n in WEIGHT_NAMES], *[results[n][1] for n in WEIGHT_NAMES],
            *[results[n][2] for n in WEIGHT_NAMES], *[results[n][3] for n in WEIGHT_NAMES])
```

```python
import functools

import jax
import jax.numpy as jnp
from jax import lax
from jax.experimental import pallas as pl
from jax.experimental.pallas import tpu as pltpu

F32 = jnp.float32
BF16 = jnp.bfloat16

N_DEV = 8
D_MODEL = 1024
GROUP = 256
HEAD_DIM = 64
N_HEADS = 4
N_IN = 2564
N_IN_PAD = 2688
COL_GATE = 2560
XA_HEADS = 4
XA_DIM = 256
MEM_LEN = 256
D_FF = 2816
FF_SHARD = 704
FF_HALF = 4
ROPE_THETA = 500000.0
ROPE_DIM = 16
RMS_EPS = 1e-6
NEG = -1e30
POOL_WINDOWS = (2, 4, 8, 16)
ADAM_LR, ADAM_B1, ADAM_B2, ADAM_EPS, ADAM_WD, ADAM_STEP = 0.001, 0.9, 0.999, 1e-08, 0.01, 10

ROW_TILE = 1024
MM_TILE = 1024
SLAB_TILE = 512
ATT_TQ = 512
BWD_HEADS = 4
VMEM_LIMIT = 56 * 1024 * 1024
ADAMW_BLOCK_BYTES = 8 * 1024 * 1024
PLACE_BLOCK_BYTES = 4 * 1024 * 1024
TN_BLOCK_BYTES = 12 * 1024 * 1024

MESH_AXES = ("x", "y", "c")


def _params(**kw):
    return pltpu.CompilerParams(vmem_limit_bytes=VMEM_LIMIT, **kw)


HBM_SPEC = pl.BlockSpec(memory_space=pltpu.HBM)
SEM_SPEC = pl.BlockSpec(memory_space=pltpu.SEMAPHORE)
DATAFLOW = pltpu.SideEffectType.DATAFLOW_SIDE_EFFECTING


def _peer_copies(x_ref, land_ref, send_sems, recv_sems, scatter):
    mx, my, mc = lax.axis_index("x"), lax.axis_index("y"), lax.axis_index("c")
    me = 4 * mx + 2 * my + mc
    pairs = []
    for k in range(1, N_DEV):
        kx, ky, kc = (k >> 2) & 1, (k >> 1) & 1, k & 1
        peer_lin = me ^ k
        send = pltpu.make_async_remote_copy(
            src_ref=x_ref.at[peer_lin] if scatter else land_ref.at[me], dst_ref=land_ref.at[me],
            send_sem=send_sems.at[k - 1], recv_sem=recv_sems.at[k - 1],
            device_id=(mx ^ kx, my ^ ky, mc ^ kc), device_id_type=pl.DeviceIdType.MESH)
        arrival = pltpu.make_async_remote_copy(
            src_ref=land_ref.at[peer_lin], dst_ref=land_ref.at[peer_lin],
            send_sem=send_sems.at[k - 1], recv_sem=recv_sems.at[k - 1],
            device_id=(mx, my, mc), device_id_type=pl.DeviceIdType.MESH)
        pairs.append((send, arrival))
    return pairs


def _exchange_start(xs, scatter, name):
    n = len(xs)
    ns = n if scatter else 0

    def body(*refs):
        srcs = refs[:ns] if scatter else (None,) * n
        lands, sends, recvs = refs[ns:ns + n], refs[ns + n:ns + 2 * n], refs[ns + 2 * n:ns + 3 * n]
        for t in range(n):
            for send, _ in _peer_copies(srcs[t], lands[t], sends[t], recvs[t], scatter):
                send.start()
        token = refs[-1]
        token[...] = jnp.zeros_like(token)

    sems = pltpu.SemaphoreType.DMA((N_DEV - 1,))
    operands = [pltpu.with_memory_space_constraint(x, pltpu.HBM) for x in xs]
    if scatter:
        operands += [pltpu.with_memory_space_constraint(lax.empty(x.shape, x.dtype), pltpu.HBM) for x in xs]
    outs = pl.pallas_call(
        body, name=name,
        out_shape=(sems,) * (2 * n) + tuple(pltpu.HBM(a.shape, a.dtype) for a in operands)
        + (jax.ShapeDtypeStruct((8, 128), F32),),
        in_specs=(HBM_SPEC,) * (ns + n),
        out_specs=(SEM_SPEC,) * (2 * n) + (HBM_SPEC,) * (ns + n) + (pl.BlockSpec(memory_space=pltpu.VMEM),),
        input_output_aliases={i: 2 * n + i for i in range(ns + n)},
        compiler_params=pltpu.CompilerParams(has_side_effects=DATAFLOW),
    )(*operands)
    return (outs[:-1], scatter), outs[-1]


def _exchange_wait(state, after, name):
    held, scatter = state
    n = len(held) // (4 if scatter else 3)
    ns = n if scatter else 0
    sems, thru = held[:2 * n], held[2 * n:]

    def body(*refs):
        srcs = refs[:ns] if scatter else (None,) * n
        lands, sends, recvs = refs[ns:ns + n], refs[ns + n:ns + 2 * n], refs[ns + 2 * n:ns + 3 * n]
        for t in range(n):
            for send, arrival in _peer_copies(srcs[t], lands[t], sends[t], recvs[t], scatter):
                send.wait_send()
                arrival.wait_recv()

    outs = pl.pallas_call(
        body, name=name,
        out_shape=tuple(pltpu.HBM(a.shape, a.dtype) for a in thru),
        in_specs=(HBM_SPEC,) * (ns + n) + (SEM_SPEC,) * (2 * n) + (pl.BlockSpec(memory_space=pl.ANY),),
        out_specs=(HBM_SPEC,) * (ns + n), input_output_aliases={i: i for i in range(ns + n)},
        compiler_params=pltpu.CompilerParams(has_side_effects=DATAFLOW),
    )(*thru, *sems, after)
    return list(outs[ns:]), list(outs[:ns])


def _place_shard(x, layer, me, dtype, name, after=None):
    _, r, c = x.shape
    tr = r
    if r * c * 4 > PLACE_BLOCK_BYTES:
        for cand in (512, 256, 128, 64, 32, 16):
            if r % cand == 0 and cand * c * 4 <= PLACE_BLOCK_BYTES:
                tr = cand
                break

    def body(me_ref, x_ref, *rest):
        o_ref = rest[-1]
        o_ref[...] = x_ref[...].astype(o_ref.dtype)

    return pl.pallas_call(
        body, name=name, out_shape=jax.ShapeDtypeStruct((N_DEV, r, c), dtype),
        grid_spec=pltpu.PrefetchScalarGridSpec(
            num_scalar_prefetch=1, grid=(r // tr,),
            in_specs=[pl.BlockSpec((None, tr, c), lambda i, me_ref: (layer, i, 0))]
            + ([ANY_SPEC] if after is not None else []),
            out_specs=pl.BlockSpec((None, tr, c), lambda i, me_ref: (me_ref[0], i, 0))),
        compiler_params=_params(),
    )(*((me.reshape(1), x) + ((after,) if after is not None else ())))


NN = ((1,), (0,))
NT = ((1,), (1,))
TN = ((0,), (0,))


def _matmul(a, b, out_shape, *, grid, a_spec, b_spec, o_spec, dims, nred, name, res=None, res_spec=None,
            out_dtype=F32, slabs=0):
    has_res = res is not None

    def body(*refs):
        a_ref, b_ref = refs[0], refs[1]
        r_ref = refs[2] if has_res else None
        o_ref = refs[3] if has_res else refs[2]
        if slabs:
            part = None
            for n in range(slabs):
                term = lax.dot_general(a_ref[n].astype(BF16), b_ref[n].astype(BF16), (dims, ((), ())),
                                       preferred_element_type=F32)
                part = term if part is None else part + term
        else:
            part = lax.dot_general(a_ref[...].astype(BF16), b_ref[...].astype(BF16), (dims, ((), ())),
                                   preferred_element_type=F32)
        if nred == 1:
            if has_res:
                part = part + r_ref[...]
            o_ref[...] = part.astype(o_ref.dtype)
        else:
            acc = refs[-1]
            r = pl.program_id(2)

            @pl.when(r == 0)
            def _():
                acc[...] = part

            @pl.when(r > 0)
            def _():
                acc[...] += part

            @pl.when(r == nred - 1)
            def _():
                tot = acc[...]
                if has_res:
                    tot = tot + r_ref[...]
                o_ref[...] = tot.astype(o_ref.dtype)

    in_specs = [a_spec, b_spec] + ([res_spec] if has_res else [])
    args = (a, b) + ((res,) if has_res else ())
    acc_shape = tuple(d for d in o_spec.block_shape if d is not None)
    return pl.pallas_call(
        body, name=name, grid=grid, out_shape=jax.ShapeDtypeStruct(out_shape, out_dtype),
        in_specs=in_specs, out_specs=o_spec,
        scratch_shapes=[pltpu.VMEM(acc_shape, F32)] if nred > 1 else [],
        compiler_params=_params(),
    )(*args)


def _mm_nn(a, w, name, res=None, tn=None, out_dtype=F32):
    m, k = a.shape
    n = w.shape[1]
    tn = tn or n
    tm = min(MM_TILE, m)
    ospec = pl.BlockSpec((tm, tn), lambda i, j, r: (i, j))
    return _matmul(a, w, (m, n), grid=(m // tm, n // tn, 1),
                   a_spec=pl.BlockSpec((tm, k), lambda i, j, r: (i, 0)),
                   b_spec=pl.BlockSpec((k, tn), lambda i, j, r: (0, j)),
                   o_spec=ospec, dims=NN, nred=1, name=name, res=res, res_spec=ospec if res is not None else None,
                   out_dtype=out_dtype)


def _mm_nt(a, w, name, out_dtype=F32):
    m, n = a.shape
    k = w.shape[0]
    tm = min(MM_TILE, m)
    return _matmul(a, w, (m, k), grid=(m // tm, 1, 1),
                   a_spec=pl.BlockSpec((tm, n), lambda i, j, r: (i, 0)),
                   b_spec=pl.BlockSpec((k, n), lambda i, j, r: (0, 0)),
                   o_spec=pl.BlockSpec((tm, k), lambda i, j, r: (i, 0)), dims=NT, nred=1, name=name,
                   out_dtype=out_dtype)


def _norm_matmul(h, g, b, out_shape, *, grid, b_spec, o_spec, name, out_dtype=F32, dims=NN):
    s, d = h.shape
    tm = s // grid[0]

    def body(h_ref, g_ref, b_ref, o_ref, xn_ref):
        @pl.when(pl.program_id(1) == 0)
        def _():
            hv = h_ref[...]
            r = lax.rsqrt(jnp.mean(hv * hv, axis=-1, keepdims=True) + RMS_EPS)
            xn_ref[...] = (hv * r * g_ref[...]).astype(xn_ref.dtype)

        if len(b_ref.shape) == 3:
            for n in range(b_ref.shape[0]):
                o_ref[n] = lax.dot_general(xn_ref[...], b_ref[n].astype(BF16), (dims, ((), ())),
                                           preferred_element_type=F32).astype(o_ref.dtype)
        else:
            o_ref[...] = lax.dot_general(xn_ref[...], b_ref[...].astype(BF16), (dims, ((), ())),
                                         preferred_element_type=F32).astype(o_ref.dtype)

    row = pl.BlockSpec((tm, d), lambda i, j: (i, 0))
    return pl.pallas_call(
        body, name=name, grid=grid,
        out_shape=(jax.ShapeDtypeStruct(out_shape, out_dtype), jax.ShapeDtypeStruct((s, d), BF16)),
        in_specs=[row, pl.BlockSpec((1, d), lambda i, j: (0, 0)), b_spec],
        out_specs=(o_spec, row), compiler_params=_params(),
    )(h, g.reshape(1, d), b)


def _matmul_rms_bwd(a, w, h, g, res, name, tm=SLAB_TILE, dims=NT):
    slabs = a.shape[0] if a.ndim == 3 else 0
    s, n = a.shape[-2:]
    d = w.shape[-2] if dims == NT else w.shape[-1]
    tm = min(tm, s)

    def body(a_ref, w_ref, h_ref, g_ref, r_ref, dh_ref, dg_ref):
        if slabs:
            dy = None
            for j in range(slabs):
                term = lax.dot_general(a_ref[j].astype(BF16), w_ref[j].astype(BF16), (dims, ((), ())),
                                       preferred_element_type=F32)
                dy = term if dy is None else dy + term
        else:
            dy = lax.dot_general(a_ref[...].astype(BF16), w_ref[...].astype(BF16), (dims, ((), ())),
                                 preferred_element_type=F32)
        hv = h_ref[...]
        r = lax.rsqrt(jnp.mean(hv * hv, axis=-1, keepdims=True) + RMS_EPS)
        hn = hv * r
        u = dy * g_ref[...]
        dh_ref[...] = r * (u - hn * jnp.mean(u * hn, axis=-1, keepdims=True)) + r_ref[...]
        part = jnp.sum(dy * hn, axis=0, keepdims=True)

        @pl.when(pl.program_id(0) == 0)
        def _():
            dg_ref[...] = part

        @pl.when(pl.program_id(0) > 0)
        def _():
            dg_ref[...] += part

    row = pl.BlockSpec((tm, d), lambda i: (i, 0))
    vec = pl.BlockSpec((1, d), lambda i: (0, 0))
    if slabs:
        a_spec = pl.BlockSpec((slabs, tm, n), lambda i: (0, i, 0))
        w_spec = pl.BlockSpec(w.shape, lambda i: (0, 0, 0))
    else:
        a_spec = pl.BlockSpec((tm, n), lambda i: (i, 0))
        w_spec = pl.BlockSpec(w.shape, lambda i: (0, 0))
    dh, dg = pl.pallas_call(
        body, name=name, grid=(s // tm,),
        out_shape=(jax.ShapeDtypeStruct((s, d), F32), jax.ShapeDtypeStruct((1, d), F32)),
        in_specs=[a_spec, w_spec, row, vec, row], out_specs=(row, vec), compiler_params=_params(),
    )(a, w, h, g.reshape(1, d), res)
    return dh, dg.reshape(d)


GRAD_DTYPE = BF16


def _mm_tn(a, b, name, tk=512, tn=None):
    s, k = a.shape
    n = b.shape[1]
    tn = tn or n
    tk = min(tk, k)
    ts = s if b.dtype == BF16 and tn * s * 2 <= TN_BLOCK_BYTES else max(s // 2, 1)
    return _matmul(a, b, (k, n), grid=(k // tk, n // tn, s // ts),
                   a_spec=pl.BlockSpec((ts, tk), lambda i, j, r: (r, i)),
                   b_spec=pl.BlockSpec((ts, tn), lambda i, j, r: (r, j)),
                   o_spec=pl.BlockSpec((tk, tn), lambda i, j, r: (i, j)), dims=TN, nred=s // ts, name=name,
                   out_dtype=GRAD_DTYPE)


def _rms_fwd(h, g, name):
    s, d = h.shape
    tm = min(ROW_TILE, s)

    def body(h_ref, g_ref, o_ref):
        hv = h_ref[...]
        r = lax.rsqrt(jnp.mean(hv * hv, axis=-1, keepdims=True) + RMS_EPS)
        o_ref[...] = (hv * r * g_ref[...]).astype(o_ref.dtype)

    return pl.pallas_call(
        body, name=name, grid=(s // tm,), out_shape=jax.ShapeDtypeStruct((s, d), BF16),
        in_specs=[pl.BlockSpec((tm, d), lambda i: (i, 0)), pl.BlockSpec((1, d), lambda i: (0, 0))],
        out_specs=pl.BlockSpec((tm, d), lambda i: (i, 0)), compiler_params=_params(),
    )(h, g.reshape(1, d))


def _rms_bwd(dy, h, g, res, name):
    s, d = h.shape
    tm = min(ROW_TILE, s)
    has_res = res is not None

    def body(*refs):
        dy_ref, h_ref, g_ref = refs[:3]
        r_ref = refs[3] if has_res else None
        dh_ref, dg_ref = refs[-2], refs[-1]
        hv = h_ref[...]
        r = lax.rsqrt(jnp.mean(hv * hv, axis=-1, keepdims=True) + RMS_EPS)
        hn = hv * r
        dyv = dy_ref[...].astype(F32)
        u = dyv * g_ref[...]
        dh = r * (u - hn * jnp.mean(u * hn, axis=-1, keepdims=True))
        if has_res:
            dh = dh + r_ref[...]
        dh_ref[...] = dh
        part = jnp.sum(dyv * hn, axis=0, keepdims=True)

        @pl.when(pl.program_id(0) == 0)
        def _():
            dg_ref[...] = part

        @pl.when(pl.program_id(0) > 0)
        def _():
            dg_ref[...] += part

    row = pl.BlockSpec((tm, d), lambda i: (i, 0))
    vec = pl.BlockSpec((1, d), lambda i: (0, 0))
    dh, dg = pl.pallas_call(
        body, name=name, grid=(s // tm,),
        out_shape=(jax.ShapeDtypeStruct((s, d), F32), jax.ShapeDtypeStruct((1, d), F32)),
        in_specs=[row, row, vec] + ([row] if has_res else []),
        out_specs=(row, vec), compiler_params=_params(),
    )(*((dy, h, g.reshape(1, d)) + ((res,) if has_res else ())))
    return dh, dg.reshape(d)


def _loss_head(h, g, target):
    s, d = h.shape
    tm = min(ROW_TILE, s)

    def body(h_ref, g_ref, t_ref, loss_ref, dh_ref, dg_ref):
        hv = h_ref[...]
        r = lax.rsqrt(jnp.mean(hv * hv, axis=-1, keepdims=True) + RMS_EPS)
        hn = hv * r
        gv = g_ref[...]
        err = hn * gv - t_ref[...]
        rows = jnp.mean(err * err, axis=-1, keepdims=True)
        lpart = 0.5 * jnp.sum(rows, axis=0, keepdims=True) + jnp.zeros((1, 128), F32)
        dy = err * (1.0 / d)
        u = dy * gv
        dh_ref[...] = r * (u - hn * jnp.mean(u * hn, axis=-1, keepdims=True))
        gpart = jnp.sum(dy * hn, axis=0, keepdims=True)

        @pl.when(pl.program_id(0) == 0)
        def _():
            dg_ref[...] = gpart
            loss_ref[...] = lpart

        @pl.when(pl.program_id(0) > 0)
        def _():
            dg_ref[...] += gpart
            loss_ref[...] += lpart

    row = pl.BlockSpec((tm, d), lambda i: (i, 0))
    vec = pl.BlockSpec((1, d), lambda i: (0, 0))
    return pl.pallas_call(
        body, name="loss_head", grid=(s // tm,),
        out_shape=(jax.ShapeDtypeStruct((1, 128), F32), jax.ShapeDtypeStruct((s, d), F32),
                   jax.ShapeDtypeStruct((1, d), F32)),
        in_specs=[row, vec, row],
        out_specs=(pl.BlockSpec((1, 128), lambda i: (0, 0)), row, vec), compiler_params=_params(),
    )(h, g.reshape(1, d), target)


def _shift_down(x, k):
    return pltpu.roll(x, k, 0)


def _shift_up(x, k):
    return pltpu.roll(x, x.shape[0] - k, 0)


def _conv3(x, w):
    return w[2:3, :] * x + w[1:2, :] * _shift_down(x, 1) + w[0:1, :] * _shift_down(x, 2)


def _conv3_t(x, w):
    return w[2:3, :] * x + w[1:2, :] * _shift_up(x, 1) + w[0:1, :] * _shift_up(x, 2)


def _sigmoid(x):
    return 1.0 / (1.0 + jnp.exp(-x))


def _prev_map(tile, halo, col):
    return lambda i: (jnp.maximum(i * (tile // halo) - 1, 0), col)


def _next_map(tile, halo, col, nrows):
    return lambda i: (jnp.minimum((i + 1) * (tile // halo), nrows // halo - 1), col)


def _sconv_fwd(proj, w):
    s = proj.shape[0]
    t = min(ROW_TILE, s)

    def body(cur_ref, prev_ref, w_ref, o_ref):
        i = pl.program_id(0)
        prev = prev_ref[...] * (i > 0).astype(F32)
        ext = jnp.concatenate([prev, cur_ref[...]], axis=0)
        sv = ext[:, 2 * GROUP:3 * GROUP] * ext[:, 0:GROUP]
        y = ext[:, GROUP:2 * GROUP] * _conv3(sv, w_ref[...])
        o_ref[...] = y[8:].astype(o_ref.dtype)

    return pl.pallas_call(
        body, name="sconv_fwd", grid=(s // t,), out_shape=jax.ShapeDtypeStruct((s, 4 * GROUP), BF16),
        in_specs=[pl.BlockSpec((t, 3 * GROUP), lambda i: (i, 0)),
                  pl.BlockSpec((8, 3 * GROUP), _prev_map(t, 8, 0)),
                  pl.BlockSpec((3, GROUP), lambda i: (0, 0))],
        out_specs=pl.BlockSpec((t, GROUP), lambda i: (i, 0)), compiler_params=_params(),
    )(proj, proj, w)


def _sconv_bwd(proj, w, dy):
    s = proj.shape[0]
    t = min(ROW_TILE, s)
    nt = s // t

    def body(cur_ref, prev_ref, next_ref, w_ref, dy_ref, dyn_ref, dp_ref, dw_ref):
        i = pl.program_id(0)
        first = (i > 0).astype(F32)
        last = (i < nt - 1).astype(F32)
        ext = jnp.concatenate([prev_ref[...] * first, cur_ref[...], next_ref[...] * last], axis=0)
        dye = jnp.concatenate([jnp.zeros((8, GROUP), F32), dy_ref[...], dyn_ref[...] * last], axis=0)
        hv, bv, cv = ext[:, 0:GROUP], ext[:, GROUP:2 * GROUP], ext[:, 2 * GROUP:3 * GROUP]
        wv = w_ref[...]
        sv = cv * hv
        conv = _conv3(sv, wv)
        dconv = dye * bv
        ds = _conv3_t(dconv, wv)
        dp = jnp.concatenate([ds * cv, dye * conv, ds * hv], axis=1)
        dp_ref[...] = dp[8:8 + t].astype(dp_ref.dtype)
        dc = dconv[8:8 + t]
        dw = jnp.concatenate([
            jnp.sum(dc * _shift_down(sv, 2)[8:8 + t], axis=0, keepdims=True),
            jnp.sum(dc * _shift_down(sv, 1)[8:8 + t], axis=0, keepdims=True),
            jnp.sum(dc * sv[8:8 + t], axis=0, keepdims=True),
            jnp.zeros((5, GROUP), F32)], axis=0)

        @pl.when(i == 0)
        def _():
            dw_ref[...] = dw

        @pl.when(i > 0)
        def _():
            dw_ref[...] += dw

    dp, dw = pl.pallas_call(
        body, name="sconv_bwd", grid=(nt,),
        out_shape=(jax.ShapeDtypeStruct((s, N_IN_PAD), BF16), jax.ShapeDtypeStruct((8, GROUP), F32)),
        in_specs=[pl.BlockSpec((t, 3 * GROUP), lambda i: (i, 0)),
                  pl.BlockSpec((8, 3 * GROUP), _prev_map(t, 8, 0)),
                  pl.BlockSpec((8, 3 * GROUP), _next_map(t, 8, 0, s)),
                  pl.BlockSpec((3, GROUP), lambda i: (0, 0)),
                  pl.BlockSpec((t, GROUP), lambda i: (i, 0)),
                  pl.BlockSpec((8, GROUP), _next_map(t, 8, 0, s))],
        out_specs=(pl.BlockSpec((t, 3 * GROUP), lambda i: (i, 0)), pl.BlockSpec((8, GROUP), lambda i: (0, 0))),
        compiler_params=_params(),
    )(proj, proj, proj, w, dy, dy)
    return dp, dw[:3]


def _lane_window(shape):
    lane = lax.broadcasted_iota(jnp.int32, shape, 1)
    return lane, jnp.where(lane < 64, 2.0, jnp.where(lane < 128, 4.0, jnp.where(lane < 192, 8.0, 16.0)))


def _by_group(lane, s1, s2, s3, s4):
    return jnp.where(lane < 64, s1, jnp.where(lane < 128, s2, jnp.where(lane < 192, s3, s4)))


def _pool_z(ext, row0):
    s1 = ext + _shift_down(ext, 1)
    s2 = s1 + _shift_down(s1, 2)
    s3 = s2 + _shift_down(s2, 4)
    s4 = s3 + _shift_down(s3, 8)
    lane, win = _lane_window(ext.shape)
    tpos = (lax.broadcasted_iota(jnp.int32, ext.shape, 0) + (row0 - 16 + 1)).astype(F32)
    cnt = jnp.maximum(jnp.minimum(tpos, win), 1.0)
    return _by_group(lane, s1, s2, s3, s4) / cnt - ext


ANY_SPEC = pl.BlockSpec(memory_space=pl.ANY)


def _pool_fwd(proj, wbd, scale, ybuf):
    s = proj.shape[0]
    t = min(ROW_TILE, s)
    col = (COL_GATE - GROUP) // GROUP

    def body(cur_ref, prev_ref, w_ref, sc_ref, buf_ref, o_ref):
        i = pl.program_id(0)
        ext = jnp.concatenate([prev_ref[...] * (i > 0).astype(F32), cur_ref[...]], axis=0)
        z = _pool_z(ext, i * t)[16:]
        y = jnp.dot(z.astype(BF16), w_ref[...].astype(BF16), preferred_element_type=F32)
        o_ref[...] = (y * sc_ref[...]).astype(o_ref.dtype)

    return pl.pallas_call(
        body, name="pool_fwd", grid=(s // t,), out_shape=jax.ShapeDtypeStruct(ybuf.shape, ybuf.dtype),
        in_specs=[pl.BlockSpec((t, GROUP), lambda i: (i, col)),
                  pl.BlockSpec((16, GROUP), _prev_map(t, 16, col)),
                  pl.BlockSpec((GROUP, GROUP), lambda i: (0, 0)),
                  pl.BlockSpec((1, GROUP), lambda i: (0, 0)), ANY_SPEC],
        out_specs=pl.BlockSpec((t, GROUP), lambda i: (i, 3)), input_output_aliases={4: 0},
        compiler_params=_params(),
    )(proj, proj, wbd, scale.reshape(1, GROUP), ybuf)


def _pool_bwd(proj, wbd, scale, dy, dbuf):
    s = proj.shape[0]
    t = min(ROW_TILE, s)
    nt = s // t
    col = (COL_GATE - GROUP) // GROUP

    def body(cur_ref, prev_ref, w_ref, sc_ref, dy_ref, dyn_ref, buf_ref, dp_ref, dw_ref, dsc_ref):
        i = pl.program_id(0)
        ext = jnp.concatenate([prev_ref[...] * (i > 0).astype(F32), cur_ref[...]], axis=0)
        z = _pool_z(ext, i * t)[16:]
        wv = w_ref[...].astype(BF16)
        dyc = dy_ref[...]
        dye = jnp.concatenate([dyc, dyn_ref[...] * (i < nt - 1).astype(F32)], axis=0) * sc_ref[...]
        dz = lax.dot_general(dye.astype(BF16), wv, (NT, ((), ())), preferred_element_type=F32)
        lane, win = _lane_window(dz.shape)
        tpos = (lax.broadcasted_iota(jnp.int32, dz.shape, 0) + (i * t + 1)).astype(F32)
        e = dz / jnp.minimum(tpos, win)
        f1 = e + _shift_up(e, 1)
        f2 = f1 + _shift_up(f1, 2)
        f3 = f2 + _shift_up(f2, 4)
        f4 = f3 + _shift_up(f3, 8)
        dp = _by_group(lane, f1, f2, f3, f4) - dz
        dp_ref[...] = dp[:t].astype(dp_ref.dtype)
        zb = z.astype(BF16)
        y = jnp.dot(zb, wv, preferred_element_type=F32)
        dsc = jnp.sum(dyc * y, axis=0, keepdims=True)
        dw = lax.dot_general(zb, dye[:t].astype(BF16), (TN, ((), ())), preferred_element_type=F32)

        @pl.when(i == 0)
        def _():
            dw_ref[...] = dw
            dsc_ref[...] = dsc

        @pl.when(i > 0)
        def _():
            dw_ref[...] += dw
            dsc_ref[...] += dsc

    dp, dw, dsc = pl.pallas_call(
        body, name="pool_bwd", grid=(nt,),
        out_shape=(jax.ShapeDtypeStruct(dbuf.shape, dbuf.dtype), jax.ShapeDtypeStruct((GROUP, GROUP), F32),
                   jax.ShapeDtypeStruct((1, GROUP), F32)),
        in_specs=[pl.BlockSpec((t, GROUP), lambda i: (i, col)),
                  pl.BlockSpec((16, GROUP), _prev_map(t, 16, col)),
                  pl.BlockSpec((GROUP, GROUP), lambda i: (0, 0)),
                  pl.BlockSpec((1, GROUP), lambda i: (0, 0)),
                  pl.BlockSpec((t, GROUP), lambda i: (i, 3)),
                  pl.BlockSpec((16, GROUP), _next_map(t, 16, 3, s)), ANY_SPEC],
        out_specs=(pl.BlockSpec((t, GROUP), lambda i: (i, col)), pl.BlockSpec((GROUP, GROUP), lambda i: (0, 0)),
                   pl.BlockSpec((1, GROUP), lambda i: (0, 0))),
        input_output_aliases={6: 0}, compiler_params=_params(),
    )(proj, proj, wbd, scale.reshape(1, GROUP), dy, dy, dbuf)
    return dp, dw, dsc.reshape(GROUP)


FF_HALO = 16


def _ffn_gate_fwd(u0, w):
    s = u0.shape[1]
    t = min(ROW_TILE, s)

    def body(a_ref, ap_ref, g_ref, gp_ref, wa_ref, wg_ref, o_ref):
        first = (pl.program_id(1) > 0).astype(F32)
        a = _conv3(jnp.concatenate([ap_ref[...] * first, a_ref[...].astype(F32)], axis=0), wa_ref[...])[FF_HALO:]
        g = _conv3(jnp.concatenate([gp_ref[...] * first, g_ref[...].astype(F32)], axis=0), wg_ref[...])[FF_HALO:]
        o_ref[...] = (a * (g * _sigmoid(g))).astype(o_ref.dtype)

    def cur(off):
        return pl.BlockSpec((None, t, FF_SHARD), lambda j, i: (j + off, i, 0))

    def prev(off):
        return pl.BlockSpec((None, FF_HALO, FF_SHARD),
                            lambda j, i: (j + off, jnp.maximum(i * (t // FF_HALO) - 1, 0), 0))

    def wspec(off):
        return pl.BlockSpec((None, 3, FF_SHARD), lambda j, i: (j + off, 0, 0))

    return pl.pallas_call(
        body, name="ffn_gate_fwd", grid=(FF_HALF, s // t),
        out_shape=jax.ShapeDtypeStruct((FF_HALF, s, FF_SHARD), BF16),
        in_specs=[cur(0), prev(0), cur(FF_HALF), prev(FF_HALF), wspec(0), wspec(FF_HALF)],
        out_specs=pl.BlockSpec((None, t, FF_SHARD), lambda j, i: (j, i, 0)), compiler_params=_params(),
    )(u0, u0, u0, u0, w, w)


def _ffn_gate_bwd(u0, w, dact):
    s = u0.shape[1]
    t = min(ROW_TILE, s)
    nt = s // t

    def body(c_ref, p_ref, n_ref, w_ref, d_ref, dn_ref, du_ref, dw_ref):
        i = pl.program_id(1)
        first = (i > 0).astype(F32)
        last = (i < nt - 1).astype(F32)
        dext = jnp.concatenate([jnp.zeros((FF_HALO, FF_SHARD), F32), d_ref[...].astype(F32), dn_ref[...] * last],
                               axis=0)
        ext = [jnp.concatenate([p_ref[n] * first, c_ref[n].astype(F32), n_ref[n] * last], axis=0) for n in range(2)]
        a = _conv3(ext[0], w_ref[0])
        g = _conv3(ext[1], w_ref[1])
        sg = _sigmoid(g)
        silu = g * sg
        dus = (dext * silu, dext * a * (sg + silu * (1.0 - sg)))
        mine = slice(FF_HALO, FF_HALO + t)
        for n in range(2):
            du_ref[n] = _conv3_t(dus[n], w_ref[n])[mine].astype(du_ref.dtype)
            dc = dus[n][mine]
            dw = jnp.concatenate([
                jnp.sum(dc * _shift_down(ext[n], 2)[mine], axis=0, keepdims=True),
                jnp.sum(dc * _shift_down(ext[n], 1)[mine], axis=0, keepdims=True),
                jnp.sum(dc * ext[n][mine], axis=0, keepdims=True),
                jnp.zeros((5, FF_SHARD), F32)], axis=0)

            @pl.when(i == 0)
            def _(n=n, dw=dw):
                dw_ref[n] = dw

            @pl.when(i > 0)
            def _(n=n, dw=dw):
                dw_ref[n] += dw

    def pair(rows, row_map):
        return pl.BlockSpec((2, None, rows, FF_SHARD), lambda j, i: (0, j, row_map(i), 0))

    prev_row = lambda i: jnp.maximum(i * (t // FF_HALO) - 1, 0)
    next_row = lambda i: jnp.minimum((i + 1) * (t // FF_HALO), s // FF_HALO - 1)
    u2 = u0.reshape(2, FF_HALF, s, FF_SHARD)
    du, dw = pl.pallas_call(
        body, name="ffn_gate_bwd", grid=(FF_HALF, nt),
        out_shape=(jax.ShapeDtypeStruct((2, FF_HALF, s, FF_SHARD), BF16),
                   jax.ShapeDtypeStruct((2, FF_HALF, 8, FF_SHARD), F32)),
        in_specs=[pair(t, lambda i: i), pair(FF_HALO, prev_row), pair(FF_HALO, next_row), pair(3, lambda i: 0),
                  pl.BlockSpec((None, t, FF_SHARD), lambda j, i: (j, i, 0)),
                  pl.BlockSpec((None, FF_HALO, FF_SHARD), lambda j, i: (j, next_row(i), 0))],
        out_specs=(pair(t, lambda i: i), pair(8, lambda i: 0)),
        compiler_params=_params(),
    )(u2, u2, u2, w.reshape(2, FF_HALF, 3, FF_SHARD), dact, dact)
    return du.reshape(2 * FF_HALF, s, FF_SHARD), dw.reshape(2 * FF_HALF, 8, FF_SHARD)[:, :3]


def _rope_tables(positions):
    inv_freq = ROPE_THETA ** (-jnp.arange(0, ROPE_DIM, 2, dtype=F32) / ROPE_DIM)
    ang = positions.astype(F32)[:, None] * inv_freq
    cos, sin = jnp.cos(ang), jnp.sin(ang)
    s = positions.shape[0]
    half = ROPE_DIM // 2
    rest = HEAD_DIM - ROPE_DIM
    ca = jnp.concatenate([cos, cos, jnp.ones((s, rest), F32)], axis=1)
    cb = jnp.concatenate([-sin, jnp.zeros((s, HEAD_DIM - half), F32)], axis=1)
    cc = jnp.concatenate([jnp.zeros((s, half), F32), sin, jnp.zeros((s, rest), F32)], axis=1)
    return tuple(jnp.tile(tb, (1, N_HEADS)) for tb in (ca, cb, cc))


QK_WIDE = 128
LANE_CQ, LANE_CK = 64, 67
KT_ROWS = 80


def _three_bf16(x):
    hi = x.astype(BF16).astype(F32)
    mid = (x - hi).astype(BF16).astype(F32)
    lo = (x - hi - mid).astype(BF16).astype(F32)
    return hi, mid, lo


def _heads_split(proj, col, tables, c, name):
    s = proj.shape[0]
    t = min(ROW_TILE, s)
    rope = tables is not None
    wide = c is not None
    width = QK_WIDE if wide else HEAD_DIM

    def body(*refs):
        x_ref = refs[0]
        q_ref, k_ref, v_ref, kt_ref, vt_ref = refs[-5:]
        xv = x_ref[...]
        parts = [xv[:, 0:GROUP], xv[:, GROUP:2 * GROUP], xv[:, 2 * GROUP:3 * GROUP]]
        if rope:
            ca, cb, cc = refs[1][...], refs[2][...], refs[3][...]
            for n in range(2):
                p = parts[n]
                parts[n] = p * ca + pltpu.roll(p, GROUP - 8, 1) * cb + pltpu.roll(p, 8, 1) * cc
        parts[0] = parts[0] * (HEAD_DIM ** -0.5)
        k_t, v_t = parts[1].T, parts[2].T
        ones_row = jnp.where(lax.broadcasted_iota(jnp.int32, (KT_ROWS - HEAD_DIM, t), 0) == 0, 1.0, 0.0)
        lane = lax.broadcasted_iota(jnp.int32, (t, QK_WIDE), 1)
        zeros = jnp.zeros((t, QK_WIDE - HEAD_DIM), F32)
        for h in range(N_HEADS):
            hs = slice(h * HEAD_DIM, (h + 1) * HEAD_DIM)
            qh, kh = parts[0][:, hs], parts[1][:, hs]
            if wide:
                terms = _three_bf16(refs[-6][:, h:h + 1])
                qh = jnp.concatenate([qh, zeros], axis=1)
                kh = jnp.concatenate([kh, zeros], axis=1)
                for n in range(3):
                    qh = jnp.where(lane == LANE_CQ + n, terms[n], jnp.where(lane == LANE_CK + n, 1.0, qh))
                    kh = jnp.where(lane == LANE_CK + n, -terms[n], jnp.where(lane == LANE_CQ + n, 1.0, kh))
            q_ref[h] = qh.astype(q_ref.dtype)
            k_ref[h] = kh.astype(k_ref.dtype)
            v_ref[h] = parts[2][:, hs].astype(v_ref.dtype)
            kt_ref[h] = jnp.concatenate([k_t[hs, :], ones_row], axis=0).astype(kt_ref.dtype)
            vt_ref[h] = v_t[hs, :].astype(vt_ref.dtype)

    tab = pl.BlockSpec((t, GROUP), lambda i: (i, 0))
    qk = pl.BlockSpec((N_HEADS, t, width), lambda i: (0, i, 0))
    heads = pl.BlockSpec((N_HEADS, t, HEAD_DIM), lambda i: (0, i, 0))
    heads_t = pl.BlockSpec((N_HEADS, HEAD_DIM, t), lambda i: (0, 0, i))
    qk_shape = jax.ShapeDtypeStruct((N_HEADS, s, width), BF16)
    return pl.pallas_call(
        body, name=name, grid=(s // t,),
        out_shape=(qk_shape, qk_shape, jax.ShapeDtypeStruct((N_HEADS, s, HEAD_DIM), BF16),
                   jax.ShapeDtypeStruct((N_HEADS, KT_ROWS, s), BF16),
                   jax.ShapeDtypeStruct((N_HEADS, HEAD_DIM, s), BF16)),
        in_specs=[pl.BlockSpec((t, 3 * GROUP), lambda i: (i, col))] + ([tab, tab, tab] if rope else [])
        + ([pl.BlockSpec((t, 128), lambda i: (i, 0))] if wide else []),
        out_specs=(qk, qk, heads, pl.BlockSpec((N_HEADS, KT_ROWS, t), lambda i: (0, 0, i)), heads_t),
        compiler_params=_params(),
    )(*((proj,) + (tuple(tables) if rope else ()) + ((c,) if wide else ())))


def _heads_merge(dqt, dk, dv, tables, name, dbuf, col):
    s = dv.shape[1]
    t = min(ROW_TILE, s)
    rope = tables is not None

    wide = dk.shape[2] == QK_WIDE

    def body(*refs):
        o_ref = refs[n_in + 1]
        dq = jnp.concatenate([refs[0][h, :HEAD_DIM, :] for h in range(N_HEADS)], axis=0).T
        parts = [dq] + [jnp.concatenate([r[h][:, :HEAD_DIM] for h in range(N_HEADS)], axis=1) for r in refs[1:3]]
        parts[0] = parts[0] * (HEAD_DIM ** -0.5)
        if rope:
            ca, cb, cc = refs[3][...], refs[4][...], refs[5][...]
            for n in range(2):
                p = parts[n]
                parts[n] = p * ca + pltpu.roll(p * cb, 8, 1) + pltpu.roll(p * cc, GROUP - 8, 1)
        o_ref[...] = jnp.concatenate(parts, axis=1).astype(o_ref.dtype)
        if wide:
            over_keys = jnp.concatenate([refs[0][h, HEAD_DIM:HEAD_DIM + 8, :] for h in range(N_HEADS)]
                                        + [jnp.zeros((128 - 8 * N_HEADS, t), F32)], axis=0).T
            lane = lax.broadcasted_iota(jnp.int32, (t, 128), 1)
            dc = jnp.zeros((t, 128), F32)
            for h in range(N_HEADS):
                dc = jnp.where(lane == h, over_keys[:, 8 * h:8 * h + 1] - refs[1][h][:, LANE_CK:LANE_CK + 1], dc)
            refs[n_in + 2][...] = dc

    tab = pl.BlockSpec((t, GROUP), lambda i: (i, 0))
    heads = pl.BlockSpec((N_HEADS, t, HEAD_DIM), lambda i: (0, i, 0))
    n_in = 6 if rope else 3
    dspec = pl.BlockSpec((t, 3 * GROUP), lambda i: (i, col))
    dshape = jax.ShapeDtypeStruct(dbuf.shape, dbuf.dtype)
    return pl.pallas_call(
        body, name=name, grid=(s // t,),
        out_shape=(dshape, jax.ShapeDtypeStruct((s, 128), F32)) if wide else dshape,
        in_specs=[pl.BlockSpec((N_HEADS, KT_ROWS, t), lambda i: (0, 0, i)),
                  pl.BlockSpec((N_HEADS, t, dk.shape[2]), lambda i: (0, i, 0)), heads]
        + ([tab, tab, tab] if rope else []) + [ANY_SPEC],
        out_specs=(dspec, pl.BlockSpec((t, 128), lambda i: (i, 0))) if wide else dspec,
        input_output_aliases={n_in: 0}, compiler_params=_params(),
    )(*((dqt, dk, dv) + (tuple(tables) if rope else ()) + (dbuf,)))


def _log_sigmoid(x):
    return jnp.minimum(x, 0.0) - jnp.log(1.0 + jnp.exp(-jnp.abs(x)))


def _scan_rows(x, reverse):
    n = x.shape[0]
    row = lax.broadcasted_iota(jnp.int32, x.shape, 0)
    k = 1
    while k < n:
        if reverse:
            x = x + jnp.where(row < n - k, _shift_up(x, k), 0.0)
        else:
            x = x + jnp.where(row >= k, _shift_down(x, k), 0.0)
        k *= 2
    return x


def _gate_cumsum(proj, bias):
    s = proj.shape[0]
    col = COL_GATE // 128

    def body(z_ref, b_ref, c_ref):
        c_ref[...] = _scan_rows(_log_sigmoid(z_ref[...] + b_ref[...]), False)

    return pl.pallas_call(
        body, name="gate_cumsum", grid=(1,), out_shape=jax.ShapeDtypeStruct((s, 128), F32),
        in_specs=[pl.BlockSpec((s, 128), lambda i: (0, col)), pl.BlockSpec((1, 128), lambda i: (0, 0))],
        out_specs=pl.BlockSpec((s, 128), lambda i: (0, 0)), compiler_params=_params(),
    )(proj, bias)


def _gate_cumsum_bwd(proj, bias, dc, dbuf):
    s = proj.shape[0]
    col = COL_GATE // 128

    def body(z_ref, b_ref, dc_ref, buf_ref, dz_ref, db_ref):
        dlogf = _scan_rows(dc_ref[...], True)
        dz = dlogf * _sigmoid(-(z_ref[...] + b_ref[...]))
        dz_ref[...] = dz.astype(dz_ref.dtype)
        db_ref[...] = jnp.sum(dz, axis=0, keepdims=True)

    return pl.pallas_call(
        body, name="gate_cumsum_bwd", grid=(1,),
        out_shape=(jax.ShapeDtypeStruct(dbuf.shape, dbuf.dtype), jax.ShapeDtypeStruct((1, 128), F32)),
        in_specs=[pl.BlockSpec((s, 128), lambda i: (0, col)), pl.BlockSpec((1, 128), lambda i: (0, 0)),
                  pl.BlockSpec((s, 128), lambda i: (0, 0)), ANY_SPEC],
        out_specs=(pl.BlockSpec((s, 128), lambda i: (0, col)), pl.BlockSpec((1, 128), lambda i: (0, 0))),
        input_output_aliases={3: 0}, compiler_params=_params(),
    )(proj, bias, dc, dbuf)


DIL_REACH = 2048


def _pair_weight(mode, d):
    if mode == "fox":
        return jnp.where(d >= 0, 1.0, 0.0)
    w1 = jnp.where(jnp.abs(d - 64) <= 64, 1.0, 0.0)
    w2 = jnp.where((d & 3) == 0, jnp.where(jnp.abs(d - 256) <= 256, 1.0, 0.0), 0.0)
    w3 = jnp.where((d & 15) == 0, jnp.where(jnp.abs(d - 1024) <= 1024, 1.0, 0.0), 0.0)
    return w1 + w2 + w3


def _bias_tables(mode, tq, tk):
    nb = 2 if mode == "fox" else DIL_REACH // tk + 1
    n = lax.broadcasted_iota(jnp.int32, (nb, tk, tq), 0)
    key = lax.broadcasted_iota(jnp.int32, (nb, tk, tq), 1)
    query = lax.broadcasted_iota(jnp.int32, (nb, tk, tq), 2)
    w = _pair_weight(mode, n * tk + query - key)
    return jnp.where(w > 0.0, jnp.log(jnp.maximum(w, 1.0)), NEG)


M_INIT = -1e29


def _first_key_chunk(mode, q0, tk):
    if mode == "fox":
        return 0
    return jnp.maximum(q0 - DIL_REACH, 0) // tk


def _attention_fwd(mode, q, k, vt, tab_t, ybuf, col):
    s, width = q.shape[1], q.shape[2]
    tq = min(ATT_TQ, s)
    tk = tq
    nb = tab_t.shape[0]

    def body(q_ref, k_ref, vt_ref, tab_ref, buf_ref, y_ref, o_ref, lse_ref):
        i = pl.program_id(0)
        lo = _first_key_chunk(mode, i * tq, tk)

        def step(c, carry):
            k0 = pl.multiple_of(c * tk, tk)
            tab = tab_ref[jnp.minimum(i - c, nb - 1)]
            scores = [lax.dot_general(k_ref[h, pl.ds(k0, tk), :], q_ref[h], (NT, ((), ())),
                                      preferred_element_type=F32) for h in range(N_HEADS)]
            stats, probs = [], []
            for h in range(N_HEADS):
                m, l = carry[3 * h:3 * h + 2]
                sc = scores[h] + tab
                m_new = jnp.maximum(m, jnp.max(sc, axis=0, keepdims=True))
                alpha = jnp.exp(m - m_new)
                p = jnp.exp(sc - m_new)
                stats.append((m_new, alpha * l + jnp.sum(p, axis=0, keepdims=True), alpha))
                probs.append(p.astype(BF16))
            pv = [jnp.dot(vt_ref[h, :, pl.ds(k0, tk)], probs[h], preferred_element_type=F32) for h in range(N_HEADS)]
            new = []
            for h in range(N_HEADS):
                m_new, l, alpha = stats[h]
                new += [m_new, l, alpha * carry[3 * h + 2] + pv[h]]
            return tuple(new)

        start = (jnp.full((1, tq), M_INIT, F32), jnp.zeros((1, tq), F32), jnp.zeros((HEAD_DIM, tq), F32))
        done = lax.fori_loop(lo, i + 1, step, start * N_HEADS)
        outs = []
        for h in range(N_HEADS):
            m, l, acc = done[3 * h:3 * h + 3]
            outs.append(acc / l)
            lse_ref[h] = m + jnp.log(l)
        out = jnp.concatenate(outs, axis=0).T
        y_ref[...] = out.astype(y_ref.dtype)
        o_ref[...] = out

    rowspec = pl.BlockSpec((N_HEADS, 1, tq), lambda i: (0, 0, i))
    return pl.pallas_call(
        body, name="attention_fwd_" + mode, grid=(s // tq,),
        out_shape=(jax.ShapeDtypeStruct(ybuf.shape, ybuf.dtype), jax.ShapeDtypeStruct((s, GROUP), F32),
                   jax.ShapeDtypeStruct((N_HEADS, 1, s), F32)),
        in_specs=[pl.BlockSpec((N_HEADS, tq, width), lambda i: (0, i, 0)),
                  pl.BlockSpec((N_HEADS, s, width), lambda i: (0, 0, 0)),
                  pl.BlockSpec((N_HEADS, HEAD_DIM, s), lambda i: (0, 0, 0)),
                  pl.BlockSpec((nb, tk, tq), lambda i: (0, 0, 0)), ANY_SPEC],
        out_specs=(pl.BlockSpec((tq, GROUP), lambda i: (i, col)), pl.BlockSpec((tq, GROUP), lambda i: (i, 0)),
                   rowspec),
        input_output_aliases={4: 0}, compiler_params=_params(),
    )(q, k, vt, tab_t, ybuf)


def _attention_delta(o, do, col):
    s = o.shape[0]
    t = min(ROW_TILE, s)

    def body(o_ref, do_ref, delta_ref, dob_ref):
        dov = do_ref[...]
        prod_t = (o_ref[...] * dov).T
        for h in range(N_HEADS):
            hs = slice(h * HEAD_DIM, (h + 1) * HEAD_DIM)
            delta_ref[h] = jnp.sum(prod_t[hs, :], axis=0, keepdims=True)
            dob_ref[h] = dov[:, hs].astype(dob_ref.dtype)

    return pl.pallas_call(
        body, name="attention_delta", grid=(s // t,),
        out_shape=(jax.ShapeDtypeStruct((N_HEADS, 1, s), F32), jax.ShapeDtypeStruct((N_HEADS, s, HEAD_DIM), BF16)),
        in_specs=[pl.BlockSpec((t, GROUP), lambda i: (i, 0)), pl.BlockSpec((t, GROUP), lambda i: (i, col))],
        out_specs=(pl.BlockSpec((N_HEADS, 1, t), lambda i: (0, 0, i)),
                   pl.BlockSpec((N_HEADS, t, HEAD_DIM), lambda i: (0, i, 0))),
        compiler_params=_params(),
    )(o, do)


def _attention_bwd(mode, q, k, v, kt, tab_t, dob, lse, delta):
    s, width = q.shape[1], q.shape[2]
    tq = min(ATT_TQ, s)
    tk = tq
    nq = s // tq
    nb = tab_t.shape[0]

    def body(q_ref, k_ref, v_ref, kt_ref, tab_ref, dob_ref, lse_ref, delta_ref, dqt_ref, dk_ref, dv_ref):
        i = pl.program_id(0)

        @pl.when(i == 0)
        def _():
            dqt_ref[...] = jnp.zeros_like(dqt_ref)

        hi = nq if mode == "fox" else jnp.minimum((i * tk + tk - 1 + DIL_REACH) // tq + 1, nq)
        for h0 in range(0, N_HEADS, BWD_HEADS):
            heads = range(h0, h0 + BWD_HEADS)

            def step(c, carry, heads=heads):
                q0 = pl.multiple_of(c * tq, tq)
                qs = pl.ds(q0, tq)
                tab = tab_ref[jnp.minimum(c - i, nb - 1)]
                qv = [q_ref[h, qs, :] for h in heads]
                dov = [dob_ref[h, qs, :] for h in heads]
                sc = [lax.dot_general(k_ref[h], qv[n], (NT, ((), ())), preferred_element_type=F32)
                      for n, h in enumerate(heads)]
                dp = [lax.dot_general(v_ref[h], dov[n], (NT, ((), ())), preferred_element_type=F32)
                      for n, h in enumerate(heads)]
                pb, dsb = [], []
                for n, h in enumerate(heads):
                    p = jnp.exp(sc[n] + tab - lse_ref[h, :, qs])
                    pb.append(p.astype(BF16))
                    dsb.append((p * (dp[n] - delta_ref[h, :, qs])).astype(BF16))
                new = []
                for n, h in enumerate(heads):
                    new += [carry[2 * n] + jnp.dot(dsb[n], qv[n], preferred_element_type=F32),
                            carry[2 * n + 1] + jnp.dot(pb[n], dov[n], preferred_element_type=F32)]
                for n, h in enumerate(heads):
                    dqt_ref[h, :, qs] += jnp.dot(kt_ref[h], dsb[n], preferred_element_type=F32)
                return tuple(new)

            start = (jnp.zeros((tk, width), F32), jnp.zeros((tk, HEAD_DIM), F32))
            done = lax.fori_loop(i, hi, step, start * BWD_HEADS)
            for n, h in enumerate(heads):
                dk_ref[h] = done[2 * n]
                dv_ref[h] = done[2 * n + 1]

    def full(shape):
        return pl.BlockSpec(shape, lambda i: (0, 0, 0))

    kblk = pl.BlockSpec((N_HEADS, tk, width), lambda i: (0, i, 0))
    vblk = pl.BlockSpec((N_HEADS, tk, HEAD_DIM), lambda i: (0, i, 0))
    return pl.pallas_call(
        body, name="attention_bwd_" + mode, grid=(s // tk,),
        out_shape=(jax.ShapeDtypeStruct((N_HEADS, KT_ROWS, s), F32), jax.ShapeDtypeStruct((N_HEADS, s, width), F32),
                   jax.ShapeDtypeStruct((N_HEADS, s, HEAD_DIM), F32)),
        in_specs=[full((N_HEADS, s, width)), kblk, vblk, pl.BlockSpec((N_HEADS, KT_ROWS, tk), lambda i: (0, 0, i)),
                  full((nb, tk, tq)), full((N_HEADS, s, HEAD_DIM)), full((N_HEADS, 1, s)), full((N_HEADS, 1, s))],
        out_specs=(full((N_HEADS, KT_ROWS, s)), kblk, vblk),
        compiler_params=_params(),
    )(q, k, v, kt, tab_t, dob, lse, delta)


def _xattn_fwd(qx, kvm):
    s = qx.shape[0]
    t = min(ROW_TILE, s)

    def body(q_ref, kv_ref, o_ref):
        heads = range(XA_HEADS)
        sc = [lax.dot_general(q_ref[:, h * XA_DIM:(h + 1) * XA_DIM].astype(BF16), kv_ref[h].astype(BF16),
                              (NT, ((), ())), preferred_element_type=F32) * (XA_DIM ** -0.5) for h in heads]
        probs = []
        for h in heads:
            e = jnp.exp(sc[h] - jnp.max(sc[h], axis=-1, keepdims=True))
            probs.append((e / jnp.sum(e, axis=-1, keepdims=True)).astype(BF16))
        outs = [jnp.dot(probs[h], kv_ref[XA_HEADS + h].astype(BF16), preferred_element_type=F32) for h in heads]
        for h in heads:
            o_ref[:, h * XA_DIM:(h + 1) * XA_DIM] = outs[h].astype(o_ref.dtype)

    return pl.pallas_call(
        body, name="xattn_fwd", grid=(s // t,), out_shape=jax.ShapeDtypeStruct((s, D_MODEL), BF16),
        in_specs=[pl.BlockSpec((t, D_MODEL), lambda i: (i, 0)),
                  pl.BlockSpec((2 * XA_HEADS, MEM_LEN, XA_DIM), lambda i: (0, 0, 0))],
        out_specs=pl.BlockSpec((t, D_MODEL), lambda i: (i, 0)), compiler_params=_params(),
    )(qx, kvm)


def _xattn_bwd(qx, kvm, do):
    s = qx.shape[0]
    t = min(ROW_TILE, s)

    def body(q_ref, kv_ref, do_ref, dq_ref, dkv_ref):
        i = pl.program_id(0)
        heads = range(XA_HEADS)
        qv = [q_ref[:, h * XA_DIM:(h + 1) * XA_DIM].astype(BF16) for h in heads]
        dov = [do_ref[:, h * XA_DIM:(h + 1) * XA_DIM].astype(BF16) for h in heads]
        kv = [kv_ref[h].astype(BF16) for h in heads]
        sc = [lax.dot_general(qv[h], kv[h], (NT, ((), ())), preferred_element_type=F32) * (XA_DIM ** -0.5)
              for h in heads]
        dp = [lax.dot_general(dov[h], kv_ref[XA_HEADS + h].astype(BF16), (NT, ((), ())), preferred_element_type=F32)
              for h in heads]
        pb, ds = [], []
        for h in heads:
            e = jnp.exp(sc[h] - jnp.max(sc[h], axis=-1, keepdims=True))
            p = e / jnp.sum(e, axis=-1, keepdims=True)
            pb.append(p.astype(BF16))
            ds.append((p * (dp[h] - jnp.sum(p * dp[h], axis=-1, keepdims=True)) * (XA_DIM ** -0.5)).astype(BF16))
        dq = [jnp.dot(ds[h], kv[h], preferred_element_type=F32) for h in heads]
        dk = [lax.dot_general(ds[h], qv[h], (TN, ((), ())), preferred_element_type=F32) for h in heads]
        dv = [lax.dot_general(pb[h], dov[h], (TN, ((), ())), preferred_element_type=F32) for h in heads]
        for h in heads:
            dq_ref[:, h * XA_DIM:(h + 1) * XA_DIM] = dq[h].astype(dq_ref.dtype)

        @pl.when(i == 0)
        def _():
            for h in heads:
                dkv_ref[h] = dk[h]
                dkv_ref[XA_HEADS + h] = dv[h]

        @pl.when(i > 0)
        def _():
            for h in heads:
                dkv_ref[h] += dk[h]
                dkv_ref[XA_HEADS + h] += dv[h]

    row = pl.BlockSpec((t, D_MODEL), lambda i: (i, 0))
    kvs = pl.BlockSpec((2 * XA_HEADS, MEM_LEN, XA_DIM), lambda i: (0, 0, 0))
    return pl.pallas_call(
        body, name="xattn_bwd", grid=(s // t,),
        out_shape=(jax.ShapeDtypeStruct((s, D_MODEL), BF16),
                   jax.ShapeDtypeStruct((2 * XA_HEADS, MEM_LEN, XA_DIM), F32)),
        in_specs=[row, kvs, row], out_specs=(row, kvs), compiler_params=_params(),
    )(qx, kvm, do)


def _adamw(parts, owns, me, w, m, v, name):
    nl, r, c = w.shape
    tr = r
    for cand in (512, 352, 256, 176, 128, 64, 32, 16, 8):
        if r % cand == 0 and r > cand and N_DEV * cand * c * 4 <= ADAMW_BLOCK_BYTES:
            tr = cand
            break
    nt = r // tr
    per_layer = N_DEV + (1 if owns is not None else 0)

    def body(me_ref, *refs):
        w_ref, m_ref, v_ref, g_ref, d_ref, nm_ref, nv_ref = refs[nl * per_layer:]
        layer = pl.program_id(0)
        g = None
        for l in range(nl):
            p_refs = refs[l * per_layer:(l + 1) * per_layer]
            gl = None
            for d in range(N_DEV):
                term = p_refs[d][...].astype(F32)
                if owns is not None:
                    term = jnp.where(me_ref[0] == d, p_refs[N_DEV][...].astype(F32), term)
                gl = term if gl is None else gl + term
            g = gl if g is None else jnp.where(layer == l, gl, g)
        mn = ADAM_B1 * m_ref[...] + (1.0 - ADAM_B1) * g
        vn = ADAM_B2 * v_ref[...] + (1.0 - ADAM_B2) * (g * g)
        m_hat = mn / (1.0 - ADAM_B1 ** ADAM_STEP)
        v_hat = vn / (1.0 - ADAM_B2 ** ADAM_STEP)
        g_ref[...] = g
        d_ref[...] = -ADAM_LR * (m_hat / (jnp.sqrt(v_hat) + ADAM_EPS) + ADAM_WD * w_ref[...])
        nm_ref[...] = mn
        nv_ref[...] = vn

    def rows(l, ll, i):
        return jnp.where(ll == l, i, jnp.where(ll < l, 0, nt - 1))

    def part_spec(l, d):
        if owns is None:
            return pl.BlockSpec((None, tr, c), lambda ll, i, me_ref: (d, rows(l, ll, i), 0))
        return pl.BlockSpec((None, tr, c),
                            lambda ll, i, me_ref: (jnp.where(me_ref[0] == d, (d + 1) % N_DEV, d), rows(l, ll, i), 0))

    def own_spec(l):
        return pl.BlockSpec((None, tr, c), lambda ll, i, me_ref: (me_ref[0], rows(l, ll, i), 0))

    in_specs, operands = [], []
    for l in range(nl):
        in_specs += [part_spec(l, d) for d in range(N_DEV)]
        operands += [parts[l]] * N_DEV
        if owns is not None:
            in_specs.append(own_spec(l))
            operands.append(owns[l])
    blk = pl.BlockSpec((None, tr, c), lambda ll, i, me_ref: (ll, i, 0))
    shp = jax.ShapeDtypeStruct((nl, r, c), F32)
    return pl.pallas_call(
        body, name=name, out_shape=(shp, shp, shp, shp),
        grid_spec=pltpu.PrefetchScalarGridSpec(
            num_scalar_prefetch=1, grid=(nl, nt), in_specs=in_specs + [blk, blk, blk],
            out_specs=(blk, blk, blk, blk)),
        compiler_params=_params(),
    )(me.reshape(1), *operands, w, m, v)


GROUPS = {"in": ("w_in",), "rest": ("w_out", "w_xq", "w_xo", "w_xkv", "w_up", "w_down")}
FULL_SHAPES = {"w_in": (D_MODEL, N_IN_PAD), "w_out": (D_MODEL, D_MODEL), "w_xq": (D_MODEL, D_MODEL),
               "w_xo": (D_MODEL, D_MODEL), "w_xkv": (N_DEV, D_MODEL, 2 * D_MODEL // N_DEV),
               "w_up": (N_DEV, FF_SHARD, D_MODEL), "w_down": (FF_HALF, FF_SHARD, D_MODEL)}
PIECE_SHAPES = {"w_in": (N_DEV, D_MODEL // N_DEV, N_IN_PAD), "w_out": (N_DEV, D_MODEL // N_DEV, D_MODEL),
                "w_xq": (N_DEV, D_MODEL // N_DEV, D_MODEL), "w_xo": (N_DEV, D_MODEL // N_DEV, D_MODEL),
                "w_xkv": (N_DEV, D_MODEL, 2 * D_MODEL // N_DEV), "w_up": (N_DEV, FF_SHARD, D_MODEL),
                "w_down": (N_DEV, D_FF // N_DEV, D_MODEL)}
GATHER_GROUPS = {"in": ("w_in",), "mid": ("w_out", "w_xq", "w_xo", "w_xkv"), "ffn": ("w_up", "w_down")}
CONV_WORDS = 8192


class _GatheredWeights:
    def __init__(self, states, layer):
        self.states, self.layer, self.full, self.extra = dict(states), layer, {}, None

    def need(self, group, after):
        if group in self.states:
            got, _ = _exchange_wait(self.states.pop(group), after, "gather_%s_wait_%d" % (group, self.layer))
            for name, g in zip(GATHER_GROUPS[group], got):
                self.full[name] = g.reshape(FULL_SHAPES[name])
            self.extra = got[len(GATHER_GROUPS[group]):]

    def __getitem__(self, name):
        return self.full[name]


def _relay_in_cols(w):
    pad = jnp.zeros(w.shape[:-1] + (N_IN_PAD - N_IN,), w.dtype)
    return jnp.concatenate([w[..., :2304], w[..., 2308:N_IN], w[..., 2304:2308], pad], axis=-1)


def _unrelay_in_cols(w):
    return jnp.concatenate([w[..., :2304], w[..., COL_GATE:COL_GATE + 4], w[..., 2304:COL_GATE]], axis=-1)


def _layer_fwd(h, memv, w, sm, tables, rest_arrived):
    sv = {"h0": h}
    s = h.shape[0]
    tm, tb = min(SLAB_TILE, s), min(MM_TILE, s)
    w.need("in", h)
    proj, xn = _norm_matmul(h, sm["g_mix"], w["w_in"], (s, N_IN_PAD), grid=(s // tm, 1),
                            b_spec=pl.BlockSpec((D_MODEL, N_IN_PAD), lambda i, j: (0, 0)),
                            o_spec=pl.BlockSpec((tm, N_IN_PAD), lambda i, j: (i, 0)), name="norm_mm_in")
    sv["xn"], sv["proj"] = xn, proj
    ycat = _sconv_fwd(proj, sm["w_sconv"])
    qd, kd, vd, ktd, vtd = _heads_split(proj, 1, tables["rope"], None, "split_dil")
    ycat, ob, lse_b = _attention_fwd("dil", qd, kd, vtd, tables["dil"], ycat, 1)
    sv["dil"] = (qd, kd, vd, ktd, ob, lse_b)
    c = _gate_cumsum(proj, sm["b_forget_pad"])
    qf, kf, vf, ktf, vtf = _heads_split(proj, 2, None, c, "split_fox")
    ycat, oc, lse_c = _attention_fwd("fox", qf, kf, vtf, tables["fox"], ycat, 2)
    sv["fox"] = (qf, kf, vf, ktf, oc, lse_c)
    ycat = _pool_fwd(proj, sm["w_pool_bd"], sm["pool_scale"], ycat)
    sv["ycat"] = ycat
    w.need("mid", ycat)
    h1 = _mm_nn(ycat, w["w_out"], "mm_out", res=h)
    sv["h1"] = h1
    memn = _rms_fwd(memv, sm["g_mem"], "rms_mem")
    qx, xq = _norm_matmul(h1, sm["g_xa"], w["w_xq"], (s, D_MODEL), grid=(s // tb, 1),
                          b_spec=pl.BlockSpec((D_MODEL, D_MODEL), lambda i, j: (0, 0)),
                          o_spec=pl.BlockSpec((tb, D_MODEL), lambda i, j: (i, 0)), name="norm_mm_xq",
                          out_dtype=BF16)
    kvm = _matmul(memn, w["w_xkv"], (N_DEV, MEM_LEN, XA_DIM), grid=(N_DEV, 1, 1),
                  a_spec=pl.BlockSpec((MEM_LEN, D_MODEL), lambda i, j, r: (0, 0)),
                  b_spec=pl.BlockSpec((None, D_MODEL, XA_DIM), lambda i, j, r: (i, 0, 0)),
                  o_spec=pl.BlockSpec((None, MEM_LEN, XA_DIM), lambda i, j, r: (i, 0, 0)),
                  dims=NN, nred=1, name="mm_xkv")
    ox = _xattn_fwd(qx, kvm)
    sv.update(xq=xq, memn=memn, qx=qx, kvm=kvm, ox=ox)
    h2 = _mm_nn(ox, w["w_xo"], "mm_xo", res=h1)
    sv["h2"] = h2
    w.need("ffn", ox)
    u0, xf = _norm_matmul(h2, sm["g_ffn"] + rest_arrived(w["w_down"]), w["w_up"], (N_DEV, s, FF_SHARD),
                          grid=(s // tb, N_DEV // 2),
                          b_spec=pl.BlockSpec((2, FF_SHARD, D_MODEL), lambda i, j: (j, 0, 0)),
                          o_spec=pl.BlockSpec((2, tb, FF_SHARD), lambda i, j: (j, i, 0)), name="norm_mm_up",
                          out_dtype=BF16, dims=NT)
    act = _ffn_gate_fwd(u0, sm["w_ffconv"])
    sv.update(xf=xf, u0=u0, act=act)
    ospec = pl.BlockSpec((tm, D_MODEL), lambda i, j, r: (i, 0))
    h3 = _matmul(act, w["w_down"], (s, D_MODEL), grid=(s // tm, 1, 1),
                 a_spec=pl.BlockSpec((FF_HALF, tm, FF_SHARD), lambda i, j, r: (0, i, 0)),
                 b_spec=pl.BlockSpec((FF_HALF, FF_SHARD, D_MODEL), lambda i, j, r: (0, 0, 0)),
                 o_spec=ospec, dims=NN, nred=1, slabs=FF_HALF, name="mm_down", res=h2, res_spec=ospec)
    return h3, sv


def _layer_bwd(dh3, memv, w, sm, tables, sv, rest_ready, in_ready):
    s = dh3.shape[0]
    tm, tb = min(ROW_TILE, s), min(MM_TILE, s)
    big, small = {}, {}
    ts = max(s // 2, 1)
    dact = _matmul(dh3, w["w_down"], (FF_HALF, s, FF_SHARD), grid=(s // tb, FF_HALF, 1),
                   a_spec=pl.BlockSpec((tb, D_MODEL), lambda i, j, r: (i, 0)),
                   b_spec=pl.BlockSpec((None, FF_SHARD, D_MODEL), lambda i, j, r: (j, 0, 0)),
                   o_spec=pl.BlockSpec((None, tb, FF_SHARD), lambda i, j, r: (j, i, 0)),
                   dims=NT, nred=1, name="mm_dact", out_dtype=BF16)
    big["w_down"] = _matmul(sv["act"], dh3, (FF_HALF, FF_SHARD, D_MODEL), grid=(FF_HALF, 1, s // ts),
                            a_spec=pl.BlockSpec((None, ts, FF_SHARD), lambda i, j, r: (i, r, 0)),
                            b_spec=pl.BlockSpec((ts, D_MODEL), lambda i, j, r: (r, 0)),
                            o_spec=pl.BlockSpec((None, FF_SHARD, D_MODEL), lambda i, j, r: (i, 0, 0)),
                            dims=TN, nred=s // ts, name="mm_dw_down", out_dtype=GRAD_DTYPE)
    du0, small["w_ffconv"] = _ffn_gate_bwd(sv["u0"], sm["w_ffconv"], dact)
    dh2, small["g_ffn"] = _matmul_rms_bwd(du0, w["w_up"], sv["h2"], sm["g_ffn"], dh3, "mm_dxf_rms_bwd",
                                          tm=SLAB_TILE // 2, dims=NN)
    big["w_up"] = _matmul(du0, sv["xf"], (N_DEV, FF_SHARD, D_MODEL), grid=(N_DEV, 1, 1),
                          a_spec=pl.BlockSpec((None, s, FF_SHARD), lambda i, j, r: (i, 0, 0)),
                          b_spec=pl.BlockSpec((s, D_MODEL), lambda i, j, r: (0, 0)),
                          o_spec=pl.BlockSpec((None, FF_SHARD, D_MODEL), lambda i, j, r: (i, 0, 0)),
                          dims=TN, nred=1, name="mm_dw_up", out_dtype=GRAD_DTYPE)
    dox = _mm_nt(dh2, w["w_xo"], "mm_dox", out_dtype=BF16)
    big["w_xo"] = _mm_tn(sv["ox"], dh2, "mm_dw_xo")
    dqx, dkvm = _xattn_bwd(sv["qx"], sv["kvm"], dox)
    big["w_xq"] = _mm_tn(sv["xq"], dqx, "mm_dw_xq")
    big["w_xkv"] = _matmul(sv["memn"], dkvm, (N_DEV, D_MODEL, XA_DIM), grid=(N_DEV, 1, 1),
                           a_spec=pl.BlockSpec((MEM_LEN, D_MODEL), lambda i, j, r: (0, 0)),
                           b_spec=pl.BlockSpec((None, MEM_LEN, XA_DIM), lambda i, j, r: (i, 0, 0)),
                           o_spec=pl.BlockSpec((None, D_MODEL, XA_DIM), lambda i, j, r: (i, 0, 0)),
                           dims=TN, nred=1, name="mm_dw_xkv", out_dtype=GRAD_DTYPE)
    dmemn = _matmul(dkvm, w["w_xkv"], (MEM_LEN, D_MODEL), grid=(1, 1, 1),
                    a_spec=pl.BlockSpec((N_DEV, MEM_LEN, XA_DIM), lambda i, j, r: (0, 0, 0)),
                    b_spec=pl.BlockSpec((N_DEV, D_MODEL, XA_DIM), lambda i, j, r: (0, 0, 0)),
                    o_spec=pl.BlockSpec((MEM_LEN, D_MODEL), lambda i, j, r: (0, 0)),
                    dims=NT, nred=1, slabs=N_DEV, name="mm_dmemn")
    _, small["g_mem"] = _rms_bwd(dmemn, memv, sm["g_mem"], None, "rms_mem_bwd")
    dh1, small["g_xa"] = _matmul_rms_bwd(dqx, w["w_xq"], sv["h1"], sm["g_xa"], dh2, "mm_dxq_rms_bwd", tm=MM_TILE)
    big["w_out"] = _mm_tn(sv["ycat"], dh1, "mm_dw_out")
    dycat = _mm_nt(dh1, w["w_out"] + rest_ready(big, small).astype(BF16), "mm_dycat")
    proj = sv["proj"]
    dproj, small["w_sconv"] = _sconv_bwd(proj, sm["w_sconv"], dycat)
    qd, kd, vd, ktd, ob, lse_b = sv["dil"]
    delta, dob = _attention_delta(ob, dycat, 1)
    dqt, dk, dv = _attention_bwd("dil", qd, kd, vd, ktd, tables["dil"], dob, lse_b, delta)
    dproj = _heads_merge(dqt, dk, dv, tables["rope"], "merge_dil", dproj, 1)
    qf, kf, vf, ktf, oc, lse_c = sv["fox"]
    delta, dob = _attention_delta(oc, dycat, 2)
    dqt, dk, dv = _attention_bwd("fox", qf, kf, vf, ktf, tables["fox"], dob, lse_c, delta)
    dproj, dc = _heads_merge(dqt, dk, dv, None, "merge_fox", dproj, 2)
    dproj, dbias = _gate_cumsum_bwd(proj, sm["b_forget_pad"], dc, dproj)
    small["b_forget"] = dbias[0, :N_HEADS]
    dproj, dwbd, small["pool_scale"] = _pool_bwd(proj, sm["w_pool_bd"], sm["pool_scale"], dycat, dproj)
    small["w_pool"] = jnp.stack([dwbd[64 * g:64 * (g + 1), 64 * g:64 * (g + 1)] for g in range(4)])
    big["w_in"] = _mm_tn(sv["xn"], dproj, "mm_dw_in")
    dh0, small["g_mix"] = _matmul_rms_bwd(dproj, w["w_in"], sv["h0"], sm["g_mix"] + in_ready(big, small), dh1,
                                          "mm_dxn_rms_bwd")
    return dh0, big, small


SMALL_NAMES = ("g_mix", "b_forget", "w_pool", "pool_scale", "g_xa", "g_mem", "g_ffn", "w_sconv", "w_ffconv")
SMALL_WITH = {"rest": ("w_ffconv", "g_ffn", "g_mem", "g_xa"),
              "in": ("w_sconv", "b_forget", "pool_scale", "w_pool")}
SMALL_SHAPES = {"w_sconv": (3, GROUP), "w_ffconv": (N_DEV, 3, FF_SHARD)}
WEIGHT_NAMES = ("g_mix", "w_in", "b_forget", "w_sconv", "w_pool", "pool_scale", "w_out", "g_xa", "g_mem", "w_xq",
                "w_xkv", "w_xo", "g_ffn", "w_up", "w_ffconv", "w_down", "g_final")


def _block_diag(w_pool):
    z = jnp.zeros((64, 64), F32)
    return jnp.concatenate(
        [jnp.concatenate([w_pool[g] if c == g else z for c in range(4)], axis=1) for g in range(4)], axis=0)


def kernel(x, mem, positions, g_mix, w_in, b_forget, w_sconv, w_pool, pool_scale, w_out, g_xa, g_mem, w_xq, w_xkv, w_xo, g_ffn, w_up, w_ffconv, w_down, g_final, loss_target, m_g_mix, m_w_in, m_b_forget, m_w_sconv, m_w_pool, m_pool_scale, m_w_out, m_g_xa, m_g_mem, m_w_xq, m_w_xkv, m_w_xo, m_g_ffn, m_w_up, m_w_ffconv, m_w_down, m_g_final, v_g_mix, v_w_in, v_b_forget, v_w_sconv, v_w_pool, v_pool_scale, v_w_out, v_g_xa, v_g_mem, v_w_xq, v_w_xkv, v_w_xo, v_g_ffn, v_w_up, v_w_ffconv, v_w_down, v_g_final):
    weights = dict(g_mix=g_mix, w_in=w_in, b_forget=b_forget, w_sconv=w_sconv, w_pool=w_pool, pool_scale=pool_scale,
                   w_out=w_out, g_xa=g_xa, g_mem=g_mem, w_xq=w_xq, w_xkv=w_xkv, w_xo=w_xo, g_ffn=g_ffn, w_up=w_up,
                   w_ffconv=w_ffconv, w_down=w_down, g_final=g_final)
    m_in = dict(g_mix=m_g_mix, w_in=m_w_in, b_forget=m_b_forget, w_sconv=m_w_sconv, w_pool=m_w_pool,
                pool_scale=m_pool_scale, w_out=m_w_out, g_xa=m_g_xa, g_mem=m_g_mem, w_xq=m_w_xq, w_xkv=m_w_xkv,
                w_xo=m_w_xo, g_ffn=m_g_ffn, w_up=m_w_up, w_ffconv=m_w_ffconv, w_down=m_w_down, g_final=m_g_final)
    v_in = dict(g_mix=v_g_mix, w_in=v_w_in, b_forget=v_b_forget, w_sconv=v_w_sconv, w_pool=v_w_pool,
                pool_scale=v_pool_scale, w_out=v_w_out, g_xa=v_g_xa, g_mem=v_g_mem, w_xq=v_w_xq, w_xkv=v_w_xkv,
                w_xo=v_w_xo, g_ffn=v_g_ffn, w_up=v_w_up, w_ffconv=v_w_ffconv, w_down=v_w_down, g_final=v_g_final)
    depth = w_in.shape[0]
    me = 4 * lax.axis_index("x") + 2 * lax.axis_index("y") + lax.axis_index("c")
    h = x[0]
    memv = mem[0]
    s = h.shape[0]
    tq = min(ATT_TQ, s)
    tables = {"rope": _rope_tables(positions[0]), "dil": _bias_tables("dil", tq, tq),
              "fox": _bias_tables("fox", tq, tq)}

    w_in_r = _relay_in_cols(w_in)
    conv_shard = jnp.concatenate([w_sconv.reshape(-1), w_ffconv.reshape(-1)])
    conv_shard = jnp.concatenate([conv_shard, jnp.zeros((CONV_WORDS - conv_shard.shape[0],), F32)])
    shards = dict(w_in=w_in_r, w_out=w_out, w_xq=w_xq, w_xo=w_xo, w_xkv=w_xkv, w_up=w_up.transpose(0, 2, 1),
                  w_down=w_down)
    gathered = [None] * depth

    def start_gathers(l, after):
        states = {}
        order = jnp.zeros((), F32)
        for group in GATHER_GROUPS:
            first = GATHER_GROUPS[group][0]
            shards[first] = shards[first] + order
            xs = [_place_shard(shards[name], l, me, BF16, "place_%s_%d" % (name, l), after)
                  for name in GATHER_GROUPS[group]]
            if l == 0 and group == "in":
                xs.append(_place_shard(conv_shard.reshape(1, CONV_WORDS // 1024, 1024), 0, me, F32, "place_conv"))
            states[group], token = _exchange_start(xs, False, "gather_%s_start_%d" % (group, l))
            order = order + token[0, 0]
        gathered[l] = _GatheredWeights(states, l)
        return order

    order = start_gathers(0, None)
    gathered[0].need("in", tables["rope"][0])
    conv_all = gathered[0].extra[0].reshape(N_DEV, CONV_WORDS)
    n_sc = depth * 3 * (GROUP // N_DEV)
    sconv_full = conv_all[:, :n_sc].reshape(N_DEV, depth, 3, GROUP // N_DEV).transpose(1, 2, 0, 3).reshape(
        depth, 3, GROUP)
    ffconv_full = conv_all[:, n_sc:n_sc + depth * 3 * FF_SHARD].reshape(N_DEV, depth, 3, FF_SHARD).transpose(
        1, 0, 2, 3)

    smalls = []
    for l in range(depth):
        smalls.append(dict(
            g_mix=g_mix[l], g_xa=g_xa[l], g_mem=g_mem[l], g_ffn=g_ffn[l], pool_scale=pool_scale[l],
            w_pool_bd=_block_diag(w_pool[l]), w_sconv=sconv_full[l], w_ffconv=ffconv_full[l],
            b_forget_pad=jnp.concatenate([b_forget[l], jnp.zeros((128 - N_HEADS,), F32)]).reshape(1, 128)))
    smalls[0]["g_mix"] = smalls[0]["g_mix"] + order

    saved = []
    for l in range(depth):
        def rest_arrived(arrived, l=l):
            return start_gathers(l + 1, arrived) if l + 1 < depth else jnp.zeros((), F32)

        h, sv = _layer_fwd(h, memv, gathered[l], smalls[l], tables, rest_arrived)
        saved.append(sv)
    loss_part, dh, dg_final = _loss_head(h, g_final, loss_target[0])
    loss = lax.psum(loss_part[0, 0], MESH_AXES)

    small_grads = [None] * depth
    scatters = {}

    def pieces_of(big, group):
        return [big[name].reshape(PIECE_SHAPES[name]) for name in GROUPS[group]]

    def rider(grads):
        flat = jnp.concatenate([g.reshape(-1) for g in grads])
        rows = -(-flat.shape[0] // 1024)
        flat = jnp.concatenate([flat, jnp.zeros((rows * 1024 - flat.shape[0],), F32)])
        return jnp.broadcast_to(flat.reshape(1, rows, 1024), (N_DEV, rows, 1024))

    riding = {}
    done_small = {}

    def start_scatter(l, group, big, extra):
        names = [(n, l) for n in SMALL_WITH[group]] + extra
        riding[l, group] = names
        grads = [dg_final if n == "g_final" else done_small[ll][n] for n, ll in names]
        scatters[l, group], token = _exchange_start(pieces_of(big, group) + [rider(grads)], True,
                                                    "scatter_%s_start_%d" % (group, l))
        return token[0, 0]

    for l in reversed(range(depth)):
        def rest_ready(big, small, l=l):
            done_small[l] = small
            extra = ([("g_final", l)] if l == depth - 1 else []) + ([("g_mix", l + 1)] if l + 1 < depth else [])
            return start_scatter(l, "rest", big, extra)

        def in_ready(big, small, l=l):
            return start_scatter(l, "in", big, [])

        dh, _, small_grads[l] = _layer_bwd(dh, memv, gathered[l], smalls[l], tables, saved[l], rest_ready, in_ready)
    grad_x = dh[None]
    riding["tail"] = [("g_mix", 0)]
    scatters["tail"], _ = _exchange_start([rider([small_grads[0]["g_mix"]])], True, "scatter_tail_start")

    parts, owns, small_parts = {}, {}, {}

    def take_rider(key, got, given):
        flat = lax.dynamic_update_slice_in_dim(got, given[:1], me, axis=0).reshape(N_DEV, -1)
        off = 0
        for name, ll in riding[key]:
            shape = SMALL_SHAPES.get(name, weights[name].shape[-1:] if name == "g_final" else weights[name].shape[1:])
            n = 1
            for dim in shape:
                n *= dim
            small_parts.setdefault(name, [None] * depth)[ll] = flat[:, off:off + n].reshape((N_DEV,) + shape)
            off += n

    def wait_group(group, after):
        for l in reversed(range(depth)):
            got, given = _exchange_wait(scatters[l, group], after, "scatter_%s_wait_%d" % (group, l))
            for name, g, x in zip(GROUPS[group], got, given):
                parts.setdefault(name, [None] * depth)[l] = g
                owns.setdefault(name, [None] * depth)[l] = x
            take_rider((l, group), got[-1], given[-1])

    results = {}

    def update(name, w3, m3, v3):
        outs = _adamw(parts[name], owns.get(name), me, w3, m3, v3, "adamw_" + name)
        results[name] = [o.reshape(weights[name].shape) for o in outs]

    wait_group("rest", grad_x)
    for name in GROUPS["rest"]:
        if name == "w_up":
            outs = _adamw(parts[name], owns[name], me, w_up.transpose(0, 2, 1), m_w_up.transpose(0, 2, 1),
                          v_w_up.transpose(0, 2, 1), "adamw_w_up")
            results[name] = [o.transpose(0, 2, 1) for o in outs]
        else:
            update(name, weights[name], m_in[name], v_in[name])
    wait_group("in", results["w_down"][1])
    outs = _adamw(parts["w_in"], owns["w_in"], me, w_in_r, _relay_in_cols(m_w_in), _relay_in_cols(v_w_in),
                  "adamw_w_in")
    results["w_in"] = [_unrelay_in_cols(o) for o in outs]
    got, given = _exchange_wait(scatters["tail"], results["w_in"][1], "scatter_tail_wait")
    take_rider("tail", got[0], given[0])
    for name in SMALL_NAMES + ("g_final",):
        wv = weights[name]
        p = small_parts[name][depth - 1] if name == "g_final" else jnp.stack(small_parts[name], axis=1)
        if name == "w_sconv":
            p = lax.dynamic_slice_in_dim(p, me * (GROUP // N_DEV), GROUP // N_DEV, axis=3)
        elif name == "w_ffconv":
            p = lax.dynamic_index_in_dim(p, me, axis=2, keepdims=False)
        shape3 = (1, 1, wv.shape[0]) if wv.ndim == 1 else (1, -1, wv.shape[-1])
        w3 = wv.reshape(shape3)
        parts[name] = [p.reshape((N_DEV,) + w3.shape[1:])]
        update(name, w3, m_in[name].reshape(shape3), v_in[name].reshape(shape3))

    return (loss, grad_x, *[results[n][0] for n in WEIGHT_NAMES], *[results[n][1] for n in WEIGHT_NAMES],
            *[results[n][2] for n in WEIGHT_NAMES], *[results[n][3] for n in WEIGHT_NAMES])
```

```python
import functools

import jax
import jax.numpy as jnp
from jax import lax
from jax.experimental import pallas as pl
from jax.experimental.pallas import tpu as pltpu

F32 = jnp.float32
BF16 = jnp.bfloat16

N_DEV = 8
D_MODEL = 1024
GROUP = 256
HEAD_DIM = 64
N_HEADS = 4
N_IN = 2564
N_IN_PAD = 2688
COL_GATE = 2560
XA_HEADS = 4
XA_DIM = 256
MEM_LEN = 256
D_FF = 2816
FF_SHARD = 704
FF_HALF = 4
ROPE_THETA = 500000.0
ROPE_DIM = 16
RMS_EPS = 1e-6
NEG = -1e30
POOL_WINDOWS = (2, 4, 8, 16)
ADAM_LR, ADAM_B1, ADAM_B2, ADAM_EPS, ADAM_WD, ADAM_STEP = 0.001, 0.9, 0.999, 1e-08, 0.01, 10

ROW_TILE = 1024
MM_TILE = 1024
SLAB_TILE = 512
ATT_TQ = 512
BWD_HEADS = 4
VMEM_LIMIT = 56 * 1024 * 1024
ADAMW_BLOCK_BYTES = 8 * 1024 * 1024
PLACE_BLOCK_BYTES = 4 * 1024 * 1024
TN_BLOCK_BYTES = 12 * 1024 * 1024

MESH_AXES = ("x", "y", "c")


def _params(**kw):
    return pltpu.CompilerParams(vmem_limit_bytes=VMEM_LIMIT, **kw)


HBM_SPEC = pl.BlockSpec(memory_space=pltpu.HBM)
SEM_SPEC = pl.BlockSpec(memory_space=pltpu.SEMAPHORE)
DATAFLOW = pltpu.SideEffectType.DATAFLOW_SIDE_EFFECTING


def _peer_copies(x_ref, land_ref, send_sems, recv_sems, scatter):
    mx, my, mc = lax.axis_index("x"), lax.axis_index("y"), lax.axis_index("c")
    me = 4 * mx + 2 * my + mc
    pairs = []
    for k in range(1, N_DEV):
        kx, ky, kc = (k >> 2) & 1, (k >> 1) & 1, k & 1
        peer_lin = me ^ k
        send = pltpu.make_async_remote_copy(
            src_ref=x_ref.at[peer_lin] if scatter else land_ref.at[me], dst_ref=land_ref.at[me],
            send_sem=send_sems.at[k - 1], recv_sem=recv_sems.at[k - 1],
            device_id=(mx ^ kx, my ^ ky, mc ^ kc), device_id_type=pl.DeviceIdType.MESH)
        arrival = pltpu.make_async_remote_copy(
            src_ref=land_ref.at[peer_lin], dst_ref=land_ref.at[peer_lin],
            send_sem=send_sems.at[k - 1], recv_sem=recv_sems.at[k - 1],
            device_id=(mx, my, mc), device_id_type=pl.DeviceIdType.MESH)
        pairs.append((send, arrival))
    return pairs


def _exchange_start(xs, scatter, name):
    n = len(xs)
    ns = n if scatter else 0

    def body(*refs):
        srcs = refs[:ns] if scatter else (None,) * n
        lands, sends, recvs = refs[ns:ns + n], refs[ns + n:ns + 2 * n], refs[ns + 2 * n:ns + 3 * n]
        for t in range(n):
            for send, _ in _peer_copies(srcs[t], lands[t], sends[t], recvs[t], scatter):
                send.start()
        token = refs[-1]
        token[...] = jnp.zeros_like(token)

    sems = pltpu.SemaphoreType.DMA((N_DEV - 1,))
    operands = [pltpu.with_memory_space_constraint(x, pltpu.HBM) for x in xs]
    if scatter:
        operands += [pltpu.with_memory_space_constraint(lax.empty(x.shape, x.dtype), pltpu.HBM) for x in xs]
    outs = pl.pallas_call(
        body, name=name,
        out_shape=(sems,) * (2 * n) + tuple(pltpu.HBM(a.shape, a.dtype) for a in operands)
        + (jax.ShapeDtypeStruct((8, 128), F32),),
        in_specs=(HBM_SPEC,) * (ns + n),
        out_specs=(SEM_SPEC,) * (2 * n) + (HBM_SPEC,) * (ns + n) + (pl.BlockSpec(memory_space=pltpu.VMEM),),
        input_output_aliases={i: 2 * n + i for i in range(ns + n)},
        compiler_params=pltpu.CompilerParams(has_side_effects=DATAFLOW),
    )(*operands)
    return (outs[:-1], scatter), outs[-1]


def _exchange_wait(state, after, name):
    held, scatter = state
    n = len(held) // (4 if scatter else 3)
    ns = n if scatter else 0
    sems, thru = held[:2 * n], held[2 * n:]

    def body(*refs):
        srcs = refs[:ns] if scatter else (None,) * n
        lands, sends, recvs = refs[ns:ns + n], refs[ns + n:ns + 2 * n], refs[ns + 2 * n:ns + 3 * n]
        for t in range(n):
            for send, arrival in _peer_copies(srcs[t], lands[t], sends[t], recvs[t], scatter):
                send.wait_send()
                arrival.wait_recv()

    outs = pl.pallas_call(
        body, name=name,
        out_shape=tuple(pltpu.HBM(a.shape, a.dtype) for a in thru),
        in_specs=(HBM_SPEC,) * (ns + n) + (SEM_SPEC,) * (2 * n) + (pl.BlockSpec(memory_space=pl.ANY),),
        out_specs=(HBM_SPEC,) * (ns + n), input_output_aliases={i: i for i in range(ns + n)},
        compiler_params=pltpu.CompilerParams(has_side_effects=DATAFLOW),
    )(*thru, *sems, after)
    return list(outs[ns:]), list(outs[:ns])


def _place_shard(x, layer, me, dtype, name, after=None):
    _, r, c = x.shape
    tr = r
    if r * c * 4 > PLACE_BLOCK_BYTES:
        for cand in (512, 256, 128, 64, 32, 16):
            if r % cand == 0 and cand * c * 4 <= PLACE_BLOCK_BYTES:
                tr = cand
                break

    def body(me_ref, x_ref, *rest):
        o_ref = rest[-1]
        o_ref[...] = x_ref[...].astype(o_ref.dtype)

    return pl.pallas_call(
        body, name=name, out_shape=jax.ShapeDtypeStruct((N_DEV, r, c), dtype),
        grid_spec=pltpu.PrefetchScalarGridSpec(
            num_scalar_prefetch=1, grid=(r // tr,),
            in_specs=[pl.BlockSpec((None, tr, c), lambda i, me_ref: (layer, i, 0))]
            + ([ANY_SPEC] if after is not None else []),
            out_specs=pl.BlockSpec((None, tr, c), lambda i, me_ref: (me_ref[0], i, 0))),
        compiler_params=_params(),
    )(*((me.reshape(1), x) + ((after,) if after is not None else ())))


NN = ((1,), (0,))
NT = ((1,), (1,))
TN = ((0,), (0,))


def _matmul(a, b, out_shape, *, grid, a_spec, b_spec, o_spec, dims, nred, name, res=None, res_spec=None,
            out_dtype=F32, slabs=0, out_slabs=0):
    has_res = res is not None

    def body(*refs):
        a_ref, b_ref = refs[0], refs[1]
        r_ref = refs[2] if has_res else None
        o_ref = refs[3] if has_res else refs[2]
        if out_slabs:
            av = a_ref[...].astype(BF16)
            for n in range(out_slabs):
                o_ref[n] = lax.dot_general(av, b_ref[n].astype(BF16), (dims, ((), ())),
                                           preferred_element_type=F32).astype(o_ref.dtype)
            return
        if slabs:
            part = None
            for n in range(slabs):
                term = lax.dot_general(a_ref[n].astype(BF16), b_ref[n].astype(BF16), (dims, ((), ())),
                                       preferred_element_type=F32)
                part = term if part is None else part + term
        else:
            part = lax.dot_general(a_ref[...].astype(BF16), b_ref[...].astype(BF16), (dims, ((), ())),
                                   preferred_element_type=F32)
        if nred == 1:
            if has_res:
                part = part + r_ref[...]
            o_ref[...] = part.astype(o_ref.dtype)
        else:
            acc = refs[-1]
            r = pl.program_id(2)

            @pl.when(r == 0)
            def _():
                acc[...] = part

            @pl.when(r > 0)
            def _():
                acc[...] += part

            @pl.when(r == nred - 1)
            def _():
                tot = acc[...]
                if has_res:
                    tot = tot + r_ref[...]
                o_ref[...] = tot.astype(o_ref.dtype)

    in_specs = [a_spec, b_spec] + ([res_spec] if has_res else [])
    args = (a, b) + ((res,) if has_res else ())
    acc_shape = tuple(d for d in o_spec.block_shape if d is not None)
    return pl.pallas_call(
        body, name=name, grid=grid, out_shape=jax.ShapeDtypeStruct(out_shape, out_dtype),
        in_specs=in_specs, out_specs=o_spec,
        scratch_shapes=[pltpu.VMEM(acc_shape, F32)] if nred > 1 else [],
        compiler_params=_params(),
    )(*args)


def _mm_nn(a, w, name, res=None, tn=None, out_dtype=F32):
    m, k = a.shape
    n = w.shape[1]
    tn = tn or n
    tm = min(MM_TILE, m)
    ospec = pl.BlockSpec((tm, tn), lambda i, j, r: (i, j))
    return _matmul(a, w, (m, n), grid=(m // tm, n // tn, 1),
                   a_spec=pl.BlockSpec((tm, k), lambda i, j, r: (i, 0)),
                   b_spec=pl.BlockSpec((k, tn), lambda i, j, r: (0, j)),
                   o_spec=ospec, dims=NN, nred=1, name=name, res=res, res_spec=ospec if res is not None else None,
                   out_dtype=out_dtype)


def _mm_nt(a, w, name, out_dtype=F32):
    m, n = a.shape
    k = w.shape[0]
    tm = min(MM_TILE, m)
    return _matmul(a, w, (m, k), grid=(m // tm, 1, 1),
                   a_spec=pl.BlockSpec((tm, n), lambda i, j, r: (i, 0)),
                   b_spec=pl.BlockSpec((k, n), lambda i, j, r: (0, 0)),
                   o_spec=pl.BlockSpec((tm, k), lambda i, j, r: (i, 0)), dims=NT, nred=1, name=name,
                   out_dtype=out_dtype)


def _norm_matmul(h, g, b, out_shape, *, grid, b_spec, o_spec, name, out_dtype=F32, dims=NN):
    s, d = h.shape
    tm = s // grid[0]

    def body(h_ref, g_ref, b_ref, o_ref, xn_ref):
        @pl.when(pl.program_id(1) == 0)
        def _():
            hv = h_ref[...]
            r = lax.rsqrt(jnp.mean(hv * hv, axis=-1, keepdims=True) + RMS_EPS)
            xn_ref[...] = (hv * r * g_ref[...]).astype(xn_ref.dtype)

        if len(b_ref.shape) == 3:
            for n in range(b_ref.shape[0]):
                o_ref[n] = lax.dot_general(xn_ref[...], b_ref[n].astype(BF16), (dims, ((), ())),
                                           preferred_element_type=F32).astype(o_ref.dtype)
        else:
            o_ref[...] = lax.dot_general(xn_ref[...], b_ref[...].astype(BF16), (dims, ((), ())),
                                         preferred_element_type=F32).astype(o_ref.dtype)

    row = pl.BlockSpec((tm, d), lambda i, j: (i, 0))
    return pl.pallas_call(
        body, name=name, grid=grid,
        out_shape=(jax.ShapeDtypeStruct(out_shape, out_dtype), jax.ShapeDtypeStruct((s, d), BF16)),
        in_specs=[row, pl.BlockSpec((1, d), lambda i, j: (0, 0)), b_spec],
        out_specs=(o_spec, row), compiler_params=_params(),
    )(h, g.reshape(1, d), b)


def _matmul_rms_bwd(a, w, h, g, res, name, tm=SLAB_TILE, dims=NT):
    slabs = a.shape[0] if a.ndim == 3 else 0
    s, n = a.shape[-2:]
    d = w.shape[-2] if dims == NT else w.shape[-1]
    tm = min(tm, s)

    def body(a_ref, w_ref, h_ref, g_ref, r_ref, dh_ref, dg_ref):
        if slabs:
            dy = None
            for j in range(slabs):
                term = lax.dot_general(a_ref[j].astype(BF16), w_ref[j].astype(BF16), (dims, ((), ())),
                                       preferred_element_type=F32)
                dy = term if dy is None else dy + term
        else:
            dy = lax.dot_general(a_ref[...].astype(BF16), w_ref[...].astype(BF16), (dims, ((), ())),
                                 preferred_element_type=F32)
        hv = h_ref[...]
        r = lax.rsqrt(jnp.mean(hv * hv, axis=-1, keepdims=True) + RMS_EPS)
        hn = hv * r
        u = dy * g_ref[...]
        dh_ref[...] = r * (u - hn * jnp.mean(u * hn, axis=-1, keepdims=True)) + r_ref[...]
        part = jnp.sum(dy * hn, axis=0, keepdims=True)

        @pl.when(pl.program_id(0) == 0)
        def _():
            dg_ref[...] = part

        @pl.when(pl.program_id(0) > 0)
        def _():
            dg_ref[...] += part

    row = pl.BlockSpec((tm, d), lambda i: (i, 0))
    vec = pl.BlockSpec((1, d), lambda i: (0, 0))
    if slabs:
        a_spec = pl.BlockSpec((slabs, tm, n), lambda i: (0, i, 0))
        w_spec = pl.BlockSpec(w.shape, lambda i: (0, 0, 0))
    else:
        a_spec = pl.BlockSpec((tm, n), lambda i: (i, 0))
        w_spec = pl.BlockSpec(w.shape, lambda i: (0, 0))
    dh, dg = pl.pallas_call(
        body, name=name, grid=(s // tm,),
        out_shape=(jax.ShapeDtypeStruct((s, d), F32), jax.ShapeDtypeStruct((1, d), F32)),
        in_specs=[a_spec, w_spec, row, vec, row], out_specs=(row, vec), compiler_params=_params(),
    )(a, w, h, g.reshape(1, d), res)
    return dh, dg.reshape(d)


GRAD_DTYPE = BF16


def _mm_tn(a, b, name, tk=512, tn=None):
    s, k = a.shape
    n = b.shape[1]
    tn = tn or n
    tk = min(tk, k)
    ts = s if b.dtype == BF16 and tn * s * 2 <= TN_BLOCK_BYTES else max(s // 2, 1)
    return _matmul(a, b, (k, n), grid=(k // tk, n // tn, s // ts),
                   a_spec=pl.BlockSpec((ts, tk), lambda i, j, r: (r, i)),
                   b_spec=pl.BlockSpec((ts, tn), lambda i, j, r: (r, j)),
                   o_spec=pl.BlockSpec((tk, tn), lambda i, j, r: (i, j)), dims=TN, nred=s // ts, name=name,
                   out_dtype=GRAD_DTYPE)


def _rms_fwd(h, g, name):
    s, d = h.shape
    tm = min(ROW_TILE, s)

    def body(h_ref, g_ref, o_ref):
        hv = h_ref[...]
        r = lax.rsqrt(jnp.mean(hv * hv, axis=-1, keepdims=True) + RMS_EPS)
        o_ref[...] = (hv * r * g_ref[...]).astype(o_ref.dtype)

    return pl.pallas_call(
        body, name=name, grid=(s // tm,), out_shape=jax.ShapeDtypeStruct((s, d), BF16),
        in_specs=[pl.BlockSpec((tm, d), lambda i: (i, 0)), pl.BlockSpec((1, d), lambda i: (0, 0))],
        out_specs=pl.BlockSpec((tm, d), lambda i: (i, 0)), compiler_params=_params(),
    )(h, g.reshape(1, d))


def _rms_bwd(dy, h, g, res, name):
    s, d = h.shape
    tm = min(ROW_TILE, s)
    has_res = res is not None

    def body(*refs):
        dy_ref, h_ref, g_ref = refs[:3]
        r_ref = refs[3] if has_res else None
        dh_ref, dg_ref = refs[-2], refs[-1]
        hv = h_ref[...]
        r = lax.rsqrt(jnp.mean(hv * hv, axis=-1, keepdims=True) + RMS_EPS)
        hn = hv * r
        dyv = dy_ref[...].astype(F32)
        u = dyv * g_ref[...]
        dh = r * (u - hn * jnp.mean(u * hn, axis=-1, keepdims=True))
        if has_res:
            dh = dh + r_ref[...]
        dh_ref[...] = dh
        part = jnp.sum(dyv * hn, axis=0, keepdims=True)

        @pl.when(pl.program_id(0) == 0)
        def _():
            dg_ref[...] = part

        @pl.when(pl.program_id(0) > 0)
        def _():
            dg_ref[...] += part

    row = pl.BlockSpec((tm, d), lambda i: (i, 0))
    vec = pl.BlockSpec((1, d), lambda i: (0, 0))
    dh, dg = pl.pallas_call(
        body, name=name, grid=(s // tm,),
        out_shape=(jax.ShapeDtypeStruct((s, d), F32), jax.ShapeDtypeStruct((1, d), F32)),
        in_specs=[row, row, vec] + ([row] if has_res else []),
        out_specs=(row, vec), compiler_params=_params(),
    )(*((dy, h, g.reshape(1, d)) + ((res,) if has_res else ())))
    return dh, dg.reshape(d)


def _loss_head(h, g, target):
    s, d = h.shape
    tm = min(ROW_TILE, s)

    def body(h_ref, g_ref, t_ref, loss_ref, dh_ref, dg_ref):
        hv = h_ref[...]
        r = lax.rsqrt(jnp.mean(hv * hv, axis=-1, keepdims=True) + RMS_EPS)
        hn = hv * r
        gv = g_ref[...]
        err = hn * gv - t_ref[...]
        rows = jnp.mean(err * err, axis=-1, keepdims=True)
        lpart = 0.5 * jnp.sum(rows, axis=0, keepdims=True) + jnp.zeros((1, 128), F32)
        dy = err * (1.0 / d)
        u = dy * gv
        dh_ref[...] = r * (u - hn * jnp.mean(u * hn, axis=-1, keepdims=True))
        gpart = jnp.sum(dy * hn, axis=0, keepdims=True)

        @pl.when(pl.program_id(0) == 0)
        def _():
            dg_ref[...] = gpart
            loss_ref[...] = lpart

        @pl.when(pl.program_id(0) > 0)
        def _():
            dg_ref[...] += gpart
            loss_ref[...] += lpart

    row = pl.BlockSpec((tm, d), lambda i: (i, 0))
    vec = pl.BlockSpec((1, d), lambda i: (0, 0))
    return pl.pallas_call(
        body, name="loss_head", grid=(s // tm,),
        out_shape=(jax.ShapeDtypeStruct((1, 128), F32), jax.ShapeDtypeStruct((s, d), F32),
                   jax.ShapeDtypeStruct((1, d), F32)),
        in_specs=[row, vec, row],
        out_specs=(pl.BlockSpec((1, 128), lambda i: (0, 0)), row, vec), compiler_params=_params(),
    )(h, g.reshape(1, d), target)


def _shift_down(x, k):
    return pltpu.roll(x, k, 0)


def _shift_up(x, k):
    return pltpu.roll(x, x.shape[0] - k, 0)


def _conv3(x, w):
    return w[2:3, :] * x + w[1:2, :] * _shift_down(x, 1) + w[0:1, :] * _shift_down(x, 2)


def _conv3_t(x, w):
    return w[2:3, :] * x + w[1:2, :] * _shift_up(x, 1) + w[0:1, :] * _shift_up(x, 2)


def _sigmoid(x):
    return 1.0 / (1.0 + jnp.exp(-x))


def _prev_map(tile, halo, col):
    return lambda i: (jnp.maximum(i * (tile // halo) - 1, 0), col)


def _next_map(tile, halo, col, nrows):
    return lambda i: (jnp.minimum((i + 1) * (tile // halo), nrows // halo - 1), col)


def _sconv_fwd(proj, w):
    s = proj.shape[0]
    t = min(ROW_TILE, s)

    def body(cur_ref, prev_ref, w_ref, o_ref):
        i = pl.program_id(0)
        prev = prev_ref[...] * (i > 0).astype(F32)
        ext = jnp.concatenate([prev, cur_ref[...]], axis=0)
        sv = ext[:, 2 * GROUP:3 * GROUP] * ext[:, 0:GROUP]
        y = ext[:, GROUP:2 * GROUP] * _conv3(sv, w_ref[...])
        o_ref[...] = y[8:].astype(o_ref.dtype)

    return pl.pallas_call(
        body, name="sconv_fwd", grid=(s // t,), out_shape=jax.ShapeDtypeStruct((s, 4 * GROUP), BF16),
        in_specs=[pl.BlockSpec((t, 3 * GROUP), lambda i: (i, 0)),
                  pl.BlockSpec((8, 3 * GROUP), _prev_map(t, 8, 0)),
                  pl.BlockSpec((3, GROUP), lambda i: (0, 0))],
        out_specs=pl.BlockSpec((t, GROUP), lambda i: (i, 0)), compiler_params=_params(),
    )(proj, proj, w)


def _sconv_bwd(proj, w, dy):
    s = proj.shape[0]
    t = min(ROW_TILE, s)
    nt = s // t

    def body(cur_ref, prev_ref, next_ref, w_ref, dy_ref, dyn_ref, dp_ref, dw_ref):
        i = pl.program_id(0)
        first = (i > 0).astype(F32)
        last = (i < nt - 1).astype(F32)
        ext = jnp.concatenate([prev_ref[...] * first, cur_ref[...], next_ref[...] * last], axis=0)
        dye = jnp.concatenate([jnp.zeros((8, GROUP), F32), dy_ref[...], dyn_ref[...] * last], axis=0)
        hv, bv, cv = ext[:, 0:GROUP], ext[:, GROUP:2 * GROUP], ext[:, 2 * GROUP:3 * GROUP]
        wv = w_ref[...]
        sv = cv * hv
        conv = _conv3(sv, wv)
        dconv = dye * bv
        ds = _conv3_t(dconv, wv)
        dp = jnp.concatenate([ds * cv, dye * conv, ds * hv], axis=1)
        dp_ref[...] = dp[8:8 + t].astype(dp_ref.dtype)
        dc = dconv[8:8 + t]
        dw = jnp.concatenate([
            jnp.sum(dc * _shift_down(sv, 2)[8:8 + t], axis=0, keepdims=True),
            jnp.sum(dc * _shift_down(sv, 1)[8:8 + t], axis=0, keepdims=True),
            jnp.sum(dc * sv[8:8 + t], axis=0, keepdims=True),
            jnp.zeros((5, GROUP), F32)], axis=0)

        @pl.when(i == 0)
        def _():
            dw_ref[...] = dw

        @pl.when(i > 0)
        def _():
            dw_ref[...] += dw

    dp, dw = pl.pallas_call(
        body, name="sconv_bwd", grid=(nt,),
        out_shape=(jax.ShapeDtypeStruct((s, N_IN_PAD), BF16), jax.ShapeDtypeStruct((8, GROUP), F32)),
        in_specs=[pl.BlockSpec((t, 3 * GROUP), lambda i: (i, 0)),
                  pl.BlockSpec((8, 3 * GROUP), _prev_map(t, 8, 0)),
                  pl.BlockSpec((8, 3 * GROUP), _next_map(t, 8, 0, s)),
                  pl.BlockSpec((3, GROUP), lambda i: (0, 0)),
                  pl.BlockSpec((t, GROUP), lambda i: (i, 0)),
                  pl.BlockSpec((8, GROUP), _next_map(t, 8, 0, s))],
        out_specs=(pl.BlockSpec((t, 3 * GROUP), lambda i: (i, 0)), pl.BlockSpec((8, GROUP), lambda i: (0, 0))),
        compiler_params=_params(),
    )(proj, proj, proj, w, dy, dy)
    return dp, dw[:3]


def _lane_window(shape):
    lane = lax.broadcasted_iota(jnp.int32, shape, 1)
    return lane, jnp.where(lane < 64, 2.0, jnp.where(lane < 128, 4.0, jnp.where(lane < 192, 8.0, 16.0)))


def _by_group(lane, s1, s2, s3, s4):
    return jnp.where(lane < 64, s1, jnp.where(lane < 128, s2, jnp.where(lane < 192, s3, s4)))


def _pool_z(ext, row0):
    s1 = ext + _shift_down(ext, 1)
    s2 = s1 + _shift_down(s1, 2)
    s3 = s2 + _shift_down(s2, 4)
    s4 = s3 + _shift_down(s3, 8)
    lane, win = _lane_window(ext.shape)
    tpos = (lax.broadcasted_iota(jnp.int32, ext.shape, 0) + (row0 - 16 + 1)).astype(F32)
    cnt = jnp.maximum(jnp.minimum(tpos, win), 1.0)
    return _by_group(lane, s1, s2, s3, s4) / cnt - ext


ANY_SPEC = pl.BlockSpec(memory_space=pl.ANY)


def _pool_fwd(proj, wbd, scale, ybuf):
    s = proj.shape[0]
    t = min(ROW_TILE, s)
    col = (COL_GATE - GROUP) // GROUP

    def body(cur_ref, prev_ref, w_ref, sc_ref, buf_ref, o_ref):
        i = pl.program_id(0)
        ext = jnp.concatenate([prev_ref[...] * (i > 0).astype(F32), cur_ref[...]], axis=0)
        z = _pool_z(ext, i * t)[16:]
        y = jnp.dot(z.astype(BF16), w_ref[...].astype(BF16), preferred_element_type=F32)
        o_ref[...] = (y * sc_ref[...]).astype(o_ref.dtype)

    return pl.pallas_call(
        body, name="pool_fwd", grid=(s // t,), out_shape=jax.ShapeDtypeStruct(ybuf.shape, ybuf.dtype),
        in_specs=[pl.BlockSpec((t, GROUP), lambda i: (i, col)),
                  pl.BlockSpec((16, GROUP), _prev_map(t, 16, col)),
                  pl.BlockSpec((GROUP, GROUP), lambda i: (0, 0)),
                  pl.BlockSpec((1, GROUP), lambda i: (0, 0)), ANY_SPEC],
        out_specs=pl.BlockSpec((t, GROUP), lambda i: (i, 3)), input_output_aliases={4: 0},
        compiler_params=_params(),
    )(proj, proj, wbd, scale.reshape(1, GROUP), ybuf)


def _pool_bwd(proj, wbd, scale, dy, dbuf):
    s = proj.shape[0]
    t = min(ROW_TILE, s)
    nt = s // t
    col = (COL_GATE - GROUP) // GROUP

    def body(cur_ref, prev_ref, w_ref, sc_ref, dy_ref, dyn_ref, buf_ref, dp_ref, dw_ref, dsc_ref):
        i = pl.program_id(0)
        ext = jnp.concatenate([prev_ref[...] * (i > 0).astype(F32), cur_ref[...]], axis=0)
        z = _pool_z(ext, i * t)[16:]
        wv = w_ref[...].astype(BF16)
        dyc = dy_ref[...]
        dye = jnp.concatenate([dyc, dyn_ref[...] * (i < nt - 1).astype(F32)], axis=0) * sc_ref[...]
        dz = lax.dot_general(dye.astype(BF16), wv, (NT, ((), ())), preferred_element_type=F32)
        lane, win = _lane_window(dz.shape)
        tpos = (lax.broadcasted_iota(jnp.int32, dz.shape, 0) + (i * t + 1)).astype(F32)
        e = dz / jnp.minimum(tpos, win)
        f1 = e + _shift_up(e, 1)
        f2 = f1 + _shift_up(f1, 2)
        f3 = f2 + _shift_up(f2, 4)
        f4 = f3 + _shift_up(f3, 8)
        dp = _by_group(lane, f1, f2, f3, f4) - dz
        dp_ref[...] = dp[:t].astype(dp_ref.dtype)
        zb = z.astype(BF16)
        y = jnp.dot(zb, wv, preferred_element_type=F32)
        dsc = jnp.sum(dyc * y, axis=0, keepdims=True)
        dw = lax.dot_general(zb, dye[:t].astype(BF16), (TN, ((), ())), preferred_element_type=F32)

        @pl.when(i == 0)
        def _():
            dw_ref[...] = dw
            dsc_ref[...] = dsc

        @pl.when(i > 0)
        def _():
            dw_ref[...] += dw
            dsc_ref[...] += dsc

    dp, dw, dsc = pl.pallas_call(
        body, name="pool_bwd", grid=(nt,),
        out_shape=(jax.ShapeDtypeStruct(dbuf.shape, dbuf.dtype), jax.ShapeDtypeStruct((GROUP, GROUP), F32),
                   jax.ShapeDtypeStruct((1, GROUP), F32)),
        in_specs=[pl.BlockSpec((t, GROUP), lambda i: (i, col)),
                  pl.BlockSpec((16, GROUP), _prev_map(t, 16, col)),
                  pl.BlockSpec((GROUP, GROUP), lambda i: (0, 0)),
                  pl.BlockSpec((1, GROUP), lambda i: (0, 0)),
                  pl.BlockSpec((t, GROUP), lambda i: (i, 3)),
                  pl.BlockSpec((16, GROUP), _next_map(t, 16, 3, s)), ANY_SPEC],
        out_specs=(pl.BlockSpec((t, GROUP), lambda i: (i, col)), pl.BlockSpec((GROUP, GROUP), lambda i: (0, 0)),
                   pl.BlockSpec((1, GROUP), lambda i: (0, 0))),
        input_output_aliases={6: 0}, compiler_params=_params(),
    )(proj, proj, wbd, scale.reshape(1, GROUP), dy, dy, dbuf)
    return dp, dw, dsc.reshape(GROUP)


FF_HALO = 16


def _ffn_gate_fwd(u0, w):
    s = u0.shape[1]
    t = min(ROW_TILE, s)

    def body(a_ref, ap_ref, g_ref, gp_ref, wa_ref, wg_ref, o_ref):
        first = (pl.program_id(1) > 0).astype(F32)
        a = _conv3(jnp.concatenate([ap_ref[...] * first, a_ref[...].astype(F32)], axis=0), wa_ref[...])[FF_HALO:]
        g = _conv3(jnp.concatenate([gp_ref[...] * first, g_ref[...].astype(F32)], axis=0), wg_ref[...])[FF_HALO:]
        o_ref[...] = (a * (g * _sigmoid(g))).astype(o_ref.dtype)

    def cur(off):
        return pl.BlockSpec((None, t, FF_SHARD), lambda j, i: (j + off, i, 0))

    def prev(off):
        return pl.BlockSpec((None, FF_HALO, FF_SHARD),
                            lambda j, i: (j + off, jnp.maximum(i * (t // FF_HALO) - 1, 0), 0))

    def wspec(off):
        return pl.BlockSpec((None, 3, FF_SHARD), lambda j, i: (j + off, 0, 0))

    return pl.pallas_call(
        body, name="ffn_gate_fwd", grid=(FF_HALF, s // t),
        out_shape=jax.ShapeDtypeStruct((FF_HALF, s, FF_SHARD), BF16),
        in_specs=[cur(0), prev(0), cur(FF_HALF), prev(FF_HALF), wspec(0), wspec(FF_HALF)],
        out_specs=pl.BlockSpec((None, t, FF_SHARD), lambda j, i: (j, i, 0)), compiler_params=_params(),
    )(u0, u0, u0, u0, w, w)


def _ffn_gate_bwd(u0, w, dact):
    s = u0.shape[1]
    t = min(ROW_TILE, s)
    nt = s // t

    def body(c_ref, p_ref, n_ref, w_ref, d_ref, dn_ref, du_ref, dw_ref):
        i = pl.program_id(1)
        first = (i > 0).astype(F32)
        last = (i < nt - 1).astype(F32)
        dext = jnp.concatenate([jnp.zeros((FF_HALO, FF_SHARD), F32), d_ref[...].astype(F32), dn_ref[...] * last],
                               axis=0)
        ext = [jnp.concatenate([p_ref[n] * first, c_ref[n].astype(F32), n_ref[n] * last], axis=0) for n in range(2)]
        a = _conv3(ext[0], w_ref[0])
        g = _conv3(ext[1], w_ref[1])
        sg = _sigmoid(g)
        silu = g * sg
        dus = (dext * silu, dext * a * (sg + silu * (1.0 - sg)))
        mine = slice(FF_HALO, FF_HALO + t)
        for n in range(2):
            du_ref[n] = _conv3_t(dus[n], w_ref[n])[mine].astype(du_ref.dtype)
            dc = dus[n][mine]
            dw = jnp.concatenate([
                jnp.sum(dc * _shift_down(ext[n], 2)[mine], axis=0, keepdims=True),
                jnp.sum(dc * _shift_down(ext[n], 1)[mine], axis=0, keepdims=True),
                jnp.sum(dc * ext[n][mine], axis=0, keepdims=True),
                jnp.zeros((5, FF_SHARD), F32)], axis=0)

            @pl.when(i == 0)
            def _(n=n, dw=dw):
                dw_ref[n] = dw

            @pl.when(i > 0)
            def _(n=n, dw=dw):
                dw_ref[n] += dw

    def pair(rows, row_map):
        return pl.BlockSpec((2, None, rows, FF_SHARD), lambda j, i: (0, j, row_map(i), 0))

    prev_row = lambda i: jnp.maximum(i * (t // FF_HALO) - 1, 0)
    next_row = lambda i: jnp.minimum((i + 1) * (t // FF_HALO), s // FF_HALO - 1)
    u2 = u0.reshape(2, FF_HALF, s, FF_SHARD)
    du, dw = pl.pallas_call(
        body, name="ffn_gate_bwd", grid=(FF_HALF, nt),
        out_shape=(jax.ShapeDtypeStruct((2, FF_HALF, s, FF_SHARD), BF16),
                   jax.ShapeDtypeStruct((2, FF_HALF, 8, FF_SHARD), F32)),
        in_specs=[pair(t, lambda i: i), pair(FF_HALO, prev_row), pair(FF_HALO, next_row), pair(3, lambda i: 0),
                  pl.BlockSpec((None, t, FF_SHARD), lambda j, i: (j, i, 0)),
                  pl.BlockSpec((None, FF_HALO, FF_SHARD), lambda j, i: (j, next_row(i), 0))],
        out_specs=(pair(t, lambda i: i), pair(8, lambda i: 0)),
        compiler_params=_params(),
    )(u2, u2, u2, w.reshape(2, FF_HALF, 3, FF_SHARD), dact, dact)
    return du.reshape(2 * FF_HALF, s, FF_SHARD), dw.reshape(2 * FF_HALF, 8, FF_SHARD)[:, :3]


def _rope_tables(positions):
    inv_freq = ROPE_THETA ** (-jnp.arange(0, ROPE_DIM, 2, dtype=F32) / ROPE_DIM)
    ang = positions.astype(F32)[:, None] * inv_freq
    cos, sin = jnp.cos(ang), jnp.sin(ang)
    s = positions.shape[0]
    half = ROPE_DIM // 2
    rest = HEAD_DIM - ROPE_DIM
    ca = jnp.concatenate([cos, cos, jnp.ones((s, rest), F32)], axis=1)
    cb = jnp.concatenate([-sin, jnp.zeros((s, HEAD_DIM - half), F32)], axis=1)
    cc = jnp.concatenate([jnp.zeros((s, half), F32), sin, jnp.zeros((s, rest), F32)], axis=1)
    return tuple(jnp.tile(tb, (1, N_HEADS)) for tb in (ca, cb, cc))


QK_WIDE = 128
LANE_CQ, LANE_CK = 64, 67
KT_ROWS = 80


def _three_bf16(x):
    hi = x.astype(BF16).astype(F32)
    mid = (x - hi).astype(BF16).astype(F32)
    lo = (x - hi - mid).astype(BF16).astype(F32)
    return hi, mid, lo


def _heads_split(proj, col, tables, c, name):
    s = proj.shape[0]
    t = min(ROW_TILE, s)
    rope = tables is not None
    wide = c is not None
    width = QK_WIDE if wide else HEAD_DIM

    def body(*refs):
        x_ref = refs[0]
        q_ref, k_ref, v_ref, kt_ref, vt_ref = refs[-5:]
        xv = x_ref[...]
        parts = [xv[:, 0:GROUP], xv[:, GROUP:2 * GROUP], xv[:, 2 * GROUP:3 * GROUP]]
        if rope:
            ca, cb, cc = refs[1][...], refs[2][...], refs[3][...]
            for n in range(2):
                p = parts[n]
                parts[n] = p * ca + pltpu.roll(p, GROUP - 8, 1) * cb + pltpu.roll(p, 8, 1) * cc
        parts[0] = parts[0] * (HEAD_DIM ** -0.5)
        k_t, v_t = parts[1].T, parts[2].T
        ones_row = jnp.where(lax.broadcasted_iota(jnp.int32, (KT_ROWS - HEAD_DIM, t), 0) == 0, 1.0, 0.0)
        lane = lax.broadcasted_iota(jnp.int32, (t, QK_WIDE), 1)
        zeros = jnp.zeros((t, QK_WIDE - HEAD_DIM), F32)
        for h in range(N_HEADS):
            hs = slice(h * HEAD_DIM, (h + 1) * HEAD_DIM)
            qh, kh = parts[0][:, hs], parts[1][:, hs]
            if wide:
                terms = _three_bf16(refs[-6][:, h:h + 1])
                qh = jnp.concatenate([qh, zeros], axis=1)
                kh = jnp.concatenate([kh, zeros], axis=1)
                for n in range(3):
                    qh = jnp.where(lane == LANE_CQ + n, terms[n], jnp.where(lane == LANE_CK + n, 1.0, qh))
                    kh = jnp.where(lane == LANE_CK + n, -terms[n], jnp.where(lane == LANE_CQ + n, 1.0, kh))
            q_ref[h] = qh.astype(q_ref.dtype)
            k_ref[h] = kh.astype(k_ref.dtype)
            v_ref[h] = parts[2][:, hs].astype(v_ref.dtype)
            kt_ref[h] = jnp.concatenate([k_t[hs, :], ones_row], axis=0).astype(kt_ref.dtype)
            vt_ref[h] = v_t[hs, :].astype(vt_ref.dtype)

    tab = pl.BlockSpec((t, GROUP), lambda i: (i, 0))
    qk = pl.BlockSpec((N_HEADS, t, width), lambda i: (0, i, 0))
    heads = pl.BlockSpec((N_HEADS, t, HEAD_DIM), lambda i: (0, i, 0))
    heads_t = pl.BlockSpec((N_HEADS, HEAD_DIM, t), lambda i: (0, 0, i))
    qk_shape = jax.ShapeDtypeStruct((N_HEADS, s, width), BF16)
    return pl.pallas_call(
        body, name=name, grid=(s // t,),
        out_shape=(qk_shape, qk_shape, jax.ShapeDtypeStruct((N_HEADS, s, HEAD_DIM), BF16),
                   jax.ShapeDtypeStruct((N_HEADS, KT_ROWS, s), BF16),
                   jax.ShapeDtypeStruct((N_HEADS, HEAD_DIM, s), BF16)),
        in_specs=[pl.BlockSpec((t, 3 * GROUP), lambda i: (i, col))] + ([tab, tab, tab] if rope else [])
        + ([pl.BlockSpec((t, 128), lambda i: (i, 0))] if wide else []),
        out_specs=(qk, qk, heads, pl.BlockSpec((N_HEADS, KT_ROWS, t), lambda i: (0, 0, i)), heads_t),
        compiler_params=_params(),
    )(*((proj,) + (tuple(tables) if rope else ()) + ((c,) if wide else ())))


def _heads_merge(dqt, dk, dv, tables, name, dbuf, col):
    s = dv.shape[1]
    t = min(ROW_TILE, s)
    rope = tables is not None

    wide = dk.shape[2] == QK_WIDE

    def body(*refs):
        o_ref = refs[n_in + 1]
        dq = jnp.concatenate([refs[0][h, :HEAD_DIM, :] for h in range(N_HEADS)], axis=0).T
        parts = [dq] + [jnp.concatenate([r[h][:, :HEAD_DIM] for h in range(N_HEADS)], axis=1) for r in refs[1:3]]
        parts[0] = parts[0] * (HEAD_DIM ** -0.5)
        if rope:
            ca, cb, cc = refs[3][...], refs[4][...], refs[5][...]
            for n in range(2):
                p = parts[n]
                parts[n] = p * ca + pltpu.roll(p * cb, 8, 1) + pltpu.roll(p * cc, GROUP - 8, 1)
        o_ref[...] = jnp.concatenate(parts, axis=1).astype(o_ref.dtype)
        if wide:
            over_keys = jnp.concatenate([refs[0][h, HEAD_DIM:HEAD_DIM + 8, :] for h in range(N_HEADS)]
                                        + [jnp.zeros((128 - 8 * N_HEADS, t), F32)], axis=0).T
            lane = lax.broadcasted_iota(jnp.int32, (t, 128), 1)
            dc = jnp.zeros((t, 128), F32)
            for h in range(N_HEADS):
                dc = jnp.where(lane == h, over_keys[:, 8 * h:8 * h + 1] - refs[1][h][:, LANE_CK:LANE_CK + 1], dc)
            refs[n_in + 2][...] = dc

    tab = pl.BlockSpec((t, GROUP), lambda i: (i, 0))
    heads = pl.BlockSpec((N_HEADS, t, HEAD_DIM), lambda i: (0, i, 0))
    n_in = 6 if rope else 3
    dspec = pl.BlockSpec((t, 3 * GROUP), lambda i: (i, col))
    dshape = jax.ShapeDtypeStruct(dbuf.shape, dbuf.dtype)
    return pl.pallas_call(
        body, name=name, grid=(s // t,),
        out_shape=(dshape, jax.ShapeDtypeStruct((s, 128), F32)) if wide else dshape,
        in_specs=[pl.BlockSpec((N_HEADS, KT_ROWS, t), lambda i: (0, 0, i)),
                  pl.BlockSpec((N_HEADS, t, dk.shape[2]), lambda i: (0, i, 0)), heads]
        + ([tab, tab, tab] if rope else []) + [ANY_SPEC],
        out_specs=(dspec, pl.BlockSpec((t, 128), lambda i: (i, 0))) if wide else dspec,
        input_output_aliases={n_in: 0}, compiler_params=_params(),
    )(*((dqt, dk, dv) + (tuple(tables) if rope else ()) + (dbuf,)))


def _log_sigmoid(x):
    return jnp.minimum(x, 0.0) - jnp.log(1.0 + jnp.exp(-jnp.abs(x)))


def _scan_rows(x, reverse):
    n = x.shape[0]
    row = lax.broadcasted_iota(jnp.int32, x.shape, 0)
    k = 1
    while k < n:
        if reverse:
            x = x + jnp.where(row < n - k, _shift_up(x, k), 0.0)
        else:
            x = x + jnp.where(row >= k, _shift_down(x, k), 0.0)
        k *= 2
    return x


def _gate_cumsum(proj, bias):
    s = proj.shape[0]
    col = COL_GATE // 128

    def body(z_ref, b_ref, c_ref):
        c_ref[...] = _scan_rows(_log_sigmoid(z_ref[...] + b_ref[...]), False)

    return pl.pallas_call(
        body, name="gate_cumsum", grid=(1,), out_shape=jax.ShapeDtypeStruct((s, 128), F32),
        in_specs=[pl.BlockSpec((s, 128), lambda i: (0, col)), pl.BlockSpec((1, 128), lambda i: (0, 0))],
        out_specs=pl.BlockSpec((s, 128), lambda i: (0, 0)), compiler_params=_params(),
    )(proj, bias)


def _gate_cumsum_bwd(proj, bias, dc, dbuf):
    s = proj.shape[0]
    col = COL_GATE // 128

    def body(z_ref, b_ref, dc_ref, buf_ref, dz_ref, db_ref):
        dlogf = _scan_rows(dc_ref[...], True)
        dz = dlogf * _sigmoid(-(z_ref[...] + b_ref[...]))
        dz_ref[...] = dz.astype(dz_ref.dtype)
        db_ref[...] = jnp.sum(dz, axis=0, keepdims=True)

    return pl.pallas_call(
        body, name="gate_cumsum_bwd", grid=(1,),
        out_shape=(jax.ShapeDtypeStruct(dbuf.shape, dbuf.dtype), jax.ShapeDtypeStruct((1, 128), F32)),
        in_specs=[pl.BlockSpec((s, 128), lambda i: (0, col)), pl.BlockSpec((1, 128), lambda i: (0, 0)),
                  pl.BlockSpec((s, 128), lambda i: (0, 0)), ANY_SPEC],
        out_specs=(pl.BlockSpec((s, 128), lambda i: (0, col)), pl.BlockSpec((1, 128), lambda i: (0, 0))),
        input_output_aliases={3: 0}, compiler_params=_params(),
    )(proj, bias, dc, dbuf)


DIL_REACH = 2048


def _pair_weight(mode, d):
    if mode == "fox":
        return jnp.where(d >= 0, 1.0, 0.0)
    w1 = jnp.where(jnp.abs(d - 64) <= 64, 1.0, 0.0)
    w2 = jnp.where((d & 3) == 0, jnp.where(jnp.abs(d - 256) <= 256, 1.0, 0.0), 0.0)
    w3 = jnp.where((d & 15) == 0, jnp.where(jnp.abs(d - 1024) <= 1024, 1.0, 0.0), 0.0)
    return w1 + w2 + w3


def _bias_tables(mode, tq, tk):
    nb = 2 if mode == "fox" else DIL_REACH // tk + 1
    n = lax.broadcasted_iota(jnp.int32, (nb, tk, tq), 0)
    key = lax.broadcasted_iota(jnp.int32, (nb, tk, tq), 1)
    query = lax.broadcasted_iota(jnp.int32, (nb, tk, tq), 2)
    w = _pair_weight(mode, n * tk + query - key)
    return jnp.where(w > 0.0, jnp.log(jnp.maximum(w, 1.0)), NEG)


M_INIT = -1e29


def _first_key_chunk(mode, q0, tk):
    if mode == "fox":
        return 0
    return jnp.maximum(q0 - DIL_REACH, 0) // tk


def _attention_fwd(mode, q, k, vt, tab_t, ybuf, col):
    s, width = q.shape[1], q.shape[2]
    tq = min(ATT_TQ, s)
    tk = tq
    nb = tab_t.shape[0]

    def body(q_ref, k_ref, vt_ref, tab_ref, buf_ref, y_ref, o_ref, lse_ref):
        i = pl.program_id(0)
        lo = _first_key_chunk(mode, i * tq, tk)

        def step(c, carry):
            k0 = pl.multiple_of(c * tk, tk)
            tab = tab_ref[jnp.minimum(i - c, nb - 1)]
            scores = [lax.dot_general(k_ref[h, pl.ds(k0, tk), :], q_ref[h], (NT, ((), ())),
                                      preferred_element_type=F32) for h in range(N_HEADS)]
            stats, probs = [], []
            for h in range(N_HEADS):
                m, l = carry[3 * h:3 * h + 2]
                sc = scores[h] + tab
                m_new = jnp.maximum(m, jnp.max(sc, axis=0, keepdims=True))
                alpha = jnp.exp(m - m_new)
                p = jnp.exp(sc - m_new)
                stats.append((m_new, alpha * l + jnp.sum(p, axis=0, keepdims=True), alpha))
                probs.append(p.astype(BF16))
            pv = [jnp.dot(vt_ref[h, :, pl.ds(k0, tk)], probs[h], preferred_element_type=F32) for h in range(N_HEADS)]
            new = []
            for h in range(N_HEADS):
                m_new, l, alpha = stats[h]
                new += [m_new, l, alpha * carry[3 * h + 2] + pv[h]]
            return tuple(new)

        start = (jnp.full((1, tq), M_INIT, F32), jnp.zeros((1, tq), F32), jnp.zeros((HEAD_DIM, tq), F32))
        done = lax.fori_loop(lo, i + 1, step, start * N_HEADS)
        outs = []
        for h in range(N_HEADS):
            m, l, acc = done[3 * h:3 * h + 3]
            outs.append(acc / l)
            lse_ref[h] = m + jnp.log(l)
        out = jnp.concatenate(outs, axis=0).T
        y_ref[...] = out.astype(y_ref.dtype)
        o_ref[...] = out

    rowspec = pl.BlockSpec((N_HEADS, 1, tq), lambda i: (0, 0, i))
    return pl.pallas_call(
        body, name="attention_fwd_" + mode, grid=(s // tq,),
        out_shape=(jax.ShapeDtypeStruct(ybuf.shape, ybuf.dtype), jax.ShapeDtypeStruct((s, GROUP), F32),
                   jax.ShapeDtypeStruct((N_HEADS, 1, s), F32)),
        in_specs=[pl.BlockSpec((N_HEADS, tq, width), lambda i: (0, i, 0)),
                  pl.BlockSpec((N_HEADS, s, width), lambda i: (0, 0, 0)),
                  pl.BlockSpec((N_HEADS, HEAD_DIM, s), lambda i: (0, 0, 0)),
                  pl.BlockSpec((nb, tk, tq), lambda i: (0, 0, 0)), ANY_SPEC],
        out_specs=(pl.BlockSpec((tq, GROUP), lambda i: (i, col)), pl.BlockSpec((tq, GROUP), lambda i: (i, 0)),
                   rowspec),
        input_output_aliases={4: 0}, compiler_params=_params(),
    )(q, k, vt, tab_t, ybuf)


def _attention_delta(o, do, col):
    s = o.shape[0]
    t = min(ROW_TILE, s)

    def body(o_ref, do_ref, delta_ref, dob_ref):
        dov = do_ref[...]
        prod_t = (o_ref[...] * dov).T
        for h in range(N_HEADS):
            hs = slice(h * HEAD_DIM, (h + 1) * HEAD_DIM)
            delta_ref[h] = jnp.sum(prod_t[hs, :], axis=0, keepdims=True)
            dob_ref[h] = dov[:, hs].astype(dob_ref.dtype)

    return pl.pallas_call(
        body, name="attention_delta", grid=(s // t,),
        out_shape=(jax.ShapeDtypeStruct((N_HEADS, 1, s), F32), jax.ShapeDtypeStruct((N_HEADS, s, HEAD_DIM), BF16)),
        in_specs=[pl.BlockSpec((t, GROUP), lambda i: (i, 0)), pl.BlockSpec((t, GROUP), lambda i: (i, col))],
        out_specs=(pl.BlockSpec((N_HEADS, 1, t), lambda i: (0, 0, i)),
                   pl.BlockSpec((N_HEADS, t, HEAD_DIM), lambda i: (0, i, 0))),
        compiler_params=_params(),
    )(o, do)


def _attention_bwd(mode, q, k, v, kt, tab_t, dob, lse, delta):
    s, width = q.shape[1], q.shape[2]
    tq = min(ATT_TQ, s)
    tk = tq
    nq = s // tq
    nb = tab_t.shape[0]

    def body(q_ref, k_ref, v_ref, kt_ref, tab_ref, dob_ref, lse_ref, delta_ref, dqt_ref, dk_ref, dv_ref):
        i = pl.program_id(0)

        @pl.when(i == 0)
        def _():
            dqt_ref[...] = jnp.zeros_like(dqt_ref)

        hi = nq if mode == "fox" else jnp.minimum((i * tk + tk - 1 + DIL_REACH) // tq + 1, nq)
        for h0 in range(0, N_HEADS, BWD_HEADS):
            heads = range(h0, h0 + BWD_HEADS)

            def step(c, carry, heads=heads):
                q0 = pl.multiple_of(c * tq, tq)
                qs = pl.ds(q0, tq)
                tab = tab_ref[jnp.minimum(c - i, nb - 1)]
                qv = [q_ref[h, qs, :] for h in heads]
                dov = [dob_ref[h, qs, :] for h in heads]
                sc = [lax.dot_general(k_ref[h], qv[n], (NT, ((), ())), preferred_element_type=F32)
                      for n, h in enumerate(heads)]
                dp = [lax.dot_general(v_ref[h], dov[n], (NT, ((), ())), preferred_element_type=F32)
                      for n, h in enumerate(heads)]
                pb, dsb = [], []
                for n, h in enumerate(heads):
                    p = jnp.exp(sc[n] + tab - lse_ref[h, :, qs])
                    pb.append(p.astype(BF16))
                    dsb.append((p * (dp[n] - delta_ref[h, :, qs])).astype(BF16))
                new = []
                for n, h in enumerate(heads):
                    new += [carry[2 * n] + jnp.dot(dsb[n], qv[n], preferred_element_type=F32),
                            carry[2 * n + 1] + jnp.dot(pb[n], dov[n], preferred_element_type=F32)]
                for n, h in enumerate(heads):
                    dqt_ref[h, :, qs] += jnp.dot(kt_ref[h], dsb[n], preferred_element_type=F32)
                return tuple(new)

            start = (jnp.zeros((tk, width), F32), jnp.zeros((tk, HEAD_DIM), F32))
            done = lax.fori_loop(i, hi, step, start * BWD_HEADS)
            for n, h in enumerate(heads):
                dk_ref[h] = done[2 * n]
                dv_ref[h] = done[2 * n + 1]

    def full(shape):
        return pl.BlockSpec(shape, lambda i: (0, 0, 0))

    kblk = pl.BlockSpec((N_HEADS, tk, width), lambda i: (0, i, 0))
    vblk = pl.BlockSpec((N_HEADS, tk, HEAD_DIM), lambda i: (0, i, 0))
    return pl.pallas_call(
        body, name="attention_bwd_" + mode, grid=(s // tk,),
        out_shape=(jax.ShapeDtypeStruct((N_HEADS, KT_ROWS, s), F32), jax.ShapeDtypeStruct((N_HEADS, s, width), F32),
                   jax.ShapeDtypeStruct((N_HEADS, s, HEAD_DIM), F32)),
        in_specs=[full((N_HEADS, s, width)), kblk, vblk, pl.BlockSpec((N_HEADS, KT_ROWS, tk), lambda i: (0, 0, i)),
                  full((nb, tk, tq)), full((N_HEADS, s, HEAD_DIM)), full((N_HEADS, 1, s)), full((N_HEADS, 1, s))],
        out_specs=(full((N_HEADS, KT_ROWS, s)), kblk, vblk),
        compiler_params=_params(),
    )(q, k, v, kt, tab_t, dob, lse, delta)


def _xattn_fwd(qx, kvm):
    s = qx.shape[0]
    t = min(ROW_TILE, s)

    def body(q_ref, kv_ref, o_ref):
        heads = range(XA_HEADS)
        sc = [lax.dot_general(q_ref[:, h * XA_DIM:(h + 1) * XA_DIM].astype(BF16), kv_ref[h].astype(BF16),
                              (NT, ((), ())), preferred_element_type=F32) * (XA_DIM ** -0.5) for h in heads]
        probs = []
        for h in heads:
            e = jnp.exp(sc[h] - jnp.max(sc[h], axis=-1, keepdims=True))
            probs.append((e / jnp.sum(e, axis=-1, keepdims=True)).astype(BF16))
        outs = [jnp.dot(probs[h], kv_ref[XA_HEADS + h].astype(BF16), preferred_element_type=F32) for h in heads]
        for h in heads:
            o_ref[:, h * XA_DIM:(h + 1) * XA_DIM] = outs[h].astype(o_ref.dtype)

    return pl.pallas_call(
        body, name="xattn_fwd", grid=(s // t,), out_shape=jax.ShapeDtypeStruct((s, D_MODEL), BF16),
        in_specs=[pl.BlockSpec((t, D_MODEL), lambda i: (i, 0)),
                  pl.BlockSpec((2 * XA_HEADS, MEM_LEN, XA_DIM), lambda i: (0, 0, 0))],
        out_specs=pl.BlockSpec((t, D_MODEL), lambda i: (i, 0)), compiler_params=_params(),
    )(qx, kvm)


def _xattn_bwd(qx, kvm, do):
    s = qx.shape[0]
    t = min(ROW_TILE, s)

    def body(q_ref, kv_ref, do_ref, dq_ref, dkv_ref):
        i = pl.program_id(0)
        heads = range(XA_HEADS)
        qv = [q_ref[:, h * XA_DIM:(h + 1) * XA_DIM].astype(BF16) for h in heads]
        dov = [do_ref[:, h * XA_DIM:(h + 1) * XA_DIM].astype(BF16) for h in heads]
        kv = [kv_ref[h].astype(BF16) for h in heads]
        sc = [lax.dot_general(qv[h], kv[h], (NT, ((), ())), preferred_element_type=F32) * (XA_DIM ** -0.5)
              for h in heads]
        dp = [lax.dot_general(dov[h], kv_ref[XA_HEADS + h].astype(BF16), (NT, ((), ())), preferred_element_type=F32)
              for h in heads]
        pb, ds = [], []
        for h in heads:
            e = jnp.exp(sc[h] - jnp.max(sc[h], axis=-1, keepdims=True))
            p = e / jnp.sum(e, axis=-1, keepdims=True)
            pb.append(p.astype(BF16))
            ds.append((p * (dp[h] - jnp.sum(p * dp[h], axis=-1, keepdims=True)) * (XA_DIM ** -0.5)).astype(BF16))
        dq = [jnp.dot(ds[h], kv[h], preferred_element_type=F32) for h in heads]
        dk = [lax.dot_general(ds[h], qv[h], (TN, ((), ())), preferred_element_type=F32) for h in heads]
        dv = [lax.dot_general(pb[h], dov[h], (TN, ((), ())), preferred_element_type=F32) for h in heads]
        for h in heads:
            dq_ref[:, h * XA_DIM:(h + 1) * XA_DIM] = dq[h].astype(dq_ref.dtype)

        @pl.when(i == 0)
        def _():
            for h in heads:
                dkv_ref[h] = dk[h]
                dkv_ref[XA_HEADS + h] = dv[h]

        @pl.when(i > 0)
        def _():
            for h in heads:
                dkv_ref[h] += dk[h]
                dkv_ref[XA_HEADS + h] += dv[h]

    row = pl.BlockSpec((t, D_MODEL), lambda i: (i, 0))
    kvs = pl.BlockSpec((2 * XA_HEADS, MEM_LEN, XA_DIM), lambda i: (0, 0, 0))
    return pl.pallas_call(
        body, name="xattn_bwd", grid=(s // t,),
        out_shape=(jax.ShapeDtypeStruct((s, D_MODEL), BF16),
                   jax.ShapeDtypeStruct((2 * XA_HEADS, MEM_LEN, XA_DIM), F32)),
        in_specs=[row, kvs, row], out_specs=(row, kvs), compiler_params=_params(),
    )(qx, kvm, do)


def _adamw(parts, owns, me, w, m, v, name):
    nl, r, c = w.shape
    tr = r
    for cand in (512, 352, 256, 176, 128, 64, 32, 16, 8):
        if r % cand == 0 and r > cand and N_DEV * cand * c * 4 <= ADAMW_BLOCK_BYTES:
            tr = cand
            break
    nt = r // tr
    per_layer = N_DEV + (1 if owns is not None else 0)

    def body(me_ref, *refs):
        w_ref, m_ref, v_ref, g_ref, d_ref, nm_ref, nv_ref = refs[nl * per_layer:]
        layer = pl.program_id(0)
        g = None
        for l in range(nl):
            p_refs = refs[l * per_layer:(l + 1) * per_layer]
            gl = None
            for d in range(N_DEV):
                term = p_refs[d][...].astype(F32)
                if owns is not None:
                    term = jnp.where(me_ref[0] == d, p_refs[N_DEV][...].astype(F32), term)
                gl = term if gl is None else gl + term
            g = gl if g is None else jnp.where(layer == l, gl, g)
        mn = ADAM_B1 * m_ref[...] + (1.0 - ADAM_B1) * g
        vn = ADAM_B2 * v_ref[...] + (1.0 - ADAM_B2) * (g * g)
        m_hat = mn / (1.0 - ADAM_B1 ** ADAM_STEP)
        v_hat = vn / (1.0 - ADAM_B2 ** ADAM_STEP)
        g_ref[...] = g
        d_ref[...] = -ADAM_LR * (m_hat / (jnp.sqrt(v_hat) + ADAM_EPS) + ADAM_WD * w_ref[...])
        nm_ref[...] = mn
        nv_ref[...] = vn

    def rows(l, ll, i):
        return jnp.where(ll == l, i, jnp.where(ll < l, 0, nt - 1))

    def part_spec(l, d):
        if owns is None:
            return pl.BlockSpec((None, tr, c), lambda ll, i, me_ref: (d, rows(l, ll, i), 0))
        return pl.BlockSpec((None, tr, c),
                            lambda ll, i, me_ref: (jnp.where(me_ref[0] == d, (d + 1) % N_DEV, d), rows(l, ll, i), 0))

    def own_spec(l):
        return pl.BlockSpec((None, tr, c), lambda ll, i, me_ref: (me_ref[0], rows(l, ll, i), 0))

    in_specs, operands = [], []
    for l in range(nl):
        in_specs += [part_spec(l, d) for d in range(N_DEV)]
        operands += [parts[l]] * N_DEV
        if owns is not None:
            in_specs.append(own_spec(l))
            operands.append(owns[l])
    blk = pl.BlockSpec((None, tr, c), lambda ll, i, me_ref: (ll, i, 0))
    shp = jax.ShapeDtypeStruct((nl, r, c), F32)
    return pl.pallas_call(
        body, name=name, out_shape=(shp, shp, shp, shp),
        grid_spec=pltpu.PrefetchScalarGridSpec(
            num_scalar_prefetch=1, grid=(nl, nt), in_specs=in_specs + [blk, blk, blk],
            out_specs=(blk, blk, blk, blk)),
        compiler_params=_params(),
    )(me.reshape(1), *operands, w, m, v)


GROUPS = {"in": ("w_in",), "rest": ("w_out", "w_xq", "w_xo", "w_xkv", "w_up", "w_down")}
FULL_SHAPES = {"w_in": (D_MODEL, N_IN_PAD), "w_out": (D_MODEL, D_MODEL), "w_xq": (D_MODEL, D_MODEL),
               "w_xo": (D_MODEL, D_MODEL), "w_xkv": (N_DEV, D_MODEL, 2 * D_MODEL // N_DEV),
               "w_up": (N_DEV, FF_SHARD, D_MODEL), "w_down": (FF_HALF, FF_SHARD, D_MODEL)}
PIECE_SHAPES = {"w_in": (N_DEV, D_MODEL // N_DEV, N_IN_PAD), "w_out": (N_DEV, D_MODEL // N_DEV, D_MODEL),
                "w_xq": (N_DEV, D_MODEL // N_DEV, D_MODEL), "w_xo": (N_DEV, D_MODEL // N_DEV, D_MODEL),
                "w_xkv": (N_DEV, D_MODEL, 2 * D_MODEL // N_DEV), "w_up": (N_DEV, FF_SHARD, D_MODEL),
                "w_down": (N_DEV, D_FF // N_DEV, D_MODEL)}
GATHER_GROUPS = {"in": ("w_in",), "mid": ("w_out", "w_xq", "w_xo", "w_xkv"), "ffn": ("w_up", "w_down")}
CONV_WORDS = 8192


class _GatheredWeights:
    def __init__(self, states, layer):
        self.states, self.layer, self.full, self.extra = dict(states), layer, {}, None

    def need(self, group, after):
        if group in self.states:
            got, _ = _exchange_wait(self.states.pop(group), after, "gather_%s_wait_%d" % (group, self.layer))
            for name, g in zip(GATHER_GROUPS[group], got):
                self.full[name] = g.reshape(FULL_SHAPES[name])
            self.extra = got[len(GATHER_GROUPS[group]):]

    def __getitem__(self, name):
        return self.full[name]


def _relay_in_cols(w):
    pad = jnp.zeros(w.shape[:-1] + (N_IN_PAD - N_IN,), w.dtype)
    return jnp.concatenate([w[..., :2304], w[..., 2308:N_IN], w[..., 2304:2308], pad], axis=-1)


def _unrelay_in_cols(w):
    return jnp.concatenate([w[..., :2304], w[..., COL_GATE:COL_GATE + 4], w[..., 2304:COL_GATE]], axis=-1)


def _layer_fwd(h, memv, w, sm, tables, rest_arrived):
    sv = {"h0": h}
    s = h.shape[0]
    tm, tb = min(SLAB_TILE, s), min(MM_TILE, s)
    w.need("in", h)
    proj, xn = _norm_matmul(h, sm["g_mix"], w["w_in"], (s, N_IN_PAD), grid=(s // tm, 1),
                            b_spec=pl.BlockSpec((D_MODEL, N_IN_PAD), lambda i, j: (0, 0)),
                            o_spec=pl.BlockSpec((tm, N_IN_PAD), lambda i, j: (i, 0)), name="norm_mm_in")
    sv["xn"], sv["proj"] = xn, proj
    ycat = _sconv_fwd(proj, sm["w_sconv"])
    qd, kd, vd, ktd, vtd = _heads_split(proj, 1, tables["rope"], None, "split_dil")
    ycat, ob, lse_b = _attention_fwd("dil", qd, kd, vtd, tables["dil"], ycat, 1)
    sv["dil"] = (qd, kd, vd, ktd, ob, lse_b)
    c = _gate_cumsum(proj, sm["b_forget_pad"])
    qf, kf, vf, ktf, vtf = _heads_split(proj, 2, None, c, "split_fox")
    ycat, oc, lse_c = _attention_fwd("fox", qf, kf, vtf, tables["fox"], ycat, 2)
    sv["fox"] = (qf, kf, vf, ktf, oc, lse_c)
    ycat = _pool_fwd(proj, sm["w_pool_bd"], sm["pool_scale"], ycat)
    sv["ycat"] = ycat
    w.need("mid", ycat)
    h1 = _mm_nn(ycat, w["w_out"], "mm_out", res=h)
    sv["h1"] = h1
    memn = _rms_fwd(memv, sm["g_mem"], "rms_mem")
    qx, xq = _norm_matmul(h1, sm["g_xa"], w["w_xq"], (s, D_MODEL), grid=(s // tb, 1),
                          b_spec=pl.BlockSpec((D_MODEL, D_MODEL), lambda i, j: (0, 0)),
                          o_spec=pl.BlockSpec((tb, D_MODEL), lambda i, j: (i, 0)), name="norm_mm_xq",
                          out_dtype=BF16)
    kvm = _matmul(memn, w["w_xkv"], (N_DEV, MEM_LEN, XA_DIM), grid=(N_DEV, 1, 1),
                  a_spec=pl.BlockSpec((MEM_LEN, D_MODEL), lambda i, j, r: (0, 0)),
                  b_spec=pl.BlockSpec((None, D_MODEL, XA_DIM), lambda i, j, r: (i, 0, 0)),
                  o_spec=pl.BlockSpec((None, MEM_LEN, XA_DIM), lambda i, j, r: (i, 0, 0)),
                  dims=NN, nred=1, name="mm_xkv")
    ox = _xattn_fwd(qx, kvm)
    sv.update(xq=xq, memn=memn, qx=qx, kvm=kvm, ox=ox)
    h2 = _mm_nn(ox, w["w_xo"], "mm_xo", res=h1)
    sv["h2"] = h2
    w.need("ffn", ox)
    u0, xf = _norm_matmul(h2, sm["g_ffn"] + rest_arrived(w["w_down"]), w["w_up"], (N_DEV, s, FF_SHARD),
                          grid=(s // tb, N_DEV // 4),
                          b_spec=pl.BlockSpec((4, FF_SHARD, D_MODEL), lambda i, j: (j, 0, 0)),
                          o_spec=pl.BlockSpec((4, tb, FF_SHARD), lambda i, j: (j, i, 0)), name="norm_mm_up",
                          out_dtype=BF16, dims=NT)
    act = _ffn_gate_fwd(u0, sm["w_ffconv"])
    sv.update(xf=xf, u0=u0, act=act)
    ospec = pl.BlockSpec((tm, D_MODEL), lambda i, j, r: (i, 0))
    h3 = _matmul(act, w["w_down"], (s, D_MODEL), grid=(s // tm, 1, 1),
                 a_spec=pl.BlockSpec((FF_HALF, tm, FF_SHARD), lambda i, j, r: (0, i, 0)),
                 b_spec=pl.BlockSpec((FF_HALF, FF_SHARD, D_MODEL), lambda i, j, r: (0, 0, 0)),
                 o_spec=ospec, dims=NN, nred=1, slabs=FF_HALF, name="mm_down", res=h2, res_spec=ospec)
    return h3, sv


def _layer_bwd(dh3, memv, w, sm, tables, sv, rest_ready, in_ready):
    s = dh3.shape[0]
    tm, tb = min(ROW_TILE, s), min(MM_TILE, s)
    big, small = {}, {}
    ts = max(s // 2, 1)
    dact = _matmul(dh3, w["w_down"], (FF_HALF, s, FF_SHARD), grid=(s // tb, FF_HALF // 2, 1),
                   a_spec=pl.BlockSpec((tb, D_MODEL), lambda i, j, r: (i, 0)),
                   b_spec=pl.BlockSpec((2, FF_SHARD, D_MODEL), lambda i, j, r: (j, 0, 0)),
                   o_spec=pl.BlockSpec((2, tb, FF_SHARD), lambda i, j, r: (j, i, 0)),
                   dims=NT, nred=1, out_slabs=2, name="mm_dact", out_dtype=BF16)
    big["w_down"] = _matmul(sv["act"], dh3, (FF_HALF, FF_SHARD, D_MODEL), grid=(FF_HALF, 1, s // ts),
                            a_spec=pl.BlockSpec((None, ts, FF_SHARD), lambda i, j, r: (i, r, 0)),
                            b_spec=pl.BlockSpec((ts, D_MODEL), lambda i, j, r: (r, 0)),
                            o_spec=pl.BlockSpec((None, FF_SHARD, D_MODEL), lambda i, j, r: (i, 0, 0)),
                            dims=TN, nred=s // ts, name="mm_dw_down", out_dtype=GRAD_DTYPE)
    du0, small["w_ffconv"] = _ffn_gate_bwd(sv["u0"], sm["w_ffconv"], dact)
    dh2, small["g_ffn"] = _matmul_rms_bwd(du0, w["w_up"], sv["h2"], sm["g_ffn"], dh3, "mm_dxf_rms_bwd",
                                          tm=SLAB_TILE // 2, dims=NN)
    big["w_up"] = _matmul(du0, sv["xf"], (N_DEV, FF_SHARD, D_MODEL), grid=(N_DEV, 1, 1),
                          a_spec=pl.BlockSpec((None, s, FF_SHARD), lambda i, j, r: (i, 0, 0)),
                          b_spec=pl.BlockSpec((s, D_MODEL), lambda i, j, r: (0, 0)),
                          o_spec=pl.BlockSpec((None, FF_SHARD, D_MODEL), lambda i, j, r: (i, 0, 0)),
                          dims=TN, nred=1, name="mm_dw_up", out_dtype=GRAD_DTYPE)
    dox = _mm_nt(dh2, w["w_xo"], "mm_dox", out_dtype=BF16)
    big["w_xo"] = _mm_tn(sv["ox"], dh2, "mm_dw_xo")
    dqx, dkvm = _xattn_bwd(sv["qx"], sv["kvm"], dox)
    big["w_xq"] = _mm_tn(sv["xq"], dqx, "mm_dw_xq")
    big["w_xkv"] = _matmul(sv["memn"], dkvm, (N_DEV, D_MODEL, XA_DIM), grid=(N_DEV, 1, 1),
                           a_spec=pl.BlockSpec((MEM_LEN, D_MODEL), lambda i, j, r: (0, 0)),
                           b_spec=pl.BlockSpec((None, MEM_LEN, XA_DIM), lambda i, j, r: (i, 0, 0)),
                           o_spec=pl.BlockSpec((None, D_MODEL, XA_DIM), lambda i, j, r: (i, 0, 0)),
                           dims=TN, nred=1, name="mm_dw_xkv", out_dtype=GRAD_DTYPE)
    dmemn = _matmul(dkvm, w["w_xkv"], (MEM_LEN, D_MODEL), grid=(1, 1, 1),
                    a_spec=pl.BlockSpec((N_DEV, MEM_LEN, XA_DIM), lambda i, j, r: (0, 0, 0)),
                    b_spec=pl.BlockSpec((N_DEV, D_MODEL, XA_DIM), lambda i, j, r: (0, 0, 0)),
                    o_spec=pl.BlockSpec((MEM_LEN, D_MODEL), lambda i, j, r: (0, 0)),
                    dims=NT, nred=1, slabs=N_DEV, name="mm_dmemn")
    _, small["g_mem"] = _rms_bwd(dmemn, memv, sm["g_mem"], None, "rms_mem_bwd")
    dh1, small["g_xa"] = _matmul_rms_bwd(dqx, w["w_xq"], sv["h1"], sm["g_xa"], dh2, "mm_dxq_rms_bwd", tm=MM_TILE)
    big["w_out"] = _mm_tn(sv["ycat"], dh1, "mm_dw_out")
    dycat = _mm_nt(dh1, w["w_out"] + rest_ready(big, small).astype(BF16), "mm_dycat")
    proj = sv["proj"]
    dproj, small["w_sconv"] = _sconv_bwd(proj, sm["w_sconv"], dycat)
    qd, kd, vd, ktd, ob, lse_b = sv["dil"]
    delta, dob = _attention_delta(ob, dycat, 1)
    dqt, dk, dv = _attention_bwd("dil", qd, kd, vd, ktd, tables["dil"], dob, lse_b, delta)
    dproj = _heads_merge(dqt, dk, dv, tables["rope"], "merge_dil", dproj, 1)
    qf, kf, vf, ktf, oc, lse_c = sv["fox"]
    delta, dob = _attention_delta(oc, dycat, 2)
    dqt, dk, dv = _attention_bwd("fox", qf, kf, vf, ktf, tables["fox"], dob, lse_c, delta)
    dproj, dc = _heads_merge(dqt, dk, dv, None, "merge_fox", dproj, 2)
    dproj, dbias = _gate_cumsum_bwd(proj, sm["b_forget_pad"], dc, dproj)
    small["b_forget"] = dbias[0, :N_HEADS]
    dproj, dwbd, small["pool_scale"] = _pool_bwd(proj, sm["w_pool_bd"], sm["pool_scale"], dycat, dproj)
    small["w_pool"] = jnp.stack([dwbd[64 * g:64 * (g + 1), 64 * g:64 * (g + 1)] for g in range(4)])
    big["w_in"] = _mm_tn(sv["xn"], dproj, "mm_dw_in")
    dh0, small["g_mix"] = _matmul_rms_bwd(dproj, w["w_in"], sv["h0"], sm["g_mix"] + in_ready(big, small), dh1,
                                          "mm_dxn_rms_bwd")
    return dh0, big, small


SMALL_NAMES = ("g_mix", "b_forget", "w_pool", "pool_scale", "g_xa", "g_mem", "g_ffn", "w_sconv", "w_ffconv")
SMALL_WITH = {"rest": ("w_ffconv", "g_ffn", "g_mem", "g_xa"),
              "in": ("w_sconv", "b_forget", "pool_scale", "w_pool")}
SMALL_SHAPES = {"w_sconv": (3, GROUP), "w_ffconv": (N_DEV, 3, FF_SHARD)}
WEIGHT_NAMES = ("g_mix", "w_in", "b_forget", "w_sconv", "w_pool", "pool_scale", "w_out", "g_xa", "g_mem", "w_xq",
                "w_xkv", "w_xo", "g_ffn", "w_up", "w_ffconv", "w_down", "g_final")


def _block_diag(w_pool):
    z = jnp.zeros((64, 64), F32)
    return jnp.concatenate(
        [jnp.concatenate([w_pool[g] if c == g else z for c in range(4)], axis=1) for g in range(4)], axis=0)


def kernel(x, mem, positions, g_mix, w_in, b_forget, w_sconv, w_pool, pool_scale, w_out, g_xa, g_mem, w_xq, w_xkv, w_xo, g_ffn, w_up, w_ffconv, w_down, g_final, loss_target, m_g_mix, m_w_in, m_b_forget, m_w_sconv, m_w_pool, m_pool_scale, m_w_out, m_g_xa, m_g_mem, m_w_xq, m_w_xkv, m_w_xo, m_g_ffn, m_w_up, m_w_ffconv, m_w_down, m_g_final, v_g_mix, v_w_in, v_b_forget, v_w_sconv, v_w_pool, v_pool_scale, v_w_out, v_g_xa, v_g_mem, v_w_xq, v_w_xkv, v_w_xo, v_g_ffn, v_w_up, v_w_ffconv, v_w_down, v_g_final):
    weights = dict(g_mix=g_mix, w_in=w_in, b_forget=b_forget, w_sconv=w_sconv, w_pool=w_pool, pool_scale=pool_scale,
                   w_out=w_out, g_xa=g_xa, g_mem=g_mem, w_xq=w_xq, w_xkv=w_xkv, w_xo=w_xo, g_ffn=g_ffn, w_up=w_up,
                   w_ffconv=w_ffconv, w_down=w_down, g_final=g_final)
    m_in = dict(g_mix=m_g_mix, w_in=m_w_in, b_forget=m_b_forget, w_sconv=m_w_sconv, w_pool=m_w_pool,
                pool_scale=m_pool_scale, w_out=m_w_out, g_xa=m_g_xa, g_mem=m_g_mem, w_xq=m_w_xq, w_xkv=m_w_xkv,
                w_xo=m_w_xo, g_ffn=m_g_ffn, w_up=m_w_up, w_ffconv=m_w_ffconv, w_down=m_w_down, g_final=m_g_final)
    v_in = dict(g_mix=v_g_mix, w_in=v_w_in, b_forget=v_b_forget, w_sconv=v_w_sconv, w_pool=v_w_pool,
                pool_scale=v_pool_scale, w_out=v_w_out, g_xa=v_g_xa, g_mem=v_g_mem, w_xq=v_w_xq, w_xkv=v_w_xkv,
                w_xo=v_w_xo, g_ffn=v_g_ffn, w_up=v_w_up, w_ffconv=v_w_ffconv, w_down=v_w_down, g_final=v_g_final)
    depth = w_in.shape[0]
    me = 4 * lax.axis_index("x") + 2 * lax.axis_index("y") + lax.axis_index("c")
    h = x[0]
    memv = mem[0]
    s = h.shape[0]
    tq = min(ATT_TQ, s)
    tables = {"rope": _rope_tables(positions[0]), "dil": _bias_tables("dil", tq, tq),
              "fox": _bias_tables("fox", tq, tq)}

    w_in_r = _relay_in_cols(w_in)
    conv_shard = jnp.concatenate([w_sconv.reshape(-1), w_ffconv.reshape(-1)])
    conv_shard = jnp.concatenate([conv_shard, jnp.zeros((CONV_WORDS - conv_shard.shape[0],), F32)])
    shards = dict(w_in=w_in_r, w_out=w_out, w_xq=w_xq, w_xo=w_xo, w_xkv=w_xkv, w_up=w_up.transpose(0, 2, 1),
                  w_down=w_down)
    gathered = [None] * depth

    def start_gathers(l, after):
        states = {}
        order = jnp.zeros((), F32)
        for group in GATHER_GROUPS:
            first = GATHER_GROUPS[group][0]
            shards[first] = shards[first] + order
            xs = [_place_shard(shards[name], l, me, BF16, "place_%s_%d" % (name, l), after)
                  for name in GATHER_GROUPS[group]]
            if l == 0 and group == "in":
                xs.append(_place_shard(conv_shard.reshape(1, CONV_WORDS // 1024, 1024), 0, me, F32, "place_conv"))
            states[group], token = _exchange_start(xs, False, "gather_%s_start_%d" % (group, l))
            order = order + token[0, 0]
        gathered[l] = _GatheredWeights(states, l)
        return order

    order = start_gathers(0, None)
    gathered[0].need("in", tables["rope"][0])
    conv_all = gathered[0].extra[0].reshape(N_DEV, CONV_WORDS)
    n_sc = depth * 3 * (GROUP // N_DEV)
    sconv_full = conv_all[:, :n_sc].reshape(N_DEV, depth, 3, GROUP // N_DEV).transpose(1, 2, 0, 3).reshape(
        depth, 3, GROUP)
    ffconv_full = conv_all[:, n_sc:n_sc + depth * 3 * FF_SHARD].reshape(N_DEV, depth, 3, FF_SHARD).transpose(
        1, 0, 2, 3)

    smalls = []
    for l in range(depth):
        smalls.append(dict(
            g_mix=g_mix[l], g_xa=g_xa[l], g_mem=g_mem[l], g_ffn=g_ffn[l], pool_scale=pool_scale[l],
            w_pool_bd=_block_diag(w_pool[l]), w_sconv=sconv_full[l], w_ffconv=ffconv_full[l],
            b_forget_pad=jnp.concatenate([b_forget[l], jnp.zeros((128 - N_HEADS,), F32)]).reshape(1, 128)))
    smalls[0]["g_mix"] = smalls[0]["g_mix"] + order

    saved = []
    for l in range(depth):
        def rest_arrived(arrived, l=l):
            return start_gathers(l + 1, arrived) if l + 1 < depth else jnp.zeros((), F32)

        h, sv = _layer_fwd(h, memv, gathered[l], smalls[l], tables, rest_arrived)
        saved.append(sv)
    loss_part, dh, dg_final = _loss_head(h, g_final, loss_target[0])
    loss = lax.psum(loss_part[0, 0], MESH_AXES)

    small_grads = [None] * depth
    scatters = {}

    def pieces_of(big, group):
        return [big[name].reshape(PIECE_SHAPES[name]) for name in GROUPS[group]]

    def rider(grads):
        flat = jnp.concatenate([g.reshape(-1) for g in grads])
        rows = -(-flat.shape[0] // 1024)
        flat = jnp.concatenate([flat, jnp.zeros((rows * 1024 - flat.shape[0],), F32)])
        return jnp.broadcast_to(flat.reshape(1, rows, 1024), (N_DEV, rows, 1024))

    riding = {}
    done_small = {}

    def start_scatter(l, group, big, extra):
        names = [(n, l) for n in SMALL_WITH[group]] + extra
        riding[l, group] = names
        grads = [dg_final if n == "g_final" else done_small[ll][n] for n, ll in names]
        scatters[l, group], token = _exchange_start(pieces_of(big, group) + [rider(grads)], True,
                                                    "scatter_%s_start_%d" % (group, l))
        return token[0, 0]

    for l in reversed(range(depth)):
        def rest_ready(big, small, l=l):
            done_small[l] = small
            extra = ([("g_final", l)] if l == depth - 1 else []) + ([("g_mix", l + 1)] if l + 1 < depth else [])
            return start_scatter(l, "rest", big, extra)

        def in_ready(big, small, l=l):
            return start_scatter(l, "in", big, [])

        dh, _, small_grads[l] = _layer_bwd(dh, memv, gathered[l], smalls[l], tables, saved[l], rest_ready, in_ready)
    grad_x = dh[None]
    riding["tail"] = [("g_mix", 0)]
    scatters["tail"], _ = _exchange_start([rider([small_grads[0]["g_mix"]])], True, "scatter_tail_start")

    parts, owns, small_parts = {}, {}, {}

    def take_rider(key, got, given):
        flat = lax.dynamic_update_slice_in_dim(got, given[:1], me, axis=0).reshape(N_DEV, -1)
        off = 0
        for name, ll in riding[key]:
            shape = SMALL_SHAPES.get(name, weights[name].shape[-1:] if name == "g_final" else weights[name].shape[1:])
            n = 1
            for dim in shape:
                n *= dim
            small_parts.setdefault(name, [None] * depth)[ll] = flat[:, off:off + n].reshape((N_DEV,) + shape)
            off += n

    def wait_group(group, after):
        for l in reversed(range(depth)):
            got, given = _exchange_wait(scatters[l, group], after, "scatter_%s_wait_%d" % (group, l))
            for name, g, x in zip(GROUPS[group], got, given):
                parts.setdefault(name, [None] * depth)[l] = g
                owns.setdefault(name, [None] * depth)[l] = x
            take_rider((l, group), got[-1], given[-1])

    results = {}

    def update(name, w3, m3, v3):
        outs = _adamw(parts[name], owns.get(name), me, w3, m3, v3, "adamw_" + name)
        results[name] = [o.reshape(weights[name].shape) for o in outs]

    wait_group("rest", grad_x)
    for name in GROUPS["rest"]:
        if name == "w_up":
            outs = _adamw(parts[name], owns[name], me, w_up.transpose(0, 2, 1), m_w_up.transpose(0, 2, 1),
                          v_w_up.transpose(0, 2, 1), "adamw_w_up")
            results[name] = [o.transpose(0, 2, 1) for o in outs]
        else:
            update(name, weights[name], m_in[name], v_in[name])
    wait_group("in", results["w_down"][1])
    outs = _adamw(parts["w_in"], owns["w_in"], me, w_in_r, _relay_in_cols(m_w_in), _relay_in_cols(v_w_in),
                  "adamw_w_in")
    results["w_in"] = [_unrelay_in_cols(o) for o in outs]
    got, given = _exchange_wait(scatters["tail"], results["w_in"][1], "scatter_tail_wait")
    take_rider("tail", got[0], given[0])
    for name in SMALL_NAMES + ("g_final",):
        wv = weights[name]
        p = small_parts[name][depth - 1] if name == "g_final" else jnp.stack(small_parts[name], axis=1)
        if name == "w_sconv":
            p = lax.dynamic_slice_in_dim(p, me * (GROUP // N_DEV), GROUP // N_DEV, axis=3)
        elif name == "w_ffconv":
            p = lax.dynamic_index_in_dim(p, me, axis=2, keepdims=False)
        shape3 = (1, 1, wv.shape[0]) if wv.ndim == 1 else (1, -1, wv.shape[-1])
        w3 = wv.reshape(shape3)
        parts[name] = [p.reshape((N_DEV,) + w3.shape[1:])]
        update(name, w3, m_in[name].reshape(shape3), v_in[name].reshape(shape3))

    return (loss, grad_x, *[results[n][0] for n in WEIGHT_NAMES], *[results[n][1] for n in WEIGHT_NAMES],
            *[results[n][2] for n in WEIGHT_NAMES], *[results[n][3] for n in WEIGHT_NAMES])
```

```python
import functools

import jax
import jax.numpy as jnp
from jax import lax
from jax.experimental import pallas as pl
from jax.experimental.pallas import tpu as pltpu

F32 = jnp.float32
BF16 = jnp.bfloat16

N_DEV = 8
D_MODEL = 1024
GROUP = 256
HEAD_DIM = 64
N_HEADS = 4
N_IN = 2564
N_IN_PAD = 2688
COL_GATE = 2560
XA_HEADS = 4
XA_DIM = 256
MEM_LEN = 256
D_FF = 2816
FF_SHARD = 704
FF_HALF = 4
ROPE_THETA = 500000.0
ROPE_DIM = 16
RMS_EPS = 1e-6
NEG = -1e30
POOL_WINDOWS = (2, 4, 8, 16)
ADAM_LR, ADAM_B1, ADAM_B2, ADAM_EPS, ADAM_WD, ADAM_STEP = 0.001, 0.9, 0.999, 1e-08, 0.01, 10

ROW_TILE = 1024
MM_TILE = 1024
SLAB_TILE = 512
ATT_TQ = 512
BWD_HEADS = 4
VMEM_LIMIT = 56 * 1024 * 1024
ADAMW_BLOCK_BYTES = 8 * 1024 * 1024
PLACE_BLOCK_BYTES = 4 * 1024 * 1024
TN_BLOCK_BYTES = 12 * 1024 * 1024

MESH_AXES = ("x", "y", "c")


def _params(**kw):
    return pltpu.CompilerParams(vmem_limit_bytes=VMEM_LIMIT, **kw)


HBM_SPEC = pl.BlockSpec(memory_space=pltpu.HBM)
SEM_SPEC = pl.BlockSpec(memory_space=pltpu.SEMAPHORE)
DATAFLOW = pltpu.SideEffectType.DATAFLOW_SIDE_EFFECTING


def _peer_copies(x_ref, land_ref, send_sems, recv_sems, scatter):
    mx, my, mc = lax.axis_index("x"), lax.axis_index("y"), lax.axis_index("c")
    me = 4 * mx + 2 * my + mc
    pairs = []
    for k in range(1, N_DEV):
        kx, ky, kc = (k >> 2) & 1, (k >> 1) & 1, k & 1
        peer_lin = me ^ k
        send = pltpu.make_async_remote_copy(
            src_ref=x_ref.at[peer_lin] if scatter else land_ref.at[me], dst_ref=land_ref.at[me],
            send_sem=send_sems.at[k - 1], recv_sem=recv_sems.at[k - 1],
            device_id=(mx ^ kx, my ^ ky, mc ^ kc), device_id_type=pl.DeviceIdType.MESH)
        arrival = pltpu.make_async_remote_copy(
            src_ref=land_ref.at[peer_lin], dst_ref=land_ref.at[peer_lin],
            send_sem=send_sems.at[k - 1], recv_sem=recv_sems.at[k - 1],
            device_id=(mx, my, mc), device_id_type=pl.DeviceIdType.MESH)
        pairs.append((send, arrival))
    return pairs


def _exchange_start(xs, scatter, name):
    n = len(xs)
    ns = n if scatter else 0

    def body(*refs):
        srcs = refs[:ns] if scatter else (None,) * n
        lands, sends, recvs = refs[ns:ns + n], refs[ns + n:ns + 2 * n], refs[ns + 2 * n:ns + 3 * n]
        for t in range(n):
            for send, _ in _peer_copies(srcs[t], lands[t], sends[t], recvs[t], scatter):
                send.start()
        token = refs[-1]
        token[...] = jnp.zeros_like(token)

    sems = pltpu.SemaphoreType.DMA((N_DEV - 1,))
    operands = [pltpu.with_memory_space_constraint(x, pltpu.HBM) for x in xs]
    if scatter:
        operands += [pltpu.with_memory_space_constraint(lax.empty(x.shape, x.dtype), pltpu.HBM) for x in xs]
    outs = pl.pallas_call(
        body, name=name,
        out_shape=(sems,) * (2 * n) + tuple(pltpu.HBM(a.shape, a.dtype) for a in operands)
        + (jax.ShapeDtypeStruct((8, 128), F32),),
        in_specs=(HBM_SPEC,) * (ns + n),
        out_specs=(SEM_SPEC,) * (2 * n) + (HBM_SPEC,) * (ns + n) + (pl.BlockSpec(memory_space=pltpu.VMEM),),
        input_output_aliases={i: 2 * n + i for i in range(ns + n)},
        compiler_params=pltpu.CompilerParams(has_side_effects=DATAFLOW),
    )(*operands)
    return (outs[:-1], scatter), outs[-1]


def _exchange_wait(state, after, name):
    held, scatter = state
    n = len(held) // (4 if scatter else 3)
    ns = n if scatter else 0
    sems, thru = held[:2 * n], held[2 * n:]

    def body(*refs):
        srcs = refs[:ns] if scatter else (None,) * n
        lands, sends, recvs = refs[ns:ns + n], refs[ns + n:ns + 2 * n], refs[ns + 2 * n:ns + 3 * n]
        for t in range(n):
            for send, arrival in _peer_copies(srcs[t], lands[t], sends[t], recvs[t], scatter):
                send.wait_send()
                arrival.wait_recv()

    outs = pl.pallas_call(
        body, name=name,
        out_shape=tuple(pltpu.HBM(a.shape, a.dtype) for a in thru),
        in_specs=(HBM_SPEC,) * (ns + n) + (SEM_SPEC,) * (2 * n) + (pl.BlockSpec(memory_space=pl.ANY),),
        out_specs=(HBM_SPEC,) * (ns + n), input_output_aliases={i: i for i in range(ns + n)},
        compiler_params=pltpu.CompilerParams(has_side_effects=DATAFLOW),
    )(*thru, *sems, after)
    return list(outs[ns:]), list(outs[:ns])


def _place_shard(x, layer, me, dtype, name, after=None):
    _, r, c = x.shape
    tr = r
    if r * c * 4 > PLACE_BLOCK_BYTES:
        for cand in (512, 256, 128, 64, 32, 16):
            if r % cand == 0 and cand * c * 4 <= PLACE_BLOCK_BYTES:
                tr = cand
                break

    def body(me_ref, x_ref, *rest):
        o_ref = rest[-1]
        o_ref[...] = x_ref[...].astype(o_ref.dtype)

    return pl.pallas_call(
        body, name=name, out_shape=jax.ShapeDtypeStruct((N_DEV, r, c), dtype),
        grid_spec=pltpu.PrefetchScalarGridSpec(
            num_scalar_prefetch=1, grid=(r // tr,),
            in_specs=[pl.BlockSpec((None, tr, c), lambda i, me_ref: (layer, i, 0))]
            + ([ANY_SPEC] if after is not None else []),
            out_specs=pl.BlockSpec((None, tr, c), lambda i, me_ref: (me_ref[0], i, 0))),
        compiler_params=_params(),
    )(*((me.reshape(1), x) + ((after,) if after is not None else ())))


NN = ((1,), (0,))
NT = ((1,), (1,))
TN = ((0,), (0,))


def _matmul(a, b, out_shape, *, grid, a_spec, b_spec, o_spec, dims, nred, name, res=None, res_spec=None,
            out_dtype=F32, slabs=0, out_slabs=0):
    has_res = res is not None

    def body(*refs):
        a_ref, b_ref = refs[0], refs[1]
        r_ref = refs[2] if has_res else None
        o_ref = refs[3] if has_res else refs[2]
        if out_slabs:
            av = a_ref[...].astype(BF16)
            for n in range(out_slabs):
                o_ref[n] = lax.dot_general(av, b_ref[n].astype(BF16), (dims, ((), ())),
                                           preferred_element_type=F32).astype(o_ref.dtype)
            return
        if slabs:
            part = None
            for n in range(slabs):
                term = lax.dot_general(a_ref[n].astype(BF16), b_ref[n].astype(BF16), (dims, ((), ())),
                                       preferred_element_type=F32)
                part = term if part is None else part + term
        else:
            part = lax.dot_general(a_ref[...].astype(BF16), b_ref[...].astype(BF16), (dims, ((), ())),
                                   preferred_element_type=F32)
        if nred == 1:
            if has_res:
                part = part + r_ref[...]
            o_ref[...] = part.astype(o_ref.dtype)
        else:
            acc = refs[-1]
            r = pl.program_id(2)

            @pl.when(r == 0)
            def _():
                acc[...] = part

            @pl.when(r > 0)
            def _():
                acc[...] += part

            @pl.when(r == nred - 1)
            def _():
                tot = acc[...]
                if has_res:
                    tot = tot + r_ref[...]
                o_ref[...] = tot.astype(o_ref.dtype)

    in_specs = [a_spec, b_spec] + ([res_spec] if has_res else [])
    args = (a, b) + ((res,) if has_res else ())
    acc_shape = tuple(d for d in o_spec.block_shape if d is not None)
    return pl.pallas_call(
        body, name=name, grid=grid, out_shape=jax.ShapeDtypeStruct(out_shape, out_dtype),
        in_specs=in_specs, out_specs=o_spec,
        scratch_shapes=[pltpu.VMEM(acc_shape, F32)] if nred > 1 else [],
        compiler_params=_params(),
    )(*args)


def _mm_nn(a, w, name, res=None, tn=None, out_dtype=F32):
    m, k = a.shape
    n = w.shape[1]
    tn = tn or n
    tm = min(MM_TILE, m)
    ospec = pl.BlockSpec((tm, tn), lambda i, j, r: (i, j))
    return _matmul(a, w, (m, n), grid=(m // tm, n // tn, 1),
                   a_spec=pl.BlockSpec((tm, k), lambda i, j, r: (i, 0)),
                   b_spec=pl.BlockSpec((k, tn), lambda i, j, r: (0, j)),
                   o_spec=ospec, dims=NN, nred=1, name=name, res=res, res_spec=ospec if res is not None else None,
                   out_dtype=out_dtype)


def _mm_nt(a, w, name, out_dtype=F32):
    m, n = a.shape
    k = w.shape[0]
    tm = min(MM_TILE, m)
    return _matmul(a, w, (m, k), grid=(m // tm, 1, 1),
                   a_spec=pl.BlockSpec((tm, n), lambda i, j, r: (i, 0)),
                   b_spec=pl.BlockSpec((k, n), lambda i, j, r: (0, 0)),
                   o_spec=pl.BlockSpec((tm, k), lambda i, j, r: (i, 0)), dims=NT, nred=1, name=name,
                   out_dtype=out_dtype)


def _norm_matmul(h, g, b, out_shape, *, grid, b_spec, o_spec, name, out_dtype=F32, dims=NN):
    s, d = h.shape
    tm = s // grid[0]

    def body(h_ref, g_ref, b_ref, o_ref, xn_ref):
        @pl.when(pl.program_id(1) == 0)
        def _():
            hv = h_ref[...]
            r = lax.rsqrt(jnp.mean(hv * hv, axis=-1, keepdims=True) + RMS_EPS)
            xn_ref[...] = (hv * r * g_ref[...]).astype(xn_ref.dtype)

        if len(b_ref.shape) == 3:
            for n in range(b_ref.shape[0]):
                o_ref[n] = lax.dot_general(xn_ref[...], b_ref[n].astype(BF16), (dims, ((), ())),
                                           preferred_element_type=F32).astype(o_ref.dtype)
        else:
            o_ref[...] = lax.dot_general(xn_ref[...], b_ref[...].astype(BF16), (dims, ((), ())),
                                         preferred_element_type=F32).astype(o_ref.dtype)

    row = pl.BlockSpec((tm, d), lambda i, j: (i, 0))
    return pl.pallas_call(
        body, name=name, grid=grid,
        out_shape=(jax.ShapeDtypeStruct(out_shape, out_dtype), jax.ShapeDtypeStruct((s, d), BF16)),
        in_specs=[row, pl.BlockSpec((1, d), lambda i, j: (0, 0)), b_spec],
        out_specs=(o_spec, row), compiler_params=_params(),
    )(h, g.reshape(1, d), b)


def _matmul_rms_bwd(a, w, h, g, res, name, tm=SLAB_TILE, dims=NT):
    slabs = a.shape[0] if a.ndim == 3 else 0
    s, n = a.shape[-2:]
    d = w.shape[-2] if dims == NT else w.shape[-1]
    tm = min(tm, s)

    def body(a_ref, w_ref, h_ref, g_ref, r_ref, dh_ref, dg_ref):
        if slabs:
            dy = None
            for j in range(slabs):
                term = lax.dot_general(a_ref[j].astype(BF16), w_ref[j].astype(BF16), (dims, ((), ())),
                                       preferred_element_type=F32)
                dy = term if dy is None else dy + term
        else:
            dy = lax.dot_general(a_ref[...].astype(BF16), w_ref[...].astype(BF16), (dims, ((), ())),
                                 preferred_element_type=F32)
        hv = h_ref[...]
        r = lax.rsqrt(jnp.mean(hv * hv, axis=-1, keepdims=True) + RMS_EPS)
        hn = hv * r
        u = dy * g_ref[...]
        dh_ref[...] = r * (u - hn * jnp.mean(u * hn, axis=-1, keepdims=True)) + r_ref[...]
        part = jnp.sum(dy * hn, axis=0, keepdims=True)

        @pl.when(pl.program_id(0) == 0)
        def _():
            dg_ref[...] = part

        @pl.when(pl.program_id(0) > 0)
        def _():
            dg_ref[...] += part

    row = pl.BlockSpec((tm, d), lambda i: (i, 0))
    vec = pl.BlockSpec((1, d), lambda i: (0, 0))
    if slabs:
        a_spec = pl.BlockSpec((slabs, tm, n), lambda i: (0, i, 0))
        w_spec = pl.BlockSpec(w.shape, lambda i: (0, 0, 0))
    else:
        a_spec = pl.BlockSpec((tm, n), lambda i: (i, 0))
        w_spec = pl.BlockSpec(w.shape, lambda i: (0, 0))
    dh, dg = pl.pallas_call(
        body, name=name, grid=(s // tm,),
        out_shape=(jax.ShapeDtypeStruct((s, d), F32), jax.ShapeDtypeStruct((1, d), F32)),
        in_specs=[a_spec, w_spec, row, vec, row], out_specs=(row, vec), compiler_params=_params(),
    )(a, w, h, g.reshape(1, d), res)
    return dh, dg.reshape(d)


GRAD_DTYPE = BF16


def _mm_tn(a, b, name, tk=512, tn=None):
    s, k = a.shape
    n = b.shape[1]
    tn = tn or n
    tk = min(tk, k)
    ts = s if b.dtype == BF16 and tn * s * 2 <= TN_BLOCK_BYTES else max(s // 2, 1)
    return _matmul(a, b, (k, n), grid=(k // tk, n // tn, s // ts),
                   a_spec=pl.BlockSpec((ts, tk), lambda i, j, r: (r, i)),
                   b_spec=pl.BlockSpec((ts, tn), lambda i, j, r: (r, j)),
                   o_spec=pl.BlockSpec((tk, tn), lambda i, j, r: (i, j)), dims=TN, nred=s // ts, name=name,
                   out_dtype=GRAD_DTYPE)


def _rms_fwd(h, g, name):
    s, d = h.shape
    tm = min(ROW_TILE, s)

    def body(h_ref, g_ref, o_ref):
        hv = h_ref[...]
        r = lax.rsqrt(jnp.mean(hv * hv, axis=-1, keepdims=True) + RMS_EPS)
        o_ref[...] = (hv * r * g_ref[...]).astype(o_ref.dtype)

    return pl.pallas_call(
        body, name=name, grid=(s // tm,), out_shape=jax.ShapeDtypeStruct((s, d), BF16),
        in_specs=[pl.BlockSpec((tm, d), lambda i: (i, 0)), pl.BlockSpec((1, d), lambda i: (0, 0))],
        out_specs=pl.BlockSpec((tm, d), lambda i: (i, 0)), compiler_params=_params(),
    )(h, g.reshape(1, d))


def _rms_bwd(dy, h, g, res, name):
    s, d = h.shape
    tm = min(ROW_TILE, s)
    has_res = res is not None

    def body(*refs):
        dy_ref, h_ref, g_ref = refs[:3]
        r_ref = refs[3] if has_res else None
        dh_ref, dg_ref = refs[-2], refs[-1]
        hv = h_ref[...]
        r = lax.rsqrt(jnp.mean(hv * hv, axis=-1, keepdims=True) + RMS_EPS)
        hn = hv * r
        dyv = dy_ref[...].astype(F32)
        u = dyv * g_ref[...]
        dh = r * (u - hn * jnp.mean(u * hn, axis=-1, keepdims=True))
        if has_res:
            dh = dh + r_ref[...]
        dh_ref[...] = dh
        part = jnp.sum(dyv * hn, axis=0, keepdims=True)

        @pl.when(pl.program_id(0) == 0)
        def _():
            dg_ref[...] = part

        @pl.when(pl.program_id(0) > 0)
        def _():
            dg_ref[...] += part

    row = pl.BlockSpec((tm, d), lambda i: (i, 0))
    vec = pl.BlockSpec((1, d), lambda i: (0, 0))
    dh, dg = pl.pallas_call(
        body, name=name, grid=(s // tm,),
        out_shape=(jax.ShapeDtypeStruct((s, d), F32), jax.ShapeDtypeStruct((1, d), F32)),
        in_specs=[row, row, vec] + ([row] if has_res else []),
        out_specs=(row, vec), compiler_params=_params(),
    )(*((dy, h, g.reshape(1, d)) + ((res,) if has_res else ())))
    return dh, dg.reshape(d)


def _loss_head(h, g, target):
    s, d = h.shape
    tm = min(ROW_TILE, s)

    def body(h_ref, g_ref, t_ref, loss_ref, dh_ref, dg_ref):
        hv = h_ref[...]
        r = lax.rsqrt(jnp.mean(hv * hv, axis=-1, keepdims=True) + RMS_EPS)
        hn = hv * r
        gv = g_ref[...]
        err = hn * gv - t_ref[...]
        rows = jnp.mean(err * err, axis=-1, keepdims=True)
        lpart = 0.5 * jnp.sum(rows, axis=0, keepdims=True) + jnp.zeros((1, 128), F32)
        dy = err * (1.0 / d)
        u = dy * gv
        dh_ref[...] = r * (u - hn * jnp.mean(u * hn, axis=-1, keepdims=True))
        gpart = jnp.sum(dy * hn, axis=0, keepdims=True)

        @pl.when(pl.program_id(0) == 0)
        def _():
            dg_ref[...] = gpart
            loss_ref[...] = lpart

        @pl.when(pl.program_id(0) > 0)
        def _():
            dg_ref[...] += gpart
            loss_ref[...] += lpart

    row = pl.BlockSpec((tm, d), lambda i: (i, 0))
    vec = pl.BlockSpec((1, d), lambda i: (0, 0))
    return pl.pallas_call(
        body, name="loss_head", grid=(s // tm,),
        out_shape=(jax.ShapeDtypeStruct((1, 128), F32), jax.ShapeDtypeStruct((s, d), F32),
                   jax.ShapeDtypeStruct((1, d), F32)),
        in_specs=[row, vec, row],
        out_specs=(pl.BlockSpec((1, 128), lambda i: (0, 0)), row, vec), compiler_params=_params(),
    )(h, g.reshape(1, d), target)


def _shift_down(x, k):
    return pltpu.roll(x, k, 0)


def _shift_up(x, k):
    return pltpu.roll(x, x.shape[0] - k, 0)


def _conv3(x, w):
    return w[2:3, :] * x + w[1:2, :] * _shift_down(x, 1) + w[0:1, :] * _shift_down(x, 2)


def _conv3_t(x, w):
    return w[2:3, :] * x + w[1:2, :] * _shift_up(x, 1) + w[0:1, :] * _shift_up(x, 2)


def _sigmoid(x):
    return 1.0 / (1.0 + jnp.exp(-x))


def _prev_map(tile, halo, col):
    return lambda i: (jnp.maximum(i * (tile // halo) - 1, 0), col)


def _next_map(tile, halo, col, nrows):
    return lambda i: (jnp.minimum((i + 1) * (tile // halo), nrows // halo - 1), col)


def _sconv_fwd(proj, w):
    s = proj.shape[0]
    t = min(ROW_TILE, s)

    def body(cur_ref, prev_ref, w_ref, o_ref):
        i = pl.program_id(0)
        prev = prev_ref[...] * (i > 0).astype(F32)
        ext = jnp.concatenate([prev, cur_ref[...]], axis=0)
        sv = ext[:, 2 * GROUP:3 * GROUP] * ext[:, 0:GROUP]
        y = ext[:, GROUP:2 * GROUP] * _conv3(sv, w_ref[...])
        o_ref[...] = y[8:].astype(o_ref.dtype)

    return pl.pallas_call(
        body, name="sconv_fwd", grid=(s // t,), out_shape=jax.ShapeDtypeStruct((s, 4 * GROUP), BF16),
        in_specs=[pl.BlockSpec((t, 3 * GROUP), lambda i: (i, 0)),
                  pl.BlockSpec((8, 3 * GROUP), _prev_map(t, 8, 0)),
                  pl.BlockSpec((3, GROUP), lambda i: (0, 0))],
        out_specs=pl.BlockSpec((t, GROUP), lambda i: (i, 0)), compiler_params=_params(),
    )(proj, proj, w)


def _sconv_bwd(proj, w, dy):
    s = proj.shape[0]
    t = min(ROW_TILE, s)
    nt = s // t

    def body(cur_ref, prev_ref, next_ref, w_ref, dy_ref, dyn_ref, dp_ref, dw_ref):
        i = pl.program_id(0)
        first = (i > 0).astype(F32)
        last = (i < nt - 1).astype(F32)
        ext = jnp.concatenate([prev_ref[...] * first, cur_ref[...], next_ref[...] * last], axis=0)
        dye = jnp.concatenate([jnp.zeros((8, GROUP), F32), dy_ref[...], dyn_ref[...] * last], axis=0)
        hv, bv, cv = ext[:, 0:GROUP], ext[:, GROUP:2 * GROUP], ext[:, 2 * GROUP:3 * GROUP]
        wv = w_ref[...]
        sv = cv * hv
        conv = _conv3(sv, wv)
        dconv = dye * bv
        ds = _conv3_t(dconv, wv)
        dp = jnp.concatenate([ds * cv, dye * conv, ds * hv], axis=1)
        dp_ref[...] = dp[8:8 + t].astype(dp_ref.dtype)
        dc = dconv[8:8 + t]
        dw = jnp.concatenate([
            jnp.sum(dc * _shift_down(sv, 2)[8:8 + t], axis=0, keepdims=True),
            jnp.sum(dc * _shift_down(sv, 1)[8:8 + t], axis=0, keepdims=True),
            jnp.sum(dc * sv[8:8 + t], axis=0, keepdims=True),
            jnp.zeros((5, GROUP), F32)], axis=0)

        @pl.when(i == 0)
        def _():
            dw_ref[...] = dw

        @pl.when(i > 0)
        def _():
            dw_ref[...] += dw

    dp, dw = pl.pallas_call(
        body, name="sconv_bwd", grid=(nt,),
        out_shape=(jax.ShapeDtypeStruct((s, N_IN_PAD), BF16), jax.ShapeDtypeStruct((8, GROUP), F32)),
        in_specs=[pl.BlockSpec((t, 3 * GROUP), lambda i: (i, 0)),
                  pl.BlockSpec((8, 3 * GROUP), _prev_map(t, 8, 0)),
                  pl.BlockSpec((8, 3 * GROUP), _next_map(t, 8, 0, s)),
                  pl.BlockSpec((3, GROUP), lambda i: (0, 0)),
                  pl.BlockSpec((t, GROUP), lambda i: (i, 0)),
                  pl.BlockSpec((8, GROUP), _next_map(t, 8, 0, s))],
        out_specs=(pl.BlockSpec((t, 3 * GROUP), lambda i: (i, 0)), pl.BlockSpec((8, GROUP), lambda i: (0, 0))),
        compiler_params=_params(),
    )(proj, proj, proj, w, dy, dy)
    return dp, dw[:3]


def _lane_window(shape):
    lane = lax.broadcasted_iota(jnp.int32, shape, 1)
    return lane, jnp.where(lane < 64, 2.0, jnp.where(lane < 128, 4.0, jnp.where(lane < 192, 8.0, 16.0)))


def _by_group(lane, s1, s2, s3, s4):
    return jnp.where(lane < 64, s1, jnp.where(lane < 128, s2, jnp.where(lane < 192, s3, s4)))


def _pool_z(ext, row0):
    s1 = ext + _shift_down(ext, 1)
    s2 = s1 + _shift_down(s1, 2)
    s3 = s2 + _shift_down(s2, 4)
    s4 = s3 + _shift_down(s3, 8)
    lane, win = _lane_window(ext.shape)
    tpos = (lax.broadcasted_iota(jnp.int32, ext.shape, 0) + (row0 - 16 + 1)).astype(F32)
    cnt = jnp.maximum(jnp.minimum(tpos, win), 1.0)
    return _by_group(lane, s1, s2, s3, s4) / cnt - ext


ANY_SPEC = pl.BlockSpec(memory_space=pl.ANY)


def _pool_fwd(proj, wbd, scale, ybuf):
    s = proj.shape[0]
    t = min(ROW_TILE, s)
    col = (COL_GATE - GROUP) // GROUP

    def body(cur_ref, prev_ref, w_ref, sc_ref, buf_ref, o_ref):
        i = pl.program_id(0)
        ext = jnp.concatenate([prev_ref[...] * (i > 0).astype(F32), cur_ref[...]], axis=0)
        z = _pool_z(ext, i * t)[16:]
        y = jnp.dot(z.astype(BF16), w_ref[...].astype(BF16), preferred_element_type=F32)
        o_ref[...] = (y * sc_ref[...]).astype(o_ref.dtype)

    return pl.pallas_call(
        body, name="pool_fwd", grid=(s // t,), out_shape=jax.ShapeDtypeStruct(ybuf.shape, ybuf.dtype),
        in_specs=[pl.BlockSpec((t, GROUP), lambda i: (i, col)),
                  pl.BlockSpec((16, GROUP), _prev_map(t, 16, col)),
                  pl.BlockSpec((GROUP, GROUP), lambda i: (0, 0)),
                  pl.BlockSpec((1, GROUP), lambda i: (0, 0)), ANY_SPEC],
        out_specs=pl.BlockSpec((t, GROUP), lambda i: (i, 3)), input_output_aliases={4: 0},
        compiler_params=_params(),
    )(proj, proj, wbd, scale.reshape(1, GROUP), ybuf)


def _pool_bwd(proj, wbd, scale, dy, dbuf):
    s = proj.shape[0]
    t = min(ROW_TILE, s)
    nt = s // t
    col = (COL_GATE - GROUP) // GROUP

    def body(cur_ref, prev_ref, w_ref, sc_ref, dy_ref, dyn_ref, buf_ref, dp_ref, dw_ref, dsc_ref):
        i = pl.program_id(0)
        ext = jnp.concatenate([prev_ref[...] * (i > 0).astype(F32), cur_ref[...]], axis=0)
        z = _pool_z(ext, i * t)[16:]
        wv = w_ref[...].astype(BF16)
        dyc = dy_ref[...]
        dye = jnp.concatenate([dyc, dyn_ref[...] * (i < nt - 1).astype(F32)], axis=0) * sc_ref[...]
        dz = lax.dot_general(dye.astype(BF16), wv, (NT, ((), ())), preferred_element_type=F32)
        lane, win = _lane_window(dz.shape)
        tpos = (lax.broadcasted_iota(jnp.int32, dz.shape, 0) + (i * t + 1)).astype(F32)
        e = dz / jnp.minimum(tpos, win)
        f1 = e + _shift_up(e, 1)
        f2 = f1 + _shift_up(f1, 2)
        f3 = f2 + _shift_up(f2, 4)
        f4 = f3 + _shift_up(f3, 8)
        dp = _by_group(lane, f1, f2, f3, f4) - dz
        dp_ref[...] = dp[:t].astype(dp_ref.dtype)
        zb = z.astype(BF16)
        y = jnp.dot(zb, wv, preferred_element_type=F32)
        dsc = jnp.sum(dyc * y, axis=0, keepdims=True)
        dw = lax.dot_general(zb, dye[:t].astype(BF16), (TN, ((), ())), preferred_element_type=F32)

        @pl.when(i == 0)
        def _():
            dw_ref[...] = dw
            dsc_ref[...] = dsc

        @pl.when(i > 0)
        def _():
            dw_ref[...] += dw
            dsc_ref[...] += dsc

    dp, dw, dsc = pl.pallas_call(
        body, name="pool_bwd", grid=(nt,),
        out_shape=(jax.ShapeDtypeStruct(dbuf.shape, dbuf.dtype), jax.ShapeDtypeStruct((GROUP, GROUP), F32),
                   jax.ShapeDtypeStruct((1, GROUP), F32)),
        in_specs=[pl.BlockSpec((t, GROUP), lambda i: (i, col)),
                  pl.BlockSpec((16, GROUP), _prev_map(t, 16, col)),
                  pl.BlockSpec((GROUP, GROUP), lambda i: (0, 0)),
                  pl.BlockSpec((1, GROUP), lambda i: (0, 0)),
                  pl.BlockSpec((t, GROUP), lambda i: (i, 3)),
                  pl.BlockSpec((16, GROUP), _next_map(t, 16, 3, s)), ANY_SPEC],
        out_specs=(pl.BlockSpec((t, GROUP), lambda i: (i, col)), pl.BlockSpec((GROUP, GROUP), lambda i: (0, 0)),
                   pl.BlockSpec((1, GROUP), lambda i: (0, 0))),
        input_output_aliases={6: 0}, compiler_params=_params(),
    )(proj, proj, wbd, scale.reshape(1, GROUP), dy, dy, dbuf)
    return dp, dw, dsc.reshape(GROUP)


FF_HALO = 16


def _ffn_gate_fwd(u0, w):
    s = u0.shape[1]
    t = min(ROW_TILE, s)

    def body(a_ref, ap_ref, g_ref, gp_ref, wa_ref, wg_ref, o_ref):
        first = (pl.program_id(1) > 0).astype(F32)
        a = _conv3(jnp.concatenate([ap_ref[...] * first, a_ref[...].astype(F32)], axis=0), wa_ref[...])[FF_HALO:]
        g = _conv3(jnp.concatenate([gp_ref[...] * first, g_ref[...].astype(F32)], axis=0), wg_ref[...])[FF_HALO:]
        o_ref[...] = (a * (g * _sigmoid(g))).astype(o_ref.dtype)

    def cur(off):
        return pl.BlockSpec((None, t, FF_SHARD), lambda j, i: (j + off, i, 0))

    def prev(off):
        return pl.BlockSpec((None, FF_HALO, FF_SHARD),
                            lambda j, i: (j + off, jnp.maximum(i * (t // FF_HALO) - 1, 0), 0))

    def wspec(off):
        return pl.BlockSpec((None, 3, FF_SHARD), lambda j, i: (j + off, 0, 0))

    return pl.pallas_call(
        body, name="ffn_gate_fwd", grid=(FF_HALF, s // t),
        out_shape=jax.ShapeDtypeStruct((FF_HALF, s, FF_SHARD), BF16),
        in_specs=[cur(0), prev(0), cur(FF_HALF), prev(FF_HALF), wspec(0), wspec(FF_HALF)],
        out_specs=pl.BlockSpec((None, t, FF_SHARD), lambda j, i: (j, i, 0)), compiler_params=_params(),
    )(u0, u0, u0, u0, w, w)


def _ffn_gate_bwd(u0, w, dact):
    s = u0.shape[1]
    t = min(ROW_TILE, s)
    nt = s // t

    def body(c_ref, p_ref, n_ref, w_ref, d_ref, dn_ref, du_ref, dw_ref):
        i = pl.program_id(1)
        first = (i > 0).astype(F32)
        last = (i < nt - 1).astype(F32)
        dext = jnp.concatenate([jnp.zeros((FF_HALO, FF_SHARD), F32), d_ref[...].astype(F32), dn_ref[...] * last],
                               axis=0)
        ext = [jnp.concatenate([p_ref[n] * first, c_ref[n].astype(F32), n_ref[n] * last], axis=0) for n in range(2)]
        a = _conv3(ext[0], w_ref[0])
        g = _conv3(ext[1], w_ref[1])
        sg = _sigmoid(g)
        silu = g * sg
        dus = (dext * silu, dext * a * (sg + silu * (1.0 - sg)))
        mine = slice(FF_HALO, FF_HALO + t)
        for n in range(2):
            du_ref[n] = _conv3_t(dus[n], w_ref[n])[mine].astype(du_ref.dtype)
            dc = dus[n][mine]
            dw = jnp.concatenate([
                jnp.sum(dc * _shift_down(ext[n], 2)[mine], axis=0, keepdims=True),
                jnp.sum(dc * _shift_down(ext[n], 1)[mine], axis=0, keepdims=True),
                jnp.sum(dc * ext[n][mine], axis=0, keepdims=True),
                jnp.zeros((5, FF_SHARD), F32)], axis=0)

            @pl.when(i == 0)
            def _(n=n, dw=dw):
                dw_ref[n] = dw

            @pl.when(i > 0)
            def _(n=n, dw=dw):
                dw_ref[n] += dw

    def pair(rows, row_map):
        return pl.BlockSpec((2, None, rows, FF_SHARD), lambda j, i: (0, j, row_map(i), 0))

    prev_row = lambda i: jnp.maximum(i * (t // FF_HALO) - 1, 0)
    next_row = lambda i: jnp.minimum((i + 1) * (t // FF_HALO), s // FF_HALO - 1)
    u2 = u0.reshape(2, FF_HALF, s, FF_SHARD)
    du, dw = pl.pallas_call(
        body, name="ffn_gate_bwd", grid=(FF_HALF, nt),
        out_shape=(jax.ShapeDtypeStruct((2, FF_HALF, s, FF_SHARD), BF16),
                   jax.ShapeDtypeStruct((2, FF_HALF, 8, FF_SHARD), F32)),
        in_specs=[pair(t, lambda i: i), pair(FF_HALO, prev_row), pair(FF_HALO, next_row), pair(3, lambda i: 0),
                  pl.BlockSpec((None, t, FF_SHARD), lambda j, i: (j, i, 0)),
                  pl.BlockSpec((None, FF_HALO, FF_SHARD), lambda j, i: (j, next_row(i), 0))],
        out_specs=(pair(t, lambda i: i), pair(8, lambda i: 0)),
        compiler_params=_params(),
    )(u2, u2, u2, w.reshape(2, FF_HALF, 3, FF_SHARD), dact, dact)
    return du.reshape(2 * FF_HALF, s, FF_SHARD), dw.reshape(2 * FF_HALF, 8, FF_SHARD)[:, :3]


def _rope_tables(positions):
    inv_freq = ROPE_THETA ** (-jnp.arange(0, ROPE_DIM, 2, dtype=F32) / ROPE_DIM)
    ang = positions.astype(F32)[:, None] * inv_freq
    cos, sin = jnp.cos(ang), jnp.sin(ang)
    s = positions.shape[0]
    half = ROPE_DIM // 2
    rest = HEAD_DIM - ROPE_DIM
    ca = jnp.concatenate([cos, cos, jnp.ones((s, rest), F32)], axis=1)
    cb = jnp.concatenate([-sin, jnp.zeros((s, HEAD_DIM - half), F32)], axis=1)
    cc = jnp.concatenate([jnp.zeros((s, half), F32), sin, jnp.zeros((s, rest), F32)], axis=1)
    return tuple(jnp.tile(tb, (1, N_HEADS)) for tb in (ca, cb, cc))


QK_WIDE = 128
LANE_CQ, LANE_CK = 64, 67
KT_ROWS = 80


def _three_bf16(x):
    hi = x.astype(BF16).astype(F32)
    mid = (x - hi).astype(BF16).astype(F32)
    lo = (x - hi - mid).astype(BF16).astype(F32)
    return hi, mid, lo


def _heads_split(proj, col, tables, c, name):
    s = proj.shape[0]
    t = min(ROW_TILE, s)
    rope = tables is not None
    wide = c is not None
    width = QK_WIDE if wide else HEAD_DIM

    def body(*refs):
        x_ref = refs[0]
        q_ref, k_ref, v_ref, kt_ref, vt_ref = refs[-5:]
        xv = x_ref[...]
        parts = [xv[:, 0:GROUP], xv[:, GROUP:2 * GROUP], xv[:, 2 * GROUP:3 * GROUP]]
        if rope:
            ca, cb, cc = refs[1][...], refs[2][...], refs[3][...]
            for n in range(2):
                p = parts[n]
                parts[n] = p * ca + pltpu.roll(p, GROUP - 8, 1) * cb + pltpu.roll(p, 8, 1) * cc
        parts[0] = parts[0] * (HEAD_DIM ** -0.5)
        k_t, v_t = parts[1].T, parts[2].T
        ones_row = jnp.where(lax.broadcasted_iota(jnp.int32, (KT_ROWS - HEAD_DIM, t), 0) == 0, 1.0, 0.0)
        lane = lax.broadcasted_iota(jnp.int32, (t, QK_WIDE), 1)
        zeros = jnp.zeros((t, QK_WIDE - HEAD_DIM), F32)
        for h in range(N_HEADS):
            hs = slice(h * HEAD_DIM, (h + 1) * HEAD_DIM)
            qh, kh = parts[0][:, hs], parts[1][:, hs]
            if wide:
                terms = _three_bf16(refs[-6][:, h:h + 1])
                qh = jnp.concatenate([qh, zeros], axis=1)
                kh = jnp.concatenate([kh, zeros], axis=1)
                for n in range(3):
                    qh = jnp.where(lane == LANE_CQ + n, terms[n], jnp.where(lane == LANE_CK + n, 1.0, qh))
                    kh = jnp.where(lane == LANE_CK + n, -terms[n], jnp.where(lane == LANE_CQ + n, 1.0, kh))
            q_ref[h] = qh.astype(q_ref.dtype)
            k_ref[h] = kh.astype(k_ref.dtype)
            v_ref[h] = parts[2][:, hs].astype(v_ref.dtype)
            kt_ref[h] = jnp.concatenate([k_t[hs, :], ones_row], axis=0).astype(kt_ref.dtype)
            vt_ref[h] = v_t[hs, :].astype(vt_ref.dtype)

    tab = pl.BlockSpec((t, GROUP), lambda i: (i, 0))
    qk = pl.BlockSpec((N_HEADS, t, width), lambda i: (0, i, 0))
    heads = pl.BlockSpec((N_HEADS, t, HEAD_DIM), lambda i: (0, i, 0))
    heads_t = pl.BlockSpec((N_HEADS, HEAD_DIM, t), lambda i: (0, 0, i))
    qk_shape = jax.ShapeDtypeStruct((N_HEADS, s, width), BF16)
    return pl.pallas_call(
        body, name=name, grid=(s // t,),
        out_shape=(qk_shape, qk_shape, jax.ShapeDtypeStruct((N_HEADS, s, HEAD_DIM), BF16),
                   jax.ShapeDtypeStruct((N_HEADS, KT_ROWS, s), BF16),
                   jax.ShapeDtypeStruct((N_HEADS, HEAD_DIM, s), BF16)),
        in_specs=[pl.BlockSpec((t, 3 * GROUP), lambda i: (i, col))] + ([tab, tab, tab] if rope else [])
        + ([pl.BlockSpec((t, 128), lambda i: (i, 0))] if wide else []),
        out_specs=(qk, qk, heads, pl.BlockSpec((N_HEADS, KT_ROWS, t), lambda i: (0, 0, i)), heads_t),
        compiler_params=_params(),
    )(*((proj,) + (tuple(tables) if rope else ()) + ((c,) if wide else ())))


def _heads_merge(dqt, dk, dv, tables, name, dbuf, col):
    s = dv.shape[1]
    t = min(ROW_TILE, s)
    rope = tables is not None

    wide = dk.shape[2] == QK_WIDE

    def body(*refs):
        o_ref = refs[n_in + 1]
        dq = jnp.concatenate([refs[0][h, :HEAD_DIM, :] for h in range(N_HEADS)], axis=0).T
        parts = [dq] + [jnp.concatenate([r[h][:, :HEAD_DIM] for h in range(N_HEADS)], axis=1) for r in refs[1:3]]
        parts[0] = parts[0] * (HEAD_DIM ** -0.5)
        if rope:
            ca, cb, cc = refs[3][...], refs[4][...], refs[5][...]
            for n in range(2):
                p = parts[n]
                parts[n] = p * ca + pltpu.roll(p * cb, 8, 1) + pltpu.roll(p * cc, GROUP - 8, 1)
        o_ref[...] = jnp.concatenate(parts, axis=1).astype(o_ref.dtype)
        if wide:
            over_keys = jnp.concatenate([refs[0][h, HEAD_DIM:HEAD_DIM + 8, :] for h in range(N_HEADS)]
                                        + [jnp.zeros((128 - 8 * N_HEADS, t), F32)], axis=0).T
            lane = lax.broadcasted_iota(jnp.int32, (t, 128), 1)
            dc = jnp.zeros((t, 128), F32)
            for h in range(N_HEADS):
                dc = jnp.where(lane == h, over_keys[:, 8 * h:8 * h + 1] - refs[1][h][:, LANE_CK:LANE_CK + 1], dc)
            refs[n_in + 2][...] = dc

    tab = pl.BlockSpec((t, GROUP), lambda i: (i, 0))
    heads = pl.BlockSpec((N_HEADS, t, HEAD_DIM), lambda i: (0, i, 0))
    n_in = 6 if rope else 3
    dspec = pl.BlockSpec((t, 3 * GROUP), lambda i: (i, col))
    dshape = jax.ShapeDtypeStruct(dbuf.shape, dbuf.dtype)
    return pl.pallas_call(
        body, name=name, grid=(s // t,),
        out_shape=(dshape, jax.ShapeDtypeStruct((s, 128), F32)) if wide else dshape,
        in_specs=[pl.BlockSpec((N_HEADS, KT_ROWS, t), lambda i: (0, 0, i)),
                  pl.BlockSpec((N_HEADS, t, dk.shape[2]), lambda i: (0, i, 0)), heads]
        + ([tab, tab, tab] if rope else []) + [ANY_SPEC],
        out_specs=(dspec, pl.BlockSpec((t, 128), lambda i: (i, 0))) if wide else dspec,
        input_output_aliases={n_in: 0}, compiler_params=_params(),
    )(*((dqt, dk, dv) + (tuple(tables) if rope else ()) + (dbuf,)))


def _log_sigmoid(x):
    return jnp.minimum(x, 0.0) - jnp.log(1.0 + jnp.exp(-jnp.abs(x)))


def _scan_rows(x, reverse):
    n = x.shape[0]
    row = lax.broadcasted_iota(jnp.int32, x.shape, 0)
    k = 1
    while k < n:
        if reverse:
            x = x + jnp.where(row < n - k, _shift_up(x, k), 0.0)
        else:
            x = x + jnp.where(row >= k, _shift_down(x, k), 0.0)
        k *= 2
    return x


def _gate_cumsum(proj, bias):
    s = proj.shape[0]
    col = COL_GATE // 128

    def body(z_ref, b_ref, c_ref):
        c_ref[...] = _scan_rows(_log_sigmoid(z_ref[...] + b_ref[...]), False)

    return pl.pallas_call(
        body, name="gate_cumsum", grid=(1,), out_shape=jax.ShapeDtypeStruct((s, 128), F32),
        in_specs=[pl.BlockSpec((s, 128), lambda i: (0, col)), pl.BlockSpec((1, 128), lambda i: (0, 0))],
        out_specs=pl.BlockSpec((s, 128), lambda i: (0, 0)), compiler_params=_params(),
    )(proj, bias)


def _gate_cumsum_bwd(proj, bias, dc, dbuf):
    s = proj.shape[0]
    col = COL_GATE // 128

    def body(z_ref, b_ref, dc_ref, buf_ref, dz_ref, db_ref):
        dlogf = _scan_rows(dc_ref[...], True)
        dz = dlogf * _sigmoid(-(z_ref[...] + b_ref[...]))
        dz_ref[...] = dz.astype(dz_ref.dtype)
        db_ref[...] = jnp.sum(dz, axis=0, keepdims=True)

    return pl.pallas_call(
        body, name="gate_cumsum_bwd", grid=(1,),
        out_shape=(jax.ShapeDtypeStruct(dbuf.shape, dbuf.dtype), jax.ShapeDtypeStruct((1, 128), F32)),
        in_specs=[pl.BlockSpec((s, 128), lambda i: (0, col)), pl.BlockSpec((1, 128), lambda i: (0, 0)),
                  pl.BlockSpec((s, 128), lambda i: (0, 0)), ANY_SPEC],
        out_specs=(pl.BlockSpec((s, 128), lambda i: (0, col)), pl.BlockSpec((1, 128), lambda i: (0, 0))),
        input_output_aliases={3: 0}, compiler_params=_params(),
    )(proj, bias, dc, dbuf)


DIL_REACH = 2048


def _pair_weight(mode, d):
    if mode == "fox":
        return jnp.where(d >= 0, 1.0, 0.0)
    w1 = jnp.where(jnp.abs(d - 64) <= 64, 1.0, 0.0)
    w2 = jnp.where((d & 3) == 0, jnp.where(jnp.abs(d - 256) <= 256, 1.0, 0.0), 0.0)
    w3 = jnp.where((d & 15) == 0, jnp.where(jnp.abs(d - 1024) <= 1024, 1.0, 0.0), 0.0)
    return w1 + w2 + w3


def _bias_tables(mode, tq, tk):
    nb = 2 if mode == "fox" else DIL_REACH // tk + 1
    n = lax.broadcasted_iota(jnp.int32, (nb, tk, tq), 0)
    key = lax.broadcasted_iota(jnp.int32, (nb, tk, tq), 1)
    query = lax.broadcasted_iota(jnp.int32, (nb, tk, tq), 2)
    w = _pair_weight(mode, n * tk + query - key)
    return jnp.where(w > 0.0, jnp.log(jnp.maximum(w, 1.0)), NEG)


M_INIT = -1e29


def _first_key_chunk(mode, q0, tk):
    if mode == "fox":
        return 0
    return jnp.maximum(q0 - DIL_REACH, 0) // tk


def _attention_fwd(mode, q, k, vt, tab_t, ybuf, col):
    s, width = q.shape[1], q.shape[2]
    tq = min(ATT_TQ, s)
    tk = tq
    nb = tab_t.shape[0]

    def body(q_ref, k_ref, vt_ref, tab_ref, buf_ref, y_ref, o_ref, lse_ref):
        i = pl.program_id(0)
        lo = _first_key_chunk(mode, i * tq, tk)

        def step(c, carry):
            k0 = pl.multiple_of(c * tk, tk)
            tab = tab_ref[jnp.minimum(i - c, nb - 1)]
            scores = [lax.dot_general(k_ref[h, pl.ds(k0, tk), :], q_ref[h], (NT, ((), ())),
                                      preferred_element_type=F32) for h in range(N_HEADS)]
            stats, probs = [], []
            for h in range(N_HEADS):
                m, l = carry[3 * h:3 * h + 2]
                sc = scores[h] + tab
                m_new = jnp.maximum(m, jnp.max(sc, axis=0, keepdims=True))
                alpha = jnp.exp(m - m_new)
                p = jnp.exp(sc - m_new)
                stats.append((m_new, alpha * l + jnp.sum(p, axis=0, keepdims=True), alpha))
                probs.append(p.astype(BF16))
            pv = [jnp.dot(vt_ref[h, :, pl.ds(k0, tk)], probs[h], preferred_element_type=F32) for h in range(N_HEADS)]
            new = []
            for h in range(N_HEADS):
                m_new, l, alpha = stats[h]
                new += [m_new, l, alpha * carry[3 * h + 2] + pv[h]]
            return tuple(new)

        start = (jnp.full((1, tq), M_INIT, F32), jnp.zeros((1, tq), F32), jnp.zeros((HEAD_DIM, tq), F32))
        done = lax.fori_loop(lo, i + 1, step, start * N_HEADS)
        outs = []
        for h in range(N_HEADS):
            m, l, acc = done[3 * h:3 * h + 3]
            outs.append(acc / l)
            lse_ref[h] = m + jnp.log(l)
        out = jnp.concatenate(outs, axis=0).T
        y_ref[...] = out.astype(y_ref.dtype)
        o_ref[...] = out

    rowspec = pl.BlockSpec((N_HEADS, 1, tq), lambda i: (0, 0, i))
    return pl.pallas_call(
        body, name="attention_fwd_" + mode, grid=(s // tq,),
        out_shape=(jax.ShapeDtypeStruct(ybuf.shape, ybuf.dtype), jax.ShapeDtypeStruct((s, GROUP), F32),
                   jax.ShapeDtypeStruct((N_HEADS, 1, s), F32)),
        in_specs=[pl.BlockSpec((N_HEADS, tq, width), lambda i: (0, i, 0)),
                  pl.BlockSpec((N_HEADS, s, width), lambda i: (0, 0, 0)),
                  pl.BlockSpec((N_HEADS, HEAD_DIM, s), lambda i: (0, 0, 0)),
                  pl.BlockSpec((nb, tk, tq), lambda i: (0, 0, 0)), ANY_SPEC],
        out_specs=(pl.BlockSpec((tq, GROUP), lambda i: (i, col)), pl.BlockSpec((tq, GROUP), lambda i: (i, 0)),
                   rowspec),
        input_output_aliases={4: 0}, compiler_params=_params(),
    )(q, k, vt, tab_t, ybuf)


def _attention_delta(o, do, col):
    s = o.shape[0]
    t = min(ROW_TILE, s)

    def body(o_ref, do_ref, delta_ref, dob_ref):
        dov = do_ref[...]
        prod_t = (o_ref[...] * dov).T
        for h in range(N_HEADS):
            hs = slice(h * HEAD_DIM, (h + 1) * HEAD_DIM)
            delta_ref[h] = jnp.sum(prod_t[hs, :], axis=0, keepdims=True)
            dob_ref[h] = dov[:, hs].astype(dob_ref.dtype)

    return pl.pallas_call(
        body, name="attention_delta", grid=(s // t,),
        out_shape=(jax.ShapeDtypeStruct((N_HEADS, 1, s), F32), jax.ShapeDtypeStruct((N_HEADS, s, HEAD_DIM), BF16)),
        in_specs=[pl.BlockSpec((t, GROUP), lambda i: (i, 0)), pl.BlockSpec((t, GROUP), lambda i: (i, col))],
        out_specs=(pl.BlockSpec((N_HEADS, 1, t), lambda i: (0, 0, i)),
                   pl.BlockSpec((N_HEADS, t, HEAD_DIM), lambda i: (0, i, 0))),
        compiler_params=_params(),
    )(o, do)


def _attention_bwd(mode, q, k, v, kt, tab_t, dob, lse, delta):
    s, width = q.shape[1], q.shape[2]
    tq = min(ATT_TQ, s)
    tk = tq
    nq = s // tq
    nb = tab_t.shape[0]

    def body(q_ref, k_ref, v_ref, kt_ref, tab_ref, dob_ref, lse_ref, delta_ref, dqt_ref, dk_ref, dv_ref):
        i = pl.program_id(0)

        @pl.when(i == 0)
        def _():
            dqt_ref[...] = jnp.zeros_like(dqt_ref)

        hi = nq if mode == "fox" else jnp.minimum((i * tk + tk - 1 + DIL_REACH) // tq + 1, nq)
        for h0 in range(0, N_HEADS, BWD_HEADS):
            heads = range(h0, h0 + BWD_HEADS)

            def step(c, carry, heads=heads):
                q0 = pl.multiple_of(c * tq, tq)
                qs = pl.ds(q0, tq)
                tab = tab_ref[jnp.minimum(c - i, nb - 1)]
                qv = [q_ref[h, qs, :] for h in heads]
                dov = [dob_ref[h, qs, :] for h in heads]
                sc = [lax.dot_general(k_ref[h], qv[n], (NT, ((), ())), preferred_element_type=F32)
                      for n, h in enumerate(heads)]
                dp = [lax.dot_general(v_ref[h], dov[n], (NT, ((), ())), preferred_element_type=F32)
                      for n, h in enumerate(heads)]
                pb, dsb = [], []
                for n, h in enumerate(heads):
                    p = jnp.exp(sc[n] + tab - lse_ref[h, :, qs])
                    pb.append(p.astype(BF16))
                    dsb.append((p * (dp[n] - delta_ref[h, :, qs])).astype(BF16))
                new = []
                for n, h in enumerate(heads):
                    new += [carry[2 * n] + jnp.dot(dsb[n], qv[n], preferred_element_type=F32),
                            carry[2 * n + 1] + jnp.dot(pb[n], dov[n], preferred_element_type=F32)]
                for n, h in enumerate(heads):
                    dqt_ref[h, :, qs] += jnp.dot(kt_ref[h], dsb[n], preferred_element_type=F32)
                return tuple(new)

            start = (jnp.zeros((tk, width), F32), jnp.zeros((tk, HEAD_DIM), F32))
            done = lax.fori_loop(i, hi, step, start * BWD_HEADS)
            for n, h in enumerate(heads):
                dk_ref[h] = done[2 * n]
                dv_ref[h] = done[2 * n + 1]

    def full(shape):
        return pl.BlockSpec(shape, lambda i: (0, 0, 0))

    kblk = pl.BlockSpec((N_HEADS, tk, width), lambda i: (0, i, 0))
    vblk = pl.BlockSpec((N_HEADS, tk, HEAD_DIM), lambda i: (0, i, 0))
    return pl.pallas_call(
        body, name="attention_bwd_" + mode, grid=(s // tk,),
        out_shape=(jax.ShapeDtypeStruct((N_HEADS, KT_ROWS, s), F32), jax.ShapeDtypeStruct((N_HEADS, s, width), F32),
                   jax.ShapeDtypeStruct((N_HEADS, s, HEAD_DIM), F32)),
        in_specs=[full((N_HEADS, s, width)), kblk, vblk, pl.BlockSpec((N_HEADS, KT_ROWS, tk), lambda i: (0, 0, i)),
                  full((nb, tk, tq)), full((N_HEADS, s, HEAD_DIM)), full((N_HEADS, 1, s)), full((N_HEADS, 1, s))],
        out_specs=(full((N_HEADS, KT_ROWS, s)), kblk, vblk),
        compiler_params=_params(),
    )(q, k, v, kt, tab_t, dob, lse, delta)


def _xattn_fwd(qx, kvm):
    s = qx.shape[0]
    t = min(ROW_TILE, s)

    def body(q_ref, kv_ref, o_ref):
        heads = range(XA_HEADS)
        sc = [lax.dot_general(q_ref[:, h * XA_DIM:(h + 1) * XA_DIM].astype(BF16), kv_ref[h].astype(BF16),
                              (NT, ((), ())), preferred_element_type=F32) * (XA_DIM ** -0.5) for h in heads]
        probs = []
        for h in heads:
            e = jnp.exp(sc[h] - jnp.max(sc[h], axis=-1, keepdims=True))
            probs.append((e / jnp.sum(e, axis=-1, keepdims=True)).astype(BF16))
        outs = [jnp.dot(probs[h], kv_ref[XA_HEADS + h].astype(BF16), preferred_element_type=F32) for h in heads]
        for h in heads:
            o_ref[:, h * XA_DIM:(h + 1) * XA_DIM] = outs[h].astype(o_ref.dtype)

    return pl.pallas_call(
        body, name="xattn_fwd", grid=(s // t,), out_shape=jax.ShapeDtypeStruct((s, D_MODEL), BF16),
        in_specs=[pl.BlockSpec((t, D_MODEL), lambda i: (i, 0)),
                  pl.BlockSpec((2 * XA_HEADS, MEM_LEN, XA_DIM), lambda i: (0, 0, 0))],
        out_specs=pl.BlockSpec((t, D_MODEL), lambda i: (i, 0)), compiler_params=_params(),
    )(qx, kvm)


def _xattn_bwd(qx, kvm, do):
    s = qx.shape[0]
    t = min(ROW_TILE, s)

    def body(q_ref, kv_ref, do_ref, dq_ref, dkv_ref):
        i = pl.program_id(0)
        heads = range(XA_HEADS)
        qv = [q_ref[:, h * XA_DIM:(h + 1) * XA_DIM].astype(BF16) for h in heads]
        dov = [do_ref[:, h * XA_DIM:(h + 1) * XA_DIM].astype(BF16) for h in heads]
        kv = [kv_ref[h].astype(BF16) for h in heads]
        sc = [lax.dot_general(qv[h], kv[h], (NT, ((), ())), preferred_element_type=F32) * (XA_DIM ** -0.5)
              for h in heads]
        dp = [lax.dot_general(dov[h], kv_ref[XA_HEADS + h].astype(BF16), (NT, ((), ())), preferred_element_type=F32)
              for h in heads]
        pb, ds = [], []
        for h in heads:
            e = jnp.exp(sc[h] - jnp.max(sc[h], axis=-1, keepdims=True))
            p = e / jnp.sum(e, axis=-1, keepdims=True)
            pb.append(p.astype(BF16))
            ds.append((p * (dp[h] - jnp.sum(p * dp[h], axis=-1, keepdims=True)) * (XA_DIM ** -0.5)).astype(BF16))
        dq = [jnp.dot(ds[h], kv[h], preferred_element_type=F32) for h in heads]
        dk = [lax.dot_general(ds[h], qv[h], (TN, ((), ())), preferred_element_type=F32) for h in heads]
        dv = [lax.dot_general(pb[h], dov[h], (TN, ((), ())), preferred_element_type=F32) for h in heads]
        for h in heads:
            dq_ref[:, h * XA_DIM:(h + 1) * XA_DIM] = dq[h].astype(dq_ref.dtype)

        @pl.when(i == 0)
        def _():
            for h in heads:
                dkv_ref[h] = dk[h]
                dkv_ref[XA_HEADS + h] = dv[h]

        @pl.when(i > 0)
        def _():
            for h in heads:
                dkv_ref[h] += dk[h]
                dkv_ref[XA_HEADS + h] += dv[h]

    row = pl.BlockSpec((t, D_MODEL), lambda i: (i, 0))
    kvs = pl.BlockSpec((2 * XA_HEADS, MEM_LEN, XA_DIM), lambda i: (0, 0, 0))
    return pl.pallas_call(
        body, name="xattn_bwd", grid=(s // t,),
        out_shape=(jax.ShapeDtypeStruct((s, D_MODEL), BF16),
                   jax.ShapeDtypeStruct((2 * XA_HEADS, MEM_LEN, XA_DIM), F32)),
        in_specs=[row, kvs, row], out_specs=(row, kvs), compiler_params=_params(),
    )(qx, kvm, do)


def _adamw(parts, owns, me, w, m, v, name):
    nl, r, c = w.shape
    tr = r
    for cand in (512, 352, 256, 176, 128, 64, 32, 16, 8):
        if r % cand == 0 and r > cand and N_DEV * cand * c * 4 <= ADAMW_BLOCK_BYTES:
            tr = cand
            break
    nt = r // tr
    per_layer = N_DEV + (1 if owns is not None else 0)

    def body(me_ref, *refs):
        w_ref, m_ref, v_ref, g_ref, d_ref, nm_ref, nv_ref = refs[nl * per_layer:]
        layer = pl.program_id(0)
        g = None
        for l in range(nl):
            p_refs = refs[l * per_layer:(l + 1) * per_layer]
            gl = None
            for d in range(N_DEV):
                term = p_refs[d][...].astype(F32)
                if owns is not None:
                    term = jnp.where(me_ref[0] == d, p_refs[N_DEV][...].astype(F32), term)
                gl = term if gl is None else gl + term
            g = gl if g is None else jnp.where(layer == l, gl, g)
        mn = ADAM_B1 * m_ref[...] + (1.0 - ADAM_B1) * g
        vn = ADAM_B2 * v_ref[...] + (1.0 - ADAM_B2) * (g * g)
        m_hat = mn / (1.0 - ADAM_B1 ** ADAM_STEP)
        v_hat = vn / (1.0 - ADAM_B2 ** ADAM_STEP)
        g_ref[...] = g
        d_ref[...] = -ADAM_LR * (m_hat / (jnp.sqrt(v_hat) + ADAM_EPS) + ADAM_WD * w_ref[...])
        nm_ref[...] = mn
        nv_ref[...] = vn

    def rows(l, ll, i):
        return jnp.where(ll == l, i, jnp.where(ll < l, 0, nt - 1))

    def part_spec(l, d):
        if owns is None:
            return pl.BlockSpec((None, tr, c), lambda ll, i, me_ref: (d, rows(l, ll, i), 0))
        return pl.BlockSpec((None, tr, c),
                            lambda ll, i, me_ref: (jnp.where(me_ref[0] == d, (d + 1) % N_DEV, d), rows(l, ll, i), 0))

    def own_spec(l):
        return pl.BlockSpec((None, tr, c), lambda ll, i, me_ref: (me_ref[0], rows(l, ll, i), 0))

    in_specs, operands = [], []
    for l in range(nl):
        in_specs += [part_spec(l, d) for d in range(N_DEV)]
        operands += [parts[l]] * N_DEV
        if owns is not None:
            in_specs.append(own_spec(l))
            operands.append(owns[l])
    blk = pl.BlockSpec((None, tr, c), lambda ll, i, me_ref: (ll, i, 0))
    shp = jax.ShapeDtypeStruct((nl, r, c), F32)
    return pl.pallas_call(
        body, name=name, out_shape=(shp, shp, shp, shp),
        grid_spec=pltpu.PrefetchScalarGridSpec(
            num_scalar_prefetch=1, grid=(nl, nt), in_specs=in_specs + [blk, blk, blk],
            out_specs=(blk, blk, blk, blk)),
        compiler_params=_params(),
    )(me.reshape(1), *operands, w, m, v)


GROUPS = {"in": ("w_in",), "rest": ("w_out", "w_xq", "w_xo", "w_xkv", "w_up", "w_down")}
FULL_SHAPES = {"w_in": (D_MODEL, N_IN_PAD), "w_out": (D_MODEL, D_MODEL), "w_xq": (D_MODEL, D_MODEL),
               "w_xo": (D_MODEL, D_MODEL), "w_xkv": (N_DEV, D_MODEL, 2 * D_MODEL // N_DEV),
               "w_up": (N_DEV, FF_SHARD, D_MODEL), "w_down": (FF_HALF, FF_SHARD, D_MODEL)}
PIECE_SHAPES = {"w_in": (N_DEV, D_MODEL // N_DEV, N_IN_PAD), "w_out": (N_DEV, D_MODEL // N_DEV, D_MODEL),
                "w_xq": (N_DEV, D_MODEL // N_DEV, D_MODEL), "w_xo": (N_DEV, D_MODEL // N_DEV, D_MODEL),
                "w_xkv": (N_DEV, D_MODEL, 2 * D_MODEL // N_DEV), "w_up": (N_DEV, FF_SHARD, D_MODEL),
                "w_down": (N_DEV, D_FF // N_DEV, D_MODEL)}
GATHER_GROUPS = {"in": ("w_in",), "mid": ("w_out", "w_xq", "w_xo", "w_xkv"), "ffn": ("w_up", "w_down")}
CONV_WORDS = 8192


class _GatheredWeights:
    def __init__(self, states, layer):
        self.states, self.layer, self.full, self.extra = dict(states), layer, {}, None

    def need(self, group, after):
        if group in self.states:
            got, _ = _exchange_wait(self.states.pop(group), after, "gather_%s_wait_%d" % (group, self.layer))
            for name, g in zip(GATHER_GROUPS[group], got):
                self.full[name] = g.reshape(FULL_SHAPES[name])
            self.extra = got[len(GATHER_GROUPS[group]):]

    def __getitem__(self, name):
        return self.full[name]


def _relay_in_cols(w):
    pad = jnp.zeros(w.shape[:-1] + (N_IN_PAD - N_IN,), w.dtype)
    return jnp.concatenate([w[..., :2304], w[..., 2308:N_IN], w[..., 2304:2308], pad], axis=-1)


def _unrelay_in_cols(w):
    return jnp.concatenate([w[..., :2304], w[..., COL_GATE:COL_GATE + 4], w[..., 2304:COL_GATE]], axis=-1)


def _layer_fwd(h, memv, w, sm, tables, rest_arrived):
    sv = {"h0": h}
    s = h.shape[0]
    tm, tb = min(SLAB_TILE, s), min(MM_TILE, s)
    w.need("in", h)
    proj, xn = _norm_matmul(h, sm["g_mix"], w["w_in"], (s, N_IN_PAD), grid=(s // tm, 1),
                            b_spec=pl.BlockSpec((D_MODEL, N_IN_PAD), lambda i, j: (0, 0)),
                            o_spec=pl.BlockSpec((tm, N_IN_PAD), lambda i, j: (i, 0)), name="norm_mm_in")
    sv["xn"], sv["proj"] = xn, proj
    ycat = _sconv_fwd(proj, sm["w_sconv"])
    qd, kd, vd, ktd, vtd = _heads_split(proj, 1, tables["rope"], None, "split_dil")
    ycat, ob, lse_b = _attention_fwd("dil", qd, kd, vtd, tables["dil"], ycat, 1)
    sv["dil"] = (qd, kd, vd, ktd, ob, lse_b)
    c = _gate_cumsum(proj, sm["b_forget_pad"])
    qf, kf, vf, ktf, vtf = _heads_split(proj, 2, None, c, "split_fox")
    ycat, oc, lse_c = _attention_fwd("fox", qf, kf, vtf, tables["fox"], ycat, 2)
    sv["fox"] = (qf, kf, vf, ktf, oc, lse_c)
    ycat = _pool_fwd(proj, sm["w_pool_bd"], sm["pool_scale"], ycat)
    sv["ycat"] = ycat
    w.need("mid", ycat)
    h1 = _mm_nn(ycat, w["w_out"], "mm_out", res=h)
    sv["h1"] = h1
    memn = _rms_fwd(memv, sm["g_mem"], "rms_mem")
    qx, xq = _norm_matmul(h1, sm["g_xa"], w["w_xq"], (s, D_MODEL), grid=(s // tb, 1),
                          b_spec=pl.BlockSpec((D_MODEL, D_MODEL), lambda i, j: (0, 0)),
                          o_spec=pl.BlockSpec((tb, D_MODEL), lambda i, j: (i, 0)), name="norm_mm_xq",
                          out_dtype=BF16)
    kvm = _matmul(memn, w["w_xkv"], (N_DEV, MEM_LEN, XA_DIM), grid=(N_DEV, 1, 1),
                  a_spec=pl.BlockSpec((MEM_LEN, D_MODEL), lambda i, j, r: (0, 0)),
                  b_spec=pl.BlockSpec((None, D_MODEL, XA_DIM), lambda i, j, r: (i, 0, 0)),
                  o_spec=pl.BlockSpec((None, MEM_LEN, XA_DIM), lambda i, j, r: (i, 0, 0)),
                  dims=NN, nred=1, name="mm_xkv")
    ox = _xattn_fwd(qx, kvm)
    sv.update(xq=xq, memn=memn, qx=qx, kvm=kvm, ox=ox)
    h2 = _mm_nn(ox, w["w_xo"], "mm_xo", res=h1)
    sv["h2"] = h2
    w.need("ffn", ox)
    u0, xf = _norm_matmul(h2, sm["g_ffn"] + rest_arrived(w["w_down"]), w["w_up"], (N_DEV, s, FF_SHARD),
                          grid=(s // tb, N_DEV // 4),
                          b_spec=pl.BlockSpec((4, FF_SHARD, D_MODEL), lambda i, j: (j, 0, 0)),
                          o_spec=pl.BlockSpec((4, tb, FF_SHARD), lambda i, j: (j, i, 0)), name="norm_mm_up",
                          out_dtype=BF16, dims=NT)
    act = _ffn_gate_fwd(u0, sm["w_ffconv"])
    sv.update(xf=xf, u0=u0, act=act)
    ospec = pl.BlockSpec((tb, D_MODEL), lambda i, j, r: (i, 0))
    h3 = _matmul(act, w["w_down"], (s, D_MODEL), grid=(s // tb, 1, 1),
                 a_spec=pl.BlockSpec((FF_HALF, tb, FF_SHARD), lambda i, j, r: (0, i, 0)),
                 b_spec=pl.BlockSpec((FF_HALF, FF_SHARD, D_MODEL), lambda i, j, r: (0, 0, 0)),
                 o_spec=ospec, dims=NN, nred=1, slabs=FF_HALF, name="mm_down", res=h2, res_spec=ospec)
    return h3, sv


def _layer_bwd(dh3, memv, w, sm, tables, sv, rest_ready, in_ready):
    s = dh3.shape[0]
    tm, tb = min(ROW_TILE, s), min(MM_TILE, s)
    big, small = {}, {}
    ts = max(s // 2, 1)
    dact = _matmul(dh3, w["w_down"], (FF_HALF, s, FF_SHARD), grid=(s // tb, 1, 1),
                   a_spec=pl.BlockSpec((tb, D_MODEL), lambda i, j, r: (i, 0)),
                   b_spec=pl.BlockSpec((FF_HALF, FF_SHARD, D_MODEL), lambda i, j, r: (0, 0, 0)),
                   o_spec=pl.BlockSpec((FF_HALF, tb, FF_SHARD), lambda i, j, r: (0, i, 0)),
                   dims=NT, nred=1, out_slabs=FF_HALF, name="mm_dact", out_dtype=BF16)
    big["w_down"] = _matmul(sv["act"], dh3, (FF_HALF, FF_SHARD, D_MODEL), grid=(FF_HALF, 1, s // ts),
                            a_spec=pl.BlockSpec((None, ts, FF_SHARD), lambda i, j, r: (i, r, 0)),
                            b_spec=pl.BlockSpec((ts, D_MODEL), lambda i, j, r: (r, 0)),
                            o_spec=pl.BlockSpec((None, FF_SHARD, D_MODEL), lambda i, j, r: (i, 0, 0)),
                            dims=TN, nred=s // ts, name="mm_dw_down", out_dtype=GRAD_DTYPE)
    du0, small["w_ffconv"] = _ffn_gate_bwd(sv["u0"], sm["w_ffconv"], dact)
    dh2, small["g_ffn"] = _matmul_rms_bwd(du0, w["w_up"], sv["h2"], sm["g_ffn"], dh3, "mm_dxf_rms_bwd",
                                          tm=SLAB_TILE // 2, dims=NN)
    big["w_up"] = _matmul(du0, sv["xf"], (N_DEV, FF_SHARD, D_MODEL), grid=(N_DEV, 1, 1),
                          a_spec=pl.BlockSpec((None, s, FF_SHARD), lambda i, j, r: (i, 0, 0)),
                          b_spec=pl.BlockSpec((s, D_MODEL), lambda i, j, r: (0, 0)),
                          o_spec=pl.BlockSpec((None, FF_SHARD, D_MODEL), lambda i, j, r: (i, 0, 0)),
                          dims=TN, nred=1, name="mm_dw_up", out_dtype=GRAD_DTYPE)
    dox = _mm_nt(dh2, w["w_xo"], "mm_dox", out_dtype=BF16)
    big["w_xo"] = _mm_tn(sv["ox"], dh2, "mm_dw_xo")
    dqx, dkvm = _xattn_bwd(sv["qx"], sv["kvm"], dox)
    big["w_xq"] = _mm_tn(sv["xq"], dqx, "mm_dw_xq")
    big["w_xkv"] = _matmul(sv["memn"], dkvm, (N_DEV, D_MODEL, XA_DIM), grid=(N_DEV, 1, 1),
                           a_spec=pl.BlockSpec((MEM_LEN, D_MODEL), lambda i, j, r: (0, 0)),
                           b_spec=pl.BlockSpec((None, MEM_LEN, XA_DIM), lambda i, j, r: (i, 0, 0)),
                           o_spec=pl.BlockSpec((None, D_MODEL, XA_DIM), lambda i, j, r: (i, 0, 0)),
                           dims=TN, nred=1, name="mm_dw_xkv", out_dtype=GRAD_DTYPE)
    dmemn = _matmul(dkvm, w["w_xkv"], (MEM_LEN, D_MODEL), grid=(1, 1, 1),
                    a_spec=pl.BlockSpec((N_DEV, MEM_LEN, XA_DIM), lambda i, j, r: (0, 0, 0)),
                    b_spec=pl.BlockSpec((N_DEV, D_MODEL, XA_DIM), lambda i, j, r: (0, 0, 0)),
                    o_spec=pl.BlockSpec((MEM_LEN, D_MODEL), lambda i, j, r: (0, 0)),
                    dims=NT, nred=1, slabs=N_DEV, name="mm_dmemn")
    _, small["g_mem"] = _rms_bwd(dmemn, memv, sm["g_mem"], None, "rms_mem_bwd")
    dh1, small["g_xa"] = _matmul_rms_bwd(dqx, w["w_xq"], sv["h1"], sm["g_xa"], dh2, "mm_dxq_rms_bwd", tm=MM_TILE)
    big["w_out"] = _mm_tn(sv["ycat"], dh1, "mm_dw_out")
    dycat = _mm_nt(dh1, w["w_out"] + rest_ready(big, small).astype(BF16), "mm_dycat")
    proj = sv["proj"]
    dproj, small["w_sconv"] = _sconv_bwd(proj, sm["w_sconv"], dycat)
    qd, kd, vd, ktd, ob, lse_b = sv["dil"]
    delta, dob = _attention_delta(ob, dycat, 1)
    dqt, dk, dv = _attention_bwd("dil", qd, kd, vd, ktd, tables["dil"], dob, lse_b, delta)
    dproj = _heads_merge(dqt, dk, dv, tables["rope"], "merge_dil", dproj, 1)
    qf, kf, vf, ktf, oc, lse_c = sv["fox"]
    delta, dob = _attention_delta(oc, dycat, 2)
    dqt, dk, dv = _attention_bwd("fox", qf, kf, vf, ktf, tables["fox"], dob, lse_c, delta)
    dproj, dc = _heads_merge(dqt, dk, dv, None, "merge_fox", dproj, 2)
    dproj, dbias = _gate_cumsum_bwd(proj, sm["b_forget_pad"], dc, dproj)
    small["b_forget"] = dbias[0, :N_HEADS]
    dproj, dwbd, small["pool_scale"] = _pool_bwd(proj, sm["w_pool_bd"], sm["pool_scale"], dycat, dproj)
    small["w_pool"] = jnp.stack([dwbd[64 * g:64 * (g + 1), 64 * g:64 * (g + 1)] for g in range(4)])
    big["w_in"] = _mm_tn(sv["xn"], dproj, "mm_dw_in")
    dh0, small["g_mix"] = _matmul_rms_bwd(dproj, w["w_in"], sv["h0"], sm["g_mix"] + in_ready(big, small), dh1,
                                          "mm_dxn_rms_bwd")
    return dh0, big, small


SMALL_NAMES = ("g_mix", "b_forget", "w_pool", "pool_scale", "g_xa", "g_mem", "g_ffn", "w_sconv", "w_ffconv")
SMALL_WITH = {"rest": ("w_ffconv", "g_ffn", "g_mem", "g_xa"),
              "in": ("w_sconv", "b_forget", "pool_scale", "w_pool")}
SMALL_SHAPES = {"w_sconv": (3, GROUP), "w_ffconv": (N_DEV, 3, FF_SHARD)}
WEIGHT_NAMES = ("g_mix", "w_in", "b_forget", "w_sconv", "w_pool", "pool_scale", "w_out", "g_xa", "g_mem", "w_xq",
                "w_xkv", "w_xo", "g_ffn", "w_up", "w_ffconv", "w_down", "g_final")


def _block_diag(w_pool):
    z = jnp.zeros((64, 64), F32)
    return jnp.concatenate(
        [jnp.concatenate([w_pool[g] if c == g else z for c in range(4)], axis=1) for g in range(4)], axis=0)


def kernel(x, mem, positions, g_mix, w_in, b_forget, w_sconv, w_pool, pool_scale, w_out, g_xa, g_mem, w_xq, w_xkv, w_xo, g_ffn, w_up, w_ffconv, w_down, g_final, loss_target, m_g_mix, m_w_in, m_b_forget, m_w_sconv, m_w_pool, m_pool_scale, m_w_out, m_g_xa, m_g_mem, m_w_xq, m_w_xkv, m_w_xo, m_g_ffn, m_w_up, m_w_ffconv, m_w_down, m_g_final, v_g_mix, v_w_in, v_b_forget, v_w_sconv, v_w_pool, v_pool_scale, v_w_out, v_g_xa, v_g_mem, v_w_xq, v_w_xkv, v_w_xo, v_g_ffn, v_w_up, v_w_ffconv, v_w_down, v_g_final):
    weights = dict(g_mix=g_mix, w_in=w_in, b_forget=b_forget, w_sconv=w_sconv, w_pool=w_pool, pool_scale=pool_scale,
                   w_out=w_out, g_xa=g_xa, g_mem=g_mem, w_xq=w_xq, w_xkv=w_xkv, w_xo=w_xo, g_ffn=g_ffn, w_up=w_up,
                   w_ffconv=w_ffconv, w_down=w_down, g_final=g_final)
    m_in = dict(g_mix=m_g_mix, w_in=m_w_in, b_forget=m_b_forget, w_sconv=m_w_sconv, w_pool=m_w_pool,
                pool_scale=m_pool_scale, w_out=m_w_out, g_xa=m_g_xa, g_mem=m_g_mem, w_xq=m_w_xq, w_xkv=m_w_xkv,
                w_xo=m_w_xo, g_ffn=m_g_ffn, w_up=m_w_up, w_ffconv=m_w_ffconv, w_down=m_w_down, g_final=m_g_final)
    v_in = dict(g_mix=v_g_mix, w_in=v_w_in, b_forget=v_b_forget, w_sconv=v_w_sconv, w_pool=v_w_pool,
                pool_scale=v_pool_scale, w_out=v_w_out, g_xa=v_g_xa, g_mem=v_g_mem, w_xq=v_w_xq, w_xkv=v_w_xkv,
                w_xo=v_w_xo, g_ffn=v_g_ffn, w_up=v_w_up, w_ffconv=v_w_ffconv, w_down=v_w_down, g_final=v_g_final)
    depth = w_in.shape[0]
    me = 4 * lax.axis_index("x") + 2 * lax.axis_index("y") + lax.axis_index("c")
    h = x[0]
    memv = mem[0]
    s = h.shape[0]
    tq = min(ATT_TQ, s)
    tables = {"rope": _rope_tables(positions[0]), "dil": _bias_tables("dil", tq, tq),
              "fox": _bias_tables("fox", tq, tq)}

    w_in_r = _relay_in_cols(w_in)
    conv_shard = jnp.concatenate([w_sconv.reshape(-1), w_ffconv.reshape(-1)])
    conv_shard = jnp.concatenate([conv_shard, jnp.zeros((CONV_WORDS - conv_shard.shape[0],), F32)])
    shards = dict(w_in=w_in_r, w_out=w_out, w_xq=w_xq, w_xo=w_xo, w_xkv=w_xkv, w_up=w_up.transpose(0, 2, 1),
                  w_down=w_down)
    gathered = [None] * depth

    def start_gathers(l, after):
        states = {}
        order = jnp.zeros((), F32)
        for group in GATHER_GROUPS:
            first = GATHER_GROUPS[group][0]
            shards[first] = shards[first] + order
            xs = [_place_shard(shards[name], l, me, BF16, "place_%s_%d" % (name, l), after)
                  for name in GATHER_GROUPS[group]]
            if l == 0 and group == "in":
                xs.append(_place_shard(conv_shard.reshape(1, CONV_WORDS // 1024, 1024), 0, me, F32, "place_conv"))
            states[group], token = _exchange_start(xs, False, "gather_%s_start_%d" % (group, l))
            order = order + token[0, 0]
        gathered[l] = _GatheredWeights(states, l)
        return order

    order = start_gathers(0, None)
    gathered[0].need("in", tables["rope"][0])
    conv_all = gathered[0].extra[0].reshape(N_DEV, CONV_WORDS)
    n_sc = depth * 3 * (GROUP // N_DEV)
    sconv_full = conv_all[:, :n_sc].reshape(N_DEV, depth, 3, GROUP // N_DEV).transpose(1, 2, 0, 3).reshape(
        depth, 3, GROUP)
    ffconv_full = conv_all[:, n_sc:n_sc + depth * 3 * FF_SHARD].reshape(N_DEV, depth, 3, FF_SHARD).transpose(
        1, 0, 2, 3)

    smalls = []
    for l in range(depth):
        smalls.append(dict(
            g_mix=g_mix[l], g_xa=g_xa[l], g_mem=g_mem[l], g_ffn=g_ffn[l], pool_scale=pool_scale[l],
            w_pool_bd=_block_diag(w_pool[l]), w_sconv=sconv_full[l], w_ffconv=ffconv_full[l],
            b_forget_pad=jnp.concatenate([b_forget[l], jnp.zeros((128 - N_HEADS,), F32)]).reshape(1, 128)))
    smalls[0]["g_mix"] = smalls[0]["g_mix"] + order

    saved = []
    for l in range(depth):
        def rest_arrived(arrived, l=l):
            return start_gathers(l + 1, arrived) if l + 1 < depth else jnp.zeros((), F32)

        h, sv = _layer_fwd(h, memv, gathered[l], smalls[l], tables, rest_arrived)
        saved.append(sv)
    loss_part, dh, dg_final = _loss_head(h, g_final, loss_target[0])
    loss = lax.psum(loss_part[0, 0], MESH_AXES)

    small_grads = [None] * depth
    scatters = {}

    def pieces_of(big, group):
        return [big[name].reshape(PIECE_SHAPES[name]) for name in GROUPS[group]]

    def rider(grads):
        flat = jnp.concatenate([g.reshape(-1) for g in grads])
        rows = -(-flat.shape[0] // 1024)
        flat = jnp.concatenate([flat, jnp.zeros((rows * 1024 - flat.shape[0],), F32)])
        return jnp.broadcast_to(flat.reshape(1, rows, 1024), (N_DEV, rows, 1024))

    riding = {}
    done_small = {}

    def start_scatter(l, group, big, extra):
        names = [(n, l) for n in SMALL_WITH[group]] + extra
        riding[l, group] = names
        grads = [dg_final if n == "g_final" else done_small[ll][n] for n, ll in names]
        scatters[l, group], token = _exchange_start(pieces_of(big, group) + [rider(grads)], True,
                                                    "scatter_%s_start_%d" % (group, l))
        return token[0, 0]

    for l in reversed(range(depth)):
        def rest_ready(big, small, l=l):
            done_small[l] = small
            extra = ([("g_final", l)] if l == depth - 1 else []) + ([("g_mix", l + 1)] if l + 1 < depth else [])
            return start_scatter(l, "rest", big, extra)

        def in_ready(big, small, l=l):
            return start_scatter(l, "in", big, [])

        dh, _, small_grads[l] = _layer_bwd(dh, memv, gathered[l], smalls[l], tables, saved[l], rest_ready, in_ready)
    grad_x = dh[None]
    riding["tail"] = [("g_mix", 0)]
    scatters["tail"], _ = _exchange_start([rider([small_grads[0]["g_mix"]])], True, "scatter_tail_start")

    parts, owns, small_parts = {}, {}, {}

    def take_rider(key, got, given):
        flat = lax.dynamic_update_slice_in_dim(got, given[:1], me, axis=0).reshape(N_DEV, -1)
        off = 0
        for name, ll in riding[key]:
            shape = SMALL_SHAPES.get(name, weights[name].shape[-1:] if name == "g_final" else weights[name].shape[1:])
            n = 1
            for dim in shape:
                n *= dim
            small_parts.setdefault(name, [None] * depth)[ll] = flat[:, off:off + n].reshape((N_DEV,) + shape)
            off += n

    def wait_group(group, after):
        for l in reversed(range(depth)):
            got, given = _exchange_wait(scatters[l, group], after, "scatter_%s_wait_%d" % (group, l))
            for name, g, x in zip(GROUPS[group], got, given):
                parts.setdefault(name, [None] * depth)[l] = g
                owns.setdefault(name, [None] * depth)[l] = x
            take_rider((l, group), got[-1], given[-1])

    results = {}

    def update(name, w3, m3, v3):
        outs = _adamw(parts[name], owns.get(name), me, w3, m3, v3, "adamw_" + name)
        results[name] = [o.reshape(weights[name].shape) for o in outs]

    wait_group("rest", grad_x)
    for name in GROUPS["rest"]:
        if name == "w_up":
            outs = _adamw(parts[name], owns[name], me, w_up.transpose(0, 2, 1), m_w_up.transpose(0, 2, 1),
                          v_w_up.transpose(0, 2, 1), "adamw_w_up")
            results[name] = [o.transpose(0, 2, 1) for o in outs]
        else:
            update(name, weights[name], m_in[name], v_in[name])
    wait_group("in", results["w_down"][1])
    outs = _adamw(parts["w_in"], owns["w_in"], me, w_in_r, _relay_in_cols(m_w_in), _relay_in_cols(v_w_in),
                  "adamw_w_in")
    results["w_in"] = [_unrelay_in_cols(o) for o in outs]
    got, given = _exchange_wait(scatters["tail"], results["w_in"][1], "scatter_tail_wait")
    take_rider("tail", got[0], given[0])
    for name in SMALL_NAMES + ("g_final",):
        wv = weights[name]
        p = small_parts[name][depth - 1] if name == "g_final" else jnp.stack(small_parts[name], axis=1)
        if name == "w_sconv":
            p = lax.dynamic_slice_in_dim(p, me * (GROUP // N_DEV), GROUP // N_DEV, axis=3)
        elif name == "w_ffconv":
            p = lax.dynamic_index_in_dim(p, me, axis=2, keepdims=False)
        shape3 = (1, 1, wv.shape[0]) if wv.ndim == 1 else (1, -1, wv.shape[-1])
        w3 = wv.reshape(shape3)
        parts[name] = [p.reshape((N_DEV,) + w3.shape[1:])]
        update(name, w3, m_in[name].reshape(shape3), v_in[name].reshape(shape3))

    return (loss, grad_x, *[results[n][0] for n in WEIGHT_NAMES], *[results[n][1] for n in WEIGHT_NAMES],
            *[results[n][2] for n in WEIGHT_NAMES], *[results[n][3] for n in WEIGHT_NAMES])
```

```python
import functools

import jax
import jax.numpy as jnp
from jax import lax
from jax.experimental import pallas as pl
from jax.experimental.pallas import tpu as pltpu

F32 = jnp.float32
BF16 = jnp.bfloat16

N_DEV = 8
D_MODEL = 1024
GROUP = 256
HEAD_DIM = 64
N_HEADS = 4
N_IN = 2564
N_IN_PAD = 2688
COL_GATE = 2560
XA_HEADS = 4
XA_DIM = 256
MEM_LEN = 256
D_FF = 2816
FF_SHARD = 704
FF_HALF = 4
ROPE_THETA = 500000.0
ROPE_DIM = 16
RMS_EPS = 1e-6
NEG = -1e30
POOL_WINDOWS = (2, 4, 8, 16)
ADAM_LR, ADAM_B1, ADAM_B2, ADAM_EPS, ADAM_WD, ADAM_STEP = 0.001, 0.9, 0.999, 1e-08, 0.01, 10

ROW_TILE = 1024
MM_TILE = 1024
SLAB_TILE = 512
ATT_TQ = 512
BWD_HEADS = 4
VMEM_LIMIT = 56 * 1024 * 1024
ADAMW_BLOCK_BYTES = 8 * 1024 * 1024
PLACE_BLOCK_BYTES = 4 * 1024 * 1024
TN_BLOCK_BYTES = 12 * 1024 * 1024

MESH_AXES = ("x", "y", "c")


def _params(**kw):
    return pltpu.CompilerParams(vmem_limit_bytes=VMEM_LIMIT, **kw)


HBM_SPEC = pl.BlockSpec(memory_space=pltpu.HBM)
SEM_SPEC = pl.BlockSpec(memory_space=pltpu.SEMAPHORE)
DATAFLOW = pltpu.SideEffectType.DATAFLOW_SIDE_EFFECTING


def _peer_copies(x_ref, land_ref, send_sems, recv_sems, scatter):
    mx, my, mc = lax.axis_index("x"), lax.axis_index("y"), lax.axis_index("c")
    me = 4 * mx + 2 * my + mc
    pairs = []
    for k in range(1, N_DEV):
        kx, ky, kc = (k >> 2) & 1, (k >> 1) & 1, k & 1
        peer_lin = me ^ k
        send = pltpu.make_async_remote_copy(
            src_ref=x_ref.at[peer_lin] if scatter else land_ref.at[me], dst_ref=land_ref.at[me],
            send_sem=send_sems.at[k - 1], recv_sem=recv_sems.at[k - 1],
            device_id=(mx ^ kx, my ^ ky, mc ^ kc), device_id_type=pl.DeviceIdType.MESH)
        arrival = pltpu.make_async_remote_copy(
            src_ref=land_ref.at[peer_lin], dst_ref=land_ref.at[peer_lin],
            send_sem=send_sems.at[k - 1], recv_sem=recv_sems.at[k - 1],
            device_id=(mx, my, mc), device_id_type=pl.DeviceIdType.MESH)
        pairs.append((send, arrival))
    return pairs


def _exchange_start(xs, scatter, name):
    n = len(xs)
    ns = n if scatter else 0

    def body(*refs):
        srcs = refs[:ns] if scatter else (None,) * n
        lands, sends, recvs = refs[ns:ns + n], refs[ns + n:ns + 2 * n], refs[ns + 2 * n:ns + 3 * n]
        for t in range(n):
            for send, _ in _peer_copies(srcs[t], lands[t], sends[t], recvs[t], scatter):
                send.start()
        token = refs[-1]
        token[...] = jnp.zeros_like(token)

    sems = pltpu.SemaphoreType.DMA((N_DEV - 1,))
    operands = [pltpu.with_memory_space_constraint(x, pltpu.HBM) for x in xs]
    if scatter:
        operands += [pltpu.with_memory_space_constraint(lax.empty(x.shape, x.dtype), pltpu.HBM) for x in xs]
    outs = pl.pallas_call(
        body, name=name,
        out_shape=(sems,) * (2 * n) + tuple(pltpu.HBM(a.shape, a.dtype) for a in operands)
        + (jax.ShapeDtypeStruct((8, 128), F32),),
        in_specs=(HBM_SPEC,) * (ns + n),
        out_specs=(SEM_SPEC,) * (2 * n) + (HBM_SPEC,) * (ns + n) + (pl.BlockSpec(memory_space=pltpu.VMEM),),
        input_output_aliases={i: 2 * n + i for i in range(ns + n)},
        compiler_params=pltpu.CompilerParams(has_side_effects=DATAFLOW),
    )(*operands)
    return (outs[:-1], scatter), outs[-1]


def _exchange_wait(state, after, name):
    held, scatter = state
    n = len(held) // (4 if scatter else 3)
    ns = n if scatter else 0
    sems, thru = held[:2 * n], held[2 * n:]

    def body(*refs):
        srcs = refs[:ns] if scatter else (None,) * n
        lands, sends, recvs = refs[ns:ns + n], refs[ns + n:ns + 2 * n], refs[ns + 2 * n:ns + 3 * n]
        for t in range(n):
            for send, arrival in _peer_copies(srcs[t], lands[t], sends[t], recvs[t], scatter):
                send.wait_send()
                arrival.wait_recv()

    outs = pl.pallas_call(
        body, name=name,
        out_shape=tuple(pltpu.HBM(a.shape, a.dtype) for a in thru),
        in_specs=(HBM_SPEC,) * (ns + n) + (SEM_SPEC,) * (2 * n) + (pl.BlockSpec(memory_space=pl.ANY),),
        out_specs=(HBM_SPEC,) * (ns + n), input_output_aliases={i: i for i in range(ns + n)},
        compiler_params=pltpu.CompilerParams(has_side_effects=DATAFLOW),
    )(*thru, *sems, after)
    return list(outs[ns:]), list(outs[:ns])


def _place_shard(x, layer, me, dtype, name, after=None):
    _, r, c = x.shape
    tr = r
    if r * c * 4 > PLACE_BLOCK_BYTES:
        for cand in (512, 256, 128, 64, 32, 16):
            if r % cand == 0 and cand * c * 4 <= PLACE_BLOCK_BYTES:
                tr = cand
                break

    def body(me_ref, x_ref, *rest):
        o_ref = rest[-1]
        o_ref[...] = x_ref[...].astype(o_ref.dtype)

    return pl.pallas_call(
        body, name=name, out_shape=jax.ShapeDtypeStruct((N_DEV, r, c), dtype),
        grid_spec=pltpu.PrefetchScalarGridSpec(
            num_scalar_prefetch=1, grid=(r // tr,),
            in_specs=[pl.BlockSpec((None, tr, c), lambda i, me_ref: (layer, i, 0))]
            + ([ANY_SPEC] if after is not None else []),
            out_specs=pl.BlockSpec((None, tr, c), lambda i, me_ref: (me_ref[0], i, 0))),
        compiler_params=_params(),
    )(*((me.reshape(1), x) + ((after,) if after is not None else ())))


NN = ((1,), (0,))
NT = ((1,), (1,))
TN = ((0,), (0,))


def _matmul(a, b, out_shape, *, grid, a_spec, b_spec, o_spec, dims, nred, name, res=None, res_spec=None,
            out_dtype=F32, slabs=0, out_slabs=0):
    has_res = res is not None

    def body(*refs):
        a_ref, b_ref = refs[0], refs[1]
        r_ref = refs[2] if has_res else None
        o_ref = refs[3] if has_res else refs[2]
        if out_slabs:
            av = a_ref[...].astype(BF16)
            for n in range(out_slabs):
                o_ref[n] = lax.dot_general(av, b_ref[n].astype(BF16), (dims, ((), ())),
                                           preferred_element_type=F32).astype(o_ref.dtype)
            return
        if slabs:
            part = None
            for n in range(slabs):
                term = lax.dot_general(a_ref[n].astype(BF16), b_ref[n].astype(BF16), (dims, ((), ())),
                                       preferred_element_type=F32)
                part = term if part is None else part + term
        else:
            part = lax.dot_general(a_ref[...].astype(BF16), b_ref[...].astype(BF16), (dims, ((), ())),
                                   preferred_element_type=F32)
        if nred == 1:
            if has_res:
                part = part + r_ref[...]
            o_ref[...] = part.astype(o_ref.dtype)
        else:
            acc = refs[-1]
            r = pl.program_id(2)

            @pl.when(r == 0)
            def _():
                acc[...] = part

            @pl.when(r > 0)
            def _():
                acc[...] += part

            @pl.when(r == nred - 1)
            def _():
                tot = acc[...]
                if has_res:
                    tot = tot + r_ref[...]
                o_ref[...] = tot.astype(o_ref.dtype)

    in_specs = [a_spec, b_spec] + ([res_spec] if has_res else [])
    args = (a, b) + ((res,) if has_res else ())
    acc_shape = tuple(d for d in o_spec.block_shape if d is not None)
    return pl.pallas_call(
        body, name=name, grid=grid, out_shape=jax.ShapeDtypeStruct(out_shape, out_dtype),
        in_specs=in_specs, out_specs=o_spec,
        scratch_shapes=[pltpu.VMEM(acc_shape, F32)] if nred > 1 else [],
        compiler_params=_params(),
    )(*args)


def _mm_nn(a, w, name, res=None, tn=None, out_dtype=F32):
    m, k = a.shape
    n = w.shape[1]
    tn = tn or n
    tm = min(MM_TILE, m)
    ospec = pl.BlockSpec((tm, tn), lambda i, j, r: (i, j))
    return _matmul(a, w, (m, n), grid=(m // tm, n // tn, 1),
                   a_spec=pl.BlockSpec((tm, k), lambda i, j, r: (i, 0)),
                   b_spec=pl.BlockSpec((k, tn), lambda i, j, r: (0, j)),
                   o_spec=ospec, dims=NN, nred=1, name=name, res=res, res_spec=ospec if res is not None else None,
                   out_dtype=out_dtype)


def _mm_nt(a, w, name, out_dtype=F32):
    m, n = a.shape
    k = w.shape[0]
    tm = min(MM_TILE, m)
    return _matmul(a, w, (m, k), grid=(m // tm, 1, 1),
                   a_spec=pl.BlockSpec((tm, n), lambda i, j, r: (i, 0)),
                   b_spec=pl.BlockSpec((k, n), lambda i, j, r: (0, 0)),
                   o_spec=pl.BlockSpec((tm, k), lambda i, j, r: (i, 0)), dims=NT, nred=1, name=name,
                   out_dtype=out_dtype)


def _norm_matmul(h, g, b, out_shape, *, grid, b_spec, o_spec, name, out_dtype=F32, dims=NN):
    s, d = h.shape
    tm = s // grid[0]

    def body(h_ref, g_ref, b_ref, o_ref, xn_ref):
        @pl.when(pl.program_id(1) == 0)
        def _():
            hv = h_ref[...]
            r = lax.rsqrt(jnp.mean(hv * hv, axis=-1, keepdims=True) + RMS_EPS)
            xn_ref[...] = (hv * r * g_ref[...]).astype(xn_ref.dtype)

        if len(b_ref.shape) == 3:
            for n in range(b_ref.shape[0]):
                o_ref[n] = lax.dot_general(xn_ref[...], b_ref[n].astype(BF16), (dims, ((), ())),
                                           preferred_element_type=F32).astype(o_ref.dtype)
        else:
            o_ref[...] = lax.dot_general(xn_ref[...], b_ref[...].astype(BF16), (dims, ((), ())),
                                         preferred_element_type=F32).astype(o_ref.dtype)

    row = pl.BlockSpec((tm, d), lambda i, j: (i, 0))
    return pl.pallas_call(
        body, name=name, grid=grid,
        out_shape=(jax.ShapeDtypeStruct(out_shape, out_dtype), jax.ShapeDtypeStruct((s, d), BF16)),
        in_specs=[row, pl.BlockSpec((1, d), lambda i, j: (0, 0)), b_spec],
        out_specs=(o_spec, row), compiler_params=_params(),
    )(h, g.reshape(1, d), b)


def _matmul_rms_bwd(a, w, h, g, res, name, tm=SLAB_TILE, dims=NT):
    slabs = a.shape[0] if a.ndim == 3 else 0
    s, n = a.shape[-2:]
    d = w.shape[-2] if dims == NT else w.shape[-1]
    tm = min(tm, s)

    def body(a_ref, w_ref, h_ref, g_ref, r_ref, dh_ref, dg_ref):
        if slabs:
            dy = None
            for j in range(slabs):
                term = lax.dot_general(a_ref[j].astype(BF16), w_ref[j].astype(BF16), (dims, ((), ())),
                                       preferred_element_type=F32)
                dy = term if dy is None else dy + term
        else:
            dy = lax.dot_general(a_ref[...].astype(BF16), w_ref[...].astype(BF16), (dims, ((), ())),
                                 preferred_element_type=F32)
        hv = h_ref[...]
        r = lax.rsqrt(jnp.mean(hv * hv, axis=-1, keepdims=True) + RMS_EPS)
        hn = hv * r
        u = dy * g_ref[...]
        dh_ref[...] = r * (u - hn * jnp.mean(u * hn, axis=-1, keepdims=True)) + r_ref[...]
        part = jnp.sum(dy * hn, axis=0, keepdims=True)

        @pl.when(pl.program_id(0) == 0)
        def _():
            dg_ref[...] = part

        @pl.when(pl.program_id(0) > 0)
        def _():
            dg_ref[...] += part

    row = pl.BlockSpec((tm, d), lambda i: (i, 0))
    vec = pl.BlockSpec((1, d), lambda i: (0, 0))
    if slabs:
        a_spec = pl.BlockSpec((slabs, tm, n), lambda i: (0, i, 0))
        w_spec = pl.BlockSpec(w.shape, lambda i: (0, 0, 0))
    else:
        a_spec = pl.BlockSpec((tm, n), lambda i: (i, 0))
        w_spec = pl.BlockSpec(w.shape, lambda i: (0, 0))
    dh, dg = pl.pallas_call(
        body, name=name, grid=(s // tm,),
        out_shape=(jax.ShapeDtypeStruct((s, d), F32), jax.ShapeDtypeStruct((1, d), F32)),
        in_specs=[a_spec, w_spec, row, vec, row], out_specs=(row, vec), compiler_params=_params(),
    )(a, w, h, g.reshape(1, d), res)
    return dh, dg.reshape(d)


GRAD_DTYPE = BF16


def _mm_tn(a, b, name, tk=512, tn=None):
    s, k = a.shape
    n = b.shape[1]
    tn = tn or n
    tk = min(tk, k)
    ts = s if b.dtype == BF16 and tn * s * 2 <= TN_BLOCK_BYTES else max(s // 2, 1)
    return _matmul(a, b, (k, n), grid=(k // tk, n // tn, s // ts),
                   a_spec=pl.BlockSpec((ts, tk), lambda i, j, r: (r, i)),
                   b_spec=pl.BlockSpec((ts, tn), lambda i, j, r: (r, j)),
                   o_spec=pl.BlockSpec((tk, tn), lambda i, j, r: (i, j)), dims=TN, nred=s // ts, name=name,
                   out_dtype=GRAD_DTYPE)


def _rms_fwd(h, g, name):
    s, d = h.shape
    tm = min(ROW_TILE, s)

    def body(h_ref, g_ref, o_ref):
        hv = h_ref[...]
        r = lax.rsqrt(jnp.mean(hv * hv, axis=-1, keepdims=True) + RMS_EPS)
        o_ref[...] = (hv * r * g_ref[...]).astype(o_ref.dtype)

    return pl.pallas_call(
        body, name=name, grid=(s // tm,), out_shape=jax.ShapeDtypeStruct((s, d), BF16),
        in_specs=[pl.BlockSpec((tm, d), lambda i: (i, 0)), pl.BlockSpec((1, d), lambda i: (0, 0))],
        out_specs=pl.BlockSpec((tm, d), lambda i: (i, 0)), compiler_params=_params(),
    )(h, g.reshape(1, d))


def _rms_bwd(dy, h, g, res, name):
    s, d = h.shape
    tm = min(ROW_TILE, s)
    has_res = res is not None

    def body(*refs):
        dy_ref, h_ref, g_ref = refs[:3]
        r_ref = refs[3] if has_res else None
        dh_ref, dg_ref = refs[-2], refs[-1]
        hv = h_ref[...]
        r = lax.rsqrt(jnp.mean(hv * hv, axis=-1, keepdims=True) + RMS_EPS)
        hn = hv * r
        dyv = dy_ref[...].astype(F32)
        u = dyv * g_ref[...]
        dh = r * (u - hn * jnp.mean(u * hn, axis=-1, keepdims=True))
        if has_res:
            dh = dh + r_ref[...]
        dh_ref[...] = dh
        part = jnp.sum(dyv * hn, axis=0, keepdims=True)

        @pl.when(pl.program_id(0) == 0)
        def _():
            dg_ref[...] = part

        @pl.when(pl.program_id(0) > 0)
        def _():
            dg_ref[...] += part

    row = pl.BlockSpec((tm, d), lambda i: (i, 0))
    vec = pl.BlockSpec((1, d), lambda i: (0, 0))
    dh, dg = pl.pallas_call(
        body, name=name, grid=(s // tm,),
        out_shape=(jax.ShapeDtypeStruct((s, d), F32), jax.ShapeDtypeStruct((1, d), F32)),
        in_specs=[row, row, vec] + ([row] if has_res else []),
        out_specs=(row, vec), compiler_params=_params(),
    )(*((dy, h, g.reshape(1, d)) + ((res,) if has_res else ())))
    return dh, dg.reshape(d)


def _loss_head(h, g, target):
    s, d = h.shape
    tm = min(ROW_TILE, s)

    def body(h_ref, g_ref, t_ref, loss_ref, dh_ref, dg_ref):
        hv = h_ref[...]
        r = lax.rsqrt(jnp.mean(hv * hv, axis=-1, keepdims=True) + RMS_EPS)
        hn = hv * r
        gv = g_ref[...]
        err = hn * gv - t_ref[...]
        rows = jnp.mean(err * err, axis=-1, keepdims=True)
        lpart = 0.5 * jnp.sum(rows, axis=0, keepdims=True) + jnp.zeros((1, 128), F32)
        dy = err * (1.0 / d)
        u = dy * gv
        dh_ref[...] = r * (u - hn * jnp.mean(u * hn, axis=-1, keepdims=True))
        gpart = jnp.sum(dy * hn, axis=0, keepdims=True)

        @pl.when(pl.program_id(0) == 0)
        def _():
            dg_ref[...] = gpart
            loss_ref[...] = lpart

        @pl.when(pl.program_id(0) > 0)
        def _():
            dg_ref[...] += gpart
            loss_ref[...] += lpart

    row = pl.BlockSpec((tm, d), lambda i: (i, 0))
    vec = pl.BlockSpec((1, d), lambda i: (0, 0))
    return pl.pallas_call(
        body, name="loss_head", grid=(s // tm,),
        out_shape=(jax.ShapeDtypeStruct((1, 128), F32), jax.ShapeDtypeStruct((s, d), F32),
                   jax.ShapeDtypeStruct((1, d), F32)),
        in_specs=[row, vec, row],
        out_specs=(pl.BlockSpec((1, 128), lambda i: (0, 0)), row, vec), compiler_params=_params(),
    )(h, g.reshape(1, d), target)


def _shift_down(x, k):
    return pltpu.roll(x, k, 0)


def _shift_up(x, k):
    return pltpu.roll(x, x.shape[0] - k, 0)


def _conv3(x, w):
    return w[2:3, :] * x + w[1:2, :] * _shift_down(x, 1) + w[0:1, :] * _shift_down(x, 2)


def _conv3_t(x, w):
    return w[2:3, :] * x + w[1:2, :] * _shift_up(x, 1) + w[0:1, :] * _shift_up(x, 2)


def _sigmoid(x):
    return 1.0 / (1.0 + jnp.exp(-x))


def _prev_map(tile, halo, col):
    return lambda i: (jnp.maximum(i * (tile // halo) - 1, 0), col)


def _next_map(tile, halo, col, nrows):
    return lambda i: (jnp.minimum((i + 1) * (tile // halo), nrows // halo - 1), col)


def _sconv_fwd(proj, w):
    s = proj.shape[0]
    t = min(ROW_TILE, s)

    def body(cur_ref, prev_ref, w_ref, o_ref):
        i = pl.program_id(0)
        prev = prev_ref[...] * (i > 0).astype(F32)
        ext = jnp.concatenate([prev, cur_ref[...]], axis=0)
        sv = ext[:, 2 * GROUP:3 * GROUP] * ext[:, 0:GROUP]
        y = ext[:, GROUP:2 * GROUP] * _conv3(sv, w_ref[...])
        o_ref[...] = y[8:].astype(o_ref.dtype)

    return pl.pallas_call(
        body, name="sconv_fwd", grid=(s // t,), out_shape=jax.ShapeDtypeStruct((s, 4 * GROUP), BF16),
        in_specs=[pl.BlockSpec((t, 3 * GROUP), lambda i: (i, 0)),
                  pl.BlockSpec((8, 3 * GROUP), _prev_map(t, 8, 0)),
                  pl.BlockSpec((3, GROUP), lambda i: (0, 0))],
        out_specs=pl.BlockSpec((t, GROUP), lambda i: (i, 0)), compiler_params=_params(),
    )(proj, proj, w)


def _sconv_bwd(proj, w, dy):
    s = proj.shape[0]
    t = min(ROW_TILE, s)
    nt = s // t

    def body(cur_ref, prev_ref, next_ref, w_ref, dy_ref, dyn_ref, dp_ref, dw_ref):
        i = pl.program_id(0)
        first = (i > 0).astype(F32)
        last = (i < nt - 1).astype(F32)
        ext = jnp.concatenate([prev_ref[...] * first, cur_ref[...], next_ref[...] * last], axis=0)
        dye = jnp.concatenate([jnp.zeros((8, GROUP), F32), dy_ref[...], dyn_ref[...] * last], axis=0)
        hv, bv, cv = ext[:, 0:GROUP], ext[:, GROUP:2 * GROUP], ext[:, 2 * GROUP:3 * GROUP]
        wv = w_ref[...]
        sv = cv * hv
        conv = _conv3(sv, wv)
        dconv = dye * bv
        ds = _conv3_t(dconv, wv)
        dp = jnp.concatenate([ds * cv, dye * conv, ds * hv], axis=1)
        dp_ref[...] = dp[8:8 + t].astype(dp_ref.dtype)
        dc = dconv[8:8 + t]
        dw = jnp.concatenate([
            jnp.sum(dc * _shift_down(sv, 2)[8:8 + t], axis=0, keepdims=True),
            jnp.sum(dc * _shift_down(sv, 1)[8:8 + t], axis=0, keepdims=True),
            jnp.sum(dc * sv[8:8 + t], axis=0, keepdims=True),
            jnp.zeros((5, GROUP), F32)], axis=0)

        @pl.when(i == 0)
        def _():
            dw_ref[...] = dw

        @pl.when(i > 0)
        def _():
            dw_ref[...] += dw

    dp, dw = pl.pallas_call(
        body, name="sconv_bwd", grid=(nt,),
        out_shape=(jax.ShapeDtypeStruct((s, N_IN_PAD), BF16), jax.ShapeDtypeStruct((8, GROUP), F32)),
        in_specs=[pl.BlockSpec((t, 3 * GROUP), lambda i: (i, 0)),
                  pl.BlockSpec((8, 3 * GROUP), _prev_map(t, 8, 0)),
                  pl.BlockSpec((8, 3 * GROUP), _next_map(t, 8, 0, s)),
                  pl.BlockSpec((3, GROUP), lambda i: (0, 0)),
                  pl.BlockSpec((t, GROUP), lambda i: (i, 0)),
                  pl.BlockSpec((8, GROUP), _next_map(t, 8, 0, s))],
        out_specs=(pl.BlockSpec((t, 3 * GROUP), lambda i: (i, 0)), pl.BlockSpec((8, GROUP), lambda i: (0, 0))),
        compiler_params=_params(),
    )(proj, proj, proj, w, dy, dy)
    return dp, dw[:3]


def _lane_window(shape):
    lane = lax.broadcasted_iota(jnp.int32, shape, 1)
    return lane, jnp.where(lane < 64, 2.0, jnp.where(lane < 128, 4.0, jnp.where(lane < 192, 8.0, 16.0)))


def _by_group(lane, s1, s2, s3, s4):
    return jnp.where(lane < 64, s1, jnp.where(lane < 128, s2, jnp.where(lane < 192, s3, s4)))


def _pool_z(ext, row0):
    s1 = ext + _shift_down(ext, 1)
    s2 = s1 + _shift_down(s1, 2)
    s3 = s2 + _shift_down(s2, 4)
    s4 = s3 + _shift_down(s3, 8)
    lane, win = _lane_window(ext.shape)
    tpos = (lax.broadcasted_iota(jnp.int32, ext.shape, 0) + (row0 - 16 + 1)).astype(F32)
    cnt = jnp.maximum(jnp.minimum(tpos, win), 1.0)
    return _by_group(lane, s1, s2, s3, s4) / cnt - ext


ANY_SPEC = pl.BlockSpec(memory_space=pl.ANY)


def _pool_fwd(proj, wbd, scale, ybuf):
    s = proj.shape[0]
    t = min(ROW_TILE, s)
    col = (COL_GATE - GROUP) // GROUP

    def body(cur_ref, prev_ref, w_ref, sc_ref, buf_ref, o_ref):
        i = pl.program_id(0)
        ext = jnp.concatenate([prev_ref[...] * (i > 0).astype(F32), cur_ref[...]], axis=0)
        z = _pool_z(ext, i * t)[16:]
        y = jnp.dot(z.astype(BF16), w_ref[...].astype(BF16), preferred_element_type=F32)
        o_ref[...] = (y * sc_ref[...]).astype(o_ref.dtype)

    return pl.pallas_call(
        body, name="pool_fwd", grid=(s // t,), out_shape=jax.ShapeDtypeStruct(ybuf.shape, ybuf.dtype),
        in_specs=[pl.BlockSpec((t, GROUP), lambda i: (i, col)),
                  pl.BlockSpec((16, GROUP), _prev_map(t, 16, col)),
                  pl.BlockSpec((GROUP, GROUP), lambda i: (0, 0)),
                  pl.BlockSpec((1, GROUP), lambda i: (0, 0)), ANY_SPEC],
        out_specs=pl.BlockSpec((t, GROUP), lambda i: (i, 3)), input_output_aliases={4: 0},
        compiler_params=_params(),
    )(proj, proj, wbd, scale.reshape(1, GROUP), ybuf)


def _pool_bwd(proj, wbd, scale, dy, dbuf):
    s = proj.shape[0]
    t = min(ROW_TILE, s)
    nt = s // t
    col = (COL_GATE - GROUP) // GROUP

    def body(cur_ref, prev_ref, w_ref, sc_ref, dy_ref, dyn_ref, buf_ref, dp_ref, dw_ref, dsc_ref):
        i = pl.program_id(0)
        ext = jnp.concatenate([prev_ref[...] * (i > 0).astype(F32), cur_ref[...]], axis=0)
        z = _pool_z(ext, i * t)[16:]
        wv = w_ref[...].astype(BF16)
        dyc = dy_ref[...]
        dye = jnp.concatenate([dyc, dyn_ref[...] * (i < nt - 1).astype(F32)], axis=0) * sc_ref[...]
        dz = lax.dot_general(dye.astype(BF16), wv, (NT, ((), ())), preferred_element_type=F32)
        lane, win = _lane_window(dz.shape)
        tpos = (lax.broadcasted_iota(jnp.int32, dz.shape, 0) + (i * t + 1)).astype(F32)
        e = dz / jnp.minimum(tpos, win)
        f1 = e + _shift_up(e, 1)
        f2 = f1 + _shift_up(f1, 2)
        f3 = f2 + _shift_up(f2, 4)
        f4 = f3 + _shift_up(f3, 8)
        dp = _by_group(lane, f1, f2, f3, f4) - dz
        dp_ref[...] = dp[:t].astype(dp_ref.dtype)
        zb = z.astype(BF16)
        y = jnp.dot(zb, wv, preferred_element_type=F32)
        dsc = jnp.sum(dyc * y, axis=0, keepdims=True)
        dw = lax.dot_general(zb, dye[:t].astype(BF16), (TN, ((), ())), preferred_element_type=F32)

        @pl.when(i == 0)
        def _():
            dw_ref[...] = dw
            dsc_ref[...] = dsc

        @pl.when(i > 0)
        def _():
            dw_ref[...] += dw
            dsc_ref[...] += dsc

    dp, dw, dsc = pl.pallas_call(
        body, name="pool_bwd", grid=(nt,),
        out_shape=(jax.ShapeDtypeStruct(dbuf.shape, dbuf.dtype), jax.ShapeDtypeStruct((GROUP, GROUP), F32),
                   jax.ShapeDtypeStruct((1, GROUP), F32)),
        in_specs=[pl.BlockSpec((t, GROUP), lambda i: (i, col)),
                  pl.BlockSpec((16, GROUP), _prev_map(t, 16, col)),
                  pl.BlockSpec((GROUP, GROUP), lambda i: (0, 0)),
                  pl.BlockSpec((1, GROUP), lambda i: (0, 0)),
                  pl.BlockSpec((t, GROUP), lambda i: (i, 3)),
                  pl.BlockSpec((16, GROUP), _next_map(t, 16, 3, s)), ANY_SPEC],
        out_specs=(pl.BlockSpec((t, GROUP), lambda i: (i, col)), pl.BlockSpec((GROUP, GROUP), lambda i: (0, 0)),
                   pl.BlockSpec((1, GROUP), lambda i: (0, 0))),
        input_output_aliases={6: 0}, compiler_params=_params(),
    )(proj, proj, wbd, scale.reshape(1, GROUP), dy, dy, dbuf)
    return dp, dw, dsc.reshape(GROUP)


FF_HALO = 16


def _ffn_gate_fwd(u0, w):
    s = u0.shape[1]
    t = min(ROW_TILE, s)

    def body(a_ref, ap_ref, g_ref, gp_ref, wa_ref, wg_ref, o_ref):
        first = (pl.program_id(1) > 0).astype(F32)
        a = _conv3(jnp.concatenate([ap_ref[...] * first, a_ref[...].astype(F32)], axis=0), wa_ref[...])[FF_HALO:]
        g = _conv3(jnp.concatenate([gp_ref[...] * first, g_ref[...].astype(F32)], axis=0), wg_ref[...])[FF_HALO:]
        o_ref[...] = (a * (g * _sigmoid(g))).astype(o_ref.dtype)

    def cur(off):
        return pl.BlockSpec((None, t, FF_SHARD), lambda j, i: (j + off, i, 0))

    def prev(off):
        return pl.BlockSpec((None, FF_HALO, FF_SHARD),
                            lambda j, i: (j + off, jnp.maximum(i * (t // FF_HALO) - 1, 0), 0))

    def wspec(off):
        return pl.BlockSpec((None, 3, FF_SHARD), lambda j, i: (j + off, 0, 0))

    return pl.pallas_call(
        body, name="ffn_gate_fwd", grid=(FF_HALF, s // t),
        out_shape=jax.ShapeDtypeStruct((FF_HALF, s, FF_SHARD), BF16),
        in_specs=[cur(0), prev(0), cur(FF_HALF), prev(FF_HALF), wspec(0), wspec(FF_HALF)],
        out_specs=pl.BlockSpec((None, t, FF_SHARD), lambda j, i: (j, i, 0)), compiler_params=_params(),
    )(u0, u0, u0, u0, w, w)


def _ffn_gate_bwd(u0, w, dact):
    s = u0.shape[1]
    t = min(ROW_TILE, s)
    nt = s // t

    def body(c_ref, p_ref, n_ref, w_ref, d_ref, dn_ref, du_ref, dw_ref):
        i = pl.program_id(1)
        first = (i > 0).astype(F32)
        last = (i < nt - 1).astype(F32)
        dext = jnp.concatenate([jnp.zeros((FF_HALO, FF_SHARD), F32), d_ref[...].astype(F32), dn_ref[...] * last],
                               axis=0)
        ext = [jnp.concatenate([p_ref[n] * first, c_ref[n].astype(F32), n_ref[n] * last], axis=0) for n in range(2)]
        a = _conv3(ext[0], w_ref[0])
        g = _conv3(ext[1], w_ref[1])
        sg = _sigmoid(g)
        silu = g * sg
        dus = (dext * silu, dext * a * (sg + silu * (1.0 - sg)))
        mine = slice(FF_HALO, FF_HALO + t)
        for n in range(2):
            du_ref[n] = _conv3_t(dus[n], w_ref[n])[mine].astype(du_ref.dtype)
            dc = dus[n][mine]
            dw = jnp.concatenate([
                jnp.sum(dc * _shift_down(ext[n], 2)[mine], axis=0, keepdims=True),
                jnp.sum(dc * _shift_down(ext[n], 1)[mine], axis=0, keepdims=True),
                jnp.sum(dc * ext[n][mine], axis=0, keepdims=True),
                jnp.zeros((5, FF_SHARD), F32)], axis=0)

            @pl.when(i == 0)
            def _(n=n, dw=dw):
                dw_ref[n] = dw

            @pl.when(i > 0)
            def _(n=n, dw=dw):
                dw_ref[n] += dw

    def pair(rows, row_map):
        return pl.BlockSpec((2, None, rows, FF_SHARD), lambda j, i: (0, j, row_map(i), 0))

    prev_row = lambda i: jnp.maximum(i * (t // FF_HALO) - 1, 0)
    next_row = lambda i: jnp.minimum((i + 1) * (t // FF_HALO), s // FF_HALO - 1)
    u2 = u0.reshape(2, FF_HALF, s, FF_SHARD)
    du, dw = pl.pallas_call(
        body, name="ffn_gate_bwd", grid=(FF_HALF, nt),
        out_shape=(jax.ShapeDtypeStruct((2, FF_HALF, s, FF_SHARD), BF16),
                   jax.ShapeDtypeStruct((2, FF_HALF, 8, FF_SHARD), F32)),
        in_specs=[pair(t, lambda i: i), pair(FF_HALO, prev_row), pair(FF_HALO, next_row), pair(3, lambda i: 0),
                  pl.BlockSpec((None, t, FF_SHARD), lambda j, i: (j, i, 0)),
                  pl.BlockSpec((None, FF_HALO, FF_SHARD), lambda j, i: (j, next_row(i), 0))],
        out_specs=(pair(t, lambda i: i), pair(8, lambda i: 0)),
        compiler_params=_params(),
    )(u2, u2, u2, w.reshape(2, FF_HALF, 3, FF_SHARD), dact, dact)
    return du.reshape(2 * FF_HALF, s, FF_SHARD), dw.reshape(2 * FF_HALF, 8, FF_SHARD)[:, :3]


def _rope_tables(positions):
    inv_freq = ROPE_THETA ** (-jnp.arange(0, ROPE_DIM, 2, dtype=F32) / ROPE_DIM)
    ang = positions.astype(F32)[:, None] * inv_freq
    cos, sin = jnp.cos(ang), jnp.sin(ang)
    s = positions.shape[0]
    half = ROPE_DIM // 2
    rest = HEAD_DIM - ROPE_DIM
    ca = jnp.concatenate([cos, cos, jnp.ones((s, rest), F32)], axis=1)
    cb = jnp.concatenate([-sin, jnp.zeros((s, HEAD_DIM - half), F32)], axis=1)
    cc = jnp.concatenate([jnp.zeros((s, half), F32), sin, jnp.zeros((s, rest), F32)], axis=1)
    return tuple(jnp.tile(tb, (1, N_HEADS)) for tb in (ca, cb, cc))


QK_WIDE = 128
LANE_CQ, LANE_CK = 64, 67
KT_ROWS = 80


def _three_bf16(x):
    hi = x.astype(BF16).astype(F32)
    mid = (x - hi).astype(BF16).astype(F32)
    lo = (x - hi - mid).astype(BF16).astype(F32)
    return hi, mid, lo


def _heads_split(proj, col, tables, c, name):
    s = proj.shape[0]
    t = min(ROW_TILE, s)
    rope = tables is not None
    wide = c is not None
    width = QK_WIDE if wide else HEAD_DIM

    def body(*refs):
        x_ref = refs[0]
        q_ref, k_ref, v_ref, kt_ref, vt_ref = refs[-5:]
        xv = x_ref[...]
        parts = [xv[:, 0:GROUP], xv[:, GROUP:2 * GROUP], xv[:, 2 * GROUP:3 * GROUP]]
        if rope:
            ca, cb, cc = refs[1][...], refs[2][...], refs[3][...]
            for n in range(2):
                p = parts[n]
                parts[n] = p * ca + pltpu.roll(p, GROUP - 8, 1) * cb + pltpu.roll(p, 8, 1) * cc
        parts[0] = parts[0] * (HEAD_DIM ** -0.5 * LOG2E)
        k_t, v_t = parts[1].T, parts[2].T
        ones_row = jnp.where(lax.broadcasted_iota(jnp.int32, (KT_ROWS - HEAD_DIM, t), 0) == 0, 1.0, 0.0)
        lane = lax.broadcasted_iota(jnp.int32, (t, QK_WIDE), 1)
        zeros = jnp.zeros((t, QK_WIDE - HEAD_DIM), F32)
        for h in range(N_HEADS):
            hs = slice(h * HEAD_DIM, (h + 1) * HEAD_DIM)
            qh, kh = parts[0][:, hs], parts[1][:, hs]
            if wide:
                terms = _three_bf16(refs[-6][:, h:h + 1] * LOG2E)
                qh = jnp.concatenate([qh, zeros], axis=1)
                kh = jnp.concatenate([kh, zeros], axis=1)
                for n in range(3):
                    qh = jnp.where(lane == LANE_CQ + n, terms[n], jnp.where(lane == LANE_CK + n, 1.0, qh))
                    kh = jnp.where(lane == LANE_CK + n, -terms[n], jnp.where(lane == LANE_CQ + n, 1.0, kh))
            q_ref[h] = qh.astype(q_ref.dtype)
            k_ref[h] = kh.astype(k_ref.dtype)
            v_ref[h] = parts[2][:, hs].astype(v_ref.dtype)
            kt_ref[h] = jnp.concatenate([k_t[hs, :], ones_row], axis=0).astype(kt_ref.dtype)
            vt_ref[h] = v_t[hs, :].astype(vt_ref.dtype)

    tab = pl.BlockSpec((t, GROUP), lambda i: (i, 0))
    qk = pl.BlockSpec((N_HEADS, t, width), lambda i: (0, i, 0))
    heads = pl.BlockSpec((N_HEADS, t, HEAD_DIM), lambda i: (0, i, 0))
    heads_t = pl.BlockSpec((N_HEADS, HEAD_DIM, t), lambda i: (0, 0, i))
    qk_shape = jax.ShapeDtypeStruct((N_HEADS, s, width), BF16)
    return pl.pallas_call(
        body, name=name, grid=(s // t,),
        out_shape=(qk_shape, qk_shape, jax.ShapeDtypeStruct((N_HEADS, s, HEAD_DIM), BF16),
                   jax.ShapeDtypeStruct((N_HEADS, KT_ROWS, s), BF16),
                   jax.ShapeDtypeStruct((N_HEADS, HEAD_DIM, s), BF16)),
        in_specs=[pl.BlockSpec((t, 3 * GROUP), lambda i: (i, col))] + ([tab, tab, tab] if rope else [])
        + ([pl.BlockSpec((t, 128), lambda i: (i, 0))] if wide else []),
        out_specs=(qk, qk, heads, pl.BlockSpec((N_HEADS, KT_ROWS, t), lambda i: (0, 0, i)), heads_t),
        compiler_params=_params(),
    )(*((proj,) + (tuple(tables) if rope else ()) + ((c,) if wide else ())))


def _heads_merge(dqt, dk, dv, tables, name, dbuf, col):
    s = dv.shape[1]
    t = min(ROW_TILE, s)
    rope = tables is not None

    wide = dk.shape[2] == QK_WIDE

    def body(*refs):
        o_ref = refs[n_in + 1]
        dq = jnp.concatenate([refs[0][h, :HEAD_DIM, :] for h in range(N_HEADS)], axis=0).T
        parts = [dq] + [jnp.concatenate([r[h][:, :HEAD_DIM] for h in range(N_HEADS)], axis=1) for r in refs[1:3]]
        parts[0] = parts[0] * (HEAD_DIM ** -0.5)
        parts[1] = parts[1] * (1.0 / LOG2E)
        if rope:
            ca, cb, cc = refs[3][...], refs[4][...], refs[5][...]
            for n in range(2):
                p = parts[n]
                parts[n] = p * ca + pltpu.roll(p * cb, 8, 1) + pltpu.roll(p * cc, GROUP - 8, 1)
        o_ref[...] = jnp.concatenate(parts, axis=1).astype(o_ref.dtype)
        if wide:
            over_keys = jnp.concatenate([refs[0][h, HEAD_DIM:HEAD_DIM + 8, :] for h in range(N_HEADS)]
                                        + [jnp.zeros((128 - 8 * N_HEADS, t), F32)], axis=0).T
            lane = lax.broadcasted_iota(jnp.int32, (t, 128), 1)
            dc = jnp.zeros((t, 128), F32)
            for h in range(N_HEADS):
                dc = jnp.where(lane == h, over_keys[:, 8 * h:8 * h + 1] - refs[1][h][:, LANE_CK:LANE_CK + 1], dc)
            refs[n_in + 2][...] = dc

    tab = pl.BlockSpec((t, GROUP), lambda i: (i, 0))
    heads = pl.BlockSpec((N_HEADS, t, HEAD_DIM), lambda i: (0, i, 0))
    n_in = 6 if rope else 3
    dspec = pl.BlockSpec((t, 3 * GROUP), lambda i: (i, col))
    dshape = jax.ShapeDtypeStruct(dbuf.shape, dbuf.dtype)
    return pl.pallas_call(
        body, name=name, grid=(s // t,),
        out_shape=(dshape, jax.ShapeDtypeStruct((s, 128), F32)) if wide else dshape,
        in_specs=[pl.BlockSpec((N_HEADS, KT_ROWS, t), lambda i: (0, 0, i)),
                  pl.BlockSpec((N_HEADS, t, dk.shape[2]), lambda i: (0, i, 0)), heads]
        + ([tab, tab, tab] if rope else []) + [ANY_SPEC],
        out_specs=(dspec, pl.BlockSpec((t, 128), lambda i: (i, 0))) if wide else dspec,
        input_output_aliases={n_in: 0}, compiler_params=_params(),
    )(*((dqt, dk, dv) + (tuple(tables) if rope else ()) + (dbuf,)))


def _log_sigmoid(x):
    return jnp.minimum(x, 0.0) - jnp.log(1.0 + jnp.exp(-jnp.abs(x)))


def _scan_rows(x, reverse):
    n = x.shape[0]
    row = lax.broadcasted_iota(jnp.int32, x.shape, 0)
    k = 1
    while k < n:
        if reverse:
            x = x + jnp.where(row < n - k, _shift_up(x, k), 0.0)
        else:
            x = x + jnp.where(row >= k, _shift_down(x, k), 0.0)
        k *= 2
    return x


def _gate_cumsum(proj, bias):
    s = proj.shape[0]
    col = COL_GATE // 128

    def body(z_ref, b_ref, c_ref):
        c_ref[...] = _scan_rows(_log_sigmoid(z_ref[...] + b_ref[...]), False)

    return pl.pallas_call(
        body, name="gate_cumsum", grid=(1,), out_shape=jax.ShapeDtypeStruct((s, 128), F32),
        in_specs=[pl.BlockSpec((s, 128), lambda i: (0, col)), pl.BlockSpec((1, 128), lambda i: (0, 0))],
        out_specs=pl.BlockSpec((s, 128), lambda i: (0, 0)), compiler_params=_params(),
    )(proj, bias)


def _gate_cumsum_bwd(proj, bias, dc, dbuf):
    s = proj.shape[0]
    col = COL_GATE // 128

    def body(z_ref, b_ref, dc_ref, buf_ref, dz_ref, db_ref):
        dlogf = _scan_rows(dc_ref[...], True)
        dz = dlogf * _sigmoid(-(z_ref[...] + b_ref[...]))
        dz_ref[...] = dz.astype(dz_ref.dtype)
        db_ref[...] = jnp.sum(dz, axis=0, keepdims=True)

    return pl.pallas_call(
        body, name="gate_cumsum_bwd", grid=(1,),
        out_shape=(jax.ShapeDtypeStruct(dbuf.shape, dbuf.dtype), jax.ShapeDtypeStruct((1, 128), F32)),
        in_specs=[pl.BlockSpec((s, 128), lambda i: (0, col)), pl.BlockSpec((1, 128), lambda i: (0, 0)),
                  pl.BlockSpec((s, 128), lambda i: (0, 0)), ANY_SPEC],
        out_specs=(pl.BlockSpec((s, 128), lambda i: (0, col)), pl.BlockSpec((1, 128), lambda i: (0, 0))),
        input_output_aliases={3: 0}, compiler_params=_params(),
    )(proj, bias, dc, dbuf)


DIL_REACH = 2048


def _pair_weight(mode, d):
    if mode == "fox":
        return jnp.where(d >= 0, 1.0, 0.0)
    w1 = jnp.where(jnp.abs(d - 64) <= 64, 1.0, 0.0)
    w2 = jnp.where((d & 3) == 0, jnp.where(jnp.abs(d - 256) <= 256, 1.0, 0.0), 0.0)
    w3 = jnp.where((d & 15) == 0, jnp.where(jnp.abs(d - 1024) <= 1024, 1.0, 0.0), 0.0)
    return w1 + w2 + w3


def _bias_tables(mode, tq, tk):
    nb = 2 if mode == "fox" else DIL_REACH // tk + 1
    n = lax.broadcasted_iota(jnp.int32, (nb, tk, tq), 0)
    key = lax.broadcasted_iota(jnp.int32, (nb, tk, tq), 1)
    query = lax.broadcasted_iota(jnp.int32, (nb, tk, tq), 2)
    w = _pair_weight(mode, n * tk + query - key)
    return jnp.where(w > 0.0, jnp.log2(jnp.maximum(w, 1.0)), NEG)


LOG2E = 1.4426950408889634
M_INIT = -1e29


def _first_key_chunk(mode, q0, tk):
    if mode == "fox":
        return 0
    return jnp.maximum(q0 - DIL_REACH, 0) // tk


def _attention_fwd(mode, q, k, vt, tab_t, ybuf, col):
    s, width = q.shape[1], q.shape[2]
    tq = min(ATT_TQ, s)
    tk = tq
    nb = tab_t.shape[0]

    def body(q_ref, k_ref, vt_ref, tab_ref, buf_ref, y_ref, o_ref, lse_ref):
        i = pl.program_id(0)
        lo = _first_key_chunk(mode, i * tq, tk)

        def step(c, carry):
            k0 = pl.multiple_of(c * tk, tk)
            tab = tab_ref[jnp.minimum(i - c, nb - 1)]
            scores = [lax.dot_general(k_ref[h, pl.ds(k0, tk), :], q_ref[h], (NT, ((), ())),
                                      preferred_element_type=F32) for h in range(N_HEADS)]
            stats, probs = [], []
            for h in range(N_HEADS):
                m, l = carry[3 * h:3 * h + 2]
                sc = scores[h] + tab
                m_new = jnp.maximum(m, jnp.max(sc, axis=0, keepdims=True))
                alpha = jnp.exp2(m - m_new)
                p = jnp.exp2(sc - m_new)
                stats.append((m_new, alpha * l + jnp.sum(p, axis=0, keepdims=True), alpha))
                probs.append(p.astype(BF16))
            pv = [jnp.dot(vt_ref[h, :, pl.ds(k0, tk)], probs[h], preferred_element_type=F32) for h in range(N_HEADS)]
            new = []
            for h in range(N_HEADS):
                m_new, l, alpha = stats[h]
                new += [m_new, l, alpha * carry[3 * h + 2] + pv[h]]
            return tuple(new)

        start = (jnp.full((1, tq), M_INIT, F32), jnp.zeros((1, tq), F32), jnp.zeros((HEAD_DIM, tq), F32))
        done = lax.fori_loop(lo, i + 1, step, start * N_HEADS)
        outs = []
        for h in range(N_HEADS):
            m, l, acc = done[3 * h:3 * h + 3]
            outs.append(acc / l)
            lse_ref[h] = m + jnp.log2(l)
        out = jnp.concatenate(outs, axis=0).T
        y_ref[...] = out.astype(y_ref.dtype)
        o_ref[...] = out

    rowspec = pl.BlockSpec((N_HEADS, 1, tq), lambda i: (0, 0, i))
    return pl.pallas_call(
        body, name="attention_fwd_" + mode, grid=(s // tq,),
        out_shape=(jax.ShapeDtypeStruct(ybuf.shape, ybuf.dtype), jax.ShapeDtypeStruct((s, GROUP), F32),
                   jax.ShapeDtypeStruct((N_HEADS, 1, s), F32)),
        in_specs=[pl.BlockSpec((N_HEADS, tq, width), lambda i: (0, i, 0)),
                  pl.BlockSpec((N_HEADS, s, width), lambda i: (0, 0, 0)),
                  pl.BlockSpec((N_HEADS, HEAD_DIM, s), lambda i: (0, 0, 0)),
                  pl.BlockSpec((nb, tk, tq), lambda i: (0, 0, 0)), ANY_SPEC],
        out_specs=(pl.BlockSpec((tq, GROUP), lambda i: (i, col)), pl.BlockSpec((tq, GROUP), lambda i: (i, 0)),
                   rowspec),
        input_output_aliases={4: 0}, compiler_params=_params(),
    )(q, k, vt, tab_t, ybuf)


def _attention_delta(o, do, col):
    s = o.shape[0]
    t = min(ROW_TILE, s)

    def body(o_ref, do_ref, delta_ref, dob_ref):
        dov = do_ref[...]
        prod_t = (o_ref[...] * dov).T
        for h in range(N_HEADS):
            hs = slice(h * HEAD_DIM, (h + 1) * HEAD_DIM)
            delta_ref[h] = jnp.sum(prod_t[hs, :], axis=0, keepdims=True)
            dob_ref[h] = dov[:, hs].astype(dob_ref.dtype)

    return pl.pallas_call(
        body, name="attention_delta", grid=(s // t,),
        out_shape=(jax.ShapeDtypeStruct((N_HEADS, 1, s), F32), jax.ShapeDtypeStruct((N_HEADS, s, HEAD_DIM), BF16)),
        in_specs=[pl.BlockSpec((t, GROUP), lambda i: (i, 0)), pl.BlockSpec((t, GROUP), lambda i: (i, col))],
        out_specs=(pl.BlockSpec((N_HEADS, 1, t), lambda i: (0, 0, i)),
                   pl.BlockSpec((N_HEADS, t, HEAD_DIM), lambda i: (0, i, 0))),
        compiler_params=_params(),
    )(o, do)


def _attention_bwd(mode, q, k, v, kt, tab_t, dob, lse, delta):
    s, width = q.shape[1], q.shape[2]
    tq = min(ATT_TQ, s)
    tk = tq
    nq = s // tq
    nb = tab_t.shape[0]

    def body(q_ref, k_ref, v_ref, kt_ref, tab_ref, dob_ref, lse_ref, delta_ref, dqt_ref, dk_ref, dv_ref):
        i = pl.program_id(0)

        @pl.when(i == 0)
        def _():
            dqt_ref[...] = jnp.zeros_like(dqt_ref)

        hi = nq if mode == "fox" else jnp.minimum((i * tk + tk - 1 + DIL_REACH) // tq + 1, nq)
        for h0 in range(0, N_HEADS, BWD_HEADS):
            heads = range(h0, h0 + BWD_HEADS)

            def step(c, carry, heads=heads):
                q0 = pl.multiple_of(c * tq, tq)
                qs = pl.ds(q0, tq)
                tab = tab_ref[jnp.minimum(c - i, nb - 1)]
                qv = [q_ref[h, qs, :] for h in heads]
                dov = [dob_ref[h, qs, :] for h in heads]
                sc = [lax.dot_general(k_ref[h], qv[n], (NT, ((), ())), preferred_element_type=F32)
                      for n, h in enumerate(heads)]
                dp = [lax.dot_general(v_ref[h], dov[n], (NT, ((), ())), preferred_element_type=F32)
                      for n, h in enumerate(heads)]
                pb, dsb = [], []
                for n, h in enumerate(heads):
                    p = jnp.exp2(sc[n] + tab - lse_ref[h, :, qs])
                    pb.append(p.astype(BF16))
                    dsb.append((p * (dp[n] - delta_ref[h, :, qs])).astype(BF16))
                new = []
                for n, h in enumerate(heads):
                    new += [carry[2 * n] + jnp.dot(dsb[n], qv[n], preferred_element_type=F32),
                            carry[2 * n + 1] + jnp.dot(pb[n], dov[n], preferred_element_type=F32)]
                for n, h in enumerate(heads):
                    dqt_ref[h, :, qs] += jnp.dot(kt_ref[h], dsb[n], preferred_element_type=F32)
                return tuple(new)

            start = (jnp.zeros((tk, width), F32), jnp.zeros((tk, HEAD_DIM), F32))
            done = lax.fori_loop(i, hi, step, start * BWD_HEADS)
            for n, h in enumerate(heads):
                dk_ref[h] = done[2 * n]
                dv_ref[h] = done[2 * n + 1]

    def full(shape):
        return pl.BlockSpec(shape, lambda i: (0, 0, 0))

    kblk = pl.BlockSpec((N_HEADS, tk, width), lambda i: (0, i, 0))
    vblk = pl.BlockSpec((N_HEADS, tk, HEAD_DIM), lambda i: (0, i, 0))
    return pl.pallas_call(
        body, name="attention_bwd_" + mode, grid=(s // tk,),
        out_shape=(jax.ShapeDtypeStruct((N_HEADS, KT_ROWS, s), F32), jax.ShapeDtypeStruct((N_HEADS, s, width), F32),
                   jax.ShapeDtypeStruct((N_HEADS, s, HEAD_DIM), F32)),
        in_specs=[full((N_HEADS, s, width)), kblk, vblk, pl.BlockSpec((N_HEADS, KT_ROWS, tk), lambda i: (0, 0, i)),
                  full((nb, tk, tq)), full((N_HEADS, s, HEAD_DIM)), full((N_HEADS, 1, s)), full((N_HEADS, 1, s))],
        out_specs=(full((N_HEADS, KT_ROWS, s)), kblk, vblk),
        compiler_params=_params(),
    )(q, k, v, kt, tab_t, dob, lse, delta)


def _xattn_fwd(qx, kvm):
    s = qx.shape[0]
    t = min(ROW_TILE, s)

    def body(q_ref, kv_ref, o_ref):
        heads = range(XA_HEADS)
        sc = [lax.dot_general(q_ref[:, h * XA_DIM:(h + 1) * XA_DIM].astype(BF16), kv_ref[h].astype(BF16),
                              (NT, ((), ())), preferred_element_type=F32) * (XA_DIM ** -0.5) for h in heads]
        probs = []
        for h in heads:
            e = jnp.exp(sc[h] - jnp.max(sc[h], axis=-1, keepdims=True))
            probs.append((e / jnp.sum(e, axis=-1, keepdims=True)).astype(BF16))
        outs = [jnp.dot(probs[h], kv_ref[XA_HEADS + h].astype(BF16), preferred_element_type=F32) for h in heads]
        for h in heads:
            o_ref[:, h * XA_DIM:(h + 1) * XA_DIM] = outs[h].astype(o_ref.dtype)

    return pl.pallas_call(
        body, name="xattn_fwd", grid=(s // t,), out_shape=jax.ShapeDtypeStruct((s, D_MODEL), BF16),
        in_specs=[pl.BlockSpec((t, D_MODEL), lambda i: (i, 0)),
                  pl.BlockSpec((2 * XA_HEADS, MEM_LEN, XA_DIM), lambda i: (0, 0, 0))],
        out_specs=pl.BlockSpec((t, D_MODEL), lambda i: (i, 0)), compiler_params=_params(),
    )(qx, kvm)


def _xattn_bwd(qx, kvm, do):
    s = qx.shape[0]
    t = min(ROW_TILE, s)

    def body(q_ref, kv_ref, do_ref, dq_ref, dkv_ref):
        i = pl.program_id(0)
        heads = range(XA_HEADS)
        qv = [q_ref[:, h * XA_DIM:(h + 1) * XA_DIM].astype(BF16) for h in heads]
        dov = [do_ref[:, h * XA_DIM:(h + 1) * XA_DIM].astype(BF16) for h in heads]
        kv = [kv_ref[h].astype(BF16) for h in heads]
        sc = [lax.dot_general(qv[h], kv[h], (NT, ((), ())), preferred_element_type=F32) * (XA_DIM ** -0.5)
              for h in heads]
        dp = [lax.dot_general(dov[h], kv_ref[XA_HEADS + h].astype(BF16), (NT, ((), ())), preferred_element_type=F32)
              for h in heads]
        pb, ds = [], []
        for h in heads:
            e = jnp.exp(sc[h] - jnp.max(sc[h], axis=-1, keepdims=True))
            p = e / jnp.sum(e, axis=-1, keepdims=True)
            pb.append(p.astype(BF16))
            ds.append((p * (dp[h] - jnp.sum(p * dp[h], axis=-1, keepdims=True)) * (XA_DIM ** -0.5)).astype(BF16))
        dq = [jnp.dot(ds[h], kv[h], preferred_element_type=F32) for h in heads]
        dk = [lax.dot_general(ds[h], qv[h], (TN, ((), ())), preferred_element_type=F32) for h in heads]
        dv = [lax.dot_general(pb[h], dov[h], (TN, ((), ())), preferred_element_type=F32) for h in heads]
        for h in heads:
            dq_ref[:, h * XA_DIM:(h + 1) * XA_DIM] = dq[h].astype(dq_ref.dtype)

        @pl.when(i == 0)
        def _():
            for h in heads:
                dkv_ref[h] = dk[h]
                dkv_ref[XA_HEADS + h] = dv[h]

        @pl.when(i > 0)
        def _():
            for h in heads:
                dkv_ref[h] += dk[h]
                dkv_ref[XA_HEADS + h] += dv[h]

    row = pl.BlockSpec((t, D_MODEL), lambda i: (i, 0))
    kvs = pl.BlockSpec((2 * XA_HEADS, MEM_LEN, XA_DIM), lambda i: (0, 0, 0))
    return pl.pallas_call(
        body, name="xattn_bwd", grid=(s // t,),
        out_shape=(jax.ShapeDtypeStruct((s, D_MODEL), BF16),
                   jax.ShapeDtypeStruct((2 * XA_HEADS, MEM_LEN, XA_DIM), F32)),
        in_specs=[row, kvs, row], out_specs=(row, kvs), compiler_params=_params(),
    )(qx, kvm, do)


def _adamw(parts, owns, me, w, m, v, name):
    nl, r, c = w.shape
    tr = r
    for cand in (512, 352, 256, 176, 128, 64, 32, 16, 8):
        if r % cand == 0 and r > cand and N_DEV * cand * c * 4 <= ADAMW_BLOCK_BYTES:
            tr = cand
            break
    nt = r // tr
    per_layer = N_DEV + (1 if owns is not None else 0)

    def body(me_ref, *refs):
        w_ref, m_ref, v_ref, g_ref, d_ref, nm_ref, nv_ref = refs[nl * per_layer:]
        layer = pl.program_id(0)
        g = None
        for l in range(nl):
            p_refs = refs[l * per_layer:(l + 1) * per_layer]
            gl = None
            for d in range(N_DEV):
                term = p_refs[d][...].astype(F32)
                if owns is not None:
                    term = jnp.where(me_ref[0] == d, p_refs[N_DEV][...].astype(F32), term)
                gl = term if gl is None else gl + term
            g = gl if g is None else jnp.where(layer == l, gl, g)
        mn = ADAM_B1 * m_ref[...] + (1.0 - ADAM_B1) * g
        vn = ADAM_B2 * v_ref[...] + (1.0 - ADAM_B2) * (g * g)
        m_hat = mn / (1.0 - ADAM_B1 ** ADAM_STEP)
        v_hat = vn / (1.0 - ADAM_B2 ** ADAM_STEP)
        g_ref[...] = g
        d_ref[...] = -ADAM_LR * (m_hat / (jnp.sqrt(v_hat) + ADAM_EPS) + ADAM_WD * w_ref[...])
        nm_ref[...] = mn
        nv_ref[...] = vn

    def rows(l, ll, i):
        return jnp.where(ll == l, i, jnp.where(ll < l, 0, nt - 1))

    def part_spec(l, d):
        if owns is None:
            return pl.BlockSpec((None, tr, c), lambda ll, i, me_ref: (d, rows(l, ll, i), 0))
        return pl.BlockSpec((None, tr, c),
                            lambda ll, i, me_ref: (jnp.where(me_ref[0] == d, (d + 1) % N_DEV, d), rows(l, ll, i), 0))

    def own_spec(l):
        return pl.BlockSpec((None, tr, c), lambda ll, i, me_ref: (me_ref[0], rows(l, ll, i), 0))

    in_specs, operands = [], []
    for l in range(nl):
        in_specs += [part_spec(l, d) for d in range(N_DEV)]
        operands += [parts[l]] * N_DEV
        if owns is not None:
            in_specs.append(own_spec(l))
            operands.append(owns[l])
    blk = pl.BlockSpec((None, tr, c), lambda ll, i, me_ref: (ll, i, 0))
    shp = jax.ShapeDtypeStruct((nl, r, c), F32)
    return pl.pallas_call(
        body, name=name, out_shape=(shp, shp, shp, shp),
        grid_spec=pltpu.PrefetchScalarGridSpec(
            num_scalar_prefetch=1, grid=(nl, nt), in_specs=in_specs + [blk, blk, blk],
            out_specs=(blk, blk, blk, blk)),
        compiler_params=_params(),
    )(me.reshape(1), *operands, w, m, v)


GROUPS = {"in": ("w_in",), "rest": ("w_out", "w_xq", "w_xo", "w_xkv", "w_up", "w_down")}
FULL_SHAPES = {"w_in": (D_MODEL, N_IN_PAD), "w_out": (D_MODEL, D_MODEL), "w_xq": (D_MODEL, D_MODEL),
               "w_xo": (D_MODEL, D_MODEL), "w_xkv": (N_DEV, D_MODEL, 2 * D_MODEL // N_DEV),
               "w_up": (N_DEV, FF_SHARD, D_MODEL), "w_down": (FF_HALF, FF_SHARD, D_MODEL)}
PIECE_SHAPES = {"w_in": (N_DEV, D_MODEL // N_DEV, N_IN_PAD), "w_out": (N_DEV, D_MODEL // N_DEV, D_MODEL),
                "w_xq": (N_DEV, D_MODEL // N_DEV, D_MODEL), "w_xo": (N_DEV, D_MODEL // N_DEV, D_MODEL),
                "w_xkv": (N_DEV, D_MODEL, 2 * D_MODEL // N_DEV), "w_up": (N_DEV, FF_SHARD, D_MODEL),
                "w_down": (N_DEV, D_FF // N_DEV, D_MODEL)}
GATHER_GROUPS = {"in": ("w_in",), "mid": ("w_out", "w_xq", "w_xo", "w_xkv"), "ffn": ("w_up", "w_down")}
CONV_WORDS = 8192


class _GatheredWeights:
    def __init__(self, states, layer):
        self.states, self.layer, self.full, self.extra = dict(states), layer, {}, None

    def need(self, group, after):
        if group in self.states:
            got, _ = _exchange_wait(self.states.pop(group), after, "gather_%s_wait_%d" % (group, self.layer))
            for name, g in zip(GATHER_GROUPS[group], got):
                self.full[name] = g.reshape(FULL_SHAPES[name])
            self.extra = got[len(GATHER_GROUPS[group]):]

    def __getitem__(self, name):
        return self.full[name]


def _relay_in_cols(w):
    pad = jnp.zeros(w.shape[:-1] + (N_IN_PAD - N_IN,), w.dtype)
    return jnp.concatenate([w[..., :2304], w[..., 2308:N_IN], w[..., 2304:2308], pad], axis=-1)


def _unrelay_in_cols(w):
    return jnp.concatenate([w[..., :2304], w[..., COL_GATE:COL_GATE + 4], w[..., 2304:COL_GATE]], axis=-1)


def _layer_fwd(h, memv, w, sm, tables, rest_arrived):
    sv = {"h0": h}
    s = h.shape[0]
    tm, tb = min(SLAB_TILE, s), min(MM_TILE, s)
    w.need("in", h)
    proj, xn = _norm_matmul(h, sm["g_mix"], w["w_in"], (s, N_IN_PAD), grid=(s // tm, 1),
                            b_spec=pl.BlockSpec((D_MODEL, N_IN_PAD), lambda i, j: (0, 0)),
                            o_spec=pl.BlockSpec((tm, N_IN_PAD), lambda i, j: (i, 0)), name="norm_mm_in")
    sv["xn"], sv["proj"] = xn, proj
    ycat = _sconv_fwd(proj, sm["w_sconv"])
    qd, kd, vd, ktd, vtd = _heads_split(proj, 1, tables["rope"], None, "split_dil")
    ycat, ob, lse_b = _attention_fwd("dil", qd, kd, vtd, tables["dil"], ycat, 1)
    sv["dil"] = (qd, kd, vd, ktd, ob, lse_b)
    c = _gate_cumsum(proj, sm["b_forget_pad"])
    qf, kf, vf, ktf, vtf = _heads_split(proj, 2, None, c, "split_fox")
    ycat, oc, lse_c = _attention_fwd("fox", qf, kf, vtf, tables["fox"], ycat, 2)
    sv["fox"] = (qf, kf, vf, ktf, oc, lse_c)
    ycat = _pool_fwd(proj, sm["w_pool_bd"], sm["pool_scale"], ycat)
    sv["ycat"] = ycat
    w.need("mid", ycat)
    h1 = _mm_nn(ycat, w["w_out"], "mm_out", res=h)
    sv["h1"] = h1
    memn = _rms_fwd(memv, sm["g_mem"], "rms_mem")
    qx, xq = _norm_matmul(h1, sm["g_xa"], w["w_xq"], (s, D_MODEL), grid=(s // tb, 1),
                          b_spec=pl.BlockSpec((D_MODEL, D_MODEL), lambda i, j: (0, 0)),
                          o_spec=pl.BlockSpec((tb, D_MODEL), lambda i, j: (i, 0)), name="norm_mm_xq",
                          out_dtype=BF16)
    kvm = _matmul(memn, w["w_xkv"], (N_DEV, MEM_LEN, XA_DIM), grid=(N_DEV, 1, 1),
                  a_spec=pl.BlockSpec((MEM_LEN, D_MODEL), lambda i, j, r: (0, 0)),
                  b_spec=pl.BlockSpec((None, D_MODEL, XA_DIM), lambda i, j, r: (i, 0, 0)),
                  o_spec=pl.BlockSpec((None, MEM_LEN, XA_DIM), lambda i, j, r: (i, 0, 0)),
                  dims=NN, nred=1, name="mm_xkv")
    ox = _xattn_fwd(qx, kvm)
    sv.update(xq=xq, memn=memn, qx=qx, kvm=kvm, ox=ox)
    h2 = _mm_nn(ox, w["w_xo"], "mm_xo", res=h1)
    sv["h2"] = h2
    w.need("ffn", ox)
    u0, xf = _norm_matmul(h2, sm["g_ffn"] + rest_arrived(w["w_down"]), w["w_up"], (N_DEV, s, FF_SHARD),
                          grid=(s // tb, N_DEV // 4),
                          b_spec=pl.BlockSpec((4, FF_SHARD, D_MODEL), lambda i, j: (j, 0, 0)),
                          o_spec=pl.BlockSpec((4, tb, FF_SHARD), lambda i, j: (j, i, 0)), name="norm_mm_up",
                          out_dtype=BF16, dims=NT)
    act = _ffn_gate_fwd(u0, sm["w_ffconv"])
    sv.update(xf=xf, u0=u0, act=act)
    ospec = pl.BlockSpec((tb, D_MODEL), lambda i, j, r: (i, 0))
    h3 = _matmul(act, w["w_down"], (s, D_MODEL), grid=(s // tb, 1, 1),
                 a_spec=pl.BlockSpec((FF_HALF, tb, FF_SHARD), lambda i, j, r: (0, i, 0)),
                 b_spec=pl.BlockSpec((FF_HALF, FF_SHARD, D_MODEL), lambda i, j, r: (0, 0, 0)),
                 o_spec=ospec, dims=NN, nred=1, slabs=FF_HALF, name="mm_down", res=h2, res_spec=ospec)
    return h3, sv


def _layer_bwd(dh3, memv, w, sm, tables, sv, rest_ready, in_ready):
    s = dh3.shape[0]
    tm, tb = min(ROW_TILE, s), min(MM_TILE, s)
    big, small = {}, {}
    ts = max(s // 2, 1)
    dact = _matmul(dh3, w["w_down"], (FF_HALF, s, FF_SHARD), grid=(s // tb, 1, 1),
                   a_spec=pl.BlockSpec((tb, D_MODEL), lambda i, j, r: (i, 0)),
                   b_spec=pl.BlockSpec((FF_HALF, FF_SHARD, D_MODEL), lambda i, j, r: (0, 0, 0)),
                   o_spec=pl.BlockSpec((FF_HALF, tb, FF_SHARD), lambda i, j, r: (0, i, 0)),
                   dims=NT, nred=1, out_slabs=FF_HALF, name="mm_dact", out_dtype=BF16)
    big["w_down"] = _matmul(sv["act"], dh3, (FF_HALF, FF_SHARD, D_MODEL), grid=(FF_HALF, 1, s // ts),
                            a_spec=pl.BlockSpec((None, ts, FF_SHARD), lambda i, j, r: (i, r, 0)),
                            b_spec=pl.BlockSpec((ts, D_MODEL), lambda i, j, r: (r, 0)),
                            o_spec=pl.BlockSpec((None, FF_SHARD, D_MODEL), lambda i, j, r: (i, 0, 0)),
                            dims=TN, nred=s // ts, name="mm_dw_down", out_dtype=GRAD_DTYPE)
    du0, small["w_ffconv"] = _ffn_gate_bwd(sv["u0"], sm["w_ffconv"], dact)
    dh2, small["g_ffn"] = _matmul_rms_bwd(du0, w["w_up"], sv["h2"], sm["g_ffn"], dh3, "mm_dxf_rms_bwd",
                                          tm=SLAB_TILE // 2, dims=NN)
    big["w_up"] = _matmul(du0, sv["xf"], (N_DEV, FF_SHARD, D_MODEL), grid=(N_DEV, 1, 1),
                          a_spec=pl.BlockSpec((None, s, FF_SHARD), lambda i, j, r: (i, 0, 0)),
                          b_spec=pl.BlockSpec((s, D_MODEL), lambda i, j, r: (0, 0)),
                          o_spec=pl.BlockSpec((None, FF_SHARD, D_MODEL), lambda i, j, r: (i, 0, 0)),
                          dims=TN, nred=1, name="mm_dw_up", out_dtype=GRAD_DTYPE)
    dox = _mm_nt(dh2, w["w_xo"], "mm_dox", out_dtype=BF16)
    big["w_xo"] = _mm_tn(sv["ox"], dh2, "mm_dw_xo")
    dqx, dkvm = _xattn_bwd(sv["qx"], sv["kvm"], dox)
    big["w_xq"] = _mm_tn(sv["xq"], dqx, "mm_dw_xq")
    big["w_xkv"] = _matmul(sv["memn"], dkvm, (N_DEV, D_MODEL, XA_DIM), grid=(N_DEV, 1, 1),
                           a_spec=pl.BlockSpec((MEM_LEN, D_MODEL), lambda i, j, r: (0, 0)),
                           b_spec=pl.BlockSpec((None, MEM_LEN, XA_DIM), lambda i, j, r: (i, 0, 0)),
                           o_spec=pl.BlockSpec((None, D_MODEL, XA_DIM), lambda i, j, r: (i, 0, 0)),
                           dims=TN, nred=1, name="mm_dw_xkv", out_dtype=GRAD_DTYPE)
    dmemn = _matmul(dkvm, w["w_xkv"], (MEM_LEN, D_MODEL), grid=(1, 1, 1),
                    a_spec=pl.BlockSpec((N_DEV, MEM_LEN, XA_DIM), lambda i, j, r: (0, 0, 0)),
                    b_spec=pl.BlockSpec((N_DEV, D_MODEL, XA_DIM), lambda i, j, r: (0, 0, 0)),
                    o_spec=pl.BlockSpec((MEM_LEN, D_MODEL), lambda i, j, r: (0, 0)),
                    dims=NT, nred=1, slabs=N_DEV, name="mm_dmemn")
    _, small["g_mem"] = _rms_bwd(dmemn, memv, sm["g_mem"], None, "rms_mem_bwd")
    dh1, small["g_xa"] = _matmul_rms_bwd(dqx, w["w_xq"], sv["h1"], sm["g_xa"], dh2, "mm_dxq_rms_bwd", tm=MM_TILE)
    big["w_out"] = _mm_tn(sv["ycat"], dh1, "mm_dw_out")
    dycat = _mm_nt(dh1, w["w_out"] + rest_ready(big, small).astype(BF16), "mm_dycat")
    proj = sv["proj"]
    dproj, small["w_sconv"] = _sconv_bwd(proj, sm["w_sconv"], dycat)
    qd, kd, vd, ktd, ob, lse_b = sv["dil"]
    delta, dob = _attention_delta(ob, dycat, 1)
    dqt, dk, dv = _attention_bwd("dil", qd, kd, vd, ktd, tables["dil"], dob, lse_b, delta)
    dproj = _heads_merge(dqt, dk, dv, tables["rope"], "merge_dil", dproj, 1)
    qf, kf, vf, ktf, oc, lse_c = sv["fox"]
    delta, dob = _attention_delta(oc, dycat, 2)
    dqt, dk, dv = _attention_bwd("fox", qf, kf, vf, ktf, tables["fox"], dob, lse_c, delta)
    dproj, dc = _heads_merge(dqt, dk, dv, None, "merge_fox", dproj, 2)
    dproj, dbias = _gate_cumsum_bwd(proj, sm["b_forget_pad"], dc, dproj)
    small["b_forget"] = dbias[0, :N_HEADS]
    dproj, dwbd, small["pool_scale"] = _pool_bwd(proj, sm["w_pool_bd"], sm["pool_scale"], dycat, dproj)
    small["w_pool"] = jnp.stack([dwbd[64 * g:64 * (g + 1), 64 * g:64 * (g + 1)] for g in range(4)])
    big["w_in"] = _mm_tn(sv["xn"], dproj, "mm_dw_in")
    dh0, small["g_mix"] = _matmul_rms_bwd(dproj, w["w_in"], sv["h0"], sm["g_mix"] + in_ready(big, small), dh1,
                                          "mm_dxn_rms_bwd")
    return dh0, big, small


SMALL_NAMES = ("g_mix", "b_forget", "w_pool", "pool_scale", "g_xa", "g_mem", "g_ffn", "w_sconv", "w_ffconv")
SMALL_WITH = {"rest": ("w_ffconv", "g_ffn", "g_mem", "g_xa"),
              "in": ("w_sconv", "b_forget", "pool_scale", "w_pool")}
SMALL_SHAPES = {"w_sconv": (3, GROUP), "w_ffconv": (N_DEV, 3, FF_SHARD)}
WEIGHT_NAMES = ("g_mix", "w_in", "b_forget", "w_sconv", "w_pool", "pool_scale", "w_out", "g_xa", "g_mem", "w_xq",
                "w_xkv", "w_xo", "g_ffn", "w_up", "w_ffconv", "w_down", "g_final")


def _block_diag(w_pool):
    z = jnp.zeros((64, 64), F32)
    return jnp.concatenate(
        [jnp.concatenate([w_pool[g] if c == g else z for c in range(4)], axis=1) for g in range(4)], axis=0)


def kernel(x, mem, positions, g_mix, w_in, b_forget, w_sconv, w_pool, pool_scale, w_out, g_xa, g_mem, w_xq, w_xkv, w_xo, g_ffn, w_up, w_ffconv, w_down, g_final, loss_target, m_g_mix, m_w_in, m_b_forget, m_w_sconv, m_w_pool, m_pool_scale, m_w_out, m_g_xa, m_g_mem, m_w_xq, m_w_xkv, m_w_xo, m_g_ffn, m_w_up, m_w_ffconv, m_w_down, m_g_final, v_g_mix, v_w_in, v_b_forget, v_w_sconv, v_w_pool, v_pool_scale, v_w_out, v_g_xa, v_g_mem, v_w_xq, v_w_xkv, v_w_xo, v_g_ffn, v_w_up, v_w_ffconv, v_w_down, v_g_final):
    weights = dict(g_mix=g_mix, w_in=w_in, b_forget=b_forget, w_sconv=w_sconv, w_pool=w_pool, pool_scale=pool_scale,
                   w_out=w_out, g_xa=g_xa, g_mem=g_mem, w_xq=w_xq, w_xkv=w_xkv, w_xo=w_xo, g_ffn=g_ffn, w_up=w_up,
                   w_ffconv=w_ffconv, w_down=w_down, g_final=g_final)
    m_in = dict(g_mix=m_g_mix, w_in=m_w_in, b_forget=m_b_forget, w_sconv=m_w_sconv, w_pool=m_w_pool,
                pool_scale=m_pool_scale, w_out=m_w_out, g_xa=m_g_xa, g_mem=m_g_mem, w_xq=m_w_xq, w_xkv=m_w_xkv,
                w_xo=m_w_xo, g_ffn=m_g_ffn, w_up=m_w_up, w_ffconv=m_w_ffconv, w_down=m_w_down, g_final=m_g_final)
    v_in = dict(g_mix=v_g_mix, w_in=v_w_in, b_forget=v_b_forget, w_sconv=v_w_sconv, w_pool=v_w_pool,
                pool_scale=v_pool_scale, w_out=v_w_out, g_xa=v_g_xa, g_mem=v_g_mem, w_xq=v_w_xq, w_xkv=v_w_xkv,
                w_xo=v_w_xo, g_ffn=v_g_ffn, w_up=v_w_up, w_ffconv=v_w_ffconv, w_down=v_w_down, g_final=v_g_final)
    depth = w_in.shape[0]
    me = 4 * lax.axis_index("x") + 2 * lax.axis_index("y") + lax.axis_index("c")
    h = x[0]
    memv = mem[0]
    s = h.shape[0]
    tq = min(ATT_TQ, s)
    tables = {"rope": _rope_tables(positions[0]), "dil": _bias_tables("dil", tq, tq),
              "fox": _bias_tables("fox", tq, tq)}

    w_in_r = _relay_in_cols(w_in)
    conv_shard = jnp.concatenate([w_sconv.reshape(-1), w_ffconv.reshape(-1)])
    conv_shard = jnp.concatenate([conv_shard, jnp.zeros((CONV_WORDS - conv_shard.shape[0],), F32)])
    shards = dict(w_in=w_in_r, w_out=w_out, w_xq=w_xq, w_xo=w_xo, w_xkv=w_xkv, w_up=w_up.transpose(0, 2, 1),
                  w_down=w_down)
    gathered = [None] * depth

    def start_gathers(l, after):
        states = {}
        order = jnp.zeros((), F32)
        for group in GATHER_GROUPS:
            first = GATHER_GROUPS[group][0]
            shards[first] = shards[first] + order
            xs = [_place_shard(shards[name], l, me, BF16, "place_%s_%d" % (name, l), after)
                  for name in GATHER_GROUPS[group]]
            if l == 0 and group == "in":
                xs.append(_place_shard(conv_shard.reshape(1, CONV_WORDS // 1024, 1024), 0, me, F32, "place_conv"))
            states[group], token = _exchange_start(xs, False, "gather_%s_start_%d" % (group, l))
            order = order + token[0, 0]
        gathered[l] = _GatheredWeights(states, l)
        return order

    order = start_gathers(0, None)
    gathered[0].need("in", tables["rope"][0])
    conv_all = gathered[0].extra[0].reshape(N_DEV, CONV_WORDS)
    n_sc = depth * 3 * (GROUP // N_DEV)
    sconv_full = conv_all[:, :n_sc].reshape(N_DEV, depth, 3, GROUP // N_DEV).transpose(1, 2, 0, 3).reshape(
        depth, 3, GROUP)
    ffconv_full = conv_all[:, n_sc:n_sc + depth * 3 * FF_SHARD].reshape(N_DEV, depth, 3, FF_SHARD).transpose(
        1, 0, 2, 3)

    smalls = []
    for l in range(depth):
        smalls.append(dict(
            g_mix=g_mix[l], g_xa=g_xa[l], g_mem=g_mem[l], g_ffn=g_ffn[l], pool_scale=pool_scale[l],
            w_pool_bd=_block_diag(w_pool[l]), w_sconv=sconv_full[l], w_ffconv=ffconv_full[l],
            b_forget_pad=jnp.concatenate([b_forget[l], jnp.zeros((128 - N_HEADS,), F32)]).reshape(1, 128)))
    smalls[0]["g_mix"] = smalls[0]["g_mix"] + order

    saved = []
    for l in range(depth):
        def rest_arrived(arrived, l=l):
            return start_gathers(l + 1, arrived) if l + 1 < depth else jnp.zeros((), F32)

        h, sv = _layer_fwd(h, memv, gathered[l], smalls[l], tables, rest_arrived)
        saved.append(sv)
    loss_part, dh, dg_final = _loss_head(h, g_final, loss_target[0])
    loss = lax.psum(loss_part[0, 0], MESH_AXES)

    small_grads = [None] * depth
    scatters = {}

    def pieces_of(big, group):
        return [big[name].reshape(PIECE_SHAPES[name]) for name in GROUPS[group]]

    def rider(grads):
        flat = jnp.concatenate([g.reshape(-1) for g in grads])
        rows = -(-flat.shape[0] // 1024)
        flat = jnp.concatenate([flat, jnp.zeros((rows * 1024 - flat.shape[0],), F32)])
        return jnp.broadcast_to(flat.reshape(1, rows, 1024), (N_DEV, rows, 1024))

    riding = {}
    done_small = {}

    def start_scatter(l, group, big, extra):
        names = [(n, l) for n in SMALL_WITH[group]] + extra
        riding[l, group] = names
        grads = [dg_final if n == "g_final" else done_small[ll][n] for n, ll in names]
        scatters[l, group], token = _exchange_start(pieces_of(big, group) + [rider(grads)], True,
                                                    "scatter_%s_start_%d" % (group, l))
        return token[0, 0]

    for l in reversed(range(depth)):
        def rest_ready(big, small, l=l):
            done_small[l] = small
            extra = ([("g_final", l)] if l == depth - 1 else []) + ([("g_mix", l + 1)] if l + 1 < depth else [])
            return start_scatter(l, "rest", big, extra)

        def in_ready(big, small, l=l):
            return start_scatter(l, "in", big, [])

        dh, _, small_grads[l] = _layer_bwd(dh, memv, gathered[l], smalls[l], tables, saved[l], rest_ready, in_ready)
    grad_x = dh[None]
    riding["tail"] = [("g_mix", 0)]
    scatters["tail"], _ = _exchange_start([rider([small_grads[0]["g_mix"]])], True, "scatter_tail_start")

    parts, owns, small_parts = {}, {}, {}

    def take_rider(key, got, given):
        flat = lax.dynamic_update_slice_in_dim(got, given[:1], me, axis=0).reshape(N_DEV, -1)
        off = 0
        for name, ll in riding[key]:
            shape = SMALL_SHAPES.get(name, weights[name].shape[-1:] if name == "g_final" else weights[name].shape[1:])
            n = 1
            for dim in shape:
                n *= dim
            small_parts.setdefault(name, [None] * depth)[ll] = flat[:, off:off + n].reshape((N_DEV,) + shape)
            off += n

    def wait_group(group, after):
        for l in reversed(range(depth)):
            got, given = _exchange_wait(scatters[l, group], after, "scatter_%s_wait_%d" % (group, l))
            for name, g, x in zip(GROUPS[group], got, given):
                parts.setdefault(name, [None] * depth)[l] = g
                owns.setdefault(name, [None] * depth)[l] = x
            take_rider((l, group), got[-1], given[-1])

    results = {}

    def update(name, w3, m3, v3):
        outs = _adamw(parts[name], owns.get(name), me, w3, m3, v3, "adamw_" + name)
        results[name] = [o.reshape(weights[name].shape) for o in outs]

    wait_group("rest", grad_x)
    for name in GROUPS["rest"]:
        if name == "w_up":
            outs = _adamw(parts[name], owns[name], me, w_up.transpose(0, 2, 1), m_w_up.transpose(0, 2, 1),
                          v_w_up.transpose(0, 2, 1), "adamw_w_up")
            results[name] = [o.transpose(0, 2, 1) for o in outs]
        else:
            update(name, weights[name], m_in[name], v_in[name])
    wait_group("in", results["w_down"][1])
    outs = _adamw(parts["w_in"], owns["w_in"], me, w_in_r, _relay_in_cols(m_w_in), _relay_in_cols(v_w_in),
                  "adamw_w_in")
    results["w_in"] = [_unrelay_in_cols(o) for o in outs]
    got, given = _exchange_wait(scatters["tail"], results["w_in"][1], "scatter_tail_wait")
    take_rider("tail", got[0], given[0])
    for name in SMALL_NAMES + ("g_final",):
        wv = weights[name]
        p = small_parts[name][depth - 1] if name == "g_final" else jnp.stack(small_parts[name], axis=1)
        if name == "w_sconv":
            p = lax.dynamic_slice_in_dim(p, me * (GROUP // N_DEV), GROUP // N_DEV, axis=3)
        elif name == "w_ffconv":
            p = lax.dynamic_index_in_dim(p, me, axis=2, keepdims=False)
        shape3 = (1, 1, wv.shape[0]) if wv.ndim == 1 else (1, -1, wv.shape[-1])
        w3 = wv.reshape(shape3)
        parts[name] = [p.reshape((N_DEV,) + w3.shape[1:])]
        update(name, w3, m_in[name].reshape(shape3), v_in[name].reshape(shape3))

    return (loss, grad_x, *[results[n][0] for n in WEIGHT_NAMES], *[results[n][1] for n in WEIGHT_NAMES],
            *[results[n][2] for n in WEIGHT_NAMES], *[results[n][3] for n in WEIGHT_NAMES])
```
